```python
import math
import jax, jax.numpy as jnp
from jax import lax
import numpy as np

D_MODEL = 1024
BATCH = 8
SEQ = 8192
DEPTH = 1

N_META = 16
BLOCK = 128
PAD = BLOCK - N_META

HG_HEADS = 4
HG_K = 128
HG_V = 128
HG_KW = HG_HEADS * HG_K
HG_VW = HG_HEADS * HG_V
SUB = 16

ATT_HEADS = 8
ATT_KV_HEADS = 2
HEAD_DIM = 64
ATT_QW = ATT_HEADS * HEAD_DIM
ATT_KVW = ATT_KV_HEADS * HEAD_DIM
WINDOW = 128
ROPE_THETA = 10000.0

N_BRANCH = 2
D_FF = ((-(-8 * D_MODEL // 3) + 255) // 256) * 256
EPS = 1e-5
ALPHA = (2.0 * DEPTH) ** 0.25
BETA = (8.0 * DEPTH) ** -0.25
SPLIT_SIZES = (HG_KW, HG_KW, HG_VW, HG_VW, ATT_QW, ATT_KVW, ATT_KVW, N_BRANCH * D_MODEL)
IN_W = sum(SPLIT_SIZES)

kernel_name = "hgrn2_swa_sink_hybrid_deepnorm"


def layer_norm(x, g, b):
    xf = x.astype(jnp.float32)
    mu = jnp.mean(xf, axis=-1, keepdims=True)
    var = jnp.mean(jnp.square(xf - mu), axis=-1, keepdims=True)
    y = (xf - mu) * lax.rsqrt(var + EPS) * g.astype(jnp.float32) + b.astype(jnp.float32)
    return y.astype(x.dtype)


def rope(x, pos):
    half = HEAD_DIM // 2
    inv = ROPE_THETA ** (-jnp.arange(half, dtype=jnp.float32) / half)
    ang = pos.astype(jnp.float32)[:, None] * inv[None, :]
    cos = jnp.cos(ang)[None, :, None, :]
    sin = jnp.sin(ang)[None, :, None, :]
    xf = x.astype(jnp.float32)
    x1, x2 = xf[..., :half], xf[..., half:]
    return jnp.concatenate([x1 * cos - x2 * sin, x2 * cos + x1 * sin], axis=-1).astype(x.dtype)


def hgrn2_chunk(state, inp):
    q, k, v, log_f = inp
    B, H, C, K = q.shape
    V = v.shape[-1]
    n = C // SUB
    b = jnp.cumsum(log_f, axis=2)
    b_last = b[:, :, -1]
    o_inter = jnp.einsum('bhck,bhkv->bhcv', q * jnp.exp(b), state)
    qs = q.reshape(B, H, n, SUB, K)
    ks = k.reshape(B, H, n, SUB, K)
    vs = v.reshape(B, H, n, SUB, V)
    bs = b.reshape(B, H, n, SUB, K)
    tri = jnp.tril(jnp.ones((SUB, SUB), dtype=bool))[:, :, None]
    diff = bs[:, :, :, :, None, :] - bs[:, :, :, None, :, :]
    decay = jnp.exp(jnp.where(tri, diff, -jnp.inf))
    a_diag = jnp.einsum('bhntk,bhnsk,bhntsk->bhnts', qs, ks, decay)
    o_diag = jnp.einsum('bhnts,bhnsv->bhntv', a_diag, vs)
    b_ref = jnp.concatenate([jnp.zeros_like(bs[:, :, :1, 0]), bs[:, :, :-1, -1]], axis=2)
    q_off = qs * jnp.exp(bs - b_ref[:, :, :, None])
    earlier = (jnp.arange(C) // SUB)[None, :] < jnp.arange(n)[:, None]
    k_off = k[:, :, None] * jnp.exp(jnp.where(earlier[:, :, None],
                                              b_ref[:, :, :, None] - b[:, :, None], -jnp.inf))
    a_off = jnp.einsum('bhntk,bhnsk->bhnts', q_off, k_off)
    o_off = jnp.einsum('bhnts,bhsv->bhntv', a_off, v)
    o = o_inter + (o_diag + o_off).reshape(B, H, C, V)
    new_state = state * jnp.exp(b_last)[..., None] + jnp.einsum(
        'bhck,bhcv->bhkv', k * jnp.exp(b_last[:, :, None] - b), v)
    return new_state, o


def hgrn2_mixer(q_raw, f_raw, i_raw, g_raw, lower_bound, norm_g, valid):
    B, P, _ = q_raw.shape
    N = P // BLOCK
    f32 = jnp.float32
    q = jax.nn.silu(q_raw.astype(f32))
    fg = lower_bound + (1.0 - lower_bound) * jax.nn.sigmoid(f_raw.astype(f32))
    m = valid[None, :, None]
    log_f = jnp.where(m, jnp.log(fg), 0.0)
    k = jnp.where(m, 1.0 - fg, 0.0)
    v = i_raw.astype(f32)

    def to_chunks(t, dh):
        return t.reshape(B, N, BLOCK, HG_HEADS, dh).transpose(1, 0, 3, 2, 4)

    xs = (to_chunks(q, HG_K), to_chunks(k, HG_K), to_chunks(v, HG_V), to_chunks(log_f, HG_K))
    s0 = jnp.zeros((B, HG_HEADS, HG_K, HG_V), f32)
    _, o = lax.scan(hgrn2_chunk, s0, xs)
    o = o.transpose(1, 0, 3, 2, 4).reshape(B, P, HG_HEADS, HG_V)
    o = o * lax.rsqrt(jnp.mean(o * o, axis=-1, keepdims=True) + EPS) * norm_g.astype(f32)
    gate = jax.nn.silu(g_raw.astype(f32)).reshape(B, P, HG_HEADS, HG_V)
    return (o * gate).reshape(B, P, HG_VW).astype(q_raw.dtype)


def swa_sink_attention(q_raw, k_raw, v_raw, sinks, pos):
    B, P, _ = q_raw.shape
    NB = P // BLOCK
    G = ATT_HEADS // ATT_KV_HEADS
    f32 = jnp.float32
    q = rope(q_raw.reshape(B, P, ATT_HEADS, HEAD_DIM), pos).astype(f32)
    k = rope(k_raw.reshape(B, P, ATT_KV_HEADS, HEAD_DIM), pos).astype(f32)
    v = v_raw.reshape(B, P, ATT_KV_HEADS, HEAD_DIM).astype(f32)
    scale = HEAD_DIM ** -0.5
    qb = q.reshape(B, NB, BLOCK, ATT_KV_HEADS, G, HEAD_DIM)
    kb = k.reshape(B, NB, BLOCK, ATT_KV_HEADS, HEAD_DIM)
    vb = v.reshape(B, NB, BLOCK, ATT_KV_HEADS, HEAD_DIM)
    shift = lambda t: jnp.pad(t, ((0, 0), (1, 0), (0, 0), (0, 0), (0, 0)))[:, :-1]
    k_band = jnp.concatenate([shift(kb), kb], axis=2)
    v_band = jnp.concatenate([shift(vb), vb], axis=2)
    k_meta = k[:, PAD:BLOCK]
    v_meta = v[:, PAD:BLOCK]
    pos_b = pos.reshape(NB, BLOCK)
    pos_prev = jnp.concatenate([jnp.full((1, BLOCK), -1, pos.dtype), pos_b[:-1]], axis=0)
    key_pos = jnp.concatenate([pos_prev, pos_b], axis=1)[:, None, :]
    meta_pos = pos[PAD:BLOCK][None, None, :]
    qp = pos_b[:, :, None]
    band_ok = (key_pos >= N_META) & (key_pos <= qp) & (qp - key_pos < WINDOW)
    meta_ok = meta_pos <= qp
    neg = jnp.finfo(f32).min
    s_band = jnp.einsum('bnqhgd,bnkhd->bnhgqk', qb, k_band) * scale
    s_meta = jnp.einsum('bnqhgd,bmhd->bnhgqm', qb, k_meta) * scale
    s_band = jnp.where(band_ok[None, :, None, None], s_band, neg)
    s_meta = jnp.where(meta_ok[None, :, None, None], s_meta, neg)
    sink = jnp.broadcast_to(sinks.astype(f32).reshape(ATT_KV_HEADS, G)[None, None, :, :, None, None],
                            s_meta.shape[:-1] + (1,))
    p = jax.nn.softmax(jnp.concatenate([s_meta, s_band, sink], axis=-1), axis=-1)
    p_meta = p[..., :N_META]
    p_band = p[..., N_META:N_META + 2 * BLOCK]
    o = (jnp.einsum('bnhgqm,bmhd->bnqhgd', p_meta, v_meta)
         + jnp.einsum('bnhgqk,bnkhd->bnqhgd', p_band, v_band))
    return o.reshape(B, P, ATT_QW).astype(q_raw.dtype)


def _fwd_setup_inputs(seed: int = 0) -> dict:
    key = jax.random.key(seed)
    ks = jax.random.split(key, 18)
    f32 = jnp.float32
    nrm = lambda k, shape, s: jax.random.normal(k, shape, f32) * s
    return {
        "x": nrm(ks[0], (BATCH, SEQ, D_MODEL), 1.0),
        "meta_tokens": nrm(ks[1], (N_META, D_MODEL), 1.0),
        "ln_emb_g": 1.0 + nrm(ks[2], (D_MODEL,), 0.02),
        "ln_emb_b": nrm(ks[3], (D_MODEL,), 0.02),
        "w_in": nrm(ks[4], (DEPTH, D_MODEL, IN_W), D_MODEL ** -0.5),
        "hg_lower_bounds": nrm(ks[5], (DEPTH + 1, HG_KW), 0.1),
        "hg_norm_g": 1.0 + nrm(ks[6], (DEPTH, HG_V), 0.02),
        "attn_sinks": nrm(ks[7], (DEPTH, ATT_HEADS), 0.5),
        "w_branch_hg": nrm(ks[8], (DEPTH, HG_VW, D_MODEL), HG_VW ** -0.5),
        "w_branch_attn": nrm(ks[9], (DEPTH, ATT_QW, D_MODEL), ATT_QW ** -0.5),
        "w_out": nrm(ks[10], (DEPTH, D_MODEL, D_MODEL), BETA * D_MODEL ** -0.5),
        "ln1_g": 1.0 + nrm(ks[11], (DEPTH, D_MODEL), 0.02),
        "ln1_b": nrm(ks[12], (DEPTH, D_MODEL), 0.02),
        "w_ffn_in": nrm(ks[13], (DEPTH, D_MODEL, 2 * D_FF), D_MODEL ** -0.5),
        "w_ffn_out": nrm(ks[14], (DEPTH, D_FF, D_MODEL), BETA * D_FF ** -0.5),
        "ln2_g": 1.0 + nrm(ks[15], (DEPTH, D_MODEL), 0.02),
        "ln2_b": nrm(ks[16], (DEPTH, D_MODEL), 0.02),
    }


def _fwd_reference(x, meta_tokens, ln_emb_g, ln_emb_b, w_in, hg_lower_bounds, hg_norm_g, attn_sinks,
              w_branch_hg, w_branch_attn, w_out, ln1_g, ln1_b, w_ffn_in, w_ffn_out, ln2_g, ln2_b):
    B, S, D = x.shape
    P = S + BLOCK
    meta = jnp.broadcast_to(meta_tokens.astype(x.dtype)[None], (B, N_META, D))
    h = layer_norm(jnp.concatenate([meta, x], axis=1), ln_emb_g, ln_emb_b)
    h = jnp.pad(h, ((0, 0), (PAD, 0), (0, 0)))
    pos = jnp.arange(P, dtype=jnp.int32) - PAD
    valid = pos >= 0
    lbs = jnp.cumsum(jax.nn.softmax(hg_lower_bounds.astype(jnp.float32), axis=0), axis=0)
    split_idx = [sum(SPLIT_SIZES[:i + 1]) for i in range(len(SPLIT_SIZES) - 1)]
    for l in range(DEPTH):
        proj = h @ w_in[l]
        hq, hf, hi, hg, aq, ak, av, gates = jnp.split(proj, split_idx, axis=-1)
        y_hg = hgrn2_mixer(hq, hf, hi, hg, lbs[l], hg_norm_g[l], valid) @ w_branch_hg[l]
        y_att = swa_sink_attention(aq, ak, av, attn_sinks[l], pos) @ w_branch_attn[l]
        g_hg, g_att = jnp.split(jax.nn.sigmoid(gates), N_BRANCH, axis=-1)
        mix = (g_hg * y_hg + g_att * y_att) @ w_out[l]
        h = layer_norm(ALPHA * h + mix, ln1_g[l], ln1_b[l])
        a, u = jnp.split(h @ w_ffn_in[l], 2, axis=-1)
        h = layer_norm(ALPHA * h + (jax.nn.silu(a) * u) @ w_ffn_out[l], ln2_g[l], ln2_b[l])
    return h[:, BLOCK:]


import jax as _jax
import jax.numpy as _jnp

TWIN_FORMAT = 'train_step'
FWD_PARAMS = ['x', 'meta_tokens', 'ln_emb_g', 'ln_emb_b', 'w_in', 'hg_lower_bounds', 'hg_norm_g', 'attn_sinks', 'w_branch_hg', 'w_branch_attn', 'w_out', 'ln1_g', 'ln1_b', 'w_ffn_in', 'w_ffn_out', 'ln2_g', 'ln2_b']
TWIN_WEIGHTS = ['meta_tokens', 'ln_emb_g', 'ln_emb_b', 'w_in', 'hg_lower_bounds', 'hg_norm_g', 'attn_sinks', 'w_branch_hg', 'w_branch_attn', 'w_out', 'ln1_g', 'ln1_b', 'w_ffn_in', 'w_ffn_out', 'ln2_g', 'ln2_b']
TWIN_DIFF_INPUT = 'x'
TWIN_INPUTS = ['x', 'meta_tokens', 'ln_emb_g', 'ln_emb_b', 'w_in', 'hg_lower_bounds', 'hg_norm_g', 'attn_sinks', 'w_branch_hg', 'w_branch_attn', 'w_out', 'ln1_g', 'ln1_b', 'w_ffn_in', 'w_ffn_out', 'ln2_g', 'ln2_b', 'loss_target', 'm_meta_tokens', 'm_ln_emb_g', 'm_ln_emb_b', 'm_w_in', 'm_hg_lower_bounds', 'm_hg_norm_g', 'm_attn_sinks', 'm_w_branch_hg', 'm_w_branch_attn', 'm_w_out', 'm_ln1_g', 'm_ln1_b', 'm_w_ffn_in', 'm_w_ffn_out', 'm_ln2_g', 'm_ln2_b', 'v_meta_tokens', 'v_ln_emb_g', 'v_ln_emb_b', 'v_w_in', 'v_hg_lower_bounds', 'v_hg_norm_g', 'v_attn_sinks', 'v_w_branch_hg', 'v_w_branch_attn', 'v_w_out', 'v_ln1_g', 'v_ln1_b', 'v_w_ffn_in', 'v_w_ffn_out', 'v_ln2_g', 'v_ln2_b']
TWIN_OUTPUTS = ['loss', 'grad_x', 'grad_meta_tokens', 'grad_ln_emb_g', 'grad_ln_emb_b', 'grad_w_in', 'grad_hg_lower_bounds', 'grad_hg_norm_g', 'grad_attn_sinks', 'grad_w_branch_hg', 'grad_w_branch_attn', 'grad_w_out', 'grad_ln1_g', 'grad_ln1_b', 'grad_w_ffn_in', 'grad_w_ffn_out', 'grad_ln2_g', 'grad_ln2_b', 'delta_meta_tokens', 'delta_ln_emb_g', 'delta_ln_emb_b', 'delta_w_in', 'delta_hg_lower_bounds', 'delta_hg_norm_g', 'delta_attn_sinks', 'delta_w_branch_hg', 'delta_w_branch_attn', 'delta_w_out', 'delta_ln1_g', 'delta_ln1_b', 'delta_w_ffn_in', 'delta_w_ffn_out', 'delta_ln2_g', 'delta_ln2_b', 'new_m_meta_tokens', 'new_m_ln_emb_g', 'new_m_ln_emb_b', 'new_m_w_in', 'new_m_hg_lower_bounds', 'new_m_hg_norm_g', 'new_m_attn_sinks', 'new_m_w_branch_hg', 'new_m_w_branch_attn', 'new_m_w_out', 'new_m_ln1_g', 'new_m_ln1_b', 'new_m_w_ffn_in', 'new_m_w_ffn_out', 'new_m_ln2_g', 'new_m_ln2_b', 'new_v_meta_tokens', 'new_v_ln_emb_g', 'new_v_ln_emb_b', 'new_v_w_in', 'new_v_hg_lower_bounds', 'new_v_hg_norm_g', 'new_v_attn_sinks', 'new_v_w_branch_hg', 'new_v_w_branch_attn', 'new_v_w_out', 'new_v_ln1_g', 'new_v_ln1_b', 'new_v_w_ffn_in', 'new_v_w_ffn_out', 'new_v_ln2_g', 'new_v_ln2_b']
TWIN_LEAF_KINDS = {'loss': 'loss', 'grad_x': 'grad_x', 'grad_meta_tokens': 'grad_w', 'grad_ln_emb_g': 'grad_w', 'grad_ln_emb_b': 'grad_w', 'grad_w_in': 'grad_w', 'grad_hg_lower_bounds': 'grad_w', 'grad_hg_norm_g': 'grad_w', 'grad_attn_sinks': 'grad_w', 'grad_w_branch_hg': 'grad_w', 'grad_w_branch_attn': 'grad_w', 'grad_w_out': 'grad_w', 'grad_ln1_g': 'grad_w', 'grad_ln1_b': 'grad_w', 'grad_w_ffn_in': 'grad_w', 'grad_w_ffn_out': 'grad_w', 'grad_ln2_g': 'grad_w', 'grad_ln2_b': 'grad_w', 'delta_meta_tokens': 'delta_w', 'delta_ln_emb_g': 'delta_w', 'delta_ln_emb_b': 'delta_w', 'delta_w_in': 'delta_w', 'delta_hg_lower_bounds': 'delta_w', 'delta_hg_norm_g': 'delta_w', 'delta_attn_sinks': 'delta_w', 'delta_w_branch_hg': 'delta_w', 'delta_w_branch_attn': 'delta_w', 'delta_w_out': 'delta_w', 'delta_ln1_g': 'delta_w', 'delta_ln1_b': 'delta_w', 'delta_w_ffn_in': 'delta_w', 'delta_w_ffn_out': 'delta_w', 'delta_ln2_g': 'delta_w', 'delta_ln2_b': 'delta_w', 'new_m_meta_tokens': 'new_m', 'new_m_ln_emb_g': 'new_m', 'new_m_ln_emb_b': 'new_m', 'new_m_w_in': 'new_m', 'new_m_hg_lower_bounds': 'new_m', 'new_m_hg_norm_g': 'new_m', 'new_m_attn_sinks': 'new_m', 'new_m_w_branch_hg': 'new_m', 'new_m_w_branch_attn': 'new_m', 'new_m_w_out': 'new_m', 'new_m_ln1_g': 'new_m', 'new_m_ln1_b': 'new_m', 'new_m_w_ffn_in': 'new_m', 'new_m_w_ffn_out': 'new_m', 'new_m_ln2_g': 'new_m', 'new_m_ln2_b': 'new_m', 'new_v_meta_tokens': 'new_v', 'new_v_ln_emb_g': 'new_v', 'new_v_ln_emb_b': 'new_v', 'new_v_w_in': 'new_v', 'new_v_hg_lower_bounds': 'new_v', 'new_v_hg_norm_g': 'new_v', 'new_v_attn_sinks': 'new_v', 'new_v_w_branch_hg': 'new_v', 'new_v_w_branch_attn': 'new_v', 'new_v_w_out': 'new_v', 'new_v_ln1_g': 'new_v', 'new_v_ln1_b': 'new_v', 'new_v_w_ffn_in': 'new_v', 'new_v_w_ffn_out': 'new_v', 'new_v_ln2_g': 'new_v', 'new_v_ln2_b': 'new_v'}


def _forward(args):
    return _fwd_reference(*[args[k] for k in FWD_PARAMS])


def _output_shape():
    out = _jax.eval_shape(lambda: _forward(_fwd_setup_inputs(0)))
    return out.shape, out.dtype

N_MICROBATCH = 1
ADAM_LR = 0.001
ADAM_B1 = 0.9
ADAM_B2 = 0.999
ADAM_EPS = 1e-08
ADAM_WD = 0.01
ADAM_STEP = 10
PER_EXAMPLE_BATCH_AXIS = {'x': 0, 'loss_target': 0}
SHARED_INPUTS = []
_WEIGHT_DTYPES = {'meta_tokens': _jnp.float32, 'ln_emb_g': _jnp.float32, 'ln_emb_b': _jnp.float32, 'w_in': _jnp.float32, 'hg_lower_bounds': _jnp.float32, 'hg_norm_g': _jnp.float32, 'attn_sinks': _jnp.float32, 'w_branch_hg': _jnp.float32, 'w_branch_attn': _jnp.float32, 'w_out': _jnp.float32, 'ln1_g': _jnp.float32, 'ln1_b': _jnp.float32, 'w_ffn_in': _jnp.float32, 'w_ffn_out': _jnp.float32, 'ln2_g': _jnp.float32, 'ln2_b': _jnp.float32}
MOMENT_SCALE = {'meta_tokens': 2.455033e-03, 'ln_emb_g': 1.653291e+00, 'ln_emb_b': 6.143592e-01, 'w_in': 3.008822e-02, 'hg_lower_bounds': 6.597062e-03, 'hg_norm_g': 1.461251e-01, 'attn_sinks': 1.632254e-03, 'w_branch_hg': 4.163392e-02, 'w_branch_attn': 1.212674e-02, 'w_out': 7.277037e-02, 'ln1_g': 1.768036e+00, 'ln1_b': 6.679026e-01, 'w_ffn_in': 4.398203e-02, 'w_ffn_out': 1.206964e-01, 'ln2_g': 6.396434e+01, 'ln2_b': 1.362233e+00}


def _to_microbatches(a, axis):
    t = _jnp.moveaxis(a, axis, 0)
    t = t.reshape((N_MICROBATCH, t.shape[0] // N_MICROBATCH) + t.shape[1:])
    return _jnp.moveaxis(t, 1, axis + 1)


def setup_inputs(seed: int = 0) -> dict:
    inp = _fwd_setup_inputs(seed)
    key = _jax.random.fold_in(_jax.random.key(seed), 7919)
    shape, _ = _output_shape()
    out = dict(inp)
    out["loss_target"] = _jax.random.normal(_jax.random.fold_in(key, 0), shape, _jnp.float32)
    for i, name in enumerate(TWIN_WEIGHTS):
        w = inp[name].astype(_jnp.float32)
        if MOMENT_SCALE is None:
            s = _jnp.sqrt(_jnp.mean(_jnp.square(w)) + 1e-30)
        else:
            s = MOMENT_SCALE[name]
        km, kv = _jax.random.split(_jax.random.fold_in(key, i + 1))
        out[name] = w
        out["m_" + name] = s * _jax.random.normal(km, w.shape, _jnp.float32)
        out["v_" + name] = (s * s) * _jax.random.uniform(kv, w.shape, _jnp.float32, 0.5, 1.5)
    if N_MICROBATCH > 1:
        for name, axis in PER_EXAMPLE_BATCH_AXIS.items():
            out[name] = _to_microbatches(out[name], axis)
    return {'x': out['x'], 'meta_tokens': out['meta_tokens'], 'ln_emb_g': out['ln_emb_g'], 'ln_emb_b': out['ln_emb_b'], 'w_in': out['w_in'], 'hg_lower_bounds': out['hg_lower_bounds'], 'hg_norm_g': out['hg_norm_g'], 'attn_sinks': out['attn_sinks'], 'w_branch_hg': out['w_branch_hg'], 'w_branch_attn': out['w_branch_attn'], 'w_out': out['w_out'], 'ln1_g': out['ln1_g'], 'ln1_b': out['ln1_b'], 'w_ffn_in': out['w_ffn_in'], 'w_ffn_out': out['w_ffn_out'], 'ln2_g': out['ln2_g'], 'ln2_b': out['ln2_b'], 'loss_target': out['loss_target'], 'm_meta_tokens': out['m_meta_tokens'], 'm_ln_emb_g': out['m_ln_emb_g'], 'm_ln_emb_b': out['m_ln_emb_b'], 'm_w_in': out['m_w_in'], 'm_hg_lower_bounds': out['m_hg_lower_bounds'], 'm_hg_norm_g': out['m_hg_norm_g'], 'm_attn_sinks': out['m_attn_sinks'], 'm_w_branch_hg': out['m_w_branch_hg'], 'm_w_branch_attn': out['m_w_branch_attn'], 'm_w_out': out['m_w_out'], 'm_ln1_g': out['m_ln1_g'], 'm_ln1_b': out['m_ln1_b'], 'm_w_ffn_in': out['m_w_ffn_in'], 'm_w_ffn_out': out['m_w_ffn_out'], 'm_ln2_g': out['m_ln2_g'], 'm_ln2_b': out['m_ln2_b'], 'v_meta_tokens': out['v_meta_tokens'], 'v_ln_emb_g': out['v_ln_emb_g'], 'v_ln_emb_b': out['v_ln_emb_b'], 'v_w_in': out['v_w_in'], 'v_hg_lower_bounds': out['v_hg_lower_bounds'], 'v_hg_norm_g': out['v_hg_norm_g'], 'v_attn_sinks': out['v_attn_sinks'], 'v_w_branch_hg': out['v_w_branch_hg'], 'v_w_branch_attn': out['v_w_branch_attn'], 'v_w_out': out['v_w_out'], 'v_ln1_g': out['v_ln1_g'], 'v_ln1_b': out['v_ln1_b'], 'v_w_ffn_in': out['v_w_ffn_in'], 'v_w_ffn_out': out['v_w_ffn_out'], 'v_ln2_g': out['v_ln2_g'], 'v_ln2_b': out['v_ln2_b']}


def _loss(weights, diff, rest, loss_target):
    with _jax.named_scope("forward"):
        args = {**rest, TWIN_DIFF_INPUT: diff, **{k: w.astype(_WEIGHT_DTYPES[k]) for k, w in weights.items()}}
        y = _forward(args)
    with _jax.named_scope("loss_head"):
        err = _jnp.square(y.astype(_jnp.float32) - loss_target)
        return 0.5 * _jnp.sum(_jnp.mean(err, axis=-1)) if err.ndim else 0.5 * err


def _adamw(w, g, m, v):
    m = ADAM_B1 * m + (1.0 - ADAM_B1) * g
    v = ADAM_B2 * v + (1.0 - ADAM_B2) * _jnp.square(g)
    m_hat = m / (1.0 - ADAM_B1 ** ADAM_STEP)
    v_hat = v / (1.0 - ADAM_B2 ** ADAM_STEP)
    delta = -ADAM_LR * (m_hat / (_jnp.sqrt(v_hat) + ADAM_EPS) + ADAM_WD * w)
    return delta, m, v


def reference(x, meta_tokens, ln_emb_g, ln_emb_b, w_in, hg_lower_bounds, hg_norm_g, attn_sinks, w_branch_hg, w_branch_attn, w_out, ln1_g, ln1_b, w_ffn_in, w_ffn_out, ln2_g, ln2_b, loss_target, m_meta_tokens, m_ln_emb_g, m_ln_emb_b, m_w_in, m_hg_lower_bounds, m_hg_norm_g, m_attn_sinks, m_w_branch_hg, m_w_branch_attn, m_w_out, m_ln1_g, m_ln1_b, m_w_ffn_in, m_w_ffn_out, m_ln2_g, m_ln2_b, v_meta_tokens, v_ln_emb_g, v_ln_emb_b, v_w_in, v_hg_lower_bounds, v_hg_norm_g, v_attn_sinks, v_w_branch_hg, v_w_branch_attn, v_w_out, v_ln1_g, v_ln1_b, v_w_ffn_in, v_w_ffn_out, v_ln2_g, v_ln2_b):
    given = dict(x=x, meta_tokens=meta_tokens, ln_emb_g=ln_emb_g, ln_emb_b=ln_emb_b, w_in=w_in, hg_lower_bounds=hg_lower_bounds, hg_norm_g=hg_norm_g, attn_sinks=attn_sinks, w_branch_hg=w_branch_hg, w_branch_attn=w_branch_attn, w_out=w_out, ln1_g=ln1_g, ln1_b=ln1_b, w_ffn_in=w_ffn_in, w_ffn_out=w_ffn_out, ln2_g=ln2_g, ln2_b=ln2_b, loss_target=loss_target, m_meta_tokens=m_meta_tokens, m_ln_emb_g=m_ln_emb_g, m_ln_emb_b=m_ln_emb_b, m_w_in=m_w_in, m_hg_lower_bounds=m_hg_lower_bounds, m_hg_norm_g=m_hg_norm_g, m_attn_sinks=m_attn_sinks, m_w_branch_hg=m_w_branch_hg, m_w_branch_attn=m_w_branch_attn, m_w_out=m_w_out, m_ln1_g=m_ln1_g, m_ln1_b=m_ln1_b, m_w_ffn_in=m_w_ffn_in, m_w_ffn_out=m_w_ffn_out, m_ln2_g=m_ln2_g, m_ln2_b=m_ln2_b, v_meta_tokens=v_meta_tokens, v_ln_emb_g=v_ln_emb_g, v_ln_emb_b=v_ln_emb_b, v_w_in=v_w_in, v_hg_lower_bounds=v_hg_lower_bounds, v_hg_norm_g=v_hg_norm_g, v_attn_sinks=v_attn_sinks, v_w_branch_hg=v_w_branch_hg, v_w_branch_attn=v_w_branch_attn, v_w_out=v_w_out, v_ln1_g=v_ln1_g, v_ln1_b=v_ln1_b, v_w_ffn_in=v_w_ffn_in, v_w_ffn_out=v_w_ffn_out, v_ln2_g=v_ln2_g, v_ln2_b=v_ln2_b)
    weights = {n: given[n] for n in TWIN_WEIGHTS}
    shared = {n: given[n] for n in SHARED_INPUTS}
    per_example = {n: given[n] for n in ['x']}
    grad_fn = _jax.value_and_grad(_loss, argnums=(0, 1))

    def one_microbatch(ex, loss_target):
        ex = dict(ex)
        diff = ex.pop(TWIN_DIFF_INPUT)
        return grad_fn(weights, diff, {**shared, **ex}, loss_target)

    if N_MICROBATCH == 1:
        loss, (grad_w, grad_x) = one_microbatch(per_example, given["loss_target"])
    else:
        def body(carry, xs):
            loss_sum, grad_sum = carry
            l_k, (gw_k, gx_k) = one_microbatch(xs[0], xs[1])
            with _jax.named_scope("update"):
                return (loss_sum + l_k, _jax.tree.map(_jnp.add, grad_sum, gw_k)), gx_k

        init = (_jnp.zeros((), _jnp.float32), _jax.tree.map(_jnp.zeros_like, weights))
        (loss, grad_w), grad_x = _jax.lax.scan(body, init, (per_example, given["loss_target"]))
    with _jax.named_scope("update"):
        delta_w, new_m, new_v = {}, {}, {}
        for n in TWIN_WEIGHTS:
            delta_w[n], new_m[n], new_v[n] = _adamw(weights[n], grad_w[n], given["m_" + n], given["v_" + n])
    return (loss, grad_x, *[grad_w[n] for n in TWIN_WEIGHTS], *[delta_w[n] for n in TWIN_WEIGHTS],
            *[new_m[n] for n in TWIN_WEIGHTS], *[new_v[n] for n in TWIN_WEIGHTS])
```

```python
import functools
import math

import numpy as np
import jax
import jax.numpy as jnp
from jax import lax
from jax.experimental import pallas as pl
from jax.experimental.pallas import tpu as pltpu

F32 = jnp.float32
BF16 = jnp.bfloat16

D_MODEL = 1024
N_META = 16
BLOCK = 128
PAD = BLOCK - N_META
HG_HEADS = 4
HG_W = 512
ATT_HEADS = 8
HEAD_DIM = 64
ATT_QW = 512
ATT_KVW = 128
D_FF = 2816
EPS = 1e-5
ALPHA = 2.0 ** 0.25
ROPE_THETA = 10000.0
N_DEV = 8

ADAM_LR = 0.001
ADAM_B1 = 0.9
ADAM_B2 = 0.999
ADAM_EPS = 1e-08
ADAM_WD = 0.01
ADAM_STEP = 10

VMEM_LIMIT_BYTES = 56 * 1024 * 1024
MESH = pl.DeviceIdType.MESH

_LEVELS = (64, 32, 16, 8, 4, 2, 1)


def _cparams(sem):
    return pltpu.CompilerParams(dimension_semantics=sem, vmem_limit_bytes=VMEM_LIMIT_BYTES)


def _row_tile(rows, target):
    nb = rows // BLOCK
    best = 1
    for d in range(1, nb + 1):
        if nb % d == 0 and d * BLOCK <= target:
            best = d
    return best * BLOCK


_DN = {"nn": (((1,), (0,)), ((), ())), "nt": (((1,), (1,)), ((), ())), "tn": (((0,), (0,)), ((), ()))}


def _dot(a, b, form):
    return lax.dot_general(a.astype(BF16), b.astype(BF16), _DN[form], preferred_element_type=F32)


@functools.partial(jax.custom_vjp, nondiff_argnums=(2,))
def _mm(a, b, form):
    return _dot(a, b, form)


def _mm_fwd(a, b, form):
    return _dot(a, b, form), (a, b)


def _mm_bwd(form, res, g):
    a, b = res
    if form == "nn":
        return _dot(g, b, "nt"), _dot(a, g, "tn")
    if form == "nt":
        return _dot(g, b, "nn"), _dot(g, a, "tn")
    return _dot(b, g, "nt"), _dot(a, g, "nn")


_mm.defvjp(_mm_fwd, _mm_bwd)


def _split_dot(lv, x, form):
    hi = x.astype(BF16)
    lo = (x - hi.astype(F32)).astype(BF16)
    return (lax.dot_general(lv, hi, _DN[form], preferred_element_type=F32)
            + lax.dot_general(lv, lo, _DN[form], preferred_element_type=F32))


@jax.custom_vjp
def _seg_sums(lv, x):
    return _split_dot(lv, x, "nn")


def _seg_sums_fwd(lv, x):
    return _split_dot(lv, x, "nn"), lv


def _seg_sums_bwd(lv, g):
    return jnp.zeros_like(lv), _split_dot(lv, g, "tn")


_seg_sums.defvjp(_seg_sums_fwd, _seg_sums_bwd)


@jax.custom_vjp
def _swap_halves(x):
    return pltpu.roll(x, 64, 1)


_swap_halves.defvjp(lambda x: (pltpu.roll(x, 64, 1), None), lambda _, g: (pltpu.roll(g, 64, 1),))


def _tiled_matmul(a, b, form, *, tm, tn, tc, out_dtype, name):
    m, c = a.shape
    n = b.shape[1] if form == "nn" else b.shape[0]
    assert m % tm == 0 and n % tn == 0 and c % tc == 0, (name, a.shape, b.shape, tm, tn, tc)
    nc = c // tc

    def body(a_ref, b_ref, o_ref, *scratch):
        part = _dot(a_ref[...], b_ref[...], form)
        if nc == 1:
            o_ref[...] = part.astype(out_dtype)
            return
        acc_ref, = scratch
        ci = pl.program_id(2)

        @pl.when(ci == 0)
        def _():
            acc_ref[...] = part

        @pl.when(ci > 0)
        def _():
            acc_ref[...] += part

        @pl.when(ci == nc - 1)
        def _():
            o_ref[...] = acc_ref[...].astype(out_dtype)

    b_spec = (pl.BlockSpec((tc, tn), lambda j, i, k: (k, j)) if form == "nn"
              else pl.BlockSpec((tn, tc), lambda j, i, k: (j, k)))
    return pl.pallas_call(
        body, name=name, grid=(n // tn, m // tm, nc),
        in_specs=[pl.BlockSpec((tm, tc), lambda j, i, k: (i, k)), b_spec],
        out_specs=pl.BlockSpec((tm, tn), lambda j, i, k: (i, j)),
        out_shape=jax.ShapeDtypeStruct((m, n), out_dtype),
        scratch_shapes=[] if nc == 1 else [pltpu.VMEM((tm, tn), F32)],
        compiler_params=_cparams(("arbitrary", "arbitrary", "arbitrary")),
    )(a, b)


def _tiled_matmul_tn(a, b, *, tm, tk, tn, out_dtype, name):
    m, k = a.shape
    n = b.shape[1]
    assert m % tm == 0 and k % tk == 0 and n % tn == 0, (name, a.shape, b.shape, tm, tk, tn)
    nm = m // tm

    def body(a_ref, b_ref, o_ref, acc_ref):
        part = _dot(a_ref[...], b_ref[...], "tn")
        mi = pl.program_id(2)

        @pl.when(mi == 0)
        def _():
            acc_ref[...] = part

        @pl.when(mi > 0)
        def _():
            acc_ref[...] += part

        @pl.when(mi == nm - 1)
        def _():
            o_ref[...] = acc_ref[...].astype(out_dtype)

    return pl.pallas_call(
        body, name=name, grid=(k // tk, n // tn, nm),
        in_specs=[pl.BlockSpec((tm, tk), lambda kk, j, i: (i, kk)), pl.BlockSpec((tm, tn), lambda kk, j, i: (i, j))],
        out_specs=pl.BlockSpec((tk, tn), lambda kk, j, i: (kk, j)),
        out_shape=jax.ShapeDtypeStruct((k, n), out_dtype),
        scratch_shapes=[pltpu.VMEM((tk, tn), F32)],
        compiler_params=_cparams(("arbitrary", "arbitrary", "arbitrary")),
    )(a, b)


def _ln_stats(r):
    mu = jnp.mean(r, axis=-1, keepdims=True)
    xc = r - mu
    var = jnp.mean(xc * xc, axis=-1, keepdims=True)
    rstd = lax.rsqrt(var + EPS)
    return xc * rstd, rstd


def _ln_bwd(dy, xhat, rstd, g):
    dxhat = dy * g
    m1 = jnp.mean(dxhat, axis=-1, keepdims=True)
    m2 = jnp.mean(dxhat * xhat, axis=-1, keepdims=True)
    dr = rstd * (dxhat - m1 - xhat * m2)
    return dr, jnp.sum(dy * xhat, axis=0, keepdims=True), jnp.sum(dy, axis=0, keepdims=True)


def _level_stack():
    t = np.arange(BLOCK)[:, None]
    r = np.arange(BLOCK)[None, :]
    mats = [r <= t, r > t, np.ones((BLOCK, BLOCK), bool)]
    for h in _LEVELS:
        same = (t // (2 * h)) == (r // (2 * h))
        up_t, up_r = (t % (2 * h)) >= h, (r % (2 * h)) >= h
        mats.append(same & up_t & up_r & (r <= t))
    for h in _LEVELS:
        same = (t // (2 * h)) == (r // (2 * h))
        lo_t, lo_r = (t % (2 * h)) < h, (r % (2 * h)) < h
        mats.append(same & lo_t & lo_r & (r > t))
    return jnp.asarray(np.concatenate(mats, axis=0).astype(np.float32), dtype=BF16)


def _hgrn_head(hq, hf, hi, hg, a0, a1, ng, st_in, valid, lv):
    lb = jax.nn.sigmoid(a0 - a1)
    q = jax.nn.silu(hq)
    fg = lb + (1.0 - lb) * jax.nn.sigmoid(hf)
    logf = jnp.where(valid, jnp.log(fg), 0.0)
    k = jnp.where(valid, 1.0 - fg, 0.0)
    v = hi
    e = _seg_sums(lv, logf)
    blk = lambda i: e[i * BLOCK:(i + 1) * BLOCK]
    rows = lax.broadcasted_iota(jnp.int32, (BLOCK, BLOCK), 0)
    cols = lax.broadcasted_iota(jnp.int32, (BLOCK, BLOCK), 1)
    o = _mm(q * jnp.exp(blk(0)), st_in, "nt")
    a = jnp.where(rows == cols, jnp.sum(q * k, axis=-1, keepdims=True), 0.0)
    for li, h in enumerate(_LEVELS):
        qs = q * jnp.exp(blk(3 + li))
        ks = k * jnp.exp(blk(3 + len(_LEVELS) + li))
        pair = ((rows // (2 * h)) == (cols // (2 * h))) & ((rows % (2 * h)) >= h) & ((cols % (2 * h)) < h)
        a = a + jnp.where(pair, _mm(qs, ks, "nt"), 0.0)
    o = o + _mm(a, v, "nn")
    st_out = st_in * jnp.exp(blk(2)) + _mm(v, k * jnp.exp(blk(1)), "tn")
    on = o * lax.rsqrt(jnp.mean(o * o, axis=-1, keepdims=True) + EPS) * ng
    return on * jax.nn.silu(hg), st_out


def _rope(x, cos, sin, first_half):
    partner = jnp.where(first_half, -pltpu.roll(x, 96, 1), pltpu.roll(x, 32, 1))
    return x * cos + partner * sin


def _rope_t(g, cos, sin, first_half):
    u = g * sin
    partner = jnp.where(first_half, pltpu.roll(u, 96, 1), -pltpu.roll(u, 32, 1))
    return g * cos + partner


def _att_core(q0, q1, q2, q3, km, kp, kc, vm, vp, vc, sinkcol0, sinkcol1, ok_m, ok_p, ok_c):
    lane = lax.broadcasted_iota(jnp.int32, (BLOCK, BLOCK), 1)
    low = lane < HEAD_DIM
    scale = HEAD_DIM ** -0.5
    neg = jnp.finfo(F32).min
    outs = []
    for g, (qa, qb, sinkcol) in enumerate(((q0, q1, sinkcol0), (q2, q3, sinkcol1))):
        def both(x, g=g):
            sw = _swap_halves(x)
            return jnp.where(low, x, sw) if g == 0 else jnp.where(low, sw, x)
        q4 = jnp.concatenate([jnp.where(low, qa, 0.0), jnp.where(low, 0.0, qa),
                              jnp.where(low, qb, 0.0), jnp.where(low, 0.0, qb)], axis=0)
        s = []
        for kk, ok in ((km, ok_m), (kp, ok_p), (kc, ok_c)):
            ok4 = jnp.concatenate([ok] * 4, axis=0)
            s.append(jnp.where(ok4, _mm(q4, both(kk), "nt") * scale, neg))
        mx = jnp.maximum(jnp.maximum(jnp.max(s[0], axis=-1, keepdims=True), jnp.max(s[1], axis=-1, keepdims=True)),
                         jnp.maximum(jnp.max(s[2], axis=-1, keepdims=True), sinkcol))
        mx = lax.stop_gradient(mx)
        p = [jnp.exp(si - mx) for si in s]
        den = (jnp.sum(p[0], axis=-1, keepdims=True) + jnp.sum(p[1], axis=-1, keepdims=True)
               + jnp.sum(p[2], axis=-1, keepdims=True) + jnp.exp(sinkcol - mx))
        inv = 1.0 / den
        o4 = (_mm(p[0] * inv, both(vm), "nn") + _mm(p[1] * inv, both(vp), "nn") + _mm(p[2] * inv, both(vc), "nn"))
        for j in range(2):
            outs.append(jnp.where(low, o4[(2 * j) * BLOCK:(2 * j + 1) * BLOCK],
                                  o4[(2 * j + 1) * BLOCK:(2 * j + 2) * BLOCK]))
    return jnp.concatenate(outs, axis=1)


def _att_masks(blk_idx):
    qpos = blk_idx * BLOCK + lax.broadcasted_iota(jnp.int32, (BLOCK, BLOCK), 0) - PAD
    kidx = lax.broadcasted_iota(jnp.int32, (BLOCK, BLOCK), 1)
    pos_m = kidx - PAD
    pos_p = (blk_idx - 1) * BLOCK + kidx - PAD
    pos_c = blk_idx * BLOCK + kidx - PAD
    ok_m = (pos_m >= 0) & (pos_m <= qpos)
    ok_p = (pos_p >= N_META) & (qpos - pos_p < BLOCK) & (blk_idx >= 1)
    ok_c = (pos_c >= N_META) & (pos_c <= qpos)
    return ok_m, ok_p, ok_c


def _embed_ln(xin, g0, b0):
    p = xin.shape[0]
    tr = _row_tile(p, 640)

    def body(x_ref, g_ref, b_ref, h_ref, xh_ref, rs_ref):
        xhat, rstd = _ln_stats(x_ref[...])
        row = pl.program_id(0) * tr + lax.broadcasted_iota(jnp.int32, (tr, 1), 0)
        h_ref[...] = jnp.where(row >= PAD, xhat * g_ref[...] + b_ref[...], 0.0)
        xh_ref[...] = xhat
        rs_ref[...] = rstd

    vec = pl.BlockSpec((1, D_MODEL), lambda i: (0, 0))
    rowsp = pl.BlockSpec((tr, D_MODEL), lambda i: (i, 0))
    return pl.pallas_call(
        body, name="embed_ln", grid=(p // tr,), in_specs=[rowsp, vec, vec],
        out_specs=[rowsp, rowsp, pl.BlockSpec((tr, 1), lambda i: (i, 0))],
        out_shape=[jax.ShapeDtypeStruct((p, D_MODEL), F32)] * 2 + [jax.ShapeDtypeStruct((p, 1), F32)],
        compiler_params=_cparams(("arbitrary",)),
    )(xin, g0, b0)


def _hgrn_fwd(proj_hg, lbounds, norm_g, lv):
    p = proj_hg.shape[0]
    nb = p // BLOCK

    def body(x_ref, lb_ref, ng_ref, lv_ref, y_ref, st_ref, carry_ref):
        c = pl.program_id(0)

        @pl.when(c == 0)
        def _():
            carry_ref[...] = jnp.zeros_like(carry_ref)

        valid = (c * BLOCK + lax.broadcasted_iota(jnp.int32, (BLOCK, 1), 0)) >= PAD
        lvv = lv_ref[...]
        for h in range(HG_HEADS):
            sl = lambda part: x_ref[:, part * HG_W + h * BLOCK: part * HG_W + (h + 1) * BLOCK]
            hs = slice(h * BLOCK, (h + 1) * BLOCK)
            st_in = carry_ref[h]
            st_ref[0, h] = st_in
            y, st_out = _hgrn_head(sl(0), sl(1), sl(2), sl(3), lb_ref[0:1, hs], lb_ref[1:2, hs], ng_ref[...],
                                   st_in, valid, lvv)
            y_ref[:, hs] = y
            carry_ref[h] = st_out

    return pl.pallas_call(
        body, name="hgrn_fwd", grid=(nb,),
        in_specs=[pl.BlockSpec((BLOCK, 4 * HG_W), lambda c: (c, 0)), pl.BlockSpec((2, HG_W), lambda c: (0, 0)),
                  pl.BlockSpec((1, BLOCK), lambda c: (0, 0)), pl.BlockSpec(lv.shape, lambda c: (0, 0))],
        out_specs=[pl.BlockSpec((BLOCK, HG_W), lambda c: (c, 0)),
                   pl.BlockSpec((1, HG_HEADS, BLOCK, BLOCK), lambda c: (c, 0, 0, 0))],
        out_shape=[jax.ShapeDtypeStruct((p, HG_W), F32), jax.ShapeDtypeStruct((nb, HG_HEADS, BLOCK, BLOCK), F32)],
        scratch_shapes=[pltpu.VMEM((HG_HEADS, BLOCK, BLOCK), F32)],
        compiler_params=_cparams(("arbitrary",)),
    )(proj_hg, lbounds, norm_g, lv)


def _rope_tables(p):
    pos = (jnp.arange(p, dtype=jnp.int32) - PAD).astype(F32)
    half = HEAD_DIM // 2
    inv = ROPE_THETA ** (-jnp.arange(half, dtype=F32) / half)
    ang = pos[:, None] * jnp.tile(inv, BLOCK // half)[None, :]
    return jnp.cos(ang), jnp.sin(ang)


def _att_sinkcols(sink_ref):
    rowhead = lax.broadcasted_iota(jnp.int32, (4 * BLOCK, 1), 0) // BLOCK
    cols = []
    for g in range(2):
        col = jnp.zeros((4 * BLOCK, 1), F32)
        for j in range(4):
            col = jnp.where(rowhead == j, sink_ref[0, 4 * g + j], col)
        cols.append(col)
    return cols


def _att_load(qkv_ref, cos_ref, sin_ref, first_half, with_q):
    cos, sin = cos_ref[...], sin_ref[...]
    qs = [_rope(qkv_ref[:, j * BLOCK:(j + 1) * BLOCK], cos, sin, first_half) for j in range(4)] if with_q else None
    k = _rope(qkv_ref[:, ATT_QW:ATT_QW + ATT_KVW], cos, sin, first_half)
    v = qkv_ref[:, ATT_QW + ATT_KVW:ATT_QW + 2 * ATT_KVW]
    return qs, k, v


def _att_specs(nb):
    w = ATT_QW + 2 * ATT_KVW
    cur = lambda width: pl.BlockSpec((BLOCK, width), lambda i: (i, 0))
    prev = lambda width: pl.BlockSpec((BLOCK, width), lambda i: (jnp.maximum(i - 1, 0), 0))
    meta = lambda width: pl.BlockSpec((BLOCK, width), lambda i: (0, 0))
    return [cur(w), prev(w), meta(w), cur(BLOCK), cur(BLOCK), prev(BLOCK), prev(BLOCK), meta(BLOCK), meta(BLOCK),
            pl.BlockSpec(memory_space=pltpu.SMEM)]


def _att_fwd(proj_att, cos, sin, sinks):
    p = proj_att.shape[0]
    nb = p // BLOCK

    def body(cur_ref, prev_ref, meta_ref, cc, sc, cp, sp, cm, sm, sink_ref, o_ref):
        i = pl.program_id(0)
        fh = (lax.broadcasted_iota(jnp.int32, (BLOCK, BLOCK), 1) % HEAD_DIM) < (HEAD_DIM // 2)
        qs, kc, vc = _att_load(cur_ref, cc, sc, fh, True)
        _, kp, vp = _att_load(prev_ref, cp, sp, fh, False)
        _, km, vm = _att_load(meta_ref, cm, sm, fh, False)
        s0, s1 = _att_sinkcols(sink_ref)
        o_ref[...] = _att_core(*qs, km, kp, kc, vm, vp, vc, s0, s1, *_att_masks(i))

    return pl.pallas_call(
        body, name="att_fwd", grid=(nb,), in_specs=_att_specs(nb),
        out_specs=pl.BlockSpec((BLOCK, ATT_QW), lambda i: (i, 0)),
        out_shape=jax.ShapeDtypeStruct((p, ATT_QW), F32),
        compiler_params=_cparams(("arbitrary",)),
    )(proj_att, proj_att, proj_att, cos, sin, cos, sin, cos, sin, sinks)


def _mix_fwd(y_hg, y_att, gates):
    p = y_hg.shape[0]
    tr = _row_tile(p, 640)

    def body(yh_ref, ya_ref, g_ref, o_ref):
        o_ref[...] = (jax.nn.sigmoid(g_ref[:, :D_MODEL]) * yh_ref[...]
                      + jax.nn.sigmoid(g_ref[:, D_MODEL:]) * ya_ref[...])

    rowsp = pl.BlockSpec((tr, D_MODEL), lambda i: (i, 0))
    return pl.pallas_call(
        body, name="mix_fwd", grid=(p // tr,),
        in_specs=[rowsp, rowsp, pl.BlockSpec((tr, 2 * D_MODEL), lambda i: (i, 0))], out_specs=rowsp,
        out_shape=jax.ShapeDtypeStruct((p, D_MODEL), F32), compiler_params=_cparams(("arbitrary",)),
    )(y_hg, y_att, gates)


def _resid_ln(h_prev, branch, g, b, name):
    p = h_prev.shape[0]
    tr = _row_tile(p, 640)

    def body(h_ref, m_ref, g_ref, b_ref, o_ref, xh_ref, rs_ref):
        xhat, rstd = _ln_stats(ALPHA * h_ref[...] + m_ref[...])
        o_ref[...] = xhat * g_ref[...] + b_ref[...]
        xh_ref[...] = xhat
        rs_ref[...] = rstd

    vec = pl.BlockSpec((1, D_MODEL), lambda i: (0, 0))
    rowsp = pl.BlockSpec((tr, D_MODEL), lambda i: (i, 0))
    return pl.pallas_call(
        body, name=name, grid=(p // tr,), in_specs=[rowsp, rowsp, vec, vec],
        out_specs=[rowsp, rowsp, pl.BlockSpec((tr, 1), lambda i: (i, 0))],
        out_shape=[jax.ShapeDtypeStruct((p, D_MODEL), F32)] * 2 + [jax.ShapeDtypeStruct((p, 1), F32)],
        compiler_params=_cparams(("arbitrary",)),
    )(h_prev, branch, g, b)


def _swiglu_fwd(au):
    p = au.shape[0]

    def body(a_ref, u_ref, o_ref):
        o_ref[...] = jax.nn.silu(a_ref[...]) * u_ref[...]

    return pl.pallas_call(
        body, name="swiglu_fwd", grid=(p // BLOCK,),
        in_specs=[pl.BlockSpec((BLOCK, D_FF), lambda i: (i, 0)), pl.BlockSpec((BLOCK, D_FF), lambda i: (i, 1))],
        out_specs=pl.BlockSpec((BLOCK, D_FF), lambda i: (i, 0)),
        out_shape=jax.ShapeDtypeStruct((p, D_FF), F32), compiler_params=_cparams(("arbitrary",)),
    )(au, au)


def _final_ln_loss(h1, f, g2, b2, target):
    p = h1.shape[0]
    tr = _row_tile(p, 640)

    def body(h_ref, f_ref, g_ref, b_ref, t_ref, dr_ref, loss_ref, dg_ref, db_ref):
        i = pl.program_id(0)
        xhat, rstd = _ln_stats(ALPHA * h_ref[...] + f_ref[...])
        y = xhat * g_ref[...] + b_ref[...]
        row = i * tr + lax.broadcasted_iota(jnp.int32, (tr, 1), 0)
        err = jnp.where(row >= BLOCK, y - t_ref[...], 0.0)
        dr, dg, db = _ln_bwd(err * (1.0 / D_MODEL), xhat, rstd, g_ref[...])
        dr_ref[...] = dr
        e2 = jnp.sum(err * err, axis=0, keepdims=True)
        part = e2[:, 0:BLOCK]
        for j in range(1, D_MODEL // BLOCK):
            part = part + e2[:, j * BLOCK:(j + 1) * BLOCK]
        part = part * (0.5 / D_MODEL)

        @pl.when(i == 0)
        def _():
            loss_ref[...] = part
            dg_ref[...] = dg
            db_ref[...] = db

        @pl.when(i > 0)
        def _():
            loss_ref[...] += part
            dg_ref[...] += dg
            db_ref[...] += db

    vec = pl.BlockSpec((1, D_MODEL), lambda i: (0, 0))
    rowsp = pl.BlockSpec((tr, D_MODEL), lambda i: (i, 0))
    return pl.pallas_call(
        body, name="final_ln_loss", grid=(p // tr,), in_specs=[rowsp, rowsp, vec, vec, rowsp],
        out_specs=[rowsp, pl.BlockSpec((1, BLOCK), lambda i: (0, 0)), vec, vec],
        out_shape=[jax.ShapeDtypeStruct((p, D_MODEL), F32), jax.ShapeDtypeStruct((1, BLOCK), F32),
                   jax.ShapeDtypeStruct((1, D_MODEL), F32), jax.ShapeDtypeStruct((1, D_MODEL), F32)],
        compiler_params=_cparams(("arbitrary",)),
    )(h1, f, g2, b2, target)


def _swiglu_bwd(au, ds):
    p = au.shape[0]

    def body(a_ref, u_ref, ds_ref, o_ref):
        _, vjp = jax.vjp(lambda a, u: jax.nn.silu(a) * u, a_ref[...], u_ref[...])
        da, du = vjp(ds_ref[...])
        o_ref[:, :D_FF] = da.astype(BF16)
        o_ref[:, D_FF:] = du.astype(BF16)

    return pl.pallas_call(
        body, name="swiglu_bwd", grid=(p // BLOCK,),
        in_specs=[pl.BlockSpec((BLOCK, D_FF), lambda i: (i, 0)), pl.BlockSpec((BLOCK, D_FF), lambda i: (i, 1)),
                  pl.BlockSpec((BLOCK, D_FF), lambda i: (i, 0))],
        out_specs=pl.BlockSpec((BLOCK, 2 * D_FF), lambda i: (i, 0)),
        out_shape=jax.ShapeDtypeStruct((p, 2 * D_FF), BF16), compiler_params=_cparams(("arbitrary",)),
    )(au, au, ds)


def _ln_bwd_call(d_a, d_b, scale_a, xhat, rstd, g, name, mask_from=None):
    p = xhat.shape[0]
    tr = _row_tile(p, 640)

    def body(a_ref, b_ref, xh_ref, rs_ref, g_ref, dr_ref, dg_ref, db_ref):
        i = pl.program_id(0)
        dy = scale_a * a_ref[...] + b_ref[...]
        if mask_from is not None:
            row = i * tr + lax.broadcasted_iota(jnp.int32, (tr, 1), 0)
            dy = jnp.where(row >= mask_from, dy, 0.0)
        dr, dg, db = _ln_bwd(dy, xh_ref[...], rs_ref[...], g_ref[...])
        dr_ref[...] = dr

        @pl.when(i == 0)
        def _():
            dg_ref[...] = dg
            db_ref[...] = db

        @pl.when(i > 0)
        def _():
            dg_ref[...] += dg
            db_ref[...] += db

    vec = pl.BlockSpec((1, D_MODEL), lambda i: (0, 0))
    rowsp = pl.BlockSpec((tr, D_MODEL), lambda i: (i, 0))
    return pl.pallas_call(
        body, name=name, grid=(p // tr,),
        in_specs=[rowsp, rowsp, rowsp, pl.BlockSpec((tr, 1), lambda i: (i, 0)), vec],
        out_specs=[rowsp, vec, vec],
        out_shape=[jax.ShapeDtypeStruct((p, D_MODEL), F32)] + [jax.ShapeDtypeStruct((1, D_MODEL), F32)] * 2,
        compiler_params=_cparams(("arbitrary",)),
    )(d_a, d_b, xhat, rstd, g)


def _mix_bwd(dmixin, y_hg, y_att, gates):
    p = y_hg.shape[0]
    tr = _row_tile(p, 640)

    def body(d_ref, yh_ref, ya_ref, g_ref, dyh_ref, dya_ref, dg_ref):
        d = d_ref[...]
        s1 = jax.nn.sigmoid(g_ref[:, :D_MODEL])
        s2 = jax.nn.sigmoid(g_ref[:, D_MODEL:])
        dyh_ref[...] = d * s1
        dya_ref[...] = d * s2
        dg_ref[:, :D_MODEL] = d * yh_ref[...] * s1 * (1.0 - s1)
        dg_ref[:, D_MODEL:] = d * ya_ref[...] * s2 * (1.0 - s2)

    rowsp = pl.BlockSpec((tr, D_MODEL), lambda i: (i, 0))
    wide = pl.BlockSpec((tr, 2 * D_MODEL), lambda i: (i, 0))
    return pl.pallas_call(
        body, name="mix_bwd", grid=(p // tr,), in_specs=[rowsp, rowsp, rowsp, wide], out_specs=[rowsp, rowsp, wide],
        out_shape=[jax.ShapeDtypeStruct((p, D_MODEL), F32)] * 2 + [jax.ShapeDtypeStruct((p, 2 * D_MODEL), F32)],
        compiler_params=_cparams(("arbitrary",)),
    )(dmixin, y_hg, y_att, gates)


def _hgrn_bwd(proj_hg, lbounds, norm_g, lv, states, dyh):
    p = proj_hg.shape[0]
    nb = p // BLOCK
    rev = lambda c: nb - 1 - c

    def body(x_ref, lb_ref, ng_ref, lv_ref, st_ref, dy_ref, dx_ref, dlb_ref, dng_ref, dcarry_ref):
        step = pl.program_id(0)
        c = rev(step)

        @pl.when(step == 0)
        def _():
            dcarry_ref[...] = jnp.zeros_like(dcarry_ref)
            dlb_ref[...] = jnp.zeros_like(dlb_ref)
            dng_ref[...] = jnp.zeros_like(dng_ref)

        valid = (c * BLOCK + lax.broadcasted_iota(jnp.int32, (BLOCK, 1), 0)) >= PAD
        lvv = lv_ref[...]
        dng = jnp.zeros((1, BLOCK), F32)
        for h in range(HG_HEADS):
            sl = lambda part: x_ref[:, part * HG_W + h * BLOCK: part * HG_W + (h + 1) * BLOCK]
            hs = slice(h * BLOCK, (h + 1) * BLOCK)
            fn = lambda hq, hf, hi, hg, a0, a1, ng, st: _hgrn_head(hq, hf, hi, hg, a0, a1, ng, st, valid, lvv)
            _, vjp = jax.vjp(fn, sl(0), sl(1), sl(2), sl(3), lb_ref[0:1, hs], lb_ref[1:2, hs], ng_ref[...],
                             st_ref[0, h])
            dhq, dhf, dhi, dhg, da0, da1, dngh, dst = vjp((dy_ref[:, hs], dcarry_ref[h]))
            for part, val in enumerate((dhq, dhf, dhi, dhg)):
                dx_ref[:, part * HG_W + h * BLOCK: part * HG_W + (h + 1) * BLOCK] = val
            dlb_ref[0:1, hs] += da0
            dlb_ref[1:2, hs] += da1
            dng = dng + dngh
            dcarry_ref[h] = dst
        dng_ref[...] += dng

    return pl.pallas_call(
        body, name="hgrn_bwd", grid=(nb,),
        in_specs=[pl.BlockSpec((BLOCK, 4 * HG_W), lambda s: (rev(s), 0)), pl.BlockSpec((2, HG_W), lambda s: (0, 0)),
                  pl.BlockSpec((1, BLOCK), lambda s: (0, 0)), pl.BlockSpec(lv.shape, lambda s: (0, 0)),
                  pl.BlockSpec((1, HG_HEADS, BLOCK, BLOCK), lambda s: (rev(s), 0, 0, 0)),
                  pl.BlockSpec((BLOCK, HG_W), lambda s: (rev(s), 0))],
        out_specs=[pl.BlockSpec((BLOCK, 4 * HG_W), lambda s: (rev(s), 0)), pl.BlockSpec((2, HG_W), lambda s: (0, 0)),
                   pl.BlockSpec((1, BLOCK), lambda s: (0, 0))],
        out_shape=[jax.ShapeDtypeStruct((p, 4 * HG_W), F32), jax.ShapeDtypeStruct((2, HG_W), F32),
                   jax.ShapeDtypeStruct((1, BLOCK), F32)],
        scratch_shapes=[pltpu.VMEM((HG_HEADS, BLOCK, BLOCK), F32)],
        compiler_params=_cparams(("arbitrary",)),
    )(proj_hg, lbounds, norm_g, lv, states, dyh)


def _att_bwd(proj_att, cos, sin, sinks, doa):
    p = proj_att.shape[0]
    nb = p // BLOCK

    def body(cur_ref, prev_ref, meta_ref, cc, sc, cp, sp, cm, sm, sink_ref, do_ref,
             dq_ref, dcur_ref, dprev_ref, dmeta_ref, dsink_ref):
        i = pl.program_id(0)
        fh = (lax.broadcasted_iota(jnp.int32, (BLOCK, BLOCK), 1) % HEAD_DIM) < (HEAD_DIM // 2)
        qs, kc, vc = _att_load(cur_ref, cc, sc, fh, True)
        _, kp, vp = _att_load(prev_ref, cp, sp, fh, False)
        _, km, vm = _att_load(meta_ref, cm, sm, fh, False)
        s0, s1 = _att_sinkcols(sink_ref)
        masks = _att_masks(i)
        fn = lambda *a: _att_core(*a, *masks)
        _, vjp = jax.vjp(fn, *qs, km, kp, kc, vm, vp, vc, s0, s1)
        dq0, dq1, dq2, dq3, dkm, dkp, dkc, dvm, dvp, dvc, ds0, ds1 = vjp(do_ref[...])
        for j, dq in enumerate((dq0, dq1, dq2, dq3)):
            dq_ref[:, j * BLOCK:(j + 1) * BLOCK] = _rope_t(dq, cc[...], sc[...], fh)
        dcur_ref[:, :BLOCK] = _rope_t(dkc, cc[...], sc[...], fh)
        dcur_ref[:, BLOCK:] = dvc
        dprev_ref[:, :BLOCK] = _rope_t(dkp, cp[...], sp[...], fh)
        dprev_ref[:, BLOCK:] = dvp
        dkm_r = _rope_t(dkm, cm[...], sm[...], fh)
        rows = []
        for g, dsg in enumerate((ds0, ds1)):
            for j in range(4):
                tot = jnp.sum(dsg[j * BLOCK:(j + 1) * BLOCK], axis=0, keepdims=True)
                rows.append(jnp.broadcast_to(tot, (1, BLOCK)))
        dsink = jnp.concatenate(rows, axis=0)

        @pl.when(i == 0)
        def _():
            dmeta_ref[:, :BLOCK] = dkm_r
            dmeta_ref[:, BLOCK:] = dvm
            dsink_ref[...] = dsink

        @pl.when(i > 0)
        def _():
            dmeta_ref[:, :BLOCK] += dkm_r
            dmeta_ref[:, BLOCK:] += dvm
            dsink_ref[...] += dsink

    kvw = 2 * ATT_KVW
    return pl.pallas_call(
        body, name="att_bwd", grid=(nb,),
        in_specs=_att_specs(nb) + [pl.BlockSpec((BLOCK, ATT_QW), lambda i: (i, 0))],
        out_specs=[pl.BlockSpec((BLOCK, ATT_QW), lambda i: (i, 0)), pl.BlockSpec((BLOCK, kvw), lambda i: (i, 0)),
                   pl.BlockSpec((BLOCK, kvw), lambda i: (i, 0)), pl.BlockSpec((BLOCK, kvw), lambda i: (0, 0)),
                   pl.BlockSpec((ATT_HEADS, BLOCK), lambda i: (0, 0))],
        out_shape=[jax.ShapeDtypeStruct((p, ATT_QW), F32), jax.ShapeDtypeStruct((p, kvw), F32),
                   jax.ShapeDtypeStruct((p, kvw), F32), jax.ShapeDtypeStruct((BLOCK, kvw), F32),
                   jax.ShapeDtypeStruct((ATT_HEADS, BLOCK), F32)],
        compiler_params=_cparams(("arbitrary",)),
    )(proj_att, proj_att, proj_att, cos, sin, cos, sin, cos, sin, sinks, doa)


def _assemble_dproj(d_hg, dq, dkv_cur, dkv_prev, dkv_meta, dgates):
    p = d_hg.shape[0]
    nb = p // BLOCK
    kvw = 2 * ATT_KVW
    wide = 4 * HG_W
    width = wide + ATT_QW + kvw + 2 * D_MODEL

    def body(hg_ref, dq_ref, cur_ref, nxt_ref, meta_ref, g_ref, o_ref):
        i = pl.program_id(0)
        dkv = cur_ref[...] + jnp.where(i < nb - 1, nxt_ref[...], 0.0) + jnp.where(i == 0, meta_ref[...], 0.0)
        o_ref[:, :wide] = hg_ref[...].astype(BF16)
        o_ref[:, wide:wide + ATT_QW] = dq_ref[...].astype(BF16)
        o_ref[:, wide + ATT_QW:wide + ATT_QW + kvw] = dkv.astype(BF16)
        o_ref[:, wide + ATT_QW + kvw:] = g_ref[...].astype(BF16)

    row = lambda w: pl.BlockSpec((BLOCK, w), lambda i: (i, 0))
    return pl.pallas_call(
        body, name="assemble_dproj", grid=(nb,),
        in_specs=[row(wide), row(ATT_QW), row(kvw), pl.BlockSpec((BLOCK, kvw), lambda i: (jnp.minimum(i + 1, nb - 1), 0)),
                  pl.BlockSpec((BLOCK, kvw), lambda i: (0, 0)), row(2 * D_MODEL)],
        out_specs=row(width), out_shape=jax.ShapeDtypeStruct((p, width), BF16),
        compiler_params=_cparams(("arbitrary",)),
    )(d_hg, dq, dkv_cur, dkv_prev, dkv_meta, dgates)


def _local_step(x, target, meta, ln_emb_g, ln_emb_b, w_in, lbounds, norm_g, sinks, w_bh, w_ba, w_out,
                ln1_g, ln1_b, w_fi, w_fo, ln2_g, ln2_b):
    s = x.shape[0]
    p = s + BLOCK
    xin = jnp.concatenate([jnp.zeros((PAD, D_MODEL), F32), meta, x], axis=0)
    tgt = jnp.concatenate([jnp.zeros((BLOCK, D_MODEL), F32), target], axis=0)
    tm = _row_tile(p, 640)
    lv = _level_stack()
    cos, sin = _rope_tables(p)
    hg_end = 4 * HG_W
    att_end = hg_end + ATT_QW + 2 * ATT_KVW
    mm = functools.partial(_tiled_matmul, tm=tm)

    h0, xhat0, rstd0 = _embed_ln(xin, ln_emb_g, ln_emb_b)
    proj_hg = mm(h0, w_in[:, :hg_end], "nn", tn=hg_end, tc=D_MODEL, out_dtype=F32, name="proj_hg")
    proj_att = mm(h0, w_in[:, hg_end:att_end], "nn", tn=att_end - hg_end, tc=D_MODEL, out_dtype=F32, name="proj_att")
    gates = mm(h0, w_in[:, att_end:], "nn", tn=2 * D_MODEL, tc=D_MODEL, out_dtype=F32, name="proj_gates")
    yh, states = _hgrn_fwd(proj_hg, lbounds, norm_g, lv)
    oa = _att_fwd(proj_att, cos, sin, sinks)
    y_hg = mm(yh, w_bh, "nn", tn=D_MODEL, tc=HG_W, out_dtype=F32, name="branch_hg")
    y_att = mm(oa, w_ba, "nn", tn=D_MODEL, tc=ATT_QW, out_dtype=F32, name="branch_att")
    mixin = _mix_fwd(y_hg, y_att, gates)
    mix = mm(mixin, w_out, "nn", tn=D_MODEL, tc=D_MODEL, out_dtype=F32, name="out_proj")
    h1, xhat1, rstd1 = _resid_ln(h0, mix, ln1_g, ln1_b, "resid_ln1")
    au = mm(h1, w_fi, "nn", tn=D_FF, tc=D_MODEL, out_dtype=F32, name="ffn_in")
    sw = _swiglu_fwd(au)
    f = mm(sw, w_fo, "nn", tn=D_MODEL, tc=D_FF, out_dtype=F32, name="ffn_out")
    dr2, loss_part, dg2, db2 = _final_ln_loss(h1, f, ln2_g, ln2_b, tgt)

    mtn = functools.partial(_tiled_matmul_tn, tm=tm, out_dtype=BF16)
    d_wfo = mtn(sw, dr2, tk=D_FF, tn=D_MODEL, name="grad_w_ffn_out")
    ds = mm(dr2, w_fo, "nt", tn=D_FF, tc=D_MODEL, out_dtype=F32, name="d_swiglu")
    dau = _swiglu_bwd(au, ds)
    d_wfi = mtn(h1, dau, tk=D_MODEL, tn=D_FF, name="grad_w_ffn_in")
    dh1_ffn = mm(dau, w_fi, "nt", tn=D_MODEL, tc=D_FF, out_dtype=F32, name="d_h1_ffn")
    dr1, dg1, db1 = _ln_bwd_call(dr2, dh1_ffn, ALPHA, xhat1, rstd1, ln1_g, "ln1_bwd")
    d_wout = mtn(mixin, dr1, tk=D_MODEL, tn=D_MODEL, name="grad_w_out")
    dmixin = mm(dr1, w_out, "nt", tn=D_MODEL, tc=D_MODEL, out_dtype=F32, name="d_mixin")
    dy_hg, dy_att, dgates = _mix_bwd(dmixin, y_hg, y_att, gates)
    d_wbh = mtn(yh, dy_hg, tk=HG_W, tn=D_MODEL, name="grad_w_branch_hg")
    d_wba = mtn(oa, dy_att, tk=ATT_QW, tn=D_MODEL, name="grad_w_branch_attn")
    dyh = mm(dy_hg, w_bh, "nt", tn=HG_W, tc=D_MODEL, out_dtype=F32, name="d_yh")
    doa = mm(dy_att, w_ba, "nt", tn=ATT_QW, tc=D_MODEL, out_dtype=F32, name="d_oa")
    d_hg, d_lb, d_ng = _hgrn_bwd(proj_hg, lbounds, norm_g, lv, states, dyh)
    dq, dkv_cur, dkv_prev, dkv_meta, d_sink = _att_bwd(proj_att, cos, sin, sinks, doa)
    dproj = _assemble_dproj(d_hg, dq, dkv_cur, dkv_prev, dkv_meta, dgates)
    in_w = dproj.shape[1]
    d_win = mtn(h0, dproj, tk=D_MODEL, tn=in_w // 2, name="grad_w_in")
    dh0_proj = mm(dproj, w_in, "nt", tn=D_MODEL, tc=in_w // 2, out_dtype=F32, name="d_h0_proj")
    dxin, dg0, db0 = _ln_bwd_call(dr1, dh0_proj, ALPHA, xhat0, rstd0, ln_emb_g, "embed_ln_bwd", mask_from=PAD)

    grads = dict(meta_tokens=dxin[PAD:BLOCK], ln_emb_g=dg0, ln_emb_b=db0, w_in=d_win, hg_lower_bounds=d_lb,
                 hg_norm_g=d_ng, attn_sinks=d_sink, w_branch_hg=d_wbh, w_branch_attn=d_wba, w_out=d_wout,
                 ln1_g=dg1, ln1_b=db1, w_ffn_in=d_wfi, w_ffn_out=d_wfo, ln2_g=dg2, ln2_b=db2)
    return loss_part, dxin[BLOCK:], grads


def _place():
    return lax.axis_index("x"), lax.axis_index("y"), lax.axis_index("c")


def _all_gather(arrs, dtypes, name):
    n = len(arrs)

    def body(*refs):
        ins, outs, stages = refs[:n], refs[n:2 * n], refs[2 * n:3 * n]
        send_sems, recv_sems, local_sems = refs[3 * n:]
        x, y, c = _place()
        sibling = (x, y, 1 - c)
        chips = [(1 - x, y), (x, 1 - y), (1 - x, 1 - y)]
        slot = lambda px, py, pc: 4 * px + 2 * py + pc

        def copy(w, k, block, to, from_stage=False):
            return pltpu.make_async_remote_copy(
                src_ref=stages[w] if from_stage else outs[w].at[slot(*block)], dst_ref=outs[w].at[slot(*block)],
                send_sem=send_sems.at[w, k], recv_sem=recv_sems.at[w, k], device_id=to, device_id_type=MESH)

        mine, first, passed = [], [], []
        for w in range(n):
            stages[w][...] = ins[w][...].astype(dtypes[w])
            mine.append(pltpu.make_async_copy(stages[w], outs[w].at[slot(x, y, c)], local_sems.at[w]))
            mine[-1].start()
        for w in range(n):
            first.append(copy(w, 0, (x, y, c), sibling, from_stage=True))
            first += [copy(w, 1 + j, (x, y, c), (*chip, c), from_stage=True) for j, chip in enumerate(chips)]
        for cp in first:
            cp.start()
        for j, chip in enumerate(chips):
            for w in range(n):
                copy(w, 1 + j, (*chip, c), (x, y, c)).wait_recv()
                passed.append(copy(w, 4 + j, (*chip, c), sibling))
                passed[-1].start()
        for w in range(n):
            copy(w, 0, sibling, (x, y, c)).wait_recv()
            for j, chip in enumerate(chips):
                copy(w, 4 + j, (*chip, 1 - c), (x, y, c)).wait_recv()
        for cp in first + passed:
            cp.wait_send()
        for cp in mine:
            cp.wait()

    return pl.pallas_call(
        body, name=name,
        in_specs=[pl.BlockSpec(memory_space=pltpu.VMEM)] * n,
        out_specs=[pl.BlockSpec(memory_space=pl.ANY)] * n,
        out_shape=[jax.ShapeDtypeStruct((N_DEV,) + a.shape, dt) for a, dt in zip(arrs, dtypes)],
        scratch_shapes=[pltpu.VMEM(a.shape, dt) for a, dt in zip(arrs, dtypes)]
        + [pltpu.SemaphoreType.DMA((n, 7)), pltpu.SemaphoreType.DMA((n, 7)), pltpu.SemaphoreType.DMA((n,))],
        compiler_params=pltpu.CompilerParams(vmem_limit_bytes=VMEM_LIMIT_BYTES),
    )(*arrs)


def _sibling_exchange(parts, name):
    n = len(parts)

    def body(*refs):
        ins, outs = refs[:n], refs[n:2 * n]
        send_sems, recv_sems = refs[2 * n:]
        x, y, c = _place()
        copies = []
        for w in range(n):
            for q in range(4):
                copies.append(pltpu.make_async_remote_copy(
                    src_ref=ins[w].at[2 * q + (1 - c)], dst_ref=outs[w].at[q], send_sem=send_sems.at[w, q],
                    recv_sem=recv_sems.at[w, q], device_id=(x, y, 1 - c), device_id_type=MESH))
        for cp in copies:
            cp.start()
        for cp in copies:
            cp.wait()

    return pl.pallas_call(
        body, name=name, in_specs=[pl.BlockSpec(memory_space=pl.ANY)] * n,
        out_specs=[pl.BlockSpec(memory_space=pl.ANY)] * n,
        out_shape=[jax.ShapeDtypeStruct((4,) + a.shape[1:], a.dtype) for a in parts],
        scratch_shapes=[pltpu.SemaphoreType.DMA((n, 4)), pltpu.SemaphoreType.DMA((n, 4))],
    )(*parts)


def _chip_exchange(parts, name):
    n = len(parts)

    def body(*refs):
        ins, outs = refs[:n], refs[n:2 * n]
        send_sems, recv_sems = refs[2 * n:]
        x, y, c = _place()
        chips = [(1 - x, y), (x, 1 - y), (1 - x, 1 - y)]
        copies = []
        for w in range(n):
            for j, (px, py) in enumerate(chips):
                copies.append(pltpu.make_async_remote_copy(
                    src_ref=ins[w].at[2 * px + py], dst_ref=outs[w].at[j], send_sem=send_sems.at[w, j],
                    recv_sem=recv_sems.at[w, j], device_id=(px, py, c), device_id_type=MESH))
        for cp in copies:
            cp.start()
        for cp in copies:
            cp.wait()

    return pl.pallas_call(
        body, name=name, in_specs=[pl.BlockSpec(memory_space=pl.ANY)] * n,
        out_specs=[pl.BlockSpec(memory_space=pl.ANY)] * n,
        out_shape=[jax.ShapeDtypeStruct((3,) + a.shape[1:], a.dtype) for a in parts],
        scratch_shapes=[pltpu.SemaphoreType.DMA((n, 3)), pltpu.SemaphoreType.DMA((n, 3))],
    )(*parts)


def _shard_rows(rows):
    return rows if rows <= 512 else 256


def _sibling_sum(part, recv, core, name):
    _, r, cdim = part.shape
    tr = _shard_rows(r)

    def body(core_ref, a_ref, b_ref, o_ref):
        o_ref[...] = (a_ref[...].astype(F32) + b_ref[...].astype(F32)).astype(BF16)

    return pl.pallas_call(
        body, name=name,
        grid_spec=pltpu.PrefetchScalarGridSpec(
            num_scalar_prefetch=1, grid=(4, r // tr),
            in_specs=[pl.BlockSpec((1, tr, cdim), lambda q, i, core_ref: (2 * q + core_ref[0], i, 0)),
                      pl.BlockSpec((1, tr, cdim), lambda q, i, core_ref: (q, i, 0))],
            out_specs=pl.BlockSpec((1, tr, cdim), lambda q, i, core_ref: (q, i, 0))),
        out_shape=jax.ShapeDtypeStruct((4, r, cdim), BF16),
        compiler_params=_cparams(("arbitrary", "arbitrary")),
    )(core, part, recv)


def _adamw_math(w, g, m, v):
    m = ADAM_B1 * m + (1.0 - ADAM_B1) * g
    v = ADAM_B2 * v + (1.0 - ADAM_B2) * (g * g)
    m_hat = m / (1.0 - ADAM_B1 ** ADAM_STEP)
    v_hat = v / (1.0 - ADAM_B2 ** ADAM_STEP)
    delta = -ADAM_LR * (m_hat / (jnp.sqrt(v_hat) + ADAM_EPS) + ADAM_WD * w)
    return delta, m, v


def _reduce_adamw(chip_part, recv, chip_idx, w, m, v, name):
    r, cdim = w.shape
    tr = _shard_rows(r)

    def body(idx_ref, p_ref, r_ref, w_ref, m_ref, v_ref, g_out, d_out, m_out, v_out):
        g = p_ref[0].astype(F32)
        for j in range(3):
            g = g + r_ref[j].astype(F32)
        d, mn, vn = _adamw_math(w_ref[...], g, m_ref[...], v_ref[...])
        g_out[...] = g
        d_out[...] = d
        m_out[...] = mn
        v_out[...] = vn

    flat = pl.BlockSpec((tr, cdim), lambda i, idx_ref: (i, 0))
    return pl.pallas_call(
        body, name=name,
        grid_spec=pltpu.PrefetchScalarGridSpec(
            num_scalar_prefetch=1, grid=(r // tr,),
            in_specs=[pl.BlockSpec((1, tr, cdim), lambda i, idx_ref: (idx_ref[0], i, 0)),
                      pl.BlockSpec((3, tr, cdim), lambda i, idx_ref: (0, i, 0)), flat, flat, flat],
            out_specs=[flat] * 4),
        out_shape=[jax.ShapeDtypeStruct((r, cdim), F32)] * 4,
        compiler_params=_cparams(("arbitrary",)),
    )(chip_idx, chip_part, recv, w, m, v)


SMALL_ROWS = 64
_SMALL_LAYOUT = (("ln_emb_g", 8), ("ln_emb_b", 8), ("hg_lower_bounds", 8), ("hg_norm_g", 1), ("attn_sinks", 1),
                 ("ln1_g", 8), ("ln1_b", 8), ("ln2_g", 8), ("ln2_b", 8))
_LOSS_ROW = sum(r for _, r in _SMALL_LAYOUT)


def _pack_small(vals, loss_row=None):
    rows = []
    for name, nrows in _SMALL_LAYOUT:
        flat = vals[name].reshape(-1).astype(F32)
        flat = jnp.pad(flat, (0, nrows * BLOCK - flat.shape[0]))
        rows.append(flat.reshape(nrows, BLOCK))
    rows.append(jnp.zeros((1, BLOCK), F32) if loss_row is None else loss_row)
    packed = jnp.concatenate(rows, axis=0)
    return jnp.pad(packed, ((0, SMALL_ROWS - packed.shape[0]), (0, 0)))


def _unpack_small(packed, shapes):
    out, row = {}, 0
    for name, nrows in _SMALL_LAYOUT:
        size = math.prod(shapes[name])
        out[name] = packed[row:row + nrows].reshape(-1)[:size].reshape(shapes[name])
        row += nrows
    return out


def _small_reduce_adamw(gathered, w, m, v):
    def body(g_ref, w_ref, m_ref, v_ref, g_out, d_out, m_out, v_out, loss_out):
        g = g_ref[0]
        for s in range(1, N_DEV):
            g = g + g_ref[s]
        d, mn, vn = _adamw_math(w_ref[...], g, m_ref[...], v_ref[...])
        g_out[...] = g
        d_out[...] = d
        m_out[...] = mn
        v_out[...] = vn
        loss_out[...] = jnp.broadcast_to(jnp.sum(g_ref[:, _LOSS_ROW, :]), (1, BLOCK))

    shp = jax.ShapeDtypeStruct((SMALL_ROWS, BLOCK), F32)
    return pl.pallas_call(body, name="small_reduce_adamw",
                          out_shape=[shp] * 4 + [jax.ShapeDtypeStruct((1, BLOCK), F32)])(gathered, w, m, v)


_BIG = ("meta_tokens", "w_in", "w_branch_hg", "w_branch_attn", "w_out", "w_ffn_in", "w_ffn_out")
_COLUMN_SHARDED = ("meta_tokens", "w_in", "w_branch_hg", "w_branch_attn", "w_ffn_in")
_WEIGHTS = ("meta_tokens", "ln_emb_g", "ln_emb_b", "w_in", "hg_lower_bounds", "hg_norm_g", "attn_sinks",
            "w_branch_hg", "w_branch_attn", "w_out", "ln1_g", "ln1_b", "w_ffn_in", "w_ffn_out", "ln2_g", "ln2_b")


def _whole(name, gathered):
    _, r, c = gathered.shape
    if name in _COLUMN_SHARDED:
        return jnp.transpose(gathered, (1, 0, 2)).reshape(r, N_DEV * c)
    return gathered.reshape(N_DEV * r, c)


def _slots(name, whole):
    r, c = whole.shape
    if name in _COLUMN_SHARDED:
        return jnp.transpose(whole.reshape(r, N_DEV, c // N_DEV), (1, 0, 2))
    return whole.reshape(N_DEV, r // N_DEV, c)


def kernel(x, meta_tokens, ln_emb_g, ln_emb_b, w_in, hg_lower_bounds, hg_norm_g, attn_sinks, w_branch_hg, w_branch_attn, w_out, ln1_g, ln1_b, w_ffn_in, w_ffn_out, ln2_g, ln2_b, loss_target, m_meta_tokens, m_ln_emb_g, m_ln_emb_b, m_w_in, m_hg_lower_bounds, m_hg_norm_g, m_attn_sinks, m_w_branch_hg, m_w_branch_attn, m_w_out, m_ln1_g, m_ln1_b, m_w_ffn_in, m_w_ffn_out, m_ln2_g, m_ln2_b, v_meta_tokens, v_ln_emb_g, v_ln_emb_b, v_w_in, v_hg_lower_bounds, v_hg_norm_g, v_attn_sinks, v_w_branch_hg, v_w_branch_attn, v_w_out, v_ln1_g, v_ln1_b, v_w_ffn_in, v_w_ffn_out, v_ln2_g, v_ln2_b):
    given = dict(locals())
    weights = {n: given[n] for n in _WEIGHTS}
    mom1 = {n: given["m_" + n] for n in _WEIGHTS}
    mom2 = {n: given["v_" + n] for n in _WEIGHTS}
    shard2d = lambda a: a.reshape(a.shape[-2:])

    shards = [shard2d(weights[n]) for n in _BIG]
    gathered = _all_gather(shards, [F32] + [BF16] * (len(_BIG) - 1), "gather_weights")
    full = {n: _whole(n, g) for n, g in zip(_BIG, gathered)}

    loss_part, grad_x, grads = _local_step(
        x[0], loss_target[0], full["meta_tokens"], ln_emb_g.reshape(1, -1), ln_emb_b.reshape(1, -1), full["w_in"],
        hg_lower_bounds, hg_norm_g, attn_sinks, full["w_branch_hg"], full["w_branch_attn"], full["w_out"],
        ln1_g, ln1_b, full["w_ffn_in"], full["w_ffn_out"], ln2_g, ln2_b)

    xi, yi, ci = _place()
    core = ci.astype(jnp.int32).reshape(1)
    chip_idx = (2 * xi + yi).astype(jnp.int32).reshape(1)
    parts = [_slots(n, grads[n].astype(BF16)) for n in _BIG]
    from_sibling = _sibling_exchange(parts, "grads_to_sibling")
    chip_parts = [_sibling_sum(p, r, core, "sibling_sum_" + n) for n, p, r in zip(_BIG, parts, from_sibling)]
    from_chips = _chip_exchange(chip_parts, "grads_to_chips")
    out = {}
    for n, cp, rc in zip(_BIG, chip_parts, from_chips):
        res = _reduce_adamw(cp, rc, chip_idx, shard2d(weights[n]), shard2d(mom1[n]), shard2d(mom2[n]), "adamw_" + n)
        out[n] = [r.reshape(weights[n].shape) for r in res]

    small_names = [n for n, _ in _SMALL_LAYOUT]
    small_grads = dict(grads)
    small_grads["attn_sinks"] = grads["attn_sinks"][:, 0]
    packed = _pack_small(small_grads, loss_part)
    all_small, = _all_gather([packed], [F32], "gather_small")
    res = _small_reduce_adamw(all_small, _pack_small(weights), _pack_small(mom1), _pack_small(mom2))
    shapes = {n: weights[n].shape for n in small_names}
    unpacked = [_unpack_small(r, shapes) for r in res[:4]]
    for n in small_names:
        out[n] = [u[n] for u in unpacked]
    loss = res[4][0, 0]

    return (loss, grad_x[None], *[out[n][0] for n in _WEIGHTS], *[out[n][1] for n in _WEIGHTS],
            *[out[n][2] for n in _WEIGHTS], *[out[n][3] for n in _WEIGHTS])
```

```python
import functools
import math

import numpy as np
import jax
import jax.numpy as jnp
from jax import lax
from jax.experimental import pallas as pl
from jax.experimental.pallas import tpu as pltpu

F32 = jnp.float32
BF16 = jnp.bfloat16

D_MODEL = 1024
N_META = 16
BLOCK = 128
PAD = BLOCK - N_META
HG_HEADS = 4
HG_W = 512
ATT_HEADS = 8
HEAD_DIM = 64
ATT_QW = 512
ATT_KVW = 128
D_FF = 2816
EPS = 1e-5
ALPHA = 2.0 ** 0.25
ROPE_THETA = 10000.0
N_DEV = 8

ADAM_LR = 0.001
ADAM_B1 = 0.9
ADAM_B2 = 0.999
ADAM_EPS = 1e-08
ADAM_WD = 0.01
ADAM_STEP = 10

VMEM_LIMIT_BYTES = 56 * 1024 * 1024
MESH = pl.DeviceIdType.MESH

_LEVELS = (64, 32, 16, 8, 4, 2, 1)


def _cparams(sem):
    return pltpu.CompilerParams(dimension_semantics=sem, vmem_limit_bytes=VMEM_LIMIT_BYTES)


def _row_tile(rows, target):
    nb = rows // BLOCK
    best = 1
    for d in range(1, nb + 1):
        if nb % d == 0 and d * BLOCK <= target:
            best = d
    return best * BLOCK


_DN = {"nn": (((1,), (0,)), ((), ())), "nt": (((1,), (1,)), ((), ())), "tn": (((0,), (0,)), ((), ()))}


def _dot(a, b, form):
    return lax.dot_general(a.astype(BF16), b.astype(BF16), _DN[form], preferred_element_type=F32)


@functools.partial(jax.custom_vjp, nondiff_argnums=(2,))
def _mm(a, b, form):
    return _dot(a, b, form)


def _mm_fwd(a, b, form):
    return _dot(a, b, form), (a, b)


def _mm_bwd(form, res, g):
    a, b = res
    if form == "nn":
        return _dot(g, b, "nt"), _dot(a, g, "tn")
    if form == "nt":
        return _dot(g, b, "nn"), _dot(g, a, "tn")
    return _dot(b, g, "nt"), _dot(a, g, "nn")


_mm.defvjp(_mm_fwd, _mm_bwd)


def _split_dot(lv, x, form):
    hi = x.astype(BF16)
    lo = (x - hi.astype(F32)).astype(BF16)
    return (lax.dot_general(lv, hi, _DN[form], preferred_element_type=F32)
            + lax.dot_general(lv, lo, _DN[form], preferred_element_type=F32))


@jax.custom_vjp
def _swap_halves(x):
    return pltpu.roll(x, 64, 1)


_swap_halves.defvjp(lambda x: (pltpu.roll(x, 64, 1), None), lambda _, g: (pltpu.roll(g, 64, 1),))


def _tiled_matmul(a, b, form, *, tm, tn, tc, out_dtype, name):
    m, c = a.shape
    n = b.shape[1] if form == "nn" else b.shape[0]
    assert m % tm == 0 and n % tn == 0 and c % tc == 0, (name, a.shape, b.shape, tm, tn, tc)
    nc = c // tc

    def body(a_ref, b_ref, o_ref, *scratch):
        part = _dot(a_ref[...], b_ref[...], form)
        if nc == 1:
            o_ref[...] = part.astype(out_dtype)
            return
        acc_ref, = scratch
        ci = pl.program_id(2)

        @pl.when(ci == 0)
        def _():
            acc_ref[...] = part

        @pl.when(ci > 0)
        def _():
            acc_ref[...] += part

        @pl.when(ci == nc - 1)
        def _():
            o_ref[...] = acc_ref[...].astype(out_dtype)

    b_spec = (pl.BlockSpec((tc, tn), lambda j, i, k: (k, j)) if form == "nn"
              else pl.BlockSpec((tn, tc), lambda j, i, k: (j, k)))
    return pl.pallas_call(
        body, name=name, grid=(n // tn, m // tm, nc),
        in_specs=[pl.BlockSpec((tm, tc), lambda j, i, k: (i, k)), b_spec],
        out_specs=pl.BlockSpec((tm, tn), lambda j, i, k: (i, j)),
        out_shape=jax.ShapeDtypeStruct((m, n), out_dtype),
        scratch_shapes=[] if nc == 1 else [pltpu.VMEM((tm, tn), F32)],
        compiler_params=_cparams(("arbitrary", "arbitrary", "arbitrary")),
    )(a, b)


def _tiled_matmul_tn(a, b, *, tm, tk, tn, out_dtype, name):
    m, k = a.shape
    n = b.shape[1]
    assert m % tm == 0 and k % tk == 0 and n % tn == 0, (name, a.shape, b.shape, tm, tk, tn)
    nm = m // tm

    def body(a_ref, b_ref, o_ref, acc_ref):
        part = _dot(a_ref[...], b_ref[...], "tn")
        mi = pl.program_id(2)

        @pl.when(mi == 0)
        def _():
            acc_ref[...] = part

        @pl.when(mi > 0)
        def _():
            acc_ref[...] += part

        @pl.when(mi == nm - 1)
        def _():
            o_ref[...] = acc_ref[...].astype(out_dtype)

    return pl.pallas_call(
        body, name=name, grid=(k // tk, n // tn, nm),
        in_specs=[pl.BlockSpec((tm, tk), lambda kk, j, i: (i, kk)), pl.BlockSpec((tm, tn), lambda kk, j, i: (i, j))],
        out_specs=pl.BlockSpec((tk, tn), lambda kk, j, i: (kk, j)),
        out_shape=jax.ShapeDtypeStruct((k, n), out_dtype),
        scratch_shapes=[pltpu.VMEM((tk, tn), F32)],
        compiler_params=_cparams(("arbitrary", "arbitrary", "arbitrary")),
    )(a, b)


def _ln_stats(r):
    mu = jnp.mean(r, axis=-1, keepdims=True)
    xc = r - mu
    var = jnp.mean(xc * xc, axis=-1, keepdims=True)
    rstd = lax.rsqrt(var + EPS)
    return xc * rstd, rstd


def _ln_bwd(dy, xhat, rstd, g):
    dxhat = dy * g
    m1 = jnp.mean(dxhat, axis=-1, keepdims=True)
    m2 = jnp.mean(dxhat * xhat, axis=-1, keepdims=True)
    dr = rstd * (dxhat - m1 - xhat * m2)
    return dr, jnp.sum(dy * xhat, axis=0, keepdims=True), jnp.sum(dy, axis=0, keepdims=True)


N_SEG = 3 + len(_LEVELS)


def _level_stack():
    t = np.arange(BLOCK)[:, None]
    r = np.arange(BLOCK)[None, :]
    mats = [r <= t, r > t, np.ones((BLOCK, BLOCK), bool)]
    for h in _LEVELS:
        same = (t // (2 * h)) == (r // (2 * h))
        up_t, up_r = (t % (2 * h)) >= h, (r % (2 * h)) >= h
        mats.append(same & ((up_t & up_r & (r <= t)) | (~up_t & ~up_r & (r > t))))
    return jnp.asarray(np.concatenate(mats, axis=0).astype(np.float32), dtype=BF16)


def _hgrn_gates(hf, a0, a1, valid):
    lb = jax.nn.sigmoid(a0 - a1)
    fg = lb + (1.0 - lb) * jax.nn.sigmoid(hf)
    return jnp.where(valid, jnp.log(fg), 0.0), jnp.where(valid, 1.0 - fg, 0.0)


def _hgrn_head(hq, k, v, hg, ng, st_in, *seg):
    q = jax.nn.silu(hq)
    rows = lax.broadcasted_iota(jnp.int32, (BLOCK, BLOCK), 0)
    cols = lax.broadcasted_iota(jnp.int32, (BLOCK, BLOCK), 1)
    o = _mm(q * jnp.exp(seg[0]), st_in, "nt")
    a = jnp.where(rows == cols, jnp.sum(q * k, axis=-1, keepdims=True), 0.0)
    for li, h in enumerate(_LEVELS):
        decay = jnp.exp(seg[3 + li])
        pair = ((rows // (2 * h)) == (cols // (2 * h))) & ((rows % (2 * h)) >= h) & ((cols % (2 * h)) < h)
        a = a + jnp.where(pair, _mm(q * decay, k * decay, "nt"), 0.0)
    o = o + _mm(a, v, "nn")
    st_out = st_in * jnp.exp(seg[2]) + _mm(v, k * jnp.exp(seg[1]), "tn")
    on = o * lax.rsqrt(jnp.mean(o * o, axis=-1, keepdims=True) + EPS) * ng
    return on * jax.nn.silu(hg), st_out


def _seg_blocks(e, h):
    return [e[i * BLOCK:(i + 1) * BLOCK, h * BLOCK:(h + 1) * BLOCK] for i in range(N_SEG)]


def _rope(x, cos, sin, first_half):
    partner = jnp.where(first_half, -pltpu.roll(x, 96, 1), pltpu.roll(x, 32, 1))
    return x * cos + partner * sin


def _rope_t(g, cos, sin, first_half):
    u = g * sin
    partner = jnp.where(first_half, pltpu.roll(u, 96, 1), -pltpu.roll(u, 32, 1))
    return g * cos + partner


def _att_core(q0, q1, q2, q3, km, kp, kc, vm, vp, vc, sinkcol0, sinkcol1, ok_m, ok_p, ok_c):
    lane = lax.broadcasted_iota(jnp.int32, (BLOCK, BLOCK), 1)
    low = lane < HEAD_DIM
    scale = HEAD_DIM ** -0.5
    neg = jnp.finfo(F32).min
    outs = []
    for g, (qa, qb, sinkcol) in enumerate(((q0, q1, sinkcol0), (q2, q3, sinkcol1))):
        def both(x, g=g):
            sw = _swap_halves(x)
            return jnp.where(low, x, sw) if g == 0 else jnp.where(low, sw, x)
        q4 = jnp.concatenate([jnp.where(low, qa, 0.0), jnp.where(low, 0.0, qa),
                              jnp.where(low, qb, 0.0), jnp.where(low, 0.0, qb)], axis=0)
        s = []
        for kk, ok in ((km, ok_m), (kp, ok_p), (kc, ok_c)):
            ok4 = jnp.concatenate([ok] * 4, axis=0)
            s.append(jnp.where(ok4, _mm(q4, both(kk), "nt") * scale, neg))
        mx = jnp.maximum(jnp.maximum(jnp.max(s[0], axis=-1, keepdims=True), jnp.max(s[1], axis=-1, keepdims=True)),
                         jnp.maximum(jnp.max(s[2], axis=-1, keepdims=True), sinkcol))
        mx = lax.stop_gradient(mx)
        p = [jnp.exp(si - mx) for si in s]
        den = (jnp.sum(p[0], axis=-1, keepdims=True) + jnp.sum(p[1], axis=-1, keepdims=True)
               + jnp.sum(p[2], axis=-1, keepdims=True) + jnp.exp(sinkcol - mx))
        inv = 1.0 / den
        o4 = (_mm(p[0] * inv, both(vm), "nn") + _mm(p[1] * inv, both(vp), "nn") + _mm(p[2] * inv, both(vc), "nn"))
        for j in range(2):
            outs.append(jnp.where(low, o4[(2 * j) * BLOCK:(2 * j + 1) * BLOCK],
                                  o4[(2 * j + 1) * BLOCK:(2 * j + 2) * BLOCK]))
    return jnp.concatenate(outs, axis=1)


def _att_masks(blk_idx):
    qpos = blk_idx * BLOCK + lax.broadcasted_iota(jnp.int32, (BLOCK, BLOCK), 0) - PAD
    kidx = lax.broadcasted_iota(jnp.int32, (BLOCK, BLOCK), 1)
    pos_m = kidx - PAD
    pos_p = (blk_idx - 1) * BLOCK + kidx - PAD
    pos_c = blk_idx * BLOCK + kidx - PAD
    ok_m = (pos_m >= 0) & (pos_m <= qpos)
    ok_p = (pos_p >= N_META) & (qpos - pos_p < BLOCK) & (blk_idx >= 1)
    ok_c = (pos_c >= N_META) & (pos_c <= qpos)
    return ok_m, ok_p, ok_c


def _embed_ln(xin, g0, b0):
    p = xin.shape[0]
    tr = _row_tile(p, 640)

    def body(x_ref, g_ref, b_ref, h_ref, xh_ref, rs_ref):
        xhat, rstd = _ln_stats(x_ref[...])
        row = pl.program_id(0) * tr + lax.broadcasted_iota(jnp.int32, (tr, 1), 0)
        h_ref[...] = jnp.where(row >= PAD, xhat * g_ref[...] + b_ref[...], 0.0)
        xh_ref[...] = xhat
        rs_ref[...] = rstd

    vec = pl.BlockSpec((1, D_MODEL), lambda i: (0, 0))
    rowsp = pl.BlockSpec((tr, D_MODEL), lambda i: (i, 0))
    return pl.pallas_call(
        body, name="embed_ln", grid=(p // tr,), in_specs=[rowsp, vec, vec],
        out_specs=[rowsp, rowsp, pl.BlockSpec((tr, 1), lambda i: (i, 0))],
        out_shape=[jax.ShapeDtypeStruct((p, D_MODEL), F32)] * 2 + [jax.ShapeDtypeStruct((p, 1), F32)],
        compiler_params=_cparams(("arbitrary",)),
    )(xin, g0, b0)


def _hgrn_fwd(proj_hg, lbounds, norm_g, lv):
    p = proj_hg.shape[0]
    nb = p // BLOCK

    def body(x_ref, lb_ref, ng_ref, lv_ref, y_ref, st_ref, carry_ref):
        c = pl.program_id(0)

        @pl.when(c == 0)
        def _():
            carry_ref[...] = jnp.zeros_like(carry_ref)

        valid = (c * BLOCK + lax.broadcasted_iota(jnp.int32, (BLOCK, 1), 0)) >= PAD
        logf, k = _hgrn_gates(x_ref[:, HG_W:2 * HG_W], lb_ref[0:1, :], lb_ref[1:2, :], valid)
        e = _split_dot(lv_ref[...], logf, "nn")
        for h in range(HG_HEADS):
            sl = lambda part: x_ref[:, part * HG_W + h * BLOCK: part * HG_W + (h + 1) * BLOCK]
            hs = slice(h * BLOCK, (h + 1) * BLOCK)
            st_in = carry_ref[h]
            st_ref[0, h] = st_in
            y, st_out = _hgrn_head(sl(0), k[:, hs], sl(2), sl(3), ng_ref[...], st_in, *_seg_blocks(e, h))
            y_ref[:, hs] = y
            carry_ref[h] = st_out

    return pl.pallas_call(
        body, name="hgrn_fwd", grid=(nb,),
        in_specs=[pl.BlockSpec((BLOCK, 4 * HG_W), lambda c: (c, 0)), pl.BlockSpec((2, HG_W), lambda c: (0, 0)),
                  pl.BlockSpec((1, BLOCK), lambda c: (0, 0)), pl.BlockSpec(lv.shape, lambda c: (0, 0))],
        out_specs=[pl.BlockSpec((BLOCK, HG_W), lambda c: (c, 0)),
                   pl.BlockSpec((1, HG_HEADS, BLOCK, BLOCK), lambda c: (c, 0, 0, 0))],
        out_shape=[jax.ShapeDtypeStruct((p, HG_W), F32), jax.ShapeDtypeStruct((nb, HG_HEADS, BLOCK, BLOCK), F32)],
        scratch_shapes=[pltpu.VMEM((HG_HEADS, BLOCK, BLOCK), F32)],
        compiler_params=_cparams(("arbitrary",)),
    )(proj_hg, lbounds, norm_g, lv)


def _rope_tables(p):
    pos = (jnp.arange(p, dtype=jnp.int32) - PAD).astype(F32)
    half = HEAD_DIM // 2
    inv = ROPE_THETA ** (-jnp.arange(half, dtype=F32) / half)
    ang = pos[:, None] * jnp.tile(inv, BLOCK // half)[None, :]
    return jnp.cos(ang), jnp.sin(ang)


def _att_sinkcols(sink_ref):
    rowhead = lax.broadcasted_iota(jnp.int32, (4 * BLOCK, 1), 0) // BLOCK
    cols = []
    for g in range(2):
        col = jnp.zeros((4 * BLOCK, 1), F32)
        for j in range(4):
            col = jnp.where(rowhead == j, sink_ref[0, 4 * g + j], col)
        cols.append(col)
    return cols


def _att_load(qkv_ref, cos_ref, sin_ref, first_half, with_q):
    cos, sin = cos_ref[...], sin_ref[...]
    qs = [_rope(qkv_ref[:, j * BLOCK:(j + 1) * BLOCK], cos, sin, first_half) for j in range(4)] if with_q else None
    k = _rope(qkv_ref[:, ATT_QW:ATT_QW + ATT_KVW], cos, sin, first_half)
    v = qkv_ref[:, ATT_QW + ATT_KVW:ATT_QW + 2 * ATT_KVW]
    return qs, k, v


def _att_specs(nb):
    w = ATT_QW + 2 * ATT_KVW
    cur = lambda width: pl.BlockSpec((BLOCK, width), lambda i: (i, 0))
    prev = lambda width: pl.BlockSpec((BLOCK, width), lambda i: (jnp.maximum(i - 1, 0), 0))
    meta = lambda width: pl.BlockSpec((BLOCK, width), lambda i: (0, 0))
    return [cur(w), prev(w), meta(w), cur(BLOCK), cur(BLOCK), prev(BLOCK), prev(BLOCK), meta(BLOCK), meta(BLOCK),
            pl.BlockSpec(memory_space=pltpu.SMEM)]


def _att_fwd(proj_att, cos, sin, sinks):
    p = proj_att.shape[0]
    nb = p // BLOCK

    def body(cur_ref, prev_ref, meta_ref, cc, sc, cp, sp, cm, sm, sink_ref, o_ref):
        i = pl.program_id(0)
        fh = (lax.broadcasted_iota(jnp.int32, (BLOCK, BLOCK), 1) % HEAD_DIM) < (HEAD_DIM // 2)
        qs, kc, vc = _att_load(cur_ref, cc, sc, fh, True)
        _, kp, vp = _att_load(prev_ref, cp, sp, fh, False)
        _, km, vm = _att_load(meta_ref, cm, sm, fh, False)
        s0, s1 = _att_sinkcols(sink_ref)
        o_ref[...] = _att_core(*qs, km, kp, kc, vm, vp, vc, s0, s1, *_att_masks(i))

    return pl.pallas_call(
        body, name="att_fwd", grid=(nb,), in_specs=_att_specs(nb),
        out_specs=pl.BlockSpec((BLOCK, ATT_QW), lambda i: (i, 0)),
        out_shape=jax.ShapeDtypeStruct((p, ATT_QW), F32),
        compiler_params=_cparams(("arbitrary",)),
    )(proj_att, proj_att, proj_att, cos, sin, cos, sin, cos, sin, sinks)


def _tile(rows, preferred):
    return preferred if rows % preferred == 0 else _row_tile(rows, preferred)


def _branch_mix(yh, oa, gates, w_bh, w_ba):
    y_hg = _dot(yh, w_bh, "nn")
    y_att = _dot(oa, w_ba, "nn")
    s1 = jax.nn.sigmoid(gates[:, :D_MODEL])
    s2 = jax.nn.sigmoid(gates[:, D_MODEL:])
    return s1 * y_hg + s2 * y_att, y_hg, y_att, s1, s2


def _mix_out_ln1(yh, oa, gates, h0, w_bh, w_ba, w_out, g1, b1):
    p = yh.shape[0]
    tr = _tile(p, 320)

    def body(yh_ref, oa_ref, g_ref, h0_ref, wbh_ref, wba_ref, wo_ref, g1_ref, b1_ref, mix_ref, h1_ref, xh_ref, rs_ref):
        mixin = _branch_mix(yh_ref[...], oa_ref[...], g_ref[...], wbh_ref[...], wba_ref[...])[0]
        mix_ref[...] = mixin.astype(BF16)
        xhat, rstd = _ln_stats(ALPHA * h0_ref[...] + _dot(mixin, wo_ref[...], "nn"))
        h1_ref[...] = xhat * g1_ref[...] + b1_ref[...]
        xh_ref[...] = xhat
        rs_ref[...] = rstd

    row = lambda w: pl.BlockSpec((tr, w), lambda i: (i, 0))
    const = lambda a: pl.BlockSpec(a.shape, lambda i: (0, 0))
    return pl.pallas_call(
        body, name="mix_out_ln1", grid=(p // tr,),
        in_specs=[row(HG_W), row(ATT_QW), row(2 * D_MODEL), row(D_MODEL), const(w_bh), const(w_ba), const(w_out),
                  const(g1), const(b1)],
        out_specs=[row(D_MODEL), row(D_MODEL), row(D_MODEL), row(1)],
        out_shape=[jax.ShapeDtypeStruct((p, D_MODEL), BF16), jax.ShapeDtypeStruct((p, D_MODEL), F32),
                   jax.ShapeDtypeStruct((p, D_MODEL), F32), jax.ShapeDtypeStruct((p, 1), F32)],
        compiler_params=_cparams(("arbitrary",)),
    )(yh, oa, gates, h0, w_bh, w_ba, w_out, g1, b1)


FF_T = D_FF // 2


def _ffn_in_swiglu(h1, w_fi):
    p = h1.shape[0]
    tm = _row_tile(p, 640)

    def body(h_ref, w_ref, au_ref, s_ref):
        au = _dot(h_ref[...], w_ref[...], "nn")
        au_ref[...] = au
        s_ref[...] = (jax.nn.silu(au[:, :FF_T]) * au[:, FF_T:]).astype(BF16)

    return pl.pallas_call(
        body, name="ffn_in_swiglu", grid=(D_FF // FF_T, p // tm),
        in_specs=[pl.BlockSpec((tm, D_MODEL), lambda j, i: (i, 0)), pl.BlockSpec((D_MODEL, 2 * FF_T), lambda j, i: (0, j))],
        out_specs=[pl.BlockSpec((tm, 2 * FF_T), lambda j, i: (i, j)), pl.BlockSpec((tm, FF_T), lambda j, i: (i, j))],
        out_shape=[jax.ShapeDtypeStruct((p, 2 * D_FF), F32), jax.ShapeDtypeStruct((p, D_FF), BF16)],
        compiler_params=_cparams(("arbitrary", "arbitrary")),
    )(h1, w_fi)


def _ffn_out_loss(s, w_fo, h1, g2, b2, target):
    p = h1.shape[0]
    tr = _row_tile(p, 640)

    def body(s_ref, w_ref, h_ref, g_ref, b_ref, t_ref, dr_ref, loss_ref, dg_ref, db_ref):
        i = pl.program_id(0)
        xhat, rstd = _ln_stats(ALPHA * h_ref[...] + _dot(s_ref[...], w_ref[...], "nn"))
        y = xhat * g_ref[...] + b_ref[...]
        row = i * tr + lax.broadcasted_iota(jnp.int32, (tr, 1), 0)
        err = jnp.where(row >= BLOCK, y - t_ref[...], 0.0)
        dr, dg, db = _ln_bwd(err * (1.0 / D_MODEL), xhat, rstd, g_ref[...])
        dr_ref[...] = dr
        e2 = jnp.sum(err * err, axis=0, keepdims=True)
        part = e2[:, 0:BLOCK]
        for j in range(1, D_MODEL // BLOCK):
            part = part + e2[:, j * BLOCK:(j + 1) * BLOCK]
        part = part * (0.5 / D_MODEL)

        @pl.when(i == 0)
        def _():
            loss_ref[...] = part
            dg_ref[...] = dg
            db_ref[...] = db

        @pl.when(i > 0)
        def _():
            loss_ref[...] += part
            dg_ref[...] += dg
            db_ref[...] += db

    vec = pl.BlockSpec((1, D_MODEL), lambda i: (0, 0))
    rowsp = pl.BlockSpec((tr, D_MODEL), lambda i: (i, 0))
    return pl.pallas_call(
        body, name="ffn_out_loss", grid=(p // tr,),
        in_specs=[pl.BlockSpec((tr, D_FF), lambda i: (i, 0)), pl.BlockSpec((D_FF, D_MODEL), lambda i: (0, 0)),
                  rowsp, vec, vec, rowsp],
        out_specs=[rowsp, pl.BlockSpec((1, BLOCK), lambda i: (0, 0)), vec, vec],
        out_shape=[jax.ShapeDtypeStruct((p, D_MODEL), F32), jax.ShapeDtypeStruct((1, BLOCK), F32),
                   jax.ShapeDtypeStruct((1, D_MODEL), F32), jax.ShapeDtypeStruct((1, D_MODEL), F32)],
        compiler_params=_cparams(("arbitrary",)),
    )(s, w_fo, h1, g2, b2, target)


def _d_ffn_hidden(dr2, w_fo, au):
    p = au.shape[0]
    tm = _row_tile(p, 640)

    def body(d_ref, w_ref, au_ref, o_ref):
        ds = _dot(d_ref[...], w_ref[...], "nt")
        _, vjp = jax.vjp(lambda a, u: jax.nn.silu(a) * u, au_ref[:, :FF_T], au_ref[:, FF_T:])
        da, du = vjp(ds)
        o_ref[:, :FF_T] = da.astype(BF16)
        o_ref[:, FF_T:] = du.astype(BF16)

    return pl.pallas_call(
        body, name="d_ffn_hidden", grid=(D_FF // FF_T, p // tm),
        in_specs=[pl.BlockSpec((tm, D_MODEL), lambda j, i: (i, 0)), pl.BlockSpec((FF_T, D_MODEL), lambda j, i: (j, 0)),
                  pl.BlockSpec((tm, 2 * FF_T), lambda j, i: (i, j))],
        out_specs=pl.BlockSpec((tm, 2 * FF_T), lambda j, i: (i, j)),
        out_shape=jax.ShapeDtypeStruct((p, 2 * D_FF), BF16), compiler_params=_cparams(("arbitrary", "arbitrary")),
    )(dr2, w_fo, au)


def _ln_bwd_call(d_a, d_b, scale_a, xhat, rstd, g, name, mask_from=None):
    p = xhat.shape[0]
    tr = _row_tile(p, 640)

    def body(a_ref, b_ref, xh_ref, rs_ref, g_ref, dr_ref, dg_ref, db_ref):
        i = pl.program_id(0)
        dy = scale_a * a_ref[...] + b_ref[...]
        if mask_from is not None:
            row = i * tr + lax.broadcasted_iota(jnp.int32, (tr, 1), 0)
            dy = jnp.where(row >= mask_from, dy, 0.0)
        dr, dg, db = _ln_bwd(dy, xh_ref[...], rs_ref[...], g_ref[...])
        dr_ref[...] = dr

        @pl.when(i == 0)
        def _():
            dg_ref[...] = dg
            db_ref[...] = db

        @pl.when(i > 0)
        def _():
            dg_ref[...] += dg
            db_ref[...] += db

    vec = pl.BlockSpec((1, D_MODEL), lambda i: (0, 0))
    rowsp = pl.BlockSpec((tr, D_MODEL), lambda i: (i, 0))
    return pl.pallas_call(
        body, name=name, grid=(p // tr,),
        in_specs=[rowsp, rowsp, rowsp, pl.BlockSpec((tr, 1), lambda i: (i, 0)), vec],
        out_specs=[rowsp, vec, vec],
        out_shape=[jax.ShapeDtypeStruct((p, D_MODEL), F32)] + [jax.ShapeDtypeStruct((1, D_MODEL), F32)] * 2,
        compiler_params=_cparams(("arbitrary",)),
    )(d_a, d_b, xhat, rstd, g)


def _ln1_mix_bwd(dr2, dh1_ffn, xhat1, rstd1, g1, yh, oa, gates, w_bh, w_ba, w_out):
    p = yh.shape[0]
    tr = _tile(p, 320)

    def body(a_ref, b_ref, xh_ref, rs_ref, g1_ref, yh_ref, oa_ref, g_ref, wbh_ref, wba_ref, wo_ref,
             dr_ref, dyhg_ref, dyat_ref, dgt_ref, dyh_ref, doa_ref, dg_ref, db_ref):
        i = pl.program_id(0)
        dr, dg, db = _ln_bwd(ALPHA * a_ref[...] + b_ref[...], xh_ref[...], rs_ref[...], g1_ref[...])
        dr_ref[...] = dr
        d = _dot(dr, wo_ref[...], "nt")
        _, y_hg, y_att, s1, s2 = _branch_mix(yh_ref[...], oa_ref[...], g_ref[...], wbh_ref[...], wba_ref[...])
        dy_hg = d * s1
        dy_att = d * s2
        dyhg_ref[...] = dy_hg.astype(BF16)
        dyat_ref[...] = dy_att.astype(BF16)
        dgt_ref[:, :D_MODEL] = (d * y_hg * s1 * (1.0 - s1)).astype(BF16)
        dgt_ref[:, D_MODEL:] = (d * y_att * s2 * (1.0 - s2)).astype(BF16)
        dyh_ref[...] = _dot(dy_hg, wbh_ref[...], "nt")
        doa_ref[...] = _dot(dy_att, wba_ref[...], "nt")

        @pl.when(i == 0)
        def _():
            dg_ref[...] = dg
            db_ref[...] = db

        @pl.when(i > 0)
        def _():
            dg_ref[...] += dg
            db_ref[...] += db

    row = lambda w: pl.BlockSpec((tr, w), lambda i: (i, 0))
    const = lambda a: pl.BlockSpec(a.shape, lambda i: (0, 0))
    vec = pl.BlockSpec((1, D_MODEL), lambda i: (0, 0))
    return pl.pallas_call(
        body, name="ln1_mix_bwd", grid=(p // tr,),
        in_specs=[row(D_MODEL), row(D_MODEL), row(D_MODEL), row(1), vec, row(HG_W), row(ATT_QW), row(2 * D_MODEL),
                  const(w_bh), const(w_ba), const(w_out)],
        out_specs=[row(D_MODEL), row(D_MODEL), row(D_MODEL), row(2 * D_MODEL), row(HG_W), row(ATT_QW), vec, vec],
        out_shape=[jax.ShapeDtypeStruct((p, D_MODEL), F32), jax.ShapeDtypeStruct((p, D_MODEL), BF16),
                   jax.ShapeDtypeStruct((p, D_MODEL), BF16), jax.ShapeDtypeStruct((p, 2 * D_MODEL), BF16),
                   jax.ShapeDtypeStruct((p, HG_W), F32), jax.ShapeDtypeStruct((p, ATT_QW), F32),
                   jax.ShapeDtypeStruct((1, D_MODEL), F32), jax.ShapeDtypeStruct((1, D_MODEL), F32)],
        compiler_params=_cparams(("arbitrary",)),
    )(dr2, dh1_ffn, xhat1, rstd1, g1, yh, oa, gates, w_bh, w_ba, w_out)


def _hgrn_bwd(proj_hg, lbounds, norm_g, lv, states, dyh):
    p = proj_hg.shape[0]
    nb = p // BLOCK
    rev = lambda c: nb - 1 - c

    def body(x_ref, lb_ref, ng_ref, lv_ref, st_ref, dy_ref, dx_ref, dlb_ref, dng_ref, dcarry_ref):
        step = pl.program_id(0)
        c = rev(step)

        @pl.when(step == 0)
        def _():
            dcarry_ref[...] = jnp.zeros_like(dcarry_ref)
            dlb_ref[...] = jnp.zeros_like(dlb_ref)
            dng_ref[...] = jnp.zeros_like(dng_ref)

        valid = (c * BLOCK + lax.broadcasted_iota(jnp.int32, (BLOCK, 1), 0)) >= PAD
        (logf, k), gates_vjp = jax.vjp(lambda hf, a0, a1: _hgrn_gates(hf, a0, a1, valid),
                                       x_ref[:, HG_W:2 * HG_W], lb_ref[0:1, :], lb_ref[1:2, :])
        lvv = lv_ref[...]
        e = _split_dot(lvv, logf, "nn")
        dng = jnp.zeros((1, BLOCK), F32)
        dk, dseg = [], []
        for h in range(HG_HEADS):
            sl = lambda part: x_ref[:, part * HG_W + h * BLOCK: part * HG_W + (h + 1) * BLOCK]
            hs = slice(h * BLOCK, (h + 1) * BLOCK)
            _, vjp = jax.vjp(_hgrn_head, sl(0), k[:, hs], sl(2), sl(3), ng_ref[...], st_ref[0, h], *_seg_blocks(e, h))
            dhq, dkh, dhi, dhg, dngh, dst, *dsegh = vjp((dy_ref[:, hs], dcarry_ref[h]))
            for part, val in ((0, dhq), (2, dhi), (3, dhg)):
                dx_ref[:, part * HG_W + h * BLOCK: part * HG_W + (h + 1) * BLOCK] = val
            dk.append(dkh)
            dseg.append(jnp.concatenate(dsegh, axis=0))
            dng = dng + dngh
            dcarry_ref[h] = dst
        dlogf = _split_dot(lvv, jnp.concatenate(dseg, axis=1), "tn")
        dhf, da0, da1 = gates_vjp((dlogf, jnp.concatenate(dk, axis=1)))
        dx_ref[:, HG_W:2 * HG_W] = dhf
        dlb_ref[0:1, :] += da0
        dlb_ref[1:2, :] += da1
        dng_ref[...] += dng

    return pl.pallas_call(
        body, name="hgrn_bwd", grid=(nb,),
        in_specs=[pl.BlockSpec((BLOCK, 4 * HG_W), lambda s: (rev(s), 0)), pl.BlockSpec((2, HG_W), lambda s: (0, 0)),
                  pl.BlockSpec((1, BLOCK), lambda s: (0, 0)), pl.BlockSpec(lv.shape, lambda s: (0, 0)),
                  pl.BlockSpec((1, HG_HEADS, BLOCK, BLOCK), lambda s: (rev(s), 0, 0, 0)),
                  pl.BlockSpec((BLOCK, HG_W), lambda s: (rev(s), 0))],
        out_specs=[pl.BlockSpec((BLOCK, 4 * HG_W), lambda s: (rev(s), 0)), pl.BlockSpec((2, HG_W), lambda s: (0, 0)),
                   pl.BlockSpec((1, BLOCK), lambda s: (0, 0))],
        out_shape=[jax.ShapeDtypeStruct((p, 4 * HG_W), F32), jax.ShapeDtypeStruct((2, HG_W), F32),
                   jax.ShapeDtypeStruct((1, BLOCK), F32)],
        scratch_shapes=[pltpu.VMEM((HG_HEADS, BLOCK, BLOCK), F32)],
        compiler_params=_cparams(("arbitrary",)),
    )(proj_hg, lbounds, norm_g, lv, states, dyh)


def _att_bwd(proj_att, cos, sin, sinks, doa):
    p = proj_att.shape[0]
    nb = p // BLOCK

    def body(cur_ref, prev_ref, meta_ref, cc, sc, cp, sp, cm, sm, sink_ref, do_ref,
             dq_ref, dcur_ref, dprev_ref, dmeta_ref, dsink_ref):
        i = pl.program_id(0)
        fh = (lax.broadcasted_iota(jnp.int32, (BLOCK, BLOCK), 1) % HEAD_DIM) < (HEAD_DIM // 2)
        qs, kc, vc = _att_load(cur_ref, cc, sc, fh, True)
        _, kp, vp = _att_load(prev_ref, cp, sp, fh, False)
        _, km, vm = _att_load(meta_ref, cm, sm, fh, False)
        s0, s1 = _att_sinkcols(sink_ref)
        masks = _att_masks(i)
        fn = lambda *a: _att_core(*a, *masks)
        _, vjp = jax.vjp(fn, *qs, km, kp, kc, vm, vp, vc, s0, s1)
        dq0, dq1, dq2, dq3, dkm, dkp, dkc, dvm, dvp, dvc, ds0, ds1 = vjp(do_ref[...])
        for j, dq in enumerate((dq0, dq1, dq2, dq3)):
            dq_ref[:, j * BLOCK:(j + 1) * BLOCK] = _rope_t(dq, cc[...], sc[...], fh)
        dcur_ref[:, :BLOCK] = _rope_t(dkc, cc[...], sc[...], fh)
        dcur_ref[:, BLOCK:] = dvc
        dprev_ref[:, :BLOCK] = _rope_t(dkp, cp[...], sp[...], fh)
        dprev_ref[:, BLOCK:] = dvp
        dkm_r = _rope_t(dkm, cm[...], sm[...], fh)
        rows = []
        for g, dsg in enumerate((ds0, ds1)):
            for j in range(4):
                tot = jnp.sum(dsg[j * BLOCK:(j + 1) * BLOCK], axis=0, keepdims=True)
                rows.append(jnp.broadcast_to(tot, (1, BLOCK)))
        dsink = jnp.concatenate(rows, axis=0)

        @pl.when(i == 0)
        def _():
            dmeta_ref[:, :BLOCK] = dkm_r
            dmeta_ref[:, BLOCK:] = dvm
            dsink_ref[...] = dsink

        @pl.when(i > 0)
        def _():
            dmeta_ref[:, :BLOCK] += dkm_r
            dmeta_ref[:, BLOCK:] += dvm
            dsink_ref[...] += dsink

    kvw = 2 * ATT_KVW
    return pl.pallas_call(
        body, name="att_bwd", grid=(nb,),
        in_specs=_att_specs(nb) + [pl.BlockSpec((BLOCK, ATT_QW), lambda i: (i, 0))],
        out_specs=[pl.BlockSpec((BLOCK, ATT_QW), lambda i: (i, 0)), pl.BlockSpec((BLOCK, kvw), lambda i: (i, 0)),
                   pl.BlockSpec((BLOCK, kvw), lambda i: (i, 0)), pl.BlockSpec((BLOCK, kvw), lambda i: (0, 0)),
                   pl.BlockSpec((ATT_HEADS, BLOCK), lambda i: (0, 0))],
        out_shape=[jax.ShapeDtypeStruct((p, ATT_QW), F32), jax.ShapeDtypeStruct((p, kvw), F32),
                   jax.ShapeDtypeStruct((p, kvw), F32), jax.ShapeDtypeStruct((BLOCK, kvw), F32),
                   jax.ShapeDtypeStruct((ATT_HEADS, BLOCK), F32)],
        compiler_params=_cparams(("arbitrary",)),
    )(proj_att, proj_att, proj_att, cos, sin, cos, sin, cos, sin, sinks, doa)


def _assemble_dproj(d_hg, dq, dkv_cur, dkv_prev, dkv_meta, dgates):
    p = d_hg.shape[0]
    nb = p // BLOCK
    kvw = 2 * ATT_KVW
    wide = 4 * HG_W
    width = wide + ATT_QW + kvw + 2 * D_MODEL

    def body(hg_ref, dq_ref, cur_ref, nxt_ref, meta_ref, g_ref, o_ref):
        i = pl.program_id(0)
        dkv = cur_ref[...] + jnp.where(i < nb - 1, nxt_ref[...], 0.0) + jnp.where(i == 0, meta_ref[...], 0.0)
        o_ref[:, :wide] = hg_ref[...].astype(BF16)
        o_ref[:, wide:wide + ATT_QW] = dq_ref[...].astype(BF16)
        o_ref[:, wide + ATT_QW:wide + ATT_QW + kvw] = dkv.astype(BF16)
        o_ref[:, wide + ATT_QW + kvw:] = g_ref[...].astype(BF16)

    row = lambda w: pl.BlockSpec((BLOCK, w), lambda i: (i, 0))
    return pl.pallas_call(
        body, name="assemble_dproj", grid=(nb,),
        in_specs=[row(wide), row(ATT_QW), row(kvw), pl.BlockSpec((BLOCK, kvw), lambda i: (jnp.minimum(i + 1, nb - 1), 0)),
                  pl.BlockSpec((BLOCK, kvw), lambda i: (0, 0)), row(2 * D_MODEL)],
        out_specs=row(width), out_shape=jax.ShapeDtypeStruct((p, width), BF16),
        compiler_params=_cparams(("arbitrary",)),
    )(d_hg, dq, dkv_cur, dkv_prev, dkv_meta, dgates)


def _local_step(x, target, meta, ln_emb_g, ln_emb_b, w_in, lbounds, norm_g, sinks, w_bh, w_ba, w_out,
                ln1_g, ln1_b, w_fi, w_fo, ln2_g, ln2_b):
    s = x.shape[0]
    p = s + BLOCK
    xin = jnp.concatenate([jnp.zeros((PAD, D_MODEL), F32), meta, x], axis=0)
    tgt = jnp.concatenate([jnp.zeros((BLOCK, D_MODEL), F32), target], axis=0)
    tm = _row_tile(p, 640)
    lv = _level_stack()
    cos, sin = _rope_tables(p)
    hg_end = 4 * HG_W
    att_end = hg_end + ATT_QW + 2 * ATT_KVW
    mm = functools.partial(_tiled_matmul, tm=tm)

    h0, xhat0, rstd0 = _embed_ln(xin, ln_emb_g, ln_emb_b)
    proj_hg = mm(h0, w_in[:, :hg_end], "nn", tn=hg_end, tc=D_MODEL, out_dtype=F32, name="proj_hg")
    proj_att = mm(h0, w_in[:, hg_end:att_end], "nn", tn=att_end - hg_end, tc=D_MODEL, out_dtype=F32, name="proj_att")
    gates = mm(h0, w_in[:, att_end:], "nn", tn=2 * D_MODEL, tc=D_MODEL, out_dtype=F32, name="proj_gates")
    yh, states = _hgrn_fwd(proj_hg, lbounds, norm_g, lv)
    oa = _att_fwd(proj_att, cos, sin, sinks)
    mixin, h1, xhat1, rstd1 = _mix_out_ln1(yh, oa, gates, h0, w_bh, w_ba, w_out, ln1_g, ln1_b)
    au, sw = _ffn_in_swiglu(h1, w_fi)
    dr2, loss_part, dg2, db2 = _ffn_out_loss(sw, w_fo, h1, ln2_g, ln2_b, tgt)

    mtn = functools.partial(_tiled_matmul_tn, tm=tm, out_dtype=BF16)
    d_wfo = mtn(sw, dr2, tk=D_FF, tn=D_MODEL, name="grad_w_ffn_out")
    dau = _d_ffn_hidden(dr2, w_fo, au)
    d_wfi = mtn(h1, dau, tk=D_MODEL, tn=D_FF, name="grad_w_ffn_in")
    dh1_ffn = mm(dau, w_fi, "nt", tn=D_MODEL, tc=D_FF, out_dtype=F32, name="d_h1_ffn")
    dr1, dy_hg, dy_att, dgates, dyh, doa, dg1, db1 = _ln1_mix_bwd(
        dr2, dh1_ffn, xhat1, rstd1, ln1_g, yh, oa, gates, w_bh, w_ba, w_out)
    d_wout = mtn(mixin, dr1, tk=D_MODEL, tn=D_MODEL, name="grad_w_out")
    d_wbh = mtn(yh, dy_hg, tk=HG_W, tn=D_MODEL, name="grad_w_branch_hg")
    d_wba = mtn(oa, dy_att, tk=ATT_QW, tn=D_MODEL, name="grad_w_branch_attn")
    d_hg, d_lb, d_ng = _hgrn_bwd(proj_hg, lbounds, norm_g, lv, states, dyh)
    dq, dkv_cur, dkv_prev, dkv_meta, d_sink = _att_bwd(proj_att, cos, sin, sinks, doa)
    dproj = _assemble_dproj(d_hg, dq, dkv_cur, dkv_prev, dkv_meta, dgates)
    in_w = dproj.shape[1]
    d_win = mtn(h0, dproj, tk=D_MODEL, tn=in_w // 2, name="grad_w_in")
    dh0_proj = mm(dproj, w_in, "nt", tn=D_MODEL, tc=in_w // 2, out_dtype=F32, name="d_h0_proj")
    dxin, dg0, db0 = _ln_bwd_call(dr1, dh0_proj, ALPHA, xhat0, rstd0, ln_emb_g, "embed_ln_bwd", mask_from=PAD)

    grads = dict(meta_tokens=dxin[PAD:BLOCK], ln_emb_g=dg0, ln_emb_b=db0, w_in=d_win, hg_lower_bounds=d_lb,
                 hg_norm_g=d_ng, attn_sinks=d_sink, w_branch_hg=d_wbh, w_branch_attn=d_wba, w_out=d_wout,
                 ln1_g=dg1, ln1_b=db1, w_ffn_in=d_wfi, w_ffn_out=d_wfo, ln2_g=dg2, ln2_b=db2)
    return loss_part, dxin[BLOCK:], grads


def _place():
    return lax.axis_index("x"), lax.axis_index("y"), lax.axis_index("c")


def _all_gather(arrs, dtypes, name):
    n = len(arrs)

    def body(*refs):
        ins, outs, stages = refs[:n], refs[n:2 * n], refs[2 * n:3 * n]
        send_sems, recv_sems, local_sems = refs[3 * n:]
        x, y, c = _place()
        sibling = (x, y, 1 - c)
        chips = [(1 - x, y), (x, 1 - y), (1 - x, 1 - y)]
        slot = lambda px, py, pc: 4 * px + 2 * py + pc

        def copy(w, k, block, to, from_stage=False):
            return pltpu.make_async_remote_copy(
                src_ref=stages[w] if from_stage else outs[w].at[slot(*block)], dst_ref=outs[w].at[slot(*block)],
                send_sem=send_sems.at[w, k], recv_sem=recv_sems.at[w, k], device_id=to, device_id_type=MESH)

        mine, first, passed = [], [], []
        for w in range(n):
            stages[w][...] = ins[w][...].astype(dtypes[w])
            mine.append(pltpu.make_async_copy(stages[w], outs[w].at[slot(x, y, c)], local_sems.at[w]))
            mine[-1].start()
        for w in range(n):
            first.append(copy(w, 0, (x, y, c), sibling, from_stage=True))
            first += [copy(w, 1 + j, (x, y, c), (*chip, c), from_stage=True) for j, chip in enumerate(chips)]
        for cp in first:
            cp.start()
        for j, chip in enumerate(chips):
            for w in range(n):
                copy(w, 1 + j, (*chip, c), (x, y, c)).wait_recv()
                passed.append(copy(w, 4 + j, (*chip, c), sibling))
                passed[-1].start()
        for w in range(n):
            copy(w, 0, sibling, (x, y, c)).wait_recv()
            for j, chip in enumerate(chips):
                copy(w, 4 + j, (*chip, 1 - c), (x, y, c)).wait_recv()
        for cp in first + passed:
            cp.wait_send()
        for cp in mine:
            cp.wait()

    return pl.pallas_call(
        body, name=name,
        in_specs=[pl.BlockSpec(memory_space=pltpu.VMEM)] * n,
        out_specs=[pl.BlockSpec(memory_space=pl.ANY)] * n,
        out_shape=[jax.ShapeDtypeStruct((N_DEV,) + a.shape, dt) for a, dt in zip(arrs, dtypes)],
        scratch_shapes=[pltpu.VMEM(a.shape, dt) for a, dt in zip(arrs, dtypes)]
        + [pltpu.SemaphoreType.DMA((n, 7)), pltpu.SemaphoreType.DMA((n, 7)), pltpu.SemaphoreType.DMA((n,))],
        compiler_params=pltpu.CompilerParams(vmem_limit_bytes=VMEM_LIMIT_BYTES),
    )(*arrs)


def _sibling_exchange(parts, name):
    n = len(parts)

    def body(*refs):
        ins, outs = refs[:n], refs[n:2 * n]
        send_sems, recv_sems = refs[2 * n:]
        x, y, c = _place()
        copies = []
        for w in range(n):
            for q in range(4):
                copies.append(pltpu.make_async_remote_copy(
                    src_ref=ins[w].at[2 * q + (1 - c)], dst_ref=outs[w].at[q], send_sem=send_sems.at[w, q],
                    recv_sem=recv_sems.at[w, q], device_id=(x, y, 1 - c), device_id_type=MESH))
        for cp in copies:
            cp.start()
        for cp in copies:
            cp.wait()

    return pl.pallas_call(
        body, name=name, in_specs=[pl.BlockSpec(memory_space=pl.ANY)] * n,
        out_specs=[pl.BlockSpec(memory_space=pl.ANY)] * n,
        out_shape=[jax.ShapeDtypeStruct((4,) + a.shape[1:], a.dtype) for a in parts],
        scratch_shapes=[pltpu.SemaphoreType.DMA((n, 4)), pltpu.SemaphoreType.DMA((n, 4))],
    )(*parts)


def _chip_exchange(parts, name):
    n = len(parts)

    def body(*refs):
        ins, outs = refs[:n], refs[n:2 * n]
        send_sems, recv_sems = refs[2 * n:]
        x, y, c = _place()
        chips = [(1 - x, y), (x, 1 - y), (1 - x, 1 - y)]
        copies = []
        for w in range(n):
            for j, (px, py) in enumerate(chips):
                copies.append(pltpu.make_async_remote_copy(
                    src_ref=ins[w].at[2 * px + py], dst_ref=outs[w].at[j], send_sem=send_sems.at[w, j],
                    recv_sem=recv_sems.at[w, j], device_id=(px, py, c), device_id_type=MESH))
        for cp in copies:
            cp.start()
        for cp in copies:
            cp.wait()

    return pl.pallas_call(
        body, name=name, in_specs=[pl.BlockSpec(memory_space=pl.ANY)] * n,
        out_specs=[pl.BlockSpec(memory_space=pl.ANY)] * n,
        out_shape=[jax.ShapeDtypeStruct((3,) + a.shape[1:], a.dtype) for a in parts],
        scratch_shapes=[pltpu.SemaphoreType.DMA((n, 3)), pltpu.SemaphoreType.DMA((n, 3))],
    )(*parts)


def _shard_rows(rows):
    return rows if rows <= 512 else 256


def _sibling_sum(part, recv, core, name):
    _, r, cdim = part.shape
    tr = _shard_rows(r)

    def body(core_ref, a_ref, b_ref, o_ref):
        o_ref[...] = (a_ref[...].astype(F32) + b_ref[...].astype(F32)).astype(BF16)

    return pl.pallas_call(
        body, name=name,
        grid_spec=pltpu.PrefetchScalarGridSpec(
            num_scalar_prefetch=1, grid=(4, r // tr),
            in_specs=[pl.BlockSpec((1, tr, cdim), lambda q, i, core_ref: (2 * q + core_ref[0], i, 0)),
                      pl.BlockSpec((1, tr, cdim), lambda q, i, core_ref: (q, i, 0))],
            out_specs=pl.BlockSpec((1, tr, cdim), lambda q, i, core_ref: (q, i, 0))),
        out_shape=jax.ShapeDtypeStruct((4, r, cdim), BF16),
        compiler_params=_cparams(("arbitrary", "arbitrary")),
    )(core, part, recv)


def _adamw_math(w, g, m, v):
    m = ADAM_B1 * m + (1.0 - ADAM_B1) * g
    v = ADAM_B2 * v + (1.0 - ADAM_B2) * (g * g)
    m_hat = m / (1.0 - ADAM_B1 ** ADAM_STEP)
    v_hat = v / (1.0 - ADAM_B2 ** ADAM_STEP)
    delta = -ADAM_LR * (m_hat / (jnp.sqrt(v_hat) + ADAM_EPS) + ADAM_WD * w)
    return delta, m, v


def _reduce_adamw(chip_part, recv, chip_idx, w, m, v, name):
    r, cdim = w.shape
    tr = _shard_rows(r)

    def body(idx_ref, p_ref, r_ref, w_ref, m_ref, v_ref, g_out, d_out, m_out, v_out):
        g = p_ref[0].astype(F32)
        for j in range(3):
            g = g + r_ref[j].astype(F32)
        d, mn, vn = _adamw_math(w_ref[...], g, m_ref[...], v_ref[...])
        g_out[...] = g
        d_out[...] = d
        m_out[...] = mn
        v_out[...] = vn

    flat = pl.BlockSpec((tr, cdim), lambda i, idx_ref: (i, 0))
    return pl.pallas_call(
        body, name=name,
        grid_spec=pltpu.PrefetchScalarGridSpec(
            num_scalar_prefetch=1, grid=(r // tr,),
            in_specs=[pl.BlockSpec((1, tr, cdim), lambda i, idx_ref: (idx_ref[0], i, 0)),
                      pl.BlockSpec((3, tr, cdim), lambda i, idx_ref: (0, i, 0)), flat, flat, flat],
            out_specs=[flat] * 4),
        out_shape=[jax.ShapeDtypeStruct((r, cdim), F32)] * 4,
        compiler_params=_cparams(("arbitrary",)),
    )(chip_idx, chip_part, recv, w, m, v)


SMALL_ROWS = 64
_SMALL_LAYOUT = (("ln_emb_g", 8), ("ln_emb_b", 8), ("hg_lower_bounds", 8), ("hg_norm_g", 1), ("attn_sinks", 1),
                 ("ln1_g", 8), ("ln1_b", 8), ("ln2_g", 8), ("ln2_b", 8))
_LOSS_ROW = sum(r for _, r in _SMALL_LAYOUT)


def _pack_small(vals, loss_row=None):
    rows = []
    for name, nrows in _SMALL_LAYOUT:
        flat = vals[name].reshape(-1).astype(F32)
        flat = jnp.pad(flat, (0, nrows * BLOCK - flat.shape[0]))
        rows.append(flat.reshape(nrows, BLOCK))
    rows.append(jnp.zeros((1, BLOCK), F32) if loss_row is None else loss_row)
    packed = jnp.concatenate(rows, axis=0)
    return jnp.pad(packed, ((0, SMALL_ROWS - packed.shape[0]), (0, 0)))


def _unpack_small(packed, shapes):
    out, row = {}, 0
    for name, nrows in _SMALL_LAYOUT:
        size = math.prod(shapes[name])
        out[name] = packed[row:row + nrows].reshape(-1)[:size].reshape(shapes[name])
        row += nrows
    return out


def _small_reduce_adamw(gathered, w, m, v):
    def body(g_ref, w_ref, m_ref, v_ref, g_out, d_out, m_out, v_out, loss_out):
        g = g_ref[0]
        for s in range(1, N_DEV):
            g = g + g_ref[s]
        d, mn, vn = _adamw_math(w_ref[...], g, m_ref[...], v_ref[...])
        g_out[...] = g
        d_out[...] = d
        m_out[...] = mn
        v_out[...] = vn
        loss_out[...] = jnp.broadcast_to(jnp.sum(g_ref[:, _LOSS_ROW, :]), (1, BLOCK))

    shp = jax.ShapeDtypeStruct((SMALL_ROWS, BLOCK), F32)
    return pl.pallas_call(body, name="small_reduce_adamw",
                          out_shape=[shp] * 4 + [jax.ShapeDtypeStruct((1, BLOCK), F32)])(gathered, w, m, v)


_BIG = ("meta_tokens", "w_in", "w_branch_hg", "w_branch_attn", "w_out", "w_ffn_in", "w_ffn_out")
_COLUMN_SHARDED = ("meta_tokens", "w_in", "w_branch_hg", "w_branch_attn", "w_ffn_in")
_WEIGHTS = ("meta_tokens", "ln_emb_g", "ln_emb_b", "w_in", "hg_lower_bounds", "hg_norm_g", "attn_sinks",
            "w_branch_hg", "w_branch_attn", "w_out", "ln1_g", "ln1_b", "w_ffn_in", "w_ffn_out", "ln2_g", "ln2_b")


def _pair_ffn_cols(w):
    r = w.shape[0]
    return jnp.transpose(w.reshape(r, 2, D_FF // FF_T, FF_T), (0, 2, 1, 3)).reshape(r, 2 * D_FF)


def _unpair_ffn_cols(w):
    r = w.shape[0]
    return jnp.transpose(w.reshape(r, D_FF // FF_T, 2, FF_T), (0, 2, 1, 3)).reshape(r, 2 * D_FF)


def _whole(name, gathered):
    _, r, c = gathered.shape
    if name in _COLUMN_SHARDED:
        whole = jnp.transpose(gathered, (1, 0, 2)).reshape(r, N_DEV * c)
        return _pair_ffn_cols(whole) if name == "w_ffn_in" else whole
    return gathered.reshape(N_DEV * r, c)


def _slots(name, whole):
    r, c = whole.shape
    if name in _COLUMN_SHARDED:
        if name == "w_ffn_in":
            whole = _unpair_ffn_cols(whole)
        return jnp.transpose(whole.reshape(r, N_DEV, c // N_DEV), (1, 0, 2))
    return whole.reshape(N_DEV, r // N_DEV, c)


def kernel(x, meta_tokens, ln_emb_g, ln_emb_b, w_in, hg_lower_bounds, hg_norm_g, attn_sinks, w_branch_hg, w_branch_attn, w_out, ln1_g, ln1_b, w_ffn_in, w_ffn_out, ln2_g, ln2_b, loss_target, m_meta_tokens, m_ln_emb_g, m_ln_emb_b, m_w_in, m_hg_lower_bounds, m_hg_norm_g, m_attn_sinks, m_w_branch_hg, m_w_branch_attn, m_w_out, m_ln1_g, m_ln1_b, m_w_ffn_in, m_w_ffn_out, m_ln2_g, m_ln2_b, v_meta_tokens, v_ln_emb_g, v_ln_emb_b, v_w_in, v_hg_lower_bounds, v_hg_norm_g, v_attn_sinks, v_w_branch_hg, v_w_branch_attn, v_w_out, v_ln1_g, v_ln1_b, v_w_ffn_in, v_w_ffn_out, v_ln2_g, v_ln2_b):
    given = dict(locals())
    weights = {n: given[n] for n in _WEIGHTS}
    mom1 = {n: given["m_" + n] for n in _WEIGHTS}
    mom2 = {n: given["v_" + n] for n in _WEIGHTS}
    shard2d = lambda a: a.reshape(a.shape[-2:])

    shards = [shard2d(weights[n]) for n in _BIG]
    gathered = _all_gather(shards, [F32] + [BF16] * (len(_BIG) - 1), "gather_weights")
    full = {n: _whole(n, g) for n, g in zip(_BIG, gathered)}

    loss_part, grad_x, grads = _local_step(
        x[0], loss_target[0], full["meta_tokens"], ln_emb_g.reshape(1, -1), ln_emb_b.reshape(1, -1), full["w_in"],
        hg_lower_bounds, hg_norm_g, attn_sinks, full["w_branch_hg"], full["w_branch_attn"], full["w_out"],
        ln1_g, ln1_b, full["w_ffn_in"], full["w_ffn_out"], ln2_g, ln2_b)

    xi, yi, ci = _place()
    core = ci.astype(jnp.int32).reshape(1)
    chip_idx = (2 * xi + yi).astype(jnp.int32).reshape(1)
    parts = [_slots(n, grads[n].astype(BF16)) for n in _BIG]
    from_sibling = _sibling_exchange(parts, "grads_to_sibling")
    chip_parts = [_sibling_sum(p, r, core, "sibling_sum_" + n) for n, p, r in zip(_BIG, parts, from_sibling)]
    from_chips = _chip_exchange(chip_parts, "grads_to_chips")
    out = {}
    for n, cp, rc in zip(_BIG, chip_parts, from_chips):
        res = _reduce_adamw(cp, rc, chip_idx, shard2d(weights[n]), shard2d(mom1[n]), shard2d(mom2[n]), "adamw_" + n)
        out[n] = [r.reshape(weights[n].shape) for r in res]

    small_names = [n for n, _ in _SMALL_LAYOUT]
    small_grads = dict(grads)
    small_grads["attn_sinks"] = grads["attn_sinks"][:, 0]
    packed = _pack_small(small_grads, loss_part)
    all_small, = _all_gather([packed], [F32], "gather_small")
    res = _small_reduce_adamw(all_small, _pack_small(weights), _pack_small(mom1), _pack_small(mom2))
    shapes = {n: weights[n].shape for n in small_names}
    unpacked = [_unpack_small(r, shapes) for r in res[:4]]
    for n in small_names:
        out[n] = [u[n] for u in unpacked]
    loss = res[4][0, 0]

    return (loss, grad_x[None], *[out[n][0] for n in _WEIGHTS], *[out[n][1] for n in _WEIGHTS],
            *[out[n][2] for n in _WEIGHTS], *[out[n][3] for n in _WEIGHTS])
```

```python
import functools
import math

import numpy as np
import jax
import jax.numpy as jnp
from jax import lax
from jax.experimental import pallas as pl
from jax.experimental.pallas import tpu as pltpu

F32 = jnp.float32
BF16 = jnp.bfloat16

D_MODEL = 1024
N_META = 16
BLOCK = 128
PAD = BLOCK - N_META
HG_HEADS = 4
HG_W = 512
ATT_HEADS = 8
HEAD_DIM = 64
ATT_QW = 512
ATT_KVW = 128
D_FF = 2816
EPS = 1e-5
ALPHA = 2.0 ** 0.25
ROPE_THETA = 10000.0
N_DEV = 8

ADAM_LR = 0.001
ADAM_B1 = 0.9
ADAM_B2 = 0.999
ADAM_EPS = 1e-08
ADAM_WD = 0.01
ADAM_STEP = 10

VMEM_LIMIT_BYTES = 56 * 1024 * 1024
MESH = pl.DeviceIdType.MESH

_LEVELS = (64, 32, 16, 8, 4, 2, 1)


def _cparams(sem):
    return pltpu.CompilerParams(dimension_semantics=sem, vmem_limit_bytes=VMEM_LIMIT_BYTES)


def _row_tile(rows, target):
    nb = rows // BLOCK
    best = 1
    for d in range(1, nb + 1):
        if nb % d == 0 and d * BLOCK <= target:
            best = d
    return best * BLOCK


_DN = {"nn": (((1,), (0,)), ((), ())), "nt": (((1,), (1,)), ((), ())), "tn": (((0,), (0,)), ((), ()))}


def _dot(a, b, form):
    return lax.dot_general(a.astype(BF16), b.astype(BF16), _DN[form], preferred_element_type=F32)


@functools.partial(jax.custom_vjp, nondiff_argnums=(2,))
def _mm(a, b, form):
    return _dot(a, b, form)


def _mm_fwd(a, b, form):
    return _dot(a, b, form), (a, b)


def _mm_bwd(form, res, g):
    a, b = res
    if form == "nn":
        return _dot(g, b, "nt"), _dot(a, g, "tn")
    if form == "nt":
        return _dot(g, b, "nn"), _dot(g, a, "tn")
    return _dot(b, g, "nt"), _dot(a, g, "nn")


_mm.defvjp(_mm_fwd, _mm_bwd)


def _split_dot(lv, x, form):
    hi = x.astype(BF16)
    lo = (x - hi.astype(F32)).astype(BF16)
    return (lax.dot_general(lv, hi, _DN[form], preferred_element_type=F32)
            + lax.dot_general(lv, lo, _DN[form], preferred_element_type=F32))


@jax.custom_vjp
def _swap_halves(x):
    return pltpu.roll(x, 64, 1)


_swap_halves.defvjp(lambda x: (pltpu.roll(x, 64, 1), None), lambda _, g: (pltpu.roll(g, 64, 1),))


def _tiled_matmul(a, b, form, *, tm, tn, tc, out_dtype, name):
    m, c = a.shape
    n = b.shape[1] if form == "nn" else b.shape[0]
    assert m % tm == 0 and n % tn == 0 and c % tc == 0, (name, a.shape, b.shape, tm, tn, tc)
    nc = c // tc

    def body(a_ref, b_ref, o_ref, *scratch):
        part = _dot(a_ref[...], b_ref[...], form)
        if nc == 1:
            o_ref[...] = part.astype(out_dtype)
            return
        acc_ref, = scratch
        ci = pl.program_id(2)

        @pl.when(ci == 0)
        def _():
            acc_ref[...] = part

        @pl.when(ci > 0)
        def _():
            acc_ref[...] += part

        @pl.when(ci == nc - 1)
        def _():
            o_ref[...] = acc_ref[...].astype(out_dtype)

    b_spec = (pl.BlockSpec((tc, tn), lambda j, i, k: (k, j)) if form == "nn"
              else pl.BlockSpec((tn, tc), lambda j, i, k: (j, k)))
    return pl.pallas_call(
        body, name=name, grid=(n // tn, m // tm, nc),
        in_specs=[pl.BlockSpec((tm, tc), lambda j, i, k: (i, k)), b_spec],
        out_specs=pl.BlockSpec((tm, tn), lambda j, i, k: (i, j)),
        out_shape=jax.ShapeDtypeStruct((m, n), out_dtype),
        scratch_shapes=[] if nc == 1 else [pltpu.VMEM((tm, tn), F32)],
        compiler_params=_cparams(("arbitrary", "arbitrary", "arbitrary")),
    )(a, b)


def _tiled_matmul_tn(a, b, *, tm, tk, tn, out_dtype, name):
    m, k = a.shape
    n = b.shape[1]
    assert m % tm == 0 and k % tk == 0 and n % tn == 0, (name, a.shape, b.shape, tm, tk, tn)
    nm = m // tm

    def body(a_ref, b_ref, o_ref, acc_ref):
        part = _dot(a_ref[...], b_ref[...], "tn")
        mi = pl.program_id(2)

        @pl.when(mi == 0)
        def _():
            acc_ref[...] = part

        @pl.when(mi > 0)
        def _():
            acc_ref[...] += part

        @pl.when(mi == nm - 1)
        def _():
            o_ref[...] = acc_ref[...].astype(out_dtype)

    return pl.pallas_call(
        body, name=name, grid=(k // tk, n // tn, nm),
        in_specs=[pl.BlockSpec((tm, tk), lambda kk, j, i: (i, kk)), pl.BlockSpec((tm, tn), lambda kk, j, i: (i, j))],
        out_specs=pl.BlockSpec((tk, tn), lambda kk, j, i: (kk, j)),
        out_shape=jax.ShapeDtypeStruct((k, n), out_dtype),
        scratch_shapes=[pltpu.VMEM((tk, tn), F32)],
        compiler_params=_cparams(("arbitrary", "arbitrary", "arbitrary")),
    )(a, b)


def _ln_stats(r):
    mu = jnp.mean(r, axis=-1, keepdims=True)
    xc = r - mu
    var = jnp.mean(xc * xc, axis=-1, keepdims=True)
    rstd = lax.rsqrt(var + EPS)
    return xc * rstd, rstd


def _ln_bwd(dy, xhat, rstd, g):
    dxhat = dy * g
    m1 = jnp.mean(dxhat, axis=-1, keepdims=True)
    m2 = jnp.mean(dxhat * xhat, axis=-1, keepdims=True)
    dr = rstd * (dxhat - m1 - xhat * m2)
    return dr, jnp.sum(dy * xhat, axis=0, keepdims=True), jnp.sum(dy, axis=0, keepdims=True)


N_SEG = 3 + len(_LEVELS)


def _level_stack():
    t = np.arange(BLOCK)[:, None]
    r = np.arange(BLOCK)[None, :]
    mats = [r <= t, r > t, np.ones((BLOCK, BLOCK), bool)]
    for h in _LEVELS:
        same = (t // (2 * h)) == (r // (2 * h))
        up_t, up_r = (t % (2 * h)) >= h, (r % (2 * h)) >= h
        mats.append(same & ((up_t & up_r & (r <= t)) | (~up_t & ~up_r & (r > t))))
    return jnp.asarray(np.concatenate(mats, axis=0).astype(np.float32), dtype=BF16)


def _hgrn_gates(hf, a0, a1, valid):
    lb = jax.nn.sigmoid(a0 - a1)
    fg = lb + (1.0 - lb) * jax.nn.sigmoid(hf)
    return jnp.where(valid, jnp.log(fg), 0.0), jnp.where(valid, 1.0 - fg, 0.0)


def _hgrn_head(hq, k, v, hg, ng, st_in, *seg):
    q = jax.nn.silu(hq)
    rows = lax.broadcasted_iota(jnp.int32, (BLOCK, BLOCK), 0)
    cols = lax.broadcasted_iota(jnp.int32, (BLOCK, BLOCK), 1)
    o = _mm(q * jnp.exp(seg[0]), st_in, "nt")
    a = jnp.where(rows == cols, jnp.sum(q * k, axis=-1, keepdims=True), 0.0)
    for li, h in enumerate(_LEVELS):
        decay = jnp.exp(seg[3 + li])
        pair = ((rows // (2 * h)) == (cols // (2 * h))) & ((rows % (2 * h)) >= h) & ((cols % (2 * h)) < h)
        a = a + jnp.where(pair, _mm(q * decay, k * decay, "nt"), 0.0)
    o = o + _mm(a, v, "nn")
    st_out = st_in * jnp.exp(seg[2]) + _mm(v, k * jnp.exp(seg[1]), "tn")
    on = o * lax.rsqrt(jnp.mean(o * o, axis=-1, keepdims=True) + EPS) * ng
    return on * jax.nn.silu(hg), st_out


def _seg_blocks(e, h):
    return [e[i * BLOCK:(i + 1) * BLOCK, h * BLOCK:(h + 1) * BLOCK] for i in range(N_SEG)]


def _rope(x, cos, sin, first_half):
    partner = jnp.where(first_half, -pltpu.roll(x, 96, 1), pltpu.roll(x, 32, 1))
    return x * cos + partner * sin


def _rope_t(g, cos, sin, first_half):
    u = g * sin
    partner = jnp.where(first_half, pltpu.roll(u, 96, 1), -pltpu.roll(u, 32, 1))
    return g * cos + partner


def _att_core(q0, q1, q2, q3, km, kp, kc, vm, vp, vc, sinkcol0, sinkcol1, ok_m, ok_p, ok_c):
    lane = lax.broadcasted_iota(jnp.int32, (BLOCK, BLOCK), 1)
    low = lane < HEAD_DIM
    scale = HEAD_DIM ** -0.5
    neg = jnp.finfo(F32).min
    outs = []
    for g, (qa, qb, sinkcol) in enumerate(((q0, q1, sinkcol0), (q2, q3, sinkcol1))):
        def both(x, g=g):
            sw = _swap_halves(x)
            return jnp.where(low, x, sw) if g == 0 else jnp.where(low, sw, x)
        q4 = jnp.concatenate([jnp.where(low, qa, 0.0), jnp.where(low, 0.0, qa),
                              jnp.where(low, qb, 0.0), jnp.where(low, 0.0, qb)], axis=0)
        s = []
        for kk, ok in ((km, ok_m), (kp, ok_p), (kc, ok_c)):
            ok4 = jnp.concatenate([ok] * 4, axis=0)
            s.append(jnp.where(ok4, _mm(q4, both(kk), "nt") * scale, neg))
        mx = jnp.maximum(jnp.maximum(jnp.max(s[0], axis=-1, keepdims=True), jnp.max(s[1], axis=-1, keepdims=True)),
                         jnp.maximum(jnp.max(s[2], axis=-1, keepdims=True), sinkcol))
        mx = lax.stop_gradient(mx)
        p = [jnp.exp(si - mx) for si in s]
        den = (jnp.sum(p[0], axis=-1, keepdims=True) + jnp.sum(p[1], axis=-1, keepdims=True)
               + jnp.sum(p[2], axis=-1, keepdims=True) + jnp.exp(sinkcol - mx))
        inv = 1.0 / den
        o4 = (_mm(p[0] * inv, both(vm), "nn") + _mm(p[1] * inv, both(vp), "nn") + _mm(p[2] * inv, both(vc), "nn"))
        for j in range(2):
            outs.append(jnp.where(low, o4[(2 * j) * BLOCK:(2 * j + 1) * BLOCK],
                                  o4[(2 * j + 1) * BLOCK:(2 * j + 2) * BLOCK]))
    return jnp.concatenate(outs, axis=1)


def _att_masks(blk_idx):
    qpos = blk_idx * BLOCK + lax.broadcasted_iota(jnp.int32, (BLOCK, BLOCK), 0) - PAD
    kidx = lax.broadcasted_iota(jnp.int32, (BLOCK, BLOCK), 1)
    pos_m = kidx - PAD
    pos_p = (blk_idx - 1) * BLOCK + kidx - PAD
    pos_c = blk_idx * BLOCK + kidx - PAD
    ok_m = (pos_m >= 0) & (pos_m <= qpos)
    ok_p = (pos_p >= N_META) & (qpos - pos_p < BLOCK) & (blk_idx >= 1)
    ok_c = (pos_c >= N_META) & (pos_c <= qpos)
    return ok_m, ok_p, ok_c


def _embed_ln(xin, g0, b0):
    p = xin.shape[0]
    tr = _row_tile(p, 640)

    def body(x_ref, g_ref, b_ref, h_ref, xh_ref, rs_ref):
        xhat, rstd = _ln_stats(x_ref[...])
        row = pl.program_id(0) * tr + lax.broadcasted_iota(jnp.int32, (tr, 1), 0)
        h_ref[...] = jnp.where(row >= PAD, xhat * g_ref[...] + b_ref[...], 0.0)
        xh_ref[...] = xhat
        rs_ref[...] = rstd

    vec = pl.BlockSpec((1, D_MODEL), lambda i: (0, 0))
    rowsp = pl.BlockSpec((tr, D_MODEL), lambda i: (i, 0))
    return pl.pallas_call(
        body, name="embed_ln", grid=(p // tr,), in_specs=[rowsp, vec, vec],
        out_specs=[rowsp, rowsp, pl.BlockSpec((tr, 1), lambda i: (i, 0))],
        out_shape=[jax.ShapeDtypeStruct((p, D_MODEL), F32)] * 2 + [jax.ShapeDtypeStruct((p, 1), F32)],
        compiler_params=_cparams(("arbitrary",)),
    )(xin, g0, b0)


def _rope_tables(p):
    pos = (np.arange(p, dtype=np.int32) - PAD).astype(np.float32)
    half = HEAD_DIM // 2
    inv = np.float32(ROPE_THETA) ** (-np.arange(half, dtype=np.float32) / np.float32(half))
    ang = pos[:, None] * np.tile(inv.astype(np.float32), BLOCK // half)[None, :]
    return jnp.asarray(np.cos(ang), F32), jnp.asarray(np.sin(ang), F32)


def _att_sinkcols(sink_ref):
    rowhead = lax.broadcasted_iota(jnp.int32, (4 * BLOCK, 1), 0) // BLOCK
    cols = []
    for g in range(2):
        col = jnp.zeros((4 * BLOCK, 1), F32)
        for j in range(4):
            col = jnp.where(rowhead == j, sink_ref[0, 4 * g + j], col)
        cols.append(col)
    return cols


def _att_load(qkv_ref, cos_ref, sin_ref, first_half, with_q):
    cos, sin = cos_ref[...], sin_ref[...]
    qs = [_rope(qkv_ref[:, j * BLOCK:(j + 1) * BLOCK], cos, sin, first_half) for j in range(4)] if with_q else None
    k = _rope(qkv_ref[:, ATT_QW:ATT_QW + ATT_KVW], cos, sin, first_half)
    v = qkv_ref[:, ATT_QW + ATT_KVW:ATT_QW + 2 * ATT_KVW]
    return qs, k, v


def _att_specs(blk):
    w = ATT_QW + 2 * ATT_KVW
    cur = lambda width: pl.BlockSpec((BLOCK, width), lambda i: (blk(i), 0))
    prev = lambda width: pl.BlockSpec((BLOCK, width), lambda i: (jnp.maximum(blk(i) - 1, 0), 0))
    meta = lambda width: pl.BlockSpec((BLOCK, width), lambda i: (0, 0))
    return [cur(w), prev(w), meta(w), cur(BLOCK), cur(BLOCK), prev(BLOCK), prev(BLOCK), meta(BLOCK), meta(BLOCK),
            pl.BlockSpec(memory_space=pltpu.SMEM)]


_FLIPS = [(dx, dy, dc) for dx in (0, 1) for dy in (0, 1) for dc in (0, 1)][1:]
N_PEERS = len(_FLIPS)


def _place():
    return lax.axis_index("x"), lax.axis_index("y"), lax.axis_index("c")


def _peer(place, flip):
    return tuple(1 - p if f else p for p, f in zip(place, flip))


def _slot(place, swapped):
    x, y, c = place
    return 4 * y + 2 * x + c if swapped else 4 * x + 2 * y + c


def _comm_specs(arrs, out_lead):
    n = len(arrs)
    outs = [jax.ShapeDtypeStruct((out_lead,) + a.shape[-2:], a.dtype) for a in arrs]
    sems = [pltpu.SemaphoreType.DMA((n, N_PEERS)), pltpu.SemaphoreType.DMA((n, N_PEERS)), pltpu.SemaphoreType.DMA((n,))]
    return [pl.BlockSpec(memory_space=pl.ANY)] * n, [pl.BlockSpec(memory_space=pl.ANY)] * n, outs, sems


def _gather_behind(shard_refs, out_refs, sems, swapped):
    send_sems, recv_sems, local_sems = sems
    place = _place()
    starts, waits = [], []
    for w, (s, o) in enumerate(zip(shard_refs, out_refs)):
        mine = _slot(place, swapped[w])
        own = pltpu.make_async_copy(s, o.at[mine], local_sems.at[w])
        starts.append(own.start)
        waits.append(own.wait)
        for r, flip in enumerate(_FLIPS):
            peer = _peer(place, flip)
            kw = dict(send_sem=send_sems.at[w, r], recv_sem=recv_sems.at[w, r], device_id=peer, device_id_type=MESH)
            out_cp = pltpu.make_async_remote_copy(src_ref=s, dst_ref=o.at[mine], **kw)
            in_cp = pltpu.make_async_remote_copy(src_ref=s, dst_ref=o.at[_slot(peer, swapped[w])], **kw)
            starts.append(out_cp.start)
            waits += [in_cp.wait_recv, out_cp.wait_send]
    return starts, waits


def _scatter_behind(part_refs, recv_refs, sems, swapped):
    send_sems, recv_sems, _ = sems
    place = _place()
    starts, waits = [], []
    for w, (p, o) in enumerate(zip(part_refs, recv_refs)):
        for r, flip in enumerate(_FLIPS):
            peer = _peer(place, flip)
            cp = pltpu.make_async_remote_copy(
                src_ref=p.at[_slot(peer, swapped[w])], dst_ref=o.at[r], send_sem=send_sems.at[w, r],
                recv_sem=recv_sems.at[w, r], device_id=peer, device_id_type=MESH)
            starts.append(cp.start)
            waits += [cp.wait_recv, cp.wait_send]
    return starts, waits


def _mixers_fwd(proj_hg, proj_att, lbounds, norm_g, lv, cos, sin, sinks, shards, swapped):
    p = proj_hg.shape[0]
    nb = p // BLOCK
    n = len(shards)
    c_in, c_out, c_shapes, c_sems = _comm_specs(shards, N_DEV)

    def body(*refs):
        x_ref, lb_ref, ng_ref, lv_ref, cur_ref, prev_ref, meta_ref, cc, sc, cp, sp, cm, sm, sink_ref = refs[:14]
        shard_refs = refs[14:14 + n]
        y_ref, st_ref, o_ref = refs[14 + n:17 + n]
        out_refs = refs[17 + n:17 + 2 * n]
        carry_ref = refs[17 + 2 * n]
        starts, waits = _gather_behind(shard_refs, out_refs, refs[18 + 2 * n:], swapped)
        c = pl.program_id(0)

        @pl.when(c == 0)
        def _():
            carry_ref[...] = jnp.zeros_like(carry_ref)
            for start in starts:
                start()

        valid = (c * BLOCK + lax.broadcasted_iota(jnp.int32, (BLOCK, 1), 0)) >= PAD
        logf, k = _hgrn_gates(x_ref[:, HG_W:2 * HG_W], lb_ref[0:1, :], lb_ref[1:2, :], valid)
        e = _split_dot(lv_ref[...], logf, "nn")
        for h in range(HG_HEADS):
            sl = lambda part: x_ref[:, part * HG_W + h * BLOCK: part * HG_W + (h + 1) * BLOCK]
            hs = slice(h * BLOCK, (h + 1) * BLOCK)
            st_in = carry_ref[h]
            st_ref[0, h] = st_in
            y, st_out = _hgrn_head(sl(0), k[:, hs], sl(2), sl(3), ng_ref[...], st_in, *_seg_blocks(e, h))
            y_ref[:, hs] = y
            carry_ref[h] = st_out

        fh = (lax.broadcasted_iota(jnp.int32, (BLOCK, BLOCK), 1) % HEAD_DIM) < (HEAD_DIM // 2)
        qs, kc, vc = _att_load(cur_ref, cc, sc, fh, True)
        _, kp, vp = _att_load(prev_ref, cp, sp, fh, False)
        _, km, vm = _att_load(meta_ref, cm, sm, fh, False)
        s0, s1 = _att_sinkcols(sink_ref)
        o_ref[...] = _att_core(*qs, km, kp, kc, vm, vp, vc, s0, s1, *_att_masks(c))

        @pl.when(c == nb - 1)
        def _():
            for wait in waits:
                wait()

    return pl.pallas_call(
        body, name="mixers_fwd", grid=(nb,),
        in_specs=[pl.BlockSpec((BLOCK, 4 * HG_W), lambda c: (c, 0)), pl.BlockSpec((2, HG_W), lambda c: (0, 0)),
                  pl.BlockSpec((1, BLOCK), lambda c: (0, 0)), pl.BlockSpec(lv.shape, lambda c: (0, 0))]
        + _att_specs(lambda c: c) + c_in,
        out_specs=[pl.BlockSpec((BLOCK, HG_W), lambda c: (c, 0)),
                   pl.BlockSpec((1, HG_HEADS, BLOCK, BLOCK), lambda c: (c, 0, 0, 0)),
                   pl.BlockSpec((BLOCK, ATT_QW), lambda c: (c, 0))] + c_out,
        out_shape=[jax.ShapeDtypeStruct((p, HG_W), F32), jax.ShapeDtypeStruct((nb, HG_HEADS, BLOCK, BLOCK), F32),
                   jax.ShapeDtypeStruct((p, ATT_QW), F32)] + c_shapes,
        scratch_shapes=[pltpu.VMEM((HG_HEADS, BLOCK, BLOCK), F32)] + c_sems,
        compiler_params=_cparams(("arbitrary",)),
    )(proj_hg, lbounds, norm_g, lv, proj_att, proj_att, proj_att, cos, sin, cos, sin, cos, sin, sinks, *shards)


def _tile(rows, preferred):
    return preferred if rows % preferred == 0 else _row_tile(rows, preferred)


def _branch_mix(yh, oa, gates, w_bh, w_ba):
    y_hg = _dot(yh, w_bh, "nn")
    y_att = _dot(oa, w_ba, "nn")
    s1 = jax.nn.sigmoid(gates[:, :D_MODEL])
    s2 = jax.nn.sigmoid(gates[:, D_MODEL:])
    return s1 * y_hg + s2 * y_att, y_hg, y_att, s1, s2


def _mix_out_ln1(yh, oa, gates, h0, w_bh, w_ba, w_out, g1, b1):
    p = yh.shape[0]
    tr = _tile(p, 320)

    def body(yh_ref, oa_ref, g_ref, h0_ref, wbh_ref, wba_ref, wo_ref, g1_ref, b1_ref, mix_ref, h1_ref, xh_ref, rs_ref):
        mixin = _branch_mix(yh_ref[...], oa_ref[...], g_ref[...], wbh_ref[...], wba_ref[...])[0]
        mix_ref[...] = mixin.astype(BF16)
        xhat, rstd = _ln_stats(ALPHA * h0_ref[...] + _dot(mixin, wo_ref[...], "nn"))
        h1_ref[...] = xhat * g1_ref[...] + b1_ref[...]
        xh_ref[...] = xhat
        rs_ref[...] = rstd

    row = lambda w: pl.BlockSpec((tr, w), lambda i: (i, 0))
    const = lambda a: pl.BlockSpec(a.shape, lambda i: (0, 0))
    return pl.pallas_call(
        body, name="mix_out_ln1", grid=(p // tr,),
        in_specs=[row(HG_W), row(ATT_QW), row(2 * D_MODEL), row(D_MODEL), const(w_bh), const(w_ba), const(w_out),
                  const(g1), const(b1)],
        out_specs=[row(D_MODEL), row(D_MODEL), row(D_MODEL), row(1)],
        out_shape=[jax.ShapeDtypeStruct((p, D_MODEL), BF16), jax.ShapeDtypeStruct((p, D_MODEL), F32),
                   jax.ShapeDtypeStruct((p, D_MODEL), F32), jax.ShapeDtypeStruct((p, 1), F32)],
        compiler_params=_cparams(("arbitrary",)),
    )(yh, oa, gates, h0, w_bh, w_ba, w_out, g1, b1)


FF_T = D_FF // 2


def _ffn_in_swiglu(h1, w_fi):
    p = h1.shape[0]
    tm = _row_tile(p, 640)

    def body(h_ref, w_ref, au_ref, s_ref):
        au = _dot(h_ref[...], w_ref[...], "nn")
        au_ref[...] = au
        s_ref[...] = (jax.nn.silu(au[:, :FF_T]) * au[:, FF_T:]).astype(BF16)

    return pl.pallas_call(
        body, name="ffn_in_swiglu", grid=(D_FF // FF_T, p // tm),
        in_specs=[pl.BlockSpec((tm, D_MODEL), lambda j, i: (i, 0)), pl.BlockSpec((D_MODEL, 2 * FF_T), lambda j, i: (0, j))],
        out_specs=[pl.BlockSpec((tm, 2 * FF_T), lambda j, i: (i, j)), pl.BlockSpec((tm, FF_T), lambda j, i: (i, j))],
        out_shape=[jax.ShapeDtypeStruct((p, 2 * D_FF), F32), jax.ShapeDtypeStruct((p, D_FF), BF16)],
        compiler_params=_cparams(("arbitrary", "arbitrary")),
    )(h1, w_fi)


def _ffn_out_loss(s, w_fo, h1, g2, b2, target):
    p = h1.shape[0]
    tr = _row_tile(p, 640)

    def body(s_ref, w_ref, h_ref, g_ref, b_ref, t_ref, dr_ref, loss_ref, dg_ref, db_ref):
        i = pl.program_id(0)
        xhat, rstd = _ln_stats(ALPHA * h_ref[...] + _dot(s_ref[...], w_ref[...], "nn"))
        y = xhat * g_ref[...] + b_ref[...]
        row = i * tr + lax.broadcasted_iota(jnp.int32, (tr, 1), 0)
        err = jnp.where(row >= BLOCK, y - t_ref[...], 0.0)
        dr, dg, db = _ln_bwd(err * (1.0 / D_MODEL), xhat, rstd, g_ref[...])
        dr_ref[...] = dr
        e2 = jnp.sum(err * err, axis=0, keepdims=True)
        part = e2[:, 0:BLOCK]
        for j in range(1, D_MODEL // BLOCK):
            part = part + e2[:, j * BLOCK:(j + 1) * BLOCK]
        part = part * (0.5 / D_MODEL)

        @pl.when(i == 0)
        def _():
            loss_ref[...] = part
            dg_ref[...] = dg
            db_ref[...] = db

        @pl.when(i > 0)
        def _():
            loss_ref[...] += part
            dg_ref[...] += dg
            db_ref[...] += db

    vec = pl.BlockSpec((1, D_MODEL), lambda i: (0, 0))
    rowsp = pl.BlockSpec((tr, D_MODEL), lambda i: (i, 0))
    return pl.pallas_call(
        body, name="ffn_out_loss", grid=(p // tr,),
        in_specs=[pl.BlockSpec((tr, D_FF), lambda i: (i, 0)), pl.BlockSpec((D_FF, D_MODEL), lambda i: (0, 0)),
                  rowsp, vec, vec, rowsp],
        out_specs=[rowsp, pl.BlockSpec((1, BLOCK), lambda i: (0, 0)), vec, vec],
        out_shape=[jax.ShapeDtypeStruct((p, D_MODEL), F32), jax.ShapeDtypeStruct((1, BLOCK), F32),
                   jax.ShapeDtypeStruct((1, D_MODEL), F32), jax.ShapeDtypeStruct((1, D_MODEL), F32)],
        compiler_params=_cparams(("arbitrary",)),
    )(s, w_fo, h1, g2, b2, target)


def _d_ffn_hidden(dr2, w_fo, au):
    p = au.shape[0]
    tm = _row_tile(p, 640)

    def body(d_ref, w_ref, au_ref, o_ref):
        ds = _dot(d_ref[...], w_ref[...], "nt")
        _, vjp = jax.vjp(lambda a, u: jax.nn.silu(a) * u, au_ref[:, :FF_T], au_ref[:, FF_T:])
        da, du = vjp(ds)
        o_ref[:, :FF_T] = da.astype(BF16)
        o_ref[:, FF_T:] = du.astype(BF16)

    return pl.pallas_call(
        body, name="d_ffn_hidden", grid=(D_FF // FF_T, p // tm),
        in_specs=[pl.BlockSpec((tm, D_MODEL), lambda j, i: (i, 0)), pl.BlockSpec((FF_T, D_MODEL), lambda j, i: (j, 0)),
                  pl.BlockSpec((tm, 2 * FF_T), lambda j, i: (i, j))],
        out_specs=pl.BlockSpec((tm, 2 * FF_T), lambda j, i: (i, j)),
        out_shape=jax.ShapeDtypeStruct((p, 2 * D_FF), BF16), compiler_params=_cparams(("arbitrary", "arbitrary")),
    )(dr2, w_fo, au)


def _ln_bwd_call(d_a, d_b, scale_a, xhat, rstd, g, name, mask_from=None):
    p = xhat.shape[0]
    tr = _row_tile(p, 640)

    def body(a_ref, b_ref, xh_ref, rs_ref, g_ref, dr_ref, dg_ref, db_ref):
        i = pl.program_id(0)
        dy = scale_a * a_ref[...] + b_ref[...]
        if mask_from is not None:
            row = i * tr + lax.broadcasted_iota(jnp.int32, (tr, 1), 0)
            dy = jnp.where(row >= mask_from, dy, 0.0)
        dr, dg, db = _ln_bwd(dy, xh_ref[...], rs_ref[...], g_ref[...])
        dr_ref[...] = dr

        @pl.when(i == 0)
        def _():
            dg_ref[...] = dg
            db_ref[...] = db

        @pl.when(i > 0)
        def _():
            dg_ref[...] += dg
            db_ref[...] += db

    vec = pl.BlockSpec((1, D_MODEL), lambda i: (0, 0))
    rowsp = pl.BlockSpec((tr, D_MODEL), lambda i: (i, 0))
    return pl.pallas_call(
        body, name=name, grid=(p // tr,),
        in_specs=[rowsp, rowsp, rowsp, pl.BlockSpec((tr, 1), lambda i: (i, 0)), vec],
        out_specs=[rowsp, vec, vec],
        out_shape=[jax.ShapeDtypeStruct((p, D_MODEL), F32)] + [jax.ShapeDtypeStruct((1, D_MODEL), F32)] * 2,
        compiler_params=_cparams(("arbitrary",)),
    )(d_a, d_b, xhat, rstd, g)


def _ln1_mix_bwd(dr2, dh1_ffn, xhat1, rstd1, g1, yh, oa, gates, w_bh, w_ba, w_out):
    p = yh.shape[0]
    tr = _tile(p, 320)

    def body(a_ref, b_ref, xh_ref, rs_ref, g1_ref, yh_ref, oa_ref, g_ref, wbh_ref, wba_ref, wo_ref,
             dr_ref, dyhg_ref, dyat_ref, dgt_ref, dyh_ref, doa_ref, dg_ref, db_ref):
        i = pl.program_id(0)
        dr, dg, db = _ln_bwd(ALPHA * a_ref[...] + b_ref[...], xh_ref[...], rs_ref[...], g1_ref[...])
        dr_ref[...] = dr
        d = _dot(dr, wo_ref[...], "nt")
        _, y_hg, y_att, s1, s2 = _branch_mix(yh_ref[...], oa_ref[...], g_ref[...], wbh_ref[...], wba_ref[...])
        dy_hg = d * s1
        dy_att = d * s2
        dyhg_ref[...] = dy_hg.astype(BF16)
        dyat_ref[...] = dy_att.astype(BF16)
        dgt_ref[:, :D_MODEL] = (d * y_hg * s1 * (1.0 - s1)).astype(BF16)
        dgt_ref[:, D_MODEL:] = (d * y_att * s2 * (1.0 - s2)).astype(BF16)
        dyh_ref[...] = _dot(dy_hg, wbh_ref[...], "nt")
        doa_ref[...] = _dot(dy_att, wba_ref[...], "nt")

        @pl.when(i == 0)
        def _():
            dg_ref[...] = dg
            db_ref[...] = db

        @pl.when(i > 0)
        def _():
            dg_ref[...] += dg
            db_ref[...] += db

    row = lambda w: pl.BlockSpec((tr, w), lambda i: (i, 0))
    const = lambda a: pl.BlockSpec(a.shape, lambda i: (0, 0))
    vec = pl.BlockSpec((1, D_MODEL), lambda i: (0, 0))
    return pl.pallas_call(
        body, name="ln1_mix_bwd", grid=(p // tr,),
        in_specs=[row(D_MODEL), row(D_MODEL), row(D_MODEL), row(1), vec, row(HG_W), row(ATT_QW), row(2 * D_MODEL),
                  const(w_bh), const(w_ba), const(w_out)],
        out_specs=[row(D_MODEL), row(D_MODEL), row(D_MODEL), row(2 * D_MODEL), row(HG_W), row(ATT_QW), vec, vec],
        out_shape=[jax.ShapeDtypeStruct((p, D_MODEL), F32), jax.ShapeDtypeStruct((p, D_MODEL), BF16),
                   jax.ShapeDtypeStruct((p, D_MODEL), BF16), jax.ShapeDtypeStruct((p, 2 * D_MODEL), BF16),
                   jax.ShapeDtypeStruct((p, HG_W), F32), jax.ShapeDtypeStruct((p, ATT_QW), F32),
                   jax.ShapeDtypeStruct((1, D_MODEL), F32), jax.ShapeDtypeStruct((1, D_MODEL), F32)],
        compiler_params=_cparams(("arbitrary",)),
    )(dr2, dh1_ffn, xhat1, rstd1, g1, yh, oa, gates, w_bh, w_ba, w_out)


MIX_W = 4 * HG_W + ATT_QW + 2 * ATT_KVW


def _mixers_bwd(proj_hg, proj_att, lbounds, norm_g, lv, states, cos, sin, sinks, dyh, doa, parts, swapped):
    p = proj_hg.shape[0]
    nb = p // BLOCK
    n = len(parts)
    kvw = 2 * ATT_KVW
    rev = lambda s: nb - 1 - s
    c_in, c_out, c_shapes, c_sems = _comm_specs(parts, N_PEERS)

    def body(*refs):
        (x_ref, lb_ref, ng_ref, lv_ref, st_ref, cur_ref, prev_ref, meta_ref, cc, sc, cp, sp, cm, sm, sink_ref,
         dy_ref, do_ref) = refs[:17]
        part_refs = refs[17:17 + n]
        dx_ref, dlb_ref, dng_ref, dsink_ref = refs[17 + n:21 + n]
        recv_refs = refs[21 + n:21 + 2 * n]
        dcarry_ref, dkv_next_ref, dkv_meta_ref = refs[21 + 2 * n:24 + 2 * n]
        starts, waits = _scatter_behind(part_refs, recv_refs, refs[24 + 2 * n:], swapped)
        step = pl.program_id(0)
        c = rev(step)

        @pl.when(step == 0)
        def _():
            dcarry_ref[...] = jnp.zeros_like(dcarry_ref)
            dkv_next_ref[...] = jnp.zeros_like(dkv_next_ref)
            dkv_meta_ref[...] = jnp.zeros_like(dkv_meta_ref)
            dlb_ref[...] = jnp.zeros_like(dlb_ref)
            dng_ref[...] = jnp.zeros_like(dng_ref)
            dsink_ref[...] = jnp.zeros_like(dsink_ref)
            for start in starts:
                start()

        fh = (lax.broadcasted_iota(jnp.int32, (BLOCK, BLOCK), 1) % HEAD_DIM) < (HEAD_DIM // 2)
        qs, kc, vc = _att_load(cur_ref, cc, sc, fh, True)
        _, kp, vp = _att_load(prev_ref, cp, sp, fh, False)
        _, km, vm = _att_load(meta_ref, cm, sm, fh, False)
        s0, s1 = _att_sinkcols(sink_ref)
        masks = _att_masks(c)
        _, att_vjp = jax.vjp(lambda *a: _att_core(*a, *masks), *qs, km, kp, kc, vm, vp, vc, s0, s1)
        dq0, dq1, dq2, dq3, dkm, dkp, dkc, dvm, dvp, dvc, ds0, ds1 = att_vjp(do_ref[...])
        att0 = 4 * HG_W
        for j, dq in enumerate((dq0, dq1, dq2, dq3)):
            dx_ref[:, att0 + j * BLOCK:att0 + (j + 1) * BLOCK] = _rope_t(dq, cc[...], sc[...], fh).astype(BF16)
        dkv_meta_ref[:, :BLOCK] += _rope_t(dkm, cm[...], sm[...], fh)
        dkv_meta_ref[:, BLOCK:] += dvm
        last = jnp.where(c == 0, 1.0, 0.0)
        dk = _rope_t(dkc, cc[...], sc[...], fh) + dkv_next_ref[:, :BLOCK] + last * dkv_meta_ref[:, :BLOCK]
        dv = dvc + dkv_next_ref[:, BLOCK:] + last * dkv_meta_ref[:, BLOCK:]
        dx_ref[:, att0 + ATT_QW:att0 + ATT_QW + ATT_KVW] = dk.astype(BF16)
        dx_ref[:, att0 + ATT_QW + ATT_KVW:] = dv.astype(BF16)
        dkv_next_ref[:, :BLOCK] = _rope_t(dkp, cp[...], sp[...], fh)
        dkv_next_ref[:, BLOCK:] = dvp
        sink_rows = []
        for dsg in (ds0, ds1):
            for j in range(4):
                tot = jnp.sum(dsg[j * BLOCK:(j + 1) * BLOCK], axis=0, keepdims=True)
                sink_rows.append(jnp.broadcast_to(tot, (1, BLOCK)))
        dsink_ref[...] += jnp.concatenate(sink_rows, axis=0)

        valid = (c * BLOCK + lax.broadcasted_iota(jnp.int32, (BLOCK, 1), 0)) >= PAD
        (logf, k), gates_vjp = jax.vjp(lambda hf, a0, a1: _hgrn_gates(hf, a0, a1, valid),
                                       x_ref[:, HG_W:2 * HG_W], lb_ref[0:1, :], lb_ref[1:2, :])
        lvv = lv_ref[...]
        e = _split_dot(lvv, logf, "nn")
        dng = jnp.zeros((1, BLOCK), F32)
        dk, dseg = [], []
        for h in range(HG_HEADS):
            sl = lambda part: x_ref[:, part * HG_W + h * BLOCK: part * HG_W + (h + 1) * BLOCK]
            hs = slice(h * BLOCK, (h + 1) * BLOCK)
            _, vjp = jax.vjp(_hgrn_head, sl(0), k[:, hs], sl(2), sl(3), ng_ref[...], st_ref[0, h], *_seg_blocks(e, h))
            dhq, dkh, dhi, dhg, dngh, dst, *dsegh = vjp((dy_ref[:, hs], dcarry_ref[h]))
            for part, val in ((0, dhq), (2, dhi), (3, dhg)):
                dx_ref[:, part * HG_W + h * BLOCK: part * HG_W + (h + 1) * BLOCK] = val.astype(BF16)
            dk.append(dkh)
            dseg.append(jnp.concatenate(dsegh, axis=0))
            dng = dng + dngh
            dcarry_ref[h] = dst
        dlogf = _split_dot(lvv, jnp.concatenate(dseg, axis=1), "tn")
        dhf, da0, da1 = gates_vjp((dlogf, jnp.concatenate(dk, axis=1)))
        dx_ref[:, HG_W:2 * HG_W] = dhf.astype(BF16)
        dlb_ref[0:1, :] += da0
        dlb_ref[1:2, :] += da1
        dng_ref[...] += dng

        @pl.when(step == nb - 1)
        def _():
            for wait in waits:
                wait()

    const = lambda shape: pl.BlockSpec(shape, lambda s: (0,) * len(shape))
    return pl.pallas_call(
        body, name="mixers_bwd", grid=(nb,),
        in_specs=[pl.BlockSpec((BLOCK, 4 * HG_W), lambda s: (rev(s), 0)), const((2, HG_W)), const((1, BLOCK)),
                  const(lv.shape), pl.BlockSpec((1, HG_HEADS, BLOCK, BLOCK), lambda s: (rev(s), 0, 0, 0))]
        + _att_specs(rev)
        + [pl.BlockSpec((BLOCK, HG_W), lambda s: (rev(s), 0)), pl.BlockSpec((BLOCK, ATT_QW), lambda s: (rev(s), 0))]
        + c_in,
        out_specs=[pl.BlockSpec((BLOCK, MIX_W), lambda s: (rev(s), 0)), const((2, HG_W)), const((1, BLOCK)),
                   const((ATT_HEADS, BLOCK))] + c_out,
        out_shape=[jax.ShapeDtypeStruct((p, MIX_W), BF16), jax.ShapeDtypeStruct((2, HG_W), F32),
                   jax.ShapeDtypeStruct((1, BLOCK), F32), jax.ShapeDtypeStruct((ATT_HEADS, BLOCK), F32)] + c_shapes,
        scratch_shapes=[pltpu.VMEM((HG_HEADS, BLOCK, BLOCK), F32), pltpu.VMEM((BLOCK, kvw), F32),
                        pltpu.VMEM((BLOCK, kvw), F32)] + c_sems,
        compiler_params=_cparams(("arbitrary",)),
    )(proj_hg, lbounds, norm_g, lv, states, proj_att, proj_att, proj_att, cos, sin, cos, sin, cos, sin, sinks,
      dyh, doa, *parts)


def _d_h0_proj(dmix, dgates, w_mix, w_gates, parts, swapped):
    p = dmix.shape[0]
    tm = _row_tile(p, 640)
    nm = p // tm
    n = len(parts)
    c_in, c_out, c_shapes, c_sems = _comm_specs(parts, N_PEERS)

    def body(*refs):
        a_ref, g_ref, wa_ref, wg_ref = refs[:4]
        o_ref = refs[4 + n]
        starts, waits = _scatter_behind(refs[4:4 + n], refs[5 + n:5 + 2 * n], refs[5 + 2 * n:], swapped)
        i = pl.program_id(0)

        @pl.when(i == 0)
        def _():
            for start in starts:
                start()

        o_ref[...] = _dot(a_ref[...], wa_ref[...], "nt") + _dot(g_ref[...], wg_ref[...], "nt")

        @pl.when(i == nm - 1)
        def _():
            for wait in waits:
                wait()

    row = lambda w: pl.BlockSpec((tm, w), lambda i: (i, 0))
    const = lambda a: pl.BlockSpec(a.shape, lambda i: (0, 0))
    return pl.pallas_call(
        body, name="d_h0_proj", grid=(nm,),
        in_specs=[row(dmix.shape[1]), row(dgates.shape[1]), const(w_mix), const(w_gates)] + c_in,
        out_specs=[row(D_MODEL)] + c_out,
        out_shape=[jax.ShapeDtypeStruct((p, D_MODEL), F32)] + c_shapes,
        scratch_shapes=c_sems, compiler_params=_cparams(("arbitrary",)),
    )(dmix, dgates, w_mix, w_gates, *parts)


_LATE = ("w_branch_hg", "w_branch_attn", "w_out", "w_ffn_in", "w_ffn_out")
_COLUMN_SHARDED = ("meta_tokens", "w_in", "w_branch_hg", "w_branch_attn", "w_ffn_in")
_SWAPPED = ("w_ffn_in",)


def _whole(name, gathered):
    _, r, c = gathered.shape
    if name in _COLUMN_SHARDED:
        return jnp.transpose(gathered, (1, 0, 2)).reshape(r, N_DEV * c)
    return gathered.reshape(N_DEV * r, c)


def _slots(name, whole):
    r, c = whole.shape
    if name in _COLUMN_SHARDED:
        return jnp.transpose(whole.reshape(r, N_DEV, c // N_DEV), (1, 0, 2))
    return whole.reshape(N_DEV, r // N_DEV, c)


def _device_step(x, target, meta, ln_emb_g, ln_emb_b, w_in, lbounds, norm_g, sinks, late_shards,
                 ln1_g, ln1_b, ln2_g, ln2_b):
    s = x.shape[0]
    p = s + BLOCK
    xin = jnp.concatenate([jnp.zeros((PAD, D_MODEL), F32), meta, x], axis=0)
    tgt = jnp.concatenate([jnp.zeros((BLOCK, D_MODEL), F32), target], axis=0)
    tm = _row_tile(p, 640)
    lv = _level_stack()
    cos, sin = _rope_tables(p)
    hg_end = 4 * HG_W
    mm = functools.partial(_tiled_matmul, tm=tm)
    swapped = [n in _SWAPPED for n in _LATE]

    h0, xhat0, rstd0 = _embed_ln(xin, ln_emb_g, ln_emb_b)
    proj_hg = mm(h0, w_in[:, :hg_end], "nn", tn=hg_end, tc=D_MODEL, out_dtype=F32, name="proj_hg")
    proj_att = mm(h0, w_in[:, hg_end:MIX_W], "nn", tn=MIX_W - hg_end, tc=D_MODEL, out_dtype=F32, name="proj_att")
    gates = mm(h0, w_in[:, MIX_W:], "nn", tn=2 * D_MODEL, tc=D_MODEL, out_dtype=F32, name="proj_gates")
    yh, states, oa, *gathered = _mixers_fwd(proj_hg, proj_att, lbounds, norm_g, lv, cos, sin, sinks, late_shards, swapped)
    w_bh, w_ba, w_out, w_fi, w_fo = [_whole(n, g) for n, g in zip(_LATE, gathered)]
    mixin, h1, xhat1, rstd1 = _mix_out_ln1(yh, oa, gates, h0, w_bh, w_ba, w_out, ln1_g, ln1_b)
    au, sw = _ffn_in_swiglu(h1, w_fi)
    dr2, loss_part, dg2, db2 = _ffn_out_loss(sw, w_fo, h1, ln2_g, ln2_b, tgt)

    mtn = functools.partial(_tiled_matmul_tn, tm=tm, out_dtype=BF16)
    d_wfo = mtn(sw, dr2, tk=D_FF, tn=D_MODEL, name="grad_w_ffn_out")
    dau = _d_ffn_hidden(dr2, w_fo, au)
    d_wfi = mtn(h1, dau, tk=D_MODEL, tn=D_FF, name="grad_w_ffn_in")
    dh1_ffn = mm(dau, w_fi, "nt", tn=D_MODEL, tc=D_FF, out_dtype=F32, name="d_h1_ffn")
    dr1, dy_hg, dy_att, dgates, dyh, doa, dg1, db1 = _ln1_mix_bwd(
        dr2, dh1_ffn, xhat1, rstd1, ln1_g, yh, oa, gates, w_bh, w_ba, w_out)
    d_wout = mtn(mixin, dr1, tk=D_MODEL, tn=D_MODEL, name="grad_w_out")
    d_wbh = mtn(yh, dy_hg, tk=HG_W, tn=D_MODEL, name="grad_w_branch_hg")
    d_wba = mtn(oa, dy_att, tk=ATT_QW, tn=D_MODEL, name="grad_w_branch_attn")
    late_parts = [_slots(n, g) for n, g in zip(_LATE, (d_wbh, d_wba, d_wout, d_wfi, d_wfo))]
    dmix, d_lb, d_ng, d_sink, *late_recv = _mixers_bwd(
        proj_hg, proj_att, lbounds, norm_g, lv, states, cos, sin, sinks, dyh, doa, late_parts, swapped)
    d_win = jnp.concatenate([mtn(h0, dmix, tk=D_MODEL, tn=MIX_W // 2, name="grad_w_in_mixers"),
                             mtn(h0, dgates, tk=D_MODEL, tn=D_MODEL, name="grad_w_in_gates")], axis=1)
    win_parts = _slots("w_in", d_win)
    dh0_proj, win_recv = _d_h0_proj(dmix, dgates, w_in[:, :MIX_W], w_in[:, MIX_W:], [win_parts], [False])
    dxin, dg0, db0 = _ln_bwd_call(dr1, dh0_proj, ALPHA, xhat0, rstd0, ln_emb_g, "embed_ln_bwd", mask_from=PAD)

    small = dict(ln_emb_g=dg0, ln_emb_b=db0, hg_lower_bounds=d_lb, hg_norm_g=d_ng, attn_sinks=d_sink[:, 0],
                 ln1_g=dg1, ln1_b=db1, ln2_g=dg2, ln2_b=db2)
    big = dict(zip(_LATE, zip(late_parts, late_recv)))
    big["w_in"] = (win_parts, win_recv)
    return loss_part, dxin[BLOCK:], small, dxin[PAD:BLOCK], big


def _all_gather(arrs, dtypes, name):
    n = len(arrs)

    def body(*refs):
        ins, outs, stages = refs[:n], refs[n:2 * n], refs[2 * n:3 * n]
        send_sems, recv_sems, local_sems = refs[3 * n:]
        x, y, c = _place()
        sibling = (x, y, 1 - c)
        chips = [(1 - x, y), (x, 1 - y), (1 - x, 1 - y)]
        slot = lambda px, py, pc: 4 * px + 2 * py + pc

        def copy(w, k, block, to, from_stage=False):
            return pltpu.make_async_remote_copy(
                src_ref=stages[w] if from_stage else outs[w].at[slot(*block)], dst_ref=outs[w].at[slot(*block)],
                send_sem=send_sems.at[w, k], recv_sem=recv_sems.at[w, k], device_id=to, device_id_type=MESH)

        mine, first, passed = [], [], []
        for w in range(n):
            stages[w][...] = ins[w][...].astype(dtypes[w])
            mine.append(pltpu.make_async_copy(stages[w], outs[w].at[slot(x, y, c)], local_sems.at[w]))
            mine[-1].start()
        for w in range(n):
            first.append(copy(w, 0, (x, y, c), sibling, from_stage=True))
            first += [copy(w, 1 + j, (x, y, c), (*chip, c), from_stage=True) for j, chip in enumerate(chips)]
        for cp in first:
            cp.start()
        for j, chip in enumerate(chips):
            for w in range(n):
                copy(w, 1 + j, (*chip, c), (x, y, c)).wait_recv()
                passed.append(copy(w, 4 + j, (*chip, c), sibling))
                passed[-1].start()
        for w in range(n):
            copy(w, 0, sibling, (x, y, c)).wait_recv()
            for j, chip in enumerate(chips):
                copy(w, 4 + j, (*chip, 1 - c), (x, y, c)).wait_recv()
        for cp in first + passed:
            cp.wait_send()
        for cp in mine:
            cp.wait()

    return pl.pallas_call(
        body, name=name,
        in_specs=[pl.BlockSpec(memory_space=pltpu.VMEM)] * n,
        out_specs=[pl.BlockSpec(memory_space=pl.ANY)] * n,
        out_shape=[jax.ShapeDtypeStruct((N_DEV,) + a.shape, dt) for a, dt in zip(arrs, dtypes)],
        scratch_shapes=[pltpu.VMEM(a.shape, dt) for a, dt in zip(arrs, dtypes)]
        + [pltpu.SemaphoreType.DMA((n, 7)), pltpu.SemaphoreType.DMA((n, 7)), pltpu.SemaphoreType.DMA((n,))],
        compiler_params=pltpu.CompilerParams(vmem_limit_bytes=VMEM_LIMIT_BYTES),
    )(*arrs)


def _cast_shards(arrs):
    def body(*refs):
        for src, dst in zip(refs[:len(arrs)], refs[len(arrs):]):
            dst[...] = src[...].astype(BF16)

    return pl.pallas_call(body, name="cast_shards", out_shape=[jax.ShapeDtypeStruct(a.shape, BF16) for a in arrs],
                          compiler_params=pltpu.CompilerParams(vmem_limit_bytes=VMEM_LIMIT_BYTES))(*arrs)


def _shard_rows(rows):
    return rows if rows <= 512 else 256


def _adamw_math(w, g, m, v):
    m = ADAM_B1 * m + (1.0 - ADAM_B1) * g
    v = ADAM_B2 * v + (1.0 - ADAM_B2) * (g * g)
    m_hat = m / (1.0 - ADAM_B1 ** ADAM_STEP)
    v_hat = v / (1.0 - ADAM_B2 ** ADAM_STEP)
    delta = -ADAM_LR * (m_hat / (jnp.sqrt(v_hat) + ADAM_EPS) + ADAM_WD * w)
    return delta, m, v


def _reduce_adamw(parts, recv, own_slot, w, m, v, name):
    r, cdim = w.shape
    tr = _shard_rows(r)

    def body(idx_ref, p_ref, r_ref, w_ref, m_ref, v_ref, g_out, d_out, m_out, v_out):
        g = p_ref[0].astype(F32)
        for j in range(N_PEERS):
            g = g + r_ref[j].astype(F32)
        d, mn, vn = _adamw_math(w_ref[...], g, m_ref[...], v_ref[...])
        g_out[...] = g
        d_out[...] = d
        m_out[...] = mn
        v_out[...] = vn

    flat = pl.BlockSpec((tr, cdim), lambda i, idx_ref: (i, 0))
    return pl.pallas_call(
        body, name=name,
        grid_spec=pltpu.PrefetchScalarGridSpec(
            num_scalar_prefetch=1, grid=(r // tr,),
            in_specs=[pl.BlockSpec((1, tr, cdim), lambda i, idx_ref: (idx_ref[0], i, 0)),
                      pl.BlockSpec((N_PEERS, tr, cdim), lambda i, idx_ref: (0, i, 0)), flat, flat, flat],
            out_specs=[flat] * 4),
        out_shape=[jax.ShapeDtypeStruct((r, cdim), F32)] * 4,
        compiler_params=_cparams(("arbitrary",)),
    )(own_slot, parts, recv, w, m, v)


def _adamw_plain(w, g, m, v, name):
    def body(w_ref, g_ref, m_ref, v_ref, d_out, m_out, v_out):
        d_out[...], m_out[...], v_out[...] = _adamw_math(w_ref[...], g_ref[...], m_ref[...], v_ref[...])

    return pl.pallas_call(body, name=name, out_shape=[jax.ShapeDtypeStruct(w.shape, F32)] * 3)(w, g, m, v)


_SMALL_LAYOUT = (("ln_emb_g", 8), ("ln_emb_b", 8), ("hg_lower_bounds", 8), ("hg_norm_g", 1), ("attn_sinks", 1),
                 ("ln1_g", 8), ("ln1_b", 8), ("ln2_g", 8), ("ln2_b", 8))
_META_ROW = sum(r for _, r in _SMALL_LAYOUT)
_META_ROWS = N_META * D_MODEL // BLOCK
_LOSS_ROW = _META_ROW + _META_ROWS
SMALL_ROWS = 192


def _pack_small(vals, meta=None, loss_row=None):
    rows = []
    for name, nrows in _SMALL_LAYOUT:
        flat = vals[name].reshape(-1).astype(F32)
        flat = jnp.pad(flat, (0, nrows * BLOCK - flat.shape[0]))
        rows.append(flat.reshape(nrows, BLOCK))
    rows.append(jnp.zeros((_META_ROWS, BLOCK), F32) if meta is None else meta.reshape(_META_ROWS, BLOCK))
    rows.append(jnp.zeros((1, BLOCK), F32) if loss_row is None else loss_row)
    packed = jnp.concatenate(rows, axis=0)
    return jnp.pad(packed, ((0, SMALL_ROWS - packed.shape[0]), (0, 0)))


def _unpack_small(packed, shapes):
    out, row = {}, 0
    for name, nrows in _SMALL_LAYOUT:
        size = math.prod(shapes[name])
        out[name] = packed[row:row + nrows].reshape(-1)[:size].reshape(shapes[name])
        row += nrows
    return out


def _small_reduce_adamw(gathered, w, m, v):
    def body(g_ref, w_ref, m_ref, v_ref, g_out, d_out, m_out, v_out, loss_out):
        g = g_ref[0]
        for s in range(1, N_DEV):
            g = g + g_ref[s]
        d, mn, vn = _adamw_math(w_ref[...], g, m_ref[...], v_ref[...])
        g_out[...] = g
        d_out[...] = d
        m_out[...] = mn
        v_out[...] = vn
        loss_out[...] = jnp.broadcast_to(jnp.sum(g_ref[:, _LOSS_ROW, :]), (1, BLOCK))

    shp = jax.ShapeDtypeStruct((SMALL_ROWS, BLOCK), F32)
    return pl.pallas_call(body, name="small_reduce_adamw",
                          out_shape=[shp] * 4 + [jax.ShapeDtypeStruct((1, BLOCK), F32)])(gathered, w, m, v)


_WEIGHTS = ("meta_tokens", "ln_emb_g", "ln_emb_b", "w_in", "hg_lower_bounds", "hg_norm_g", "attn_sinks",
            "w_branch_hg", "w_branch_attn", "w_out", "ln1_g", "ln1_b", "w_ffn_in", "w_ffn_out", "ln2_g", "ln2_b")


def kernel(x, meta_tokens, ln_emb_g, ln_emb_b, w_in, hg_lower_bounds, hg_norm_g, attn_sinks, w_branch_hg, w_branch_attn, w_out, ln1_g, ln1_b, w_ffn_in, w_ffn_out, ln2_g, ln2_b, loss_target, m_meta_tokens, m_ln_emb_g, m_ln_emb_b, m_w_in, m_hg_lower_bounds, m_hg_norm_g, m_attn_sinks, m_w_branch_hg, m_w_branch_attn, m_w_out, m_ln1_g, m_ln1_b, m_w_ffn_in, m_w_ffn_out, m_ln2_g, m_ln2_b, v_meta_tokens, v_ln_emb_g, v_ln_emb_b, v_w_in, v_hg_lower_bounds, v_hg_norm_g, v_attn_sinks, v_w_branch_hg, v_w_branch_attn, v_w_out, v_ln1_g, v_ln1_b, v_w_ffn_in, v_w_ffn_out, v_ln2_g, v_ln2_b):
    given = dict(locals())
    weights = {n: given[n] for n in _WEIGHTS}
    mom1 = {n: given["m_" + n] for n in _WEIGHTS}
    mom2 = {n: given["v_" + n] for n in _WEIGHTS}
    shard2d = lambda a: a.reshape(a.shape[-2:])

    g_meta, g_win = _all_gather([meta_tokens, shard2d(w_in)], [F32, BF16], "gather_first")
    late_shards = _cast_shards([shard2d(weights[n]) for n in _LATE])

    loss_part, grad_x, small_grads, meta_grad, big = _device_step(
        x[0], loss_target[0], _whole("meta_tokens", g_meta), ln_emb_g.reshape(1, -1), ln_emb_b.reshape(1, -1),
        _whole("w_in", g_win), hg_lower_bounds, hg_norm_g, attn_sinks, late_shards, ln1_g, ln1_b, ln2_g, ln2_b)

    place = _place()
    out = {}
    for n, (parts, recv) in big.items():
        own = _slot(place, n in _SWAPPED).astype(jnp.int32).reshape(1)
        res = _reduce_adamw(parts, recv, own, shard2d(weights[n]), shard2d(mom1[n]), shard2d(mom2[n]), "adamw_" + n)
        out[n] = [r.reshape(weights[n].shape) for r in res]

    small_names = [n for n, _ in _SMALL_LAYOUT]
    packed = _pack_small(small_grads, meta_grad, loss_part)
    all_small, = _all_gather([packed], [F32], "gather_small")
    res = _small_reduce_adamw(all_small, _pack_small(weights), _pack_small(mom1), _pack_small(mom2))
    shapes = {n: weights[n].shape for n in small_names}
    unpacked = [_unpack_small(r, shapes) for r in res[:4]]
    for n in small_names:
        out[n] = [u[n] for u in unpacked]
    loss = res[4][0, 0]
    meta_whole = res[0][_META_ROW:_META_ROW + _META_ROWS].reshape(N_META, N_DEV, D_MODEL // N_DEV)
    g_meta_mine = lax.dynamic_index_in_dim(meta_whole, _slot(place, False), axis=1, keepdims=False)
    out["meta_tokens"] = [g_meta_mine, *_adamw_plain(meta_tokens, g_meta_mine, m_meta_tokens, v_meta_tokens,
                                                     "adamw_meta")]

    return (loss, grad_x[None], *[out[n][0] for n in _WEIGHTS], *[out[n][1] for n in _WEIGHTS],
            *[out[n][2] for n in _WEIGHTS], *[out[n][3] for n in _WEIGHTS])
```

```python
import functools
import math

import numpy as np
import jax
import jax.numpy as jnp
from jax import lax
from jax.experimental import pallas as pl
from jax.experimental.pallas import tpu as pltpu

F32 = jnp.float32
BF16 = jnp.bfloat16

D_MODEL = 1024
N_META = 16
BLOCK = 128
PAD = BLOCK - N_META
HG_HEADS = 4
HG_W = 512
ATT_HEADS = 8
HEAD_DIM = 64
ATT_QW = 512
ATT_KVW = 128
D_FF = 2816
EPS = 1e-5
ALPHA = 2.0 ** 0.25
ROPE_THETA = 10000.0
N_DEV = 8

ADAM_LR = 0.001
ADAM_B1 = 0.9
ADAM_B2 = 0.999
ADAM_EPS = 1e-08
ADAM_WD = 0.01
ADAM_STEP = 10

VMEM_LIMIT_BYTES = 56 * 1024 * 1024
MESH = pl.DeviceIdType.MESH

_LEVELS = (64, 32, 16, 8, 4, 2, 1)


def _cparams(sem):
    return pltpu.CompilerParams(dimension_semantics=sem, vmem_limit_bytes=VMEM_LIMIT_BYTES)


def _row_tile(rows, target):
    nb = rows // BLOCK
    best = 1
    for d in range(1, nb + 1):
        if nb % d == 0 and d * BLOCK <= target:
            best = d
    return best * BLOCK


_DN = {"nn": (((1,), (0,)), ((), ())), "nt": (((1,), (1,)), ((), ())), "tn": (((0,), (0,)), ((), ()))}


def _dot(a, b, form):
    return lax.dot_general(a.astype(BF16), b.astype(BF16), _DN[form], preferred_element_type=F32)


@functools.partial(jax.custom_vjp, nondiff_argnums=(2,))
def _mm(a, b, form):
    return _dot(a, b, form)


def _mm_fwd(a, b, form):
    a, b = a.astype(BF16), b.astype(BF16)
    return _dot(a, b, form), (a, b)


def _mm_bwd(form, res, g):
    a, b = res
    if form == "nn":
        return _dot(g, b, "nt"), _dot(a, g, "tn")
    if form == "nt":
        return _dot(g, b, "nn"), _dot(g, a, "tn")
    return _dot(b, g, "nt"), _dot(a, g, "nn")


_mm.defvjp(_mm_fwd, _mm_bwd)


def _split_dot(lv, x, form):
    hi = x.astype(BF16)
    lo = (x - hi.astype(F32)).astype(BF16)
    return (lax.dot_general(lv, hi, _DN[form], preferred_element_type=F32)
            + lax.dot_general(lv, lo, _DN[form], preferred_element_type=F32))


@jax.custom_vjp
def _swap_halves(x):
    return pltpu.roll(x, 64, 1)


_swap_halves.defvjp(lambda x: (pltpu.roll(x, 64, 1), None), lambda _, g: (pltpu.roll(g, 64, 1),))


def _tiled_matmul(a, b, form, *, tm, tn, tc, out_dtype, name):
    m, c = a.shape
    n = b.shape[1] if form == "nn" else b.shape[0]
    assert m % tm == 0 and n % tn == 0 and c % tc == 0, (name, a.shape, b.shape, tm, tn, tc)
    nc = c // tc

    def body(a_ref, b_ref, o_ref, *scratch):
        part = _dot(a_ref[...], b_ref[...], form)
        if nc == 1:
            o_ref[...] = part.astype(out_dtype)
            return
        acc_ref, = scratch
        ci = pl.program_id(2)

        @pl.when(ci == 0)
        def _():
            acc_ref[...] = part

        @pl.when(ci > 0)
        def _():
            acc_ref[...] += part

        @pl.when(ci == nc - 1)
        def _():
            o_ref[...] = acc_ref[...].astype(out_dtype)

    b_spec = (pl.BlockSpec((tc, tn), lambda j, i, k: (k, j)) if form == "nn"
              else pl.BlockSpec((tn, tc), lambda j, i, k: (j, k)))
    return pl.pallas_call(
        body, name=name, grid=(n // tn, m // tm, nc),
        in_specs=[pl.BlockSpec((tm, tc), lambda j, i, k: (i, k)), b_spec],
        out_specs=pl.BlockSpec((tm, tn), lambda j, i, k: (i, j)),
        out_shape=jax.ShapeDtypeStruct((m, n), out_dtype),
        scratch_shapes=[] if nc == 1 else [pltpu.VMEM((tm, tn), F32)],
        compiler_params=_cparams(("arbitrary", "arbitrary", "arbitrary")),
    )(a, b)


def _tiled_matmul_tn(a, b, *, tm, tk, tn, out_dtype, name):
    m, k = a.shape
    n = b.shape[1]
    assert m % tm == 0 and k % tk == 0 and n % tn == 0, (name, a.shape, b.shape, tm, tk, tn)
    nm = m // tm

    def body(a_ref, b_ref, o_ref, acc_ref):
        part = _dot(a_ref[...], b_ref[...], "tn")
        mi = pl.program_id(2)

        @pl.when(mi == 0)
        def _():
            acc_ref[...] = part

        @pl.when(mi > 0)
        def _():
            acc_ref[...] += part

        @pl.when(mi == nm - 1)
        def _():
            o_ref[...] = acc_ref[...].astype(out_dtype)

    return pl.pallas_call(
        body, name=name, grid=(k // tk, n // tn, nm),
        in_specs=[pl.BlockSpec((tm, tk), lambda kk, j, i: (i, kk)), pl.BlockSpec((tm, tn), lambda kk, j, i: (i, j))],
        out_specs=pl.BlockSpec((tk, tn), lambda kk, j, i: (kk, j)),
        out_shape=jax.ShapeDtypeStruct((k, n), out_dtype),
        scratch_shapes=[pltpu.VMEM((tk, tn), F32)],
        compiler_params=_cparams(("arbitrary", "arbitrary", "arbitrary")),
    )(a, b)


def _ln_stats(r):
    mu = jnp.mean(r, axis=-1, keepdims=True)
    xc = r - mu
    var = jnp.mean(xc * xc, axis=-1, keepdims=True)
    rstd = lax.rsqrt(var + EPS)
    return xc * rstd, rstd


def _ln_bwd(dy, xhat, rstd, g):
    dxhat = dy * g
    m1 = jnp.mean(dxhat, axis=-1, keepdims=True)
    m2 = jnp.mean(dxhat * xhat, axis=-1, keepdims=True)
    dr = rstd * (dxhat - m1 - xhat * m2)
    return dr, jnp.sum(dy * xhat, axis=0, keepdims=True), jnp.sum(dy, axis=0, keepdims=True)


N_SEG = 3 + len(_LEVELS)


def _level_stack():
    t = np.arange(BLOCK)[:, None]
    r = np.arange(BLOCK)[None, :]
    mats = [r <= t, r > t, np.ones((BLOCK, BLOCK), bool)]
    for h in _LEVELS:
        same = (t // (2 * h)) == (r // (2 * h))
        up_t, up_r = (t % (2 * h)) >= h, (r % (2 * h)) >= h
        mats.append(same & ((up_t & up_r & (r <= t)) | (~up_t & ~up_r & (r > t))))
    return jnp.asarray(np.concatenate(mats, axis=0).astype(np.float32), dtype=BF16)


def _hgrn_gates(hf, a0, a1, valid):
    lb = jax.nn.sigmoid(a0 - a1)
    fg = lb + (1.0 - lb) * jax.nn.sigmoid(hf)
    return jnp.where(valid, jnp.log(fg), 0.0), jnp.where(valid, 1.0 - fg, 0.0)


def _hgrn_head(hq, k, v, hg, ng, st_in, *seg):
    q = jax.nn.silu(hq)
    rows = lax.broadcasted_iota(jnp.int32, (BLOCK, BLOCK), 0)
    cols = lax.broadcasted_iota(jnp.int32, (BLOCK, BLOCK), 1)
    o = _mm(q * jnp.exp(seg[0]), st_in, "nt")
    a = jnp.where(rows == cols, jnp.sum(q * k, axis=-1, keepdims=True), 0.0)
    for li, h in enumerate(_LEVELS):
        decay = jnp.exp(seg[3 + li])
        pair = ((rows // (2 * h)) == (cols // (2 * h))) & ((rows % (2 * h)) >= h) & ((cols % (2 * h)) < h)
        a = a + jnp.where(pair, _mm(q * decay, k * decay, "nt"), 0.0)
    o = o + _mm(a, v, "nn")
    st_out = st_in * jnp.exp(seg[2]) + _mm(v, k * jnp.exp(seg[1]), "tn")
    on = o * lax.rsqrt(jnp.mean(o * o, axis=-1, keepdims=True) + EPS) * ng
    return on * jax.nn.silu(hg), st_out


def _seg_blocks(e, h):
    return [e[i * BLOCK:(i + 1) * BLOCK, h * BLOCK:(h + 1) * BLOCK] for i in range(N_SEG)]


def _rope(x, cos, sin, first_half):
    partner = jnp.where(first_half, -pltpu.roll(x, 96, 1), pltpu.roll(x, 32, 1))
    return x * cos + partner * sin


def _rope_t(g, cos, sin, first_half):
    u = g * sin
    partner = jnp.where(first_half, pltpu.roll(u, 96, 1), -pltpu.roll(u, 32, 1))
    return g * cos + partner


def _att_core(q0, q1, q2, q3, km, kp, kc, vm, vp, vc, sinkcol0, sinkcol1, ok_m, ok_p, ok_c):
    lane = lax.broadcasted_iota(jnp.int32, (BLOCK, BLOCK), 1)
    low = lane < HEAD_DIM
    scale = HEAD_DIM ** -0.5
    neg = jnp.finfo(F32).min
    outs = []
    for g, (qa, qb, sinkcol) in enumerate(((q0, q1, sinkcol0), (q2, q3, sinkcol1))):
        def both(x, g=g):
            sw = _swap_halves(x)
            return jnp.where(low, x, sw) if g == 0 else jnp.where(low, sw, x)
        q4 = jnp.concatenate([jnp.where(low, qa, 0.0), jnp.where(low, 0.0, qa),
                              jnp.where(low, qb, 0.0), jnp.where(low, 0.0, qb)], axis=0)
        s = []
        for kk, ok in ((km, ok_m), (kp, ok_p), (kc, ok_c)):
            ok4 = jnp.concatenate([ok] * 4, axis=0)
            s.append(jnp.where(ok4, _mm(q4, both(kk), "nt") * scale, neg))
        mx = jnp.maximum(jnp.maximum(jnp.max(s[0], axis=-1, keepdims=True), jnp.max(s[1], axis=-1, keepdims=True)),
                         jnp.maximum(jnp.max(s[2], axis=-1, keepdims=True), sinkcol))
        mx = lax.stop_gradient(mx)
        p = [jnp.exp(si - mx) for si in s]
        den = (jnp.sum(p[0], axis=-1, keepdims=True) + jnp.sum(p[1], axis=-1, keepdims=True)
               + jnp.sum(p[2], axis=-1, keepdims=True) + jnp.exp(sinkcol - mx))
        inv = 1.0 / den
        o4 = (_mm(p[0] * inv, both(vm), "nn") + _mm(p[1] * inv, both(vp), "nn") + _mm(p[2] * inv, both(vc), "nn"))
        for j in range(2):
            outs.append(jnp.where(low, o4[(2 * j) * BLOCK:(2 * j + 1) * BLOCK],
                                  o4[(2 * j + 1) * BLOCK:(2 * j + 2) * BLOCK]))
    return jnp.concatenate(outs, axis=1)


def _att_masks(blk_idx):
    qpos = blk_idx * BLOCK + lax.broadcasted_iota(jnp.int32, (BLOCK, BLOCK), 0) - PAD
    kidx = lax.broadcasted_iota(jnp.int32, (BLOCK, BLOCK), 1)
    pos_m = kidx - PAD
    pos_p = (blk_idx - 1) * BLOCK + kidx - PAD
    pos_c = blk_idx * BLOCK + kidx - PAD
    ok_m = (pos_m >= 0) & (pos_m <= qpos)
    ok_p = (pos_p >= N_META) & (qpos - pos_p < BLOCK) & (blk_idx >= 1)
    ok_c = (pos_c >= N_META) & (pos_c <= qpos)
    return ok_m, ok_p, ok_c


def _token_streams(tr):
    k = tr // BLOCK
    return [pl.BlockSpec((BLOCK, D_MODEL), lambda i, j=j: (jnp.maximum(k * i - 1 + j, 0), 0)) for j in range(k)]


def _embed_ln(x, lead, g0, b0):
    p = x.shape[0] + BLOCK
    tr = _row_tile(p, 640)
    k = tr // BLOCK

    def body(*refs):
        lead_ref, g_ref, b_ref, h_ref, hb_ref, xh_ref, rs_ref = refs[k:]
        first = jnp.where(pl.program_id(0) == 0, lead_ref[...], refs[0][...])
        xhat, rstd = _ln_stats(jnp.concatenate([first] + [r[...] for r in refs[1:k]], axis=0))
        row = pl.program_id(0) * tr + lax.broadcasted_iota(jnp.int32, (tr, 1), 0)
        h = jnp.where(row >= PAD, xhat * g_ref[...] + b_ref[...], 0.0)
        h_ref[...] = h
        hb_ref[...] = h.astype(BF16)
        xh_ref[...] = xhat
        rs_ref[...] = rstd

    vec = pl.BlockSpec((1, D_MODEL), lambda i: (0, 0))
    rowsp = pl.BlockSpec((tr, D_MODEL), lambda i: (i, 0))
    return pl.pallas_call(
        body, name="embed_ln", grid=(p // tr,),
        in_specs=_token_streams(tr) + [pl.BlockSpec((BLOCK, D_MODEL), lambda i: (0, 0)), vec, vec],
        out_specs=[rowsp, rowsp, rowsp, pl.BlockSpec((tr, 1), lambda i: (i, 0))],
        out_shape=[jax.ShapeDtypeStruct((p, D_MODEL), F32), jax.ShapeDtypeStruct((p, D_MODEL), BF16),
                   jax.ShapeDtypeStruct((p, D_MODEL), F32), jax.ShapeDtypeStruct((p, 1), F32)],
        compiler_params=_cparams(("arbitrary",)),
    )(*([x] * k), lead, g0, b0)


def _rope_tables(p):
    pos = (np.arange(p, dtype=np.int32) - PAD).astype(np.float32)
    half = HEAD_DIM // 2
    inv = np.float32(ROPE_THETA) ** (-np.arange(half, dtype=np.float32) / np.float32(half))
    ang = pos[:, None] * np.tile(inv.astype(np.float32), BLOCK // half)[None, :]
    return jnp.asarray(np.cos(ang), F32), jnp.asarray(np.sin(ang), F32)


def _att_sinkcols(sink_ref):
    rowhead = lax.broadcasted_iota(jnp.int32, (4 * BLOCK, 1), 0) // BLOCK
    cols = []
    for g in range(2):
        col = jnp.zeros((4 * BLOCK, 1), F32)
        for j in range(4):
            col = jnp.where(rowhead == j, sink_ref[0, 4 * g + j], col)
        cols.append(col)
    return cols


def _att_load(qkv_ref, cos_ref, sin_ref, first_half, with_q):
    cos, sin = cos_ref[...], sin_ref[...]
    qs = [_rope(qkv_ref[:, j * BLOCK:(j + 1) * BLOCK], cos, sin, first_half) for j in range(4)] if with_q else None
    k = _rope(qkv_ref[:, ATT_QW:ATT_QW + ATT_KVW], cos, sin, first_half)
    v = qkv_ref[:, ATT_QW + ATT_KVW:ATT_QW + 2 * ATT_KVW]
    return qs, k, v


def _att_specs(blk):
    w = ATT_QW + 2 * ATT_KVW
    cur = lambda width: pl.BlockSpec((BLOCK, width), lambda i: (blk(i), 0))
    prev = lambda width: pl.BlockSpec((BLOCK, width), lambda i: (jnp.maximum(blk(i) - 1, 0), 0))
    meta = lambda width: pl.BlockSpec((BLOCK, width), lambda i: (0, 0))
    return [cur(w), prev(w), meta(w), cur(BLOCK), cur(BLOCK), prev(BLOCK), prev(BLOCK), meta(BLOCK), meta(BLOCK),
            pl.BlockSpec(memory_space=pltpu.SMEM)]


_FLIPS = [(dx, dy, dc) for dx in (0, 1) for dy in (0, 1) for dc in (0, 1)][1:]
N_PEERS = len(_FLIPS)


def _place():
    return lax.axis_index("x"), lax.axis_index("y"), lax.axis_index("c")


def _peer(place, flip):
    return tuple(1 - p if f else p for p, f in zip(place, flip))


def _slot(place, swapped):
    x, y, c = place
    return 4 * y + 2 * x + c if swapped else 4 * x + 2 * y + c


def _comm_specs(arrs, out_lead):
    n = len(arrs)
    outs = [jax.ShapeDtypeStruct((out_lead,) + a.shape[-2:], a.dtype) for a in arrs]
    sems = [pltpu.SemaphoreType.DMA((n, N_PEERS)), pltpu.SemaphoreType.DMA((n, N_PEERS)), pltpu.SemaphoreType.DMA((n,))]
    return [pl.BlockSpec(memory_space=pl.ANY)] * n, [pl.BlockSpec(memory_space=pl.ANY)] * n, outs, sems


def _gather_behind(shard_refs, out_refs, sems, swapped):
    send_sems, recv_sems, local_sems = sems
    place = _place()
    starts, waits = [], []
    for w, (s, o) in enumerate(zip(shard_refs, out_refs)):
        mine = _slot(place, swapped[w])
        own = pltpu.make_async_copy(s, o.at[mine], local_sems.at[w])
        starts.append(own.start)
        waits.append(own.wait)
        for r, flip in enumerate(_FLIPS):
            peer = _peer(place, flip)
            kw = dict(send_sem=send_sems.at[w, r], recv_sem=recv_sems.at[w, r], device_id=peer, device_id_type=MESH)
            out_cp = pltpu.make_async_remote_copy(src_ref=s, dst_ref=o.at[mine], **kw)
            in_cp = pltpu.make_async_remote_copy(src_ref=s, dst_ref=o.at[_slot(peer, swapped[w])], **kw)
            starts.append(out_cp.start)
            waits += [in_cp.wait_recv, out_cp.wait_send]
    return starts, waits


def _scatter_behind(part_refs, recv_refs, sems, swapped):
    send_sems, recv_sems, _ = sems
    place = _place()
    starts, waits = [], []
    for w, (p, o) in enumerate(zip(part_refs, recv_refs)):
        for r, flip in enumerate(_FLIPS):
            peer = _peer(place, flip)
            cp = pltpu.make_async_remote_copy(
                src_ref=p.at[_slot(peer, swapped[w])], dst_ref=o.at[r], send_sem=send_sems.at[w, r],
                recv_sem=recv_sems.at[w, r], device_id=peer, device_id_type=MESH)
            starts.append(cp.start)
            waits += [cp.wait_recv, cp.wait_send]
    return starts, waits


def _mixers_fwd(proj_hg, proj_att, lbounds, norm_g, lv, cos, sin, sinks, shards, swapped):
    p = proj_hg.shape[0]
    nb = p // BLOCK
    n = len(shards)
    c_in, c_out, c_shapes, c_sems = _comm_specs(shards, N_DEV)

    def body(*refs):
        x_ref, lb_ref, ng_ref, lv_ref, cur_ref, prev_ref, meta_ref, cc, sc, cp, sp, cm, sm, sink_ref = refs[:14]
        shard_refs = refs[14:14 + n]
        y_ref, st_ref, o_ref = refs[14 + n:17 + n]
        out_refs = refs[17 + n:17 + 2 * n]
        carry_ref = refs[17 + 2 * n]
        starts, waits = _gather_behind(shard_refs, out_refs, refs[18 + 2 * n:], swapped)
        c = pl.program_id(0)

        @pl.when(c == 0)
        def _():
            carry_ref[...] = jnp.zeros_like(carry_ref)
            for start in starts:
                start()

        valid = (c * BLOCK + lax.broadcasted_iota(jnp.int32, (BLOCK, 1), 0)) >= PAD
        logf, k = _hgrn_gates(x_ref[:, HG_W:2 * HG_W], lb_ref[0:1, :], lb_ref[1:2, :], valid)
        e = _split_dot(lv_ref[...], logf, "nn")
        for h in range(HG_HEADS):
            sl = lambda part: x_ref[:, part * HG_W + h * BLOCK: part * HG_W + (h + 1) * BLOCK]
            hs = slice(h * BLOCK, (h + 1) * BLOCK)
            st_in = carry_ref[h]
            st_ref[0, h] = st_in
            y, st_out = _hgrn_head(sl(0), k[:, hs], sl(2), sl(3), ng_ref[...], st_in, *_seg_blocks(e, h))
            y_ref[:, hs] = y.astype(BF16)
            carry_ref[h] = st_out

        fh = (lax.broadcasted_iota(jnp.int32, (BLOCK, BLOCK), 1) % HEAD_DIM) < (HEAD_DIM // 2)
        qs, kc, vc = _att_load(cur_ref, cc, sc, fh, True)
        _, kp, vp = _att_load(prev_ref, cp, sp, fh, False)
        _, km, vm = _att_load(meta_ref, cm, sm, fh, False)
        s0, s1 = _att_sinkcols(sink_ref)
        o_ref[...] = _att_core(*qs, km, kp, kc, vm, vp, vc, s0, s1, *_att_masks(c)).astype(BF16)

        @pl.when(c == nb - 1)
        def _():
            for wait in waits:
                wait()

    return pl.pallas_call(
        body, name="mixers_fwd", grid=(nb,),
        in_specs=[pl.BlockSpec((BLOCK, 4 * HG_W), lambda c: (c, 0)), pl.BlockSpec((2, HG_W), lambda c: (0, 0)),
                  pl.BlockSpec((1, BLOCK), lambda c: (0, 0)), pl.BlockSpec(lv.shape, lambda c: (0, 0))]
        + _att_specs(lambda c: c) + c_in,
        out_specs=[pl.BlockSpec((BLOCK, HG_W), lambda c: (c, 0)),
                   pl.BlockSpec((1, HG_HEADS, BLOCK, BLOCK), lambda c: (c, 0, 0, 0)),
                   pl.BlockSpec((BLOCK, ATT_QW), lambda c: (c, 0))] + c_out,
        out_shape=[jax.ShapeDtypeStruct((p, HG_W), BF16), jax.ShapeDtypeStruct((nb, HG_HEADS, BLOCK, BLOCK), F32),
                   jax.ShapeDtypeStruct((p, ATT_QW), BF16)] + c_shapes,
        scratch_shapes=[pltpu.VMEM((HG_HEADS, BLOCK, BLOCK), F32)] + c_sems,
        compiler_params=_cparams(("arbitrary",)),
    )(proj_hg, lbounds, norm_g, lv, proj_att, proj_att, proj_att, cos, sin, cos, sin, cos, sin, sinks, *shards)


def _tile(rows, preferred):
    return preferred if rows % preferred == 0 else _row_tile(rows, preferred)


def _branch_mix(yh, oa, gates, w_bh, w_ba):
    y_hg = _dot(yh, w_bh, "nn")
    y_att = _dot(oa, w_ba, "nn")
    s1 = jax.nn.sigmoid(gates[:, :D_MODEL])
    s2 = jax.nn.sigmoid(gates[:, D_MODEL:])
    return s1 * y_hg + s2 * y_att, y_hg, y_att, s1, s2


def _mix_out_ln1(yh, oa, gates, h0, w_bh, w_ba, w_out, g1, b1):
    p = yh.shape[0]
    tr = _tile(p, 320)

    def body(yh_ref, oa_ref, g_ref, h0_ref, wbh_ref, wba_ref, wo_ref, g1_ref, b1_ref,
             mix_ref, h1_ref, h1b_ref, xh_ref, rs_ref):
        mixin = _branch_mix(yh_ref[...], oa_ref[...], g_ref[...], wbh_ref[...], wba_ref[...])[0]
        mix_ref[...] = mixin.astype(BF16)
        xhat, rstd = _ln_stats(ALPHA * h0_ref[...] + _dot(mixin, wo_ref[...], "nn"))
        h1 = xhat * g1_ref[...] + b1_ref[...]
        h1_ref[...] = h1
        h1b_ref[...] = h1.astype(BF16)
        xh_ref[...] = xhat
        rs_ref[...] = rstd

    row = lambda w: pl.BlockSpec((tr, w), lambda i: (i, 0))
    const = lambda a: pl.BlockSpec(a.shape, lambda i: (0, 0))
    return pl.pallas_call(
        body, name="mix_out_ln1", grid=(p // tr,),
        in_specs=[row(HG_W), row(ATT_QW), row(2 * D_MODEL), row(D_MODEL), const(w_bh), const(w_ba), const(w_out),
                  const(g1), const(b1)],
        out_specs=[row(D_MODEL), row(D_MODEL), row(D_MODEL), row(D_MODEL), row(1)],
        out_shape=[jax.ShapeDtypeStruct((p, D_MODEL), BF16), jax.ShapeDtypeStruct((p, D_MODEL), F32),
                   jax.ShapeDtypeStruct((p, D_MODEL), BF16), jax.ShapeDtypeStruct((p, D_MODEL), F32),
                   jax.ShapeDtypeStruct((p, 1), F32)],
        compiler_params=_cparams(("arbitrary",)),
    )(yh, oa, gates, h0, w_bh, w_ba, w_out, g1, b1)


FF_T = D_FF // 2


def _ffn_in_swiglu(h1, w_fi):
    p = h1.shape[0]
    tm = _row_tile(p, 640)

    def body(h_ref, w_ref, au_ref, s_ref):
        au = _dot(h_ref[...], w_ref[...], "nn")
        au_ref[...] = au
        s_ref[...] = (jax.nn.silu(au[:, :FF_T]) * au[:, FF_T:]).astype(BF16)

    return pl.pallas_call(
        body, name="ffn_in_swiglu", grid=(D_FF // FF_T, p // tm),
        in_specs=[pl.BlockSpec((tm, D_MODEL), lambda j, i: (i, 0)), pl.BlockSpec((D_MODEL, 2 * FF_T), lambda j, i: (0, j))],
        out_specs=[pl.BlockSpec((tm, 2 * FF_T), lambda j, i: (i, j)), pl.BlockSpec((tm, FF_T), lambda j, i: (i, j))],
        out_shape=[jax.ShapeDtypeStruct((p, 2 * D_FF), F32), jax.ShapeDtypeStruct((p, D_FF), BF16)],
        compiler_params=_cparams(("arbitrary", "arbitrary")),
    )(h1, w_fi)


def _ffn_out_loss(s, w_fo, h1, g2, b2, target):
    p = h1.shape[0]
    tr = _row_tile(p, 640)
    k = tr // BLOCK

    def body(*refs):
        s_ref, w_ref, h_ref, g_ref, b_ref = refs[:5]
        dr_ref, loss_ref, dg_ref, db_ref = refs[5 + k:]
        i = pl.program_id(0)
        xhat, rstd = _ln_stats(ALPHA * h_ref[...] + _dot(s_ref[...], w_ref[...], "nn"))
        y = xhat * g_ref[...] + b_ref[...]
        row = i * tr + lax.broadcasted_iota(jnp.int32, (tr, 1), 0)
        tgt = jnp.concatenate([r[...] for r in refs[5:5 + k]], axis=0)
        err = jnp.where(row >= BLOCK, y - tgt, 0.0)
        dr, dg, db = _ln_bwd(err * (1.0 / D_MODEL), xhat, rstd, g_ref[...])
        dr_ref[...] = dr
        e2 = jnp.sum(err * err, axis=0, keepdims=True)
        part = e2[:, 0:BLOCK]
        for j in range(1, D_MODEL // BLOCK):
            part = part + e2[:, j * BLOCK:(j + 1) * BLOCK]
        part = part * (0.5 / D_MODEL)

        @pl.when(i == 0)
        def _():
            loss_ref[...] = part
            dg_ref[...] = dg
            db_ref[...] = db

        @pl.when(i > 0)
        def _():
            loss_ref[...] += part
            dg_ref[...] += dg
            db_ref[...] += db

    vec = pl.BlockSpec((1, D_MODEL), lambda i: (0, 0))
    rowsp = pl.BlockSpec((tr, D_MODEL), lambda i: (i, 0))
    return pl.pallas_call(
        body, name="ffn_out_loss", grid=(p // tr,),
        in_specs=[pl.BlockSpec((tr, D_FF), lambda i: (i, 0)), pl.BlockSpec((D_FF, D_MODEL), lambda i: (0, 0)),
                  rowsp, vec, vec] + _token_streams(tr),
        out_specs=[rowsp, pl.BlockSpec((1, BLOCK), lambda i: (0, 0)), vec, vec],
        out_shape=[jax.ShapeDtypeStruct((p, D_MODEL), F32), jax.ShapeDtypeStruct((1, BLOCK), F32),
                   jax.ShapeDtypeStruct((1, D_MODEL), F32), jax.ShapeDtypeStruct((1, D_MODEL), F32)],
        compiler_params=_cparams(("arbitrary",)),
    )(s, w_fo, h1, g2, b2, *([target] * k))


def _d_ffn_hidden(dr2, w_fo, au):
    p = au.shape[0]
    tm = _row_tile(p, 640)

    def body(d_ref, w_ref, au_ref, o_ref):
        ds = _dot(d_ref[...], w_ref[...], "nt")
        _, vjp = jax.vjp(lambda a, u: jax.nn.silu(a) * u, au_ref[:, :FF_T], au_ref[:, FF_T:])
        da, du = vjp(ds)
        o_ref[:, :FF_T] = da.astype(BF16)
        o_ref[:, FF_T:] = du.astype(BF16)

    return pl.pallas_call(
        body, name="d_ffn_hidden", grid=(D_FF // FF_T, p // tm),
        in_specs=[pl.BlockSpec((tm, D_MODEL), lambda j, i: (i, 0)), pl.BlockSpec((FF_T, D_MODEL), lambda j, i: (j, 0)),
                  pl.BlockSpec((tm, 2 * FF_T), lambda j, i: (i, j))],
        out_specs=pl.BlockSpec((tm, 2 * FF_T), lambda j, i: (i, j)),
        out_shape=jax.ShapeDtypeStruct((p, 2 * D_FF), BF16), compiler_params=_cparams(("arbitrary", "arbitrary")),
    )(dr2, w_fo, au)


def _ln_bwd_call(d_a, d_b, scale_a, xhat, rstd, g, name, mask_from=None):
    p = xhat.shape[0]
    tr = _row_tile(p, 640)

    def body(a_ref, b_ref, xh_ref, rs_ref, g_ref, dr_ref, dg_ref, db_ref):
        i = pl.program_id(0)
        dy = scale_a * a_ref[...] + b_ref[...]
        if mask_from is not None:
            row = i * tr + lax.broadcasted_iota(jnp.int32, (tr, 1), 0)
            dy = jnp.where(row >= mask_from, dy, 0.0)
        dr, dg, db = _ln_bwd(dy, xh_ref[...], rs_ref[...], g_ref[...])
        dr_ref[...] = dr

        @pl.when(i == 0)
        def _():
            dg_ref[...] = dg
            db_ref[...] = db

        @pl.when(i > 0)
        def _():
            dg_ref[...] += dg
            db_ref[...] += db

    vec = pl.BlockSpec((1, D_MODEL), lambda i: (0, 0))
    rowsp = pl.BlockSpec((tr, D_MODEL), lambda i: (i, 0))
    return pl.pallas_call(
        body, name=name, grid=(p // tr,),
        in_specs=[rowsp, rowsp, rowsp, pl.BlockSpec((tr, 1), lambda i: (i, 0)), vec],
        out_specs=[rowsp, vec, vec],
        out_shape=[jax.ShapeDtypeStruct((p, D_MODEL), F32)] + [jax.ShapeDtypeStruct((1, D_MODEL), F32)] * 2,
        compiler_params=_cparams(("arbitrary",)),
    )(d_a, d_b, xhat, rstd, g)


def _ln1_mix_bwd(dr2, dh1_ffn, xhat1, rstd1, g1, yh, oa, gates, w_bh, w_ba, w_out):
    p = yh.shape[0]
    tr = _tile(p, 320)

    def body(a_ref, b_ref, xh_ref, rs_ref, g1_ref, yh_ref, oa_ref, g_ref, wbh_ref, wba_ref, wo_ref,
             dr_ref, dyhg_ref, dyat_ref, dgt_ref, dyh_ref, doa_ref, dg_ref, db_ref):
        i = pl.program_id(0)
        dr, dg, db = _ln_bwd(ALPHA * a_ref[...] + b_ref[...], xh_ref[...], rs_ref[...], g1_ref[...])
        dr_ref[...] = dr
        d = _dot(dr, wo_ref[...], "nt")
        _, y_hg, y_att, s1, s2 = _branch_mix(yh_ref[...], oa_ref[...], g_ref[...], wbh_ref[...], wba_ref[...])
        dy_hg = d * s1
        dy_att = d * s2
        dyhg_ref[...] = dy_hg.astype(BF16)
        dyat_ref[...] = dy_att.astype(BF16)
        dgt_ref[:, :D_MODEL] = (d * y_hg * s1 * (1.0 - s1)).astype(BF16)
        dgt_ref[:, D_MODEL:] = (d * y_att * s2 * (1.0 - s2)).astype(BF16)
        dyh_ref[...] = _dot(dy_hg, wbh_ref[...], "nt")
        doa_ref[...] = _dot(dy_att, wba_ref[...], "nt")

        @pl.when(i == 0)
        def _():
            dg_ref[...] = dg
            db_ref[...] = db

        @pl.when(i > 0)
        def _():
            dg_ref[...] += dg
            db_ref[...] += db

    row = lambda w: pl.BlockSpec((tr, w), lambda i: (i, 0))
    const = lambda a: pl.BlockSpec(a.shape, lambda i: (0, 0))
    vec = pl.BlockSpec((1, D_MODEL), lambda i: (0, 0))
    return pl.pallas_call(
        body, name="ln1_mix_bwd", grid=(p // tr,),
        in_specs=[row(D_MODEL), row(D_MODEL), row(D_MODEL), row(1), vec, row(HG_W), row(ATT_QW), row(2 * D_MODEL),
                  const(w_bh), const(w_ba), const(w_out)],
        out_specs=[row(D_MODEL), row(D_MODEL), row(D_MODEL), row(2 * D_MODEL), row(HG_W), row(ATT_QW), vec, vec],
        out_shape=[jax.ShapeDtypeStruct((p, D_MODEL), F32), jax.ShapeDtypeStruct((p, D_MODEL), BF16),
                   jax.ShapeDtypeStruct((p, D_MODEL), BF16), jax.ShapeDtypeStruct((p, 2 * D_MODEL), BF16),
                   jax.ShapeDtypeStruct((p, HG_W), F32), jax.ShapeDtypeStruct((p, ATT_QW), F32),
                   jax.ShapeDtypeStruct((1, D_MODEL), F32), jax.ShapeDtypeStruct((1, D_MODEL), F32)],
        compiler_params=_cparams(("arbitrary",)),
    )(dr2, dh1_ffn, xhat1, rstd1, g1, yh, oa, gates, w_bh, w_ba, w_out)


MIX_W = 4 * HG_W + ATT_QW + 2 * ATT_KVW


def _mixers_bwd(proj_hg, proj_att, lbounds, norm_g, lv, states, cos, sin, sinks, dyh, doa, parts, swapped):
    p = proj_hg.shape[0]
    nb = p // BLOCK
    n = len(parts)
    kvw = 2 * ATT_KVW
    rev = lambda s: nb - 1 - s
    c_in, c_out, c_shapes, c_sems = _comm_specs(parts, N_PEERS)

    def body(*refs):
        (x_ref, lb_ref, ng_ref, lv_ref, st_ref, cur_ref, prev_ref, meta_ref, cc, sc, cp, sp, cm, sm, sink_ref,
         dy_ref, do_ref) = refs[:17]
        part_refs = refs[17:17 + n]
        dx_ref, dlb_ref, dng_ref, dsink_ref = refs[17 + n:21 + n]
        recv_refs = refs[21 + n:21 + 2 * n]
        dcarry_ref, dkv_next_ref, dkv_meta_ref = refs[21 + 2 * n:24 + 2 * n]
        starts, waits = _scatter_behind(part_refs, recv_refs, refs[24 + 2 * n:], swapped)
        step = pl.program_id(0)
        c = rev(step)

        @pl.when(step == 0)
        def _():
            dcarry_ref[...] = jnp.zeros_like(dcarry_ref)
            dkv_next_ref[...] = jnp.zeros_like(dkv_next_ref)
            dkv_meta_ref[...] = jnp.zeros_like(dkv_meta_ref)
            dlb_ref[...] = jnp.zeros_like(dlb_ref)
            dng_ref[...] = jnp.zeros_like(dng_ref)
            dsink_ref[...] = jnp.zeros_like(dsink_ref)
            for start in starts:
                start()

        fh = (lax.broadcasted_iota(jnp.int32, (BLOCK, BLOCK), 1) % HEAD_DIM) < (HEAD_DIM // 2)
        qs, kc, vc = _att_load(cur_ref, cc, sc, fh, True)
        _, kp, vp = _att_load(prev_ref, cp, sp, fh, False)
        _, km, vm = _att_load(meta_ref, cm, sm, fh, False)
        s0, s1 = _att_sinkcols(sink_ref)
        masks = _att_masks(c)
        _, att_vjp = jax.vjp(lambda *a: _att_core(*a, *masks), *qs, km, kp, kc, vm, vp, vc, s0, s1)
        dq0, dq1, dq2, dq3, dkm, dkp, dkc, dvm, dvp, dvc, ds0, ds1 = att_vjp(do_ref[...])
        att0 = 4 * HG_W
        for j, dq in enumerate((dq0, dq1, dq2, dq3)):
            dx_ref[:, att0 + j * BLOCK:att0 + (j + 1) * BLOCK] = _rope_t(dq, cc[...], sc[...], fh).astype(BF16)
        dkv_meta_ref[:, :BLOCK] += _rope_t(dkm, cm[...], sm[...], fh)
        dkv_meta_ref[:, BLOCK:] += dvm
        last = jnp.where(c == 0, 1.0, 0.0)
        dk = _rope_t(dkc, cc[...], sc[...], fh) + dkv_next_ref[:, :BLOCK] + last * dkv_meta_ref[:, :BLOCK]
        dv = dvc + dkv_next_ref[:, BLOCK:] + last * dkv_meta_ref[:, BLOCK:]
        dx_ref[:, att0 + ATT_QW:att0 + ATT_QW + ATT_KVW] = dk.astype(BF16)
        dx_ref[:, att0 + ATT_QW + ATT_KVW:] = dv.astype(BF16)
        dkv_next_ref[:, :BLOCK] = _rope_t(dkp, cp[...], sp[...], fh)
        dkv_next_ref[:, BLOCK:] = dvp
        sink_rows = []
        for dsg in (ds0, ds1):
            for j in range(4):
                tot = jnp.sum(dsg[j * BLOCK:(j + 1) * BLOCK], axis=0, keepdims=True)
                sink_rows.append(jnp.broadcast_to(tot, (1, BLOCK)))
        dsink_ref[...] += jnp.concatenate(sink_rows, axis=0)

        valid = (c * BLOCK + lax.broadcasted_iota(jnp.int32, (BLOCK, 1), 0)) >= PAD
        (logf, k), gates_vjp = jax.vjp(lambda hf, a0, a1: _hgrn_gates(hf, a0, a1, valid),
                                       x_ref[:, HG_W:2 * HG_W], lb_ref[0:1, :], lb_ref[1:2, :])
        lvv = lv_ref[...]
        e = _split_dot(lvv, logf, "nn")
        dng = jnp.zeros((1, BLOCK), F32)
        dk, dseg = [], []
        for h in range(HG_HEADS):
            sl = lambda part: x_ref[:, part * HG_W + h * BLOCK: part * HG_W + (h + 1) * BLOCK]
            hs = slice(h * BLOCK, (h + 1) * BLOCK)
            _, vjp = jax.vjp(_hgrn_head, sl(0), k[:, hs], sl(2), sl(3), ng_ref[...], st_ref[0, h], *_seg_blocks(e, h))
            dhq, dkh, dhi, dhg, dngh, dst, *dsegh = vjp((dy_ref[:, hs], dcarry_ref[h]))
            for part, val in ((0, dhq), (2, dhi), (3, dhg)):
                dx_ref[:, part * HG_W + h * BLOCK: part * HG_W + (h + 1) * BLOCK] = val.astype(BF16)
            dk.append(dkh)
            dseg.append(jnp.concatenate(dsegh, axis=0))
            dng = dng + dngh
            dcarry_ref[h] = dst
        dlogf = _split_dot(lvv, jnp.concatenate(dseg, axis=1), "tn")
        dhf, da0, da1 = gates_vjp((dlogf, jnp.concatenate(dk, axis=1)))
        dx_ref[:, HG_W:2 * HG_W] = dhf.astype(BF16)
        dlb_ref[0:1, :] += da0
        dlb_ref[1:2, :] += da1
        dng_ref[...] += dng

        @pl.when(step == nb - 1)
        def _():
            for wait in waits:
                wait()

    const = lambda shape: pl.BlockSpec(shape, lambda s: (0,) * len(shape))
    return pl.pallas_call(
        body, name="mixers_bwd", grid=(nb,),
        in_specs=[pl.BlockSpec((BLOCK, 4 * HG_W), lambda s: (rev(s), 0)), const((2, HG_W)), const((1, BLOCK)),
                  const(lv.shape), pl.BlockSpec((1, HG_HEADS, BLOCK, BLOCK), lambda s: (rev(s), 0, 0, 0))]
        + _att_specs(rev)
        + [pl.BlockSpec((BLOCK, HG_W), lambda s: (rev(s), 0)), pl.BlockSpec((BLOCK, ATT_QW), lambda s: (rev(s), 0))]
        + c_in,
        out_specs=[pl.BlockSpec((BLOCK, MIX_W), lambda s: (rev(s), 0)), const((2, HG_W)), const((1, BLOCK)),
                   const((ATT_HEADS, BLOCK))] + c_out,
        out_shape=[jax.ShapeDtypeStruct((p, MIX_W), BF16), jax.ShapeDtypeStruct((2, HG_W), F32),
                   jax.ShapeDtypeStruct((1, BLOCK), F32), jax.ShapeDtypeStruct((ATT_HEADS, BLOCK), F32)] + c_shapes,
        scratch_shapes=[pltpu.VMEM((HG_HEADS, BLOCK, BLOCK), F32), pltpu.VMEM((BLOCK, kvw), F32),
                        pltpu.VMEM((BLOCK, kvw), F32)] + c_sems,
        compiler_params=_cparams(("arbitrary",)),
    )(proj_hg, lbounds, norm_g, lv, states, proj_att, proj_att, proj_att, cos, sin, cos, sin, cos, sin, sinks,
      dyh, doa, *parts)


def _d_h0_proj(dmix, dgates, w_mix, w_gates, parts, swapped):
    p = dmix.shape[0]
    tm = _row_tile(p, 640)
    nm = p // tm
    n = len(parts)
    c_in, c_out, c_shapes, c_sems = _comm_specs(parts, N_PEERS)

    def body(*refs):
        a_ref, g_ref, wa_ref, wg_ref = refs[:4]
        o_ref = refs[4 + n]
        starts, waits = _scatter_behind(refs[4:4 + n], refs[5 + n:5 + 2 * n], refs[5 + 2 * n:], swapped)
        i = pl.program_id(0)

        @pl.when(i == 0)
        def _():
            for start in starts:
                start()

        o_ref[...] = _dot(a_ref[...], wa_ref[...], "nt") + _dot(g_ref[...], wg_ref[...], "nt")

        @pl.when(i == nm - 1)
        def _():
            for wait in waits:
                wait()

    row = lambda w: pl.BlockSpec((tm, w), lambda i: (i, 0))
    const = lambda a: pl.BlockSpec(a.shape, lambda i: (0, 0))
    return pl.pallas_call(
        body, name="d_h0_proj", grid=(nm,),
        in_specs=[row(dmix.shape[1]), row(dgates.shape[1]), const(w_mix), const(w_gates)] + c_in,
        out_specs=[row(D_MODEL)] + c_out,
        out_shape=[jax.ShapeDtypeStruct((p, D_MODEL), F32)] + c_shapes,
        scratch_shapes=c_sems, compiler_params=_cparams(("arbitrary",)),
    )(dmix, dgates, w_mix, w_gates, *parts)


_LATE = ("w_branch_hg", "w_branch_attn", "w_out", "w_ffn_in", "w_ffn_out")
_COLUMN_SHARDED = ("meta_tokens", "w_in", "w_branch_hg", "w_branch_attn", "w_ffn_in")
_SWAPPED = ("w_ffn_in",)


def _whole(name, gathered):
    _, r, c = gathered.shape
    if name in _COLUMN_SHARDED:
        return jnp.transpose(gathered, (1, 0, 2)).reshape(r, N_DEV * c)
    return gathered.reshape(N_DEV * r, c)


def _slots(name, whole):
    r, c = whole.shape
    if name in _COLUMN_SHARDED:
        return jnp.transpose(whole.reshape(r, N_DEV, c // N_DEV), (1, 0, 2))
    return whole.reshape(N_DEV, r // N_DEV, c)


def _device_step(x, target, meta, ln_emb_g, ln_emb_b, w_in, lbounds, norm_g, sinks, late_shards,
                 ln1_g, ln1_b, ln2_g, ln2_b):
    s = x.shape[0]
    p = s + BLOCK
    lead = jnp.concatenate([jnp.zeros((PAD, D_MODEL), F32), meta], axis=0)
    tm = _row_tile(p, 640)
    lv = _level_stack()
    cos, sin = _rope_tables(p)
    hg_end = 4 * HG_W
    mm = functools.partial(_tiled_matmul, tm=tm)
    swapped = [n in _SWAPPED for n in _LATE]

    h0, h0b, xhat0, rstd0 = _embed_ln(x, lead, ln_emb_g, ln_emb_b)
    proj_hg = mm(h0b, w_in[:, :hg_end], "nn", tn=hg_end, tc=D_MODEL, out_dtype=F32, name="proj_hg")
    proj_att = mm(h0b, w_in[:, hg_end:MIX_W], "nn", tn=MIX_W - hg_end, tc=D_MODEL, out_dtype=F32, name="proj_att")
    gates = mm(h0b, w_in[:, MIX_W:], "nn", tn=2 * D_MODEL, tc=D_MODEL, out_dtype=F32, name="proj_gates")
    yh, states, oa, *gathered = _mixers_fwd(proj_hg, proj_att, lbounds, norm_g, lv, cos, sin, sinks, late_shards, swapped)
    w_bh, w_ba, w_out, w_fi, w_fo = [_whole(n, g) for n, g in zip(_LATE, gathered)]
    mixin, h1, h1b, xhat1, rstd1 = _mix_out_ln1(yh, oa, gates, h0, w_bh, w_ba, w_out, ln1_g, ln1_b)
    au, sw = _ffn_in_swiglu(h1b, w_fi)
    dr2, loss_part, dg2, db2 = _ffn_out_loss(sw, w_fo, h1, ln2_g, ln2_b, target)

    mtn = functools.partial(_tiled_matmul_tn, tm=_row_tile(p, 1664), out_dtype=BF16)
    d_wfo = mtn(sw, dr2, tk=FF_T, tn=D_MODEL, name="grad_w_ffn_out")
    dau = _d_ffn_hidden(dr2, w_fo, au)
    d_wfi = mtn(h1b, dau, tk=D_MODEL, tn=FF_T, name="grad_w_ffn_in")
    dh1_ffn = mm(dau, w_fi, "nt", tn=D_MODEL, tc=D_FF, out_dtype=F32, name="d_h1_ffn")
    dr1, dy_hg, dy_att, dgates, dyh, doa, dg1, db1 = _ln1_mix_bwd(
        dr2, dh1_ffn, xhat1, rstd1, ln1_g, yh, oa, gates, w_bh, w_ba, w_out)
    d_wout = mtn(mixin, dr1, tk=D_MODEL, tn=D_MODEL, name="grad_w_out")
    d_wbh = mtn(yh, dy_hg, tk=HG_W, tn=D_MODEL, name="grad_w_branch_hg")
    d_wba = mtn(oa, dy_att, tk=ATT_QW, tn=D_MODEL, name="grad_w_branch_attn")
    late_parts = [_slots(n, g) for n, g in zip(_LATE, (d_wbh, d_wba, d_wout, d_wfi, d_wfo))]
    dmix, d_lb, d_ng, d_sink, *late_recv = _mixers_bwd(
        proj_hg, proj_att, lbounds, norm_g, lv, states, cos, sin, sinks, dyh, doa, late_parts, swapped)
    d_win = jnp.concatenate([mtn(h0b, dmix, tk=D_MODEL, tn=MIX_W // 2, name="grad_w_in_mixers"),
                             mtn(h0b, dgates, tk=D_MODEL, tn=D_MODEL, name="grad_w_in_gates")], axis=1)
    win_parts = _slots("w_in", d_win)
    dh0_proj, win_recv = _d_h0_proj(dmix, dgates, w_in[:, :MIX_W], w_in[:, MIX_W:], [win_parts], [False])
    dxin, dg0, db0 = _ln_bwd_call(dr1, dh0_proj, ALPHA, xhat0, rstd0, ln_emb_g, "embed_ln_bwd", mask_from=PAD)

    small = dict(ln_emb_g=dg0, ln_emb_b=db0, hg_lower_bounds=d_lb, hg_norm_g=d_ng, attn_sinks=d_sink[:, 0],
                 ln1_g=dg1, ln1_b=db1, ln2_g=dg2, ln2_b=db2)
    big = dict(zip(_LATE, zip(late_parts, late_recv)))
    big["w_in"] = (win_parts, win_recv)
    return loss_part, dxin[BLOCK:], small, dxin[PAD:BLOCK], big


def _all_gather(arrs, dtypes, name):
    n = len(arrs)

    def body(*refs):
        ins, outs, stages = refs[:n], refs[n:2 * n], refs[2 * n:3 * n]
        send_sems, recv_sems, local_sems = refs[3 * n:]
        x, y, c = _place()
        sibling = (x, y, 1 - c)
        chips = [(1 - x, y), (x, 1 - y), (1 - x, 1 - y)]
        slot = lambda px, py, pc: 4 * px + 2 * py + pc

        def copy(w, k, block, to, from_stage=False):
            return pltpu.make_async_remote_copy(
                src_ref=stages[w] if from_stage else outs[w].at[slot(*block)], dst_ref=outs[w].at[slot(*block)],
                send_sem=send_sems.at[w, k], recv_sem=recv_sems.at[w, k], device_id=to, device_id_type=MESH)

        mine, first, passed = [], [], []
        for w in range(n):
            stages[w][...] = ins[w][...].astype(dtypes[w])
            mine.append(pltpu.make_async_copy(stages[w], outs[w].at[slot(x, y, c)], local_sems.at[w]))
            mine[-1].start()
        for w in range(n):
            first.append(copy(w, 0, (x, y, c), sibling, from_stage=True))
            first += [copy(w, 1 + j, (x, y, c), (*chip, c), from_stage=True) for j, chip in enumerate(chips)]
        for cp in first:
            cp.start()
        for j, chip in enumerate(chips):
            for w in range(n):
                copy(w, 1 + j, (*chip, c), (x, y, c)).wait_recv()
                passed.append(copy(w, 4 + j, (*chip, c), sibling))
                passed[-1].start()
        for w in range(n):
            copy(w, 0, sibling, (x, y, c)).wait_recv()
            for j, chip in enumerate(chips):
                copy(w, 4 + j, (*chip, 1 - c), (x, y, c)).wait_recv()
        for cp in first + passed:
            cp.wait_send()
        for cp in mine:
            cp.wait()

    return pl.pallas_call(
        body, name=name,
        in_specs=[pl.BlockSpec(memory_space=pltpu.VMEM)] * n,
        out_specs=[pl.BlockSpec(memory_space=pl.ANY)] * n,
        out_shape=[jax.ShapeDtypeStruct((N_DEV,) + a.shape, dt) for a, dt in zip(arrs, dtypes)],
        scratch_shapes=[pltpu.VMEM(a.shape, dt) for a, dt in zip(arrs, dtypes)]
        + [pltpu.SemaphoreType.DMA((n, 7)), pltpu.SemaphoreType.DMA((n, 7)), pltpu.SemaphoreType.DMA((n,))],
        compiler_params=pltpu.CompilerParams(vmem_limit_bytes=VMEM_LIMIT_BYTES),
    )(*arrs)


def _cast_shards(arrs):
    def body(*refs):
        for src, dst in zip(refs[:len(arrs)], refs[len(arrs):]):
            dst[...] = src[...].astype(BF16)

    return pl.pallas_call(body, name="cast_shards", out_shape=[jax.ShapeDtypeStruct(a.shape, BF16) for a in arrs],
                          compiler_params=pltpu.CompilerParams(vmem_limit_bytes=VMEM_LIMIT_BYTES))(*arrs)


def _shard_rows(rows):
    return rows if rows <= 512 else 256


def _adamw_math(w, g, m, v):
    m = ADAM_B1 * m + (1.0 - ADAM_B1) * g
    v = ADAM_B2 * v + (1.0 - ADAM_B2) * (g * g)
    m_hat = m / (1.0 - ADAM_B1 ** ADAM_STEP)
    v_hat = v / (1.0 - ADAM_B2 ** ADAM_STEP)
    delta = -ADAM_LR * (m_hat / (jnp.sqrt(v_hat) + ADAM_EPS) + ADAM_WD * w)
    return delta, m, v


def _reduce_adamw(parts, recv, own_slot, w, m, v, name):
    r, cdim = w.shape
    tr = _shard_rows(r)

    def body(idx_ref, p_ref, r_ref, w_ref, m_ref, v_ref, g_out, d_out, m_out, v_out):
        g = p_ref[0].astype(F32)
        for j in range(N_PEERS):
            g = g + r_ref[j].astype(F32)
        d, mn, vn = _adamw_math(w_ref[...], g, m_ref[...], v_ref[...])
        g_out[...] = g
        d_out[...] = d
        m_out[...] = mn
        v_out[...] = vn

    flat = pl.BlockSpec((tr, cdim), lambda i, idx_ref: (i, 0))
    return pl.pallas_call(
        body, name=name,
        grid_spec=pltpu.PrefetchScalarGridSpec(
            num_scalar_prefetch=1, grid=(r // tr,),
            in_specs=[pl.BlockSpec((1, tr, cdim), lambda i, idx_ref: (idx_ref[0], i, 0)),
                      pl.BlockSpec((N_PEERS, tr, cdim), lambda i, idx_ref: (0, i, 0)), flat, flat, flat],
            out_specs=[flat] * 4),
        out_shape=[jax.ShapeDtypeStruct((r, cdim), F32)] * 4,
        compiler_params=_cparams(("arbitrary",)),
    )(own_slot, parts, recv, w, m, v)


def _adamw_plain(w, g, m, v, name):
    def body(w_ref, g_ref, m_ref, v_ref, d_out, m_out, v_out):
        d_out[...], m_out[...], v_out[...] = _adamw_math(w_ref[...], g_ref[...], m_ref[...], v_ref[...])

    return pl.pallas_call(body, name=name, out_shape=[jax.ShapeDtypeStruct(w.shape, F32)] * 3)(w, g, m, v)


_SMALL_LAYOUT = (("ln_emb_g", 8), ("ln_emb_b", 8), ("hg_lower_bounds", 8), ("hg_norm_g", 1), ("attn_sinks", 1),
                 ("ln1_g", 8), ("ln1_b", 8), ("ln2_g", 8), ("ln2_b", 8))
_META_ROW = sum(r for _, r in _SMALL_LAYOUT)
_META_ROWS = N_META * D_MODEL // BLOCK
_LOSS_ROW = _META_ROW + _META_ROWS
SMALL_ROWS = 192


def _pack_small(vals, meta=None, loss_row=None):
    rows = []
    for name, nrows in _SMALL_LAYOUT:
        flat = vals[name].reshape(-1).astype(F32)
        flat = jnp.pad(flat, (0, nrows * BLOCK - flat.shape[0]))
        rows.append(flat.reshape(nrows, BLOCK))
    rows.append(jnp.zeros((_META_ROWS, BLOCK), F32) if meta is None else meta.reshape(_META_ROWS, BLOCK))
    rows.append(jnp.zeros((1, BLOCK), F32) if loss_row is None else loss_row)
    packed = jnp.concatenate(rows, axis=0)
    return jnp.pad(packed, ((0, SMALL_ROWS - packed.shape[0]), (0, 0)))


def _unpack_small(packed, shapes):
    out, row = {}, 0
    for name, nrows in _SMALL_LAYOUT:
        size = math.prod(shapes[name])
        out[name] = packed[row:row + nrows].reshape(-1)[:size].reshape(shapes[name])
        row += nrows
    return out


def _small_reduce_adamw(gathered, w, m, v):
    def body(g_ref, w_ref, m_ref, v_ref, g_out, d_out, m_out, v_out, loss_out):
        g = g_ref[0]
        for s in range(1, N_DEV):
            g = g + g_ref[s]
        d, mn, vn = _adamw_math(w_ref[...], g, m_ref[...], v_ref[...])
        g_out[...] = g
        d_out[...] = d
        m_out[...] = mn
        v_out[...] = vn
        loss_out[...] = jnp.broadcast_to(jnp.sum(g_ref[:, _LOSS_ROW, :]), (1, BLOCK))

    shp = jax.ShapeDtypeStruct((SMALL_ROWS, BLOCK), F32)
    return pl.pallas_call(body, name="small_reduce_adamw",
                          out_shape=[shp] * 4 + [jax.ShapeDtypeStruct((1, BLOCK), F32)])(gathered, w, m, v)


_WEIGHTS = ("meta_tokens", "ln_emb_g", "ln_emb_b", "w_in", "hg_lower_bounds", "hg_norm_g", "attn_sinks",
            "w_branch_hg", "w_branch_attn", "w_out", "ln1_g", "ln1_b", "w_ffn_in", "w_ffn_out", "ln2_g", "ln2_b")


def kernel(x, meta_tokens, ln_emb_g, ln_emb_b, w_in, hg_lower_bounds, hg_norm_g, attn_sinks, w_branch_hg, w_branch_attn, w_out, ln1_g, ln1_b, w_ffn_in, w_ffn_out, ln2_g, ln2_b, loss_target, m_meta_tokens, m_ln_emb_g, m_ln_emb_b, m_w_in, m_hg_lower_bounds, m_hg_norm_g, m_attn_sinks, m_w_branch_hg, m_w_branch_attn, m_w_out, m_ln1_g, m_ln1_b, m_w_ffn_in, m_w_ffn_out, m_ln2_g, m_ln2_b, v_meta_tokens, v_ln_emb_g, v_ln_emb_b, v_w_in, v_hg_lower_bounds, v_hg_norm_g, v_attn_sinks, v_w_branch_hg, v_w_branch_attn, v_w_out, v_ln1_g, v_ln1_b, v_w_ffn_in, v_w_ffn_out, v_ln2_g, v_ln2_b):
    given = dict(locals())
    weights = {n: given[n] for n in _WEIGHTS}
    mom1 = {n: given["m_" + n] for n in _WEIGHTS}
    mom2 = {n: given["v_" + n] for n in _WEIGHTS}
    shard2d = lambda a: a.reshape(a.shape[-2:])

    g_meta, g_win = _all_gather([meta_tokens, shard2d(w_in)], [F32, BF16], "gather_first")
    late_shards = _cast_shards([shard2d(weights[n]) for n in _LATE])

    loss_part, grad_x, small_grads, meta_grad, big = _device_step(
        x[0], loss_target[0], _whole("meta_tokens", g_meta), ln_emb_g.reshape(1, -1), ln_emb_b.reshape(1, -1),
        _whole("w_in", g_win), hg_lower_bounds, hg_norm_g, attn_sinks, late_shards, ln1_g, ln1_b, ln2_g, ln2_b)

    place = _place()
    out = {}
    for n, (parts, recv) in big.items():
        own = _slot(place, n in _SWAPPED).astype(jnp.int32).reshape(1)
        res = _reduce_adamw(parts, recv, own, shard2d(weights[n]), shard2d(mom1[n]), shard2d(mom2[n]), "adamw_" + n)
        out[n] = [r.reshape(weights[n].shape) for r in res]

    small_names = [n for n, _ in _SMALL_LAYOUT]
    packed = _pack_small(small_grads, meta_grad, loss_part)
    all_small, = _all_gather([packed], [F32], "gather_small")
    res = _small_reduce_adamw(all_small, _pack_small(weights), _pack_small(mom1), _pack_small(mom2))
    shapes = {n: weights[n].shape for n in small_names}
    unpacked = [_unpack_small(r, shapes) for r in res[:4]]
    for n in small_names:
        out[n] = [u[n] for u in unpacked]
    loss = res[4][0, 0]
    meta_whole = res[0][_META_ROW:_META_ROW + _META_ROWS].reshape(N_META, N_DEV, D_MODEL // N_DEV)
    g_meta_mine = lax.dynamic_index_in_dim(meta_whole, _slot(place, False), axis=1, keepdims=False)
    out["meta_tokens"] = [g_meta_mine, *_adamw_plain(meta_tokens, g_meta_mine, m_meta_tokens, v_meta_tokens,
                                                     "adamw_meta")]

    return (loss, grad_x[None], *[out[n][0] for n in _WEIGHTS], *[out[n][1] for n in _WEIGHTS],
            *[out[n][2] for n in _WEIGHTS], *[out[n][3] for n in _WEIGHTS])
```

```python
import functools
import math

import numpy as np
import jax
import jax.numpy as jnp
from jax import lax
from jax.experimental import pallas as pl
from jax.experimental.pallas import tpu as pltpu

F32 = jnp.float32
BF16 = jnp.bfloat16

D_MODEL = 1024
N_META = 16
BLOCK = 128
PAD = BLOCK - N_META
HG_HEADS = 4
HG_W = 512
ATT_HEADS = 8
HEAD_DIM = 64
ATT_QW = 512
ATT_KVW = 128
D_FF = 2816
EPS = 1e-5
ALPHA = 2.0 ** 0.25
ROPE_THETA = 10000.0
N_DEV = 8

ADAM_LR = 0.001
ADAM_B1 = 0.9
ADAM_B2 = 0.999
ADAM_EPS = 1e-08
ADAM_WD = 0.01
ADAM_STEP = 10

VMEM_LIMIT_BYTES = 56 * 1024 * 1024
MESH = pl.DeviceIdType.MESH

_LEVELS = (64, 32, 16, 8, 4, 2, 1)


def _cparams(sem):
    return pltpu.CompilerParams(dimension_semantics=sem, vmem_limit_bytes=VMEM_LIMIT_BYTES)


def _row_tile(rows, target):
    nb = rows // BLOCK
    best = 1
    for d in range(1, nb + 1):
        if nb % d == 0 and d * BLOCK <= target:
            best = d
    return best * BLOCK


_DN = {"nn": (((1,), (0,)), ((), ())), "nt": (((1,), (1,)), ((), ())), "tn": (((0,), (0,)), ((), ()))}


def _dot(a, b, form):
    return lax.dot_general(a.astype(BF16), b.astype(BF16), _DN[form], preferred_element_type=F32)


@functools.partial(jax.custom_vjp, nondiff_argnums=(2,))
def _mm(a, b, form):
    return _dot(a, b, form)


def _mm_fwd(a, b, form):
    a, b = a.astype(BF16), b.astype(BF16)
    return _dot(a, b, form), (a, b)


def _mm_bwd(form, res, g):
    a, b = res
    if form == "nn":
        return _dot(g, b, "nt"), _dot(a, g, "tn")
    if form == "nt":
        return _dot(g, b, "nn"), _dot(g, a, "tn")
    return _dot(b, g, "nt"), _dot(a, g, "nn")


_mm.defvjp(_mm_fwd, _mm_bwd)


def _split_dot(lv, x, form):
    hi = x.astype(BF16)
    lo = (x - hi.astype(F32)).astype(BF16)
    return (lax.dot_general(lv, hi, _DN[form], preferred_element_type=F32)
            + lax.dot_general(lv, lo, _DN[form], preferred_element_type=F32))


@jax.custom_vjp
def _swap_halves(x):
    return pltpu.roll(x, 64, 1)


_swap_halves.defvjp(lambda x: (pltpu.roll(x, 64, 1), None), lambda _, g: (pltpu.roll(g, 64, 1),))


def _tiled_matmul(a, b, form, *, tm, tn, tc, out_dtype, name):
    m, c = a.shape
    n = b.shape[1] if form == "nn" else b.shape[0]
    assert m % tm == 0 and n % tn == 0 and c % tc == 0, (name, a.shape, b.shape, tm, tn, tc)
    nc = c // tc

    def body(a_ref, b_ref, o_ref, *scratch):
        part = _dot(a_ref[...], b_ref[...], form)
        if nc == 1:
            o_ref[...] = part.astype(out_dtype)
            return
        acc_ref, = scratch
        ci = pl.program_id(2)

        @pl.when(ci == 0)
        def _():
            acc_ref[...] = part

        @pl.when(ci > 0)
        def _():
            acc_ref[...] += part

        @pl.when(ci == nc - 1)
        def _():
            o_ref[...] = acc_ref[...].astype(out_dtype)

    b_spec = (pl.BlockSpec((tc, tn), lambda j, i, k: (k, j)) if form == "nn"
              else pl.BlockSpec((tn, tc), lambda j, i, k: (j, k)))
    return pl.pallas_call(
        body, name=name, grid=(n // tn, m // tm, nc),
        in_specs=[pl.BlockSpec((tm, tc), lambda j, i, k: (i, k)), b_spec],
        out_specs=pl.BlockSpec((tm, tn), lambda j, i, k: (i, j)),
        out_shape=jax.ShapeDtypeStruct((m, n), out_dtype),
        scratch_shapes=[] if nc == 1 else [pltpu.VMEM((tm, tn), F32)],
        compiler_params=_cparams(("arbitrary", "arbitrary", "arbitrary")),
    )(a, b)


def _tiled_matmul_tn(a, b, *, tm, tk, tn, out_dtype, name):
    m, k = a.shape
    n = b.shape[1]
    assert m % tm == 0 and k % tk == 0 and n % tn == 0, (name, a.shape, b.shape, tm, tk, tn)
    nm = m // tm

    def body(a_ref, b_ref, o_ref, acc_ref):
        part = _dot(a_ref[...], b_ref[...], "tn")
        mi = pl.program_id(2)

        @pl.when(mi == 0)
        def _():
            acc_ref[...] = part

        @pl.when(mi > 0)
        def _():
            acc_ref[...] += part

        @pl.when(mi == nm - 1)
        def _():
            o_ref[...] = acc_ref[...].astype(out_dtype)

    return pl.pallas_call(
        body, name=name, grid=(k // tk, n // tn, nm),
        in_specs=[pl.BlockSpec((tm, tk), lambda kk, j, i: (i, kk)), pl.BlockSpec((tm, tn), lambda kk, j, i: (i, j))],
        out_specs=pl.BlockSpec((tk, tn), lambda kk, j, i: (kk, j)),
        out_shape=jax.ShapeDtypeStruct((k, n), out_dtype),
        scratch_shapes=[pltpu.VMEM((tk, tn), F32)],
        compiler_params=_cparams(("arbitrary", "arbitrary", "arbitrary")),
    )(a, b)


def _ln_stats(r):
    mu = jnp.mean(r, axis=-1, keepdims=True)
    xc = r - mu
    var = jnp.mean(xc * xc, axis=-1, keepdims=True)
    rstd = lax.rsqrt(var + EPS)
    return xc * rstd, rstd


def _ln_bwd(dy, xhat, rstd, g):
    dxhat = dy * g
    m1 = jnp.mean(dxhat, axis=-1, keepdims=True)
    m2 = jnp.mean(dxhat * xhat, axis=-1, keepdims=True)
    dr = rstd * (dxhat - m1 - xhat * m2)
    return dr, jnp.sum(dy * xhat, axis=0, keepdims=True), jnp.sum(dy, axis=0, keepdims=True)


N_SEG = 3 + len(_LEVELS)


def _level_stack():
    t = np.arange(BLOCK)[:, None]
    r = np.arange(BLOCK)[None, :]
    mats = [r <= t, r > t, np.ones((BLOCK, BLOCK), bool)]
    for h in _LEVELS:
        same = (t // (2 * h)) == (r // (2 * h))
        up_t, up_r = (t % (2 * h)) >= h, (r % (2 * h)) >= h
        mats.append(same & ((up_t & up_r & (r <= t)) | (~up_t & ~up_r & (r > t))))
    return jnp.asarray(np.concatenate(mats, axis=0).astype(np.float32), dtype=BF16)


def _hgrn_gates(hf, a0, a1, valid):
    lb = jax.nn.sigmoid(a0 - a1)
    fg = lb + (1.0 - lb) * jax.nn.sigmoid(hf)
    return jnp.where(valid, jnp.log(fg), 0.0), jnp.where(valid, 1.0 - fg, 0.0)


def _hgrn_head(hq, k, v, hg, ng, st_in, *seg):
    q = jax.nn.silu(hq)
    rows = lax.broadcasted_iota(jnp.int32, (BLOCK, BLOCK), 0)
    cols = lax.broadcasted_iota(jnp.int32, (BLOCK, BLOCK), 1)
    o = _mm(q * jnp.exp(seg[0]), st_in, "nt")
    a = jnp.where(rows == cols, jnp.sum(q * k, axis=-1, keepdims=True), 0.0)
    for li, h in enumerate(_LEVELS):
        decay = jnp.exp(seg[3 + li])
        pair = ((rows // (2 * h)) == (cols // (2 * h))) & ((rows % (2 * h)) >= h) & ((cols % (2 * h)) < h)
        a = a + jnp.where(pair, _mm(q * decay, k * decay, "nt"), 0.0)
    o = o + _mm(a, v, "nn")
    st_out = st_in * jnp.exp(seg[2]) + _mm(v, k * jnp.exp(seg[1]), "tn")
    on = o * lax.rsqrt(jnp.mean(o * o, axis=-1, keepdims=True) + EPS) * ng
    return on * jax.nn.silu(hg), st_out


def _seg_blocks(e, h):
    return [e[i * BLOCK:(i + 1) * BLOCK, h * BLOCK:(h + 1) * BLOCK] for i in range(N_SEG)]


def _rope(x, cos, sin, first_half):
    partner = jnp.where(first_half, -pltpu.roll(x, 96, 1), pltpu.roll(x, 32, 1))
    return x * cos + partner * sin


def _rope_t(g, cos, sin, first_half):
    u = g * sin
    partner = jnp.where(first_half, pltpu.roll(u, 96, 1), -pltpu.roll(u, 32, 1))
    return g * cos + partner


def _att_core(q0, q1, q2, q3, km, kp, kc, vm, vp, vc, sinkrow0, sinkrow1, own_side, ok_band, ok_meta):
    low = lambda x: lax.broadcasted_iota(jnp.int32, x.shape, 1) < HEAD_DIM
    scale = HEAD_DIM ** -0.5
    neg = jnp.finfo(F32).min
    wide = lambda m: jnp.concatenate([m] * 4, axis=1)
    own4, band4, meta4 = wide(own_side), wide(ok_band), wide(ok_meta)
    outs = []
    for g, (qa, qb, sinkrow) in enumerate(((q0, q1, sinkrow0), (q2, q3, sinkrow1))):
        def both(x, g=g):
            sw = _swap_halves(x)
            return jnp.where(low(x), x, sw) if g == 0 else jnp.where(low(x), sw, x)
        q4 = jnp.concatenate([jnp.where(low(qa), qa, 0.0), jnp.where(low(qa), 0.0, qa),
                              jnp.where(low(qb), qb, 0.0), jnp.where(low(qb), 0.0, qb)], axis=0)
        s = jnp.where(own4, _mm(both(kc), q4, "nt"), _mm(both(kp), q4, "nt"))
        s = jnp.where(band4, s * scale, neg)
        sm = jnp.where(meta4, _mm(both(km), q4, "nt") * scale, neg)
        mx = jnp.maximum(jnp.maximum(jnp.max(s, axis=0, keepdims=True), jnp.max(sm, axis=0, keepdims=True)), sinkrow)
        mx = lax.stop_gradient(mx)
        p, pm = jnp.exp(s - mx), jnp.exp(sm - mx)
        inv = 1.0 / (jnp.sum(p, axis=0, keepdims=True) + jnp.sum(pm, axis=0, keepdims=True) + jnp.exp(sinkrow - mx))
        p = p * inv
        o4 = (_mm(jnp.where(own4, p, 0.0), both(vc), "tn") + _mm(jnp.where(own4, 0.0, p), both(vp), "tn")
              + _mm(pm * inv, both(vm), "tn"))
        for j in range(2):
            upper = o4[(2 * j) * BLOCK:(2 * j + 1) * BLOCK]
            outs.append(jnp.where(low(upper), upper, o4[(2 * j + 1) * BLOCK:(2 * j + 2) * BLOCK]))
    return jnp.concatenate(outs, axis=1)


def _att_masks(blk_idx):
    kidx = lax.broadcasted_iota(jnp.int32, (BLOCK, BLOCK), 0)
    qrow = lax.broadcasted_iota(jnp.int32, (BLOCK, BLOCK), 1)
    own_side = kidx <= qrow
    pos_own = blk_idx * BLOCK + kidx - PAD
    ok_band = (own_side & (pos_own >= N_META)) | (~own_side & (pos_own - BLOCK >= N_META) & (blk_idx >= 1))
    qpos = blk_idx * BLOCK + lax.broadcasted_iota(jnp.int32, (N_META, BLOCK), 1) - PAD
    ok_meta = lax.broadcasted_iota(jnp.int32, (N_META, BLOCK), 0) <= qpos
    return own_side, ok_band, ok_meta


def _token_streams(tr):
    k = tr // BLOCK
    return [pl.BlockSpec((BLOCK, D_MODEL), lambda i, j=j: (jnp.maximum(k * i - 1 + j, 0), 0)) for j in range(k)]


def _embed_ln(x, lead, g0, b0):
    p = x.shape[0] + BLOCK
    tr = _row_tile(p, 640)
    k = tr // BLOCK

    def body(*refs):
        lead_ref, g_ref, b_ref, h_ref, hb_ref, xh_ref, rs_ref = refs[k:]
        first = jnp.where(pl.program_id(0) == 0, lead_ref[...], refs[0][...])
        xhat, rstd = _ln_stats(jnp.concatenate([first] + [r[...] for r in refs[1:k]], axis=0))
        row = pl.program_id(0) * tr + lax.broadcasted_iota(jnp.int32, (tr, 1), 0)
        h = jnp.where(row >= PAD, xhat * g_ref[...] + b_ref[...], 0.0)
        h_ref[...] = h
        hb_ref[...] = h.astype(BF16)
        xh_ref[...] = xhat
        rs_ref[...] = rstd

    vec = pl.BlockSpec((1, D_MODEL), lambda i: (0, 0))
    rowsp = pl.BlockSpec((tr, D_MODEL), lambda i: (i, 0))
    return pl.pallas_call(
        body, name="embed_ln", grid=(p // tr,),
        in_specs=_token_streams(tr) + [pl.BlockSpec((BLOCK, D_MODEL), lambda i: (0, 0)), vec, vec],
        out_specs=[rowsp, rowsp, rowsp, pl.BlockSpec((tr, 1), lambda i: (i, 0))],
        out_shape=[jax.ShapeDtypeStruct((p, D_MODEL), F32), jax.ShapeDtypeStruct((p, D_MODEL), BF16),
                   jax.ShapeDtypeStruct((p, D_MODEL), F32), jax.ShapeDtypeStruct((p, 1), F32)],
        compiler_params=_cparams(("arbitrary",)),
    )(*([x] * k), lead, g0, b0)


def _rope_tables(p):
    pos = (np.arange(p, dtype=np.int32) - PAD).astype(np.float32)
    half = HEAD_DIM // 2
    inv = np.float32(ROPE_THETA) ** (-np.arange(half, dtype=np.float32) / np.float32(half))
    ang = pos[:, None] * np.tile(inv.astype(np.float32), BLOCK // half)[None, :]
    return jnp.asarray(np.cos(ang), F32), jnp.asarray(np.sin(ang), F32)


def _att_sinkrows(sink_ref):
    lanehead = lax.broadcasted_iota(jnp.int32, (1, 4 * BLOCK), 1) // BLOCK
    rows = []
    for g in range(2):
        row = jnp.zeros((1, 4 * BLOCK), F32)
        for j in range(4):
            row = jnp.where(lanehead == j, sink_ref[0, 4 * g + j], row)
        rows.append(row)
    return rows


def _first_half(rows):
    return (lax.broadcasted_iota(jnp.int32, (rows, BLOCK), 1) % HEAD_DIM) < (HEAD_DIM // 2)


def _att_load(qkv_ref, cos_ref, sin_ref, with_q):
    cos, sin, fh = cos_ref[...], sin_ref[...], _first_half(BLOCK)
    qs = [_rope(qkv_ref[:, j * BLOCK:(j + 1) * BLOCK], cos, sin, fh) for j in range(4)] if with_q else None
    k = _rope(qkv_ref[:, ATT_QW:ATT_QW + ATT_KVW], cos, sin, fh)
    v = qkv_ref[:, ATT_QW + ATT_KVW:ATT_QW + 2 * ATT_KVW]
    return qs, k, v


def _att_load_meta(qkv_ref, cos_ref, sin_ref):
    k = _rope(qkv_ref[PAD:BLOCK, ATT_QW:ATT_QW + ATT_KVW], cos_ref[PAD:BLOCK, :], sin_ref[PAD:BLOCK, :],
              _first_half(N_META))
    return k, qkv_ref[PAD:BLOCK, ATT_QW + ATT_KVW:ATT_QW + 2 * ATT_KVW]


def _att_specs(blk):
    w = ATT_QW + 2 * ATT_KVW
    cur = lambda width: pl.BlockSpec((BLOCK, width), lambda i: (blk(i), 0))
    prev = lambda width: pl.BlockSpec((BLOCK, width), lambda i: (jnp.maximum(blk(i) - 1, 0), 0))
    meta = lambda width: pl.BlockSpec((BLOCK, width), lambda i: (0, 0))
    return [cur(w), prev(w), meta(w), cur(BLOCK), cur(BLOCK), prev(BLOCK), prev(BLOCK), meta(BLOCK), meta(BLOCK),
            pl.BlockSpec(memory_space=pltpu.SMEM)]


_FLIPS = [(dx, dy, dc) for dx in (0, 1) for dy in (0, 1) for dc in (0, 1)][1:]
N_PEERS = len(_FLIPS)


def _place():
    return lax.axis_index("x"), lax.axis_index("y"), lax.axis_index("c")


def _peer(place, flip):
    return tuple(1 - p if f else p for p, f in zip(place, flip))


def _slot(place, swapped):
    x, y, c = place
    return 4 * y + 2 * x + c if swapped else 4 * x + 2 * y + c


def _comm_specs(arrs, out_lead):
    n = len(arrs)
    outs = [jax.ShapeDtypeStruct((out_lead,) + a.shape[-2:], a.dtype) for a in arrs]
    sems = [pltpu.SemaphoreType.DMA((n, N_PEERS)), pltpu.SemaphoreType.DMA((n, N_PEERS)), pltpu.SemaphoreType.DMA((n,))]
    return [pl.BlockSpec(memory_space=pl.ANY)] * n, [pl.BlockSpec(memory_space=pl.ANY)] * n, outs, sems


def _gather_behind(shard_refs, out_refs, sems, swapped):
    send_sems, recv_sems, local_sems = sems
    place = _place()
    starts, waits = [], []
    for w, (s, o) in enumerate(zip(shard_refs, out_refs)):
        mine = _slot(place, swapped[w])
        own = pltpu.make_async_copy(s, o.at[mine], local_sems.at[w])
        starts.append(own.start)
        waits.append(own.wait)
        for r, flip in enumerate(_FLIPS):
            peer = _peer(place, flip)
            kw = dict(send_sem=send_sems.at[w, r], recv_sem=recv_sems.at[w, r], device_id=peer, device_id_type=MESH)
            out_cp = pltpu.make_async_remote_copy(src_ref=s, dst_ref=o.at[mine], **kw)
            in_cp = pltpu.make_async_remote_copy(src_ref=s, dst_ref=o.at[_slot(peer, swapped[w])], **kw)
            starts.append(out_cp.start)
            waits += [in_cp.wait_recv, out_cp.wait_send]
    return starts, waits


def _scatter_behind(part_refs, recv_refs, sems, swapped):
    send_sems, recv_sems, _ = sems
    place = _place()
    starts, waits = [], []
    for w, (p, o) in enumerate(zip(part_refs, recv_refs)):
        for r, flip in enumerate(_FLIPS):
            peer = _peer(place, flip)
            cp = pltpu.make_async_remote_copy(
                src_ref=p.at[_slot(peer, swapped[w])], dst_ref=o.at[r], send_sem=send_sems.at[w, r],
                recv_sem=recv_sems.at[w, r], device_id=peer, device_id_type=MESH)
            starts.append(cp.start)
            waits += [cp.wait_recv, cp.wait_send]
    return starts, waits


def _mixers_fwd(proj_hg, proj_att, lbounds, norm_g, lv, cos, sin, sinks, shards, swapped):
    p = proj_hg.shape[0]
    nb = p // BLOCK
    n = len(shards)
    c_in, c_out, c_shapes, c_sems = _comm_specs(shards, N_DEV)

    def body(*refs):
        x_ref, lb_ref, ng_ref, lv_ref, cur_ref, prev_ref, meta_ref, cc, sc, cp, sp, cm, sm, sink_ref = refs[:14]
        shard_refs = refs[14:14 + n]
        y_ref, st_ref, o_ref = refs[14 + n:17 + n]
        out_refs = refs[17 + n:17 + 2 * n]
        carry_ref = refs[17 + 2 * n]
        starts, waits = _gather_behind(shard_refs, out_refs, refs[18 + 2 * n:], swapped)
        c = pl.program_id(0)

        @pl.when(c == 0)
        def _():
            carry_ref[...] = jnp.zeros_like(carry_ref)
            for start in starts:
                start()

        valid = (c * BLOCK + lax.broadcasted_iota(jnp.int32, (BLOCK, 1), 0)) >= PAD
        logf, k = _hgrn_gates(x_ref[:, HG_W:2 * HG_W], lb_ref[0:1, :], lb_ref[1:2, :], valid)
        e = _split_dot(lv_ref[...], logf, "nn")
        for h in range(HG_HEADS):
            sl = lambda part: x_ref[:, part * HG_W + h * BLOCK: part * HG_W + (h + 1) * BLOCK]
            hs = slice(h * BLOCK, (h + 1) * BLOCK)
            st_in = carry_ref[h]
            st_ref[0, h] = st_in
            y, st_out = _hgrn_head(sl(0), k[:, hs], sl(2), sl(3), ng_ref[...], st_in, *_seg_blocks(e, h))
            y_ref[:, hs] = y.astype(BF16)
            carry_ref[h] = st_out

        qs, kc, vc = _att_load(cur_ref, cc, sc, True)
        _, kp, vp = _att_load(prev_ref, cp, sp, False)
        km, vm = _att_load_meta(meta_ref, cm, sm)
        s0, s1 = _att_sinkrows(sink_ref)
        o_ref[...] = _att_core(*qs, km, kp, kc, vm, vp, vc, s0, s1, *_att_masks(c)).astype(BF16)

        @pl.when(c == nb - 1)
        def _():
            for wait in waits:
                wait()

    return pl.pallas_call(
        body, name="mixers_fwd", grid=(nb,),
        in_specs=[pl.BlockSpec((BLOCK, 4 * HG_W), lambda c: (c, 0)), pl.BlockSpec((2, HG_W), lambda c: (0, 0)),
                  pl.BlockSpec((1, BLOCK), lambda c: (0, 0)), pl.BlockSpec(lv.shape, lambda c: (0, 0))]
        + _att_specs(lambda c: c) + c_in,
        out_specs=[pl.BlockSpec((BLOCK, HG_W), lambda c: (c, 0)),
                   pl.BlockSpec((1, HG_HEADS, BLOCK, BLOCK), lambda c: (c, 0, 0, 0)),
                   pl.BlockSpec((BLOCK, ATT_QW), lambda c: (c, 0))] + c_out,
        out_shape=[jax.ShapeDtypeStruct((p, HG_W), BF16), jax.ShapeDtypeStruct((nb, HG_HEADS, BLOCK, BLOCK), F32),
                   jax.ShapeDtypeStruct((p, ATT_QW), BF16)] + c_shapes,
        scratch_shapes=[pltpu.VMEM((HG_HEADS, BLOCK, BLOCK), F32)] + c_sems,
        compiler_params=_cparams(("arbitrary",)),
    )(proj_hg, lbounds, norm_g, lv, proj_att, proj_att, proj_att, cos, sin, cos, sin, cos, sin, sinks, *shards)


def _tile(rows, preferred):
    return preferred if rows % preferred == 0 else _row_tile(rows, preferred)


def _branch_mix(yh, oa, gates, w_bh, w_ba):
    y_hg = _dot(yh, w_bh, "nn")
    y_att = _dot(oa, w_ba, "nn")
    s1 = jax.nn.sigmoid(gates[:, :D_MODEL])
    s2 = jax.nn.sigmoid(gates[:, D_MODEL:])
    return s1 * y_hg + s2 * y_att, y_hg, y_att, s1, s2


def _mix_out_ln1(yh, oa, gates, h0, w_bh, w_ba, w_out, g1, b1):
    p = yh.shape[0]
    tr = _tile(p, 320)

    def body(yh_ref, oa_ref, g_ref, h0_ref, wbh_ref, wba_ref, wo_ref, g1_ref, b1_ref,
             mix_ref, h1_ref, h1b_ref, xh_ref, rs_ref):
        mixin = _branch_mix(yh_ref[...], oa_ref[...], g_ref[...], wbh_ref[...], wba_ref[...])[0]
        mix_ref[...] = mixin.astype(BF16)
        xhat, rstd = _ln_stats(ALPHA * h0_ref[...] + _dot(mixin, wo_ref[...], "nn"))
        h1 = xhat * g1_ref[...] + b1_ref[...]
        h1_ref[...] = h1
        h1b_ref[...] = h1.astype(BF16)
        xh_ref[...] = xhat
        rs_ref[...] = rstd

    row = lambda w: pl.BlockSpec((tr, w), lambda i: (i, 0))
    const = lambda a: pl.BlockSpec(a.shape, lambda i: (0, 0))
    return pl.pallas_call(
        body, name="mix_out_ln1", grid=(p // tr,),
        in_specs=[row(HG_W), row(ATT_QW), row(2 * D_MODEL), row(D_MODEL), const(w_bh), const(w_ba), const(w_out),
                  const(g1), const(b1)],
        out_specs=[row(D_MODEL), row(D_MODEL), row(D_MODEL), row(D_MODEL), row(1)],
        out_shape=[jax.ShapeDtypeStruct((p, D_MODEL), BF16), jax.ShapeDtypeStruct((p, D_MODEL), F32),
                   jax.ShapeDtypeStruct((p, D_MODEL), BF16), jax.ShapeDtypeStruct((p, D_MODEL), F32),
                   jax.ShapeDtypeStruct((p, 1), F32)],
        compiler_params=_cparams(("arbitrary",)),
    )(yh, oa, gates, h0, w_bh, w_ba, w_out, g1, b1)


FF_T = D_FF // 2


def _ffn_in_swiglu(h1, w_fi):
    p = h1.shape[0]
    tm = _row_tile(p, 640)

    def body(h_ref, w_ref, au_ref, s_ref):
        au = _dot(h_ref[...], w_ref[...], "nn")
        au_ref[...] = au
        s_ref[...] = (jax.nn.silu(au[:, :FF_T]) * au[:, FF_T:]).astype(BF16)

    return pl.pallas_call(
        body, name="ffn_in_swiglu", grid=(D_FF // FF_T, p // tm),
        in_specs=[pl.BlockSpec((tm, D_MODEL), lambda j, i: (i, 0)), pl.BlockSpec((D_MODEL, 2 * FF_T), lambda j, i: (0, j))],
        out_specs=[pl.BlockSpec((tm, 2 * FF_T), lambda j, i: (i, j)), pl.BlockSpec((tm, FF_T), lambda j, i: (i, j))],
        out_shape=[jax.ShapeDtypeStruct((p, 2 * D_FF), F32), jax.ShapeDtypeStruct((p, D_FF), BF16)],
        compiler_params=_cparams(("arbitrary", "arbitrary")),
    )(h1, w_fi)


def _ffn_out_loss(s, w_fo, h1, g2, b2, target):
    p = h1.shape[0]
    tr = _row_tile(p, 640)
    k = tr // BLOCK

    def body(*refs):
        s_ref, w_ref, h_ref, g_ref, b_ref = refs[:5]
        dr_ref, loss_ref, dg_ref, db_ref = refs[5 + k:]
        i = pl.program_id(0)
        xhat, rstd = _ln_stats(ALPHA * h_ref[...] + _dot(s_ref[...], w_ref[...], "nn"))
        y = xhat * g_ref[...] + b_ref[...]
        row = i * tr + lax.broadcasted_iota(jnp.int32, (tr, 1), 0)
        tgt = jnp.concatenate([r[...] for r in refs[5:5 + k]], axis=0)
        err = jnp.where(row >= BLOCK, y - tgt, 0.0)
        dr, dg, db = _ln_bwd(err * (1.0 / D_MODEL), xhat, rstd, g_ref[...])
        dr_ref[...] = dr
        e2 = jnp.sum(err * err, axis=0, keepdims=True)
        part = e2[:, 0:BLOCK]
        for j in range(1, D_MODEL // BLOCK):
            part = part + e2[:, j * BLOCK:(j + 1) * BLOCK]
        part = part * (0.5 / D_MODEL)

        @pl.when(i == 0)
        def _():
            loss_ref[...] = part
            dg_ref[...] = dg
            db_ref[...] = db

        @pl.when(i > 0)
        def _():
            loss_ref[...] += part
            dg_ref[...] += dg
            db_ref[...] += db

    vec = pl.BlockSpec((1, D_MODEL), lambda i: (0, 0))
    rowsp = pl.BlockSpec((tr, D_MODEL), lambda i: (i, 0))
    return pl.pallas_call(
        body, name="ffn_out_loss", grid=(p // tr,),
        in_specs=[pl.BlockSpec((tr, D_FF), lambda i: (i, 0)), pl.BlockSpec((D_FF, D_MODEL), lambda i: (0, 0)),
                  rowsp, vec, vec] + _token_streams(tr),
        out_specs=[rowsp, pl.BlockSpec((1, BLOCK), lambda i: (0, 0)), vec, vec],
        out_shape=[jax.ShapeDtypeStruct((p, D_MODEL), F32), jax.ShapeDtypeStruct((1, BLOCK), F32),
                   jax.ShapeDtypeStruct((1, D_MODEL), F32), jax.ShapeDtypeStruct((1, D_MODEL), F32)],
        compiler_params=_cparams(("arbitrary",)),
    )(s, w_fo, h1, g2, b2, *([target] * k))


def _d_ffn_hidden(dr2, w_fo, au):
    p = au.shape[0]
    tm = _row_tile(p, 640)

    def body(d_ref, w_ref, au_ref, o_ref):
        ds = _dot(d_ref[...], w_ref[...], "nt")
        _, vjp = jax.vjp(lambda a, u: jax.nn.silu(a) * u, au_ref[:, :FF_T], au_ref[:, FF_T:])
        da, du = vjp(ds)
        o_ref[:, :FF_T] = da.astype(BF16)
        o_ref[:, FF_T:] = du.astype(BF16)

    return pl.pallas_call(
        body, name="d_ffn_hidden", grid=(D_FF // FF_T, p // tm),
        in_specs=[pl.BlockSpec((tm, D_MODEL), lambda j, i: (i, 0)), pl.BlockSpec((FF_T, D_MODEL), lambda j, i: (j, 0)),
                  pl.BlockSpec((tm, 2 * FF_T), lambda j, i: (i, j))],
        out_specs=pl.BlockSpec((tm, 2 * FF_T), lambda j, i: (i, j)),
        out_shape=jax.ShapeDtypeStruct((p, 2 * D_FF), BF16), compiler_params=_cparams(("arbitrary", "arbitrary")),
    )(dr2, w_fo, au)


def _ln_bwd_call(d_a, d_b, scale_a, xhat, rstd, g, name, mask_from=None):
    p = xhat.shape[0]
    tr = _row_tile(p, 640)

    def body(a_ref, b_ref, xh_ref, rs_ref, g_ref, dr_ref, dg_ref, db_ref):
        i = pl.program_id(0)
        dy = scale_a * a_ref[...] + b_ref[...]
        if mask_from is not None:
            row = i * tr + lax.broadcasted_iota(jnp.int32, (tr, 1), 0)
            dy = jnp.where(row >= mask_from, dy, 0.0)
        dr, dg, db = _ln_bwd(dy, xh_ref[...], rs_ref[...], g_ref[...])
        dr_ref[...] = dr

        @pl.when(i == 0)
        def _():
            dg_ref[...] = dg
            db_ref[...] = db

        @pl.when(i > 0)
        def _():
            dg_ref[...] += dg
            db_ref[...] += db

    vec = pl.BlockSpec((1, D_MODEL), lambda i: (0, 0))
    rowsp = pl.BlockSpec((tr, D_MODEL), lambda i: (i, 0))
    return pl.pallas_call(
        body, name=name, grid=(p // tr,),
        in_specs=[rowsp, rowsp, rowsp, pl.BlockSpec((tr, 1), lambda i: (i, 0)), vec],
        out_specs=[rowsp, vec, vec],
        out_shape=[jax.ShapeDtypeStruct((p, D_MODEL), F32)] + [jax.ShapeDtypeStruct((1, D_MODEL), F32)] * 2,
        compiler_params=_cparams(("arbitrary",)),
    )(d_a, d_b, xhat, rstd, g)


def _ln1_mix_bwd(dr2, dh1_ffn, xhat1, rstd1, g1, yh, oa, gates, w_bh, w_ba, w_out):
    p = yh.shape[0]
    tr = _tile(p, 320)

    def body(a_ref, b_ref, xh_ref, rs_ref, g1_ref, yh_ref, oa_ref, g_ref, wbh_ref, wba_ref, wo_ref,
             dr_ref, dyhg_ref, dyat_ref, dgt_ref, dyh_ref, doa_ref, dg_ref, db_ref):
        i = pl.program_id(0)
        dr, dg, db = _ln_bwd(ALPHA * a_ref[...] + b_ref[...], xh_ref[...], rs_ref[...], g1_ref[...])
        dr_ref[...] = dr
        d = _dot(dr, wo_ref[...], "nt")
        _, y_hg, y_att, s1, s2 = _branch_mix(yh_ref[...], oa_ref[...], g_ref[...], wbh_ref[...], wba_ref[...])
        dy_hg = d * s1
        dy_att = d * s2
        dyhg_ref[...] = dy_hg.astype(BF16)
        dyat_ref[...] = dy_att.astype(BF16)
        dgt_ref[:, :D_MODEL] = (d * y_hg * s1 * (1.0 - s1)).astype(BF16)
        dgt_ref[:, D_MODEL:] = (d * y_att * s2 * (1.0 - s2)).astype(BF16)
        dyh_ref[...] = _dot(dy_hg, wbh_ref[...], "nt")
        doa_ref[...] = _dot(dy_att, wba_ref[...], "nt")

        @pl.when(i == 0)
        def _():
            dg_ref[...] = dg
            db_ref[...] = db

        @pl.when(i > 0)
        def _():
            dg_ref[...] += dg
            db_ref[...] += db

    row = lambda w: pl.BlockSpec((tr, w), lambda i: (i, 0))
    const = lambda a: pl.BlockSpec(a.shape, lambda i: (0, 0))
    vec = pl.BlockSpec((1, D_MODEL), lambda i: (0, 0))
    return pl.pallas_call(
        body, name="ln1_mix_bwd", grid=(p // tr,),
        in_specs=[row(D_MODEL), row(D_MODEL), row(D_MODEL), row(1), vec, row(HG_W), row(ATT_QW), row(2 * D_MODEL),
                  const(w_bh), const(w_ba), const(w_out)],
        out_specs=[row(D_MODEL), row(D_MODEL), row(D_MODEL), row(2 * D_MODEL), row(HG_W), row(ATT_QW), vec, vec],
        out_shape=[jax.ShapeDtypeStruct((p, D_MODEL), F32), jax.ShapeDtypeStruct((p, D_MODEL), BF16),
                   jax.ShapeDtypeStruct((p, D_MODEL), BF16), jax.ShapeDtypeStruct((p, 2 * D_MODEL), BF16),
                   jax.ShapeDtypeStruct((p, HG_W), F32), jax.ShapeDtypeStruct((p, ATT_QW), F32),
                   jax.ShapeDtypeStruct((1, D_MODEL), F32), jax.ShapeDtypeStruct((1, D_MODEL), F32)],
        compiler_params=_cparams(("arbitrary",)),
    )(dr2, dh1_ffn, xhat1, rstd1, g1, yh, oa, gates, w_bh, w_ba, w_out)


MIX_W = 4 * HG_W + ATT_QW + 2 * ATT_KVW


def _mixers_bwd(proj_hg, proj_att, lbounds, norm_g, lv, states, cos, sin, sinks, dyh, doa, parts, swapped):
    p = proj_hg.shape[0]
    nb = p // BLOCK
    n = len(parts)
    kvw = 2 * ATT_KVW
    rev = lambda s: nb - 1 - s
    c_in, c_out, c_shapes, c_sems = _comm_specs(parts, N_PEERS)

    def body(*refs):
        (x_ref, lb_ref, ng_ref, lv_ref, st_ref, cur_ref, prev_ref, meta_ref, cc, sc, cp, sp, cm, sm, sink_ref,
         dy_ref, do_ref) = refs[:17]
        part_refs = refs[17:17 + n]
        dx_ref, dlb_ref, dng_ref, dsink_ref = refs[17 + n:21 + n]
        recv_refs = refs[21 + n:21 + 2 * n]
        dcarry_ref, dkv_next_ref, dkv_meta_ref = refs[21 + 2 * n:24 + 2 * n]
        starts, waits = _scatter_behind(part_refs, recv_refs, refs[24 + 2 * n:], swapped)
        step = pl.program_id(0)
        c = rev(step)

        @pl.when(step == 0)
        def _():
            dcarry_ref[...] = jnp.zeros_like(dcarry_ref)
            dkv_next_ref[...] = jnp.zeros_like(dkv_next_ref)
            dkv_meta_ref[...] = jnp.zeros_like(dkv_meta_ref)
            dlb_ref[...] = jnp.zeros_like(dlb_ref)
            dng_ref[...] = jnp.zeros_like(dng_ref)
            dsink_ref[...] = jnp.zeros_like(dsink_ref)
            for start in starts:
                start()

        fh = _first_half(BLOCK)
        qs, kc, vc = _att_load(cur_ref, cc, sc, True)
        _, kp, vp = _att_load(prev_ref, cp, sp, False)
        km, vm = _att_load_meta(meta_ref, cm, sm)
        s0, s1 = _att_sinkrows(sink_ref)
        masks = _att_masks(c)
        _, att_vjp = jax.vjp(lambda *a: _att_core(*a, *masks), *qs, km, kp, kc, vm, vp, vc, s0, s1)
        dq0, dq1, dq2, dq3, dkm, dkp, dkc, dvm, dvp, dvc, ds0, ds1 = att_vjp(do_ref[...])
        att0 = 4 * HG_W
        for j, dq in enumerate((dq0, dq1, dq2, dq3)):
            dx_ref[:, att0 + j * BLOCK:att0 + (j + 1) * BLOCK] = _rope_t(dq, cc[...], sc[...], fh).astype(BF16)
        dkv_meta_ref[:, :BLOCK] += _rope_t(dkm, cm[PAD:BLOCK, :], sm[PAD:BLOCK, :], _first_half(N_META))
        dkv_meta_ref[:, BLOCK:] += dvm
        last = jnp.where(c == 0, 1.0, 0.0)
        to_meta_rows = lambda m: jnp.concatenate([jnp.zeros((PAD, BLOCK), F32), last * m], axis=0)
        dk = _rope_t(dkc, cc[...], sc[...], fh) + dkv_next_ref[:, :BLOCK] + to_meta_rows(dkv_meta_ref[:, :BLOCK])
        dv = dvc + dkv_next_ref[:, BLOCK:] + to_meta_rows(dkv_meta_ref[:, BLOCK:])
        dx_ref[:, att0 + ATT_QW:att0 + ATT_QW + ATT_KVW] = dk.astype(BF16)
        dx_ref[:, att0 + ATT_QW + ATT_KVW:] = dv.astype(BF16)
        dkv_next_ref[:, :BLOCK] = _rope_t(dkp, cp[...], sp[...], fh)
        dkv_next_ref[:, BLOCK:] = dvp
        sink_rows = []
        for dsg in (ds0, ds1):
            for j in range(4):
                tot = jnp.sum(dsg[:, j * BLOCK:(j + 1) * BLOCK], axis=1, keepdims=True)
                sink_rows.append(jnp.broadcast_to(tot, (1, BLOCK)))
        dsink_ref[...] += jnp.concatenate(sink_rows, axis=0)

        valid = (c * BLOCK + lax.broadcasted_iota(jnp.int32, (BLOCK, 1), 0)) >= PAD
        (logf, k), gates_vjp = jax.vjp(lambda hf, a0, a1: _hgrn_gates(hf, a0, a1, valid),
                                       x_ref[:, HG_W:2 * HG_W], lb_ref[0:1, :], lb_ref[1:2, :])
        lvv = lv_ref[...]
        e = _split_dot(lvv, logf, "nn")
        dng = jnp.zeros((1, BLOCK), F32)
        dk, dseg = [], []
        for h in range(HG_HEADS):
            sl = lambda part: x_ref[:, part * HG_W + h * BLOCK: part * HG_W + (h + 1) * BLOCK]
            hs = slice(h * BLOCK, (h + 1) * BLOCK)
            _, vjp = jax.vjp(_hgrn_head, sl(0), k[:, hs], sl(2), sl(3), ng_ref[...], st_ref[0, h], *_seg_blocks(e, h))
            dhq, dkh, dhi, dhg, dngh, dst, *dsegh = vjp((dy_ref[:, hs], dcarry_ref[h]))
            for part, val in ((0, dhq), (2, dhi), (3, dhg)):
                dx_ref[:, part * HG_W + h * BLOCK: part * HG_W + (h + 1) * BLOCK] = val.astype(BF16)
            dk.append(dkh)
            dseg.append(jnp.concatenate(dsegh, axis=0))
            dng = dng + dngh
            dcarry_ref[h] = dst
        dlogf = _split_dot(lvv, jnp.concatenate(dseg, axis=1), "tn")
        dhf, da0, da1 = gates_vjp((dlogf, jnp.concatenate(dk, axis=1)))
        dx_ref[:, HG_W:2 * HG_W] = dhf.astype(BF16)
        dlb_ref[0:1, :] += da0
        dlb_ref[1:2, :] += da1
        dng_ref[...] += dng

        @pl.when(step == nb - 1)
        def _():
            for wait in waits:
                wait()

    const = lambda shape: pl.BlockSpec(shape, lambda s: (0,) * len(shape))
    return pl.pallas_call(
        body, name="mixers_bwd", grid=(nb,),
        in_specs=[pl.BlockSpec((BLOCK, 4 * HG_W), lambda s: (rev(s), 0)), const((2, HG_W)), const((1, BLOCK)),
                  const(lv.shape), pl.BlockSpec((1, HG_HEADS, BLOCK, BLOCK), lambda s: (rev(s), 0, 0, 0))]
        + _att_specs(rev)
        + [pl.BlockSpec((BLOCK, HG_W), lambda s: (rev(s), 0)), pl.BlockSpec((BLOCK, ATT_QW), lambda s: (rev(s), 0))]
        + c_in,
        out_specs=[pl.BlockSpec((BLOCK, MIX_W), lambda s: (rev(s), 0)), const((2, HG_W)), const((1, BLOCK)),
                   const((ATT_HEADS, BLOCK))] + c_out,
        out_shape=[jax.ShapeDtypeStruct((p, MIX_W), BF16), jax.ShapeDtypeStruct((2, HG_W), F32),
                   jax.ShapeDtypeStruct((1, BLOCK), F32), jax.ShapeDtypeStruct((ATT_HEADS, BLOCK), F32)] + c_shapes,
        scratch_shapes=[pltpu.VMEM((HG_HEADS, BLOCK, BLOCK), F32), pltpu.VMEM((BLOCK, kvw), F32),
                        pltpu.VMEM((N_META, kvw), F32)] + c_sems,
        compiler_params=_cparams(("arbitrary",)),
    )(proj_hg, lbounds, norm_g, lv, states, proj_att, proj_att, proj_att, cos, sin, cos, sin, cos, sin, sinks,
      dyh, doa, *parts)


def _d_h0_proj(dmix, dgates, w_mix, w_gates, parts, swapped):
    p = dmix.shape[0]
    tm = _row_tile(p, 640)
    nm = p // tm
    n = len(parts)
    c_in, c_out, c_shapes, c_sems = _comm_specs(parts, N_PEERS)

    def body(*refs):
        a_ref, g_ref, wa_ref, wg_ref = refs[:4]
        o_ref = refs[4 + n]
        starts, waits = _scatter_behind(refs[4:4 + n], refs[5 + n:5 + 2 * n], refs[5 + 2 * n:], swapped)
        i = pl.program_id(0)

        @pl.when(i == 0)
        def _():
            for start in starts:
                start()

        o_ref[...] = _dot(a_ref[...], wa_ref[...], "nt") + _dot(g_ref[...], wg_ref[...], "nt")

        @pl.when(i == nm - 1)
        def _():
            for wait in waits:
                wait()

    row = lambda w: pl.BlockSpec((tm, w), lambda i: (i, 0))
    const = lambda a: pl.BlockSpec(a.shape, lambda i: (0, 0))
    return pl.pallas_call(
        body, name="d_h0_proj", grid=(nm,),
        in_specs=[row(dmix.shape[1]), row(dgates.shape[1]), const(w_mix), const(w_gates)] + c_in,
        out_specs=[row(D_MODEL)] + c_out,
        out_shape=[jax.ShapeDtypeStruct((p, D_MODEL), F32)] + c_shapes,
        scratch_shapes=c_sems, compiler_params=_cparams(("arbitrary",)),
    )(dmix, dgates, w_mix, w_gates, *parts)


_LATE = ("w_branch_hg", "w_branch_attn", "w_out", "w_ffn_in", "w_ffn_out")
_COLUMN_SHARDED = ("meta_tokens", "w_in", "w_branch_hg", "w_branch_attn", "w_ffn_in")
_SWAPPED = ("w_ffn_in",)


def _whole(name, gathered):
    _, r, c = gathered.shape
    if name in _COLUMN_SHARDED:
        return jnp.transpose(gathered, (1, 0, 2)).reshape(r, N_DEV * c)
    return gathered.reshape(N_DEV * r, c)


def _slots(name, whole):
    r, c = whole.shape
    if name in _COLUMN_SHARDED:
        return jnp.transpose(whole.reshape(r, N_DEV, c // N_DEV), (1, 0, 2))
    return whole.reshape(N_DEV, r // N_DEV, c)


def _device_step(x, target, meta, ln_emb_g, ln_emb_b, w_in, lbounds, norm_g, sinks, late_shards,
                 ln1_g, ln1_b, ln2_g, ln2_b):
    s = x.shape[0]
    p = s + BLOCK
    lead = jnp.concatenate([jnp.zeros((PAD, D_MODEL), F32), meta], axis=0)
    tm = _row_tile(p, 640)
    lv = _level_stack()
    cos, sin = _rope_tables(p)
    hg_end = 4 * HG_W
    mm = functools.partial(_tiled_matmul, tm=tm)
    swapped = [n in _SWAPPED for n in _LATE]

    h0, h0b, xhat0, rstd0 = _embed_ln(x, lead, ln_emb_g, ln_emb_b)
    proj_hg = mm(h0b, w_in[:, :hg_end], "nn", tn=hg_end, tc=D_MODEL, out_dtype=F32, name="proj_hg")
    proj_att = mm(h0b, w_in[:, hg_end:MIX_W], "nn", tn=MIX_W - hg_end, tc=D_MODEL, out_dtype=F32, name="proj_att")
    gates = mm(h0b, w_in[:, MIX_W:], "nn", tn=2 * D_MODEL, tc=D_MODEL, out_dtype=F32, name="proj_gates")
    yh, states, oa, *gathered = _mixers_fwd(proj_hg, proj_att, lbounds, norm_g, lv, cos, sin, sinks, late_shards, swapped)
    w_bh, w_ba, w_out, w_fi, w_fo = [_whole(n, g) for n, g in zip(_LATE, gathered)]
    mixin, h1, h1b, xhat1, rstd1 = _mix_out_ln1(yh, oa, gates, h0, w_bh, w_ba, w_out, ln1_g, ln1_b)
    au, sw = _ffn_in_swiglu(h1b, w_fi)
    dr2, loss_part, dg2, db2 = _ffn_out_loss(sw, w_fo, h1, ln2_g, ln2_b, target)

    mtn = functools.partial(_tiled_matmul_tn, tm=_row_tile(p, 1664), out_dtype=BF16)
    d_wfo = mtn(sw, dr2, tk=FF_T, tn=D_MODEL, name="grad_w_ffn_out")
    dau = _d_ffn_hidden(dr2, w_fo, au)
    d_wfi = mtn(h1b, dau, tk=D_MODEL, tn=FF_T, name="grad_w_ffn_in")
    dh1_ffn = mm(dau, w_fi, "nt", tn=D_MODEL, tc=D_FF, out_dtype=F32, name="d_h1_ffn")
    dr1, dy_hg, dy_att, dgates, dyh, doa, dg1, db1 = _ln1_mix_bwd(
        dr2, dh1_ffn, xhat1, rstd1, ln1_g, yh, oa, gates, w_bh, w_ba, w_out)
    d_wout = mtn(mixin, dr1, tk=D_MODEL, tn=D_MODEL, name="grad_w_out")
    d_wbh = mtn(yh, dy_hg, tk=HG_W, tn=D_MODEL, name="grad_w_branch_hg")
    d_wba = mtn(oa, dy_att, tk=ATT_QW, tn=D_MODEL, name="grad_w_branch_attn")
    late_parts = [_slots(n, g) for n, g in zip(_LATE, (d_wbh, d_wba, d_wout, d_wfi, d_wfo))]
    dmix, d_lb, d_ng, d_sink, *late_recv = _mixers_bwd(
        proj_hg, proj_att, lbounds, norm_g, lv, states, cos, sin, sinks, dyh, doa, late_parts, swapped)
    d_win = jnp.concatenate([mtn(h0b, dmix, tk=D_MODEL, tn=MIX_W // 2, name="grad_w_in_mixers"),
                             mtn(h0b, dgates, tk=D_MODEL, tn=D_MODEL, name="grad_w_in_gates")], axis=1)
    win_parts = _slots("w_in", d_win)
    dh0_proj, win_recv = _d_h0_proj(dmix, dgates, w_in[:, :MIX_W], w_in[:, MIX_W:], [win_parts], [False])
    dxin, dg0, db0 = _ln_bwd_call(dr1, dh0_proj, ALPHA, xhat0, rstd0, ln_emb_g, "embed_ln_bwd", mask_from=PAD)

    small = dict(ln_emb_g=dg0, ln_emb_b=db0, hg_lower_bounds=d_lb, hg_norm_g=d_ng, attn_sinks=d_sink[:, 0],
                 ln1_g=dg1, ln1_b=db1, ln2_g=dg2, ln2_b=db2)
    big = dict(zip(_LATE, zip(late_parts, late_recv)))
    big["w_in"] = (win_parts, win_recv)
    return loss_part, dxin[BLOCK:], small, dxin[PAD:BLOCK], big


def _all_gather(arrs, dtypes, name):
    n = len(arrs)

    def body(*refs):
        ins, outs, stages = refs[:n], refs[n:2 * n], refs[2 * n:3 * n]
        send_sems, recv_sems, local_sems = refs[3 * n:]
        x, y, c = _place()
        sibling = (x, y, 1 - c)
        chips = [(1 - x, y), (x, 1 - y), (1 - x, 1 - y)]
        slot = lambda px, py, pc: 4 * px + 2 * py + pc

        def copy(w, k, block, to, from_stage=False):
            return pltpu.make_async_remote_copy(
                src_ref=stages[w] if from_stage else outs[w].at[slot(*block)], dst_ref=outs[w].at[slot(*block)],
                send_sem=send_sems.at[w, k], recv_sem=recv_sems.at[w, k], device_id=to, device_id_type=MESH)

        mine, first, passed = [], [], []
        for w in range(n):
            stages[w][...] = ins[w][...].astype(dtypes[w])
            mine.append(pltpu.make_async_copy(stages[w], outs[w].at[slot(x, y, c)], local_sems.at[w]))
            mine[-1].start()
        for w in range(n):
            first.append(copy(w, 0, (x, y, c), sibling, from_stage=True))
            first += [copy(w, 1 + j, (x, y, c), (*chip, c), from_stage=True) for j, chip in enumerate(chips)]
        for cp in first:
            cp.start()
        for j, chip in enumerate(chips):
            for w in range(n):
                copy(w, 1 + j, (*chip, c), (x, y, c)).wait_recv()
                passed.append(copy(w, 4 + j, (*chip, c), sibling))
                passed[-1].start()
        for w in range(n):
            copy(w, 0, sibling, (x, y, c)).wait_recv()
            for j, chip in enumerate(chips):
                copy(w, 4 + j, (*chip, 1 - c), (x, y, c)).wait_recv()
        for cp in first + passed:
            cp.wait_send()
        for cp in mine:
            cp.wait()

    return pl.pallas_call(
        body, name=name,
        in_specs=[pl.BlockSpec(memory_space=pltpu.VMEM)] * n,
        out_specs=[pl.BlockSpec(memory_space=pl.ANY)] * n,
        out_shape=[jax.ShapeDtypeStruct((N_DEV,) + a.shape, dt) for a, dt in zip(arrs, dtypes)],
        scratch_shapes=[pltpu.VMEM(a.shape, dt) for a, dt in zip(arrs, dtypes)]
        + [pltpu.SemaphoreType.DMA((n, 7)), pltpu.SemaphoreType.DMA((n, 7)), pltpu.SemaphoreType.DMA((n,))],
        compiler_params=pltpu.CompilerParams(vmem_limit_bytes=VMEM_LIMIT_BYTES),
    )(*arrs)


def _cast_shards(arrs):
    def body(*refs):
        for src, dst in zip(refs[:len(arrs)], refs[len(arrs):]):
            dst[...] = src[...].astype(BF16)

    return pl.pallas_call(body, name="cast_shards", out_shape=[jax.ShapeDtypeStruct(a.shape, BF16) for a in arrs],
                          compiler_params=pltpu.CompilerParams(vmem_limit_bytes=VMEM_LIMIT_BYTES))(*arrs)


def _shard_rows(rows):
    return rows if rows <= 512 else 256


def _adamw_math(w, g, m, v):
    m = ADAM_B1 * m + (1.0 - ADAM_B1) * g
    v = ADAM_B2 * v + (1.0 - ADAM_B2) * (g * g)
    m_hat = m / (1.0 - ADAM_B1 ** ADAM_STEP)
    v_hat = v / (1.0 - ADAM_B2 ** ADAM_STEP)
    delta = -ADAM_LR * (m_hat / (jnp.sqrt(v_hat) + ADAM_EPS) + ADAM_WD * w)
    return delta, m, v


def _reduce_adamw(parts, recv, own_slot, w, m, v, name):
    r, cdim = w.shape
    tr = _shard_rows(r)

    def body(idx_ref, p_ref, r_ref, w_ref, m_ref, v_ref, g_out, d_out, m_out, v_out):
        g = p_ref[0].astype(F32)
        for j in range(N_PEERS):
            g = g + r_ref[j].astype(F32)
        d, mn, vn = _adamw_math(w_ref[...], g, m_ref[...], v_ref[...])
        g_out[...] = g
        d_out[...] = d
        m_out[...] = mn
        v_out[...] = vn

    flat = pl.BlockSpec((tr, cdim), lambda i, idx_ref: (i, 0))
    return pl.pallas_call(
        body, name=name,
        grid_spec=pltpu.PrefetchScalarGridSpec(
            num_scalar_prefetch=1, grid=(r // tr,),
            in_specs=[pl.BlockSpec((1, tr, cdim), lambda i, idx_ref: (idx_ref[0], i, 0)),
                      pl.BlockSpec((N_PEERS, tr, cdim), lambda i, idx_ref: (0, i, 0)), flat, flat, flat],
            out_specs=[flat] * 4),
        out_shape=[jax.ShapeDtypeStruct((r, cdim), F32)] * 4,
        compiler_params=_cparams(("arbitrary",)),
    )(own_slot, parts, recv, w, m, v)


def _adamw_plain(w, g, m, v, name):
    def body(w_ref, g_ref, m_ref, v_ref, d_out, m_out, v_out):
        d_out[...], m_out[...], v_out[...] = _adamw_math(w_ref[...], g_ref[...], m_ref[...], v_ref[...])

    return pl.pallas_call(body, name=name, out_shape=[jax.ShapeDtypeStruct(w.shape, F32)] * 3)(w, g, m, v)


_SMALL_LAYOUT = (("ln_emb_g", 8), ("ln_emb_b", 8), ("hg_lower_bounds", 8), ("hg_norm_g", 1), ("attn_sinks", 1),
                 ("ln1_g", 8), ("ln1_b", 8), ("ln2_g", 8), ("ln2_b", 8))
_META_ROW = sum(r for _, r in _SMALL_LAYOUT)
_META_ROWS = N_META * D_MODEL // BLOCK
_LOSS_ROW = _META_ROW + _META_ROWS
SMALL_ROWS = 192


def _pack_small(vals, meta=None, loss_row=None):
    rows = []
    for name, nrows in _SMALL_LAYOUT:
        flat = vals[name].reshape(-1).astype(F32)
        flat = jnp.pad(flat, (0, nrows * BLOCK - flat.shape[0]))
        rows.append(flat.reshape(nrows, BLOCK))
    rows.append(jnp.zeros((_META_ROWS, BLOCK), F32) if meta is None else meta.reshape(_META_ROWS, BLOCK))
    rows.append(jnp.zeros((1, BLOCK), F32) if loss_row is None else loss_row)
    packed = jnp.concatenate(rows, axis=0)
    return jnp.pad(packed, ((0, SMALL_ROWS - packed.shape[0]), (0, 0)))


def _unpack_small(packed, shapes):
    out, row = {}, 0
    for name, nrows in _SMALL_LAYOUT:
        size = math.prod(shapes[name])
        out[name] = packed[row:row + nrows].reshape(-1)[:size].reshape(shapes[name])
        row += nrows
    return out


def _small_reduce_adamw(gathered, w, m, v):
    def body(g_ref, w_ref, m_ref, v_ref, g_out, d_out, m_out, v_out, loss_out):
        g = g_ref[0]
        for s in range(1, N_DEV):
            g = g + g_ref[s]
        d, mn, vn = _adamw_math(w_ref[...], g, m_ref[...], v_ref[...])
        g_out[...] = g
        d_out[...] = d
        m_out[...] = mn
        v_out[...] = vn
        loss_out[...] = jnp.broadcast_to(jnp.sum(g_ref[:, _LOSS_ROW, :]), (1, BLOCK))

    shp = jax.ShapeDtypeStruct((SMALL_ROWS, BLOCK), F32)
    return pl.pallas_call(body, name="small_reduce_adamw",
                          out_shape=[shp] * 4 + [jax.ShapeDtypeStruct((1, BLOCK), F32)])(gathered, w, m, v)


_WEIGHTS = ("meta_tokens", "ln_emb_g", "ln_emb_b", "w_in", "hg_lower_bounds", "hg_norm_g", "attn_sinks",
            "w_branch_hg", "w_branch_attn", "w_out", "ln1_g", "ln1_b", "w_ffn_in", "w_ffn_out", "ln2_g", "ln2_b")


def kernel(x, meta_tokens, ln_emb_g, ln_emb_b, w_in, hg_lower_bounds, hg_norm_g, attn_sinks, w_branch_hg, w_branch_attn, w_out, ln1_g, ln1_b, w_ffn_in, w_ffn_out, ln2_g, ln2_b, loss_target, m_meta_tokens, m_ln_emb_g, m_ln_emb_b, m_w_in, m_hg_lower_bounds, m_hg_norm_g, m_attn_sinks, m_w_branch_hg, m_w_branch_attn, m_w_out, m_ln1_g, m_ln1_b, m_w_ffn_in, m_w_ffn_out, m_ln2_g, m_ln2_b, v_meta_tokens, v_ln_emb_g, v_ln_emb_b, v_w_in, v_hg_lower_bounds, v_hg_norm_g, v_attn_sinks, v_w_branch_hg, v_w_branch_attn, v_w_out, v_ln1_g, v_ln1_b, v_w_ffn_in, v_w_ffn_out, v_ln2_g, v_ln2_b):
    given = dict(locals())
    weights = {n: given[n] for n in _WEIGHTS}
    mom1 = {n: given["m_" + n] for n in _WEIGHTS}
    mom2 = {n: given["v_" + n] for n in _WEIGHTS}
    shard2d = lambda a: a.reshape(a.shape[-2:])

    g_meta, g_win = _all_gather([meta_tokens, shard2d(w_in)], [F32, BF16], "gather_first")
    late_shards = _cast_shards([shard2d(weights[n]) for n in _LATE])

    loss_part, grad_x, small_grads, meta_grad, big = _device_step(
        x[0], loss_target[0], _whole("meta_tokens", g_meta), ln_emb_g.reshape(1, -1), ln_emb_b.reshape(1, -1),
        _whole("w_in", g_win), hg_lower_bounds, hg_norm_g, attn_sinks, late_shards, ln1_g, ln1_b, ln2_g, ln2_b)

    place = _place()
    out = {}
    for n, (parts, recv) in big.items():
        own = _slot(place, n in _SWAPPED).astype(jnp.int32).reshape(1)
        res = _reduce_adamw(parts, recv, own, shard2d(weights[n]), shard2d(mom1[n]), shard2d(mom2[n]), "adamw_" + n)
        out[n] = [r.reshape(weights[n].shape) for r in res]

    small_names = [n for n, _ in _SMALL_LAYOUT]
    packed = _pack_small(small_grads, meta_grad, loss_part)
    all_small, = _all_gather([packed], [F32], "gather_small")
    res = _small_reduce_adamw(all_small, _pack_small(weights), _pack_small(mom1), _pack_small(mom2))
    shapes = {n: weights[n].shape for n in small_names}
    unpacked = [_unpack_small(r, shapes) for r in res[:4]]
    for n in small_names:
        out[n] = [u[n] for u in unpacked]
    loss = res[4][0, 0]
    meta_whole = res[0][_META_ROW:_META_ROW + _META_ROWS].reshape(N_META, N_DEV, D_MODEL // N_DEV)
    g_meta_mine = lax.dynamic_index_in_dim(meta_whole, _slot(place, False), axis=1, keepdims=False)
    out["meta_tokens"] = [g_meta_mine, *_adamw_plain(meta_tokens, g_meta_mine, m_meta_tokens, v_meta_tokens,
                                                     "adamw_meta")]

    return (loss, grad_x[None], *[out[n][0] for n in _WEIGHTS], *[out[n][1] for n in _WEIGHTS],
            *[out[n][2] for n in _WEIGHTS], *[out[n][3] for n in _WEIGHTS])
```

```python
import functools
import math

import numpy as np
import jax
import jax.numpy as jnp
from jax import lax
from jax.experimental import pallas as pl
from jax.experimental.pallas import tpu as pltpu

F32 = jnp.float32
BF16 = jnp.bfloat16

D_MODEL = 1024
N_META = 16
BLOCK = 128
PAD = BLOCK - N_META
HG_HEADS = 4
HG_W = 512
ATT_HEADS = 8
HEAD_DIM = 64
ATT_QW = 512
ATT_KVW = 128
D_FF = 2816
EPS = 1e-5
ALPHA = 2.0 ** 0.25
ROPE_THETA = 10000.0
N_DEV = 8

ADAM_LR = 0.001
ADAM_B1 = 0.9
ADAM_B2 = 0.999
ADAM_EPS = 1e-08
ADAM_WD = 0.01
ADAM_STEP = 10

VMEM_LIMIT_BYTES = 56 * 1024 * 1024
MESH = pl.DeviceIdType.MESH

_LEVELS = (64, 32, 16, 8, 4, 2, 1)


def _cparams(sem):
    return pltpu.CompilerParams(dimension_semantics=sem, vmem_limit_bytes=VMEM_LIMIT_BYTES)


def _row_tile(rows, target):
    nb = rows // BLOCK
    best = 1
    for d in range(1, nb + 1):
        if nb % d == 0 and d * BLOCK <= target:
            best = d
    return best * BLOCK


_DN = {"nn": (((1,), (0,)), ((), ())), "nt": (((1,), (1,)), ((), ())), "tn": (((0,), (0,)), ((), ()))}


def _dot(a, b, form):
    return lax.dot_general(a.astype(BF16), b.astype(BF16), _DN[form], preferred_element_type=F32)


@functools.partial(jax.custom_vjp, nondiff_argnums=(2,))
def _mm(a, b, form):
    return _dot(a, b, form)


def _mm_fwd(a, b, form):
    a, b = a.astype(BF16), b.astype(BF16)
    return _dot(a, b, form), (a, b)


def _mm_bwd(form, res, g):
    a, b = res
    if form == "nn":
        return _dot(g, b, "nt"), _dot(a, g, "tn")
    if form == "nt":
        return _dot(g, b, "nn"), _dot(g, a, "tn")
    return _dot(b, g, "nt"), _dot(a, g, "nn")


_mm.defvjp(_mm_fwd, _mm_bwd)


def _split_dot(lv, x, form):
    return lax.dot_general(lv, x.astype(BF16), _DN[form], preferred_element_type=F32)


@jax.custom_vjp
def _swap_halves(x):
    return pltpu.roll(x, 64, 1)


_swap_halves.defvjp(lambda x: (pltpu.roll(x, 64, 1), None), lambda _, g: (pltpu.roll(g, 64, 1),))


def _tiled_matmul(a, b, form, *, tm, tn, tc, out_dtype, name):
    m, c = a.shape
    n = b.shape[1] if form == "nn" else b.shape[0]
    assert m % tm == 0 and n % tn == 0 and c % tc == 0, (name, a.shape, b.shape, tm, tn, tc)
    nc = c // tc

    def body(a_ref, b_ref, o_ref, *scratch):
        part = _dot(a_ref[...], b_ref[...], form)
        if nc == 1:
            o_ref[...] = part.astype(out_dtype)
            return
        acc_ref, = scratch
        ci = pl.program_id(2)

        @pl.when(ci == 0)
        def _():
            acc_ref[...] = part

        @pl.when(ci > 0)
        def _():
            acc_ref[...] += part

        @pl.when(ci == nc - 1)
        def _():
            o_ref[...] = acc_ref[...].astype(out_dtype)

    b_spec = (pl.BlockSpec((tc, tn), lambda j, i, k: (k, j)) if form == "nn"
              else pl.BlockSpec((tn, tc), lambda j, i, k: (j, k)))
    return pl.pallas_call(
        body, name=name, grid=(n // tn, m // tm, nc),
        in_specs=[pl.BlockSpec((tm, tc), lambda j, i, k: (i, k)), b_spec],
        out_specs=pl.BlockSpec((tm, tn), lambda j, i, k: (i, j)),
        out_shape=jax.ShapeDtypeStruct((m, n), out_dtype),
        scratch_shapes=[] if nc == 1 else [pltpu.VMEM((tm, tn), F32)],
        compiler_params=_cparams(("arbitrary", "arbitrary", "arbitrary")),
    )(a, b)


def _tiled_matmul_tn(a, b, *, tm, tk, tn, out_dtype, name):
    m, k = a.shape
    n = b.shape[1]
    assert m % tm == 0 and k % tk == 0 and n % tn == 0, (name, a.shape, b.shape, tm, tk, tn)
    nm = m // tm

    def body(a_ref, b_ref, o_ref, acc_ref):
        part = _dot(a_ref[...], b_ref[...], "tn")
        mi = pl.program_id(2)

        @pl.when(mi == 0)
        def _():
            acc_ref[...] = part

        @pl.when(mi > 0)
        def _():
            acc_ref[...] += part

        @pl.when(mi == nm - 1)
        def _():
            o_ref[...] = acc_ref[...].astype(out_dtype)

    return pl.pallas_call(
        body, name=name, grid=(k // tk, n // tn, nm),
        in_specs=[pl.BlockSpec((tm, tk), lambda kk, j, i: (i, kk)), pl.BlockSpec((tm, tn), lambda kk, j, i: (i, j))],
        out_specs=pl.BlockSpec((tk, tn), lambda kk, j, i: (kk, j)),
        out_shape=jax.ShapeDtypeStruct((k, n), out_dtype),
        scratch_shapes=[pltpu.VMEM((tk, tn), F32)],
        compiler_params=_cparams(("arbitrary", "arbitrary", "arbitrary")),
    )(a, b)


def _ln_stats(r):
    mu = jnp.mean(r, axis=-1, keepdims=True)
    xc = r - mu
    var = jnp.mean(xc * xc, axis=-1, keepdims=True)
    rstd = lax.rsqrt(var + EPS)
    return xc * rstd, rstd


def _ln_bwd(dy, xhat, rstd, g):
    dxhat = dy * g
    m1 = jnp.mean(dxhat, axis=-1, keepdims=True)
    m2 = jnp.mean(dxhat * xhat, axis=-1, keepdims=True)
    dr = rstd * (dxhat - m1 - xhat * m2)
    return dr, jnp.sum(dy * xhat, axis=0, keepdims=True), jnp.sum(dy, axis=0, keepdims=True)


N_SEG = 3 + len(_LEVELS)


def _level_stack():
    t = np.arange(BLOCK)[:, None]
    r = np.arange(BLOCK)[None, :]
    mats = [r <= t, r > t, np.ones((BLOCK, BLOCK), bool)]
    for h in _LEVELS:
        same = (t // (2 * h)) == (r // (2 * h))
        up_t, up_r = (t % (2 * h)) >= h, (r % (2 * h)) >= h
        mats.append(same & ((up_t & up_r & (r <= t)) | (~up_t & ~up_r & (r > t))))
    return jnp.asarray(np.concatenate(mats, axis=0).astype(np.float32), dtype=BF16)


def _hgrn_gates(hf, a0, a1, valid):
    lb = jax.nn.sigmoid(a0 - a1)
    fg = lb + (1.0 - lb) * jax.nn.sigmoid(hf)
    return jnp.where(valid, jnp.log(fg), 0.0), jnp.where(valid, 1.0 - fg, 0.0)


def _hgrn_head(hq, k, v, hg, ng, st_in, *seg):
    q = jax.nn.silu(hq)
    rows = lax.broadcasted_iota(jnp.int32, (BLOCK, BLOCK), 0)
    cols = lax.broadcasted_iota(jnp.int32, (BLOCK, BLOCK), 1)
    o = _mm(q * jnp.exp(seg[0]), st_in, "nt")
    a = jnp.where(rows == cols, jnp.sum(q * k, axis=-1, keepdims=True), 0.0)
    for li, h in enumerate(_LEVELS):
        decay = jnp.exp(seg[3 + li])
        pair = ((rows // (2 * h)) == (cols // (2 * h))) & ((rows % (2 * h)) >= h) & ((cols % (2 * h)) < h)
        a = a + jnp.where(pair, _mm(q * decay, k * decay, "nt"), 0.0)
    o = o + _mm(a, v, "nn")
    st_out = st_in * jnp.exp(seg[2]) + _mm(v, k * jnp.exp(seg[1]), "tn")
    on = o * lax.rsqrt(jnp.mean(o * o, axis=-1, keepdims=True) + EPS) * ng
    return on * jax.nn.silu(hg), st_out


def _seg_blocks(e, h):
    return [e[i * BLOCK:(i + 1) * BLOCK, h * BLOCK:(h + 1) * BLOCK] for i in range(N_SEG)]


def _rope(x, cos, sin, first_half):
    partner = jnp.where(first_half, -pltpu.roll(x, 96, 1), pltpu.roll(x, 32, 1))
    return x * cos + partner * sin


def _rope_t(g, cos, sin, first_half):
    u = g * sin
    partner = jnp.where(first_half, pltpu.roll(u, 96, 1), -pltpu.roll(u, 32, 1))
    return g * cos + partner


def _att_core(q0, q1, q2, q3, km, kp, kc, vm, vp, vc, sinkrow0, sinkrow1, own_side, ok_band, ok_meta):
    low = lambda x: lax.broadcasted_iota(jnp.int32, x.shape, 1) < HEAD_DIM
    scale = HEAD_DIM ** -0.5
    neg = jnp.finfo(F32).min
    wide = lambda m: jnp.concatenate([m] * 4, axis=1)
    own4, band4, meta4 = wide(own_side), wide(ok_band), wide(ok_meta)
    outs = []
    for g, (qa, qb, sinkrow) in enumerate(((q0, q1, sinkrow0), (q2, q3, sinkrow1))):
        def both(x, g=g):
            sw = _swap_halves(x)
            return jnp.where(low(x), x, sw) if g == 0 else jnp.where(low(x), sw, x)
        q4 = jnp.concatenate([jnp.where(low(qa), qa, 0.0), jnp.where(low(qa), 0.0, qa),
                              jnp.where(low(qb), qb, 0.0), jnp.where(low(qb), 0.0, qb)], axis=0)
        s = jnp.where(own4, _mm(both(kc), q4, "nt"), _mm(both(kp), q4, "nt"))
        s = jnp.where(band4, s * scale, neg)
        sm = jnp.where(meta4, _mm(both(km), q4, "nt") * scale, neg)
        mx = jnp.maximum(jnp.maximum(jnp.max(s, axis=0, keepdims=True), jnp.max(sm, axis=0, keepdims=True)), sinkrow)
        mx = lax.stop_gradient(mx)
        p, pm = jnp.exp(s - mx), jnp.exp(sm - mx)
        inv = 1.0 / (jnp.sum(p, axis=0, keepdims=True) + jnp.sum(pm, axis=0, keepdims=True) + jnp.exp(sinkrow - mx))
        p = p * inv
        o4 = (_mm(jnp.where(own4, p, 0.0), both(vc), "tn") + _mm(jnp.where(own4, 0.0, p), both(vp), "tn")
              + _mm(pm * inv, both(vm), "tn"))
        for j in range(2):
            upper = o4[(2 * j) * BLOCK:(2 * j + 1) * BLOCK]
            outs.append(jnp.where(low(upper), upper, o4[(2 * j + 1) * BLOCK:(2 * j + 2) * BLOCK]))
    return jnp.concatenate(outs, axis=1)


def _att_masks(blk_idx):
    kidx = lax.broadcasted_iota(jnp.int32, (BLOCK, BLOCK), 0)
    qrow = lax.broadcasted_iota(jnp.int32, (BLOCK, BLOCK), 1)
    own_side = kidx <= qrow
    pos_own = blk_idx * BLOCK + kidx - PAD
    ok_band = (own_side & (pos_own >= N_META)) | (~own_side & (pos_own - BLOCK >= N_META) & (blk_idx >= 1))
    qpos = blk_idx * BLOCK + lax.broadcasted_iota(jnp.int32, (N_META, BLOCK), 1) - PAD
    ok_meta = lax.broadcasted_iota(jnp.int32, (N_META, BLOCK), 0) <= qpos
    return own_side, ok_band, ok_meta


def _token_streams(tr):
    k = tr // BLOCK
    return [pl.BlockSpec((BLOCK, D_MODEL), lambda i, j=j: (jnp.maximum(k * i - 1 + j, 0), 0)) for j in range(k)]


def _embed_ln(x, lead, g0, b0):
    p = x.shape[0] + BLOCK
    tr = _row_tile(p, 640)
    k = tr // BLOCK

    def body(*refs):
        lead_ref, g_ref, b_ref, h_ref, hb_ref, xh_ref, rs_ref = refs[k:]
        first = jnp.where(pl.program_id(0) == 0, lead_ref[...], refs[0][...])
        xhat, rstd = _ln_stats(jnp.concatenate([first] + [r[...] for r in refs[1:k]], axis=0))
        row = pl.program_id(0) * tr + lax.broadcasted_iota(jnp.int32, (tr, 1), 0)
        h = jnp.where(row >= PAD, xhat * g_ref[...] + b_ref[...], 0.0)
        h_ref[...] = h
        hb_ref[...] = h.astype(BF16)
        xh_ref[...] = xhat
        rs_ref[...] = rstd

    vec = pl.BlockSpec((1, D_MODEL), lambda i: (0, 0))
    rowsp = pl.BlockSpec((tr, D_MODEL), lambda i: (i, 0))
    return pl.pallas_call(
        body, name="embed_ln", grid=(p // tr,),
        in_specs=_token_streams(tr) + [pl.BlockSpec((BLOCK, D_MODEL), lambda i: (0, 0)), vec, vec],
        out_specs=[rowsp, rowsp, rowsp, pl.BlockSpec((tr, 1), lambda i: (i, 0))],
        out_shape=[jax.ShapeDtypeStruct((p, D_MODEL), F32), jax.ShapeDtypeStruct((p, D_MODEL), BF16),
                   jax.ShapeDtypeStruct((p, D_MODEL), F32), jax.ShapeDtypeStruct((p, 1), F32)],
        compiler_params=_cparams(("arbitrary",)),
    )(*([x] * k), lead, g0, b0)


def _rope_tables(p):
    pos = (np.arange(p, dtype=np.int32) - PAD).astype(np.float32)
    half = HEAD_DIM // 2
    inv = np.float32(ROPE_THETA) ** (-np.arange(half, dtype=np.float32) / np.float32(half))
    ang = pos[:, None] * np.tile(inv.astype(np.float32), BLOCK // half)[None, :]
    return jnp.asarray(np.cos(ang), F32), jnp.asarray(np.sin(ang), F32)


def _att_sinkrows(sink_ref):
    lanehead = lax.broadcasted_iota(jnp.int32, (1, 4 * BLOCK), 1) // BLOCK
    rows = []
    for g in range(2):
        row = jnp.zeros((1, 4 * BLOCK), F32)
        for j in range(4):
            row = jnp.where(lanehead == j, sink_ref[0, 4 * g + j], row)
        rows.append(row)
    return rows


def _first_half(rows):
    return (lax.broadcasted_iota(jnp.int32, (rows, BLOCK), 1) % HEAD_DIM) < (HEAD_DIM // 2)


def _att_load(qkv_ref, cos_ref, sin_ref, with_q):
    cos, sin, fh = cos_ref[...], sin_ref[...], _first_half(BLOCK)
    qs = [_rope(qkv_ref[:, j * BLOCK:(j + 1) * BLOCK], cos, sin, fh) for j in range(4)] if with_q else None
    k = _rope(qkv_ref[:, ATT_QW:ATT_QW + ATT_KVW], cos, sin, fh)
    v = qkv_ref[:, ATT_QW + ATT_KVW:ATT_QW + 2 * ATT_KVW]
    return qs, k, v


def _att_load_meta(qkv_ref, cos_ref, sin_ref):
    k = _rope(qkv_ref[PAD:BLOCK, ATT_QW:ATT_QW + ATT_KVW], cos_ref[PAD:BLOCK, :], sin_ref[PAD:BLOCK, :],
              _first_half(N_META))
    return k, qkv_ref[PAD:BLOCK, ATT_QW + ATT_KVW:ATT_QW + 2 * ATT_KVW]


def _att_specs(blk):
    w = ATT_QW + 2 * ATT_KVW
    cur = lambda width: pl.BlockSpec((BLOCK, width), lambda i: (blk(i), 0))
    prev = lambda width: pl.BlockSpec((BLOCK, width), lambda i: (jnp.maximum(blk(i) - 1, 0), 0))
    meta = lambda width: pl.BlockSpec((BLOCK, width), lambda i: (0, 0))
    return [cur(w), prev(w), meta(w), cur(BLOCK), cur(BLOCK), prev(BLOCK), prev(BLOCK), meta(BLOCK), meta(BLOCK),
            pl.BlockSpec(memory_space=pltpu.SMEM)]


_FLIPS = [(dx, dy, dc) for dx in (0, 1) for dy in (0, 1) for dc in (0, 1)][1:]
N_PEERS = len(_FLIPS)


def _place():
    return lax.axis_index("x"), lax.axis_index("y"), lax.axis_index("c")


def _peer(place, flip):
    return tuple(1 - p if f else p for p, f in zip(place, flip))


def _slot(place, swapped):
    x, y, c = place
    return 4 * y + 2 * x + c if swapped else 4 * x + 2 * y + c


def _comm_specs(arrs, out_lead):
    n = len(arrs)
    outs = [jax.ShapeDtypeStruct((out_lead,) + a.shape[-2:], a.dtype) for a in arrs]
    sems = [pltpu.SemaphoreType.DMA((n, N_PEERS)), pltpu.SemaphoreType.DMA((n, N_PEERS)), pltpu.SemaphoreType.DMA((n,))]
    return [pl.BlockSpec(memory_space=pl.ANY)] * n, [pl.BlockSpec(memory_space=pl.ANY)] * n, outs, sems


def _gather_behind(shard_refs, out_refs, sems, swapped):
    send_sems, recv_sems, local_sems = sems
    x, y, c = _place()
    me, sibling = (x, y, c), (x, y, 1 - c)
    chips = [(1 - x, y), (x, 1 - y), (1 - x, 1 - y)]
    starts, passes, waits = [], [], []
    for w, (s, o) in enumerate(zip(shard_refs, out_refs)):
        def copy(k, block, to, from_shard=False, w=w, s=s, o=o):
            rows = o.at[_slot(block, swapped[w])]
            return pltpu.make_async_remote_copy(
                src_ref=s if from_shard else rows, dst_ref=rows, send_sem=send_sems.at[w, k],
                recv_sem=recv_sems.at[w, k], device_id=to, device_id_type=MESH)

        own = pltpu.make_async_copy(s, o.at[_slot(me, swapped[w])], local_sems.at[w])
        first = [copy(0, me, sibling, True)] + [copy(1 + j, me, (*chip, c), True) for j, chip in enumerate(chips)]
        handed = [copy(4 + j, (*chip, c), sibling) for j, chip in enumerate(chips)]
        starts += [own.start] + [cp.start for cp in first]
        for j, chip in enumerate(chips):
            passes += [copy(1 + j, (*chip, c), me).wait_recv, handed[j].start]
        waits.append(copy(0, sibling, me).wait_recv)
        waits += [copy(4 + j, (*chip, 1 - c), me).wait_recv for j, chip in enumerate(chips)]
        waits += [cp.wait_send for cp in first + handed] + [own.wait]
    return starts, passes, waits


def _scatter_behind(part_refs, recv_refs, sems, swapped):
    send_sems, recv_sems, _ = sems
    place = _place()
    starts, waits = [], []
    for w, (p, o) in enumerate(zip(part_refs, recv_refs)):
        for r, flip in enumerate(_FLIPS):
            peer = _peer(place, flip)
            cp = pltpu.make_async_remote_copy(
                src_ref=p.at[_slot(peer, swapped[w])], dst_ref=o.at[r], send_sem=send_sems.at[w, r],
                recv_sem=recv_sems.at[w, r], device_id=peer, device_id_type=MESH)
            starts.append(cp.start)
            waits += [cp.wait_recv, cp.wait_send]
    return starts, waits


def _mixers_fwd(proj_hg, proj_att, lbounds, norm_g, lv, cos, sin, sinks, shards, swapped):
    p = proj_hg.shape[0]
    nb = p // BLOCK
    n = len(shards)
    c_in, c_out, c_shapes, c_sems = _comm_specs(shards, N_DEV)
    pass_step = min(nb - 1, max(1, (5 * nb) // 8))

    def body(*refs):
        x_ref, lb_ref, ng_ref, lv_ref, cur_ref, prev_ref, meta_ref, cc, sc, cp, sp, cm, sm, sink_ref = refs[:14]
        shard_refs = refs[14:14 + n]
        y_ref, st_ref, o_ref = refs[14 + n:17 + n]
        out_refs = refs[17 + n:17 + 2 * n]
        carry_ref = refs[17 + 2 * n]
        starts, passes, waits = _gather_behind(shard_refs, out_refs, refs[18 + 2 * n:], swapped)
        c = pl.program_id(0)

        @pl.when(c == 0)
        def _():
            carry_ref[...] = jnp.zeros_like(carry_ref)
            for start in starts:
                start()

        @pl.when(c == pass_step)
        def _():
            for step in passes:
                step()

        valid = (c * BLOCK + lax.broadcasted_iota(jnp.int32, (BLOCK, 1), 0)) >= PAD
        logf, k = _hgrn_gates(x_ref[:, HG_W:2 * HG_W], lb_ref[0:1, :], lb_ref[1:2, :], valid)
        e = _split_dot(lv_ref[...], logf, "nn")
        for h in range(HG_HEADS):
            sl = lambda part: x_ref[:, part * HG_W + h * BLOCK: part * HG_W + (h + 1) * BLOCK]
            hs = slice(h * BLOCK, (h + 1) * BLOCK)
            st_in = carry_ref[h]
            st_ref[0, h] = st_in
            y, st_out = _hgrn_head(sl(0), k[:, hs], sl(2), sl(3), ng_ref[...], st_in, *_seg_blocks(e, h))
            y_ref[:, hs] = y.astype(BF16)
            carry_ref[h] = st_out

        qs, kc, vc = _att_load(cur_ref, cc, sc, True)
        _, kp, vp = _att_load(prev_ref, cp, sp, False)
        km, vm = _att_load_meta(meta_ref, cm, sm)
        s0, s1 = _att_sinkrows(sink_ref)
        o_ref[...] = _att_core(*qs, km, kp, kc, vm, vp, vc, s0, s1, *_att_masks(c)).astype(BF16)

        @pl.when(c == nb - 1)
        def _():
            for wait in waits:
                wait()

    return pl.pallas_call(
        body, name="mixers_fwd", grid=(nb,),
        in_specs=[pl.BlockSpec((BLOCK, 4 * HG_W), lambda c: (c, 0)), pl.BlockSpec((2, HG_W), lambda c: (0, 0)),
                  pl.BlockSpec((1, BLOCK), lambda c: (0, 0)), pl.BlockSpec(lv.shape, lambda c: (0, 0))]
        + _att_specs(lambda c: c) + c_in,
        out_specs=[pl.BlockSpec((BLOCK, HG_W), lambda c: (c, 0)),
                   pl.BlockSpec((1, HG_HEADS, BLOCK, BLOCK), lambda c: (c, 0, 0, 0)),
                   pl.BlockSpec((BLOCK, ATT_QW), lambda c: (c, 0))] + c_out,
        out_shape=[jax.ShapeDtypeStruct((p, HG_W), BF16), jax.ShapeDtypeStruct((nb, HG_HEADS, BLOCK, BLOCK), F32),
                   jax.ShapeDtypeStruct((p, ATT_QW), BF16)] + c_shapes,
        scratch_shapes=[pltpu.VMEM((HG_HEADS, BLOCK, BLOCK), F32)] + c_sems,
        compiler_params=_cparams(("arbitrary",)),
    )(proj_hg, lbounds, norm_g, lv, proj_att, proj_att, proj_att, cos, sin, cos, sin, cos, sin, sinks, *shards)


def _tile(rows, preferred):
    return preferred if rows % preferred == 0 else _row_tile(rows, preferred)


def _branch_mix(yh, oa, gates, w_bh, w_ba):
    y_hg = _dot(yh, w_bh, "nn")
    y_att = _dot(oa, w_ba, "nn")
    s1 = jax.nn.sigmoid(gates[:, :D_MODEL])
    s2 = jax.nn.sigmoid(gates[:, D_MODEL:])
    return s1 * y_hg + s2 * y_att, y_hg, y_att, s1, s2


def _mix_out_ln1(yh, oa, gates, h0, w_bh, w_ba, w_out, g1, b1):
    p = yh.shape[0]
    tr = _tile(p, 320)

    def body(yh_ref, oa_ref, g_ref, h0_ref, wbh_ref, wba_ref, wo_ref, g1_ref, b1_ref,
             mix_ref, h1_ref, h1b_ref, xh_ref, rs_ref):
        mixin = _branch_mix(yh_ref[...], oa_ref[...], g_ref[...], wbh_ref[...], wba_ref[...])[0]
        mix_ref[...] = mixin.astype(BF16)
        xhat, rstd = _ln_stats(ALPHA * h0_ref[...] + _dot(mixin, wo_ref[...], "nn"))
        h1 = xhat * g1_ref[...] + b1_ref[...]
        h1_ref[...] = h1
        h1b_ref[...] = h1.astype(BF16)
        xh_ref[...] = xhat
        rs_ref[...] = rstd

    row = lambda w: pl.BlockSpec((tr, w), lambda i: (i, 0))
    const = lambda a: pl.BlockSpec(a.shape, lambda i: (0, 0))
    return pl.pallas_call(
        body, name="mix_out_ln1", grid=(p // tr,),
        in_specs=[row(HG_W), row(ATT_QW), row(2 * D_MODEL), row(D_MODEL), const(w_bh), const(w_ba), const(w_out),
                  const(g1), const(b1)],
        out_specs=[row(D_MODEL), row(D_MODEL), row(D_MODEL), row(D_MODEL), row(1)],
        out_shape=[jax.ShapeDtypeStruct((p, D_MODEL), BF16), jax.ShapeDtypeStruct((p, D_MODEL), F32),
                   jax.ShapeDtypeStruct((p, D_MODEL), BF16), jax.ShapeDtypeStruct((p, D_MODEL), F32),
                   jax.ShapeDtypeStruct((p, 1), F32)],
        compiler_params=_cparams(("arbitrary",)),
    )(yh, oa, gates, h0, w_bh, w_ba, w_out, g1, b1)


FF_T = D_FF // 2


def _ffn_in_swiglu(h1, w_fi):
    p = h1.shape[0]
    tm = _row_tile(p, 640)

    def body(h_ref, w_ref, au_ref, s_ref):
        au = _dot(h_ref[...], w_ref[...], "nn")
        au_ref[...] = au
        s_ref[...] = (jax.nn.silu(au[:, :FF_T]) * au[:, FF_T:]).astype(BF16)

    return pl.pallas_call(
        body, name="ffn_in_swiglu", grid=(D_FF // FF_T, p // tm),
        in_specs=[pl.BlockSpec((tm, D_MODEL), lambda j, i: (i, 0)), pl.BlockSpec((D_MODEL, 2 * FF_T), lambda j, i: (0, j))],
        out_specs=[pl.BlockSpec((tm, 2 * FF_T), lambda j, i: (i, j)), pl.BlockSpec((tm, FF_T), lambda j, i: (i, j))],
        out_shape=[jax.ShapeDtypeStruct((p, 2 * D_FF), F32), jax.ShapeDtypeStruct((p, D_FF), BF16)],
        compiler_params=_cparams(("arbitrary", "arbitrary")),
    )(h1, w_fi)


def _ffn_out_loss(s, w_fo, h1, g2, b2, target):
    p = h1.shape[0]
    tr = _row_tile(p, 640)
    k = tr // BLOCK

    def body(*refs):
        s_ref, w_ref, h_ref, g_ref, b_ref = refs[:5]
        dr_ref, loss_ref, dg_ref, db_ref = refs[5 + k:]
        i = pl.program_id(0)
        xhat, rstd = _ln_stats(ALPHA * h_ref[...] + _dot(s_ref[...], w_ref[...], "nn"))
        y = xhat * g_ref[...] + b_ref[...]
        row = i * tr + lax.broadcasted_iota(jnp.int32, (tr, 1), 0)
        tgt = jnp.concatenate([r[...] for r in refs[5:5 + k]], axis=0)
        err = jnp.where(row >= BLOCK, y - tgt, 0.0)
        dr, dg, db = _ln_bwd(err * (1.0 / D_MODEL), xhat, rstd, g_ref[...])
        dr_ref[...] = dr
        e2 = jnp.sum(err * err, axis=0, keepdims=True)
        part = e2[:, 0:BLOCK]
        for j in range(1, D_MODEL // BLOCK):
            part = part + e2[:, j * BLOCK:(j + 1) * BLOCK]
        part = part * (0.5 / D_MODEL)

        @pl.when(i == 0)
        def _():
            loss_ref[...] = part
            dg_ref[...] = dg
            db_ref[...] = db

        @pl.when(i > 0)
        def _():
            loss_ref[...] += part
            dg_ref[...] += dg
            db_ref[...] += db

    vec = pl.BlockSpec((1, D_MODEL), lambda i: (0, 0))
    rowsp = pl.BlockSpec((tr, D_MODEL), lambda i: (i, 0))
    return pl.pallas_call(
        body, name="ffn_out_loss", grid=(p // tr,),
        in_specs=[pl.BlockSpec((tr, D_FF), lambda i: (i, 0)), pl.BlockSpec((D_FF, D_MODEL), lambda i: (0, 0)),
                  rowsp, vec, vec] + _token_streams(tr),
        out_specs=[rowsp, pl.BlockSpec((1, BLOCK), lambda i: (0, 0)), vec, vec],
        out_shape=[jax.ShapeDtypeStruct((p, D_MODEL), F32), jax.ShapeDtypeStruct((1, BLOCK), F32),
                   jax.ShapeDtypeStruct((1, D_MODEL), F32), jax.ShapeDtypeStruct((1, D_MODEL), F32)],
        compiler_params=_cparams(("arbitrary",)),
    )(s, w_fo, h1, g2, b2, *([target] * k))


def _d_ffn_hidden(dr2, w_fo, au):
    p = au.shape[0]
    tm = _row_tile(p, 640)

    def body(d_ref, w_ref, au_ref, o_ref):
        ds = _dot(d_ref[...], w_ref[...], "nt")
        _, vjp = jax.vjp(lambda a, u: jax.nn.silu(a) * u, au_ref[:, :FF_T], au_ref[:, FF_T:])
        da, du = vjp(ds)
        o_ref[:, :FF_T] = da.astype(BF16)
        o_ref[:, FF_T:] = du.astype(BF16)

    return pl.pallas_call(
        body, name="d_ffn_hidden", grid=(D_FF // FF_T, p // tm),
        in_specs=[pl.BlockSpec((tm, D_MODEL), lambda j, i: (i, 0)), pl.BlockSpec((FF_T, D_MODEL), lambda j, i: (j, 0)),
                  pl.BlockSpec((tm, 2 * FF_T), lambda j, i: (i, j))],
        out_specs=pl.BlockSpec((tm, 2 * FF_T), lambda j, i: (i, j)),
        out_shape=jax.ShapeDtypeStruct((p, 2 * D_FF), BF16), compiler_params=_cparams(("arbitrary", "arbitrary")),
    )(dr2, w_fo, au)


def _ln_bwd_call(d_a, d_b, scale_a, xhat, rstd, g, name, mask_from=None):
    p = xhat.shape[0]
    tr = _row_tile(p, 640)

    def body(a_ref, b_ref, xh_ref, rs_ref, g_ref, dr_ref, dg_ref, db_ref):
        i = pl.program_id(0)
        dy = scale_a * a_ref[...] + b_ref[...]
        if mask_from is not None:
            row = i * tr + lax.broadcasted_iota(jnp.int32, (tr, 1), 0)
            dy = jnp.where(row >= mask_from, dy, 0.0)
        dr, dg, db = _ln_bwd(dy, xh_ref[...], rs_ref[...], g_ref[...])
        dr_ref[...] = dr

        @pl.when(i == 0)
        def _():
            dg_ref[...] = dg
            db_ref[...] = db

        @pl.when(i > 0)
        def _():
            dg_ref[...] += dg
            db_ref[...] += db

    vec = pl.BlockSpec((1, D_MODEL), lambda i: (0, 0))
    rowsp = pl.BlockSpec((tr, D_MODEL), lambda i: (i, 0))
    return pl.pallas_call(
        body, name=name, grid=(p // tr,),
        in_specs=[rowsp, rowsp, rowsp, pl.BlockSpec((tr, 1), lambda i: (i, 0)), vec],
        out_specs=[rowsp, vec, vec],
        out_shape=[jax.ShapeDtypeStruct((p, D_MODEL), F32)] + [jax.ShapeDtypeStruct((1, D_MODEL), F32)] * 2,
        compiler_params=_cparams(("arbitrary",)),
    )(d_a, d_b, xhat, rstd, g)


def _ln1_mix_bwd(dr2, dh1_ffn, xhat1, rstd1, g1, yh, oa, gates, w_bh, w_ba, w_out):
    p = yh.shape[0]
    tr = _tile(p, 320)

    def body(a_ref, b_ref, xh_ref, rs_ref, g1_ref, yh_ref, oa_ref, g_ref, wbh_ref, wba_ref, wo_ref,
             dr_ref, dyhg_ref, dyat_ref, dgt_ref, dyh_ref, doa_ref, dg_ref, db_ref):
        i = pl.program_id(0)
        dr, dg, db = _ln_bwd(ALPHA * a_ref[...] + b_ref[...], xh_ref[...], rs_ref[...], g1_ref[...])
        dr_ref[...] = dr
        d = _dot(dr, wo_ref[...], "nt")
        _, y_hg, y_att, s1, s2 = _branch_mix(yh_ref[...], oa_ref[...], g_ref[...], wbh_ref[...], wba_ref[...])
        dy_hg = d * s1
        dy_att = d * s2
        dyhg_ref[...] = dy_hg.astype(BF16)
        dyat_ref[...] = dy_att.astype(BF16)
        dgt_ref[:, :D_MODEL] = (d * y_hg * s1 * (1.0 - s1)).astype(BF16)
        dgt_ref[:, D_MODEL:] = (d * y_att * s2 * (1.0 - s2)).astype(BF16)
        dyh_ref[...] = _dot(dy_hg, wbh_ref[...], "nt")
        doa_ref[...] = _dot(dy_att, wba_ref[...], "nt")

        @pl.when(i == 0)
        def _():
            dg_ref[...] = dg
            db_ref[...] = db

        @pl.when(i > 0)
        def _():
            dg_ref[...] += dg
            db_ref[...] += db

    row = lambda w: pl.BlockSpec((tr, w), lambda i: (i, 0))
    const = lambda a: pl.BlockSpec(a.shape, lambda i: (0, 0))
    vec = pl.BlockSpec((1, D_MODEL), lambda i: (0, 0))
    return pl.pallas_call(
        body, name="ln1_mix_bwd", grid=(p // tr,),
        in_specs=[row(D_MODEL), row(D_MODEL), row(D_MODEL), row(1), vec, row(HG_W), row(ATT_QW), row(2 * D_MODEL),
                  const(w_bh), const(w_ba), const(w_out)],
        out_specs=[row(D_MODEL), row(D_MODEL), row(D_MODEL), row(2 * D_MODEL), row(HG_W), row(ATT_QW), vec, vec],
        out_shape=[jax.ShapeDtypeStruct((p, D_MODEL), F32), jax.ShapeDtypeStruct((p, D_MODEL), BF16),
                   jax.ShapeDtypeStruct((p, D_MODEL), BF16), jax.ShapeDtypeStruct((p, 2 * D_MODEL), BF16),
                   jax.ShapeDtypeStruct((p, HG_W), F32), jax.ShapeDtypeStruct((p, ATT_QW), F32),
                   jax.ShapeDtypeStruct((1, D_MODEL), F32), jax.ShapeDtypeStruct((1, D_MODEL), F32)],
        compiler_params=_cparams(("arbitrary",)),
    )(dr2, dh1_ffn, xhat1, rstd1, g1, yh, oa, gates, w_bh, w_ba, w_out)


MIX_W = 4 * HG_W + ATT_QW + 2 * ATT_KVW


def _mixers_bwd(proj_hg, proj_att, lbounds, norm_g, lv, states, cos, sin, sinks, dyh, doa, parts, swapped):
    p = proj_hg.shape[0]
    nb = p // BLOCK
    n = len(parts)
    kvw = 2 * ATT_KVW
    rev = lambda s: nb - 1 - s
    c_in, c_out, c_shapes, c_sems = _comm_specs(parts, N_PEERS)

    def body(*refs):
        (x_ref, lb_ref, ng_ref, lv_ref, st_ref, cur_ref, prev_ref, meta_ref, cc, sc, cp, sp, cm, sm, sink_ref,
         dy_ref, do_ref) = refs[:17]
        part_refs = refs[17:17 + n]
        dx_ref, dlb_ref, dng_ref, dsink_ref = refs[17 + n:21 + n]
        recv_refs = refs[21 + n:21 + 2 * n]
        dcarry_ref, dkv_next_ref, dkv_meta_ref = refs[21 + 2 * n:24 + 2 * n]
        starts, waits = _scatter_behind(part_refs, recv_refs, refs[24 + 2 * n:], swapped)
        step = pl.program_id(0)
        c = rev(step)

        @pl.when(step == 0)
        def _():
            dcarry_ref[...] = jnp.zeros_like(dcarry_ref)
            dkv_next_ref[...] = jnp.zeros_like(dkv_next_ref)
            dkv_meta_ref[...] = jnp.zeros_like(dkv_meta_ref)
            dlb_ref[...] = jnp.zeros_like(dlb_ref)
            dng_ref[...] = jnp.zeros_like(dng_ref)
            dsink_ref[...] = jnp.zeros_like(dsink_ref)
            for start in starts:
                start()

        fh = _first_half(BLOCK)
        qs, kc, vc = _att_load(cur_ref, cc, sc, True)
        _, kp, vp = _att_load(prev_ref, cp, sp, False)
        km, vm = _att_load_meta(meta_ref, cm, sm)
        s0, s1 = _att_sinkrows(sink_ref)
        masks = _att_masks(c)
        _, att_vjp = jax.vjp(lambda *a: _att_core(*a, *masks), *qs, km, kp, kc, vm, vp, vc, s0, s1)
        dq0, dq1, dq2, dq3, dkm, dkp, dkc, dvm, dvp, dvc, ds0, ds1 = att_vjp(do_ref[...])
        att0 = 4 * HG_W
        for j, dq in enumerate((dq0, dq1, dq2, dq3)):
            dx_ref[:, att0 + j * BLOCK:att0 + (j + 1) * BLOCK] = _rope_t(dq, cc[...], sc[...], fh).astype(BF16)
        dkv_meta_ref[:, :BLOCK] += _rope_t(dkm, cm[PAD:BLOCK, :], sm[PAD:BLOCK, :], _first_half(N_META))
        dkv_meta_ref[:, BLOCK:] += dvm
        last = jnp.where(c == 0, 1.0, 0.0)
        to_meta_rows = lambda m: jnp.concatenate([jnp.zeros((PAD, BLOCK), F32), last * m], axis=0)
        dk = _rope_t(dkc, cc[...], sc[...], fh) + dkv_next_ref[:, :BLOCK] + to_meta_rows(dkv_meta_ref[:, :BLOCK])
        dv = dvc + dkv_next_ref[:, BLOCK:] + to_meta_rows(dkv_meta_ref[:, BLOCK:])
        dx_ref[:, att0 + ATT_QW:att0 + ATT_QW + ATT_KVW] = dk.astype(BF16)
        dx_ref[:, att0 + ATT_QW + ATT_KVW:] = dv.astype(BF16)
        dkv_next_ref[:, :BLOCK] = _rope_t(dkp, cp[...], sp[...], fh)
        dkv_next_ref[:, BLOCK:] = dvp
        sink_rows = []
        for dsg in (ds0, ds1):
            for j in range(4):
                tot = jnp.sum(dsg[:, j * BLOCK:(j + 1) * BLOCK], axis=1, keepdims=True)
                sink_rows.append(jnp.broadcast_to(tot, (1, BLOCK)))
        dsink_ref[...] += jnp.concatenate(sink_rows, axis=0)

        valid = (c * BLOCK + lax.broadcasted_iota(jnp.int32, (BLOCK, 1), 0)) >= PAD
        (logf, k), gates_vjp = jax.vjp(lambda hf, a0, a1: _hgrn_gates(hf, a0, a1, valid),
                                       x_ref[:, HG_W:2 * HG_W], lb_ref[0:1, :], lb_ref[1:2, :])
        lvv = lv_ref[...]
        e = _split_dot(lvv, logf, "nn")
        dng = jnp.zeros((1, BLOCK), F32)
        dk, dseg = [], []
        for h in range(HG_HEADS):
            sl = lambda part: x_ref[:, part * HG_W + h * BLOCK: part * HG_W + (h + 1) * BLOCK]
            hs = slice(h * BLOCK, (h + 1) * BLOCK)
            _, vjp = jax.vjp(_hgrn_head, sl(0), k[:, hs], sl(2), sl(3), ng_ref[...], st_ref[0, h], *_seg_blocks(e, h))
            dhq, dkh, dhi, dhg, dngh, dst, *dsegh = vjp((dy_ref[:, hs], dcarry_ref[h]))
            for part, val in ((0, dhq), (2, dhi), (3, dhg)):
                dx_ref[:, part * HG_W + h * BLOCK: part * HG_W + (h + 1) * BLOCK] = val.astype(BF16)
            dk.append(dkh)
            dseg.append(jnp.concatenate(dsegh, axis=0))
            dng = dng + dngh
            dcarry_ref[h] = dst
        dlogf = _split_dot(lvv, jnp.concatenate(dseg, axis=1), "tn")
        dhf, da0, da1 = gates_vjp((dlogf, jnp.concatenate(dk, axis=1)))
        dx_ref[:, HG_W:2 * HG_W] = dhf.astype(BF16)
        dlb_ref[0:1, :] += da0
        dlb_ref[1:2, :] += da1
        dng_ref[...] += dng

        @pl.when(step == nb - 1)
        def _():
            for wait in waits:
                wait()

    const = lambda shape: pl.BlockSpec(shape, lambda s: (0,) * len(shape))
    return pl.pallas_call(
        body, name="mixers_bwd", grid=(nb,),
        in_specs=[pl.BlockSpec((BLOCK, 4 * HG_W), lambda s: (rev(s), 0)), const((2, HG_W)), const((1, BLOCK)),
                  const(lv.shape), pl.BlockSpec((1, HG_HEADS, BLOCK, BLOCK), lambda s: (rev(s), 0, 0, 0))]
        + _att_specs(rev)
        + [pl.BlockSpec((BLOCK, HG_W), lambda s: (rev(s), 0)), pl.BlockSpec((BLOCK, ATT_QW), lambda s: (rev(s), 0))]
        + c_in,
        out_specs=[pl.BlockSpec((BLOCK, MIX_W), lambda s: (rev(s), 0)), const((2, HG_W)), const((1, BLOCK)),
                   const((ATT_HEADS, BLOCK))] + c_out,
        out_shape=[jax.ShapeDtypeStruct((p, MIX_W), BF16), jax.ShapeDtypeStruct((2, HG_W), F32),
                   jax.ShapeDtypeStruct((1, BLOCK), F32), jax.ShapeDtypeStruct((ATT_HEADS, BLOCK), F32)] + c_shapes,
        scratch_shapes=[pltpu.VMEM((HG_HEADS, BLOCK, BLOCK), F32), pltpu.VMEM((BLOCK, kvw), F32),
                        pltpu.VMEM((N_META, kvw), F32)] + c_sems,
        compiler_params=_cparams(("arbitrary",)),
    )(proj_hg, lbounds, norm_g, lv, states, proj_att, proj_att, proj_att, cos, sin, cos, sin, cos, sin, sinks,
      dyh, doa, *parts)


def _d_h0_proj(dmix, dgates, w_mix, w_gates, parts, swapped):
    p = dmix.shape[0]
    tm = _row_tile(p, 640)
    nm = p // tm
    n = len(parts)
    c_in, c_out, c_shapes, c_sems = _comm_specs(parts, N_PEERS)

    def body(*refs):
        a_ref, g_ref, wa_ref, wg_ref = refs[:4]
        o_ref = refs[4 + n]
        starts, waits = _scatter_behind(refs[4:4 + n], refs[5 + n:5 + 2 * n], refs[5 + 2 * n:], swapped)
        i = pl.program_id(0)

        @pl.when(i == 0)
        def _():
            for start in starts:
                start()

        o_ref[...] = _dot(a_ref[...], wa_ref[...], "nt") + _dot(g_ref[...], wg_ref[...], "nt")

        @pl.when(i == nm - 1)
        def _():
            for wait in waits:
                wait()

    row = lambda w: pl.BlockSpec((tm, w), lambda i: (i, 0))
    const = lambda a: pl.BlockSpec(a.shape, lambda i: (0, 0))
    return pl.pallas_call(
        body, name="d_h0_proj", grid=(nm,),
        in_specs=[row(dmix.shape[1]), row(dgates.shape[1]), const(w_mix), const(w_gates)] + c_in,
        out_specs=[row(D_MODEL)] + c_out,
        out_shape=[jax.ShapeDtypeStruct((p, D_MODEL), F32)] + c_shapes,
        scratch_shapes=c_sems, compiler_params=_cparams(("arbitrary",)),
    )(dmix, dgates, w_mix, w_gates, *parts)


_LATE = ("w_branch_hg", "w_branch_attn", "w_out", "w_ffn_in", "w_ffn_out")
_COLUMN_SHARDED = ("meta_tokens", "w_in", "w_branch_hg", "w_branch_attn", "w_ffn_in")
_SWAPPED = ("w_ffn_in",)


def _whole(name, gathered):
    _, r, c = gathered.shape
    if name in _COLUMN_SHARDED:
        return jnp.transpose(gathered, (1, 0, 2)).reshape(r, N_DEV * c)
    return gathered.reshape(N_DEV * r, c)


def _slots(name, whole):
    r, c = whole.shape
    if name in _COLUMN_SHARDED:
        return jnp.transpose(whole.reshape(r, N_DEV, c // N_DEV), (1, 0, 2))
    return whole.reshape(N_DEV, r // N_DEV, c)


def _device_step(x, target, meta, ln_emb_g, ln_emb_b, w_in, lbounds, norm_g, sinks, late_shards,
                 ln1_g, ln1_b, ln2_g, ln2_b):
    s = x.shape[0]
    p = s + BLOCK
    lead = jnp.concatenate([jnp.zeros((PAD, D_MODEL), F32), meta], axis=0)
    tm = _row_tile(p, 640)
    lv = _level_stack()
    cos, sin = _rope_tables(p)
    hg_end = 4 * HG_W
    mm = functools.partial(_tiled_matmul, tm=tm)
    swapped = [n in _SWAPPED for n in _LATE]

    h0, h0b, xhat0, rstd0 = _embed_ln(x, lead, ln_emb_g, ln_emb_b)
    proj_hg = mm(h0b, w_in[:, :hg_end], "nn", tn=hg_end, tc=D_MODEL, out_dtype=F32, name="proj_hg")
    proj_att = mm(h0b, w_in[:, hg_end:MIX_W], "nn", tn=MIX_W - hg_end, tc=D_MODEL, out_dtype=F32, name="proj_att")
    gates = mm(h0b, w_in[:, MIX_W:], "nn", tn=2 * D_MODEL, tc=D_MODEL, out_dtype=F32, name="proj_gates")
    yh, states, oa, *gathered = _mixers_fwd(proj_hg, proj_att, lbounds, norm_g, lv, cos, sin, sinks, late_shards, swapped)
    w_bh, w_ba, w_out, w_fi, w_fo = [_whole(n, g) for n, g in zip(_LATE, gathered)]
    mixin, h1, h1b, xhat1, rstd1 = _mix_out_ln1(yh, oa, gates, h0, w_bh, w_ba, w_out, ln1_g, ln1_b)
    au, sw = _ffn_in_swiglu(h1b, w_fi)
    dr2, loss_part, dg2, db2 = _ffn_out_loss(sw, w_fo, h1, ln2_g, ln2_b, target)

    mtn = functools.partial(_tiled_matmul_tn, tm=_row_tile(p, 1664), out_dtype=BF16)
    d_wfo = mtn(sw, dr2, tk=FF_T, tn=D_MODEL, name="grad_w_ffn_out")
    dau = _d_ffn_hidden(dr2, w_fo, au)
    d_wfi = mtn(h1b, dau, tk=D_MODEL, tn=FF_T, name="grad_w_ffn_in")
    dh1_ffn = mm(dau, w_fi, "nt", tn=D_MODEL, tc=D_FF, out_dtype=F32, name="d_h1_ffn")
    dr1, dy_hg, dy_att, dgates, dyh, doa, dg1, db1 = _ln1_mix_bwd(
        dr2, dh1_ffn, xhat1, rstd1, ln1_g, yh, oa, gates, w_bh, w_ba, w_out)
    d_wout = mtn(mixin, dr1, tk=D_MODEL, tn=D_MODEL, name="grad_w_out")
    d_wbh = mtn(yh, dy_hg, tk=HG_W, tn=D_MODEL, name="grad_w_branch_hg")
    d_wba = mtn(oa, dy_att, tk=ATT_QW, tn=D_MODEL, name="grad_w_branch_attn")
    late_parts = [_slots(n, g) for n, g in zip(_LATE, (d_wbh, d_wba, d_wout, d_wfi, d_wfo))]
    dmix, d_lb, d_ng, d_sink, *late_recv = _mixers_bwd(
        proj_hg, proj_att, lbounds, norm_g, lv, states, cos, sin, sinks, dyh, doa, late_parts, swapped)
    d_win = jnp.concatenate([mtn(h0b, dmix, tk=D_MODEL, tn=MIX_W // 2, name="grad_w_in_mixers"),
                             mtn(h0b, dgates, tk=D_MODEL, tn=D_MODEL, name="grad_w_in_gates")], axis=1)
    win_parts = _slots("w_in", d_win)
    dh0_proj, win_recv = _d_h0_proj(dmix, dgates, w_in[:, :MIX_W], w_in[:, MIX_W:], [win_parts], [False])
    dxin, dg0, db0 = _ln_bwd_call(dr1, dh0_proj, ALPHA, xhat0, rstd0, ln_emb_g, "embed_ln_bwd", mask_from=PAD)

    small = dict(ln_emb_g=dg0, ln_emb_b=db0, hg_lower_bounds=d_lb, hg_norm_g=d_ng, attn_sinks=d_sink[:, 0],
                 ln1_g=dg1, ln1_b=db1, ln2_g=dg2, ln2_b=db2)
    big = dict(zip(_LATE, zip(late_parts, late_recv)))
    big["w_in"] = (win_parts, win_recv)
    return loss_part, dxin[BLOCK:], small, dxin[PAD:BLOCK], big


def _all_gather(arrs, dtypes, name):
    n = len(arrs)

    def body(*refs):
        ins, outs, stages = refs[:n], refs[n:2 * n], refs[2 * n:3 * n]
        send_sems, recv_sems, local_sems = refs[3 * n:]
        x, y, c = _place()
        sibling = (x, y, 1 - c)
        chips = [(1 - x, y), (x, 1 - y), (1 - x, 1 - y)]
        slot = lambda px, py, pc: 4 * px + 2 * py + pc

        def copy(w, k, block, to, from_stage=False):
            return pltpu.make_async_remote_copy(
                src_ref=stages[w] if from_stage else outs[w].at[slot(*block)], dst_ref=outs[w].at[slot(*block)],
                send_sem=send_sems.at[w, k], recv_sem=recv_sems.at[w, k], device_id=to, device_id_type=MESH)

        mine, first, passed = [], [], []
        for w in range(n):
            stages[w][...] = ins[w][...].astype(dtypes[w])
            mine.append(pltpu.make_async_copy(stages[w], outs[w].at[slot(x, y, c)], local_sems.at[w]))
            mine[-1].start()
        for w in range(n):
            first.append(copy(w, 0, (x, y, c), sibling, from_stage=True))
            first += [copy(w, 1 + j, (x, y, c), (*chip, c), from_stage=True) for j, chip in enumerate(chips)]
        for cp in first:
            cp.start()
        for j, chip in enumerate(chips):
            for w in range(n):
                copy(w, 1 + j, (*chip, c), (x, y, c)).wait_recv()
                passed.append(copy(w, 4 + j, (*chip, c), sibling))
                passed[-1].start()
        for w in range(n):
            copy(w, 0, sibling, (x, y, c)).wait_recv()
            for j, chip in enumerate(chips):
                copy(w, 4 + j, (*chip, 1 - c), (x, y, c)).wait_recv()
        for cp in first + passed:
            cp.wait_send()
        for cp in mine:
            cp.wait()

    return pl.pallas_call(
        body, name=name,
        in_specs=[pl.BlockSpec(memory_space=pltpu.VMEM)] * n,
        out_specs=[pl.BlockSpec(memory_space=pl.ANY)] * n,
        out_shape=[jax.ShapeDtypeStruct((N_DEV,) + a.shape, dt) for a, dt in zip(arrs, dtypes)],
        scratch_shapes=[pltpu.VMEM(a.shape, dt) for a, dt in zip(arrs, dtypes)]
        + [pltpu.SemaphoreType.DMA((n, 7)), pltpu.SemaphoreType.DMA((n, 7)), pltpu.SemaphoreType.DMA((n,))],
        compiler_params=pltpu.CompilerParams(vmem_limit_bytes=VMEM_LIMIT_BYTES),
    )(*arrs)


def _cast_shards(arrs):
    def body(*refs):
        for src, dst in zip(refs[:len(arrs)], refs[len(arrs):]):
            dst[...] = src[...].astype(BF16)

    return pl.pallas_call(body, name="cast_shards", out_shape=[jax.ShapeDtypeStruct(a.shape, BF16) for a in arrs],
                          compiler_params=pltpu.CompilerParams(vmem_limit_bytes=VMEM_LIMIT_BYTES))(*arrs)


def _shard_rows(rows):
    return rows if rows <= 512 else 256


def _adamw_math(w, g, m, v):
    m = ADAM_B1 * m + (1.0 - ADAM_B1) * g
    v = ADAM_B2 * v + (1.0 - ADAM_B2) * (g * g)
    m_hat = m / (1.0 - ADAM_B1 ** ADAM_STEP)
    v_hat = v / (1.0 - ADAM_B2 ** ADAM_STEP)
    delta = -ADAM_LR * (m_hat / (jnp.sqrt(v_hat) + ADAM_EPS) + ADAM_WD * w)
    return delta, m, v


def _reduce_adamw(parts, recv, own_slot, w, m, v, name):
    r, cdim = w.shape
    tr = _shard_rows(r)

    def body(idx_ref, p_ref, r_ref, w_ref, m_ref, v_ref, g_out, d_out, m_out, v_out):
        g = p_ref[0].astype(F32)
        for j in range(N_PEERS):
            g = g + r_ref[j].astype(F32)
        d, mn, vn = _adamw_math(w_ref[...], g, m_ref[...], v_ref[...])
        g_out[...] = g
        d_out[...] = d
        m_out[...] = mn
        v_out[...] = vn

    flat = pl.BlockSpec((tr, cdim), lambda i, idx_ref: (i, 0))
    return pl.pallas_call(
        body, name=name,
        grid_spec=pltpu.PrefetchScalarGridSpec(
            num_scalar_prefetch=1, grid=(r // tr,),
            in_specs=[pl.BlockSpec((1, tr, cdim), lambda i, idx_ref: (idx_ref[0], i, 0)),
                      pl.BlockSpec((N_PEERS, tr, cdim), lambda i, idx_ref: (0, i, 0)), flat, flat, flat],
            out_specs=[flat] * 4),
        out_shape=[jax.ShapeDtypeStruct((r, cdim), F32)] * 4,
        compiler_params=_cparams(("arbitrary",)),
    )(own_slot, parts, recv, w, m, v)


def _adamw_plain(w, g, m, v, name):
    def body(w_ref, g_ref, m_ref, v_ref, d_out, m_out, v_out):
        d_out[...], m_out[...], v_out[...] = _adamw_math(w_ref[...], g_ref[...], m_ref[...], v_ref[...])

    return pl.pallas_call(body, name=name, out_shape=[jax.ShapeDtypeStruct(w.shape, F32)] * 3)(w, g, m, v)


_SMALL_LAYOUT = (("ln_emb_g", 8), ("ln_emb_b", 8), ("hg_lower_bounds", 8), ("hg_norm_g", 1), ("attn_sinks", 1),
                 ("ln1_g", 8), ("ln1_b", 8), ("ln2_g", 8), ("ln2_b", 8))
_META_ROW = sum(r for _, r in _SMALL_LAYOUT)
_META_ROWS = N_META * D_MODEL // BLOCK
_LOSS_ROW = _META_ROW + _META_ROWS
SMALL_ROWS = 192


def _pack_small(vals, meta=None, loss_row=None):
    rows = []
    for name, nrows in _SMALL_LAYOUT:
        flat = vals[name].reshape(-1).astype(F32)
        flat = jnp.pad(flat, (0, nrows * BLOCK - flat.shape[0]))
        rows.append(flat.reshape(nrows, BLOCK))
    rows.append(jnp.zeros((_META_ROWS, BLOCK), F32) if meta is None else meta.reshape(_META_ROWS, BLOCK))
    rows.append(jnp.zeros((1, BLOCK), F32) if loss_row is None else loss_row)
    packed = jnp.concatenate(rows, axis=0)
    return jnp.pad(packed, ((0, SMALL_ROWS - packed.shape[0]), (0, 0)))


def _unpack_small(packed, shapes):
    out, row = {}, 0
    for name, nrows in _SMALL_LAYOUT:
        size = math.prod(shapes[name])
        out[name] = packed[row:row + nrows].reshape(-1)[:size].reshape(shapes[name])
        row += nrows
    return out


def _small_reduce_adamw(gathered, w, m, v):
    def body(g_ref, w_ref, m_ref, v_ref, g_out, d_out, m_out, v_out, loss_out):
        g = g_ref[0]
        for s in range(1, N_DEV):
            g = g + g_ref[s]
        d, mn, vn = _adamw_math(w_ref[...], g, m_ref[...], v_ref[...])
        g_out[...] = g
        d_out[...] = d
        m_out[...] = mn
        v_out[...] = vn
        loss_out[...] = jnp.broadcast_to(jnp.sum(g_ref[:, _LOSS_ROW, :]), (1, BLOCK))

    shp = jax.ShapeDtypeStruct((SMALL_ROWS, BLOCK), F32)
    return pl.pallas_call(body, name="small_reduce_adamw",
                          out_shape=[shp] * 4 + [jax.ShapeDtypeStruct((1, BLOCK), F32)])(gathered, w, m, v)


_WEIGHTS = ("meta_tokens", "ln_emb_g", "ln_emb_b", "w_in", "hg_lower_bounds", "hg_norm_g", "attn_sinks",
            "w_branch_hg", "w_branch_attn", "w_out", "ln1_g", "ln1_b", "w_ffn_in", "w_ffn_out", "ln2_g", "ln2_b")


def kernel(x, meta_tokens, ln_emb_g, ln_emb_b, w_in, hg_lower_bounds, hg_norm_g, attn_sinks, w_branch_hg, w_branch_attn, w_out, ln1_g, ln1_b, w_ffn_in, w_ffn_out, ln2_g, ln2_b, loss_target, m_meta_tokens, m_ln_emb_g, m_ln_emb_b, m_w_in, m_hg_lower_bounds, m_hg_norm_g, m_attn_sinks, m_w_branch_hg, m_w_branch_attn, m_w_out, m_ln1_g, m_ln1_b, m_w_ffn_in, m_w_ffn_out, m_ln2_g, m_ln2_b, v_meta_tokens, v_ln_emb_g, v_ln_emb_b, v_w_in, v_hg_lower_bounds, v_hg_norm_g, v_attn_sinks, v_w_branch_hg, v_w_branch_attn, v_w_out, v_ln1_g, v_ln1_b, v_w_ffn_in, v_w_ffn_out, v_ln2_g, v_ln2_b):
    given = dict(locals())
    weights = {n: given[n] for n in _WEIGHTS}
    mom1 = {n: given["m_" + n] for n in _WEIGHTS}
    mom2 = {n: given["v_" + n] for n in _WEIGHTS}
    shard2d = lambda a: a.reshape(a.shape[-2:])

    g_meta, g_win = _all_gather([meta_tokens, shard2d(w_in)], [F32, BF16], "gather_first")
    late_shards = _cast_shards([shard2d(weights[n]) for n in _LATE])

    loss_part, grad_x, small_grads, meta_grad, big = _device_step(
        x[0], loss_target[0], _whole("meta_tokens", g_meta), ln_emb_g.reshape(1, -1), ln_emb_b.reshape(1, -1),
        _whole("w_in", g_win), hg_lower_bounds, hg_norm_g, attn_sinks, late_shards, ln1_g, ln1_b, ln2_g, ln2_b)

    place = _place()
    out = {}
    for n, (parts, recv) in big.items():
        own = _slot(place, n in _SWAPPED).astype(jnp.int32).reshape(1)
        res = _reduce_adamw(parts, recv, own, shard2d(weights[n]), shard2d(mom1[n]), shard2d(mom2[n]), "adamw_" + n)
        out[n] = [r.reshape(weights[n].shape) for r in res]

    small_names = [n for n, _ in _SMALL_LAYOUT]
    packed = _pack_small(small_grads, meta_grad, loss_part)
    all_small, = _all_gather([packed], [F32], "gather_small")
    res = _small_reduce_adamw(all_small, _pack_small(weights), _pack_small(mom1), _pack_small(mom2))
    shapes = {n: weights[n].shape for n in small_names}
    unpacked = [_unpack_small(r, shapes) for r in res[:4]]
    for n in small_names:
        out[n] = [u[n] for u in unpacked]
    loss = res[4][0, 0]
    meta_whole = res[0][_META_ROW:_META_ROW + _META_ROWS].reshape(N_META, N_DEV, D_MODEL // N_DEV)
    g_meta_mine = lax.dynamic_index_in_dim(meta_whole, _slot(place, False), axis=1, keepdims=False)
    out["meta_tokens"] = [g_meta_mine, *_adamw_plain(meta_tokens, g_meta_mine, m_meta_tokens, v_meta_tokens,
                                                     "adamw_meta")]

    return (loss, grad_x[None], *[out[n][0] for n in _WEIGHTS], *[out[n][1] for n in _WEIGHTS],
            *[out[n][2] for n in _WEIGHTS], *[out[n][3] for n in _WEIGHTS])
```

```python
import functools
import math

import numpy as np
import jax
import jax.numpy as jnp
from jax import lax
from jax.experimental import pallas as pl
from jax.experimental.pallas import tpu as pltpu

F32 = jnp.float32
BF16 = jnp.bfloat16

D_MODEL = 1024
N_META = 16
BLOCK = 128
PAD = BLOCK - N_META
HG_HEADS = 4
HG_W = 512
ATT_HEADS = 8
HEAD_DIM = 64
ATT_QW = 512
ATT_KVW = 128
D_FF = 2816
EPS = 1e-5
ALPHA = 2.0 ** 0.25
ROPE_THETA = 10000.0
N_DEV = 8

ADAM_LR = 0.001
ADAM_B1 = 0.9
ADAM_B2 = 0.999
ADAM_EPS = 1e-08
ADAM_WD = 0.01
ADAM_STEP = 10

VMEM_LIMIT_BYTES = 56 * 1024 * 1024
MESH = pl.DeviceIdType.MESH

_LEVELS = (64, 32, 16, 8, 4, 2, 1)


def _cparams(sem):
    return pltpu.CompilerParams(dimension_semantics=sem, vmem_limit_bytes=VMEM_LIMIT_BYTES)


def _row_tile(rows, target):
    nb = rows // BLOCK
    best = 1
    for d in range(1, nb + 1):
        if nb % d == 0 and d * BLOCK <= target:
            best = d
    return best * BLOCK


_DN = {"nn": (((1,), (0,)), ((), ())), "nt": (((1,), (1,)), ((), ())), "tn": (((0,), (0,)), ((), ()))}


def _dot(a, b, form):
    return lax.dot_general(a.astype(BF16), b.astype(BF16), _DN[form], preferred_element_type=F32)


@functools.partial(jax.custom_vjp, nondiff_argnums=(2,))
def _mm(a, b, form):
    return _dot(a, b, form)


def _mm_fwd(a, b, form):
    a, b = a.astype(BF16), b.astype(BF16)
    return _dot(a, b, form), (a, b)


def _mm_bwd(form, res, g):
    a, b = res
    if form == "nn":
        return _dot(g, b, "nt"), _dot(a, g, "tn")
    if form == "nt":
        return _dot(g, b, "nn"), _dot(g, a, "tn")
    return _dot(b, g, "nt"), _dot(a, g, "nn")


_mm.defvjp(_mm_fwd, _mm_bwd)


def _split_dot(lv, x, form):
    return lax.dot_general(lv, x.astype(BF16), _DN[form], preferred_element_type=F32)


@jax.custom_vjp
def _swap_halves(x):
    return pltpu.roll(x, 64, 1)


_swap_halves.defvjp(lambda x: (pltpu.roll(x, 64, 1), None), lambda _, g: (pltpu.roll(g, 64, 1),))


def _tiled_matmul(a, b, form, *, tm, tn, tc, out_dtype, name):
    m, c = a.shape
    n = b.shape[1] if form == "nn" else b.shape[0]
    assert m % tm == 0 and n % tn == 0 and c % tc == 0, (name, a.shape, b.shape, tm, tn, tc)
    nc = c // tc

    def body(a_ref, b_ref, o_ref, *scratch):
        part = _dot(a_ref[...], b_ref[...], form)
        if nc == 1:
            o_ref[...] = part.astype(out_dtype)
            return
        acc_ref, = scratch
        ci = pl.program_id(2)

        @pl.when(ci == 0)
        def _():
            acc_ref[...] = part

        @pl.when(ci > 0)
        def _():
            acc_ref[...] += part

        @pl.when(ci == nc - 1)
        def _():
            o_ref[...] = acc_ref[...].astype(out_dtype)

    b_spec = (pl.BlockSpec((tc, tn), lambda j, i, k: (k, j)) if form == "nn"
              else pl.BlockSpec((tn, tc), lambda j, i, k: (j, k)))
    return pl.pallas_call(
        body, name=name, grid=(n // tn, m // tm, nc),
        in_specs=[pl.BlockSpec((tm, tc), lambda j, i, k: (i, k)), b_spec],
        out_specs=pl.BlockSpec((tm, tn), lambda j, i, k: (i, j)),
        out_shape=jax.ShapeDtypeStruct((m, n), out_dtype),
        scratch_shapes=[] if nc == 1 else [pltpu.VMEM((tm, tn), F32)],
        compiler_params=_cparams(("arbitrary", "arbitrary", "arbitrary")),
    )(a, b)


def _tiled_matmul_tn(a, b, *, tm, tk, tn, out_dtype, name):
    m, k = a.shape
    n = b.shape[1]
    assert m % tm == 0 and k % tk == 0 and n % tn == 0, (name, a.shape, b.shape, tm, tk, tn)
    nm = m // tm

    def body(a_ref, b_ref, o_ref, acc_ref):
        part = _dot(a_ref[...], b_ref[...], "tn")
        mi = pl.program_id(2)

        @pl.when(mi == 0)
        def _():
            acc_ref[...] = part

        @pl.when(mi > 0)
        def _():
            acc_ref[...] += part

        @pl.when(mi == nm - 1)
        def _():
            o_ref[...] = acc_ref[...].astype(out_dtype)

    return pl.pallas_call(
        body, name=name, grid=(k // tk, n // tn, nm),
        in_specs=[pl.BlockSpec((tm, tk), lambda kk, j, i: (i, kk)), pl.BlockSpec((tm, tn), lambda kk, j, i: (i, j))],
        out_specs=pl.BlockSpec((tk, tn), lambda kk, j, i: (kk, j)),
        out_shape=jax.ShapeDtypeStruct((k, n), out_dtype),
        scratch_shapes=[pltpu.VMEM((tk, tn), F32)],
        compiler_params=_cparams(("arbitrary", "arbitrary", "arbitrary")),
    )(a, b)


def _ln_stats(r):
    mu = jnp.mean(r, axis=-1, keepdims=True)
    xc = r - mu
    var = jnp.mean(xc * xc, axis=-1, keepdims=True)
    rstd = lax.rsqrt(var + EPS)
    return xc * rstd, rstd


def _ln_bwd(dy, xhat, rstd, g):
    dxhat = dy * g
    m1 = jnp.mean(dxhat, axis=-1, keepdims=True)
    m2 = jnp.mean(dxhat * xhat, axis=-1, keepdims=True)
    dr = rstd * (dxhat - m1 - xhat * m2)
    return dr, jnp.sum(dy * xhat, axis=0, keepdims=True), jnp.sum(dy, axis=0, keepdims=True)


N_SEG = 3 + len(_LEVELS)


def _level_stack():
    t = np.arange(BLOCK)[:, None]
    r = np.arange(BLOCK)[None, :]
    mats = [r <= t, r > t, np.ones((BLOCK, BLOCK), bool)]
    for h in _LEVELS:
        same = (t // (2 * h)) == (r // (2 * h))
        up_t, up_r = (t % (2 * h)) >= h, (r % (2 * h)) >= h
        mats.append(same & ((up_t & up_r & (r <= t)) | (~up_t & ~up_r & (r > t))))
    return jnp.asarray(np.concatenate(mats, axis=0).astype(np.float32), dtype=BF16)


def _hgrn_gates(hf, a0, a1, valid):
    lb = jax.nn.sigmoid(a0 - a1)
    fg = lb + (1.0 - lb) * jax.nn.sigmoid(hf)
    return jnp.where(valid, jnp.log(fg), 0.0), jnp.where(valid, 1.0 - fg, 0.0)


def _hgrn_head(hq, k, v, hg, ng, st_in, *seg):
    q = jax.nn.silu(hq)
    rows = lax.broadcasted_iota(jnp.int32, (BLOCK, BLOCK), 0)
    cols = lax.broadcasted_iota(jnp.int32, (BLOCK, BLOCK), 1)
    o = _mm(q * jnp.exp(seg[0]), st_in, "nt")
    a = jnp.where(rows == cols, jnp.sum(q * k, axis=-1, keepdims=True), 0.0)
    for li, h in enumerate(_LEVELS):
        decay = jnp.exp(seg[3 + li])
        pair = ((rows // (2 * h)) == (cols // (2 * h))) & ((rows % (2 * h)) >= h) & ((cols % (2 * h)) < h)
        a = a + jnp.where(pair, _mm(q * decay, k * decay, "nt"), 0.0)
    o = o + _mm(a, v, "nn")
    st_out = st_in * jnp.exp(seg[2]) + _mm(v, k * jnp.exp(seg[1]), "tn")
    on = o * lax.rsqrt(jnp.mean(o * o, axis=-1, keepdims=True) + EPS) * ng
    return on * jax.nn.silu(hg), st_out


def _seg_blocks(e, h):
    return [e[i * BLOCK:(i + 1) * BLOCK, h * BLOCK:(h + 1) * BLOCK] for i in range(N_SEG)]


def _rope(x, cos, sin, first_half):
    partner = jnp.where(first_half, -pltpu.roll(x, 96, 1), pltpu.roll(x, 32, 1))
    return x * cos + partner * sin


def _rope_t(g, cos, sin, first_half):
    u = g * sin
    partner = jnp.where(first_half, pltpu.roll(u, 96, 1), -pltpu.roll(u, 32, 1))
    return g * cos + partner


def _att_core(q0, q1, q2, q3, km, kp, kc, vm, vp, vc, sinkrow0, sinkrow1, own_side, ok_band, ok_meta):
    low = lambda x: lax.broadcasted_iota(jnp.int32, x.shape, 1) < HEAD_DIM
    scale = HEAD_DIM ** -0.5
    neg = jnp.finfo(F32).min
    wide = lambda m: jnp.concatenate([m] * 4, axis=1)
    own4, band4, meta4 = wide(own_side), wide(ok_band), wide(ok_meta)
    outs = []
    for g, (qa, qb, sinkrow) in enumerate(((q0, q1, sinkrow0), (q2, q3, sinkrow1))):
        def both(x, g=g):
            sw = _swap_halves(x)
            return jnp.where(low(x), x, sw) if g == 0 else jnp.where(low(x), sw, x)
        q4 = jnp.concatenate([jnp.where(low(qa), qa, 0.0), jnp.where(low(qa), 0.0, qa),
                              jnp.where(low(qb), qb, 0.0), jnp.where(low(qb), 0.0, qb)], axis=0)
        s = jnp.where(own4, _mm(both(kc), q4, "nt"), _mm(both(kp), q4, "nt"))
        s = jnp.where(band4, s * scale, neg)
        sm = jnp.where(meta4, _mm(both(km), q4, "nt") * scale, neg)
        mx = jnp.maximum(jnp.maximum(jnp.max(s, axis=0, keepdims=True), jnp.max(sm, axis=0, keepdims=True)), sinkrow)
        mx = lax.stop_gradient(mx)
        p, pm = jnp.exp(s - mx), jnp.exp(sm - mx)
        inv = 1.0 / (jnp.sum(p, axis=0, keepdims=True) + jnp.sum(pm, axis=0, keepdims=True) + jnp.exp(sinkrow - mx))
        p = p * inv
        o4 = (_mm(jnp.where(own4, p, 0.0), both(vc), "tn") + _mm(jnp.where(own4, 0.0, p), both(vp), "tn")
              + _mm(pm * inv, both(vm), "tn"))
        for j in range(2):
            upper = o4[(2 * j) * BLOCK:(2 * j + 1) * BLOCK]
            outs.append(jnp.where(low(upper), upper, o4[(2 * j + 1) * BLOCK:(2 * j + 2) * BLOCK]))
    return jnp.concatenate(outs, axis=1)


def _att_masks(blk_idx):
    kidx = lax.broadcasted_iota(jnp.int32, (BLOCK, BLOCK), 0)
    qrow = lax.broadcasted_iota(jnp.int32, (BLOCK, BLOCK), 1)
    own_side = kidx <= qrow
    pos_own = blk_idx * BLOCK + kidx - PAD
    ok_band = (own_side & (pos_own >= N_META)) | (~own_side & (pos_own - BLOCK >= N_META) & (blk_idx >= 1))
    qpos = blk_idx * BLOCK + lax.broadcasted_iota(jnp.int32, (N_META, BLOCK), 1) - PAD
    ok_meta = lax.broadcasted_iota(jnp.int32, (N_META, BLOCK), 0) <= qpos
    return own_side, ok_band, ok_meta


def _token_streams(tr, tile_of=lambda i: i):
    k = tr // BLOCK
    return [pl.BlockSpec((BLOCK, D_MODEL), lambda i, j=j: (jnp.maximum(k * tile_of(i) - 1 + j, 0), 0))
            for j in range(k)]


def _embed_ln(x, meta_shard, w_in_shard, g0, b0):
    p = x.shape[0] + BLOCK
    tr = _row_tile(p, 640)
    k = tr // BLOCK
    nt = p // tr
    tile_of = lambda s: (s + 1) % nt
    shards = [meta_shard, w_in_shard]
    c_in, c_out, c_shapes, c_sems = _comm_specs(shards, N_DEV)

    def body(*refs):
        g_ref, b_ref = refs[k:k + 2]
        h_ref, hb_ref, xh_ref, rs_ref = refs[k + 4:k + 8]
        out_refs = refs[k + 8:k + 10]
        lead_ref, meta_ref = refs[k + 10:k + 12]
        starts, passes, waits = _gather_behind(refs[k + 2:k + 4], out_refs, refs[k + 12:], [False, False])
        s = pl.program_id(0)
        t = tile_of(s)

        @pl.when(s == 0)
        def _():
            lead_ref[...] = jnp.zeros_like(lead_ref)
            for start in starts:
                start()

        @pl.when(s == nt - 1)
        def _():
            for step in passes + waits:
                step()
            pltpu.sync_copy(out_refs[0], meta_ref)
            for d in range(N_DEV):
                lead_ref[PAD:BLOCK, d * BLOCK:(d + 1) * BLOCK] = meta_ref[d]

        first = jnp.where(t == 0, lead_ref[...], refs[0][...])
        xhat, rstd = _ln_stats(jnp.concatenate([first] + [r[...] for r in refs[1:k]], axis=0))
        row = t * tr + lax.broadcasted_iota(jnp.int32, (tr, 1), 0)
        h = jnp.where(row >= PAD, xhat * g_ref[...] + b_ref[...], 0.0)
        h_ref[...] = h
        hb_ref[...] = h.astype(BF16)
        xh_ref[...] = xhat
        rs_ref[...] = rstd

    vec = pl.BlockSpec((1, D_MODEL), lambda s: (0, 0))
    rowsp = pl.BlockSpec((tr, D_MODEL), lambda s: (tile_of(s), 0))
    return pl.pallas_call(
        body, name="embed_ln", grid=(nt,),
        in_specs=_token_streams(tr, tile_of) + [vec, vec] + c_in,
        out_specs=[rowsp, rowsp, rowsp, pl.BlockSpec((tr, 1), lambda s: (tile_of(s), 0))] + c_out,
        out_shape=[jax.ShapeDtypeStruct((p, D_MODEL), F32), jax.ShapeDtypeStruct((p, D_MODEL), BF16),
                   jax.ShapeDtypeStruct((p, D_MODEL), F32), jax.ShapeDtypeStruct((p, 1), F32)] + c_shapes,
        scratch_shapes=[pltpu.VMEM((BLOCK, D_MODEL), F32), pltpu.VMEM((N_DEV, N_META, BLOCK), F32)] + c_sems,
        compiler_params=_cparams(("arbitrary",)),
    )(*([x] * k), g0, b0, *shards)


def _rope_tables(p):
    pos = (np.arange(p, dtype=np.int32) - PAD).astype(np.float32)
    half = HEAD_DIM // 2
    inv = np.float32(ROPE_THETA) ** (-np.arange(half, dtype=np.float32) / np.float32(half))
    ang = pos[:, None] * np.tile(inv.astype(np.float32), BLOCK // half)[None, :]
    return jnp.asarray(np.cos(ang), F32), jnp.asarray(np.sin(ang), F32)


def _att_sinkrows(sink_ref):
    lanehead = lax.broadcasted_iota(jnp.int32, (1, 4 * BLOCK), 1) // BLOCK
    rows = []
    for g in range(2):
        row = jnp.zeros((1, 4 * BLOCK), F32)
        for j in range(4):
            row = jnp.where(lanehead == j, sink_ref[0, 4 * g + j], row)
        rows.append(row)
    return rows


def _first_half(rows):
    return (lax.broadcasted_iota(jnp.int32, (rows, BLOCK), 1) % HEAD_DIM) < (HEAD_DIM // 2)


def _att_load(qkv_ref, cos_ref, sin_ref, with_q):
    cos, sin, fh = cos_ref[...], sin_ref[...], _first_half(BLOCK)
    qs = [_rope(qkv_ref[:, j * BLOCK:(j + 1) * BLOCK], cos, sin, fh) for j in range(4)] if with_q else None
    k = _rope(qkv_ref[:, ATT_QW:ATT_QW + ATT_KVW], cos, sin, fh)
    v = qkv_ref[:, ATT_QW + ATT_KVW:ATT_QW + 2 * ATT_KVW]
    return qs, k, v


def _att_load_meta(qkv_ref, cos_ref, sin_ref):
    k = _rope(qkv_ref[PAD:BLOCK, ATT_QW:ATT_QW + ATT_KVW], cos_ref[PAD:BLOCK, :], sin_ref[PAD:BLOCK, :],
              _first_half(N_META))
    return k, qkv_ref[PAD:BLOCK, ATT_QW + ATT_KVW:ATT_QW + 2 * ATT_KVW]


def _att_specs(blk):
    w = ATT_QW + 2 * ATT_KVW
    cur = lambda width: pl.BlockSpec((BLOCK, width), lambda i: (blk(i), 0))
    prev = lambda width: pl.BlockSpec((BLOCK, width), lambda i: (jnp.maximum(blk(i) - 1, 0), 0))
    meta = lambda width: pl.BlockSpec((BLOCK, width), lambda i: (0, 0))
    return [cur(w), prev(w), meta(w), cur(BLOCK), cur(BLOCK), prev(BLOCK), prev(BLOCK), meta(BLOCK), meta(BLOCK),
            pl.BlockSpec(memory_space=pltpu.SMEM)]


_FLIPS = [(dx, dy, dc) for dx in (0, 1) for dy in (0, 1) for dc in (0, 1)][1:]
N_PEERS = len(_FLIPS)


def _place():
    return lax.axis_index("x"), lax.axis_index("y"), lax.axis_index("c")


def _peer(place, flip):
    return tuple(1 - p if f else p for p, f in zip(place, flip))


def _slot(place, swapped):
    x, y, c = place
    return 4 * y + 2 * x + c if swapped else 4 * x + 2 * y + c


def _comm_specs(arrs, out_lead):
    n = len(arrs)
    outs = [jax.ShapeDtypeStruct((out_lead,) + a.shape[-2:], a.dtype) for a in arrs]
    sems = [pltpu.SemaphoreType.DMA((n, N_PEERS)), pltpu.SemaphoreType.DMA((n, N_PEERS)), pltpu.SemaphoreType.DMA((n,))]
    return [pl.BlockSpec(memory_space=pl.ANY)] * n, [pl.BlockSpec(memory_space=pl.ANY)] * n, outs, sems


def _gather_behind(shard_refs, out_refs, sems, swapped):
    send_sems, recv_sems, local_sems = sems
    x, y, c = _place()
    me, sibling = (x, y, c), (x, y, 1 - c)
    chips = [(1 - x, y), (x, 1 - y), (1 - x, 1 - y)]
    starts, passes, waits = [], [], []
    for w, (s, o) in enumerate(zip(shard_refs, out_refs)):
        def copy(k, block, to, from_shard=False, w=w, s=s, o=o):
            rows = o.at[_slot(block, swapped[w])]
            return pltpu.make_async_remote_copy(
                src_ref=s if from_shard else rows, dst_ref=rows, send_sem=send_sems.at[w, k],
                recv_sem=recv_sems.at[w, k], device_id=to, device_id_type=MESH)

        own = pltpu.make_async_copy(s, o.at[_slot(me, swapped[w])], local_sems.at[w])
        first = [copy(0, me, sibling, True)] + [copy(1 + j, me, (*chip, c), True) for j, chip in enumerate(chips)]
        handed = [copy(4 + j, (*chip, c), sibling) for j, chip in enumerate(chips)]
        starts += [own.start] + [cp.start for cp in first]
        for j, chip in enumerate(chips):
            passes += [copy(1 + j, (*chip, c), me).wait_recv, handed[j].start]
        waits.append(copy(0, sibling, me).wait_recv)
        waits += [copy(4 + j, (*chip, 1 - c), me).wait_recv for j, chip in enumerate(chips)]
        waits += [cp.wait_send for cp in first + handed] + [own.wait]
    return starts, passes, waits


def _scatter_behind(part_refs, recv_refs, sems, swapped):
    send_sems, recv_sems, _ = sems
    place = _place()
    starts, waits = [], []
    for w, (p, o) in enumerate(zip(part_refs, recv_refs)):
        for r, flip in enumerate(_FLIPS):
            peer = _peer(place, flip)
            cp = pltpu.make_async_remote_copy(
                src_ref=p.at[_slot(peer, swapped[w])], dst_ref=o.at[r], send_sem=send_sems.at[w, r],
                recv_sem=recv_sems.at[w, r], device_id=peer, device_id_type=MESH)
            starts.append(cp.start)
            waits += [cp.wait_recv, cp.wait_send]
    return starts, waits


def _mixers_fwd(proj_hg, proj_att, lbounds, norm_g, lv, cos, sin, sinks, shards, swapped):
    p = proj_hg.shape[0]
    nb = p // BLOCK
    n = len(shards)
    c_in, c_out, c_shapes, c_sems = _comm_specs(shards, N_DEV)
    pass_step = min(nb - 1, max(1, (5 * nb) // 8))

    def body(*refs):
        x_ref, lb_ref, ng_ref, lv_ref, cur_ref, prev_ref, meta_ref, cc, sc, cp, sp, cm, sm, sink_ref = refs[:14]
        shard_refs = refs[14:14 + n]
        y_ref, st_ref, o_ref = refs[14 + n:17 + n]
        out_refs = refs[17 + n:17 + 2 * n]
        carry_ref = refs[17 + 2 * n]
        starts, passes, waits = _gather_behind(shard_refs, out_refs, refs[18 + 2 * n:], swapped)
        c = pl.program_id(0)

        @pl.when(c == 0)
        def _():
            carry_ref[...] = jnp.zeros_like(carry_ref)
            for start in starts:
                start()

        @pl.when(c == pass_step)
        def _():
            for step in passes:
                step()

        valid = (c * BLOCK + lax.broadcasted_iota(jnp.int32, (BLOCK, 1), 0)) >= PAD
        logf, k = _hgrn_gates(x_ref[:, HG_W:2 * HG_W], lb_ref[0:1, :], lb_ref[1:2, :], valid)
        e = _split_dot(lv_ref[...], logf, "nn")
        for h in range(HG_HEADS):
            sl = lambda part: x_ref[:, part * HG_W + h * BLOCK: part * HG_W + (h + 1) * BLOCK]
            hs = slice(h * BLOCK, (h + 1) * BLOCK)
            st_in = carry_ref[h]
            st_ref[0, h] = st_in
            y, st_out = _hgrn_head(sl(0), k[:, hs], sl(2), sl(3), ng_ref[...], st_in, *_seg_blocks(e, h))
            y_ref[:, hs] = y.astype(BF16)
            carry_ref[h] = st_out

        qs, kc, vc = _att_load(cur_ref, cc, sc, True)
        _, kp, vp = _att_load(prev_ref, cp, sp, False)
        km, vm = _att_load_meta(meta_ref, cm, sm)
        s0, s1 = _att_sinkrows(sink_ref)
        o_ref[...] = _att_core(*qs, km, kp, kc, vm, vp, vc, s0, s1, *_att_masks(c)).astype(BF16)

        @pl.when(c == nb - 1)
        def _():
            for wait in waits:
                wait()

    return pl.pallas_call(
        body, name="mixers_fwd", grid=(nb,),
        in_specs=[pl.BlockSpec((BLOCK, 4 * HG_W), lambda c: (c, 0)), pl.BlockSpec((2, HG_W), lambda c: (0, 0)),
                  pl.BlockSpec((1, BLOCK), lambda c: (0, 0)), pl.BlockSpec(lv.shape, lambda c: (0, 0))]
        + _att_specs(lambda c: c) + c_in,
        out_specs=[pl.BlockSpec((BLOCK, HG_W), lambda c: (c, 0)),
                   pl.BlockSpec((1, HG_HEADS, BLOCK, BLOCK), lambda c: (c, 0, 0, 0)),
                   pl.BlockSpec((BLOCK, ATT_QW), lambda c: (c, 0))] + c_out,
        out_shape=[jax.ShapeDtypeStruct((p, HG_W), BF16), jax.ShapeDtypeStruct((nb, HG_HEADS, BLOCK, BLOCK), F32),
                   jax.ShapeDtypeStruct((p, ATT_QW), BF16)] + c_shapes,
        scratch_shapes=[pltpu.VMEM((HG_HEADS, BLOCK, BLOCK), F32)] + c_sems,
        compiler_params=_cparams(("arbitrary",)),
    )(proj_hg, lbounds, norm_g, lv, proj_att, proj_att, proj_att, cos, sin, cos, sin, cos, sin, sinks, *shards)


def _tile(rows, preferred):
    return preferred if rows % preferred == 0 else _row_tile(rows, preferred)


def _branch_mix(yh, oa, gates, w_bh, w_ba):
    y_hg = _dot(yh, w_bh, "nn")
    y_att = _dot(oa, w_ba, "nn")
    s1 = jax.nn.sigmoid(gates[:, :D_MODEL])
    s2 = jax.nn.sigmoid(gates[:, D_MODEL:])
    return s1 * y_hg + s2 * y_att, y_hg, y_att, s1, s2


def _mix_out_ln1(yh, oa, gates, h0, w_bh, w_ba, w_out, g1, b1):
    p = yh.shape[0]
    tr = _tile(p, 320)

    def body(yh_ref, oa_ref, g_ref, h0_ref, wbh_ref, wba_ref, wo_ref, g1_ref, b1_ref,
             mix_ref, h1_ref, h1b_ref, xh_ref, rs_ref):
        mixin = _branch_mix(yh_ref[...], oa_ref[...], g_ref[...], wbh_ref[...], wba_ref[...])[0]
        mix_ref[...] = mixin.astype(BF16)
        xhat, rstd = _ln_stats(ALPHA * h0_ref[...] + _dot(mixin, wo_ref[...], "nn"))
        h1 = xhat * g1_ref[...] + b1_ref[...]
        h1_ref[...] = h1
        h1b_ref[...] = h1.astype(BF16)
        xh_ref[...] = xhat
        rs_ref[...] = rstd

    row = lambda w: pl.BlockSpec((tr, w), lambda i: (i, 0))
    const = lambda a: pl.BlockSpec(a.shape, lambda i: (0, 0))
    return pl.pallas_call(
        body, name="mix_out_ln1", grid=(p // tr,),
        in_specs=[row(HG_W), row(ATT_QW), row(2 * D_MODEL), row(D_MODEL), const(w_bh), const(w_ba), const(w_out),
                  const(g1), const(b1)],
        out_specs=[row(D_MODEL), row(D_MODEL), row(D_MODEL), row(D_MODEL), row(1)],
        out_shape=[jax.ShapeDtypeStruct((p, D_MODEL), BF16), jax.ShapeDtypeStruct((p, D_MODEL), F32),
                   jax.ShapeDtypeStruct((p, D_MODEL), BF16), jax.ShapeDtypeStruct((p, D_MODEL), F32),
                   jax.ShapeDtypeStruct((p, 1), F32)],
        compiler_params=_cparams(("arbitrary",)),
    )(yh, oa, gates, h0, w_bh, w_ba, w_out, g1, b1)


FF_T = D_FF // 2


def _ffn_in_swiglu(h1, w_fi):
    p = h1.shape[0]
    tm = _row_tile(p, 640)

    def body(h_ref, w_ref, au_ref, s_ref):
        au = _dot(h_ref[...], w_ref[...], "nn")
        au_ref[...] = au
        s_ref[...] = (jax.nn.silu(au[:, :FF_T]) * au[:, FF_T:]).astype(BF16)

    return pl.pallas_call(
        body, name="ffn_in_swiglu", grid=(D_FF // FF_T, p // tm),
        in_specs=[pl.BlockSpec((tm, D_MODEL), lambda j, i: (i, 0)), pl.BlockSpec((D_MODEL, 2 * FF_T), lambda j, i: (0, j))],
        out_specs=[pl.BlockSpec((tm, 2 * FF_T), lambda j, i: (i, j)), pl.BlockSpec((tm, FF_T), lambda j, i: (i, j))],
        out_shape=[jax.ShapeDtypeStruct((p, 2 * D_FF), F32), jax.ShapeDtypeStruct((p, D_FF), BF16)],
        compiler_params=_cparams(("arbitrary", "arbitrary")),
    )(h1, w_fi)


def _ffn_out_loss(s, w_fo, h1, g2, b2, target):
    p = h1.shape[0]
    tr = _row_tile(p, 640)
    k = tr // BLOCK

    def body(*refs):
        s_ref, w_ref, h_ref, g_ref, b_ref = refs[:5]
        dr_ref, loss_ref, dg_ref, db_ref = refs[5 + k:]
        i = pl.program_id(0)
        xhat, rstd = _ln_stats(ALPHA * h_ref[...] + _dot(s_ref[...], w_ref[...], "nn"))
        y = xhat * g_ref[...] + b_ref[...]
        row = i * tr + lax.broadcasted_iota(jnp.int32, (tr, 1), 0)
        tgt = jnp.concatenate([r[...] for r in refs[5:5 + k]], axis=0)
        err = jnp.where(row >= BLOCK, y - tgt, 0.0)
        dr, dg, db = _ln_bwd(err * (1.0 / D_MODEL), xhat, rstd, g_ref[...])
        dr_ref[...] = dr
        e2 = jnp.sum(err * err, axis=0, keepdims=True)
        part = e2[:, 0:BLOCK]
        for j in range(1, D_MODEL // BLOCK):
            part = part + e2[:, j * BLOCK:(j + 1) * BLOCK]
        part = part * (0.5 / D_MODEL)

        @pl.when(i == 0)
        def _():
            loss_ref[...] = part
            dg_ref[...] = dg
            db_ref[...] = db

        @pl.when(i > 0)
        def _():
            loss_ref[...] += part
            dg_ref[...] += dg
            db_ref[...] += db

    vec = pl.BlockSpec((1, D_MODEL), lambda i: (0, 0))
    rowsp = pl.BlockSpec((tr, D_MODEL), lambda i: (i, 0))
    return pl.pallas_call(
        body, name="ffn_out_loss", grid=(p // tr,),
        in_specs=[pl.BlockSpec((tr, D_FF), lambda i: (i, 0)), pl.BlockSpec((D_FF, D_MODEL), lambda i: (0, 0)),
                  rowsp, vec, vec] + _token_streams(tr),
        out_specs=[rowsp, pl.BlockSpec((1, BLOCK), lambda i: (0, 0)), vec, vec],
        out_shape=[jax.ShapeDtypeStruct((p, D_MODEL), F32), jax.ShapeDtypeStruct((1, BLOCK), F32),
                   jax.ShapeDtypeStruct((1, D_MODEL), F32), jax.ShapeDtypeStruct((1, D_MODEL), F32)],
        compiler_params=_cparams(("arbitrary",)),
    )(s, w_fo, h1, g2, b2, *([target] * k))


def _d_ffn_hidden(dr2, w_fo, au):
    p = au.shape[0]
    tm = _row_tile(p, 640)

    def body(d_ref, w_ref, au_ref, o_ref):
        ds = _dot(d_ref[...], w_ref[...], "nt")
        _, vjp = jax.vjp(lambda a, u: jax.nn.silu(a) * u, au_ref[:, :FF_T], au_ref[:, FF_T:])
        da, du = vjp(ds)
        o_ref[:, :FF_T] = da.astype(BF16)
        o_ref[:, FF_T:] = du.astype(BF16)

    return pl.pallas_call(
        body, name="d_ffn_hidden", grid=(D_FF // FF_T, p // tm),
        in_specs=[pl.BlockSpec((tm, D_MODEL), lambda j, i: (i, 0)), pl.BlockSpec((FF_T, D_MODEL), lambda j, i: (j, 0)),
                  pl.BlockSpec((tm, 2 * FF_T), lambda j, i: (i, j))],
        out_specs=pl.BlockSpec((tm, 2 * FF_T), lambda j, i: (i, j)),
        out_shape=jax.ShapeDtypeStruct((p, 2 * D_FF), BF16), compiler_params=_cparams(("arbitrary", "arbitrary")),
    )(dr2, w_fo, au)


def _ln1_mix_bwd(dr2, dh1_ffn, xhat1, rstd1, g1, yh, oa, gates, w_bh, w_ba, w_out):
    p = yh.shape[0]
    tr = _tile(p, 320)

    def body(a_ref, b_ref, xh_ref, rs_ref, g1_ref, yh_ref, oa_ref, g_ref, wbh_ref, wba_ref, wo_ref,
             dr_ref, dyhg_ref, dyat_ref, dgt_ref, dyh_ref, doa_ref, dg_ref, db_ref):
        i = pl.program_id(0)
        dr, dg, db = _ln_bwd(ALPHA * a_ref[...] + b_ref[...], xh_ref[...], rs_ref[...], g1_ref[...])
        dr_ref[...] = dr
        d = _dot(dr, wo_ref[...], "nt")
        _, y_hg, y_att, s1, s2 = _branch_mix(yh_ref[...], oa_ref[...], g_ref[...], wbh_ref[...], wba_ref[...])
        dy_hg = d * s1
        dy_att = d * s2
        dyhg_ref[...] = dy_hg.astype(BF16)
        dyat_ref[...] = dy_att.astype(BF16)
        dgt_ref[:, :D_MODEL] = (d * y_hg * s1 * (1.0 - s1)).astype(BF16)
        dgt_ref[:, D_MODEL:] = (d * y_att * s2 * (1.0 - s2)).astype(BF16)
        dyh_ref[...] = _dot(dy_hg, wbh_ref[...], "nt")
        doa_ref[...] = _dot(dy_att, wba_ref[...], "nt")

        @pl.when(i == 0)
        def _():
            dg_ref[...] = dg
            db_ref[...] = db

        @pl.when(i > 0)
        def _():
            dg_ref[...] += dg
            db_ref[...] += db

    row = lambda w: pl.BlockSpec((tr, w), lambda i: (i, 0))
    const = lambda a: pl.BlockSpec(a.shape, lambda i: (0, 0))
    vec = pl.BlockSpec((1, D_MODEL), lambda i: (0, 0))
    return pl.pallas_call(
        body, name="ln1_mix_bwd", grid=(p // tr,),
        in_specs=[row(D_MODEL), row(D_MODEL), row(D_MODEL), row(1), vec, row(HG_W), row(ATT_QW), row(2 * D_MODEL),
                  const(w_bh), const(w_ba), const(w_out)],
        out_specs=[row(D_MODEL), row(D_MODEL), row(D_MODEL), row(2 * D_MODEL), row(HG_W), row(ATT_QW), vec, vec],
        out_shape=[jax.ShapeDtypeStruct((p, D_MODEL), F32), jax.ShapeDtypeStruct((p, D_MODEL), BF16),
                   jax.ShapeDtypeStruct((p, D_MODEL), BF16), jax.ShapeDtypeStruct((p, 2 * D_MODEL), BF16),
                   jax.ShapeDtypeStruct((p, HG_W), F32), jax.ShapeDtypeStruct((p, ATT_QW), F32),
                   jax.ShapeDtypeStruct((1, D_MODEL), F32), jax.ShapeDtypeStruct((1, D_MODEL), F32)],
        compiler_params=_cparams(("arbitrary",)),
    )(dr2, dh1_ffn, xhat1, rstd1, g1, yh, oa, gates, w_bh, w_ba, w_out)


MIX_W = 4 * HG_W + ATT_QW + 2 * ATT_KVW


def _mixers_bwd(proj_hg, proj_att, lbounds, norm_g, lv, states, cos, sin, sinks, dyh, doa, parts, swapped):
    p = proj_hg.shape[0]
    nb = p // BLOCK
    n = len(parts)
    kvw = 2 * ATT_KVW
    rev = lambda s: nb - 1 - s
    c_in, c_out, c_shapes, c_sems = _comm_specs(parts, N_PEERS)

    def body(*refs):
        (x_ref, lb_ref, ng_ref, lv_ref, st_ref, cur_ref, prev_ref, meta_ref, cc, sc, cp, sp, cm, sm, sink_ref,
         dy_ref, do_ref) = refs[:17]
        part_refs = refs[17:17 + n]
        dx_ref, dlb_ref, dng_ref, dsink_ref = refs[17 + n:21 + n]
        recv_refs = refs[21 + n:21 + 2 * n]
        dcarry_ref, dkv_next_ref, dkv_meta_ref = refs[21 + 2 * n:24 + 2 * n]
        starts, waits = _scatter_behind(part_refs, recv_refs, refs[24 + 2 * n:], swapped)
        step = pl.program_id(0)
        c = rev(step)

        @pl.when(step == 0)
        def _():
            dcarry_ref[...] = jnp.zeros_like(dcarry_ref)
            dkv_next_ref[...] = jnp.zeros_like(dkv_next_ref)
            dkv_meta_ref[...] = jnp.zeros_like(dkv_meta_ref)
            dlb_ref[...] = jnp.zeros_like(dlb_ref)
            dng_ref[...] = jnp.zeros_like(dng_ref)
            dsink_ref[...] = jnp.zeros_like(dsink_ref)
            for start in starts:
                start()

        fh = _first_half(BLOCK)
        qs, kc, vc = _att_load(cur_ref, cc, sc, True)
        _, kp, vp = _att_load(prev_ref, cp, sp, False)
        km, vm = _att_load_meta(meta_ref, cm, sm)
        s0, s1 = _att_sinkrows(sink_ref)
        masks = _att_masks(c)
        _, att_vjp = jax.vjp(lambda *a: _att_core(*a, *masks), *qs, km, kp, kc, vm, vp, vc, s0, s1)
        dq0, dq1, dq2, dq3, dkm, dkp, dkc, dvm, dvp, dvc, ds0, ds1 = att_vjp(do_ref[...])
        att0 = 4 * HG_W
        for j, dq in enumerate((dq0, dq1, dq2, dq3)):
            dx_ref[:, att0 + j * BLOCK:att0 + (j + 1) * BLOCK] = _rope_t(dq, cc[...], sc[...], fh).astype(BF16)
        dkv_meta_ref[:, :BLOCK] += _rope_t(dkm, cm[PAD:BLOCK, :], sm[PAD:BLOCK, :], _first_half(N_META))
        dkv_meta_ref[:, BLOCK:] += dvm
        last = jnp.where(c == 0, 1.0, 0.0)
        to_meta_rows = lambda m: jnp.concatenate([jnp.zeros((PAD, BLOCK), F32), last * m], axis=0)
        dk = _rope_t(dkc, cc[...], sc[...], fh) + dkv_next_ref[:, :BLOCK] + to_meta_rows(dkv_meta_ref[:, :BLOCK])
        dv = dvc + dkv_next_ref[:, BLOCK:] + to_meta_rows(dkv_meta_ref[:, BLOCK:])
        dx_ref[:, att0 + ATT_QW:att0 + ATT_QW + ATT_KVW] = dk.astype(BF16)
        dx_ref[:, att0 + ATT_QW + ATT_KVW:] = dv.astype(BF16)
        dkv_next_ref[:, :BLOCK] = _rope_t(dkp, cp[...], sp[...], fh)
        dkv_next_ref[:, BLOCK:] = dvp
        sink_rows = []
        for dsg in (ds0, ds1):
            for j in range(4):
                tot = jnp.sum(dsg[:, j * BLOCK:(j + 1) * BLOCK], axis=1, keepdims=True)
                sink_rows.append(jnp.broadcast_to(tot, (1, BLOCK)))
        dsink_ref[...] += jnp.concatenate(sink_rows, axis=0)

        valid = (c * BLOCK + lax.broadcasted_iota(jnp.int32, (BLOCK, 1), 0)) >= PAD
        (logf, k), gates_vjp = jax.vjp(lambda hf, a0, a1: _hgrn_gates(hf, a0, a1, valid),
                                       x_ref[:, HG_W:2 * HG_W], lb_ref[0:1, :], lb_ref[1:2, :])
        lvv = lv_ref[...]
        e = _split_dot(lvv, logf, "nn")
        dng = jnp.zeros((1, BLOCK), F32)
        dk, dseg = [], []
        for h in range(HG_HEADS):
            sl = lambda part: x_ref[:, part * HG_W + h * BLOCK: part * HG_W + (h + 1) * BLOCK]
            hs = slice(h * BLOCK, (h + 1) * BLOCK)
            _, vjp = jax.vjp(_hgrn_head, sl(0), k[:, hs], sl(2), sl(3), ng_ref[...], st_ref[0, h], *_seg_blocks(e, h))
            dhq, dkh, dhi, dhg, dngh, dst, *dsegh = vjp((dy_ref[:, hs], dcarry_ref[h]))
            for part, val in ((0, dhq), (2, dhi), (3, dhg)):
                dx_ref[:, part * HG_W + h * BLOCK: part * HG_W + (h + 1) * BLOCK] = val.astype(BF16)
            dk.append(dkh)
            dseg.append(jnp.concatenate(dsegh, axis=0))
            dng = dng + dngh
            dcarry_ref[h] = dst
        dlogf = _split_dot(lvv, jnp.concatenate(dseg, axis=1), "tn")
        dhf, da0, da1 = gates_vjp((dlogf, jnp.concatenate(dk, axis=1)))
        dx_ref[:, HG_W:2 * HG_W] = dhf.astype(BF16)
        dlb_ref[0:1, :] += da0
        dlb_ref[1:2, :] += da1
        dng_ref[...] += dng

        @pl.when(step == nb - 1)
        def _():
            for wait in waits:
                wait()

    const = lambda shape: pl.BlockSpec(shape, lambda s: (0,) * len(shape))
    return pl.pallas_call(
        body, name="mixers_bwd", grid=(nb,),
        in_specs=[pl.BlockSpec((BLOCK, 4 * HG_W), lambda s: (rev(s), 0)), const((2, HG_W)), const((1, BLOCK)),
                  const(lv.shape), pl.BlockSpec((1, HG_HEADS, BLOCK, BLOCK), lambda s: (rev(s), 0, 0, 0))]
        + _att_specs(rev)
        + [pl.BlockSpec((BLOCK, HG_W), lambda s: (rev(s), 0)), pl.BlockSpec((BLOCK, ATT_QW), lambda s: (rev(s), 0))]
        + c_in,
        out_specs=[pl.BlockSpec((BLOCK, MIX_W), lambda s: (rev(s), 0)), const((2, HG_W)), const((1, BLOCK)),
                   const((ATT_HEADS, BLOCK))] + c_out,
        out_shape=[jax.ShapeDtypeStruct((p, MIX_W), BF16), jax.ShapeDtypeStruct((2, HG_W), F32),
                   jax.ShapeDtypeStruct((1, BLOCK), F32), jax.ShapeDtypeStruct((ATT_HEADS, BLOCK), F32)] + c_shapes,
        scratch_shapes=[pltpu.VMEM((HG_HEADS, BLOCK, BLOCK), F32), pltpu.VMEM((BLOCK, kvw), F32),
                        pltpu.VMEM((N_META, kvw), F32)] + c_sems,
        compiler_params=_cparams(("arbitrary",)),
    )(proj_hg, lbounds, norm_g, lv, states, proj_att, proj_att, proj_att, cos, sin, cos, sin, cos, sin, sinks,
      dyh, doa, *parts)


def _embed_bwd(dmix, dgates, w_mix, w_gates, dr1, xhat0, rstd0, g0, parts, swapped):
    p = dmix.shape[0]
    tm = _row_tile(p, 640)
    nm = p // tm
    n = len(parts)
    c_in, c_out, c_shapes, c_sems = _comm_specs(parts, N_PEERS)

    def body(*refs):
        a_ref, g_ref, wa_ref, wg_ref, dr_ref, xh_ref, rs_ref, g0_ref = refs[:8]
        o_ref, dg_ref, db_ref = refs[8 + n:11 + n]
        starts, waits = _scatter_behind(refs[8:8 + n], refs[11 + n:11 + 2 * n], refs[11 + 2 * n:], swapped)
        i = pl.program_id(0)

        @pl.when(i == 0)
        def _():
            for start in starts:
                start()

        dh0 = ALPHA * dr_ref[...] + _dot(a_ref[...], wa_ref[...], "nt") + _dot(g_ref[...], wg_ref[...], "nt")
        row = i * tm + lax.broadcasted_iota(jnp.int32, (tm, 1), 0)
        dx, dg, db = _ln_bwd(jnp.where(row >= PAD, dh0, 0.0), xh_ref[...], rs_ref[...], g0_ref[...])
        o_ref[...] = dx

        @pl.when(i == 0)
        def _():
            dg_ref[...] = dg
            db_ref[...] = db

        @pl.when(i > 0)
        def _():
            dg_ref[...] += dg
            db_ref[...] += db

        @pl.when(i == nm - 1)
        def _():
            for wait in waits:
                wait()

    row = lambda w: pl.BlockSpec((tm, w), lambda i: (i, 0))
    const = lambda a: pl.BlockSpec(a.shape, lambda i: (0, 0))
    vec = pl.BlockSpec((1, D_MODEL), lambda i: (0, 0))
    return pl.pallas_call(
        body, name="embed_bwd", grid=(nm,),
        in_specs=[row(dmix.shape[1]), row(dgates.shape[1]), const(w_mix), const(w_gates), row(D_MODEL), row(D_MODEL),
                  row(1), vec] + c_in,
        out_specs=[row(D_MODEL), vec, vec] + c_out,
        out_shape=[jax.ShapeDtypeStruct((p, D_MODEL), F32), jax.ShapeDtypeStruct((1, D_MODEL), F32),
                   jax.ShapeDtypeStruct((1, D_MODEL), F32)] + c_shapes,
        scratch_shapes=c_sems, compiler_params=_cparams(("arbitrary",)),
    )(dmix, dgates, w_mix, w_gates, dr1, xhat0, rstd0, g0, *parts)


_LATE = ("w_branch_hg", "w_branch_attn", "w_out", "w_ffn_in", "w_ffn_out")
_COLUMN_SHARDED = ("meta_tokens", "w_in", "w_branch_hg", "w_branch_attn", "w_ffn_in")
_SWAPPED = ("w_ffn_in",)


def _whole(name, gathered):
    _, r, c = gathered.shape
    if name in _COLUMN_SHARDED:
        return jnp.transpose(gathered, (1, 0, 2)).reshape(r, N_DEV * c)
    return gathered.reshape(N_DEV * r, c)


def _slots(name, whole):
    r, c = whole.shape
    if name in _COLUMN_SHARDED:
        return jnp.transpose(whole.reshape(r, N_DEV, c // N_DEV), (1, 0, 2))
    return whole.reshape(N_DEV, r // N_DEV, c)


def _device_step(x, target, meta_shard, ln_emb_g, ln_emb_b, w_in_shard, lbounds, norm_g, sinks, late_shards,
                 ln1_g, ln1_b, ln2_g, ln2_b):
    s = x.shape[0]
    p = s + BLOCK
    tm = _row_tile(p, 640)
    lv = _level_stack()
    cos, sin = _rope_tables(p)
    hg_end = 4 * HG_W
    mm = functools.partial(_tiled_matmul, tm=tm)
    swapped = [n in _SWAPPED for n in _LATE]

    h0, h0b, xhat0, rstd0, _, g_win = _embed_ln(x, meta_shard, w_in_shard, ln_emb_g, ln_emb_b)
    w_in = _whole("w_in", g_win)
    proj_hg = mm(h0b, w_in[:, :hg_end], "nn", tn=hg_end, tc=D_MODEL, out_dtype=F32, name="proj_hg")
    proj_att = mm(h0b, w_in[:, hg_end:MIX_W], "nn", tn=MIX_W - hg_end, tc=D_MODEL, out_dtype=F32, name="proj_att")
    gates = mm(h0b, w_in[:, MIX_W:], "nn", tn=2 * D_MODEL, tc=D_MODEL, out_dtype=F32, name="proj_gates")
    yh, states, oa, *gathered = _mixers_fwd(proj_hg, proj_att, lbounds, norm_g, lv, cos, sin, sinks, late_shards, swapped)
    w_bh, w_ba, w_out, w_fi, w_fo = [_whole(n, g) for n, g in zip(_LATE, gathered)]
    mixin, h1, h1b, xhat1, rstd1 = _mix_out_ln1(yh, oa, gates, h0, w_bh, w_ba, w_out, ln1_g, ln1_b)
    au, sw = _ffn_in_swiglu(h1b, w_fi)
    dr2, loss_part, dg2, db2 = _ffn_out_loss(sw, w_fo, h1, ln2_g, ln2_b, target)

    mtn = functools.partial(_tiled_matmul_tn, tm=_row_tile(p, 1664), out_dtype=BF16)
    d_wfo = mtn(sw, dr2, tk=FF_T, tn=D_MODEL, name="grad_w_ffn_out")
    dau = _d_ffn_hidden(dr2, w_fo, au)
    d_wfi = mtn(h1b, dau, tk=D_MODEL, tn=FF_T, name="grad_w_ffn_in")
    dh1_ffn = mm(dau, w_fi, "nt", tn=D_MODEL, tc=D_FF, out_dtype=F32, name="d_h1_ffn")
    dr1, dy_hg, dy_att, dgates, dyh, doa, dg1, db1 = _ln1_mix_bwd(
        dr2, dh1_ffn, xhat1, rstd1, ln1_g, yh, oa, gates, w_bh, w_ba, w_out)
    d_wout = mtn(mixin, dr1, tk=D_MODEL, tn=D_MODEL, name="grad_w_out")
    d_wbh = mtn(yh, dy_hg, tk=HG_W, tn=D_MODEL, name="grad_w_branch_hg")
    d_wba = mtn(oa, dy_att, tk=ATT_QW, tn=D_MODEL, name="grad_w_branch_attn")
    late_parts = [_slots(n, g) for n, g in zip(_LATE, (d_wbh, d_wba, d_wout, d_wfi, d_wfo))]
    dmix, d_lb, d_ng, d_sink, *late_recv = _mixers_bwd(
        proj_hg, proj_att, lbounds, norm_g, lv, states, cos, sin, sinks, dyh, doa, late_parts, swapped)
    d_win = jnp.concatenate([mtn(h0b, dmix, tk=D_MODEL, tn=MIX_W // 2, name="grad_w_in_mixers"),
                             mtn(h0b, dgates, tk=D_MODEL, tn=D_MODEL, name="grad_w_in_gates")], axis=1)
    win_parts = _slots("w_in", d_win)
    dxin, dg0, db0, win_recv = _embed_bwd(dmix, dgates, w_in[:, :MIX_W], w_in[:, MIX_W:], dr1, xhat0, rstd0, ln_emb_g,
                                          [win_parts], [False])

    small = dict(ln_emb_g=dg0, ln_emb_b=db0, hg_lower_bounds=d_lb, hg_norm_g=d_ng, attn_sinks=d_sink[:, 0],
                 ln1_g=dg1, ln1_b=db1, ln2_g=dg2, ln2_b=db2)
    big = dict(zip(_LATE, zip(late_parts, late_recv)))
    big["w_in"] = (win_parts, win_recv)
    return loss_part, dxin[BLOCK:], small, dxin[PAD:BLOCK], big


def _all_gather(arrs, dtypes, name):
    n = len(arrs)

    def body(*refs):
        ins, outs, stages = refs[:n], refs[n:2 * n], refs[2 * n:3 * n]
        send_sems, recv_sems, local_sems = refs[3 * n:]
        x, y, c = _place()
        sibling = (x, y, 1 - c)
        chips = [(1 - x, y), (x, 1 - y), (1 - x, 1 - y)]
        slot = lambda px, py, pc: 4 * px + 2 * py + pc

        def copy(w, k, block, to, from_stage=False):
            return pltpu.make_async_remote_copy(
                src_ref=stages[w] if from_stage else outs[w].at[slot(*block)], dst_ref=outs[w].at[slot(*block)],
                send_sem=send_sems.at[w, k], recv_sem=recv_sems.at[w, k], device_id=to, device_id_type=MESH)

        mine, first, passed = [], [], []
        for w in range(n):
            stages[w][...] = ins[w][...].astype(dtypes[w])
            mine.append(pltpu.make_async_copy(stages[w], outs[w].at[slot(x, y, c)], local_sems.at[w]))
            mine[-1].start()
        for w in range(n):
            first.append(copy(w, 0, (x, y, c), sibling, from_stage=True))
            first += [copy(w, 1 + j, (x, y, c), (*chip, c), from_stage=True) for j, chip in enumerate(chips)]
        for cp in first:
            cp.start()
        for j, chip in enumerate(chips):
            for w in range(n):
                copy(w, 1 + j, (*chip, c), (x, y, c)).wait_recv()
                passed.append(copy(w, 4 + j, (*chip, c), sibling))
                passed[-1].start()
        for w in range(n):
            copy(w, 0, sibling, (x, y, c)).wait_recv()
            for j, chip in enumerate(chips):
                copy(w, 4 + j, (*chip, 1 - c), (x, y, c)).wait_recv()
        for cp in first + passed:
            cp.wait_send()
        for cp in mine:
            cp.wait()

    return pl.pallas_call(
        body, name=name,
        in_specs=[pl.BlockSpec(memory_space=pltpu.VMEM)] * n,
        out_specs=[pl.BlockSpec(memory_space=pl.ANY)] * n,
        out_shape=[jax.ShapeDtypeStruct((N_DEV,) + a.shape, dt) for a, dt in zip(arrs, dtypes)],
        scratch_shapes=[pltpu.VMEM(a.shape, dt) for a, dt in zip(arrs, dtypes)]
        + [pltpu.SemaphoreType.DMA((n, 7)), pltpu.SemaphoreType.DMA((n, 7)), pltpu.SemaphoreType.DMA((n,))],
        compiler_params=pltpu.CompilerParams(vmem_limit_bytes=VMEM_LIMIT_BYTES),
    )(*arrs)


def _cast_shards(arrs):
    def body(*refs):
        for src, dst in zip(refs[:len(arrs)], refs[len(arrs):]):
            dst[...] = src[...].astype(BF16)

    return pl.pallas_call(body, name="cast_shards", out_shape=[jax.ShapeDtypeStruct(a.shape, BF16) for a in arrs],
                          compiler_params=pltpu.CompilerParams(vmem_limit_bytes=VMEM_LIMIT_BYTES))(*arrs)


def _shard_rows(rows):
    return rows if rows <= 512 else 256


def _adamw_math(w, g, m, v):
    m = ADAM_B1 * m + (1.0 - ADAM_B1) * g
    v = ADAM_B2 * v + (1.0 - ADAM_B2) * (g * g)
    m_hat = m / (1.0 - ADAM_B1 ** ADAM_STEP)
    v_hat = v / (1.0 - ADAM_B2 ** ADAM_STEP)
    delta = -ADAM_LR * (m_hat / (jnp.sqrt(v_hat) + ADAM_EPS) + ADAM_WD * w)
    return delta, m, v


def _reduce_adamw(parts, recv, own_slot, w, m, v, name):
    r, cdim = w.shape
    tr = _shard_rows(r)

    def body(idx_ref, p_ref, r_ref, w_ref, m_ref, v_ref, g_out, d_out, m_out, v_out):
        g = p_ref[0].astype(F32)
        for j in range(N_PEERS):
            g = g + r_ref[j].astype(F32)
        d, mn, vn = _adamw_math(w_ref[...], g, m_ref[...], v_ref[...])
        g_out[...] = g
        d_out[...] = d
        m_out[...] = mn
        v_out[...] = vn

    flat = pl.BlockSpec((tr, cdim), lambda i, idx_ref: (i, 0))
    return pl.pallas_call(
        body, name=name,
        grid_spec=pltpu.PrefetchScalarGridSpec(
            num_scalar_prefetch=1, grid=(r // tr,),
            in_specs=[pl.BlockSpec((1, tr, cdim), lambda i, idx_ref: (idx_ref[0], i, 0)),
                      pl.BlockSpec((N_PEERS, tr, cdim), lambda i, idx_ref: (0, i, 0)), flat, flat, flat],
            out_specs=[flat] * 4),
        out_shape=[jax.ShapeDtypeStruct((r, cdim), F32)] * 4,
        compiler_params=_cparams(("arbitrary",)),
    )(own_slot, parts, recv, w, m, v)


def _adamw_plain(w, g, m, v, name):
    def body(w_ref, g_ref, m_ref, v_ref, d_out, m_out, v_out):
        d_out[...], m_out[...], v_out[...] = _adamw_math(w_ref[...], g_ref[...], m_ref[...], v_ref[...])

    return pl.pallas_call(body, name=name, out_shape=[jax.ShapeDtypeStruct(w.shape, F32)] * 3)(w, g, m, v)


_SMALL_LAYOUT = (("ln_emb_g", 8), ("ln_emb_b", 8), ("hg_lower_bounds", 8), ("hg_norm_g", 1), ("attn_sinks", 1),
                 ("ln1_g", 8), ("ln1_b", 8), ("ln2_g", 8), ("ln2_b", 8))
_META_ROW = sum(r for _, r in _SMALL_LAYOUT)
_META_ROWS = N_META * D_MODEL // BLOCK
_LOSS_ROW = _META_ROW + _META_ROWS
SMALL_ROWS = 192


def _pack_small(vals, meta=None, loss_row=None):
    rows = []
    for name, nrows in _SMALL_LAYOUT:
        flat = vals[name].reshape(-1).astype(F32)
        flat = jnp.pad(flat, (0, nrows * BLOCK - flat.shape[0]))
        rows.append(flat.reshape(nrows, BLOCK))
    rows.append(jnp.zeros((_META_ROWS, BLOCK), F32) if meta is None else meta.reshape(_META_ROWS, BLOCK))
    rows.append(jnp.zeros((1, BLOCK), F32) if loss_row is None else loss_row)
    packed = jnp.concatenate(rows, axis=0)
    return jnp.pad(packed, ((0, SMALL_ROWS - packed.shape[0]), (0, 0)))


def _unpack_small(packed, shapes):
    out, row = {}, 0
    for name, nrows in _SMALL_LAYOUT:
        size = math.prod(shapes[name])
        out[name] = packed[row:row + nrows].reshape(-1)[:size].reshape(shapes[name])
        row += nrows
    return out


def _small_reduce_adamw(gathered, w, m, v):
    def body(g_ref, w_ref, m_ref, v_ref, g_out, d_out, m_out, v_out, loss_out):
        g = g_ref[0]
        for s in range(1, N_DEV):
            g = g + g_ref[s]
        d, mn, vn = _adamw_math(w_ref[...], g, m_ref[...], v_ref[...])
        g_out[...] = g
        d_out[...] = d
        m_out[...] = mn
        v_out[...] = vn
        loss_out[...] = jnp.broadcast_to(jnp.sum(g_ref[:, _LOSS_ROW, :]), (1, BLOCK))

    shp = jax.ShapeDtypeStruct((SMALL_ROWS, BLOCK), F32)
    return pl.pallas_call(body, name="small_reduce_adamw",
                          out_shape=[shp] * 4 + [jax.ShapeDtypeStruct((1, BLOCK), F32)])(gathered, w, m, v)


_WEIGHTS = ("meta_tokens", "ln_emb_g", "ln_emb_b", "w_in", "hg_lower_bounds", "hg_norm_g", "attn_sinks",
            "w_branch_hg", "w_branch_attn", "w_out", "ln1_g", "ln1_b", "w_ffn_in", "w_ffn_out", "ln2_g", "ln2_b")


def kernel(x, meta_tokens, ln_emb_g, ln_emb_b, w_in, hg_lower_bounds, hg_norm_g, attn_sinks, w_branch_hg, w_branch_attn, w_out, ln1_g, ln1_b, w_ffn_in, w_ffn_out, ln2_g, ln2_b, loss_target, m_meta_tokens, m_ln_emb_g, m_ln_emb_b, m_w_in, m_hg_lower_bounds, m_hg_norm_g, m_attn_sinks, m_w_branch_hg, m_w_branch_attn, m_w_out, m_ln1_g, m_ln1_b, m_w_ffn_in, m_w_ffn_out, m_ln2_g, m_ln2_b, v_meta_tokens, v_ln_emb_g, v_ln_emb_b, v_w_in, v_hg_lower_bounds, v_hg_norm_g, v_attn_sinks, v_w_branch_hg, v_w_branch_attn, v_w_out, v_ln1_g, v_ln1_b, v_w_ffn_in, v_w_ffn_out, v_ln2_g, v_ln2_b):
    given = dict(locals())
    weights = {n: given[n] for n in _WEIGHTS}
    mom1 = {n: given["m_" + n] for n in _WEIGHTS}
    mom2 = {n: given["v_" + n] for n in _WEIGHTS}
    shard2d = lambda a: a.reshape(a.shape[-2:])

    w_in_shard, *late_shards = _cast_shards([shard2d(weights[n]) for n in ("w_in",) + _LATE])
    loss_part, grad_x, small_grads, meta_grad, big = _device_step(
        x[0], loss_target[0], meta_tokens, ln_emb_g.reshape(1, -1), ln_emb_b.reshape(1, -1), w_in_shard,
        hg_lower_bounds, hg_norm_g, attn_sinks, late_shards, ln1_g, ln1_b, ln2_g, ln2_b)

    place = _place()
    out = {}
    for n, (parts, recv) in big.items():
        own = _slot(place, n in _SWAPPED).astype(jnp.int32).reshape(1)
        res = _reduce_adamw(parts, recv, own, shard2d(weights[n]), shard2d(mom1[n]), shard2d(mom2[n]), "adamw_" + n)
        out[n] = [r.reshape(weights[n].shape) for r in res]

    small_names = [n for n, _ in _SMALL_LAYOUT]
    packed = _pack_small(small_grads, meta_grad, loss_part)
    all_small, = _all_gather([packed], [F32], "gather_small")
    res = _small_reduce_adamw(all_small, _pack_small(weights), _pack_small(mom1), _pack_small(mom2))
    shapes = {n: weights[n].shape for n in small_names}
    unpacked = [_unpack_small(r, shapes) for r in res[:4]]
    for n in small_names:
        out[n] = [u[n] for u in unpacked]
    loss = res[4][0, 0]
    meta_whole = res[0][_META_ROW:_META_ROW + _META_ROWS].reshape(N_META, N_DEV, D_MODEL // N_DEV)
    g_meta_mine = lax.dynamic_index_in_dim(meta_whole, _slot(place, False), axis=1, keepdims=False)
    out["meta_tokens"] = [g_meta_mine, *_adamw_plain(meta_tokens, g_meta_mine, m_meta_tokens, v_meta_tokens,
                                                     "adamw_meta")]

    return (loss, grad_x[None], *[out[n][0] for n in _WEIGHTS], *[out[n][1] for n in _WEIGHTS],
            *[out[n][2] for n in _WEIGHTS], *[out[n][3] for n in _WEIGHTS])
```

```python
import functools
import math

import numpy as np
import jax
import jax.numpy as jnp
from jax import lax
from jax.experimental import pallas as pl
from jax.experimental.pallas import tpu as pltpu

F32 = jnp.float32
BF16 = jnp.bfloat16

D_MODEL = 1024
N_META = 16
BLOCK = 128
PAD = BLOCK - N_META
HG_HEADS = 4
HG_W = 512
ATT_HEADS = 8
HEAD_DIM = 64
ATT_QW = 512
ATT_KVW = 128
D_FF = 2816
EPS = 1e-5
ALPHA = 2.0 ** 0.25
ROPE_THETA = 10000.0
N_DEV = 8

ADAM_LR = 0.001
ADAM_B1 = 0.9
ADAM_B2 = 0.999
ADAM_EPS = 1e-08
ADAM_WD = 0.01
ADAM_STEP = 10

VMEM_LIMIT_BYTES = 56 * 1024 * 1024
MESH = pl.DeviceIdType.MESH

_LEVELS = (64, 32, 16, 8, 4, 2, 1)


def _cparams(sem):
    return pltpu.CompilerParams(dimension_semantics=sem, vmem_limit_bytes=VMEM_LIMIT_BYTES)


def _row_tile(rows, target):
    nb = rows // BLOCK
    best = 1
    for d in range(1, nb + 1):
        if nb % d == 0 and d * BLOCK <= target:
            best = d
    return best * BLOCK


_DN = {"nn": (((1,), (0,)), ((), ())), "nt": (((1,), (1,)), ((), ())), "tn": (((0,), (0,)), ((), ()))}


def _dot(a, b, form):
    return lax.dot_general(a.astype(BF16), b.astype(BF16), _DN[form], preferred_element_type=F32)


@functools.partial(jax.custom_vjp, nondiff_argnums=(2,))
def _mm(a, b, form):
    return _dot(a, b, form)


def _mm_fwd(a, b, form):
    a, b = a.astype(BF16), b.astype(BF16)
    return _dot(a, b, form), (a, b)


def _mm_bwd(form, res, g):
    a, b = res
    if form == "nn":
        return _dot(g, b, "nt"), _dot(a, g, "tn")
    if form == "nt":
        return _dot(g, b, "nn"), _dot(g, a, "tn")
    return _dot(b, g, "nt"), _dot(a, g, "nn")


_mm.defvjp(_mm_fwd, _mm_bwd)


def _split_dot(lv, x, form):
    return lax.dot_general(lv, x.astype(BF16), _DN[form], preferred_element_type=F32)


@jax.custom_vjp
def _swap_halves(x):
    return pltpu.roll(x, 64, 1)


_swap_halves.defvjp(lambda x: (pltpu.roll(x, 64, 1), None), lambda _, g: (pltpu.roll(g, 64, 1),))


def _tiled_matmul(a, b, form, *, tm, tn, tc, out_dtype, name):
    m, c = a.shape
    n = b.shape[1] if form == "nn" else b.shape[0]
    assert m % tm == 0 and n % tn == 0 and c % tc == 0, (name, a.shape, b.shape, tm, tn, tc)
    nc = c // tc

    def body(a_ref, b_ref, o_ref, *scratch):
        part = _dot(a_ref[...], b_ref[...], form)
        if nc == 1:
            o_ref[...] = part.astype(out_dtype)
            return
        acc_ref, = scratch
        ci = pl.program_id(2)

        @pl.when(ci == 0)
        def _():
            acc_ref[...] = part

        @pl.when(ci > 0)
        def _():
            acc_ref[...] += part

        @pl.when(ci == nc - 1)
        def _():
            o_ref[...] = acc_ref[...].astype(out_dtype)

    b_spec = (pl.BlockSpec((tc, tn), lambda j, i, k: (k, j)) if form == "nn"
              else pl.BlockSpec((tn, tc), lambda j, i, k: (j, k)))
    return pl.pallas_call(
        body, name=name, grid=(n // tn, m // tm, nc),
        in_specs=[pl.BlockSpec((tm, tc), lambda j, i, k: (i, k)), b_spec],
        out_specs=pl.BlockSpec((tm, tn), lambda j, i, k: (i, j)),
        out_shape=jax.ShapeDtypeStruct((m, n), out_dtype),
        scratch_shapes=[] if nc == 1 else [pltpu.VMEM((tm, tn), F32)],
        compiler_params=_cparams(("arbitrary", "arbitrary", "arbitrary")),
    )(a, b)


def _tiled_matmul_tn(a, b, *, tm, tk, tn, out_dtype, name):
    m, k = a.shape
    n = b.shape[1]
    assert m % tm == 0 and k % tk == 0 and n % tn == 0, (name, a.shape, b.shape, tm, tk, tn)
    nm = m // tm

    def body(a_ref, b_ref, o_ref, acc_ref):
        part = _dot(a_ref[...], b_ref[...], "tn")
        mi = pl.program_id(2)

        @pl.when(mi == 0)
        def _():
            acc_ref[...] = part

        @pl.when(mi > 0)
        def _():
            acc_ref[...] += part

        @pl.when(mi == nm - 1)
        def _():
            o_ref[...] = acc_ref[...].astype(out_dtype)

    return pl.pallas_call(
        body, name=name, grid=(k // tk, n // tn, nm),
        in_specs=[pl.BlockSpec((tm, tk), lambda kk, j, i: (i, kk)), pl.BlockSpec((tm, tn), lambda kk, j, i: (i, j))],
        out_specs=pl.BlockSpec((tk, tn), lambda kk, j, i: (kk, j)),
        out_shape=jax.ShapeDtypeStruct((k, n), out_dtype),
        scratch_shapes=[pltpu.VMEM((tk, tn), F32)],
        compiler_params=_cparams(("arbitrary", "arbitrary", "arbitrary")),
    )(a, b)


def _ln_stats(r):
    mu = jnp.mean(r, axis=-1, keepdims=True)
    xc = r - mu
    var = jnp.mean(xc * xc, axis=-1, keepdims=True)
    rstd = lax.rsqrt(var + EPS)
    return xc * rstd, rstd


def _ln_bwd(dy, xhat, rstd, g):
    dxhat = dy * g
    m1 = jnp.mean(dxhat, axis=-1, keepdims=True)
    m2 = jnp.mean(dxhat * xhat, axis=-1, keepdims=True)
    dr = rstd * (dxhat - m1 - xhat * m2)
    return dr, jnp.sum(dy * xhat, axis=0, keepdims=True), jnp.sum(dy, axis=0, keepdims=True)


N_SEG = 3 + len(_LEVELS)


def _level_stack():
    t = np.arange(BLOCK)[:, None]
    r = np.arange(BLOCK)[None, :]
    mats = [r <= t, r > t, np.ones((BLOCK, BLOCK), bool)]
    for h in _LEVELS:
        same = (t // (2 * h)) == (r // (2 * h))
        up_t, up_r = (t % (2 * h)) >= h, (r % (2 * h)) >= h
        mats.append(same & ((up_t & up_r & (r <= t)) | (~up_t & ~up_r & (r > t))))
    return jnp.asarray(np.concatenate(mats, axis=0).astype(np.float32), dtype=BF16)


def _hgrn_gates(hf, a0, a1, valid):
    lb = jax.nn.sigmoid(a0 - a1)
    fg = lb + (1.0 - lb) * jax.nn.sigmoid(hf)
    return jnp.where(valid, jnp.log(fg), 0.0), jnp.where(valid, 1.0 - fg, 0.0)


def _hgrn_head(hq, k, v, hg, ng, st_in, *seg):
    q = jax.nn.silu(hq)
    rows = lax.broadcasted_iota(jnp.int32, (BLOCK, BLOCK), 0)
    cols = lax.broadcasted_iota(jnp.int32, (BLOCK, BLOCK), 1)
    o = _mm(q * jnp.exp(seg[0]), st_in, "nt")
    a = jnp.where(rows == cols, jnp.sum(q * k, axis=-1, keepdims=True), 0.0)
    differ = jnp.bitwise_xor(rows, cols)
    for li, h in enumerate(_LEVELS):
        decay = jnp.exp(seg[3 + li])
        pair = (cols < rows) & (differ >= h) & (differ < 2 * h)
        a = a + jnp.where(pair, _mm(q * decay, k * decay, "nt"), 0.0)
    o = o + _mm(a, v, "nn")
    st_out = st_in * jnp.exp(seg[2]) + _mm(v, k * jnp.exp(seg[1]), "tn")
    on = o * lax.rsqrt(jnp.mean(o * o, axis=-1, keepdims=True) + EPS) * ng
    return on * jax.nn.silu(hg), st_out


def _seg_blocks(e, h):
    return [e[i * BLOCK:(i + 1) * BLOCK, h * BLOCK:(h + 1) * BLOCK] for i in range(N_SEG)]


def _rope(x, cos, sin, first_half):
    partner = jnp.where(first_half, -pltpu.roll(x, 96, 1), pltpu.roll(x, 32, 1))
    return x * cos + partner * sin


def _rope_t(g, cos, sin, first_half):
    u = g * sin
    partner = jnp.where(first_half, pltpu.roll(u, 96, 1), -pltpu.roll(u, 32, 1))
    return g * cos + partner


def _att_core(q0, q1, q2, q3, km, kp, kc, vm, vp, vc, sinkrow0, sinkrow1, own_side, ok_band, ok_meta):
    low = lambda x: lax.broadcasted_iota(jnp.int32, x.shape, 1) < HEAD_DIM
    scale = HEAD_DIM ** -0.5
    neg = jnp.finfo(F32).min
    wide = lambda m: jnp.concatenate([m] * 4, axis=1)
    own4, band4, meta4 = wide(own_side), wide(ok_band), wide(ok_meta)
    outs = []
    for g, (qa, qb, sinkrow) in enumerate(((q0, q1, sinkrow0), (q2, q3, sinkrow1))):
        def both(x, g=g):
            sw = _swap_halves(x)
            return jnp.where(low(x), x, sw) if g == 0 else jnp.where(low(x), sw, x)
        q4 = jnp.concatenate([jnp.where(low(qa), qa, 0.0), jnp.where(low(qa), 0.0, qa),
                              jnp.where(low(qb), qb, 0.0), jnp.where(low(qb), 0.0, qb)], axis=0)
        s = jnp.where(own4, _mm(both(kc), q4, "nt"), _mm(both(kp), q4, "nt"))
        s = jnp.where(band4, s * scale, neg)
        sm = jnp.where(meta4, _mm(both(km), q4, "nt") * scale, neg)
        mx = jnp.maximum(jnp.maximum(jnp.max(s, axis=0, keepdims=True), jnp.max(sm, axis=0, keepdims=True)), sinkrow)
        mx = lax.stop_gradient(mx)
        p, pm = jnp.exp(s - mx), jnp.exp(sm - mx)
        inv = 1.0 / (jnp.sum(p, axis=0, keepdims=True) + jnp.sum(pm, axis=0, keepdims=True) + jnp.exp(sinkrow - mx))
        p = p * inv
        o4 = (_mm(jnp.where(own4, p, 0.0), both(vc), "tn") + _mm(jnp.where(own4, 0.0, p), both(vp), "tn")
              + _mm(pm * inv, both(vm), "tn"))
        for j in range(2):
            upper = o4[(2 * j) * BLOCK:(2 * j + 1) * BLOCK]
            outs.append(jnp.where(low(upper), upper, o4[(2 * j + 1) * BLOCK:(2 * j + 2) * BLOCK]))
    return jnp.concatenate(outs, axis=1)


def _att_masks(blk_idx):
    kidx = lax.broadcasted_iota(jnp.int32, (BLOCK, BLOCK), 0)
    qrow = lax.broadcasted_iota(jnp.int32, (BLOCK, BLOCK), 1)
    own_side = kidx <= qrow
    pos_own = blk_idx * BLOCK + kidx - PAD
    ok_band = (own_side & (pos_own >= N_META)) | (~own_side & (pos_own - BLOCK >= N_META) & (blk_idx >= 1))
    qpos = blk_idx * BLOCK + lax.broadcasted_iota(jnp.int32, (N_META, BLOCK), 1) - PAD
    ok_meta = lax.broadcasted_iota(jnp.int32, (N_META, BLOCK), 0) <= qpos
    return own_side, ok_band, ok_meta


def _token_streams(tr, tile_of=lambda i: i):
    k = tr // BLOCK
    return [pl.BlockSpec((BLOCK, D_MODEL), lambda i, j=j: (jnp.maximum(k * tile_of(i) - 1 + j, 0), 0))
            for j in range(k)]


def _embed_ln(x, meta_shard, w_in_shard, g0, b0):
    p = x.shape[0] + BLOCK
    tr = _row_tile(p, 640)
    k = tr // BLOCK
    nt = p // tr
    tile_of = lambda s: (s + 1) % nt
    shards = [meta_shard, w_in_shard]
    c_in, c_out, c_shapes, c_sems = _comm_specs(shards, N_DEV)

    def body(*refs):
        g_ref, b_ref = refs[k:k + 2]
        h_ref, hb_ref, xh_ref, rs_ref = refs[k + 4:k + 8]
        out_refs = refs[k + 8:k + 10]
        lead_ref, meta_ref = refs[k + 10:k + 12]
        starts, passes, waits = _gather_behind(refs[k + 2:k + 4], out_refs, refs[k + 12:], [False, False])
        s = pl.program_id(0)
        t = tile_of(s)

        @pl.when(s == 0)
        def _():
            lead_ref[...] = jnp.zeros_like(lead_ref)
            for start in starts:
                start()

        @pl.when(s == nt - 1)
        def _():
            for step in passes + waits:
                step()
            pltpu.sync_copy(out_refs[0], meta_ref)
            for d in range(N_DEV):
                lead_ref[PAD:BLOCK, d * BLOCK:(d + 1) * BLOCK] = meta_ref[d]

        first = jnp.where(t == 0, lead_ref[...], refs[0][...])
        xhat, rstd = _ln_stats(jnp.concatenate([first] + [r[...] for r in refs[1:k]], axis=0))
        row = t * tr + lax.broadcasted_iota(jnp.int32, (tr, 1), 0)
        h = jnp.where(row >= PAD, xhat * g_ref[...] + b_ref[...], 0.0)
        h_ref[...] = h
        hb_ref[...] = h.astype(BF16)
        xh_ref[...] = xhat
        rs_ref[...] = rstd

    vec = pl.BlockSpec((1, D_MODEL), lambda s: (0, 0))
    rowsp = pl.BlockSpec((tr, D_MODEL), lambda s: (tile_of(s), 0))
    return pl.pallas_call(
        body, name="embed_ln", grid=(nt,),
        in_specs=_token_streams(tr, tile_of) + [vec, vec] + c_in,
        out_specs=[rowsp, rowsp, rowsp, pl.BlockSpec((tr, 1), lambda s: (tile_of(s), 0))] + c_out,
        out_shape=[jax.ShapeDtypeStruct((p, D_MODEL), F32), jax.ShapeDtypeStruct((p, D_MODEL), BF16),
                   jax.ShapeDtypeStruct((p, D_MODEL), F32), jax.ShapeDtypeStruct((p, 1), F32)] + c_shapes,
        scratch_shapes=[pltpu.VMEM((BLOCK, D_MODEL), F32), pltpu.VMEM((N_DEV, N_META, BLOCK), F32)] + c_sems,
        compiler_params=_cparams(("arbitrary",)),
    )(*([x] * k), g0, b0, *shards)


def _rope_tables(p):
    pos = (np.arange(p, dtype=np.int32) - PAD).astype(np.float32)
    half = HEAD_DIM // 2
    inv = np.float32(ROPE_THETA) ** (-np.arange(half, dtype=np.float32) / np.float32(half))
    ang = pos[:, None] * np.tile(inv.astype(np.float32), BLOCK // half)[None, :]
    return jnp.asarray(np.cos(ang), F32), jnp.asarray(np.sin(ang), F32)


def _att_sinkrows(sink_ref):
    lanehead = lax.broadcasted_iota(jnp.int32, (1, 4 * BLOCK), 1) // BLOCK
    rows = []
    for g in range(2):
        row = jnp.zeros((1, 4 * BLOCK), F32)
        for j in range(4):
            row = jnp.where(lanehead == j, sink_ref[0, 4 * g + j], row)
        rows.append(row)
    return rows


def _first_half(rows):
    return (lax.broadcasted_iota(jnp.int32, (rows, BLOCK), 1) % HEAD_DIM) < (HEAD_DIM // 2)


def _att_load(qkv_ref, cos_ref, sin_ref, with_q):
    cos, sin, fh = cos_ref[...], sin_ref[...], _first_half(BLOCK)
    qs = [_rope(qkv_ref[:, j * BLOCK:(j + 1) * BLOCK], cos, sin, fh) for j in range(4)] if with_q else None
    k = _rope(qkv_ref[:, ATT_QW:ATT_QW + ATT_KVW], cos, sin, fh)
    v = qkv_ref[:, ATT_QW + ATT_KVW:ATT_QW + 2 * ATT_KVW]
    return qs, k, v


def _att_load_meta(qkv_ref, cos_ref, sin_ref):
    k = _rope(qkv_ref[PAD:BLOCK, ATT_QW:ATT_QW + ATT_KVW], cos_ref[PAD:BLOCK, :], sin_ref[PAD:BLOCK, :],
              _first_half(N_META))
    return k, qkv_ref[PAD:BLOCK, ATT_QW + ATT_KVW:ATT_QW + 2 * ATT_KVW]


def _att_specs(blk):
    w = ATT_QW + 2 * ATT_KVW
    cur = lambda width: pl.BlockSpec((BLOCK, width), lambda i: (blk(i), 0))
    prev = lambda width: pl.BlockSpec((BLOCK, width), lambda i: (jnp.maximum(blk(i) - 1, 0), 0))
    meta = lambda width: pl.BlockSpec((BLOCK, width), lambda i: (0, 0))
    return [cur(w), prev(w), meta(w), cur(BLOCK), cur(BLOCK), prev(BLOCK), prev(BLOCK), meta(BLOCK), meta(BLOCK),
            pl.BlockSpec(memory_space=pltpu.SMEM)]


_FLIPS = [(dx, dy, dc) for dx in (0, 1) for dy in (0, 1) for dc in (0, 1)][1:]
N_PEERS = len(_FLIPS)


def _place():
    return lax.axis_index("x"), lax.axis_index("y"), lax.axis_index("c")


def _peer(place, flip):
    return tuple(1 - p if f else p for p, f in zip(place, flip))


def _slot(place, swapped):
    x, y, c = place
    return 4 * y + 2 * x + c if swapped else 4 * x + 2 * y + c


def _comm_specs(arrs, out_lead):
    n = len(arrs)
    outs = [jax.ShapeDtypeStruct((out_lead,) + a.shape[-2:], a.dtype) for a in arrs]
    sems = [pltpu.SemaphoreType.DMA((n, N_PEERS)), pltpu.SemaphoreType.DMA((n, N_PEERS)), pltpu.SemaphoreType.DMA((n,))]
    return [pl.BlockSpec(memory_space=pl.ANY)] * n, [pl.BlockSpec(memory_space=pl.ANY)] * n, outs, sems


def _gather_behind(shard_refs, out_refs, sems, swapped):
    send_sems, recv_sems, local_sems = sems
    x, y, c = _place()
    me, sibling = (x, y, c), (x, y, 1 - c)
    chips = [(1 - x, y), (x, 1 - y), (1 - x, 1 - y)]
    starts, passes, waits = [], [], []
    for w, (s, o) in enumerate(zip(shard_refs, out_refs)):
        def copy(k, block, to, from_shard=False, w=w, s=s, o=o):
            rows = o.at[_slot(block, swapped[w])]
            return pltpu.make_async_remote_copy(
                src_ref=s if from_shard else rows, dst_ref=rows, send_sem=send_sems.at[w, k],
                recv_sem=recv_sems.at[w, k], device_id=to, device_id_type=MESH)

        own = pltpu.make_async_copy(s, o.at[_slot(me, swapped[w])], local_sems.at[w])
        first = [copy(0, me, sibling, True)] + [copy(1 + j, me, (*chip, c), True) for j, chip in enumerate(chips)]
        handed = [copy(4 + j, (*chip, c), sibling) for j, chip in enumerate(chips)]
        starts += [own.start] + [cp.start for cp in first]
        for j, chip in enumerate(chips):
            passes += [copy(1 + j, (*chip, c), me).wait_recv, handed[j].start]
        waits.append(copy(0, sibling, me).wait_recv)
        waits += [copy(4 + j, (*chip, 1 - c), me).wait_recv for j, chip in enumerate(chips)]
        waits += [cp.wait_send for cp in first + handed] + [own.wait]
    return starts, passes, waits


def _scatter_behind(part_refs, recv_refs, sems, swapped):
    send_sems, recv_sems, _ = sems
    place = _place()
    starts, waits = [], []
    for w, (p, o) in enumerate(zip(part_refs, recv_refs)):
        for r, flip in enumerate(_FLIPS):
            peer = _peer(place, flip)
            cp = pltpu.make_async_remote_copy(
                src_ref=p.at[_slot(peer, swapped[w])], dst_ref=o.at[r], send_sem=send_sems.at[w, r],
                recv_sem=recv_sems.at[w, r], device_id=peer, device_id_type=MESH)
            starts.append(cp.start)
            waits += [cp.wait_recv, cp.wait_send]
    return starts, waits


def _mixers_fwd(proj_hg, proj_att, lbounds, norm_g, lv, cos, sin, sinks, shards, swapped):
    p = proj_hg.shape[0]
    nb = p // BLOCK
    n = len(shards)
    c_in, c_out, c_shapes, c_sems = _comm_specs(shards, N_DEV)
    pass_step = min(nb - 1, max(1, (5 * nb) // 8))

    def body(*refs):
        x_ref, lb_ref, ng_ref, lv_ref, cur_ref, prev_ref, meta_ref, cc, sc, cp, sp, cm, sm, sink_ref = refs[:14]
        shard_refs = refs[14:14 + n]
        y_ref, st_ref, o_ref = refs[14 + n:17 + n]
        out_refs = refs[17 + n:17 + 2 * n]
        carry_ref = refs[17 + 2 * n]
        starts, passes, waits = _gather_behind(shard_refs, out_refs, refs[18 + 2 * n:], swapped)
        c = pl.program_id(0)

        @pl.when(c == 0)
        def _():
            carry_ref[...] = jnp.zeros_like(carry_ref)
            for start in starts:
                start()

        @pl.when(c == pass_step)
        def _():
            for step in passes:
                step()

        valid = (c * BLOCK + lax.broadcasted_iota(jnp.int32, (BLOCK, 1), 0)) >= PAD
        logf, k = _hgrn_gates(x_ref[:, HG_W:2 * HG_W], lb_ref[0:1, :], lb_ref[1:2, :], valid)
        e = _split_dot(lv_ref[...], logf, "nn")
        for h in range(HG_HEADS):
            sl = lambda part: x_ref[:, part * HG_W + h * BLOCK: part * HG_W + (h + 1) * BLOCK]
            hs = slice(h * BLOCK, (h + 1) * BLOCK)
            st_in = carry_ref[h]
            st_ref[0, h] = st_in
            y, st_out = _hgrn_head(sl(0), k[:, hs], sl(2), sl(3), ng_ref[...], st_in, *_seg_blocks(e, h))
            y_ref[:, hs] = y.astype(BF16)
            carry_ref[h] = st_out

        qs, kc, vc = _att_load(cur_ref, cc, sc, True)
        _, kp, vp = _att_load(prev_ref, cp, sp, False)
        km, vm = _att_load_meta(meta_ref, cm, sm)
        s0, s1 = _att_sinkrows(sink_ref)
        o_ref[...] = _att_core(*qs, km, kp, kc, vm, vp, vc, s0, s1, *_att_masks(c)).astype(BF16)

        @pl.when(c == nb - 1)
        def _():
            for wait in waits:
                wait()

    return pl.pallas_call(
        body, name="mixers_fwd", grid=(nb,),
        in_specs=[pl.BlockSpec((BLOCK, 4 * HG_W), lambda c: (c, 0)), pl.BlockSpec((2, HG_W), lambda c: (0, 0)),
                  pl.BlockSpec((1, BLOCK), lambda c: (0, 0)), pl.BlockSpec(lv.shape, lambda c: (0, 0))]
        + _att_specs(lambda c: c) + c_in,
        out_specs=[pl.BlockSpec((BLOCK, HG_W), lambda c: (c, 0)),
                   pl.BlockSpec((1, HG_HEADS, BLOCK, BLOCK), lambda c: (c, 0, 0, 0)),
                   pl.BlockSpec((BLOCK, ATT_QW), lambda c: (c, 0))] + c_out,
        out_shape=[jax.ShapeDtypeStruct((p, HG_W), BF16), jax.ShapeDtypeStruct((nb, HG_HEADS, BLOCK, BLOCK), F32),
                   jax.ShapeDtypeStruct((p, ATT_QW), BF16)] + c_shapes,
        scratch_shapes=[pltpu.VMEM((HG_HEADS, BLOCK, BLOCK), F32)] + c_sems,
        compiler_params=_cparams(("arbitrary",)),
    )(proj_hg, lbounds, norm_g, lv, proj_att, proj_att, proj_att, cos, sin, cos, sin, cos, sin, sinks, *shards)


def _tile(rows, preferred):
    return preferred if rows % preferred == 0 else _row_tile(rows, preferred)


def _branch_mix(yh, oa, gates, w_bh, w_ba):
    y_hg = _dot(yh, w_bh, "nn")
    y_att = _dot(oa, w_ba, "nn")
    s1 = jax.nn.sigmoid(gates[:, :D_MODEL])
    s2 = jax.nn.sigmoid(gates[:, D_MODEL:])
    return s1 * y_hg + s2 * y_att, y_hg, y_att, s1, s2


def _mix_out_ln1(yh, oa, gates, h0, w_bh, w_ba, w_out, g1, b1):
    p = yh.shape[0]
    tr = _tile(p, 320)

    def body(yh_ref, oa_ref, g_ref, h0_ref, wbh_ref, wba_ref, wo_ref, g1_ref, b1_ref,
             mix_ref, h1_ref, h1b_ref, xh_ref, rs_ref):
        mixin = _branch_mix(yh_ref[...], oa_ref[...], g_ref[...], wbh_ref[...], wba_ref[...])[0]
        mix_ref[...] = mixin.astype(BF16)
        xhat, rstd = _ln_stats(ALPHA * h0_ref[...] + _dot(mixin, wo_ref[...], "nn"))
        h1 = xhat * g1_ref[...] + b1_ref[...]
        h1_ref[...] = h1
        h1b_ref[...] = h1.astype(BF16)
        xh_ref[...] = xhat
        rs_ref[...] = rstd

    row = lambda w: pl.BlockSpec((tr, w), lambda i: (i, 0))
    const = lambda a: pl.BlockSpec(a.shape, lambda i: (0, 0))
    return pl.pallas_call(
        body, name="mix_out_ln1", grid=(p // tr,),
        in_specs=[row(HG_W), row(ATT_QW), row(2 * D_MODEL), row(D_MODEL), const(w_bh), const(w_ba), const(w_out),
                  const(g1), const(b1)],
        out_specs=[row(D_MODEL), row(D_MODEL), row(D_MODEL), row(D_MODEL), row(1)],
        out_shape=[jax.ShapeDtypeStruct((p, D_MODEL), BF16), jax.ShapeDtypeStruct((p, D_MODEL), F32),
                   jax.ShapeDtypeStruct((p, D_MODEL), BF16), jax.ShapeDtypeStruct((p, D_MODEL), F32),
                   jax.ShapeDtypeStruct((p, 1), F32)],
        compiler_params=_cparams(("arbitrary",)),
    )(yh, oa, gates, h0, w_bh, w_ba, w_out, g1, b1)


FF_T = D_FF // 2


def _ffn_in_swiglu(h1, w_fi):
    p = h1.shape[0]
    tm = _row_tile(p, 640)

    def body(h_ref, w_ref, au_ref, s_ref):
        au = _dot(h_ref[...], w_ref[...], "nn")
        au_ref[...] = au.astype(BF16)
        s_ref[...] = (jax.nn.silu(au[:, :FF_T]) * au[:, FF_T:]).astype(BF16)

    return pl.pallas_call(
        body, name="ffn_in_swiglu", grid=(D_FF // FF_T, p // tm),
        in_specs=[pl.BlockSpec((tm, D_MODEL), lambda j, i: (i, 0)), pl.BlockSpec((D_MODEL, 2 * FF_T), lambda j, i: (0, j))],
        out_specs=[pl.BlockSpec((tm, 2 * FF_T), lambda j, i: (i, j)), pl.BlockSpec((tm, FF_T), lambda j, i: (i, j))],
        out_shape=[jax.ShapeDtypeStruct((p, 2 * D_FF), BF16), jax.ShapeDtypeStruct((p, D_FF), BF16)],
        compiler_params=_cparams(("arbitrary", "arbitrary")),
    )(h1, w_fi)


def _ffn_out_loss(s, w_fo, h1, g2, b2, target):
    p = h1.shape[0]
    tr = _row_tile(p, 640)
    k = tr // BLOCK

    def body(*refs):
        s_ref, w_ref, h_ref, g_ref, b_ref = refs[:5]
        dr_ref, loss_ref, dg_ref, db_ref = refs[5 + k:]
        i = pl.program_id(0)
        xhat, rstd = _ln_stats(ALPHA * h_ref[...] + _dot(s_ref[...], w_ref[...], "nn"))
        y = xhat * g_ref[...] + b_ref[...]
        row = i * tr + lax.broadcasted_iota(jnp.int32, (tr, 1), 0)
        tgt = jnp.concatenate([r[...] for r in refs[5:5 + k]], axis=0)
        err = jnp.where(row >= BLOCK, y - tgt, 0.0)
        dr, dg, db = _ln_bwd(err * (1.0 / D_MODEL), xhat, rstd, g_ref[...])
        dr_ref[...] = dr
        e2 = jnp.sum(err * err, axis=0, keepdims=True)
        part = e2[:, 0:BLOCK]
        for j in range(1, D_MODEL // BLOCK):
            part = part + e2[:, j * BLOCK:(j + 1) * BLOCK]
        part = part * (0.5 / D_MODEL)

        @pl.when(i == 0)
        def _():
            loss_ref[...] = part
            dg_ref[...] = dg
            db_ref[...] = db

        @pl.when(i > 0)
        def _():
            loss_ref[...] += part
            dg_ref[...] += dg
            db_ref[...] += db

    vec = pl.BlockSpec((1, D_MODEL), lambda i: (0, 0))
    rowsp = pl.BlockSpec((tr, D_MODEL), lambda i: (i, 0))
    return pl.pallas_call(
        body, name="ffn_out_loss", grid=(p // tr,),
        in_specs=[pl.BlockSpec((tr, D_FF), lambda i: (i, 0)), pl.BlockSpec((D_FF, D_MODEL), lambda i: (0, 0)),
                  rowsp, vec, vec] + _token_streams(tr),
        out_specs=[rowsp, pl.BlockSpec((1, BLOCK), lambda i: (0, 0)), vec, vec],
        out_shape=[jax.ShapeDtypeStruct((p, D_MODEL), F32), jax.ShapeDtypeStruct((1, BLOCK), F32),
                   jax.ShapeDtypeStruct((1, D_MODEL), F32), jax.ShapeDtypeStruct((1, D_MODEL), F32)],
        compiler_params=_cparams(("arbitrary",)),
    )(s, w_fo, h1, g2, b2, *([target] * k))


def _d_ffn_hidden(dr2, w_fo, au):
    p = au.shape[0]
    tm = _row_tile(p, 640)

    def body(d_ref, w_ref, au_ref, o_ref):
        ds = _dot(d_ref[...], w_ref[...], "nt")
        _, vjp = jax.vjp(lambda a, u: jax.nn.silu(a) * u, au_ref[:, :FF_T].astype(F32), au_ref[:, FF_T:].astype(F32))
        da, du = vjp(ds)
        o_ref[:, :FF_T] = da.astype(BF16)
        o_ref[:, FF_T:] = du.astype(BF16)

    return pl.pallas_call(
        body, name="d_ffn_hidden", grid=(D_FF // FF_T, p // tm),
        in_specs=[pl.BlockSpec((tm, D_MODEL), lambda j, i: (i, 0)), pl.BlockSpec((FF_T, D_MODEL), lambda j, i: (j, 0)),
                  pl.BlockSpec((tm, 2 * FF_T), lambda j, i: (i, j))],
        out_specs=pl.BlockSpec((tm, 2 * FF_T), lambda j, i: (i, j)),
        out_shape=jax.ShapeDtypeStruct((p, 2 * D_FF), BF16), compiler_params=_cparams(("arbitrary", "arbitrary")),
    )(dr2, w_fo, au)


def _ln1_mix_bwd(dr2, dh1_ffn, xhat1, rstd1, g1, yh, oa, gates, w_bh, w_ba, w_out):
    p = yh.shape[0]
    tr = _tile(p, 320)

    def body(a_ref, b_ref, xh_ref, rs_ref, g1_ref, yh_ref, oa_ref, g_ref, wbh_ref, wba_ref, wo_ref,
             dr_ref, dyhg_ref, dyat_ref, dgt_ref, dyh_ref, doa_ref, dg_ref, db_ref):
        i = pl.program_id(0)
        dr, dg, db = _ln_bwd(ALPHA * a_ref[...] + b_ref[...], xh_ref[...], rs_ref[...], g1_ref[...])
        dr_ref[...] = dr
        d = _dot(dr, wo_ref[...], "nt")
        _, y_hg, y_att, s1, s2 = _branch_mix(yh_ref[...], oa_ref[...], g_ref[...], wbh_ref[...], wba_ref[...])
        dy_hg = d * s1
        dy_att = d * s2
        dyhg_ref[...] = dy_hg.astype(BF16)
        dyat_ref[...] = dy_att.astype(BF16)
        dgt_ref[:, :D_MODEL] = (d * y_hg * s1 * (1.0 - s1)).astype(BF16)
        dgt_ref[:, D_MODEL:] = (d * y_att * s2 * (1.0 - s2)).astype(BF16)
        dyh_ref[...] = _dot(dy_hg, wbh_ref[...], "nt")
        doa_ref[...] = _dot(dy_att, wba_ref[...], "nt")

        @pl.when(i == 0)
        def _():
            dg_ref[...] = dg
            db_ref[...] = db

        @pl.when(i > 0)
        def _():
            dg_ref[...] += dg
            db_ref[...] += db

    row = lambda w: pl.BlockSpec((tr, w), lambda i: (i, 0))
    const = lambda a: pl.BlockSpec(a.shape, lambda i: (0, 0))
    vec = pl.BlockSpec((1, D_MODEL), lambda i: (0, 0))
    return pl.pallas_call(
        body, name="ln1_mix_bwd", grid=(p // tr,),
        in_specs=[row(D_MODEL), row(D_MODEL), row(D_MODEL), row(1), vec, row(HG_W), row(ATT_QW), row(2 * D_MODEL),
                  const(w_bh), const(w_ba), const(w_out)],
        out_specs=[row(D_MODEL), row(D_MODEL), row(D_MODEL), row(2 * D_MODEL), row(HG_W), row(ATT_QW), vec, vec],
        out_shape=[jax.ShapeDtypeStruct((p, D_MODEL), F32), jax.ShapeDtypeStruct((p, D_MODEL), BF16),
                   jax.ShapeDtypeStruct((p, D_MODEL), BF16), jax.ShapeDtypeStruct((p, 2 * D_MODEL), BF16),
                   jax.ShapeDtypeStruct((p, HG_W), F32), jax.ShapeDtypeStruct((p, ATT_QW), F32),
                   jax.ShapeDtypeStruct((1, D_MODEL), F32), jax.ShapeDtypeStruct((1, D_MODEL), F32)],
        compiler_params=_cparams(("arbitrary",)),
    )(dr2, dh1_ffn, xhat1, rstd1, g1, yh, oa, gates, w_bh, w_ba, w_out)


MIX_W = 4 * HG_W + ATT_QW + 2 * ATT_KVW


def _mixers_bwd(proj_hg, proj_att, lbounds, norm_g, lv, states, cos, sin, sinks, dyh, doa, parts, swapped):
    p = proj_hg.shape[0]
    nb = p // BLOCK
    n = len(parts)
    kvw = 2 * ATT_KVW
    rev = lambda s: nb - 1 - s
    c_in, c_out, c_shapes, c_sems = _comm_specs(parts, N_PEERS)

    def body(*refs):
        (x_ref, lb_ref, ng_ref, lv_ref, st_ref, cur_ref, prev_ref, meta_ref, cc, sc, cp, sp, cm, sm, sink_ref,
         dy_ref, do_ref) = refs[:17]
        part_refs = refs[17:17 + n]
        dx_ref, dlb_ref, dng_ref, dsink_ref = refs[17 + n:21 + n]
        recv_refs = refs[21 + n:21 + 2 * n]
        dcarry_ref, dkv_next_ref, dkv_meta_ref = refs[21 + 2 * n:24 + 2 * n]
        starts, waits = _scatter_behind(part_refs, recv_refs, refs[24 + 2 * n:], swapped)
        step = pl.program_id(0)
        c = rev(step)

        @pl.when(step == 0)
        def _():
            dcarry_ref[...] = jnp.zeros_like(dcarry_ref)
            dkv_next_ref[...] = jnp.zeros_like(dkv_next_ref)
            dkv_meta_ref[...] = jnp.zeros_like(dkv_meta_ref)
            dlb_ref[...] = jnp.zeros_like(dlb_ref)
            dng_ref[...] = jnp.zeros_like(dng_ref)
            dsink_ref[...] = jnp.zeros_like(dsink_ref)
            for start in starts:
                start()

        fh = _first_half(BLOCK)
        qs, kc, vc = _att_load(cur_ref, cc, sc, True)
        _, kp, vp = _att_load(prev_ref, cp, sp, False)
        km, vm = _att_load_meta(meta_ref, cm, sm)
        s0, s1 = _att_sinkrows(sink_ref)
        masks = _att_masks(c)
        _, att_vjp = jax.vjp(lambda *a: _att_core(*a, *masks), *qs, km, kp, kc, vm, vp, vc, s0, s1)
        dq0, dq1, dq2, dq3, dkm, dkp, dkc, dvm, dvp, dvc, ds0, ds1 = att_vjp(do_ref[...])
        att0 = 4 * HG_W
        for j, dq in enumerate((dq0, dq1, dq2, dq3)):
            dx_ref[:, att0 + j * BLOCK:att0 + (j + 1) * BLOCK] = _rope_t(dq, cc[...], sc[...], fh).astype(BF16)
        dkv_meta_ref[:, :BLOCK] += _rope_t(dkm, cm[PAD:BLOCK, :], sm[PAD:BLOCK, :], _first_half(N_META))
        dkv_meta_ref[:, BLOCK:] += dvm
        last = jnp.where(c == 0, 1.0, 0.0)
        to_meta_rows = lambda m: jnp.concatenate([jnp.zeros((PAD, BLOCK), F32), last * m], axis=0)
        dk = _rope_t(dkc, cc[...], sc[...], fh) + dkv_next_ref[:, :BLOCK] + to_meta_rows(dkv_meta_ref[:, :BLOCK])
        dv = dvc + dkv_next_ref[:, BLOCK:] + to_meta_rows(dkv_meta_ref[:, BLOCK:])
        dx_ref[:, att0 + ATT_QW:att0 + ATT_QW + ATT_KVW] = dk.astype(BF16)
        dx_ref[:, att0 + ATT_QW + ATT_KVW:] = dv.astype(BF16)
        dkv_next_ref[:, :BLOCK] = _rope_t(dkp, cp[...], sp[...], fh)
        dkv_next_ref[:, BLOCK:] = dvp
        sink_rows = []
        for dsg in (ds0, ds1):
            for j in range(4):
                tot = jnp.sum(dsg[:, j * BLOCK:(j + 1) * BLOCK], axis=1, keepdims=True)
                sink_rows.append(jnp.broadcast_to(tot, (1, BLOCK)))
        dsink_ref[...] += jnp.concatenate(sink_rows, axis=0)

        valid = (c * BLOCK + lax.broadcasted_iota(jnp.int32, (BLOCK, 1), 0)) >= PAD
        (logf, k), gates_vjp = jax.vjp(lambda hf, a0, a1: _hgrn_gates(hf, a0, a1, valid),
                                       x_ref[:, HG_W:2 * HG_W], lb_ref[0:1, :], lb_ref[1:2, :])
        lvv = lv_ref[...]
        e = _split_dot(lvv, logf, "nn")
        dng = jnp.zeros((1, BLOCK), F32)
        dk, dseg = [], []
        for h in range(HG_HEADS):
            sl = lambda part: x_ref[:, part * HG_W + h * BLOCK: part * HG_W + (h + 1) * BLOCK]
            hs = slice(h * BLOCK, (h + 1) * BLOCK)
            _, vjp = jax.vjp(_hgrn_head, sl(0), k[:, hs], sl(2), sl(3), ng_ref[...], st_ref[0, h], *_seg_blocks(e, h))
            dhq, dkh, dhi, dhg, dngh, dst, *dsegh = vjp((dy_ref[:, hs], dcarry_ref[h]))
            for part, val in ((0, dhq), (2, dhi), (3, dhg)):
                dx_ref[:, part * HG_W + h * BLOCK: part * HG_W + (h + 1) * BLOCK] = val.astype(BF16)
            dk.append(dkh)
            dseg.append(jnp.concatenate(dsegh, axis=0))
            dng = dng + dngh
            dcarry_ref[h] = dst
        dlogf = _split_dot(lvv, jnp.concatenate(dseg, axis=1), "tn")
        dhf, da0, da1 = gates_vjp((dlogf, jnp.concatenate(dk, axis=1)))
        dx_ref[:, HG_W:2 * HG_W] = dhf.astype(BF16)
        dlb_ref[0:1, :] += da0
        dlb_ref[1:2, :] += da1
        dng_ref[...] += dng

        @pl.when(step == nb - 1)
        def _():
            for wait in waits:
                wait()

    const = lambda shape: pl.BlockSpec(shape, lambda s: (0,) * len(shape))
    return pl.pallas_call(
        body, name="mixers_bwd", grid=(nb,),
        in_specs=[pl.BlockSpec((BLOCK, 4 * HG_W), lambda s: (rev(s), 0)), const((2, HG_W)), const((1, BLOCK)),
                  const(lv.shape), pl.BlockSpec((1, HG_HEADS, BLOCK, BLOCK), lambda s: (rev(s), 0, 0, 0))]
        + _att_specs(rev)
        + [pl.BlockSpec((BLOCK, HG_W), lambda s: (rev(s), 0)), pl.BlockSpec((BLOCK, ATT_QW), lambda s: (rev(s), 0))]
        + c_in,
        out_specs=[pl.BlockSpec((BLOCK, MIX_W), lambda s: (rev(s), 0)), const((2, HG_W)), const((1, BLOCK)),
                   const((ATT_HEADS, BLOCK))] + c_out,
        out_shape=[jax.ShapeDtypeStruct((p, MIX_W), BF16), jax.ShapeDtypeStruct((2, HG_W), F32),
                   jax.ShapeDtypeStruct((1, BLOCK), F32), jax.ShapeDtypeStruct((ATT_HEADS, BLOCK), F32)] + c_shapes,
        scratch_shapes=[pltpu.VMEM((HG_HEADS, BLOCK, BLOCK), F32), pltpu.VMEM((BLOCK, kvw), F32),
                        pltpu.VMEM((N_META, kvw), F32)] + c_sems,
        compiler_params=_cparams(("arbitrary",)),
    )(proj_hg, lbounds, norm_g, lv, states, proj_att, proj_att, proj_att, cos, sin, cos, sin, cos, sin, sinks,
      dyh, doa, *parts)


def _embed_bwd(dmix, dgates, w_mix, w_gates, dr1, xhat0, rstd0, g0, parts, swapped):
    p = dmix.shape[0]
    tm = _row_tile(p, 640)
    nm = p // tm
    n = len(parts)
    c_in, c_out, c_shapes, c_sems = _comm_specs(parts, N_PEERS)

    def body(*refs):
        a_ref, g_ref, wa_ref, wg_ref, dr_ref, xh_ref, rs_ref, g0_ref = refs[:8]
        o_ref, dg_ref, db_ref = refs[8 + n:11 + n]
        starts, waits = _scatter_behind(refs[8:8 + n], refs[11 + n:11 + 2 * n], refs[11 + 2 * n:], swapped)
        i = pl.program_id(0)

        @pl.when(i == 0)
        def _():
            for start in starts:
                start()

        dh0 = ALPHA * dr_ref[...] + _dot(a_ref[...], wa_ref[...], "nt") + _dot(g_ref[...], wg_ref[...], "nt")
        row = i * tm + lax.broadcasted_iota(jnp.int32, (tm, 1), 0)
        dx, dg, db = _ln_bwd(jnp.where(row >= PAD, dh0, 0.0), xh_ref[...], rs_ref[...], g0_ref[...])
        o_ref[...] = dx

        @pl.when(i == 0)
        def _():
            dg_ref[...] = dg
            db_ref[...] = db

        @pl.when(i > 0)
        def _():
            dg_ref[...] += dg
            db_ref[...] += db

        @pl.when(i == nm - 1)
        def _():
            for wait in waits:
                wait()

    row = lambda w: pl.BlockSpec((tm, w), lambda i: (i, 0))
    const = lambda a: pl.BlockSpec(a.shape, lambda i: (0, 0))
    vec = pl.BlockSpec((1, D_MODEL), lambda i: (0, 0))
    return pl.pallas_call(
        body, name="embed_bwd", grid=(nm,),
        in_specs=[row(dmix.shape[1]), row(dgates.shape[1]), const(w_mix), const(w_gates), row(D_MODEL), row(D_MODEL),
                  row(1), vec] + c_in,
        out_specs=[row(D_MODEL), vec, vec] + c_out,
        out_shape=[jax.ShapeDtypeStruct((p, D_MODEL), F32), jax.ShapeDtypeStruct((1, D_MODEL), F32),
                   jax.ShapeDtypeStruct((1, D_MODEL), F32)] + c_shapes,
        scratch_shapes=c_sems, compiler_params=_cparams(("arbitrary",)),
    )(dmix, dgates, w_mix, w_gates, dr1, xhat0, rstd0, g0, *parts)


_LATE = ("w_branch_hg", "w_branch_attn", "w_out", "w_ffn_in", "w_ffn_out")
_COLUMN_SHARDED = ("meta_tokens", "w_in", "w_branch_hg", "w_branch_attn", "w_ffn_in")
_SWAPPED = ("w_ffn_in",)


def _whole(name, gathered):
    _, r, c = gathered.shape
    if name in _COLUMN_SHARDED:
        return jnp.transpose(gathered, (1, 0, 2)).reshape(r, N_DEV * c)
    return gathered.reshape(N_DEV * r, c)


def _slots(name, whole):
    r, c = whole.shape
    if name in _COLUMN_SHARDED:
        return jnp.transpose(whole.reshape(r, N_DEV, c // N_DEV), (1, 0, 2))
    return whole.reshape(N_DEV, r // N_DEV, c)


def _device_step(x, target, meta_shard, ln_emb_g, ln_emb_b, w_in_shard, lbounds, norm_g, sinks, late_shards,
                 ln1_g, ln1_b, ln2_g, ln2_b):
    s = x.shape[0]
    p = s + BLOCK
    tm = _row_tile(p, 640)
    lv = _level_stack()
    cos, sin = _rope_tables(p)
    hg_end = 4 * HG_W
    mm = functools.partial(_tiled_matmul, tm=tm)
    swapped = [n in _SWAPPED for n in _LATE]

    h0, h0b, xhat0, rstd0, _, g_win = _embed_ln(x, meta_shard, w_in_shard, ln_emb_g, ln_emb_b)
    w_in = _whole("w_in", g_win)
    proj_hg = mm(h0b, w_in[:, :hg_end], "nn", tn=hg_end, tc=D_MODEL, out_dtype=F32, name="proj_hg")
    proj_att = mm(h0b, w_in[:, hg_end:MIX_W], "nn", tn=MIX_W - hg_end, tc=D_MODEL, out_dtype=F32, name="proj_att")
    gates = mm(h0b, w_in[:, MIX_W:], "nn", tn=2 * D_MODEL, tc=D_MODEL, out_dtype=F32, name="proj_gates")
    yh, states, oa, *gathered = _mixers_fwd(proj_hg, proj_att, lbounds, norm_g, lv, cos, sin, sinks, late_shards, swapped)
    w_bh, w_ba, w_out, w_fi, w_fo = [_whole(n, g) for n, g in zip(_LATE, gathered)]
    mixin, h1, h1b, xhat1, rstd1 = _mix_out_ln1(yh, oa, gates, h0, w_bh, w_ba, w_out, ln1_g, ln1_b)
    au, sw = _ffn_in_swiglu(h1b, w_fi)
    dr2, loss_part, dg2, db2 = _ffn_out_loss(sw, w_fo, h1, ln2_g, ln2_b, target)

    mtn = functools.partial(_tiled_matmul_tn, tm=_row_tile(p, 1664), out_dtype=BF16)
    d_wfo = mtn(sw, dr2, tk=FF_T, tn=D_MODEL, name="grad_w_ffn_out")
    dau = _d_ffn_hidden(dr2, w_fo, au)
    d_wfi = mtn(h1b, dau, tk=D_MODEL, tn=FF_T, name="grad_w_ffn_in")
    dh1_ffn = mm(dau, w_fi, "nt", tn=D_MODEL, tc=D_FF, out_dtype=F32, name="d_h1_ffn")
    dr1, dy_hg, dy_att, dgates, dyh, doa, dg1, db1 = _ln1_mix_bwd(
        dr2, dh1_ffn, xhat1, rstd1, ln1_g, yh, oa, gates, w_bh, w_ba, w_out)
    d_wout = mtn(mixin, dr1, tk=D_MODEL, tn=D_MODEL, name="grad_w_out")
    d_wbh = mtn(yh, dy_hg, tk=HG_W, tn=D_MODEL, name="grad_w_branch_hg")
    d_wba = mtn(oa, dy_att, tk=ATT_QW, tn=D_MODEL, name="grad_w_branch_attn")
    late_parts = [_slots(n, g) for n, g in zip(_LATE, (d_wbh, d_wba, d_wout, d_wfi, d_wfo))]
    dmix, d_lb, d_ng, d_sink, *late_recv = _mixers_bwd(
        proj_hg, proj_att, lbounds, norm_g, lv, states, cos, sin, sinks, dyh, doa, late_parts, swapped)
    d_win = jnp.concatenate([mtn(h0b, dmix, tk=D_MODEL, tn=MIX_W // 2, name="grad_w_in_mixers"),
                             mtn(h0b, dgates, tk=D_MODEL, tn=D_MODEL, name="grad_w_in_gates")], axis=1)
    win_parts = _slots("w_in", d_win)
    dxin, dg0, db0, win_recv = _embed_bwd(dmix, dgates, w_in[:, :MIX_W], w_in[:, MIX_W:], dr1, xhat0, rstd0, ln_emb_g,
                                          [win_parts], [False])

    small = dict(ln_emb_g=dg0, ln_emb_b=db0, hg_lower_bounds=d_lb, hg_norm_g=d_ng, attn_sinks=d_sink[:, 0],
                 ln1_g=dg1, ln1_b=db1, ln2_g=dg2, ln2_b=db2)
    big = dict(zip(_LATE, zip(late_parts, late_recv)))
    big["w_in"] = (win_parts, win_recv)
    return loss_part, dxin[BLOCK:], small, dxin[PAD:BLOCK], big


def _all_gather(arrs, dtypes, name):
    n = len(arrs)

    def body(*refs):
        ins, outs, stages = refs[:n], refs[n:2 * n], refs[2 * n:3 * n]
        send_sems, recv_sems, local_sems = refs[3 * n:]
        x, y, c = _place()
        sibling = (x, y, 1 - c)
        chips = [(1 - x, y), (x, 1 - y), (1 - x, 1 - y)]
        slot = lambda px, py, pc: 4 * px + 2 * py + pc

        def copy(w, k, block, to, from_stage=False):
            return pltpu.make_async_remote_copy(
                src_ref=stages[w] if from_stage else outs[w].at[slot(*block)], dst_ref=outs[w].at[slot(*block)],
                send_sem=send_sems.at[w, k], recv_sem=recv_sems.at[w, k], device_id=to, device_id_type=MESH)

        mine, first, passed = [], [], []
        for w in range(n):
            stages[w][...] = ins[w][...].astype(dtypes[w])
            mine.append(pltpu.make_async_copy(stages[w], outs[w].at[slot(x, y, c)], local_sems.at[w]))
            mine[-1].start()
        for w in range(n):
            first.append(copy(w, 0, (x, y, c), sibling, from_stage=True))
            first += [copy(w, 1 + j, (x, y, c), (*chip, c), from_stage=True) for j, chip in enumerate(chips)]
        for cp in first:
            cp.start()
        for j, chip in enumerate(chips):
            for w in range(n):
                copy(w, 1 + j, (*chip, c), (x, y, c)).wait_recv()
                passed.append(copy(w, 4 + j, (*chip, c), sibling))
                passed[-1].start()
        for w in range(n):
            copy(w, 0, sibling, (x, y, c)).wait_recv()
            for j, chip in enumerate(chips):
                copy(w, 4 + j, (*chip, 1 - c), (x, y, c)).wait_recv()
        for cp in first + passed:
            cp.wait_send()
        for cp in mine:
            cp.wait()

    return pl.pallas_call(
        body, name=name,
        in_specs=[pl.BlockSpec(memory_space=pltpu.VMEM)] * n,
        out_specs=[pl.BlockSpec(memory_space=pl.ANY)] * n,
        out_shape=[jax.ShapeDtypeStruct((N_DEV,) + a.shape, dt) for a, dt in zip(arrs, dtypes)],
        scratch_shapes=[pltpu.VMEM(a.shape, dt) for a, dt in zip(arrs, dtypes)]
        + [pltpu.SemaphoreType.DMA((n, 7)), pltpu.SemaphoreType.DMA((n, 7)), pltpu.SemaphoreType.DMA((n,))],
        compiler_params=pltpu.CompilerParams(vmem_limit_bytes=VMEM_LIMIT_BYTES),
    )(*arrs)


def _cast_shards(arrs):
    def body(*refs):
        for src, dst in zip(refs[:len(arrs)], refs[len(arrs):]):
            dst[...] = src[...].astype(BF16)

    return pl.pallas_call(body, name="cast_shards", out_shape=[jax.ShapeDtypeStruct(a.shape, BF16) for a in arrs],
                          compiler_params=pltpu.CompilerParams(vmem_limit_bytes=VMEM_LIMIT_BYTES))(*arrs)


def _shard_rows(rows):
    return rows if rows <= 512 else 256


def _adamw_math(w, g, m, v):
    m = ADAM_B1 * m + (1.0 - ADAM_B1) * g
    v = ADAM_B2 * v + (1.0 - ADAM_B2) * (g * g)
    m_hat = m / (1.0 - ADAM_B1 ** ADAM_STEP)
    v_hat = v / (1.0 - ADAM_B2 ** ADAM_STEP)
    delta = -ADAM_LR * (m_hat / (jnp.sqrt(v_hat) + ADAM_EPS) + ADAM_WD * w)
    return delta, m, v


def _reduce_adamw(parts, recv, own_slot, w, m, v, name):
    r, cdim = w.shape
    tr = _shard_rows(r)

    def body(idx_ref, p_ref, r_ref, w_ref, m_ref, v_ref, g_out, d_out, m_out, v_out):
        g = p_ref[0].astype(F32)
        for j in range(N_PEERS):
            g = g + r_ref[j].astype(F32)
        d, mn, vn = _adamw_math(w_ref[...], g, m_ref[...], v_ref[...])
        g_out[...] = g
        d_out[...] = d
        m_out[...] = mn
        v_out[...] = vn

    flat = pl.BlockSpec((tr, cdim), lambda i, idx_ref: (i, 0))
    return pl.pallas_call(
        body, name=name,
        grid_spec=pltpu.PrefetchScalarGridSpec(
            num_scalar_prefetch=1, grid=(r // tr,),
            in_specs=[pl.BlockSpec((1, tr, cdim), lambda i, idx_ref: (idx_ref[0], i, 0)),
                      pl.BlockSpec((N_PEERS, tr, cdim), lambda i, idx_ref: (0, i, 0)), flat, flat, flat],
            out_specs=[flat] * 4),
        out_shape=[jax.ShapeDtypeStruct((r, cdim), F32)] * 4,
        compiler_params=_cparams(("arbitrary",)),
    )(own_slot, parts, recv, w, m, v)


def _adamw_plain(w, g, m, v, name):
    def body(w_ref, g_ref, m_ref, v_ref, d_out, m_out, v_out):
        d_out[...], m_out[...], v_out[...] = _adamw_math(w_ref[...], g_ref[...], m_ref[...], v_ref[...])

    return pl.pallas_call(body, name=name, out_shape=[jax.ShapeDtypeStruct(w.shape, F32)] * 3)(w, g, m, v)


_SMALL_LAYOUT = (("ln_emb_g", 8), ("ln_emb_b", 8), ("hg_lower_bounds", 8), ("hg_norm_g", 1), ("attn_sinks", 1),
                 ("ln1_g", 8), ("ln1_b", 8), ("ln2_g", 8), ("ln2_b", 8))
_META_ROW = sum(r for _, r in _SMALL_LAYOUT)
_META_ROWS = N_META * D_MODEL // BLOCK
_LOSS_ROW = _META_ROW + _META_ROWS
SMALL_ROWS = 192


def _pack_small(vals, meta=None, loss_row=None):
    rows = []
    for name, nrows in _SMALL_LAYOUT:
        flat = vals[name].reshape(-1).astype(F32)
        flat = jnp.pad(flat, (0, nrows * BLOCK - flat.shape[0]))
        rows.append(flat.reshape(nrows, BLOCK))
    rows.append(jnp.zeros((_META_ROWS, BLOCK), F32) if meta is None else meta.reshape(_META_ROWS, BLOCK))
    rows.append(jnp.zeros((1, BLOCK), F32) if loss_row is None else loss_row)
    packed = jnp.concatenate(rows, axis=0)
    return jnp.pad(packed, ((0, SMALL_ROWS - packed.shape[0]), (0, 0)))


def _unpack_small(packed, shapes):
    out, row = {}, 0
    for name, nrows in _SMALL_LAYOUT:
        size = math.prod(shapes[name])
        out[name] = packed[row:row + nrows].reshape(-1)[:size].reshape(shapes[name])
        row += nrows
    return out


def _small_reduce_adamw(gathered, w, m, v):
    def body(g_ref, w_ref, m_ref, v_ref, g_out, d_out, m_out, v_out, loss_out):
        g = g_ref[0]
        for s in range(1, N_DEV):
            g = g + g_ref[s]
        d, mn, vn = _adamw_math(w_ref[...], g, m_ref[...], v_ref[...])
        g_out[...] = g
        d_out[...] = d
        m_out[...] = mn
        v_out[...] = vn
        loss_out[...] = jnp.broadcast_to(jnp.sum(g_ref[:, _LOSS_ROW, :]), (1, BLOCK))

    shp = jax.ShapeDtypeStruct((SMALL_ROWS, BLOCK), F32)
    return pl.pallas_call(body, name="small_reduce_adamw",
                          out_shape=[shp] * 4 + [jax.ShapeDtypeStruct((1, BLOCK), F32)])(gathered, w, m, v)


_WEIGHTS = ("meta_tokens", "ln_emb_g", "ln_emb_b", "w_in", "hg_lower_bounds", "hg_norm_g", "attn_sinks",
            "w_branch_hg", "w_branch_attn", "w_out", "ln1_g", "ln1_b", "w_ffn_in", "w_ffn_out", "ln2_g", "ln2_b")


def kernel(x, meta_tokens, ln_emb_g, ln_emb_b, w_in, hg_lower_bounds, hg_norm_g, attn_sinks, w_branch_hg, w_branch_attn, w_out, ln1_g, ln1_b, w_ffn_in, w_ffn_out, ln2_g, ln2_b, loss_target, m_meta_tokens, m_ln_emb_g, m_ln_emb_b, m_w_in, m_hg_lower_bounds, m_hg_norm_g, m_attn_sinks, m_w_branch_hg, m_w_branch_attn, m_w_out, m_ln1_g, m_ln1_b, m_w_ffn_in, m_w_ffn_out, m_ln2_g, m_ln2_b, v_meta_tokens, v_ln_emb_g, v_ln_emb_b, v_w_in, v_hg_lower_bounds, v_hg_norm_g, v_attn_sinks, v_w_branch_hg, v_w_branch_attn, v_w_out, v_ln1_g, v_ln1_b, v_w_ffn_in, v_w_ffn_out, v_ln2_g, v_ln2_b):
    given = dict(locals())
    weights = {n: given[n] for n in _WEIGHTS}
    mom1 = {n: given["m_" + n] for n in _WEIGHTS}
    mom2 = {n: given["v_" + n] for n in _WEIGHTS}
    shard2d = lambda a: a.reshape(a.shape[-2:])

    w_in_shard, *late_shards = _cast_shards([shard2d(weights[n]) for n in ("w_in",) + _LATE])
    loss_part, grad_x, small_grads, meta_grad, big = _device_step(
        x[0], loss_target[0], meta_tokens, ln_emb_g.reshape(1, -1), ln_emb_b.reshape(1, -1), w_in_shard,
        hg_lower_bounds, hg_norm_g, attn_sinks, late_shards, ln1_g, ln1_b, ln2_g, ln2_b)

    place = _place()
    out = {}
    for n, (parts, recv) in big.items():
        own = _slot(place, n in _SWAPPED).astype(jnp.int32).reshape(1)
        res = _reduce_adamw(parts, recv, own, shard2d(weights[n]), shard2d(mom1[n]), shard2d(mom2[n]), "adamw_" + n)
        out[n] = [r.reshape(weights[n].shape) for r in res]

    small_names = [n for n, _ in _SMALL_LAYOUT]
    packed = _pack_small(small_grads, meta_grad, loss_part)
    all_small, = _all_gather([packed], [F32], "gather_small")
    res = _small_reduce_adamw(all_small, _pack_small(weights), _pack_small(mom1), _pack_small(mom2))
    shapes = {n: weights[n].shape for n in small_names}
    unpacked = [_unpack_small(r, shapes) for r in res[:4]]
    for n in small_names:
        out[n] = [u[n] for u in unpacked]
    loss = res[4][0, 0]
    meta_whole = res[0][_META_ROW:_META_ROW + _META_ROWS].reshape(N_META, N_DEV, D_MODEL // N_DEV)
    g_meta_mine = lax.dynamic_index_in_dim(meta_whole, _slot(place, False), axis=1, keepdims=False)
    out["meta_tokens"] = [g_meta_mine, *_adamw_plain(meta_tokens, g_meta_mine, m_meta_tokens, v_meta_tokens,
                                                     "adamw_meta")]

    return (loss, grad_x[None], *[out[n][0] for n in _WEIGHTS], *[out[n][1] for n in _WEIGHTS],
            *[out[n][2] for n in _WEIGHTS], *[out[n][3] for n in _WEIGHTS])
```

```python
import functools
import math

import numpy as np
import jax
import jax.numpy as jnp
from jax import lax
from jax.experimental import pallas as pl
from jax.experimental.pallas import tpu as pltpu

F32 = jnp.float32
BF16 = jnp.bfloat16

D_MODEL = 1024
N_META = 16
BLOCK = 128
PAD = BLOCK - N_META
HG_HEADS = 4
HG_W = 512
ATT_HEADS = 8
HEAD_DIM = 64
ATT_QW = 512
ATT_KVW = 128
D_FF = 2816
EPS = 1e-5
ALPHA = 2.0 ** 0.25
ROPE_THETA = 10000.0
N_DEV = 8

ADAM_LR = 0.001
ADAM_B1 = 0.9
ADAM_B2 = 0.999
ADAM_EPS = 1e-08
ADAM_WD = 0.01
ADAM_STEP = 10

VMEM_LIMIT_BYTES = 56 * 1024 * 1024
MESH = pl.DeviceIdType.MESH

_LEVELS = (64, 32, 16, 8, 4, 2, 1)


def _cparams(sem):
    return pltpu.CompilerParams(dimension_semantics=sem, vmem_limit_bytes=VMEM_LIMIT_BYTES)


def _row_tile(rows, target):
    nb = rows // BLOCK
    best = 1
    for d in range(1, nb + 1):
        if nb % d == 0 and d * BLOCK <= target:
            best = d
    return best * BLOCK


_DN = {"nn": (((1,), (0,)), ((), ())), "nt": (((1,), (1,)), ((), ())), "tn": (((0,), (0,)), ((), ()))}


def _dot(a, b, form):
    return lax.dot_general(a.astype(BF16), b.astype(BF16), _DN[form], preferred_element_type=F32)


@functools.partial(jax.custom_vjp, nondiff_argnums=(2,))
def _mm(a, b, form):
    return _dot(a, b, form)


def _mm_fwd(a, b, form):
    a, b = a.astype(BF16), b.astype(BF16)
    return _dot(a, b, form), (a, b)


def _mm_bwd(form, res, g):
    a, b = res
    if form == "nn":
        return _dot(g, b, "nt"), _dot(a, g, "tn")
    if form == "nt":
        return _dot(g, b, "nn"), _dot(g, a, "tn")
    return _dot(b, g, "nt"), _dot(a, g, "nn")


_mm.defvjp(_mm_fwd, _mm_bwd)


def _split_dot(lv, x, form):
    return lax.dot_general(lv, x.astype(BF16), _DN[form], preferred_element_type=F32)


@jax.custom_vjp
def _swap_halves(x):
    return pltpu.roll(x, 64, 1)


_swap_halves.defvjp(lambda x: (pltpu.roll(x, 64, 1), None), lambda _, g: (pltpu.roll(g, 64, 1),))


def _tiled_matmul(a, b, form, *, tm, tn, tc, out_dtype, name):
    m, c = a.shape
    n = b.shape[1] if form == "nn" else b.shape[0]
    assert m % tm == 0 and n % tn == 0 and c % tc == 0, (name, a.shape, b.shape, tm, tn, tc)
    nc = c // tc

    def body(a_ref, b_ref, o_ref, *scratch):
        if nc == 1:
            o_ref[...] = _dot(a_ref[...], b_ref[...], form).astype(out_dtype)
            return
        acc_ref, = scratch
        ci = pl.program_id(2)

        @pl.when(ci == 0)
        def _():
            acc_ref[...] = jnp.zeros_like(acc_ref)

        acc_ref[...] += _dot(a_ref[...], b_ref[...], form)

        @pl.when(ci == nc - 1)
        def _():
            o_ref[...] = acc_ref[...].astype(out_dtype)

    b_spec = (pl.BlockSpec((tc, tn), lambda j, i, k: (k, j)) if form == "nn"
              else pl.BlockSpec((tn, tc), lambda j, i, k: (j, k)))
    return pl.pallas_call(
        body, name=name, grid=(n // tn, m // tm, nc),
        in_specs=[pl.BlockSpec((tm, tc), lambda j, i, k: (i, k)), b_spec],
        out_specs=pl.BlockSpec((tm, tn), lambda j, i, k: (i, j)),
        out_shape=jax.ShapeDtypeStruct((m, n), out_dtype),
        scratch_shapes=[] if nc == 1 else [pltpu.VMEM((tm, tn), F32)],
        compiler_params=_cparams(("arbitrary", "arbitrary", "arbitrary")),
    )(a, b)


def _tiled_matmul_tn(a, b, *, tm, tk, tn, out_dtype, name):
    m, k = a.shape
    n = b.shape[1]
    assert m % tm == 0 and k % tk == 0 and n % tn == 0, (name, a.shape, b.shape, tm, tk, tn)
    nm = m // tm

    def body(a_ref, b_ref, o_ref, acc_ref):
        mi = pl.program_id(2)

        @pl.when(mi == 0)
        def _():
            acc_ref[...] = jnp.zeros_like(acc_ref)

        acc_ref[...] += _dot(a_ref[...], b_ref[...], "tn")

        @pl.when(mi == nm - 1)
        def _():
            o_ref[...] = acc_ref[...].astype(out_dtype)

    return pl.pallas_call(
        body, name=name, grid=(k // tk, n // tn, nm),
        in_specs=[pl.BlockSpec((tm, tk), lambda kk, j, i: (i, kk)), pl.BlockSpec((tm, tn), lambda kk, j, i: (i, j))],
        out_specs=pl.BlockSpec((tk, tn), lambda kk, j, i: (kk, j)),
        out_shape=jax.ShapeDtypeStruct((k, n), out_dtype),
        scratch_shapes=[pltpu.VMEM((tk, tn), F32)],
        compiler_params=_cparams(("arbitrary", "arbitrary", "arbitrary")),
    )(a, b)


def _ln_stats(r):
    mu = jnp.mean(r, axis=-1, keepdims=True)
    xc = r - mu
    var = jnp.mean(xc * xc, axis=-1, keepdims=True)
    rstd = lax.rsqrt(var + EPS)
    return xc * rstd, rstd


def _ln_bwd(dy, xhat, rstd, g):
    dxhat = dy * g
    m1 = jnp.mean(dxhat, axis=-1, keepdims=True)
    m2 = jnp.mean(dxhat * xhat, axis=-1, keepdims=True)
    dr = rstd * (dxhat - m1 - xhat * m2)
    return dr, jnp.sum(dy * xhat, axis=0, keepdims=True), jnp.sum(dy, axis=0, keepdims=True)


N_SEG = 3 + len(_LEVELS)


def _level_stack():
    t = np.arange(BLOCK)[:, None]
    r = np.arange(BLOCK)[None, :]
    mats = [r <= t, r > t, np.ones((BLOCK, BLOCK), bool)]
    for h in _LEVELS:
        same = (t // (2 * h)) == (r // (2 * h))
        up_t, up_r = (t % (2 * h)) >= h, (r % (2 * h)) >= h
        mats.append(same & ((up_t & up_r & (r <= t)) | (~up_t & ~up_r & (r > t))))
    return jnp.asarray(np.concatenate(mats, axis=0).astype(np.float32), dtype=BF16)


def _hgrn_gates(hf, a0, a1, valid):
    lb = jax.nn.sigmoid(a0 - a1)
    fg = lb + (1.0 - lb) * jax.nn.sigmoid(hf)
    return jnp.where(valid, jnp.log(fg), 0.0), jnp.where(valid, 1.0 - fg, 0.0)


def _hgrn_scores(hq, k, *levels):
    q = jax.nn.silu(hq)
    rows = lax.broadcasted_iota(jnp.int32, (BLOCK, BLOCK), 0)
    cols = lax.broadcasted_iota(jnp.int32, (BLOCK, BLOCK), 1)
    a = jnp.where(rows == cols, jnp.sum(q * k, axis=-1, keepdims=True), 0.0)
    differ = jnp.bitwise_xor(rows, cols)
    for h, lvl in zip(_LEVELS, levels):
        decay = jnp.exp(lvl)
        pair = (cols < rows) & (differ >= h) & (differ < 2 * h)
        a = a + jnp.where(pair, _mm(q * decay, k * decay, "nt"), 0.0)
    return a


def _hgrn_mix(hq, k, v, st_in, a, seg_incl, seg_after, seg_total):
    o = _mm(jax.nn.silu(hq) * jnp.exp(seg_incl), st_in, "nt") + _mm(a, v, "nn")
    return o, st_in * jnp.exp(seg_total) + _mm(v, k * jnp.exp(seg_after), "tn")


def _hgrn_norm(o, hg, ng):
    return o * lax.rsqrt(jnp.mean(o * o, axis=-1, keepdims=True) + EPS) * ng * jax.nn.silu(hg)


def _seg_blocks(e, h):
    return [e[i * BLOCK:(i + 1) * BLOCK, h * BLOCK:(h + 1) * BLOCK] for i in range(N_SEG)]


def _rope(x, cos, sin, first_half):
    partner = jnp.where(first_half, -pltpu.roll(x, 96, 1), pltpu.roll(x, 32, 1))
    return x * cos + partner * sin


def _rope_t(g, cos, sin, first_half):
    u = g * sin
    partner = jnp.where(first_half, pltpu.roll(u, 96, 1), -pltpu.roll(u, 32, 1))
    return g * cos + partner


def _att_core(q0, q1, q2, q3, km, kp, kc, vm, vp, vc, sinkrow0, sinkrow1, own_side, ok_band, ok_meta):
    low = lambda x: lax.broadcasted_iota(jnp.int32, x.shape, 1) < HEAD_DIM
    scale = HEAD_DIM ** -0.5
    neg = jnp.finfo(F32).min
    wide = lambda m: jnp.concatenate([m] * 4, axis=1)
    own4, band4, meta4 = wide(own_side), wide(ok_band), wide(ok_meta)
    outs = []
    for g, (qa, qb, sinkrow) in enumerate(((q0, q1, sinkrow0), (q2, q3, sinkrow1))):
        def both(x, g=g):
            sw = _swap_halves(x)
            return jnp.where(low(x), x, sw) if g == 0 else jnp.where(low(x), sw, x)
        q4 = jnp.concatenate([jnp.where(low(qa), qa, 0.0), jnp.where(low(qa), 0.0, qa),
                              jnp.where(low(qb), qb, 0.0), jnp.where(low(qb), 0.0, qb)], axis=0)
        s = jnp.where(own4, _mm(both(kc), q4, "nt"), _mm(both(kp), q4, "nt"))
        s = jnp.where(band4, s * scale, neg)
        sm = jnp.where(meta4, _mm(both(km), q4, "nt") * scale, neg)
        mx = jnp.maximum(jnp.maximum(jnp.max(s, axis=0, keepdims=True), jnp.max(sm, axis=0, keepdims=True)), sinkrow)
        mx = lax.stop_gradient(mx)
        p, pm = jnp.exp(s - mx), jnp.exp(sm - mx)
        inv = 1.0 / (jnp.sum(p, axis=0, keepdims=True) + jnp.sum(pm, axis=0, keepdims=True) + jnp.exp(sinkrow - mx))
        p = p * inv
        o4 = (_mm(jnp.where(own4, p, 0.0), both(vc), "tn") + _mm(jnp.where(own4, 0.0, p), both(vp), "tn")
              + _mm(pm * inv, both(vm), "tn"))
        for j in range(2):
            upper = o4[(2 * j) * BLOCK:(2 * j + 1) * BLOCK]
            outs.append(jnp.where(low(upper), upper, o4[(2 * j + 1) * BLOCK:(2 * j + 2) * BLOCK]))
    return jnp.concatenate(outs, axis=1)


def _att_masks(blk_idx):
    kidx = lax.broadcasted_iota(jnp.int32, (BLOCK, BLOCK), 0)
    qrow = lax.broadcasted_iota(jnp.int32, (BLOCK, BLOCK), 1)
    own_side = kidx <= qrow
    pos_own = blk_idx * BLOCK + kidx - PAD
    ok_band = (own_side & (pos_own >= N_META)) | (~own_side & (pos_own - BLOCK >= N_META) & (blk_idx >= 1))
    qpos = blk_idx * BLOCK + lax.broadcasted_iota(jnp.int32, (N_META, BLOCK), 1) - PAD
    ok_meta = lax.broadcasted_iota(jnp.int32, (N_META, BLOCK), 0) <= qpos
    return own_side, ok_band, ok_meta


def _token_streams(tr, tile_of=lambda i: i):
    k = tr // BLOCK
    return [pl.BlockSpec((BLOCK, D_MODEL), lambda i, j=j: (jnp.maximum(k * tile_of(i) - 1 + j, 0), 0))
            for j in range(k)]


def _embed_ln(x, meta_shard, w_in_shard, g0, b0):
    p = x.shape[0] + BLOCK
    tr = _row_tile(p, 640)
    k = tr // BLOCK
    nt = p // tr
    tile_of = lambda s: (s + 1) % nt
    shards = [meta_shard, w_in_shard]
    c_in, c_out, c_shapes, c_sems = _comm_specs(shards, N_DEV)

    def body(*refs):
        g_ref, b_ref = refs[k:k + 2]
        h_ref, hb_ref, xh_ref, rs_ref = refs[k + 4:k + 8]
        out_refs = refs[k + 8:k + 10]
        lead_ref, meta_ref = refs[k + 10:k + 12]
        starts, passes, waits = _gather_behind(refs[k + 2:k + 4], out_refs, refs[k + 12:], [False, False])
        s = pl.program_id(0)
        t = tile_of(s)

        @pl.when(s == 0)
        def _():
            lead_ref[...] = jnp.zeros_like(lead_ref)
            for start in starts:
                start()

        @pl.when(s == nt - 1)
        def _():
            for step in passes + waits:
                step()
            pltpu.sync_copy(out_refs[0], meta_ref)
            for d in range(N_DEV):
                lead_ref[PAD:BLOCK, d * BLOCK:(d + 1) * BLOCK] = meta_ref[d]

        first = jnp.where(t == 0, lead_ref[...], refs[0][...])
        xhat, rstd = _ln_stats(jnp.concatenate([first] + [r[...] for r in refs[1:k]], axis=0))
        row = t * tr + lax.broadcasted_iota(jnp.int32, (tr, 1), 0)
        h = jnp.where(row >= PAD, xhat * g_ref[...] + b_ref[...], 0.0)
        h_ref[...] = h
        hb_ref[...] = h.astype(BF16)
        xh_ref[...] = xhat
        rs_ref[...] = rstd

    vec = pl.BlockSpec((1, D_MODEL), lambda s: (0, 0))
    rowsp = pl.BlockSpec((tr, D_MODEL), lambda s: (tile_of(s), 0))
    return pl.pallas_call(
        body, name="embed_ln", grid=(nt,),
        in_specs=_token_streams(tr, tile_of) + [vec, vec] + c_in,
        out_specs=[rowsp, rowsp, rowsp, pl.BlockSpec((tr, 1), lambda s: (tile_of(s), 0))] + c_out,
        out_shape=[jax.ShapeDtypeStruct((p, D_MODEL), F32), jax.ShapeDtypeStruct((p, D_MODEL), BF16),
                   jax.ShapeDtypeStruct((p, D_MODEL), F32), jax.ShapeDtypeStruct((p, 1), F32)] + c_shapes,
        scratch_shapes=[pltpu.VMEM((BLOCK, D_MODEL), F32), pltpu.VMEM((N_DEV, N_META, BLOCK), F32)] + c_sems,
        compiler_params=_cparams(("arbitrary",)),
    )(*([x] * k), g0, b0, *shards)


def _rope_tables(p):
    pos = (np.arange(p, dtype=np.int32) - PAD).astype(np.float32)
    half = HEAD_DIM // 2
    inv = np.float32(ROPE_THETA) ** (-np.arange(half, dtype=np.float32) / np.float32(half))
    ang = pos[:, None] * np.tile(inv.astype(np.float32), BLOCK // half)[None, :]
    return jnp.asarray(np.cos(ang), F32), jnp.asarray(np.sin(ang), F32)


def _att_sinkrows(sink_ref):
    lanehead = lax.broadcasted_iota(jnp.int32, (1, 4 * BLOCK), 1) // BLOCK
    rows = []
    for g in range(2):
        row = jnp.zeros((1, 4 * BLOCK), F32)
        for j in range(4):
            row = jnp.where(lanehead == j, sink_ref[0, 4 * g + j], row)
        rows.append(row)
    return rows


def _first_half(rows):
    return (lax.broadcasted_iota(jnp.int32, (rows, BLOCK), 1) % HEAD_DIM) < (HEAD_DIM // 2)


def _att_load(qkv_ref, cos_ref, sin_ref, with_q):
    cos, sin, fh = cos_ref[...], sin_ref[...], _first_half(BLOCK)
    qs = [_rope(qkv_ref[:, j * BLOCK:(j + 1) * BLOCK], cos, sin, fh) for j in range(4)] if with_q else None
    k = _rope(qkv_ref[:, ATT_QW:ATT_QW + ATT_KVW], cos, sin, fh)
    v = qkv_ref[:, ATT_QW + ATT_KVW:ATT_QW + 2 * ATT_KVW]
    return qs, k, v


def _att_load_meta(qkv_ref, cos_ref, sin_ref):
    k = _rope(qkv_ref[PAD:BLOCK, ATT_QW:ATT_QW + ATT_KVW], cos_ref[PAD:BLOCK, :], sin_ref[PAD:BLOCK, :],
              _first_half(N_META))
    return k, qkv_ref[PAD:BLOCK, ATT_QW + ATT_KVW:ATT_QW + 2 * ATT_KVW]


def _att_specs(blk):
    w = ATT_QW + 2 * ATT_KVW
    cur = lambda width: pl.BlockSpec((BLOCK, width), lambda i: (blk(i), 0))
    prev = lambda width: pl.BlockSpec((BLOCK, width), lambda i: (jnp.maximum(blk(i) - 1, 0), 0))
    meta = lambda width: pl.BlockSpec((BLOCK, width), lambda i: (0, 0))
    return [cur(w), prev(w), meta(w), cur(BLOCK), cur(BLOCK), prev(BLOCK), prev(BLOCK), meta(BLOCK), meta(BLOCK),
            pl.BlockSpec(memory_space=pltpu.SMEM)]


_FLIPS = [(dx, dy, dc) for dx in (0, 1) for dy in (0, 1) for dc in (0, 1)][1:]
N_PEERS = len(_FLIPS)


def _place():
    return lax.axis_index("x"), lax.axis_index("y"), lax.axis_index("c")


def _peer(place, flip):
    return tuple(1 - p if f else p for p, f in zip(place, flip))


def _slot(place, swapped):
    x, y, c = place
    return 4 * y + 2 * x + c if swapped else 4 * x + 2 * y + c


def _comm_specs(arrs, out_lead):
    n = len(arrs)
    outs = [jax.ShapeDtypeStruct((out_lead,) + a.shape[-2:], a.dtype) for a in arrs]
    sems = [pltpu.SemaphoreType.DMA((n, N_PEERS)), pltpu.SemaphoreType.DMA((n, N_PEERS)), pltpu.SemaphoreType.DMA((n,))]
    return [pl.BlockSpec(memory_space=pl.ANY)] * n, [pl.BlockSpec(memory_space=pl.ANY)] * n, outs, sems


def _gather_behind(shard_refs, out_refs, sems, swapped):
    send_sems, recv_sems, local_sems = sems
    x, y, c = _place()
    me, sibling = (x, y, c), (x, y, 1 - c)
    chips = [(1 - x, y), (x, 1 - y), (1 - x, 1 - y)]
    starts, passes, waits = [], [], []
    for w, (s, o) in enumerate(zip(shard_refs, out_refs)):
        def copy(k, block, to, from_shard=False, w=w, s=s, o=o):
            rows = o.at[_slot(block, swapped[w])]
            return pltpu.make_async_remote_copy(
                src_ref=s if from_shard else rows, dst_ref=rows, send_sem=send_sems.at[w, k],
                recv_sem=recv_sems.at[w, k], device_id=to, device_id_type=MESH)

        own = pltpu.make_async_copy(s, o.at[_slot(me, swapped[w])], local_sems.at[w])
        first = [copy(0, me, sibling, True)] + [copy(1 + j, me, (*chip, c), True) for j, chip in enumerate(chips)]
        handed = [copy(4 + j, (*chip, c), sibling) for j, chip in enumerate(chips)]
        starts += [own.start] + [cp.start for cp in first]
        for j, chip in enumerate(chips):
            passes += [copy(1 + j, (*chip, c), me).wait_recv, handed[j].start]
        waits.append(copy(0, sibling, me).wait_recv)
        waits += [copy(4 + j, (*chip, 1 - c), me).wait_recv for j, chip in enumerate(chips)]
        waits += [cp.wait_send for cp in first + handed] + [own.wait]
    return starts, passes, waits


def _scatter_behind(part_refs, recv_refs, sems, swapped):
    send_sems, recv_sems, _ = sems
    place = _place()
    starts, waits = [], []
    for w, (p, o) in enumerate(zip(part_refs, recv_refs)):
        for r, flip in enumerate(_FLIPS):
            peer = _peer(place, flip)
            cp = pltpu.make_async_remote_copy(
                src_ref=p.at[_slot(peer, swapped[w])], dst_ref=o.at[r], send_sem=send_sems.at[w, r],
                recv_sem=recv_sems.at[w, r], device_id=peer, device_id_type=MESH)
            starts.append(cp.start)
            waits += [cp.wait_recv, cp.wait_send]
    return starts, waits


def _mixers_fwd(proj_hg, proj_att, lbounds, norm_g, lv, cos, sin, sinks, shards, swapped):
    p = proj_hg.shape[0]
    nb = p // BLOCK
    n = len(shards)
    c_in, c_out, c_shapes, c_sems = _comm_specs(shards, N_DEV)
    pass_step = min(nb - 1, max(1, (5 * nb) // 8))

    def body(*refs):
        x_ref, lb_ref, ng_ref, lv_ref, cur_ref, prev_ref, meta_ref, cc, sc, cp, sp, cm, sm, sink_ref = refs[:14]
        shard_refs = refs[14:14 + n]
        y_ref, o_ref, st_ref, a_ref, raw_ref = refs[14 + n:19 + n]
        out_refs = refs[19 + n:19 + 2 * n]
        carry_ref = refs[19 + 2 * n]
        starts, passes, waits = _gather_behind(shard_refs, out_refs, refs[20 + 2 * n:], swapped)
        c = pl.program_id(0)

        @pl.when(c == 0)
        def _():
            carry_ref[...] = jnp.zeros_like(carry_ref)
            for start in starts:
                start()

        @pl.when(c == pass_step)
        def _():
            for step in passes:
                step()

        valid = (c * BLOCK + lax.broadcasted_iota(jnp.int32, (BLOCK, 1), 0)) >= PAD
        logf, k = _hgrn_gates(x_ref[:, HG_W:2 * HG_W], lb_ref[0:1, :], lb_ref[1:2, :], valid)
        e = _split_dot(lv_ref[...], logf, "nn")
        for h in range(HG_HEADS):
            sl = lambda part: x_ref[:, part * HG_W + h * BLOCK: part * HG_W + (h + 1) * BLOCK]
            hs = slice(h * BLOCK, (h + 1) * BLOCK)
            st_in = carry_ref[h]
            st_ref[0, h] = st_in
            seg = _seg_blocks(e, h)
            a = _hgrn_scores(sl(0), k[:, hs], *seg[3:])
            a_ref[0, h] = a.astype(BF16)
            raw, st_out = _hgrn_mix(sl(0), k[:, hs], sl(2), st_in, a, *seg[:3])
            raw_ref[:, hs] = raw
            y_ref[:, hs] = _hgrn_norm(raw, sl(3), ng_ref[...]).astype(BF16)
            carry_ref[h] = st_out

        qs, kc, vc = _att_load(cur_ref, cc, sc, True)
        _, kp, vp = _att_load(prev_ref, cp, sp, False)
        km, vm = _att_load_meta(meta_ref, cm, sm)
        s0, s1 = _att_sinkrows(sink_ref)
        o_ref[...] = _att_core(*qs, km, kp, kc, vm, vp, vc, s0, s1, *_att_masks(c)).astype(BF16)

        @pl.when(c == nb - 1)
        def _():
            for wait in waits:
                wait()

    return pl.pallas_call(
        body, name="mixers_fwd", grid=(nb,),
        in_specs=[pl.BlockSpec((BLOCK, 4 * HG_W), lambda c: (c, 0)), pl.BlockSpec((2, HG_W), lambda c: (0, 0)),
                  pl.BlockSpec((1, BLOCK), lambda c: (0, 0)), pl.BlockSpec(lv.shape, lambda c: (0, 0))]
        + _att_specs(lambda c: c) + c_in,
        out_specs=[pl.BlockSpec((BLOCK, HG_W), lambda c: (c, 0)), pl.BlockSpec((BLOCK, ATT_QW), lambda c: (c, 0)),
                   pl.BlockSpec((1, HG_HEADS, BLOCK, BLOCK), lambda c: (c, 0, 0, 0)),
                   pl.BlockSpec((1, HG_HEADS, BLOCK, BLOCK), lambda c: (c, 0, 0, 0)),
                   pl.BlockSpec((BLOCK, HG_W), lambda c: (c, 0))] + c_out,
        out_shape=[jax.ShapeDtypeStruct((p, HG_W), BF16), jax.ShapeDtypeStruct((p, ATT_QW), BF16),
                   jax.ShapeDtypeStruct((nb, HG_HEADS, BLOCK, BLOCK), F32),
                   jax.ShapeDtypeStruct((nb, HG_HEADS, BLOCK, BLOCK), BF16),
                   jax.ShapeDtypeStruct((p, HG_W), F32)] + c_shapes,
        scratch_shapes=[pltpu.VMEM((HG_HEADS, BLOCK, BLOCK), F32)] + c_sems,
        compiler_params=_cparams(("arbitrary",)),
    )(proj_hg, lbounds, norm_g, lv, proj_att, proj_att, proj_att, cos, sin, cos, sin, cos, sin, sinks, *shards)


def _tile(rows, preferred):
    return preferred if rows % preferred == 0 else _row_tile(rows, preferred)


def _branch_mix(yh, oa, gates, w_bh, w_ba):
    y_hg = _dot(yh, w_bh, "nn")
    y_att = _dot(oa, w_ba, "nn")
    s1 = jax.nn.sigmoid(gates[:, :D_MODEL])
    s2 = jax.nn.sigmoid(gates[:, D_MODEL:])
    return s1 * y_hg + s2 * y_att, y_hg, y_att, s1, s2


def _mix_out_ln1(yh, oa, gates, h0, w_bh, w_ba, w_out, g1, b1):
    p = yh.shape[0]
    tr = _tile(p, 320)

    def body(yh_ref, oa_ref, g_ref, h0_ref, wbh_ref, wba_ref, wo_ref, g1_ref, b1_ref,
             mix_ref, h1_ref, h1b_ref, xh_ref, rs_ref):
        mixin = _branch_mix(yh_ref[...], oa_ref[...], g_ref[...], wbh_ref[...], wba_ref[...])[0]
        mix_ref[...] = mixin.astype(BF16)
        xhat, rstd = _ln_stats(ALPHA * h0_ref[...] + _dot(mixin, wo_ref[...], "nn"))
        h1 = xhat * g1_ref[...] + b1_ref[...]
        h1_ref[...] = h1
        h1b_ref[...] = h1.astype(BF16)
        xh_ref[...] = xhat
        rs_ref[...] = rstd

    row = lambda w: pl.BlockSpec((tr, w), lambda i: (i, 0))
    const = lambda a: pl.BlockSpec(a.shape, lambda i: (0, 0))
    return pl.pallas_call(
        body, name="mix_out_ln1", grid=(p // tr,),
        in_specs=[row(HG_W), row(ATT_QW), row(2 * D_MODEL), row(D_MODEL), const(w_bh), const(w_ba), const(w_out),
                  const(g1), const(b1)],
        out_specs=[row(D_MODEL), row(D_MODEL), row(D_MODEL), row(D_MODEL), row(1)],
        out_shape=[jax.ShapeDtypeStruct((p, D_MODEL), BF16), jax.ShapeDtypeStruct((p, D_MODEL), F32),
                   jax.ShapeDtypeStruct((p, D_MODEL), BF16), jax.ShapeDtypeStruct((p, D_MODEL), F32),
                   jax.ShapeDtypeStruct((p, 1), F32)],
        compiler_params=_cparams(("arbitrary",)),
    )(yh, oa, gates, h0, w_bh, w_ba, w_out, g1, b1)


FF_T = D_FF // 2


def _ffn_in_swiglu(h1, w_fi):
    p = h1.shape[0]
    tm = _row_tile(p, 640)

    def body(h_ref, w_ref, au_ref, s_ref):
        au = _dot(h_ref[...], w_ref[...], "nn")
        au_ref[...] = au.astype(BF16)
        s_ref[...] = (jax.nn.silu(au[:, :FF_T]) * au[:, FF_T:]).astype(BF16)

    return pl.pallas_call(
        body, name="ffn_in_swiglu", grid=(D_FF // FF_T, p // tm),
        in_specs=[pl.BlockSpec((tm, D_MODEL), lambda j, i: (i, 0)), pl.BlockSpec((D_MODEL, 2 * FF_T), lambda j, i: (0, j))],
        out_specs=[pl.BlockSpec((tm, 2 * FF_T), lambda j, i: (i, j)), pl.BlockSpec((tm, FF_T), lambda j, i: (i, j))],
        out_shape=[jax.ShapeDtypeStruct((p, 2 * D_FF), BF16), jax.ShapeDtypeStruct((p, D_FF), BF16)],
        compiler_params=_cparams(("arbitrary", "arbitrary")),
    )(h1, w_fi)


def _ffn_out_loss(s, w_fo, h1, g2, b2, target):
    p = h1.shape[0]
    tr = _row_tile(p, 640)
    k = tr // BLOCK

    def body(*refs):
        s_ref, w_ref, h_ref, g_ref, b_ref = refs[:5]
        dr_ref, loss_ref, dg_ref, db_ref = refs[5 + k:]
        i = pl.program_id(0)
        xhat, rstd = _ln_stats(ALPHA * h_ref[...] + _dot(s_ref[...], w_ref[...], "nn"))
        y = xhat * g_ref[...] + b_ref[...]
        row = i * tr + lax.broadcasted_iota(jnp.int32, (tr, 1), 0)
        tgt = jnp.concatenate([r[...] for r in refs[5:5 + k]], axis=0)
        err = jnp.where(row >= BLOCK, y - tgt, 0.0)
        dr, dg, db = _ln_bwd(err * (1.0 / D_MODEL), xhat, rstd, g_ref[...])
        dr_ref[...] = dr
        e2 = jnp.sum(err * err, axis=0, keepdims=True)
        part = e2[:, 0:BLOCK]
        for j in range(1, D_MODEL // BLOCK):
            part = part + e2[:, j * BLOCK:(j + 1) * BLOCK]
        part = part * (0.5 / D_MODEL)

        @pl.when(i == 0)
        def _():
            loss_ref[...] = part
            dg_ref[...] = dg
            db_ref[...] = db

        @pl.when(i > 0)
        def _():
            loss_ref[...] += part
            dg_ref[...] += dg
            db_ref[...] += db

    vec = pl.BlockSpec((1, D_MODEL), lambda i: (0, 0))
    rowsp = pl.BlockSpec((tr, D_MODEL), lambda i: (i, 0))
    return pl.pallas_call(
        body, name="ffn_out_loss", grid=(p // tr,),
        in_specs=[pl.BlockSpec((tr, D_FF), lambda i: (i, 0)), pl.BlockSpec((D_FF, D_MODEL), lambda i: (0, 0)),
                  rowsp, vec, vec] + _token_streams(tr),
        out_specs=[rowsp, pl.BlockSpec((1, BLOCK), lambda i: (0, 0)), vec, vec],
        out_shape=[jax.ShapeDtypeStruct((p, D_MODEL), F32), jax.ShapeDtypeStruct((1, BLOCK), F32),
                   jax.ShapeDtypeStruct((1, D_MODEL), F32), jax.ShapeDtypeStruct((1, D_MODEL), F32)],
        compiler_params=_cparams(("arbitrary",)),
    )(s, w_fo, h1, g2, b2, *([target] * k))


def _d_ffn_hidden(dr2, w_fo, au):
    p = au.shape[0]
    tm = _row_tile(p, 640)

    def body(d_ref, w_ref, au_ref, o_ref):
        ds = _dot(d_ref[...], w_ref[...], "nt")
        _, vjp = jax.vjp(lambda a, u: jax.nn.silu(a) * u, au_ref[:, :FF_T].astype(F32), au_ref[:, FF_T:].astype(F32))
        da, du = vjp(ds)
        o_ref[:, :FF_T] = da.astype(BF16)
        o_ref[:, FF_T:] = du.astype(BF16)

    return pl.pallas_call(
        body, name="d_ffn_hidden", grid=(D_FF // FF_T, p // tm),
        in_specs=[pl.BlockSpec((tm, D_MODEL), lambda j, i: (i, 0)), pl.BlockSpec((FF_T, D_MODEL), lambda j, i: (j, 0)),
                  pl.BlockSpec((tm, 2 * FF_T), lambda j, i: (i, j))],
        out_specs=pl.BlockSpec((tm, 2 * FF_T), lambda j, i: (i, j)),
        out_shape=jax.ShapeDtypeStruct((p, 2 * D_FF), BF16), compiler_params=_cparams(("arbitrary", "arbitrary")),
    )(dr2, w_fo, au)


def _ln1_mix_bwd(dr2, dh1_ffn, xhat1, rstd1, g1, yh, oa, gates, w_bh, w_ba, w_out):
    p = yh.shape[0]
    tr = _tile(p, 320)

    def body(a_ref, b_ref, xh_ref, rs_ref, g1_ref, yh_ref, oa_ref, g_ref, wbh_ref, wba_ref, wo_ref,
             dr_ref, dyhg_ref, dyat_ref, dgt_ref, dyh_ref, doa_ref, dg_ref, db_ref):
        i = pl.program_id(0)
        dr, dg, db = _ln_bwd(ALPHA * a_ref[...] + b_ref[...], xh_ref[...], rs_ref[...], g1_ref[...])
        dr_ref[...] = dr
        d = _dot(dr, wo_ref[...], "nt")
        _, y_hg, y_att, s1, s2 = _branch_mix(yh_ref[...], oa_ref[...], g_ref[...], wbh_ref[...], wba_ref[...])
        dy_hg = d * s1
        dy_att = d * s2
        dyhg_ref[...] = dy_hg.astype(BF16)
        dyat_ref[...] = dy_att.astype(BF16)
        dgt_ref[:, :D_MODEL] = (d * y_hg * s1 * (1.0 - s1)).astype(BF16)
        dgt_ref[:, D_MODEL:] = (d * y_att * s2 * (1.0 - s2)).astype(BF16)
        dyh_ref[...] = _dot(dy_hg, wbh_ref[...], "nt")
        doa_ref[...] = _dot(dy_att, wba_ref[...], "nt")

        @pl.when(i == 0)
        def _():
            dg_ref[...] = dg
            db_ref[...] = db

        @pl.when(i > 0)
        def _():
            dg_ref[...] += dg
            db_ref[...] += db

    row = lambda w: pl.BlockSpec((tr, w), lambda i: (i, 0))
    const = lambda a: pl.BlockSpec(a.shape, lambda i: (0, 0))
    vec = pl.BlockSpec((1, D_MODEL), lambda i: (0, 0))
    return pl.pallas_call(
        body, name="ln1_mix_bwd", grid=(p // tr,),
        in_specs=[row(D_MODEL), row(D_MODEL), row(D_MODEL), row(1), vec, row(HG_W), row(ATT_QW), row(2 * D_MODEL),
                  const(w_bh), const(w_ba), const(w_out)],
        out_specs=[row(D_MODEL), row(D_MODEL), row(D_MODEL), row(2 * D_MODEL), row(HG_W), row(ATT_QW), vec, vec],
        out_shape=[jax.ShapeDtypeStruct((p, D_MODEL), F32), jax.ShapeDtypeStruct((p, D_MODEL), BF16),
                   jax.ShapeDtypeStruct((p, D_MODEL), BF16), jax.ShapeDtypeStruct((p, 2 * D_MODEL), BF16),
                   jax.ShapeDtypeStruct((p, HG_W), F32), jax.ShapeDtypeStruct((p, ATT_QW), F32),
                   jax.ShapeDtypeStruct((1, D_MODEL), F32), jax.ShapeDtypeStruct((1, D_MODEL), F32)],
        compiler_params=_cparams(("arbitrary",)),
    )(dr2, dh1_ffn, xhat1, rstd1, g1, yh, oa, gates, w_bh, w_ba, w_out)


MIX_W = 4 * HG_W + ATT_QW + 2 * ATT_KVW


def _mixers_bwd(proj_hg, proj_att, lbounds, norm_g, lv, states, scores, raw, cos, sin, sinks, dyh, doa, parts,
                swapped):
    p = proj_hg.shape[0]
    nb = p // BLOCK
    n = len(parts)
    kvw = 2 * ATT_KVW
    rev = lambda s: nb - 1 - s
    c_in, c_out, c_shapes, c_sems = _comm_specs(parts, N_PEERS)

    def body(*refs):
        (x_ref, lb_ref, ng_ref, lv_ref, st_ref, a_ref, raw_ref, cur_ref, prev_ref, meta_ref, cc, sc, cp, sp, cm, sm,
         sink_ref, dy_ref, do_ref) = refs[:19]
        part_refs = refs[19:19 + n]
        dx_ref, dlb_ref, dng_ref, dsink_ref = refs[19 + n:23 + n]
        recv_refs = refs[23 + n:23 + 2 * n]
        dcarry_ref, dkv_next_ref, dkv_meta_ref = refs[23 + 2 * n:26 + 2 * n]
        starts, waits = _scatter_behind(part_refs, recv_refs, refs[26 + 2 * n:], swapped)
        step = pl.program_id(0)
        c = rev(step)

        @pl.when(step == 0)
        def _():
            dcarry_ref[...] = jnp.zeros_like(dcarry_ref)
            dkv_next_ref[...] = jnp.zeros_like(dkv_next_ref)
            dkv_meta_ref[...] = jnp.zeros_like(dkv_meta_ref)
            dlb_ref[...] = jnp.zeros_like(dlb_ref)
            dng_ref[...] = jnp.zeros_like(dng_ref)
            dsink_ref[...] = jnp.zeros_like(dsink_ref)
            for start in starts:
                start()

        fh = _first_half(BLOCK)
        qs, kc, vc = _att_load(cur_ref, cc, sc, True)
        _, kp, vp = _att_load(prev_ref, cp, sp, False)
        km, vm = _att_load_meta(meta_ref, cm, sm)
        s0, s1 = _att_sinkrows(sink_ref)
        masks = _att_masks(c)
        _, att_vjp = jax.vjp(lambda *a: _att_core(*a, *masks), *qs, km, kp, kc, vm, vp, vc, s0, s1)
        dq0, dq1, dq2, dq3, dkm, dkp, dkc, dvm, dvp, dvc, ds0, ds1 = att_vjp(do_ref[...])
        att0 = 4 * HG_W
        for j, dq in enumerate((dq0, dq1, dq2, dq3)):
            dx_ref[:, att0 + j * BLOCK:att0 + (j + 1) * BLOCK] = _rope_t(dq, cc[...], sc[...], fh).astype(BF16)
        dkv_meta_ref[:, :BLOCK] += _rope_t(dkm, cm[PAD:BLOCK, :], sm[PAD:BLOCK, :], _first_half(N_META))
        dkv_meta_ref[:, BLOCK:] += dvm
        last = jnp.where(c == 0, 1.0, 0.0)
        to_meta_rows = lambda m: jnp.concatenate([jnp.zeros((PAD, BLOCK), F32), last * m], axis=0)
        dk = _rope_t(dkc, cc[...], sc[...], fh) + dkv_next_ref[:, :BLOCK] + to_meta_rows(dkv_meta_ref[:, :BLOCK])
        dv = dvc + dkv_next_ref[:, BLOCK:] + to_meta_rows(dkv_meta_ref[:, BLOCK:])
        dx_ref[:, att0 + ATT_QW:att0 + ATT_QW + ATT_KVW] = dk.astype(BF16)
        dx_ref[:, att0 + ATT_QW + ATT_KVW:] = dv.astype(BF16)
        dkv_next_ref[:, :BLOCK] = _rope_t(dkp, cp[...], sp[...], fh)
        dkv_next_ref[:, BLOCK:] = dvp
        sink_rows = []
        for dsg in (ds0, ds1):
            for j in range(4):
                tot = jnp.sum(dsg[:, j * BLOCK:(j + 1) * BLOCK], axis=1, keepdims=True)
                sink_rows.append(jnp.broadcast_to(tot, (1, BLOCK)))
        dsink_ref[...] += jnp.concatenate(sink_rows, axis=0)

        valid = (c * BLOCK + lax.broadcasted_iota(jnp.int32, (BLOCK, 1), 0)) >= PAD
        (logf, k), gates_vjp = jax.vjp(lambda hf, a0, a1: _hgrn_gates(hf, a0, a1, valid),
                                       x_ref[:, HG_W:2 * HG_W], lb_ref[0:1, :], lb_ref[1:2, :])
        lvv = lv_ref[...]
        e = _split_dot(lvv, logf, "nn")
        dng = jnp.zeros((1, BLOCK), F32)
        dk, dseg = [], []
        for h in range(HG_HEADS):
            sl = lambda part: x_ref[:, part * HG_W + h * BLOCK: part * HG_W + (h + 1) * BLOCK]
            hs = slice(h * BLOCK, (h + 1) * BLOCK)
            seg = _seg_blocks(e, h)
            _, norm_vjp = jax.vjp(_hgrn_norm, raw_ref[:, hs], sl(3), ng_ref[...])
            draw, dhg, dngh = norm_vjp(dy_ref[:, hs])
            _, mix_vjp = jax.vjp(_hgrn_mix, sl(0), k[:, hs], sl(2), st_ref[0, h], a_ref[0, h].astype(F32), *seg[:3])
            dhq, dkh, dhi, dst, da, *dseg_mix = mix_vjp((draw, dcarry_ref[h]))
            _, scores_vjp = jax.vjp(_hgrn_scores, sl(0), k[:, hs], *seg[3:])
            dhq2, dkh2, *dseg_lvl = scores_vjp(da)
            for part, val in ((0, dhq + dhq2), (2, dhi), (3, dhg)):
                dx_ref[:, part * HG_W + h * BLOCK: part * HG_W + (h + 1) * BLOCK] = val.astype(BF16)
            dk.append(dkh + dkh2)
            dseg.append(jnp.concatenate(dseg_mix + dseg_lvl, axis=0))
            dng = dng + dngh
            dcarry_ref[h] = dst
        dlogf = _split_dot(lvv, jnp.concatenate(dseg, axis=1), "tn")
        dhf, da0, da1 = gates_vjp((dlogf, jnp.concatenate(dk, axis=1)))
        dx_ref[:, HG_W:2 * HG_W] = dhf.astype(BF16)
        dlb_ref[0:1, :] += da0
        dlb_ref[1:2, :] += da1
        dng_ref[...] += dng

        @pl.when(step == nb - 1)
        def _():
            for wait in waits:
                wait()

    const = lambda shape: pl.BlockSpec(shape, lambda s: (0,) * len(shape))
    per_head = pl.BlockSpec((1, HG_HEADS, BLOCK, BLOCK), lambda s: (rev(s), 0, 0, 0))
    return pl.pallas_call(
        body, name="mixers_bwd", grid=(nb,),
        in_specs=[pl.BlockSpec((BLOCK, 4 * HG_W), lambda s: (rev(s), 0)), const((2, HG_W)), const((1, BLOCK)),
                  const(lv.shape), per_head, per_head, pl.BlockSpec((BLOCK, HG_W), lambda s: (rev(s), 0))]
        + _att_specs(rev)
        + [pl.BlockSpec((BLOCK, HG_W), lambda s: (rev(s), 0)), pl.BlockSpec((BLOCK, ATT_QW), lambda s: (rev(s), 0))]
        + c_in,
        out_specs=[pl.BlockSpec((BLOCK, MIX_W), lambda s: (rev(s), 0)), const((2, HG_W)), const((1, BLOCK)),
                   const((ATT_HEADS, BLOCK))] + c_out,
        out_shape=[jax.ShapeDtypeStruct((p, MIX_W), BF16), jax.ShapeDtypeStruct((2, HG_W), F32),
                   jax.ShapeDtypeStruct((1, BLOCK), F32), jax.ShapeDtypeStruct((ATT_HEADS, BLOCK), F32)] + c_shapes,
        scratch_shapes=[pltpu.VMEM((HG_HEADS, BLOCK, BLOCK), F32), pltpu.VMEM((BLOCK, kvw), F32),
                        pltpu.VMEM((N_META, kvw), F32)] + c_sems,
        compiler_params=_cparams(("arbitrary",)),
    )(proj_hg, lbounds, norm_g, lv, states, scores, raw, proj_att, proj_att, proj_att, cos, sin, cos, sin, cos, sin,
      sinks, dyh, doa, *parts)


def _embed_bwd(dmix, dgates, w_mix, w_gates, dr1, xhat0, rstd0, g0, parts, swapped):
    p = dmix.shape[0]
    tm = _row_tile(p, 640)
    nm = p // tm
    n = len(parts)
    c_in, c_out, c_shapes, c_sems = _comm_specs(parts, N_PEERS)

    def body(*refs):
        a_ref, g_ref, wa_ref, wg_ref, dr_ref, xh_ref, rs_ref, g0_ref = refs[:8]
        o_ref, dg_ref, db_ref = refs[8 + n:11 + n]
        starts, waits = _scatter_behind(refs[8:8 + n], refs[11 + n:11 + 2 * n], refs[11 + 2 * n:], swapped)
        i = pl.program_id(0)

        @pl.when(i == 0)
        def _():
            for start in starts:
                start()

        dh0 = ALPHA * dr_ref[...] + _dot(a_ref[...], wa_ref[...], "nt") + _dot(g_ref[...], wg_ref[...], "nt")
        row = i * tm + lax.broadcasted_iota(jnp.int32, (tm, 1), 0)
        dx, dg, db = _ln_bwd(jnp.where(row >= PAD, dh0, 0.0), xh_ref[...], rs_ref[...], g0_ref[...])
        o_ref[...] = dx

        @pl.when(i == 0)
        def _():
            dg_ref[...] = dg
            db_ref[...] = db

        @pl.when(i > 0)
        def _():
            dg_ref[...] += dg
            db_ref[...] += db

        @pl.when(i == nm - 1)
        def _():
            for wait in waits:
                wait()

    row = lambda w: pl.BlockSpec((tm, w), lambda i: (i, 0))
    const = lambda a: pl.BlockSpec(a.shape, lambda i: (0, 0))
    vec = pl.BlockSpec((1, D_MODEL), lambda i: (0, 0))
    return pl.pallas_call(
        body, name="embed_bwd", grid=(nm,),
        in_specs=[row(dmix.shape[1]), row(dgates.shape[1]), const(w_mix), const(w_gates), row(D_MODEL), row(D_MODEL),
                  row(1), vec] + c_in,
        out_specs=[row(D_MODEL), vec, vec] + c_out,
        out_shape=[jax.ShapeDtypeStruct((p, D_MODEL), F32), jax.ShapeDtypeStruct((1, D_MODEL), F32),
                   jax.ShapeDtypeStruct((1, D_MODEL), F32)] + c_shapes,
        scratch_shapes=c_sems, compiler_params=_cparams(("arbitrary",)),
    )(dmix, dgates, w_mix, w_gates, dr1, xhat0, rstd0, g0, *parts)


_LATE = ("w_branch_hg", "w_branch_attn", "w_out", "w_ffn_in", "w_ffn_out")
_COLUMN_SHARDED = ("meta_tokens", "w_in", "w_branch_hg", "w_branch_attn", "w_ffn_in")
_SWAPPED = ("w_ffn_in",)


def _whole(name, gathered):
    _, r, c = gathered.shape
    if name in _COLUMN_SHARDED:
        return jnp.transpose(gathered, (1, 0, 2)).reshape(r, N_DEV * c)
    return gathered.reshape(N_DEV * r, c)


def _slots(name, whole):
    r, c = whole.shape
    if name in _COLUMN_SHARDED:
        return jnp.transpose(whole.reshape(r, N_DEV, c // N_DEV), (1, 0, 2))
    return whole.reshape(N_DEV, r // N_DEV, c)


def _device_step(x, target, meta_shard, ln_emb_g, ln_emb_b, w_in_shard, lbounds, norm_g, sinks, late_shards,
                 ln1_g, ln1_b, ln2_g, ln2_b):
    s = x.shape[0]
    p = s + BLOCK
    tm = _row_tile(p, 640)
    lv = _level_stack()
    cos, sin = _rope_tables(p)
    hg_end = 4 * HG_W
    mm = functools.partial(_tiled_matmul, tm=tm)
    swapped = [n in _SWAPPED for n in _LATE]

    h0, h0b, xhat0, rstd0, _, g_win = _embed_ln(x, meta_shard, w_in_shard, ln_emb_g, ln_emb_b)
    w_in = _whole("w_in", g_win)
    proj_hg = mm(h0b, w_in[:, :hg_end], "nn", tn=hg_end, tc=D_MODEL, out_dtype=F32, name="proj_hg")
    proj_att = mm(h0b, w_in[:, hg_end:MIX_W], "nn", tn=MIX_W - hg_end, tc=D_MODEL, out_dtype=F32, name="proj_att")
    gates = mm(h0b, w_in[:, MIX_W:], "nn", tn=2 * D_MODEL, tc=D_MODEL, out_dtype=F32, name="proj_gates")
    yh, oa, states, scores, raw, *gathered = _mixers_fwd(
        proj_hg, proj_att, lbounds, norm_g, lv, cos, sin, sinks, late_shards, swapped)
    w_bh, w_ba, w_out, w_fi, w_fo = [_whole(n, g) for n, g in zip(_LATE, gathered)]
    mixin, h1, h1b, xhat1, rstd1 = _mix_out_ln1(yh, oa, gates, h0, w_bh, w_ba, w_out, ln1_g, ln1_b)
    au, sw = _ffn_in_swiglu(h1b, w_fi)
    dr2, loss_part, dg2, db2 = _ffn_out_loss(sw, w_fo, h1, ln2_g, ln2_b, target)

    mtn = functools.partial(_tiled_matmul_tn, tm=_row_tile(p, 1664), out_dtype=BF16)
    d_wfo = mtn(sw, dr2, tk=FF_T, tn=D_MODEL, name="grad_w_ffn_out")
    dau = _d_ffn_hidden(dr2, w_fo, au)
    d_wfi = mtn(h1b, dau, tk=D_MODEL, tn=FF_T, name="grad_w_ffn_in")
    dh1_ffn = mm(dau, w_fi, "nt", tn=D_MODEL, tc=D_FF, out_dtype=F32, name="d_h1_ffn")
    dr1, dy_hg, dy_att, dgates, dyh, doa, dg1, db1 = _ln1_mix_bwd(
        dr2, dh1_ffn, xhat1, rstd1, ln1_g, yh, oa, gates, w_bh, w_ba, w_out)
    d_wout = mtn(mixin, dr1, tk=D_MODEL, tn=D_MODEL, name="grad_w_out")
    d_wbh = mtn(yh, dy_hg, tk=HG_W, tn=D_MODEL, name="grad_w_branch_hg")
    d_wba = mtn(oa, dy_att, tk=ATT_QW, tn=D_MODEL, name="grad_w_branch_attn")
    late_parts = [_slots(n, g) for n, g in zip(_LATE, (d_wbh, d_wba, d_wout, d_wfi, d_wfo))]
    dmix, d_lb, d_ng, d_sink, *late_recv = _mixers_bwd(
        proj_hg, proj_att, lbounds, norm_g, lv, states, scores, raw, cos, sin, sinks, dyh, doa, late_parts, swapped)
    d_win = jnp.concatenate([mtn(h0b, dmix, tk=D_MODEL, tn=MIX_W // 2, name="grad_w_in_mixers"),
                             mtn(h0b, dgates, tk=D_MODEL, tn=D_MODEL, name="grad_w_in_gates")], axis=1)
    win_parts = _slots("w_in", d_win)
    dxin, dg0, db0, win_recv = _embed_bwd(dmix, dgates, w_in[:, :MIX_W], w_in[:, MIX_W:], dr1, xhat0, rstd0, ln_emb_g,
                                          [win_parts], [False])

    small = dict(ln_emb_g=dg0, ln_emb_b=db0, hg_lower_bounds=d_lb, hg_norm_g=d_ng, attn_sinks=d_sink[:, 0],
                 ln1_g=dg1, ln1_b=db1, ln2_g=dg2, ln2_b=db2)
    big = dict(zip(_LATE, zip(late_parts, late_recv)))
    big["w_in"] = (win_parts, win_recv)
    return loss_part, dxin[BLOCK:], small, dxin[PAD:BLOCK], big


def _all_gather(arrs, dtypes, name):
    n = len(arrs)

    def body(*refs):
        ins, outs, stages = refs[:n], refs[n:2 * n], refs[2 * n:3 * n]
        send_sems, recv_sems, local_sems = refs[3 * n:]
        x, y, c = _place()
        sibling = (x, y, 1 - c)
        chips = [(1 - x, y), (x, 1 - y), (1 - x, 1 - y)]
        slot = lambda px, py, pc: 4 * px + 2 * py + pc

        def copy(w, k, block, to, from_stage=False):
            return pltpu.make_async_remote_copy(
                src_ref=stages[w] if from_stage else outs[w].at[slot(*block)], dst_ref=outs[w].at[slot(*block)],
                send_sem=send_sems.at[w, k], recv_sem=recv_sems.at[w, k], device_id=to, device_id_type=MESH)

        mine, first, passed = [], [], []
        for w in range(n):
            stages[w][...] = ins[w][...].astype(dtypes[w])
            mine.append(pltpu.make_async_copy(stages[w], outs[w].at[slot(x, y, c)], local_sems.at[w]))
            mine[-1].start()
        for w in range(n):
            first.append(copy(w, 0, (x, y, c), sibling, from_stage=True))
            first += [copy(w, 1 + j, (x, y, c), (*chip, c), from_stage=True) for j, chip in enumerate(chips)]
        for cp in first:
            cp.start()
        for j, chip in enumerate(chips):
            for w in range(n):
                copy(w, 1 + j, (*chip, c), (x, y, c)).wait_recv()
                passed.append(copy(w, 4 + j, (*chip, c), sibling))
                passed[-1].start()
        for w in range(n):
            copy(w, 0, sibling, (x, y, c)).wait_recv()
            for j, chip in enumerate(chips):
                copy(w, 4 + j, (*chip, 1 - c), (x, y, c)).wait_recv()
        for cp in first + passed:
            cp.wait_send()
        for cp in mine:
            cp.wait()

    return pl.pallas_call(
        body, name=name,
        in_specs=[pl.BlockSpec(memory_space=pltpu.VMEM)] * n,
        out_specs=[pl.BlockSpec(memory_space=pl.ANY)] * n,
        out_shape=[jax.ShapeDtypeStruct((N_DEV,) + a.shape, dt) for a, dt in zip(arrs, dtypes)],
        scratch_shapes=[pltpu.VMEM(a.shape, dt) for a, dt in zip(arrs, dtypes)]
        + [pltpu.SemaphoreType.DMA((n, 7)), pltpu.SemaphoreType.DMA((n, 7)), pltpu.SemaphoreType.DMA((n,))],
        compiler_params=pltpu.CompilerParams(vmem_limit_bytes=VMEM_LIMIT_BYTES),
    )(*arrs)


def _cast_shards(arrs):
    def body(*refs):
        for src, dst in zip(refs[:len(arrs)], refs[len(arrs):]):
            dst[...] = src[...].astype(BF16)

    return pl.pallas_call(body, name="cast_shards", out_shape=[jax.ShapeDtypeStruct(a.shape, BF16) for a in arrs],
                          compiler_params=pltpu.CompilerParams(vmem_limit_bytes=VMEM_LIMIT_BYTES))(*arrs)


def _shard_rows(rows):
    return rows if rows <= 512 else 256


def _adamw_math(w, g, m, v):
    m = ADAM_B1 * m + (1.0 - ADAM_B1) * g
    v = ADAM_B2 * v + (1.0 - ADAM_B2) * (g * g)
    m_hat = m / (1.0 - ADAM_B1 ** ADAM_STEP)
    v_hat = v / (1.0 - ADAM_B2 ** ADAM_STEP)
    delta = -ADAM_LR * (m_hat / (jnp.sqrt(v_hat) + ADAM_EPS) + ADAM_WD * w)
    return delta, m, v


def _reduce_adamw(parts, recv, own_slot, w, m, v, name):
    r, cdim = w.shape
    tr = _shard_rows(r)

    def body(idx_ref, p_ref, r_ref, w_ref, m_ref, v_ref, g_out, d_out, m_out, v_out):
        g = p_ref[0].astype(F32)
        for j in range(N_PEERS):
            g = g + r_ref[j].astype(F32)
        d, mn, vn = _adamw_math(w_ref[...], g, m_ref[...], v_ref[...])
        g_out[...] = g
        d_out[...] = d
        m_out[...] = mn
        v_out[...] = vn

    flat = pl.BlockSpec((tr, cdim), lambda i, idx_ref: (i, 0))
    return pl.pallas_call(
        body, name=name,
        grid_spec=pltpu.PrefetchScalarGridSpec(
            num_scalar_prefetch=1, grid=(r // tr,),
            in_specs=[pl.BlockSpec((1, tr, cdim), lambda i, idx_ref: (idx_ref[0], i, 0)),
                      pl.BlockSpec((N_PEERS, tr, cdim), lambda i, idx_ref: (0, i, 0)), flat, flat, flat],
            out_specs=[flat] * 4),
        out_shape=[jax.ShapeDtypeStruct((r, cdim), F32)] * 4,
        compiler_params=_cparams(("arbitrary",)),
    )(own_slot, parts, recv, w, m, v)


def _adamw_plain(w, g, m, v, name):
    def body(w_ref, g_ref, m_ref, v_ref, d_out, m_out, v_out):
        d_out[...], m_out[...], v_out[...] = _adamw_math(w_ref[...], g_ref[...], m_ref[...], v_ref[...])

    return pl.pallas_call(body, name=name, out_shape=[jax.ShapeDtypeStruct(w.shape, F32)] * 3)(w, g, m, v)


_SMALL_LAYOUT = (("ln_emb_g", 8), ("ln_emb_b", 8), ("hg_lower_bounds", 8), ("hg_norm_g", 1), ("attn_sinks", 1),
                 ("ln1_g", 8), ("ln1_b", 8), ("ln2_g", 8), ("ln2_b", 8))
_META_ROW = sum(r for _, r in _SMALL_LAYOUT)
_META_ROWS = N_META * D_MODEL // BLOCK
_LOSS_ROW = _META_ROW + _META_ROWS
SMALL_ROWS = 192


def _pack_small(vals, meta=None, loss_row=None):
    rows = []
    for name, nrows in _SMALL_LAYOUT:
        flat = vals[name].reshape(-1).astype(F32)
        flat = jnp.pad(flat, (0, nrows * BLOCK - flat.shape[0]))
        rows.append(flat.reshape(nrows, BLOCK))
    rows.append(jnp.zeros((_META_ROWS, BLOCK), F32) if meta is None else meta.reshape(_META_ROWS, BLOCK))
    rows.append(jnp.zeros((1, BLOCK), F32) if loss_row is None else loss_row)
    packed = jnp.concatenate(rows, axis=0)
    return jnp.pad(packed, ((0, SMALL_ROWS - packed.shape[0]), (0, 0)))


def _unpack_small(packed, shapes):
    out, row = {}, 0
    for name, nrows in _SMALL_LAYOUT:
        size = math.prod(shapes[name])
        out[name] = packed[row:row + nrows].reshape(-1)[:size].reshape(shapes[name])
        row += nrows
    return out


def _small_reduce_adamw(gathered, w, m, v):
    def body(g_ref, w_ref, m_ref, v_ref, g_out, d_out, m_out, v_out, loss_out):
        g = g_ref[0]
        for s in range(1, N_DEV):
            g = g + g_ref[s]
        d, mn, vn = _adamw_math(w_ref[...], g, m_ref[...], v_ref[...])
        g_out[...] = g
        d_out[...] = d
        m_out[...] = mn
        v_out[...] = vn
        loss_out[...] = jnp.broadcast_to(jnp.sum(g_ref[:, _LOSS_ROW, :]), (1, BLOCK))

    shp = jax.ShapeDtypeStruct((SMALL_ROWS, BLOCK), F32)
    return pl.pallas_call(body, name="small_reduce_adamw",
                          out_shape=[shp] * 4 + [jax.ShapeDtypeStruct((1, BLOCK), F32)])(gathered, w, m, v)


_WEIGHTS = ("meta_tokens", "ln_emb_g", "ln_emb_b", "w_in", "hg_lower_bounds", "hg_norm_g", "attn_sinks",
            "w_branch_hg", "w_branch_attn", "w_out", "ln1_g", "ln1_b", "w_ffn_in", "w_ffn_out", "ln2_g", "ln2_b")


def kernel(x, meta_tokens, ln_emb_g, ln_emb_b, w_in, hg_lower_bounds, hg_norm_g, attn_sinks, w_branch_hg, w_branch_attn, w_out, ln1_g, ln1_b, w_ffn_in, w_ffn_out, ln2_g, ln2_b, loss_target, m_meta_tokens, m_ln_emb_g, m_ln_emb_b, m_w_in, m_hg_lower_bounds, m_hg_norm_g, m_attn_sinks, m_w_branch_hg, m_w_branch_attn, m_w_out, m_ln1_g, m_ln1_b, m_w_ffn_in, m_w_ffn_out, m_ln2_g, m_ln2_b, v_meta_tokens, v_ln_emb_g, v_ln_emb_b, v_w_in, v_hg_lower_bounds, v_hg_norm_g, v_attn_sinks, v_w_branch_hg, v_w_branch_attn, v_w_out, v_ln1_g, v_ln1_b, v_w_ffn_in, v_w_ffn_out, v_ln2_g, v_ln2_b):
    given = dict(locals())
    weights = {n: given[n] for n in _WEIGHTS}
    mom1 = {n: given["m_" + n] for n in _WEIGHTS}
    mom2 = {n: given["v_" + n] for n in _WEIGHTS}
    shard2d = lambda a: a.reshape(a.shape[-2:])

    w_in_shard, *late_shards = _cast_shards([shard2d(weights[n]) for n in ("w_in",) + _LATE])
    loss_part, grad_x, small_grads, meta_grad, big = _device_step(
        x[0], loss_target[0], meta_tokens, ln_emb_g.reshape(1, -1), ln_emb_b.reshape(1, -1), w_in_shard,
        hg_lower_bounds, hg_norm_g, attn_sinks, late_shards, ln1_g, ln1_b, ln2_g, ln2_b)

    place = _place()
    out = {}
    for n, (parts, recv) in big.items():
        own = _slot(place, n in _SWAPPED).astype(jnp.int32).reshape(1)
        res = _reduce_adamw(parts, recv, own, shard2d(weights[n]), shard2d(mom1[n]), shard2d(mom2[n]), "adamw_" + n)
        out[n] = [r.reshape(weights[n].shape) for r in res]

    small_names = [n for n, _ in _SMALL_LAYOUT]
    packed = _pack_small(small_grads, meta_grad, loss_part)
    all_small, = _all_gather([packed], [F32], "gather_small")
    res = _small_reduce_adamw(all_small, _pack_small(weights), _pack_small(mom1), _pack_small(mom2))
    shapes = {n: weights[n].shape for n in small_names}
    unpacked = [_unpack_small(r, shapes) for r in res[:4]]
    for n in small_names:
        out[n] = [u[n] for u in unpacked]
    loss = res[4][0, 0]
    meta_whole = res[0][_META_ROW:_META_ROW + _META_ROWS].reshape(N_META, N_DEV, D_MODEL // N_DEV)
    g_meta_mine = lax.dynamic_index_in_dim(meta_whole, _slot(place, False), axis=1, keepdims=False)
    out["meta_tokens"] = [g_meta_mine, *_adamw_plain(meta_tokens, g_meta_mine, m_meta_tokens, v_meta_tokens,
                                                     "adamw_meta")]

    return (loss, grad_x[None], *[out[n][0] for n in _WEIGHTS], *[out[n][1] for n in _WEIGHTS],
            *[out[n][2] for n in _WEIGHTS], *[out[n][3] for n in _WEIGHTS])
```

```python
import functools
import math

import numpy as np
import jax
import jax.numpy as jnp
from jax import lax
from jax.experimental import pallas as pl
from jax.experimental.pallas import tpu as pltpu

F32 = jnp.float32
BF16 = jnp.bfloat16

D_MODEL = 1024
N_META = 16
BLOCK = 128
PAD = BLOCK - N_META
HG_HEADS = 4
HG_W = 512
ATT_HEADS = 8
HEAD_DIM = 64
ATT_QW = 512
ATT_KVW = 128
D_FF = 2816
EPS = 1e-5
ALPHA = 2.0 ** 0.25
ROPE_THETA = 10000.0
N_DEV = 8

ADAM_LR = 0.001
ADAM_B1 = 0.9
ADAM_B2 = 0.999
ADAM_EPS = 1e-08
ADAM_WD = 0.01
ADAM_STEP = 10

VMEM_LIMIT_BYTES = 56 * 1024 * 1024
MESH = pl.DeviceIdType.MESH

_LEVELS = (64, 32, 16, 8, 4, 2, 1)


def _cparams(sem):
    return pltpu.CompilerParams(dimension_semantics=sem, vmem_limit_bytes=VMEM_LIMIT_BYTES)


def _row_tile(rows, target):
    nb = rows // BLOCK
    best = 1
    for d in range(1, nb + 1):
        if nb % d == 0 and d * BLOCK <= target:
            best = d
    return best * BLOCK


_DN = {"nn": (((1,), (0,)), ((), ())), "nt": (((1,), (1,)), ((), ())), "tn": (((0,), (0,)), ((), ()))}


def _dot(a, b, form):
    return lax.dot_general(a.astype(BF16), b.astype(BF16), _DN[form], preferred_element_type=F32)


@functools.partial(jax.custom_vjp, nondiff_argnums=(2,))
def _mm(a, b, form):
    return _dot(a, b, form)


def _mm_fwd(a, b, form):
    a, b = a.astype(BF16), b.astype(BF16)
    return _dot(a, b, form), (a, b)


def _mm_bwd(form, res, g):
    a, b = res
    if form == "nn":
        return _dot(g, b, "nt"), _dot(a, g, "tn")
    if form == "nt":
        return _dot(g, b, "nn"), _dot(g, a, "tn")
    return _dot(b, g, "nt"), _dot(a, g, "nn")


_mm.defvjp(_mm_fwd, _mm_bwd)


def _split_dot(lv, x, form):
    return lax.dot_general(lv, x.astype(BF16), _DN[form], preferred_element_type=F32)


@jax.custom_vjp
def _swap_halves(x):
    return pltpu.roll(x, 64, 1)


_swap_halves.defvjp(lambda x: (pltpu.roll(x, 64, 1), None), lambda _, g: (pltpu.roll(g, 64, 1),))


def _tiled_matmul(a, b, form, *, tm, tn, tc, out_dtype, name):
    m, c = a.shape
    n = b.shape[1] if form == "nn" else b.shape[0]
    assert m % tm == 0 and n % tn == 0 and c % tc == 0, (name, a.shape, b.shape, tm, tn, tc)
    nc = c // tc

    def body(a_ref, b_ref, o_ref, *scratch):
        if nc == 1:
            o_ref[...] = _dot(a_ref[...], b_ref[...], form).astype(out_dtype)
            return
        acc_ref, = scratch
        ci = pl.program_id(2)

        @pl.when(ci == 0)
        def _():
            acc_ref[...] = jnp.zeros_like(acc_ref)

        acc_ref[...] += _dot(a_ref[...], b_ref[...], form)

        @pl.when(ci == nc - 1)
        def _():
            o_ref[...] = acc_ref[...].astype(out_dtype)

    b_spec = (pl.BlockSpec((tc, tn), lambda j, i, k: (k, j)) if form == "nn"
              else pl.BlockSpec((tn, tc), lambda j, i, k: (j, k)))
    return pl.pallas_call(
        body, name=name, grid=(n // tn, m // tm, nc),
        in_specs=[pl.BlockSpec((tm, tc), lambda j, i, k: (i, k)), b_spec],
        out_specs=pl.BlockSpec((tm, tn), lambda j, i, k: (i, j)),
        out_shape=jax.ShapeDtypeStruct((m, n), out_dtype),
        scratch_shapes=[] if nc == 1 else [pltpu.VMEM((tm, tn), F32)],
        compiler_params=_cparams(("arbitrary", "arbitrary", "arbitrary")),
    )(a, b)


def _tiled_matmul_tn(a, b, *, tm, tk, tn, out_dtype, name):
    m, k = a.shape
    n = b.shape[1]
    assert m % tm == 0 and k % tk == 0 and n % tn == 0, (name, a.shape, b.shape, tm, tk, tn)
    nm = m // tm

    def body(a_ref, b_ref, o_ref, acc_ref):
        mi = pl.program_id(2)

        @pl.when(mi == 0)
        def _():
            acc_ref[...] = jnp.zeros_like(acc_ref)

        acc_ref[...] += _dot(a_ref[...], b_ref[...], "tn")

        @pl.when(mi == nm - 1)
        def _():
            o_ref[...] = acc_ref[...].astype(out_dtype)

    return pl.pallas_call(
        body, name=name, grid=(k // tk, n // tn, nm),
        in_specs=[pl.BlockSpec((tm, tk), lambda kk, j, i: (i, kk)), pl.BlockSpec((tm, tn), lambda kk, j, i: (i, j))],
        out_specs=pl.BlockSpec((tk, tn), lambda kk, j, i: (kk, j)),
        out_shape=jax.ShapeDtypeStruct((k, n), out_dtype),
        scratch_shapes=[pltpu.VMEM((tk, tn), F32)],
        compiler_params=_cparams(("arbitrary", "arbitrary", "arbitrary")),
    )(a, b)


def _ln_stats(r):
    mu = jnp.mean(r, axis=-1, keepdims=True)
    xc = r - mu
    var = jnp.mean(xc * xc, axis=-1, keepdims=True)
    rstd = lax.rsqrt(var + EPS)
    return xc * rstd, rstd


def _ln_bwd(dy, xhat, rstd, g):
    dxhat = dy * g
    m1 = jnp.mean(dxhat, axis=-1, keepdims=True)
    m2 = jnp.mean(dxhat * xhat, axis=-1, keepdims=True)
    dr = rstd * (dxhat - m1 - xhat * m2)
    return dr, jnp.sum(dy * xhat, axis=0, keepdims=True), jnp.sum(dy, axis=0, keepdims=True)


N_SEG = 3 + len(_LEVELS)


def _level_stack():
    t = np.arange(BLOCK)[:, None]
    r = np.arange(BLOCK)[None, :]
    mats = [r <= t, r > t, np.ones((BLOCK, BLOCK), bool)]
    for h in _LEVELS:
        same = (t // (2 * h)) == (r // (2 * h))
        up_t, up_r = (t % (2 * h)) >= h, (r % (2 * h)) >= h
        mats.append(same & ((up_t & up_r & (r <= t)) | (~up_t & ~up_r & (r > t))))
    return jnp.asarray(np.concatenate(mats, axis=0).astype(np.float32), dtype=BF16)


def _hgrn_gates(hf, a0, a1, valid):
    lb = jax.nn.sigmoid(a0 - a1)
    fg = lb + (1.0 - lb) * jax.nn.sigmoid(hf)
    return jnp.where(valid, jnp.log(fg), 0.0), jnp.where(valid, 1.0 - fg, 0.0)


def _hgrn_scores(hq, k, *levels):
    q = jax.nn.silu(hq)
    rows = lax.broadcasted_iota(jnp.int32, (BLOCK, BLOCK), 0)
    cols = lax.broadcasted_iota(jnp.int32, (BLOCK, BLOCK), 1)
    a = jnp.where(rows == cols, jnp.sum(q * k, axis=-1, keepdims=True), 0.0)
    differ = jnp.bitwise_xor(rows, cols)
    for h, lvl in zip(_LEVELS, levels):
        decay = jnp.exp(lvl)
        pair = (cols < rows) & (differ >= h) & (differ < 2 * h)
        a = a + jnp.where(pair, _mm(q * decay, k * decay, "nt"), 0.0)
    return a


def _hgrn_mix(hq, k, v, st_in, a, seg_incl, seg_after, seg_total):
    o = _mm(jax.nn.silu(hq) * jnp.exp(seg_incl), st_in, "nt") + _mm(a, v, "nn")
    return o, st_in * jnp.exp(seg_total) + _mm(v, k * jnp.exp(seg_after), "tn")


def _hgrn_norm(o, hg, ng):
    return o * lax.rsqrt(jnp.mean(o * o, axis=-1, keepdims=True) + EPS) * ng * jax.nn.silu(hg)


def _seg_blocks(e, h):
    return [e[i * BLOCK:(i + 1) * BLOCK, h * BLOCK:(h + 1) * BLOCK] for i in range(N_SEG)]


def _rope(x, cos, sin, first_half):
    partner = jnp.where(first_half, -pltpu.roll(x, 96, 1), pltpu.roll(x, 32, 1))
    return x * cos + partner * sin


def _rope_t(g, cos, sin, first_half):
    u = g * sin
    partner = jnp.where(first_half, pltpu.roll(u, 96, 1), -pltpu.roll(u, 32, 1))
    return g * cos + partner


def _low_half(x):
    return lax.broadcasted_iota(jnp.int32, x.shape, 1) < HEAD_DIM


def _both_halves(x, g):
    sw = _swap_halves(x)
    return jnp.where(_low_half(x), x, sw) if g == 0 else jnp.where(_low_half(x), sw, x)


def _att_scores(qa, qb, kc, kp, km, g, own4, band4, meta4):
    low = _low_half(qa)
    q4 = jnp.concatenate([jnp.where(low, qa, 0.0), jnp.where(low, 0.0, qa),
                          jnp.where(low, qb, 0.0), jnp.where(low, 0.0, qb)], axis=0)
    scale = HEAD_DIM ** -0.5
    neg = jnp.finfo(F32).min
    s = jnp.where(own4, _mm(_both_halves(kc, g), q4, "nt"), _mm(_both_halves(kp, g), q4, "nt"))
    return (jnp.where(band4, s * scale, neg), jnp.where(meta4, _mm(_both_halves(km, g), q4, "nt") * scale, neg))


def _att_probs(s, sm, sinkrow):
    mx = jnp.maximum(jnp.maximum(jnp.max(s, axis=0, keepdims=True), jnp.max(sm, axis=0, keepdims=True)), sinkrow)
    p, pm, ps = jnp.exp(s - mx), jnp.exp(sm - mx), jnp.exp(sinkrow - mx)
    inv = 1.0 / (jnp.sum(p, axis=0, keepdims=True) + jnp.sum(pm, axis=0, keepdims=True) + ps)
    return p * inv, pm * inv, ps * inv


def _att_probs_bwd(p, pm, ps, dp, dpm):
    r = jnp.sum(p * dp, axis=0, keepdims=True) + jnp.sum(pm * dpm, axis=0, keepdims=True)
    return p * (dp - r), pm * (dpm - r), -ps * r


def _att_values(p, pm, vc, vp, vm, g, own4):
    o4 = (_mm(jnp.where(own4, p, 0.0), _both_halves(vc, g), "tn") + _mm(jnp.where(own4, 0.0, p), _both_halves(vp, g), "tn")
          + _mm(pm, _both_halves(vm, g), "tn"))
    tiles = []
    for j in range(2):
        upper = o4[(2 * j) * BLOCK:(2 * j + 1) * BLOCK]
        tiles.append(jnp.where(_low_half(upper), upper, o4[(2 * j + 1) * BLOCK:(2 * j + 2) * BLOCK]))
    return tiles


def _att_masks(blk_idx):
    kidx = lax.broadcasted_iota(jnp.int32, (BLOCK, BLOCK), 0)
    qrow = lax.broadcasted_iota(jnp.int32, (BLOCK, BLOCK), 1)
    own_side = kidx <= qrow
    pos_own = blk_idx * BLOCK + kidx - PAD
    ok_band = (own_side & (pos_own >= N_META)) | (~own_side & (pos_own - BLOCK >= N_META) & (blk_idx >= 1))
    qpos = blk_idx * BLOCK + lax.broadcasted_iota(jnp.int32, (N_META, BLOCK), 1) - PAD
    ok_meta = lax.broadcasted_iota(jnp.int32, (N_META, BLOCK), 0) <= qpos
    return [jnp.concatenate([m] * 4, axis=1) for m in (own_side, ok_band, ok_meta)]


def _token_streams(tr, tile_of=lambda i: i):
    k = tr // BLOCK
    return [pl.BlockSpec((BLOCK, D_MODEL), lambda i, j=j: (jnp.maximum(k * tile_of(i) - 1 + j, 0), 0))
            for j in range(k)]


def _embed_ln(x, meta_shard, w_in_shard, g0, b0):
    p = x.shape[0] + BLOCK
    tr = _row_tile(p, 640)
    k = tr // BLOCK
    nt = p // tr
    tile_of = lambda s: (s + 1) % nt
    shards = [meta_shard, w_in_shard]
    c_in, c_out, c_shapes, c_sems = _comm_specs(shards, N_DEV)

    def body(*refs):
        g_ref, b_ref = refs[k:k + 2]
        h_ref, hb_ref, xh_ref, rs_ref = refs[k + 4:k + 8]
        out_refs = refs[k + 8:k + 10]
        lead_ref, meta_ref = refs[k + 10:k + 12]
        starts, passes, waits = _gather_behind(refs[k + 2:k + 4], out_refs, refs[k + 12:], [False, False])
        s = pl.program_id(0)
        t = tile_of(s)

        @pl.when(s == 0)
        def _():
            lead_ref[...] = jnp.zeros_like(lead_ref)
            for start in starts:
                start()

        @pl.when(s == nt - 1)
        def _():
            for step in passes + waits:
                step()
            pltpu.sync_copy(out_refs[0], meta_ref)
            for d in range(N_DEV):
                lead_ref[PAD:BLOCK, d * BLOCK:(d + 1) * BLOCK] = meta_ref[d]

        first = jnp.where(t == 0, lead_ref[...], refs[0][...])
        xhat, rstd = _ln_stats(jnp.concatenate([first] + [r[...] for r in refs[1:k]], axis=0))
        row = t * tr + lax.broadcasted_iota(jnp.int32, (tr, 1), 0)
        h = jnp.where(row >= PAD, xhat * g_ref[...] + b_ref[...], 0.0)
        h_ref[...] = h
        hb_ref[...] = h.astype(BF16)
        xh_ref[...] = xhat
        rs_ref[...] = rstd

    vec = pl.BlockSpec((1, D_MODEL), lambda s: (0, 0))
    rowsp = pl.BlockSpec((tr, D_MODEL), lambda s: (tile_of(s), 0))
    return pl.pallas_call(
        body, name="embed_ln", grid=(nt,),
        in_specs=_token_streams(tr, tile_of) + [vec, vec] + c_in,
        out_specs=[rowsp, rowsp, rowsp, pl.BlockSpec((tr, 1), lambda s: (tile_of(s), 0))] + c_out,
        out_shape=[jax.ShapeDtypeStruct((p, D_MODEL), F32), jax.ShapeDtypeStruct((p, D_MODEL), BF16),
                   jax.ShapeDtypeStruct((p, D_MODEL), F32), jax.ShapeDtypeStruct((p, 1), F32)] + c_shapes,
        scratch_shapes=[pltpu.VMEM((BLOCK, D_MODEL), F32), pltpu.VMEM((N_DEV, N_META, BLOCK), F32)] + c_sems,
        compiler_params=_cparams(("arbitrary",)),
    )(*([x] * k), g0, b0, *shards)


def _rope_tables(p):
    pos = (np.arange(p, dtype=np.int32) - PAD).astype(np.float32)
    half = HEAD_DIM // 2
    inv = np.float32(ROPE_THETA) ** (-np.arange(half, dtype=np.float32) / np.float32(half))
    ang = pos[:, None] * np.tile(inv.astype(np.float32), BLOCK // half)[None, :]
    return jnp.asarray(np.cos(ang), F32), jnp.asarray(np.sin(ang), F32)


def _att_sinkrows(sink_ref):
    lanehead = lax.broadcasted_iota(jnp.int32, (1, 4 * BLOCK), 1) // BLOCK
    rows = []
    for g in range(2):
        row = jnp.zeros((1, 4 * BLOCK), F32)
        for j in range(4):
            row = jnp.where(lanehead == j, sink_ref[0, 4 * g + j], row)
        rows.append(row)
    return rows


def _first_half(rows):
    return (lax.broadcasted_iota(jnp.int32, (rows, BLOCK), 1) % HEAD_DIM) < (HEAD_DIM // 2)


def _att_load(qkv_ref, cos_ref, sin_ref, with_q):
    cos, sin, fh = cos_ref[...], sin_ref[...], _first_half(BLOCK)
    qs = [_rope(qkv_ref[:, j * BLOCK:(j + 1) * BLOCK], cos, sin, fh) for j in range(4)] if with_q else None
    k = _rope(qkv_ref[:, ATT_QW:ATT_QW + ATT_KVW], cos, sin, fh)
    v = qkv_ref[:, ATT_QW + ATT_KVW:ATT_QW + 2 * ATT_KVW]
    return qs, k, v


def _att_load_meta(qkv_ref, cos_ref, sin_ref):
    k = _rope(qkv_ref[PAD:BLOCK, ATT_QW:ATT_QW + ATT_KVW], cos_ref[PAD:BLOCK, :], sin_ref[PAD:BLOCK, :],
              _first_half(N_META))
    return k, qkv_ref[PAD:BLOCK, ATT_QW + ATT_KVW:ATT_QW + 2 * ATT_KVW]


def _att_specs(blk):
    w = ATT_QW + 2 * ATT_KVW
    cur = lambda width: pl.BlockSpec((BLOCK, width), lambda i: (blk(i), 0))
    prev = lambda width: pl.BlockSpec((BLOCK, width), lambda i: (jnp.maximum(blk(i) - 1, 0), 0))
    meta = lambda width: pl.BlockSpec((BLOCK, width), lambda i: (0, 0))
    return [cur(w), prev(w), meta(w), cur(BLOCK), cur(BLOCK), prev(BLOCK), prev(BLOCK), meta(BLOCK), meta(BLOCK),
            pl.BlockSpec(memory_space=pltpu.SMEM)]


_FLIPS = [(dx, dy, dc) for dx in (0, 1) for dy in (0, 1) for dc in (0, 1)][1:]
N_PEERS = len(_FLIPS)


def _place():
    return lax.axis_index("x"), lax.axis_index("y"), lax.axis_index("c")


def _peer(place, flip):
    return tuple(1 - p if f else p for p, f in zip(place, flip))


def _slot(place, swapped):
    x, y, c = place
    return 4 * y + 2 * x + c if swapped else 4 * x + 2 * y + c


def _comm_specs(arrs, out_lead):
    n = len(arrs)
    outs = [jax.ShapeDtypeStruct((out_lead,) + a.shape[-2:], a.dtype) for a in arrs]
    sems = [pltpu.SemaphoreType.DMA((n, N_PEERS)), pltpu.SemaphoreType.DMA((n, N_PEERS)), pltpu.SemaphoreType.DMA((n,))]
    return [pl.BlockSpec(memory_space=pl.ANY)] * n, [pl.BlockSpec(memory_space=pl.ANY)] * n, outs, sems


def _gather_behind(shard_refs, out_refs, sems, swapped):
    send_sems, recv_sems, local_sems = sems
    x, y, c = _place()
    me, sibling = (x, y, c), (x, y, 1 - c)
    chips = [(1 - x, y), (x, 1 - y), (1 - x, 1 - y)]
    starts, passes, waits = [], [], []
    for w, (s, o) in enumerate(zip(shard_refs, out_refs)):
        def copy(k, block, to, from_shard=False, w=w, s=s, o=o):
            rows = o.at[_slot(block, swapped[w])]
            return pltpu.make_async_remote_copy(
                src_ref=s if from_shard else rows, dst_ref=rows, send_sem=send_sems.at[w, k],
                recv_sem=recv_sems.at[w, k], device_id=to, device_id_type=MESH)

        own = pltpu.make_async_copy(s, o.at[_slot(me, swapped[w])], local_sems.at[w])
        first = [copy(0, me, sibling, True)] + [copy(1 + j, me, (*chip, c), True) for j, chip in enumerate(chips)]
        handed = [copy(4 + j, (*chip, c), sibling) for j, chip in enumerate(chips)]
        starts += [own.start] + [cp.start for cp in first]
        for j, chip in enumerate(chips):
            passes += [copy(1 + j, (*chip, c), me).wait_recv, handed[j].start]
        waits.append(copy(0, sibling, me).wait_recv)
        waits += [copy(4 + j, (*chip, 1 - c), me).wait_recv for j, chip in enumerate(chips)]
        waits += [cp.wait_send for cp in first + handed] + [own.wait]
    return starts, passes, waits


def _scatter_behind(part_refs, recv_refs, sems, swapped):
    send_sems, recv_sems, _ = sems
    place = _place()
    starts, waits = [], []
    for w, (p, o) in enumerate(zip(part_refs, recv_refs)):
        for r, flip in enumerate(_FLIPS):
            peer = _peer(place, flip)
            cp = pltpu.make_async_remote_copy(
                src_ref=p.at[_slot(peer, swapped[w])], dst_ref=o.at[r], send_sem=send_sems.at[w, r],
                recv_sem=recv_sems.at[w, r], device_id=peer, device_id_type=MESH)
            starts.append(cp.start)
            waits += [cp.wait_recv, cp.wait_send]
    return starts, waits


def _mixers_fwd(proj_hg, proj_att, lbounds, norm_g, lv, cos, sin, sinks, shards, swapped):
    p = proj_hg.shape[0]
    nb = p // BLOCK
    n = len(shards)
    c_in, c_out, c_shapes, c_sems = _comm_specs(shards, N_DEV)
    pass_step = min(nb - 1, max(1, (5 * nb) // 8))

    def body(*refs):
        x_ref, lb_ref, ng_ref, lv_ref, cur_ref, prev_ref, meta_ref, cc, sc, cp, sp, cm, sm, sink_ref = refs[:14]
        shard_refs = refs[14:14 + n]
        y_ref, o_ref, st_ref, a_ref, raw_ref, pr_ref = refs[14 + n:20 + n]
        out_refs = refs[20 + n:20 + 2 * n]
        carry_ref = refs[20 + 2 * n]
        starts, passes, waits = _gather_behind(shard_refs, out_refs, refs[21 + 2 * n:], swapped)
        c = pl.program_id(0)

        @pl.when(c == 0)
        def _():
            carry_ref[...] = jnp.zeros_like(carry_ref)
            for start in starts:
                start()

        @pl.when(c == pass_step)
        def _():
            for step in passes:
                step()

        valid = (c * BLOCK + lax.broadcasted_iota(jnp.int32, (BLOCK, 1), 0)) >= PAD
        logf, k = _hgrn_gates(x_ref[:, HG_W:2 * HG_W], lb_ref[0:1, :], lb_ref[1:2, :], valid)
        e = _split_dot(lv_ref[...], logf, "nn")
        for h in range(HG_HEADS):
            sl = lambda part: x_ref[:, part * HG_W + h * BLOCK: part * HG_W + (h + 1) * BLOCK]
            hs = slice(h * BLOCK, (h + 1) * BLOCK)
            st_in = carry_ref[h]
            st_ref[0, h] = st_in
            seg = _seg_blocks(e, h)
            a = _hgrn_scores(sl(0), k[:, hs], *seg[3:])
            a_ref[0, h] = a.astype(BF16)
            raw, st_out = _hgrn_mix(sl(0), k[:, hs], sl(2), st_in, a, *seg[:3])
            raw_ref[:, hs] = raw
            y_ref[:, hs] = _hgrn_norm(raw, sl(3), ng_ref[...]).astype(BF16)
            carry_ref[h] = st_out

        qs, kc, vc = _att_load(cur_ref, cc, sc, True)
        _, kp, vp = _att_load(prev_ref, cp, sp, False)
        km, vm = _att_load_meta(meta_ref, cm, sm)
        sinkrows = _att_sinkrows(sink_ref)
        own4, band4, meta4 = _att_masks(c)
        for g in range(2):
            s, s_meta = _att_scores(qs[2 * g], qs[2 * g + 1], kc, kp, km, g, own4, band4, meta4)
            pr, pr_meta, pr_sink = _att_probs(s, s_meta, sinkrows[g])
            pr_ref[0, g, :BLOCK, :] = pr.astype(BF16)
            pr_ref[0, g, BLOCK:BLOCK + N_META, :] = pr_meta.astype(BF16)
            pr_ref[0, g, BLOCK + N_META:, :] = jnp.broadcast_to(pr_sink, (N_META, 4 * BLOCK)).astype(BF16)
            for j, tile in enumerate(_att_values(pr, pr_meta, vc, vp, vm, g, own4)):
                o_ref[:, (2 * g + j) * BLOCK:(2 * g + j + 1) * BLOCK] = tile.astype(BF16)

        @pl.when(c == nb - 1)
        def _():
            for wait in waits:
                wait()

    return pl.pallas_call(
        body, name="mixers_fwd", grid=(nb,),
        in_specs=[pl.BlockSpec((BLOCK, 4 * HG_W), lambda c: (c, 0)), pl.BlockSpec((2, HG_W), lambda c: (0, 0)),
                  pl.BlockSpec((1, BLOCK), lambda c: (0, 0)), pl.BlockSpec(lv.shape, lambda c: (0, 0))]
        + _att_specs(lambda c: c) + c_in,
        out_specs=[pl.BlockSpec((BLOCK, HG_W), lambda c: (c, 0)), pl.BlockSpec((BLOCK, ATT_QW), lambda c: (c, 0)),
                   pl.BlockSpec((1, HG_HEADS, BLOCK, BLOCK), lambda c: (c, 0, 0, 0)),
                   pl.BlockSpec((1, HG_HEADS, BLOCK, BLOCK), lambda c: (c, 0, 0, 0)),
                   pl.BlockSpec((BLOCK, HG_W), lambda c: (c, 0)),
                   pl.BlockSpec((1, 2, ATT_KEYS, 4 * BLOCK), lambda c: (c, 0, 0, 0))] + c_out,
        out_shape=[jax.ShapeDtypeStruct((p, HG_W), BF16), jax.ShapeDtypeStruct((p, ATT_QW), BF16),
                   jax.ShapeDtypeStruct((nb, HG_HEADS, BLOCK, BLOCK), F32),
                   jax.ShapeDtypeStruct((nb, HG_HEADS, BLOCK, BLOCK), BF16),
                   jax.ShapeDtypeStruct((p, HG_W), F32),
                   jax.ShapeDtypeStruct((nb, 2, ATT_KEYS, 4 * BLOCK), BF16)] + c_shapes,
        scratch_shapes=[pltpu.VMEM((HG_HEADS, BLOCK, BLOCK), F32)] + c_sems,
        compiler_params=_cparams(("arbitrary",)),
    )(proj_hg, lbounds, norm_g, lv, proj_att, proj_att, proj_att, cos, sin, cos, sin, cos, sin, sinks, *shards)


def _tile(rows, preferred):
    return preferred if rows % preferred == 0 else _row_tile(rows, preferred)


def _branch_mix(yh, oa, gates, w_bh, w_ba):
    y_hg = _dot(yh, w_bh, "nn")
    y_att = _dot(oa, w_ba, "nn")
    s1 = jax.nn.sigmoid(gates[:, :D_MODEL])
    s2 = jax.nn.sigmoid(gates[:, D_MODEL:])
    return s1 * y_hg + s2 * y_att, y_hg, y_att, s1, s2


def _mix_out_ln1(yh, oa, gates, h0, w_bh, w_ba, w_out, g1, b1):
    p = yh.shape[0]
    tr = _tile(p, 320)

    def body(yh_ref, oa_ref, g_ref, h0_ref, wbh_ref, wba_ref, wo_ref, g1_ref, b1_ref,
             mix_ref, h1_ref, h1b_ref, xh_ref, rs_ref):
        mixin = _branch_mix(yh_ref[...], oa_ref[...], g_ref[...], wbh_ref[...], wba_ref[...])[0]
        mix_ref[...] = mixin.astype(BF16)
        xhat, rstd = _ln_stats(ALPHA * h0_ref[...] + _dot(mixin, wo_ref[...], "nn"))
        h1 = xhat * g1_ref[...] + b1_ref[...]
        h1_ref[...] = h1
        h1b_ref[...] = h1.astype(BF16)
        xh_ref[...] = xhat
        rs_ref[...] = rstd

    row = lambda w: pl.BlockSpec((tr, w), lambda i: (i, 0))
    const = lambda a: pl.BlockSpec(a.shape, lambda i: (0, 0))
    return pl.pallas_call(
        body, name="mix_out_ln1", grid=(p // tr,),
        in_specs=[row(HG_W), row(ATT_QW), row(2 * D_MODEL), row(D_MODEL), const(w_bh), const(w_ba), const(w_out),
                  const(g1), const(b1)],
        out_specs=[row(D_MODEL), row(D_MODEL), row(D_MODEL), row(D_MODEL), row(1)],
        out_shape=[jax.ShapeDtypeStruct((p, D_MODEL), BF16), jax.ShapeDtypeStruct((p, D_MODEL), F32),
                   jax.ShapeDtypeStruct((p, D_MODEL), BF16), jax.ShapeDtypeStruct((p, D_MODEL), F32),
                   jax.ShapeDtypeStruct((p, 1), F32)],
        compiler_params=_cparams(("arbitrary",)),
    )(yh, oa, gates, h0, w_bh, w_ba, w_out, g1, b1)


FF_T = D_FF // 2


def _ffn_in_swiglu(h1, w_fi):
    p = h1.shape[0]
    tm = _row_tile(p, 640)

    def body(h_ref, w_ref, au_ref, s_ref):
        au = _dot(h_ref[...], w_ref[...], "nn")
        au_ref[...] = au.astype(BF16)
        s_ref[...] = (jax.nn.silu(au[:, :FF_T]) * au[:, FF_T:]).astype(BF16)

    return pl.pallas_call(
        body, name="ffn_in_swiglu", grid=(D_FF // FF_T, p // tm),
        in_specs=[pl.BlockSpec((tm, D_MODEL), lambda j, i: (i, 0)), pl.BlockSpec((D_MODEL, 2 * FF_T), lambda j, i: (0, j))],
        out_specs=[pl.BlockSpec((tm, 2 * FF_T), lambda j, i: (i, j)), pl.BlockSpec((tm, FF_T), lambda j, i: (i, j))],
        out_shape=[jax.ShapeDtypeStruct((p, 2 * D_FF), BF16), jax.ShapeDtypeStruct((p, D_FF), BF16)],
        compiler_params=_cparams(("arbitrary", "arbitrary")),
    )(h1, w_fi)


def _ffn_out_loss(s, w_fo, h1, g2, b2, target):
    p = h1.shape[0]
    tr = _row_tile(p, 640)
    k = tr // BLOCK

    def body(*refs):
        s_ref, w_ref, h_ref, g_ref, b_ref = refs[:5]
        dr_ref, loss_ref, dg_ref, db_ref = refs[5 + k:]
        i = pl.program_id(0)
        xhat, rstd = _ln_stats(ALPHA * h_ref[...] + _dot(s_ref[...], w_ref[...], "nn"))
        y = xhat * g_ref[...] + b_ref[...]
        row = i * tr + lax.broadcasted_iota(jnp.int32, (tr, 1), 0)
        tgt = jnp.concatenate([r[...] for r in refs[5:5 + k]], axis=0)
        err = jnp.where(row >= BLOCK, y - tgt, 0.0)
        dr, dg, db = _ln_bwd(err * (1.0 / D_MODEL), xhat, rstd, g_ref[...])
        dr_ref[...] = dr
        e2 = jnp.sum(err * err, axis=0, keepdims=True)
        part = e2[:, 0:BLOCK]
        for j in range(1, D_MODEL // BLOCK):
            part = part + e2[:, j * BLOCK:(j + 1) * BLOCK]
        part = part * (0.5 / D_MODEL)

        @pl.when(i == 0)
        def _():
            loss_ref[...] = part
            dg_ref[...] = dg
            db_ref[...] = db

        @pl.when(i > 0)
        def _():
            loss_ref[...] += part
            dg_ref[...] += dg
            db_ref[...] += db

    vec = pl.BlockSpec((1, D_MODEL), lambda i: (0, 0))
    rowsp = pl.BlockSpec((tr, D_MODEL), lambda i: (i, 0))
    return pl.pallas_call(
        body, name="ffn_out_loss", grid=(p // tr,),
        in_specs=[pl.BlockSpec((tr, D_FF), lambda i: (i, 0)), pl.BlockSpec((D_FF, D_MODEL), lambda i: (0, 0)),
                  rowsp, vec, vec] + _token_streams(tr),
        out_specs=[rowsp, pl.BlockSpec((1, BLOCK), lambda i: (0, 0)), vec, vec],
        out_shape=[jax.ShapeDtypeStruct((p, D_MODEL), F32), jax.ShapeDtypeStruct((1, BLOCK), F32),
                   jax.ShapeDtypeStruct((1, D_MODEL), F32), jax.ShapeDtypeStruct((1, D_MODEL), F32)],
        compiler_params=_cparams(("arbitrary",)),
    )(s, w_fo, h1, g2, b2, *([target] * k))


def _d_ffn_hidden(dr2, w_fo, au):
    p = au.shape[0]
    tm = _row_tile(p, 640)

    def body(d_ref, w_ref, au_ref, o_ref):
        ds = _dot(d_ref[...], w_ref[...], "nt")
        _, vjp = jax.vjp(lambda a, u: jax.nn.silu(a) * u, au_ref[:, :FF_T].astype(F32), au_ref[:, FF_T:].astype(F32))
        da, du = vjp(ds)
        o_ref[:, :FF_T] = da.astype(BF16)
        o_ref[:, FF_T:] = du.astype(BF16)

    return pl.pallas_call(
        body, name="d_ffn_hidden", grid=(D_FF // FF_T, p // tm),
        in_specs=[pl.BlockSpec((tm, D_MODEL), lambda j, i: (i, 0)), pl.BlockSpec((FF_T, D_MODEL), lambda j, i: (j, 0)),
                  pl.BlockSpec((tm, 2 * FF_T), lambda j, i: (i, j))],
        out_specs=pl.BlockSpec((tm, 2 * FF_T), lambda j, i: (i, j)),
        out_shape=jax.ShapeDtypeStruct((p, 2 * D_FF), BF16), compiler_params=_cparams(("arbitrary", "arbitrary")),
    )(dr2, w_fo, au)


def _ln1_mix_bwd(dr2, dh1_ffn, xhat1, rstd1, g1, yh, oa, gates, w_bh, w_ba, w_out):
    p = yh.shape[0]
    tr = _tile(p, 320)

    def body(a_ref, b_ref, xh_ref, rs_ref, g1_ref, yh_ref, oa_ref, g_ref, wbh_ref, wba_ref, wo_ref,
             dr_ref, dyhg_ref, dyat_ref, dgt_ref, dyh_ref, doa_ref, dg_ref, db_ref):
        i = pl.program_id(0)
        dr, dg, db = _ln_bwd(ALPHA * a_ref[...] + b_ref[...], xh_ref[...], rs_ref[...], g1_ref[...])
        dr_ref[...] = dr
        d = _dot(dr, wo_ref[...], "nt")
        _, y_hg, y_att, s1, s2 = _branch_mix(yh_ref[...], oa_ref[...], g_ref[...], wbh_ref[...], wba_ref[...])
        dy_hg = d * s1
        dy_att = d * s2
        dyhg_ref[...] = dy_hg.astype(BF16)
        dyat_ref[...] = dy_att.astype(BF16)
        dgt_ref[:, :D_MODEL] = (d * y_hg * s1 * (1.0 - s1)).astype(BF16)
        dgt_ref[:, D_MODEL:] = (d * y_att * s2 * (1.0 - s2)).astype(BF16)
        dyh_ref[...] = _dot(dy_hg, wbh_ref[...], "nt")
        doa_ref[...] = _dot(dy_att, wba_ref[...], "nt")

        @pl.when(i == 0)
        def _():
            dg_ref[...] = dg
            db_ref[...] = db

        @pl.when(i > 0)
        def _():
            dg_ref[...] += dg
            db_ref[...] += db

    row = lambda w: pl.BlockSpec((tr, w), lambda i: (i, 0))
    const = lambda a: pl.BlockSpec(a.shape, lambda i: (0, 0))
    vec = pl.BlockSpec((1, D_MODEL), lambda i: (0, 0))
    return pl.pallas_call(
        body, name="ln1_mix_bwd", grid=(p // tr,),
        in_specs=[row(D_MODEL), row(D_MODEL), row(D_MODEL), row(1), vec, row(HG_W), row(ATT_QW), row(2 * D_MODEL),
                  const(w_bh), const(w_ba), const(w_out)],
        out_specs=[row(D_MODEL), row(D_MODEL), row(D_MODEL), row(2 * D_MODEL), row(HG_W), row(ATT_QW), vec, vec],
        out_shape=[jax.ShapeDtypeStruct((p, D_MODEL), F32), jax.ShapeDtypeStruct((p, D_MODEL), BF16),
                   jax.ShapeDtypeStruct((p, D_MODEL), BF16), jax.ShapeDtypeStruct((p, 2 * D_MODEL), BF16),
                   jax.ShapeDtypeStruct((p, HG_W), F32), jax.ShapeDtypeStruct((p, ATT_QW), F32),
                   jax.ShapeDtypeStruct((1, D_MODEL), F32), jax.ShapeDtypeStruct((1, D_MODEL), F32)],
        compiler_params=_cparams(("arbitrary",)),
    )(dr2, dh1_ffn, xhat1, rstd1, g1, yh, oa, gates, w_bh, w_ba, w_out)


MIX_W = 4 * HG_W + ATT_QW + 2 * ATT_KVW
ATT_KEYS = BLOCK + 2 * N_META


def _mixers_bwd(proj_hg, proj_att, lbounds, norm_g, lv, states, scores, raw, probs, cos, sin, sinks, dyh, doa,
                parts, swapped):
    p = proj_hg.shape[0]
    nb = p // BLOCK
    n = len(parts)
    kvw = 2 * ATT_KVW
    rev = lambda s: nb - 1 - s
    c_in, c_out, c_shapes, c_sems = _comm_specs(parts, N_PEERS)

    def body(*refs):
        (x_ref, lb_ref, ng_ref, lv_ref, st_ref, a_ref, raw_ref, pr_ref, cur_ref, prev_ref, meta_ref, cc, sc, cp, sp,
         cm, sm, sink_ref, dy_ref, do_ref) = refs[:20]
        part_refs = refs[20:20 + n]
        dx_ref, dlb_ref, dng_ref, dsink_ref = refs[20 + n:24 + n]
        recv_refs = refs[24 + n:24 + 2 * n]
        dcarry_ref, dkv_next_ref, dkv_meta_ref = refs[24 + 2 * n:27 + 2 * n]
        starts, waits = _scatter_behind(part_refs, recv_refs, refs[27 + 2 * n:], swapped)
        step = pl.program_id(0)
        c = rev(step)

        @pl.when(step == 0)
        def _():
            dcarry_ref[...] = jnp.zeros_like(dcarry_ref)
            dkv_next_ref[...] = jnp.zeros_like(dkv_next_ref)
            dkv_meta_ref[...] = jnp.zeros_like(dkv_meta_ref)
            dlb_ref[...] = jnp.zeros_like(dlb_ref)
            dng_ref[...] = jnp.zeros_like(dng_ref)
            dsink_ref[...] = jnp.zeros_like(dsink_ref)
            for start in starts:
                start()

        fh = _first_half(BLOCK)
        qs, kc, vc = _att_load(cur_ref, cc, sc, True)
        _, kp, vp = _att_load(prev_ref, cp, sp, False)
        km, vm = _att_load_meta(meta_ref, cm, sm)
        own4, band4, meta4 = _att_masks(c)
        att0 = 4 * HG_W
        dkm = dkp = dkc = dvm = dvp = dvc = 0.0
        dsinkrows = []
        for g in range(2):
            pr = pr_ref[0, g, :BLOCK, :].astype(F32)
            pr_meta = pr_ref[0, g, BLOCK:BLOCK + N_META, :].astype(F32)
            pr_sink = jnp.max(pr_ref[0, g, BLOCK + N_META:, :].astype(F32), axis=0, keepdims=True)
            _, values_vjp = jax.vjp(lambda *a, g=g: _att_values(*a, g, own4), pr, pr_meta, vc, vp, vm)
            dpr, dpr_meta, dvc_g, dvp_g, dvm_g = values_vjp(
                [do_ref[:, (2 * g + j) * BLOCK:(2 * g + j + 1) * BLOCK] for j in range(2)])
            ds, ds_meta, dsinkrow = _att_probs_bwd(pr, pr_meta, pr_sink, dpr, dpr_meta)
            _, scores_vjp = jax.vjp(lambda *a, g=g: _att_scores(*a, g, own4, band4, meta4),
                                    qs[2 * g], qs[2 * g + 1], kc, kp, km)
            dqa, dqb, dkc_g, dkp_g, dkm_g = scores_vjp((ds, ds_meta))
            for j, dq in enumerate((dqa, dqb)):
                dx_ref[:, att0 + (2 * g + j) * BLOCK:att0 + (2 * g + j + 1) * BLOCK] = _rope_t(
                    dq, cc[...], sc[...], fh).astype(BF16)
            dkm, dkp, dkc = dkm + dkm_g, dkp + dkp_g, dkc + dkc_g
            dvm, dvp, dvc = dvm + dvm_g, dvp + dvp_g, dvc + dvc_g
            dsinkrows.append(dsinkrow)
        ds0, ds1 = dsinkrows
        dkv_meta_ref[:, :BLOCK] += _rope_t(dkm, cm[PAD:BLOCK, :], sm[PAD:BLOCK, :], _first_half(N_META))
        dkv_meta_ref[:, BLOCK:] += dvm
        last = jnp.where(c == 0, 1.0, 0.0)
        to_meta_rows = lambda m: jnp.concatenate([jnp.zeros((PAD, BLOCK), F32), last * m], axis=0)
        dk = _rope_t(dkc, cc[...], sc[...], fh) + dkv_next_ref[:, :BLOCK] + to_meta_rows(dkv_meta_ref[:, :BLOCK])
        dv = dvc + dkv_next_ref[:, BLOCK:] + to_meta_rows(dkv_meta_ref[:, BLOCK:])
        dx_ref[:, att0 + ATT_QW:att0 + ATT_QW + ATT_KVW] = dk.astype(BF16)
        dx_ref[:, att0 + ATT_QW + ATT_KVW:] = dv.astype(BF16)
        dkv_next_ref[:, :BLOCK] = _rope_t(dkp, cp[...], sp[...], fh)
        dkv_next_ref[:, BLOCK:] = dvp
        sink_rows = []
        for dsg in (ds0, ds1):
            for j in range(4):
                tot = jnp.sum(dsg[:, j * BLOCK:(j + 1) * BLOCK], axis=1, keepdims=True)
                sink_rows.append(jnp.broadcast_to(tot, (1, BLOCK)))
        dsink_ref[...] += jnp.concatenate(sink_rows, axis=0)

        valid = (c * BLOCK + lax.broadcasted_iota(jnp.int32, (BLOCK, 1), 0)) >= PAD
        (logf, k), gates_vjp = jax.vjp(lambda hf, a0, a1: _hgrn_gates(hf, a0, a1, valid),
                                       x_ref[:, HG_W:2 * HG_W], lb_ref[0:1, :], lb_ref[1:2, :])
        lvv = lv_ref[...]
        e = _split_dot(lvv, logf, "nn")
        dng = jnp.zeros((1, BLOCK), F32)
        dk, dseg = [], []
        for h in range(HG_HEADS):
            sl = lambda part: x_ref[:, part * HG_W + h * BLOCK: part * HG_W + (h + 1) * BLOCK]
            hs = slice(h * BLOCK, (h + 1) * BLOCK)
            seg = _seg_blocks(e, h)
            _, norm_vjp = jax.vjp(_hgrn_norm, raw_ref[:, hs], sl(3), ng_ref[...])
            draw, dhg, dngh = norm_vjp(dy_ref[:, hs])
            _, mix_vjp = jax.vjp(_hgrn_mix, sl(0), k[:, hs], sl(2), st_ref[0, h], a_ref[0, h].astype(F32), *seg[:3])
            dhq, dkh, dhi, dst, da, *dseg_mix = mix_vjp((draw, dcarry_ref[h]))
            _, scores_vjp = jax.vjp(_hgrn_scores, sl(0), k[:, hs], *seg[3:])
            dhq2, dkh2, *dseg_lvl = scores_vjp(da)
            for part, val in ((0, dhq + dhq2), (2, dhi), (3, dhg)):
                dx_ref[:, part * HG_W + h * BLOCK: part * HG_W + (h + 1) * BLOCK] = val.astype(BF16)
            dk.append(dkh + dkh2)
            dseg.append(jnp.concatenate(dseg_mix + dseg_lvl, axis=0))
            dng = dng + dngh
            dcarry_ref[h] = dst
        dlogf = _split_dot(lvv, jnp.concatenate(dseg, axis=1), "tn")
        dhf, da0, da1 = gates_vjp((dlogf, jnp.concatenate(dk, axis=1)))
        dx_ref[:, HG_W:2 * HG_W] = dhf.astype(BF16)
        dlb_ref[0:1, :] += da0
        dlb_ref[1:2, :] += da1
        dng_ref[...] += dng

        @pl.when(step == nb - 1)
        def _():
            for wait in waits:
                wait()

    const = lambda shape: pl.BlockSpec(shape, lambda s: (0,) * len(shape))
    per_head = pl.BlockSpec((1, HG_HEADS, BLOCK, BLOCK), lambda s: (rev(s), 0, 0, 0))
    return pl.pallas_call(
        body, name="mixers_bwd", grid=(nb,),
        in_specs=[pl.BlockSpec((BLOCK, 4 * HG_W), lambda s: (rev(s), 0)), const((2, HG_W)), const((1, BLOCK)),
                  const(lv.shape), per_head, per_head, pl.BlockSpec((BLOCK, HG_W), lambda s: (rev(s), 0)),
                  pl.BlockSpec((1, 2, ATT_KEYS, 4 * BLOCK), lambda s: (rev(s), 0, 0, 0))]
        + _att_specs(rev)
        + [pl.BlockSpec((BLOCK, HG_W), lambda s: (rev(s), 0)), pl.BlockSpec((BLOCK, ATT_QW), lambda s: (rev(s), 0))]
        + c_in,
        out_specs=[pl.BlockSpec((BLOCK, MIX_W), lambda s: (rev(s), 0)), const((2, HG_W)), const((1, BLOCK)),
                   const((ATT_HEADS, BLOCK))] + c_out,
        out_shape=[jax.ShapeDtypeStruct((p, MIX_W), BF16), jax.ShapeDtypeStruct((2, HG_W), F32),
                   jax.ShapeDtypeStruct((1, BLOCK), F32), jax.ShapeDtypeStruct((ATT_HEADS, BLOCK), F32)] + c_shapes,
        scratch_shapes=[pltpu.VMEM((HG_HEADS, BLOCK, BLOCK), F32), pltpu.VMEM((BLOCK, kvw), F32),
                        pltpu.VMEM((N_META, kvw), F32)] + c_sems,
        compiler_params=_cparams(("arbitrary",)),
    )(proj_hg, lbounds, norm_g, lv, states, scores, raw, probs, proj_att, proj_att, proj_att, cos, sin, cos, sin,
      cos, sin, sinks, dyh, doa, *parts)


def _embed_bwd(dmix, dgates, w_mix, w_gates, dr1, xhat0, rstd0, g0, parts, swapped):
    p = dmix.shape[0]
    tm = _row_tile(p, 640)
    nm = p // tm
    n = len(parts)
    c_in, c_out, c_shapes, c_sems = _comm_specs(parts, N_PEERS)

    def body(*refs):
        a_ref, g_ref, wa_ref, wg_ref, dr_ref, xh_ref, rs_ref, g0_ref = refs[:8]
        o_ref, dg_ref, db_ref = refs[8 + n:11 + n]
        starts, waits = _scatter_behind(refs[8:8 + n], refs[11 + n:11 + 2 * n], refs[11 + 2 * n:], swapped)
        i = pl.program_id(0)

        @pl.when(i == 0)
        def _():
            for start in starts:
                start()

        dh0 = ALPHA * dr_ref[...] + _dot(a_ref[...], wa_ref[...], "nt") + _dot(g_ref[...], wg_ref[...], "nt")
        row = i * tm + lax.broadcasted_iota(jnp.int32, (tm, 1), 0)
        dx, dg, db = _ln_bwd(jnp.where(row >= PAD, dh0, 0.0), xh_ref[...], rs_ref[...], g0_ref[...])
        o_ref[...] = dx

        @pl.when(i == 0)
        def _():
            dg_ref[...] = dg
            db_ref[...] = db

        @pl.when(i > 0)
        def _():
            dg_ref[...] += dg
            db_ref[...] += db

        @pl.when(i == nm - 1)
        def _():
            for wait in waits:
                wait()

    row = lambda w: pl.BlockSpec((tm, w), lambda i: (i, 0))
    const = lambda a: pl.BlockSpec(a.shape, lambda i: (0, 0))
    vec = pl.BlockSpec((1, D_MODEL), lambda i: (0, 0))
    return pl.pallas_call(
        body, name="embed_bwd", grid=(nm,),
        in_specs=[row(dmix.shape[1]), row(dgates.shape[1]), const(w_mix), const(w_gates), row(D_MODEL), row(D_MODEL),
                  row(1), vec] + c_in,
        out_specs=[row(D_MODEL), vec, vec] + c_out,
        out_shape=[jax.ShapeDtypeStruct((p, D_MODEL), F32), jax.ShapeDtypeStruct((1, D_MODEL), F32),
                   jax.ShapeDtypeStruct((1, D_MODEL), F32)] + c_shapes,
        scratch_shapes=c_sems, compiler_params=_cparams(("arbitrary",)),
    )(dmix, dgates, w_mix, w_gates, dr1, xhat0, rstd0, g0, *parts)


_LATE = ("w_branch_hg", "w_branch_attn", "w_out", "w_ffn_in", "w_ffn_out")
_COLUMN_SHARDED = ("meta_tokens", "w_in", "w_branch_hg", "w_branch_attn", "w_ffn_in")
_SWAPPED = ("w_ffn_in",)


def _whole(name, gathered):
    _, r, c = gathered.shape
    if name in _COLUMN_SHARDED:
        return jnp.transpose(gathered, (1, 0, 2)).reshape(r, N_DEV * c)
    return gathered.reshape(N_DEV * r, c)


def _slots(name, whole):
    r, c = whole.shape
    if name in _COLUMN_SHARDED:
        return jnp.transpose(whole.reshape(r, N_DEV, c // N_DEV), (1, 0, 2))
    return whole.reshape(N_DEV, r // N_DEV, c)


def _device_step(x, target, meta_shard, ln_emb_g, ln_emb_b, w_in_shard, lbounds, norm_g, sinks, late_shards,
                 ln1_g, ln1_b, ln2_g, ln2_b):
    s = x.shape[0]
    p = s + BLOCK
    tm = _row_tile(p, 640)
    lv = _level_stack()
    cos, sin = _rope_tables(p)
    hg_end = 4 * HG_W
    mm = functools.partial(_tiled_matmul, tm=tm)
    swapped = [n in _SWAPPED for n in _LATE]

    h0, h0b, xhat0, rstd0, _, g_win = _embed_ln(x, meta_shard, w_in_shard, ln_emb_g, ln_emb_b)
    w_in = _whole("w_in", g_win)
    proj_hg = mm(h0b, w_in[:, :hg_end], "nn", tn=hg_end, tc=D_MODEL, out_dtype=F32, name="proj_hg")
    proj_att = mm(h0b, w_in[:, hg_end:MIX_W], "nn", tn=MIX_W - hg_end, tc=D_MODEL, out_dtype=F32, name="proj_att")
    gates = mm(h0b, w_in[:, MIX_W:], "nn", tn=2 * D_MODEL, tc=D_MODEL, out_dtype=F32, name="proj_gates")
    yh, oa, states, scores, raw, probs, *gathered = _mixers_fwd(
        proj_hg, proj_att, lbounds, norm_g, lv, cos, sin, sinks, late_shards, swapped)
    w_bh, w_ba, w_out, w_fi, w_fo = [_whole(n, g) for n, g in zip(_LATE, gathered)]
    mixin, h1, h1b, xhat1, rstd1 = _mix_out_ln1(yh, oa, gates, h0, w_bh, w_ba, w_out, ln1_g, ln1_b)
    au, sw = _ffn_in_swiglu(h1b, w_fi)
    dr2, loss_part, dg2, db2 = _ffn_out_loss(sw, w_fo, h1, ln2_g, ln2_b, target)

    mtn = functools.partial(_tiled_matmul_tn, tm=_row_tile(p, 1664), out_dtype=BF16)
    d_wfo = mtn(sw, dr2, tk=FF_T, tn=D_MODEL, name="grad_w_ffn_out")
    dau = _d_ffn_hidden(dr2, w_fo, au)
    d_wfi = mtn(h1b, dau, tk=D_MODEL, tn=FF_T, name="grad_w_ffn_in")
    dh1_ffn = mm(dau, w_fi, "nt", tn=D_MODEL, tc=D_FF, out_dtype=F32, name="d_h1_ffn")
    dr1, dy_hg, dy_att, dgates, dyh, doa, dg1, db1 = _ln1_mix_bwd(
        dr2, dh1_ffn, xhat1, rstd1, ln1_g, yh, oa, gates, w_bh, w_ba, w_out)
    d_wout = mtn(mixin, dr1, tk=D_MODEL, tn=D_MODEL, name="grad_w_out")
    d_wbh = mtn(yh, dy_hg, tk=HG_W, tn=D_MODEL, name="grad_w_branch_hg")
    d_wba = mtn(oa, dy_att, tk=ATT_QW, tn=D_MODEL, name="grad_w_branch_attn")
    late_parts = [_slots(n, g) for n, g in zip(_LATE, (d_wbh, d_wba, d_wout, d_wfi, d_wfo))]
    dmix, d_lb, d_ng, d_sink, *late_recv = _mixers_bwd(
        proj_hg, proj_att, lbounds, norm_g, lv, states, scores, raw, probs, cos, sin, sinks, dyh, doa, late_parts,
        swapped)
    d_win = jnp.concatenate([mtn(h0b, dmix, tk=D_MODEL, tn=MIX_W // 2, name="grad_w_in_mixers"),
                             mtn(h0b, dgates, tk=D_MODEL, tn=D_MODEL, name="grad_w_in_gates")], axis=1)
    win_parts = _slots("w_in", d_win)
    dxin, dg0, db0, win_recv = _embed_bwd(dmix, dgates, w_in[:, :MIX_W], w_in[:, MIX_W:], dr1, xhat0, rstd0, ln_emb_g,
                                          [win_parts], [False])

    small = dict(ln_emb_g=dg0, ln_emb_b=db0, hg_lower_bounds=d_lb, hg_norm_g=d_ng, attn_sinks=d_sink[:, 0],
                 ln1_g=dg1, ln1_b=db1, ln2_g=dg2, ln2_b=db2)
    big = dict(zip(_LATE, zip(late_parts, late_recv)))
    big["w_in"] = (win_parts, win_recv)
    return loss_part, dxin[BLOCK:], small, dxin[PAD:BLOCK], big


def _all_gather(arrs, dtypes, name):
    n = len(arrs)

    def body(*refs):
        ins, outs, stages = refs[:n], refs[n:2 * n], refs[2 * n:3 * n]
        send_sems, recv_sems, local_sems = refs[3 * n:]
        x, y, c = _place()
        sibling = (x, y, 1 - c)
        chips = [(1 - x, y), (x, 1 - y), (1 - x, 1 - y)]
        slot = lambda px, py, pc: 4 * px + 2 * py + pc

        def copy(w, k, block, to, from_stage=False):
            return pltpu.make_async_remote_copy(
                src_ref=stages[w] if from_stage else outs[w].at[slot(*block)], dst_ref=outs[w].at[slot(*block)],
                send_sem=send_sems.at[w, k], recv_sem=recv_sems.at[w, k], device_id=to, device_id_type=MESH)

        mine, first, passed = [], [], []
        for w in range(n):
            stages[w][...] = ins[w][...].astype(dtypes[w])
            mine.append(pltpu.make_async_copy(stages[w], outs[w].at[slot(x, y, c)], local_sems.at[w]))
            mine[-1].start()
        for w in range(n):
            first.append(copy(w, 0, (x, y, c), sibling, from_stage=True))
            first += [copy(w, 1 + j, (x, y, c), (*chip, c), from_stage=True) for j, chip in enumerate(chips)]
        for cp in first:
            cp.start()
        for j, chip in enumerate(chips):
            for w in range(n):
                copy(w, 1 + j, (*chip, c), (x, y, c)).wait_recv()
                passed.append(copy(w, 4 + j, (*chip, c), sibling))
                passed[-1].start()
        for w in range(n):
            copy(w, 0, sibling, (x, y, c)).wait_recv()
            for j, chip in enumerate(chips):
                copy(w, 4 + j, (*chip, 1 - c), (x, y, c)).wait_recv()
        for cp in first + passed:
            cp.wait_send()
        for cp in mine:
            cp.wait()

    return pl.pallas_call(
        body, name=name,
        in_specs=[pl.BlockSpec(memory_space=pltpu.VMEM)] * n,
        out_specs=[pl.BlockSpec(memory_space=pl.ANY)] * n,
        out_shape=[jax.ShapeDtypeStruct((N_DEV,) + a.shape, dt) for a, dt in zip(arrs, dtypes)],
        scratch_shapes=[pltpu.VMEM(a.shape, dt) for a, dt in zip(arrs, dtypes)]
        + [pltpu.SemaphoreType.DMA((n, 7)), pltpu.SemaphoreType.DMA((n, 7)), pltpu.SemaphoreType.DMA((n,))],
        compiler_params=pltpu.CompilerParams(vmem_limit_bytes=VMEM_LIMIT_BYTES),
    )(*arrs)


def _cast_shards(arrs):
    def body(*refs):
        for src, dst in zip(refs[:len(arrs)], refs[len(arrs):]):
            dst[...] = src[...].astype(BF16)

    return pl.pallas_call(body, name="cast_shards", out_shape=[jax.ShapeDtypeStruct(a.shape, BF16) for a in arrs],
                          compiler_params=pltpu.CompilerParams(vmem_limit_bytes=VMEM_LIMIT_BYTES))(*arrs)


def _shard_rows(rows):
    return rows if rows <= 512 else 256


def _adamw_math(w, g, m, v):
    m = ADAM_B1 * m + (1.0 - ADAM_B1) * g
    v = ADAM_B2 * v + (1.0 - ADAM_B2) * (g * g)
    m_hat = m / (1.0 - ADAM_B1 ** ADAM_STEP)
    v_hat = v / (1.0 - ADAM_B2 ** ADAM_STEP)
    delta = -ADAM_LR * (m_hat / (jnp.sqrt(v_hat) + ADAM_EPS) + ADAM_WD * w)
    return delta, m, v


def _reduce_adamw(parts, recv, own_slot, w, m, v, name):
    r, cdim = w.shape
    tr = _shard_rows(r)

    def body(idx_ref, p_ref, r_ref, w_ref, m_ref, v_ref, g_out, d_out, m_out, v_out):
        g = p_ref[0].astype(F32)
        for j in range(N_PEERS):
            g = g + r_ref[j].astype(F32)
        d, mn, vn = _adamw_math(w_ref[...], g, m_ref[...], v_ref[...])
        g_out[...] = g
        d_out[...] = d
        m_out[...] = mn
        v_out[...] = vn

    flat = pl.BlockSpec((tr, cdim), lambda i, idx_ref: (i, 0))
    return pl.pallas_call(
        body, name=name,
        grid_spec=pltpu.PrefetchScalarGridSpec(
            num_scalar_prefetch=1, grid=(r // tr,),
            in_specs=[pl.BlockSpec((1, tr, cdim), lambda i, idx_ref: (idx_ref[0], i, 0)),
                      pl.BlockSpec((N_PEERS, tr, cdim), lambda i, idx_ref: (0, i, 0)), flat, flat, flat],
            out_specs=[flat] * 4),
        out_shape=[jax.ShapeDtypeStruct((r, cdim), F32)] * 4,
        compiler_params=_cparams(("arbitrary",)),
    )(own_slot, parts, recv, w, m, v)


def _adamw_plain(w, g, m, v, name):
    def body(w_ref, g_ref, m_ref, v_ref, d_out, m_out, v_out):
        d_out[...], m_out[...], v_out[...] = _adamw_math(w_ref[...], g_ref[...], m_ref[...], v_ref[...])

    return pl.pallas_call(body, name=name, out_shape=[jax.ShapeDtypeStruct(w.shape, F32)] * 3)(w, g, m, v)


_SMALL_LAYOUT = (("ln_emb_g", 8), ("ln_emb_b", 8), ("hg_lower_bounds", 8), ("hg_norm_g", 1), ("attn_sinks", 1),
                 ("ln1_g", 8), ("ln1_b", 8), ("ln2_g", 8), ("ln2_b", 8))
_META_ROW = sum(r for _, r in _SMALL_LAYOUT)
_META_ROWS = N_META * D_MODEL // BLOCK
_LOSS_ROW = _META_ROW + _META_ROWS
SMALL_ROWS = 192


def _pack_small(vals, meta=None, loss_row=None):
    rows = []
    for name, nrows in _SMALL_LAYOUT:
        flat = vals[name].reshape(-1).astype(F32)
        flat = jnp.pad(flat, (0, nrows * BLOCK - flat.shape[0]))
        rows.append(flat.reshape(nrows, BLOCK))
    rows.append(jnp.zeros((_META_ROWS, BLOCK), F32) if meta is None else meta.reshape(_META_ROWS, BLOCK))
    rows.append(jnp.zeros((1, BLOCK), F32) if loss_row is None else loss_row)
    packed = jnp.concatenate(rows, axis=0)
    return jnp.pad(packed, ((0, SMALL_ROWS - packed.shape[0]), (0, 0)))


def _unpack_small(packed, shapes):
    out, row = {}, 0
    for name, nrows in _SMALL_LAYOUT:
        size = math.prod(shapes[name])
        out[name] = packed[row:row + nrows].reshape(-1)[:size].reshape(shapes[name])
        row += nrows
    return out


def _small_reduce_adamw(gathered, w, m, v):
    def body(g_ref, w_ref, m_ref, v_ref, g_out, d_out, m_out, v_out, loss_out):
        g = g_ref[0]
        for s in range(1, N_DEV):
            g = g + g_ref[s]
        d, mn, vn = _adamw_math(w_ref[...], g, m_ref[...], v_ref[...])
        g_out[...] = g
        d_out[...] = d
        m_out[...] = mn
        v_out[...] = vn
        loss_out[...] = jnp.broadcast_to(jnp.sum(g_ref[:, _LOSS_ROW, :]), (1, BLOCK))

    shp = jax.ShapeDtypeStruct((SMALL_ROWS, BLOCK), F32)
    return pl.pallas_call(body, name="small_reduce_adamw",
                          out_shape=[shp] * 4 + [jax.ShapeDtypeStruct((1, BLOCK), F32)])(gathered, w, m, v)


_WEIGHTS = ("meta_tokens", "ln_emb_g", "ln_emb_b", "w_in", "hg_lower_bounds", "hg_norm_g", "attn_sinks",
            "w_branch_hg", "w_branch_attn", "w_out", "ln1_g", "ln1_b", "w_ffn_in", "w_ffn_out", "ln2_g", "ln2_b")


def kernel(x, meta_tokens, ln_emb_g, ln_emb_b, w_in, hg_lower_bounds, hg_norm_g, attn_sinks, w_branch_hg, w_branch_attn, w_out, ln1_g, ln1_b, w_ffn_in, w_ffn_out, ln2_g, ln2_b, loss_target, m_meta_tokens, m_ln_emb_g, m_ln_emb_b, m_w_in, m_hg_lower_bounds, m_hg_norm_g, m_attn_sinks, m_w_branch_hg, m_w_branch_attn, m_w_out, m_ln1_g, m_ln1_b, m_w_ffn_in, m_w_ffn_out, m_ln2_g, m_ln2_b, v_meta_tokens, v_ln_emb_g, v_ln_emb_b, v_w_in, v_hg_lower_bounds, v_hg_norm_g, v_attn_sinks, v_w_branch_hg, v_w_branch_attn, v_w_out, v_ln1_g, v_ln1_b, v_w_ffn_in, v_w_ffn_out, v_ln2_g, v_ln2_b):
    given = dict(locals())
    weights = {n: given[n] for n in _WEIGHTS}
    mom1 = {n: given["m_" + n] for n in _WEIGHTS}
    mom2 = {n: given["v_" + n] for n in _WEIGHTS}
    shard2d = lambda a: a.reshape(a.shape[-2:])

    w_in_shard, *late_shards = _cast_shards([shard2d(weights[n]) for n in ("w_in",) + _LATE])
    loss_part, grad_x, small_grads, meta_grad, big = _device_step(
        x[0], loss_target[0], meta_tokens, ln_emb_g.reshape(1, -1), ln_emb_b.reshape(1, -1), w_in_shard,
        hg_lower_bounds, hg_norm_g, attn_sinks, late_shards, ln1_g, ln1_b, ln2_g, ln2_b)

    place = _place()
    out = {}
    for n, (parts, recv) in big.items():
        own = _slot(place, n in _SWAPPED).astype(jnp.int32).reshape(1)
        res = _reduce_adamw(parts, recv, own, shard2d(weights[n]), shard2d(mom1[n]), shard2d(mom2[n]), "adamw_" + n)
        out[n] = [r.reshape(weights[n].shape) for r in res]

    small_names = [n for n, _ in _SMALL_LAYOUT]
    packed = _pack_small(small_grads, meta_grad, loss_part)
    all_small, = _all_gather([packed], [F32], "gather_small")
    res = _small_reduce_adamw(all_small, _pack_small(weights), _pack_small(mom1), _pack_small(mom2))
    shapes = {n: weights[n].shape for n in small_names}
    unpacked = [_unpack_small(r, shapes) for r in res[:4]]
    for n in small_names:
        out[n] = [u[n] for u in unpacked]
    loss = res[4][0, 0]
    meta_whole = res[0][_META_ROW:_META_ROW + _META_ROWS].reshape(N_META, N_DEV, D_MODEL // N_DEV)
    g_meta_mine = lax.dynamic_index_in_dim(meta_whole, _slot(place, False), axis=1, keepdims=False)
    out["meta_tokens"] = [g_meta_mine, *_adamw_plain(meta_tokens, g_meta_mine, m_meta_tokens, v_meta_tokens,
                                                     "adamw_meta")]

    return (loss, grad_x[None], *[out[n][0] for n in _WEIGHTS], *[out[n][1] for n in _WEIGHTS],
            *[out[n][2] for n in _WEIGHTS], *[out[n][3] for n in _WEIGHTS])
```

```python
import functools
import math

import numpy as np
import jax
import jax.numpy as jnp
from jax import lax
from jax.experimental import pallas as pl
from jax.experimental.pallas import tpu as pltpu

F32 = jnp.float32
BF16 = jnp.bfloat16

D_MODEL = 1024
N_META = 16
BLOCK = 128
PAD = BLOCK - N_META
HG_HEADS = 4
HG_W = 512
ATT_HEADS = 8
HEAD_DIM = 64
ATT_QW = 512
ATT_KVW = 128
D_FF = 2816
EPS = 1e-5
ALPHA = 2.0 ** 0.25
ROPE_THETA = 10000.0
N_DEV = 8

ADAM_LR = 0.001
ADAM_B1 = 0.9
ADAM_B2 = 0.999
ADAM_EPS = 1e-08
ADAM_WD = 0.01
ADAM_STEP = 10

VMEM_LIMIT_BYTES = 56 * 1024 * 1024
MESH = pl.DeviceIdType.MESH

_LEVELS = (64, 32, 16, 8, 4, 2, 1)


def _cparams(sem):
    return pltpu.CompilerParams(dimension_semantics=sem, vmem_limit_bytes=VMEM_LIMIT_BYTES)


def _row_tile(rows, target):
    nb = rows // BLOCK
    best = 1
    for d in range(1, nb + 1):
        if nb % d == 0 and d * BLOCK <= target:
            best = d
    return best * BLOCK


_DN = {"nn": (((1,), (0,)), ((), ())), "nt": (((1,), (1,)), ((), ())), "tn": (((0,), (0,)), ((), ()))}


def _dot(a, b, form):
    return lax.dot_general(a.astype(BF16), b.astype(BF16), _DN[form], preferred_element_type=F32)


@functools.partial(jax.custom_vjp, nondiff_argnums=(2,))
def _mm(a, b, form):
    return _dot(a, b, form)


def _mm_fwd(a, b, form):
    a, b = a.astype(BF16), b.astype(BF16)
    return _dot(a, b, form), (a, b)


def _mm_bwd(form, res, g):
    a, b = res
    if form == "nn":
        return _dot(g, b, "nt"), _dot(a, g, "tn")
    if form == "nt":
        return _dot(g, b, "nn"), _dot(g, a, "tn")
    return _dot(b, g, "nt"), _dot(a, g, "nn")


_mm.defvjp(_mm_fwd, _mm_bwd)


def _split_dot(lv, x, form):
    return lax.dot_general(lv, x.astype(BF16), _DN[form], preferred_element_type=F32)


@jax.custom_vjp
def _swap_halves(x):
    return pltpu.roll(x, 64, 1)


_swap_halves.defvjp(lambda x: (pltpu.roll(x, 64, 1), None), lambda _, g: (pltpu.roll(g, 64, 1),))


def _tiled_matmul(a, b, form, *, tm, tn, tc, out_dtype, name):
    m, c = a.shape
    n = b.shape[1] if form == "nn" else b.shape[0]
    assert m % tm == 0 and n % tn == 0 and c % tc == 0, (name, a.shape, b.shape, tm, tn, tc)
    nc = c // tc

    def body(a_ref, b_ref, o_ref, *scratch):
        if nc == 1:
            o_ref[...] = _dot(a_ref[...], b_ref[...], form).astype(out_dtype)
            return
        acc_ref, = scratch
        ci = pl.program_id(2)

        @pl.when(ci == 0)
        def _():
            acc_ref[...] = jnp.zeros_like(acc_ref)

        acc_ref[...] += _dot(a_ref[...], b_ref[...], form)

        @pl.when(ci == nc - 1)
        def _():
            o_ref[...] = acc_ref[...].astype(out_dtype)

    b_spec = (pl.BlockSpec((tc, tn), lambda j, i, k: (k, j)) if form == "nn"
              else pl.BlockSpec((tn, tc), lambda j, i, k: (j, k)))
    return pl.pallas_call(
        body, name=name, grid=(n // tn, m // tm, nc),
        in_specs=[pl.BlockSpec((tm, tc), lambda j, i, k: (i, k)), b_spec],
        out_specs=pl.BlockSpec((tm, tn), lambda j, i, k: (i, j)),
        out_shape=jax.ShapeDtypeStruct((m, n), out_dtype),
        scratch_shapes=[] if nc == 1 else [pltpu.VMEM((tm, tn), F32)],
        compiler_params=_cparams(("arbitrary", "arbitrary", "arbitrary")),
    )(a, b)


def _tiled_matmul_tn(a, b, *, tm, tk, tn, out_dtype, name):
    m, k = a.shape
    n = b.shape[1]
    assert m % tm == 0 and k % tk == 0 and n % tn == 0, (name, a.shape, b.shape, tm, tk, tn)
    nm = m // tm

    def body(a_ref, b_ref, o_ref, acc_ref):
        mi = pl.program_id(2)

        @pl.when(mi == 0)
        def _():
            acc_ref[...] = jnp.zeros_like(acc_ref)

        acc_ref[...] += _dot(a_ref[...], b_ref[...], "tn")

        @pl.when(mi == nm - 1)
        def _():
            o_ref[...] = acc_ref[...].astype(out_dtype)

    return pl.pallas_call(
        body, name=name, grid=(k // tk, n // tn, nm),
        in_specs=[pl.BlockSpec((tm, tk), lambda kk, j, i: (i, kk)), pl.BlockSpec((tm, tn), lambda kk, j, i: (i, j))],
        out_specs=pl.BlockSpec((tk, tn), lambda kk, j, i: (kk, j)),
        out_shape=jax.ShapeDtypeStruct((k, n), out_dtype),
        scratch_shapes=[pltpu.VMEM((tk, tn), F32)],
        compiler_params=_cparams(("arbitrary", "arbitrary", "arbitrary")),
    )(a, b)


def _ln_stats(r):
    mu = jnp.mean(r, axis=-1, keepdims=True)
    xc = r - mu
    var = jnp.mean(xc * xc, axis=-1, keepdims=True)
    rstd = lax.rsqrt(var + EPS)
    return xc * rstd, rstd


def _ln_bwd(dy, xhat, rstd, g):
    dxhat = dy * g
    m1 = jnp.mean(dxhat, axis=-1, keepdims=True)
    m2 = jnp.mean(dxhat * xhat, axis=-1, keepdims=True)
    dr = rstd * (dxhat - m1 - xhat * m2)
    return dr, jnp.sum(dy * xhat, axis=0, keepdims=True), jnp.sum(dy, axis=0, keepdims=True)


N_SEG = 3 + len(_LEVELS)


def _level_stack():
    t = np.arange(BLOCK)[:, None]
    r = np.arange(BLOCK)[None, :]
    mats = [r <= t, r > t, np.ones((BLOCK, BLOCK), bool)]
    for h in _LEVELS:
        same = (t // (2 * h)) == (r // (2 * h))
        up_t, up_r = (t % (2 * h)) >= h, (r % (2 * h)) >= h
        mats.append(same & ((up_t & up_r & (r <= t)) | (~up_t & ~up_r & (r > t))))
    return jnp.asarray(np.concatenate(mats, axis=0).astype(np.float32), dtype=BF16)


def _hgrn_gates(hf, a0, a1, valid):
    lb = jax.nn.sigmoid(a0 - a1)
    fg = lb + (1.0 - lb) * jax.nn.sigmoid(hf)
    return jnp.where(valid, jnp.log(fg), 0.0), jnp.where(valid, 1.0 - fg, 0.0)


def _hgrn_scores(hq, k, *levels):
    q = jax.nn.silu(hq)
    rows = lax.broadcasted_iota(jnp.int32, (BLOCK, BLOCK), 0)
    cols = lax.broadcasted_iota(jnp.int32, (BLOCK, BLOCK), 1)
    a = jnp.where(rows == cols, jnp.sum(q * k, axis=-1, keepdims=True), 0.0)
    differ = jnp.bitwise_xor(rows, cols)
    for h, lvl in zip(_LEVELS, levels):
        decay = jnp.exp(lvl)
        pair = (cols < rows) & (differ >= h) & (differ < 2 * h)
        a = a + jnp.where(pair, _mm(q * decay, k * decay, "nt"), 0.0)
    return a


def _hgrn_mix(hq, k, v, st_in, a, seg_incl, seg_after, seg_total):
    o = _mm(jax.nn.silu(hq) * jnp.exp(seg_incl), st_in, "nt") + _mm(a, v, "nn")
    return o, st_in * jnp.exp(seg_total) + _mm(v, k * jnp.exp(seg_after), "tn")


def _hgrn_norm(o, hg, ng):
    return o * lax.rsqrt(jnp.mean(o * o, axis=-1, keepdims=True) + EPS) * ng * jax.nn.silu(hg)


def _seg_blocks(e, h):
    return [e[i * BLOCK:(i + 1) * BLOCK, h * BLOCK:(h + 1) * BLOCK] for i in range(N_SEG)]


def _rope(x, cos, sin, first_half):
    partner = jnp.where(first_half, -pltpu.roll(x, 96, 1), pltpu.roll(x, 32, 1))
    return x * cos + partner * sin


def _rope_t(g, cos, sin, first_half):
    u = g * sin
    partner = jnp.where(first_half, pltpu.roll(u, 96, 1), -pltpu.roll(u, 32, 1))
    return g * cos + partner


def _low_half(x):
    return lax.broadcasted_iota(jnp.int32, x.shape, 1) < HEAD_DIM


def _both_halves(x, g):
    sw = _swap_halves(x)
    return jnp.where(_low_half(x), x, sw) if g == 0 else jnp.where(_low_half(x), sw, x)


def _att_scores(qa, qb, kc, kp, km, g, own4, band4, meta4):
    low = _low_half(qa)
    q4 = jnp.concatenate([jnp.where(low, qa, 0.0), jnp.where(low, 0.0, qa),
                          jnp.where(low, qb, 0.0), jnp.where(low, 0.0, qb)], axis=0)
    scale = HEAD_DIM ** -0.5
    neg = jnp.finfo(F32).min
    s = jnp.where(own4, _mm(_both_halves(kc, g), q4, "nt"), _mm(_both_halves(kp, g), q4, "nt"))
    return (jnp.where(band4, s * scale, neg), jnp.where(meta4, _mm(_both_halves(km, g), q4, "nt") * scale, neg))


def _att_probs(s, sm, sinkrow):
    mx = jnp.maximum(jnp.maximum(jnp.max(s, axis=0, keepdims=True), jnp.max(sm, axis=0, keepdims=True)), sinkrow)
    p, pm, ps = jnp.exp(s - mx), jnp.exp(sm - mx), jnp.exp(sinkrow - mx)
    inv = 1.0 / (jnp.sum(p, axis=0, keepdims=True) + jnp.sum(pm, axis=0, keepdims=True) + ps)
    return p * inv, pm * inv, ps * inv


def _att_probs_bwd(p, pm, ps, dp, dpm):
    r = jnp.sum(p * dp, axis=0, keepdims=True) + jnp.sum(pm * dpm, axis=0, keepdims=True)
    return p * (dp - r), pm * (dpm - r), -ps * r


def _att_values(p, pm, vc, vp, vm, g, own4):
    o4 = (_mm(jnp.where(own4, p, 0.0), _both_halves(vc, g), "tn") + _mm(jnp.where(own4, 0.0, p), _both_halves(vp, g), "tn")
          + _mm(pm, _both_halves(vm, g), "tn"))
    tiles = []
    for j in range(2):
        upper = o4[(2 * j) * BLOCK:(2 * j + 1) * BLOCK]
        tiles.append(jnp.where(_low_half(upper), upper, o4[(2 * j + 1) * BLOCK:(2 * j + 2) * BLOCK]))
    return tiles


def _att_masks(blk_idx):
    kidx = lax.broadcasted_iota(jnp.int32, (BLOCK, BLOCK), 0)
    qrow = lax.broadcasted_iota(jnp.int32, (BLOCK, BLOCK), 1)
    own_side = kidx <= qrow
    pos_own = blk_idx * BLOCK + kidx - PAD
    ok_band = (own_side & (pos_own >= N_META)) | (~own_side & (pos_own - BLOCK >= N_META) & (blk_idx >= 1))
    qpos = blk_idx * BLOCK + lax.broadcasted_iota(jnp.int32, (N_META, BLOCK), 1) - PAD
    ok_meta = lax.broadcasted_iota(jnp.int32, (N_META, BLOCK), 0) <= qpos
    return [jnp.concatenate([m] * 4, axis=1) for m in (own_side, ok_band, ok_meta)]


def _token_streams(tr, tile_of=lambda i: i):
    k = tr // BLOCK
    return [pl.BlockSpec((BLOCK, D_MODEL), lambda i, j=j: (jnp.maximum(k * tile_of(i) - 1 + j, 0), 0))
            for j in range(k)]


def _embed_ln(x, meta_shard, w_in_shard, g0, b0):
    p = x.shape[0] + BLOCK
    tr = _row_tile(p, 640)
    k = tr // BLOCK
    nt = p // tr
    tile_of = lambda s: (s + 1) % nt
    shards = [meta_shard, w_in_shard]
    c_in, c_out, c_shapes, c_sems = _comm_specs(shards, N_DEV)

    def body(*refs):
        g_ref, b_ref = refs[k:k + 2]
        h_ref, hb_ref, xh_ref, rs_ref = refs[k + 4:k + 8]
        out_refs = refs[k + 8:k + 10]
        lead_ref, meta_ref = refs[k + 10:k + 12]
        starts, passes, waits = _gather_behind(refs[k + 2:k + 4], out_refs, refs[k + 12:], [False, False])
        s = pl.program_id(0)
        t = tile_of(s)

        @pl.when(s == 0)
        def _():
            lead_ref[...] = jnp.zeros_like(lead_ref)
            for start in starts:
                start()

        @pl.when(s == nt - 1)
        def _():
            for step in passes + waits:
                step()
            pltpu.sync_copy(out_refs[0], meta_ref)
            for d in range(N_DEV):
                lead_ref[PAD:BLOCK, d * BLOCK:(d + 1) * BLOCK] = meta_ref[d]

        first = jnp.where(t == 0, lead_ref[...], refs[0][...])
        xhat, rstd = _ln_stats(jnp.concatenate([first] + [r[...] for r in refs[1:k]], axis=0))
        row = t * tr + lax.broadcasted_iota(jnp.int32, (tr, 1), 0)
        h = jnp.where(row >= PAD, xhat * g_ref[...] + b_ref[...], 0.0)
        h_ref[...] = h
        hb_ref[...] = h.astype(BF16)
        xh_ref[...] = xhat
        rs_ref[...] = rstd

    vec = pl.BlockSpec((1, D_MODEL), lambda s: (0, 0))
    rowsp = pl.BlockSpec((tr, D_MODEL), lambda s: (tile_of(s), 0))
    return pl.pallas_call(
        body, name="embed_ln", grid=(nt,),
        in_specs=_token_streams(tr, tile_of) + [vec, vec] + c_in,
        out_specs=[rowsp, rowsp, rowsp, pl.BlockSpec((tr, 1), lambda s: (tile_of(s), 0))] + c_out,
        out_shape=[jax.ShapeDtypeStruct((p, D_MODEL), F32), jax.ShapeDtypeStruct((p, D_MODEL), BF16),
                   jax.ShapeDtypeStruct((p, D_MODEL), F32), jax.ShapeDtypeStruct((p, 1), F32)] + c_shapes,
        scratch_shapes=[pltpu.VMEM((BLOCK, D_MODEL), F32), pltpu.VMEM((N_DEV, N_META, BLOCK), F32)] + c_sems,
        compiler_params=_cparams(("arbitrary",)),
    )(*([x] * k), g0, b0, *shards)


def _rope_tables(p):
    pos = (np.arange(p, dtype=np.int32) - PAD).astype(np.float32)
    half = HEAD_DIM // 2
    inv = np.float32(ROPE_THETA) ** (-np.arange(half, dtype=np.float32) / np.float32(half))
    ang = pos[:, None] * np.tile(inv.astype(np.float32), BLOCK // half)[None, :]
    return jnp.asarray(np.cos(ang), F32), jnp.asarray(np.sin(ang), F32)


def _att_sinkrows(sink_ref):
    lanehead = lax.broadcasted_iota(jnp.int32, (1, 4 * BLOCK), 1) // BLOCK
    rows = []
    for g in range(2):
        row = jnp.zeros((1, 4 * BLOCK), F32)
        for j in range(4):
            row = jnp.where(lanehead == j, sink_ref[0, 4 * g + j], row)
        rows.append(row)
    return rows


def _first_half(rows):
    return (lax.broadcasted_iota(jnp.int32, (rows, BLOCK), 1) % HEAD_DIM) < (HEAD_DIM // 2)


def _att_load(qkv_ref, cos_ref, sin_ref, with_q):
    cos, sin, fh = cos_ref[...], sin_ref[...], _first_half(BLOCK)
    qs = [_rope(qkv_ref[:, j * BLOCK:(j + 1) * BLOCK], cos, sin, fh) for j in range(4)] if with_q else None
    k = _rope(qkv_ref[:, ATT_QW:ATT_QW + ATT_KVW], cos, sin, fh)
    v = qkv_ref[:, ATT_QW + ATT_KVW:ATT_QW + 2 * ATT_KVW]
    return qs, k, v


def _att_load_meta(qkv_ref, cos_ref, sin_ref):
    k = _rope(qkv_ref[PAD:BLOCK, ATT_QW:ATT_QW + ATT_KVW], cos_ref[PAD:BLOCK, :], sin_ref[PAD:BLOCK, :],
              _first_half(N_META))
    return k, qkv_ref[PAD:BLOCK, ATT_QW + ATT_KVW:ATT_QW + 2 * ATT_KVW]


def _att_specs(blk):
    w = ATT_QW + 2 * ATT_KVW
    cur = lambda width: pl.BlockSpec((BLOCK, width), lambda i: (blk(i), 0))
    prev = lambda width: pl.BlockSpec((BLOCK, width), lambda i: (jnp.maximum(blk(i) - 1, 0), 0))
    meta = lambda width: pl.BlockSpec((BLOCK, width), lambda i: (0, 0))
    return [cur(w), prev(w), meta(w), cur(BLOCK), cur(BLOCK), prev(BLOCK), prev(BLOCK), meta(BLOCK), meta(BLOCK),
            pl.BlockSpec(memory_space=pltpu.SMEM)]


_FLIPS = [(dx, dy, dc) for dx in (0, 1) for dy in (0, 1) for dc in (0, 1)][1:]
N_PEERS = len(_FLIPS)


def _place():
    return lax.axis_index("x"), lax.axis_index("y"), lax.axis_index("c")


def _peer(place, flip):
    return tuple(1 - p if f else p for p, f in zip(place, flip))


def _slot(place, swapped):
    x, y, c = place
    return 4 * y + 2 * x + c if swapped else 4 * x + 2 * y + c


def _comm_specs(arrs, out_lead):
    n = len(arrs)
    outs = [jax.ShapeDtypeStruct((out_lead,) + a.shape[-2:], a.dtype) for a in arrs]
    sems = [pltpu.SemaphoreType.DMA((n, N_PEERS)), pltpu.SemaphoreType.DMA((n, N_PEERS)), pltpu.SemaphoreType.DMA((n,))]
    return [pl.BlockSpec(memory_space=pl.ANY)] * n, [pl.BlockSpec(memory_space=pl.ANY)] * n, outs, sems


def _gather_behind(shard_refs, out_refs, sems, swapped):
    send_sems, recv_sems, local_sems = sems
    x, y, c = _place()
    me, sibling = (x, y, c), (x, y, 1 - c)
    chips = [(1 - x, y), (x, 1 - y), (1 - x, 1 - y)]
    starts, passes, waits = [], [], []
    for w, (s, o) in enumerate(zip(shard_refs, out_refs)):
        def copy(k, block, to, from_shard=False, w=w, s=s, o=o):
            rows = o.at[_slot(block, swapped[w])]
            return pltpu.make_async_remote_copy(
                src_ref=s if from_shard else rows, dst_ref=rows, send_sem=send_sems.at[w, k],
                recv_sem=recv_sems.at[w, k], device_id=to, device_id_type=MESH)

        own = pltpu.make_async_copy(s, o.at[_slot(me, swapped[w])], local_sems.at[w])
        first = [copy(0, me, sibling, True)] + [copy(1 + j, me, (*chip, c), True) for j, chip in enumerate(chips)]
        handed = [copy(4 + j, (*chip, c), sibling) for j, chip in enumerate(chips)]
        starts += [own.start] + [cp.start for cp in first]
        for j, chip in enumerate(chips):
            passes += [copy(1 + j, (*chip, c), me).wait_recv, handed[j].start]
        waits.append(copy(0, sibling, me).wait_recv)
        waits += [copy(4 + j, (*chip, 1 - c), me).wait_recv for j, chip in enumerate(chips)]
        waits += [cp.wait_send for cp in first + handed] + [own.wait]
    return starts, passes, waits


def _scatter_behind(part_refs, recv_refs, sems, swapped):
    send_sems, recv_sems, _ = sems
    place = _place()
    starts, waits = [], []
    for w, (p, o) in enumerate(zip(part_refs, recv_refs)):
        for r, flip in enumerate(_FLIPS):
            peer = _peer(place, flip)
            cp = pltpu.make_async_remote_copy(
                src_ref=p.at[_slot(peer, swapped[w])], dst_ref=o.at[r], send_sem=send_sems.at[w, r],
                recv_sem=recv_sems.at[w, r], device_id=peer, device_id_type=MESH)
            starts.append(cp.start)
            waits += [cp.wait_recv, cp.wait_send]
    return starts, waits


def _mixers_fwd(proj_hg, proj_att, lbounds, norm_g, lv, cos, sin, sinks, shards, swapped):
    p = proj_hg.shape[0]
    nb = p // BLOCK
    n = len(shards)
    c_in, c_out, c_shapes, c_sems = _comm_specs(shards, N_DEV)
    pass_step = min(nb - 1, max(1, (5 * nb) // 8))

    def body(*refs):
        x_ref, lb_ref, ng_ref, lv_ref, cur_ref, prev_ref, meta_ref, cc, sc, cp, sp, cm, sm, sink_ref = refs[:14]
        shard_refs = refs[14:14 + n]
        y_ref, o_ref, st_ref, a_ref, raw_ref, pr_ref = refs[14 + n:20 + n]
        out_refs = refs[20 + n:20 + 2 * n]
        carry_ref = refs[20 + 2 * n]
        starts, passes, waits = _gather_behind(shard_refs, out_refs, refs[21 + 2 * n:], swapped)
        c = pl.program_id(0)

        @pl.when(c == 0)
        def _():
            carry_ref[...] = jnp.zeros_like(carry_ref)
            for start in starts:
                start()

        @pl.when(c == pass_step)
        def _():
            for step in passes:
                step()

        valid = (c * BLOCK + lax.broadcasted_iota(jnp.int32, (BLOCK, 1), 0)) >= PAD
        logf, k = _hgrn_gates(x_ref[:, HG_W:2 * HG_W], lb_ref[0:1, :], lb_ref[1:2, :], valid)
        e = _split_dot(lv_ref[...], logf, "nn")
        for h in range(HG_HEADS):
            sl = lambda part: x_ref[:, part * HG_W + h * BLOCK: part * HG_W + (h + 1) * BLOCK]
            hs = slice(h * BLOCK, (h + 1) * BLOCK)
            st_in = carry_ref[h]
            st_ref[0, h] = st_in
            seg = _seg_blocks(e, h)
            a = _hgrn_scores(sl(0), k[:, hs], *seg[3:])
            a_ref[0, h] = a.astype(BF16)
            raw, st_out = _hgrn_mix(sl(0), k[:, hs], sl(2), st_in, a, *seg[:3])
            raw_ref[:, hs] = raw
            y_ref[:, hs] = _hgrn_norm(raw, sl(3), ng_ref[...]).astype(BF16)
            carry_ref[h] = st_out

        qs, kc, vc = _att_load(cur_ref, cc, sc, True)
        _, kp, vp = _att_load(prev_ref, cp, sp, False)
        km, vm = _att_load_meta(meta_ref, cm, sm)
        sinkrows = _att_sinkrows(sink_ref)
        own4, band4, meta4 = _att_masks(c)
        for g in range(2):
            s, s_meta = _att_scores(qs[2 * g], qs[2 * g + 1], kc, kp, km, g, own4, band4, meta4)
            pr, pr_meta, pr_sink = _att_probs(s, s_meta, sinkrows[g])
            pr_ref[0, g, :BLOCK, :] = pr.astype(BF16)
            pr_ref[0, g, BLOCK:BLOCK + N_META, :] = pr_meta.astype(BF16)
            pr_ref[0, g, BLOCK + N_META:, :] = jnp.broadcast_to(pr_sink, (N_META, 4 * BLOCK)).astype(BF16)
            for j, tile in enumerate(_att_values(pr, pr_meta, vc, vp, vm, g, own4)):
                o_ref[:, (2 * g + j) * BLOCK:(2 * g + j + 1) * BLOCK] = tile.astype(BF16)

        @pl.when(c == nb - 1)
        def _():
            for wait in waits:
                wait()

    return pl.pallas_call(
        body, name="mixers_fwd", grid=(nb,),
        in_specs=[pl.BlockSpec((BLOCK, 4 * HG_W), lambda c: (c, 0)), pl.BlockSpec((2, HG_W), lambda c: (0, 0)),
                  pl.BlockSpec((1, BLOCK), lambda c: (0, 0)), pl.BlockSpec(lv.shape, lambda c: (0, 0))]
        + _att_specs(lambda c: c) + c_in,
        out_specs=[pl.BlockSpec((BLOCK, HG_W), lambda c: (c, 0)), pl.BlockSpec((BLOCK, ATT_QW), lambda c: (c, 0)),
                   pl.BlockSpec((1, HG_HEADS, BLOCK, BLOCK), lambda c: (c, 0, 0, 0)),
                   pl.BlockSpec((1, HG_HEADS, BLOCK, BLOCK), lambda c: (c, 0, 0, 0)),
                   pl.BlockSpec((BLOCK, HG_W), lambda c: (c, 0)),
                   pl.BlockSpec((1, 2, ATT_KEYS, 4 * BLOCK), lambda c: (c, 0, 0, 0))] + c_out,
        out_shape=[jax.ShapeDtypeStruct((p, HG_W), BF16), jax.ShapeDtypeStruct((p, ATT_QW), BF16),
                   jax.ShapeDtypeStruct((nb, HG_HEADS, BLOCK, BLOCK), F32),
                   jax.ShapeDtypeStruct((nb, HG_HEADS, BLOCK, BLOCK), BF16),
                   jax.ShapeDtypeStruct((p, HG_W), F32),
                   jax.ShapeDtypeStruct((nb, 2, ATT_KEYS, 4 * BLOCK), BF16)] + c_shapes,
        scratch_shapes=[pltpu.VMEM((HG_HEADS, BLOCK, BLOCK), F32)] + c_sems,
        compiler_params=_cparams(("arbitrary",)),
    )(proj_hg, lbounds, norm_g, lv, proj_att, proj_att, proj_att, cos, sin, cos, sin, cos, sin, sinks, *shards)


def _tile(rows, preferred):
    return preferred if rows % preferred == 0 else _row_tile(rows, preferred)


def _branch_mix(yh, oa, gates, w_bh, w_ba):
    y_hg = _dot(yh, w_bh, "nn")
    y_att = _dot(oa, w_ba, "nn")
    s1 = jax.nn.sigmoid(gates[:, :D_MODEL].astype(F32))
    s2 = jax.nn.sigmoid(gates[:, D_MODEL:].astype(F32))
    return s1 * y_hg + s2 * y_att, y_hg, y_att, s1, s2


def _mix_out_ln1(yh, oa, gates, h0, w_bh, w_ba, w_out, g1, b1):
    p = yh.shape[0]
    tr = _tile(p, 320)

    def body(yh_ref, oa_ref, g_ref, h0_ref, wbh_ref, wba_ref, wo_ref, g1_ref, b1_ref,
             mix_ref, h1_ref, h1b_ref, xh_ref, rs_ref):
        mixin = _branch_mix(yh_ref[...], oa_ref[...], g_ref[...], wbh_ref[...], wba_ref[...])[0]
        mix_ref[...] = mixin.astype(BF16)
        xhat, rstd = _ln_stats(ALPHA * h0_ref[...] + _dot(mixin, wo_ref[...], "nn"))
        h1 = xhat * g1_ref[...] + b1_ref[...]
        h1_ref[...] = h1
        h1b_ref[...] = h1.astype(BF16)
        xh_ref[...] = xhat
        rs_ref[...] = rstd

    row = lambda w: pl.BlockSpec((tr, w), lambda i: (i, 0))
    const = lambda a: pl.BlockSpec(a.shape, lambda i: (0, 0))
    return pl.pallas_call(
        body, name="mix_out_ln1", grid=(p // tr,),
        in_specs=[row(HG_W), row(ATT_QW), row(2 * D_MODEL), row(D_MODEL), const(w_bh), const(w_ba), const(w_out),
                  const(g1), const(b1)],
        out_specs=[row(D_MODEL), row(D_MODEL), row(D_MODEL), row(D_MODEL), row(1)],
        out_shape=[jax.ShapeDtypeStruct((p, D_MODEL), BF16), jax.ShapeDtypeStruct((p, D_MODEL), F32),
                   jax.ShapeDtypeStruct((p, D_MODEL), BF16), jax.ShapeDtypeStruct((p, D_MODEL), F32),
                   jax.ShapeDtypeStruct((p, 1), F32)],
        compiler_params=_cparams(("arbitrary",)),
    )(yh, oa, gates, h0, w_bh, w_ba, w_out, g1, b1)


FF_T = D_FF // 2


def _ffn_in_swiglu(h1, w_fi):
    p = h1.shape[0]
    tm = _row_tile(p, 640)

    def body(h_ref, w_ref, au_ref, s_ref):
        au = _dot(h_ref[...], w_ref[...], "nn")
        au_ref[...] = au.astype(BF16)
        s_ref[...] = (jax.nn.silu(au[:, :FF_T]) * au[:, FF_T:]).astype(BF16)

    return pl.pallas_call(
        body, name="ffn_in_swiglu", grid=(D_FF // FF_T, p // tm),
        in_specs=[pl.BlockSpec((tm, D_MODEL), lambda j, i: (i, 0)), pl.BlockSpec((D_MODEL, 2 * FF_T), lambda j, i: (0, j))],
        out_specs=[pl.BlockSpec((tm, 2 * FF_T), lambda j, i: (i, j)), pl.BlockSpec((tm, FF_T), lambda j, i: (i, j))],
        out_shape=[jax.ShapeDtypeStruct((p, 2 * D_FF), BF16), jax.ShapeDtypeStruct((p, D_FF), BF16)],
        compiler_params=_cparams(("arbitrary", "arbitrary")),
    )(h1, w_fi)


def _ffn_out_loss(s, w_fo, h1, g2, b2, target):
    p = h1.shape[0]
    tr = _row_tile(p, 640)
    k = tr // BLOCK

    def body(*refs):
        s_ref, w_ref, h_ref, g_ref, b_ref = refs[:5]
        dr_ref, loss_ref, dg_ref, db_ref = refs[5 + k:]
        i = pl.program_id(0)
        xhat, rstd = _ln_stats(ALPHA * h_ref[...] + _dot(s_ref[...], w_ref[...], "nn"))
        y = xhat * g_ref[...] + b_ref[...]
        row = i * tr + lax.broadcasted_iota(jnp.int32, (tr, 1), 0)
        tgt = jnp.concatenate([r[...] for r in refs[5:5 + k]], axis=0)
        err = jnp.where(row >= BLOCK, y - tgt, 0.0)
        dr, dg, db = _ln_bwd(err * (1.0 / D_MODEL), xhat, rstd, g_ref[...])
        dr_ref[...] = dr
        e2 = jnp.sum(err * err, axis=0, keepdims=True)
        part = e2[:, 0:BLOCK]
        for j in range(1, D_MODEL // BLOCK):
            part = part + e2[:, j * BLOCK:(j + 1) * BLOCK]
        part = part * (0.5 / D_MODEL)

        @pl.when(i == 0)
        def _():
            loss_ref[...] = part
            dg_ref[...] = dg
            db_ref[...] = db

        @pl.when(i > 0)
        def _():
            loss_ref[...] += part
            dg_ref[...] += dg
            db_ref[...] += db

    vec = pl.BlockSpec((1, D_MODEL), lambda i: (0, 0))
    rowsp = pl.BlockSpec((tr, D_MODEL), lambda i: (i, 0))
    return pl.pallas_call(
        body, name="ffn_out_loss", grid=(p // tr,),
        in_specs=[pl.BlockSpec((tr, D_FF), lambda i: (i, 0)), pl.BlockSpec((D_FF, D_MODEL), lambda i: (0, 0)),
                  rowsp, vec, vec] + _token_streams(tr),
        out_specs=[rowsp, pl.BlockSpec((1, BLOCK), lambda i: (0, 0)), vec, vec],
        out_shape=[jax.ShapeDtypeStruct((p, D_MODEL), F32), jax.ShapeDtypeStruct((1, BLOCK), F32),
                   jax.ShapeDtypeStruct((1, D_MODEL), F32), jax.ShapeDtypeStruct((1, D_MODEL), F32)],
        compiler_params=_cparams(("arbitrary",)),
    )(s, w_fo, h1, g2, b2, *([target] * k))


def _d_ffn_hidden(dr2, w_fo, au):
    p = au.shape[0]
    tm = _row_tile(p, 640)

    def body(d_ref, w_ref, au_ref, o_ref):
        ds = _dot(d_ref[...], w_ref[...], "nt")
        _, vjp = jax.vjp(lambda a, u: jax.nn.silu(a) * u, au_ref[:, :FF_T].astype(F32), au_ref[:, FF_T:].astype(F32))
        da, du = vjp(ds)
        o_ref[:, :FF_T] = da.astype(BF16)
        o_ref[:, FF_T:] = du.astype(BF16)

    return pl.pallas_call(
        body, name="d_ffn_hidden", grid=(D_FF // FF_T, p // tm),
        in_specs=[pl.BlockSpec((tm, D_MODEL), lambda j, i: (i, 0)), pl.BlockSpec((FF_T, D_MODEL), lambda j, i: (j, 0)),
                  pl.BlockSpec((tm, 2 * FF_T), lambda j, i: (i, j))],
        out_specs=pl.BlockSpec((tm, 2 * FF_T), lambda j, i: (i, j)),
        out_shape=jax.ShapeDtypeStruct((p, 2 * D_FF), BF16), compiler_params=_cparams(("arbitrary", "arbitrary")),
    )(dr2, w_fo, au)


def _ln1_mix_bwd(dr2, dh1_ffn, xhat1, rstd1, g1, yh, oa, gates, w_bh, w_ba, w_out):
    p = yh.shape[0]
    tr = _tile(p, 320)

    def body(a_ref, b_ref, xh_ref, rs_ref, g1_ref, yh_ref, oa_ref, g_ref, wbh_ref, wba_ref, wo_ref,
             dr_ref, dyhg_ref, dyat_ref, dgt_ref, dyh_ref, doa_ref, dg_ref, db_ref):
        i = pl.program_id(0)
        dr, dg, db = _ln_bwd(ALPHA * a_ref[...] + b_ref[...], xh_ref[...], rs_ref[...], g1_ref[...])
        dr_ref[...] = dr
        d = _dot(dr, wo_ref[...], "nt")
        _, y_hg, y_att, s1, s2 = _branch_mix(yh_ref[...], oa_ref[...], g_ref[...], wbh_ref[...], wba_ref[...])
        dy_hg = d * s1
        dy_att = d * s2
        dyhg_ref[...] = dy_hg.astype(BF16)
        dyat_ref[...] = dy_att.astype(BF16)
        dgt_ref[:, :D_MODEL] = (d * y_hg * s1 * (1.0 - s1)).astype(BF16)
        dgt_ref[:, D_MODEL:] = (d * y_att * s2 * (1.0 - s2)).astype(BF16)
        dyh_ref[...] = _dot(dy_hg, wbh_ref[...], "nt")
        doa_ref[...] = _dot(dy_att, wba_ref[...], "nt")

        @pl.when(i == 0)
        def _():
            dg_ref[...] = dg
            db_ref[...] = db

        @pl.when(i > 0)
        def _():
            dg_ref[...] += dg
            db_ref[...] += db

    row = lambda w: pl.BlockSpec((tr, w), lambda i: (i, 0))
    const = lambda a: pl.BlockSpec(a.shape, lambda i: (0, 0))
    vec = pl.BlockSpec((1, D_MODEL), lambda i: (0, 0))
    return pl.pallas_call(
        body, name="ln1_mix_bwd", grid=(p // tr,),
        in_specs=[row(D_MODEL), row(D_MODEL), row(D_MODEL), row(1), vec, row(HG_W), row(ATT_QW), row(2 * D_MODEL),
                  const(w_bh), const(w_ba), const(w_out)],
        out_specs=[row(D_MODEL), row(D_MODEL), row(D_MODEL), row(2 * D_MODEL), row(HG_W), row(ATT_QW), vec, vec],
        out_shape=[jax.ShapeDtypeStruct((p, D_MODEL), F32), jax.ShapeDtypeStruct((p, D_MODEL), BF16),
                   jax.ShapeDtypeStruct((p, D_MODEL), BF16), jax.ShapeDtypeStruct((p, 2 * D_MODEL), BF16),
                   jax.ShapeDtypeStruct((p, HG_W), F32), jax.ShapeDtypeStruct((p, ATT_QW), F32),
                   jax.ShapeDtypeStruct((1, D_MODEL), F32), jax.ShapeDtypeStruct((1, D_MODEL), F32)],
        compiler_params=_cparams(("arbitrary",)),
    )(dr2, dh1_ffn, xhat1, rstd1, g1, yh, oa, gates, w_bh, w_ba, w_out)


MIX_W = 4 * HG_W + ATT_QW + 2 * ATT_KVW
ATT_KEYS = BLOCK + 2 * N_META


def _mixers_bwd(proj_hg, proj_att, lbounds, norm_g, lv, states, scores, raw, probs, cos, sin, sinks, dyh, doa,
                parts, swapped):
    p = proj_hg.shape[0]
    nb = p // BLOCK
    n = len(parts)
    kvw = 2 * ATT_KVW
    rev = lambda s: nb - 1 - s
    c_in, c_out, c_shapes, c_sems = _comm_specs(parts, N_PEERS)

    def body(*refs):
        (x_ref, lb_ref, ng_ref, lv_ref, st_ref, a_ref, raw_ref, pr_ref, cur_ref, prev_ref, meta_ref, cc, sc, cp, sp,
         cm, sm, sink_ref, dy_ref, do_ref) = refs[:20]
        part_refs = refs[20:20 + n]
        dx_ref, dlb_ref, dng_ref, dsink_ref = refs[20 + n:24 + n]
        recv_refs = refs[24 + n:24 + 2 * n]
        dcarry_ref, dkv_next_ref, dkv_meta_ref = refs[24 + 2 * n:27 + 2 * n]
        starts, waits = _scatter_behind(part_refs, recv_refs, refs[27 + 2 * n:], swapped)
        step = pl.program_id(0)
        c = rev(step)

        @pl.when(step == 0)
        def _():
            dcarry_ref[...] = jnp.zeros_like(dcarry_ref)
            dkv_next_ref[...] = jnp.zeros_like(dkv_next_ref)
            dkv_meta_ref[...] = jnp.zeros_like(dkv_meta_ref)
            dlb_ref[...] = jnp.zeros_like(dlb_ref)
            dng_ref[...] = jnp.zeros_like(dng_ref)
            dsink_ref[...] = jnp.zeros_like(dsink_ref)
            for start in starts:
                start()

        fh = _first_half(BLOCK)
        qs, kc, vc = _att_load(cur_ref, cc, sc, True)
        _, kp, vp = _att_load(prev_ref, cp, sp, False)
        km, vm = _att_load_meta(meta_ref, cm, sm)
        own4, band4, meta4 = _att_masks(c)
        att0 = 4 * HG_W
        dkm = dkp = dkc = dvm = dvp = dvc = 0.0
        dsinkrows = []
        for g in range(2):
            pr = pr_ref[0, g, :BLOCK, :].astype(F32)
            pr_meta = pr_ref[0, g, BLOCK:BLOCK + N_META, :].astype(F32)
            pr_sink = jnp.max(pr_ref[0, g, BLOCK + N_META:, :].astype(F32), axis=0, keepdims=True)
            _, values_vjp = jax.vjp(lambda *a, g=g: _att_values(*a, g, own4), pr, pr_meta, vc, vp, vm)
            dpr, dpr_meta, dvc_g, dvp_g, dvm_g = values_vjp(
                [do_ref[:, (2 * g + j) * BLOCK:(2 * g + j + 1) * BLOCK] for j in range(2)])
            ds, ds_meta, dsinkrow = _att_probs_bwd(pr, pr_meta, pr_sink, dpr, dpr_meta)
            _, scores_vjp = jax.vjp(lambda *a, g=g: _att_scores(*a, g, own4, band4, meta4),
                                    qs[2 * g], qs[2 * g + 1], kc, kp, km)
            dqa, dqb, dkc_g, dkp_g, dkm_g = scores_vjp((ds, ds_meta))
            for j, dq in enumerate((dqa, dqb)):
                dx_ref[:, att0 + (2 * g + j) * BLOCK:att0 + (2 * g + j + 1) * BLOCK] = _rope_t(
                    dq, cc[...], sc[...], fh).astype(BF16)
            dkm, dkp, dkc = dkm + dkm_g, dkp + dkp_g, dkc + dkc_g
            dvm, dvp, dvc = dvm + dvm_g, dvp + dvp_g, dvc + dvc_g
            dsinkrows.append(dsinkrow)
        ds0, ds1 = dsinkrows
        dkv_meta_ref[:, :BLOCK] += _rope_t(dkm, cm[PAD:BLOCK, :], sm[PAD:BLOCK, :], _first_half(N_META))
        dkv_meta_ref[:, BLOCK:] += dvm
        last = jnp.where(c == 0, 1.0, 0.0)
        to_meta_rows = lambda m: jnp.concatenate([jnp.zeros((PAD, BLOCK), F32), last * m], axis=0)
        dk = _rope_t(dkc, cc[...], sc[...], fh) + dkv_next_ref[:, :BLOCK] + to_meta_rows(dkv_meta_ref[:, :BLOCK])
        dv = dvc + dkv_next_ref[:, BLOCK:] + to_meta_rows(dkv_meta_ref[:, BLOCK:])
        dx_ref[:, att0 + ATT_QW:att0 + ATT_QW + ATT_KVW] = dk.astype(BF16)
        dx_ref[:, att0 + ATT_QW + ATT_KVW:] = dv.astype(BF16)
        dkv_next_ref[:, :BLOCK] = _rope_t(dkp, cp[...], sp[...], fh)
        dkv_next_ref[:, BLOCK:] = dvp
        sink_rows = []
        for dsg in (ds0, ds1):
            for j in range(4):
                tot = jnp.sum(dsg[:, j * BLOCK:(j + 1) * BLOCK], axis=1, keepdims=True)
                sink_rows.append(jnp.broadcast_to(tot, (1, BLOCK)))
        dsink_ref[...] += jnp.concatenate(sink_rows, axis=0)

        valid = (c * BLOCK + lax.broadcasted_iota(jnp.int32, (BLOCK, 1), 0)) >= PAD
        (logf, k), gates_vjp = jax.vjp(lambda hf, a0, a1: _hgrn_gates(hf, a0, a1, valid),
                                       x_ref[:, HG_W:2 * HG_W], lb_ref[0:1, :], lb_ref[1:2, :])
        lvv = lv_ref[...]
        e = _split_dot(lvv, logf, "nn")
        dng = jnp.zeros((1, BLOCK), F32)
        dk, dseg = [], []
        for h in range(HG_HEADS):
            sl = lambda part: x_ref[:, part * HG_W + h * BLOCK: part * HG_W + (h + 1) * BLOCK]
            hs = slice(h * BLOCK, (h + 1) * BLOCK)
            seg = _seg_blocks(e, h)
            _, norm_vjp = jax.vjp(_hgrn_norm, raw_ref[:, hs], sl(3), ng_ref[...])
            draw, dhg, dngh = norm_vjp(dy_ref[:, hs])
            _, mix_vjp = jax.vjp(_hgrn_mix, sl(0), k[:, hs], sl(2), st_ref[0, h], a_ref[0, h].astype(F32), *seg[:3])
            dhq, dkh, dhi, dst, da, *dseg_mix = mix_vjp((draw, dcarry_ref[h]))
            _, scores_vjp = jax.vjp(_hgrn_scores, sl(0), k[:, hs], *seg[3:])
            dhq2, dkh2, *dseg_lvl = scores_vjp(da)
            for part, val in ((0, dhq + dhq2), (2, dhi), (3, dhg)):
                dx_ref[:, part * HG_W + h * BLOCK: part * HG_W + (h + 1) * BLOCK] = val.astype(BF16)
            dk.append(dkh + dkh2)
            dseg.append(jnp.concatenate(dseg_mix + dseg_lvl, axis=0))
            dng = dng + dngh
            dcarry_ref[h] = dst
        dlogf = _split_dot(lvv, jnp.concatenate(dseg, axis=1), "tn")
        dhf, da0, da1 = gates_vjp((dlogf, jnp.concatenate(dk, axis=1)))
        dx_ref[:, HG_W:2 * HG_W] = dhf.astype(BF16)
        dlb_ref[0:1, :] += da0
        dlb_ref[1:2, :] += da1
        dng_ref[...] += dng

        @pl.when(step == nb - 1)
        def _():
            for wait in waits:
                wait()

    const = lambda shape: pl.BlockSpec(shape, lambda s: (0,) * len(shape))
    per_head = pl.BlockSpec((1, HG_HEADS, BLOCK, BLOCK), lambda s: (rev(s), 0, 0, 0))
    return pl.pallas_call(
        body, name="mixers_bwd", grid=(nb,),
        in_specs=[pl.BlockSpec((BLOCK, 4 * HG_W), lambda s: (rev(s), 0)), const((2, HG_W)), const((1, BLOCK)),
                  const(lv.shape), per_head, per_head, pl.BlockSpec((BLOCK, HG_W), lambda s: (rev(s), 0)),
                  pl.BlockSpec((1, 2, ATT_KEYS, 4 * BLOCK), lambda s: (rev(s), 0, 0, 0))]
        + _att_specs(rev)
        + [pl.BlockSpec((BLOCK, HG_W), lambda s: (rev(s), 0)), pl.BlockSpec((BLOCK, ATT_QW), lambda s: (rev(s), 0))]
        + c_in,
        out_specs=[pl.BlockSpec((BLOCK, MIX_W), lambda s: (rev(s), 0)), const((2, HG_W)), const((1, BLOCK)),
                   const((ATT_HEADS, BLOCK))] + c_out,
        out_shape=[jax.ShapeDtypeStruct((p, MIX_W), BF16), jax.ShapeDtypeStruct((2, HG_W), F32),
                   jax.ShapeDtypeStruct((1, BLOCK), F32), jax.ShapeDtypeStruct((ATT_HEADS, BLOCK), F32)] + c_shapes,
        scratch_shapes=[pltpu.VMEM((HG_HEADS, BLOCK, BLOCK), F32), pltpu.VMEM((BLOCK, kvw), F32),
                        pltpu.VMEM((N_META, kvw), F32)] + c_sems,
        compiler_params=_cparams(("arbitrary",)),
    )(proj_hg, lbounds, norm_g, lv, states, scores, raw, probs, proj_att, proj_att, proj_att, cos, sin, cos, sin,
      cos, sin, sinks, dyh, doa, *parts)


_HBM = pl.BlockSpec(memory_space=pltpu.HBM)
_SEM = pl.BlockSpec(memory_space=pltpu.SEMAPHORE)
_ORDERED_BY_DATA = pltpu.CompilerParams(has_side_effects=pltpu.SideEffectType.DATAFLOW_SIDE_EFFECTING)


def _exchange_copies(part_ref, land_ref, send_sems, recv_sems):
    place = _place()
    return [pltpu.make_async_remote_copy(
        src_ref=part_ref.at[_slot(_peer(place, flip), False)], dst_ref=land_ref.at[r], send_sem=send_sems.at[r],
        recv_sem=recv_sems.at[r], device_id=_peer(place, flip), device_id_type=MESH) for r, flip in enumerate(_FLIPS)]


def _exchange_start(parts, name):
    def body(part_ref, land_ref, send_sems, recv_sems, part_thru, land_thru, token):
        for cp in _exchange_copies(part_ref, land_ref, send_sems, recv_sems):
            cp.start()
        token[...] = jnp.zeros_like(token)

    land = (N_PEERS,) + parts.shape[1:]
    return pl.pallas_call(
        body, name=name,
        out_shape=(pltpu.SemaphoreType.DMA((N_PEERS,)), pltpu.SemaphoreType.DMA((N_PEERS,)),
                   pltpu.HBM(parts.shape, parts.dtype), pltpu.HBM(land, parts.dtype), jax.ShapeDtypeStruct((8, BLOCK), F32)),
        in_specs=(_HBM, _HBM), out_specs=(_SEM, _SEM, _HBM, _HBM, pl.BlockSpec(memory_space=pltpu.VMEM)),
        input_output_aliases={0: 2, 1: 3}, compiler_params=_ORDERED_BY_DATA,
    )(pltpu.with_memory_space_constraint(parts, pltpu.HBM),
      pltpu.with_memory_space_constraint(lax.empty(land, parts.dtype), pltpu.HBM))


def _exchange_wait(send_sems, recv_sems, part_thru, land_thru, after, name):
    def body(part_ref, land_ref, send_sems, recv_sems, after_ref, part_out, land_out):
        for cp in _exchange_copies(part_ref, land_ref, send_sems, recv_sems):
            cp.wait_send()
            cp.wait_recv()

    return pl.pallas_call(
        body, name=name,
        out_shape=(pltpu.HBM(part_thru.shape, part_thru.dtype), pltpu.HBM(land_thru.shape, land_thru.dtype)),
        in_specs=(_HBM, _HBM, _SEM, _SEM, pl.BlockSpec(memory_space=pl.ANY)), out_specs=(_HBM, _HBM),
        input_output_aliases={0: 0, 1: 1}, compiler_params=_ORDERED_BY_DATA,
    )(part_thru, land_thru, send_sems, recv_sems, after)


def _embed_bwd(dmix, dgates, w_mix, w_gates, dr1, xhat0, rstd0, g0):
    p = dmix.shape[0]
    tm = _row_tile(p, 640)
    nm = p // tm

    def body(a_ref, g_ref, wa_ref, wg_ref, dr_ref, xh_ref, rs_ref, g0_ref, o_ref, dg_ref, db_ref):
        i = pl.program_id(0)
        dh0 = ALPHA * dr_ref[...] + _dot(a_ref[...], wa_ref[...], "nt") + _dot(g_ref[...], wg_ref[...], "nt")
        row = i * tm + lax.broadcasted_iota(jnp.int32, (tm, 1), 0)
        dx, dg, db = _ln_bwd(jnp.where(row >= PAD, dh0, 0.0), xh_ref[...], rs_ref[...], g0_ref[...])
        o_ref[...] = dx

        @pl.when(i == 0)
        def _():
            dg_ref[...] = dg
            db_ref[...] = db

        @pl.when(i > 0)
        def _():
            dg_ref[...] += dg
            db_ref[...] += db

    row = lambda w: pl.BlockSpec((tm, w), lambda i: (i, 0))
    const = lambda a: pl.BlockSpec(a.shape, lambda i: (0, 0))
    vec = pl.BlockSpec((1, D_MODEL), lambda i: (0, 0))
    return pl.pallas_call(
        body, name="embed_bwd", grid=(nm,),
        in_specs=[row(dmix.shape[1]), row(dgates.shape[1]), const(w_mix), const(w_gates), row(D_MODEL), row(D_MODEL),
                  row(1), vec],
        out_specs=[row(D_MODEL), vec, vec],
        out_shape=[jax.ShapeDtypeStruct((p, D_MODEL), F32), jax.ShapeDtypeStruct((1, D_MODEL), F32),
                   jax.ShapeDtypeStruct((1, D_MODEL), F32)],
        compiler_params=_cparams(("arbitrary",)),
    )(dmix, dgates, w_mix, w_gates, dr1, xhat0, rstd0, g0)


_LATE = ("w_branch_hg", "w_branch_attn", "w_out", "w_ffn_in", "w_ffn_out")
_COLUMN_SHARDED = ("meta_tokens", "w_in", "w_branch_hg", "w_branch_attn", "w_ffn_in")
_SWAPPED = ("w_ffn_in",)


def _whole(name, gathered):
    _, r, c = gathered.shape
    if name in _COLUMN_SHARDED:
        return jnp.transpose(gathered, (1, 0, 2)).reshape(r, N_DEV * c)
    return gathered.reshape(N_DEV * r, c)


def _slots(name, whole):
    r, c = whole.shape
    if name in _COLUMN_SHARDED:
        return jnp.transpose(whole.reshape(r, N_DEV, c // N_DEV), (1, 0, 2))
    return whole.reshape(N_DEV, r // N_DEV, c)


def _device_step(x, target, meta_shard, ln_emb_g, ln_emb_b, w_in_shard, lbounds, norm_g, sinks, late_shards,
                 ln1_g, ln1_b, ln2_g, ln2_b):
    s = x.shape[0]
    p = s + BLOCK
    tm = _row_tile(p, 640)
    lv = _level_stack()
    cos, sin = _rope_tables(p)
    hg_end = 4 * HG_W
    mm = functools.partial(_tiled_matmul, tm=tm)
    swapped = [n in _SWAPPED for n in _LATE]

    h0, h0b, xhat0, rstd0, _, g_win = _embed_ln(x, meta_shard, w_in_shard, ln_emb_g, ln_emb_b)
    w_in = _whole("w_in", g_win)
    proj_hg = mm(h0b, w_in[:, :hg_end], "nn", tn=hg_end, tc=D_MODEL, out_dtype=F32, name="proj_hg")
    proj_att = mm(h0b, w_in[:, hg_end:MIX_W], "nn", tn=MIX_W - hg_end, tc=D_MODEL, out_dtype=F32, name="proj_att")
    gates = mm(h0b, w_in[:, MIX_W:], "nn", tn=2 * D_MODEL, tc=D_MODEL, out_dtype=BF16, name="proj_gates")
    yh, oa, states, scores, raw, probs, *gathered = _mixers_fwd(
        proj_hg, proj_att, lbounds, norm_g, lv, cos, sin, sinks, late_shards, swapped)
    w_bh, w_ba, w_out, w_fi, w_fo = [_whole(n, g) for n, g in zip(_LATE, gathered)]
    mixin, h1, h1b, xhat1, rstd1 = _mix_out_ln1(yh, oa, gates, h0, w_bh, w_ba, w_out, ln1_g, ln1_b)
    au, sw = _ffn_in_swiglu(h1b, w_fi)
    dr2, loss_part, dg2, db2 = _ffn_out_loss(sw, w_fo, h1, ln2_g, ln2_b, target)

    mtn = functools.partial(_tiled_matmul_tn, tm=_row_tile(p, 1664), out_dtype=BF16)
    d_wfo = mtn(sw, dr2, tk=FF_T, tn=D_MODEL, name="grad_w_ffn_out")
    dau = _d_ffn_hidden(dr2, w_fo, au)
    d_wfi = mtn(h1b, dau, tk=D_MODEL, tn=FF_T, name="grad_w_ffn_in")
    dh1_ffn = mm(dau, w_fi, "nt", tn=D_MODEL, tc=D_FF, out_dtype=F32, name="d_h1_ffn")
    dr1, dy_hg, dy_att, dgates, dyh, doa, dg1, db1 = _ln1_mix_bwd(
        dr2, dh1_ffn, xhat1, rstd1, ln1_g, yh, oa, gates, w_bh, w_ba, w_out)
    d_wout = mtn(mixin, dr1, tk=D_MODEL, tn=D_MODEL, name="grad_w_out")
    d_wbh = mtn(yh, dy_hg, tk=HG_W, tn=D_MODEL, name="grad_w_branch_hg")
    d_wba = mtn(oa, dy_att, tk=ATT_QW, tn=D_MODEL, name="grad_w_branch_attn")
    late_parts = [_slots(n, g) for n, g in zip(_LATE, (d_wbh, d_wba, d_wout, d_wfi, d_wfo))]
    dmix, d_lb, d_ng, d_sink, *late_recv = _mixers_bwd(
        proj_hg, proj_att, lbounds, norm_g, lv, states, scores, raw, probs, cos, sin, sinks, dyh, doa, late_parts,
        swapped)
    d_win = jnp.concatenate([mtn(h0b, dmix, tk=D_MODEL, tn=MIX_W // 2, name="grad_w_in_mixers"),
                             mtn(h0b, dgates, tk=D_MODEL, tn=D_MODEL, name="grad_w_in_gates")], axis=1)
    *win_flight, token = _exchange_start(_slots("w_in", d_win), "w_in_grads_start")
    dxin, dg0, db0 = _embed_bwd(dmix, dgates, w_in[:, :MIX_W], w_in[:, MIX_W:], dr1, xhat0, rstd0,
                                ln_emb_g + token[0:1, 0:1])

    small = dict(ln_emb_g=dg0, ln_emb_b=db0, hg_lower_bounds=d_lb, hg_norm_g=d_ng, attn_sinks=d_sink[:, 0],
                 ln1_g=dg1, ln1_b=db1, ln2_g=dg2, ln2_b=db2)
    big = dict(zip(_LATE, zip(late_parts, late_recv)))
    return loss_part, dxin[BLOCK:], small, dxin[PAD:BLOCK], big, win_flight


def _all_gather(arrs, dtypes, name):
    n = len(arrs)

    def body(*refs):
        ins, outs, stages = refs[:n], refs[n:2 * n], refs[2 * n:3 * n]
        send_sems, recv_sems, local_sems = refs[3 * n:]
        x, y, c = _place()
        sibling = (x, y, 1 - c)
        chips = [(1 - x, y), (x, 1 - y), (1 - x, 1 - y)]
        slot = lambda px, py, pc: 4 * px + 2 * py + pc

        def copy(w, k, block, to, from_stage=False):
            return pltpu.make_async_remote_copy(
                src_ref=stages[w] if from_stage else outs[w].at[slot(*block)], dst_ref=outs[w].at[slot(*block)],
                send_sem=send_sems.at[w, k], recv_sem=recv_sems.at[w, k], device_id=to, device_id_type=MESH)

        mine, first, passed = [], [], []
        for w in range(n):
            stages[w][...] = ins[w][...].astype(dtypes[w])
            mine.append(pltpu.make_async_copy(stages[w], outs[w].at[slot(x, y, c)], local_sems.at[w]))
            mine[-1].start()
        for w in range(n):
            first.append(copy(w, 0, (x, y, c), sibling, from_stage=True))
            first += [copy(w, 1 + j, (x, y, c), (*chip, c), from_stage=True) for j, chip in enumerate(chips)]
        for cp in first:
            cp.start()
        for j, chip in enumerate(chips):
            for w in range(n):
                copy(w, 1 + j, (*chip, c), (x, y, c)).wait_recv()
                passed.append(copy(w, 4 + j, (*chip, c), sibling))
                passed[-1].start()
        for w in range(n):
            copy(w, 0, sibling, (x, y, c)).wait_recv()
            for j, chip in enumerate(chips):
                copy(w, 4 + j, (*chip, 1 - c), (x, y, c)).wait_recv()
        for cp in first + passed:
            cp.wait_send()
        for cp in mine:
            cp.wait()

    return pl.pallas_call(
        body, name=name,
        in_specs=[pl.BlockSpec(memory_space=pltpu.VMEM)] * n,
        out_specs=[pl.BlockSpec(memory_space=pl.ANY)] * n,
        out_shape=[jax.ShapeDtypeStruct((N_DEV,) + a.shape, dt) for a, dt in zip(arrs, dtypes)],
        scratch_shapes=[pltpu.VMEM(a.shape, dt) for a, dt in zip(arrs, dtypes)]
        + [pltpu.SemaphoreType.DMA((n, 7)), pltpu.SemaphoreType.DMA((n, 7)), pltpu.SemaphoreType.DMA((n,))],
        compiler_params=pltpu.CompilerParams(vmem_limit_bytes=VMEM_LIMIT_BYTES),
    )(*arrs)


def _cast_shards(arrs):
    def body(*refs):
        for src, dst in zip(refs[:len(arrs)], refs[len(arrs):]):
            dst[...] = src[...].astype(BF16)

    return pl.pallas_call(body, name="cast_shards", out_shape=[jax.ShapeDtypeStruct(a.shape, BF16) for a in arrs],
                          compiler_params=pltpu.CompilerParams(vmem_limit_bytes=VMEM_LIMIT_BYTES))(*arrs)


def _shard_rows(rows):
    return rows if rows <= 512 else 256


def _adamw_math(w, g, m, v):
    m = ADAM_B1 * m + (1.0 - ADAM_B1) * g
    v = ADAM_B2 * v + (1.0 - ADAM_B2) * (g * g)
    m_hat = m / (1.0 - ADAM_B1 ** ADAM_STEP)
    v_hat = v / (1.0 - ADAM_B2 ** ADAM_STEP)
    delta = -ADAM_LR * (m_hat / (jnp.sqrt(v_hat) + ADAM_EPS) + ADAM_WD * w)
    return delta, m, v


def _reduce_adamw(parts, recv, own_slot, w, m, v, name):
    r, cdim = w.shape
    tr = _shard_rows(r)

    def body(idx_ref, p_ref, r_ref, w_ref, m_ref, v_ref, g_out, d_out, m_out, v_out):
        g = p_ref[0].astype(F32)
        for j in range(N_PEERS):
            g = g + r_ref[j].astype(F32)
        d, mn, vn = _adamw_math(w_ref[...], g, m_ref[...], v_ref[...])
        g_out[...] = g
        d_out[...] = d
        m_out[...] = mn
        v_out[...] = vn

    flat = pl.BlockSpec((tr, cdim), lambda i, idx_ref: (i, 0))
    return pl.pallas_call(
        body, name=name,
        grid_spec=pltpu.PrefetchScalarGridSpec(
            num_scalar_prefetch=1, grid=(r // tr,),
            in_specs=[pl.BlockSpec((1, tr, cdim), lambda i, idx_ref: (idx_ref[0], i, 0)),
                      pl.BlockSpec((N_PEERS, tr, cdim), lambda i, idx_ref: (0, i, 0)), flat, flat, flat],
            out_specs=[flat] * 4),
        out_shape=[jax.ShapeDtypeStruct((r, cdim), F32)] * 4,
        compiler_params=_cparams(("arbitrary",)),
    )(own_slot, parts, recv, w, m, v)


def _adamw_plain(w, g, m, v, name):
    def body(w_ref, g_ref, m_ref, v_ref, d_out, m_out, v_out):
        d_out[...], m_out[...], v_out[...] = _adamw_math(w_ref[...], g_ref[...], m_ref[...], v_ref[...])

    return pl.pallas_call(body, name=name, out_shape=[jax.ShapeDtypeStruct(w.shape, F32)] * 3)(w, g, m, v)


_SMALL_LAYOUT = (("ln_emb_g", 8), ("ln_emb_b", 8), ("hg_lower_bounds", 8), ("hg_norm_g", 1), ("attn_sinks", 1),
                 ("ln1_g", 8), ("ln1_b", 8), ("ln2_g", 8), ("ln2_b", 8))
_META_ROW = sum(r for _, r in _SMALL_LAYOUT)
_META_ROWS = N_META * D_MODEL // BLOCK
_LOSS_ROW = _META_ROW + _META_ROWS
SMALL_ROWS = 192


def _pack_small(vals, meta=None, loss_row=None):
    rows = []
    for name, nrows in _SMALL_LAYOUT:
        flat = vals[name].reshape(-1).astype(F32)
        flat = jnp.pad(flat, (0, nrows * BLOCK - flat.shape[0]))
        rows.append(flat.reshape(nrows, BLOCK))
    rows.append(jnp.zeros((_META_ROWS, BLOCK), F32) if meta is None else meta.reshape(_META_ROWS, BLOCK))
    rows.append(jnp.zeros((1, BLOCK), F32) if loss_row is None else loss_row)
    packed = jnp.concatenate(rows, axis=0)
    return jnp.pad(packed, ((0, SMALL_ROWS - packed.shape[0]), (0, 0)))


def _unpack_small(packed, shapes):
    out, row = {}, 0
    for name, nrows in _SMALL_LAYOUT:
        size = math.prod(shapes[name])
        out[name] = packed[row:row + nrows].reshape(-1)[:size].reshape(shapes[name])
        row += nrows
    return out


def _small_reduce_adamw(gathered, w, m, v):
    def body(g_ref, w_ref, m_ref, v_ref, g_out, d_out, m_out, v_out, loss_out):
        g = g_ref[0]
        for s in range(1, N_DEV):
            g = g + g_ref[s]
        d, mn, vn = _adamw_math(w_ref[...], g, m_ref[...], v_ref[...])
        g_out[...] = g
        d_out[...] = d
        m_out[...] = mn
        v_out[...] = vn
        loss_out[...] = jnp.broadcast_to(jnp.sum(g_ref[:, _LOSS_ROW, :]), (1, BLOCK))

    shp = jax.ShapeDtypeStruct((SMALL_ROWS, BLOCK), F32)
    return pl.pallas_call(body, name="small_reduce_adamw",
                          out_shape=[shp] * 4 + [jax.ShapeDtypeStruct((1, BLOCK), F32)])(gathered, w, m, v)


_WEIGHTS = ("meta_tokens", "ln_emb_g", "ln_emb_b", "w_in", "hg_lower_bounds", "hg_norm_g", "attn_sinks",
            "w_branch_hg", "w_branch_attn", "w_out", "ln1_g", "ln1_b", "w_ffn_in", "w_ffn_out", "ln2_g", "ln2_b")


def kernel(x, meta_tokens, ln_emb_g, ln_emb_b, w_in, hg_lower_bounds, hg_norm_g, attn_sinks, w_branch_hg, w_branch_attn, w_out, ln1_g, ln1_b, w_ffn_in, w_ffn_out, ln2_g, ln2_b, loss_target, m_meta_tokens, m_ln_emb_g, m_ln_emb_b, m_w_in, m_hg_lower_bounds, m_hg_norm_g, m_attn_sinks, m_w_branch_hg, m_w_branch_attn, m_w_out, m_ln1_g, m_ln1_b, m_w_ffn_in, m_w_ffn_out, m_ln2_g, m_ln2_b, v_meta_tokens, v_ln_emb_g, v_ln_emb_b, v_w_in, v_hg_lower_bounds, v_hg_norm_g, v_attn_sinks, v_w_branch_hg, v_w_branch_attn, v_w_out, v_ln1_g, v_ln1_b, v_w_ffn_in, v_w_ffn_out, v_ln2_g, v_ln2_b):
    given = dict(locals())
    weights = {n: given[n] for n in _WEIGHTS}
    mom1 = {n: given["m_" + n] for n in _WEIGHTS}
    mom2 = {n: given["v_" + n] for n in _WEIGHTS}
    shard2d = lambda a: a.reshape(a.shape[-2:])

    w_in_shard, *late_shards = _cast_shards([shard2d(weights[n]) for n in ("w_in",) + _LATE])
    loss_part, grad_x, small_grads, meta_grad, big, win_flight = _device_step(
        x[0], loss_target[0], meta_tokens, ln_emb_g.reshape(1, -1), ln_emb_b.reshape(1, -1), w_in_shard,
        hg_lower_bounds, hg_norm_g, attn_sinks, late_shards, ln1_g, ln1_b, ln2_g, ln2_b)

    place = _place()
    out = {}

    def reduce_adamw(n, parts, recv):
        own = _slot(place, n in _SWAPPED).astype(jnp.int32).reshape(1)
        res = _reduce_adamw(parts, recv, own, shard2d(weights[n]), shard2d(mom1[n]), shard2d(mom2[n]), "adamw_" + n)
        out[n] = [r.reshape(weights[n].shape) for r in res]

    for n, (parts, recv) in big.items():
        reduce_adamw(n, parts, recv)

    small_names = [n for n, _ in _SMALL_LAYOUT]
    packed = _pack_small(small_grads, meta_grad, loss_part)
    all_small, = _all_gather([packed], [F32], "gather_small")
    res = _small_reduce_adamw(all_small, _pack_small(weights), _pack_small(mom1), _pack_small(mom2))
    shapes = {n: weights[n].shape for n in small_names}
    unpacked = [_unpack_small(r, shapes) for r in res[:4]]
    for n in small_names:
        out[n] = [u[n] for u in unpacked]
    loss = res[4][0, 0]
    meta_whole = res[0][_META_ROW:_META_ROW + _META_ROWS].reshape(N_META, N_DEV, D_MODEL // N_DEV)
    g_meta_mine = lax.dynamic_index_in_dim(meta_whole, _slot(place, False), axis=1, keepdims=False)
    out["meta_tokens"] = [g_meta_mine, *_adamw_plain(meta_tokens, g_meta_mine, m_meta_tokens, v_meta_tokens,
                                                     "adamw_meta")]

    reduce_adamw("w_in", *_exchange_wait(*win_flight, after=all_small, name="w_in_grads_wait"))

    return (loss, grad_x[None], *[out[n][0] for n in _WEIGHTS], *[out[n][1] for n in _WEIGHTS],
            *[out[n][2] for n in _WEIGHTS], *[out[n][3] for n in _WEIGHTS])
```

```python
import functools
import math

import numpy as np
import jax
import jax.numpy as jnp
from jax import lax
from jax.experimental import pallas as pl
from jax.experimental.pallas import tpu as pltpu

F32 = jnp.float32
BF16 = jnp.bfloat16

D_MODEL = 1024
N_META = 16
BLOCK = 128
PAD = BLOCK - N_META
HG_HEADS = 4
HG_W = 512
ATT_HEADS = 8
HEAD_DIM = 64
ATT_QW = 512
ATT_KVW = 128
D_FF = 2816
EPS = 1e-5
ALPHA = 2.0 ** 0.25
ROPE_THETA = 10000.0
N_DEV = 8

ADAM_LR = 0.001
ADAM_B1 = 0.9
ADAM_B2 = 0.999
ADAM_EPS = 1e-08
ADAM_WD = 0.01
ADAM_STEP = 10

VMEM_LIMIT_BYTES = 56 * 1024 * 1024
MESH = pl.DeviceIdType.MESH

_LEVELS = (64, 32, 16, 8, 4, 2, 1)


def _cparams(sem):
    return pltpu.CompilerParams(dimension_semantics=sem, vmem_limit_bytes=VMEM_LIMIT_BYTES)


def _row_tile(rows, target):
    nb = rows // BLOCK
    best = 1
    for d in range(1, nb + 1):
        if nb % d == 0 and d * BLOCK <= target:
            best = d
    return best * BLOCK


_DN = {"nn": (((1,), (0,)), ((), ())), "nt": (((1,), (1,)), ((), ())), "tn": (((0,), (0,)), ((), ()))}


def _dot(a, b, form):
    return lax.dot_general(a.astype(BF16), b.astype(BF16), _DN[form], preferred_element_type=F32)


@functools.partial(jax.custom_vjp, nondiff_argnums=(2,))
def _mm(a, b, form):
    return _dot(a, b, form)


def _mm_fwd(a, b, form):
    a, b = a.astype(BF16), b.astype(BF16)
    return _dot(a, b, form), (a, b)


def _mm_bwd(form, res, g):
    a, b = res
    if form == "nn":
        return _dot(g, b, "nt"), _dot(a, g, "tn")
    if form == "nt":
        return _dot(g, b, "nn"), _dot(g, a, "tn")
    return _dot(b, g, "nt"), _dot(a, g, "nn")


_mm.defvjp(_mm_fwd, _mm_bwd)


def _split_dot(lv, x, form):
    return lax.dot_general(lv, x.astype(BF16), _DN[form], preferred_element_type=F32)


@jax.custom_vjp
def _swap_halves(x):
    return pltpu.roll(x, 64, 1)


_swap_halves.defvjp(lambda x: (pltpu.roll(x, 64, 1), None), lambda _, g: (pltpu.roll(g, 64, 1),))


def _tiled_matmul(a, b, form, *, tm, tn, tc, out_dtype, name):
    m, c = a.shape
    n = b.shape[1] if form == "nn" else b.shape[0]
    assert m % tm == 0 and n % tn == 0 and c % tc == 0, (name, a.shape, b.shape, tm, tn, tc)
    nc = c // tc

    def body(a_ref, b_ref, o_ref, *scratch):
        if nc == 1:
            o_ref[...] = _dot(a_ref[...], b_ref[...], form).astype(out_dtype)
            return
        acc_ref, = scratch
        ci = pl.program_id(2)

        @pl.when(ci == 0)
        def _():
            acc_ref[...] = jnp.zeros_like(acc_ref)

        acc_ref[...] += _dot(a_ref[...], b_ref[...], form)

        @pl.when(ci == nc - 1)
        def _():
            o_ref[...] = acc_ref[...].astype(out_dtype)

    b_spec = (pl.BlockSpec((tc, tn), lambda j, i, k: (k, j)) if form == "nn"
              else pl.BlockSpec((tn, tc), lambda j, i, k: (j, k)))
    return pl.pallas_call(
        body, name=name, grid=(n // tn, m // tm, nc),
        in_specs=[pl.BlockSpec((tm, tc), lambda j, i, k: (i, k)), b_spec],
        out_specs=pl.BlockSpec((tm, tn), lambda j, i, k: (i, j)),
        out_shape=jax.ShapeDtypeStruct((m, n), out_dtype),
        scratch_shapes=[] if nc == 1 else [pltpu.VMEM((tm, tn), F32)],
        compiler_params=_cparams(("arbitrary", "arbitrary", "arbitrary")),
    )(a, b)


def _tiled_matmul_tn(a, b, *, tm, tk, tn, out_dtype, name):
    m, k = a.shape
    n = b.shape[1]
    assert m % tm == 0 and k % tk == 0 and n % tn == 0, (name, a.shape, b.shape, tm, tk, tn)
    nm = m // tm

    def body(a_ref, b_ref, o_ref, acc_ref):
        mi = pl.program_id(2)

        @pl.when(mi == 0)
        def _():
            acc_ref[...] = jnp.zeros_like(acc_ref)

        acc_ref[...] += _dot(a_ref[...], b_ref[...], "tn")

        @pl.when(mi == nm - 1)
        def _():
            o_ref[...] = acc_ref[...].astype(out_dtype)

    return pl.pallas_call(
        body, name=name, grid=(k // tk, n // tn, nm),
        in_specs=[pl.BlockSpec((tm, tk), lambda kk, j, i: (i, kk)), pl.BlockSpec((tm, tn), lambda kk, j, i: (i, j))],
        out_specs=pl.BlockSpec((tk, tn), lambda kk, j, i: (kk, j)),
        out_shape=jax.ShapeDtypeStruct((k, n), out_dtype),
        scratch_shapes=[pltpu.VMEM((tk, tn), F32)],
        compiler_params=_cparams(("arbitrary", "arbitrary", "arbitrary")),
    )(a, b)


def _ln_stats(r):
    mu = jnp.mean(r, axis=-1, keepdims=True)
    xc = r - mu
    var = jnp.mean(xc * xc, axis=-1, keepdims=True)
    rstd = lax.rsqrt(var + EPS)
    return xc * rstd, rstd


def _ln_bwd(dy, xhat, rstd, g):
    dxhat = dy * g
    m1 = jnp.mean(dxhat, axis=-1, keepdims=True)
    m2 = jnp.mean(dxhat * xhat, axis=-1, keepdims=True)
    dr = rstd * (dxhat - m1 - xhat * m2)
    return dr, jnp.sum(dy * xhat, axis=0, keepdims=True), jnp.sum(dy, axis=0, keepdims=True)


N_SEG = 3 + len(_LEVELS)


def _level_stack():
    t = np.arange(BLOCK)[:, None]
    r = np.arange(BLOCK)[None, :]
    mats = [r <= t, r > t, np.ones((BLOCK, BLOCK), bool)]
    for h in _LEVELS:
        same = (t // (2 * h)) == (r // (2 * h))
        up_t, up_r = (t % (2 * h)) >= h, (r % (2 * h)) >= h
        mats.append(same & ((up_t & up_r & (r <= t)) | (~up_t & ~up_r & (r > t))))
    return jnp.asarray(np.concatenate(mats, axis=0).astype(np.float32), dtype=BF16)


def _hgrn_gates(hf, a0, a1, valid):
    lb = jax.nn.sigmoid(a0 - a1)
    fg = lb + (1.0 - lb) * jax.nn.sigmoid(hf)
    return jnp.where(valid, jnp.log(fg), 0.0), jnp.where(valid, 1.0 - fg, 0.0)


def _hgrn_scores(hq, k, *levels):
    q = jax.nn.silu(hq)
    rows = lax.broadcasted_iota(jnp.int32, (BLOCK, BLOCK), 0)
    cols = lax.broadcasted_iota(jnp.int32, (BLOCK, BLOCK), 1)
    a = jnp.where(rows == cols, jnp.sum(q * k, axis=-1, keepdims=True), 0.0)
    differ = jnp.bitwise_xor(rows, cols)
    for h, lvl in zip(_LEVELS, levels):
        decay = jnp.exp(lvl)
        pair = (cols < rows) & (differ >= h) & (differ < 2 * h)
        a = a + jnp.where(pair, _mm(q * decay, k * decay, "nt"), 0.0)
    return a


def _hgrn_mix(hq, k, v, st_in, a, seg_incl, seg_after, seg_total):
    o = _mm(jax.nn.silu(hq) * jnp.exp(seg_incl), st_in, "nt") + _mm(a, v, "nn")
    return o, st_in * jnp.exp(seg_total) + _mm(v, k * jnp.exp(seg_after), "tn")


def _hgrn_norm(o, hg, ng):
    return o * lax.rsqrt(jnp.mean(o * o, axis=-1, keepdims=True) + EPS) * ng * jax.nn.silu(hg)


def _seg_blocks(e, h):
    return [e[i * BLOCK:(i + 1) * BLOCK, h * BLOCK:(h + 1) * BLOCK] for i in range(N_SEG)]


def _rope(x, cos, sin, first_half):
    partner = jnp.where(first_half, -pltpu.roll(x, 96, 1), pltpu.roll(x, 32, 1))
    return x * cos + partner * sin


def _rope_t(g, cos, sin, first_half):
    u = g * sin
    partner = jnp.where(first_half, pltpu.roll(u, 96, 1), -pltpu.roll(u, 32, 1))
    return g * cos + partner


def _low_half(x):
    return lax.broadcasted_iota(jnp.int32, x.shape, 1) < HEAD_DIM


def _both_halves(x, g):
    sw = _swap_halves(x)
    return jnp.where(_low_half(x), x, sw) if g == 0 else jnp.where(_low_half(x), sw, x)


def _att_scores(qa, qb, kc, kp, km, g, own4, band4, meta4):
    low = _low_half(qa)
    q4 = jnp.concatenate([jnp.where(low, qa, 0.0), jnp.where(low, 0.0, qa),
                          jnp.where(low, qb, 0.0), jnp.where(low, 0.0, qb)], axis=0)
    scale = HEAD_DIM ** -0.5
    neg = jnp.finfo(F32).min
    s = jnp.where(own4, _mm(_both_halves(kc, g), q4, "nt"), _mm(_both_halves(kp, g), q4, "nt"))
    return (jnp.where(band4, s * scale, neg), jnp.where(meta4, _mm(_both_halves(km, g), q4, "nt") * scale, neg))


def _att_probs(s, sm, sinkrow):
    mx = jnp.maximum(jnp.maximum(jnp.max(s, axis=0, keepdims=True), jnp.max(sm, axis=0, keepdims=True)), sinkrow)
    p, pm, ps = jnp.exp(s - mx), jnp.exp(sm - mx), jnp.exp(sinkrow - mx)
    inv = 1.0 / (jnp.sum(p, axis=0, keepdims=True) + jnp.sum(pm, axis=0, keepdims=True) + ps)
    return p * inv, pm * inv, ps * inv


def _att_probs_bwd(p, pm, ps, dp, dpm):
    r = jnp.sum(p * dp, axis=0, keepdims=True) + jnp.sum(pm * dpm, axis=0, keepdims=True)
    return p * (dp - r), pm * (dpm - r), -ps * r


def _att_values(p, pm, vc, vp, vm, g, own4):
    o4 = (_mm(jnp.where(own4, p, 0.0), _both_halves(vc, g), "tn") + _mm(jnp.where(own4, 0.0, p), _both_halves(vp, g), "tn")
          + _mm(pm, _both_halves(vm, g), "tn"))
    tiles = []
    for j in range(2):
        upper = o4[(2 * j) * BLOCK:(2 * j + 1) * BLOCK]
        tiles.append(jnp.where(_low_half(upper), upper, o4[(2 * j + 1) * BLOCK:(2 * j + 2) * BLOCK]))
    return tiles


def _att_masks(blk_idx):
    kidx = lax.broadcasted_iota(jnp.int32, (BLOCK, BLOCK), 0)
    qrow = lax.broadcasted_iota(jnp.int32, (BLOCK, BLOCK), 1)
    own_side = kidx <= qrow
    pos_own = blk_idx * BLOCK + kidx - PAD
    ok_band = (own_side & (pos_own >= N_META)) | (~own_side & (pos_own - BLOCK >= N_META) & (blk_idx >= 1))
    qpos = blk_idx * BLOCK + lax.broadcasted_iota(jnp.int32, (N_META, BLOCK), 1) - PAD
    ok_meta = lax.broadcasted_iota(jnp.int32, (N_META, BLOCK), 0) <= qpos
    return [jnp.concatenate([m] * 4, axis=1) for m in (own_side, ok_band, ok_meta)]


def _token_streams(tr, tile_of=lambda i: i):
    k = tr // BLOCK
    return [pl.BlockSpec((BLOCK, D_MODEL), lambda i, j=j: (jnp.maximum(k * tile_of(i) - 1 + j, 0), 0))
            for j in range(k)]


def _embed_ln(x, meta_shard, w_in_shard, g0, b0):
    p = x.shape[0] + BLOCK
    tr = _row_tile(p, 640)
    k = tr // BLOCK
    nt = p // tr
    tile_of = lambda s: (s + 1) % nt
    shards = [meta_shard, w_in_shard]
    c_in, c_out, c_shapes, c_sems = _comm_specs(shards, N_DEV)

    def body(*refs):
        g_ref, b_ref = refs[k:k + 2]
        h_ref, hb_ref, xh_ref, rs_ref = refs[k + 4:k + 8]
        out_refs = refs[k + 8:k + 10]
        lead_ref, meta_ref = refs[k + 10:k + 12]
        starts, passes, waits = _gather_behind(refs[k + 2:k + 4], out_refs, refs[k + 12:], [False, False])
        s = pl.program_id(0)
        t = tile_of(s)

        @pl.when(s == 0)
        def _():
            lead_ref[...] = jnp.zeros_like(lead_ref)
            for start in starts:
                start()

        @pl.when(s == nt - 1)
        def _():
            for step in passes + waits:
                step()
            pltpu.sync_copy(out_refs[0], meta_ref)
            for d in range(N_DEV):
                lead_ref[PAD:BLOCK, d * BLOCK:(d + 1) * BLOCK] = meta_ref[d]

        first = jnp.where(t == 0, lead_ref[...], refs[0][...])
        xhat, rstd = _ln_stats(jnp.concatenate([first] + [r[...] for r in refs[1:k]], axis=0))
        row = t * tr + lax.broadcasted_iota(jnp.int32, (tr, 1), 0)
        h = jnp.where(row >= PAD, xhat * g_ref[...] + b_ref[...], 0.0)
        h_ref[...] = h
        hb_ref[...] = h.astype(BF16)
        xh_ref[...] = xhat
        rs_ref[...] = rstd

    vec = pl.BlockSpec((1, D_MODEL), lambda s: (0, 0))
    rowsp = pl.BlockSpec((tr, D_MODEL), lambda s: (tile_of(s), 0))
    return pl.pallas_call(
        body, name="embed_ln", grid=(nt,),
        in_specs=_token_streams(tr, tile_of) + [vec, vec] + c_in,
        out_specs=[rowsp, rowsp, rowsp, pl.BlockSpec((tr, 1), lambda s: (tile_of(s), 0))] + c_out,
        out_shape=[jax.ShapeDtypeStruct((p, D_MODEL), F32), jax.ShapeDtypeStruct((p, D_MODEL), BF16),
                   jax.ShapeDtypeStruct((p, D_MODEL), F32), jax.ShapeDtypeStruct((p, 1), F32)] + c_shapes,
        scratch_shapes=[pltpu.VMEM((BLOCK, D_MODEL), F32), pltpu.VMEM((N_DEV, N_META, BLOCK), F32)] + c_sems,
        compiler_params=_cparams(("arbitrary",)),
    )(*([x] * k), g0, b0, *shards)


def _rope_tables(p):
    pos = (np.arange(p, dtype=np.int32) - PAD).astype(np.float32)
    half = HEAD_DIM // 2
    inv = np.float32(ROPE_THETA) ** (-np.arange(half, dtype=np.float32) / np.float32(half))
    ang = pos[:, None] * np.tile(inv.astype(np.float32), BLOCK // half)[None, :]
    return jnp.asarray(np.cos(ang), F32), jnp.asarray(np.sin(ang), F32)


def _att_sinkrows(sink_ref):
    lanehead = lax.broadcasted_iota(jnp.int32, (1, 4 * BLOCK), 1) // BLOCK
    rows = []
    for g in range(2):
        row = jnp.zeros((1, 4 * BLOCK), F32)
        for j in range(4):
            row = jnp.where(lanehead == j, sink_ref[0, 4 * g + j], row)
        rows.append(row)
    return rows


def _first_half(rows):
    return (lax.broadcasted_iota(jnp.int32, (rows, BLOCK), 1) % HEAD_DIM) < (HEAD_DIM // 2)


def _att_load(qkv_ref, cos_ref, sin_ref, with_q):
    cos, sin, fh = cos_ref[...], sin_ref[...], _first_half(BLOCK)
    qs = [_rope(qkv_ref[:, j * BLOCK:(j + 1) * BLOCK], cos, sin, fh) for j in range(4)] if with_q else None
    k = _rope(qkv_ref[:, ATT_QW:ATT_QW + ATT_KVW], cos, sin, fh)
    v = qkv_ref[:, ATT_QW + ATT_KVW:ATT_QW + 2 * ATT_KVW]
    return qs, k, v


def _att_load_meta(qkv_ref, cos_ref, sin_ref):
    k = _rope(qkv_ref[PAD:BLOCK, ATT_QW:ATT_QW + ATT_KVW], cos_ref[PAD:BLOCK, :], sin_ref[PAD:BLOCK, :],
              _first_half(N_META))
    return k, qkv_ref[PAD:BLOCK, ATT_QW + ATT_KVW:ATT_QW + 2 * ATT_KVW]


def _att_specs(blk):
    w = ATT_QW + 2 * ATT_KVW
    cur = lambda width: pl.BlockSpec((BLOCK, width), lambda i: (blk(i), 0))
    prev = lambda width: pl.BlockSpec((BLOCK, width), lambda i: (jnp.maximum(blk(i) - 1, 0), 0))
    meta = lambda width: pl.BlockSpec((BLOCK, width), lambda i: (0, 0))
    return [cur(w), prev(w), meta(w), cur(BLOCK), cur(BLOCK), prev(BLOCK), prev(BLOCK), meta(BLOCK), meta(BLOCK),
            pl.BlockSpec(memory_space=pltpu.SMEM)]


_FLIPS = [(dx, dy, dc) for dx in (0, 1) for dy in (0, 1) for dc in (0, 1)][1:]
N_PEERS = len(_FLIPS)


def _place():
    return lax.axis_index("x"), lax.axis_index("y"), lax.axis_index("c")


def _peer(place, flip):
    return tuple(1 - p if f else p for p, f in zip(place, flip))


def _slot(place, swapped):
    x, y, c = place
    return 4 * y + 2 * x + c if swapped else 4 * x + 2 * y + c


def _comm_specs(arrs, out_lead):
    n = len(arrs)
    outs = [jax.ShapeDtypeStruct((out_lead,) + a.shape[-2:], a.dtype) for a in arrs]
    sems = [pltpu.SemaphoreType.DMA((n, N_PEERS)), pltpu.SemaphoreType.DMA((n, N_PEERS)), pltpu.SemaphoreType.DMA((n,))]
    return [pl.BlockSpec(memory_space=pl.ANY)] * n, [pl.BlockSpec(memory_space=pl.ANY)] * n, outs, sems


def _gather_behind(shard_refs, out_refs, sems, swapped):
    send_sems, recv_sems, local_sems = sems
    x, y, c = _place()
    me, sibling = (x, y, c), (x, y, 1 - c)
    chips = [(1 - x, y), (x, 1 - y), (1 - x, 1 - y)]
    starts, passes, waits = [], [], []
    for w, (s, o) in enumerate(zip(shard_refs, out_refs)):
        def copy(k, block, to, from_shard=False, w=w, s=s, o=o):
            rows = o.at[_slot(block, swapped[w])]
            return pltpu.make_async_remote_copy(
                src_ref=s if from_shard else rows, dst_ref=rows, send_sem=send_sems.at[w, k],
                recv_sem=recv_sems.at[w, k], device_id=to, device_id_type=MESH)

        own = pltpu.make_async_copy(s, o.at[_slot(me, swapped[w])], local_sems.at[w])
        first = [copy(0, me, sibling, True)] + [copy(1 + j, me, (*chip, c), True) for j, chip in enumerate(chips)]
        handed = [copy(4 + j, (*chip, c), sibling) for j, chip in enumerate(chips)]
        starts += [own.start] + [cp.start for cp in first]
        for j, chip in enumerate(chips):
            passes += [copy(1 + j, (*chip, c), me).wait_recv, handed[j].start]
        waits.append(copy(0, sibling, me).wait_recv)
        waits += [copy(4 + j, (*chip, 1 - c), me).wait_recv for j, chip in enumerate(chips)]
        waits += [cp.wait_send for cp in first + handed] + [own.wait]
    return starts, passes, waits


def _scatter_behind(part_refs, recv_refs, sems, swapped):
    send_sems, recv_sems, _ = sems
    place = _place()
    starts, waits = [], []
    for w, (p, o) in enumerate(zip(part_refs, recv_refs)):
        for r, flip in enumerate(_FLIPS):
            peer = _peer(place, flip)
            cp = pltpu.make_async_remote_copy(
                src_ref=p.at[_slot(peer, swapped[w])], dst_ref=o.at[r], send_sem=send_sems.at[w, r],
                recv_sem=recv_sems.at[w, r], device_id=peer, device_id_type=MESH)
            starts.append(cp.start)
            waits += [cp.wait_recv, cp.wait_send]
    return starts, waits


def _mixers_fwd(proj_hg, proj_att, lbounds, norm_g, lv, cos, sin, sinks, shards, swapped):
    p = proj_hg.shape[0]
    nb = p // BLOCK
    n = len(shards)
    c_in, c_out, c_shapes, c_sems = _comm_specs(shards, N_DEV)
    pass_step = min(nb - 1, max(1, (5 * nb) // 8))

    def body(*refs):
        x_ref, lb_ref, ng_ref, lv_ref, cur_ref, prev_ref, meta_ref, cc, sc, cp, sp, cm, sm, sink_ref = refs[:14]
        shard_refs = refs[14:14 + n]
        y_ref, o_ref, st_ref, a_ref, raw_ref, pr_ref = refs[14 + n:20 + n]
        out_refs = refs[20 + n:20 + 2 * n]
        carry_ref = refs[20 + 2 * n]
        starts, passes, waits = _gather_behind(shard_refs, out_refs, refs[21 + 2 * n:], swapped)
        c = pl.program_id(0)

        @pl.when(c == 0)
        def _():
            carry_ref[...] = jnp.zeros_like(carry_ref)
            for start in starts:
                start()

        @pl.when(c == pass_step)
        def _():
            for step in passes:
                step()

        valid = (c * BLOCK + lax.broadcasted_iota(jnp.int32, (BLOCK, 1), 0)) >= PAD
        logf, k = _hgrn_gates(x_ref[:, HG_W:2 * HG_W], lb_ref[0:1, :], lb_ref[1:2, :], valid)
        e = _split_dot(lv_ref[...], logf, "nn")
        for h in range(HG_HEADS):
            sl = lambda part: x_ref[:, part * HG_W + h * BLOCK: part * HG_W + (h + 1) * BLOCK]
            hs = slice(h * BLOCK, (h + 1) * BLOCK)
            st_in = carry_ref[h]
            st_ref[0, h] = st_in
            seg = _seg_blocks(e, h)
            a = _hgrn_scores(sl(0), k[:, hs], *seg[3:])
            a_ref[0, h] = a.astype(BF16)
            raw, st_out = _hgrn_mix(sl(0), k[:, hs], sl(2), st_in, a, *seg[:3])
            raw_ref[:, hs] = raw
            y_ref[:, hs] = _hgrn_norm(raw, sl(3), ng_ref[...]).astype(BF16)
            carry_ref[h] = st_out

        qs, kc, vc = _att_load(cur_ref, cc, sc, True)
        _, kp, vp = _att_load(prev_ref, cp, sp, False)
        km, vm = _att_load_meta(meta_ref, cm, sm)
        sinkrows = _att_sinkrows(sink_ref)
        own4, band4, meta4 = _att_masks(c)
        for g in range(2):
            s, s_meta = _att_scores(qs[2 * g], qs[2 * g + 1], kc, kp, km, g, own4, band4, meta4)
            pr, pr_meta, pr_sink = _att_probs(s, s_meta, sinkrows[g])
            pr_ref[0, g, :BLOCK, :] = pr.astype(BF16)
            pr_ref[0, g, BLOCK:BLOCK + N_META, :] = pr_meta.astype(BF16)
            pr_ref[0, g, BLOCK + N_META:, :] = jnp.broadcast_to(pr_sink, (N_META, 4 * BLOCK)).astype(BF16)
            for j, tile in enumerate(_att_values(pr, pr_meta, vc, vp, vm, g, own4)):
                o_ref[:, (2 * g + j) * BLOCK:(2 * g + j + 1) * BLOCK] = tile.astype(BF16)

        @pl.when(c == nb - 1)
        def _():
            for wait in waits:
                wait()

    return pl.pallas_call(
        body, name="mixers_fwd", grid=(nb,),
        in_specs=[pl.BlockSpec((BLOCK, 4 * HG_W), lambda c: (c, 0)), pl.BlockSpec((2, HG_W), lambda c: (0, 0)),
                  pl.BlockSpec((1, BLOCK), lambda c: (0, 0)), pl.BlockSpec(lv.shape, lambda c: (0, 0))]
        + _att_specs(lambda c: c) + c_in,
        out_specs=[pl.BlockSpec((BLOCK, HG_W), lambda c: (c, 0)), pl.BlockSpec((BLOCK, ATT_QW), lambda c: (c, 0)),
                   pl.BlockSpec((1, HG_HEADS, BLOCK, BLOCK), lambda c: (c, 0, 0, 0)),
                   pl.BlockSpec((1, HG_HEADS, BLOCK, BLOCK), lambda c: (c, 0, 0, 0)),
                   pl.BlockSpec((BLOCK, HG_W), lambda c: (c, 0)),
                   pl.BlockSpec((1, 2, ATT_KEYS, 4 * BLOCK), lambda c: (c, 0, 0, 0))] + c_out,
        out_shape=[jax.ShapeDtypeStruct((p, HG_W), BF16), jax.ShapeDtypeStruct((p, ATT_QW), BF16),
                   jax.ShapeDtypeStruct((nb, HG_HEADS, BLOCK, BLOCK), F32),
                   jax.ShapeDtypeStruct((nb, HG_HEADS, BLOCK, BLOCK), BF16),
                   jax.ShapeDtypeStruct((p, HG_W), F32),
                   jax.ShapeDtypeStruct((nb, 2, ATT_KEYS, 4 * BLOCK), BF16)] + c_shapes,
        scratch_shapes=[pltpu.VMEM((HG_HEADS, BLOCK, BLOCK), F32)] + c_sems,
        compiler_params=_cparams(("arbitrary",)),
    )(proj_hg, lbounds, norm_g, lv, proj_att, proj_att, proj_att, cos, sin, cos, sin, cos, sin, sinks, *shards)


def _tile(rows, preferred):
    return preferred if rows % preferred == 0 else _row_tile(rows, preferred)


def _branch_mix(yh, oa, gates, w_bh, w_ba):
    y_hg = _dot(yh, w_bh, "nn")
    y_att = _dot(oa, w_ba, "nn")
    s1 = jax.nn.sigmoid(gates[:, :D_MODEL].astype(F32))
    s2 = jax.nn.sigmoid(gates[:, D_MODEL:].astype(F32))
    return s1 * y_hg + s2 * y_att, y_hg, y_att, s1, s2


def _mix_out_ln1(yh, oa, gates, h0, w_bh, w_ba, w_out, g1, b1):
    p = yh.shape[0]
    tr = _tile(p, 320)

    def body(yh_ref, oa_ref, g_ref, h0_ref, wbh_ref, wba_ref, wo_ref, g1_ref, b1_ref,
             mix_ref, h1_ref, h1b_ref, xh_ref, rs_ref):
        mixin = _branch_mix(yh_ref[...], oa_ref[...], g_ref[...], wbh_ref[...], wba_ref[...])[0]
        mix_ref[...] = mixin.astype(BF16)
        xhat, rstd = _ln_stats(ALPHA * h0_ref[...] + _dot(mixin, wo_ref[...], "nn"))
        h1 = xhat * g1_ref[...] + b1_ref[...]
        h1_ref[...] = h1
        h1b_ref[...] = h1.astype(BF16)
        xh_ref[...] = xhat
        rs_ref[...] = rstd

    row = lambda w: pl.BlockSpec((tr, w), lambda i: (i, 0))
    const = lambda a: pl.BlockSpec(a.shape, lambda i: (0, 0))
    return pl.pallas_call(
        body, name="mix_out_ln1", grid=(p // tr,),
        in_specs=[row(HG_W), row(ATT_QW), row(2 * D_MODEL), row(D_MODEL), const(w_bh), const(w_ba), const(w_out),
                  const(g1), const(b1)],
        out_specs=[row(D_MODEL), row(D_MODEL), row(D_MODEL), row(D_MODEL), row(1)],
        out_shape=[jax.ShapeDtypeStruct((p, D_MODEL), BF16), jax.ShapeDtypeStruct((p, D_MODEL), F32),
                   jax.ShapeDtypeStruct((p, D_MODEL), BF16), jax.ShapeDtypeStruct((p, D_MODEL), F32),
                   jax.ShapeDtypeStruct((p, 1), F32)],
        compiler_params=_cparams(("arbitrary",)),
    )(yh, oa, gates, h0, w_bh, w_ba, w_out, g1, b1)


FF_T = D_FF // 2


def _ffn_in_swiglu(h1, w_fi):
    p = h1.shape[0]
    tm = _row_tile(p, 640)

    def body(h_ref, w_ref, au_ref, s_ref):
        au = _dot(h_ref[...], w_ref[...], "nn")
        au_ref[...] = au.astype(BF16)
        s_ref[...] = (jax.nn.silu(au[:, :FF_T]) * au[:, FF_T:]).astype(BF16)

    return pl.pallas_call(
        body, name="ffn_in_swiglu", grid=(D_FF // FF_T, p // tm),
        in_specs=[pl.BlockSpec((tm, D_MODEL), lambda j, i: (i, 0)), pl.BlockSpec((D_MODEL, 2 * FF_T), lambda j, i: (0, j))],
        out_specs=[pl.BlockSpec((tm, 2 * FF_T), lambda j, i: (i, j)), pl.BlockSpec((tm, FF_T), lambda j, i: (i, j))],
        out_shape=[jax.ShapeDtypeStruct((p, 2 * D_FF), BF16), jax.ShapeDtypeStruct((p, D_FF), BF16)],
        compiler_params=_cparams(("arbitrary", "arbitrary")),
    )(h1, w_fi)


def _ffn_out_loss(s, w_fo, h1, g2, b2, target):
    p = h1.shape[0]
    tr = _row_tile(p, 640)
    k = tr // BLOCK

    def body(*refs):
        s_ref, w_ref, h_ref, g_ref, b_ref = refs[:5]
        dr_ref, loss_ref, dg_ref, db_ref = refs[5 + k:]
        i = pl.program_id(0)
        xhat, rstd = _ln_stats(ALPHA * h_ref[...] + _dot(s_ref[...], w_ref[...], "nn"))
        y = xhat * g_ref[...] + b_ref[...]
        row = i * tr + lax.broadcasted_iota(jnp.int32, (tr, 1), 0)
        tgt = jnp.concatenate([r[...] for r in refs[5:5 + k]], axis=0)
        err = jnp.where(row >= BLOCK, y - tgt, 0.0)
        dr, dg, db = _ln_bwd(err * (1.0 / D_MODEL), xhat, rstd, g_ref[...])
        dr_ref[...] = dr
        e2 = jnp.sum(err * err, axis=0, keepdims=True)
        part = e2[:, 0:BLOCK]
        for j in range(1, D_MODEL // BLOCK):
            part = part + e2[:, j * BLOCK:(j + 1) * BLOCK]
        part = part * (0.5 / D_MODEL)

        @pl.when(i == 0)
        def _():
            loss_ref[...] = part
            dg_ref[...] = dg
            db_ref[...] = db

        @pl.when(i > 0)
        def _():
            loss_ref[...] += part
            dg_ref[...] += dg
            db_ref[...] += db

    vec = pl.BlockSpec((1, D_MODEL), lambda i: (0, 0))
    rowsp = pl.BlockSpec((tr, D_MODEL), lambda i: (i, 0))
    return pl.pallas_call(
        body, name="ffn_out_loss", grid=(p // tr,),
        in_specs=[pl.BlockSpec((tr, D_FF), lambda i: (i, 0)), pl.BlockSpec((D_FF, D_MODEL), lambda i: (0, 0)),
                  rowsp, vec, vec] + _token_streams(tr),
        out_specs=[rowsp, pl.BlockSpec((1, BLOCK), lambda i: (0, 0)), vec, vec],
        out_shape=[jax.ShapeDtypeStruct((p, D_MODEL), F32), jax.ShapeDtypeStruct((1, BLOCK), F32),
                   jax.ShapeDtypeStruct((1, D_MODEL), F32), jax.ShapeDtypeStruct((1, D_MODEL), F32)],
        compiler_params=_cparams(("arbitrary",)),
    )(s, w_fo, h1, g2, b2, *([target] * k))


def _ffn_bwd(dr2, w_fo, au, w_fi):
    p = au.shape[0]
    tm = _tile(p, 320)

    def body(d_ref, wo_ref, au_ref, wi_ref, dau_ref, dh_ref):
        d = d_ref[...]
        dh = 0.0
        for j in range(D_FF // FF_T):
            a_cols = slice(2 * j * FF_T, (2 * j + 1) * FF_T)
            u_cols = slice((2 * j + 1) * FF_T, (2 * j + 2) * FF_T)
            ds = _dot(d, wo_ref[j * FF_T:(j + 1) * FF_T, :], "nt")
            _, vjp = jax.vjp(lambda a, u: jax.nn.silu(a) * u, au_ref[:, a_cols].astype(F32), au_ref[:, u_cols].astype(F32))
            da, du = vjp(ds)
            da, du = da.astype(BF16), du.astype(BF16)
            dau_ref[:, a_cols] = da
            dau_ref[:, u_cols] = du
            dh = dh + _dot(da, wi_ref[:, a_cols], "nt") + _dot(du, wi_ref[:, u_cols], "nt")
        dh_ref[...] = dh

    row = lambda w: pl.BlockSpec((tm, w), lambda i: (i, 0))
    kept = lambda a: pl.BlockSpec(a.shape, lambda i: (0, 0), pipeline_mode=pl.Buffered(1))
    return pl.pallas_call(
        body, name="ffn_bwd", grid=(p // tm,),
        in_specs=[row(D_MODEL), kept(w_fo), row(2 * D_FF), kept(w_fi)],
        out_specs=[row(2 * D_FF), row(D_MODEL)],
        out_shape=[jax.ShapeDtypeStruct((p, 2 * D_FF), BF16), jax.ShapeDtypeStruct((p, D_MODEL), F32)],
        compiler_params=_cparams(("arbitrary",)),
    )(dr2, w_fo, au, w_fi)


def _ln1_mix_bwd(dr2, dh1_ffn, xhat1, rstd1, g1, yh, oa, gates, w_bh, w_ba, w_out):
    p = yh.shape[0]
    tr = _tile(p, 320)

    def body(a_ref, b_ref, xh_ref, rs_ref, g1_ref, yh_ref, oa_ref, g_ref, wbh_ref, wba_ref, wo_ref,
             dr_ref, dyhg_ref, dyat_ref, dgt_ref, dyh_ref, doa_ref, dg_ref, db_ref):
        i = pl.program_id(0)
        dr, dg, db = _ln_bwd(ALPHA * a_ref[...] + b_ref[...], xh_ref[...], rs_ref[...], g1_ref[...])
        dr_ref[...] = dr
        d = _dot(dr, wo_ref[...], "nt")
        _, y_hg, y_att, s1, s2 = _branch_mix(yh_ref[...], oa_ref[...], g_ref[...], wbh_ref[...], wba_ref[...])
        dy_hg = d * s1
        dy_att = d * s2
        dyhg_ref[...] = dy_hg.astype(BF16)
        dyat_ref[...] = dy_att.astype(BF16)
        dgt_ref[:, :D_MODEL] = (d * y_hg * s1 * (1.0 - s1)).astype(BF16)
        dgt_ref[:, D_MODEL:] = (d * y_att * s2 * (1.0 - s2)).astype(BF16)
        dyh_ref[...] = _dot(dy_hg, wbh_ref[...], "nt")
        doa_ref[...] = _dot(dy_att, wba_ref[...], "nt")

        @pl.when(i == 0)
        def _():
            dg_ref[...] = dg
            db_ref[...] = db

        @pl.when(i > 0)
        def _():
            dg_ref[...] += dg
            db_ref[...] += db

    row = lambda w: pl.BlockSpec((tr, w), lambda i: (i, 0))
    const = lambda a: pl.BlockSpec(a.shape, lambda i: (0, 0))
    vec = pl.BlockSpec((1, D_MODEL), lambda i: (0, 0))
    return pl.pallas_call(
        body, name="ln1_mix_bwd", grid=(p // tr,),
        in_specs=[row(D_MODEL), row(D_MODEL), row(D_MODEL), row(1), vec, row(HG_W), row(ATT_QW), row(2 * D_MODEL),
                  const(w_bh), const(w_ba), const(w_out)],
        out_specs=[row(D_MODEL), row(D_MODEL), row(D_MODEL), row(2 * D_MODEL), row(HG_W), row(ATT_QW), vec, vec],
        out_shape=[jax.ShapeDtypeStruct((p, D_MODEL), F32), jax.ShapeDtypeStruct((p, D_MODEL), BF16),
                   jax.ShapeDtypeStruct((p, D_MODEL), BF16), jax.ShapeDtypeStruct((p, 2 * D_MODEL), BF16),
                   jax.ShapeDtypeStruct((p, HG_W), F32), jax.ShapeDtypeStruct((p, ATT_QW), F32),
                   jax.ShapeDtypeStruct((1, D_MODEL), F32), jax.ShapeDtypeStruct((1, D_MODEL), F32)],
        compiler_params=_cparams(("arbitrary",)),
    )(dr2, dh1_ffn, xhat1, rstd1, g1, yh, oa, gates, w_bh, w_ba, w_out)


MIX_W = 4 * HG_W + ATT_QW + 2 * ATT_KVW
ATT_KEYS = BLOCK + 2 * N_META


def _mixers_bwd(proj_hg, proj_att, lbounds, norm_g, lv, states, scores, raw, probs, cos, sin, sinks, dyh, doa,
                parts, swapped):
    p = proj_hg.shape[0]
    nb = p // BLOCK
    n = len(parts)
    kvw = 2 * ATT_KVW
    rev = lambda s: nb - 1 - s
    c_in, c_out, c_shapes, c_sems = _comm_specs(parts, N_PEERS)

    def body(*refs):
        (x_ref, lb_ref, ng_ref, lv_ref, st_ref, a_ref, raw_ref, pr_ref, cur_ref, prev_ref, meta_ref, cc, sc, cp, sp,
         cm, sm, sink_ref, dy_ref, do_ref) = refs[:20]
        part_refs = refs[20:20 + n]
        dx_ref, dlb_ref, dng_ref, dsink_ref = refs[20 + n:24 + n]
        recv_refs = refs[24 + n:24 + 2 * n]
        dcarry_ref, dkv_next_ref, dkv_meta_ref = refs[24 + 2 * n:27 + 2 * n]
        starts, waits = _scatter_behind(part_refs, recv_refs, refs[27 + 2 * n:], swapped)
        step = pl.program_id(0)
        c = rev(step)

        @pl.when(step == 0)
        def _():
            dcarry_ref[...] = jnp.zeros_like(dcarry_ref)
            dkv_next_ref[...] = jnp.zeros_like(dkv_next_ref)
            dkv_meta_ref[...] = jnp.zeros_like(dkv_meta_ref)
            dlb_ref[...] = jnp.zeros_like(dlb_ref)
            dng_ref[...] = jnp.zeros_like(dng_ref)
            dsink_ref[...] = jnp.zeros_like(dsink_ref)
            for start in starts:
                start()

        fh = _first_half(BLOCK)
        qs, kc, vc = _att_load(cur_ref, cc, sc, True)
        _, kp, vp = _att_load(prev_ref, cp, sp, False)
        km, vm = _att_load_meta(meta_ref, cm, sm)
        own4, band4, meta4 = _att_masks(c)
        att0 = 4 * HG_W
        dkm = dkp = dkc = dvm = dvp = dvc = 0.0
        dsinkrows = []
        for g in range(2):
            pr = pr_ref[0, g, :BLOCK, :].astype(F32)
            pr_meta = pr_ref[0, g, BLOCK:BLOCK + N_META, :].astype(F32)
            pr_sink = jnp.max(pr_ref[0, g, BLOCK + N_META:, :].astype(F32), axis=0, keepdims=True)
            _, values_vjp = jax.vjp(lambda *a, g=g: _att_values(*a, g, own4), pr, pr_meta, vc, vp, vm)
            dpr, dpr_meta, dvc_g, dvp_g, dvm_g = values_vjp(
                [do_ref[:, (2 * g + j) * BLOCK:(2 * g + j + 1) * BLOCK] for j in range(2)])
            ds, ds_meta, dsinkrow = _att_probs_bwd(pr, pr_meta, pr_sink, dpr, dpr_meta)
            _, scores_vjp = jax.vjp(lambda *a, g=g: _att_scores(*a, g, own4, band4, meta4),
                                    qs[2 * g], qs[2 * g + 1], kc, kp, km)
            dqa, dqb, dkc_g, dkp_g, dkm_g = scores_vjp((ds, ds_meta))
            for j, dq in enumerate((dqa, dqb)):
                dx_ref[:, att0 + (2 * g + j) * BLOCK:att0 + (2 * g + j + 1) * BLOCK] = _rope_t(
                    dq, cc[...], sc[...], fh).astype(BF16)
            dkm, dkp, dkc = dkm + dkm_g, dkp + dkp_g, dkc + dkc_g
            dvm, dvp, dvc = dvm + dvm_g, dvp + dvp_g, dvc + dvc_g
            dsinkrows.append(dsinkrow)
        ds0, ds1 = dsinkrows
        dkv_meta_ref[:, :BLOCK] += _rope_t(dkm, cm[PAD:BLOCK, :], sm[PAD:BLOCK, :], _first_half(N_META))
        dkv_meta_ref[:, BLOCK:] += dvm
        last = jnp.where(c == 0, 1.0, 0.0)
        to_meta_rows = lambda m: jnp.concatenate([jnp.zeros((PAD, BLOCK), F32), last * m], axis=0)
        dk = _rope_t(dkc, cc[...], sc[...], fh) + dkv_next_ref[:, :BLOCK] + to_meta_rows(dkv_meta_ref[:, :BLOCK])
        dv = dvc + dkv_next_ref[:, BLOCK:] + to_meta_rows(dkv_meta_ref[:, BLOCK:])
        dx_ref[:, att0 + ATT_QW:att0 + ATT_QW + ATT_KVW] = dk.astype(BF16)
        dx_ref[:, att0 + ATT_QW + ATT_KVW:] = dv.astype(BF16)
        dkv_next_ref[:, :BLOCK] = _rope_t(dkp, cp[...], sp[...], fh)
        dkv_next_ref[:, BLOCK:] = dvp
        sink_rows = []
        for dsg in (ds0, ds1):
            for j in range(4):
                tot = jnp.sum(dsg[:, j * BLOCK:(j + 1) * BLOCK], axis=1, keepdims=True)
                sink_rows.append(jnp.broadcast_to(tot, (1, BLOCK)))
        dsink_ref[...] += jnp.concatenate(sink_rows, axis=0)

        valid = (c * BLOCK + lax.broadcasted_iota(jnp.int32, (BLOCK, 1), 0)) >= PAD
        (logf, k), gates_vjp = jax.vjp(lambda hf, a0, a1: _hgrn_gates(hf, a0, a1, valid),
                                       x_ref[:, HG_W:2 * HG_W], lb_ref[0:1, :], lb_ref[1:2, :])
        lvv = lv_ref[...]
        e = _split_dot(lvv, logf, "nn")
        dng = jnp.zeros((1, BLOCK), F32)
        dk, dseg = [], []
        for h in range(HG_HEADS):
            sl = lambda part: x_ref[:, part * HG_W + h * BLOCK: part * HG_W + (h + 1) * BLOCK]
            hs = slice(h * BLOCK, (h + 1) * BLOCK)
            seg = _seg_blocks(e, h)
            _, norm_vjp = jax.vjp(_hgrn_norm, raw_ref[:, hs], sl(3), ng_ref[...])
            draw, dhg, dngh = norm_vjp(dy_ref[:, hs])
            _, mix_vjp = jax.vjp(_hgrn_mix, sl(0), k[:, hs], sl(2), st_ref[0, h], a_ref[0, h].astype(F32), *seg[:3])
            dhq, dkh, dhi, dst, da, *dseg_mix = mix_vjp((draw, dcarry_ref[h]))
            _, scores_vjp = jax.vjp(_hgrn_scores, sl(0), k[:, hs], *seg[3:])
            dhq2, dkh2, *dseg_lvl = scores_vjp(da)
            for part, val in ((0, dhq + dhq2), (2, dhi), (3, dhg)):
                dx_ref[:, part * HG_W + h * BLOCK: part * HG_W + (h + 1) * BLOCK] = val.astype(BF16)
            dk.append(dkh + dkh2)
            dseg.append(jnp.concatenate(dseg_mix + dseg_lvl, axis=0))
            dng = dng + dngh
            dcarry_ref[h] = dst
        dlogf = _split_dot(lvv, jnp.concatenate(dseg, axis=1), "tn")
        dhf, da0, da1 = gates_vjp((dlogf, jnp.concatenate(dk, axis=1)))
        dx_ref[:, HG_W:2 * HG_W] = dhf.astype(BF16)
        dlb_ref[0:1, :] += da0
        dlb_ref[1:2, :] += da1
        dng_ref[...] += dng

        @pl.when(step == nb - 1)
        def _():
            for wait in waits:
                wait()

    const = lambda shape: pl.BlockSpec(shape, lambda s: (0,) * len(shape))
    per_head = pl.BlockSpec((1, HG_HEADS, BLOCK, BLOCK), lambda s: (rev(s), 0, 0, 0))
    return pl.pallas_call(
        body, name="mixers_bwd", grid=(nb,),
        in_specs=[pl.BlockSpec((BLOCK, 4 * HG_W), lambda s: (rev(s), 0)), const((2, HG_W)), const((1, BLOCK)),
                  const(lv.shape), per_head, per_head, pl.BlockSpec((BLOCK, HG_W), lambda s: (rev(s), 0)),
                  pl.BlockSpec((1, 2, ATT_KEYS, 4 * BLOCK), lambda s: (rev(s), 0, 0, 0))]
        + _att_specs(rev)
        + [pl.BlockSpec((BLOCK, HG_W), lambda s: (rev(s), 0)), pl.BlockSpec((BLOCK, ATT_QW), lambda s: (rev(s), 0))]
        + c_in,
        out_specs=[pl.BlockSpec((BLOCK, MIX_W), lambda s: (rev(s), 0)), const((2, HG_W)), const((1, BLOCK)),
                   const((ATT_HEADS, BLOCK))] + c_out,
        out_shape=[jax.ShapeDtypeStruct((p, MIX_W), BF16), jax.ShapeDtypeStruct((2, HG_W), F32),
                   jax.ShapeDtypeStruct((1, BLOCK), F32), jax.ShapeDtypeStruct((ATT_HEADS, BLOCK), F32)] + c_shapes,
        scratch_shapes=[pltpu.VMEM((HG_HEADS, BLOCK, BLOCK), F32), pltpu.VMEM((BLOCK, kvw), F32),
                        pltpu.VMEM((N_META, kvw), F32)] + c_sems,
        compiler_params=_cparams(("arbitrary",)),
    )(proj_hg, lbounds, norm_g, lv, states, scores, raw, probs, proj_att, proj_att, proj_att, cos, sin, cos, sin,
      cos, sin, sinks, dyh, doa, *parts)


_HBM = pl.BlockSpec(memory_space=pltpu.HBM)
_SEM = pl.BlockSpec(memory_space=pltpu.SEMAPHORE)
_ORDERED_BY_DATA = pltpu.CompilerParams(has_side_effects=pltpu.SideEffectType.DATAFLOW_SIDE_EFFECTING)


def _exchange_copies(part_ref, land_ref, send_sems, recv_sems):
    place = _place()
    return [pltpu.make_async_remote_copy(
        src_ref=part_ref.at[_slot(_peer(place, flip), False)], dst_ref=land_ref.at[r], send_sem=send_sems.at[r],
        recv_sem=recv_sems.at[r], device_id=_peer(place, flip), device_id_type=MESH) for r, flip in enumerate(_FLIPS)]


def _exchange_start(parts, name):
    def body(part_ref, land_ref, send_sems, recv_sems, part_thru, land_thru, token):
        for cp in _exchange_copies(part_ref, land_ref, send_sems, recv_sems):
            cp.start()
        token[...] = jnp.zeros_like(token)

    land = (N_PEERS,) + parts.shape[1:]
    return pl.pallas_call(
        body, name=name,
        out_shape=(pltpu.SemaphoreType.DMA((N_PEERS,)), pltpu.SemaphoreType.DMA((N_PEERS,)),
                   pltpu.HBM(parts.shape, parts.dtype), pltpu.HBM(land, parts.dtype), jax.ShapeDtypeStruct((8, BLOCK), F32)),
        in_specs=(_HBM, _HBM), out_specs=(_SEM, _SEM, _HBM, _HBM, pl.BlockSpec(memory_space=pltpu.VMEM)),
        input_output_aliases={0: 2, 1: 3}, compiler_params=_ORDERED_BY_DATA,
    )(pltpu.with_memory_space_constraint(parts, pltpu.HBM),
      pltpu.with_memory_space_constraint(lax.empty(land, parts.dtype), pltpu.HBM))


def _exchange_wait(send_sems, recv_sems, part_thru, land_thru, after, name):
    def body(part_ref, land_ref, send_sems, recv_sems, after_ref, part_out, land_out):
        for cp in _exchange_copies(part_ref, land_ref, send_sems, recv_sems):
            cp.wait_send()
            cp.wait_recv()

    return pl.pallas_call(
        body, name=name,
        out_shape=(pltpu.HBM(part_thru.shape, part_thru.dtype), pltpu.HBM(land_thru.shape, land_thru.dtype)),
        in_specs=(_HBM, _HBM, _SEM, _SEM, pl.BlockSpec(memory_space=pl.ANY)), out_specs=(_HBM, _HBM),
        input_output_aliases={0: 0, 1: 1}, compiler_params=_ORDERED_BY_DATA,
    )(part_thru, land_thru, send_sems, recv_sems, after)


def _embed_bwd(dmix, dgates, w_mix, w_gates, dr1, xhat0, rstd0, g0):
    p = dmix.shape[0]
    tm = _row_tile(p, 640)
    nm = p // tm

    def body(a_ref, g_ref, wa_ref, wg_ref, dr_ref, xh_ref, rs_ref, g0_ref, o_ref, dg_ref, db_ref):
        i = pl.program_id(0)
        dh0 = ALPHA * dr_ref[...] + _dot(a_ref[...], wa_ref[...], "nt") + _dot(g_ref[...], wg_ref[...], "nt")
        row = i * tm + lax.broadcasted_iota(jnp.int32, (tm, 1), 0)
        dx, dg, db = _ln_bwd(jnp.where(row >= PAD, dh0, 0.0), xh_ref[...], rs_ref[...], g0_ref[...])
        o_ref[...] = dx

        @pl.when(i == 0)
        def _():
            dg_ref[...] = dg
            db_ref[...] = db

        @pl.when(i > 0)
        def _():
            dg_ref[...] += dg
            db_ref[...] += db

    row = lambda w: pl.BlockSpec((tm, w), lambda i: (i, 0))
    const = lambda a: pl.BlockSpec(a.shape, lambda i: (0, 0))
    vec = pl.BlockSpec((1, D_MODEL), lambda i: (0, 0))
    return pl.pallas_call(
        body, name="embed_bwd", grid=(nm,),
        in_specs=[row(dmix.shape[1]), row(dgates.shape[1]), const(w_mix), const(w_gates), row(D_MODEL), row(D_MODEL),
                  row(1), vec],
        out_specs=[row(D_MODEL), vec, vec],
        out_shape=[jax.ShapeDtypeStruct((p, D_MODEL), F32), jax.ShapeDtypeStruct((1, D_MODEL), F32),
                   jax.ShapeDtypeStruct((1, D_MODEL), F32)],
        compiler_params=_cparams(("arbitrary",)),
    )(dmix, dgates, w_mix, w_gates, dr1, xhat0, rstd0, g0)


_LATE = ("w_branch_hg", "w_branch_attn", "w_out", "w_ffn_in", "w_ffn_out")
_COLUMN_SHARDED = ("meta_tokens", "w_in", "w_branch_hg", "w_branch_attn", "w_ffn_in")
_SWAPPED = ("w_ffn_in",)


def _whole(name, gathered):
    _, r, c = gathered.shape
    if name in _COLUMN_SHARDED:
        return jnp.transpose(gathered, (1, 0, 2)).reshape(r, N_DEV * c)
    return gathered.reshape(N_DEV * r, c)


def _slots(name, whole):
    r, c = whole.shape
    if name in _COLUMN_SHARDED:
        return jnp.transpose(whole.reshape(r, N_DEV, c // N_DEV), (1, 0, 2))
    return whole.reshape(N_DEV, r // N_DEV, c)


def _device_step(x, target, meta_shard, ln_emb_g, ln_emb_b, w_in_shard, lbounds, norm_g, sinks, late_shards,
                 ln1_g, ln1_b, ln2_g, ln2_b):
    s = x.shape[0]
    p = s + BLOCK
    tm = _row_tile(p, 640)
    lv = _level_stack()
    cos, sin = _rope_tables(p)
    hg_end = 4 * HG_W
    mm = functools.partial(_tiled_matmul, tm=tm)
    swapped = [n in _SWAPPED for n in _LATE]

    h0, h0b, xhat0, rstd0, _, g_win = _embed_ln(x, meta_shard, w_in_shard, ln_emb_g, ln_emb_b)
    w_in = _whole("w_in", g_win)
    proj_hg = mm(h0b, w_in[:, :hg_end], "nn", tn=hg_end, tc=D_MODEL, out_dtype=F32, name="proj_hg")
    proj_att = mm(h0b, w_in[:, hg_end:MIX_W], "nn", tn=MIX_W - hg_end, tc=D_MODEL, out_dtype=F32, name="proj_att")
    gates = mm(h0b, w_in[:, MIX_W:], "nn", tn=2 * D_MODEL, tc=D_MODEL, out_dtype=BF16, name="proj_gates")
    yh, oa, states, scores, raw, probs, *gathered = _mixers_fwd(
        proj_hg, proj_att, lbounds, norm_g, lv, cos, sin, sinks, late_shards, swapped)
    w_bh, w_ba, w_out, w_fi, w_fo = [_whole(n, g) for n, g in zip(_LATE, gathered)]
    mixin, h1, h1b, xhat1, rstd1 = _mix_out_ln1(yh, oa, gates, h0, w_bh, w_ba, w_out, ln1_g, ln1_b)
    au, sw = _ffn_in_swiglu(h1b, w_fi)
    dr2, loss_part, dg2, db2 = _ffn_out_loss(sw, w_fo, h1, ln2_g, ln2_b, target)

    mtn = functools.partial(_tiled_matmul_tn, tm=_row_tile(p, 1664), out_dtype=BF16)
    d_wfo = mtn(sw, dr2, tk=FF_T, tn=D_MODEL, name="grad_w_ffn_out")
    dau, dh1_ffn = _ffn_bwd(dr2, w_fo, au, w_fi)
    d_wfi = mtn(h1b, dau, tk=D_MODEL, tn=FF_T, name="grad_w_ffn_in")
    dr1, dy_hg, dy_att, dgates, dyh, doa, dg1, db1 = _ln1_mix_bwd(
        dr2, dh1_ffn, xhat1, rstd1, ln1_g, yh, oa, gates, w_bh, w_ba, w_out)
    d_wout = mtn(mixin, dr1, tk=D_MODEL, tn=D_MODEL, name="grad_w_out")
    d_wbh = mtn(yh, dy_hg, tk=HG_W, tn=D_MODEL, name="grad_w_branch_hg")
    d_wba = mtn(oa, dy_att, tk=ATT_QW, tn=D_MODEL, name="grad_w_branch_attn")
    late_parts = [_slots(n, g) for n, g in zip(_LATE, (d_wbh, d_wba, d_wout, d_wfi, d_wfo))]
    dmix, d_lb, d_ng, d_sink, *late_recv = _mixers_bwd(
        proj_hg, proj_att, lbounds, norm_g, lv, states, scores, raw, probs, cos, sin, sinks, dyh, doa, late_parts,
        swapped)
    d_win = jnp.concatenate([mtn(h0b, dmix, tk=D_MODEL, tn=MIX_W // 2, name="grad_w_in_mixers"),
                             mtn(h0b, dgates, tk=D_MODEL, tn=D_MODEL, name="grad_w_in_gates")], axis=1)
    *win_flight, token = _exchange_start(_slots("w_in", d_win), "w_in_grads_start")
    dxin, dg0, db0 = _embed_bwd(dmix, dgates, w_in[:, :MIX_W], w_in[:, MIX_W:], dr1, xhat0, rstd0,
                                ln_emb_g + token[0:1, 0:1])

    small = dict(ln_emb_g=dg0, ln_emb_b=db0, hg_lower_bounds=d_lb, hg_norm_g=d_ng, attn_sinks=d_sink[:, 0],
                 ln1_g=dg1, ln1_b=db1, ln2_g=dg2, ln2_b=db2)
    big = dict(zip(_LATE, zip(late_parts, late_recv)))
    return loss_part, dxin[BLOCK:], small, dxin[PAD:BLOCK], big, win_flight


def _all_gather(arrs, dtypes, name):
    n = len(arrs)

    def body(*refs):
        ins, outs, stages = refs[:n], refs[n:2 * n], refs[2 * n:3 * n]
        send_sems, recv_sems, local_sems = refs[3 * n:]
        x, y, c = _place()
        sibling = (x, y, 1 - c)
        chips = [(1 - x, y), (x, 1 - y), (1 - x, 1 - y)]
        slot = lambda px, py, pc: 4 * px + 2 * py + pc

        def copy(w, k, block, to, from_stage=False):
            return pltpu.make_async_remote_copy(
                src_ref=stages[w] if from_stage else outs[w].at[slot(*block)], dst_ref=outs[w].at[slot(*block)],
                send_sem=send_sems.at[w, k], recv_sem=recv_sems.at[w, k], device_id=to, device_id_type=MESH)

        mine, first, passed = [], [], []
        for w in range(n):
            stages[w][...] = ins[w][...].astype(dtypes[w])
            mine.append(pltpu.make_async_copy(stages[w], outs[w].at[slot(x, y, c)], local_sems.at[w]))
            mine[-1].start()
        for w in range(n):
            first.append(copy(w, 0, (x, y, c), sibling, from_stage=True))
            first += [copy(w, 1 + j, (x, y, c), (*chip, c), from_stage=True) for j, chip in enumerate(chips)]
        for cp in first:
            cp.start()
        for j, chip in enumerate(chips):
            for w in range(n):
                copy(w, 1 + j, (*chip, c), (x, y, c)).wait_recv()
                passed.append(copy(w, 4 + j, (*chip, c), sibling))
                passed[-1].start()
        for w in range(n):
            copy(w, 0, sibling, (x, y, c)).wait_recv()
            for j, chip in enumerate(chips):
                copy(w, 4 + j, (*chip, 1 - c), (x, y, c)).wait_recv()
        for cp in first + passed:
            cp.wait_send()
        for cp in mine:
            cp.wait()

    return pl.pallas_call(
        body, name=name,
        in_specs=[pl.BlockSpec(memory_space=pltpu.VMEM)] * n,
        out_specs=[pl.BlockSpec(memory_space=pl.ANY)] * n,
        out_shape=[jax.ShapeDtypeStruct((N_DEV,) + a.shape, dt) for a, dt in zip(arrs, dtypes)],
        scratch_shapes=[pltpu.VMEM(a.shape, dt) for a, dt in zip(arrs, dtypes)]
        + [pltpu.SemaphoreType.DMA((n, 7)), pltpu.SemaphoreType.DMA((n, 7)), pltpu.SemaphoreType.DMA((n,))],
        compiler_params=pltpu.CompilerParams(vmem_limit_bytes=VMEM_LIMIT_BYTES),
    )(*arrs)


def _cast_shards(arrs):
    def body(*refs):
        for src, dst in zip(refs[:len(arrs)], refs[len(arrs):]):
            dst[...] = src[...].astype(BF16)

    return pl.pallas_call(body, name="cast_shards", out_shape=[jax.ShapeDtypeStruct(a.shape, BF16) for a in arrs],
                          compiler_params=pltpu.CompilerParams(vmem_limit_bytes=VMEM_LIMIT_BYTES))(*arrs)


def _shard_rows(rows):
    return rows if rows <= 512 else 256


def _adamw_math(w, g, m, v):
    m = ADAM_B1 * m + (1.0 - ADAM_B1) * g
    v = ADAM_B2 * v + (1.0 - ADAM_B2) * (g * g)
    m_hat = m / (1.0 - ADAM_B1 ** ADAM_STEP)
    v_hat = v / (1.0 - ADAM_B2 ** ADAM_STEP)
    delta = -ADAM_LR * (m_hat / (jnp.sqrt(v_hat) + ADAM_EPS) + ADAM_WD * w)
    return delta, m, v


def _reduce_adamw(parts, recv, own_slot, w, m, v, name):
    r, cdim = w.shape
    tr = _shard_rows(r)

    def body(idx_ref, p_ref, r_ref, w_ref, m_ref, v_ref, g_out, d_out, m_out, v_out):
        g = p_ref[0].astype(F32)
        for j in range(N_PEERS):
            g = g + r_ref[j].astype(F32)
        d, mn, vn = _adamw_math(w_ref[...], g, m_ref[...], v_ref[...])
        g_out[...] = g
        d_out[...] = d
        m_out[...] = mn
        v_out[...] = vn

    flat = pl.BlockSpec((tr, cdim), lambda i, idx_ref: (i, 0))
    return pl.pallas_call(
        body, name=name,
        grid_spec=pltpu.PrefetchScalarGridSpec(
            num_scalar_prefetch=1, grid=(r // tr,),
            in_specs=[pl.BlockSpec((1, tr, cdim), lambda i, idx_ref: (idx_ref[0], i, 0)),
                      pl.BlockSpec((N_PEERS, tr, cdim), lambda i, idx_ref: (0, i, 0)), flat, flat, flat],
            out_specs=[flat] * 4),
        out_shape=[jax.ShapeDtypeStruct((r, cdim), F32)] * 4,
        compiler_params=_cparams(("arbitrary",)),
    )(own_slot, parts, recv, w, m, v)


def _adamw_plain(w, g, m, v, name):
    def body(w_ref, g_ref, m_ref, v_ref, d_out, m_out, v_out):
        d_out[...], m_out[...], v_out[...] = _adamw_math(w_ref[...], g_ref[...], m_ref[...], v_ref[...])

    return pl.pallas_call(body, name=name, out_shape=[jax.ShapeDtypeStruct(w.shape, F32)] * 3)(w, g, m, v)


_SMALL_LAYOUT = (("ln_emb_g", 8), ("ln_emb_b", 8), ("hg_lower_bounds", 8), ("hg_norm_g", 1), ("attn_sinks", 1),
                 ("ln1_g", 8), ("ln1_b", 8), ("ln2_g", 8), ("ln2_b", 8))
_META_ROW = sum(r for _, r in _SMALL_LAYOUT)
_META_ROWS = N_META * D_MODEL // BLOCK
_LOSS_ROW = _META_ROW + _META_ROWS
SMALL_ROWS = 192


def _pack_small(vals, meta=None, loss_row=None):
    rows = []
    for name, nrows in _SMALL_LAYOUT:
        flat = vals[name].reshape(-1).astype(F32)
        flat = jnp.pad(flat, (0, nrows * BLOCK - flat.shape[0]))
        rows.append(flat.reshape(nrows, BLOCK))
    rows.append(jnp.zeros((_META_ROWS, BLOCK), F32) if meta is None else meta.reshape(_META_ROWS, BLOCK))
    rows.append(jnp.zeros((1, BLOCK), F32) if loss_row is None else loss_row)
    packed = jnp.concatenate(rows, axis=0)
    return jnp.pad(packed, ((0, SMALL_ROWS - packed.shape[0]), (0, 0)))


def _unpack_small(packed, shapes):
    out, row = {}, 0
    for name, nrows in _SMALL_LAYOUT:
        size = math.prod(shapes[name])
        out[name] = packed[row:row + nrows].reshape(-1)[:size].reshape(shapes[name])
        row += nrows
    return out


def _small_reduce_adamw(gathered, w, m, v):
    def body(g_ref, w_ref, m_ref, v_ref, g_out, d_out, m_out, v_out, loss_out):
        g = g_ref[0]
        for s in range(1, N_DEV):
            g = g + g_ref[s]
        d, mn, vn = _adamw_math(w_ref[...], g, m_ref[...], v_ref[...])
        g_out[...] = g
        d_out[...] = d
        m_out[...] = mn
        v_out[...] = vn
        loss_out[...] = jnp.broadcast_to(jnp.sum(g_ref[:, _LOSS_ROW, :]), (1, BLOCK))

    shp = jax.ShapeDtypeStruct((SMALL_ROWS, BLOCK), F32)
    return pl.pallas_call(body, name="small_reduce_adamw",
                          out_shape=[shp] * 4 + [jax.ShapeDtypeStruct((1, BLOCK), F32)])(gathered, w, m, v)


_WEIGHTS = ("meta_tokens", "ln_emb_g", "ln_emb_b", "w_in", "hg_lower_bounds", "hg_norm_g", "attn_sinks",
            "w_branch_hg", "w_branch_attn", "w_out", "ln1_g", "ln1_b", "w_ffn_in", "w_ffn_out", "ln2_g", "ln2_b")


def kernel(x, meta_tokens, ln_emb_g, ln_emb_b, w_in, hg_lower_bounds, hg_norm_g, attn_sinks, w_branch_hg, w_branch_attn, w_out, ln1_g, ln1_b, w_ffn_in, w_ffn_out, ln2_g, ln2_b, loss_target, m_meta_tokens, m_ln_emb_g, m_ln_emb_b, m_w_in, m_hg_lower_bounds, m_hg_norm_g, m_attn_sinks, m_w_branch_hg, m_w_branch_attn, m_w_out, m_ln1_g, m_ln1_b, m_w_ffn_in, m_w_ffn_out, m_ln2_g, m_ln2_b, v_meta_tokens, v_ln_emb_g, v_ln_emb_b, v_w_in, v_hg_lower_bounds, v_hg_norm_g, v_attn_sinks, v_w_branch_hg, v_w_branch_attn, v_w_out, v_ln1_g, v_ln1_b, v_w_ffn_in, v_w_ffn_out, v_ln2_g, v_ln2_b):
    given = dict(locals())
    weights = {n: given[n] for n in _WEIGHTS}
    mom1 = {n: given["m_" + n] for n in _WEIGHTS}
    mom2 = {n: given["v_" + n] for n in _WEIGHTS}
    shard2d = lambda a: a.reshape(a.shape[-2:])

    w_in_shard, *late_shards = _cast_shards([shard2d(weights[n]) for n in ("w_in",) + _LATE])
    loss_part, grad_x, small_grads, meta_grad, big, win_flight = _device_step(
        x[0], loss_target[0], meta_tokens, ln_emb_g.reshape(1, -1), ln_emb_b.reshape(1, -1), w_in_shard,
        hg_lower_bounds, hg_norm_g, attn_sinks, late_shards, ln1_g, ln1_b, ln2_g, ln2_b)

    place = _place()
    out = {}

    def reduce_adamw(n, parts, recv):
        own = _slot(place, n in _SWAPPED).astype(jnp.int32).reshape(1)
        res = _reduce_adamw(parts, recv, own, shard2d(weights[n]), shard2d(mom1[n]), shard2d(mom2[n]), "adamw_" + n)
        out[n] = [r.reshape(weights[n].shape) for r in res]

    for n, (parts, recv) in big.items():
        reduce_adamw(n, parts, recv)

    small_names = [n for n, _ in _SMALL_LAYOUT]
    packed = _pack_small(small_grads, meta_grad, loss_part)
    all_small, = _all_gather([packed], [F32], "gather_small")
    res = _small_reduce_adamw(all_small, _pack_small(weights), _pack_small(mom1), _pack_small(mom2))
    shapes = {n: weights[n].shape for n in small_names}
    unpacked = [_unpack_small(r, shapes) for r in res[:4]]
    for n in small_names:
        out[n] = [u[n] for u in unpacked]
    loss = res[4][0, 0]
    meta_whole = res[0][_META_ROW:_META_ROW + _META_ROWS].reshape(N_META, N_DEV, D_MODEL // N_DEV)
    g_meta_mine = lax.dynamic_index_in_dim(meta_whole, _slot(place, False), axis=1, keepdims=False)
    out["meta_tokens"] = [g_meta_mine, *_adamw_plain(meta_tokens, g_meta_mine, m_meta_tokens, v_meta_tokens,
                                                     "adamw_meta")]

    reduce_adamw("w_in", *_exchange_wait(*win_flight, after=all_small, name="w_in_grads_wait"))

    return (loss, grad_x[None], *[out[n][0] for n in _WEIGHTS], *[out[n][1] for n in _WEIGHTS],
            *[out[n][2] for n in _WEIGHTS], *[out[n][3] for n in _WEIGHTS])
```

```python
import functools
import math

import numpy as np
import jax
import jax.numpy as jnp
from jax import lax
from jax.experimental import pallas as pl
from jax.experimental.pallas import tpu as pltpu

F32 = jnp.float32
BF16 = jnp.bfloat16

D_MODEL = 1024
N_META = 16
BLOCK = 128
PAD = BLOCK - N_META
HG_HEADS = 4
HG_W = 512
ATT_HEADS = 8
HEAD_DIM = 64
ATT_QW = 512
ATT_KVW = 128
D_FF = 2816
EPS = 1e-5
ALPHA = 2.0 ** 0.25
ROPE_THETA = 10000.0
N_DEV = 8

ADAM_LR = 0.001
ADAM_B1 = 0.9
ADAM_B2 = 0.999
ADAM_EPS = 1e-08
ADAM_WD = 0.01
ADAM_STEP = 10

VMEM_LIMIT_BYTES = 56 * 1024 * 1024
MESH = pl.DeviceIdType.MESH

_LEVELS = (64, 32, 16, 8, 4, 2, 1)


def _cparams(sem):
    return pltpu.CompilerParams(dimension_semantics=sem, vmem_limit_bytes=VMEM_LIMIT_BYTES)


def _row_tile(rows, target):
    nb = rows // BLOCK
    best = 1
    for d in range(1, nb + 1):
        if nb % d == 0 and d * BLOCK <= target:
            best = d
    return best * BLOCK


_DN = {"nn": (((1,), (0,)), ((), ())), "nt": (((1,), (1,)), ((), ())), "tn": (((0,), (0,)), ((), ()))}


def _dot(a, b, form):
    return lax.dot_general(a.astype(BF16), b.astype(BF16), _DN[form], preferred_element_type=F32)


@functools.partial(jax.custom_vjp, nondiff_argnums=(2,))
def _mm(a, b, form):
    return _dot(a, b, form)


def _mm_fwd(a, b, form):
    a, b = a.astype(BF16), b.astype(BF16)
    return _dot(a, b, form), (a, b)


def _mm_bwd(form, res, g):
    a, b = res
    if form == "nn":
        return _dot(g, b, "nt"), _dot(a, g, "tn")
    if form == "nt":
        return _dot(g, b, "nn"), _dot(g, a, "tn")
    return _dot(b, g, "nt"), _dot(a, g, "nn")


_mm.defvjp(_mm_fwd, _mm_bwd)


def _split_dot(lv, x, form):
    return lax.dot_general(lv, x.astype(BF16), _DN[form], preferred_element_type=F32)


@jax.custom_vjp
def _swap_halves(x):
    return pltpu.roll(x, 64, 1)


_swap_halves.defvjp(lambda x: (pltpu.roll(x, 64, 1), None), lambda _, g: (pltpu.roll(g, 64, 1),))


def _tiled_matmul(a, b, form, *, tm, tn, tc, out_dtype, name):
    m, c = a.shape
    n = b.shape[1] if form == "nn" else b.shape[0]
    assert m % tm == 0 and n % tn == 0 and c % tc == 0, (name, a.shape, b.shape, tm, tn, tc)
    nc = c // tc

    def body(a_ref, b_ref, o_ref, *scratch):
        if nc == 1:
            o_ref[...] = _dot(a_ref[...], b_ref[...], form).astype(out_dtype)
            return
        acc_ref, = scratch
        ci = pl.program_id(2)

        @pl.when(ci == 0)
        def _():
            acc_ref[...] = jnp.zeros_like(acc_ref)

        acc_ref[...] += _dot(a_ref[...], b_ref[...], form)

        @pl.when(ci == nc - 1)
        def _():
            o_ref[...] = acc_ref[...].astype(out_dtype)

    b_spec = (pl.BlockSpec((tc, tn), lambda j, i, k: (k, j)) if form == "nn"
              else pl.BlockSpec((tn, tc), lambda j, i, k: (j, k)))
    return pl.pallas_call(
        body, name=name, grid=(n // tn, m // tm, nc),
        in_specs=[pl.BlockSpec((tm, tc), lambda j, i, k: (i, k)), b_spec],
        out_specs=pl.BlockSpec((tm, tn), lambda j, i, k: (i, j)),
        out_shape=jax.ShapeDtypeStruct((m, n), out_dtype),
        scratch_shapes=[] if nc == 1 else [pltpu.VMEM((tm, tn), F32)],
        compiler_params=_cparams(("arbitrary", "arbitrary", "arbitrary")),
    )(a, b)


def _tiled_matmul_tn(a, b, *, tm, tk, tn, out_dtype, name):
    m, k = a.shape
    n = b.shape[1]
    assert m % tm == 0 and k % tk == 0 and n % tn == 0, (name, a.shape, b.shape, tm, tk, tn)
    nm = m // tm

    def body(a_ref, b_ref, o_ref, acc_ref):
        mi = pl.program_id(2)

        @pl.when(mi == 0)
        def _():
            acc_ref[...] = jnp.zeros_like(acc_ref)

        acc_ref[...] += _dot(a_ref[...], b_ref[...], "tn")

        @pl.when(mi == nm - 1)
        def _():
            o_ref[...] = acc_ref[...].astype(out_dtype)

    return pl.pallas_call(
        body, name=name, grid=(k // tk, n // tn, nm),
        in_specs=[pl.BlockSpec((tm, tk), lambda kk, j, i: (i, kk)), pl.BlockSpec((tm, tn), lambda kk, j, i: (i, j))],
        out_specs=pl.BlockSpec((tk, tn), lambda kk, j, i: (kk, j)),
        out_shape=jax.ShapeDtypeStruct((k, n), out_dtype),
        scratch_shapes=[pltpu.VMEM((tk, tn), F32)],
        compiler_params=_cparams(("arbitrary", "arbitrary", "arbitrary")),
    )(a, b)


def _ln_stats(r):
    mu = jnp.mean(r, axis=-1, keepdims=True)
    xc = r - mu
    var = jnp.mean(xc * xc, axis=-1, keepdims=True)
    rstd = lax.rsqrt(var + EPS)
    return xc * rstd, rstd


def _ln_bwd(dy, xhat, rstd, g):
    dxhat = dy * g
    m1 = jnp.mean(dxhat, axis=-1, keepdims=True)
    m2 = jnp.mean(dxhat * xhat, axis=-1, keepdims=True)
    dr = rstd * (dxhat - m1 - xhat * m2)
    return dr, jnp.sum(dy * xhat, axis=0, keepdims=True), jnp.sum(dy, axis=0, keepdims=True)


N_SEG = 3 + len(_LEVELS)


def _level_stack():
    t = np.arange(BLOCK)[:, None]
    r = np.arange(BLOCK)[None, :]
    mats = [r <= t, r > t, np.ones((BLOCK, BLOCK), bool)]
    for h in _LEVELS:
        same = (t // (2 * h)) == (r // (2 * h))
        up_t, up_r = (t % (2 * h)) >= h, (r % (2 * h)) >= h
        mats.append(same & ((up_t & up_r & (r <= t)) | (~up_t & ~up_r & (r > t))))
    return jnp.asarray(np.concatenate(mats, axis=0).astype(np.float32), dtype=BF16)


def _hgrn_gates(hf, a0, a1, valid):
    lb = jax.nn.sigmoid(a0 - a1)
    fg = lb + (1.0 - lb) * jax.nn.sigmoid(hf)
    return jnp.where(valid, jnp.log(fg), 0.0), jnp.where(valid, 1.0 - fg, 0.0)


def _hgrn_scores(hq, k, *levels):
    q = jax.nn.silu(hq)
    rows = lax.broadcasted_iota(jnp.int32, (BLOCK, BLOCK), 0)
    cols = lax.broadcasted_iota(jnp.int32, (BLOCK, BLOCK), 1)
    a = jnp.where(rows == cols, jnp.sum(q * k, axis=-1, keepdims=True), 0.0)
    differ = jnp.bitwise_xor(rows, cols)
    for h, lvl in zip(_LEVELS, levels):
        decay = jnp.exp(lvl)
        pair = (cols < rows) & (differ >= h) & (differ < 2 * h)
        a = a + jnp.where(pair, _mm(q * decay, k * decay, "nt"), 0.0)
    return a


def _hgrn_mix(hq, k, v, st_in, a, seg_incl, seg_after, seg_total):
    o = _mm(jax.nn.silu(hq) * jnp.exp(seg_incl), st_in, "nt") + _mm(a, v, "nn")
    return o, st_in * jnp.exp(seg_total) + _mm(v, k * jnp.exp(seg_after), "tn")


def _hgrn_norm(o, hg, ng):
    return o * lax.rsqrt(jnp.mean(o * o, axis=-1, keepdims=True) + EPS) * ng * jax.nn.silu(hg)


def _seg_blocks(e, h):
    return [e[i * BLOCK:(i + 1) * BLOCK, h * BLOCK:(h + 1) * BLOCK] for i in range(N_SEG)]


def _rope(x, cos, sin, first_half):
    partner = jnp.where(first_half, -pltpu.roll(x, 96, 1), pltpu.roll(x, 32, 1))
    return x * cos + partner * sin


def _rope_t(g, cos, sin, first_half):
    u = g * sin
    partner = jnp.where(first_half, pltpu.roll(u, 96, 1), -pltpu.roll(u, 32, 1))
    return g * cos + partner


def _low_half(x):
    return lax.broadcasted_iota(jnp.int32, x.shape, 1) < HEAD_DIM


def _both_halves(x, g):
    sw = _swap_halves(x)
    return jnp.where(_low_half(x), x, sw) if g == 0 else jnp.where(_low_half(x), sw, x)


def _att_scores(qa, qb, kc, kp, km, g, own4, band4, meta4):
    low = _low_half(qa)
    q4 = jnp.concatenate([jnp.where(low, qa, 0.0), jnp.where(low, 0.0, qa),
                          jnp.where(low, qb, 0.0), jnp.where(low, 0.0, qb)], axis=0)
    scale = HEAD_DIM ** -0.5
    neg = jnp.finfo(F32).min
    s = jnp.where(own4, _mm(_both_halves(kc, g), q4, "nt"), _mm(_both_halves(kp, g), q4, "nt"))
    return (jnp.where(band4, s * scale, neg), jnp.where(meta4, _mm(_both_halves(km, g), q4, "nt") * scale, neg))


def _att_probs(s, sm, sinkrow):
    mx = jnp.maximum(jnp.maximum(jnp.max(s, axis=0, keepdims=True), jnp.max(sm, axis=0, keepdims=True)), sinkrow)
    p, pm, ps = jnp.exp(s - mx), jnp.exp(sm - mx), jnp.exp(sinkrow - mx)
    inv = 1.0 / (jnp.sum(p, axis=0, keepdims=True) + jnp.sum(pm, axis=0, keepdims=True) + ps)
    return p * inv, pm * inv, ps * inv


def _att_probs_bwd(p, pm, ps, dp, dpm):
    r = jnp.sum(p * dp, axis=0, keepdims=True) + jnp.sum(pm * dpm, axis=0, keepdims=True)
    return p * (dp - r), pm * (dpm - r), -ps * r


def _att_values(p, pm, vc, vp, vm, g, own4):
    o4 = (_mm(jnp.where(own4, p, 0.0), _both_halves(vc, g), "tn") + _mm(jnp.where(own4, 0.0, p), _both_halves(vp, g), "tn")
          + _mm(pm, _both_halves(vm, g), "tn"))
    tiles = []
    for j in range(2):
        upper = o4[(2 * j) * BLOCK:(2 * j + 1) * BLOCK]
        tiles.append(jnp.where(_low_half(upper), upper, o4[(2 * j + 1) * BLOCK:(2 * j + 2) * BLOCK]))
    return tiles


def _att_masks(blk_idx):
    kidx = lax.broadcasted_iota(jnp.int32, (BLOCK, BLOCK), 0)
    qrow = lax.broadcasted_iota(jnp.int32, (BLOCK, BLOCK), 1)
    own_side = kidx <= qrow
    pos_own = blk_idx * BLOCK + kidx - PAD
    ok_band = (own_side & (pos_own >= N_META)) | (~own_side & (pos_own - BLOCK >= N_META) & (blk_idx >= 1))
    qpos = blk_idx * BLOCK + lax.broadcasted_iota(jnp.int32, (N_META, BLOCK), 1) - PAD
    ok_meta = lax.broadcasted_iota(jnp.int32, (N_META, BLOCK), 0) <= qpos
    return [jnp.concatenate([m] * 4, axis=1) for m in (own_side, ok_band, ok_meta)]


def _token_streams(tr, tile_of=lambda i: i):
    k = tr // BLOCK
    return [pl.BlockSpec((BLOCK, D_MODEL), lambda i, j=j: (jnp.maximum(k * tile_of(i) - 1 + j, 0), 0))
            for j in range(k)]


def _embed_ln(x, meta_shard, w_in_shard, g0, b0):
    p = x.shape[0] + BLOCK
    tr = _row_tile(p, 640)
    k = tr // BLOCK
    nt = p // tr
    tile_of = lambda s: (s + 1) % nt
    shards = [meta_shard, w_in_shard]
    c_in, c_out, c_shapes, c_sems = _comm_specs(shards, N_DEV)

    def body(*refs):
        g_ref, b_ref = refs[k:k + 2]
        h_ref, hb_ref, xh_ref, rs_ref = refs[k + 4:k + 8]
        out_refs = refs[k + 8:k + 10]
        lead_ref, meta_ref = refs[k + 10:k + 12]
        starts, passes, waits = _gather_behind(refs[k + 2:k + 4], out_refs, refs[k + 12:], [False, False])
        s = pl.program_id(0)
        t = tile_of(s)

        @pl.when(s == 0)
        def _():
            lead_ref[...] = jnp.zeros_like(lead_ref)
            for start in starts:
                start()

        @pl.when(s == nt - 1)
        def _():
            for step in passes + waits:
                step()
            pltpu.sync_copy(out_refs[0], meta_ref)
            for d in range(N_DEV):
                lead_ref[PAD:BLOCK, d * BLOCK:(d + 1) * BLOCK] = meta_ref[d]

        first = jnp.where(t == 0, lead_ref[...], refs[0][...])
        xhat, rstd = _ln_stats(jnp.concatenate([first] + [r[...] for r in refs[1:k]], axis=0))
        row = t * tr + lax.broadcasted_iota(jnp.int32, (tr, 1), 0)
        h = jnp.where(row >= PAD, xhat * g_ref[...] + b_ref[...], 0.0)
        h_ref[...] = h
        hb_ref[...] = h.astype(BF16)
        xh_ref[...] = xhat
        rs_ref[...] = rstd

    vec = pl.BlockSpec((1, D_MODEL), lambda s: (0, 0))
    rowsp = pl.BlockSpec((tr, D_MODEL), lambda s: (tile_of(s), 0))
    return pl.pallas_call(
        body, name="embed_ln", grid=(nt,),
        in_specs=_token_streams(tr, tile_of) + [vec, vec] + c_in,
        out_specs=[rowsp, rowsp, rowsp, pl.BlockSpec((tr, 1), lambda s: (tile_of(s), 0))] + c_out,
        out_shape=[jax.ShapeDtypeStruct((p, D_MODEL), F32), jax.ShapeDtypeStruct((p, D_MODEL), BF16),
                   jax.ShapeDtypeStruct((p, D_MODEL), F32), jax.ShapeDtypeStruct((p, 1), F32)] + c_shapes,
        scratch_shapes=[pltpu.VMEM((BLOCK, D_MODEL), F32), pltpu.VMEM((N_DEV, N_META, BLOCK), F32)] + c_sems,
        compiler_params=_cparams(("arbitrary",)),
    )(*([x] * k), g0, b0, *shards)


def _rope_tables(p):
    pos = (np.arange(p, dtype=np.int32) - PAD).astype(np.float32)
    half = HEAD_DIM // 2
    inv = np.float32(ROPE_THETA) ** (-np.arange(half, dtype=np.float32) / np.float32(half))
    ang = pos[:, None] * np.tile(inv.astype(np.float32), BLOCK // half)[None, :]
    return jnp.asarray(np.cos(ang), F32), jnp.asarray(np.sin(ang), F32)


def _att_sinkrows(sink_ref):
    lanehead = lax.broadcasted_iota(jnp.int32, (1, 4 * BLOCK), 1) // BLOCK
    rows = []
    for g in range(2):
        row = jnp.zeros((1, 4 * BLOCK), F32)
        for j in range(4):
            row = jnp.where(lanehead == j, sink_ref[0, 4 * g + j], row)
        rows.append(row)
    return rows


def _first_half(rows):
    return (lax.broadcasted_iota(jnp.int32, (rows, BLOCK), 1) % HEAD_DIM) < (HEAD_DIM // 2)


def _att_load(qkv_ref, cos_ref, sin_ref, with_q):
    cos, sin, fh = cos_ref[...], sin_ref[...], _first_half(BLOCK)
    qs = [_rope(qkv_ref[:, j * BLOCK:(j + 1) * BLOCK], cos, sin, fh) for j in range(4)] if with_q else None
    k = _rope(qkv_ref[:, ATT_QW:ATT_QW + ATT_KVW], cos, sin, fh)
    v = qkv_ref[:, ATT_QW + ATT_KVW:ATT_QW + 2 * ATT_KVW]
    return qs, k, v


def _att_load_meta(qkv_ref, cos_ref, sin_ref):
    k = _rope(qkv_ref[PAD:BLOCK, ATT_QW:ATT_QW + ATT_KVW], cos_ref[PAD:BLOCK, :], sin_ref[PAD:BLOCK, :],
              _first_half(N_META))
    return k, qkv_ref[PAD:BLOCK, ATT_QW + ATT_KVW:ATT_QW + 2 * ATT_KVW]


def _att_specs(blk):
    w = ATT_QW + 2 * ATT_KVW
    cur = lambda width: pl.BlockSpec((BLOCK, width), lambda i: (blk(i), 0))
    prev = lambda width: pl.BlockSpec((BLOCK, width), lambda i: (jnp.maximum(blk(i) - 1, 0), 0))
    meta = lambda width: pl.BlockSpec((BLOCK, width), lambda i: (0, 0))
    return [cur(w), prev(w), meta(w), cur(BLOCK), cur(BLOCK), prev(BLOCK), prev(BLOCK), meta(BLOCK), meta(BLOCK),
            pl.BlockSpec(memory_space=pltpu.SMEM)]


_FLIPS = [(dx, dy, dc) for dx in (0, 1) for dy in (0, 1) for dc in (0, 1)][1:]
N_PEERS = len(_FLIPS)


def _place():
    return lax.axis_index("x"), lax.axis_index("y"), lax.axis_index("c")


def _peer(place, flip):
    return tuple(1 - p if f else p for p, f in zip(place, flip))


def _slot(place, swapped):
    x, y, c = place
    return 4 * y + 2 * x + c if swapped else 4 * x + 2 * y + c


def _comm_specs(arrs, out_lead):
    n = len(arrs)
    outs = [jax.ShapeDtypeStruct((out_lead,) + a.shape[-2:], a.dtype) for a in arrs]
    sems = [pltpu.SemaphoreType.DMA((n, N_PEERS)), pltpu.SemaphoreType.DMA((n, N_PEERS)), pltpu.SemaphoreType.DMA((n,))]
    return [pl.BlockSpec(memory_space=pl.ANY)] * n, [pl.BlockSpec(memory_space=pl.ANY)] * n, outs, sems


def _gather_behind(shard_refs, out_refs, sems, swapped):
    send_sems, recv_sems, local_sems = sems
    x, y, c = _place()
    me, sibling = (x, y, c), (x, y, 1 - c)
    chips = [(1 - x, y), (x, 1 - y), (1 - x, 1 - y)]
    starts, passes, waits = [], [], []
    for w, (s, o) in enumerate(zip(shard_refs, out_refs)):
        def copy(k, block, to, from_shard=False, w=w, s=s, o=o):
            rows = o.at[_slot(block, swapped[w])]
            return pltpu.make_async_remote_copy(
                src_ref=s if from_shard else rows, dst_ref=rows, send_sem=send_sems.at[w, k],
                recv_sem=recv_sems.at[w, k], device_id=to, device_id_type=MESH)

        own = pltpu.make_async_copy(s, o.at[_slot(me, swapped[w])], local_sems.at[w])
        first = [copy(0, me, sibling, True)] + [copy(1 + j, me, (*chip, c), True) for j, chip in enumerate(chips)]
        handed = [copy(4 + j, (*chip, c), sibling) for j, chip in enumerate(chips)]
        starts += [own.start] + [cp.start for cp in first]
        for j, chip in enumerate(chips):
            passes += [copy(1 + j, (*chip, c), me).wait_recv, handed[j].start]
        waits.append(copy(0, sibling, me).wait_recv)
        waits += [copy(4 + j, (*chip, 1 - c), me).wait_recv for j, chip in enumerate(chips)]
        waits += [cp.wait_send for cp in first + handed] + [own.wait]
    return starts, passes, waits


def _scatter_behind(part_refs, recv_refs, sems, swapped):
    send_sems, recv_sems, _ = sems
    place = _place()
    starts, waits = [], []
    for w, (p, o) in enumerate(zip(part_refs, recv_refs)):
        for r, flip in enumerate(_FLIPS):
            peer = _peer(place, flip)
            cp = pltpu.make_async_remote_copy(
                src_ref=p.at[_slot(peer, swapped[w])], dst_ref=o.at[r], send_sem=send_sems.at[w, r],
                recv_sem=recv_sems.at[w, r], device_id=peer, device_id_type=MESH)
            starts.append(cp.start)
            waits += [cp.wait_recv, cp.wait_send]
    return starts, waits


def _mixers_fwd(proj_hg, proj_att, lbounds, norm_g, lv, cos, sin, sinks, shards, swapped):
    p = proj_hg.shape[0]
    nb = p // BLOCK
    n = len(shards)
    c_in, c_out, c_shapes, c_sems = _comm_specs(shards, N_DEV)
    pass_step = min(nb - 1, max(1, (5 * nb) // 8))

    def body(*refs):
        x_ref, lb_ref, ng_ref, lv_ref, cur_ref, prev_ref, meta_ref, cc, sc, cp, sp, cm, sm, sink_ref = refs[:14]
        shard_refs = refs[14:14 + n]
        y_ref, o_ref, st_ref, a_ref, raw_ref, pr_ref = refs[14 + n:20 + n]
        out_refs = refs[20 + n:20 + 2 * n]
        carry_ref = refs[20 + 2 * n]
        starts, passes, waits = _gather_behind(shard_refs, out_refs, refs[21 + 2 * n:], swapped)
        c = pl.program_id(0)

        @pl.when(c == 0)
        def _():
            carry_ref[...] = jnp.zeros_like(carry_ref)
            for start in starts:
                start()

        @pl.when(c == pass_step)
        def _():
            for step in passes:
                step()

        valid = (c * BLOCK + lax.broadcasted_iota(jnp.int32, (BLOCK, 1), 0)) >= PAD
        logf, k = _hgrn_gates(x_ref[:, HG_W:2 * HG_W], lb_ref[0:1, :], lb_ref[1:2, :], valid)
        e = _split_dot(lv_ref[...], logf, "nn")
        for h in range(HG_HEADS):
            sl = lambda part: x_ref[:, part * HG_W + h * BLOCK: part * HG_W + (h + 1) * BLOCK]
            hs = slice(h * BLOCK, (h + 1) * BLOCK)
            st_in = carry_ref[h]
            st_ref[0, h] = st_in
            seg = _seg_blocks(e, h)
            a = _hgrn_scores(sl(0), k[:, hs], *seg[3:])
            a_ref[0, h] = a.astype(BF16)
            raw, st_out = _hgrn_mix(sl(0), k[:, hs], sl(2), st_in, a, *seg[:3])
            raw_ref[:, hs] = raw
            y_ref[:, hs] = _hgrn_norm(raw, sl(3), ng_ref[...]).astype(BF16)
            carry_ref[h] = st_out

        qs, kc, vc = _att_load(cur_ref, cc, sc, True)
        _, kp, vp = _att_load(prev_ref, cp, sp, False)
        km, vm = _att_load_meta(meta_ref, cm, sm)
        sinkrows = _att_sinkrows(sink_ref)
        own4, band4, meta4 = _att_masks(c)
        for g in range(2):
            s, s_meta = _att_scores(qs[2 * g], qs[2 * g + 1], kc, kp, km, g, own4, band4, meta4)
            pr, pr_meta, pr_sink = _att_probs(s, s_meta, sinkrows[g])
            pr_ref[0, g, :BLOCK, :] = pr.astype(BF16)
            pr_ref[0, g, BLOCK:BLOCK + N_META, :] = pr_meta.astype(BF16)
            pr_ref[0, g, BLOCK + N_META:, :] = jnp.broadcast_to(pr_sink, (N_META, 4 * BLOCK)).astype(BF16)
            for j, tile in enumerate(_att_values(pr, pr_meta, vc, vp, vm, g, own4)):
                o_ref[:, (2 * g + j) * BLOCK:(2 * g + j + 1) * BLOCK] = tile.astype(BF16)

        @pl.when(c == nb - 1)
        def _():
            for wait in waits:
                wait()

    return pl.pallas_call(
        body, name="mixers_fwd", grid=(nb,),
        in_specs=[pl.BlockSpec((BLOCK, 4 * HG_W), lambda c: (c, 0)), pl.BlockSpec((2, HG_W), lambda c: (0, 0)),
                  pl.BlockSpec((1, BLOCK), lambda c: (0, 0)), pl.BlockSpec(lv.shape, lambda c: (0, 0))]
        + _att_specs(lambda c: c) + c_in,
        out_specs=[pl.BlockSpec((BLOCK, HG_W), lambda c: (c, 0)), pl.BlockSpec((BLOCK, ATT_QW), lambda c: (c, 0)),
                   pl.BlockSpec((1, HG_HEADS, BLOCK, BLOCK), lambda c: (c, 0, 0, 0)),
                   pl.BlockSpec((1, HG_HEADS, BLOCK, BLOCK), lambda c: (c, 0, 0, 0)),
                   pl.BlockSpec((BLOCK, HG_W), lambda c: (c, 0)),
                   pl.BlockSpec((1, 2, ATT_KEYS, 4 * BLOCK), lambda c: (c, 0, 0, 0))] + c_out,
        out_shape=[jax.ShapeDtypeStruct((p, HG_W), BF16), jax.ShapeDtypeStruct((p, ATT_QW), BF16),
                   jax.ShapeDtypeStruct((nb, HG_HEADS, BLOCK, BLOCK), F32),
                   jax.ShapeDtypeStruct((nb, HG_HEADS, BLOCK, BLOCK), BF16),
                   jax.ShapeDtypeStruct((p, HG_W), F32),
                   jax.ShapeDtypeStruct((nb, 2, ATT_KEYS, 4 * BLOCK), BF16)] + c_shapes,
        scratch_shapes=[pltpu.VMEM((HG_HEADS, BLOCK, BLOCK), F32)] + c_sems,
        compiler_params=_cparams(("arbitrary",)),
    )(proj_hg, lbounds, norm_g, lv, proj_att, proj_att, proj_att, cos, sin, cos, sin, cos, sin, sinks, *shards)


def _tile(rows, preferred):
    return preferred if rows % preferred == 0 else _row_tile(rows, preferred)


def _branch_mix(yh, oa, gates, w_bh, w_ba):
    y_hg = _dot(yh, w_bh, "nn")
    y_att = _dot(oa, w_ba, "nn")
    s1 = jax.nn.sigmoid(gates[:, :D_MODEL].astype(F32))
    s2 = jax.nn.sigmoid(gates[:, D_MODEL:].astype(F32))
    return s1 * y_hg + s2 * y_att, y_hg, y_att, s1, s2


def _mix_out_ln1(yh, oa, gates, h0, w_bh, w_ba, w_out, g1, b1):
    p = yh.shape[0]
    tr = _tile(p, 320)

    def body(yh_ref, oa_ref, g_ref, h0_ref, wbh_ref, wba_ref, wo_ref, g1_ref, b1_ref,
             mix_ref, h1_ref, h1b_ref, xh_ref, rs_ref):
        mixin = _branch_mix(yh_ref[...], oa_ref[...], g_ref[...], wbh_ref[...], wba_ref[...])[0]
        mix_ref[...] = mixin.astype(BF16)
        xhat, rstd = _ln_stats(ALPHA * h0_ref[...] + _dot(mixin, wo_ref[...], "nn"))
        h1 = xhat * g1_ref[...] + b1_ref[...]
        h1_ref[...] = h1
        h1b_ref[...] = h1.astype(BF16)
        xh_ref[...] = xhat
        rs_ref[...] = rstd

    row = lambda w: pl.BlockSpec((tr, w), lambda i: (i, 0))
    const = lambda a: pl.BlockSpec(a.shape, lambda i: (0, 0))
    return pl.pallas_call(
        body, name="mix_out_ln1", grid=(p // tr,),
        in_specs=[row(HG_W), row(ATT_QW), row(2 * D_MODEL), row(D_MODEL), const(w_bh), const(w_ba), const(w_out),
                  const(g1), const(b1)],
        out_specs=[row(D_MODEL), row(D_MODEL), row(D_MODEL), row(D_MODEL), row(1)],
        out_shape=[jax.ShapeDtypeStruct((p, D_MODEL), BF16), jax.ShapeDtypeStruct((p, D_MODEL), F32),
                   jax.ShapeDtypeStruct((p, D_MODEL), BF16), jax.ShapeDtypeStruct((p, D_MODEL), F32),
                   jax.ShapeDtypeStruct((p, 1), F32)],
        compiler_params=_cparams(("arbitrary",)),
    )(yh, oa, gates, h0, w_bh, w_ba, w_out, g1, b1)


FF_T = D_FF // 2


def _ffn_in_swiglu(h1, w_fi):
    p = h1.shape[0]
    tm = _row_tile(p, 640)

    def body(h_ref, w_ref, au_ref, s_ref):
        au = _dot(h_ref[...], w_ref[...], "nn")
        au_ref[...] = au.astype(BF16)
        s_ref[...] = (jax.nn.silu(au[:, :FF_T]) * au[:, FF_T:]).astype(BF16)

    return pl.pallas_call(
        body, name="ffn_in_swiglu", grid=(D_FF // FF_T, p // tm),
        in_specs=[pl.BlockSpec((tm, D_MODEL), lambda j, i: (i, 0)), pl.BlockSpec((D_MODEL, 2 * FF_T), lambda j, i: (0, j))],
        out_specs=[pl.BlockSpec((tm, 2 * FF_T), lambda j, i: (i, j)), pl.BlockSpec((tm, FF_T), lambda j, i: (i, j))],
        out_shape=[jax.ShapeDtypeStruct((p, 2 * D_FF), BF16), jax.ShapeDtypeStruct((p, D_FF), BF16)],
        compiler_params=_cparams(("arbitrary", "arbitrary")),
    )(h1, w_fi)


def _ffn_out_loss(s, w_fo, h1, g2, b2, target):
    p = h1.shape[0]
    tr = _row_tile(p, 640)
    k = tr // BLOCK

    def body(*refs):
        s_ref, w_ref, h_ref, g_ref, b_ref = refs[:5]
        dr_ref, loss_ref, dg_ref, db_ref = refs[5 + k:]
        i = pl.program_id(0)
        xhat, rstd = _ln_stats(ALPHA * h_ref[...] + _dot(s_ref[...], w_ref[...], "nn"))
        y = xhat * g_ref[...] + b_ref[...]
        row = i * tr + lax.broadcasted_iota(jnp.int32, (tr, 1), 0)
        tgt = jnp.concatenate([r[...] for r in refs[5:5 + k]], axis=0)
        err = jnp.where(row >= BLOCK, y - tgt, 0.0)
        dr, dg, db = _ln_bwd(err * (1.0 / D_MODEL), xhat, rstd, g_ref[...])
        dr_ref[...] = dr
        e2 = jnp.sum(err * err, axis=0, keepdims=True)
        part = e2[:, 0:BLOCK]
        for j in range(1, D_MODEL // BLOCK):
            part = part + e2[:, j * BLOCK:(j + 1) * BLOCK]
        part = part * (0.5 / D_MODEL)

        @pl.when(i == 0)
        def _():
            loss_ref[...] = part
            dg_ref[...] = dg
            db_ref[...] = db

        @pl.when(i > 0)
        def _():
            loss_ref[...] += part
            dg_ref[...] += dg
            db_ref[...] += db

    vec = pl.BlockSpec((1, D_MODEL), lambda i: (0, 0))
    rowsp = pl.BlockSpec((tr, D_MODEL), lambda i: (i, 0))
    return pl.pallas_call(
        body, name="ffn_out_loss", grid=(p // tr,),
        in_specs=[pl.BlockSpec((tr, D_FF), lambda i: (i, 0)), pl.BlockSpec((D_FF, D_MODEL), lambda i: (0, 0)),
                  rowsp, vec, vec] + _token_streams(tr),
        out_specs=[rowsp, pl.BlockSpec((1, BLOCK), lambda i: (0, 0)), vec, vec],
        out_shape=[jax.ShapeDtypeStruct((p, D_MODEL), F32), jax.ShapeDtypeStruct((1, BLOCK), F32),
                   jax.ShapeDtypeStruct((1, D_MODEL), F32), jax.ShapeDtypeStruct((1, D_MODEL), F32)],
        compiler_params=_cparams(("arbitrary",)),
    )(s, w_fo, h1, g2, b2, *([target] * k))


def _ffn_bwd(dr2, w_fo, au, w_fi):
    p = au.shape[0]
    tm = _tile(p, 320)

    def body(d_ref, wo_ref, au_ref, wi_ref, dau_ref, dh_ref):
        d = d_ref[...]
        dh = 0.0
        for j in range(D_FF // FF_T):
            a_cols = slice(2 * j * FF_T, (2 * j + 1) * FF_T)
            u_cols = slice((2 * j + 1) * FF_T, (2 * j + 2) * FF_T)
            ds = _dot(d, wo_ref[j * FF_T:(j + 1) * FF_T, :], "nt")
            _, vjp = jax.vjp(lambda a, u: jax.nn.silu(a) * u, au_ref[:, a_cols].astype(F32), au_ref[:, u_cols].astype(F32))
            da, du = vjp(ds)
            da, du = da.astype(BF16), du.astype(BF16)
            dau_ref[:, a_cols] = da
            dau_ref[:, u_cols] = du
            dh = dh + _dot(da, wi_ref[:, a_cols], "nt") + _dot(du, wi_ref[:, u_cols], "nt")
        dh_ref[...] = dh

    row = lambda w: pl.BlockSpec((tm, w), lambda i: (i, 0))
    kept = lambda a: pl.BlockSpec(a.shape, lambda i: (0, 0), pipeline_mode=pl.Buffered(1))
    return pl.pallas_call(
        body, name="ffn_bwd", grid=(p // tm,),
        in_specs=[row(D_MODEL), kept(w_fo), row(2 * D_FF), kept(w_fi)],
        out_specs=[row(2 * D_FF), row(D_MODEL)],
        out_shape=[jax.ShapeDtypeStruct((p, 2 * D_FF), BF16), jax.ShapeDtypeStruct((p, D_MODEL), F32)],
        compiler_params=_cparams(("arbitrary",)),
    )(dr2, w_fo, au, w_fi)


def _ln1_mix_bwd(dr2, dh1_ffn, xhat1, rstd1, g1, yh, oa, gates, w_bh, w_ba, w_out):
    p = yh.shape[0]
    tr = _tile(p, 320)

    def body(a_ref, b_ref, xh_ref, rs_ref, g1_ref, yh_ref, oa_ref, g_ref, wbh_ref, wba_ref, wo_ref,
             dr_ref, dyhg_ref, dyat_ref, dgt_ref, dyh_ref, doa_ref, dg_ref, db_ref):
        i = pl.program_id(0)
        dr, dg, db = _ln_bwd(ALPHA * a_ref[...] + b_ref[...], xh_ref[...], rs_ref[...], g1_ref[...])
        dr_ref[...] = dr
        d = _dot(dr, wo_ref[...], "nt")
        _, y_hg, y_att, s1, s2 = _branch_mix(yh_ref[...], oa_ref[...], g_ref[...], wbh_ref[...], wba_ref[...])
        dy_hg = d * s1
        dy_att = d * s2
        dyhg_ref[...] = dy_hg.astype(BF16)
        dyat_ref[...] = dy_att.astype(BF16)
        dgt_ref[:, :D_MODEL] = (d * y_hg * s1 * (1.0 - s1)).astype(BF16)
        dgt_ref[:, D_MODEL:] = (d * y_att * s2 * (1.0 - s2)).astype(BF16)
        dyh_ref[...] = _dot(dy_hg, wbh_ref[...], "nt")
        doa_ref[...] = _dot(dy_att, wba_ref[...], "nt")

        @pl.when(i == 0)
        def _():
            dg_ref[...] = dg
            db_ref[...] = db

        @pl.when(i > 0)
        def _():
            dg_ref[...] += dg
            db_ref[...] += db

    row = lambda w: pl.BlockSpec((tr, w), lambda i: (i, 0))
    const = lambda a: pl.BlockSpec(a.shape, lambda i: (0, 0))
    vec = pl.BlockSpec((1, D_MODEL), lambda i: (0, 0))
    return pl.pallas_call(
        body, name="ln1_mix_bwd", grid=(p // tr,),
        in_specs=[row(D_MODEL), row(D_MODEL), row(D_MODEL), row(1), vec, row(HG_W), row(ATT_QW), row(2 * D_MODEL),
                  const(w_bh), const(w_ba), const(w_out)],
        out_specs=[row(D_MODEL), row(D_MODEL), row(D_MODEL), row(2 * D_MODEL), row(HG_W), row(ATT_QW), vec, vec],
        out_shape=[jax.ShapeDtypeStruct((p, D_MODEL), F32), jax.ShapeDtypeStruct((p, D_MODEL), BF16),
                   jax.ShapeDtypeStruct((p, D_MODEL), BF16), jax.ShapeDtypeStruct((p, 2 * D_MODEL), BF16),
                   jax.ShapeDtypeStruct((p, HG_W), F32), jax.ShapeDtypeStruct((p, ATT_QW), F32),
                   jax.ShapeDtypeStruct((1, D_MODEL), F32), jax.ShapeDtypeStruct((1, D_MODEL), F32)],
        compiler_params=_cparams(("arbitrary",)),
    )(dr2, dh1_ffn, xhat1, rstd1, g1, yh, oa, gates, w_bh, w_ba, w_out)


MIX_W = 4 * HG_W + ATT_QW + 2 * ATT_KVW
ATT_KEYS = BLOCK + 2 * N_META


def _mixers_bwd(proj_hg, proj_att, lbounds, norm_g, lv, states, scores, raw, probs, cos, sin, sinks, dyh, doa,
                parts, swapped):
    p = proj_hg.shape[0]
    nb = p // BLOCK
    n = len(parts)
    kvw = 2 * ATT_KVW
    rev = lambda s: nb - 1 - s
    c_in, c_out, c_shapes, c_sems = _comm_specs(parts, N_PEERS)

    def body(*refs):
        (x_ref, lb_ref, ng_ref, lv_ref, st_ref, a_ref, raw_ref, pr_ref, cur_ref, prev_ref, meta_ref, cc, sc, cp, sp,
         cm, sm, sink_ref, dy_ref, do_ref) = refs[:20]
        part_refs = refs[20:20 + n]
        dx_ref, dlb_ref, dng_ref, dsink_ref = refs[20 + n:24 + n]
        recv_refs = refs[24 + n:24 + 2 * n]
        dcarry_ref, dkv_next_ref, dkv_meta_ref = refs[24 + 2 * n:27 + 2 * n]
        starts, waits = _scatter_behind(part_refs, recv_refs, refs[27 + 2 * n:], swapped)
        step = pl.program_id(0)
        c = rev(step)

        @pl.when(step == 0)
        def _():
            dcarry_ref[...] = jnp.zeros_like(dcarry_ref)
            dkv_next_ref[...] = jnp.zeros_like(dkv_next_ref)
            dkv_meta_ref[...] = jnp.zeros_like(dkv_meta_ref)
            dlb_ref[...] = jnp.zeros_like(dlb_ref)
            dng_ref[...] = jnp.zeros_like(dng_ref)
            dsink_ref[...] = jnp.zeros_like(dsink_ref)
            for start in starts:
                start()

        fh = _first_half(BLOCK)
        qs, kc, vc = _att_load(cur_ref, cc, sc, True)
        _, kp, vp = _att_load(prev_ref, cp, sp, False)
        km, vm = _att_load_meta(meta_ref, cm, sm)
        own4, band4, meta4 = _att_masks(c)
        att0 = 4 * HG_W
        dkm = dkp = dkc = dvm = dvp = dvc = 0.0
        dsinkrows = []
        for g in range(2):
            pr = pr_ref[0, g, :BLOCK, :].astype(F32)
            pr_meta = pr_ref[0, g, BLOCK:BLOCK + N_META, :].astype(F32)
            pr_sink = jnp.max(pr_ref[0, g, BLOCK + N_META:, :].astype(F32), axis=0, keepdims=True)
            _, values_vjp = jax.vjp(lambda *a, g=g: _att_values(*a, g, own4), pr, pr_meta, vc, vp, vm)
            dpr, dpr_meta, dvc_g, dvp_g, dvm_g = values_vjp(
                [do_ref[:, (2 * g + j) * BLOCK:(2 * g + j + 1) * BLOCK] for j in range(2)])
            ds, ds_meta, dsinkrow = _att_probs_bwd(pr, pr_meta, pr_sink, dpr, dpr_meta)
            _, scores_vjp = jax.vjp(lambda *a, g=g: _att_scores(*a, g, own4, band4, meta4),
                                    qs[2 * g], qs[2 * g + 1], kc, kp, km)
            dqa, dqb, dkc_g, dkp_g, dkm_g = scores_vjp((ds, ds_meta))
            for j, dq in enumerate((dqa, dqb)):
                dx_ref[:, att0 + (2 * g + j) * BLOCK:att0 + (2 * g + j + 1) * BLOCK] = _rope_t(
                    dq, cc[...], sc[...], fh).astype(BF16)
            dkm, dkp, dkc = dkm + dkm_g, dkp + dkp_g, dkc + dkc_g
            dvm, dvp, dvc = dvm + dvm_g, dvp + dvp_g, dvc + dvc_g
            dsinkrows.append(dsinkrow)
        ds0, ds1 = dsinkrows
        dkv_meta_ref[:, :BLOCK] += _rope_t(dkm, cm[PAD:BLOCK, :], sm[PAD:BLOCK, :], _first_half(N_META))
        dkv_meta_ref[:, BLOCK:] += dvm
        last = jnp.where(c == 0, 1.0, 0.0)
        to_meta_rows = lambda m: jnp.concatenate([jnp.zeros((PAD, BLOCK), F32), last * m], axis=0)
        dk = _rope_t(dkc, cc[...], sc[...], fh) + dkv_next_ref[:, :BLOCK] + to_meta_rows(dkv_meta_ref[:, :BLOCK])
        dv = dvc + dkv_next_ref[:, BLOCK:] + to_meta_rows(dkv_meta_ref[:, BLOCK:])
        dx_ref[:, att0 + ATT_QW:att0 + ATT_QW + ATT_KVW] = dk.astype(BF16)
        dx_ref[:, att0 + ATT_QW + ATT_KVW:] = dv.astype(BF16)
        dkv_next_ref[:, :BLOCK] = _rope_t(dkp, cp[...], sp[...], fh)
        dkv_next_ref[:, BLOCK:] = dvp
        sink_rows = []
        for dsg in (ds0, ds1):
            for j in range(4):
                tot = jnp.sum(dsg[:, j * BLOCK:(j + 1) * BLOCK], axis=1, keepdims=True)
                sink_rows.append(jnp.broadcast_to(tot, (1, BLOCK)))
        dsink_ref[...] += jnp.concatenate(sink_rows, axis=0)

        valid = (c * BLOCK + lax.broadcasted_iota(jnp.int32, (BLOCK, 1), 0)) >= PAD
        (logf, k), gates_vjp = jax.vjp(lambda hf, a0, a1: _hgrn_gates(hf, a0, a1, valid),
                                       x_ref[:, HG_W:2 * HG_W], lb_ref[0:1, :], lb_ref[1:2, :])
        lvv = lv_ref[...]
        e = _split_dot(lvv, logf, "nn")
        dng = jnp.zeros((1, BLOCK), F32)
        dk, dseg = [], []
        for h in range(HG_HEADS):
            sl = lambda part: x_ref[:, part * HG_W + h * BLOCK: part * HG_W + (h + 1) * BLOCK]
            hs = slice(h * BLOCK, (h + 1) * BLOCK)
            seg = _seg_blocks(e, h)
            _, norm_vjp = jax.vjp(_hgrn_norm, raw_ref[:, hs], sl(3), ng_ref[...])
            draw, dhg, dngh = norm_vjp(dy_ref[:, hs])
            _, mix_vjp = jax.vjp(_hgrn_mix, sl(0), k[:, hs], sl(2), st_ref[0, h], a_ref[0, h].astype(F32), *seg[:3])
            dhq, dkh, dhi, dst, da, *dseg_mix = mix_vjp((draw, dcarry_ref[h]))
            _, scores_vjp = jax.vjp(_hgrn_scores, sl(0), k[:, hs], *seg[3:])
            dhq2, dkh2, *dseg_lvl = scores_vjp(da)
            for part, val in ((0, dhq + dhq2), (2, dhi), (3, dhg)):
                dx_ref[:, part * HG_W + h * BLOCK: part * HG_W + (h + 1) * BLOCK] = val.astype(BF16)
            dk.append(dkh + dkh2)
            dseg.append(jnp.concatenate(dseg_mix + dseg_lvl, axis=0))
            dng = dng + dngh
            dcarry_ref[h] = dst
        dlogf = _split_dot(lvv, jnp.concatenate(dseg, axis=1), "tn")
        dhf, da0, da1 = gates_vjp((dlogf, jnp.concatenate(dk, axis=1)))
        dx_ref[:, HG_W:2 * HG_W] = dhf.astype(BF16)
        dlb_ref[0:1, :] += da0
        dlb_ref[1:2, :] += da1
        dng_ref[...] += dng

        @pl.when(step == nb - 1)
        def _():
            for wait in waits:
                wait()

    const = lambda shape: pl.BlockSpec(shape, lambda s: (0,) * len(shape))
    per_head = pl.BlockSpec((1, HG_HEADS, BLOCK, BLOCK), lambda s: (rev(s), 0, 0, 0))
    return pl.pallas_call(
        body, name="mixers_bwd", grid=(nb,),
        in_specs=[pl.BlockSpec((BLOCK, 4 * HG_W), lambda s: (rev(s), 0)), const((2, HG_W)), const((1, BLOCK)),
                  const(lv.shape), per_head, per_head, pl.BlockSpec((BLOCK, HG_W), lambda s: (rev(s), 0)),
                  pl.BlockSpec((1, 2, ATT_KEYS, 4 * BLOCK), lambda s: (rev(s), 0, 0, 0))]
        + _att_specs(rev)
        + [pl.BlockSpec((BLOCK, HG_W), lambda s: (rev(s), 0)), pl.BlockSpec((BLOCK, ATT_QW), lambda s: (rev(s), 0))]
        + c_in,
        out_specs=[pl.BlockSpec((BLOCK, MIX_W), lambda s: (rev(s), 0)), const((2, HG_W)), const((1, BLOCK)),
                   const((ATT_HEADS, BLOCK))] + c_out,
        out_shape=[jax.ShapeDtypeStruct((p, MIX_W), BF16), jax.ShapeDtypeStruct((2, HG_W), F32),
                   jax.ShapeDtypeStruct((1, BLOCK), F32), jax.ShapeDtypeStruct((ATT_HEADS, BLOCK), F32)] + c_shapes,
        scratch_shapes=[pltpu.VMEM((HG_HEADS, BLOCK, BLOCK), F32), pltpu.VMEM((BLOCK, kvw), F32),
                        pltpu.VMEM((N_META, kvw), F32)] + c_sems,
        compiler_params=_cparams(("arbitrary",)),
    )(proj_hg, lbounds, norm_g, lv, states, scores, raw, probs, proj_att, proj_att, proj_att, cos, sin, cos, sin,
      cos, sin, sinks, dyh, doa, *parts)


_HBM = pl.BlockSpec(memory_space=pltpu.HBM)
_SEM = pl.BlockSpec(memory_space=pltpu.SEMAPHORE)
_ORDERED_BY_DATA = pltpu.CompilerParams(has_side_effects=pltpu.SideEffectType.DATAFLOW_SIDE_EFFECTING)


def _exchange_copies(part_ref, land_ref, send_sems, recv_sems):
    place = _place()
    return [pltpu.make_async_remote_copy(
        src_ref=part_ref.at[_slot(_peer(place, flip), False)], dst_ref=land_ref.at[r], send_sem=send_sems.at[r],
        recv_sem=recv_sems.at[r], device_id=_peer(place, flip), device_id_type=MESH) for r, flip in enumerate(_FLIPS)]


def _exchange_start(parts, name):
    def body(part_ref, land_ref, send_sems, recv_sems, part_thru, land_thru, token):
        for cp in _exchange_copies(part_ref, land_ref, send_sems, recv_sems):
            cp.start()
        token[...] = jnp.zeros_like(token)

    land = (N_PEERS,) + parts.shape[1:]
    return pl.pallas_call(
        body, name=name,
        out_shape=(pltpu.SemaphoreType.DMA((N_PEERS,)), pltpu.SemaphoreType.DMA((N_PEERS,)),
                   pltpu.HBM(parts.shape, parts.dtype), pltpu.HBM(land, parts.dtype), jax.ShapeDtypeStruct((8, BLOCK), F32)),
        in_specs=(_HBM, _HBM), out_specs=(_SEM, _SEM, _HBM, _HBM, pl.BlockSpec(memory_space=pltpu.VMEM)),
        input_output_aliases={0: 2, 1: 3}, compiler_params=_ORDERED_BY_DATA,
    )(pltpu.with_memory_space_constraint(parts, pltpu.HBM),
      pltpu.with_memory_space_constraint(lax.empty(land, parts.dtype), pltpu.HBM))


def _exchange_wait(send_sems, recv_sems, part_thru, land_thru, after, name):
    def body(part_ref, land_ref, send_sems, recv_sems, after_ref, part_out, land_out):
        for cp in _exchange_copies(part_ref, land_ref, send_sems, recv_sems):
            cp.wait_send()
            cp.wait_recv()

    return pl.pallas_call(
        body, name=name,
        out_shape=(pltpu.HBM(part_thru.shape, part_thru.dtype), pltpu.HBM(land_thru.shape, land_thru.dtype)),
        in_specs=(_HBM, _HBM, _SEM, _SEM, pl.BlockSpec(memory_space=pl.ANY)), out_specs=(_HBM, _HBM),
        input_output_aliases={0: 0, 1: 1}, compiler_params=_ORDERED_BY_DATA,
    )(part_thru, land_thru, send_sems, recv_sems, after)


def _embed_bwd(dmix, dgates, w_mix, w_gates, dr1, xhat0, rstd0, g0):
    p = dmix.shape[0]
    tm = _row_tile(p, 640)
    nm = p // tm

    def body(a_ref, g_ref, wa_ref, wg_ref, dr_ref, xh_ref, rs_ref, g0_ref, o_ref, dg_ref, db_ref):
        i = pl.program_id(0)
        dh0 = ALPHA * dr_ref[...] + _dot(a_ref[...], wa_ref[...], "nt") + _dot(g_ref[...], wg_ref[...], "nt")
        row = i * tm + lax.broadcasted_iota(jnp.int32, (tm, 1), 0)
        dx, dg, db = _ln_bwd(jnp.where(row >= PAD, dh0, 0.0), xh_ref[...], rs_ref[...], g0_ref[...])
        o_ref[...] = dx

        @pl.when(i == 0)
        def _():
            dg_ref[...] = dg
            db_ref[...] = db

        @pl.when(i > 0)
        def _():
            dg_ref[...] += dg
            db_ref[...] += db

    row = lambda w: pl.BlockSpec((tm, w), lambda i: (i, 0))
    const = lambda a: pl.BlockSpec(a.shape, lambda i: (0, 0))
    vec = pl.BlockSpec((1, D_MODEL), lambda i: (0, 0))
    return pl.pallas_call(
        body, name="embed_bwd", grid=(nm,),
        in_specs=[row(dmix.shape[1]), row(dgates.shape[1]), const(w_mix), const(w_gates), row(D_MODEL), row(D_MODEL),
                  row(1), vec],
        out_specs=[row(D_MODEL), vec, vec],
        out_shape=[jax.ShapeDtypeStruct((p, D_MODEL), F32), jax.ShapeDtypeStruct((1, D_MODEL), F32),
                   jax.ShapeDtypeStruct((1, D_MODEL), F32)],
        compiler_params=_cparams(("arbitrary",)),
    )(dmix, dgates, w_mix, w_gates, dr1, xhat0, rstd0, g0)


_LATE = ("w_branch_hg", "w_branch_attn", "w_out", "w_ffn_in", "w_ffn_out")
_COLUMN_SHARDED = ("meta_tokens", "w_in", "w_branch_hg", "w_branch_attn", "w_ffn_in")
_SWAPPED = ("w_ffn_in",)


def _whole(name, gathered):
    _, r, c = gathered.shape
    if name in _COLUMN_SHARDED:
        return jnp.transpose(gathered, (1, 0, 2)).reshape(r, N_DEV * c)
    return gathered.reshape(N_DEV * r, c)


def _slots(name, whole):
    r, c = whole.shape
    if name in _COLUMN_SHARDED:
        return jnp.transpose(whole.reshape(r, N_DEV, c // N_DEV), (1, 0, 2))
    return whole.reshape(N_DEV, r // N_DEV, c)


def _device_step(x, target, meta_shard, ln_emb_g, ln_emb_b, w_in_shard, lbounds, norm_g, sinks, late_shards,
                 ln1_g, ln1_b, ln2_g, ln2_b):
    s = x.shape[0]
    p = s + BLOCK
    tm = _row_tile(p, 640)
    lv = _level_stack()
    cos, sin = _rope_tables(p)
    hg_end = 4 * HG_W
    mm = functools.partial(_tiled_matmul, tm=tm)
    swapped = [n in _SWAPPED for n in _LATE]

    h0, h0b, xhat0, rstd0, _, g_win = _embed_ln(x, meta_shard, w_in_shard, ln_emb_g, ln_emb_b)
    w_in = _whole("w_in", g_win)
    proj_hg = mm(h0b, w_in[:, :hg_end], "nn", tn=hg_end, tc=D_MODEL, out_dtype=F32, name="proj_hg")
    proj_att = mm(h0b, w_in[:, hg_end:MIX_W], "nn", tn=MIX_W - hg_end, tc=D_MODEL, out_dtype=F32, name="proj_att")
    gates = mm(h0b, w_in[:, MIX_W:], "nn", tn=2 * D_MODEL, tc=D_MODEL, out_dtype=BF16, name="proj_gates")
    yh, oa, states, scores, raw, probs, *gathered = _mixers_fwd(
        proj_hg, proj_att, lbounds, norm_g, lv, cos, sin, sinks, late_shards, swapped)
    w_bh, w_ba, w_out, w_fi, w_fo = [_whole(n, g) for n, g in zip(_LATE, gathered)]
    mixin, h1, h1b, xhat1, rstd1 = _mix_out_ln1(yh, oa, gates, h0, w_bh, w_ba, w_out, ln1_g, ln1_b)
    au, sw = _ffn_in_swiglu(h1b, w_fi)
    dr2, loss_part, dg2, db2 = _ffn_out_loss(sw, w_fo, h1, ln2_g, ln2_b, target)

    mtn = functools.partial(_tiled_matmul_tn, tm=_row_tile(p, 1664), out_dtype=BF16)
    whole = functools.partial(_tiled_matmul_tn, tm=p, out_dtype=BF16)
    d_wfo = mtn(sw, dr2, tk=FF_T, tn=D_MODEL, name="grad_w_ffn_out")
    dau, dh1_ffn = _ffn_bwd(dr2, w_fo, au, w_fi)
    d_wfi = whole(h1b, dau, tk=D_MODEL // 2, tn=D_MODEL // 2, name="grad_w_ffn_in")
    dr1, dy_hg, dy_att, dgates, dyh, doa, dg1, db1 = _ln1_mix_bwd(
        dr2, dh1_ffn, xhat1, rstd1, ln1_g, yh, oa, gates, w_bh, w_ba, w_out)
    d_wout = whole(mixin, dr1, tk=D_MODEL // 2, tn=D_MODEL // 4, name="grad_w_out")
    d_wbh = whole(yh, dy_hg, tk=HG_W, tn=D_MODEL // 2, name="grad_w_branch_hg")
    d_wba = whole(oa, dy_att, tk=ATT_QW, tn=D_MODEL // 2, name="grad_w_branch_attn")
    late_parts = [_slots(n, g) for n, g in zip(_LATE, (d_wbh, d_wba, d_wout, d_wfi, d_wfo))]
    dmix, d_lb, d_ng, d_sink, *late_recv = _mixers_bwd(
        proj_hg, proj_att, lbounds, norm_g, lv, states, scores, raw, probs, cos, sin, sinks, dyh, doa, late_parts,
        swapped)
    d_win = jnp.concatenate([whole(h0b, dmix, tk=D_MODEL, tn=D_MODEL // 4, name="grad_w_in_mixers"),
                             whole(h0b, dgates, tk=D_MODEL, tn=D_MODEL // 4, name="grad_w_in_gates")], axis=1)
    *win_flight, token = _exchange_start(_slots("w_in", d_win), "w_in_grads_start")
    dxin, dg0, db0 = _embed_bwd(dmix, dgates, w_in[:, :MIX_W], w_in[:, MIX_W:], dr1, xhat0, rstd0,
                                ln_emb_g + token[0:1, 0:1])

    small = dict(ln_emb_g=dg0, ln_emb_b=db0, hg_lower_bounds=d_lb, hg_norm_g=d_ng, attn_sinks=d_sink[:, 0],
                 ln1_g=dg1, ln1_b=db1, ln2_g=dg2, ln2_b=db2)
    big = dict(zip(_LATE, zip(late_parts, late_recv)))
    return loss_part, dxin[BLOCK:], small, dxin[PAD:BLOCK], big, win_flight


def _all_gather(arrs, dtypes, name):
    n = len(arrs)

    def body(*refs):
        ins, outs, stages = refs[:n], refs[n:2 * n], refs[2 * n:3 * n]
        send_sems, recv_sems, local_sems = refs[3 * n:]
        x, y, c = _place()
        sibling = (x, y, 1 - c)
        chips = [(1 - x, y), (x, 1 - y), (1 - x, 1 - y)]
        slot = lambda px, py, pc: 4 * px + 2 * py + pc

        def copy(w, k, block, to, from_stage=False):
            return pltpu.make_async_remote_copy(
                src_ref=stages[w] if from_stage else outs[w].at[slot(*block)], dst_ref=outs[w].at[slot(*block)],
                send_sem=send_sems.at[w, k], recv_sem=recv_sems.at[w, k], device_id=to, device_id_type=MESH)

        mine, first, passed = [], [], []
        for w in range(n):
            stages[w][...] = ins[w][...].astype(dtypes[w])
            mine.append(pltpu.make_async_copy(stages[w], outs[w].at[slot(x, y, c)], local_sems.at[w]))
            mine[-1].start()
        for w in range(n):
            first.append(copy(w, 0, (x, y, c), sibling, from_stage=True))
            first += [copy(w, 1 + j, (x, y, c), (*chip, c), from_stage=True) for j, chip in enumerate(chips)]
        for cp in first:
            cp.start()
        for j, chip in enumerate(chips):
            for w in range(n):
                copy(w, 1 + j, (*chip, c), (x, y, c)).wait_recv()
                passed.append(copy(w, 4 + j, (*chip, c), sibling))
                passed[-1].start()
        for w in range(n):
            copy(w, 0, sibling, (x, y, c)).wait_recv()
            for j, chip in enumerate(chips):
                copy(w, 4 + j, (*chip, 1 - c), (x, y, c)).wait_recv()
        for cp in first + passed:
            cp.wait_send()
        for cp in mine:
            cp.wait()

    return pl.pallas_call(
        body, name=name,
        in_specs=[pl.BlockSpec(memory_space=pltpu.VMEM)] * n,
        out_specs=[pl.BlockSpec(memory_space=pl.ANY)] * n,
        out_shape=[jax.ShapeDtypeStruct((N_DEV,) + a.shape, dt) for a, dt in zip(arrs, dtypes)],
        scratch_shapes=[pltpu.VMEM(a.shape, dt) for a, dt in zip(arrs, dtypes)]
        + [pltpu.SemaphoreType.DMA((n, 7)), pltpu.SemaphoreType.DMA((n, 7)), pltpu.SemaphoreType.DMA((n,))],
        compiler_params=pltpu.CompilerParams(vmem_limit_bytes=VMEM_LIMIT_BYTES),
    )(*arrs)


def _cast_shards(arrs):
    def body(*refs):
        for src, dst in zip(refs[:len(arrs)], refs[len(arrs):]):
            dst[...] = src[...].astype(BF16)

    return pl.pallas_call(body, name="cast_shards", out_shape=[jax.ShapeDtypeStruct(a.shape, BF16) for a in arrs],
                          compiler_params=pltpu.CompilerParams(vmem_limit_bytes=VMEM_LIMIT_BYTES))(*arrs)


def _shard_rows(rows):
    return rows if rows <= 512 else 256


def _adamw_math(w, g, m, v):
    m = ADAM_B1 * m + (1.0 - ADAM_B1) * g
    v = ADAM_B2 * v + (1.0 - ADAM_B2) * (g * g)
    m_hat = m / (1.0 - ADAM_B1 ** ADAM_STEP)
    v_hat = v / (1.0 - ADAM_B2 ** ADAM_STEP)
    delta = -ADAM_LR * (m_hat / (jnp.sqrt(v_hat) + ADAM_EPS) + ADAM_WD * w)
    return delta, m, v


def _reduce_adamw(parts, recv, own_slot, w, m, v, name):
    r, cdim = w.shape
    tr = _shard_rows(r)

    def body(idx_ref, p_ref, r_ref, w_ref, m_ref, v_ref, g_out, d_out, m_out, v_out):
        g = p_ref[0].astype(F32)
        for j in range(N_PEERS):
            g = g + r_ref[j].astype(F32)
        d, mn, vn = _adamw_math(w_ref[...], g, m_ref[...], v_ref[...])
        g_out[...] = g
        d_out[...] = d
        m_out[...] = mn
        v_out[...] = vn

    flat = pl.BlockSpec((tr, cdim), lambda i, idx_ref: (i, 0))
    return pl.pallas_call(
        body, name=name,
        grid_spec=pltpu.PrefetchScalarGridSpec(
            num_scalar_prefetch=1, grid=(r // tr,),
            in_specs=[pl.BlockSpec((1, tr, cdim), lambda i, idx_ref: (idx_ref[0], i, 0)),
                      pl.BlockSpec((N_PEERS, tr, cdim), lambda i, idx_ref: (0, i, 0)), flat, flat, flat],
            out_specs=[flat] * 4),
        out_shape=[jax.ShapeDtypeStruct((r, cdim), F32)] * 4,
        compiler_params=_cparams(("arbitrary",)),
    )(own_slot, parts, recv, w, m, v)


def _adamw_plain(w, g, m, v, name):
    def body(w_ref, g_ref, m_ref, v_ref, d_out, m_out, v_out):
        d_out[...], m_out[...], v_out[...] = _adamw_math(w_ref[...], g_ref[...], m_ref[...], v_ref[...])

    return pl.pallas_call(body, name=name, out_shape=[jax.ShapeDtypeStruct(w.shape, F32)] * 3)(w, g, m, v)


_SMALL_LAYOUT = (("ln_emb_g", 8), ("ln_emb_b", 8), ("hg_lower_bounds", 8), ("hg_norm_g", 1), ("attn_sinks", 1),
                 ("ln1_g", 8), ("ln1_b", 8), ("ln2_g", 8), ("ln2_b", 8))
_META_ROW = sum(r for _, r in _SMALL_LAYOUT)
_META_ROWS = N_META * D_MODEL // BLOCK
_LOSS_ROW = _META_ROW + _META_ROWS
SMALL_ROWS = 192


def _pack_small(vals, meta=None, loss_row=None):
    rows = []
    for name, nrows in _SMALL_LAYOUT:
        flat = vals[name].reshape(-1).astype(F32)
        flat = jnp.pad(flat, (0, nrows * BLOCK - flat.shape[0]))
        rows.append(flat.reshape(nrows, BLOCK))
    rows.append(jnp.zeros((_META_ROWS, BLOCK), F32) if meta is None else meta.reshape(_META_ROWS, BLOCK))
    rows.append(jnp.zeros((1, BLOCK), F32) if loss_row is None else loss_row)
    packed = jnp.concatenate(rows, axis=0)
    return jnp.pad(packed, ((0, SMALL_ROWS - packed.shape[0]), (0, 0)))


def _unpack_small(packed, shapes):
    out, row = {}, 0
    for name, nrows in _SMALL_LAYOUT:
        size = math.prod(shapes[name])
        out[name] = packed[row:row + nrows].reshape(-1)[:size].reshape(shapes[name])
        row += nrows
    return out


def _small_reduce_adamw(gathered, w, m, v):
    def body(g_ref, w_ref, m_ref, v_ref, g_out, d_out, m_out, v_out, loss_out):
        g = g_ref[0]
        for s in range(1, N_DEV):
            g = g + g_ref[s]
        d, mn, vn = _adamw_math(w_ref[...], g, m_ref[...], v_ref[...])
        g_out[...] = g
        d_out[...] = d
        m_out[...] = mn
        v_out[...] = vn
        loss_out[...] = jnp.broadcast_to(jnp.sum(g_ref[:, _LOSS_ROW, :]), (1, BLOCK))

    shp = jax.ShapeDtypeStruct((SMALL_ROWS, BLOCK), F32)
    return pl.pallas_call(body, name="small_reduce_adamw",
                          out_shape=[shp] * 4 + [jax.ShapeDtypeStruct((1, BLOCK), F32)])(gathered, w, m, v)


_WEIGHTS = ("meta_tokens", "ln_emb_g", "ln_emb_b", "w_in", "hg_lower_bounds", "hg_norm_g", "attn_sinks",
            "w_branch_hg", "w_branch_attn", "w_out", "ln1_g", "ln1_b", "w_ffn_in", "w_ffn_out", "ln2_g", "ln2_b")


def kernel(x, meta_tokens, ln_emb_g, ln_emb_b, w_in, hg_lower_bounds, hg_norm_g, attn_sinks, w_branch_hg, w_branch_attn, w_out, ln1_g, ln1_b, w_ffn_in, w_ffn_out, ln2_g, ln2_b, loss_target, m_meta_tokens, m_ln_emb_g, m_ln_emb_b, m_w_in, m_hg_lower_bounds, m_hg_norm_g, m_attn_sinks, m_w_branch_hg, m_w_branch_attn, m_w_out, m_ln1_g, m_ln1_b, m_w_ffn_in, m_w_ffn_out, m_ln2_g, m_ln2_b, v_meta_tokens, v_ln_emb_g, v_ln_emb_b, v_w_in, v_hg_lower_bounds, v_hg_norm_g, v_attn_sinks, v_w_branch_hg, v_w_branch_attn, v_w_out, v_ln1_g, v_ln1_b, v_w_ffn_in, v_w_ffn_out, v_ln2_g, v_ln2_b):
    given = dict(locals())
    weights = {n: given[n] for n in _WEIGHTS}
    mom1 = {n: given["m_" + n] for n in _WEIGHTS}
    mom2 = {n: given["v_" + n] for n in _WEIGHTS}
    shard2d = lambda a: a.reshape(a.shape[-2:])

    w_in_shard, *late_shards = _cast_shards([shard2d(weights[n]) for n in ("w_in",) + _LATE])
    loss_part, grad_x, small_grads, meta_grad, big, win_flight = _device_step(
        x[0], loss_target[0], meta_tokens, ln_emb_g.reshape(1, -1), ln_emb_b.reshape(1, -1), w_in_shard,
        hg_lower_bounds, hg_norm_g, attn_sinks, late_shards, ln1_g, ln1_b, ln2_g, ln2_b)

    place = _place()
    out = {}

    def reduce_adamw(n, parts, recv):
        own = _slot(place, n in _SWAPPED).astype(jnp.int32).reshape(1)
        res = _reduce_adamw(parts, recv, own, shard2d(weights[n]), shard2d(mom1[n]), shard2d(mom2[n]), "adamw_" + n)
        out[n] = [r.reshape(weights[n].shape) for r in res]

    for n, (parts, recv) in big.items():
        reduce_adamw(n, parts, recv)

    small_names = [n for n, _ in _SMALL_LAYOUT]
    packed = _pack_small(small_grads, meta_grad, loss_part)
    all_small, = _all_gather([packed], [F32], "gather_small")
    res = _small_reduce_adamw(all_small, _pack_small(weights), _pack_small(mom1), _pack_small(mom2))
    shapes = {n: weights[n].shape for n in small_names}
    unpacked = [_unpack_small(r, shapes) for r in res[:4]]
    for n in small_names:
        out[n] = [u[n] for u in unpacked]
    loss = res[4][0, 0]
    meta_whole = res[0][_META_ROW:_META_ROW + _META_ROWS].reshape(N_META, N_DEV, D_MODEL // N_DEV)
    g_meta_mine = lax.dynamic_index_in_dim(meta_whole, _slot(place, False), axis=1, keepdims=False)
    out["meta_tokens"] = [g_meta_mine, *_adamw_plain(meta_tokens, g_meta_mine, m_meta_tokens, v_meta_tokens,
                                                     "adamw_meta")]

    reduce_adamw("w_in", *_exchange_wait(*win_flight, after=all_small, name="w_in_grads_wait"))

    return (loss, grad_x[None], *[out[n][0] for n in _WEIGHTS], *[out[n][1] for n in _WEIGHTS],
            *[out[n][2] for n in _WEIGHTS], *[out[n][3] for n in _WEIGHTS])
```

```python
import functools
import math

import numpy as np
import jax
import jax.numpy as jnp
from jax import lax
from jax.experimental import pallas as pl
from jax.experimental.pallas import tpu as pltpu

F32 = jnp.float32
BF16 = jnp.bfloat16

D_MODEL = 1024
N_META = 16
BLOCK = 128
PAD = BLOCK - N_META
HG_HEADS = 4
HG_W = 512
ATT_HEADS = 8
HEAD_DIM = 64
ATT_QW = 512
ATT_KVW = 128
D_FF = 2816
EPS = 1e-5
ALPHA = 2.0 ** 0.25
ROPE_THETA = 10000.0
N_DEV = 8

ADAM_LR = 0.001
ADAM_B1 = 0.9
ADAM_B2 = 0.999
ADAM_EPS = 1e-08
ADAM_WD = 0.01
ADAM_STEP = 10

VMEM_LIMIT_BYTES = 56 * 1024 * 1024
MESH = pl.DeviceIdType.MESH

_LEVELS = (64, 32, 16, 8, 4, 2, 1)


def _cparams(sem):
    return pltpu.CompilerParams(dimension_semantics=sem, vmem_limit_bytes=VMEM_LIMIT_BYTES)


def _row_tile(rows, target):
    nb = rows // BLOCK
    best = 1
    for d in range(1, nb + 1):
        if nb % d == 0 and d * BLOCK <= target:
            best = d
    return best * BLOCK


_DN = {"nn": (((1,), (0,)), ((), ())), "nt": (((1,), (1,)), ((), ())), "tn": (((0,), (0,)), ((), ()))}


def _dot(a, b, form):
    return lax.dot_general(a.astype(BF16), b.astype(BF16), _DN[form], preferred_element_type=F32)


@functools.partial(jax.custom_vjp, nondiff_argnums=(2,))
def _mm(a, b, form):
    return _dot(a, b, form)


def _mm_fwd(a, b, form):
    a, b = a.astype(BF16), b.astype(BF16)
    return _dot(a, b, form), (a, b)


def _mm_bwd(form, res, g):
    a, b = res
    if form == "nn":
        return _dot(g, b, "nt"), _dot(a, g, "tn")
    if form == "nt":
        return _dot(g, b, "nn"), _dot(g, a, "tn")
    return _dot(b, g, "nt"), _dot(a, g, "nn")


_mm.defvjp(_mm_fwd, _mm_bwd)


def _split_dot(lv, x, form):
    return lax.dot_general(lv, x.astype(BF16), _DN[form], preferred_element_type=F32)


@jax.custom_vjp
def _swap_halves(x):
    return pltpu.roll(x, 64, 1)


_swap_halves.defvjp(lambda x: (pltpu.roll(x, 64, 1), None), lambda _, g: (pltpu.roll(g, 64, 1),))


def _tiled_matmul_tn(a, b, *, tm, tk, tn, out_dtype, name):
    m, k = a.shape
    n = b.shape[1]
    assert m % tm == 0 and k % tk == 0 and n % tn == 0, (name, a.shape, b.shape, tm, tk, tn)
    nm = m // tm

    def body(a_ref, b_ref, o_ref, acc_ref):
        mi = pl.program_id(2)

        @pl.when(mi == 0)
        def _():
            acc_ref[...] = jnp.zeros_like(acc_ref)

        acc_ref[...] += _dot(a_ref[...], b_ref[...], "tn")

        @pl.when(mi == nm - 1)
        def _():
            o_ref[...] = acc_ref[...].astype(out_dtype)

    return pl.pallas_call(
        body, name=name, grid=(k // tk, n // tn, nm),
        in_specs=[pl.BlockSpec((tm, tk), lambda kk, j, i: (i, kk)), pl.BlockSpec((tm, tn), lambda kk, j, i: (i, j))],
        out_specs=pl.BlockSpec((tk, tn), lambda kk, j, i: (kk, j)),
        out_shape=jax.ShapeDtypeStruct((k, n), out_dtype),
        scratch_shapes=[pltpu.VMEM((tk, tn), F32)],
        compiler_params=_cparams(("arbitrary", "arbitrary", "arbitrary")),
    )(a, b)


def _ln_stats(r):
    mu = jnp.mean(r, axis=-1, keepdims=True)
    xc = r - mu
    var = jnp.mean(xc * xc, axis=-1, keepdims=True)
    rstd = lax.rsqrt(var + EPS)
    return xc * rstd, rstd


def _ln_bwd(dy, xhat, rstd, g):
    dxhat = dy * g
    m1 = jnp.mean(dxhat, axis=-1, keepdims=True)
    m2 = jnp.mean(dxhat * xhat, axis=-1, keepdims=True)
    dr = rstd * (dxhat - m1 - xhat * m2)
    return dr, jnp.sum(dy * xhat, axis=0, keepdims=True), jnp.sum(dy, axis=0, keepdims=True)


N_SEG = 3 + len(_LEVELS)


def _level_stack():
    t = np.arange(BLOCK)[:, None]
    r = np.arange(BLOCK)[None, :]
    mats = [r <= t, r > t, np.ones((BLOCK, BLOCK), bool)]
    for h in _LEVELS:
        same = (t // (2 * h)) == (r // (2 * h))
        up_t, up_r = (t % (2 * h)) >= h, (r % (2 * h)) >= h
        mats.append(same & ((up_t & up_r & (r <= t)) | (~up_t & ~up_r & (r > t))))
    return jnp.asarray(np.concatenate(mats, axis=0).astype(np.float32), dtype=BF16)


def _hgrn_gates(hf, a0, a1, valid):
    lb = jax.nn.sigmoid(a0 - a1)
    fg = lb + (1.0 - lb) * jax.nn.sigmoid(hf)
    return jnp.where(valid, jnp.log(fg), 0.0), jnp.where(valid, 1.0 - fg, 0.0)


def _hgrn_scores(hq, k, *levels):
    q = jax.nn.silu(hq)
    rows = lax.broadcasted_iota(jnp.int32, (BLOCK, BLOCK), 0)
    cols = lax.broadcasted_iota(jnp.int32, (BLOCK, BLOCK), 1)
    a = jnp.where(rows == cols, jnp.sum(q * k, axis=-1, keepdims=True), 0.0)
    differ = jnp.bitwise_xor(rows, cols)
    for h, lvl in zip(_LEVELS, levels):
        decay = jnp.exp(lvl)
        pair = (cols < rows) & (differ >= h) & (differ < 2 * h)
        a = a + jnp.where(pair, _mm(q * decay, k * decay, "nt"), 0.0)
    return a


def _hgrn_mix(hq, k, v, st_in, a, seg_incl, seg_after, seg_total):
    o = _mm(jax.nn.silu(hq) * jnp.exp(seg_incl), st_in, "nt") + _mm(a, v, "nn")
    return o, st_in * jnp.exp(seg_total) + _mm(v, k * jnp.exp(seg_after), "tn")


def _hgrn_norm(o, hg, ng):
    return o * lax.rsqrt(jnp.mean(o * o, axis=-1, keepdims=True) + EPS) * ng * jax.nn.silu(hg)


def _seg_blocks(e, h):
    return [e[i * BLOCK:(i + 1) * BLOCK, h * BLOCK:(h + 1) * BLOCK] for i in range(N_SEG)]


def _rope(x, cos, sin, first_half):
    partner = jnp.where(first_half, -pltpu.roll(x, 96, 1), pltpu.roll(x, 32, 1))
    return x * cos + partner * sin


def _rope_t(g, cos, sin, first_half):
    u = g * sin
    partner = jnp.where(first_half, pltpu.roll(u, 96, 1), -pltpu.roll(u, 32, 1))
    return g * cos + partner


def _low_half(x):
    return lax.broadcasted_iota(jnp.int32, x.shape, 1) < HEAD_DIM


def _both_halves(x, g):
    sw = _swap_halves(x)
    return jnp.where(_low_half(x), x, sw) if g == 0 else jnp.where(_low_half(x), sw, x)


def _att_scores(qa, qb, kc, kp, km, g, own4, band4, meta4):
    low = _low_half(qa)
    q4 = jnp.concatenate([jnp.where(low, qa, 0.0), jnp.where(low, 0.0, qa),
                          jnp.where(low, qb, 0.0), jnp.where(low, 0.0, qb)], axis=0)
    scale = HEAD_DIM ** -0.5
    neg = jnp.finfo(F32).min
    s = jnp.where(own4, _mm(_both_halves(kc, g), q4, "nt"), _mm(_both_halves(kp, g), q4, "nt"))
    return (jnp.where(band4, s * scale, neg), jnp.where(meta4, _mm(_both_halves(km, g), q4, "nt") * scale, neg))


def _att_probs(s, sm, sinkrow):
    mx = jnp.maximum(jnp.maximum(jnp.max(s, axis=0, keepdims=True), jnp.max(sm, axis=0, keepdims=True)), sinkrow)
    p, pm, ps = jnp.exp(s - mx), jnp.exp(sm - mx), jnp.exp(sinkrow - mx)
    inv = 1.0 / (jnp.sum(p, axis=0, keepdims=True) + jnp.sum(pm, axis=0, keepdims=True) + ps)
    return p * inv, pm * inv, ps * inv


def _att_probs_bwd(p, pm, ps, dp, dpm):
    r = jnp.sum(p * dp, axis=0, keepdims=True) + jnp.sum(pm * dpm, axis=0, keepdims=True)
    return p * (dp - r), pm * (dpm - r), -ps * r


def _att_values(p, pm, vc, vp, vm, g, own4):
    o4 = (_mm(jnp.where(own4, p, 0.0), _both_halves(vc, g), "tn") + _mm(jnp.where(own4, 0.0, p), _both_halves(vp, g), "tn")
          + _mm(pm, _both_halves(vm, g), "tn"))
    tiles = []
    for j in range(2):
        upper = o4[(2 * j) * BLOCK:(2 * j + 1) * BLOCK]
        tiles.append(jnp.where(_low_half(upper), upper, o4[(2 * j + 1) * BLOCK:(2 * j + 2) * BLOCK]))
    return tiles


def _att_masks(blk_idx):
    kidx = lax.broadcasted_iota(jnp.int32, (BLOCK, BLOCK), 0)
    qrow = lax.broadcasted_iota(jnp.int32, (BLOCK, BLOCK), 1)
    own_side = kidx <= qrow
    pos_own = blk_idx * BLOCK + kidx - PAD
    ok_band = (own_side & (pos_own >= N_META)) | (~own_side & (pos_own - BLOCK >= N_META) & (blk_idx >= 1))
    qpos = blk_idx * BLOCK + lax.broadcasted_iota(jnp.int32, (N_META, BLOCK), 1) - PAD
    ok_meta = lax.broadcasted_iota(jnp.int32, (N_META, BLOCK), 0) <= qpos
    return [jnp.concatenate([m] * 4, axis=1) for m in (own_side, ok_band, ok_meta)]


def _token_streams(tr, tile_of=lambda i: i):
    k = tr // BLOCK
    return [pl.BlockSpec((BLOCK, D_MODEL), lambda i, j=j: (jnp.maximum(k * tile_of(i) - 1 + j, 0), 0))
            for j in range(k)]


def _embed_ln(x, meta_shard, w_in_shard, g0, b0):
    p = x.shape[0] + BLOCK
    tr = _row_tile(p, 640)
    k = tr // BLOCK
    nt = p // tr
    tile_of = lambda s: (s + 1) % nt
    shards = [meta_shard, w_in_shard]
    c_in, c_out, c_shapes, c_sems = _comm_specs(shards, N_DEV)

    def body(*refs):
        g_ref, b_ref = refs[k:k + 2]
        h_ref, hb_ref, xh_ref, rs_ref = refs[k + 4:k + 8]
        out_refs = refs[k + 8:k + 10]
        lead_ref, meta_ref = refs[k + 10:k + 12]
        starts, passes, waits = _gather_behind(refs[k + 2:k + 4], out_refs, refs[k + 12:], [False, False])
        s = pl.program_id(0)
        t = tile_of(s)

        @pl.when(s == 0)
        def _():
            lead_ref[...] = jnp.zeros_like(lead_ref)
            for start in starts:
                start()

        @pl.when(s == nt - 1)
        def _():
            for step in passes + waits:
                step()
            pltpu.sync_copy(out_refs[0], meta_ref)
            for d in range(N_DEV):
                lead_ref[PAD:BLOCK, d * BLOCK:(d + 1) * BLOCK] = meta_ref[d]

        first = jnp.where(t == 0, lead_ref[...], refs[0][...])
        xhat, rstd = _ln_stats(jnp.concatenate([first] + [r[...] for r in refs[1:k]], axis=0))
        row = t * tr + lax.broadcasted_iota(jnp.int32, (tr, 1), 0)
        h = jnp.where(row >= PAD, xhat * g_ref[...] + b_ref[...], 0.0)
        h_ref[...] = h
        hb_ref[...] = h.astype(BF16)
        xh_ref[...] = xhat
        rs_ref[...] = rstd

    vec = pl.BlockSpec((1, D_MODEL), lambda s: (0, 0))
    rowsp = pl.BlockSpec((tr, D_MODEL), lambda s: (tile_of(s), 0))
    return pl.pallas_call(
        body, name="embed_ln", grid=(nt,),
        in_specs=_token_streams(tr, tile_of) + [vec, vec] + c_in,
        out_specs=[rowsp, rowsp, rowsp, pl.BlockSpec((tr, 1), lambda s: (tile_of(s), 0))] + c_out,
        out_shape=[jax.ShapeDtypeStruct((p, D_MODEL), F32), jax.ShapeDtypeStruct((p, D_MODEL), BF16),
                   jax.ShapeDtypeStruct((p, D_MODEL), F32), jax.ShapeDtypeStruct((p, 1), F32)] + c_shapes,
        scratch_shapes=[pltpu.VMEM((BLOCK, D_MODEL), F32), pltpu.VMEM((N_DEV, N_META, BLOCK), F32)] + c_sems,
        compiler_params=_cparams(("arbitrary",)),
    )(*([x] * k), g0, b0, *shards)


def _rope_tables(p):
    pos = (np.arange(p, dtype=np.int32) - PAD).astype(np.float32)
    half = HEAD_DIM // 2
    inv = np.float32(ROPE_THETA) ** (-np.arange(half, dtype=np.float32) / np.float32(half))
    ang = pos[:, None] * np.tile(inv.astype(np.float32), BLOCK // half)[None, :]
    return jnp.asarray(np.cos(ang), F32), jnp.asarray(np.sin(ang), F32)


def _att_sinkrows(sink_ref):
    lanehead = lax.broadcasted_iota(jnp.int32, (1, 4 * BLOCK), 1) // BLOCK
    rows = []
    for g in range(2):
        row = jnp.zeros((1, 4 * BLOCK), F32)
        for j in range(4):
            row = jnp.where(lanehead == j, sink_ref[0, 4 * g + j], row)
        rows.append(row)
    return rows


def _first_half(rows):
    return (lax.broadcasted_iota(jnp.int32, (rows, BLOCK), 1) % HEAD_DIM) < (HEAD_DIM // 2)


def _att_load(qkv_ref, cos_ref, sin_ref, with_q):
    cos, sin, fh = cos_ref[...], sin_ref[...], _first_half(BLOCK)
    qs = [_rope(qkv_ref[:, j * BLOCK:(j + 1) * BLOCK], cos, sin, fh) for j in range(4)] if with_q else None
    k = _rope(qkv_ref[:, ATT_QW:ATT_QW + ATT_KVW], cos, sin, fh)
    v = qkv_ref[:, ATT_QW + ATT_KVW:ATT_QW + 2 * ATT_KVW]
    return qs, k, v


def _att_load_meta(qkv_ref, cos_ref, sin_ref):
    k = _rope(qkv_ref[PAD:BLOCK, ATT_QW:ATT_QW + ATT_KVW], cos_ref[PAD:BLOCK, :], sin_ref[PAD:BLOCK, :],
              _first_half(N_META))
    return k, qkv_ref[PAD:BLOCK, ATT_QW + ATT_KVW:ATT_QW + 2 * ATT_KVW]


def _att_specs(blk):
    w = ATT_QW + 2 * ATT_KVW
    cur = lambda width: pl.BlockSpec((BLOCK, width), lambda i: (blk(i), 0))
    prev = lambda width: pl.BlockSpec((BLOCK, width), lambda i: (jnp.maximum(blk(i) - 1, 0), 0))
    meta = lambda width: pl.BlockSpec((BLOCK, width), lambda i: (0, 0))
    return [cur(w), prev(w), meta(w), cur(BLOCK), cur(BLOCK), prev(BLOCK), prev(BLOCK), meta(BLOCK), meta(BLOCK),
            pl.BlockSpec(memory_space=pltpu.SMEM)]


_FLIPS = [(dx, dy, dc) for dx in (0, 1) for dy in (0, 1) for dc in (0, 1)][1:]
N_PEERS = len(_FLIPS)


def _place():
    return lax.axis_index("x"), lax.axis_index("y"), lax.axis_index("c")


def _peer(place, flip):
    return tuple(1 - p if f else p for p, f in zip(place, flip))


def _slot(place, swapped):
    x, y, c = place
    return 4 * y + 2 * x + c if swapped else 4 * x + 2 * y + c


def _comm_specs(arrs, out_lead):
    n = len(arrs)
    outs = [jax.ShapeDtypeStruct((out_lead,) + a.shape[-2:], a.dtype) for a in arrs]
    sems = [pltpu.SemaphoreType.DMA((n, N_PEERS)), pltpu.SemaphoreType.DMA((n, N_PEERS)), pltpu.SemaphoreType.DMA((n,))]
    return [pl.BlockSpec(memory_space=pl.ANY)] * n, [pl.BlockSpec(memory_space=pl.ANY)] * n, outs, sems


def _gather_behind(shard_refs, out_refs, sems, swapped):
    send_sems, recv_sems, local_sems = sems
    x, y, c = _place()
    me, sibling = (x, y, c), (x, y, 1 - c)
    chips = [(1 - x, y), (x, 1 - y), (1 - x, 1 - y)]
    starts, passes, waits = [], [], []
    for w, (s, o) in enumerate(zip(shard_refs, out_refs)):
        def copy(k, block, to, from_shard=False, w=w, s=s, o=o):
            rows = o.at[_slot(block, swapped[w])]
            return pltpu.make_async_remote_copy(
                src_ref=s if from_shard else rows, dst_ref=rows, send_sem=send_sems.at[w, k],
                recv_sem=recv_sems.at[w, k], device_id=to, device_id_type=MESH)

        own = pltpu.make_async_copy(s, o.at[_slot(me, swapped[w])], local_sems.at[w])
        first = [copy(0, me, sibling, True)] + [copy(1 + j, me, (*chip, c), True) for j, chip in enumerate(chips)]
        handed = [copy(4 + j, (*chip, c), sibling) for j, chip in enumerate(chips)]
        starts += [own.start] + [cp.start for cp in first]
        for j, chip in enumerate(chips):
            passes += [copy(1 + j, (*chip, c), me).wait_recv, handed[j].start]
        waits.append(copy(0, sibling, me).wait_recv)
        waits += [copy(4 + j, (*chip, 1 - c), me).wait_recv for j, chip in enumerate(chips)]
        waits += [cp.wait_send for cp in first + handed] + [own.wait]
    return starts, passes, waits


def _scatter_behind(part_refs, recv_refs, sems, swapped):
    send_sems, recv_sems, _ = sems
    place = _place()
    starts, waits = [], []
    for w, (p, o) in enumerate(zip(part_refs, recv_refs)):
        for r, flip in enumerate(_FLIPS):
            peer = _peer(place, flip)
            cp = pltpu.make_async_remote_copy(
                src_ref=p.at[_slot(peer, swapped[w])], dst_ref=o.at[r], send_sem=send_sems.at[w, r],
                recv_sem=recv_sems.at[w, r], device_id=peer, device_id_type=MESH)
            starts.append(cp.start)
            waits += [cp.wait_recv, cp.wait_send]
    return starts, waits


def _mixers_fwd(proj_hg, proj_att, lbounds, norm_g, lv, cos, sin, sinks, shards, swapped):
    p = proj_hg.shape[0]
    nb = p // BLOCK
    n = len(shards)
    c_in, c_out, c_shapes, c_sems = _comm_specs(shards, N_DEV)
    pass_step = min(nb - 1, max(1, (5 * nb) // 8))

    def body(*refs):
        x_ref, lb_ref, ng_ref, lv_ref, cur_ref, prev_ref, meta_ref, cc, sc, cp, sp, cm, sm, sink_ref = refs[:14]
        shard_refs = refs[14:14 + n]
        y_ref, o_ref, st_ref, a_ref, raw_ref, pr_ref = refs[14 + n:20 + n]
        out_refs = refs[20 + n:20 + 2 * n]
        carry_ref = refs[20 + 2 * n]
        starts, passes, waits = _gather_behind(shard_refs, out_refs, refs[21 + 2 * n:], swapped)
        c = pl.program_id(0)

        @pl.when(c == 0)
        def _():
            carry_ref[...] = jnp.zeros_like(carry_ref)
            for start in starts:
                start()

        @pl.when(c == pass_step)
        def _():
            for step in passes:
                step()

        valid = (c * BLOCK + lax.broadcasted_iota(jnp.int32, (BLOCK, 1), 0)) >= PAD
        logf, k = _hgrn_gates(x_ref[:, HG_W:2 * HG_W], lb_ref[0:1, :], lb_ref[1:2, :], valid)
        e = _split_dot(lv_ref[...], logf, "nn")
        for h in range(HG_HEADS):
            sl = lambda part: x_ref[:, part * HG_W + h * BLOCK: part * HG_W + (h + 1) * BLOCK]
            hs = slice(h * BLOCK, (h + 1) * BLOCK)
            st_in = carry_ref[h]
            st_ref[0, h] = st_in
            seg = _seg_blocks(e, h)
            a = _hgrn_scores(sl(0), k[:, hs], *seg[3:])
            a_ref[0, h] = a.astype(BF16)
            raw, st_out = _hgrn_mix(sl(0), k[:, hs], sl(2), st_in, a, *seg[:3])
            raw_ref[:, hs] = raw
            y_ref[:, hs] = _hgrn_norm(raw, sl(3), ng_ref[...]).astype(BF16)
            carry_ref[h] = st_out

        qs, kc, vc = _att_load(cur_ref, cc, sc, True)
        _, kp, vp = _att_load(prev_ref, cp, sp, False)
        km, vm = _att_load_meta(meta_ref, cm, sm)
        sinkrows = _att_sinkrows(sink_ref)
        own4, band4, meta4 = _att_masks(c)
        for g in range(2):
            s, s_meta = _att_scores(qs[2 * g], qs[2 * g + 1], kc, kp, km, g, own4, band4, meta4)
            pr, pr_meta, pr_sink = _att_probs(s, s_meta, sinkrows[g])
            pr_ref[0, g, :BLOCK, :] = pr.astype(BF16)
            pr_ref[0, g, BLOCK:BLOCK + N_META, :] = pr_meta.astype(BF16)
            pr_ref[0, g, BLOCK + N_META:, :] = jnp.broadcast_to(pr_sink, (N_META, 4 * BLOCK)).astype(BF16)
            for j, tile in enumerate(_att_values(pr, pr_meta, vc, vp, vm, g, own4)):
                o_ref[:, (2 * g + j) * BLOCK:(2 * g + j + 1) * BLOCK] = tile.astype(BF16)

        @pl.when(c == nb - 1)
        def _():
            for wait in waits:
                wait()

    return pl.pallas_call(
        body, name="mixers_fwd", grid=(nb,),
        in_specs=[pl.BlockSpec((BLOCK, 4 * HG_W), lambda c: (c, 0)), pl.BlockSpec((2, HG_W), lambda c: (0, 0)),
                  pl.BlockSpec((1, BLOCK), lambda c: (0, 0)), pl.BlockSpec(lv.shape, lambda c: (0, 0))]
        + _att_specs(lambda c: c) + c_in,
        out_specs=[pl.BlockSpec((BLOCK, HG_W), lambda c: (c, 0)), pl.BlockSpec((BLOCK, ATT_QW), lambda c: (c, 0)),
                   pl.BlockSpec((1, HG_HEADS, BLOCK, BLOCK), lambda c: (c, 0, 0, 0)),
                   pl.BlockSpec((1, HG_HEADS, BLOCK, BLOCK), lambda c: (c, 0, 0, 0)),
                   pl.BlockSpec((BLOCK, HG_W), lambda c: (c, 0)),
                   pl.BlockSpec((1, 2, ATT_KEYS, 4 * BLOCK), lambda c: (c, 0, 0, 0))] + c_out,
        out_shape=[jax.ShapeDtypeStruct((p, HG_W), BF16), jax.ShapeDtypeStruct((p, ATT_QW), BF16),
                   jax.ShapeDtypeStruct((nb, HG_HEADS, BLOCK, BLOCK), F32),
                   jax.ShapeDtypeStruct((nb, HG_HEADS, BLOCK, BLOCK), BF16),
                   jax.ShapeDtypeStruct((p, HG_W), F32),
                   jax.ShapeDtypeStruct((nb, 2, ATT_KEYS, 4 * BLOCK), BF16)] + c_shapes,
        scratch_shapes=[pltpu.VMEM((HG_HEADS, BLOCK, BLOCK), F32)] + c_sems,
        compiler_params=_cparams(("arbitrary",)),
    )(proj_hg, lbounds, norm_g, lv, proj_att, proj_att, proj_att, cos, sin, cos, sin, cos, sin, sinks, *shards)


def _tile(rows, preferred):
    return preferred if rows % preferred == 0 else _row_tile(rows, preferred)


def _in_proj(h0b, w_in):
    p = h0b.shape[0]
    tm = _row_tile(p, 640)
    hg_end = 4 * HG_W

    def body(h_ref, w_ref, hg_ref, att_ref, gates_ref):
        h = h_ref[...]
        hg_ref[...] = _dot(h, w_ref[:, :hg_end], "nn")
        att_ref[...] = _dot(h, w_ref[:, hg_end:MIX_W], "nn")
        gates_ref[...] = _dot(h, w_ref[:, MIX_W:], "nn").astype(BF16)

    row = lambda w: pl.BlockSpec((tm, w), lambda i: (i, 0))
    return pl.pallas_call(
        body, name="in_proj", grid=(p // tm,),
        in_specs=[row(D_MODEL), pl.BlockSpec(w_in.shape, lambda i: (0, 0), pipeline_mode=pl.Buffered(1))],
        out_specs=[row(hg_end), row(MIX_W - hg_end), row(2 * D_MODEL)],
        out_shape=[jax.ShapeDtypeStruct((p, hg_end), F32), jax.ShapeDtypeStruct((p, MIX_W - hg_end), F32),
                   jax.ShapeDtypeStruct((p, 2 * D_MODEL), BF16)],
        compiler_params=_cparams(("arbitrary",)),
    )(h0b, w_in)


def _branch_mix(yh, oa, gates, w_bh, w_ba):
    y_hg = _dot(yh, w_bh, "nn")
    y_att = _dot(oa, w_ba, "nn")
    s1 = jax.nn.sigmoid(gates[:, :D_MODEL].astype(F32))
    s2 = jax.nn.sigmoid(gates[:, D_MODEL:].astype(F32))
    return s1 * y_hg + s2 * y_att, y_hg, y_att, s1, s2


def _mix_out_ln1(yh, oa, gates, h0, w_bh, w_ba, w_out, g1, b1):
    p = yh.shape[0]
    tr = _tile(p, 320)

    def body(yh_ref, oa_ref, g_ref, h0_ref, wbh_ref, wba_ref, wo_ref, g1_ref, b1_ref,
             mix_ref, h1_ref, h1b_ref, xh_ref, rs_ref):
        mixin = _branch_mix(yh_ref[...], oa_ref[...], g_ref[...], wbh_ref[...], wba_ref[...])[0]
        mix_ref[...] = mixin.astype(BF16)
        xhat, rstd = _ln_stats(ALPHA * h0_ref[...] + _dot(mixin, wo_ref[...], "nn"))
        h1 = xhat * g1_ref[...] + b1_ref[...]
        h1_ref[...] = h1
        h1b_ref[...] = h1.astype(BF16)
        xh_ref[...] = xhat
        rs_ref[...] = rstd

    row = lambda w: pl.BlockSpec((tr, w), lambda i: (i, 0))
    const = lambda a: pl.BlockSpec(a.shape, lambda i: (0, 0))
    return pl.pallas_call(
        body, name="mix_out_ln1", grid=(p // tr,),
        in_specs=[row(HG_W), row(ATT_QW), row(2 * D_MODEL), row(D_MODEL), const(w_bh), const(w_ba), const(w_out),
                  const(g1), const(b1)],
        out_specs=[row(D_MODEL), row(D_MODEL), row(D_MODEL), row(D_MODEL), row(1)],
        out_shape=[jax.ShapeDtypeStruct((p, D_MODEL), BF16), jax.ShapeDtypeStruct((p, D_MODEL), F32),
                   jax.ShapeDtypeStruct((p, D_MODEL), BF16), jax.ShapeDtypeStruct((p, D_MODEL), F32),
                   jax.ShapeDtypeStruct((p, 1), F32)],
        compiler_params=_cparams(("arbitrary",)),
    )(yh, oa, gates, h0, w_bh, w_ba, w_out, g1, b1)


FF_T = D_FF // 2


def _ffn_in_swiglu(h1, w_fi):
    p = h1.shape[0]
    tm = _row_tile(p, 640)

    def body(h_ref, w_ref, au_ref, s_ref):
        au = _dot(h_ref[...], w_ref[...], "nn")
        au_ref[...] = au.astype(BF16)
        s_ref[...] = (jax.nn.silu(au[:, :FF_T]) * au[:, FF_T:]).astype(BF16)

    return pl.pallas_call(
        body, name="ffn_in_swiglu", grid=(D_FF // FF_T, p // tm),
        in_specs=[pl.BlockSpec((tm, D_MODEL), lambda j, i: (i, 0)), pl.BlockSpec((D_MODEL, 2 * FF_T), lambda j, i: (0, j))],
        out_specs=[pl.BlockSpec((tm, 2 * FF_T), lambda j, i: (i, j)), pl.BlockSpec((tm, FF_T), lambda j, i: (i, j))],
        out_shape=[jax.ShapeDtypeStruct((p, 2 * D_FF), BF16), jax.ShapeDtypeStruct((p, D_FF), BF16)],
        compiler_params=_cparams(("arbitrary", "arbitrary")),
    )(h1, w_fi)


def _ffn_out_loss(s, w_fo, h1, g2, b2, target):
    p = h1.shape[0]
    tr = _row_tile(p, 640)
    k = tr // BLOCK

    def body(*refs):
        s_ref, w_ref, h_ref, g_ref, b_ref = refs[:5]
        dr_ref, drb_ref, loss_ref, dg_ref, db_ref = refs[5 + k:]
        i = pl.program_id(0)
        xhat, rstd = _ln_stats(ALPHA * h_ref[...] + _dot(s_ref[...], w_ref[...], "nn"))
        y = xhat * g_ref[...] + b_ref[...]
        row = i * tr + lax.broadcasted_iota(jnp.int32, (tr, 1), 0)
        tgt = jnp.concatenate([r[...] for r in refs[5:5 + k]], axis=0)
        err = jnp.where(row >= BLOCK, y - tgt, 0.0)
        dr, dg, db = _ln_bwd(err * (1.0 / D_MODEL), xhat, rstd, g_ref[...])
        dr_ref[...] = dr
        drb_ref[...] = dr.astype(BF16)
        e2 = jnp.sum(err * err, axis=0, keepdims=True)
        part = e2[:, 0:BLOCK]
        for j in range(1, D_MODEL // BLOCK):
            part = part + e2[:, j * BLOCK:(j + 1) * BLOCK]
        part = part * (0.5 / D_MODEL)

        @pl.when(i == 0)
        def _():
            loss_ref[...] = part
            dg_ref[...] = dg
            db_ref[...] = db

        @pl.when(i > 0)
        def _():
            loss_ref[...] += part
            dg_ref[...] += dg
            db_ref[...] += db

    vec = pl.BlockSpec((1, D_MODEL), lambda i: (0, 0))
    rowsp = pl.BlockSpec((tr, D_MODEL), lambda i: (i, 0))
    return pl.pallas_call(
        body, name="ffn_out_loss", grid=(p // tr,),
        in_specs=[pl.BlockSpec((tr, D_FF), lambda i: (i, 0)), pl.BlockSpec((D_FF, D_MODEL), lambda i: (0, 0)),
                  rowsp, vec, vec] + _token_streams(tr),
        out_specs=[rowsp, rowsp, pl.BlockSpec((1, BLOCK), lambda i: (0, 0)), vec, vec],
        out_shape=[jax.ShapeDtypeStruct((p, D_MODEL), F32), jax.ShapeDtypeStruct((p, D_MODEL), BF16),
                   jax.ShapeDtypeStruct((1, BLOCK), F32), jax.ShapeDtypeStruct((1, D_MODEL), F32),
                   jax.ShapeDtypeStruct((1, D_MODEL), F32)],
        compiler_params=_cparams(("arbitrary",)),
    )(s, w_fo, h1, g2, b2, *([target] * k))


def _ffn_bwd(dr2, w_fo, au, w_fi):
    p = au.shape[0]
    tm = _tile(p, 320)

    def body(d_ref, wo_ref, au_ref, wi_ref, dau_ref, dh_ref):
        d = d_ref[...]
        dh = 0.0
        for j in range(D_FF // FF_T):
            a_cols = slice(2 * j * FF_T, (2 * j + 1) * FF_T)
            u_cols = slice((2 * j + 1) * FF_T, (2 * j + 2) * FF_T)
            ds = _dot(d, wo_ref[j * FF_T:(j + 1) * FF_T, :], "nt")
            _, vjp = jax.vjp(lambda a, u: jax.nn.silu(a) * u, au_ref[:, a_cols].astype(F32), au_ref[:, u_cols].astype(F32))
            da, du = vjp(ds)
            da, du = da.astype(BF16), du.astype(BF16)
            dau_ref[:, a_cols] = da
            dau_ref[:, u_cols] = du
            dh = dh + _dot(da, wi_ref[:, a_cols], "nt") + _dot(du, wi_ref[:, u_cols], "nt")
        dh_ref[...] = dh

    row = lambda w: pl.BlockSpec((tm, w), lambda i: (i, 0))
    kept = lambda a: pl.BlockSpec(a.shape, lambda i: (0, 0), pipeline_mode=pl.Buffered(1))
    return pl.pallas_call(
        body, name="ffn_bwd", grid=(p // tm,),
        in_specs=[row(D_MODEL), kept(w_fo), row(2 * D_FF), kept(w_fi)],
        out_specs=[row(2 * D_FF), row(D_MODEL)],
        out_shape=[jax.ShapeDtypeStruct((p, 2 * D_FF), BF16), jax.ShapeDtypeStruct((p, D_MODEL), F32)],
        compiler_params=_cparams(("arbitrary",)),
    )(dr2, w_fo, au, w_fi)


def _ln1_mix_bwd(dr2, dh1_ffn, xhat1, rstd1, g1, yh, oa, gates, w_bh, w_ba, w_out):
    p = yh.shape[0]
    tr = _tile(p, 320)

    def body(a_ref, b_ref, xh_ref, rs_ref, g1_ref, yh_ref, oa_ref, g_ref, wbh_ref, wba_ref, wo_ref,
             dr_ref, dyhg_ref, dyat_ref, dgt_ref, dyh_ref, doa_ref, dg_ref, db_ref):
        i = pl.program_id(0)
        dr, dg, db = _ln_bwd(ALPHA * a_ref[...] + b_ref[...], xh_ref[...], rs_ref[...], g1_ref[...])
        dr_ref[...] = dr
        d = _dot(dr, wo_ref[...], "nt")
        _, y_hg, y_att, s1, s2 = _branch_mix(yh_ref[...], oa_ref[...], g_ref[...], wbh_ref[...], wba_ref[...])
        dy_hg = d * s1
        dy_att = d * s2
        dyhg_ref[...] = dy_hg.astype(BF16)
        dyat_ref[...] = dy_att.astype(BF16)
        dgt_ref[:, :D_MODEL] = (d * y_hg * s1 * (1.0 - s1)).astype(BF16)
        dgt_ref[:, D_MODEL:] = (d * y_att * s2 * (1.0 - s2)).astype(BF16)
        dyh_ref[...] = _dot(dy_hg, wbh_ref[...], "nt")
        doa_ref[...] = _dot(dy_att, wba_ref[...], "nt")

        @pl.when(i == 0)
        def _():
            dg_ref[...] = dg
            db_ref[...] = db

        @pl.when(i > 0)
        def _():
            dg_ref[...] += dg
            db_ref[...] += db

    row = lambda w: pl.BlockSpec((tr, w), lambda i: (i, 0))
    const = lambda a: pl.BlockSpec(a.shape, lambda i: (0, 0))
    vec = pl.BlockSpec((1, D_MODEL), lambda i: (0, 0))
    return pl.pallas_call(
        body, name="ln1_mix_bwd", grid=(p // tr,),
        in_specs=[row(D_MODEL), row(D_MODEL), row(D_MODEL), row(1), vec, row(HG_W), row(ATT_QW), row(2 * D_MODEL),
                  const(w_bh), const(w_ba), const(w_out)],
        out_specs=[row(D_MODEL), row(D_MODEL), row(D_MODEL), row(2 * D_MODEL), row(HG_W), row(ATT_QW), vec, vec],
        out_shape=[jax.ShapeDtypeStruct((p, D_MODEL), F32), jax.ShapeDtypeStruct((p, D_MODEL), BF16),
                   jax.ShapeDtypeStruct((p, D_MODEL), BF16), jax.ShapeDtypeStruct((p, 2 * D_MODEL), BF16),
                   jax.ShapeDtypeStruct((p, HG_W), F32), jax.ShapeDtypeStruct((p, ATT_QW), F32),
                   jax.ShapeDtypeStruct((1, D_MODEL), F32), jax.ShapeDtypeStruct((1, D_MODEL), F32)],
        compiler_params=_cparams(("arbitrary",)),
    )(dr2, dh1_ffn, xhat1, rstd1, g1, yh, oa, gates, w_bh, w_ba, w_out)


MIX_W = 4 * HG_W + ATT_QW + 2 * ATT_KVW
ATT_KEYS = BLOCK + 2 * N_META


def _mixers_bwd(proj_hg, proj_att, lbounds, norm_g, lv, states, scores, raw, probs, cos, sin, sinks, dyh, doa,
                parts, swapped):
    p = proj_hg.shape[0]
    nb = p // BLOCK
    n = len(parts)
    kvw = 2 * ATT_KVW
    rev = lambda s: nb - 1 - s
    c_in, c_out, c_shapes, c_sems = _comm_specs(parts, N_PEERS)

    def body(*refs):
        (x_ref, lb_ref, ng_ref, lv_ref, st_ref, a_ref, raw_ref, pr_ref, cur_ref, prev_ref, meta_ref, cc, sc, cp, sp,
         cm, sm, sink_ref, dy_ref, do_ref) = refs[:20]
        part_refs = refs[20:20 + n]
        dx_ref, dlb_ref, dng_ref, dsink_ref = refs[20 + n:24 + n]
        recv_refs = refs[24 + n:24 + 2 * n]
        dcarry_ref, dkv_next_ref, dkv_meta_ref = refs[24 + 2 * n:27 + 2 * n]
        starts, waits = _scatter_behind(part_refs, recv_refs, refs[27 + 2 * n:], swapped)
        step = pl.program_id(0)
        c = rev(step)

        @pl.when(step == 0)
        def _():
            dcarry_ref[...] = jnp.zeros_like(dcarry_ref)
            dkv_next_ref[...] = jnp.zeros_like(dkv_next_ref)
            dkv_meta_ref[...] = jnp.zeros_like(dkv_meta_ref)
            dlb_ref[...] = jnp.zeros_like(dlb_ref)
            dng_ref[...] = jnp.zeros_like(dng_ref)
            dsink_ref[...] = jnp.zeros_like(dsink_ref)
            for start in starts:
                start()

        fh = _first_half(BLOCK)
        qs, kc, vc = _att_load(cur_ref, cc, sc, True)
        _, kp, vp = _att_load(prev_ref, cp, sp, False)
        km, vm = _att_load_meta(meta_ref, cm, sm)
        own4, band4, meta4 = _att_masks(c)
        att0 = 4 * HG_W
        dkm = dkp = dkc = dvm = dvp = dvc = 0.0
        dsinkrows = []
        for g in range(2):
            pr = pr_ref[0, g, :BLOCK, :].astype(F32)
            pr_meta = pr_ref[0, g, BLOCK:BLOCK + N_META, :].astype(F32)
            pr_sink = jnp.max(pr_ref[0, g, BLOCK + N_META:, :].astype(F32), axis=0, keepdims=True)
            _, values_vjp = jax.vjp(lambda *a, g=g: _att_values(*a, g, own4), pr, pr_meta, vc, vp, vm)
            dpr, dpr_meta, dvc_g, dvp_g, dvm_g = values_vjp(
                [do_ref[:, (2 * g + j) * BLOCK:(2 * g + j + 1) * BLOCK] for j in range(2)])
            ds, ds_meta, dsinkrow = _att_probs_bwd(pr, pr_meta, pr_sink, dpr, dpr_meta)
            _, scores_vjp = jax.vjp(lambda *a, g=g: _att_scores(*a, g, own4, band4, meta4),
                                    qs[2 * g], qs[2 * g + 1], kc, kp, km)
            dqa, dqb, dkc_g, dkp_g, dkm_g = scores_vjp((ds, ds_meta))
            for j, dq in enumerate((dqa, dqb)):
                dx_ref[:, att0 + (2 * g + j) * BLOCK:att0 + (2 * g + j + 1) * BLOCK] = _rope_t(
                    dq, cc[...], sc[...], fh).astype(BF16)
            dkm, dkp, dkc = dkm + dkm_g, dkp + dkp_g, dkc + dkc_g
            dvm, dvp, dvc = dvm + dvm_g, dvp + dvp_g, dvc + dvc_g
            dsinkrows.append(dsinkrow)
        ds0, ds1 = dsinkrows
        dkv_meta_ref[:, :BLOCK] += _rope_t(dkm, cm[PAD:BLOCK, :], sm[PAD:BLOCK, :], _first_half(N_META))
        dkv_meta_ref[:, BLOCK:] += dvm
        last = jnp.where(c == 0, 1.0, 0.0)
        to_meta_rows = lambda m: jnp.concatenate([jnp.zeros((PAD, BLOCK), F32), last * m], axis=0)
        dk = _rope_t(dkc, cc[...], sc[...], fh) + dkv_next_ref[:, :BLOCK] + to_meta_rows(dkv_meta_ref[:, :BLOCK])
        dv = dvc + dkv_next_ref[:, BLOCK:] + to_meta_rows(dkv_meta_ref[:, BLOCK:])
        dx_ref[:, att0 + ATT_QW:att0 + ATT_QW + ATT_KVW] = dk.astype(BF16)
        dx_ref[:, att0 + ATT_QW + ATT_KVW:] = dv.astype(BF16)
        dkv_next_ref[:, :BLOCK] = _rope_t(dkp, cp[...], sp[...], fh)
        dkv_next_ref[:, BLOCK:] = dvp
        sink_rows = []
        for dsg in (ds0, ds1):
            for j in range(4):
                tot = jnp.sum(dsg[:, j * BLOCK:(j + 1) * BLOCK], axis=1, keepdims=True)
                sink_rows.append(jnp.broadcast_to(tot, (1, BLOCK)))
        dsink_ref[...] += jnp.concatenate(sink_rows, axis=0)

        valid = (c * BLOCK + lax.broadcasted_iota(jnp.int32, (BLOCK, 1), 0)) >= PAD
        (logf, k), gates_vjp = jax.vjp(lambda hf, a0, a1: _hgrn_gates(hf, a0, a1, valid),
                                       x_ref[:, HG_W:2 * HG_W], lb_ref[0:1, :], lb_ref[1:2, :])
        lvv = lv_ref[...]
        e = _split_dot(lvv, logf, "nn")
        dng = jnp.zeros((1, BLOCK), F32)
        dk, dseg = [], []
        for h in range(HG_HEADS):
            sl = lambda part: x_ref[:, part * HG_W + h * BLOCK: part * HG_W + (h + 1) * BLOCK]
            hs = slice(h * BLOCK, (h + 1) * BLOCK)
            seg = _seg_blocks(e, h)
            _, norm_vjp = jax.vjp(_hgrn_norm, raw_ref[:, hs], sl(3), ng_ref[...])
            draw, dhg, dngh = norm_vjp(dy_ref[:, hs])
            _, mix_vjp = jax.vjp(_hgrn_mix, sl(0), k[:, hs], sl(2), st_ref[0, h], a_ref[0, h].astype(F32), *seg[:3])
            dhq, dkh, dhi, dst, da, *dseg_mix = mix_vjp((draw, dcarry_ref[h]))
            _, scores_vjp = jax.vjp(_hgrn_scores, sl(0), k[:, hs], *seg[3:])
            dhq2, dkh2, *dseg_lvl = scores_vjp(da)
            for part, val in ((0, dhq + dhq2), (2, dhi), (3, dhg)):
                dx_ref[:, part * HG_W + h * BLOCK: part * HG_W + (h + 1) * BLOCK] = val.astype(BF16)
            dk.append(dkh + dkh2)
            dseg.append(jnp.concatenate(dseg_mix + dseg_lvl, axis=0))
            dng = dng + dngh
            dcarry_ref[h] = dst
        dlogf = _split_dot(lvv, jnp.concatenate(dseg, axis=1), "tn")
        dhf, da0, da1 = gates_vjp((dlogf, jnp.concatenate(dk, axis=1)))
        dx_ref[:, HG_W:2 * HG_W] = dhf.astype(BF16)
        dlb_ref[0:1, :] += da0
        dlb_ref[1:2, :] += da1
        dng_ref[...] += dng

        @pl.when(step == nb - 1)
        def _():
            for wait in waits:
                wait()

    const = lambda shape: pl.BlockSpec(shape, lambda s: (0,) * len(shape))
    per_head = pl.BlockSpec((1, HG_HEADS, BLOCK, BLOCK), lambda s: (rev(s), 0, 0, 0))
    return pl.pallas_call(
        body, name="mixers_bwd", grid=(nb,),
        in_specs=[pl.BlockSpec((BLOCK, 4 * HG_W), lambda s: (rev(s), 0)), const((2, HG_W)), const((1, BLOCK)),
                  const(lv.shape), per_head, per_head, pl.BlockSpec((BLOCK, HG_W), lambda s: (rev(s), 0)),
                  pl.BlockSpec((1, 2, ATT_KEYS, 4 * BLOCK), lambda s: (rev(s), 0, 0, 0))]
        + _att_specs(rev)
        + [pl.BlockSpec((BLOCK, HG_W), lambda s: (rev(s), 0)), pl.BlockSpec((BLOCK, ATT_QW), lambda s: (rev(s), 0))]
        + c_in,
        out_specs=[pl.BlockSpec((BLOCK, MIX_W), lambda s: (rev(s), 0)), const((2, HG_W)), const((1, BLOCK)),
                   const((ATT_HEADS, BLOCK))] + c_out,
        out_shape=[jax.ShapeDtypeStruct((p, MIX_W), BF16), jax.ShapeDtypeStruct((2, HG_W), F32),
                   jax.ShapeDtypeStruct((1, BLOCK), F32), jax.ShapeDtypeStruct((ATT_HEADS, BLOCK), F32)] + c_shapes,
        scratch_shapes=[pltpu.VMEM((HG_HEADS, BLOCK, BLOCK), F32), pltpu.VMEM((BLOCK, kvw), F32),
                        pltpu.VMEM((N_META, kvw), F32)] + c_sems,
        compiler_params=_cparams(("arbitrary",)),
    )(proj_hg, lbounds, norm_g, lv, states, scores, raw, probs, proj_att, proj_att, proj_att, cos, sin, cos, sin,
      cos, sin, sinks, dyh, doa, *parts)


_HBM = pl.BlockSpec(memory_space=pltpu.HBM)
_SEM = pl.BlockSpec(memory_space=pltpu.SEMAPHORE)
_ORDERED_BY_DATA = pltpu.CompilerParams(has_side_effects=pltpu.SideEffectType.DATAFLOW_SIDE_EFFECTING)


def _exchange_copies(part_ref, land_ref, send_sems, recv_sems):
    place = _place()
    return [pltpu.make_async_remote_copy(
        src_ref=part_ref.at[_slot(_peer(place, flip), False)], dst_ref=land_ref.at[r], send_sem=send_sems.at[r],
        recv_sem=recv_sems.at[r], device_id=_peer(place, flip), device_id_type=MESH) for r, flip in enumerate(_FLIPS)]


def _exchange_start(parts, name):
    def body(part_ref, land_ref, send_sems, recv_sems, part_thru, land_thru, token):
        for cp in _exchange_copies(part_ref, land_ref, send_sems, recv_sems):
            cp.start()
        token[...] = jnp.zeros_like(token)

    land = (N_PEERS,) + parts.shape[1:]
    return pl.pallas_call(
        body, name=name,
        out_shape=(pltpu.SemaphoreType.DMA((N_PEERS,)), pltpu.SemaphoreType.DMA((N_PEERS,)),
                   pltpu.HBM(parts.shape, parts.dtype), pltpu.HBM(land, parts.dtype), jax.ShapeDtypeStruct((8, BLOCK), F32)),
        in_specs=(_HBM, _HBM), out_specs=(_SEM, _SEM, _HBM, _HBM, pl.BlockSpec(memory_space=pltpu.VMEM)),
        input_output_aliases={0: 2, 1: 3}, compiler_params=_ORDERED_BY_DATA,
    )(pltpu.with_memory_space_constraint(parts, pltpu.HBM),
      pltpu.with_memory_space_constraint(lax.empty(land, parts.dtype), pltpu.HBM))


def _exchange_wait(send_sems, recv_sems, part_thru, land_thru, after, name):
    def body(part_ref, land_ref, send_sems, recv_sems, after_ref, part_out, land_out):
        for cp in _exchange_copies(part_ref, land_ref, send_sems, recv_sems):
            cp.wait_send()
            cp.wait_recv()

    return pl.pallas_call(
        body, name=name,
        out_shape=(pltpu.HBM(part_thru.shape, part_thru.dtype), pltpu.HBM(land_thru.shape, land_thru.dtype)),
        in_specs=(_HBM, _HBM, _SEM, _SEM, pl.BlockSpec(memory_space=pl.ANY)), out_specs=(_HBM, _HBM),
        input_output_aliases={0: 0, 1: 1}, compiler_params=_ORDERED_BY_DATA,
    )(part_thru, land_thru, send_sems, recv_sems, after)


def _embed_bwd(dmix, dgates, w_mix, w_gates, dr1, xhat0, rstd0, g0):
    p = dmix.shape[0]
    tm = _row_tile(p, 640)
    nm = p // tm

    def body(a_ref, g_ref, wa_ref, wg_ref, dr_ref, xh_ref, rs_ref, g0_ref, gx_ref, lead_ref, dg_ref, db_ref,
             buf_ref, sem):
        i = pl.program_id(0)
        first = pltpu.make_async_copy(buf_ref.at[pl.ds(BLOCK, tm - BLOCK)], gx_ref.at[pl.ds(0, tm - BLOCK)], sem)
        later = lambda t: pltpu.make_async_copy(buf_ref, gx_ref.at[pl.ds(t * tm - BLOCK, tm)], sem)

        @pl.when(i == 1)
        def _():
            first.wait()

        @pl.when(i > 1)
        def _():
            later(i - 1).wait()

        dh0 = ALPHA * dr_ref[...] + _dot(a_ref[...], wa_ref[...], "nt") + _dot(g_ref[...], wg_ref[...], "nt")
        row = i * tm + lax.broadcasted_iota(jnp.int32, (tm, 1), 0)
        dx, dg, db = _ln_bwd(jnp.where(row >= PAD, dh0, 0.0), xh_ref[...], rs_ref[...], g0_ref[...])
        buf_ref[...] = dx

        @pl.when(i == 0)
        def _():
            lead_ref[...] = dx[:BLOCK]
            dg_ref[...] = dg
            db_ref[...] = db
            first.start()

        @pl.when(i > 0)
        def _():
            dg_ref[...] += dg
            db_ref[...] += db
            later(i).start()

        @pl.when(i == nm - 1)
        def _():
            (first if nm == 1 else later(i)).wait()

    row = lambda w: pl.BlockSpec((tm, w), lambda i: (i, 0))
    const = lambda a: pl.BlockSpec(a.shape, lambda i: (0, 0))
    vec = pl.BlockSpec((1, D_MODEL), lambda i: (0, 0))
    return pl.pallas_call(
        body, name="embed_bwd", grid=(nm,),
        in_specs=[row(dmix.shape[1]), row(dgates.shape[1]), const(w_mix), const(w_gates), row(D_MODEL), row(D_MODEL),
                  row(1), vec],
        out_specs=[pl.BlockSpec(memory_space=pl.ANY), pl.BlockSpec((BLOCK, D_MODEL), lambda i: (0, 0)), vec, vec],
        out_shape=[jax.ShapeDtypeStruct((p - BLOCK, D_MODEL), F32), jax.ShapeDtypeStruct((BLOCK, D_MODEL), F32),
                   jax.ShapeDtypeStruct((1, D_MODEL), F32), jax.ShapeDtypeStruct((1, D_MODEL), F32)],
        scratch_shapes=[pltpu.VMEM((tm, D_MODEL), F32), pltpu.SemaphoreType.DMA],
        compiler_params=_cparams(("arbitrary",)),
    )(dmix, dgates, w_mix, w_gates, dr1, xhat0, rstd0, g0)


_LATE = ("w_branch_hg", "w_branch_attn", "w_out", "w_ffn_in", "w_ffn_out")
_COLUMN_SHARDED = ("meta_tokens", "w_in", "w_branch_hg", "w_branch_attn", "w_ffn_in")
_SWAPPED = ("w_ffn_in",)


def _whole(name, gathered):
    _, r, c = gathered.shape
    if name in _COLUMN_SHARDED:
        return jnp.transpose(gathered, (1, 0, 2)).reshape(r, N_DEV * c)
    return gathered.reshape(N_DEV * r, c)


def _slots(name, whole):
    r, c = whole.shape
    if name in _COLUMN_SHARDED:
        return jnp.transpose(whole.reshape(r, N_DEV, c // N_DEV), (1, 0, 2))
    return whole.reshape(N_DEV, r // N_DEV, c)


def _device_step(x, target, meta_shard, ln_emb_g, ln_emb_b, w_in_shard, lbounds, norm_g, sinks, late_shards,
                 ln1_g, ln1_b, ln2_g, ln2_b):
    p = x.shape[0] + BLOCK
    lv = _level_stack()
    cos, sin = _rope_tables(p)
    swapped = [n in _SWAPPED for n in _LATE]

    h0, h0b, xhat0, rstd0, _, g_win = _embed_ln(x, meta_shard, w_in_shard, ln_emb_g, ln_emb_b)
    w_in = _whole("w_in", g_win)
    proj_hg, proj_att, gates = _in_proj(h0b, w_in)
    yh, oa, states, scores, raw, probs, *gathered = _mixers_fwd(
        proj_hg, proj_att, lbounds, norm_g, lv, cos, sin, sinks, late_shards, swapped)
    w_bh, w_ba, w_out, w_fi, w_fo = [_whole(n, g) for n, g in zip(_LATE, gathered)]
    mixin, h1, h1b, xhat1, rstd1 = _mix_out_ln1(yh, oa, gates, h0, w_bh, w_ba, w_out, ln1_g, ln1_b)
    au, sw = _ffn_in_swiglu(h1b, w_fi)
    dr2, dr2b, loss_part, dg2, db2 = _ffn_out_loss(sw, w_fo, h1, ln2_g, ln2_b, target)

    mtn = functools.partial(_tiled_matmul_tn, tm=_row_tile(p, 1664), out_dtype=BF16)
    whole = functools.partial(_tiled_matmul_tn, tm=p, out_dtype=BF16)
    d_wfo = whole(sw, dr2b, tk=2 * BLOCK, tn=D_MODEL, name="grad_w_ffn_out")
    dau, dh1_ffn = _ffn_bwd(dr2b, w_fo, au, w_fi)
    d_wfi = whole(h1b, dau, tk=D_MODEL // 2, tn=D_MODEL // 2, name="grad_w_ffn_in")
    dr1, dy_hg, dy_att, dgates, dyh, doa, dg1, db1 = _ln1_mix_bwd(
        dr2, dh1_ffn, xhat1, rstd1, ln1_g, yh, oa, gates, w_bh, w_ba, w_out)
    d_wout = mtn(mixin, dr1, tk=D_MODEL, tn=D_MODEL, name="grad_w_out")
    d_wbh = mtn(yh, dy_hg, tk=HG_W, tn=D_MODEL, name="grad_w_branch_hg")
    d_wba = mtn(oa, dy_att, tk=ATT_QW, tn=D_MODEL, name="grad_w_branch_attn")
    late_parts = [_slots(n, g) for n, g in zip(_LATE, (d_wbh, d_wba, d_wout, d_wfi, d_wfo))]
    dmix, d_lb, d_ng, d_sink, *late_recv = _mixers_bwd(
        proj_hg, proj_att, lbounds, norm_g, lv, states, scores, raw, probs, cos, sin, sinks, dyh, doa, late_parts,
        swapped)
    d_win = jnp.concatenate([whole(h0b, dmix, tk=D_MODEL, tn=D_MODEL // 4, name="grad_w_in_mixers"),
                             mtn(h0b, dgates, tk=D_MODEL, tn=D_MODEL, name="grad_w_in_gates")], axis=1)
    *win_flight, token = _exchange_start(_slots("w_in", d_win), "w_in_grads_start")
    grad_x, dlead, dg0, db0 = _embed_bwd(dmix, dgates, w_in[:, :MIX_W], w_in[:, MIX_W:], dr1, xhat0, rstd0,
                                         ln_emb_g + token[0:1, 0:1])

    small = dict(ln_emb_g=dg0, ln_emb_b=db0, hg_lower_bounds=d_lb, hg_norm_g=d_ng, attn_sinks=d_sink[:, 0],
                 ln1_g=dg1, ln1_b=db1, ln2_g=dg2, ln2_b=db2)
    big = dict(zip(_LATE, zip(late_parts, late_recv)))
    return loss_part, grad_x, small, dlead[PAD:BLOCK], big, win_flight


def _all_gather(arrs, dtypes, name):
    n = len(arrs)

    def body(*refs):
        ins, outs, stages = refs[:n], refs[n:2 * n], refs[2 * n:3 * n]
        send_sems, recv_sems, local_sems = refs[3 * n:]
        x, y, c = _place()
        sibling = (x, y, 1 - c)
        chips = [(1 - x, y), (x, 1 - y), (1 - x, 1 - y)]
        slot = lambda px, py, pc: 4 * px + 2 * py + pc

        def copy(w, k, block, to, from_stage=False):
            return pltpu.make_async_remote_copy(
                src_ref=stages[w] if from_stage else outs[w].at[slot(*block)], dst_ref=outs[w].at[slot(*block)],
                send_sem=send_sems.at[w, k], recv_sem=recv_sems.at[w, k], device_id=to, device_id_type=MESH)

        mine, first, passed = [], [], []
        for w in range(n):
            stages[w][...] = ins[w][...].astype(dtypes[w])
            mine.append(pltpu.make_async_copy(stages[w], outs[w].at[slot(x, y, c)], local_sems.at[w]))
            mine[-1].start()
        for w in range(n):
            first.append(copy(w, 0, (x, y, c), sibling, from_stage=True))
            first += [copy(w, 1 + j, (x, y, c), (*chip, c), from_stage=True) for j, chip in enumerate(chips)]
        for cp in first:
            cp.start()
        for j, chip in enumerate(chips):
            for w in range(n):
                copy(w, 1 + j, (*chip, c), (x, y, c)).wait_recv()
                passed.append(copy(w, 4 + j, (*chip, c), sibling))
                passed[-1].start()
        for w in range(n):
            copy(w, 0, sibling, (x, y, c)).wait_recv()
            for j, chip in enumerate(chips):
                copy(w, 4 + j, (*chip, 1 - c), (x, y, c)).wait_recv()
        for cp in first + passed:
            cp.wait_send()
        for cp in mine:
            cp.wait()

    return pl.pallas_call(
        body, name=name,
        in_specs=[pl.BlockSpec(memory_space=pltpu.VMEM)] * n,
        out_specs=[pl.BlockSpec(memory_space=pl.ANY)] * n,
        out_shape=[jax.ShapeDtypeStruct((N_DEV,) + a.shape, dt) for a, dt in zip(arrs, dtypes)],
        scratch_shapes=[pltpu.VMEM(a.shape, dt) for a, dt in zip(arrs, dtypes)]
        + [pltpu.SemaphoreType.DMA((n, 7)), pltpu.SemaphoreType.DMA((n, 7)), pltpu.SemaphoreType.DMA((n,))],
        compiler_params=pltpu.CompilerParams(vmem_limit_bytes=VMEM_LIMIT_BYTES),
    )(*arrs)


def _cast_shards(arrs):
    def body(*refs):
        for src, dst in zip(refs[:len(arrs)], refs[len(arrs):]):
            dst[...] = src[...].astype(BF16)

    return pl.pallas_call(body, name="cast_shards", out_shape=[jax.ShapeDtypeStruct(a.shape, BF16) for a in arrs],
                          compiler_params=pltpu.CompilerParams(vmem_limit_bytes=VMEM_LIMIT_BYTES))(*arrs)


def _shard_rows(rows):
    return rows if rows <= 512 else 256


def _adamw_math(w, g, m, v):
    m = ADAM_B1 * m + (1.0 - ADAM_B1) * g
    v = ADAM_B2 * v + (1.0 - ADAM_B2) * (g * g)
    m_hat = m / (1.0 - ADAM_B1 ** ADAM_STEP)
    v_hat = v / (1.0 - ADAM_B2 ** ADAM_STEP)
    delta = -ADAM_LR * (m_hat / (jnp.sqrt(v_hat) + ADAM_EPS) + ADAM_WD * w)
    return delta, m, v


def _reduce_adamw(parts, recv, own_slot, w, m, v, name):
    r, cdim = w.shape
    tr = _shard_rows(r)

    def body(idx_ref, p_ref, r_ref, w_ref, m_ref, v_ref, g_out, d_out, m_out, v_out):
        g = p_ref[0].astype(F32)
        for j in range(N_PEERS):
            g = g + r_ref[j].astype(F32)
        d, mn, vn = _adamw_math(w_ref[...], g, m_ref[...], v_ref[...])
        g_out[...] = g
        d_out[...] = d
        m_out[...] = mn
        v_out[...] = vn

    flat = pl.BlockSpec((tr, cdim), lambda i, idx_ref: (i, 0))
    return pl.pallas_call(
        body, name=name,
        grid_spec=pltpu.PrefetchScalarGridSpec(
            num_scalar_prefetch=1, grid=(r // tr,),
            in_specs=[pl.BlockSpec((1, tr, cdim), lambda i, idx_ref: (idx_ref[0], i, 0)),
                      pl.BlockSpec((N_PEERS, tr, cdim), lambda i, idx_ref: (0, i, 0)), flat, flat, flat],
            out_specs=[flat] * 4),
        out_shape=[jax.ShapeDtypeStruct((r, cdim), F32)] * 4,
        compiler_params=_cparams(("arbitrary",)),
    )(own_slot, parts, recv, w, m, v)


def _adamw_plain(w, g, m, v, name):
    def body(w_ref, g_ref, m_ref, v_ref, d_out, m_out, v_out):
        d_out[...], m_out[...], v_out[...] = _adamw_math(w_ref[...], g_ref[...], m_ref[...], v_ref[...])

    return pl.pallas_call(body, name=name, out_shape=[jax.ShapeDtypeStruct(w.shape, F32)] * 3)(w, g, m, v)


_SMALL_LAYOUT = (("ln_emb_g", 8), ("ln_emb_b", 8), ("hg_lower_bounds", 8), ("hg_norm_g", 1), ("attn_sinks", 1),
                 ("ln1_g", 8), ("ln1_b", 8), ("ln2_g", 8), ("ln2_b", 8))
_META_ROW = sum(r for _, r in _SMALL_LAYOUT)
_META_ROWS = N_META * D_MODEL // BLOCK
_LOSS_ROW = _META_ROW + _META_ROWS
SMALL_ROWS = 192


def _pack_small(vals, meta=None, loss_row=None):
    rows = []
    for name, nrows in _SMALL_LAYOUT:
        flat = vals[name].reshape(-1).astype(F32)
        flat = jnp.pad(flat, (0, nrows * BLOCK - flat.shape[0]))
        rows.append(flat.reshape(nrows, BLOCK))
    rows.append(jnp.zeros((_META_ROWS, BLOCK), F32) if meta is None else meta.reshape(_META_ROWS, BLOCK))
    rows.append(jnp.zeros((1, BLOCK), F32) if loss_row is None else loss_row)
    packed = jnp.concatenate(rows, axis=0)
    return jnp.pad(packed, ((0, SMALL_ROWS - packed.shape[0]), (0, 0)))


def _unpack_small(packed, shapes):
    out, row = {}, 0
    for name, nrows in _SMALL_LAYOUT:
        size = math.prod(shapes[name])
        out[name] = packed[row:row + nrows].reshape(-1)[:size].reshape(shapes[name])
        row += nrows
    return out


def _small_reduce_adamw(gathered, w, m, v):
    def body(g_ref, w_ref, m_ref, v_ref, g_out, d_out, m_out, v_out, loss_out):
        g = g_ref[0]
        for s in range(1, N_DEV):
            g = g + g_ref[s]
        d, mn, vn = _adamw_math(w_ref[...], g, m_ref[...], v_ref[...])
        g_out[...] = g
        d_out[...] = d
        m_out[...] = mn
        v_out[...] = vn
        loss_out[...] = jnp.broadcast_to(jnp.sum(g_ref[:, _LOSS_ROW, :]), (1, BLOCK))

    shp = jax.ShapeDtypeStruct((SMALL_ROWS, BLOCK), F32)
    return pl.pallas_call(body, name="small_reduce_adamw",
                          out_shape=[shp] * 4 + [jax.ShapeDtypeStruct((1, BLOCK), F32)])(gathered, w, m, v)


_WEIGHTS = ("meta_tokens", "ln_emb_g", "ln_emb_b", "w_in", "hg_lower_bounds", "hg_norm_g", "attn_sinks",
            "w_branch_hg", "w_branch_attn", "w_out", "ln1_g", "ln1_b", "w_ffn_in", "w_ffn_out", "ln2_g", "ln2_b")


def kernel(x, meta_tokens, ln_emb_g, ln_emb_b, w_in, hg_lower_bounds, hg_norm_g, attn_sinks, w_branch_hg, w_branch_attn, w_out, ln1_g, ln1_b, w_ffn_in, w_ffn_out, ln2_g, ln2_b, loss_target, m_meta_tokens, m_ln_emb_g, m_ln_emb_b, m_w_in, m_hg_lower_bounds, m_hg_norm_g, m_attn_sinks, m_w_branch_hg, m_w_branch_attn, m_w_out, m_ln1_g, m_ln1_b, m_w_ffn_in, m_w_ffn_out, m_ln2_g, m_ln2_b, v_meta_tokens, v_ln_emb_g, v_ln_emb_b, v_w_in, v_hg_lower_bounds, v_hg_norm_g, v_attn_sinks, v_w_branch_hg, v_w_branch_attn, v_w_out, v_ln1_g, v_ln1_b, v_w_ffn_in, v_w_ffn_out, v_ln2_g, v_ln2_b):
    given = dict(locals())
    weights = {n: given[n] for n in _WEIGHTS}
    mom1 = {n: given["m_" + n] for n in _WEIGHTS}
    mom2 = {n: given["v_" + n] for n in _WEIGHTS}
    shard2d = lambda a: a.reshape(a.shape[-2:])

    w_in_shard, *late_shards = _cast_shards([shard2d(weights[n]) for n in ("w_in",) + _LATE])
    loss_part, grad_x, small_grads, meta_grad, big, win_flight = _device_step(
        x[0], loss_target[0], meta_tokens, ln_emb_g.reshape(1, -1), ln_emb_b.reshape(1, -1), w_in_shard,
        hg_lower_bounds, hg_norm_g, attn_sinks, late_shards, ln1_g, ln1_b, ln2_g, ln2_b)

    place = _place()
    out = {}

    def reduce_adamw(n, parts, recv):
        own = _slot(place, n in _SWAPPED).astype(jnp.int32).reshape(1)
        res = _reduce_adamw(parts, recv, own, shard2d(weights[n]), shard2d(mom1[n]), shard2d(mom2[n]), "adamw_" + n)
        out[n] = [r.reshape(weights[n].shape) for r in res]

    for n, (parts, recv) in big.items():
        reduce_adamw(n, parts, recv)

    small_names = [n for n, _ in _SMALL_LAYOUT]
    packed = _pack_small(small_grads, meta_grad, loss_part)
    all_small, = _all_gather([packed], [F32], "gather_small")
    res = _small_reduce_adamw(all_small, _pack_small(weights), _pack_small(mom1), _pack_small(mom2))
    shapes = {n: weights[n].shape for n in small_names}
    unpacked = [_unpack_small(r, shapes) for r in res[:4]]
    for n in small_names:
        out[n] = [u[n] for u in unpacked]
    loss = res[4][0, 0]
    meta_whole = res[0][_META_ROW:_META_ROW + _META_ROWS].reshape(N_META, N_DEV, D_MODEL // N_DEV)
    g_meta_mine = lax.dynamic_index_in_dim(meta_whole, _slot(place, False), axis=1, keepdims=False)
    out["meta_tokens"] = [g_meta_mine, *_adamw_plain(meta_tokens, g_meta_mine, m_meta_tokens, v_meta_tokens,
                                                     "adamw_meta")]

    reduce_adamw("w_in", *_exchange_wait(*win_flight, after=all_small, name="w_in_grads_wait"))

    return (loss, grad_x[None], *[out[n][0] for n in _WEIGHTS], *[out[n][1] for n in _WEIGHTS],
            *[out[n][2] for n in _WEIGHTS], *[out[n][3] for n in _WEIGHTS])
```

```python
import functools
import math

import numpy as np
import jax
import jax.numpy as jnp
from jax import lax
from jax.experimental import pallas as pl
from jax.experimental.pallas import tpu as pltpu

F32 = jnp.float32
BF16 = jnp.bfloat16

D_MODEL = 1024
N_META = 16
BLOCK = 128
PAD = BLOCK - N_META
HG_HEADS = 4
HG_W = 512
ATT_HEADS = 8
HEAD_DIM = 64
ATT_QW = 512
ATT_KVW = 128
D_FF = 2816
EPS = 1e-5
ALPHA = 2.0 ** 0.25
ROPE_THETA = 10000.0
N_DEV = 8

ADAM_LR = 0.001
ADAM_B1 = 0.9
ADAM_B2 = 0.999
ADAM_EPS = 1e-08
ADAM_WD = 0.01
ADAM_STEP = 10

VMEM_LIMIT_BYTES = 56 * 1024 * 1024
MESH = pl.DeviceIdType.MESH

_LEVELS = (64, 32, 16, 8, 4, 2, 1)


def _cparams(sem):
    return pltpu.CompilerParams(dimension_semantics=sem, vmem_limit_bytes=VMEM_LIMIT_BYTES)


def _row_tile(rows, target):
    nb = rows // BLOCK
    best = 1
    for d in range(1, nb + 1):
        if nb % d == 0 and d * BLOCK <= target:
            best = d
    return best * BLOCK


_DN = {"nn": (((1,), (0,)), ((), ())), "nt": (((1,), (1,)), ((), ())), "tn": (((0,), (0,)), ((), ()))}


def _dot(a, b, form):
    return lax.dot_general(a.astype(BF16), b.astype(BF16), _DN[form], preferred_element_type=F32)


@functools.partial(jax.custom_vjp, nondiff_argnums=(2,))
def _mm(a, b, form):
    return _dot(a, b, form)


def _mm_fwd(a, b, form):
    a, b = a.astype(BF16), b.astype(BF16)
    return _dot(a, b, form), (a, b)


def _mm_bwd(form, res, g):
    a, b = res
    if form == "nn":
        return _dot(g, b, "nt"), _dot(a, g, "tn")
    if form == "nt":
        return _dot(g, b, "nn"), _dot(g, a, "tn")
    return _dot(b, g, "nt"), _dot(a, g, "nn")


_mm.defvjp(_mm_fwd, _mm_bwd)


def _split_dot(lv, x, form):
    return lax.dot_general(lv, x.astype(BF16), _DN[form], preferred_element_type=F32)


@jax.custom_vjp
def _swap_halves(x):
    return pltpu.roll(x, 64, 1)


_swap_halves.defvjp(lambda x: (pltpu.roll(x, 64, 1), None), lambda _, g: (pltpu.roll(g, 64, 1),))


def _tiled_matmul_tn(a, b, *, tm, tk, tn, out_dtype, name):
    m, k = a.shape
    n = b.shape[1]
    assert m % tm == 0 and k % tk == 0 and n % tn == 0, (name, a.shape, b.shape, tm, tk, tn)
    nm = m // tm

    def body(a_ref, b_ref, o_ref, acc_ref):
        mi = pl.program_id(2)

        @pl.when(mi == 0)
        def _():
            acc_ref[...] = jnp.zeros_like(acc_ref)

        acc_ref[...] += _dot(a_ref[...], b_ref[...], "tn")

        @pl.when(mi == nm - 1)
        def _():
            o_ref[...] = acc_ref[...].astype(out_dtype)

    return pl.pallas_call(
        body, name=name, grid=(k // tk, n // tn, nm),
        in_specs=[pl.BlockSpec((tm, tk), lambda kk, j, i: (i, kk)), pl.BlockSpec((tm, tn), lambda kk, j, i: (i, j))],
        out_specs=pl.BlockSpec((tk, tn), lambda kk, j, i: (kk, j)),
        out_shape=jax.ShapeDtypeStruct((k, n), out_dtype),
        scratch_shapes=[pltpu.VMEM((tk, tn), F32)],
        compiler_params=_cparams(("arbitrary", "arbitrary", "arbitrary")),
    )(a, b)


def _ln_stats(r):
    mu = jnp.mean(r, axis=-1, keepdims=True)
    xc = r - mu
    var = jnp.mean(xc * xc, axis=-1, keepdims=True)
    rstd = lax.rsqrt(var + EPS)
    return xc * rstd, rstd


def _ln_bwd(dy, xhat, rstd, g):
    dxhat = dy * g
    m1 = jnp.mean(dxhat, axis=-1, keepdims=True)
    m2 = jnp.mean(dxhat * xhat, axis=-1, keepdims=True)
    dr = rstd * (dxhat - m1 - xhat * m2)
    return dr, jnp.sum(dy * xhat, axis=0, keepdims=True), jnp.sum(dy, axis=0, keepdims=True)


N_SEG = 3 + len(_LEVELS)


def _level_stack():
    t = np.arange(BLOCK)[:, None]
    r = np.arange(BLOCK)[None, :]
    mats = [r <= t, r > t, np.ones((BLOCK, BLOCK), bool)]
    for h in _LEVELS:
        same = (t // (2 * h)) == (r // (2 * h))
        up_t, up_r = (t % (2 * h)) >= h, (r % (2 * h)) >= h
        mats.append(same & ((up_t & up_r & (r <= t)) | (~up_t & ~up_r & (r > t))))
    return jnp.asarray(np.concatenate(mats, axis=0).astype(np.float32), dtype=BF16)


def _hgrn_gates(hf, a0, a1, valid):
    lb = jax.nn.sigmoid(a0 - a1)
    fg = lb + (1.0 - lb) * jax.nn.sigmoid(hf)
    return jnp.where(valid, jnp.log(fg), 0.0), jnp.where(valid, 1.0 - fg, 0.0)


def _hgrn_scores(hq, k, *levels):
    q = jax.nn.silu(hq)
    rows = lax.broadcasted_iota(jnp.int32, (BLOCK, BLOCK), 0)
    cols = lax.broadcasted_iota(jnp.int32, (BLOCK, BLOCK), 1)
    a = jnp.where(rows == cols, jnp.sum(q * k, axis=-1, keepdims=True), 0.0)
    differ = jnp.bitwise_xor(rows, cols)
    for h, lvl in zip(_LEVELS, levels):
        decay = jnp.exp(lvl)
        pair = (cols < rows) & (differ >= h) & (differ < 2 * h)
        a = a + jnp.where(pair, _mm(q * decay, k * decay, "nt"), 0.0)
    return a


def _hgrn_mix(hq, k, v, st_in, a, seg_incl, seg_after, seg_total):
    o = _mm(jax.nn.silu(hq) * jnp.exp(seg_incl), st_in, "nt") + _mm(a, v, "nn")
    return o, st_in * jnp.exp(seg_total) + _mm(v, k * jnp.exp(seg_after), "tn")


def _hgrn_norm(o, hg, ng):
    return o * lax.rsqrt(jnp.mean(o * o, axis=-1, keepdims=True) + EPS) * ng * jax.nn.silu(hg)


def _seg_blocks(e, h):
    return [e[i * BLOCK:(i + 1) * BLOCK, h * BLOCK:(h + 1) * BLOCK] for i in range(N_SEG)]


def _rope(x, cos, sin, first_half):
    partner = jnp.where(first_half, -pltpu.roll(x, 96, 1), pltpu.roll(x, 32, 1))
    return x * cos + partner * sin


def _rope_t(g, cos, sin, first_half):
    u = g * sin
    partner = jnp.where(first_half, pltpu.roll(u, 96, 1), -pltpu.roll(u, 32, 1))
    return g * cos + partner


def _low_half(x):
    return lax.broadcasted_iota(jnp.int32, x.shape, 1) < HEAD_DIM


def _both_halves(x, g):
    sw = _swap_halves(x)
    return jnp.where(_low_half(x), x, sw) if g == 0 else jnp.where(_low_half(x), sw, x)


def _att_scores(qa, qb, kc, kp, km, g, own4, band4, meta4):
    low = _low_half(qa)
    q4 = jnp.concatenate([jnp.where(low, qa, 0.0), jnp.where(low, 0.0, qa),
                          jnp.where(low, qb, 0.0), jnp.where(low, 0.0, qb)], axis=0)
    scale = HEAD_DIM ** -0.5
    neg = jnp.finfo(F32).min
    s = jnp.where(own4, _mm(_both_halves(kc, g), q4, "nt"), _mm(_both_halves(kp, g), q4, "nt"))
    return (jnp.where(band4, s * scale, neg), jnp.where(meta4, _mm(_both_halves(km, g), q4, "nt") * scale, neg))


def _att_probs(s, sm, sinkrow):
    mx = jnp.maximum(jnp.maximum(jnp.max(s, axis=0, keepdims=True), jnp.max(sm, axis=0, keepdims=True)), sinkrow)
    p, pm, ps = jnp.exp(s - mx), jnp.exp(sm - mx), jnp.exp(sinkrow - mx)
    inv = 1.0 / (jnp.sum(p, axis=0, keepdims=True) + jnp.sum(pm, axis=0, keepdims=True) + ps)
    return p * inv, pm * inv, ps * inv


def _att_probs_bwd(p, pm, ps, dp, dpm):
    r = jnp.sum(p * dp, axis=0, keepdims=True) + jnp.sum(pm * dpm, axis=0, keepdims=True)
    return p * (dp - r), pm * (dpm - r), -ps * r


def _att_values(p, pm, vc, vp, vm, g, own4):
    o4 = (_mm(jnp.where(own4, p, 0.0), _both_halves(vc, g), "tn") + _mm(jnp.where(own4, 0.0, p), _both_halves(vp, g), "tn")
          + _mm(pm, _both_halves(vm, g), "tn"))
    tiles = []
    for j in range(2):
        upper = o4[(2 * j) * BLOCK:(2 * j + 1) * BLOCK]
        tiles.append(jnp.where(_low_half(upper), upper, o4[(2 * j + 1) * BLOCK:(2 * j + 2) * BLOCK]))
    return tiles


def _att_masks(blk_idx):
    kidx = lax.broadcasted_iota(jnp.int32, (BLOCK, BLOCK), 0)
    qrow = lax.broadcasted_iota(jnp.int32, (BLOCK, BLOCK), 1)
    own_side = kidx <= qrow
    pos_own = blk_idx * BLOCK + kidx - PAD
    ok_band = (own_side & (pos_own >= N_META)) | (~own_side & (pos_own - BLOCK >= N_META) & (blk_idx >= 1))
    qpos = blk_idx * BLOCK + lax.broadcasted_iota(jnp.int32, (N_META, BLOCK), 1) - PAD
    ok_meta = lax.broadcasted_iota(jnp.int32, (N_META, BLOCK), 0) <= qpos
    return [jnp.concatenate([m] * 4, axis=1) for m in (own_side, ok_band, ok_meta)]


def _token_streams(tr, tile_of=lambda i: i):
    k = tr // BLOCK
    return [pl.BlockSpec((BLOCK, D_MODEL), lambda i, j=j: (jnp.maximum(k * tile_of(i) - 1 + j, 0), 0))
            for j in range(k)]


def _embed_ln(x, meta_shard, w_in_shard, g0, b0):
    p = x.shape[0] + BLOCK
    tr = _row_tile(p, 640)
    k = tr // BLOCK
    nt = p // tr
    tile_of = lambda s: (s + 1) % nt
    shards = [meta_shard, w_in_shard]
    c_in, c_out, c_shapes, c_sems = _comm_specs(shards, N_DEV)

    def body(*refs):
        g_ref, b_ref = refs[k:k + 2]
        h_ref, hb_ref, xh_ref, rs_ref = refs[k + 4:k + 8]
        out_refs = refs[k + 8:k + 10]
        lead_ref, meta_ref = refs[k + 10:k + 12]
        starts, passes, waits = _gather_behind(refs[k + 2:k + 4], out_refs, refs[k + 12:], [False, False])
        s = pl.program_id(0)
        t = tile_of(s)

        @pl.when(s == 0)
        def _():
            lead_ref[...] = jnp.zeros_like(lead_ref)
            for start in starts:
                start()

        @pl.when(s == nt - 1)
        def _():
            for step in passes + waits:
                step()
            pltpu.sync_copy(out_refs[0], meta_ref)
            for d in range(N_DEV):
                lead_ref[PAD:BLOCK, d * BLOCK:(d + 1) * BLOCK] = meta_ref[d]

        first = jnp.where(t == 0, lead_ref[...], refs[0][...])
        xhat, rstd = _ln_stats(jnp.concatenate([first] + [r[...] for r in refs[1:k]], axis=0))
        row = t * tr + lax.broadcasted_iota(jnp.int32, (tr, 1), 0)
        h = jnp.where(row >= PAD, xhat * g_ref[...] + b_ref[...], 0.0)
        h_ref[...] = h
        hb_ref[...] = h.astype(BF16)
        xh_ref[...] = xhat
        rs_ref[...] = rstd

    vec = pl.BlockSpec((1, D_MODEL), lambda s: (0, 0))
    rowsp = pl.BlockSpec((tr, D_MODEL), lambda s: (tile_of(s), 0))
    return pl.pallas_call(
        body, name="embed_ln", grid=(nt,),
        in_specs=_token_streams(tr, tile_of) + [vec, vec] + c_in,
        out_specs=[rowsp, rowsp, rowsp, pl.BlockSpec((tr, 1), lambda s: (tile_of(s), 0))] + c_out,
        out_shape=[jax.ShapeDtypeStruct((p, D_MODEL), F32), jax.ShapeDtypeStruct((p, D_MODEL), BF16),
                   jax.ShapeDtypeStruct((p, D_MODEL), F32), jax.ShapeDtypeStruct((p, 1), F32)] + c_shapes,
        scratch_shapes=[pltpu.VMEM((BLOCK, D_MODEL), F32), pltpu.VMEM((N_DEV, N_META, BLOCK), F32)] + c_sems,
        compiler_params=_cparams(("arbitrary",)),
    )(*([x] * k), g0, b0, *shards)


def _rope_tables(p):
    pos = (np.arange(p, dtype=np.int32) - PAD).astype(np.float32)
    half = HEAD_DIM // 2
    inv = np.float32(ROPE_THETA) ** (-np.arange(half, dtype=np.float32) / np.float32(half))
    ang = pos[:, None] * np.tile(inv.astype(np.float32), BLOCK // half)[None, :]
    return jnp.asarray(np.cos(ang), F32), jnp.asarray(np.sin(ang), F32)


def _att_sinkrows(sink_ref):
    lanehead = lax.broadcasted_iota(jnp.int32, (1, 4 * BLOCK), 1) // BLOCK
    rows = []
    for g in range(2):
        row = jnp.zeros((1, 4 * BLOCK), F32)
        for j in range(4):
            row = jnp.where(lanehead == j, sink_ref[0, 4 * g + j], row)
        rows.append(row)
    return rows


def _first_half(rows):
    return (lax.broadcasted_iota(jnp.int32, (rows, BLOCK), 1) % HEAD_DIM) < (HEAD_DIM // 2)


def _att_load(qkv_ref, cos_ref, sin_ref, with_q):
    cos, sin, fh = cos_ref[...], sin_ref[...], _first_half(BLOCK)
    qs = [_rope(qkv_ref[:, j * BLOCK:(j + 1) * BLOCK], cos, sin, fh) for j in range(4)] if with_q else None
    k = _rope(qkv_ref[:, ATT_QW:ATT_QW + ATT_KVW], cos, sin, fh)
    v = qkv_ref[:, ATT_QW + ATT_KVW:ATT_QW + 2 * ATT_KVW]
    return qs, k, v


def _att_load_meta(qkv_ref, cos_ref, sin_ref):
    k = _rope(qkv_ref[PAD:BLOCK, ATT_QW:ATT_QW + ATT_KVW], cos_ref[PAD:BLOCK, :], sin_ref[PAD:BLOCK, :],
              _first_half(N_META))
    return k, qkv_ref[PAD:BLOCK, ATT_QW + ATT_KVW:ATT_QW + 2 * ATT_KVW]


def _att_specs(blk):
    w = ATT_QW + 2 * ATT_KVW
    cur = lambda width: pl.BlockSpec((BLOCK, width), lambda i: (blk(i), 0))
    prev = lambda width: pl.BlockSpec((BLOCK, width), lambda i: (jnp.maximum(blk(i) - 1, 0), 0))
    meta = lambda width: pl.BlockSpec((BLOCK, width), lambda i: (0, 0))
    return [cur(w), prev(w), meta(w), cur(BLOCK), cur(BLOCK), prev(BLOCK), prev(BLOCK), meta(BLOCK), meta(BLOCK),
            pl.BlockSpec(memory_space=pltpu.SMEM)]


_FLIPS = [(dx, dy, dc) for dx in (0, 1) for dy in (0, 1) for dc in (0, 1)][1:]
N_PEERS = len(_FLIPS)


def _place():
    return lax.axis_index("x"), lax.axis_index("y"), lax.axis_index("c")


def _peer(place, flip):
    return tuple(1 - p if f else p for p, f in zip(place, flip))


def _slot(place, swapped):
    x, y, c = place
    return 4 * y + 2 * x + c if swapped else 4 * x + 2 * y + c


def _comm_specs(arrs, out_lead):
    n = len(arrs)
    outs = [jax.ShapeDtypeStruct((out_lead,) + a.shape[-2:], a.dtype) for a in arrs]
    sems = [pltpu.SemaphoreType.DMA((n, N_PEERS)), pltpu.SemaphoreType.DMA((n, N_PEERS)), pltpu.SemaphoreType.DMA((n,))]
    return [pl.BlockSpec(memory_space=pl.ANY)] * n, [pl.BlockSpec(memory_space=pl.ANY)] * n, outs, sems


def _gather_behind(shard_refs, out_refs, sems, swapped):
    send_sems, recv_sems, local_sems = sems
    x, y, c = _place()
    me, sibling = (x, y, c), (x, y, 1 - c)
    chips = [(1 - x, y), (x, 1 - y), (1 - x, 1 - y)]
    starts, passes, waits = [], [], []
    for w, (s, o) in enumerate(zip(shard_refs, out_refs)):
        def copy(k, block, to, from_shard=False, w=w, s=s, o=o):
            rows = o.at[_slot(block, swapped[w])]
            return pltpu.make_async_remote_copy(
                src_ref=s if from_shard else rows, dst_ref=rows, send_sem=send_sems.at[w, k],
                recv_sem=recv_sems.at[w, k], device_id=to, device_id_type=MESH)

        own = pltpu.make_async_copy(s, o.at[_slot(me, swapped[w])], local_sems.at[w])
        first = [copy(0, me, sibling, True)] + [copy(1 + j, me, (*chip, c), True) for j, chip in enumerate(chips)]
        handed = [copy(4 + j, (*chip, c), sibling) for j, chip in enumerate(chips)]
        starts += [own.start] + [cp.start for cp in first]
        for j, chip in enumerate(chips):
            passes += [copy(1 + j, (*chip, c), me).wait_recv, handed[j].start]
        waits.append(copy(0, sibling, me).wait_recv)
        waits += [copy(4 + j, (*chip, 1 - c), me).wait_recv for j, chip in enumerate(chips)]
        waits += [cp.wait_send for cp in first + handed] + [own.wait]
    return starts, passes, waits


def _scatter_behind(part_refs, recv_refs, sems, swapped):
    send_sems, recv_sems, _ = sems
    place = _place()
    starts, waits = [], []
    for w, (p, o) in enumerate(zip(part_refs, recv_refs)):
        for r, flip in enumerate(_FLIPS):
            peer = _peer(place, flip)
            cp = pltpu.make_async_remote_copy(
                src_ref=p.at[_slot(peer, swapped[w])], dst_ref=o.at[r], send_sem=send_sems.at[w, r],
                recv_sem=recv_sems.at[w, r], device_id=peer, device_id_type=MESH)
            starts.append(cp.start)
            waits += [cp.wait_recv, cp.wait_send]
    return starts, waits


def _mixers_fwd(proj_hg, proj_att, lbounds, norm_g, lv, cos, sin, sinks, shards, swapped):
    p = proj_hg.shape[0]
    nb = p // BLOCK
    n = len(shards)
    c_in, c_out, c_shapes, c_sems = _comm_specs(shards, N_DEV)
    pass_step = min(nb - 1, max(1, (5 * nb) // 8))

    def body(*refs):
        x_ref, lb_ref, ng_ref, lv_ref, cur_ref, prev_ref, meta_ref, cc, sc, cp, sp, cm, sm, sink_ref = refs[:14]
        shard_refs = refs[14:14 + n]
        y_ref, o_ref, st_ref, a_ref, raw_ref, pr_ref = refs[14 + n:20 + n]
        out_refs = refs[20 + n:20 + 2 * n]
        carry_ref = refs[20 + 2 * n]
        starts, passes, waits = _gather_behind(shard_refs, out_refs, refs[21 + 2 * n:], swapped)
        c = pl.program_id(0)

        @pl.when(c == 0)
        def _():
            carry_ref[...] = jnp.zeros_like(carry_ref)
            for start in starts:
                start()

        @pl.when(c == pass_step)
        def _():
            for step in passes:
                step()

        valid = (c * BLOCK + lax.broadcasted_iota(jnp.int32, (BLOCK, 1), 0)) >= PAD
        logf, k = _hgrn_gates(x_ref[:, HG_W:2 * HG_W], lb_ref[0:1, :], lb_ref[1:2, :], valid)
        e = _split_dot(lv_ref[...], logf, "nn")
        for h in range(HG_HEADS):
            sl = lambda part: x_ref[:, part * HG_W + h * BLOCK: part * HG_W + (h + 1) * BLOCK]
            hs = slice(h * BLOCK, (h + 1) * BLOCK)
            st_in = carry_ref[h]
            st_ref[0, h] = st_in
            seg = _seg_blocks(e, h)
            a = _hgrn_scores(sl(0), k[:, hs], *seg[3:])
            a_ref[0, h] = a.astype(BF16)
            raw, st_out = _hgrn_mix(sl(0), k[:, hs], sl(2), st_in, a, *seg[:3])
            raw_ref[:, hs] = raw
            y_ref[:, hs] = _hgrn_norm(raw, sl(3), ng_ref[...]).astype(BF16)
            carry_ref[h] = st_out

        qs, kc, vc = _att_load(cur_ref, cc, sc, True)
        _, kp, vp = _att_load(prev_ref, cp, sp, False)
        km, vm = _att_load_meta(meta_ref, cm, sm)
        sinkrows = _att_sinkrows(sink_ref)
        own4, band4, meta4 = _att_masks(c)
        for g in range(2):
            s, s_meta = _att_scores(qs[2 * g], qs[2 * g + 1], kc, kp, km, g, own4, band4, meta4)
            pr, pr_meta, pr_sink = _att_probs(s, s_meta, sinkrows[g])
            pr_ref[0, g, :BLOCK, :] = pr.astype(BF16)
            pr_ref[0, g, BLOCK:BLOCK + N_META, :] = pr_meta.astype(BF16)
            pr_ref[0, g, BLOCK + N_META:, :] = jnp.broadcast_to(pr_sink, (N_META, 4 * BLOCK)).astype(BF16)
            for j, tile in enumerate(_att_values(pr, pr_meta, vc, vp, vm, g, own4)):
                o_ref[:, (2 * g + j) * BLOCK:(2 * g + j + 1) * BLOCK] = tile.astype(BF16)

        @pl.when(c == nb - 1)
        def _():
            for wait in waits:
                wait()

    return pl.pallas_call(
        body, name="mixers_fwd", grid=(nb,),
        in_specs=[pl.BlockSpec((BLOCK, 4 * HG_W), lambda c: (c, 0)), pl.BlockSpec((2, HG_W), lambda c: (0, 0)),
                  pl.BlockSpec((1, BLOCK), lambda c: (0, 0)), pl.BlockSpec(lv.shape, lambda c: (0, 0))]
        + _att_specs(lambda c: c) + c_in,
        out_specs=[pl.BlockSpec((BLOCK, HG_W), lambda c: (c, 0)), pl.BlockSpec((BLOCK, ATT_QW), lambda c: (c, 0)),
                   pl.BlockSpec((1, HG_HEADS, BLOCK, BLOCK), lambda c: (c, 0, 0, 0)),
                   pl.BlockSpec((1, HG_HEADS, BLOCK, BLOCK), lambda c: (c, 0, 0, 0)),
                   pl.BlockSpec((BLOCK, HG_W), lambda c: (c, 0)),
                   pl.BlockSpec((1, 2, ATT_KEYS, 4 * BLOCK), lambda c: (c, 0, 0, 0))] + c_out,
        out_shape=[jax.ShapeDtypeStruct((p, HG_W), BF16), jax.ShapeDtypeStruct((p, ATT_QW), BF16),
                   jax.ShapeDtypeStruct((nb, HG_HEADS, BLOCK, BLOCK), F32),
                   jax.ShapeDtypeStruct((nb, HG_HEADS, BLOCK, BLOCK), BF16),
                   jax.ShapeDtypeStruct((p, HG_W), F32),
                   jax.ShapeDtypeStruct((nb, 2, ATT_KEYS, 4 * BLOCK), BF16)] + c_shapes,
        scratch_shapes=[pltpu.VMEM((HG_HEADS, BLOCK, BLOCK), F32)] + c_sems,
        compiler_params=_cparams(("arbitrary",)),
    )(proj_hg, lbounds, norm_g, lv, proj_att, proj_att, proj_att, cos, sin, cos, sin, cos, sin, sinks, *shards)


def _tile(rows, preferred):
    return preferred if rows % preferred == 0 else _row_tile(rows, preferred)


def _in_proj(h0b, w_in):
    p = h0b.shape[0]
    tm = _row_tile(p, 640)
    hg_end = 4 * HG_W

    def body(h_ref, w_ref, hg_ref, att_ref, gates_ref):
        h = h_ref[...]
        hg_ref[...] = _dot(h, w_ref[:, :hg_end], "nn")
        att_ref[...] = _dot(h, w_ref[:, hg_end:MIX_W], "nn")
        gates_ref[...] = _dot(h, w_ref[:, MIX_W:], "nn").astype(BF16)

    row = lambda w: pl.BlockSpec((tm, w), lambda i: (i, 0))
    return pl.pallas_call(
        body, name="in_proj", grid=(p // tm,),
        in_specs=[row(D_MODEL), pl.BlockSpec(w_in.shape, lambda i: (0, 0), pipeline_mode=pl.Buffered(1))],
        out_specs=[row(hg_end), row(MIX_W - hg_end), row(2 * D_MODEL)],
        out_shape=[jax.ShapeDtypeStruct((p, hg_end), F32), jax.ShapeDtypeStruct((p, MIX_W - hg_end), F32),
                   jax.ShapeDtypeStruct((p, 2 * D_MODEL), BF16)],
        compiler_params=_cparams(("arbitrary",)),
    )(h0b, w_in)


def _branch_mix(yh, oa, gates, w_bh, w_ba):
    y_hg = _dot(yh, w_bh, "nn")
    y_att = _dot(oa, w_ba, "nn")
    s1 = jax.nn.sigmoid(gates[:, :D_MODEL].astype(F32))
    s2 = jax.nn.sigmoid(gates[:, D_MODEL:].astype(F32))
    return s1 * y_hg + s2 * y_att, y_hg, y_att, s1, s2


def _mix_out_ln1(yh, oa, gates, h0, w_bh, w_ba, w_out, g1, b1):
    p = yh.shape[0]
    tr = _tile(p, 320)

    def body(yh_ref, oa_ref, g_ref, h0_ref, wbh_ref, wba_ref, wo_ref, g1_ref, b1_ref,
             mix_ref, h1_ref, h1b_ref, xh_ref, rs_ref):
        mixin = _branch_mix(yh_ref[...], oa_ref[...], g_ref[...], wbh_ref[...], wba_ref[...])[0]
        mix_ref[...] = mixin.astype(BF16)
        xhat, rstd = _ln_stats(ALPHA * h0_ref[...] + _dot(mixin, wo_ref[...], "nn"))
        h1 = xhat * g1_ref[...] + b1_ref[...]
        h1_ref[...] = h1
        h1b_ref[...] = h1.astype(BF16)
        xh_ref[...] = xhat
        rs_ref[...] = rstd

    row = lambda w: pl.BlockSpec((tr, w), lambda i: (i, 0))
    const = lambda a: pl.BlockSpec(a.shape, lambda i: (0, 0))
    return pl.pallas_call(
        body, name="mix_out_ln1", grid=(p // tr,),
        in_specs=[row(HG_W), row(ATT_QW), row(2 * D_MODEL), row(D_MODEL), const(w_bh), const(w_ba), const(w_out),
                  const(g1), const(b1)],
        out_specs=[row(D_MODEL), row(D_MODEL), row(D_MODEL), row(D_MODEL), row(1)],
        out_shape=[jax.ShapeDtypeStruct((p, D_MODEL), BF16), jax.ShapeDtypeStruct((p, D_MODEL), F32),
                   jax.ShapeDtypeStruct((p, D_MODEL), BF16), jax.ShapeDtypeStruct((p, D_MODEL), F32),
                   jax.ShapeDtypeStruct((p, 1), F32)],
        compiler_params=_cparams(("arbitrary",)),
    )(yh, oa, gates, h0, w_bh, w_ba, w_out, g1, b1)


FF_T = D_FF // 2


def _ffn_in_swiglu(h1, w_fi):
    p = h1.shape[0]
    tm = _row_tile(p, 640)

    def body(h_ref, w_ref, au_ref, s_ref):
        au = _dot(h_ref[...], w_ref[...], "nn")
        au_ref[...] = au.astype(BF16)
        s_ref[...] = (jax.nn.silu(au[:, :FF_T]) * au[:, FF_T:]).astype(BF16)

    return pl.pallas_call(
        body, name="ffn_in_swiglu", grid=(D_FF // FF_T, p // tm),
        in_specs=[pl.BlockSpec((tm, D_MODEL), lambda j, i: (i, 0)), pl.BlockSpec((D_MODEL, 2 * FF_T), lambda j, i: (0, j))],
        out_specs=[pl.BlockSpec((tm, 2 * FF_T), lambda j, i: (i, j)), pl.BlockSpec((tm, FF_T), lambda j, i: (i, j))],
        out_shape=[jax.ShapeDtypeStruct((p, 2 * D_FF), BF16), jax.ShapeDtypeStruct((p, D_FF), BF16)],
        compiler_params=_cparams(("arbitrary", "arbitrary")),
    )(h1, w_fi)


def _ffn_out_loss(s, w_fo, h1, g2, b2, target):
    p = h1.shape[0]
    tr = _row_tile(p, 640)
    k = tr // BLOCK

    def body(*refs):
        s_ref, w_ref, h_ref, g_ref, b_ref = refs[:5]
        dr_ref, drb_ref, loss_ref, dg_ref, db_ref = refs[5 + k:]
        i = pl.program_id(0)
        xhat, rstd = _ln_stats(ALPHA * h_ref[...] + _dot(s_ref[...], w_ref[...], "nn"))
        y = xhat * g_ref[...] + b_ref[...]
        row = i * tr + lax.broadcasted_iota(jnp.int32, (tr, 1), 0)
        tgt = jnp.concatenate([r[...] for r in refs[5:5 + k]], axis=0)
        err = jnp.where(row >= BLOCK, y - tgt, 0.0)
        dr, dg, db = _ln_bwd(err * (1.0 / D_MODEL), xhat, rstd, g_ref[...])
        dr_ref[...] = dr
        drb_ref[...] = dr.astype(BF16)
        e2 = jnp.sum(err * err, axis=0, keepdims=True)
        part = e2[:, 0:BLOCK]
        for j in range(1, D_MODEL // BLOCK):
            part = part + e2[:, j * BLOCK:(j + 1) * BLOCK]
        part = part * (0.5 / D_MODEL)

        @pl.when(i == 0)
        def _():
            loss_ref[...] = part
            dg_ref[...] = dg
            db_ref[...] = db

        @pl.when(i > 0)
        def _():
            loss_ref[...] += part
            dg_ref[...] += dg
            db_ref[...] += db

    vec = pl.BlockSpec((1, D_MODEL), lambda i: (0, 0))
    rowsp = pl.BlockSpec((tr, D_MODEL), lambda i: (i, 0))
    return pl.pallas_call(
        body, name="ffn_out_loss", grid=(p // tr,),
        in_specs=[pl.BlockSpec((tr, D_FF), lambda i: (i, 0)), pl.BlockSpec((D_FF, D_MODEL), lambda i: (0, 0)),
                  rowsp, vec, vec] + _token_streams(tr),
        out_specs=[rowsp, rowsp, pl.BlockSpec((1, BLOCK), lambda i: (0, 0)), vec, vec],
        out_shape=[jax.ShapeDtypeStruct((p, D_MODEL), F32), jax.ShapeDtypeStruct((p, D_MODEL), BF16),
                   jax.ShapeDtypeStruct((1, BLOCK), F32), jax.ShapeDtypeStruct((1, D_MODEL), F32),
                   jax.ShapeDtypeStruct((1, D_MODEL), F32)],
        compiler_params=_cparams(("arbitrary",)),
    )(s, w_fo, h1, g2, b2, *([target] * k))


def _ffn_bwd(dr2, w_fo, au, w_fi):
    p = au.shape[0]
    tm = _tile(p, 320)

    def body(d_ref, wo_ref, au_ref, wi_ref, dau_ref, dh_ref):
        d = d_ref[...]
        dh = 0.0
        for j in range(D_FF // FF_T):
            a_cols = slice(2 * j * FF_T, (2 * j + 1) * FF_T)
            u_cols = slice((2 * j + 1) * FF_T, (2 * j + 2) * FF_T)
            ds = _dot(d, wo_ref[j * FF_T:(j + 1) * FF_T, :], "nt")
            _, vjp = jax.vjp(lambda a, u: jax.nn.silu(a) * u, au_ref[:, a_cols].astype(F32), au_ref[:, u_cols].astype(F32))
            da, du = vjp(ds)
            da, du = da.astype(BF16), du.astype(BF16)
            dau_ref[:, a_cols] = da
            dau_ref[:, u_cols] = du
            dh = dh + _dot(da, wi_ref[:, a_cols], "nt") + _dot(du, wi_ref[:, u_cols], "nt")
        dh_ref[...] = dh

    row = lambda w: pl.BlockSpec((tm, w), lambda i: (i, 0))
    kept = lambda a: pl.BlockSpec(a.shape, lambda i: (0, 0), pipeline_mode=pl.Buffered(1))
    return pl.pallas_call(
        body, name="ffn_bwd", grid=(p // tm,),
        in_specs=[row(D_MODEL), kept(w_fo), row(2 * D_FF), kept(w_fi)],
        out_specs=[row(2 * D_FF), row(D_MODEL)],
        out_shape=[jax.ShapeDtypeStruct((p, 2 * D_FF), BF16), jax.ShapeDtypeStruct((p, D_MODEL), F32)],
        compiler_params=_cparams(("arbitrary",)),
    )(dr2, w_fo, au, w_fi)


def _ln1_mix_bwd(dr2, dh1_ffn, xhat1, rstd1, g1, yh, oa, gates, w_bh, w_ba, w_out):
    p = yh.shape[0]
    tr = _tile(p, 320)

    def body(a_ref, b_ref, xh_ref, rs_ref, g1_ref, yh_ref, oa_ref, g_ref, wbh_ref, wba_ref, wo_ref,
             dr_ref, dyhg_ref, dyat_ref, dgt_ref, dyh_ref, doa_ref, dg_ref, db_ref):
        i = pl.program_id(0)
        dr, dg, db = _ln_bwd(ALPHA * a_ref[...] + b_ref[...], xh_ref[...], rs_ref[...], g1_ref[...])
        dr_ref[...] = dr
        d = _dot(dr, wo_ref[...], "nt")
        _, y_hg, y_att, s1, s2 = _branch_mix(yh_ref[...], oa_ref[...], g_ref[...], wbh_ref[...], wba_ref[...])
        dy_hg = d * s1
        dy_att = d * s2
        dyhg_ref[...] = dy_hg.astype(BF16)
        dyat_ref[...] = dy_att.astype(BF16)
        dgt_ref[:, :D_MODEL] = (d * y_hg * s1 * (1.0 - s1)).astype(BF16)
        dgt_ref[:, D_MODEL:] = (d * y_att * s2 * (1.0 - s2)).astype(BF16)
        dyh_ref[...] = _dot(dy_hg, wbh_ref[...], "nt")
        doa_ref[...] = _dot(dy_att, wba_ref[...], "nt")

        @pl.when(i == 0)
        def _():
            dg_ref[...] = dg
            db_ref[...] = db

        @pl.when(i > 0)
        def _():
            dg_ref[...] += dg
            db_ref[...] += db

    row = lambda w: pl.BlockSpec((tr, w), lambda i: (i, 0))
    const = lambda a: pl.BlockSpec(a.shape, lambda i: (0, 0))
    vec = pl.BlockSpec((1, D_MODEL), lambda i: (0, 0))
    return pl.pallas_call(
        body, name="ln1_mix_bwd", grid=(p // tr,),
        in_specs=[row(D_MODEL), row(D_MODEL), row(D_MODEL), row(1), vec, row(HG_W), row(ATT_QW), row(2 * D_MODEL),
                  const(w_bh), const(w_ba), const(w_out)],
        out_specs=[row(D_MODEL), row(D_MODEL), row(D_MODEL), row(2 * D_MODEL), row(HG_W), row(ATT_QW), vec, vec],
        out_shape=[jax.ShapeDtypeStruct((p, D_MODEL), F32), jax.ShapeDtypeStruct((p, D_MODEL), BF16),
                   jax.ShapeDtypeStruct((p, D_MODEL), BF16), jax.ShapeDtypeStruct((p, 2 * D_MODEL), BF16),
                   jax.ShapeDtypeStruct((p, HG_W), F32), jax.ShapeDtypeStruct((p, ATT_QW), F32),
                   jax.ShapeDtypeStruct((1, D_MODEL), F32), jax.ShapeDtypeStruct((1, D_MODEL), F32)],
        compiler_params=_cparams(("arbitrary",)),
    )(dr2, dh1_ffn, xhat1, rstd1, g1, yh, oa, gates, w_bh, w_ba, w_out)


MIX_W = 4 * HG_W + ATT_QW + 2 * ATT_KVW
ATT_KEYS = BLOCK + 2 * N_META


def _mixers_bwd(proj_hg, proj_att, lbounds, norm_g, lv, states, scores, raw, probs, cos, sin, sinks, dyh, doa,
                parts, swapped):
    p = proj_hg.shape[0]
    nb = p // BLOCK
    n = len(parts)
    kvw = 2 * ATT_KVW
    rev = lambda s: nb - 1 - s
    c_in, c_out, c_shapes, c_sems = _comm_specs(parts, N_PEERS)

    def body(*refs):
        (x_ref, lb_ref, ng_ref, lv_ref, st_ref, a_ref, raw_ref, pr_ref, cur_ref, prev_ref, meta_ref, cc, sc, cp, sp,
         cm, sm, sink_ref, dy_ref, do_ref) = refs[:20]
        part_refs = refs[20:20 + n]
        dx_ref, dlb_ref, dng_ref, dsink_ref = refs[20 + n:24 + n]
        recv_refs = refs[24 + n:24 + 2 * n]
        dcarry_ref, dkv_next_ref, dkv_meta_ref = refs[24 + 2 * n:27 + 2 * n]
        starts, waits = _scatter_behind(part_refs, recv_refs, refs[27 + 2 * n:], swapped)
        step = pl.program_id(0)
        c = rev(step)

        @pl.when(step == 0)
        def _():
            dcarry_ref[...] = jnp.zeros_like(dcarry_ref)
            dkv_next_ref[...] = jnp.zeros_like(dkv_next_ref)
            dkv_meta_ref[...] = jnp.zeros_like(dkv_meta_ref)
            dlb_ref[...] = jnp.zeros_like(dlb_ref)
            dng_ref[...] = jnp.zeros_like(dng_ref)
            dsink_ref[...] = jnp.zeros_like(dsink_ref)
            for start in starts:
                start()

        fh = _first_half(BLOCK)
        qs, kc, vc = _att_load(cur_ref, cc, sc, True)
        _, kp, vp = _att_load(prev_ref, cp, sp, False)
        km, vm = _att_load_meta(meta_ref, cm, sm)
        own4, band4, meta4 = _att_masks(c)
        att0 = 4 * HG_W
        dkm = dkp = dkc = dvm = dvp = dvc = 0.0
        dsinkrows = []
        for g in range(2):
            pr = pr_ref[0, g, :BLOCK, :].astype(F32)
            pr_meta = pr_ref[0, g, BLOCK:BLOCK + N_META, :].astype(F32)
            pr_sink = jnp.max(pr_ref[0, g, BLOCK + N_META:, :].astype(F32), axis=0, keepdims=True)
            _, values_vjp = jax.vjp(lambda *a, g=g: _att_values(*a, g, own4), pr, pr_meta, vc, vp, vm)
            dpr, dpr_meta, dvc_g, dvp_g, dvm_g = values_vjp(
                [do_ref[:, (2 * g + j) * BLOCK:(2 * g + j + 1) * BLOCK] for j in range(2)])
            ds, ds_meta, dsinkrow = _att_probs_bwd(pr, pr_meta, pr_sink, dpr, dpr_meta)
            _, scores_vjp = jax.vjp(lambda *a, g=g: _att_scores(*a, g, own4, band4, meta4),
                                    qs[2 * g], qs[2 * g + 1], kc, kp, km)
            dqa, dqb, dkc_g, dkp_g, dkm_g = scores_vjp((ds, ds_meta))
            for j, dq in enumerate((dqa, dqb)):
                dx_ref[:, att0 + (2 * g + j) * BLOCK:att0 + (2 * g + j + 1) * BLOCK] = _rope_t(
                    dq, cc[...], sc[...], fh).astype(BF16)
            dkm, dkp, dkc = dkm + dkm_g, dkp + dkp_g, dkc + dkc_g
            dvm, dvp, dvc = dvm + dvm_g, dvp + dvp_g, dvc + dvc_g
            dsinkrows.append(dsinkrow)
        ds0, ds1 = dsinkrows
        dkv_meta_ref[:, :BLOCK] += _rope_t(dkm, cm[PAD:BLOCK, :], sm[PAD:BLOCK, :], _first_half(N_META))
        dkv_meta_ref[:, BLOCK:] += dvm
        last = jnp.where(c == 0, 1.0, 0.0)
        to_meta_rows = lambda m: jnp.concatenate([jnp.zeros((PAD, BLOCK), F32), last * m], axis=0)
        dk = _rope_t(dkc, cc[...], sc[...], fh) + dkv_next_ref[:, :BLOCK] + to_meta_rows(dkv_meta_ref[:, :BLOCK])
        dv = dvc + dkv_next_ref[:, BLOCK:] + to_meta_rows(dkv_meta_ref[:, BLOCK:])
        dx_ref[:, att0 + ATT_QW:att0 + ATT_QW + ATT_KVW] = dk.astype(BF16)
        dx_ref[:, att0 + ATT_QW + ATT_KVW:] = dv.astype(BF16)
        dkv_next_ref[:, :BLOCK] = _rope_t(dkp, cp[...], sp[...], fh)
        dkv_next_ref[:, BLOCK:] = dvp
        sink_rows = []
        for dsg in (ds0, ds1):
            for j in range(4):
                tot = jnp.sum(dsg[:, j * BLOCK:(j + 1) * BLOCK], axis=1, keepdims=True)
                sink_rows.append(jnp.broadcast_to(tot, (1, BLOCK)))
        dsink_ref[...] += jnp.concatenate(sink_rows, axis=0)

        valid = (c * BLOCK + lax.broadcasted_iota(jnp.int32, (BLOCK, 1), 0)) >= PAD
        (logf, k), gates_vjp = jax.vjp(lambda hf, a0, a1: _hgrn_gates(hf, a0, a1, valid),
                                       x_ref[:, HG_W:2 * HG_W], lb_ref[0:1, :], lb_ref[1:2, :])
        lvv = lv_ref[...]
        e = _split_dot(lvv, logf, "nn")
        dng = jnp.zeros((1, BLOCK), F32)
        dk, dseg = [], []
        for h in range(HG_HEADS):
            sl = lambda part: x_ref[:, part * HG_W + h * BLOCK: part * HG_W + (h + 1) * BLOCK]
            hs = slice(h * BLOCK, (h + 1) * BLOCK)
            seg = _seg_blocks(e, h)
            _, norm_vjp = jax.vjp(_hgrn_norm, raw_ref[:, hs], sl(3), ng_ref[...])
            draw, dhg, dngh = norm_vjp(dy_ref[:, hs])
            _, mix_vjp = jax.vjp(_hgrn_mix, sl(0), k[:, hs], sl(2), st_ref[0, h], a_ref[0, h].astype(F32), *seg[:3])
            dhq, dkh, dhi, dst, da, *dseg_mix = mix_vjp((draw, dcarry_ref[h]))
            _, scores_vjp = jax.vjp(_hgrn_scores, sl(0), k[:, hs], *seg[3:])
            dhq2, dkh2, *dseg_lvl = scores_vjp(da)
            for part, val in ((0, dhq + dhq2), (2, dhi), (3, dhg)):
                dx_ref[:, part * HG_W + h * BLOCK: part * HG_W + (h + 1) * BLOCK] = val.astype(BF16)
            dk.append(dkh + dkh2)
            dseg.append(jnp.concatenate(dseg_mix + dseg_lvl, axis=0))
            dng = dng + dngh
            dcarry_ref[h] = dst
        dlogf = _split_dot(lvv, jnp.concatenate(dseg, axis=1), "tn")
        dhf, da0, da1 = gates_vjp((dlogf, jnp.concatenate(dk, axis=1)))
        dx_ref[:, HG_W:2 * HG_W] = dhf.astype(BF16)
        dlb_ref[0:1, :] += da0
        dlb_ref[1:2, :] += da1
        dng_ref[...] += dng

        @pl.when(step == nb - 1)
        def _():
            for wait in waits:
                wait()

    const = lambda shape: pl.BlockSpec(shape, lambda s: (0,) * len(shape))
    per_head = pl.BlockSpec((1, HG_HEADS, BLOCK, BLOCK), lambda s: (rev(s), 0, 0, 0))
    return pl.pallas_call(
        body, name="mixers_bwd", grid=(nb,),
        in_specs=[pl.BlockSpec((BLOCK, 4 * HG_W), lambda s: (rev(s), 0)), const((2, HG_W)), const((1, BLOCK)),
                  const(lv.shape), per_head, per_head, pl.BlockSpec((BLOCK, HG_W), lambda s: (rev(s), 0)),
                  pl.BlockSpec((1, 2, ATT_KEYS, 4 * BLOCK), lambda s: (rev(s), 0, 0, 0))]
        + _att_specs(rev)
        + [pl.BlockSpec((BLOCK, HG_W), lambda s: (rev(s), 0)), pl.BlockSpec((BLOCK, ATT_QW), lambda s: (rev(s), 0))]
        + c_in,
        out_specs=[pl.BlockSpec((BLOCK, MIX_W), lambda s: (rev(s), 0)), const((2, HG_W)), const((1, BLOCK)),
                   const((ATT_HEADS, BLOCK))] + c_out,
        out_shape=[jax.ShapeDtypeStruct((p, MIX_W), BF16), jax.ShapeDtypeStruct((2, HG_W), F32),
                   jax.ShapeDtypeStruct((1, BLOCK), F32), jax.ShapeDtypeStruct((ATT_HEADS, BLOCK), F32)] + c_shapes,
        scratch_shapes=[pltpu.VMEM((HG_HEADS, BLOCK, BLOCK), F32), pltpu.VMEM((BLOCK, kvw), F32),
                        pltpu.VMEM((N_META, kvw), F32)] + c_sems,
        compiler_params=_cparams(("arbitrary",)),
    )(proj_hg, lbounds, norm_g, lv, states, scores, raw, probs, proj_att, proj_att, proj_att, cos, sin, cos, sin,
      cos, sin, sinks, dyh, doa, *parts)


_HBM = pl.BlockSpec(memory_space=pltpu.HBM)
_SEM = pl.BlockSpec(memory_space=pltpu.SEMAPHORE)
_ORDERED_BY_DATA = pltpu.CompilerParams(has_side_effects=pltpu.SideEffectType.DATAFLOW_SIDE_EFFECTING)


def _exchange_copies(part_ref, land_ref, send_sems, recv_sems):
    place = _place()
    return [pltpu.make_async_remote_copy(
        src_ref=part_ref.at[_slot(_peer(place, flip), False)], dst_ref=land_ref.at[r], send_sem=send_sems.at[r],
        recv_sem=recv_sems.at[r], device_id=_peer(place, flip), device_id_type=MESH) for r, flip in enumerate(_FLIPS)]


def _exchange_start(parts, name):
    def body(part_ref, land_ref, send_sems, recv_sems, part_thru, land_thru, token):
        for cp in _exchange_copies(part_ref, land_ref, send_sems, recv_sems):
            cp.start()
        token[...] = jnp.zeros_like(token)

    land = (N_PEERS,) + parts.shape[1:]
    return pl.pallas_call(
        body, name=name,
        out_shape=(pltpu.SemaphoreType.DMA((N_PEERS,)), pltpu.SemaphoreType.DMA((N_PEERS,)),
                   pltpu.HBM(parts.shape, parts.dtype), pltpu.HBM(land, parts.dtype), jax.ShapeDtypeStruct((8, BLOCK), F32)),
        in_specs=(_HBM, _HBM), out_specs=(_SEM, _SEM, _HBM, _HBM, pl.BlockSpec(memory_space=pltpu.VMEM)),
        input_output_aliases={0: 2, 1: 3}, compiler_params=_ORDERED_BY_DATA,
    )(pltpu.with_memory_space_constraint(parts, pltpu.HBM),
      pltpu.with_memory_space_constraint(lax.empty(land, parts.dtype), pltpu.HBM))


def _exchange_wait(send_sems, recv_sems, part_thru, land_thru, after, name):
    def body(part_ref, land_ref, send_sems, recv_sems, after_ref, part_out, land_out):
        for cp in _exchange_copies(part_ref, land_ref, send_sems, recv_sems):
            cp.wait_send()
            cp.wait_recv()

    return pl.pallas_call(
        body, name=name,
        out_shape=(pltpu.HBM(part_thru.shape, part_thru.dtype), pltpu.HBM(land_thru.shape, land_thru.dtype)),
        in_specs=(_HBM, _HBM, _SEM, _SEM, pl.BlockSpec(memory_space=pl.ANY)), out_specs=(_HBM, _HBM),
        input_output_aliases={0: 0, 1: 1}, compiler_params=_ORDERED_BY_DATA,
    )(part_thru, land_thru, send_sems, recv_sems, after)


def _embed_bwd(dmix, dgates, w_mix, w_gates, dr1, xhat0, rstd0, g0):
    p = dmix.shape[0]
    tm = _row_tile(p, 640)
    nm = p // tm

    def body(a_ref, g_ref, wa_ref, wg_ref, dr_ref, xh_ref, rs_ref, g0_ref, gx_ref, lead_ref, dg_ref, db_ref,
             buf_ref, sem):
        i = pl.program_id(0)
        first = pltpu.make_async_copy(buf_ref.at[0, pl.ds(BLOCK, tm - BLOCK)], gx_ref.at[pl.ds(0, tm - BLOCK)],
                                      sem.at[0])
        later = lambda t: pltpu.make_async_copy(buf_ref.at[t % 2], gx_ref.at[pl.ds(t * tm - BLOCK, tm)], sem.at[t % 2])

        @pl.when(i == 2)
        def _():
            first.wait()

        @pl.when(i > 2)
        def _():
            later(i - 2).wait()

        dh0 = ALPHA * dr_ref[...] + _dot(a_ref[...], wa_ref[...], "nt") + _dot(g_ref[...], wg_ref[...], "nt")
        row = i * tm + lax.broadcasted_iota(jnp.int32, (tm, 1), 0)
        dx, dg, db = _ln_bwd(jnp.where(row >= PAD, dh0, 0.0), xh_ref[...], rs_ref[...], g0_ref[...])
        buf_ref[i % 2] = dx

        @pl.when(i == 0)
        def _():
            lead_ref[...] = dx[:BLOCK]
            dg_ref[...] = dg
            db_ref[...] = db
            first.start()

        @pl.when(i > 0)
        def _():
            dg_ref[...] += dg
            db_ref[...] += db
            later(i).start()

        @pl.when(i == nm - 1)
        def _():
            for t in (nm - 2, nm - 1):
                if t >= 0:
                    (first if t == 0 else later(t)).wait()

    row = lambda w: pl.BlockSpec((tm, w), lambda i: (i, 0))
    const = lambda a: pl.BlockSpec(a.shape, lambda i: (0, 0))
    vec = pl.BlockSpec((1, D_MODEL), lambda i: (0, 0))
    return pl.pallas_call(
        body, name="embed_bwd", grid=(nm,),
        in_specs=[row(dmix.shape[1]), row(dgates.shape[1]), const(w_mix), const(w_gates), row(D_MODEL), row(D_MODEL),
                  row(1), vec],
        out_specs=[pl.BlockSpec(memory_space=pl.ANY), pl.BlockSpec((BLOCK, D_MODEL), lambda i: (0, 0)), vec, vec],
        out_shape=[jax.ShapeDtypeStruct((p - BLOCK, D_MODEL), F32), jax.ShapeDtypeStruct((BLOCK, D_MODEL), F32),
                   jax.ShapeDtypeStruct((1, D_MODEL), F32), jax.ShapeDtypeStruct((1, D_MODEL), F32)],
        scratch_shapes=[pltpu.VMEM((2, tm, D_MODEL), F32), pltpu.SemaphoreType.DMA((2,))],
        compiler_params=_cparams(("arbitrary",)),
    )(dmix, dgates, w_mix, w_gates, dr1, xhat0, rstd0, g0)


_LATE = ("w_branch_hg", "w_branch_attn", "w_out", "w_ffn_in", "w_ffn_out")
_COLUMN_SHARDED = ("meta_tokens", "w_in", "w_branch_hg", "w_branch_attn", "w_ffn_in")
_SWAPPED = ("w_ffn_in",)


def _whole(name, gathered):
    _, r, c = gathered.shape
    if name in _COLUMN_SHARDED:
        return jnp.transpose(gathered, (1, 0, 2)).reshape(r, N_DEV * c)
    return gathered.reshape(N_DEV * r, c)


def _slots(name, whole):
    r, c = whole.shape
    if name in _COLUMN_SHARDED:
        return jnp.transpose(whole.reshape(r, N_DEV, c // N_DEV), (1, 0, 2))
    return whole.reshape(N_DEV, r // N_DEV, c)


def _device_step(x, target, meta_shard, ln_emb_g, ln_emb_b, w_in_shard, lbounds, norm_g, sinks, late_shards,
                 ln1_g, ln1_b, ln2_g, ln2_b):
    p = x.shape[0] + BLOCK
    lv = _level_stack()
    cos, sin = _rope_tables(p)
    swapped = [n in _SWAPPED for n in _LATE]

    h0, h0b, xhat0, rstd0, _, g_win = _embed_ln(x, meta_shard, w_in_shard, ln_emb_g, ln_emb_b)
    w_in = _whole("w_in", g_win)
    proj_hg, proj_att, gates = _in_proj(h0b, w_in)
    yh, oa, states, scores, raw, probs, *gathered = _mixers_fwd(
        proj_hg, proj_att, lbounds, norm_g, lv, cos, sin, sinks, late_shards, swapped)
    w_bh, w_ba, w_out, w_fi, w_fo = [_whole(n, g) for n, g in zip(_LATE, gathered)]
    mixin, h1, h1b, xhat1, rstd1 = _mix_out_ln1(yh, oa, gates, h0, w_bh, w_ba, w_out, ln1_g, ln1_b)
    au, sw = _ffn_in_swiglu(h1b, w_fi)
    dr2, dr2b, loss_part, dg2, db2 = _ffn_out_loss(sw, w_fo, h1, ln2_g, ln2_b, target)

    mtn = functools.partial(_tiled_matmul_tn, tm=_row_tile(p, 1664), out_dtype=BF16)
    whole = functools.partial(_tiled_matmul_tn, tm=p, out_dtype=BF16)
    d_wfo = mtn(sw, dr2b, tk=FF_T, tn=D_MODEL, name="grad_w_ffn_out")
    dau, dh1_ffn = _ffn_bwd(dr2b, w_fo, au, w_fi)
    d_wfi = whole(h1b, dau, tk=D_MODEL // 2, tn=D_MODEL // 2, name="grad_w_ffn_in")
    dr1, dy_hg, dy_att, dgates, dyh, doa, dg1, db1 = _ln1_mix_bwd(
        dr2, dh1_ffn, xhat1, rstd1, ln1_g, yh, oa, gates, w_bh, w_ba, w_out)
    d_wout = mtn(mixin, dr1, tk=D_MODEL, tn=D_MODEL, name="grad_w_out")
    d_wbh = mtn(yh, dy_hg, tk=HG_W, tn=D_MODEL, name="grad_w_branch_hg")
    d_wba = mtn(oa, dy_att, tk=ATT_QW, tn=D_MODEL, name="grad_w_branch_attn")
    late_parts = [_slots(n, g) for n, g in zip(_LATE, (d_wbh, d_wba, d_wout, d_wfi, d_wfo))]
    dmix, d_lb, d_ng, d_sink, *late_recv = _mixers_bwd(
        proj_hg, proj_att, lbounds, norm_g, lv, states, scores, raw, probs, cos, sin, sinks, dyh, doa, late_parts,
        swapped)
    d_win = jnp.concatenate([whole(h0b, dmix, tk=D_MODEL, tn=D_MODEL // 4, name="grad_w_in_mixers"),
                             mtn(h0b, dgates, tk=D_MODEL, tn=D_MODEL, name="grad_w_in_gates")], axis=1)
    *win_flight, token = _exchange_start(_slots("w_in", d_win), "w_in_grads_start")
    grad_x, dlead, dg0, db0 = _embed_bwd(dmix, dgates, w_in[:, :MIX_W], w_in[:, MIX_W:], dr1, xhat0, rstd0,
                                         ln_emb_g + token[0:1, 0:1])

    small = dict(ln_emb_g=dg0, ln_emb_b=db0, hg_lower_bounds=d_lb, hg_norm_g=d_ng, attn_sinks=d_sink[:, 0],
                 ln1_g=dg1, ln1_b=db1, ln2_g=dg2, ln2_b=db2)
    big = dict(zip(_LATE, zip(late_parts, late_recv)))
    return loss_part, grad_x, small, dlead[PAD:BLOCK], big, win_flight


def _all_gather(arrs, dtypes, name):
    n = len(arrs)

    def body(*refs):
        ins, outs, stages = refs[:n], refs[n:2 * n], refs[2 * n:3 * n]
        send_sems, recv_sems, local_sems = refs[3 * n:]
        x, y, c = _place()
        sibling = (x, y, 1 - c)
        chips = [(1 - x, y), (x, 1 - y), (1 - x, 1 - y)]
        slot = lambda px, py, pc: 4 * px + 2 * py + pc

        def copy(w, k, block, to, from_stage=False):
            return pltpu.make_async_remote_copy(
                src_ref=stages[w] if from_stage else outs[w].at[slot(*block)], dst_ref=outs[w].at[slot(*block)],
                send_sem=send_sems.at[w, k], recv_sem=recv_sems.at[w, k], device_id=to, device_id_type=MESH)

        mine, first, passed = [], [], []
        for w in range(n):
            stages[w][...] = ins[w][...].astype(dtypes[w])
            mine.append(pltpu.make_async_copy(stages[w], outs[w].at[slot(x, y, c)], local_sems.at[w]))
            mine[-1].start()
        for w in range(n):
            first.append(copy(w, 0, (x, y, c), sibling, from_stage=True))
            first += [copy(w, 1 + j, (x, y, c), (*chip, c), from_stage=True) for j, chip in enumerate(chips)]
        for cp in first:
            cp.start()
        for j, chip in enumerate(chips):
            for w in range(n):
                copy(w, 1 + j, (*chip, c), (x, y, c)).wait_recv()
                passed.append(copy(w, 4 + j, (*chip, c), sibling))
                passed[-1].start()
        for w in range(n):
            copy(w, 0, sibling, (x, y, c)).wait_recv()
            for j, chip in enumerate(chips):
                copy(w, 4 + j, (*chip, 1 - c), (x, y, c)).wait_recv()
        for cp in first + passed:
            cp.wait_send()
        for cp in mine:
            cp.wait()

    return pl.pallas_call(
        body, name=name,
        in_specs=[pl.BlockSpec(memory_space=pltpu.VMEM)] * n,
        out_specs=[pl.BlockSpec(memory_space=pl.ANY)] * n,
        out_shape=[jax.ShapeDtypeStruct((N_DEV,) + a.shape, dt) for a, dt in zip(arrs, dtypes)],
        scratch_shapes=[pltpu.VMEM(a.shape, dt) for a, dt in zip(arrs, dtypes)]
        + [pltpu.SemaphoreType.DMA((n, 7)), pltpu.SemaphoreType.DMA((n, 7)), pltpu.SemaphoreType.DMA((n,))],
        compiler_params=pltpu.CompilerParams(vmem_limit_bytes=VMEM_LIMIT_BYTES),
    )(*arrs)


def _cast_shards(arrs):
    def body(*refs):
        for src, dst in zip(refs[:len(arrs)], refs[len(arrs):]):
            dst[...] = src[...].astype(BF16)

    return pl.pallas_call(body, name="cast_shards", out_shape=[jax.ShapeDtypeStruct(a.shape, BF16) for a in arrs],
                          compiler_params=pltpu.CompilerParams(vmem_limit_bytes=VMEM_LIMIT_BYTES))(*arrs)


def _shard_rows(rows):
    return rows if rows <= 512 else 256


def _adamw_math(w, g, m, v):
    m = ADAM_B1 * m + (1.0 - ADAM_B1) * g
    v = ADAM_B2 * v + (1.0 - ADAM_B2) * (g * g)
    m_hat = m / (1.0 - ADAM_B1 ** ADAM_STEP)
    v_hat = v / (1.0 - ADAM_B2 ** ADAM_STEP)
    delta = -ADAM_LR * (m_hat / (jnp.sqrt(v_hat) + ADAM_EPS) + ADAM_WD * w)
    return delta, m, v


def _reduce_adamw(parts, recv, own_slot, w, m, v, name):
    r, cdim = w.shape
    tr = _shard_rows(r)

    def body(idx_ref, p_ref, r_ref, w_ref, m_ref, v_ref, g_out, d_out, m_out, v_out):
        g = p_ref[0].astype(F32)
        for j in range(N_PEERS):
            g = g + r_ref[j].astype(F32)
        d, mn, vn = _adamw_math(w_ref[...], g, m_ref[...], v_ref[...])
        g_out[...] = g
        d_out[...] = d
        m_out[...] = mn
        v_out[...] = vn

    flat = pl.BlockSpec((tr, cdim), lambda i, idx_ref: (i, 0))
    return pl.pallas_call(
        body, name=name,
        grid_spec=pltpu.PrefetchScalarGridSpec(
            num_scalar_prefetch=1, grid=(r // tr,),
            in_specs=[pl.BlockSpec((1, tr, cdim), lambda i, idx_ref: (idx_ref[0], i, 0)),
                      pl.BlockSpec((N_PEERS, tr, cdim), lambda i, idx_ref: (0, i, 0)), flat, flat, flat],
            out_specs=[flat] * 4),
        out_shape=[jax.ShapeDtypeStruct((r, cdim), F32)] * 4,
        compiler_params=_cparams(("arbitrary",)),
    )(own_slot, parts, recv, w, m, v)


def _adamw_plain(w, g, m, v, name):
    def body(w_ref, g_ref, m_ref, v_ref, d_out, m_out, v_out):
        d_out[...], m_out[...], v_out[...] = _adamw_math(w_ref[...], g_ref[...], m_ref[...], v_ref[...])

    return pl.pallas_call(body, name=name, out_shape=[jax.ShapeDtypeStruct(w.shape, F32)] * 3)(w, g, m, v)


_SMALL_LAYOUT = (("ln_emb_g", 8), ("ln_emb_b", 8), ("hg_lower_bounds", 8), ("hg_norm_g", 1), ("attn_sinks", 1),
                 ("ln1_g", 8), ("ln1_b", 8), ("ln2_g", 8), ("ln2_b", 8))
_META_ROW = sum(r for _, r in _SMALL_LAYOUT)
_META_ROWS = N_META * D_MODEL // BLOCK
_LOSS_ROW = _META_ROW + _META_ROWS
SMALL_ROWS = 192


def _pack_small(vals, meta=None, loss_row=None):
    rows = []
    for name, nrows in _SMALL_LAYOUT:
        flat = vals[name].reshape(-1).astype(F32)
        flat = jnp.pad(flat, (0, nrows * BLOCK - flat.shape[0]))
        rows.append(flat.reshape(nrows, BLOCK))
    rows.append(jnp.zeros((_META_ROWS, BLOCK), F32) if meta is None else meta.reshape(_META_ROWS, BLOCK))
    rows.append(jnp.zeros((1, BLOCK), F32) if loss_row is None else loss_row)
    packed = jnp.concatenate(rows, axis=0)
    return jnp.pad(packed, ((0, SMALL_ROWS - packed.shape[0]), (0, 0)))


def _unpack_small(packed, shapes):
    out, row = {}, 0
    for name, nrows in _SMALL_LAYOUT:
        size = math.prod(shapes[name])
        out[name] = packed[row:row + nrows].reshape(-1)[:size].reshape(shapes[name])
        row += nrows
    return out


def _small_reduce_adamw(gathered, w, m, v):
    def body(g_ref, w_ref, m_ref, v_ref, g_out, d_out, m_out, v_out, loss_out):
        g = g_ref[0]
        for s in range(1, N_DEV):
            g = g + g_ref[s]
        d, mn, vn = _adamw_math(w_ref[...], g, m_ref[...], v_ref[...])
        g_out[...] = g
        d_out[...] = d
        m_out[...] = mn
        v_out[...] = vn
        loss_out[...] = jnp.broadcast_to(jnp.sum(g_ref[:, _LOSS_ROW, :]), (1, BLOCK))

    shp = jax.ShapeDtypeStruct((SMALL_ROWS, BLOCK), F32)
    return pl.pallas_call(body, name="small_reduce_adamw",
                          out_shape=[shp] * 4 + [jax.ShapeDtypeStruct((1, BLOCK), F32)])(gathered, w, m, v)


_WEIGHTS = ("meta_tokens", "ln_emb_g", "ln_emb_b", "w_in", "hg_lower_bounds", "hg_norm_g", "attn_sinks",
            "w_branch_hg", "w_branch_attn", "w_out", "ln1_g", "ln1_b", "w_ffn_in", "w_ffn_out", "ln2_g", "ln2_b")


def kernel(x, meta_tokens, ln_emb_g, ln_emb_b, w_in, hg_lower_bounds, hg_norm_g, attn_sinks, w_branch_hg, w_branch_attn, w_out, ln1_g, ln1_b, w_ffn_in, w_ffn_out, ln2_g, ln2_b, loss_target, m_meta_tokens, m_ln_emb_g, m_ln_emb_b, m_w_in, m_hg_lower_bounds, m_hg_norm_g, m_attn_sinks, m_w_branch_hg, m_w_branch_attn, m_w_out, m_ln1_g, m_ln1_b, m_w_ffn_in, m_w_ffn_out, m_ln2_g, m_ln2_b, v_meta_tokens, v_ln_emb_g, v_ln_emb_b, v_w_in, v_hg_lower_bounds, v_hg_norm_g, v_attn_sinks, v_w_branch_hg, v_w_branch_attn, v_w_out, v_ln1_g, v_ln1_b, v_w_ffn_in, v_w_ffn_out, v_ln2_g, v_ln2_b):
    given = dict(locals())
    weights = {n: given[n] for n in _WEIGHTS}
    mom1 = {n: given["m_" + n] for n in _WEIGHTS}
    mom2 = {n: given["v_" + n] for n in _WEIGHTS}
    shard2d = lambda a: a.reshape(a.shape[-2:])

    w_in_shard, *late_shards = _cast_shards([shard2d(weights[n]) for n in ("w_in",) + _LATE])
    loss_part, grad_x, small_grads, meta_grad, big, win_flight = _device_step(
        x[0], loss_target[0], meta_tokens, ln_emb_g.reshape(1, -1), ln_emb_b.reshape(1, -1), w_in_shard,
        hg_lower_bounds, hg_norm_g, attn_sinks, late_shards, ln1_g, ln1_b, ln2_g, ln2_b)

    place = _place()
    out = {}

    def reduce_adamw(n, parts, recv):
        own = _slot(place, n in _SWAPPED).astype(jnp.int32).reshape(1)
        res = _reduce_adamw(parts, recv, own, shard2d(weights[n]), shard2d(mom1[n]), shard2d(mom2[n]), "adamw_" + n)
        out[n] = [r.reshape(weights[n].shape) for r in res]

    for n, (parts, recv) in big.items():
        reduce_adamw(n, parts, recv)

    small_names = [n for n, _ in _SMALL_LAYOUT]
    packed = _pack_small(small_grads, meta_grad, loss_part)
    all_small, = _all_gather([packed], [F32], "gather_small")
    res = _small_reduce_adamw(all_small, _pack_small(weights), _pack_small(mom1), _pack_small(mom2))
    shapes = {n: weights[n].shape for n in small_names}
    unpacked = [_unpack_small(r, shapes) for r in res[:4]]
    for n in small_names:
        out[n] = [u[n] for u in unpacked]
    loss = res[4][0, 0]
    meta_whole = res[0][_META_ROW:_META_ROW + _META_ROWS].reshape(N_META, N_DEV, D_MODEL // N_DEV)
    g_meta_mine = lax.dynamic_index_in_dim(meta_whole, _slot(place, False), axis=1, keepdims=False)
    out["meta_tokens"] = [g_meta_mine, *_adamw_plain(meta_tokens, g_meta_mine, m_meta_tokens, v_meta_tokens,
                                                     "adamw_meta")]

    reduce_adamw("w_in", *_exchange_wait(*win_flight, after=all_small, name="w_in_grads_wait"))

    return (loss, grad_x[None], *[out[n][0] for n in _WEIGHTS], *[out[n][1] for n in _WEIGHTS],
            *[out[n][2] for n in _WEIGHTS], *[out[n][3] for n in _WEIGHTS])
```

```python
import functools

import numpy as np
import jax
import jax.numpy as jnp
from jax import lax
from jax.experimental import pallas as pl
from jax.experimental.pallas import tpu as pltpu

F32 = jnp.float32
BF16 = jnp.bfloat16

D_MODEL = 1024
N_META = 16
BLOCK = 128
PAD = BLOCK - N_META
HG_HEADS = 4
HG_W = 512
ATT_HEADS = 8
HEAD_DIM = 64
ATT_QW = 512
ATT_KVW = 128
D_FF = 2816
EPS = 1e-5
ALPHA = 2.0 ** 0.25
ROPE_THETA = 10000.0
N_DEV = 8

ADAM_LR = 0.001
ADAM_B1 = 0.9
ADAM_B2 = 0.999
ADAM_EPS = 1e-08
ADAM_WD = 0.01
ADAM_STEP = 10

VMEM_LIMIT_BYTES = 56 * 1024 * 1024
MESH = pl.DeviceIdType.MESH

_LEVELS = (64, 32, 16, 8, 4, 2, 1)


def _cparams(sem):
    return pltpu.CompilerParams(dimension_semantics=sem, vmem_limit_bytes=VMEM_LIMIT_BYTES)


def _row_tile(rows, target):
    nb = rows // BLOCK
    best = 1
    for d in range(1, nb + 1):
        if nb % d == 0 and d * BLOCK <= target:
            best = d
    return best * BLOCK


_DN = {"nn": (((1,), (0,)), ((), ())), "nt": (((1,), (1,)), ((), ())), "tn": (((0,), (0,)), ((), ()))}


def _dot(a, b, form):
    return lax.dot_general(a.astype(BF16), b.astype(BF16), _DN[form], preferred_element_type=F32)


@functools.partial(jax.custom_vjp, nondiff_argnums=(2,))
def _mm(a, b, form):
    return _dot(a, b, form)


def _mm_fwd(a, b, form):
    a, b = a.astype(BF16), b.astype(BF16)
    return _dot(a, b, form), (a, b)


def _mm_bwd(form, res, g):
    a, b = res
    if form == "nn":
        return _dot(g, b, "nt"), _dot(a, g, "tn")
    if form == "nt":
        return _dot(g, b, "nn"), _dot(g, a, "tn")
    return _dot(b, g, "nt"), _dot(a, g, "nn")


_mm.defvjp(_mm_fwd, _mm_bwd)


def _split_dot(lv, x, form):
    return lax.dot_general(lv, x.astype(BF16), _DN[form], preferred_element_type=F32)


@jax.custom_vjp
def _swap_halves(x):
    return pltpu.roll(x, 64, 1)


_swap_halves.defvjp(lambda x: (pltpu.roll(x, 64, 1), None), lambda _, g: (pltpu.roll(g, 64, 1),))


def _tiled_matmul_tn(a, b, *, tm, tk, tn, out_dtype, name):
    m, k = a.shape
    n = b.shape[1]
    assert m % tm == 0 and k % tk == 0 and n % tn == 0, (name, a.shape, b.shape, tm, tk, tn)
    nm = m // tm

    def body(a_ref, b_ref, o_ref, acc_ref):
        mi = pl.program_id(2)

        @pl.when(mi == 0)
        def _():
            acc_ref[...] = jnp.zeros_like(acc_ref)

        acc_ref[...] += _dot(a_ref[...], b_ref[...], "tn")

        @pl.when(mi == nm - 1)
        def _():
            o_ref[...] = acc_ref[...].astype(out_dtype)

    return pl.pallas_call(
        body, name=name, grid=(k // tk, n // tn, nm),
        in_specs=[pl.BlockSpec((tm, tk), lambda kk, j, i: (i, kk)), pl.BlockSpec((tm, tn), lambda kk, j, i: (i, j))],
        out_specs=pl.BlockSpec((tk, tn), lambda kk, j, i: (kk, j)),
        out_shape=jax.ShapeDtypeStruct((k, n), out_dtype),
        scratch_shapes=[pltpu.VMEM((tk, tn), F32)],
        compiler_params=_cparams(("arbitrary", "arbitrary", "arbitrary")),
    )(a, b)


def _ln_stats(r):
    mu = jnp.mean(r, axis=-1, keepdims=True)
    xc = r - mu
    var = jnp.mean(xc * xc, axis=-1, keepdims=True)
    rstd = lax.rsqrt(var + EPS)
    return xc * rstd, rstd


def _ln_bwd(dy, xhat, rstd, g):
    dxhat = dy * g
    m1 = jnp.mean(dxhat, axis=-1, keepdims=True)
    m2 = jnp.mean(dxhat * xhat, axis=-1, keepdims=True)
    dr = rstd * (dxhat - m1 - xhat * m2)
    return dr, jnp.sum(dy * xhat, axis=0, keepdims=True), jnp.sum(dy, axis=0, keepdims=True)


N_SEG = 3 + len(_LEVELS)


def _level_stack():
    t = np.arange(BLOCK)[:, None]
    r = np.arange(BLOCK)[None, :]
    mats = [r <= t, r > t, np.ones((BLOCK, BLOCK), bool)]
    for h in _LEVELS:
        same = (t // (2 * h)) == (r // (2 * h))
        up_t, up_r = (t % (2 * h)) >= h, (r % (2 * h)) >= h
        mats.append(same & ((up_t & up_r & (r <= t)) | (~up_t & ~up_r & (r > t))))
    return jnp.asarray(np.concatenate(mats, axis=0).astype(np.float32), dtype=BF16)


def _hgrn_gates(hf, a0, a1, valid):
    lb = jax.nn.sigmoid(a0 - a1)
    fg = lb + (1.0 - lb) * jax.nn.sigmoid(hf)
    return jnp.where(valid, jnp.log(fg), 0.0), jnp.where(valid, 1.0 - fg, 0.0)


def _hgrn_scores(hq, k, *levels):
    q = jax.nn.silu(hq)
    rows = lax.broadcasted_iota(jnp.int32, (BLOCK, BLOCK), 0)
    cols = lax.broadcasted_iota(jnp.int32, (BLOCK, BLOCK), 1)
    a = jnp.where(rows == cols, jnp.sum(q * k, axis=-1, keepdims=True), 0.0)
    differ = jnp.bitwise_xor(rows, cols)
    for h, lvl in zip(_LEVELS, levels):
        decay = jnp.exp(lvl)
        pair = (cols < rows) & (differ >= h) & (differ < 2 * h)
        a = a + jnp.where(pair, _mm(q * decay, k * decay, "nt"), 0.0)
    return a


def _hgrn_mix(hq, k, v, st_in, a, seg_incl, seg_after, seg_total):
    o = _mm(jax.nn.silu(hq) * jnp.exp(seg_incl), st_in, "nt") + _mm(a, v, "nn")
    return o, st_in * jnp.exp(seg_total) + _mm(v, k * jnp.exp(seg_after), "tn")


def _hgrn_norm(o, hg, ng):
    return o * lax.rsqrt(jnp.mean(o * o, axis=-1, keepdims=True) + EPS) * ng * jax.nn.silu(hg)


def _seg_blocks(e, h):
    return [e[i * BLOCK:(i + 1) * BLOCK, h * BLOCK:(h + 1) * BLOCK] for i in range(N_SEG)]


def _rope(x, cos, sin, first_half):
    partner = jnp.where(first_half, -pltpu.roll(x, 96, 1), pltpu.roll(x, 32, 1))
    return x * cos + partner * sin


def _rope_t(g, cos, sin, first_half):
    u = g * sin
    partner = jnp.where(first_half, pltpu.roll(u, 96, 1), -pltpu.roll(u, 32, 1))
    return g * cos + partner


def _low_half(x):
    return lax.broadcasted_iota(jnp.int32, x.shape, 1) < HEAD_DIM


def _both_halves(x, g):
    sw = _swap_halves(x)
    return jnp.where(_low_half(x), x, sw) if g == 0 else jnp.where(_low_half(x), sw, x)


def _att_scores(qa, qb, kc, kp, km, g, own4, band4, meta4):
    low = _low_half(qa)
    q4 = jnp.concatenate([jnp.where(low, qa, 0.0), jnp.where(low, 0.0, qa),
                          jnp.where(low, qb, 0.0), jnp.where(low, 0.0, qb)], axis=0)
    scale = HEAD_DIM ** -0.5
    neg = jnp.finfo(F32).min
    s = jnp.where(own4, _mm(_both_halves(kc, g), q4, "nt"), _mm(_both_halves(kp, g), q4, "nt"))
    return (jnp.where(band4, s * scale, neg), jnp.where(meta4, _mm(_both_halves(km, g), q4, "nt") * scale, neg))


def _att_probs(s, sm, sinkrow):
    mx = jnp.maximum(jnp.maximum(jnp.max(s, axis=0, keepdims=True), jnp.max(sm, axis=0, keepdims=True)), sinkrow)
    p, pm, ps = jnp.exp(s - mx), jnp.exp(sm - mx), jnp.exp(sinkrow - mx)
    inv = 1.0 / (jnp.sum(p, axis=0, keepdims=True) + jnp.sum(pm, axis=0, keepdims=True) + ps)
    return p * inv, pm * inv, ps * inv


def _att_probs_bwd(p, pm, ps, dp, dpm):
    r = jnp.sum(p * dp, axis=0, keepdims=True) + jnp.sum(pm * dpm, axis=0, keepdims=True)
    return p * (dp - r), pm * (dpm - r), -ps * r


def _att_values(p, pm, vc, vp, vm, g, own4):
    o4 = (_mm(jnp.where(own4, p, 0.0), _both_halves(vc, g), "tn") + _mm(jnp.where(own4, 0.0, p), _both_halves(vp, g), "tn")
          + _mm(pm, _both_halves(vm, g), "tn"))
    tiles = []
    for j in range(2):
        upper = o4[(2 * j) * BLOCK:(2 * j + 1) * BLOCK]
        tiles.append(jnp.where(_low_half(upper), upper, o4[(2 * j + 1) * BLOCK:(2 * j + 2) * BLOCK]))
    return tiles


def _att_masks(blk_idx):
    kidx = lax.broadcasted_iota(jnp.int32, (BLOCK, BLOCK), 0)
    qrow = lax.broadcasted_iota(jnp.int32, (BLOCK, BLOCK), 1)
    own_side = kidx <= qrow
    pos_own = blk_idx * BLOCK + kidx - PAD
    ok_band = (own_side & (pos_own >= N_META)) | (~own_side & (pos_own - BLOCK >= N_META) & (blk_idx >= 1))
    qpos = blk_idx * BLOCK + lax.broadcasted_iota(jnp.int32, (N_META, BLOCK), 1) - PAD
    ok_meta = lax.broadcasted_iota(jnp.int32, (N_META, BLOCK), 0) <= qpos
    return [jnp.concatenate([m] * 4, axis=1) for m in (own_side, ok_band, ok_meta)]


def _token_streams(tr, tile_of=lambda i: i):
    k = tr // BLOCK
    return [pl.BlockSpec((BLOCK, D_MODEL), lambda i, j=j: (jnp.maximum(k * tile_of(i) - 1 + j, 0), 0))
            for j in range(k)]


def _embed_ln(x, meta_shard, w_in_shard, g0, b0):
    p = x.shape[0] + BLOCK
    tr = _row_tile(p, 640)
    k = tr // BLOCK
    nt = p // tr
    tile_of = lambda s: (s + 1) % nt
    shards = [meta_shard, w_in_shard]
    c_in, c_out, c_shapes, c_sems = _comm_specs(shards, N_DEV)

    def body(*refs):
        g_ref, b_ref = refs[k:k + 2]
        h_ref, hb_ref, xh_ref, rs_ref = refs[k + 4:k + 8]
        out_refs = refs[k + 8:k + 10]
        lead_ref, meta_ref = refs[k + 10:k + 12]
        starts, passes, waits = _gather_behind(refs[k + 2:k + 4], out_refs, refs[k + 12:], [False, False])
        s = pl.program_id(0)
        t = tile_of(s)

        @pl.when(s == 0)
        def _():
            lead_ref[...] = jnp.zeros_like(lead_ref)
            for start in starts:
                start()

        @pl.when(s == nt - 1)
        def _():
            for step in passes + waits:
                step()
            pltpu.sync_copy(out_refs[0], meta_ref)
            for d in range(N_DEV):
                lead_ref[PAD:BLOCK, d * BLOCK:(d + 1) * BLOCK] = meta_ref[d]

        first = jnp.where(t == 0, lead_ref[...], refs[0][...])
        xhat, rstd = _ln_stats(jnp.concatenate([first] + [r[...] for r in refs[1:k]], axis=0))
        row = t * tr + lax.broadcasted_iota(jnp.int32, (tr, 1), 0)
        h = jnp.where(row >= PAD, xhat * g_ref[...] + b_ref[...], 0.0)
        h_ref[...] = h
        hb_ref[...] = h.astype(BF16)
        xh_ref[...] = xhat
        rs_ref[...] = rstd

    vec = pl.BlockSpec((1, D_MODEL), lambda s: (0, 0))
    rowsp = pl.BlockSpec((tr, D_MODEL), lambda s: (tile_of(s), 0))
    return pl.pallas_call(
        body, name="embed_ln", grid=(nt,),
        in_specs=_token_streams(tr, tile_of) + [vec, vec] + c_in,
        out_specs=[rowsp, rowsp, rowsp, pl.BlockSpec((tr, 1), lambda s: (tile_of(s), 0))] + c_out,
        out_shape=[jax.ShapeDtypeStruct((p, D_MODEL), F32), jax.ShapeDtypeStruct((p, D_MODEL), BF16),
                   jax.ShapeDtypeStruct((p, D_MODEL), F32), jax.ShapeDtypeStruct((p, 1), F32)] + c_shapes,
        scratch_shapes=[pltpu.VMEM((BLOCK, D_MODEL), F32), pltpu.VMEM((N_DEV, N_META, BLOCK), F32)] + c_sems,
        compiler_params=_cparams(("arbitrary",)),
    )(*([x] * k), g0, b0, *shards)


def _rope_tables(p):
    pos = (np.arange(p, dtype=np.int32) - PAD).astype(np.float32)
    half = HEAD_DIM // 2
    inv = np.float32(ROPE_THETA) ** (-np.arange(half, dtype=np.float32) / np.float32(half))
    ang = pos[:, None] * np.tile(inv.astype(np.float32), BLOCK // half)[None, :]
    return jnp.asarray(np.cos(ang), F32), jnp.asarray(np.sin(ang), F32)


def _att_sinkrows(sink_ref):
    lanehead = lax.broadcasted_iota(jnp.int32, (1, 4 * BLOCK), 1) // BLOCK
    rows = []
    for g in range(2):
        row = jnp.zeros((1, 4 * BLOCK), F32)
        for j in range(4):
            row = jnp.where(lanehead == j, sink_ref[0, 4 * g + j], row)
        rows.append(row)
    return rows


def _first_half(rows):
    return (lax.broadcasted_iota(jnp.int32, (rows, BLOCK), 1) % HEAD_DIM) < (HEAD_DIM // 2)


def _att_load(qkv_ref, cos_ref, sin_ref, with_q):
    cos, sin, fh = cos_ref[...], sin_ref[...], _first_half(BLOCK)
    qs = [_rope(qkv_ref[:, j * BLOCK:(j + 1) * BLOCK], cos, sin, fh) for j in range(4)] if with_q else None
    k = _rope(qkv_ref[:, ATT_QW:ATT_QW + ATT_KVW], cos, sin, fh)
    v = qkv_ref[:, ATT_QW + ATT_KVW:ATT_QW + 2 * ATT_KVW]
    return qs, k, v


def _att_load_meta(qkv_ref, cos_ref, sin_ref):
    k = _rope(qkv_ref[PAD:BLOCK, ATT_QW:ATT_QW + ATT_KVW], cos_ref[PAD:BLOCK, :], sin_ref[PAD:BLOCK, :],
              _first_half(N_META))
    return k, qkv_ref[PAD:BLOCK, ATT_QW + ATT_KVW:ATT_QW + 2 * ATT_KVW]


def _att_specs(blk):
    w = ATT_QW + 2 * ATT_KVW
    cur = lambda width: pl.BlockSpec((BLOCK, width), lambda i: (blk(i), 0))
    prev = lambda width: pl.BlockSpec((BLOCK, width), lambda i: (jnp.maximum(blk(i) - 1, 0), 0))
    meta = lambda width: pl.BlockSpec((BLOCK, width), lambda i: (0, 0))
    return [cur(w), prev(w), meta(w), cur(BLOCK), cur(BLOCK), prev(BLOCK), prev(BLOCK), meta(BLOCK), meta(BLOCK),
            pl.BlockSpec(memory_space=pltpu.SMEM)]


_FLIPS = [(dx, dy, dc) for dx in (0, 1) for dy in (0, 1) for dc in (0, 1)][1:]
N_PEERS = len(_FLIPS)


def _place():
    return lax.axis_index("x"), lax.axis_index("y"), lax.axis_index("c")


def _peer(place, flip):
    return tuple(1 - p if f else p for p, f in zip(place, flip))


def _slot(place, swapped):
    x, y, c = place
    return 4 * y + 2 * x + c if swapped else 4 * x + 2 * y + c


def _comm_specs(arrs, out_lead):
    n = len(arrs)
    outs = [jax.ShapeDtypeStruct((out_lead,) + a.shape[-2:], a.dtype) for a in arrs]
    sems = [pltpu.SemaphoreType.DMA((n, N_PEERS)), pltpu.SemaphoreType.DMA((n, N_PEERS)), pltpu.SemaphoreType.DMA((n,))]
    return [pl.BlockSpec(memory_space=pl.ANY)] * n, [pl.BlockSpec(memory_space=pl.ANY)] * n, outs, sems


def _gather_behind(shard_refs, out_refs, sems, swapped):
    send_sems, recv_sems, local_sems = sems
    x, y, c = _place()
    me, sibling = (x, y, c), (x, y, 1 - c)
    chips = [(1 - x, y), (x, 1 - y), (1 - x, 1 - y)]
    starts, passes, waits = [], [], []
    for w, (s, o) in enumerate(zip(shard_refs, out_refs)):
        def copy(k, block, to, from_shard=False, w=w, s=s, o=o):
            rows = o.at[_slot(block, swapped[w])]
            return pltpu.make_async_remote_copy(
                src_ref=s if from_shard else rows, dst_ref=rows, send_sem=send_sems.at[w, k],
                recv_sem=recv_sems.at[w, k], device_id=to, device_id_type=MESH)

        own = pltpu.make_async_copy(s, o.at[_slot(me, swapped[w])], local_sems.at[w])
        first = [copy(0, me, sibling, True)] + [copy(1 + j, me, (*chip, c), True) for j, chip in enumerate(chips)]
        handed = [copy(4 + j, (*chip, c), sibling) for j, chip in enumerate(chips)]
        starts += [own.start] + [cp.start for cp in first]
        for j, chip in enumerate(chips):
            passes += [copy(1 + j, (*chip, c), me).wait_recv, handed[j].start]
        waits.append(copy(0, sibling, me).wait_recv)
        waits += [copy(4 + j, (*chip, 1 - c), me).wait_recv for j, chip in enumerate(chips)]
        waits += [cp.wait_send for cp in first + handed] + [own.wait]
    return starts, passes, waits


def _scatter_behind(part_refs, recv_refs, sems, swapped):
    send_sems, recv_sems, _ = sems
    place = _place()
    starts, waits = [], []
    for w, (p, o) in enumerate(zip(part_refs, recv_refs)):
        for r, flip in enumerate(_FLIPS):
            peer = _peer(place, flip)
            cp = pltpu.make_async_remote_copy(
                src_ref=p.at[_slot(peer, swapped[w])], dst_ref=o.at[r], send_sem=send_sems.at[w, r],
                recv_sem=recv_sems.at[w, r], device_id=peer, device_id_type=MESH)
            starts.append(cp.start)
            waits += [cp.wait_recv, cp.wait_send]
    return starts, waits


def _mixers_fwd(proj_hg, proj_att, lbounds, norm_g, lv, cos, sin, sinks, shards, swapped):
    p = proj_hg.shape[0]
    nb = p // BLOCK
    n = len(shards)
    c_in, c_out, c_shapes, c_sems = _comm_specs(shards, N_DEV)
    pass_step = min(nb - 1, max(1, (5 * nb) // 8))

    def body(*refs):
        x_ref, lb_ref, ng_ref, lv_ref, cur_ref, prev_ref, meta_ref, cc, sc, cp, sp, cm, sm, sink_ref = refs[:14]
        shard_refs = refs[14:14 + n]
        y_ref, o_ref, st_ref, a_ref, raw_ref, pr_ref = refs[14 + n:20 + n]
        out_refs = refs[20 + n:20 + 2 * n]
        carry_ref = refs[20 + 2 * n]
        starts, passes, waits = _gather_behind(shard_refs, out_refs, refs[21 + 2 * n:], swapped)
        c = pl.program_id(0)

        @pl.when(c == 0)
        def _():
            carry_ref[...] = jnp.zeros_like(carry_ref)
            for start in starts:
                start()

        @pl.when(c == pass_step)
        def _():
            for step in passes:
                step()

        valid = (c * BLOCK + lax.broadcasted_iota(jnp.int32, (BLOCK, 1), 0)) >= PAD
        logf, k = _hgrn_gates(x_ref[:, HG_W:2 * HG_W], lb_ref[0:1, :], lb_ref[1:2, :], valid)
        e = _split_dot(lv_ref[...], logf, "nn")
        for h in range(HG_HEADS):
            sl = lambda part: x_ref[:, part * HG_W + h * BLOCK: part * HG_W + (h + 1) * BLOCK]
            hs = slice(h * BLOCK, (h + 1) * BLOCK)
            st_in = carry_ref[h]
            st_ref[0, h] = st_in
            seg = _seg_blocks(e, h)
            a = _hgrn_scores(sl(0), k[:, hs], *seg[3:])
            a_ref[0, h] = a.astype(BF16)
            raw, st_out = _hgrn_mix(sl(0), k[:, hs], sl(2), st_in, a, *seg[:3])
            raw_ref[:, hs] = raw
            y_ref[:, hs] = _hgrn_norm(raw, sl(3), ng_ref[...]).astype(BF16)
            carry_ref[h] = st_out

        qs, kc, vc = _att_load(cur_ref, cc, sc, True)
        _, kp, vp = _att_load(prev_ref, cp, sp, False)
        km, vm = _att_load_meta(meta_ref, cm, sm)
        sinkrows = _att_sinkrows(sink_ref)
        own4, band4, meta4 = _att_masks(c)
        for g in range(2):
            s, s_meta = _att_scores(qs[2 * g], qs[2 * g + 1], kc, kp, km, g, own4, band4, meta4)
            pr, pr_meta, pr_sink = _att_probs(s, s_meta, sinkrows[g])
            pr_ref[0, g, :BLOCK, :] = pr.astype(BF16)
            pr_ref[0, g, BLOCK:BLOCK + N_META, :] = pr_meta.astype(BF16)
            pr_ref[0, g, BLOCK + N_META:, :] = jnp.broadcast_to(pr_sink, (N_META, 4 * BLOCK)).astype(BF16)
            for j, tile in enumerate(_att_values(pr, pr_meta, vc, vp, vm, g, own4)):
                o_ref[:, (2 * g + j) * BLOCK:(2 * g + j + 1) * BLOCK] = tile.astype(BF16)

        @pl.when(c == nb - 1)
        def _():
            for wait in waits:
                wait()

    return pl.pallas_call(
        body, name="mixers_fwd", grid=(nb,),
        in_specs=[pl.BlockSpec((BLOCK, 4 * HG_W), lambda c: (c, 0)), pl.BlockSpec((2, HG_W), lambda c: (0, 0)),
                  pl.BlockSpec((1, BLOCK), lambda c: (0, 0)), pl.BlockSpec(lv.shape, lambda c: (0, 0))]
        + _att_specs(lambda c: c) + c_in,
        out_specs=[pl.BlockSpec((BLOCK, HG_W), lambda c: (c, 0)), pl.BlockSpec((BLOCK, ATT_QW), lambda c: (c, 0)),
                   pl.BlockSpec((1, HG_HEADS, BLOCK, BLOCK), lambda c: (c, 0, 0, 0)),
                   pl.BlockSpec((1, HG_HEADS, BLOCK, BLOCK), lambda c: (c, 0, 0, 0)),
                   pl.BlockSpec((BLOCK, HG_W), lambda c: (c, 0)),
                   pl.BlockSpec((1, 2, ATT_KEYS, 4 * BLOCK), lambda c: (c, 0, 0, 0))] + c_out,
        out_shape=[jax.ShapeDtypeStruct((p, HG_W), BF16), jax.ShapeDtypeStruct((p, ATT_QW), BF16),
                   jax.ShapeDtypeStruct((nb, HG_HEADS, BLOCK, BLOCK), F32),
                   jax.ShapeDtypeStruct((nb, HG_HEADS, BLOCK, BLOCK), BF16),
                   jax.ShapeDtypeStruct((p, HG_W), F32),
                   jax.ShapeDtypeStruct((nb, 2, ATT_KEYS, 4 * BLOCK), BF16)] + c_shapes,
        scratch_shapes=[pltpu.VMEM((HG_HEADS, BLOCK, BLOCK), F32)] + c_sems,
        compiler_params=_cparams(("arbitrary",)),
    )(proj_hg, lbounds, norm_g, lv, proj_att, proj_att, proj_att, cos, sin, cos, sin, cos, sin, sinks, *shards)


def _tile(rows, preferred):
    return preferred if rows % preferred == 0 else _row_tile(rows, preferred)


def _in_proj(h0b, w_in):
    p = h0b.shape[0]
    tm = _row_tile(p, 640)
    hg_end = 4 * HG_W

    def body(h_ref, w_ref, hg_ref, att_ref, gates_ref):
        h = h_ref[...]
        hg_ref[...] = _dot(h, w_ref[:, :hg_end], "nn")
        att_ref[...] = _dot(h, w_ref[:, hg_end:MIX_W], "nn")
        gates_ref[...] = _dot(h, w_ref[:, MIX_W:], "nn").astype(BF16)

    row = lambda w: pl.BlockSpec((tm, w), lambda i: (i, 0))
    return pl.pallas_call(
        body, name="in_proj", grid=(p // tm,),
        in_specs=[row(D_MODEL), pl.BlockSpec(w_in.shape, lambda i: (0, 0), pipeline_mode=pl.Buffered(1))],
        out_specs=[row(hg_end), row(MIX_W - hg_end), row(2 * D_MODEL)],
        out_shape=[jax.ShapeDtypeStruct((p, hg_end), F32), jax.ShapeDtypeStruct((p, MIX_W - hg_end), F32),
                   jax.ShapeDtypeStruct((p, 2 * D_MODEL), BF16)],
        compiler_params=_cparams(("arbitrary",)),
    )(h0b, w_in)


def _branch_mix(yh, oa, gates, w_bh, w_ba):
    y_hg = _dot(yh, w_bh, "nn")
    y_att = _dot(oa, w_ba, "nn")
    s1 = jax.nn.sigmoid(gates[:, :D_MODEL].astype(F32))
    s2 = jax.nn.sigmoid(gates[:, D_MODEL:].astype(F32))
    return s1 * y_hg + s2 * y_att, y_hg, y_att, s1, s2


def _mix_out_ln1(yh, oa, gates, h0, w_bh, w_ba, w_out, g1, b1):
    p = yh.shape[0]
    tr = _tile(p, 320)

    def body(yh_ref, oa_ref, g_ref, h0_ref, wbh_ref, wba_ref, wo_ref, g1_ref, b1_ref,
             mix_ref, h1_ref, h1b_ref, xh_ref, rs_ref):
        mixin = _branch_mix(yh_ref[...], oa_ref[...], g_ref[...], wbh_ref[...], wba_ref[...])[0]
        mix_ref[...] = mixin.astype(BF16)
        xhat, rstd = _ln_stats(ALPHA * h0_ref[...] + _dot(mixin, wo_ref[...], "nn"))
        h1 = xhat * g1_ref[...] + b1_ref[...]
        h1_ref[...] = h1
        h1b_ref[...] = h1.astype(BF16)
        xh_ref[...] = xhat
        rs_ref[...] = rstd

    row = lambda w: pl.BlockSpec((tr, w), lambda i: (i, 0))
    const = lambda a: pl.BlockSpec(a.shape, lambda i: (0, 0))
    return pl.pallas_call(
        body, name="mix_out_ln1", grid=(p // tr,),
        in_specs=[row(HG_W), row(ATT_QW), row(2 * D_MODEL), row(D_MODEL), const(w_bh), const(w_ba), const(w_out),
                  const(g1), const(b1)],
        out_specs=[row(D_MODEL), row(D_MODEL), row(D_MODEL), row(D_MODEL), row(1)],
        out_shape=[jax.ShapeDtypeStruct((p, D_MODEL), BF16), jax.ShapeDtypeStruct((p, D_MODEL), F32),
                   jax.ShapeDtypeStruct((p, D_MODEL), BF16), jax.ShapeDtypeStruct((p, D_MODEL), F32),
                   jax.ShapeDtypeStruct((p, 1), F32)],
        compiler_params=_cparams(("arbitrary",)),
    )(yh, oa, gates, h0, w_bh, w_ba, w_out, g1, b1)


FF_T = D_FF // 2


def _ffn_in_swiglu(h1, w_fi):
    p = h1.shape[0]
    tm = _row_tile(p, 640)

    def body(h_ref, w_ref, au_ref, s_ref):
        au = _dot(h_ref[...], w_ref[...], "nn")
        au_ref[...] = au.astype(BF16)
        s_ref[...] = (jax.nn.silu(au[:, :FF_T]) * au[:, FF_T:]).astype(BF16)

    return pl.pallas_call(
        body, name="ffn_in_swiglu", grid=(D_FF // FF_T, p // tm),
        in_specs=[pl.BlockSpec((tm, D_MODEL), lambda j, i: (i, 0)), pl.BlockSpec((D_MODEL, 2 * FF_T), lambda j, i: (0, j))],
        out_specs=[pl.BlockSpec((tm, 2 * FF_T), lambda j, i: (i, j)), pl.BlockSpec((tm, FF_T), lambda j, i: (i, j))],
        out_shape=[jax.ShapeDtypeStruct((p, 2 * D_FF), BF16), jax.ShapeDtypeStruct((p, D_FF), BF16)],
        compiler_params=_cparams(("arbitrary", "arbitrary")),
    )(h1, w_fi)


def _ffn_out_loss(s, w_fo, h1, g2, b2, target):
    p = h1.shape[0]
    tr = _row_tile(p, 640)
    k = tr // BLOCK

    def body(*refs):
        s_ref, w_ref, h_ref, g_ref, b_ref = refs[:5]
        dr_ref, drb_ref, loss_ref, dg_ref, db_ref = refs[5 + k:]
        i = pl.program_id(0)
        xhat, rstd = _ln_stats(ALPHA * h_ref[...] + _dot(s_ref[...], w_ref[...], "nn"))
        y = xhat * g_ref[...] + b_ref[...]
        row = i * tr + lax.broadcasted_iota(jnp.int32, (tr, 1), 0)
        tgt = jnp.concatenate([r[...] for r in refs[5:5 + k]], axis=0)
        err = jnp.where(row >= BLOCK, y - tgt, 0.0)
        dr, dg, db = _ln_bwd(err * (1.0 / D_MODEL), xhat, rstd, g_ref[...])
        dr_ref[...] = dr
        drb_ref[...] = dr.astype(BF16)
        e2 = jnp.sum(err * err, axis=0, keepdims=True)
        part = e2[:, 0:BLOCK]
        for j in range(1, D_MODEL // BLOCK):
            part = part + e2[:, j * BLOCK:(j + 1) * BLOCK]
        part = part * (0.5 / D_MODEL)

        @pl.when(i == 0)
        def _():
            loss_ref[...] = part
            dg_ref[...] = dg
            db_ref[...] = db

        @pl.when(i > 0)
        def _():
            loss_ref[...] += part
            dg_ref[...] += dg
            db_ref[...] += db

    vec = pl.BlockSpec((1, D_MODEL), lambda i: (0, 0))
    rowsp = pl.BlockSpec((tr, D_MODEL), lambda i: (i, 0))
    return pl.pallas_call(
        body, name="ffn_out_loss", grid=(p // tr,),
        in_specs=[pl.BlockSpec((tr, D_FF), lambda i: (i, 0)), pl.BlockSpec((D_FF, D_MODEL), lambda i: (0, 0)),
                  rowsp, vec, vec] + _token_streams(tr),
        out_specs=[rowsp, rowsp, pl.BlockSpec((1, BLOCK), lambda i: (0, 0)), vec, vec],
        out_shape=[jax.ShapeDtypeStruct((p, D_MODEL), F32), jax.ShapeDtypeStruct((p, D_MODEL), BF16),
                   jax.ShapeDtypeStruct((1, BLOCK), F32), jax.ShapeDtypeStruct((1, D_MODEL), F32),
                   jax.ShapeDtypeStruct((1, D_MODEL), F32)],
        compiler_params=_cparams(("arbitrary",)),
    )(s, w_fo, h1, g2, b2, *([target] * k))


def _ffn_bwd(dr2, w_fo, au, w_fi):
    p = au.shape[0]
    tm = _tile(p, 320)

    def body(d_ref, wo_ref, au_ref, wi_ref, dau_ref, dh_ref):
        d = d_ref[...]
        dh = 0.0
        for j in range(D_FF // FF_T):
            a_cols = slice(2 * j * FF_T, (2 * j + 1) * FF_T)
            u_cols = slice((2 * j + 1) * FF_T, (2 * j + 2) * FF_T)
            ds = _dot(d, wo_ref[j * FF_T:(j + 1) * FF_T, :], "nt")
            _, vjp = jax.vjp(lambda a, u: jax.nn.silu(a) * u, au_ref[:, a_cols].astype(F32), au_ref[:, u_cols].astype(F32))
            da, du = vjp(ds)
            da, du = da.astype(BF16), du.astype(BF16)
            dau_ref[:, a_cols] = da
            dau_ref[:, u_cols] = du
            dh = dh + _dot(da, wi_ref[:, a_cols], "nt") + _dot(du, wi_ref[:, u_cols], "nt")
        dh_ref[...] = dh

    row = lambda w: pl.BlockSpec((tm, w), lambda i: (i, 0))
    kept = lambda a: pl.BlockSpec(a.shape, lambda i: (0, 0), pipeline_mode=pl.Buffered(1))
    return pl.pallas_call(
        body, name="ffn_bwd", grid=(p // tm,),
        in_specs=[row(D_MODEL), kept(w_fo), row(2 * D_FF), kept(w_fi)],
        out_specs=[row(2 * D_FF), row(D_MODEL)],
        out_shape=[jax.ShapeDtypeStruct((p, 2 * D_FF), BF16), jax.ShapeDtypeStruct((p, D_MODEL), F32)],
        compiler_params=_cparams(("arbitrary",)),
    )(dr2, w_fo, au, w_fi)


def _ln1_mix_bwd(dr2, dh1_ffn, xhat1, rstd1, g1, yh, oa, gates, w_bh, w_ba, w_out):
    p = yh.shape[0]
    tr = _tile(p, 320)

    def body(a_ref, b_ref, xh_ref, rs_ref, g1_ref, yh_ref, oa_ref, g_ref, wbh_ref, wba_ref, wo_ref,
             dr_ref, dyhg_ref, dyat_ref, dgt_ref, dyh_ref, doa_ref, dg_ref, db_ref):
        i = pl.program_id(0)
        dr, dg, db = _ln_bwd(ALPHA * a_ref[...] + b_ref[...], xh_ref[...], rs_ref[...], g1_ref[...])
        dr_ref[...] = dr
        d = _dot(dr, wo_ref[...], "nt")
        _, y_hg, y_att, s1, s2 = _branch_mix(yh_ref[...], oa_ref[...], g_ref[...], wbh_ref[...], wba_ref[...])
        dy_hg = d * s1
        dy_att = d * s2
        dyhg_ref[...] = dy_hg.astype(BF16)
        dyat_ref[...] = dy_att.astype(BF16)
        dgt_ref[:, :D_MODEL] = (d * y_hg * s1 * (1.0 - s1)).astype(BF16)
        dgt_ref[:, D_MODEL:] = (d * y_att * s2 * (1.0 - s2)).astype(BF16)
        dyh_ref[...] = _dot(dy_hg, wbh_ref[...], "nt")
        doa_ref[...] = _dot(dy_att, wba_ref[...], "nt")

        @pl.when(i == 0)
        def _():
            dg_ref[...] = dg
            db_ref[...] = db

        @pl.when(i > 0)
        def _():
            dg_ref[...] += dg
            db_ref[...] += db

    row = lambda w: pl.BlockSpec((tr, w), lambda i: (i, 0))
    const = lambda a: pl.BlockSpec(a.shape, lambda i: (0, 0))
    vec = pl.BlockSpec((1, D_MODEL), lambda i: (0, 0))
    return pl.pallas_call(
        body, name="ln1_mix_bwd", grid=(p // tr,),
        in_specs=[row(D_MODEL), row(D_MODEL), row(D_MODEL), row(1), vec, row(HG_W), row(ATT_QW), row(2 * D_MODEL),
                  const(w_bh), const(w_ba), const(w_out)],
        out_specs=[row(D_MODEL), row(D_MODEL), row(D_MODEL), row(2 * D_MODEL), row(HG_W), row(ATT_QW), vec, vec],
        out_shape=[jax.ShapeDtypeStruct((p, D_MODEL), F32), jax.ShapeDtypeStruct((p, D_MODEL), BF16),
                   jax.ShapeDtypeStruct((p, D_MODEL), BF16), jax.ShapeDtypeStruct((p, 2 * D_MODEL), BF16),
                   jax.ShapeDtypeStruct((p, HG_W), F32), jax.ShapeDtypeStruct((p, ATT_QW), F32),
                   jax.ShapeDtypeStruct((1, D_MODEL), F32), jax.ShapeDtypeStruct((1, D_MODEL), F32)],
        compiler_params=_cparams(("arbitrary",)),
    )(dr2, dh1_ffn, xhat1, rstd1, g1, yh, oa, gates, w_bh, w_ba, w_out)


MIX_W = 4 * HG_W + ATT_QW + 2 * ATT_KVW
ATT_KEYS = BLOCK + 2 * N_META


def _mixers_bwd(proj_hg, proj_att, lbounds, norm_g, lv, states, scores, raw, probs, cos, sin, sinks, dyh, doa,
                parts, swapped):
    p = proj_hg.shape[0]
    nb = p // BLOCK
    n = len(parts)
    kvw = 2 * ATT_KVW
    rev = lambda s: nb - 1 - s
    c_in, c_out, c_shapes, c_sems = _comm_specs(parts, N_PEERS)

    def body(*refs):
        (x_ref, lb_ref, ng_ref, lv_ref, st_ref, a_ref, raw_ref, pr_ref, cur_ref, prev_ref, meta_ref, cc, sc, cp, sp,
         cm, sm, sink_ref, dy_ref, do_ref) = refs[:20]
        part_refs = refs[20:20 + n]
        dx_ref, dlb_ref, dng_ref, dsink_ref = refs[20 + n:24 + n]
        recv_refs = refs[24 + n:24 + 2 * n]
        dcarry_ref, dkv_next_ref, dkv_meta_ref = refs[24 + 2 * n:27 + 2 * n]
        starts, waits = _scatter_behind(part_refs, recv_refs, refs[27 + 2 * n:], swapped)
        step = pl.program_id(0)
        c = rev(step)

        @pl.when(step == 0)
        def _():
            dcarry_ref[...] = jnp.zeros_like(dcarry_ref)
            dkv_next_ref[...] = jnp.zeros_like(dkv_next_ref)
            dkv_meta_ref[...] = jnp.zeros_like(dkv_meta_ref)
            dlb_ref[...] = jnp.zeros_like(dlb_ref)
            dng_ref[...] = jnp.zeros_like(dng_ref)
            dsink_ref[...] = jnp.zeros_like(dsink_ref)
            for start in starts:
                start()

        fh = _first_half(BLOCK)
        qs, kc, vc = _att_load(cur_ref, cc, sc, True)
        _, kp, vp = _att_load(prev_ref, cp, sp, False)
        km, vm = _att_load_meta(meta_ref, cm, sm)
        own4, band4, meta4 = _att_masks(c)
        att0 = 4 * HG_W
        dkm = dkp = dkc = dvm = dvp = dvc = 0.0
        dsinkrows = []
        for g in range(2):
            pr = pr_ref[0, g, :BLOCK, :].astype(F32)
            pr_meta = pr_ref[0, g, BLOCK:BLOCK + N_META, :].astype(F32)
            pr_sink = jnp.max(pr_ref[0, g, BLOCK + N_META:, :].astype(F32), axis=0, keepdims=True)
            _, values_vjp = jax.vjp(lambda *a, g=g: _att_values(*a, g, own4), pr, pr_meta, vc, vp, vm)
            dpr, dpr_meta, dvc_g, dvp_g, dvm_g = values_vjp(
                [do_ref[:, (2 * g + j) * BLOCK:(2 * g + j + 1) * BLOCK] for j in range(2)])
            ds, ds_meta, dsinkrow = _att_probs_bwd(pr, pr_meta, pr_sink, dpr, dpr_meta)
            _, scores_vjp = jax.vjp(lambda *a, g=g: _att_scores(*a, g, own4, band4, meta4),
                                    qs[2 * g], qs[2 * g + 1], kc, kp, km)
            dqa, dqb, dkc_g, dkp_g, dkm_g = scores_vjp((ds, ds_meta))
            for j, dq in enumerate((dqa, dqb)):
                dx_ref[:, att0 + (2 * g + j) * BLOCK:att0 + (2 * g + j + 1) * BLOCK] = _rope_t(
                    dq, cc[...], sc[...], fh).astype(BF16)
            dkm, dkp, dkc = dkm + dkm_g, dkp + dkp_g, dkc + dkc_g
            dvm, dvp, dvc = dvm + dvm_g, dvp + dvp_g, dvc + dvc_g
            dsinkrows.append(dsinkrow)
        ds0, ds1 = dsinkrows
        dkv_meta_ref[:, :BLOCK] += _rope_t(dkm, cm[PAD:BLOCK, :], sm[PAD:BLOCK, :], _first_half(N_META))
        dkv_meta_ref[:, BLOCK:] += dvm
        last = jnp.where(c == 0, 1.0, 0.0)
        to_meta_rows = lambda m: jnp.concatenate([jnp.zeros((PAD, BLOCK), F32), last * m], axis=0)
        dk = _rope_t(dkc, cc[...], sc[...], fh) + dkv_next_ref[:, :BLOCK] + to_meta_rows(dkv_meta_ref[:, :BLOCK])
        dv = dvc + dkv_next_ref[:, BLOCK:] + to_meta_rows(dkv_meta_ref[:, BLOCK:])
        dx_ref[:, att0 + ATT_QW:att0 + ATT_QW + ATT_KVW] = dk.astype(BF16)
        dx_ref[:, att0 + ATT_QW + ATT_KVW:] = dv.astype(BF16)
        dkv_next_ref[:, :BLOCK] = _rope_t(dkp, cp[...], sp[...], fh)
        dkv_next_ref[:, BLOCK:] = dvp
        sink_rows = []
        for dsg in (ds0, ds1):
            for j in range(4):
                tot = jnp.sum(dsg[:, j * BLOCK:(j + 1) * BLOCK], axis=1, keepdims=True)
                sink_rows.append(jnp.broadcast_to(tot, (1, BLOCK)))
        dsink_ref[...] += jnp.concatenate(sink_rows, axis=0)

        valid = (c * BLOCK + lax.broadcasted_iota(jnp.int32, (BLOCK, 1), 0)) >= PAD
        (logf, k), gates_vjp = jax.vjp(lambda hf, a0, a1: _hgrn_gates(hf, a0, a1, valid),
                                       x_ref[:, HG_W:2 * HG_W], lb_ref[0:1, :], lb_ref[1:2, :])
        lvv = lv_ref[...]
        e = _split_dot(lvv, logf, "nn")
        dng = jnp.zeros((1, BLOCK), F32)
        dk, dseg = [], []
        for h in range(HG_HEADS):
            sl = lambda part: x_ref[:, part * HG_W + h * BLOCK: part * HG_W + (h + 1) * BLOCK]
            hs = slice(h * BLOCK, (h + 1) * BLOCK)
            seg = _seg_blocks(e, h)
            _, norm_vjp = jax.vjp(_hgrn_norm, raw_ref[:, hs], sl(3), ng_ref[...])
            draw, dhg, dngh = norm_vjp(dy_ref[:, hs])
            _, mix_vjp = jax.vjp(_hgrn_mix, sl(0), k[:, hs], sl(2), st_ref[0, h], a_ref[0, h].astype(F32), *seg[:3])
            dhq, dkh, dhi, dst, da, *dseg_mix = mix_vjp((draw, dcarry_ref[h]))
            _, scores_vjp = jax.vjp(_hgrn_scores, sl(0), k[:, hs], *seg[3:])
            dhq2, dkh2, *dseg_lvl = scores_vjp(da)
            for part, val in ((0, dhq + dhq2), (2, dhi), (3, dhg)):
                dx_ref[:, part * HG_W + h * BLOCK: part * HG_W + (h + 1) * BLOCK] = val.astype(BF16)
            dk.append(dkh + dkh2)
            dseg.append(jnp.concatenate(dseg_mix + dseg_lvl, axis=0))
            dng = dng + dngh
            dcarry_ref[h] = dst
        dlogf = _split_dot(lvv, jnp.concatenate(dseg, axis=1), "tn")
        dhf, da0, da1 = gates_vjp((dlogf, jnp.concatenate(dk, axis=1)))
        dx_ref[:, HG_W:2 * HG_W] = dhf.astype(BF16)
        dlb_ref[0:1, :] += da0
        dlb_ref[1:2, :] += da1
        dng_ref[...] += dng

        @pl.when(step == nb - 1)
        def _():
            for wait in waits:
                wait()

    const = lambda shape: pl.BlockSpec(shape, lambda s: (0,) * len(shape))
    per_head = pl.BlockSpec((1, HG_HEADS, BLOCK, BLOCK), lambda s: (rev(s), 0, 0, 0))
    return pl.pallas_call(
        body, name="mixers_bwd", grid=(nb,),
        in_specs=[pl.BlockSpec((BLOCK, 4 * HG_W), lambda s: (rev(s), 0)), const((2, HG_W)), const((1, BLOCK)),
                  const(lv.shape), per_head, per_head, pl.BlockSpec((BLOCK, HG_W), lambda s: (rev(s), 0)),
                  pl.BlockSpec((1, 2, ATT_KEYS, 4 * BLOCK), lambda s: (rev(s), 0, 0, 0))]
        + _att_specs(rev)
        + [pl.BlockSpec((BLOCK, HG_W), lambda s: (rev(s), 0)), pl.BlockSpec((BLOCK, ATT_QW), lambda s: (rev(s), 0))]
        + c_in,
        out_specs=[pl.BlockSpec((BLOCK, MIX_W), lambda s: (rev(s), 0)), const((2, HG_W)), const((1, BLOCK)),
                   const((ATT_HEADS, BLOCK))] + c_out,
        out_shape=[jax.ShapeDtypeStruct((p, MIX_W), BF16), jax.ShapeDtypeStruct((2, HG_W), F32),
                   jax.ShapeDtypeStruct((1, BLOCK), F32), jax.ShapeDtypeStruct((ATT_HEADS, BLOCK), F32)] + c_shapes,
        scratch_shapes=[pltpu.VMEM((HG_HEADS, BLOCK, BLOCK), F32), pltpu.VMEM((BLOCK, kvw), F32),
                        pltpu.VMEM((N_META, kvw), F32)] + c_sems,
        compiler_params=_cparams(("arbitrary",)),
    )(proj_hg, lbounds, norm_g, lv, states, scores, raw, probs, proj_att, proj_att, proj_att, cos, sin, cos, sin,
      cos, sin, sinks, dyh, doa, *parts)


_HBM = pl.BlockSpec(memory_space=pltpu.HBM)
_SEM = pl.BlockSpec(memory_space=pltpu.SEMAPHORE)
_ORDERED_BY_DATA = pltpu.CompilerParams(has_side_effects=pltpu.SideEffectType.DATAFLOW_SIDE_EFFECTING)


def _exchange_copies(part_ref, land_ref, send_sems, recv_sems):
    place = _place()
    return [pltpu.make_async_remote_copy(
        src_ref=part_ref.at[_slot(_peer(place, flip), False)], dst_ref=land_ref.at[r], send_sem=send_sems.at[r],
        recv_sem=recv_sems.at[r], device_id=_peer(place, flip), device_id_type=MESH) for r, flip in enumerate(_FLIPS)]


def _exchange_start(parts, name):
    def body(part_ref, land_ref, send_sems, recv_sems, part_thru, land_thru, token):
        for cp in _exchange_copies(part_ref, land_ref, send_sems, recv_sems):
            cp.start()
        token[...] = jnp.zeros_like(token)

    land = (N_PEERS,) + parts.shape[1:]
    return pl.pallas_call(
        body, name=name,
        out_shape=(pltpu.SemaphoreType.DMA((N_PEERS,)), pltpu.SemaphoreType.DMA((N_PEERS,)),
                   pltpu.HBM(parts.shape, parts.dtype), pltpu.HBM(land, parts.dtype), jax.ShapeDtypeStruct((8, BLOCK), F32)),
        in_specs=(_HBM, _HBM), out_specs=(_SEM, _SEM, _HBM, _HBM, pl.BlockSpec(memory_space=pltpu.VMEM)),
        input_output_aliases={0: 2, 1: 3}, compiler_params=_ORDERED_BY_DATA,
    )(pltpu.with_memory_space_constraint(parts, pltpu.HBM),
      pltpu.with_memory_space_constraint(lax.empty(land, parts.dtype), pltpu.HBM))


def _exchange_wait(send_sems, recv_sems, part_thru, land_thru, after, name):
    def body(part_ref, land_ref, send_sems, recv_sems, after_ref, part_out, land_out):
        for cp in _exchange_copies(part_ref, land_ref, send_sems, recv_sems):
            cp.wait_send()
            cp.wait_recv()

    return pl.pallas_call(
        body, name=name,
        out_shape=(pltpu.HBM(part_thru.shape, part_thru.dtype), pltpu.HBM(land_thru.shape, land_thru.dtype)),
        in_specs=(_HBM, _HBM, _SEM, _SEM, pl.BlockSpec(memory_space=pl.ANY)), out_specs=(_HBM, _HBM),
        input_output_aliases={0: 0, 1: 1}, compiler_params=_ORDERED_BY_DATA,
    )(part_thru, land_thru, send_sems, recv_sems, after)


def _embed_bwd(dmix, dgates, w_in, dr1, xhat0, rstd0, g0):
    p = dmix.shape[0]
    tm = _row_tile(p, 640)
    nm = p // tm

    def body(a_ref, g_ref, w_ref, dr_ref, xh_ref, rs_ref, g0_ref, gx_ref, lead_ref, dg_ref, db_ref, buf_ref, sem):
        i = pl.program_id(0)
        first = pltpu.make_async_copy(buf_ref.at[0, pl.ds(BLOCK, tm - BLOCK)], gx_ref.at[pl.ds(0, tm - BLOCK)],
                                      sem.at[0])
        later = lambda t: pltpu.make_async_copy(buf_ref.at[t % 2], gx_ref.at[pl.ds(t * tm - BLOCK, tm)], sem.at[t % 2])

        @pl.when(i == 2)
        def _():
            first.wait()

        @pl.when(i > 2)
        def _():
            later(i - 2).wait()

        dh0 = (ALPHA * dr_ref[...] + _dot(a_ref[...], w_ref[:, :MIX_W], "nt")
               + _dot(g_ref[...], w_ref[:, MIX_W:], "nt"))
        row = i * tm + lax.broadcasted_iota(jnp.int32, (tm, 1), 0)
        dx, dg, db = _ln_bwd(jnp.where(row >= PAD, dh0, 0.0), xh_ref[...], rs_ref[...], g0_ref[...])
        buf_ref[i % 2] = dx

        @pl.when(i == 0)
        def _():
            lead_ref[...] = dx[:BLOCK]
            dg_ref[...] = dg
            db_ref[...] = db
            first.start()

        @pl.when(i > 0)
        def _():
            dg_ref[...] += dg
            db_ref[...] += db
            later(i).start()

        @pl.when(i == nm - 1)
        def _():
            for t in (nm - 2, nm - 1):
                if t >= 0:
                    (first if t == 0 else later(t)).wait()

    row = lambda w: pl.BlockSpec((tm, w), lambda i: (i, 0))
    vec = pl.BlockSpec((1, D_MODEL), lambda i: (0, 0))
    return pl.pallas_call(
        body, name="embed_bwd", grid=(nm,),
        in_specs=[row(dmix.shape[1]), row(dgates.shape[1]),
                  pl.BlockSpec(w_in.shape, lambda i: (0, 0), pipeline_mode=pl.Buffered(1)), row(D_MODEL), row(D_MODEL),
                  row(1), vec],
        out_specs=[pl.BlockSpec(memory_space=pl.ANY), pl.BlockSpec((BLOCK, D_MODEL), lambda i: (0, 0)), vec, vec],
        out_shape=[jax.ShapeDtypeStruct((p - BLOCK, D_MODEL), F32), jax.ShapeDtypeStruct((BLOCK, D_MODEL), F32),
                   jax.ShapeDtypeStruct((1, D_MODEL), F32), jax.ShapeDtypeStruct((1, D_MODEL), F32)],
        scratch_shapes=[pltpu.VMEM((2, tm, D_MODEL), F32), pltpu.SemaphoreType.DMA((2,))],
        compiler_params=_cparams(("arbitrary",)),
    )(dmix, dgates, w_in, dr1, xhat0, rstd0, g0)


_LATE = ("w_branch_hg", "w_branch_attn", "w_out", "w_ffn_in", "w_ffn_out")
_COLUMN_SHARDED = ("meta_tokens", "w_in", "w_branch_hg", "w_branch_attn", "w_ffn_in")
_SWAPPED = ("w_ffn_in",)


def _whole(name, gathered):
    _, r, c = gathered.shape
    if name in _COLUMN_SHARDED:
        return jnp.transpose(gathered, (1, 0, 2)).reshape(r, N_DEV * c)
    return gathered.reshape(N_DEV * r, c)


def _slots(name, whole):
    r, c = whole.shape
    if name in _COLUMN_SHARDED:
        return jnp.transpose(whole.reshape(r, N_DEV, c // N_DEV), (1, 0, 2))
    return whole.reshape(N_DEV, r // N_DEV, c)


def _device_step(x, target, meta_shard, ln_emb_g, ln_emb_b, w_in_shard, lbounds, norm_g, sinks, late_shards,
                 ln1_g, ln1_b, ln2_g, ln2_b):
    p = x.shape[0] + BLOCK
    lv = _level_stack()
    cos, sin = _rope_tables(p)
    swapped = [n in _SWAPPED for n in _LATE]

    h0, h0b, xhat0, rstd0, _, g_win = _embed_ln(x, meta_shard, w_in_shard, ln_emb_g, ln_emb_b)
    w_in = _whole("w_in", g_win)
    proj_hg, proj_att, gates = _in_proj(h0b, w_in)
    yh, oa, states, scores, raw, probs, *gathered = _mixers_fwd(
        proj_hg, proj_att, lbounds, norm_g, lv, cos, sin, sinks, late_shards, swapped)
    w_bh, w_ba, w_out, w_fi, w_fo = [_whole(n, g) for n, g in zip(_LATE, gathered)]
    mixin, h1, h1b, xhat1, rstd1 = _mix_out_ln1(yh, oa, gates, h0, w_bh, w_ba, w_out, ln1_g, ln1_b)
    au, sw = _ffn_in_swiglu(h1b, w_fi)
    dr2, dr2b, loss_part, dg2, db2 = _ffn_out_loss(sw, w_fo, h1, ln2_g, ln2_b, target)

    mtn = functools.partial(_tiled_matmul_tn, tm=_row_tile(p, 1664), out_dtype=BF16)
    whole = functools.partial(_tiled_matmul_tn, tm=p, out_dtype=BF16)
    d_wfo = mtn(sw, dr2b, tk=FF_T, tn=D_MODEL, name="grad_w_ffn_out")
    dau, dh1_ffn = _ffn_bwd(dr2b, w_fo, au, w_fi)
    d_wfi = whole(h1b, dau, tk=D_MODEL // 2, tn=D_MODEL // 2, name="grad_w_ffn_in")
    dr1, dy_hg, dy_att, dgates, dyh, doa, dg1, db1 = _ln1_mix_bwd(
        dr2, dh1_ffn, xhat1, rstd1, ln1_g, yh, oa, gates, w_bh, w_ba, w_out)
    d_wout = mtn(mixin, dr1, tk=D_MODEL, tn=D_MODEL, name="grad_w_out")
    d_wbh = mtn(yh, dy_hg, tk=HG_W, tn=D_MODEL, name="grad_w_branch_hg")
    d_wba = mtn(oa, dy_att, tk=ATT_QW, tn=D_MODEL, name="grad_w_branch_attn")
    late_parts = [_slots(n, g) for n, g in zip(_LATE, (d_wbh, d_wba, d_wout, d_wfi, d_wfo))]
    dmix, d_lb, d_ng, d_sink, *late_recv = _mixers_bwd(
        proj_hg, proj_att, lbounds, norm_g, lv, states, scores, raw, probs, cos, sin, sinks, dyh, doa, late_parts,
        swapped)
    d_win = jnp.concatenate([whole(h0b, dmix, tk=D_MODEL, tn=D_MODEL // 4, name="grad_w_in_mixers"),
                             mtn(h0b, dgates, tk=D_MODEL, tn=D_MODEL, name="grad_w_in_gates")], axis=1)
    *win_flight, token = _exchange_start(_slots("w_in", d_win), "w_in_grads_start")
    grad_x, dlead, dg0, db0 = _embed_bwd(dmix, dgates, w_in, dr1, xhat0, rstd0, ln_emb_g + token[0:1, 0:1])

    small = dict(ln_emb_g=dg0, ln_emb_b=db0, hg_lower_bounds=d_lb, hg_norm_g=d_ng, ln1_g=dg1, ln1_b=db1, ln2_g=dg2,
                 ln2_b=db2)
    big = dict(zip(_LATE, zip(late_parts, late_recv)))
    return _pack_small(small, d_sink, dlead, loss_part), grad_x, big, win_flight


def _all_gather(arrs, dtypes, name):
    n = len(arrs)

    def body(*refs):
        ins, outs, stages = refs[:n], refs[n:2 * n], refs[2 * n:3 * n]
        send_sems, recv_sems, local_sems = refs[3 * n:]
        x, y, c = _place()
        sibling = (x, y, 1 - c)
        chips = [(1 - x, y), (x, 1 - y), (1 - x, 1 - y)]
        slot = lambda px, py, pc: 4 * px + 2 * py + pc

        def copy(w, k, block, to, from_stage=False):
            return pltpu.make_async_remote_copy(
                src_ref=stages[w] if from_stage else outs[w].at[slot(*block)], dst_ref=outs[w].at[slot(*block)],
                send_sem=send_sems.at[w, k], recv_sem=recv_sems.at[w, k], device_id=to, device_id_type=MESH)

        mine, first, passed = [], [], []
        for w in range(n):
            stages[w][...] = ins[w][...].astype(dtypes[w])
            mine.append(pltpu.make_async_copy(stages[w], outs[w].at[slot(x, y, c)], local_sems.at[w]))
            mine[-1].start()
        for w in range(n):
            first.append(copy(w, 0, (x, y, c), sibling, from_stage=True))
            first += [copy(w, 1 + j, (x, y, c), (*chip, c), from_stage=True) for j, chip in enumerate(chips)]
        for cp in first:
            cp.start()
        for j, chip in enumerate(chips):
            for w in range(n):
                copy(w, 1 + j, (*chip, c), (x, y, c)).wait_recv()
                passed.append(copy(w, 4 + j, (*chip, c), sibling))
                passed[-1].start()
        for w in range(n):
            copy(w, 0, sibling, (x, y, c)).wait_recv()
            for j, chip in enumerate(chips):
                copy(w, 4 + j, (*chip, 1 - c), (x, y, c)).wait_recv()
        for cp in first + passed:
            cp.wait_send()
        for cp in mine:
            cp.wait()

    return pl.pallas_call(
        body, name=name,
        in_specs=[pl.BlockSpec(memory_space=pltpu.VMEM)] * n,
        out_specs=[pl.BlockSpec(memory_space=pl.ANY)] * n,
        out_shape=[jax.ShapeDtypeStruct((N_DEV,) + a.shape, dt) for a, dt in zip(arrs, dtypes)],
        scratch_shapes=[pltpu.VMEM(a.shape, dt) for a, dt in zip(arrs, dtypes)]
        + [pltpu.SemaphoreType.DMA((n, 7)), pltpu.SemaphoreType.DMA((n, 7)), pltpu.SemaphoreType.DMA((n,))],
        compiler_params=pltpu.CompilerParams(vmem_limit_bytes=VMEM_LIMIT_BYTES),
    )(*arrs)


def _cast_shards(arrs):
    def body(*refs):
        for src, dst in zip(refs[:len(arrs)], refs[len(arrs):]):
            dst[...] = src[...].astype(BF16)

    return pl.pallas_call(body, name="cast_shards", out_shape=[jax.ShapeDtypeStruct(a.shape, BF16) for a in arrs],
                          compiler_params=pltpu.CompilerParams(vmem_limit_bytes=VMEM_LIMIT_BYTES))(*arrs)


def _shard_rows(rows):
    return rows if rows <= 512 else 256


def _adamw_math(w, g, m, v):
    m = ADAM_B1 * m + (1.0 - ADAM_B1) * g
    v = ADAM_B2 * v + (1.0 - ADAM_B2) * (g * g)
    m_hat = m / (1.0 - ADAM_B1 ** ADAM_STEP)
    v_hat = v / (1.0 - ADAM_B2 ** ADAM_STEP)
    delta = -ADAM_LR * (m_hat / (jnp.sqrt(v_hat) + ADAM_EPS) + ADAM_WD * w)
    return delta, m, v


def _reduce_adamw(parts, recv, own_slot, w, m, v, name):
    r, cdim = w.shape
    tr = _shard_rows(r)

    def body(idx_ref, p_ref, r_ref, w_ref, m_ref, v_ref, g_out, d_out, m_out, v_out):
        g = p_ref[0].astype(F32)
        for j in range(N_PEERS):
            g = g + r_ref[j].astype(F32)
        d, mn, vn = _adamw_math(w_ref[...], g, m_ref[...], v_ref[...])
        g_out[...] = g
        d_out[...] = d
        m_out[...] = mn
        v_out[...] = vn

    flat = pl.BlockSpec((tr, cdim), lambda i, idx_ref: (i, 0))
    return pl.pallas_call(
        body, name=name,
        grid_spec=pltpu.PrefetchScalarGridSpec(
            num_scalar_prefetch=1, grid=(r // tr,),
            in_specs=[pl.BlockSpec((1, tr, cdim), lambda i, idx_ref: (idx_ref[0], i, 0)),
                      pl.BlockSpec((N_PEERS, tr, cdim), lambda i, idx_ref: (0, i, 0)), flat, flat, flat],
            out_specs=[flat] * 4),
        out_shape=[jax.ShapeDtypeStruct((r, cdim), F32)] * 4,
        compiler_params=_cparams(("arbitrary",)),
    )(own_slot, parts, recv, w, m, v)


def _adamw_plain(w, g, m, v, name):
    def body(w_ref, g_ref, m_ref, v_ref, d_out, m_out, v_out):
        d_out[...], m_out[...], v_out[...] = _adamw_math(w_ref[...], g_ref[...], m_ref[...], v_ref[...])

    return pl.pallas_call(body, name=name, out_shape=[jax.ShapeDtypeStruct(w.shape, F32)] * 3)(w, g, m, v)


_SMALL = (("ln_emb_g", (1, D_MODEL)), ("ln_emb_b", (1, D_MODEL)), ("hg_lower_bounds", (2, HG_W)),
          ("hg_norm_g", (1, BLOCK)), ("attn_sinks", (1, ATT_HEADS)), ("ln1_g", (1, D_MODEL)), ("ln1_b", (1, D_MODEL)),
          ("ln2_g", (1, D_MODEL)), ("ln2_b", (1, D_MODEL)))
_SMALL_ROW, _LOSS_ROW = {}, 0
for _name, (_rows, _) in _SMALL:
    _SMALL_ROW[_name], _LOSS_ROW = _LOSS_ROW, _LOSS_ROW + _rows
_META_ROW = 16
SMALL_ROWS = _META_ROW + N_META
assert _LOSS_ROW < _META_ROW


def _pack_small(grads, d_sink, dlead, loss_part):
    names = [n for n, _ in _SMALL if n != "attn_sinks"]

    def body(*refs):
        ins = dict(zip(names, refs))
        sink_ref, lead_ref, loss_ref, o_ref = refs[len(names):]
        o_ref[...] = jnp.zeros_like(o_ref)
        for name, (rows, cols) in _SMALL:
            if name != "attn_sinks":
                o_ref[_SMALL_ROW[name]:_SMALL_ROW[name] + rows, :cols] = ins[name][...]
        head = lax.broadcasted_iota(jnp.int32, (ATT_HEADS, BLOCK), 0)
        lane = lax.broadcasted_iota(jnp.int32, (ATT_HEADS, BLOCK), 1)
        o_ref[_SMALL_ROW["attn_sinks"]:_SMALL_ROW["attn_sinks"] + 1, :BLOCK] = jnp.sum(
            jnp.where(head == lane, sink_ref[...], 0.0), axis=0, keepdims=True)
        o_ref[_LOSS_ROW:_LOSS_ROW + 1, :BLOCK] = loss_ref[...]
        o_ref[_META_ROW:, :] = lead_ref[PAD:BLOCK, :]

    return pl.pallas_call(body, name="pack_small", out_shape=jax.ShapeDtypeStruct((SMALL_ROWS, D_MODEL), F32))(
        *[grads[n] for n in names], d_sink, dlead, loss_part)


def _small_reduce_adamw(gathered, weights, mom1, mom2):
    n = len(_SMALL)

    def body(*refs):
        g_ref, w_refs, m_refs, v_refs = refs[0], refs[1:1 + n], refs[1 + n:1 + 2 * n], refs[1 + 2 * n:1 + 3 * n]
        outs = refs[1 + 3 * n:1 + 7 * n]
        meta_out, loss_out, sum_ref = refs[1 + 7 * n:]
        total = g_ref[0]
        for s in range(1, N_DEV):
            total = total + g_ref[s]
        sum_ref[...] = total
        for i, (name, (rows, cols)) in enumerate(_SMALL):
            g = sum_ref[_SMALL_ROW[name]:_SMALL_ROW[name] + rows, :cols]
            d, mn, vn = _adamw_math(w_refs[i][...], g, m_refs[i][...], v_refs[i][...])
            for out, val in zip(outs[4 * i:4 * i + 4], (g, d, mn, vn)):
                out[...] = val
        meta_out[...] = sum_ref[_META_ROW:, :]
        loss_out[...] = jnp.broadcast_to(jnp.sum(sum_ref[_LOSS_ROW:_LOSS_ROW + 1, :BLOCK]), (1, BLOCK))

    per_param = [jax.ShapeDtypeStruct(shape, F32) for _, shape in _SMALL for _ in range(4)]
    res = pl.pallas_call(
        body, name="small_reduce_adamw",
        out_shape=per_param + [jax.ShapeDtypeStruct((N_META, D_MODEL), F32), jax.ShapeDtypeStruct((1, BLOCK), F32)],
        scratch_shapes=[pltpu.VMEM((SMALL_ROWS, D_MODEL), F32)],
    )(gathered, *[d[name] for d in (weights, mom1, mom2) for name, _ in _SMALL])
    return {name: res[4 * i:4 * i + 4] for i, (name, _) in enumerate(_SMALL)}, res[-2], res[-1]


_WEIGHTS = ("meta_tokens", "ln_emb_g", "ln_emb_b", "w_in", "hg_lower_bounds", "hg_norm_g", "attn_sinks",
            "w_branch_hg", "w_branch_attn", "w_out", "ln1_g", "ln1_b", "w_ffn_in", "w_ffn_out", "ln2_g", "ln2_b")


def kernel(x, meta_tokens, ln_emb_g, ln_emb_b, w_in, hg_lower_bounds, hg_norm_g, attn_sinks, w_branch_hg, w_branch_attn, w_out, ln1_g, ln1_b, w_ffn_in, w_ffn_out, ln2_g, ln2_b, loss_target, m_meta_tokens, m_ln_emb_g, m_ln_emb_b, m_w_in, m_hg_lower_bounds, m_hg_norm_g, m_attn_sinks, m_w_branch_hg, m_w_branch_attn, m_w_out, m_ln1_g, m_ln1_b, m_w_ffn_in, m_w_ffn_out, m_ln2_g, m_ln2_b, v_meta_tokens, v_ln_emb_g, v_ln_emb_b, v_w_in, v_hg_lower_bounds, v_hg_norm_g, v_attn_sinks, v_w_branch_hg, v_w_branch_attn, v_w_out, v_ln1_g, v_ln1_b, v_w_ffn_in, v_w_ffn_out, v_ln2_g, v_ln2_b):
    given = dict(locals())
    weights = {n: given[n] for n in _WEIGHTS}
    mom1 = {n: given["m_" + n] for n in _WEIGHTS}
    mom2 = {n: given["v_" + n] for n in _WEIGHTS}
    shard2d = lambda a: a.reshape(a.shape[-2:])

    w_in_shard, *late_shards = _cast_shards([shard2d(weights[n]) for n in ("w_in",) + _LATE])
    packed, grad_x, big, win_flight = _device_step(
        x[0], loss_target[0], meta_tokens, ln_emb_g.reshape(1, -1), ln_emb_b.reshape(1, -1), w_in_shard,
        hg_lower_bounds, hg_norm_g, attn_sinks, late_shards, ln1_g, ln1_b, ln2_g, ln2_b)

    place = _place()
    out = {}

    def reduce_adamw(n, parts, recv):
        own = _slot(place, n in _SWAPPED).astype(jnp.int32).reshape(1)
        res = _reduce_adamw(parts, recv, own, shard2d(weights[n]), shard2d(mom1[n]), shard2d(mom2[n]), "adamw_" + n)
        out[n] = [r.reshape(weights[n].shape) for r in res]

    for n, (parts, recv) in big.items():
        reduce_adamw(n, parts, recv)

    all_small, = _all_gather([packed], [F32], "gather_small")
    as_2d = lambda d: {n: d[n].reshape(shape) for n, shape in _SMALL}
    small_out, meta_whole, loss_row = _small_reduce_adamw(all_small, as_2d(weights), as_2d(mom1), as_2d(mom2))
    for n, res in small_out.items():
        out[n] = [r.reshape(weights[n].shape) for r in res]
    loss = loss_row[0, 0]
    g_meta_mine = lax.dynamic_index_in_dim(meta_whole.reshape(N_META, N_DEV, D_MODEL // N_DEV), _slot(place, False),
                                           axis=1, keepdims=False)
    out["meta_tokens"] = [g_meta_mine, *_adamw_plain(meta_tokens, g_meta_mine, m_meta_tokens, v_meta_tokens,
                                                     "adamw_meta")]

    reduce_adamw("w_in", *_exchange_wait(*win_flight, after=all_small, name="w_in_grads_wait"))

    return (loss, grad_x[None], *[out[n][0] for n in _WEIGHTS], *[out[n][1] for n in _WEIGHTS],
            *[out[n][2] for n in _WEIGHTS], *[out[n][3] for n in _WEIGHTS])
```

```python
import functools

import numpy as np
import jax
import jax.numpy as jnp
from jax import lax
from jax.experimental import pallas as pl
from jax.experimental.pallas import tpu as pltpu

F32 = jnp.float32
BF16 = jnp.bfloat16

D_MODEL = 1024
N_META = 16
BLOCK = 128
PAD = BLOCK - N_META
HG_HEADS = 4
HG_W = 512
ATT_HEADS = 8
HEAD_DIM = 64
ATT_QW = 512
ATT_KVW = 128
D_FF = 2816
EPS = 1e-5
ALPHA = 2.0 ** 0.25
ROPE_THETA = 10000.0
N_DEV = 8

ADAM_LR = 0.001
ADAM_B1 = 0.9
ADAM_B2 = 0.999
ADAM_EPS = 1e-08
ADAM_WD = 0.01
ADAM_STEP = 10

VMEM_LIMIT_BYTES = 56 * 1024 * 1024
MESH = pl.DeviceIdType.MESH

_LEVELS = (64, 32, 16, 8, 4, 2, 1)


def _cparams(sem):
    return pltpu.CompilerParams(dimension_semantics=sem, vmem_limit_bytes=VMEM_LIMIT_BYTES)


def _row_tile(rows, target):
    nb = rows // BLOCK
    best = 1
    for d in range(1, nb + 1):
        if nb % d == 0 and d * BLOCK <= target:
            best = d
    return best * BLOCK


_DN = {"nn": (((1,), (0,)), ((), ())), "nt": (((1,), (1,)), ((), ())), "tn": (((0,), (0,)), ((), ()))}


def _dot(a, b, form):
    return lax.dot_general(a.astype(BF16), b.astype(BF16), _DN[form], preferred_element_type=F32)


@functools.partial(jax.custom_vjp, nondiff_argnums=(2,))
def _mm(a, b, form):
    return _dot(a, b, form)


def _mm_fwd(a, b, form):
    a, b = a.astype(BF16), b.astype(BF16)
    return _dot(a, b, form), (a, b)


def _mm_bwd(form, res, g):
    a, b = res
    if form == "nn":
        return _dot(g, b, "nt"), _dot(a, g, "tn")
    if form == "nt":
        return _dot(g, b, "nn"), _dot(g, a, "tn")
    return _dot(b, g, "nt"), _dot(a, g, "nn")


_mm.defvjp(_mm_fwd, _mm_bwd)


def _split_dot(lv, x, form):
    return lax.dot_general(lv, x.astype(BF16), _DN[form], preferred_element_type=F32)


@jax.custom_vjp
def _swap_halves(x):
    return pltpu.roll(x, 64, 1)


_swap_halves.defvjp(lambda x: (pltpu.roll(x, 64, 1), None), lambda _, g: (pltpu.roll(g, 64, 1),))


def _tiled_matmul_tn(a, b, *, tm, tk, tn, out_dtype, name):
    m, k = a.shape
    n = b.shape[1]
    assert m % tm == 0 and k % tk == 0 and n % tn == 0, (name, a.shape, b.shape, tm, tk, tn)
    nm = m // tm

    def body(a_ref, b_ref, o_ref, acc_ref):
        mi = pl.program_id(2)

        @pl.when(mi == 0)
        def _():
            acc_ref[...] = jnp.zeros_like(acc_ref)

        acc_ref[...] += _dot(a_ref[...], b_ref[...], "tn")

        @pl.when(mi == nm - 1)
        def _():
            o_ref[...] = acc_ref[...].astype(out_dtype)

    return pl.pallas_call(
        body, name=name, grid=(k // tk, n // tn, nm),
        in_specs=[pl.BlockSpec((tm, tk), lambda kk, j, i: (i, kk)), pl.BlockSpec((tm, tn), lambda kk, j, i: (i, j))],
        out_specs=pl.BlockSpec((tk, tn), lambda kk, j, i: (kk, j)),
        out_shape=jax.ShapeDtypeStruct((k, n), out_dtype),
        scratch_shapes=[pltpu.VMEM((tk, tn), F32)],
        compiler_params=_cparams(("arbitrary", "arbitrary", "arbitrary")),
    )(a, b)


def _weight_grad_t(cots, h, *, tk, name):
    p, d = h.shape
    steps = [c.shape[1] // tk for c in cots]
    assert all(c.shape == (p, n * tk) for c, n in zip(cots, steps)), (name, [c.shape for c in cots], tk)
    first = [sum(steps[:i]) for i in range(len(cots))]

    def body(*refs):
        h_ref, o_ref = refs[len(cots)], refs[len(cots) + 1]
        k = pl.program_id(0)
        for c_ref, lo, n in zip(refs, first, steps):
            @pl.when((k >= lo) & (k < lo + n))
            def _(c_ref=c_ref):
                o_ref[...] = _dot(c_ref[...], h_ref[...], "tn").astype(BF16)

    cot_spec = lambda lo, n: pl.BlockSpec((p, tk), lambda k: (0, jnp.clip(k - lo, 0, n - 1)))
    return pl.pallas_call(
        body, name=name, grid=(sum(steps),),
        in_specs=[cot_spec(lo, n) for lo, n in zip(first, steps)]
                 + [pl.BlockSpec((p, d), lambda k: (0, 0), pipeline_mode=pl.Buffered(1))],
        out_specs=pl.BlockSpec((tk, d), lambda k: (k, 0)),
        out_shape=jax.ShapeDtypeStruct((sum(steps) * tk, d), BF16),
        compiler_params=_cparams(("arbitrary",)),
    )(*cots, h)


def _ln_stats(r):
    mu = jnp.mean(r, axis=-1, keepdims=True)
    xc = r - mu
    var = jnp.mean(xc * xc, axis=-1, keepdims=True)
    rstd = lax.rsqrt(var + EPS)
    return xc * rstd, rstd


def _ln_bwd(dy, xhat, rstd, g):
    dxhat = dy * g
    m1 = jnp.mean(dxhat, axis=-1, keepdims=True)
    m2 = jnp.mean(dxhat * xhat, axis=-1, keepdims=True)
    dr = rstd * (dxhat - m1 - xhat * m2)
    return dr, jnp.sum(dy * xhat, axis=0, keepdims=True), jnp.sum(dy, axis=0, keepdims=True)


N_SEG = 3 + len(_LEVELS)


def _level_stack():
    t = np.arange(BLOCK)[:, None]
    r = np.arange(BLOCK)[None, :]
    mats = [r <= t, r > t, np.ones((BLOCK, BLOCK), bool)]
    for h in _LEVELS:
        same = (t // (2 * h)) == (r // (2 * h))
        up_t, up_r = (t % (2 * h)) >= h, (r % (2 * h)) >= h
        mats.append(same & ((up_t & up_r & (r <= t)) | (~up_t & ~up_r & (r > t))))
    return jnp.asarray(np.concatenate(mats, axis=0).astype(np.float32), dtype=BF16)


def _hgrn_gates(hf, a0, a1, valid):
    lb = jax.nn.sigmoid(a0 - a1)
    fg = lb + (1.0 - lb) * jax.nn.sigmoid(hf)
    return jnp.where(valid, jnp.log(fg), 0.0), jnp.where(valid, 1.0 - fg, 0.0)


def _hgrn_scores(hq, k, *levels):
    q = jax.nn.silu(hq)
    rows = lax.broadcasted_iota(jnp.int32, (BLOCK, BLOCK), 0)
    cols = lax.broadcasted_iota(jnp.int32, (BLOCK, BLOCK), 1)
    a = jnp.where(rows == cols, jnp.sum(q * k, axis=-1, keepdims=True), 0.0)
    differ = jnp.bitwise_xor(rows, cols)
    for h, lvl in zip(_LEVELS, levels):
        decay = jnp.exp(lvl)
        pair = (cols < rows) & (differ >= h) & (differ < 2 * h)
        a = a + jnp.where(pair, _mm(q * decay, k * decay, "nt"), 0.0)
    return a


def _hgrn_mix(hq, k, v, st_in, a, seg_incl, seg_after, seg_total):
    o = _mm(jax.nn.silu(hq) * jnp.exp(seg_incl), st_in, "nt") + _mm(a, v, "nn")
    return o, st_in * jnp.exp(seg_total) + _mm(v, k * jnp.exp(seg_after), "tn")


def _hgrn_norm(o, hg, ng):
    return o * lax.rsqrt(jnp.mean(o * o, axis=-1, keepdims=True) + EPS) * ng * jax.nn.silu(hg)


def _seg_blocks(e, h):
    return [e[i * BLOCK:(i + 1) * BLOCK, h * BLOCK:(h + 1) * BLOCK] for i in range(N_SEG)]


def _rope(x, cos, sin, first_half):
    partner = jnp.where(first_half, -pltpu.roll(x, 96, 1), pltpu.roll(x, 32, 1))
    return x * cos + partner * sin


def _rope_t(g, cos, sin, first_half):
    u = g * sin
    partner = jnp.where(first_half, pltpu.roll(u, 96, 1), -pltpu.roll(u, 32, 1))
    return g * cos + partner


def _low_half(x):
    return lax.broadcasted_iota(jnp.int32, x.shape, 1) < HEAD_DIM


def _both_halves(x, g):
    sw = _swap_halves(x)
    return jnp.where(_low_half(x), x, sw) if g == 0 else jnp.where(_low_half(x), sw, x)


def _att_scores(qa, qb, kc, kp, km, g, own4, band4, meta4):
    low = _low_half(qa)
    q4 = jnp.concatenate([jnp.where(low, qa, 0.0), jnp.where(low, 0.0, qa),
                          jnp.where(low, qb, 0.0), jnp.where(low, 0.0, qb)], axis=0)
    scale = HEAD_DIM ** -0.5
    neg = jnp.finfo(F32).min
    s = jnp.where(own4, _mm(_both_halves(kc, g), q4, "nt"), _mm(_both_halves(kp, g), q4, "nt"))
    return (jnp.where(band4, s * scale, neg), jnp.where(meta4, _mm(_both_halves(km, g), q4, "nt") * scale, neg))


def _att_probs(s, sm, sinkrow):
    mx = jnp.maximum(jnp.maximum(jnp.max(s, axis=0, keepdims=True), jnp.max(sm, axis=0, keepdims=True)), sinkrow)
    p, pm, ps = jnp.exp(s - mx), jnp.exp(sm - mx), jnp.exp(sinkrow - mx)
    inv = 1.0 / (jnp.sum(p, axis=0, keepdims=True) + jnp.sum(pm, axis=0, keepdims=True) + ps)
    return p * inv, pm * inv, ps * inv


def _att_probs_bwd(p, pm, ps, dp, dpm):
    r = jnp.sum(p * dp, axis=0, keepdims=True) + jnp.sum(pm * dpm, axis=0, keepdims=True)
    return p * (dp - r), pm * (dpm - r), -ps * r


def _att_values(p, pm, vc, vp, vm, g, own4):
    o4 = (_mm(jnp.where(own4, p, 0.0), _both_halves(vc, g), "tn") + _mm(jnp.where(own4, 0.0, p), _both_halves(vp, g), "tn")
          + _mm(pm, _both_halves(vm, g), "tn"))
    tiles = []
    for j in range(2):
        upper = o4[(2 * j) * BLOCK:(2 * j + 1) * BLOCK]
        tiles.append(jnp.where(_low_half(upper), upper, o4[(2 * j + 1) * BLOCK:(2 * j + 2) * BLOCK]))
    return tiles


def _att_masks(blk_idx):
    kidx = lax.broadcasted_iota(jnp.int32, (BLOCK, BLOCK), 0)
    qrow = lax.broadcasted_iota(jnp.int32, (BLOCK, BLOCK), 1)
    own_side = kidx <= qrow
    pos_own = blk_idx * BLOCK + kidx - PAD
    ok_band = (own_side & (pos_own >= N_META)) | (~own_side & (pos_own - BLOCK >= N_META) & (blk_idx >= 1))
    qpos = blk_idx * BLOCK + lax.broadcasted_iota(jnp.int32, (N_META, BLOCK), 1) - PAD
    ok_meta = lax.broadcasted_iota(jnp.int32, (N_META, BLOCK), 0) <= qpos
    return [jnp.concatenate([m] * 4, axis=1) for m in (own_side, ok_band, ok_meta)]


def _token_streams(tr, tile_of=lambda i: i):
    k = tr // BLOCK
    return [pl.BlockSpec((BLOCK, D_MODEL), lambda i, j=j: (jnp.maximum(k * tile_of(i) - 1 + j, 0), 0))
            for j in range(k)]


def _embed_ln(x, meta_shard, w_in_shard, g0, b0):
    p = x.shape[0] + BLOCK
    tr = _row_tile(p, 640)
    k = tr // BLOCK
    nt = p // tr
    tile_of = lambda s: (s + 1) % nt
    shards = [meta_shard, w_in_shard]
    c_in, c_out, c_shapes, c_sems = _comm_specs(shards, N_DEV)

    def body(*refs):
        g_ref, b_ref = refs[k:k + 2]
        h_ref, hb_ref, xh_ref, rs_ref = refs[k + 4:k + 8]
        out_refs = refs[k + 8:k + 10]
        lead_ref, meta_ref = refs[k + 10:k + 12]
        starts, passes, waits = _gather_behind(refs[k + 2:k + 4], out_refs, refs[k + 12:], [False, False])
        s = pl.program_id(0)
        t = tile_of(s)

        @pl.when(s == 0)
        def _():
            lead_ref[...] = jnp.zeros_like(lead_ref)
            for start in starts:
                start()

        @pl.when(s == nt - 1)
        def _():
            for step in passes + waits:
                step()
            pltpu.sync_copy(out_refs[0], meta_ref)
            for d in range(N_DEV):
                lead_ref[PAD:BLOCK, d * BLOCK:(d + 1) * BLOCK] = meta_ref[d]

        first = jnp.where(t == 0, lead_ref[...], refs[0][...])
        xhat, rstd = _ln_stats(jnp.concatenate([first] + [r[...] for r in refs[1:k]], axis=0))
        row = t * tr + lax.broadcasted_iota(jnp.int32, (tr, 1), 0)
        h = jnp.where(row >= PAD, xhat * g_ref[...] + b_ref[...], 0.0)
        h_ref[...] = h
        hb_ref[...] = h.astype(BF16)
        xh_ref[...] = xhat
        rs_ref[...] = rstd

    vec = pl.BlockSpec((1, D_MODEL), lambda s: (0, 0))
    rowsp = pl.BlockSpec((tr, D_MODEL), lambda s: (tile_of(s), 0))
    return pl.pallas_call(
        body, name="embed_ln", grid=(nt,),
        in_specs=_token_streams(tr, tile_of) + [vec, vec] + c_in,
        out_specs=[rowsp, rowsp, rowsp, pl.BlockSpec((tr, 1), lambda s: (tile_of(s), 0))] + c_out,
        out_shape=[jax.ShapeDtypeStruct((p, D_MODEL), F32), jax.ShapeDtypeStruct((p, D_MODEL), BF16),
                   jax.ShapeDtypeStruct((p, D_MODEL), F32), jax.ShapeDtypeStruct((p, 1), F32)] + c_shapes,
        scratch_shapes=[pltpu.VMEM((BLOCK, D_MODEL), F32), pltpu.VMEM((N_DEV, N_META, BLOCK), F32)] + c_sems,
        compiler_params=_cparams(("arbitrary",)),
    )(*([x] * k), g0, b0, *shards)


def _rope_tables(p):
    pos = (np.arange(p, dtype=np.int32) - PAD).astype(np.float32)
    half = HEAD_DIM // 2
    inv = np.float32(ROPE_THETA) ** (-np.arange(half, dtype=np.float32) / np.float32(half))
    ang = pos[:, None] * np.tile(inv.astype(np.float32), BLOCK // half)[None, :]
    return jnp.asarray(np.cos(ang), F32), jnp.asarray(np.sin(ang), F32)


def _att_sinkrows(sink_ref):
    lanehead = lax.broadcasted_iota(jnp.int32, (1, 4 * BLOCK), 1) // BLOCK
    rows = []
    for g in range(2):
        row = jnp.zeros((1, 4 * BLOCK), F32)
        for j in range(4):
            row = jnp.where(lanehead == j, sink_ref[0, 4 * g + j], row)
        rows.append(row)
    return rows


def _first_half(rows):
    return (lax.broadcasted_iota(jnp.int32, (rows, BLOCK), 1) % HEAD_DIM) < (HEAD_DIM // 2)


def _att_load(qkv_ref, cos_ref, sin_ref, with_q):
    cos, sin, fh = cos_ref[...], sin_ref[...], _first_half(BLOCK)
    qs = [_rope(qkv_ref[:, j * BLOCK:(j + 1) * BLOCK], cos, sin, fh) for j in range(4)] if with_q else None
    k = _rope(qkv_ref[:, ATT_QW:ATT_QW + ATT_KVW], cos, sin, fh)
    v = qkv_ref[:, ATT_QW + ATT_KVW:ATT_QW + 2 * ATT_KVW]
    return qs, k, v


def _att_load_meta(qkv_ref, cos_ref, sin_ref):
    k = _rope(qkv_ref[PAD:BLOCK, ATT_QW:ATT_QW + ATT_KVW], cos_ref[PAD:BLOCK, :], sin_ref[PAD:BLOCK, :],
              _first_half(N_META))
    return k, qkv_ref[PAD:BLOCK, ATT_QW + ATT_KVW:ATT_QW + 2 * ATT_KVW]


def _att_specs(blk):
    w = ATT_QW + 2 * ATT_KVW
    cur = lambda width: pl.BlockSpec((BLOCK, width), lambda i: (blk(i), 0))
    prev = lambda width: pl.BlockSpec((BLOCK, width), lambda i: (jnp.maximum(blk(i) - 1, 0), 0))
    meta = lambda width: pl.BlockSpec((BLOCK, width), lambda i: (0, 0))
    return [cur(w), prev(w), meta(w), cur(BLOCK), cur(BLOCK), prev(BLOCK), prev(BLOCK), meta(BLOCK), meta(BLOCK),
            pl.BlockSpec(memory_space=pltpu.SMEM)]


_FLIPS = [(dx, dy, dc) for dx in (0, 1) for dy in (0, 1) for dc in (0, 1)][1:]
N_PEERS = len(_FLIPS)


def _place():
    return lax.axis_index("x"), lax.axis_index("y"), lax.axis_index("c")


def _peer(place, flip):
    return tuple(1 - p if f else p for p, f in zip(place, flip))


def _slot(place, swapped):
    x, y, c = place
    return 4 * y + 2 * x + c if swapped else 4 * x + 2 * y + c


def _comm_specs(arrs, out_lead):
    n = len(arrs)
    outs = [jax.ShapeDtypeStruct((out_lead,) + a.shape[-2:], a.dtype) for a in arrs]
    sems = [pltpu.SemaphoreType.DMA((n, N_PEERS)), pltpu.SemaphoreType.DMA((n, N_PEERS)), pltpu.SemaphoreType.DMA((n,))]
    return [pl.BlockSpec(memory_space=pl.ANY)] * n, [pl.BlockSpec(memory_space=pl.ANY)] * n, outs, sems


def _gather_behind(shard_refs, out_refs, sems, swapped):
    send_sems, recv_sems, local_sems = sems
    x, y, c = _place()
    me, sibling = (x, y, c), (x, y, 1 - c)
    chips = [(1 - x, y), (x, 1 - y), (1 - x, 1 - y)]
    starts, passes, waits = [], [], []
    for w, (s, o) in enumerate(zip(shard_refs, out_refs)):
        def copy(k, block, to, from_shard=False, w=w, s=s, o=o):
            rows = o.at[_slot(block, swapped[w])]
            return pltpu.make_async_remote_copy(
                src_ref=s if from_shard else rows, dst_ref=rows, send_sem=send_sems.at[w, k],
                recv_sem=recv_sems.at[w, k], device_id=to, device_id_type=MESH)

        own = pltpu.make_async_copy(s, o.at[_slot(me, swapped[w])], local_sems.at[w])
        first = [copy(0, me, sibling, True)] + [copy(1 + j, me, (*chip, c), True) for j, chip in enumerate(chips)]
        handed = [copy(4 + j, (*chip, c), sibling) for j, chip in enumerate(chips)]
        starts += [own.start] + [cp.start for cp in first]
        for j, chip in enumerate(chips):
            passes += [copy(1 + j, (*chip, c), me).wait_recv, handed[j].start]
        waits.append(copy(0, sibling, me).wait_recv)
        waits += [copy(4 + j, (*chip, 1 - c), me).wait_recv for j, chip in enumerate(chips)]
        waits += [cp.wait_send for cp in first + handed] + [own.wait]
    return starts, passes, waits


def _scatter_behind(part_refs, recv_refs, sems, swapped):
    send_sems, recv_sems, _ = sems
    place = _place()
    starts, waits = [], []
    for w, (p, o) in enumerate(zip(part_refs, recv_refs)):
        for r, flip in enumerate(_FLIPS):
            peer = _peer(place, flip)
            cp = pltpu.make_async_remote_copy(
                src_ref=p.at[_slot(peer, swapped[w])], dst_ref=o.at[r], send_sem=send_sems.at[w, r],
                recv_sem=recv_sems.at[w, r], device_id=peer, device_id_type=MESH)
            starts.append(cp.start)
            waits += [cp.wait_recv, cp.wait_send]
    return starts, waits


def _mixers_fwd(proj_hg, proj_att, lbounds, norm_g, lv, cos, sin, sinks, shards, swapped):
    p = proj_hg.shape[0]
    nb = p // BLOCK
    n = len(shards)
    c_in, c_out, c_shapes, c_sems = _comm_specs(shards, N_DEV)
    pass_step = min(nb - 1, max(1, (5 * nb) // 8))

    def body(*refs):
        x_ref, lb_ref, ng_ref, lv_ref, cur_ref, prev_ref, meta_ref, cc, sc, cp, sp, cm, sm, sink_ref = refs[:14]
        shard_refs = refs[14:14 + n]
        y_ref, o_ref, st_ref, a_ref, raw_ref, pr_ref = refs[14 + n:20 + n]
        out_refs = refs[20 + n:20 + 2 * n]
        carry_ref = refs[20 + 2 * n]
        starts, passes, waits = _gather_behind(shard_refs, out_refs, refs[21 + 2 * n:], swapped)
        c = pl.program_id(0)

        @pl.when(c == 0)
        def _():
            carry_ref[...] = jnp.zeros_like(carry_ref)
            for start in starts:
                start()

        @pl.when(c == pass_step)
        def _():
            for step in passes:
                step()

        valid = (c * BLOCK + lax.broadcasted_iota(jnp.int32, (BLOCK, 1), 0)) >= PAD
        logf, k = _hgrn_gates(x_ref[:, HG_W:2 * HG_W], lb_ref[0:1, :], lb_ref[1:2, :], valid)
        e = _split_dot(lv_ref[...], logf, "nn")
        for h in range(HG_HEADS):
            sl = lambda part: x_ref[:, part * HG_W + h * BLOCK: part * HG_W + (h + 1) * BLOCK]
            hs = slice(h * BLOCK, (h + 1) * BLOCK)
            st_in = carry_ref[h]
            st_ref[0, h] = st_in
            seg = _seg_blocks(e, h)
            a = _hgrn_scores(sl(0), k[:, hs], *seg[3:])
            a_ref[0, h] = a.astype(BF16)
            raw, st_out = _hgrn_mix(sl(0), k[:, hs], sl(2), st_in, a, *seg[:3])
            raw_ref[:, hs] = raw
            y_ref[:, hs] = _hgrn_norm(raw, sl(3), ng_ref[...]).astype(BF16)
            carry_ref[h] = st_out

        qs, kc, vc = _att_load(cur_ref, cc, sc, True)
        _, kp, vp = _att_load(prev_ref, cp, sp, False)
        km, vm = _att_load_meta(meta_ref, cm, sm)
        sinkrows = _att_sinkrows(sink_ref)
        own4, band4, meta4 = _att_masks(c)
        for g in range(2):
            s, s_meta = _att_scores(qs[2 * g], qs[2 * g + 1], kc, kp, km, g, own4, band4, meta4)
            pr, pr_meta, pr_sink = _att_probs(s, s_meta, sinkrows[g])
            pr_ref[0, g, :BLOCK, :] = pr.astype(BF16)
            pr_ref[0, g, BLOCK:BLOCK + N_META, :] = pr_meta.astype(BF16)
            pr_ref[0, g, BLOCK + N_META:, :] = jnp.broadcast_to(pr_sink, (N_META, 4 * BLOCK)).astype(BF16)
            for j, tile in enumerate(_att_values(pr, pr_meta, vc, vp, vm, g, own4)):
                o_ref[:, (2 * g + j) * BLOCK:(2 * g + j + 1) * BLOCK] = tile.astype(BF16)

        @pl.when(c == nb - 1)
        def _():
            for wait in waits:
                wait()

    return pl.pallas_call(
        body, name="mixers_fwd", grid=(nb,),
        in_specs=[pl.BlockSpec((BLOCK, 4 * HG_W), lambda c: (c, 0)), pl.BlockSpec((2, HG_W), lambda c: (0, 0)),
                  pl.BlockSpec((1, BLOCK), lambda c: (0, 0)), pl.BlockSpec(lv.shape, lambda c: (0, 0))]
        + _att_specs(lambda c: c) + c_in,
        out_specs=[pl.BlockSpec((BLOCK, HG_W), lambda c: (c, 0)), pl.BlockSpec((BLOCK, ATT_QW), lambda c: (c, 0)),
                   pl.BlockSpec((1, HG_HEADS, BLOCK, BLOCK), lambda c: (c, 0, 0, 0)),
                   pl.BlockSpec((1, HG_HEADS, BLOCK, BLOCK), lambda c: (c, 0, 0, 0)),
                   pl.BlockSpec((BLOCK, HG_W), lambda c: (c, 0)),
                   pl.BlockSpec((1, 2, ATT_KEYS, 4 * BLOCK), lambda c: (c, 0, 0, 0))] + c_out,
        out_shape=[jax.ShapeDtypeStruct((p, HG_W), BF16), jax.ShapeDtypeStruct((p, ATT_QW), BF16),
                   jax.ShapeDtypeStruct((nb, HG_HEADS, BLOCK, BLOCK), F32),
                   jax.ShapeDtypeStruct((nb, HG_HEADS, BLOCK, BLOCK), BF16),
                   jax.ShapeDtypeStruct((p, HG_W), F32),
                   jax.ShapeDtypeStruct((nb, 2, ATT_KEYS, 4 * BLOCK), BF16)] + c_shapes,
        scratch_shapes=[pltpu.VMEM((HG_HEADS, BLOCK, BLOCK), F32)] + c_sems,
        compiler_params=_cparams(("arbitrary",)),
    )(proj_hg, lbounds, norm_g, lv, proj_att, proj_att, proj_att, cos, sin, cos, sin, cos, sin, sinks, *shards)


def _tile(rows, preferred):
    return preferred if rows % preferred == 0 else _row_tile(rows, preferred)


def _in_proj(h0b, w_in_t):
    p = h0b.shape[0]
    tm = _row_tile(p, 640)
    hg_end = 4 * HG_W

    def body(h_ref, w_ref, hg_ref, att_ref, gates_ref):
        h = h_ref[...]
        hg_ref[...] = _dot(h, w_ref[:hg_end, :], "nt")
        att_ref[...] = _dot(h, w_ref[hg_end:MIX_W, :], "nt")
        gates_ref[...] = _dot(h, w_ref[MIX_W:, :], "nt").astype(BF16)

    row = lambda w: pl.BlockSpec((tm, w), lambda i: (i, 0))
    return pl.pallas_call(
        body, name="in_proj", grid=(p // tm,),
        in_specs=[row(D_MODEL), pl.BlockSpec(w_in_t.shape, lambda i: (0, 0), pipeline_mode=pl.Buffered(1))],
        out_specs=[row(hg_end), row(MIX_W - hg_end), row(2 * D_MODEL)],
        out_shape=[jax.ShapeDtypeStruct((p, hg_end), F32), jax.ShapeDtypeStruct((p, MIX_W - hg_end), F32),
                   jax.ShapeDtypeStruct((p, 2 * D_MODEL), BF16)],
        compiler_params=_cparams(("arbitrary",)),
    )(h0b, w_in_t)


def _branch_mix(yh, oa, gates, w_bh, w_ba):
    y_hg = _dot(yh, w_bh, "nn")
    y_att = _dot(oa, w_ba, "nn")
    s1 = jax.nn.sigmoid(gates[:, :D_MODEL].astype(F32))
    s2 = jax.nn.sigmoid(gates[:, D_MODEL:].astype(F32))
    return s1 * y_hg + s2 * y_att, y_hg, y_att, s1, s2


def _mix_out_ln1(yh, oa, gates, h0, w_bh, w_ba, w_out, g1, b1):
    p = yh.shape[0]
    tr = _tile(p, 320)

    def body(yh_ref, oa_ref, g_ref, h0_ref, wbh_ref, wba_ref, wo_ref, g1_ref, b1_ref,
             mix_ref, h1_ref, h1b_ref, xh_ref, rs_ref):
        mixin = _branch_mix(yh_ref[...], oa_ref[...], g_ref[...], wbh_ref[...], wba_ref[...])[0]
        mix_ref[...] = mixin.astype(BF16)
        xhat, rstd = _ln_stats(ALPHA * h0_ref[...] + _dot(mixin, wo_ref[...], "nn"))
        h1 = xhat * g1_ref[...] + b1_ref[...]
        h1_ref[...] = h1
        h1b_ref[...] = h1.astype(BF16)
        xh_ref[...] = xhat
        rs_ref[...] = rstd

    row = lambda w: pl.BlockSpec((tr, w), lambda i: (i, 0))
    const = lambda a: pl.BlockSpec(a.shape, lambda i: (0, 0))
    return pl.pallas_call(
        body, name="mix_out_ln1", grid=(p // tr,),
        in_specs=[row(HG_W), row(ATT_QW), row(2 * D_MODEL), row(D_MODEL), const(w_bh), const(w_ba), const(w_out),
                  const(g1), const(b1)],
        out_specs=[row(D_MODEL), row(D_MODEL), row(D_MODEL), row(D_MODEL), row(1)],
        out_shape=[jax.ShapeDtypeStruct((p, D_MODEL), BF16), jax.ShapeDtypeStruct((p, D_MODEL), F32),
                   jax.ShapeDtypeStruct((p, D_MODEL), BF16), jax.ShapeDtypeStruct((p, D_MODEL), F32),
                   jax.ShapeDtypeStruct((p, 1), F32)],
        compiler_params=_cparams(("arbitrary",)),
    )(yh, oa, gates, h0, w_bh, w_ba, w_out, g1, b1)


FF_T = D_FF // 2


def _ffn_in_swiglu(h1, w_fi_t):
    p = h1.shape[0]
    tm = _row_tile(p, 640)

    def body(h_ref, w_ref, au_ref, s_ref):
        au = _dot(h_ref[...], w_ref[...], "nt")
        au_ref[...] = au.astype(BF16)
        s_ref[...] = (jax.nn.silu(au[:, :FF_T]) * au[:, FF_T:]).astype(BF16)

    return pl.pallas_call(
        body, name="ffn_in_swiglu", grid=(D_FF // FF_T, p // tm),
        in_specs=[pl.BlockSpec((tm, D_MODEL), lambda j, i: (i, 0)), pl.BlockSpec((2 * FF_T, D_MODEL), lambda j, i: (j, 0))],
        out_specs=[pl.BlockSpec((tm, 2 * FF_T), lambda j, i: (i, j)), pl.BlockSpec((tm, FF_T), lambda j, i: (i, j))],
        out_shape=[jax.ShapeDtypeStruct((p, 2 * D_FF), BF16), jax.ShapeDtypeStruct((p, D_FF), BF16)],
        compiler_params=_cparams(("arbitrary", "arbitrary")),
    )(h1, w_fi_t)


def _ffn_out_loss(s, w_fo, h1, g2, b2, target):
    p = h1.shape[0]
    tr = _row_tile(p, 640)
    k = tr // BLOCK

    def body(*refs):
        s_ref, w_ref, h_ref, g_ref, b_ref = refs[:5]
        dr_ref, drb_ref, loss_ref, dg_ref, db_ref = refs[5 + k:]
        i = pl.program_id(0)
        xhat, rstd = _ln_stats(ALPHA * h_ref[...] + _dot(s_ref[...], w_ref[...], "nn"))
        y = xhat * g_ref[...] + b_ref[...]
        row = i * tr + lax.broadcasted_iota(jnp.int32, (tr, 1), 0)
        tgt = jnp.concatenate([r[...] for r in refs[5:5 + k]], axis=0)
        err = jnp.where(row >= BLOCK, y - tgt, 0.0)
        dr, dg, db = _ln_bwd(err * (1.0 / D_MODEL), xhat, rstd, g_ref[...])
        dr_ref[...] = dr
        drb_ref[...] = dr.astype(BF16)
        e2 = jnp.sum(err * err, axis=0, keepdims=True)
        part = e2[:, 0:BLOCK]
        for j in range(1, D_MODEL // BLOCK):
            part = part + e2[:, j * BLOCK:(j + 1) * BLOCK]
        part = part * (0.5 / D_MODEL)

        @pl.when(i == 0)
        def _():
            loss_ref[...] = part
            dg_ref[...] = dg
            db_ref[...] = db

        @pl.when(i > 0)
        def _():
            loss_ref[...] += part
            dg_ref[...] += dg
            db_ref[...] += db

    vec = pl.BlockSpec((1, D_MODEL), lambda i: (0, 0))
    rowsp = pl.BlockSpec((tr, D_MODEL), lambda i: (i, 0))
    return pl.pallas_call(
        body, name="ffn_out_loss", grid=(p // tr,),
        in_specs=[pl.BlockSpec((tr, D_FF), lambda i: (i, 0)), pl.BlockSpec((D_FF, D_MODEL), lambda i: (0, 0)),
                  rowsp, vec, vec] + _token_streams(tr),
        out_specs=[rowsp, rowsp, pl.BlockSpec((1, BLOCK), lambda i: (0, 0)), vec, vec],
        out_shape=[jax.ShapeDtypeStruct((p, D_MODEL), F32), jax.ShapeDtypeStruct((p, D_MODEL), BF16),
                   jax.ShapeDtypeStruct((1, BLOCK), F32), jax.ShapeDtypeStruct((1, D_MODEL), F32),
                   jax.ShapeDtypeStruct((1, D_MODEL), F32)],
        compiler_params=_cparams(("arbitrary",)),
    )(s, w_fo, h1, g2, b2, *([target] * k))


def _ffn_bwd(dr2, w_fo, au, w_fi_t):
    p = au.shape[0]
    tm = _tile(p, 320)

    def body(d_ref, wo_ref, au_ref, wi_ref, dau_ref, dh_ref):
        d = d_ref[...]
        dh = 0.0
        for j in range(D_FF // FF_T):
            a_cols = slice(2 * j * FF_T, (2 * j + 1) * FF_T)
            u_cols = slice((2 * j + 1) * FF_T, (2 * j + 2) * FF_T)
            ds = _dot(d, wo_ref[j * FF_T:(j + 1) * FF_T, :], "nt")
            _, vjp = jax.vjp(lambda a, u: jax.nn.silu(a) * u, au_ref[:, a_cols].astype(F32), au_ref[:, u_cols].astype(F32))
            da, du = vjp(ds)
            da, du = da.astype(BF16), du.astype(BF16)
            dau_ref[:, a_cols] = da
            dau_ref[:, u_cols] = du
            dh = dh + _dot(da, wi_ref[a_cols, :], "nn") + _dot(du, wi_ref[u_cols, :], "nn")
        dh_ref[...] = dh

    row = lambda w: pl.BlockSpec((tm, w), lambda i: (i, 0))
    kept = lambda a: pl.BlockSpec(a.shape, lambda i: (0, 0), pipeline_mode=pl.Buffered(1))
    return pl.pallas_call(
        body, name="ffn_bwd", grid=(p // tm,),
        in_specs=[row(D_MODEL), kept(w_fo), row(2 * D_FF), kept(w_fi_t)],
        out_specs=[row(2 * D_FF), row(D_MODEL)],
        out_shape=[jax.ShapeDtypeStruct((p, 2 * D_FF), BF16), jax.ShapeDtypeStruct((p, D_MODEL), F32)],
        compiler_params=_cparams(("arbitrary",)),
    )(dr2, w_fo, au, w_fi_t)


def _ln1_mix_bwd(dr2, dh1_ffn, xhat1, rstd1, g1, yh, oa, gates, w_bh, w_ba, w_out):
    p = yh.shape[0]
    tr = _tile(p, 320)

    def body(a_ref, b_ref, xh_ref, rs_ref, g1_ref, yh_ref, oa_ref, g_ref, wbh_ref, wba_ref, wo_ref,
             dr_ref, dyhg_ref, dyat_ref, dgt_ref, dyh_ref, doa_ref, dg_ref, db_ref):
        i = pl.program_id(0)
        dr, dg, db = _ln_bwd(ALPHA * a_ref[...] + b_ref[...], xh_ref[...], rs_ref[...], g1_ref[...])
        dr_ref[...] = dr
        d = _dot(dr, wo_ref[...], "nt")
        _, y_hg, y_att, s1, s2 = _branch_mix(yh_ref[...], oa_ref[...], g_ref[...], wbh_ref[...], wba_ref[...])
        dy_hg = d * s1
        dy_att = d * s2
        dyhg_ref[...] = dy_hg.astype(BF16)
        dyat_ref[...] = dy_att.astype(BF16)
        dgt_ref[:, :D_MODEL] = (d * y_hg * s1 * (1.0 - s1)).astype(BF16)
        dgt_ref[:, D_MODEL:] = (d * y_att * s2 * (1.0 - s2)).astype(BF16)
        dyh_ref[...] = _dot(dy_hg, wbh_ref[...], "nt")
        doa_ref[...] = _dot(dy_att, wba_ref[...], "nt")

        @pl.when(i == 0)
        def _():
            dg_ref[...] = dg
            db_ref[...] = db

        @pl.when(i > 0)
        def _():
            dg_ref[...] += dg
            db_ref[...] += db

    row = lambda w: pl.BlockSpec((tr, w), lambda i: (i, 0))
    const = lambda a: pl.BlockSpec(a.shape, lambda i: (0, 0))
    vec = pl.BlockSpec((1, D_MODEL), lambda i: (0, 0))
    return pl.pallas_call(
        body, name="ln1_mix_bwd", grid=(p // tr,),
        in_specs=[row(D_MODEL), row(D_MODEL), row(D_MODEL), row(1), vec, row(HG_W), row(ATT_QW), row(2 * D_MODEL),
                  const(w_bh), const(w_ba), const(w_out)],
        out_specs=[row(D_MODEL), row(D_MODEL), row(D_MODEL), row(2 * D_MODEL), row(HG_W), row(ATT_QW), vec, vec],
        out_shape=[jax.ShapeDtypeStruct((p, D_MODEL), F32), jax.ShapeDtypeStruct((p, D_MODEL), BF16),
                   jax.ShapeDtypeStruct((p, D_MODEL), BF16), jax.ShapeDtypeStruct((p, 2 * D_MODEL), BF16),
                   jax.ShapeDtypeStruct((p, HG_W), F32), jax.ShapeDtypeStruct((p, ATT_QW), F32),
                   jax.ShapeDtypeStruct((1, D_MODEL), F32), jax.ShapeDtypeStruct((1, D_MODEL), F32)],
        compiler_params=_cparams(("arbitrary",)),
    )(dr2, dh1_ffn, xhat1, rstd1, g1, yh, oa, gates, w_bh, w_ba, w_out)


MIX_W = 4 * HG_W + ATT_QW + 2 * ATT_KVW
ATT_KEYS = BLOCK + 2 * N_META


def _mixers_bwd(proj_hg, proj_att, lbounds, norm_g, lv, states, scores, raw, probs, cos, sin, sinks, dyh, doa,
                parts, swapped):
    p = proj_hg.shape[0]
    nb = p // BLOCK
    n = len(parts)
    kvw = 2 * ATT_KVW
    rev = lambda s: nb - 1 - s
    c_in, c_out, c_shapes, c_sems = _comm_specs(parts, N_PEERS)

    def body(*refs):
        (x_ref, lb_ref, ng_ref, lv_ref, st_ref, a_ref, raw_ref, pr_ref, cur_ref, prev_ref, meta_ref, cc, sc, cp, sp,
         cm, sm, sink_ref, dy_ref, do_ref) = refs[:20]
        part_refs = refs[20:20 + n]
        dx_ref, dlb_ref, dng_ref, dsink_ref = refs[20 + n:24 + n]
        recv_refs = refs[24 + n:24 + 2 * n]
        dcarry_ref, dkv_next_ref, dkv_meta_ref = refs[24 + 2 * n:27 + 2 * n]
        starts, waits = _scatter_behind(part_refs, recv_refs, refs[27 + 2 * n:], swapped)
        step = pl.program_id(0)
        c = rev(step)

        @pl.when(step == 0)
        def _():
            dcarry_ref[...] = jnp.zeros_like(dcarry_ref)
            dkv_next_ref[...] = jnp.zeros_like(dkv_next_ref)
            dkv_meta_ref[...] = jnp.zeros_like(dkv_meta_ref)
            dlb_ref[...] = jnp.zeros_like(dlb_ref)
            dng_ref[...] = jnp.zeros_like(dng_ref)
            dsink_ref[...] = jnp.zeros_like(dsink_ref)
            for start in starts:
                start()

        fh = _first_half(BLOCK)
        qs, kc, vc = _att_load(cur_ref, cc, sc, True)
        _, kp, vp = _att_load(prev_ref, cp, sp, False)
        km, vm = _att_load_meta(meta_ref, cm, sm)
        own4, band4, meta4 = _att_masks(c)
        att0 = 4 * HG_W
        dkm = dkp = dkc = dvm = dvp = dvc = 0.0
        dsinkrows = []
        for g in range(2):
            pr = pr_ref[0, g, :BLOCK, :].astype(F32)
            pr_meta = pr_ref[0, g, BLOCK:BLOCK + N_META, :].astype(F32)
            pr_sink = jnp.max(pr_ref[0, g, BLOCK + N_META:, :].astype(F32), axis=0, keepdims=True)
            _, values_vjp = jax.vjp(lambda *a, g=g: _att_values(*a, g, own4), pr, pr_meta, vc, vp, vm)
            dpr, dpr_meta, dvc_g, dvp_g, dvm_g = values_vjp(
                [do_ref[:, (2 * g + j) * BLOCK:(2 * g + j + 1) * BLOCK] for j in range(2)])
            ds, ds_meta, dsinkrow = _att_probs_bwd(pr, pr_meta, pr_sink, dpr, dpr_meta)
            _, scores_vjp = jax.vjp(lambda *a, g=g: _att_scores(*a, g, own4, band4, meta4),
                                    qs[2 * g], qs[2 * g + 1], kc, kp, km)
            dqa, dqb, dkc_g, dkp_g, dkm_g = scores_vjp((ds, ds_meta))
            for j, dq in enumerate((dqa, dqb)):
                dx_ref[:, att0 + (2 * g + j) * BLOCK:att0 + (2 * g + j + 1) * BLOCK] = _rope_t(
                    dq, cc[...], sc[...], fh).astype(BF16)
            dkm, dkp, dkc = dkm + dkm_g, dkp + dkp_g, dkc + dkc_g
            dvm, dvp, dvc = dvm + dvm_g, dvp + dvp_g, dvc + dvc_g
            dsinkrows.append(dsinkrow)
        ds0, ds1 = dsinkrows
        dkv_meta_ref[:, :BLOCK] += _rope_t(dkm, cm[PAD:BLOCK, :], sm[PAD:BLOCK, :], _first_half(N_META))
        dkv_meta_ref[:, BLOCK:] += dvm
        last = jnp.where(c == 0, 1.0, 0.0)
        to_meta_rows = lambda m: jnp.concatenate([jnp.zeros((PAD, BLOCK), F32), last * m], axis=0)
        dk = _rope_t(dkc, cc[...], sc[...], fh) + dkv_next_ref[:, :BLOCK] + to_meta_rows(dkv_meta_ref[:, :BLOCK])
        dv = dvc + dkv_next_ref[:, BLOCK:] + to_meta_rows(dkv_meta_ref[:, BLOCK:])
        dx_ref[:, att0 + ATT_QW:att0 + ATT_QW + ATT_KVW] = dk.astype(BF16)
        dx_ref[:, att0 + ATT_QW + ATT_KVW:] = dv.astype(BF16)
        dkv_next_ref[:, :BLOCK] = _rope_t(dkp, cp[...], sp[...], fh)
        dkv_next_ref[:, BLOCK:] = dvp
        sink_rows = []
        for dsg in (ds0, ds1):
            for j in range(4):
                tot = jnp.sum(dsg[:, j * BLOCK:(j + 1) * BLOCK], axis=1, keepdims=True)
                sink_rows.append(jnp.broadcast_to(tot, (1, BLOCK)))
        dsink_ref[...] += jnp.concatenate(sink_rows, axis=0)

        valid = (c * BLOCK + lax.broadcasted_iota(jnp.int32, (BLOCK, 1), 0)) >= PAD
        (logf, k), gates_vjp = jax.vjp(lambda hf, a0, a1: _hgrn_gates(hf, a0, a1, valid),
                                       x_ref[:, HG_W:2 * HG_W], lb_ref[0:1, :], lb_ref[1:2, :])
        lvv = lv_ref[...]
        e = _split_dot(lvv, logf, "nn")
        dng = jnp.zeros((1, BLOCK), F32)
        dk, dseg = [], []
        for h in range(HG_HEADS):
            sl = lambda part: x_ref[:, part * HG_W + h * BLOCK: part * HG_W + (h + 1) * BLOCK]
            hs = slice(h * BLOCK, (h + 1) * BLOCK)
            seg = _seg_blocks(e, h)
            _, norm_vjp = jax.vjp(_hgrn_norm, raw_ref[:, hs], sl(3), ng_ref[...])
            draw, dhg, dngh = norm_vjp(dy_ref[:, hs])
            _, mix_vjp = jax.vjp(_hgrn_mix, sl(0), k[:, hs], sl(2), st_ref[0, h], a_ref[0, h].astype(F32), *seg[:3])
            dhq, dkh, dhi, dst, da, *dseg_mix = mix_vjp((draw, dcarry_ref[h]))
            _, scores_vjp = jax.vjp(_hgrn_scores, sl(0), k[:, hs], *seg[3:])
            dhq2, dkh2, *dseg_lvl = scores_vjp(da)
            for part, val in ((0, dhq + dhq2), (2, dhi), (3, dhg)):
                dx_ref[:, part * HG_W + h * BLOCK: part * HG_W + (h + 1) * BLOCK] = val.astype(BF16)
            dk.append(dkh + dkh2)
            dseg.append(jnp.concatenate(dseg_mix + dseg_lvl, axis=0))
            dng = dng + dngh
            dcarry_ref[h] = dst
        dlogf = _split_dot(lvv, jnp.concatenate(dseg, axis=1), "tn")
        dhf, da0, da1 = gates_vjp((dlogf, jnp.concatenate(dk, axis=1)))
        dx_ref[:, HG_W:2 * HG_W] = dhf.astype(BF16)
        dlb_ref[0:1, :] += da0
        dlb_ref[1:2, :] += da1
        dng_ref[...] += dng

        @pl.when(step == nb - 1)
        def _():
            for wait in waits:
                wait()

    const = lambda shape: pl.BlockSpec(shape, lambda s: (0,) * len(shape))
    per_head = pl.BlockSpec((1, HG_HEADS, BLOCK, BLOCK), lambda s: (rev(s), 0, 0, 0))
    return pl.pallas_call(
        body, name="mixers_bwd", grid=(nb,),
        in_specs=[pl.BlockSpec((BLOCK, 4 * HG_W), lambda s: (rev(s), 0)), const((2, HG_W)), const((1, BLOCK)),
                  const(lv.shape), per_head, per_head, pl.BlockSpec((BLOCK, HG_W), lambda s: (rev(s), 0)),
                  pl.BlockSpec((1, 2, ATT_KEYS, 4 * BLOCK), lambda s: (rev(s), 0, 0, 0))]
        + _att_specs(rev)
        + [pl.BlockSpec((BLOCK, HG_W), lambda s: (rev(s), 0)), pl.BlockSpec((BLOCK, ATT_QW), lambda s: (rev(s), 0))]
        + c_in,
        out_specs=[pl.BlockSpec((BLOCK, MIX_W), lambda s: (rev(s), 0)), const((2, HG_W)), const((1, BLOCK)),
                   const((ATT_HEADS, BLOCK))] + c_out,
        out_shape=[jax.ShapeDtypeStruct((p, MIX_W), BF16), jax.ShapeDtypeStruct((2, HG_W), F32),
                   jax.ShapeDtypeStruct((1, BLOCK), F32), jax.ShapeDtypeStruct((ATT_HEADS, BLOCK), F32)] + c_shapes,
        scratch_shapes=[pltpu.VMEM((HG_HEADS, BLOCK, BLOCK), F32), pltpu.VMEM((BLOCK, kvw), F32),
                        pltpu.VMEM((N_META, kvw), F32)] + c_sems,
        compiler_params=_cparams(("arbitrary",)),
    )(proj_hg, lbounds, norm_g, lv, states, scores, raw, probs, proj_att, proj_att, proj_att, cos, sin, cos, sin,
      cos, sin, sinks, dyh, doa, *parts)


_HBM = pl.BlockSpec(memory_space=pltpu.HBM)
_SEM = pl.BlockSpec(memory_space=pltpu.SEMAPHORE)
_ORDERED_BY_DATA = pltpu.CompilerParams(has_side_effects=pltpu.SideEffectType.DATAFLOW_SIDE_EFFECTING)


def _exchange_copies(part_ref, land_ref, send_sems, recv_sems):
    place = _place()
    return [pltpu.make_async_remote_copy(
        src_ref=part_ref.at[_slot(_peer(place, flip), False)], dst_ref=land_ref.at[r], send_sem=send_sems.at[r],
        recv_sem=recv_sems.at[r], device_id=_peer(place, flip), device_id_type=MESH) for r, flip in enumerate(_FLIPS)]


def _exchange_start(parts, name):
    def body(part_ref, land_ref, send_sems, recv_sems, part_thru, land_thru, token):
        for cp in _exchange_copies(part_ref, land_ref, send_sems, recv_sems):
            cp.start()
        token[...] = jnp.zeros_like(token)

    land = (N_PEERS,) + parts.shape[1:]
    return pl.pallas_call(
        body, name=name,
        out_shape=(pltpu.SemaphoreType.DMA((N_PEERS,)), pltpu.SemaphoreType.DMA((N_PEERS,)),
                   pltpu.HBM(parts.shape, parts.dtype), pltpu.HBM(land, parts.dtype), jax.ShapeDtypeStruct((8, BLOCK), F32)),
        in_specs=(_HBM, _HBM), out_specs=(_SEM, _SEM, _HBM, _HBM, pl.BlockSpec(memory_space=pltpu.VMEM)),
        input_output_aliases={0: 2, 1: 3}, compiler_params=_ORDERED_BY_DATA,
    )(pltpu.with_memory_space_constraint(parts, pltpu.HBM),
      pltpu.with_memory_space_constraint(lax.empty(land, parts.dtype), pltpu.HBM))


def _exchange_wait(send_sems, recv_sems, part_thru, land_thru, after, name):
    def body(part_ref, land_ref, send_sems, recv_sems, after_ref, part_out, land_out):
        for cp in _exchange_copies(part_ref, land_ref, send_sems, recv_sems):
            cp.wait_send()
            cp.wait_recv()

    return pl.pallas_call(
        body, name=name,
        out_shape=(pltpu.HBM(part_thru.shape, part_thru.dtype), pltpu.HBM(land_thru.shape, land_thru.dtype)),
        in_specs=(_HBM, _HBM, _SEM, _SEM, pl.BlockSpec(memory_space=pl.ANY)), out_specs=(_HBM, _HBM),
        input_output_aliases={0: 0, 1: 1}, compiler_params=_ORDERED_BY_DATA,
    )(part_thru, land_thru, send_sems, recv_sems, after)


def _embed_bwd(dmix, dgates, w_in_t, dr1, xhat0, rstd0, g0):
    p = dmix.shape[0]
    tm = _row_tile(p, 640)
    nm = p // tm

    def body(a_ref, g_ref, w_ref, dr_ref, xh_ref, rs_ref, g0_ref, gx_ref, lead_ref, dg_ref, db_ref, buf_ref, sem):
        i = pl.program_id(0)
        first = pltpu.make_async_copy(buf_ref.at[0, pl.ds(BLOCK, tm - BLOCK)], gx_ref.at[pl.ds(0, tm - BLOCK)],
                                      sem.at[0])
        later = lambda t: pltpu.make_async_copy(buf_ref.at[t % 2], gx_ref.at[pl.ds(t * tm - BLOCK, tm)], sem.at[t % 2])

        @pl.when(i == 2)
        def _():
            first.wait()

        @pl.when(i > 2)
        def _():
            later(i - 2).wait()

        dh0 = (ALPHA * dr_ref[...] + _dot(a_ref[...], w_ref[:MIX_W, :], "nn")
               + _dot(g_ref[...], w_ref[MIX_W:, :], "nn"))
        row = i * tm + lax.broadcasted_iota(jnp.int32, (tm, 1), 0)
        dx, dg, db = _ln_bwd(jnp.where(row >= PAD, dh0, 0.0), xh_ref[...], rs_ref[...], g0_ref[...])
        buf_ref[i % 2] = dx

        @pl.when(i == 0)
        def _():
            lead_ref[...] = dx[:BLOCK]
            dg_ref[...] = dg
            db_ref[...] = db
            first.start()

        @pl.when(i > 0)
        def _():
            dg_ref[...] += dg
            db_ref[...] += db
            later(i).start()

        @pl.when(i == nm - 1)
        def _():
            for t in (nm - 2, nm - 1):
                if t >= 0:
                    (first if t == 0 else later(t)).wait()

    row = lambda w: pl.BlockSpec((tm, w), lambda i: (i, 0))
    vec = pl.BlockSpec((1, D_MODEL), lambda i: (0, 0))
    return pl.pallas_call(
        body, name="embed_bwd", grid=(nm,),
        in_specs=[row(dmix.shape[1]), row(dgates.shape[1]),
                  pl.BlockSpec(w_in_t.shape, lambda i: (0, 0), pipeline_mode=pl.Buffered(1)), row(D_MODEL), row(D_MODEL),
                  row(1), vec],
        out_specs=[pl.BlockSpec(memory_space=pl.ANY), pl.BlockSpec((BLOCK, D_MODEL), lambda i: (0, 0)), vec, vec],
        out_shape=[jax.ShapeDtypeStruct((p - BLOCK, D_MODEL), F32), jax.ShapeDtypeStruct((BLOCK, D_MODEL), F32),
                   jax.ShapeDtypeStruct((1, D_MODEL), F32), jax.ShapeDtypeStruct((1, D_MODEL), F32)],
        scratch_shapes=[pltpu.VMEM((2, tm, D_MODEL), F32), pltpu.SemaphoreType.DMA((2,))],
        compiler_params=_cparams(("arbitrary",)),
    )(dmix, dgates, w_in_t, dr1, xhat0, rstd0, g0)


_LATE = ("w_branch_hg", "w_branch_attn", "w_out", "w_ffn_in", "w_ffn_out")
_TRANSPOSED = ("w_in", "w_ffn_in")
_COLUMN_SHARDED = ("meta_tokens", "w_branch_hg", "w_branch_attn")
_SWAPPED = ("w_ffn_in",)


def _whole(name, gathered):
    _, r, c = gathered.shape
    if name in _COLUMN_SHARDED:
        return jnp.transpose(gathered, (1, 0, 2)).reshape(r, N_DEV * c)
    return gathered.reshape(N_DEV * r, c)


def _slots(name, whole):
    r, c = whole.shape
    if name in _COLUMN_SHARDED:
        return jnp.transpose(whole.reshape(r, N_DEV, c // N_DEV), (1, 0, 2))
    return whole.reshape(N_DEV, r // N_DEV, c)


def _device_step(x, target, meta_shard, ln_emb_g, ln_emb_b, w_in_shard, lbounds, norm_g, sinks, late_shards,
                 ln1_g, ln1_b, ln2_g, ln2_b):
    p = x.shape[0] + BLOCK
    lv = _level_stack()
    cos, sin = _rope_tables(p)
    swapped = [n in _SWAPPED for n in _LATE]

    h0, h0b, xhat0, rstd0, _, g_win = _embed_ln(x, meta_shard, w_in_shard, ln_emb_g, ln_emb_b)
    w_in = _whole("w_in", g_win)
    proj_hg, proj_att, gates = _in_proj(h0b, w_in)
    yh, oa, states, scores, raw, probs, *gathered = _mixers_fwd(
        proj_hg, proj_att, lbounds, norm_g, lv, cos, sin, sinks, late_shards, swapped)
    w_bh, w_ba, w_out, w_fi, w_fo = [_whole(n, g) for n, g in zip(_LATE, gathered)]
    mixin, h1, h1b, xhat1, rstd1 = _mix_out_ln1(yh, oa, gates, h0, w_bh, w_ba, w_out, ln1_g, ln1_b)
    au, sw = _ffn_in_swiglu(h1b, w_fi)
    dr2, dr2b, loss_part, dg2, db2 = _ffn_out_loss(sw, w_fo, h1, ln2_g, ln2_b, target)

    mtn = functools.partial(_tiled_matmul_tn, tm=_row_tile(p, 1664), out_dtype=BF16)
    d_wfo = mtn(sw, dr2b, tk=FF_T, tn=D_MODEL, name="grad_w_ffn_out")
    dau, dh1_ffn = _ffn_bwd(dr2b, w_fo, au, w_fi)
    d_wfi = _weight_grad_t([dau], h1b, tk=4 * BLOCK, name="grad_w_ffn_in")
    dr1, dy_hg, dy_att, dgates, dyh, doa, dg1, db1 = _ln1_mix_bwd(
        dr2, dh1_ffn, xhat1, rstd1, ln1_g, yh, oa, gates, w_bh, w_ba, w_out)
    d_wout = mtn(mixin, dr1, tk=D_MODEL, tn=D_MODEL, name="grad_w_out")
    d_wbh = mtn(yh, dy_hg, tk=HG_W, tn=D_MODEL, name="grad_w_branch_hg")
    d_wba = mtn(oa, dy_att, tk=ATT_QW, tn=D_MODEL, name="grad_w_branch_attn")
    late_parts = [_slots(n, g) for n, g in zip(_LATE, (d_wbh, d_wba, d_wout, d_wfi, d_wfo))]
    dmix, d_lb, d_ng, d_sink, *late_recv = _mixers_bwd(
        proj_hg, proj_att, lbounds, norm_g, lv, states, scores, raw, probs, cos, sin, sinks, dyh, doa, late_parts,
        swapped)
    d_win = _weight_grad_t([dmix, dgates], h0b, tk=2 * BLOCK, name="grad_w_in")
    *win_flight, token = _exchange_start(_slots("w_in", d_win), "w_in_grads_start")
    grad_x, dlead, dg0, db0 = _embed_bwd(dmix, dgates, w_in, dr1, xhat0, rstd0, ln_emb_g + token[0:1, 0:1])

    small = dict(ln_emb_g=dg0, ln_emb_b=db0, hg_lower_bounds=d_lb, hg_norm_g=d_ng, ln1_g=dg1, ln1_b=db1, ln2_g=dg2,
                 ln2_b=db2)
    big = dict(zip(_LATE, zip(late_parts, late_recv)))
    return _pack_small(small, d_sink, dlead, loss_part), grad_x, big, win_flight


def _all_gather(arrs, dtypes, name):
    n = len(arrs)

    def body(*refs):
        ins, outs, stages = refs[:n], refs[n:2 * n], refs[2 * n:3 * n]
        send_sems, recv_sems, local_sems = refs[3 * n:]
        x, y, c = _place()
        sibling = (x, y, 1 - c)
        chips = [(1 - x, y), (x, 1 - y), (1 - x, 1 - y)]
        slot = lambda px, py, pc: 4 * px + 2 * py + pc

        def copy(w, k, block, to, from_stage=False):
            return pltpu.make_async_remote_copy(
                src_ref=stages[w] if from_stage else outs[w].at[slot(*block)], dst_ref=outs[w].at[slot(*block)],
                send_sem=send_sems.at[w, k], recv_sem=recv_sems.at[w, k], device_id=to, device_id_type=MESH)

        mine, first, passed = [], [], []
        for w in range(n):
            stages[w][...] = ins[w][...].astype(dtypes[w])
            mine.append(pltpu.make_async_copy(stages[w], outs[w].at[slot(x, y, c)], local_sems.at[w]))
            mine[-1].start()
        for w in range(n):
            first.append(copy(w, 0, (x, y, c), sibling, from_stage=True))
            first += [copy(w, 1 + j, (x, y, c), (*chip, c), from_stage=True) for j, chip in enumerate(chips)]
        for cp in first:
            cp.start()
        for j, chip in enumerate(chips):
            for w in range(n):
                copy(w, 1 + j, (*chip, c), (x, y, c)).wait_recv()
                passed.append(copy(w, 4 + j, (*chip, c), sibling))
                passed[-1].start()
        for w in range(n):
            copy(w, 0, sibling, (x, y, c)).wait_recv()
            for j, chip in enumerate(chips):
                copy(w, 4 + j, (*chip, 1 - c), (x, y, c)).wait_recv()
        for cp in first + passed:
            cp.wait_send()
        for cp in mine:
            cp.wait()

    return pl.pallas_call(
        body, name=name,
        in_specs=[pl.BlockSpec(memory_space=pltpu.VMEM)] * n,
        out_specs=[pl.BlockSpec(memory_space=pl.ANY)] * n,
        out_shape=[jax.ShapeDtypeStruct((N_DEV,) + a.shape, dt) for a, dt in zip(arrs, dtypes)],
        scratch_shapes=[pltpu.VMEM(a.shape, dt) for a, dt in zip(arrs, dtypes)]
        + [pltpu.SemaphoreType.DMA((n, 7)), pltpu.SemaphoreType.DMA((n, 7)), pltpu.SemaphoreType.DMA((n,))],
        compiler_params=pltpu.CompilerParams(vmem_limit_bytes=VMEM_LIMIT_BYTES),
    )(*arrs)


def _cast_shards(arrs):
    def body(*refs):
        for src, dst in zip(refs[:len(arrs)], refs[len(arrs):]):
            dst[...] = src[...].astype(BF16)

    return pl.pallas_call(body, name="cast_shards", out_shape=[jax.ShapeDtypeStruct(a.shape, BF16) for a in arrs],
                          compiler_params=pltpu.CompilerParams(vmem_limit_bytes=VMEM_LIMIT_BYTES))(*arrs)


def _shard_rows(rows):
    return rows if rows <= 512 else max(t for t in range(16, 353, 16) if rows % t == 0)


def _adamw_math(w, g, m, v):
    m = ADAM_B1 * m + (1.0 - ADAM_B1) * g
    v = ADAM_B2 * v + (1.0 - ADAM_B2) * (g * g)
    m_hat = m / (1.0 - ADAM_B1 ** ADAM_STEP)
    v_hat = v / (1.0 - ADAM_B2 ** ADAM_STEP)
    delta = -ADAM_LR * (m_hat / (jnp.sqrt(v_hat) + ADAM_EPS) + ADAM_WD * w)
    return delta, m, v


def _reduce_adamw(parts, recv, own_slot, w, m, v, name):
    r, cdim = w.shape
    tr = _shard_rows(r)

    def body(idx_ref, p_ref, r_ref, w_ref, m_ref, v_ref, g_out, d_out, m_out, v_out):
        g = p_ref[0].astype(F32)
        for j in range(N_PEERS):
            g = g + r_ref[j].astype(F32)
        d, mn, vn = _adamw_math(w_ref[...], g, m_ref[...], v_ref[...])
        g_out[...] = g
        d_out[...] = d
        m_out[...] = mn
        v_out[...] = vn

    flat = pl.BlockSpec((tr, cdim), lambda i, idx_ref: (i, 0))
    return pl.pallas_call(
        body, name=name,
        grid_spec=pltpu.PrefetchScalarGridSpec(
            num_scalar_prefetch=1, grid=(r // tr,),
            in_specs=[pl.BlockSpec((1, tr, cdim), lambda i, idx_ref: (idx_ref[0], i, 0)),
                      pl.BlockSpec((N_PEERS, tr, cdim), lambda i, idx_ref: (0, i, 0)), flat, flat, flat],
            out_specs=[flat] * 4),
        out_shape=[jax.ShapeDtypeStruct((r, cdim), F32)] * 4,
        compiler_params=_cparams(("arbitrary",)),
    )(own_slot, parts, recv, w, m, v)


def _adamw_plain(w, g, m, v, name):
    def body(w_ref, g_ref, m_ref, v_ref, d_out, m_out, v_out):
        d_out[...], m_out[...], v_out[...] = _adamw_math(w_ref[...], g_ref[...], m_ref[...], v_ref[...])

    return pl.pallas_call(body, name=name, out_shape=[jax.ShapeDtypeStruct(w.shape, F32)] * 3)(w, g, m, v)


_SMALL = (("ln_emb_g", (1, D_MODEL)), ("ln_emb_b", (1, D_MODEL)), ("hg_lower_bounds", (2, HG_W)),
          ("hg_norm_g", (1, BLOCK)), ("attn_sinks", (1, ATT_HEADS)), ("ln1_g", (1, D_MODEL)), ("ln1_b", (1, D_MODEL)),
          ("ln2_g", (1, D_MODEL)), ("ln2_b", (1, D_MODEL)))
_SMALL_ROW, _LOSS_ROW = {}, 0
for _name, (_rows, _) in _SMALL:
    _SMALL_ROW[_name], _LOSS_ROW = _LOSS_ROW, _LOSS_ROW + _rows
_META_ROW = 16
SMALL_ROWS = _META_ROW + N_META
assert _LOSS_ROW < _META_ROW


def _pack_small(grads, d_sink, dlead, loss_part):
    names = [n for n, _ in _SMALL if n != "attn_sinks"]

    def body(*refs):
        ins = dict(zip(names, refs))
        sink_ref, lead_ref, loss_ref, o_ref = refs[len(names):]
        o_ref[...] = jnp.zeros_like(o_ref)
        for name, (rows, cols) in _SMALL:
            if name != "attn_sinks":
                o_ref[_SMALL_ROW[name]:_SMALL_ROW[name] + rows, :cols] = ins[name][...]
        head = lax.broadcasted_iota(jnp.int32, (ATT_HEADS, BLOCK), 0)
        lane = lax.broadcasted_iota(jnp.int32, (ATT_HEADS, BLOCK), 1)
        o_ref[_SMALL_ROW["attn_sinks"]:_SMALL_ROW["attn_sinks"] + 1, :BLOCK] = jnp.sum(
            jnp.where(head == lane, sink_ref[...], 0.0), axis=0, keepdims=True)
        o_ref[_LOSS_ROW:_LOSS_ROW + 1, :BLOCK] = loss_ref[...]
        o_ref[_META_ROW:, :] = lead_ref[PAD:BLOCK, :]

    return pl.pallas_call(body, name="pack_small", out_shape=jax.ShapeDtypeStruct((SMALL_ROWS, D_MODEL), F32))(
        *[grads[n] for n in names], d_sink, dlead, loss_part)


def _small_reduce_adamw(gathered, weights, mom1, mom2):
    n = len(_SMALL)

    def body(*refs):
        g_ref, w_refs, m_refs, v_refs = refs[0], refs[1:1 + n], refs[1 + n:1 + 2 * n], refs[1 + 2 * n:1 + 3 * n]
        outs = refs[1 + 3 * n:1 + 7 * n]
        meta_out, loss_out, sum_ref = refs[1 + 7 * n:]
        total = g_ref[0]
        for s in range(1, N_DEV):
            total = total + g_ref[s]
        sum_ref[...] = total
        for i, (name, (rows, cols)) in enumerate(_SMALL):
            g = sum_ref[_SMALL_ROW[name]:_SMALL_ROW[name] + rows, :cols]
            d, mn, vn = _adamw_math(w_refs[i][...], g, m_refs[i][...], v_refs[i][...])
            for out, val in zip(outs[4 * i:4 * i + 4], (g, d, mn, vn)):
                out[...] = val
        meta_out[...] = sum_ref[_META_ROW:, :]
        loss_out[...] = jnp.broadcast_to(jnp.sum(sum_ref[_LOSS_ROW:_LOSS_ROW + 1, :BLOCK]), (1, BLOCK))

    per_param = [jax.ShapeDtypeStruct(shape, F32) for _, shape in _SMALL for _ in range(4)]
    res = pl.pallas_call(
        body, name="small_reduce_adamw",
        out_shape=per_param + [jax.ShapeDtypeStruct((N_META, D_MODEL), F32), jax.ShapeDtypeStruct((1, BLOCK), F32)],
        scratch_shapes=[pltpu.VMEM((SMALL_ROWS, D_MODEL), F32)],
    )(gathered, *[d[name] for d in (weights, mom1, mom2) for name, _ in _SMALL])
    return {name: res[4 * i:4 * i + 4] for i, (name, _) in enumerate(_SMALL)}, res[-2], res[-1]


_WEIGHTS = ("meta_tokens", "ln_emb_g", "ln_emb_b", "w_in", "hg_lower_bounds", "hg_norm_g", "attn_sinks",
            "w_branch_hg", "w_branch_attn", "w_out", "ln1_g", "ln1_b", "w_ffn_in", "w_ffn_out", "ln2_g", "ln2_b")


def kernel(x, meta_tokens, ln_emb_g, ln_emb_b, w_in, hg_lower_bounds, hg_norm_g, attn_sinks, w_branch_hg, w_branch_attn, w_out, ln1_g, ln1_b, w_ffn_in, w_ffn_out, ln2_g, ln2_b, loss_target, m_meta_tokens, m_ln_emb_g, m_ln_emb_b, m_w_in, m_hg_lower_bounds, m_hg_norm_g, m_attn_sinks, m_w_branch_hg, m_w_branch_attn, m_w_out, m_ln1_g, m_ln1_b, m_w_ffn_in, m_w_ffn_out, m_ln2_g, m_ln2_b, v_meta_tokens, v_ln_emb_g, v_ln_emb_b, v_w_in, v_hg_lower_bounds, v_hg_norm_g, v_attn_sinks, v_w_branch_hg, v_w_branch_attn, v_w_out, v_ln1_g, v_ln1_b, v_w_ffn_in, v_w_ffn_out, v_ln2_g, v_ln2_b):
    given = dict(locals())
    weights = {n: given[n] for n in _WEIGHTS}
    mom1 = {n: given["m_" + n] for n in _WEIGHTS}
    mom2 = {n: given["v_" + n] for n in _WEIGHTS}
    shard2d = lambda n, a: a.reshape(a.shape[-2:]).T if n in _TRANSPOSED else a.reshape(a.shape[-2:])

    w_in_shard, *late_shards = _cast_shards([shard2d(n, weights[n]) for n in ("w_in",) + _LATE])
    packed, grad_x, big, win_flight = _device_step(
        x[0], loss_target[0], meta_tokens, ln_emb_g.reshape(1, -1), ln_emb_b.reshape(1, -1), w_in_shard,
        hg_lower_bounds, hg_norm_g, attn_sinks, late_shards, ln1_g, ln1_b, ln2_g, ln2_b)

    place = _place()
    out = {}

    def reduce_adamw(n, parts, recv):
        own = _slot(place, n in _SWAPPED).astype(jnp.int32).reshape(1)
        res = _reduce_adamw(parts, recv, own, shard2d(n, weights[n]), shard2d(n, mom1[n]), shard2d(n, mom2[n]),
                            "adamw_" + n)
        out[n] = [(r.T if n in _TRANSPOSED else r).reshape(weights[n].shape) for r in res]

    for n, (parts, recv) in big.items():
        reduce_adamw(n, parts, recv)

    all_small, = _all_gather([packed], [F32], "gather_small")
    as_2d = lambda d: {n: d[n].reshape(shape) for n, shape in _SMALL}
    small_out, meta_whole, loss_row = _small_reduce_adamw(all_small, as_2d(weights), as_2d(mom1), as_2d(mom2))
    for n, res in small_out.items():
        out[n] = [r.reshape(weights[n].shape) for r in res]
    loss = loss_row[0, 0]
    g_meta_mine = lax.dynamic_index_in_dim(meta_whole.reshape(N_META, N_DEV, D_MODEL // N_DEV), _slot(place, False),
                                           axis=1, keepdims=False)
    out["meta_tokens"] = [g_meta_mine, *_adamw_plain(meta_tokens, g_meta_mine, m_meta_tokens, v_meta_tokens,
                                                     "adamw_meta")]

    reduce_adamw("w_in", *_exchange_wait(*win_flight, after=all_small, name="w_in_grads_wait"))

    return (loss, grad_x[None], *[out[n][0] for n in _WEIGHTS], *[out[n][1] for n in _WEIGHTS],
            *[out[n][2] for n in _WEIGHTS], *[out[n][3] for n in _WEIGHTS])
```

```python
import functools

import numpy as np
import jax
import jax.numpy as jnp
from jax import lax
from jax.experimental import pallas as pl
from jax.experimental.pallas import tpu as pltpu

F32 = jnp.float32
BF16 = jnp.bfloat16

D_MODEL = 1024
N_META = 16
BLOCK = 128
PAD = BLOCK - N_META
HG_HEADS = 4
HG_W = 512
ATT_HEADS = 8
HEAD_DIM = 64
ATT_QW = 512
ATT_KVW = 128
D_FF = 2816
EPS = 1e-5
ALPHA = 2.0 ** 0.25
ROPE_THETA = 10000.0
N_DEV = 8

ADAM_LR = 0.001
ADAM_B1 = 0.9
ADAM_B2 = 0.999
ADAM_EPS = 1e-08
ADAM_WD = 0.01
ADAM_STEP = 10

VMEM_LIMIT_BYTES = 56 * 1024 * 1024
MESH = pl.DeviceIdType.MESH

_LEVELS = (64, 32, 16, 8, 4, 2, 1)


def _cparams(sem):
    return pltpu.CompilerParams(dimension_semantics=sem, vmem_limit_bytes=VMEM_LIMIT_BYTES)


def _row_tile(rows, target):
    nb = rows // BLOCK
    best = 1
    for d in range(1, nb + 1):
        if nb % d == 0 and d * BLOCK <= target:
            best = d
    return best * BLOCK


_DN = {"nn": (((1,), (0,)), ((), ())), "nt": (((1,), (1,)), ((), ())), "tn": (((0,), (0,)), ((), ()))}


def _dot(a, b, form):
    return lax.dot_general(a.astype(BF16), b.astype(BF16), _DN[form], preferred_element_type=F32)


@functools.partial(jax.custom_vjp, nondiff_argnums=(2,))
def _mm(a, b, form):
    return _dot(a, b, form)


def _mm_fwd(a, b, form):
    a, b = a.astype(BF16), b.astype(BF16)
    return _dot(a, b, form), (a, b)


def _mm_bwd(form, res, g):
    a, b = res
    if form == "nn":
        return _dot(g, b, "nt"), _dot(a, g, "tn")
    if form == "nt":
        return _dot(g, b, "nn"), _dot(g, a, "tn")
    return _dot(b, g, "nt"), _dot(a, g, "nn")


_mm.defvjp(_mm_fwd, _mm_bwd)


def _split_dot(lv, x, form):
    return lax.dot_general(lv, x.astype(BF16), _DN[form], preferred_element_type=F32)


@jax.custom_vjp
def _swap_halves(x):
    return pltpu.roll(x, 64, 1)


_swap_halves.defvjp(lambda x: (pltpu.roll(x, 64, 1), None), lambda _, g: (pltpu.roll(g, 64, 1),))


def _tiled_matmul_tn(a, b, *, tm, tk, tn, out_dtype, name):
    m, k = a.shape
    n = b.shape[1]
    assert m % tm == 0 and k % tk == 0 and n % tn == 0, (name, a.shape, b.shape, tm, tk, tn)
    nm = m // tm

    def body(a_ref, b_ref, o_ref, acc_ref):
        mi = pl.program_id(2)

        @pl.when(mi == 0)
        def _():
            acc_ref[...] = jnp.zeros_like(acc_ref)

        acc_ref[...] += _dot(a_ref[...], b_ref[...], "tn")

        @pl.when(mi == nm - 1)
        def _():
            o_ref[...] = acc_ref[...].astype(out_dtype)

    return pl.pallas_call(
        body, name=name, grid=(k // tk, n // tn, nm),
        in_specs=[pl.BlockSpec((tm, tk), lambda kk, j, i: (i, kk)), pl.BlockSpec((tm, tn), lambda kk, j, i: (i, j))],
        out_specs=pl.BlockSpec((tk, tn), lambda kk, j, i: (kk, j)),
        out_shape=jax.ShapeDtypeStruct((k, n), out_dtype),
        scratch_shapes=[pltpu.VMEM((tk, tn), F32)],
        compiler_params=_cparams(("arbitrary", "arbitrary", "arbitrary")),
    )(a, b)


def _weight_grad_t(cots, h, *, tk, name):
    p, d = h.shape
    steps = [c.shape[1] // tk for c in cots]
    assert all(c.shape == (p, n * tk) for c, n in zip(cots, steps)), (name, [c.shape for c in cots], tk)
    first = [sum(steps[:i]) for i in range(len(cots))]

    def body(*refs):
        h_ref, o_ref = refs[len(cots)], refs[len(cots) + 1]
        k = pl.program_id(0)
        for c_ref, lo, n in zip(refs, first, steps):
            @pl.when((k >= lo) & (k < lo + n))
            def _(c_ref=c_ref):
                o_ref[...] = _dot(c_ref[...], h_ref[...], "tn").astype(BF16)

    cot_spec = lambda lo, n: pl.BlockSpec((p, tk), lambda k: (0, jnp.clip(k - lo, 0, n - 1)))
    return pl.pallas_call(
        body, name=name, grid=(sum(steps),),
        in_specs=[cot_spec(lo, n) for lo, n in zip(first, steps)]
                 + [pl.BlockSpec((p, d), lambda k: (0, 0), pipeline_mode=pl.Buffered(1))],
        out_specs=pl.BlockSpec((tk, d), lambda k: (k, 0)),
        out_shape=jax.ShapeDtypeStruct((sum(steps) * tk, d), BF16),
        compiler_params=_cparams(("arbitrary",)),
    )(*cots, h)


def _ln_stats(r):
    mu = jnp.mean(r, axis=-1, keepdims=True)
    xc = r - mu
    var = jnp.mean(xc * xc, axis=-1, keepdims=True)
    rstd = lax.rsqrt(var + EPS)
    return xc * rstd, rstd


def _ln_bwd(dy, xhat, rstd, g):
    dxhat = dy * g
    m1 = jnp.mean(dxhat, axis=-1, keepdims=True)
    m2 = jnp.mean(dxhat * xhat, axis=-1, keepdims=True)
    dr = rstd * (dxhat - m1 - xhat * m2)
    return dr, jnp.sum(dy * xhat, axis=0, keepdims=True), jnp.sum(dy, axis=0, keepdims=True)


N_SEG = 3 + len(_LEVELS)


def _level_stack():
    t = np.arange(BLOCK)[:, None]
    r = np.arange(BLOCK)[None, :]
    mats = [r <= t, r > t, np.ones((BLOCK, BLOCK), bool)]
    for h in _LEVELS:
        same = (t // (2 * h)) == (r // (2 * h))
        up_t, up_r = (t % (2 * h)) >= h, (r % (2 * h)) >= h
        mats.append(same & ((up_t & up_r & (r <= t)) | (~up_t & ~up_r & (r > t))))
    return jnp.asarray(np.concatenate(mats, axis=0).astype(np.float32), dtype=BF16)


def _hgrn_gates(hf, a0, a1, valid):
    lb = jax.nn.sigmoid(a0 - a1)
    fg = lb + (1.0 - lb) * jax.nn.sigmoid(hf)
    return jnp.where(valid, jnp.log(fg), 0.0), jnp.where(valid, 1.0 - fg, 0.0)


def _hgrn_scores(hq, k, *levels):
    q = jax.nn.silu(hq)
    rows = lax.broadcasted_iota(jnp.int32, (BLOCK, BLOCK), 0)
    cols = lax.broadcasted_iota(jnp.int32, (BLOCK, BLOCK), 1)
    a = jnp.where(rows == cols, jnp.sum(q * k, axis=-1, keepdims=True), 0.0)
    differ = jnp.bitwise_xor(rows, cols)
    for h, lvl in zip(_LEVELS, levels):
        decay = jnp.exp(lvl)
        pair = (cols < rows) & (differ >= h) & (differ < 2 * h)
        a = a + jnp.where(pair, _mm(q * decay, k * decay, "nt"), 0.0)
    return a


def _hgrn_mix(hq, k, v, st_in, a, seg_incl, seg_after, seg_total):
    o = _mm(jax.nn.silu(hq) * jnp.exp(seg_incl), st_in, "nt") + _mm(a, v, "nn")
    return o, st_in * jnp.exp(seg_total) + _mm(v, k * jnp.exp(seg_after), "tn")


def _hgrn_norm(o, hg, ng):
    return o * lax.rsqrt(jnp.mean(o * o, axis=-1, keepdims=True) + EPS) * ng * jax.nn.silu(hg)


def _seg_blocks(e, h):
    return [e[i * BLOCK:(i + 1) * BLOCK, h * BLOCK:(h + 1) * BLOCK] for i in range(N_SEG)]


def _rope(x, cos, sin, first_half):
    partner = jnp.where(first_half, -pltpu.roll(x, 96, 1), pltpu.roll(x, 32, 1))
    return x * cos + partner * sin


def _rope_t(g, cos, sin, first_half):
    u = g * sin
    partner = jnp.where(first_half, pltpu.roll(u, 96, 1), -pltpu.roll(u, 32, 1))
    return g * cos + partner


def _low_half(x):
    return lax.broadcasted_iota(jnp.int32, x.shape, 1) < HEAD_DIM


def _both_halves(x, g):
    sw = _swap_halves(x)
    return jnp.where(_low_half(x), x, sw) if g == 0 else jnp.where(_low_half(x), sw, x)


def _att_scores(qa, qb, kc, kp, km, g, own4, band4, meta4):
    low = _low_half(qa)
    q4 = jnp.concatenate([jnp.where(low, qa, 0.0), jnp.where(low, 0.0, qa),
                          jnp.where(low, qb, 0.0), jnp.where(low, 0.0, qb)], axis=0)
    scale = HEAD_DIM ** -0.5
    neg = jnp.finfo(F32).min
    s = jnp.where(own4, _mm(_both_halves(kc, g), q4, "nt"), _mm(_both_halves(kp, g), q4, "nt"))
    return (jnp.where(band4, s * scale, neg), jnp.where(meta4, _mm(_both_halves(km, g), q4, "nt") * scale, neg))


def _att_probs(s, sm, sinkrow):
    mx = jnp.maximum(jnp.maximum(jnp.max(s, axis=0, keepdims=True), jnp.max(sm, axis=0, keepdims=True)), sinkrow)
    p, pm, ps = jnp.exp(s - mx), jnp.exp(sm - mx), jnp.exp(sinkrow - mx)
    inv = 1.0 / (jnp.sum(p, axis=0, keepdims=True) + jnp.sum(pm, axis=0, keepdims=True) + ps)
    return p * inv, pm * inv, ps * inv


def _att_probs_bwd(p, pm, ps, dp, dpm):
    r = jnp.sum(p * dp, axis=0, keepdims=True) + jnp.sum(pm * dpm, axis=0, keepdims=True)
    return p * (dp - r), pm * (dpm - r), -ps * r


def _att_values(p, pm, vc, vp, vm, g, own4):
    o4 = (_mm(jnp.where(own4, p, 0.0), _both_halves(vc, g), "tn") + _mm(jnp.where(own4, 0.0, p), _both_halves(vp, g), "tn")
          + _mm(pm, _both_halves(vm, g), "tn"))
    tiles = []
    for j in range(2):
        upper = o4[(2 * j) * BLOCK:(2 * j + 1) * BLOCK]
        tiles.append(jnp.where(_low_half(upper), upper, o4[(2 * j + 1) * BLOCK:(2 * j + 2) * BLOCK]))
    return tiles


def _att_masks(blk_idx):
    kidx = lax.broadcasted_iota(jnp.int32, (BLOCK, BLOCK), 0)
    qrow = lax.broadcasted_iota(jnp.int32, (BLOCK, BLOCK), 1)
    own_side = kidx <= qrow
    pos_own = blk_idx * BLOCK + kidx - PAD
    ok_band = (own_side & (pos_own >= N_META)) | (~own_side & (pos_own - BLOCK >= N_META) & (blk_idx >= 1))
    qpos = blk_idx * BLOCK + lax.broadcasted_iota(jnp.int32, (N_META, BLOCK), 1) - PAD
    ok_meta = lax.broadcasted_iota(jnp.int32, (N_META, BLOCK), 0) <= qpos
    return [jnp.concatenate([m] * 4, axis=1) for m in (own_side, ok_band, ok_meta)]


def _token_streams(tr, tile_of=lambda i: i):
    k = tr // BLOCK
    return [pl.BlockSpec((BLOCK, D_MODEL), lambda i, j=j: (jnp.maximum(k * tile_of(i) - 1 + j, 0), 0))
            for j in range(k)]


def _embed_ln(x, meta_shard, w_in_shard, g0, b0):
    p = x.shape[0] + BLOCK
    tr = _row_tile(p, 640)
    k = tr // BLOCK
    nt = p // tr
    tile_of = lambda s: (s + 1) % nt
    shards = [meta_shard, w_in_shard]
    c_in, c_out, c_shapes, c_sems = _comm_specs(shards, N_DEV)

    def body(*refs):
        g_ref, b_ref = refs[k:k + 2]
        h_ref, hb_ref, xh_ref, rs_ref = refs[k + 4:k + 8]
        out_refs = refs[k + 8:k + 10]
        lead_ref, meta_ref = refs[k + 10:k + 12]
        starts, passes, waits = _gather_behind(refs[k + 2:k + 4], out_refs, refs[k + 12:], [False, False])
        s = pl.program_id(0)
        t = tile_of(s)

        @pl.when(s == 0)
        def _():
            lead_ref[...] = jnp.zeros_like(lead_ref)
            for start in starts:
                start()

        @pl.when(s == nt - 1)
        def _():
            for step in passes + waits:
                step()
            pltpu.sync_copy(out_refs[0], meta_ref)
            for d in range(N_DEV):
                lead_ref[PAD:BLOCK, d * BLOCK:(d + 1) * BLOCK] = meta_ref[d]

        first = jnp.where(t == 0, lead_ref[...], refs[0][...])
        xhat, rstd = _ln_stats(jnp.concatenate([first] + [r[...] for r in refs[1:k]], axis=0))
        row = t * tr + lax.broadcasted_iota(jnp.int32, (tr, 1), 0)
        h = jnp.where(row >= PAD, xhat * g_ref[...] + b_ref[...], 0.0)
        h_ref[...] = h
        hb_ref[...] = h.astype(BF16)
        xh_ref[...] = xhat
        rs_ref[...] = rstd

    vec = pl.BlockSpec((1, D_MODEL), lambda s: (0, 0))
    rowsp = pl.BlockSpec((tr, D_MODEL), lambda s: (tile_of(s), 0))
    return pl.pallas_call(
        body, name="embed_ln", grid=(nt,),
        in_specs=_token_streams(tr, tile_of) + [vec, vec] + c_in,
        out_specs=[rowsp, rowsp, rowsp, pl.BlockSpec((tr, 1), lambda s: (tile_of(s), 0))] + c_out,
        out_shape=[jax.ShapeDtypeStruct((p, D_MODEL), F32), jax.ShapeDtypeStruct((p, D_MODEL), BF16),
                   jax.ShapeDtypeStruct((p, D_MODEL), F32), jax.ShapeDtypeStruct((p, 1), F32)] + c_shapes,
        scratch_shapes=[pltpu.VMEM((BLOCK, D_MODEL), F32), pltpu.VMEM((N_DEV, N_META, BLOCK), F32)] + c_sems,
        compiler_params=_cparams(("arbitrary",)),
    )(*([x] * k), g0, b0, *shards)


def _rope_tables(p):
    pos = (np.arange(p, dtype=np.int32) - PAD).astype(np.float32)
    half = HEAD_DIM // 2
    inv = np.float32(ROPE_THETA) ** (-np.arange(half, dtype=np.float32) / np.float32(half))
    ang = pos[:, None] * np.tile(inv.astype(np.float32), BLOCK // half)[None, :]
    return jnp.asarray(np.cos(ang), F32), jnp.asarray(np.sin(ang), F32)


def _att_sinkrows(sink_ref):
    lanehead = lax.broadcasted_iota(jnp.int32, (1, 4 * BLOCK), 1) // BLOCK
    rows = []
    for g in range(2):
        row = jnp.zeros((1, 4 * BLOCK), F32)
        for j in range(4):
            row = jnp.where(lanehead == j, sink_ref[0, 4 * g + j], row)
        rows.append(row)
    return rows


def _first_half(rows):
    return (lax.broadcasted_iota(jnp.int32, (rows, BLOCK), 1) % HEAD_DIM) < (HEAD_DIM // 2)


def _att_load(qkv_ref, cos_ref, sin_ref, with_q):
    cos, sin, fh = cos_ref[...], sin_ref[...], _first_half(BLOCK)
    qs = [_rope(qkv_ref[:, j * BLOCK:(j + 1) * BLOCK], cos, sin, fh) for j in range(4)] if with_q else None
    k = _rope(qkv_ref[:, ATT_QW:ATT_QW + ATT_KVW], cos, sin, fh)
    v = qkv_ref[:, ATT_QW + ATT_KVW:ATT_QW + 2 * ATT_KVW]
    return qs, k, v


def _att_load_meta(qkv_ref, cos_ref, sin_ref):
    k = _rope(qkv_ref[PAD:BLOCK, ATT_QW:ATT_QW + ATT_KVW], cos_ref[PAD:BLOCK, :], sin_ref[PAD:BLOCK, :],
              _first_half(N_META))
    return k, qkv_ref[PAD:BLOCK, ATT_QW + ATT_KVW:ATT_QW + 2 * ATT_KVW]


def _att_specs(blk):
    w = ATT_QW + 2 * ATT_KVW
    cur = lambda width: pl.BlockSpec((BLOCK, width), lambda i: (blk(i), 0))
    prev = lambda width: pl.BlockSpec((BLOCK, width), lambda i: (jnp.maximum(blk(i) - 1, 0), 0))
    meta = lambda width: pl.BlockSpec((BLOCK, width), lambda i: (0, 0))
    return [cur(w), prev(w), meta(w), cur(BLOCK), cur(BLOCK), prev(BLOCK), prev(BLOCK), meta(BLOCK), meta(BLOCK),
            pl.BlockSpec(memory_space=pltpu.SMEM)]


_FLIPS = [(dx, dy, dc) for dx in (0, 1) for dy in (0, 1) for dc in (0, 1)][1:]
N_PEERS = len(_FLIPS)


def _place():
    return lax.axis_index("x"), lax.axis_index("y"), lax.axis_index("c")


def _peer(place, flip):
    return tuple(1 - p if f else p for p, f in zip(place, flip))


def _slot(place, swapped):
    x, y, c = place
    return 4 * y + 2 * x + c if swapped else 4 * x + 2 * y + c


def _comm_specs(arrs, out_lead):
    n = len(arrs)
    outs = [jax.ShapeDtypeStruct((out_lead,) + a.shape[-2:], a.dtype) for a in arrs]
    sems = [pltpu.SemaphoreType.DMA((n, N_PEERS)), pltpu.SemaphoreType.DMA((n, N_PEERS)), pltpu.SemaphoreType.DMA((n,))]
    return [pl.BlockSpec(memory_space=pl.ANY)] * n, [pl.BlockSpec(memory_space=pl.ANY)] * n, outs, sems


def _gather_behind(shard_refs, out_refs, sems, swapped):
    send_sems, recv_sems, local_sems = sems
    x, y, c = _place()
    me, sibling = (x, y, c), (x, y, 1 - c)
    chips = [(1 - x, y), (x, 1 - y), (1 - x, 1 - y)]
    starts, passes, waits = [], [], []
    for w, (s, o) in enumerate(zip(shard_refs, out_refs)):
        def copy(k, block, to, from_shard=False, w=w, s=s, o=o):
            rows = o.at[_slot(block, swapped[w])]
            return pltpu.make_async_remote_copy(
                src_ref=s if from_shard else rows, dst_ref=rows, send_sem=send_sems.at[w, k],
                recv_sem=recv_sems.at[w, k], device_id=to, device_id_type=MESH)

        own = pltpu.make_async_copy(s, o.at[_slot(me, swapped[w])], local_sems.at[w])
        first = [copy(0, me, sibling, True)] + [copy(1 + j, me, (*chip, c), True) for j, chip in enumerate(chips)]
        handed = [copy(4 + j, (*chip, c), sibling) for j, chip in enumerate(chips)]
        starts += [own.start] + [cp.start for cp in first]
        for j, chip in enumerate(chips):
            passes += [copy(1 + j, (*chip, c), me).wait_recv, handed[j].start]
        waits.append(copy(0, sibling, me).wait_recv)
        waits += [copy(4 + j, (*chip, 1 - c), me).wait_recv for j, chip in enumerate(chips)]
        waits += [cp.wait_send for cp in first + handed] + [own.wait]
    return starts, passes, waits


def _scatter_behind(part_refs, recv_refs, sems, swapped):
    send_sems, recv_sems, _ = sems
    place = _place()
    starts, waits = [], []
    for w, (p, o) in enumerate(zip(part_refs, recv_refs)):
        for r, flip in enumerate(_FLIPS):
            peer = _peer(place, flip)
            cp = pltpu.make_async_remote_copy(
                src_ref=p.at[_slot(peer, swapped[w])], dst_ref=o.at[r], send_sem=send_sems.at[w, r],
                recv_sem=recv_sems.at[w, r], device_id=peer, device_id_type=MESH)
            starts.append(cp.start)
            waits += [cp.wait_recv, cp.wait_send]
    return starts, waits


def _mixers_fwd(proj_hg, proj_att, lbounds, norm_g, lv, cos, sin, sinks, shards, swapped):
    p = proj_hg.shape[0]
    nb = p // BLOCK
    n = len(shards)
    c_in, c_out, c_shapes, c_sems = _comm_specs(shards, N_DEV)
    pass_step = min(nb - 1, max(1, (5 * nb) // 8))

    def body(*refs):
        x_ref, lb_ref, ng_ref, lv_ref, cur_ref, prev_ref, meta_ref, cc, sc, cp, sp, cm, sm, sink_ref = refs[:14]
        shard_refs = refs[14:14 + n]
        y_ref, o_ref, st_ref, a_ref, raw_ref, pr_ref = refs[14 + n:20 + n]
        out_refs = refs[20 + n:20 + 2 * n]
        carry_ref = refs[20 + 2 * n]
        starts, passes, waits = _gather_behind(shard_refs, out_refs, refs[21 + 2 * n:], swapped)
        c = pl.program_id(0)

        @pl.when(c == 0)
        def _():
            carry_ref[...] = jnp.zeros_like(carry_ref)
            for start in starts:
                start()

        @pl.when(c == pass_step)
        def _():
            for step in passes:
                step()

        valid = (c * BLOCK + lax.broadcasted_iota(jnp.int32, (BLOCK, 1), 0)) >= PAD
        logf, k = _hgrn_gates(x_ref[:, HG_W:2 * HG_W], lb_ref[0:1, :], lb_ref[1:2, :], valid)
        e = _split_dot(lv_ref[...], logf, "nn")
        for h in range(HG_HEADS):
            sl = lambda part: x_ref[:, part * HG_W + h * BLOCK: part * HG_W + (h + 1) * BLOCK]
            hs = slice(h * BLOCK, (h + 1) * BLOCK)
            st_in = carry_ref[h]
            st_ref[0, h] = st_in
            seg = _seg_blocks(e, h)
            a = _hgrn_scores(sl(0), k[:, hs], *seg[3:])
            a_ref[0, h] = a.astype(BF16)
            raw, st_out = _hgrn_mix(sl(0), k[:, hs], sl(2), st_in, a, *seg[:3])
            raw_ref[:, hs] = raw
            y_ref[:, hs] = _hgrn_norm(raw, sl(3), ng_ref[...]).astype(BF16)
            carry_ref[h] = st_out

        qs, kc, vc = _att_load(cur_ref, cc, sc, True)
        _, kp, vp = _att_load(prev_ref, cp, sp, False)
        km, vm = _att_load_meta(meta_ref, cm, sm)
        sinkrows = _att_sinkrows(sink_ref)
        own4, band4, meta4 = _att_masks(c)
        for g in range(2):
            s, s_meta = _att_scores(qs[2 * g], qs[2 * g + 1], kc, kp, km, g, own4, band4, meta4)
            pr, pr_meta, pr_sink = _att_probs(s, s_meta, sinkrows[g])
            pr_ref[0, g, :BLOCK, :] = pr.astype(BF16)
            pr_ref[0, g, BLOCK:BLOCK + N_META, :] = pr_meta.astype(BF16)
            pr_ref[0, g, BLOCK + N_META:, :] = jnp.broadcast_to(pr_sink, (N_META, 4 * BLOCK)).astype(BF16)
            for j, tile in enumerate(_att_values(pr, pr_meta, vc, vp, vm, g, own4)):
                o_ref[:, (2 * g + j) * BLOCK:(2 * g + j + 1) * BLOCK] = tile.astype(BF16)

        @pl.when(c == nb - 1)
        def _():
            for wait in waits:
                wait()

    return pl.pallas_call(
        body, name="mixers_fwd", grid=(nb,),
        in_specs=[pl.BlockSpec((BLOCK, 4 * HG_W), lambda c: (c, 0)), pl.BlockSpec((2, HG_W), lambda c: (0, 0)),
                  pl.BlockSpec((1, BLOCK), lambda c: (0, 0)), pl.BlockSpec(lv.shape, lambda c: (0, 0))]
        + _att_specs(lambda c: c) + c_in,
        out_specs=[pl.BlockSpec((BLOCK, HG_W), lambda c: (c, 0)), pl.BlockSpec((BLOCK, ATT_QW), lambda c: (c, 0)),
                   pl.BlockSpec((1, HG_HEADS, BLOCK, BLOCK), lambda c: (c, 0, 0, 0)),
                   pl.BlockSpec((1, HG_HEADS, BLOCK, BLOCK), lambda c: (c, 0, 0, 0)),
                   pl.BlockSpec((BLOCK, HG_W), lambda c: (c, 0)),
                   pl.BlockSpec((1, 2, ATT_KEYS, 4 * BLOCK), lambda c: (c, 0, 0, 0))] + c_out,
        out_shape=[jax.ShapeDtypeStruct((p, HG_W), BF16), jax.ShapeDtypeStruct((p, ATT_QW), BF16),
                   jax.ShapeDtypeStruct((nb, HG_HEADS, BLOCK, BLOCK), F32),
                   jax.ShapeDtypeStruct((nb, HG_HEADS, BLOCK, BLOCK), BF16),
                   jax.ShapeDtypeStruct((p, HG_W), F32),
                   jax.ShapeDtypeStruct((nb, 2, ATT_KEYS, 4 * BLOCK), BF16)] + c_shapes,
        scratch_shapes=[pltpu.VMEM((HG_HEADS, BLOCK, BLOCK), F32)] + c_sems,
        compiler_params=_cparams(("arbitrary",)),
    )(proj_hg, lbounds, norm_g, lv, proj_att, proj_att, proj_att, cos, sin, cos, sin, cos, sin, sinks, *shards)


def _tile(rows, preferred):
    return preferred if rows % preferred == 0 else _row_tile(rows, preferred)


def _in_proj(h0b, w_in_t):
    p = h0b.shape[0]
    tm = _row_tile(p, 640)
    hg_end = 4 * HG_W

    def body(h_ref, w_ref, hg_ref, att_ref):
        h = h_ref[...]
        hg_ref[...] = _dot(h, w_ref[:hg_end, :], "nt")
        att_ref[...] = _dot(h, w_ref[hg_end:, :], "nt")

    row = lambda w: pl.BlockSpec((tm, w), lambda i: (i, 0))
    return pl.pallas_call(
        body, name="in_proj", grid=(p // tm,),
        in_specs=[row(D_MODEL), pl.BlockSpec((MIX_W, D_MODEL), lambda i: (0, 0), pipeline_mode=pl.Buffered(1))],
        out_specs=[row(hg_end), row(MIX_W - hg_end)],
        out_shape=[jax.ShapeDtypeStruct((p, hg_end), F32), jax.ShapeDtypeStruct((p, MIX_W - hg_end), F32)],
        compiler_params=_cparams(("arbitrary",)),
    )(h0b, w_in_t)


def _branch_mix(yh, oa, gates, w_bh, w_ba):
    y_hg = _dot(yh, w_bh, "nn")
    y_att = _dot(oa, w_ba, "nn")
    s1 = jax.nn.sigmoid(gates[:, :D_MODEL].astype(F32))
    s2 = jax.nn.sigmoid(gates[:, D_MODEL:].astype(F32))
    return s1 * y_hg + s2 * y_att, y_hg, y_att, s1, s2


def _mix_out_ln1(yh, oa, h0b, w_in_t, h0, w_bh, w_ba, w_out, g1, b1):
    p = yh.shape[0]
    tr = _tile(p, 320)

    def body(yh_ref, oa_ref, h0b_ref, wi_ref, h0_ref, wbh_ref, wba_ref, wo_ref, g1_ref, b1_ref,
             g_ref, mix_ref, h1_ref, h1b_ref, xh_ref, rs_ref):
        g_ref[...] = _dot(h0b_ref[...], wi_ref[MIX_W:, :], "nt").astype(BF16)
        mixin = _branch_mix(yh_ref[...], oa_ref[...], g_ref[...], wbh_ref[...], wba_ref[...])[0]
        mix_ref[...] = mixin.astype(BF16)
        xhat, rstd = _ln_stats(ALPHA * h0_ref[...] + _dot(mixin, wo_ref[...], "nn"))
        h1 = xhat * g1_ref[...] + b1_ref[...]
        h1_ref[...] = h1
        h1b_ref[...] = h1.astype(BF16)
        xh_ref[...] = xhat
        rs_ref[...] = rstd

    row = lambda w: pl.BlockSpec((tr, w), lambda i: (i, 0))
    const = lambda a: pl.BlockSpec(a.shape, lambda i: (0, 0))
    return pl.pallas_call(
        body, name="mix_out_ln1", grid=(p // tr,),
        in_specs=[row(HG_W), row(ATT_QW), row(D_MODEL),
                  pl.BlockSpec(w_in_t.shape, lambda i: (0, 0), pipeline_mode=pl.Buffered(1)), row(D_MODEL),
                  const(w_bh), const(w_ba), const(w_out), const(g1), const(b1)],
        out_specs=[row(2 * D_MODEL), row(D_MODEL), row(D_MODEL), row(D_MODEL), row(D_MODEL), row(1)],
        out_shape=[jax.ShapeDtypeStruct((p, 2 * D_MODEL), BF16), jax.ShapeDtypeStruct((p, D_MODEL), BF16),
                   jax.ShapeDtypeStruct((p, D_MODEL), F32), jax.ShapeDtypeStruct((p, D_MODEL), BF16),
                   jax.ShapeDtypeStruct((p, D_MODEL), F32), jax.ShapeDtypeStruct((p, 1), F32)],
        compiler_params=_cparams(("arbitrary",)),
    )(yh, oa, h0b, w_in_t, h0, w_bh, w_ba, w_out, g1, b1)


FF_T = D_FF // 2


def _ffn_in_swiglu(h1, w_fi_t):
    p = h1.shape[0]
    tm = _row_tile(p, 640)

    def body(h_ref, w_ref, au_ref, s_ref):
        au = _dot(h_ref[...], w_ref[...], "nt")
        au_ref[...] = au.astype(BF16)
        s_ref[...] = (jax.nn.silu(au[:, :FF_T]) * au[:, FF_T:]).astype(BF16)

    return pl.pallas_call(
        body, name="ffn_in_swiglu", grid=(D_FF // FF_T, p // tm),
        in_specs=[pl.BlockSpec((tm, D_MODEL), lambda j, i: (i, 0)), pl.BlockSpec((2 * FF_T, D_MODEL), lambda j, i: (j, 0))],
        out_specs=[pl.BlockSpec((tm, 2 * FF_T), lambda j, i: (i, j)), pl.BlockSpec((tm, FF_T), lambda j, i: (i, j))],
        out_shape=[jax.ShapeDtypeStruct((p, 2 * D_FF), BF16), jax.ShapeDtypeStruct((p, D_FF), BF16)],
        compiler_params=_cparams(("arbitrary", "arbitrary")),
    )(h1, w_fi_t)


def _ffn_out_loss(s, w_fo, h1, g2, b2, target):
    p = h1.shape[0]
    tr = _row_tile(p, 640)
    k = tr // BLOCK

    def body(*refs):
        s_ref, w_ref, h_ref, g_ref, b_ref = refs[:5]
        dr_ref, drb_ref, loss_ref, dg_ref, db_ref = refs[5 + k:]
        i = pl.program_id(0)
        xhat, rstd = _ln_stats(ALPHA * h_ref[...] + _dot(s_ref[...], w_ref[...], "nn"))
        y = xhat * g_ref[...] + b_ref[...]
        row = i * tr + lax.broadcasted_iota(jnp.int32, (tr, 1), 0)
        tgt = jnp.concatenate([r[...] for r in refs[5:5 + k]], axis=0)
        err = jnp.where(row >= BLOCK, y - tgt, 0.0)
        dr, dg, db = _ln_bwd(err * (1.0 / D_MODEL), xhat, rstd, g_ref[...])
        dr_ref[...] = dr
        drb_ref[...] = dr.astype(BF16)
        e2 = jnp.sum(err * err, axis=0, keepdims=True)
        part = e2[:, 0:BLOCK]
        for j in range(1, D_MODEL // BLOCK):
            part = part + e2[:, j * BLOCK:(j + 1) * BLOCK]
        part = part * (0.5 / D_MODEL)

        @pl.when(i == 0)
        def _():
            loss_ref[...] = part
            dg_ref[...] = dg
            db_ref[...] = db

        @pl.when(i > 0)
        def _():
            loss_ref[...] += part
            dg_ref[...] += dg
            db_ref[...] += db

    vec = pl.BlockSpec((1, D_MODEL), lambda i: (0, 0))
    rowsp = pl.BlockSpec((tr, D_MODEL), lambda i: (i, 0))
    return pl.pallas_call(
        body, name="ffn_out_loss", grid=(p // tr,),
        in_specs=[pl.BlockSpec((tr, D_FF), lambda i: (i, 0)), pl.BlockSpec((D_FF, D_MODEL), lambda i: (0, 0)),
                  rowsp, vec, vec] + _token_streams(tr),
        out_specs=[rowsp, rowsp, pl.BlockSpec((1, BLOCK), lambda i: (0, 0)), vec, vec],
        out_shape=[jax.ShapeDtypeStruct((p, D_MODEL), F32), jax.ShapeDtypeStruct((p, D_MODEL), BF16),
                   jax.ShapeDtypeStruct((1, BLOCK), F32), jax.ShapeDtypeStruct((1, D_MODEL), F32),
                   jax.ShapeDtypeStruct((1, D_MODEL), F32)],
        compiler_params=_cparams(("arbitrary",)),
    )(s, w_fo, h1, g2, b2, *([target] * k))


def _ffn_bwd(dr2, w_fo, au, w_fi_t):
    p = au.shape[0]
    tm = _tile(p, 320)

    def body(d_ref, wo_ref, au_ref, wi_ref, dau_ref, dh_ref):
        d = d_ref[...].astype(BF16)
        dh = ALPHA * d_ref[...]
        for j in range(D_FF // FF_T):
            a_cols = slice(2 * j * FF_T, (2 * j + 1) * FF_T)
            u_cols = slice((2 * j + 1) * FF_T, (2 * j + 2) * FF_T)
            ds = _dot(d, wo_ref[j * FF_T:(j + 1) * FF_T, :], "nt")
            _, vjp = jax.vjp(lambda a, u: jax.nn.silu(a) * u, au_ref[:, a_cols].astype(F32), au_ref[:, u_cols].astype(F32))
            da, du = vjp(ds)
            dau_ref[:, a_cols] = da.astype(BF16)
            dau_ref[:, u_cols] = du.astype(BF16)
            pair = slice(2 * j * FF_T, (2 * j + 2) * FF_T)
            dh = dh + _dot(dau_ref[:, pair], wi_ref[pair, :], "nn")
        dh_ref[...] = dh

    row = lambda w: pl.BlockSpec((tm, w), lambda i: (i, 0))
    kept = lambda a: pl.BlockSpec(a.shape, lambda i: (0, 0), pipeline_mode=pl.Buffered(1))
    return pl.pallas_call(
        body, name="ffn_bwd", grid=(p // tm,),
        in_specs=[row(D_MODEL), kept(w_fo), row(2 * D_FF), kept(w_fi_t)],
        out_specs=[row(2 * D_FF), row(D_MODEL)],
        out_shape=[jax.ShapeDtypeStruct((p, 2 * D_FF), BF16), jax.ShapeDtypeStruct((p, D_MODEL), F32)],
        compiler_params=_cparams(("arbitrary",)),
    )(dr2, w_fo, au, w_fi_t)


def _ln1_mix_bwd(dh1, xhat1, rstd1, g1, yh, oa, gates, w_bh, w_ba, w_out):
    p = yh.shape[0]
    tr = _tile(p, 320)

    def body(dh_ref, xh_ref, rs_ref, g1_ref, yh_ref, oa_ref, g_ref, wbh_ref, wba_ref, wo_ref,
             dr_ref, dyhg_ref, dyat_ref, dgt_ref, dyh_ref, doa_ref, dg_ref, db_ref):
        i = pl.program_id(0)
        dr, dg, db = _ln_bwd(dh_ref[...], xh_ref[...], rs_ref[...], g1_ref[...])
        dr_ref[...] = dr
        d = _dot(dr, wo_ref[...], "nt")
        _, y_hg, y_att, s1, s2 = _branch_mix(yh_ref[...], oa_ref[...], g_ref[...], wbh_ref[...], wba_ref[...])
        dy_hg = d * s1
        dy_att = d * s2
        dyhg_ref[...] = dy_hg.astype(BF16)
        dyat_ref[...] = dy_att.astype(BF16)
        dgt_ref[:, :D_MODEL] = (d * y_hg * s1 * (1.0 - s1)).astype(BF16)
        dgt_ref[:, D_MODEL:] = (d * y_att * s2 * (1.0 - s2)).astype(BF16)
        dyh_ref[...] = _dot(dy_hg, wbh_ref[...], "nt")
        doa_ref[...] = _dot(dy_att, wba_ref[...], "nt")

        @pl.when(i == 0)
        def _():
            dg_ref[...] = dg
            db_ref[...] = db

        @pl.when(i > 0)
        def _():
            dg_ref[...] += dg
            db_ref[...] += db

    row = lambda w: pl.BlockSpec((tr, w), lambda i: (i, 0))
    const = lambda a: pl.BlockSpec(a.shape, lambda i: (0, 0))
    vec = pl.BlockSpec((1, D_MODEL), lambda i: (0, 0))
    return pl.pallas_call(
        body, name="ln1_mix_bwd", grid=(p // tr,),
        in_specs=[row(D_MODEL), row(D_MODEL), row(1), vec, row(HG_W), row(ATT_QW), row(2 * D_MODEL),
                  const(w_bh), const(w_ba), const(w_out)],
        out_specs=[row(D_MODEL), row(D_MODEL), row(D_MODEL), row(2 * D_MODEL), row(HG_W), row(ATT_QW), vec, vec],
        out_shape=[jax.ShapeDtypeStruct((p, D_MODEL), F32), jax.ShapeDtypeStruct((p, D_MODEL), BF16),
                   jax.ShapeDtypeStruct((p, D_MODEL), BF16), jax.ShapeDtypeStruct((p, 2 * D_MODEL), BF16),
                   jax.ShapeDtypeStruct((p, HG_W), F32), jax.ShapeDtypeStruct((p, ATT_QW), F32),
                   jax.ShapeDtypeStruct((1, D_MODEL), F32), jax.ShapeDtypeStruct((1, D_MODEL), F32)],
        compiler_params=_cparams(("arbitrary",)),
    )(dh1, xhat1, rstd1, g1, yh, oa, gates, w_bh, w_ba, w_out)


MIX_W = 4 * HG_W + ATT_QW + 2 * ATT_KVW
ATT_KEYS = BLOCK + 2 * N_META


def _mixers_bwd(proj_hg, proj_att, lbounds, norm_g, lv, states, scores, raw, probs, cos, sin, sinks, dyh, doa,
                parts, swapped):
    p = proj_hg.shape[0]
    nb = p // BLOCK
    n = len(parts)
    kvw = 2 * ATT_KVW
    rev = lambda s: nb - 1 - s
    c_in, c_out, c_shapes, c_sems = _comm_specs(parts, N_PEERS)

    def body(*refs):
        (x_ref, lb_ref, ng_ref, lv_ref, st_ref, a_ref, raw_ref, pr_ref, cur_ref, prev_ref, meta_ref, cc, sc, cp, sp,
         cm, sm, sink_ref, dy_ref, do_ref) = refs[:20]
        part_refs = refs[20:20 + n]
        dx_ref, dlb_ref, dng_ref, dsink_ref = refs[20 + n:24 + n]
        recv_refs = refs[24 + n:24 + 2 * n]
        dcarry_ref, dkv_next_ref, dkv_meta_ref = refs[24 + 2 * n:27 + 2 * n]
        starts, waits = _scatter_behind(part_refs, recv_refs, refs[27 + 2 * n:], swapped)
        step = pl.program_id(0)
        c = rev(step)

        @pl.when(step == 0)
        def _():
            dcarry_ref[...] = jnp.zeros_like(dcarry_ref)
            dkv_next_ref[...] = jnp.zeros_like(dkv_next_ref)
            dkv_meta_ref[...] = jnp.zeros_like(dkv_meta_ref)
            dlb_ref[...] = jnp.zeros_like(dlb_ref)
            dng_ref[...] = jnp.zeros_like(dng_ref)
            dsink_ref[...] = jnp.zeros_like(dsink_ref)
            for start in starts:
                start()

        fh = _first_half(BLOCK)
        qs, kc, vc = _att_load(cur_ref, cc, sc, True)
        _, kp, vp = _att_load(prev_ref, cp, sp, False)
        km, vm = _att_load_meta(meta_ref, cm, sm)
        own4, band4, meta4 = _att_masks(c)
        att0 = 4 * HG_W
        dkm = dkp = dkc = dvm = dvp = dvc = 0.0
        dsinkrows = []
        for g in range(2):
            pr = pr_ref[0, g, :BLOCK, :].astype(F32)
            pr_meta = pr_ref[0, g, BLOCK:BLOCK + N_META, :].astype(F32)
            pr_sink = jnp.max(pr_ref[0, g, BLOCK + N_META:, :].astype(F32), axis=0, keepdims=True)
            _, values_vjp = jax.vjp(lambda *a, g=g: _att_values(*a, g, own4), pr, pr_meta, vc, vp, vm)
            dpr, dpr_meta, dvc_g, dvp_g, dvm_g = values_vjp(
                [do_ref[:, (2 * g + j) * BLOCK:(2 * g + j + 1) * BLOCK] for j in range(2)])
            ds, ds_meta, dsinkrow = _att_probs_bwd(pr, pr_meta, pr_sink, dpr, dpr_meta)
            _, scores_vjp = jax.vjp(lambda *a, g=g: _att_scores(*a, g, own4, band4, meta4),
                                    qs[2 * g], qs[2 * g + 1], kc, kp, km)
            dqa, dqb, dkc_g, dkp_g, dkm_g = scores_vjp((ds, ds_meta))
            for j, dq in enumerate((dqa, dqb)):
                dx_ref[:, att0 + (2 * g + j) * BLOCK:att0 + (2 * g + j + 1) * BLOCK] = _rope_t(
                    dq, cc[...], sc[...], fh).astype(BF16)
            dkm, dkp, dkc = dkm + dkm_g, dkp + dkp_g, dkc + dkc_g
            dvm, dvp, dvc = dvm + dvm_g, dvp + dvp_g, dvc + dvc_g
            dsinkrows.append(dsinkrow)
        ds0, ds1 = dsinkrows
        dkv_meta_ref[:, :BLOCK] += _rope_t(dkm, cm[PAD:BLOCK, :], sm[PAD:BLOCK, :], _first_half(N_META))
        dkv_meta_ref[:, BLOCK:] += dvm
        last = jnp.where(c == 0, 1.0, 0.0)
        to_meta_rows = lambda m: jnp.concatenate([jnp.zeros((PAD, BLOCK), F32), last * m], axis=0)
        dk = _rope_t(dkc, cc[...], sc[...], fh) + dkv_next_ref[:, :BLOCK] + to_meta_rows(dkv_meta_ref[:, :BLOCK])
        dv = dvc + dkv_next_ref[:, BLOCK:] + to_meta_rows(dkv_meta_ref[:, BLOCK:])
        dx_ref[:, att0 + ATT_QW:att0 + ATT_QW + ATT_KVW] = dk.astype(BF16)
        dx_ref[:, att0 + ATT_QW + ATT_KVW:] = dv.astype(BF16)
        dkv_next_ref[:, :BLOCK] = _rope_t(dkp, cp[...], sp[...], fh)
        dkv_next_ref[:, BLOCK:] = dvp
        sink_rows = []
        for dsg in (ds0, ds1):
            for j in range(4):
                tot = jnp.sum(dsg[:, j * BLOCK:(j + 1) * BLOCK], axis=1, keepdims=True)
                sink_rows.append(jnp.broadcast_to(tot, (1, BLOCK)))
        dsink_ref[...] += jnp.concatenate(sink_rows, axis=0)

        valid = (c * BLOCK + lax.broadcasted_iota(jnp.int32, (BLOCK, 1), 0)) >= PAD
        (logf, k), gates_vjp = jax.vjp(lambda hf, a0, a1: _hgrn_gates(hf, a0, a1, valid),
                                       x_ref[:, HG_W:2 * HG_W], lb_ref[0:1, :], lb_ref[1:2, :])
        lvv = lv_ref[...]
        e = _split_dot(lvv, logf, "nn")
        dng = jnp.zeros((1, BLOCK), F32)
        dk, dseg = [], []
        for h in range(HG_HEADS):
            sl = lambda part: x_ref[:, part * HG_W + h * BLOCK: part * HG_W + (h + 1) * BLOCK]
            hs = slice(h * BLOCK, (h + 1) * BLOCK)
            seg = _seg_blocks(e, h)
            _, norm_vjp = jax.vjp(_hgrn_norm, raw_ref[:, hs], sl(3), ng_ref[...])
            draw, dhg, dngh = norm_vjp(dy_ref[:, hs])
            _, mix_vjp = jax.vjp(_hgrn_mix, sl(0), k[:, hs], sl(2), st_ref[0, h], a_ref[0, h].astype(F32), *seg[:3])
            dhq, dkh, dhi, dst, da, *dseg_mix = mix_vjp((draw, dcarry_ref[h]))
            _, scores_vjp = jax.vjp(_hgrn_scores, sl(0), k[:, hs], *seg[3:])
            dhq2, dkh2, *dseg_lvl = scores_vjp(da)
            for part, val in ((0, dhq + dhq2), (2, dhi), (3, dhg)):
                dx_ref[:, part * HG_W + h * BLOCK: part * HG_W + (h + 1) * BLOCK] = val.astype(BF16)
            dk.append(dkh + dkh2)
            dseg.append(jnp.concatenate(dseg_mix + dseg_lvl, axis=0))
            dng = dng + dngh
            dcarry_ref[h] = dst
        dlogf = _split_dot(lvv, jnp.concatenate(dseg, axis=1), "tn")
        dhf, da0, da1 = gates_vjp((dlogf, jnp.concatenate(dk, axis=1)))
        dx_ref[:, HG_W:2 * HG_W] = dhf.astype(BF16)
        dlb_ref[0:1, :] += da0
        dlb_ref[1:2, :] += da1
        dng_ref[...] += dng

        @pl.when(step == nb - 1)
        def _():
            for wait in waits:
                wait()

    const = lambda shape: pl.BlockSpec(shape, lambda s: (0,) * len(shape))
    per_head = pl.BlockSpec((1, HG_HEADS, BLOCK, BLOCK), lambda s: (rev(s), 0, 0, 0))
    return pl.pallas_call(
        body, name="mixers_bwd", grid=(nb,),
        in_specs=[pl.BlockSpec((BLOCK, 4 * HG_W), lambda s: (rev(s), 0)), const((2, HG_W)), const((1, BLOCK)),
                  const(lv.shape), per_head, per_head, pl.BlockSpec((BLOCK, HG_W), lambda s: (rev(s), 0)),
                  pl.BlockSpec((1, 2, ATT_KEYS, 4 * BLOCK), lambda s: (rev(s), 0, 0, 0))]
        + _att_specs(rev)
        + [pl.BlockSpec((BLOCK, HG_W), lambda s: (rev(s), 0)), pl.BlockSpec((BLOCK, ATT_QW), lambda s: (rev(s), 0))]
        + c_in,
        out_specs=[pl.BlockSpec((BLOCK, MIX_W), lambda s: (rev(s), 0)), const((2, HG_W)), const((1, BLOCK)),
                   const((ATT_HEADS, BLOCK))] + c_out,
        out_shape=[jax.ShapeDtypeStruct((p, MIX_W), BF16), jax.ShapeDtypeStruct((2, HG_W), F32),
                   jax.ShapeDtypeStruct((1, BLOCK), F32), jax.ShapeDtypeStruct((ATT_HEADS, BLOCK), F32)] + c_shapes,
        scratch_shapes=[pltpu.VMEM((HG_HEADS, BLOCK, BLOCK), F32), pltpu.VMEM((BLOCK, kvw), F32),
                        pltpu.VMEM((N_META, kvw), F32)] + c_sems,
        compiler_params=_cparams(("arbitrary",)),
    )(proj_hg, lbounds, norm_g, lv, states, scores, raw, probs, proj_att, proj_att, proj_att, cos, sin, cos, sin,
      cos, sin, sinks, dyh, doa, *parts)


_HBM = pl.BlockSpec(memory_space=pltpu.HBM)
_SEM = pl.BlockSpec(memory_space=pltpu.SEMAPHORE)
_ORDERED_BY_DATA = pltpu.CompilerParams(has_side_effects=pltpu.SideEffectType.DATAFLOW_SIDE_EFFECTING)


def _exchange_copies(part_ref, land_ref, send_sems, recv_sems):
    place = _place()
    return [pltpu.make_async_remote_copy(
        src_ref=part_ref.at[_slot(_peer(place, flip), False)], dst_ref=land_ref.at[r], send_sem=send_sems.at[r],
        recv_sem=recv_sems.at[r], device_id=_peer(place, flip), device_id_type=MESH) for r, flip in enumerate(_FLIPS)]


def _exchange_start(parts, name):
    def body(part_ref, land_ref, send_sems, recv_sems, part_thru, land_thru, token):
        for cp in _exchange_copies(part_ref, land_ref, send_sems, recv_sems):
            cp.start()
        token[...] = jnp.zeros_like(token)

    land = (N_PEERS,) + parts.shape[1:]
    return pl.pallas_call(
        body, name=name,
        out_shape=(pltpu.SemaphoreType.DMA((N_PEERS,)), pltpu.SemaphoreType.DMA((N_PEERS,)),
                   pltpu.HBM(parts.shape, parts.dtype), pltpu.HBM(land, parts.dtype), jax.ShapeDtypeStruct((8, BLOCK), F32)),
        in_specs=(_HBM, _HBM), out_specs=(_SEM, _SEM, _HBM, _HBM, pl.BlockSpec(memory_space=pltpu.VMEM)),
        input_output_aliases={0: 2, 1: 3}, compiler_params=_ORDERED_BY_DATA,
    )(pltpu.with_memory_space_constraint(parts, pltpu.HBM),
      pltpu.with_memory_space_constraint(lax.empty(land, parts.dtype), pltpu.HBM))


def _exchange_wait(send_sems, recv_sems, part_thru, land_thru, after, name):
    def body(part_ref, land_ref, send_sems, recv_sems, after_ref, part_out, land_out):
        for cp in _exchange_copies(part_ref, land_ref, send_sems, recv_sems):
            cp.wait_send()
            cp.wait_recv()

    return pl.pallas_call(
        body, name=name,
        out_shape=(pltpu.HBM(part_thru.shape, part_thru.dtype), pltpu.HBM(land_thru.shape, land_thru.dtype)),
        in_specs=(_HBM, _HBM, _SEM, _SEM, pl.BlockSpec(memory_space=pl.ANY)), out_specs=(_HBM, _HBM),
        input_output_aliases={0: 0, 1: 1}, compiler_params=_ORDERED_BY_DATA,
    )(part_thru, land_thru, send_sems, recv_sems, after)


def _embed_bwd(dmix, dgates, w_in_t, dr1, xhat0, rstd0, g0):
    p = dmix.shape[0]
    tm = _row_tile(p, 640)
    nm = p // tm

    def body(a_ref, g_ref, w_ref, dr_ref, xh_ref, rs_ref, g0_ref, gx_ref, lead_ref, dg_ref, db_ref, buf_ref, sem):
        i = pl.program_id(0)
        first = pltpu.make_async_copy(buf_ref.at[0, pl.ds(BLOCK, tm - BLOCK)], gx_ref.at[pl.ds(0, tm - BLOCK)],
                                      sem.at[0])
        later = lambda t: pltpu.make_async_copy(buf_ref.at[t % 2], gx_ref.at[pl.ds(t * tm - BLOCK, tm)], sem.at[t % 2])

        @pl.when(i == 2)
        def _():
            first.wait()

        @pl.when(i > 2)
        def _():
            later(i - 2).wait()

        dh0 = (ALPHA * dr_ref[...] + _dot(a_ref[...], w_ref[:MIX_W, :], "nn")
               + _dot(g_ref[...], w_ref[MIX_W:, :], "nn"))
        row = i * tm + lax.broadcasted_iota(jnp.int32, (tm, 1), 0)
        dx, dg, db = _ln_bwd(jnp.where(row >= PAD, dh0, 0.0), xh_ref[...], rs_ref[...], g0_ref[...])
        buf_ref[i % 2] = dx

        @pl.when(i == 0)
        def _():
            lead_ref[...] = dx[:BLOCK]
            dg_ref[...] = dg
            db_ref[...] = db
            first.start()

        @pl.when(i > 0)
        def _():
            dg_ref[...] += dg
            db_ref[...] += db
            later(i).start()

        @pl.when(i == nm - 1)
        def _():
            for t in (nm - 2, nm - 1):
                if t >= 0:
                    (first if t == 0 else later(t)).wait()

    row = lambda w: pl.BlockSpec((tm, w), lambda i: (i, 0))
    vec = pl.BlockSpec((1, D_MODEL), lambda i: (0, 0))
    return pl.pallas_call(
        body, name="embed_bwd", grid=(nm,),
        in_specs=[row(dmix.shape[1]), row(dgates.shape[1]),
                  pl.BlockSpec(w_in_t.shape, lambda i: (0, 0), pipeline_mode=pl.Buffered(1)), row(D_MODEL), row(D_MODEL),
                  row(1), vec],
        out_specs=[pl.BlockSpec(memory_space=pl.ANY), pl.BlockSpec((BLOCK, D_MODEL), lambda i: (0, 0)), vec, vec],
        out_shape=[jax.ShapeDtypeStruct((p - BLOCK, D_MODEL), F32), jax.ShapeDtypeStruct((BLOCK, D_MODEL), F32),
                   jax.ShapeDtypeStruct((1, D_MODEL), F32), jax.ShapeDtypeStruct((1, D_MODEL), F32)],
        scratch_shapes=[pltpu.VMEM((2, tm, D_MODEL), F32), pltpu.SemaphoreType.DMA((2,))],
        compiler_params=_cparams(("arbitrary",)),
    )(dmix, dgates, w_in_t, dr1, xhat0, rstd0, g0)


_LATE = ("w_branch_hg", "w_branch_attn", "w_out", "w_ffn_in", "w_ffn_out")
_TRANSPOSED = ("w_in", "w_ffn_in")
_COLUMN_SHARDED = ("meta_tokens", "w_branch_hg", "w_branch_attn")
_SWAPPED = ("w_ffn_in",)


def _whole(name, gathered):
    _, r, c = gathered.shape
    if name in _COLUMN_SHARDED:
        return jnp.transpose(gathered, (1, 0, 2)).reshape(r, N_DEV * c)
    return gathered.reshape(N_DEV * r, c)


def _slots(name, whole):
    r, c = whole.shape
    if name in _COLUMN_SHARDED:
        return jnp.transpose(whole.reshape(r, N_DEV, c // N_DEV), (1, 0, 2))
    return whole.reshape(N_DEV, r // N_DEV, c)


def _device_step(x, target, meta_shard, ln_emb_g, ln_emb_b, w_in_shard, lbounds, norm_g, sinks, late_shards,
                 ln1_g, ln1_b, ln2_g, ln2_b):
    p = x.shape[0] + BLOCK
    lv = _level_stack()
    cos, sin = _rope_tables(p)
    swapped = [n in _SWAPPED for n in _LATE]

    h0, h0b, xhat0, rstd0, _, g_win = _embed_ln(x, meta_shard, w_in_shard, ln_emb_g, ln_emb_b)
    w_in = _whole("w_in", g_win)
    proj_hg, proj_att = _in_proj(h0b, w_in)
    yh, oa, states, scores, raw, probs, *gathered = _mixers_fwd(
        proj_hg, proj_att, lbounds, norm_g, lv, cos, sin, sinks, late_shards, swapped)
    w_bh, w_ba, w_out, w_fi, w_fo = [_whole(n, g) for n, g in zip(_LATE, gathered)]
    gates, mixin, h1, h1b, xhat1, rstd1 = _mix_out_ln1(yh, oa, h0b, w_in, h0, w_bh, w_ba, w_out, ln1_g, ln1_b)
    au, sw = _ffn_in_swiglu(h1b, w_fi)
    dr2, dr2b, loss_part, dg2, db2 = _ffn_out_loss(sw, w_fo, h1, ln2_g, ln2_b, target)

    mtn = functools.partial(_tiled_matmul_tn, tm=_row_tile(p, 1664), out_dtype=BF16)
    d_wfo = mtn(sw, dr2b, tk=FF_T, tn=D_MODEL, name="grad_w_ffn_out")
    dau, dh1 = _ffn_bwd(dr2, w_fo, au, w_fi)
    d_wfi = _weight_grad_t([dau], h1b, tk=4 * BLOCK, name="grad_w_ffn_in")
    dr1, dy_hg, dy_att, dgates, dyh, doa, dg1, db1 = _ln1_mix_bwd(
        dh1, xhat1, rstd1, ln1_g, yh, oa, gates, w_bh, w_ba, w_out)
    d_wout = mtn(mixin, dr1, tk=D_MODEL, tn=D_MODEL, name="grad_w_out")
    d_wbh = mtn(yh, dy_hg, tk=HG_W, tn=D_MODEL, name="grad_w_branch_hg")
    d_wba = mtn(oa, dy_att, tk=ATT_QW, tn=D_MODEL, name="grad_w_branch_attn")
    late_parts = [_slots(n, g) for n, g in zip(_LATE, (d_wbh, d_wba, d_wout, d_wfi, d_wfo))]
    dmix, d_lb, d_ng, d_sink, *late_recv = _mixers_bwd(
        proj_hg, proj_att, lbounds, norm_g, lv, states, scores, raw, probs, cos, sin, sinks, dyh, doa, late_parts,
        swapped)
    d_win = _weight_grad_t([dmix, dgates], h0b, tk=2 * BLOCK, name="grad_w_in")
    *win_flight, token = _exchange_start(_slots("w_in", d_win), "w_in_grads_start")
    grad_x, dlead, dg0, db0 = _embed_bwd(dmix, dgates, w_in, dr1, xhat0, rstd0, ln_emb_g + token[0:1, 0:1])

    small = dict(ln_emb_g=dg0, ln_emb_b=db0, hg_lower_bounds=d_lb, hg_norm_g=d_ng, ln1_g=dg1, ln1_b=db1, ln2_g=dg2,
                 ln2_b=db2)
    big = dict(zip(_LATE, zip(late_parts, late_recv)))
    return _pack_small(small, d_sink, dlead, loss_part), grad_x, big, win_flight


def _all_gather(arrs, dtypes, name):
    n = len(arrs)

    def body(*refs):
        ins, outs, stages = refs[:n], refs[n:2 * n], refs[2 * n:3 * n]
        send_sems, recv_sems, local_sems = refs[3 * n:]
        x, y, c = _place()
        sibling = (x, y, 1 - c)
        chips = [(1 - x, y), (x, 1 - y), (1 - x, 1 - y)]
        slot = lambda px, py, pc: 4 * px + 2 * py + pc

        def copy(w, k, block, to, from_stage=False):
            return pltpu.make_async_remote_copy(
                src_ref=stages[w] if from_stage else outs[w].at[slot(*block)], dst_ref=outs[w].at[slot(*block)],
                send_sem=send_sems.at[w, k], recv_sem=recv_sems.at[w, k], device_id=to, device_id_type=MESH)

        mine, first, passed = [], [], []
        for w in range(n):
            stages[w][...] = ins[w][...].astype(dtypes[w])
            mine.append(pltpu.make_async_copy(stages[w], outs[w].at[slot(x, y, c)], local_sems.at[w]))
            mine[-1].start()
        for w in range(n):
            first.append(copy(w, 0, (x, y, c), sibling, from_stage=True))
            first += [copy(w, 1 + j, (x, y, c), (*chip, c), from_stage=True) for j, chip in enumerate(chips)]
        for cp in first:
            cp.start()
        for j, chip in enumerate(chips):
            for w in range(n):
                copy(w, 1 + j, (*chip, c), (x, y, c)).wait_recv()
                passed.append(copy(w, 4 + j, (*chip, c), sibling))
                passed[-1].start()
        for w in range(n):
            copy(w, 0, sibling, (x, y, c)).wait_recv()
            for j, chip in enumerate(chips):
                copy(w, 4 + j, (*chip, 1 - c), (x, y, c)).wait_recv()
        for cp in first + passed:
            cp.wait_send()
        for cp in mine:
            cp.wait()

    return pl.pallas_call(
        body, name=name,
        in_specs=[pl.BlockSpec(memory_space=pltpu.VMEM)] * n,
        out_specs=[pl.BlockSpec(memory_space=pl.ANY)] * n,
        out_shape=[jax.ShapeDtypeStruct((N_DEV,) + a.shape, dt) for a, dt in zip(arrs, dtypes)],
        scratch_shapes=[pltpu.VMEM(a.shape, dt) for a, dt in zip(arrs, dtypes)]
        + [pltpu.SemaphoreType.DMA((n, 7)), pltpu.SemaphoreType.DMA((n, 7)), pltpu.SemaphoreType.DMA((n,))],
        compiler_params=pltpu.CompilerParams(vmem_limit_bytes=VMEM_LIMIT_BYTES),
    )(*arrs)


def _cast_shards(arrs):
    def body(*refs):
        for src, dst in zip(refs[:len(arrs)], refs[len(arrs):]):
            dst[...] = src[...].astype(BF16)

    return pl.pallas_call(body, name="cast_shards", out_shape=[jax.ShapeDtypeStruct(a.shape, BF16) for a in arrs],
                          compiler_params=pltpu.CompilerParams(vmem_limit_bytes=VMEM_LIMIT_BYTES))(*arrs)


def _shard_rows(rows):
    return rows if rows <= 512 else max(t for t in range(16, 353, 16) if rows % t == 0)


def _adamw_math(w, g, m, v):
    m = ADAM_B1 * m + (1.0 - ADAM_B1) * g
    v = ADAM_B2 * v + (1.0 - ADAM_B2) * (g * g)
    m_hat = m / (1.0 - ADAM_B1 ** ADAM_STEP)
    v_hat = v / (1.0 - ADAM_B2 ** ADAM_STEP)
    delta = -ADAM_LR * (m_hat / (jnp.sqrt(v_hat) + ADAM_EPS) + ADAM_WD * w)
    return delta, m, v


def _reduce_adamw(parts, recv, own_slot, w, m, v, name):
    r, cdim = w.shape
    tr = _shard_rows(r)

    def body(idx_ref, p_ref, r_ref, w_ref, m_ref, v_ref, g_out, d_out, m_out, v_out):
        g = p_ref[0].astype(F32)
        for j in range(N_PEERS):
            g = g + r_ref[j].astype(F32)
        d, mn, vn = _adamw_math(w_ref[...], g, m_ref[...], v_ref[...])
        g_out[...] = g
        d_out[...] = d
        m_out[...] = mn
        v_out[...] = vn

    flat = pl.BlockSpec((tr, cdim), lambda i, idx_ref: (i, 0))
    return pl.pallas_call(
        body, name=name,
        grid_spec=pltpu.PrefetchScalarGridSpec(
            num_scalar_prefetch=1, grid=(r // tr,),
            in_specs=[pl.BlockSpec((1, tr, cdim), lambda i, idx_ref: (idx_ref[0], i, 0)),
                      pl.BlockSpec((N_PEERS, tr, cdim), lambda i, idx_ref: (0, i, 0)), flat, flat, flat],
            out_specs=[flat] * 4),
        out_shape=[jax.ShapeDtypeStruct((r, cdim), F32)] * 4,
        compiler_params=_cparams(("arbitrary",)),
    )(own_slot, parts, recv, w, m, v)


def _adamw_plain(w, g, m, v, name):
    def body(w_ref, g_ref, m_ref, v_ref, d_out, m_out, v_out):
        d_out[...], m_out[...], v_out[...] = _adamw_math(w_ref[...], g_ref[...], m_ref[...], v_ref[...])

    return pl.pallas_call(body, name=name, out_shape=[jax.ShapeDtypeStruct(w.shape, F32)] * 3)(w, g, m, v)


_SMALL = (("ln_emb_g", (1, D_MODEL)), ("ln_emb_b", (1, D_MODEL)), ("hg_lower_bounds", (2, HG_W)),
          ("hg_norm_g", (1, BLOCK)), ("attn_sinks", (1, ATT_HEADS)), ("ln1_g", (1, D_MODEL)), ("ln1_b", (1, D_MODEL)),
          ("ln2_g", (1, D_MODEL)), ("ln2_b", (1, D_MODEL)))
_SMALL_ROW, _LOSS_ROW = {}, 0
for _name, (_rows, _) in _SMALL:
    _SMALL_ROW[_name], _LOSS_ROW = _LOSS_ROW, _LOSS_ROW + _rows
_META_ROW = 16
SMALL_ROWS = _META_ROW + N_META
assert _LOSS_ROW < _META_ROW


def _pack_small(grads, d_sink, dlead, loss_part):
    names = [n for n, _ in _SMALL if n != "attn_sinks"]

    def body(*refs):
        ins = dict(zip(names, refs))
        sink_ref, lead_ref, loss_ref, o_ref = refs[len(names):]
        o_ref[...] = jnp.zeros_like(o_ref)
        for name, (rows, cols) in _SMALL:
            if name != "attn_sinks":
                o_ref[_SMALL_ROW[name]:_SMALL_ROW[name] + rows, :cols] = ins[name][...]
        head = lax.broadcasted_iota(jnp.int32, (ATT_HEADS, BLOCK), 0)
        lane = lax.broadcasted_iota(jnp.int32, (ATT_HEADS, BLOCK), 1)
        o_ref[_SMALL_ROW["attn_sinks"]:_SMALL_ROW["attn_sinks"] + 1, :BLOCK] = jnp.sum(
            jnp.where(head == lane, sink_ref[...], 0.0), axis=0, keepdims=True)
        o_ref[_LOSS_ROW:_LOSS_ROW + 1, :BLOCK] = loss_ref[...]
        o_ref[_META_ROW:, :] = lead_ref[PAD:BLOCK, :]

    return pl.pallas_call(body, name="pack_small", out_shape=jax.ShapeDtypeStruct((SMALL_ROWS, D_MODEL), F32))(
        *[grads[n] for n in names], d_sink, dlead, loss_part)


def _small_reduce_adamw(gathered, weights, mom1, mom2):
    n = len(_SMALL)

    def body(*refs):
        g_ref, w_refs, m_refs, v_refs = refs[0], refs[1:1 + n], refs[1 + n:1 + 2 * n], refs[1 + 2 * n:1 + 3 * n]
        outs = refs[1 + 3 * n:1 + 7 * n]
        meta_out, loss_out, sum_ref = refs[1 + 7 * n:]
        total = g_ref[0]
        for s in range(1, N_DEV):
            total = total + g_ref[s]
        sum_ref[...] = total
        for i, (name, (rows, cols)) in enumerate(_SMALL):
            g = sum_ref[_SMALL_ROW[name]:_SMALL_ROW[name] + rows, :cols]
            d, mn, vn = _adamw_math(w_refs[i][...], g, m_refs[i][...], v_refs[i][...])
            for out, val in zip(outs[4 * i:4 * i + 4], (g, d, mn, vn)):
                out[...] = val
        meta_out[...] = sum_ref[_META_ROW:, :]
        loss_out[...] = jnp.broadcast_to(jnp.sum(sum_ref[_LOSS_ROW:_LOSS_ROW + 1, :BLOCK]), (1, BLOCK))

    per_param = [jax.ShapeDtypeStruct(shape, F32) for _, shape in _SMALL for _ in range(4)]
    res = pl.pallas_call(
        body, name="small_reduce_adamw",
        out_shape=per_param + [jax.ShapeDtypeStruct((N_META, D_MODEL), F32), jax.ShapeDtypeStruct((1, BLOCK), F32)],
        scratch_shapes=[pltpu.VMEM((SMALL_ROWS, D_MODEL), F32)],
    )(gathered, *[d[name] for d in (weights, mom1, mom2) for name, _ in _SMALL])
    return {name: res[4 * i:4 * i + 4] for i, (name, _) in enumerate(_SMALL)}, res[-2], res[-1]


_WEIGHTS = ("meta_tokens", "ln_emb_g", "ln_emb_b", "w_in", "hg_lower_bounds", "hg_norm_g", "attn_sinks",
            "w_branch_hg", "w_branch_attn", "w_out", "ln1_g", "ln1_b", "w_ffn_in", "w_ffn_out", "ln2_g", "ln2_b")


def kernel(x, meta_tokens, ln_emb_g, ln_emb_b, w_in, hg_lower_bounds, hg_norm_g, attn_sinks, w_branch_hg, w_branch_attn, w_out, ln1_g, ln1_b, w_ffn_in, w_ffn_out, ln2_g, ln2_b, loss_target, m_meta_tokens, m_ln_emb_g, m_ln_emb_b, m_w_in, m_hg_lower_bounds, m_hg_norm_g, m_attn_sinks, m_w_branch_hg, m_w_branch_attn, m_w_out, m_ln1_g, m_ln1_b, m_w_ffn_in, m_w_ffn_out, m_ln2_g, m_ln2_b, v_meta_tokens, v_ln_emb_g, v_ln_emb_b, v_w_in, v_hg_lower_bounds, v_hg_norm_g, v_attn_sinks, v_w_branch_hg, v_w_branch_attn, v_w_out, v_ln1_g, v_ln1_b, v_w_ffn_in, v_w_ffn_out, v_ln2_g, v_ln2_b):
    given = dict(locals())
    weights = {n: given[n] for n in _WEIGHTS}
    mom1 = {n: given["m_" + n] for n in _WEIGHTS}
    mom2 = {n: given["v_" + n] for n in _WEIGHTS}
    shard2d = lambda n, a: a.reshape(a.shape[-2:]).T if n in _TRANSPOSED else a.reshape(a.shape[-2:])

    w_in_shard, *late_shards = _cast_shards([shard2d(n, weights[n]) for n in ("w_in",) + _LATE])
    packed, grad_x, big, win_flight = _device_step(
        x[0], loss_target[0], meta_tokens, ln_emb_g.reshape(1, -1), ln_emb_b.reshape(1, -1), w_in_shard,
        hg_lower_bounds, hg_norm_g, attn_sinks, late_shards, ln1_g, ln1_b, ln2_g, ln2_b)

    place = _place()
    out = {}

    def reduce_adamw(n, parts, recv):
        own = _slot(place, n in _SWAPPED).astype(jnp.int32).reshape(1)
        res = _reduce_adamw(parts, recv, own, shard2d(n, weights[n]), shard2d(n, mom1[n]), shard2d(n, mom2[n]),
                            "adamw_" + n)
        out[n] = [(r.T if n in _TRANSPOSED else r).reshape(weights[n].shape) for r in res]

    for n, (parts, recv) in big.items():
        reduce_adamw(n, parts, recv)

    all_small, = _all_gather([packed], [F32], "gather_small")
    as_2d = lambda d: {n: d[n].reshape(shape) for n, shape in _SMALL}
    small_out, meta_whole, loss_row = _small_reduce_adamw(all_small, as_2d(weights), as_2d(mom1), as_2d(mom2))
    for n, res in small_out.items():
        out[n] = [r.reshape(weights[n].shape) for r in res]
    loss = loss_row[0, 0]
    g_meta_mine = lax.dynamic_index_in_dim(meta_whole.reshape(N_META, N_DEV, D_MODEL // N_DEV), _slot(place, False),
                                           axis=1, keepdims=False)
    out["meta_tokens"] = [g_meta_mine, *_adamw_plain(meta_tokens, g_meta_mine, m_meta_tokens, v_meta_tokens,
                                                     "adamw_meta")]

    reduce_adamw("w_in", *_exchange_wait(*win_flight, after=all_small, name="w_in_grads_wait"))

    return (loss, grad_x[None], *[out[n][0] for n in _WEIGHTS], *[out[n][1] for n in _WEIGHTS],
            *[out[n][2] for n in _WEIGHTS], *[out[n][3] for n in _WEIGHTS])
```

```python
import functools

import numpy as np
import jax
import jax.numpy as jnp
from jax import lax
from jax.experimental import pallas as pl
from jax.experimental.pallas import tpu as pltpu

F32 = jnp.float32
BF16 = jnp.bfloat16

D_MODEL = 1024
N_META = 16
BLOCK = 128
PAD = BLOCK - N_META
HG_HEADS = 4
HG_W = 512
ATT_HEADS = 8
HEAD_DIM = 64
ATT_QW = 512
ATT_KVW = 128
D_FF = 2816
EPS = 1e-5
ALPHA = 2.0 ** 0.25
ROPE_THETA = 10000.0
N_DEV = 8

ADAM_LR = 0.001
ADAM_B1 = 0.9
ADAM_B2 = 0.999
ADAM_EPS = 1e-08
ADAM_WD = 0.01
ADAM_STEP = 10

VMEM_LIMIT_BYTES = 56 * 1024 * 1024
MESH = pl.DeviceIdType.MESH

_LEVELS = (64, 32, 16, 8, 4, 2, 1)


def _cparams(sem):
    return pltpu.CompilerParams(dimension_semantics=sem, vmem_limit_bytes=VMEM_LIMIT_BYTES)


def _row_tile(rows, target):
    nb = rows // BLOCK
    best = 1
    for d in range(1, nb + 1):
        if nb % d == 0 and d * BLOCK <= target:
            best = d
    return best * BLOCK


_DN = {"nn": (((1,), (0,)), ((), ())), "nt": (((1,), (1,)), ((), ())), "tn": (((0,), (0,)), ((), ()))}


def _dot(a, b, form):
    return lax.dot_general(a.astype(BF16), b.astype(BF16), _DN[form], preferred_element_type=F32)


@functools.partial(jax.custom_vjp, nondiff_argnums=(2,))
def _mm(a, b, form):
    return _dot(a, b, form)


def _mm_fwd(a, b, form):
    a, b = a.astype(BF16), b.astype(BF16)
    return _dot(a, b, form), (a, b)


def _mm_bwd(form, res, g):
    a, b = res
    if form == "nn":
        return _dot(g, b, "nt"), _dot(a, g, "tn")
    if form == "nt":
        return _dot(g, b, "nn"), _dot(g, a, "tn")
    return _dot(b, g, "nt"), _dot(a, g, "nn")


_mm.defvjp(_mm_fwd, _mm_bwd)


def _split_dot(lv, x, form):
    return lax.dot_general(lv, x.astype(BF16), _DN[form], preferred_element_type=F32)


@jax.custom_vjp
def _swap_halves(x):
    return pltpu.roll(x, 64, 1)


_swap_halves.defvjp(lambda x: (pltpu.roll(x, 64, 1), None), lambda _, g: (pltpu.roll(g, 64, 1),))


def _tiled_matmul_tn(a, b, *, tm, tk, tn, out_dtype, name):
    m, k = a.shape
    n = b.shape[1]
    assert m % tm == 0 and k % tk == 0 and n % tn == 0, (name, a.shape, b.shape, tm, tk, tn)
    nm = m // tm

    def body(a_ref, b_ref, o_ref, acc_ref):
        mi = pl.program_id(2)

        @pl.when(mi == 0)
        def _():
            acc_ref[...] = jnp.zeros_like(acc_ref)

        acc_ref[...] += _dot(a_ref[...], b_ref[...], "tn")

        @pl.when(mi == nm - 1)
        def _():
            o_ref[...] = acc_ref[...].astype(out_dtype)

    return pl.pallas_call(
        body, name=name, grid=(k // tk, n // tn, nm),
        in_specs=[pl.BlockSpec((tm, tk), lambda kk, j, i: (i, kk)), pl.BlockSpec((tm, tn), lambda kk, j, i: (i, j))],
        out_specs=pl.BlockSpec((tk, tn), lambda kk, j, i: (kk, j)),
        out_shape=jax.ShapeDtypeStruct((k, n), out_dtype),
        scratch_shapes=[pltpu.VMEM((tk, tn), F32)],
        compiler_params=_cparams(("arbitrary", "arbitrary", "arbitrary")),
    )(a, b)


def _weight_grad_t(cots, h, *, tk, name):
    p, d = h.shape
    steps = [c.shape[1] // tk for c in cots]
    assert all(c.shape == (p, n * tk) for c, n in zip(cots, steps)), (name, [c.shape for c in cots], tk)
    first = [sum(steps[:i]) for i in range(len(cots))]

    def body(*refs):
        h_ref, o_ref = refs[len(cots)], refs[len(cots) + 1]
        k = pl.program_id(0)
        for c_ref, lo, n in zip(refs, first, steps):
            @pl.when((k >= lo) & (k < lo + n))
            def _(c_ref=c_ref):
                o_ref[...] = _dot(c_ref[...], h_ref[...], "tn").astype(BF16)

    cot_spec = lambda lo, n: pl.BlockSpec((p, tk), lambda k: (0, jnp.clip(k - lo, 0, n - 1)))
    return pl.pallas_call(
        body, name=name, grid=(sum(steps),),
        in_specs=[cot_spec(lo, n) for lo, n in zip(first, steps)]
                 + [pl.BlockSpec((p, d), lambda k: (0, 0), pipeline_mode=pl.Buffered(1))],
        out_specs=pl.BlockSpec((tk, d), lambda k: (k, 0)),
        out_shape=jax.ShapeDtypeStruct((sum(steps) * tk, d), BF16),
        compiler_params=_cparams(("arbitrary",)),
    )(*cots, h)


def _ln_stats(r):
    mu = jnp.mean(r, axis=-1, keepdims=True)
    xc = r - mu
    var = jnp.mean(xc * xc, axis=-1, keepdims=True)
    rstd = lax.rsqrt(var + EPS)
    return xc * rstd, rstd


def _ln_bwd(dy, xhat, rstd, g):
    dxhat = dy * g
    m1 = jnp.mean(dxhat, axis=-1, keepdims=True)
    m2 = jnp.mean(dxhat * xhat, axis=-1, keepdims=True)
    dr = rstd * (dxhat - m1 - xhat * m2)
    return dr, jnp.sum(dy * xhat, axis=0, keepdims=True), jnp.sum(dy, axis=0, keepdims=True)


N_SEG = 3 + len(_LEVELS)


def _level_stack():
    t = np.arange(BLOCK)[:, None]
    r = np.arange(BLOCK)[None, :]
    mats = [r <= t, r > t, np.ones((BLOCK, BLOCK), bool)]
    for h in _LEVELS:
        same = (t // (2 * h)) == (r // (2 * h))
        up_t, up_r = (t % (2 * h)) >= h, (r % (2 * h)) >= h
        mats.append(same & ((up_t & up_r & (r <= t)) | (~up_t & ~up_r & (r > t))))
    return jnp.asarray(np.concatenate(mats, axis=0).astype(np.float32), dtype=BF16)


def _hgrn_gates(hf, a0, a1, valid):
    lb = jax.nn.sigmoid(a0 - a1)
    fg = lb + (1.0 - lb) * jax.nn.sigmoid(hf)
    return jnp.where(valid, jnp.log(fg), 0.0), jnp.where(valid, 1.0 - fg, 0.0)


def _hgrn_scores(hq, k, *levels):
    q = jax.nn.silu(hq)
    rows = lax.broadcasted_iota(jnp.int32, (BLOCK, BLOCK), 0)
    cols = lax.broadcasted_iota(jnp.int32, (BLOCK, BLOCK), 1)
    a = jnp.where(rows == cols, jnp.sum(q * k, axis=-1, keepdims=True), 0.0)
    differ = jnp.bitwise_xor(rows, cols)
    for h, lvl in zip(_LEVELS, levels):
        decay = jnp.exp(lvl)
        pair = (cols < rows) & (differ >= h) & (differ < 2 * h)
        a = a + jnp.where(pair, _mm(q * decay, k * decay, "nt"), 0.0)
    return a


def _hgrn_mix(hq, k, v, st_in, a, seg_incl, seg_after, seg_total):
    o = _mm(jax.nn.silu(hq) * jnp.exp(seg_incl), st_in, "nt") + _mm(a, v, "nn")
    return o, st_in * jnp.exp(seg_total) + _mm(v, k * jnp.exp(seg_after), "tn")


def _hgrn_norm(o, hg, ng):
    return o * lax.rsqrt(jnp.mean(o * o, axis=-1, keepdims=True) + EPS) * ng * jax.nn.silu(hg)


def _seg_blocks(e, h):
    return [e[i * BLOCK:(i + 1) * BLOCK, h * BLOCK:(h + 1) * BLOCK] for i in range(N_SEG)]


def _rope(x, cos, sin, first_half):
    partner = jnp.where(first_half, -pltpu.roll(x, 96, 1), pltpu.roll(x, 32, 1))
    return x * cos + partner * sin


def _rope_t(g, cos, sin, first_half):
    u = g * sin
    partner = jnp.where(first_half, pltpu.roll(u, 96, 1), -pltpu.roll(u, 32, 1))
    return g * cos + partner


def _low_half(x):
    return lax.broadcasted_iota(jnp.int32, x.shape, 1) < HEAD_DIM


def _both_halves(x, g):
    sw = _swap_halves(x)
    return jnp.where(_low_half(x), x, sw) if g == 0 else jnp.where(_low_half(x), sw, x)


def _att_scores(qa, qb, kc, kp, km, g, own4, band4, meta4):
    low = _low_half(qa)
    q4 = jnp.concatenate([jnp.where(low, qa, 0.0), jnp.where(low, 0.0, qa),
                          jnp.where(low, qb, 0.0), jnp.where(low, 0.0, qb)], axis=0)
    scale = HEAD_DIM ** -0.5
    neg = jnp.finfo(F32).min
    s = jnp.where(own4, _mm(_both_halves(kc, g), q4, "nt"), _mm(_both_halves(kp, g), q4, "nt"))
    return (jnp.where(band4, s * scale, neg), jnp.where(meta4, _mm(_both_halves(km, g), q4, "nt") * scale, neg))


def _att_probs(s, sm, sinkrow):
    mx = jnp.maximum(jnp.maximum(jnp.max(s, axis=0, keepdims=True), jnp.max(sm, axis=0, keepdims=True)), sinkrow)
    p, pm, ps = jnp.exp(s - mx), jnp.exp(sm - mx), jnp.exp(sinkrow - mx)
    inv = 1.0 / (jnp.sum(p, axis=0, keepdims=True) + jnp.sum(pm, axis=0, keepdims=True) + ps)
    return p * inv, pm * inv, ps * inv


def _att_probs_bwd(p, pm, ps, dp, dpm):
    r = jnp.sum(p * dp, axis=0, keepdims=True) + jnp.sum(pm * dpm, axis=0, keepdims=True)
    return p * (dp - r), pm * (dpm - r), -ps * r


def _att_values(p, pm, vc, vp, vm, g, own4):
    o4 = (_mm(jnp.where(own4, p, 0.0), _both_halves(vc, g), "tn") + _mm(jnp.where(own4, 0.0, p), _both_halves(vp, g), "tn")
          + _mm(pm, _both_halves(vm, g), "tn"))
    tiles = []
    for j in range(2):
        upper = o4[(2 * j) * BLOCK:(2 * j + 1) * BLOCK]
        tiles.append(jnp.where(_low_half(upper), upper, o4[(2 * j + 1) * BLOCK:(2 * j + 2) * BLOCK]))
    return tiles


def _att_masks(blk_idx):
    kidx = lax.broadcasted_iota(jnp.int32, (BLOCK, BLOCK), 0)
    qrow = lax.broadcasted_iota(jnp.int32, (BLOCK, BLOCK), 1)
    own_side = kidx <= qrow
    pos_own = blk_idx * BLOCK + kidx - PAD
    ok_band = (own_side & (pos_own >= N_META)) | (~own_side & (pos_own - BLOCK >= N_META) & (blk_idx >= 1))
    qpos = blk_idx * BLOCK + lax.broadcasted_iota(jnp.int32, (N_META, BLOCK), 1) - PAD
    ok_meta = lax.broadcasted_iota(jnp.int32, (N_META, BLOCK), 0) <= qpos
    return [jnp.concatenate([m] * 4, axis=1) for m in (own_side, ok_band, ok_meta)]


def _token_streams(tr, tile_of=lambda i: i):
    k = tr // BLOCK
    return [pl.BlockSpec((BLOCK, D_MODEL), lambda i, j=j: (jnp.maximum(k * tile_of(i) - 1 + j, 0), 0))
            for j in range(k)]


def _embed_ln(x, meta_shard, w_in_shard, g0, b0):
    p = x.shape[0] + BLOCK
    tr = _row_tile(p, 640)
    k = tr // BLOCK
    nt = p // tr
    tile_of = lambda s: (s + 1) % nt
    shards = [meta_shard, w_in_shard]
    c_in, c_out, c_shapes, c_sems = _comm_specs(shards, N_DEV)

    def body(*refs):
        g_ref, b_ref = refs[k:k + 2]
        h_ref, hb_ref, xh_ref, rs_ref = refs[k + 4:k + 8]
        out_refs = refs[k + 8:k + 10]
        lead_ref, meta_ref = refs[k + 10:k + 12]
        starts, passes, waits = _gather_behind(refs[k + 2:k + 4], out_refs, refs[k + 12:], [False, False])
        s = pl.program_id(0)
        t = tile_of(s)

        @pl.when(s == 0)
        def _():
            lead_ref[...] = jnp.zeros_like(lead_ref)
            for start in starts:
                start()

        @pl.when(s == nt - 1)
        def _():
            for step in passes + waits:
                step()
            pltpu.sync_copy(out_refs[0], meta_ref)
            for d in range(N_DEV):
                lead_ref[PAD:BLOCK, d * BLOCK:(d + 1) * BLOCK] = meta_ref[d]

        first = jnp.where(t == 0, lead_ref[...], refs[0][...])
        xhat, rstd = _ln_stats(jnp.concatenate([first] + [r[...] for r in refs[1:k]], axis=0))
        row = t * tr + lax.broadcasted_iota(jnp.int32, (tr, 1), 0)
        h = jnp.where(row >= PAD, xhat * g_ref[...] + b_ref[...], 0.0)
        h_ref[...] = h
        hb_ref[...] = h.astype(BF16)
        xh_ref[...] = xhat
        rs_ref[...] = rstd

    vec = pl.BlockSpec((1, D_MODEL), lambda s: (0, 0))
    rowsp = pl.BlockSpec((tr, D_MODEL), lambda s: (tile_of(s), 0))
    return pl.pallas_call(
        body, name="embed_ln", grid=(nt,),
        in_specs=_token_streams(tr, tile_of) + [vec, vec] + c_in,
        out_specs=[rowsp, rowsp, rowsp, pl.BlockSpec((tr, 1), lambda s: (tile_of(s), 0))] + c_out,
        out_shape=[jax.ShapeDtypeStruct((p, D_MODEL), F32), jax.ShapeDtypeStruct((p, D_MODEL), BF16),
                   jax.ShapeDtypeStruct((p, D_MODEL), F32), jax.ShapeDtypeStruct((p, 1), F32)] + c_shapes,
        scratch_shapes=[pltpu.VMEM((BLOCK, D_MODEL), F32), pltpu.VMEM((N_DEV, N_META, BLOCK), F32)] + c_sems,
        compiler_params=_cparams(("arbitrary",)),
    )(*([x] * k), g0, b0, *shards)


def _rope_tables(p):
    pos = (np.arange(p, dtype=np.int32) - PAD).astype(np.float32)
    half = HEAD_DIM // 2
    inv = np.float32(ROPE_THETA) ** (-np.arange(half, dtype=np.float32) / np.float32(half))
    ang = pos[:, None] * np.tile(inv.astype(np.float32), BLOCK // half)[None, :]
    return jnp.asarray(np.cos(ang), F32), jnp.asarray(np.sin(ang), F32)


def _att_sinkrows(sink_ref):
    lanehead = lax.broadcasted_iota(jnp.int32, (1, 4 * BLOCK), 1) // BLOCK
    rows = []
    for g in range(2):
        row = jnp.zeros((1, 4 * BLOCK), F32)
        for j in range(4):
            row = jnp.where(lanehead == j, sink_ref[0, 4 * g + j], row)
        rows.append(row)
    return rows


def _first_half(rows):
    return (lax.broadcasted_iota(jnp.int32, (rows, BLOCK), 1) % HEAD_DIM) < (HEAD_DIM // 2)


def _att_load(qkv_ref, cos_ref, sin_ref, with_q):
    cos, sin, fh = cos_ref[...], sin_ref[...], _first_half(BLOCK)
    qs = [_rope(qkv_ref[:, j * BLOCK:(j + 1) * BLOCK], cos, sin, fh) for j in range(4)] if with_q else None
    k = _rope(qkv_ref[:, ATT_QW:ATT_QW + ATT_KVW], cos, sin, fh)
    v = qkv_ref[:, ATT_QW + ATT_KVW:ATT_QW + 2 * ATT_KVW]
    return qs, k, v


def _att_load_meta(qkv_ref, cos_ref, sin_ref):
    k = _rope(qkv_ref[PAD:BLOCK, ATT_QW:ATT_QW + ATT_KVW], cos_ref[PAD:BLOCK, :], sin_ref[PAD:BLOCK, :],
              _first_half(N_META))
    return k, qkv_ref[PAD:BLOCK, ATT_QW + ATT_KVW:ATT_QW + 2 * ATT_KVW]


def _att_specs(blk):
    w = ATT_QW + 2 * ATT_KVW
    cur = lambda width: pl.BlockSpec((BLOCK, width), lambda i: (blk(i), 0))
    prev = lambda width: pl.BlockSpec((BLOCK, width), lambda i: (jnp.maximum(blk(i) - 1, 0), 0))
    meta = lambda width: pl.BlockSpec((BLOCK, width), lambda i: (0, 0))
    return [cur(w), prev(w), meta(w), cur(BLOCK), cur(BLOCK), prev(BLOCK), prev(BLOCK), meta(BLOCK), meta(BLOCK),
            pl.BlockSpec(memory_space=pltpu.SMEM)]


_FLIPS = [(dx, dy, dc) for dx in (0, 1) for dy in (0, 1) for dc in (0, 1)][1:]
N_PEERS = len(_FLIPS)


def _place():
    return lax.axis_index("x"), lax.axis_index("y"), lax.axis_index("c")


def _peer(place, flip):
    return tuple(1 - p if f else p for p, f in zip(place, flip))


def _slot(place, swapped):
    x, y, c = place
    return 4 * y + 2 * x + c if swapped else 4 * x + 2 * y + c


def _comm_specs(arrs, out_lead):
    n = len(arrs)
    outs = [jax.ShapeDtypeStruct((out_lead,) + a.shape[-2:], a.dtype) for a in arrs]
    sems = [pltpu.SemaphoreType.DMA((n, N_PEERS)), pltpu.SemaphoreType.DMA((n, N_PEERS)), pltpu.SemaphoreType.DMA((n,))]
    return [pl.BlockSpec(memory_space=pl.ANY)] * n, [pl.BlockSpec(memory_space=pl.ANY)] * n, outs, sems


def _gather_behind(shard_refs, out_refs, sems, swapped):
    send_sems, recv_sems, local_sems = sems
    x, y, c = _place()
    me, sibling = (x, y, c), (x, y, 1 - c)
    chips = [(1 - x, y), (x, 1 - y), (1 - x, 1 - y)]
    starts, passes, waits = [], [], []
    for w, (s, o) in enumerate(zip(shard_refs, out_refs)):
        def copy(k, block, to, from_shard=False, w=w, s=s, o=o):
            rows = o.at[_slot(block, swapped[w])]
            return pltpu.make_async_remote_copy(
                src_ref=s if from_shard else rows, dst_ref=rows, send_sem=send_sems.at[w, k],
                recv_sem=recv_sems.at[w, k], device_id=to, device_id_type=MESH)

        own = pltpu.make_async_copy(s, o.at[_slot(me, swapped[w])], local_sems.at[w])
        first = [copy(0, me, sibling, True)] + [copy(1 + j, me, (*chip, c), True) for j, chip in enumerate(chips)]
        handed = [copy(4 + j, (*chip, c), sibling) for j, chip in enumerate(chips)]
        starts += [own.start] + [cp.start for cp in first]
        for j, chip in enumerate(chips):
            passes += [copy(1 + j, (*chip, c), me).wait_recv, handed[j].start]
        waits.append(copy(0, sibling, me).wait_recv)
        waits += [copy(4 + j, (*chip, 1 - c), me).wait_recv for j, chip in enumerate(chips)]
        waits += [cp.wait_send for cp in first + handed] + [own.wait]
    return starts, passes, waits


def _scatter_behind(part_refs, recv_refs, sems, swapped):
    send_sems, recv_sems, _ = sems
    place = _place()
    starts, waits = [], []
    for w, (p, o) in enumerate(zip(part_refs, recv_refs)):
        for r, flip in enumerate(_FLIPS):
            peer = _peer(place, flip)
            cp = pltpu.make_async_remote_copy(
                src_ref=p.at[_slot(peer, swapped[w])], dst_ref=o.at[r], send_sem=send_sems.at[w, r],
                recv_sem=recv_sems.at[w, r], device_id=peer, device_id_type=MESH)
            starts.append(cp.start)
            waits += [cp.wait_recv, cp.wait_send]
    return starts, waits


def _mixers_fwd(proj_hg, proj_att, lbounds, norm_g, lv, cos, sin, sinks, shards, swapped):
    p = proj_hg.shape[0]
    nb = p // BLOCK
    n = len(shards)
    c_in, c_out, c_shapes, c_sems = _comm_specs(shards, N_DEV)
    pass_step = min(nb - 1, max(1, (5 * nb) // 8))

    def body(*refs):
        x_ref, lb_ref, ng_ref, lv_ref, cur_ref, prev_ref, meta_ref, cc, sc, cp, sp, cm, sm, sink_ref = refs[:14]
        shard_refs = refs[14:14 + n]
        y_ref, o_ref, st_ref, a_ref, raw_ref, pr_ref = refs[14 + n:20 + n]
        out_refs = refs[20 + n:20 + 2 * n]
        carry_ref = refs[20 + 2 * n]
        starts, passes, waits = _gather_behind(shard_refs, out_refs, refs[21 + 2 * n:], swapped)
        c = pl.program_id(0)

        @pl.when(c == 0)
        def _():
            carry_ref[...] = jnp.zeros_like(carry_ref)
            for start in starts:
                start()

        @pl.when(c == pass_step)
        def _():
            for step in passes:
                step()

        valid = (c * BLOCK + lax.broadcasted_iota(jnp.int32, (BLOCK, 1), 0)) >= PAD
        logf, k = _hgrn_gates(x_ref[:, HG_W:2 * HG_W], lb_ref[0:1, :], lb_ref[1:2, :], valid)
        e = _split_dot(lv_ref[...], logf, "nn")
        for h in range(HG_HEADS):
            sl = lambda part: x_ref[:, part * HG_W + h * BLOCK: part * HG_W + (h + 1) * BLOCK]
            hs = slice(h * BLOCK, (h + 1) * BLOCK)
            st_in = carry_ref[h]
            st_ref[0, h] = st_in
            seg = _seg_blocks(e, h)
            a = _hgrn_scores(sl(0), k[:, hs], *seg[3:])
            a_ref[0, h] = a.astype(BF16)
            raw, st_out = _hgrn_mix(sl(0), k[:, hs], sl(2), st_in, a, *seg[:3])
            raw_ref[:, hs] = raw
            y_ref[:, hs] = _hgrn_norm(raw, sl(3), ng_ref[...]).astype(BF16)
            carry_ref[h] = st_out

        qs, kc, vc = _att_load(cur_ref, cc, sc, True)
        _, kp, vp = _att_load(prev_ref, cp, sp, False)
        km, vm = _att_load_meta(meta_ref, cm, sm)
        sinkrows = _att_sinkrows(sink_ref)
        own4, band4, meta4 = _att_masks(c)
        for g in range(2):
            s, s_meta = _att_scores(qs[2 * g], qs[2 * g + 1], kc, kp, km, g, own4, band4, meta4)
            pr, pr_meta, pr_sink = _att_probs(s, s_meta, sinkrows[g])
            pr_ref[0, g, :BLOCK, :] = pr.astype(BF16)
            pr_ref[0, g, BLOCK:BLOCK + N_META, :] = pr_meta.astype(BF16)
            pr_ref[0, g, BLOCK + N_META:, :] = jnp.broadcast_to(pr_sink, (N_META, 4 * BLOCK)).astype(BF16)
            for j, tile in enumerate(_att_values(pr, pr_meta, vc, vp, vm, g, own4)):
                o_ref[:, (2 * g + j) * BLOCK:(2 * g + j + 1) * BLOCK] = tile.astype(BF16)

        @pl.when(c == nb - 1)
        def _():
            for wait in waits:
                wait()

    return pl.pallas_call(
        body, name="mixers_fwd", grid=(nb,),
        in_specs=[pl.BlockSpec((BLOCK, 4 * HG_W), lambda c: (c, 0)), pl.BlockSpec((2, HG_W), lambda c: (0, 0)),
                  pl.BlockSpec((1, BLOCK), lambda c: (0, 0)), pl.BlockSpec(lv.shape, lambda c: (0, 0))]
        + _att_specs(lambda c: c) + c_in,
        out_specs=[pl.BlockSpec((BLOCK, HG_W), lambda c: (c, 0)), pl.BlockSpec((BLOCK, ATT_QW), lambda c: (c, 0)),
                   pl.BlockSpec((1, HG_HEADS, BLOCK, BLOCK), lambda c: (c, 0, 0, 0)),
                   pl.BlockSpec((1, HG_HEADS, BLOCK, BLOCK), lambda c: (c, 0, 0, 0)),
                   pl.BlockSpec((BLOCK, HG_W), lambda c: (c, 0)),
                   pl.BlockSpec((1, 2, ATT_KEYS, 4 * BLOCK), lambda c: (c, 0, 0, 0))] + c_out,
        out_shape=[jax.ShapeDtypeStruct((p, HG_W), BF16), jax.ShapeDtypeStruct((p, ATT_QW), BF16),
                   jax.ShapeDtypeStruct((nb, HG_HEADS, BLOCK, BLOCK), F32),
                   jax.ShapeDtypeStruct((nb, HG_HEADS, BLOCK, BLOCK), BF16),
                   jax.ShapeDtypeStruct((p, HG_W), F32),
                   jax.ShapeDtypeStruct((nb, 2, ATT_KEYS, 4 * BLOCK), BF16)] + c_shapes,
        scratch_shapes=[pltpu.VMEM((HG_HEADS, BLOCK, BLOCK), F32)] + c_sems,
        compiler_params=_cparams(("arbitrary",)),
    )(proj_hg, lbounds, norm_g, lv, proj_att, proj_att, proj_att, cos, sin, cos, sin, cos, sin, sinks, *shards)


def _tile(rows, preferred):
    return preferred if rows % preferred == 0 else _row_tile(rows, preferred)


def _in_proj(h0b, w_in_t):
    p = h0b.shape[0]
    tm = _row_tile(p, 640)
    hg_end = 4 * HG_W

    def body(h_ref, w_ref, hg_ref, att_ref):
        h = h_ref[...]
        hg_ref[...] = _dot(h, w_ref[:hg_end, :], "nt")
        att_ref[...] = _dot(h, w_ref[hg_end:, :], "nt")

    row = lambda w: pl.BlockSpec((tm, w), lambda i: (i, 0))
    return pl.pallas_call(
        body, name="in_proj", grid=(p // tm,),
        in_specs=[row(D_MODEL), pl.BlockSpec((MIX_W, D_MODEL), lambda i: (0, 0), pipeline_mode=pl.Buffered(1))],
        out_specs=[row(hg_end), row(MIX_W - hg_end)],
        out_shape=[jax.ShapeDtypeStruct((p, hg_end), F32), jax.ShapeDtypeStruct((p, MIX_W - hg_end), F32)],
        compiler_params=_cparams(("arbitrary",)),
    )(h0b, w_in_t)


def _branch_mix(yh, oa, gates, w_bh, w_ba):
    y_hg = _dot(yh, w_bh, "nn")
    y_att = _dot(oa, w_ba, "nn")
    s1 = jax.nn.sigmoid(gates[:, :D_MODEL].astype(F32))
    s2 = jax.nn.sigmoid(gates[:, D_MODEL:].astype(F32))
    return s1 * y_hg + s2 * y_att, y_hg, y_att, s1, s2


def _mix_out_ln1(yh, oa, h0b, w_in_t, h0, w_bh, w_ba, w_out, g1, b1):
    p = yh.shape[0]
    tr = _tile(p, 320)

    def body(yh_ref, oa_ref, h0b_ref, wi_ref, h0_ref, wbh_ref, wba_ref, wo_ref, g1_ref, b1_ref,
             g_ref, mix_ref, h1_ref, h1b_ref, xh_ref, rs_ref):
        g_ref[...] = _dot(h0b_ref[...], wi_ref[MIX_W:, :], "nt").astype(BF16)
        mixin = _branch_mix(yh_ref[...], oa_ref[...], g_ref[...], wbh_ref[...], wba_ref[...])[0]
        mix_ref[...] = mixin.astype(BF16)
        xhat, rstd = _ln_stats(ALPHA * h0_ref[...] + _dot(mixin, wo_ref[...], "nn"))
        h1 = xhat * g1_ref[...] + b1_ref[...]
        h1_ref[...] = h1
        h1b_ref[...] = h1.astype(BF16)
        xh_ref[...] = xhat
        rs_ref[...] = rstd

    row = lambda w: pl.BlockSpec((tr, w), lambda i: (i, 0))
    const = lambda a: pl.BlockSpec(a.shape, lambda i: (0, 0))
    return pl.pallas_call(
        body, name="mix_out_ln1", grid=(p // tr,),
        in_specs=[row(HG_W), row(ATT_QW), row(D_MODEL),
                  pl.BlockSpec(w_in_t.shape, lambda i: (0, 0), pipeline_mode=pl.Buffered(1)), row(D_MODEL),
                  const(w_bh), const(w_ba), const(w_out), const(g1), const(b1)],
        out_specs=[row(2 * D_MODEL), row(D_MODEL), row(D_MODEL), row(D_MODEL), row(D_MODEL), row(1)],
        out_shape=[jax.ShapeDtypeStruct((p, 2 * D_MODEL), BF16), jax.ShapeDtypeStruct((p, D_MODEL), BF16),
                   jax.ShapeDtypeStruct((p, D_MODEL), F32), jax.ShapeDtypeStruct((p, D_MODEL), BF16),
                   jax.ShapeDtypeStruct((p, D_MODEL), F32), jax.ShapeDtypeStruct((p, 1), F32)],
        compiler_params=_cparams(("arbitrary",)),
    )(yh, oa, h0b, w_in_t, h0, w_bh, w_ba, w_out, g1, b1)


FF_T = D_FF // 2


def _ffn_in_swiglu(h1, w_fi_t):
    p = h1.shape[0]
    tm = _row_tile(p, 640)

    def body(h_ref, w_ref, au_ref, s_ref):
        au = _dot(h_ref[...], w_ref[...], "nt")
        au_ref[...] = au.astype(BF16)
        s_ref[...] = (jax.nn.silu(au[:, :FF_T]) * au[:, FF_T:]).astype(BF16)

    return pl.pallas_call(
        body, name="ffn_in_swiglu", grid=(D_FF // FF_T, p // tm),
        in_specs=[pl.BlockSpec((tm, D_MODEL), lambda j, i: (i, 0)), pl.BlockSpec((2 * FF_T, D_MODEL), lambda j, i: (j, 0))],
        out_specs=[pl.BlockSpec((tm, 2 * FF_T), lambda j, i: (i, j)), pl.BlockSpec((tm, FF_T), lambda j, i: (i, j))],
        out_shape=[jax.ShapeDtypeStruct((p, 2 * D_FF), BF16), jax.ShapeDtypeStruct((p, D_FF), BF16)],
        compiler_params=_cparams(("arbitrary", "arbitrary")),
    )(h1, w_fi_t)


def _ffn_out_loss(s, w_fo, h1, g2, b2, target):
    p = h1.shape[0]
    tr = _row_tile(p, 640)
    k = tr // BLOCK

    def body(*refs):
        s_ref, w_ref, h_ref, g_ref, b_ref = refs[:5]
        dr_ref, drb_ref, loss_ref, dg_ref, db_ref = refs[5 + k:]
        i = pl.program_id(0)
        xhat, rstd = _ln_stats(ALPHA * h_ref[...] + _dot(s_ref[...], w_ref[...], "nn"))
        y = xhat * g_ref[...] + b_ref[...]
        row = i * tr + lax.broadcasted_iota(jnp.int32, (tr, 1), 0)
        tgt = jnp.concatenate([r[...] for r in refs[5:5 + k]], axis=0)
        err = jnp.where(row >= BLOCK, y - tgt, 0.0)
        dr, dg, db = _ln_bwd(err * (1.0 / D_MODEL), xhat, rstd, g_ref[...])
        dr_ref[...] = dr
        drb_ref[...] = dr.astype(BF16)
        e2 = jnp.sum(err * err, axis=0, keepdims=True)
        part = e2[:, 0:BLOCK]
        for j in range(1, D_MODEL // BLOCK):
            part = part + e2[:, j * BLOCK:(j + 1) * BLOCK]
        part = part * (0.5 / D_MODEL)

        @pl.when(i == 0)
        def _():
            loss_ref[...] = part
            dg_ref[...] = dg
            db_ref[...] = db

        @pl.when(i > 0)
        def _():
            loss_ref[...] += part
            dg_ref[...] += dg
            db_ref[...] += db

    vec = pl.BlockSpec((1, D_MODEL), lambda i: (0, 0))
    rowsp = pl.BlockSpec((tr, D_MODEL), lambda i: (i, 0))
    return pl.pallas_call(
        body, name="ffn_out_loss", grid=(p // tr,),
        in_specs=[pl.BlockSpec((tr, D_FF), lambda i: (i, 0)), pl.BlockSpec((D_FF, D_MODEL), lambda i: (0, 0)),
                  rowsp, vec, vec] + _token_streams(tr),
        out_specs=[rowsp, rowsp, pl.BlockSpec((1, BLOCK), lambda i: (0, 0)), vec, vec],
        out_shape=[jax.ShapeDtypeStruct((p, D_MODEL), F32), jax.ShapeDtypeStruct((p, D_MODEL), BF16),
                   jax.ShapeDtypeStruct((1, BLOCK), F32), jax.ShapeDtypeStruct((1, D_MODEL), F32),
                   jax.ShapeDtypeStruct((1, D_MODEL), F32)],
        compiler_params=_cparams(("arbitrary",)),
    )(s, w_fo, h1, g2, b2, *([target] * k))


def _ffn_bwd(dr2, w_fo, au, w_fi_t):
    p = au.shape[0]
    tm = _tile(p, 320)

    def body(d_ref, wo_ref, au_ref, wi_ref, dau_ref, dh_ref):
        d = d_ref[...].astype(BF16)
        dh = ALPHA * d_ref[...]
        for j in range(D_FF // FF_T):
            a_cols = slice(2 * j * FF_T, (2 * j + 1) * FF_T)
            u_cols = slice((2 * j + 1) * FF_T, (2 * j + 2) * FF_T)
            ds = _dot(d, wo_ref[j * FF_T:(j + 1) * FF_T, :], "nt")
            _, vjp = jax.vjp(lambda a, u: jax.nn.silu(a) * u, au_ref[:, a_cols].astype(F32), au_ref[:, u_cols].astype(F32))
            da, du = vjp(ds)
            dau_ref[:, a_cols] = da.astype(BF16)
            dau_ref[:, u_cols] = du.astype(BF16)
            pair = slice(2 * j * FF_T, (2 * j + 2) * FF_T)
            dh = dh + _dot(dau_ref[:, pair], wi_ref[pair, :], "nn")
        dh_ref[...] = dh

    row = lambda w: pl.BlockSpec((tm, w), lambda i: (i, 0))
    kept = lambda a: pl.BlockSpec(a.shape, lambda i: (0, 0), pipeline_mode=pl.Buffered(1))
    return pl.pallas_call(
        body, name="ffn_bwd", grid=(p // tm,),
        in_specs=[row(D_MODEL), kept(w_fo), row(2 * D_FF), kept(w_fi_t)],
        out_specs=[row(2 * D_FF), row(D_MODEL)],
        out_shape=[jax.ShapeDtypeStruct((p, 2 * D_FF), BF16), jax.ShapeDtypeStruct((p, D_MODEL), F32)],
        compiler_params=_cparams(("arbitrary",)),
    )(dr2, w_fo, au, w_fi_t)


def _ln1_mix_bwd(dh1, xhat1, rstd1, g1, yh, oa, gates, mixin, w_bh, w_ba, w_out):
    p = yh.shape[0]
    tr = _tile(p, 320)
    nt = p // tr

    def body(dh_ref, xh_ref, rs_ref, g1_ref, yh_ref, oa_ref, g_ref, mix_ref, wbh_ref, wba_ref, wo_ref,
             dr_ref, dgt_ref, dyh_ref, doa_ref, dg_ref, db_ref, dwbh_ref, dwba_ref, dwo_ref,
             abh_ref, aba_ref, ao_ref):
        i = pl.program_id(0)
        dr, dg, db = _ln_bwd(dh_ref[...], xh_ref[...], rs_ref[...], g1_ref[...])
        dr_ref[...] = dr
        d = _dot(dr, wo_ref[...], "nt")
        _, y_hg, y_att, s1, s2 = _branch_mix(yh_ref[...], oa_ref[...], g_ref[...], wbh_ref[...], wba_ref[...])
        dy_hg = (d * s1).astype(BF16)
        dy_att = (d * s2).astype(BF16)
        dgt_ref[:, :D_MODEL] = (d * y_hg * s1 * (1.0 - s1)).astype(BF16)
        dgt_ref[:, D_MODEL:] = (d * y_att * s2 * (1.0 - s2)).astype(BF16)
        dyh_ref[...] = _dot(dy_hg, wbh_ref[...], "nt")
        doa_ref[...] = _dot(dy_att, wba_ref[...], "nt")
        sums = [(dg_ref, dg), (db_ref, db), (abh_ref, _dot(yh_ref[...], dy_hg, "tn")),
                (aba_ref, _dot(oa_ref[...], dy_att, "tn")), (ao_ref, _dot(mix_ref[...], dr, "tn"))]

        @pl.when(i == 0)
        def _():
            for ref, term in sums:
                ref[...] = term

        @pl.when(i > 0)
        def _():
            for ref, term in sums:
                ref[...] += term

        @pl.when(i == nt - 1)
        def _():
            dwbh_ref[...] = abh_ref[...].astype(BF16)
            dwba_ref[...] = aba_ref[...].astype(BF16)
            dwo_ref[...] = ao_ref[...].astype(BF16)

    row = lambda w: pl.BlockSpec((tr, w), lambda i: (i, 0))
    const = lambda a: pl.BlockSpec(a.shape, lambda i: (0, 0))
    vec = pl.BlockSpec((1, D_MODEL), lambda i: (0, 0))
    weights = (w_bh, w_ba, w_out)
    return pl.pallas_call(
        body, name="ln1_mix_bwd", grid=(nt,),
        in_specs=[row(D_MODEL), row(D_MODEL), row(1), vec, row(HG_W), row(ATT_QW), row(2 * D_MODEL), row(D_MODEL)]
                 + [const(w) for w in weights],
        out_specs=[row(D_MODEL), row(2 * D_MODEL), row(HG_W), row(ATT_QW), vec, vec] + [const(w) for w in weights],
        out_shape=[jax.ShapeDtypeStruct((p, D_MODEL), F32), jax.ShapeDtypeStruct((p, 2 * D_MODEL), BF16),
                   jax.ShapeDtypeStruct((p, HG_W), F32), jax.ShapeDtypeStruct((p, ATT_QW), F32),
                   jax.ShapeDtypeStruct((1, D_MODEL), F32), jax.ShapeDtypeStruct((1, D_MODEL), F32)]
                  + [jax.ShapeDtypeStruct(w.shape, BF16) for w in weights],
        scratch_shapes=[pltpu.VMEM(w.shape, F32) for w in weights],
        compiler_params=_cparams(("arbitrary",)),
    )(dh1, xhat1, rstd1, g1, yh, oa, gates, mixin, w_bh, w_ba, w_out)


MIX_W = 4 * HG_W + ATT_QW + 2 * ATT_KVW
ATT_KEYS = BLOCK + 2 * N_META


def _mixers_bwd(proj_hg, proj_att, lbounds, norm_g, lv, states, scores, raw, probs, cos, sin, sinks, dyh, doa,
                parts, swapped):
    p = proj_hg.shape[0]
    nb = p // BLOCK
    n = len(parts)
    kvw = 2 * ATT_KVW
    rev = lambda s: nb - 1 - s
    c_in, c_out, c_shapes, c_sems = _comm_specs(parts, N_PEERS)

    def body(*refs):
        (x_ref, lb_ref, ng_ref, lv_ref, st_ref, a_ref, raw_ref, pr_ref, cur_ref, prev_ref, meta_ref, cc, sc, cp, sp,
         cm, sm, sink_ref, dy_ref, do_ref) = refs[:20]
        part_refs = refs[20:20 + n]
        dx_ref, dlb_ref, dng_ref, dsink_ref = refs[20 + n:24 + n]
        recv_refs = refs[24 + n:24 + 2 * n]
        dcarry_ref, dkv_next_ref, dkv_meta_ref = refs[24 + 2 * n:27 + 2 * n]
        starts, waits = _scatter_behind(part_refs, recv_refs, refs[27 + 2 * n:], swapped)
        step = pl.program_id(0)
        c = rev(step)

        @pl.when(step == 0)
        def _():
            dcarry_ref[...] = jnp.zeros_like(dcarry_ref)
            dkv_next_ref[...] = jnp.zeros_like(dkv_next_ref)
            dkv_meta_ref[...] = jnp.zeros_like(dkv_meta_ref)
            dlb_ref[...] = jnp.zeros_like(dlb_ref)
            dng_ref[...] = jnp.zeros_like(dng_ref)
            dsink_ref[...] = jnp.zeros_like(dsink_ref)
            for start in starts:
                start()

        fh = _first_half(BLOCK)
        qs, kc, vc = _att_load(cur_ref, cc, sc, True)
        _, kp, vp = _att_load(prev_ref, cp, sp, False)
        km, vm = _att_load_meta(meta_ref, cm, sm)
        own4, band4, meta4 = _att_masks(c)
        att0 = 4 * HG_W
        dkm = dkp = dkc = dvm = dvp = dvc = 0.0
        dsinkrows = []
        for g in range(2):
            pr = pr_ref[0, g, :BLOCK, :].astype(F32)
            pr_meta = pr_ref[0, g, BLOCK:BLOCK + N_META, :].astype(F32)
            pr_sink = jnp.max(pr_ref[0, g, BLOCK + N_META:, :].astype(F32), axis=0, keepdims=True)
            _, values_vjp = jax.vjp(lambda *a, g=g: _att_values(*a, g, own4), pr, pr_meta, vc, vp, vm)
            dpr, dpr_meta, dvc_g, dvp_g, dvm_g = values_vjp(
                [do_ref[:, (2 * g + j) * BLOCK:(2 * g + j + 1) * BLOCK] for j in range(2)])
            ds, ds_meta, dsinkrow = _att_probs_bwd(pr, pr_meta, pr_sink, dpr, dpr_meta)
            _, scores_vjp = jax.vjp(lambda *a, g=g: _att_scores(*a, g, own4, band4, meta4),
                                    qs[2 * g], qs[2 * g + 1], kc, kp, km)
            dqa, dqb, dkc_g, dkp_g, dkm_g = scores_vjp((ds, ds_meta))
            for j, dq in enumerate((dqa, dqb)):
                dx_ref[:, att0 + (2 * g + j) * BLOCK:att0 + (2 * g + j + 1) * BLOCK] = _rope_t(
                    dq, cc[...], sc[...], fh).astype(BF16)
            dkm, dkp, dkc = dkm + dkm_g, dkp + dkp_g, dkc + dkc_g
            dvm, dvp, dvc = dvm + dvm_g, dvp + dvp_g, dvc + dvc_g
            dsinkrows.append(dsinkrow)
        ds0, ds1 = dsinkrows
        dkv_meta_ref[:, :BLOCK] += _rope_t(dkm, cm[PAD:BLOCK, :], sm[PAD:BLOCK, :], _first_half(N_META))
        dkv_meta_ref[:, BLOCK:] += dvm
        last = jnp.where(c == 0, 1.0, 0.0)
        to_meta_rows = lambda m: jnp.concatenate([jnp.zeros((PAD, BLOCK), F32), last * m], axis=0)
        dk = _rope_t(dkc, cc[...], sc[...], fh) + dkv_next_ref[:, :BLOCK] + to_meta_rows(dkv_meta_ref[:, :BLOCK])
        dv = dvc + dkv_next_ref[:, BLOCK:] + to_meta_rows(dkv_meta_ref[:, BLOCK:])
        dx_ref[:, att0 + ATT_QW:att0 + ATT_QW + ATT_KVW] = dk.astype(BF16)
        dx_ref[:, att0 + ATT_QW + ATT_KVW:] = dv.astype(BF16)
        dkv_next_ref[:, :BLOCK] = _rope_t(dkp, cp[...], sp[...], fh)
        dkv_next_ref[:, BLOCK:] = dvp
        sink_rows = []
        for dsg in (ds0, ds1):
            for j in range(4):
                tot = jnp.sum(dsg[:, j * BLOCK:(j + 1) * BLOCK], axis=1, keepdims=True)
                sink_rows.append(jnp.broadcast_to(tot, (1, BLOCK)))
        dsink_ref[...] += jnp.concatenate(sink_rows, axis=0)

        valid = (c * BLOCK + lax.broadcasted_iota(jnp.int32, (BLOCK, 1), 0)) >= PAD
        (logf, k), gates_vjp = jax.vjp(lambda hf, a0, a1: _hgrn_gates(hf, a0, a1, valid),
                                       x_ref[:, HG_W:2 * HG_W], lb_ref[0:1, :], lb_ref[1:2, :])
        lvv = lv_ref[...]
        e = _split_dot(lvv, logf, "nn")
        dng = jnp.zeros((1, BLOCK), F32)
        dk, dseg = [], []
        for h in range(HG_HEADS):
            sl = lambda part: x_ref[:, part * HG_W + h * BLOCK: part * HG_W + (h + 1) * BLOCK]
            hs = slice(h * BLOCK, (h + 1) * BLOCK)
            seg = _seg_blocks(e, h)
            _, norm_vjp = jax.vjp(_hgrn_norm, raw_ref[:, hs], sl(3), ng_ref[...])
            draw, dhg, dngh = norm_vjp(dy_ref[:, hs])
            _, mix_vjp = jax.vjp(_hgrn_mix, sl(0), k[:, hs], sl(2), st_ref[0, h], a_ref[0, h].astype(F32), *seg[:3])
            dhq, dkh, dhi, dst, da, *dseg_mix = mix_vjp((draw, dcarry_ref[h]))
            _, scores_vjp = jax.vjp(_hgrn_scores, sl(0), k[:, hs], *seg[3:])
            dhq2, dkh2, *dseg_lvl = scores_vjp(da)
            for part, val in ((0, dhq + dhq2), (2, dhi), (3, dhg)):
                dx_ref[:, part * HG_W + h * BLOCK: part * HG_W + (h + 1) * BLOCK] = val.astype(BF16)
            dk.append(dkh + dkh2)
            dseg.append(jnp.concatenate(dseg_mix + dseg_lvl, axis=0))
            dng = dng + dngh
            dcarry_ref[h] = dst
        dlogf = _split_dot(lvv, jnp.concatenate(dseg, axis=1), "tn")
        dhf, da0, da1 = gates_vjp((dlogf, jnp.concatenate(dk, axis=1)))
        dx_ref[:, HG_W:2 * HG_W] = dhf.astype(BF16)
        dlb_ref[0:1, :] += da0
        dlb_ref[1:2, :] += da1
        dng_ref[...] += dng

        @pl.when(step == nb - 1)
        def _():
            for wait in waits:
                wait()

    const = lambda shape: pl.BlockSpec(shape, lambda s: (0,) * len(shape))
    per_head = pl.BlockSpec((1, HG_HEADS, BLOCK, BLOCK), lambda s: (rev(s), 0, 0, 0))
    return pl.pallas_call(
        body, name="mixers_bwd", grid=(nb,),
        in_specs=[pl.BlockSpec((BLOCK, 4 * HG_W), lambda s: (rev(s), 0)), const((2, HG_W)), const((1, BLOCK)),
                  const(lv.shape), per_head, per_head, pl.BlockSpec((BLOCK, HG_W), lambda s: (rev(s), 0)),
                  pl.BlockSpec((1, 2, ATT_KEYS, 4 * BLOCK), lambda s: (rev(s), 0, 0, 0))]
        + _att_specs(rev)
        + [pl.BlockSpec((BLOCK, HG_W), lambda s: (rev(s), 0)), pl.BlockSpec((BLOCK, ATT_QW), lambda s: (rev(s), 0))]
        + c_in,
        out_specs=[pl.BlockSpec((BLOCK, MIX_W), lambda s: (rev(s), 0)), const((2, HG_W)), const((1, BLOCK)),
                   const((ATT_HEADS, BLOCK))] + c_out,
        out_shape=[jax.ShapeDtypeStruct((p, MIX_W), BF16), jax.ShapeDtypeStruct((2, HG_W), F32),
                   jax.ShapeDtypeStruct((1, BLOCK), F32), jax.ShapeDtypeStruct((ATT_HEADS, BLOCK), F32)] + c_shapes,
        scratch_shapes=[pltpu.VMEM((HG_HEADS, BLOCK, BLOCK), F32), pltpu.VMEM((BLOCK, kvw), F32),
                        pltpu.VMEM((N_META, kvw), F32)] + c_sems,
        compiler_params=_cparams(("arbitrary",)),
    )(proj_hg, lbounds, norm_g, lv, states, scores, raw, probs, proj_att, proj_att, proj_att, cos, sin, cos, sin,
      cos, sin, sinks, dyh, doa, *parts)


_HBM = pl.BlockSpec(memory_space=pltpu.HBM)
_SEM = pl.BlockSpec(memory_space=pltpu.SEMAPHORE)
_ORDERED_BY_DATA = pltpu.CompilerParams(has_side_effects=pltpu.SideEffectType.DATAFLOW_SIDE_EFFECTING)


def _exchange_copies(part_ref, land_ref, send_sems, recv_sems):
    place = _place()
    return [pltpu.make_async_remote_copy(
        src_ref=part_ref.at[_slot(_peer(place, flip), False)], dst_ref=land_ref.at[r], send_sem=send_sems.at[r],
        recv_sem=recv_sems.at[r], device_id=_peer(place, flip), device_id_type=MESH) for r, flip in enumerate(_FLIPS)]


def _exchange_start(parts, name):
    def body(part_ref, land_ref, send_sems, recv_sems, part_thru, land_thru, token):
        for cp in _exchange_copies(part_ref, land_ref, send_sems, recv_sems):
            cp.start()
        token[...] = jnp.zeros_like(token)

    land = (N_PEERS,) + parts.shape[1:]
    return pl.pallas_call(
        body, name=name,
        out_shape=(pltpu.SemaphoreType.DMA((N_PEERS,)), pltpu.SemaphoreType.DMA((N_PEERS,)),
                   pltpu.HBM(parts.shape, parts.dtype), pltpu.HBM(land, parts.dtype), jax.ShapeDtypeStruct((8, BLOCK), F32)),
        in_specs=(_HBM, _HBM), out_specs=(_SEM, _SEM, _HBM, _HBM, pl.BlockSpec(memory_space=pltpu.VMEM)),
        input_output_aliases={0: 2, 1: 3}, compiler_params=_ORDERED_BY_DATA,
    )(pltpu.with_memory_space_constraint(parts, pltpu.HBM),
      pltpu.with_memory_space_constraint(lax.empty(land, parts.dtype), pltpu.HBM))


def _exchange_wait(send_sems, recv_sems, part_thru, land_thru, after, name):
    def body(part_ref, land_ref, send_sems, recv_sems, after_ref, part_out, land_out):
        for cp in _exchange_copies(part_ref, land_ref, send_sems, recv_sems):
            cp.wait_send()
            cp.wait_recv()

    return pl.pallas_call(
        body, name=name,
        out_shape=(pltpu.HBM(part_thru.shape, part_thru.dtype), pltpu.HBM(land_thru.shape, land_thru.dtype)),
        in_specs=(_HBM, _HBM, _SEM, _SEM, pl.BlockSpec(memory_space=pl.ANY)), out_specs=(_HBM, _HBM),
        input_output_aliases={0: 0, 1: 1}, compiler_params=_ORDERED_BY_DATA,
    )(part_thru, land_thru, send_sems, recv_sems, after)


def _embed_bwd(dmix, dgates, w_in_t, dr1, xhat0, rstd0, g0):
    p = dmix.shape[0]
    tm = _row_tile(p, 640)
    nm = p // tm

    def body(a_ref, g_ref, w_ref, dr_ref, xh_ref, rs_ref, g0_ref, gx_ref, lead_ref, dg_ref, db_ref, buf_ref, sem):
        i = pl.program_id(0)
        first = pltpu.make_async_copy(buf_ref.at[0, pl.ds(BLOCK, tm - BLOCK)], gx_ref.at[pl.ds(0, tm - BLOCK)],
                                      sem.at[0])
        later = lambda t: pltpu.make_async_copy(buf_ref.at[t % 2], gx_ref.at[pl.ds(t * tm - BLOCK, tm)], sem.at[t % 2])

        @pl.when(i == 2)
        def _():
            first.wait()

        @pl.when(i > 2)
        def _():
            later(i - 2).wait()

        dh0 = (ALPHA * dr_ref[...] + _dot(a_ref[...], w_ref[:MIX_W, :], "nn")
               + _dot(g_ref[...], w_ref[MIX_W:, :], "nn"))
        row = i * tm + lax.broadcasted_iota(jnp.int32, (tm, 1), 0)
        dx, dg, db = _ln_bwd(jnp.where(row >= PAD, dh0, 0.0), xh_ref[...], rs_ref[...], g0_ref[...])
        buf_ref[i % 2] = dx

        @pl.when(i == 0)
        def _():
            lead_ref[...] = dx[:BLOCK]
            dg_ref[...] = dg
            db_ref[...] = db
            first.start()

        @pl.when(i > 0)
        def _():
            dg_ref[...] += dg
            db_ref[...] += db
            later(i).start()

        @pl.when(i == nm - 1)
        def _():
            for t in (nm - 2, nm - 1):
                if t >= 0:
                    (first if t == 0 else later(t)).wait()

    row = lambda w: pl.BlockSpec((tm, w), lambda i: (i, 0))
    vec = pl.BlockSpec((1, D_MODEL), lambda i: (0, 0))
    return pl.pallas_call(
        body, name="embed_bwd", grid=(nm,),
        in_specs=[row(dmix.shape[1]), row(dgates.shape[1]),
                  pl.BlockSpec(w_in_t.shape, lambda i: (0, 0), pipeline_mode=pl.Buffered(1)), row(D_MODEL), row(D_MODEL),
                  row(1), vec],
        out_specs=[pl.BlockSpec(memory_space=pl.ANY), pl.BlockSpec((BLOCK, D_MODEL), lambda i: (0, 0)), vec, vec],
        out_shape=[jax.ShapeDtypeStruct((p - BLOCK, D_MODEL), F32), jax.ShapeDtypeStruct((BLOCK, D_MODEL), F32),
                   jax.ShapeDtypeStruct((1, D_MODEL), F32), jax.ShapeDtypeStruct((1, D_MODEL), F32)],
        scratch_shapes=[pltpu.VMEM((2, tm, D_MODEL), F32), pltpu.SemaphoreType.DMA((2,))],
        compiler_params=_cparams(("arbitrary",)),
    )(dmix, dgates, w_in_t, dr1, xhat0, rstd0, g0)


_LATE = ("w_branch_hg", "w_branch_attn", "w_out", "w_ffn_in", "w_ffn_out")
_TRANSPOSED = ("w_in", "w_ffn_in")
_COLUMN_SHARDED = ("meta_tokens", "w_branch_hg", "w_branch_attn")
_SWAPPED = ("w_ffn_in",)


def _whole(name, gathered):
    _, r, c = gathered.shape
    if name in _COLUMN_SHARDED:
        return jnp.transpose(gathered, (1, 0, 2)).reshape(r, N_DEV * c)
    return gathered.reshape(N_DEV * r, c)


def _slots(name, whole):
    r, c = whole.shape
    if name in _COLUMN_SHARDED:
        return jnp.transpose(whole.reshape(r, N_DEV, c // N_DEV), (1, 0, 2))
    return whole.reshape(N_DEV, r // N_DEV, c)


def _device_step(x, target, meta_shard, ln_emb_g, ln_emb_b, w_in_shard, lbounds, norm_g, sinks, late_shards,
                 ln1_g, ln1_b, ln2_g, ln2_b):
    p = x.shape[0] + BLOCK
    lv = _level_stack()
    cos, sin = _rope_tables(p)
    swapped = [n in _SWAPPED for n in _LATE]

    h0, h0b, xhat0, rstd0, _, g_win = _embed_ln(x, meta_shard, w_in_shard, ln_emb_g, ln_emb_b)
    w_in = _whole("w_in", g_win)
    proj_hg, proj_att = _in_proj(h0b, w_in)
    yh, oa, states, scores, raw, probs, *gathered = _mixers_fwd(
        proj_hg, proj_att, lbounds, norm_g, lv, cos, sin, sinks, late_shards, swapped)
    w_bh, w_ba, w_out, w_fi, w_fo = [_whole(n, g) for n, g in zip(_LATE, gathered)]
    gates, mixin, h1, h1b, xhat1, rstd1 = _mix_out_ln1(yh, oa, h0b, w_in, h0, w_bh, w_ba, w_out, ln1_g, ln1_b)
    au, sw = _ffn_in_swiglu(h1b, w_fi)
    dr2, dr2b, loss_part, dg2, db2 = _ffn_out_loss(sw, w_fo, h1, ln2_g, ln2_b, target)

    mtn = functools.partial(_tiled_matmul_tn, tm=_row_tile(p, 1664), out_dtype=BF16)
    d_wfo = mtn(sw, dr2b, tk=FF_T, tn=D_MODEL, name="grad_w_ffn_out")
    dau, dh1 = _ffn_bwd(dr2, w_fo, au, w_fi)
    d_wfi = _weight_grad_t([dau], h1b, tk=4 * BLOCK, name="grad_w_ffn_in")
    dr1, dgates, dyh, doa, dg1, db1, d_wbh, d_wba, d_wout = _ln1_mix_bwd(
        dh1, xhat1, rstd1, ln1_g, yh, oa, gates, mixin, w_bh, w_ba, w_out)
    late_parts = [_slots(n, g) for n, g in zip(_LATE, (d_wbh, d_wba, d_wout, d_wfi, d_wfo))]
    dmix, d_lb, d_ng, d_sink, *late_recv = _mixers_bwd(
        proj_hg, proj_att, lbounds, norm_g, lv, states, scores, raw, probs, cos, sin, sinks, dyh, doa, late_parts,
        swapped)
    d_win = _weight_grad_t([dmix, dgates], h0b, tk=2 * BLOCK, name="grad_w_in")
    *win_flight, token = _exchange_start(_slots("w_in", d_win), "w_in_grads_start")
    grad_x, dlead, dg0, db0 = _embed_bwd(dmix, dgates, w_in, dr1, xhat0, rstd0, ln_emb_g + token[0:1, 0:1])

    small = dict(ln_emb_g=dg0, ln_emb_b=db0, hg_lower_bounds=d_lb, hg_norm_g=d_ng, ln1_g=dg1, ln1_b=db1, ln2_g=dg2,
                 ln2_b=db2)
    big = dict(zip(_LATE, zip(late_parts, late_recv)))
    return _pack_small(small, d_sink, dlead, loss_part), grad_x, big, win_flight


def _all_gather(arrs, dtypes, name):
    n = len(arrs)

    def body(*refs):
        ins, outs, stages = refs[:n], refs[n:2 * n], refs[2 * n:3 * n]
        send_sems, recv_sems, local_sems = refs[3 * n:]
        x, y, c = _place()
        sibling = (x, y, 1 - c)
        chips = [(1 - x, y), (x, 1 - y), (1 - x, 1 - y)]
        slot = lambda px, py, pc: 4 * px + 2 * py + pc

        def copy(w, k, block, to, from_stage=False):
            return pltpu.make_async_remote_copy(
                src_ref=stages[w] if from_stage else outs[w].at[slot(*block)], dst_ref=outs[w].at[slot(*block)],
                send_sem=send_sems.at[w, k], recv_sem=recv_sems.at[w, k], device_id=to, device_id_type=MESH)

        mine, first, passed = [], [], []
        for w in range(n):
            stages[w][...] = ins[w][...].astype(dtypes[w])
            mine.append(pltpu.make_async_copy(stages[w], outs[w].at[slot(x, y, c)], local_sems.at[w]))
            mine[-1].start()
        for w in range(n):
            first.append(copy(w, 0, (x, y, c), sibling, from_stage=True))
            first += [copy(w, 1 + j, (x, y, c), (*chip, c), from_stage=True) for j, chip in enumerate(chips)]
        for cp in first:
            cp.start()
        for j, chip in enumerate(chips):
            for w in range(n):
                copy(w, 1 + j, (*chip, c), (x, y, c)).wait_recv()
                passed.append(copy(w, 4 + j, (*chip, c), sibling))
                passed[-1].start()
        for w in range(n):
            copy(w, 0, sibling, (x, y, c)).wait_recv()
            for j, chip in enumerate(chips):
                copy(w, 4 + j, (*chip, 1 - c), (x, y, c)).wait_recv()
        for cp in first + passed:
            cp.wait_send()
        for cp in mine:
            cp.wait()

    return pl.pallas_call(
        body, name=name,
        in_specs=[pl.BlockSpec(memory_space=pltpu.VMEM)] * n,
        out_specs=[pl.BlockSpec(memory_space=pl.ANY)] * n,
        out_shape=[jax.ShapeDtypeStruct((N_DEV,) + a.shape, dt) for a, dt in zip(arrs, dtypes)],
        scratch_shapes=[pltpu.VMEM(a.shape, dt) for a, dt in zip(arrs, dtypes)]
        + [pltpu.SemaphoreType.DMA((n, 7)), pltpu.SemaphoreType.DMA((n, 7)), pltpu.SemaphoreType.DMA((n,))],
        compiler_params=pltpu.CompilerParams(vmem_limit_bytes=VMEM_LIMIT_BYTES),
    )(*arrs)


def _cast_shards(arrs):
    def body(*refs):
        for src, dst in zip(refs[:len(arrs)], refs[len(arrs):]):
            dst[...] = src[...].astype(BF16)

    return pl.pallas_call(body, name="cast_shards", out_shape=[jax.ShapeDtypeStruct(a.shape, BF16) for a in arrs],
                          compiler_params=pltpu.CompilerParams(vmem_limit_bytes=VMEM_LIMIT_BYTES))(*arrs)


def _shard_rows(rows):
    return rows if rows <= 512 else max(t for t in range(16, 353, 16) if rows % t == 0)


def _adamw_math(w, g, m, v):
    m = ADAM_B1 * m + (1.0 - ADAM_B1) * g
    v = ADAM_B2 * v + (1.0 - ADAM_B2) * (g * g)
    m_hat = m / (1.0 - ADAM_B1 ** ADAM_STEP)
    v_hat = v / (1.0 - ADAM_B2 ** ADAM_STEP)
    delta = -ADAM_LR * (m_hat / (jnp.sqrt(v_hat) + ADAM_EPS) + ADAM_WD * w)
    return delta, m, v


def _reduce_adamw(parts, recv, own_slot, w, m, v, name):
    r, cdim = w.shape
    tr = _shard_rows(r)

    def body(idx_ref, p_ref, r_ref, w_ref, m_ref, v_ref, g_out, d_out, m_out, v_out):
        g = p_ref[0].astype(F32)
        for j in range(N_PEERS):
            g = g + r_ref[j].astype(F32)
        d, mn, vn = _adamw_math(w_ref[...], g, m_ref[...], v_ref[...])
        g_out[...] = g
        d_out[...] = d
        m_out[...] = mn
        v_out[...] = vn

    flat = pl.BlockSpec((tr, cdim), lambda i, idx_ref: (i, 0))
    return pl.pallas_call(
        body, name=name,
        grid_spec=pltpu.PrefetchScalarGridSpec(
            num_scalar_prefetch=1, grid=(r // tr,),
            in_specs=[pl.BlockSpec((1, tr, cdim), lambda i, idx_ref: (idx_ref[0], i, 0)),
                      pl.BlockSpec((N_PEERS, tr, cdim), lambda i, idx_ref: (0, i, 0)), flat, flat, flat],
            out_specs=[flat] * 4),
        out_shape=[jax.ShapeDtypeStruct((r, cdim), F32)] * 4,
        compiler_params=_cparams(("arbitrary",)),
    )(own_slot, parts, recv, w, m, v)


def _adamw_plain(w, g, m, v, name):
    def body(w_ref, g_ref, m_ref, v_ref, d_out, m_out, v_out):
        d_out[...], m_out[...], v_out[...] = _adamw_math(w_ref[...], g_ref[...], m_ref[...], v_ref[...])

    return pl.pallas_call(body, name=name, out_shape=[jax.ShapeDtypeStruct(w.shape, F32)] * 3)(w, g, m, v)


_SMALL = (("ln_emb_g", (1, D_MODEL)), ("ln_emb_b", (1, D_MODEL)), ("hg_lower_bounds", (2, HG_W)),
          ("hg_norm_g", (1, BLOCK)), ("attn_sinks", (1, ATT_HEADS)), ("ln1_g", (1, D_MODEL)), ("ln1_b", (1, D_MODEL)),
          ("ln2_g", (1, D_MODEL)), ("ln2_b", (1, D_MODEL)))
_SMALL_ROW, _LOSS_ROW = {}, 0
for _name, (_rows, _) in _SMALL:
    _SMALL_ROW[_name], _LOSS_ROW = _LOSS_ROW, _LOSS_ROW + _rows
_META_ROW = 16
SMALL_ROWS = _META_ROW + N_META
assert _LOSS_ROW < _META_ROW


def _pack_small(grads, d_sink, dlead, loss_part):
    names = [n for n, _ in _SMALL if n != "attn_sinks"]

    def body(*refs):
        ins = dict(zip(names, refs))
        sink_ref, lead_ref, loss_ref, o_ref = refs[len(names):]
        o_ref[...] = jnp.zeros_like(o_ref)
        for name, (rows, cols) in _SMALL:
            if name != "attn_sinks":
                o_ref[_SMALL_ROW[name]:_SMALL_ROW[name] + rows, :cols] = ins[name][...]
        head = lax.broadcasted_iota(jnp.int32, (ATT_HEADS, BLOCK), 0)
        lane = lax.broadcasted_iota(jnp.int32, (ATT_HEADS, BLOCK), 1)
        o_ref[_SMALL_ROW["attn_sinks"]:_SMALL_ROW["attn_sinks"] + 1, :BLOCK] = jnp.sum(
            jnp.where(head == lane, sink_ref[...], 0.0), axis=0, keepdims=True)
        o_ref[_LOSS_ROW:_LOSS_ROW + 1, :BLOCK] = loss_ref[...]
        o_ref[_META_ROW:, :] = lead_ref[PAD:BLOCK, :]

    return pl.pallas_call(body, name="pack_small", out_shape=jax.ShapeDtypeStruct((SMALL_ROWS, D_MODEL), F32))(
        *[grads[n] for n in names], d_sink, dlead, loss_part)


def _small_reduce_adamw(gathered, weights, mom1, mom2):
    n = len(_SMALL)

    def body(*refs):
        g_ref, w_refs, m_refs, v_refs = refs[0], refs[1:1 + n], refs[1 + n:1 + 2 * n], refs[1 + 2 * n:1 + 3 * n]
        outs = refs[1 + 3 * n:1 + 7 * n]
        meta_out, loss_out, sum_ref = refs[1 + 7 * n:]
        total = g_ref[0]
        for s in range(1, N_DEV):
            total = total + g_ref[s]
        sum_ref[...] = total
        for i, (name, (rows, cols)) in enumerate(_SMALL):
            g = sum_ref[_SMALL_ROW[name]:_SMALL_ROW[name] + rows, :cols]
            d, mn, vn = _adamw_math(w_refs[i][...], g, m_refs[i][...], v_refs[i][...])
            for out, val in zip(outs[4 * i:4 * i + 4], (g, d, mn, vn)):
                out[...] = val
        meta_out[...] = sum_ref[_META_ROW:, :]
        loss_out[...] = jnp.broadcast_to(jnp.sum(sum_ref[_LOSS_ROW:_LOSS_ROW + 1, :BLOCK]), (1, BLOCK))

    per_param = [jax.ShapeDtypeStruct(shape, F32) for _, shape in _SMALL for _ in range(4)]
    res = pl.pallas_call(
        body, name="small_reduce_adamw",
        out_shape=per_param + [jax.ShapeDtypeStruct((N_META, D_MODEL), F32), jax.ShapeDtypeStruct((1, BLOCK), F32)],
        scratch_shapes=[pltpu.VMEM((SMALL_ROWS, D_MODEL), F32)],
    )(gathered, *[d[name] for d in (weights, mom1, mom2) for name, _ in _SMALL])
    return {name: res[4 * i:4 * i + 4] for i, (name, _) in enumerate(_SMALL)}, res[-2], res[-1]


_WEIGHTS = ("meta_tokens", "ln_emb_g", "ln_emb_b", "w_in", "hg_lower_bounds", "hg_norm_g", "attn_sinks",
            "w_branch_hg", "w_branch_attn", "w_out", "ln1_g", "ln1_b", "w_ffn_in", "w_ffn_out", "ln2_g", "ln2_b")


def kernel(x, meta_tokens, ln_emb_g, ln_emb_b, w_in, hg_lower_bounds, hg_norm_g, attn_sinks, w_branch_hg, w_branch_attn, w_out, ln1_g, ln1_b, w_ffn_in, w_ffn_out, ln2_g, ln2_b, loss_target, m_meta_tokens, m_ln_emb_g, m_ln_emb_b, m_w_in, m_hg_lower_bounds, m_hg_norm_g, m_attn_sinks, m_w_branch_hg, m_w_branch_attn, m_w_out, m_ln1_g, m_ln1_b, m_w_ffn_in, m_w_ffn_out, m_ln2_g, m_ln2_b, v_meta_tokens, v_ln_emb_g, v_ln_emb_b, v_w_in, v_hg_lower_bounds, v_hg_norm_g, v_attn_sinks, v_w_branch_hg, v_w_branch_attn, v_w_out, v_ln1_g, v_ln1_b, v_w_ffn_in, v_w_ffn_out, v_ln2_g, v_ln2_b):
    given = dict(locals())
    weights = {n: given[n] for n in _WEIGHTS}
    mom1 = {n: given["m_" + n] for n in _WEIGHTS}
    mom2 = {n: given["v_" + n] for n in _WEIGHTS}
    shard2d = lambda n, a: a.reshape(a.shape[-2:]).T if n in _TRANSPOSED else a.reshape(a.shape[-2:])

    w_in_shard, *late_shards = _cast_shards([shard2d(n, weights[n]) for n in ("w_in",) + _LATE])
    packed, grad_x, big, win_flight = _device_step(
        x[0], loss_target[0], meta_tokens, ln_emb_g.reshape(1, -1), ln_emb_b.reshape(1, -1), w_in_shard,
        hg_lower_bounds, hg_norm_g, attn_sinks, late_shards, ln1_g, ln1_b, ln2_g, ln2_b)

    place = _place()
    out = {}

    def reduce_adamw(n, parts, recv):
        own = _slot(place, n in _SWAPPED).astype(jnp.int32).reshape(1)
        res = _reduce_adamw(parts, recv, own, shard2d(n, weights[n]), shard2d(n, mom1[n]), shard2d(n, mom2[n]),
                            "adamw_" + n)
        out[n] = [(r.T if n in _TRANSPOSED else r).reshape(weights[n].shape) for r in res]

    for n, (parts, recv) in big.items():
        reduce_adamw(n, parts, recv)

    all_small, = _all_gather([packed], [F32], "gather_small")
    as_2d = lambda d: {n: d[n].reshape(shape) for n, shape in _SMALL}
    small_out, meta_whole, loss_row = _small_reduce_adamw(all_small, as_2d(weights), as_2d(mom1), as_2d(mom2))
    for n, res in small_out.items():
        out[n] = [r.reshape(weights[n].shape) for r in res]
    loss = loss_row[0, 0]
    g_meta_mine = lax.dynamic_index_in_dim(meta_whole.reshape(N_META, N_DEV, D_MODEL // N_DEV), _slot(place, False),
                                           axis=1, keepdims=False)
    out["meta_tokens"] = [g_meta_mine, *_adamw_plain(meta_tokens, g_meta_mine, m_meta_tokens, v_meta_tokens,
                                                     "adamw_meta")]

    reduce_adamw("w_in", *_exchange_wait(*win_flight, after=all_small, name="w_in_grads_wait"))

    return (loss, grad_x[None], *[out[n][0] for n in _WEIGHTS], *[out[n][1] for n in _WEIGHTS],
            *[out[n][2] for n in _WEIGHTS], *[out[n][3] for n in _WEIGHTS])
```

```python
import functools

import numpy as np
import jax
import jax.numpy as jnp
from jax import lax
from jax.experimental import pallas as pl
from jax.experimental.pallas import tpu as pltpu

F32 = jnp.float32
BF16 = jnp.bfloat16

D_MODEL = 1024
N_META = 16
BLOCK = 128
PAD = BLOCK - N_META
HG_HEADS = 4
HG_W = 512
ATT_HEADS = 8
HEAD_DIM = 64
ATT_QW = 512
ATT_KVW = 128
D_FF = 2816
EPS = 1e-5
ALPHA = 2.0 ** 0.25
ROPE_THETA = 10000.0
N_DEV = 8

ADAM_LR = 0.001
ADAM_B1 = 0.9
ADAM_B2 = 0.999
ADAM_EPS = 1e-08
ADAM_WD = 0.01
ADAM_STEP = 10

VMEM_LIMIT_BYTES = 56 * 1024 * 1024
MESH = pl.DeviceIdType.MESH

_LEVELS = (64, 32, 16, 8, 4, 2, 1)


def _cparams(sem):
    return pltpu.CompilerParams(dimension_semantics=sem, vmem_limit_bytes=VMEM_LIMIT_BYTES)


def _row_tile(rows, target):
    nb = rows // BLOCK
    best = 1
    for d in range(1, nb + 1):
        if nb % d == 0 and d * BLOCK <= target:
            best = d
    return best * BLOCK


_DN = {"nn": (((1,), (0,)), ((), ())), "nt": (((1,), (1,)), ((), ())), "tn": (((0,), (0,)), ((), ()))}


def _dot(a, b, form):
    return lax.dot_general(a.astype(BF16), b.astype(BF16), _DN[form], preferred_element_type=F32)


@functools.partial(jax.custom_vjp, nondiff_argnums=(2,))
def _mm(a, b, form):
    return _dot(a, b, form)


def _mm_fwd(a, b, form):
    a, b = a.astype(BF16), b.astype(BF16)
    return _dot(a, b, form), (a, b)


def _mm_bwd(form, res, g):
    a, b = res
    if form == "nn":
        return _dot(g, b, "nt"), _dot(a, g, "tn")
    if form == "nt":
        return _dot(g, b, "nn"), _dot(g, a, "tn")
    return _dot(b, g, "nt"), _dot(a, g, "nn")


_mm.defvjp(_mm_fwd, _mm_bwd)


def _split_dot(lv, x, form):
    return lax.dot_general(lv, x.astype(BF16), _DN[form], preferred_element_type=F32)


@jax.custom_vjp
def _swap_halves(x):
    return pltpu.roll(x, 64, 1)


_swap_halves.defvjp(lambda x: (pltpu.roll(x, 64, 1), None), lambda _, g: (pltpu.roll(g, 64, 1),))


def _tiled_matmul_tn(a, b, *, tm, tk, tn, out_dtype, name):
    m, k = a.shape
    n = b.shape[1]
    assert m % tm == 0 and k % tk == 0 and n % tn == 0, (name, a.shape, b.shape, tm, tk, tn)
    nm = m // tm

    def body(a_ref, b_ref, o_ref, acc_ref):
        mi = pl.program_id(2)

        @pl.when(mi == 0)
        def _():
            acc_ref[...] = jnp.zeros_like(acc_ref)

        acc_ref[...] += _dot(a_ref[...], b_ref[...], "tn")

        @pl.when(mi == nm - 1)
        def _():
            o_ref[...] = acc_ref[...].astype(out_dtype)

    return pl.pallas_call(
        body, name=name, grid=(k // tk, n // tn, nm),
        in_specs=[pl.BlockSpec((tm, tk), lambda kk, j, i: (i, kk)), pl.BlockSpec((tm, tn), lambda kk, j, i: (i, j))],
        out_specs=pl.BlockSpec((tk, tn), lambda kk, j, i: (kk, j)),
        out_shape=jax.ShapeDtypeStruct((k, n), out_dtype),
        scratch_shapes=[pltpu.VMEM((tk, tn), F32)],
        compiler_params=_cparams(("arbitrary", "arbitrary", "arbitrary")),
    )(a, b)


def _weight_grad_t(cots, h, *, tk, name):
    p, d = h.shape
    steps = [c.shape[1] // tk for c in cots]
    assert all(c.shape == (p, n * tk) for c, n in zip(cots, steps)), (name, [c.shape for c in cots], tk)
    first = [sum(steps[:i]) for i in range(len(cots))]

    def body(*refs):
        h_ref, o_ref = refs[len(cots)], refs[len(cots) + 1]
        k = pl.program_id(0)
        for c_ref, lo, n in zip(refs, first, steps):
            @pl.when((k >= lo) & (k < lo + n))
            def _(c_ref=c_ref):
                o_ref[...] = _dot(c_ref[...], h_ref[...], "tn").astype(BF16)

    cot_spec = lambda lo, n: pl.BlockSpec((p, tk), lambda k: (0, jnp.clip(k - lo, 0, n - 1)))
    return pl.pallas_call(
        body, name=name, grid=(sum(steps),),
        in_specs=[cot_spec(lo, n) for lo, n in zip(first, steps)]
                 + [pl.BlockSpec((p, d), lambda k: (0, 0), pipeline_mode=pl.Buffered(1))],
        out_specs=pl.BlockSpec((tk, d), lambda k: (k, 0)),
        out_shape=jax.ShapeDtypeStruct((sum(steps) * tk, d), BF16),
        compiler_params=_cparams(("arbitrary",)),
    )(*cots, h)


def _ln_stats(r):
    mu = jnp.mean(r, axis=-1, keepdims=True)
    xc = r - mu
    var = jnp.mean(xc * xc, axis=-1, keepdims=True)
    rstd = lax.rsqrt(var + EPS)
    return xc * rstd, rstd


def _ln_bwd(dy, xhat, rstd, g):
    dxhat = dy * g
    m1 = jnp.mean(dxhat, axis=-1, keepdims=True)
    m2 = jnp.mean(dxhat * xhat, axis=-1, keepdims=True)
    dr = rstd * (dxhat - m1 - xhat * m2)
    return dr, jnp.sum(dy * xhat, axis=0, keepdims=True), jnp.sum(dy, axis=0, keepdims=True)


N_SEG = 3 + len(_LEVELS)


def _level_stack():
    t = np.arange(BLOCK)[:, None]
    r = np.arange(BLOCK)[None, :]
    mats = [r <= t, r > t, np.ones((BLOCK, BLOCK), bool)]
    for h in _LEVELS:
        same = (t // (2 * h)) == (r // (2 * h))
        up_t, up_r = (t % (2 * h)) >= h, (r % (2 * h)) >= h
        mats.append(same & ((up_t & up_r & (r <= t)) | (~up_t & ~up_r & (r > t))))
    return jnp.asarray(np.concatenate(mats, axis=0).astype(np.float32), dtype=BF16)


def _hgrn_gates(hf, a0, a1, valid):
    lb = jax.nn.sigmoid(a0 - a1)
    fg = lb + (1.0 - lb) * jax.nn.sigmoid(hf)
    return jnp.where(valid, jnp.log(fg), 0.0), jnp.where(valid, 1.0 - fg, 0.0)


def _hgrn_scores(hq, k, *levels):
    q = jax.nn.silu(hq)
    rows = lax.broadcasted_iota(jnp.int32, (BLOCK, BLOCK), 0)
    cols = lax.broadcasted_iota(jnp.int32, (BLOCK, BLOCK), 1)
    a = jnp.where(rows == cols, jnp.sum(q * k, axis=-1, keepdims=True), 0.0)
    differ = jnp.bitwise_xor(rows, cols)
    for h, lvl in zip(_LEVELS, levels):
        decay = jnp.exp(lvl)
        pair = (cols < rows) & (differ >= h) & (differ < 2 * h)
        a = a + jnp.where(pair, _mm(q * decay, k * decay, "nt"), 0.0)
    return a


def _hgrn_mix(hq, k, v, st_in, a, seg_incl, seg_after, seg_total):
    o = _mm(jax.nn.silu(hq) * jnp.exp(seg_incl), st_in, "nt") + _mm(a, v, "nn")
    return o, st_in * jnp.exp(seg_total) + _mm(v, k * jnp.exp(seg_after), "tn")


def _hgrn_norm(o, hg, ng):
    return o * lax.rsqrt(jnp.mean(o * o, axis=-1, keepdims=True) + EPS) * ng * jax.nn.silu(hg)


def _seg_blocks(e, h):
    return [e[i * BLOCK:(i + 1) * BLOCK, h * BLOCK:(h + 1) * BLOCK] for i in range(N_SEG)]


def _rope(x, cos, sin, first_half):
    partner = jnp.where(first_half, -pltpu.roll(x, 96, 1), pltpu.roll(x, 32, 1))
    return x * cos + partner * sin


def _rope_t(g, cos, sin, first_half):
    u = g * sin
    partner = jnp.where(first_half, pltpu.roll(u, 96, 1), -pltpu.roll(u, 32, 1))
    return g * cos + partner


def _low_half(x):
    return lax.broadcasted_iota(jnp.int32, x.shape, 1) < HEAD_DIM


def _both_halves(x, g):
    sw = _swap_halves(x)
    return jnp.where(_low_half(x), x, sw) if g == 0 else jnp.where(_low_half(x), sw, x)


def _att_scores(qa, qb, kc, kp, km, g, own4, band4, meta4):
    low = _low_half(qa)
    q4 = jnp.concatenate([jnp.where(low, qa, 0.0), jnp.where(low, 0.0, qa),
                          jnp.where(low, qb, 0.0), jnp.where(low, 0.0, qb)], axis=0)
    scale = HEAD_DIM ** -0.5
    neg = jnp.finfo(F32).min
    s = jnp.where(own4, _mm(_both_halves(kc, g), q4, "nt"), _mm(_both_halves(kp, g), q4, "nt"))
    return (jnp.where(band4, s * scale, neg), jnp.where(meta4, _mm(_both_halves(km, g), q4, "nt") * scale, neg))


def _att_probs(s, sm, sinkrow):
    mx = jnp.maximum(jnp.maximum(jnp.max(s, axis=0, keepdims=True), jnp.max(sm, axis=0, keepdims=True)), sinkrow)
    p, pm, ps = jnp.exp(s - mx), jnp.exp(sm - mx), jnp.exp(sinkrow - mx)
    inv = 1.0 / (jnp.sum(p, axis=0, keepdims=True) + jnp.sum(pm, axis=0, keepdims=True) + ps)
    return p * inv, pm * inv, ps * inv


def _att_probs_bwd(p, pm, ps, dp, dpm):
    r = jnp.sum(p * dp, axis=0, keepdims=True) + jnp.sum(pm * dpm, axis=0, keepdims=True)
    return p * (dp - r), pm * (dpm - r), -ps * r


def _att_values(p, pm, vc, vp, vm, g, own4):
    o4 = (_mm(jnp.where(own4, p, 0.0), _both_halves(vc, g), "tn") + _mm(jnp.where(own4, 0.0, p), _both_halves(vp, g), "tn")
          + _mm(pm, _both_halves(vm, g), "tn"))
    tiles = []
    for j in range(2):
        upper = o4[(2 * j) * BLOCK:(2 * j + 1) * BLOCK]
        tiles.append(jnp.where(_low_half(upper), upper, o4[(2 * j + 1) * BLOCK:(2 * j + 2) * BLOCK]))
    return tiles


def _att_masks(blk_idx):
    kidx = lax.broadcasted_iota(jnp.int32, (BLOCK, BLOCK), 0)
    qrow = lax.broadcasted_iota(jnp.int32, (BLOCK, BLOCK), 1)
    own_side = kidx <= qrow
    pos_own = blk_idx * BLOCK + kidx - PAD
    ok_band = (own_side & (pos_own >= N_META)) | (~own_side & (pos_own - BLOCK >= N_META) & (blk_idx >= 1))
    qpos = blk_idx * BLOCK + lax.broadcasted_iota(jnp.int32, (N_META, BLOCK), 1) - PAD
    ok_meta = lax.broadcasted_iota(jnp.int32, (N_META, BLOCK), 0) <= qpos
    return [jnp.concatenate([m] * 4, axis=1) for m in (own_side, ok_band, ok_meta)]


def _token_streams(tr, tile_of=lambda i: i):
    k = tr // BLOCK
    return [pl.BlockSpec((BLOCK, D_MODEL), lambda i, j=j: (jnp.maximum(k * tile_of(i) - 1 + j, 0), 0))
            for j in range(k)]


def _embed_ln(x, meta_shard, w_in_shard, g0, b0):
    p = x.shape[0] + BLOCK
    tr = _row_tile(p, 640)
    k = tr // BLOCK
    nt = p // tr
    tile_of = lambda s: (s + 1) % nt
    shards = [meta_shard, w_in_shard]
    c_in, c_out, c_shapes, c_sems = _comm_specs(shards, N_DEV)

    def body(*refs):
        g_ref, b_ref = refs[k:k + 2]
        h_ref, hb_ref, xh_ref, rs_ref = refs[k + 4:k + 8]
        out_refs = refs[k + 8:k + 10]
        lead_ref, meta_ref = refs[k + 10:k + 12]
        starts, passes, waits = _gather_behind(refs[k + 2:k + 4], out_refs, refs[k + 12:], [False, False])
        s = pl.program_id(0)
        t = tile_of(s)

        @pl.when(s == 0)
        def _():
            lead_ref[...] = jnp.zeros_like(lead_ref)
            for start in starts:
                start()

        @pl.when(s == nt - 1)
        def _():
            for step in passes + waits:
                step()
            pltpu.sync_copy(out_refs[0], meta_ref)
            for d in range(N_DEV):
                lead_ref[PAD:BLOCK, d * BLOCK:(d + 1) * BLOCK] = meta_ref[d]

        first = jnp.where(t == 0, lead_ref[...], refs[0][...])
        xhat, rstd = _ln_stats(jnp.concatenate([first] + [r[...] for r in refs[1:k]], axis=0))
        row = t * tr + lax.broadcasted_iota(jnp.int32, (tr, 1), 0)
        h = jnp.where(row >= PAD, xhat * g_ref[...] + b_ref[...], 0.0)
        h_ref[...] = h
        hb_ref[...] = h.astype(BF16)
        xh_ref[...] = xhat
        rs_ref[...] = rstd

    vec = pl.BlockSpec((1, D_MODEL), lambda s: (0, 0))
    rowsp = pl.BlockSpec((tr, D_MODEL), lambda s: (tile_of(s), 0))
    return pl.pallas_call(
        body, name="embed_ln", grid=(nt,),
        in_specs=_token_streams(tr, tile_of) + [vec, vec] + c_in,
        out_specs=[rowsp, rowsp, rowsp, pl.BlockSpec((tr, 1), lambda s: (tile_of(s), 0))] + c_out,
        out_shape=[jax.ShapeDtypeStruct((p, D_MODEL), F32), jax.ShapeDtypeStruct((p, D_MODEL), BF16),
                   jax.ShapeDtypeStruct((p, D_MODEL), F32), jax.ShapeDtypeStruct((p, 1), F32)] + c_shapes,
        scratch_shapes=[pltpu.VMEM((BLOCK, D_MODEL), F32), pltpu.VMEM((N_DEV, N_META, BLOCK), F32)] + c_sems,
        compiler_params=_cparams(("arbitrary",)),
    )(*([x] * k), g0, b0, *shards)


def _rope_tables(p):
    pos = (np.arange(p, dtype=np.int32) - PAD).astype(np.float32)
    half = HEAD_DIM // 2
    inv = np.float32(ROPE_THETA) ** (-np.arange(half, dtype=np.float32) / np.float32(half))
    ang = pos[:, None] * np.tile(inv.astype(np.float32), BLOCK // half)[None, :]
    return jnp.asarray(np.cos(ang), F32), jnp.asarray(np.sin(ang), F32)


def _att_sinkrows(sink_ref):
    lanehead = lax.broadcasted_iota(jnp.int32, (1, 4 * BLOCK), 1) // BLOCK
    rows = []
    for g in range(2):
        row = jnp.zeros((1, 4 * BLOCK), F32)
        for j in range(4):
            row = jnp.where(lanehead == j, sink_ref[0, 4 * g + j], row)
        rows.append(row)
    return rows


def _first_half(rows):
    return (lax.broadcasted_iota(jnp.int32, (rows, BLOCK), 1) % HEAD_DIM) < (HEAD_DIM // 2)


def _att_load(qkv_ref, cos_ref, sin_ref, with_q):
    cos, sin, fh = cos_ref[...], sin_ref[...], _first_half(BLOCK)
    qs = [_rope(qkv_ref[:, j * BLOCK:(j + 1) * BLOCK], cos, sin, fh) for j in range(4)] if with_q else None
    k = _rope(qkv_ref[:, ATT_QW:ATT_QW + ATT_KVW], cos, sin, fh)
    v = qkv_ref[:, ATT_QW + ATT_KVW:ATT_QW + 2 * ATT_KVW]
    return qs, k, v


def _att_load_meta(qkv_ref, cos_ref, sin_ref):
    k = _rope(qkv_ref[PAD:BLOCK, ATT_QW:ATT_QW + ATT_KVW], cos_ref[PAD:BLOCK, :], sin_ref[PAD:BLOCK, :],
              _first_half(N_META))
    return k, qkv_ref[PAD:BLOCK, ATT_QW + ATT_KVW:ATT_QW + 2 * ATT_KVW]


def _att_specs(blk):
    w = ATT_QW + 2 * ATT_KVW
    cur = lambda width: pl.BlockSpec((BLOCK, width), lambda i: (blk(i), 0))
    prev = lambda width: pl.BlockSpec((BLOCK, width), lambda i: (jnp.maximum(blk(i) - 1, 0), 0))
    meta = lambda width: pl.BlockSpec((BLOCK, width), lambda i: (0, 0))
    return [cur(w), prev(w), meta(w), cur(BLOCK), cur(BLOCK), prev(BLOCK), prev(BLOCK), meta(BLOCK), meta(BLOCK),
            pl.BlockSpec(memory_space=pltpu.SMEM)]


_FLIPS = [(dx, dy, dc) for dx in (0, 1) for dy in (0, 1) for dc in (0, 1)][1:]
N_PEERS = len(_FLIPS)


def _place():
    return lax.axis_index("x"), lax.axis_index("y"), lax.axis_index("c")


def _peer(place, flip):
    return tuple(1 - p if f else p for p, f in zip(place, flip))


def _slot(place, swapped):
    x, y, c = place
    return 4 * y + 2 * x + c if swapped else 4 * x + 2 * y + c


def _comm_specs(arrs, out_lead):
    n = len(arrs)
    outs = [jax.ShapeDtypeStruct((out_lead,) + a.shape[-2:], a.dtype) for a in arrs]
    sems = [pltpu.SemaphoreType.DMA((n, N_PEERS)), pltpu.SemaphoreType.DMA((n, N_PEERS)), pltpu.SemaphoreType.DMA((n,))]
    return [pl.BlockSpec(memory_space=pl.ANY)] * n, [pl.BlockSpec(memory_space=pl.ANY)] * n, outs, sems


def _gather_behind(shard_refs, out_refs, sems, swapped):
    send_sems, recv_sems, local_sems = sems
    x, y, c = _place()
    me, sibling = (x, y, c), (x, y, 1 - c)
    chips = [(1 - x, y), (x, 1 - y), (1 - x, 1 - y)]
    starts, passes, waits = [], [], []
    for w, (s, o) in enumerate(zip(shard_refs, out_refs)):
        def copy(k, block, to, from_shard=False, w=w, s=s, o=o):
            rows = o.at[_slot(block, swapped[w])]
            return pltpu.make_async_remote_copy(
                src_ref=s if from_shard else rows, dst_ref=rows, send_sem=send_sems.at[w, k],
                recv_sem=recv_sems.at[w, k], device_id=to, device_id_type=MESH)

        own = pltpu.make_async_copy(s, o.at[_slot(me, swapped[w])], local_sems.at[w])
        first = [copy(0, me, sibling, True)] + [copy(1 + j, me, (*chip, c), True) for j, chip in enumerate(chips)]
        handed = [copy(4 + j, (*chip, c), sibling) for j, chip in enumerate(chips)]
        starts += [own.start] + [cp.start for cp in first]
        for j, chip in enumerate(chips):
            passes += [copy(1 + j, (*chip, c), me).wait_recv, handed[j].start]
        waits.append(copy(0, sibling, me).wait_recv)
        waits += [copy(4 + j, (*chip, 1 - c), me).wait_recv for j, chip in enumerate(chips)]
        waits += [cp.wait_send for cp in first + handed] + [own.wait]
    return starts, passes, waits


def _scatter_behind(part_refs, recv_refs, sems, swapped):
    send_sems, recv_sems, _ = sems
    place = _place()
    starts, waits = [], []
    for w, (p, o) in enumerate(zip(part_refs, recv_refs)):
        for r, flip in enumerate(_FLIPS):
            peer = _peer(place, flip)
            cp = pltpu.make_async_remote_copy(
                src_ref=p.at[_slot(peer, swapped[w])], dst_ref=o.at[r], send_sem=send_sems.at[w, r],
                recv_sem=recv_sems.at[w, r], device_id=peer, device_id_type=MESH)
            starts.append(cp.start)
            waits += [cp.wait_recv, cp.wait_send]
    return starts, waits


def _mixers_fwd(proj_hg, proj_att, lbounds, norm_g, lv, cos, sin, sinks, shards, swapped):
    p = proj_hg.shape[0]
    nb = p // BLOCK
    n = len(shards)
    c_in, c_out, c_shapes, c_sems = _comm_specs(shards, N_DEV)
    pass_step = min(nb - 1, max(1, (5 * nb) // 8))

    def body(*refs):
        x_ref, lb_ref, ng_ref, lv_ref, cur_ref, prev_ref, meta_ref, cc, sc, cp, sp, cm, sm, sink_ref = refs[:14]
        shard_refs = refs[14:14 + n]
        y_ref, o_ref, st_ref, a_ref, raw_ref, pr_ref = refs[14 + n:20 + n]
        out_refs = refs[20 + n:20 + 2 * n]
        carry_ref = refs[20 + 2 * n]
        starts, passes, waits = _gather_behind(shard_refs, out_refs, refs[21 + 2 * n:], swapped)
        c = pl.program_id(0)

        @pl.when(c == 0)
        def _():
            carry_ref[...] = jnp.zeros_like(carry_ref)
            for start in starts:
                start()

        @pl.when(c == pass_step)
        def _():
            for step in passes:
                step()

        valid = (c * BLOCK + lax.broadcasted_iota(jnp.int32, (BLOCK, 1), 0)) >= PAD
        logf, k = _hgrn_gates(x_ref[:, HG_W:2 * HG_W], lb_ref[0:1, :], lb_ref[1:2, :], valid)
        e = _split_dot(lv_ref[...], logf, "nn")
        for h in range(HG_HEADS):
            sl = lambda part: x_ref[:, part * HG_W + h * BLOCK: part * HG_W + (h + 1) * BLOCK]
            hs = slice(h * BLOCK, (h + 1) * BLOCK)
            st_in = carry_ref[h]
            st_ref[0, h] = st_in
            seg = _seg_blocks(e, h)
            a = _hgrn_scores(sl(0), k[:, hs], *seg[3:])
            a_ref[0, h] = a.astype(BF16)
            raw, st_out = _hgrn_mix(sl(0), k[:, hs], sl(2), st_in, a, *seg[:3])
            raw_ref[:, hs] = raw
            y_ref[:, hs] = _hgrn_norm(raw, sl(3), ng_ref[...]).astype(BF16)
            carry_ref[h] = st_out

        qs, kc, vc = _att_load(cur_ref, cc, sc, True)
        _, kp, vp = _att_load(prev_ref, cp, sp, False)
        km, vm = _att_load_meta(meta_ref, cm, sm)
        sinkrows = _att_sinkrows(sink_ref)
        own4, band4, meta4 = _att_masks(c)
        for g in range(2):
            s, s_meta = _att_scores(qs[2 * g], qs[2 * g + 1], kc, kp, km, g, own4, band4, meta4)
            pr, pr_meta, pr_sink = _att_probs(s, s_meta, sinkrows[g])
            pr_ref[0, g, :BLOCK, :] = pr.astype(BF16)
            pr_ref[0, g, BLOCK:BLOCK + N_META, :] = pr_meta.astype(BF16)
            pr_ref[0, g, BLOCK + N_META:, :] = jnp.broadcast_to(pr_sink, (N_META, 4 * BLOCK)).astype(BF16)
            for j, tile in enumerate(_att_values(pr, pr_meta, vc, vp, vm, g, own4)):
                o_ref[:, (2 * g + j) * BLOCK:(2 * g + j + 1) * BLOCK] = tile.astype(BF16)

        @pl.when(c == nb - 1)
        def _():
            for wait in waits:
                wait()

    return pl.pallas_call(
        body, name="mixers_fwd", grid=(nb,),
        in_specs=[pl.BlockSpec((BLOCK, 4 * HG_W), lambda c: (c, 0)), pl.BlockSpec((2, HG_W), lambda c: (0, 0)),
                  pl.BlockSpec((1, BLOCK), lambda c: (0, 0)), pl.BlockSpec(lv.shape, lambda c: (0, 0))]
        + _att_specs(lambda c: c) + c_in,
        out_specs=[pl.BlockSpec((BLOCK, HG_W), lambda c: (c, 0)), pl.BlockSpec((BLOCK, ATT_QW), lambda c: (c, 0)),
                   pl.BlockSpec((1, HG_HEADS, BLOCK, BLOCK), lambda c: (c, 0, 0, 0)),
                   pl.BlockSpec((1, HG_HEADS, BLOCK, BLOCK), lambda c: (c, 0, 0, 0)),
                   pl.BlockSpec((BLOCK, HG_W), lambda c: (c, 0)),
                   pl.BlockSpec((1, 2, ATT_KEYS, 4 * BLOCK), lambda c: (c, 0, 0, 0))] + c_out,
        out_shape=[jax.ShapeDtypeStruct((p, HG_W), BF16), jax.ShapeDtypeStruct((p, ATT_QW), BF16),
                   jax.ShapeDtypeStruct((nb, HG_HEADS, BLOCK, BLOCK), F32),
                   jax.ShapeDtypeStruct((nb, HG_HEADS, BLOCK, BLOCK), BF16),
                   jax.ShapeDtypeStruct((p, HG_W), F32),
                   jax.ShapeDtypeStruct((nb, 2, ATT_KEYS, 4 * BLOCK), BF16)] + c_shapes,
        scratch_shapes=[pltpu.VMEM((HG_HEADS, BLOCK, BLOCK), F32)] + c_sems,
        compiler_params=_cparams(("arbitrary",)),
    )(proj_hg, lbounds, norm_g, lv, proj_att, proj_att, proj_att, cos, sin, cos, sin, cos, sin, sinks, *shards)


def _tile(rows, preferred):
    return preferred if rows % preferred == 0 else _row_tile(rows, preferred)


def _in_proj(h0b, w_in_t):
    p = h0b.shape[0]
    tm = _row_tile(p, 640)
    hg_end = 4 * HG_W

    def body(h_ref, w_ref, hg_ref, att_ref):
        h = h_ref[...]
        hg_ref[...] = _dot(h, w_ref[:hg_end, :], "nt")
        att_ref[...] = _dot(h, w_ref[hg_end:, :], "nt")

    row = lambda w: pl.BlockSpec((tm, w), lambda i: (i, 0))
    return pl.pallas_call(
        body, name="in_proj", grid=(p // tm,),
        in_specs=[row(D_MODEL), pl.BlockSpec((MIX_W, D_MODEL), lambda i: (0, 0), pipeline_mode=pl.Buffered(1))],
        out_specs=[row(hg_end), row(MIX_W - hg_end)],
        out_shape=[jax.ShapeDtypeStruct((p, hg_end), F32), jax.ShapeDtypeStruct((p, MIX_W - hg_end), F32)],
        compiler_params=_cparams(("arbitrary",)),
    )(h0b, w_in_t)


def _branch_mix(yh, oa, gates, w_bh, w_ba):
    y_hg = _dot(yh, w_bh, "nn")
    y_att = _dot(oa, w_ba, "nn")
    s1 = jax.nn.sigmoid(gates[:, :D_MODEL].astype(F32))
    s2 = jax.nn.sigmoid(gates[:, D_MODEL:].astype(F32))
    return s1 * y_hg + s2 * y_att, y_hg, y_att, s1, s2


def _mix_out_ln1(yh, oa, h0b, w_in_t, h0, w_bh, w_ba, w_out, g1, b1):
    p = yh.shape[0]
    tr = _tile(p, 320)

    def body(yh_ref, oa_ref, h0b_ref, wi_ref, h0_ref, wbh_ref, wba_ref, wo_ref, g1_ref, b1_ref,
             g_ref, mix_ref, h1_ref, h1b_ref, xh_ref, rs_ref):
        g_ref[...] = _dot(h0b_ref[...], wi_ref[MIX_W:, :], "nt").astype(BF16)
        mixin = _branch_mix(yh_ref[...], oa_ref[...], g_ref[...], wbh_ref[...], wba_ref[...])[0]
        mix_ref[...] = mixin.astype(BF16)
        xhat, rstd = _ln_stats(ALPHA * h0_ref[...] + _dot(mixin, wo_ref[...], "nn"))
        h1 = xhat * g1_ref[...] + b1_ref[...]
        h1_ref[...] = h1
        h1b_ref[...] = h1.astype(BF16)
        xh_ref[...] = xhat
        rs_ref[...] = rstd

    row = lambda w: pl.BlockSpec((tr, w), lambda i: (i, 0))
    const = lambda a: pl.BlockSpec(a.shape, lambda i: (0, 0))
    return pl.pallas_call(
        body, name="mix_out_ln1", grid=(p // tr,),
        in_specs=[row(HG_W), row(ATT_QW), row(D_MODEL),
                  pl.BlockSpec(w_in_t.shape, lambda i: (0, 0), pipeline_mode=pl.Buffered(1)), row(D_MODEL),
                  const(w_bh), const(w_ba), const(w_out), const(g1), const(b1)],
        out_specs=[row(2 * D_MODEL), row(D_MODEL), row(D_MODEL), row(D_MODEL), row(D_MODEL), row(1)],
        out_shape=[jax.ShapeDtypeStruct((p, 2 * D_MODEL), BF16), jax.ShapeDtypeStruct((p, D_MODEL), BF16),
                   jax.ShapeDtypeStruct((p, D_MODEL), F32), jax.ShapeDtypeStruct((p, D_MODEL), BF16),
                   jax.ShapeDtypeStruct((p, D_MODEL), F32), jax.ShapeDtypeStruct((p, 1), F32)],
        compiler_params=_cparams(("arbitrary",)),
    )(yh, oa, h0b, w_in_t, h0, w_bh, w_ba, w_out, g1, b1)


FF_T = D_FF // 2


def _ffn_in_swiglu(h1, w_fi_t):
    p = h1.shape[0]
    tm = _row_tile(p, 640)

    def body(h_ref, w_ref, au_ref, s_ref):
        au = _dot(h_ref[...], w_ref[...], "nt")
        au_ref[...] = au.astype(BF16)
        s_ref[...] = (jax.nn.silu(au[:, :FF_T]) * au[:, FF_T:]).astype(BF16)

    return pl.pallas_call(
        body, name="ffn_in_swiglu", grid=(D_FF // FF_T, p // tm),
        in_specs=[pl.BlockSpec((tm, D_MODEL), lambda j, i: (i, 0)), pl.BlockSpec((2 * FF_T, D_MODEL), lambda j, i: (j, 0))],
        out_specs=[pl.BlockSpec((tm, 2 * FF_T), lambda j, i: (i, j)), pl.BlockSpec((tm, FF_T), lambda j, i: (i, j))],
        out_shape=[jax.ShapeDtypeStruct((p, 2 * D_FF), BF16), jax.ShapeDtypeStruct((p, D_FF), BF16)],
        compiler_params=_cparams(("arbitrary", "arbitrary")),
    )(h1, w_fi_t)


def _ffn_out_loss(s, w_fo, h1, g2, b2, target):
    p = h1.shape[0]
    tr = _row_tile(p, 640)
    k = tr // BLOCK

    def body(*refs):
        s_ref, w_ref, h_ref, g_ref, b_ref = refs[:5]
        dr_ref, drb_ref, loss_ref, dg_ref, db_ref = refs[5 + k:]
        i = pl.program_id(0)
        xhat, rstd = _ln_stats(ALPHA * h_ref[...] + _dot(s_ref[...], w_ref[...], "nn"))
        y = xhat * g_ref[...] + b_ref[...]
        row = i * tr + lax.broadcasted_iota(jnp.int32, (tr, 1), 0)
        tgt = jnp.concatenate([r[...] for r in refs[5:5 + k]], axis=0)
        err = jnp.where(row >= BLOCK, y - tgt, 0.0)
        dr, dg, db = _ln_bwd(err * (1.0 / D_MODEL), xhat, rstd, g_ref[...])
        dr_ref[...] = dr
        drb_ref[...] = dr.astype(BF16)
        e2 = jnp.sum(err * err, axis=0, keepdims=True)
        part = e2[:, 0:BLOCK]
        for j in range(1, D_MODEL // BLOCK):
            part = part + e2[:, j * BLOCK:(j + 1) * BLOCK]
        part = part * (0.5 / D_MODEL)

        @pl.when(i == 0)
        def _():
            loss_ref[...] = part
            dg_ref[...] = dg
            db_ref[...] = db

        @pl.when(i > 0)
        def _():
            loss_ref[...] += part
            dg_ref[...] += dg
            db_ref[...] += db

    vec = pl.BlockSpec((1, D_MODEL), lambda i: (0, 0))
    rowsp = pl.BlockSpec((tr, D_MODEL), lambda i: (i, 0))
    return pl.pallas_call(
        body, name="ffn_out_loss", grid=(p // tr,),
        in_specs=[pl.BlockSpec((tr, D_FF), lambda i: (i, 0)), pl.BlockSpec((D_FF, D_MODEL), lambda i: (0, 0)),
                  rowsp, vec, vec] + _token_streams(tr),
        out_specs=[rowsp, rowsp, pl.BlockSpec((1, BLOCK), lambda i: (0, 0)), vec, vec],
        out_shape=[jax.ShapeDtypeStruct((p, D_MODEL), F32), jax.ShapeDtypeStruct((p, D_MODEL), BF16),
                   jax.ShapeDtypeStruct((1, BLOCK), F32), jax.ShapeDtypeStruct((1, D_MODEL), F32),
                   jax.ShapeDtypeStruct((1, D_MODEL), F32)],
        compiler_params=_cparams(("arbitrary",)),
    )(s, w_fo, h1, g2, b2, *([target] * k))


def _ffn_bwd(dr2, w_fo, au, w_fi_t):
    p = au.shape[0]
    tm = _tile(p, 320)

    def body(d_ref, wo_ref, au_ref, wi_ref, dau_ref, dh_ref):
        d = d_ref[...].astype(BF16)
        dh = ALPHA * d_ref[...]
        for j in range(D_FF // FF_T):
            a_cols = slice(2 * j * FF_T, (2 * j + 1) * FF_T)
            u_cols = slice((2 * j + 1) * FF_T, (2 * j + 2) * FF_T)
            ds = _dot(d, wo_ref[j * FF_T:(j + 1) * FF_T, :], "nt")
            _, vjp = jax.vjp(lambda a, u: jax.nn.silu(a) * u, au_ref[:, a_cols].astype(F32), au_ref[:, u_cols].astype(F32))
            da, du = vjp(ds)
            dau_ref[:, a_cols] = da.astype(BF16)
            dau_ref[:, u_cols] = du.astype(BF16)
            pair = slice(2 * j * FF_T, (2 * j + 2) * FF_T)
            dh = dh + _dot(dau_ref[:, pair], wi_ref[pair, :], "nn")
        dh_ref[...] = dh

    row = lambda w: pl.BlockSpec((tm, w), lambda i: (i, 0))
    kept = lambda a: pl.BlockSpec(a.shape, lambda i: (0, 0), pipeline_mode=pl.Buffered(1))
    return pl.pallas_call(
        body, name="ffn_bwd", grid=(p // tm,),
        in_specs=[row(D_MODEL), kept(w_fo), row(2 * D_FF), kept(w_fi_t)],
        out_specs=[row(2 * D_FF), row(D_MODEL)],
        out_shape=[jax.ShapeDtypeStruct((p, 2 * D_FF), BF16), jax.ShapeDtypeStruct((p, D_MODEL), F32)],
        compiler_params=_cparams(("arbitrary",)),
    )(dr2, w_fo, au, w_fi_t)


def _ln1_mix_bwd(dh1, xhat1, rstd1, g1, yh, oa, gates, mixin, w_bh, w_ba, w_out):
    p = yh.shape[0]
    tr = _tile(p, 320)
    nt = p // tr
    group = 2
    assert nt % group == 0, (p, tr)

    def body(dh_ref, xh_ref, rs_ref, g1_ref, yh_ref, oa_ref, g_ref, mix_ref, wbh_ref, wba_ref, wo_ref,
             dr_ref, dgt_ref, dyh_ref, doa_ref, dg_ref, db_ref, dwbh_ref, dwba_ref, dwo_ref,
             abh_ref, aba_ref, ao_ref, kept_l, kept_r):
        i = pl.program_id(0)
        dr, dg, db = _ln_bwd(dh_ref[...], xh_ref[...], rs_ref[...], g1_ref[...])
        dr_ref[...] = dr
        d = _dot(dr, wo_ref[...], "nt")
        _, y_hg, y_att, s1, s2 = _branch_mix(yh_ref[...], oa_ref[...], g_ref[...], wbh_ref[...], wba_ref[...])
        dy_hg = (d * s1).astype(BF16)
        dy_att = (d * s2).astype(BF16)
        dgt_ref[:, :D_MODEL] = (d * y_hg * s1 * (1.0 - s1)).astype(BF16)
        dgt_ref[:, D_MODEL:] = (d * y_att * s2 * (1.0 - s2)).astype(BF16)
        dyh_ref[...] = _dot(dy_hg, wbh_ref[...], "nt")
        doa_ref[...] = _dot(dy_att, wba_ref[...], "nt")

        rows = pl.ds(pl.multiple_of((i % group) * tr, tr), tr)
        kept_l[rows, :HG_W] = yh_ref[...]
        kept_l[rows, HG_W:HG_W + ATT_QW] = oa_ref[...]
        kept_l[rows, HG_W + ATT_QW:] = mix_ref[...]
        kept_r[rows, :D_MODEL] = dy_hg
        kept_r[rows, D_MODEL:2 * D_MODEL] = dy_att
        kept_r[rows, 2 * D_MODEL:] = dr.astype(BF16)

        @pl.when(i == 0)
        def _():
            dg_ref[...] = dg
            db_ref[...] = db
            for ref in (abh_ref, aba_ref, ao_ref):
                ref[...] = jnp.zeros_like(ref)

        @pl.when(i > 0)
        def _():
            dg_ref[...] += dg
            db_ref[...] += db

        @pl.when(i % group == group - 1)
        def _():
            abh_ref[...] += _dot(kept_l[:, :HG_W], kept_r[:, :D_MODEL], "tn")
            aba_ref[...] += _dot(kept_l[:, HG_W:HG_W + ATT_QW], kept_r[:, D_MODEL:2 * D_MODEL], "tn")
            ao_ref[...] += _dot(kept_l[:, HG_W + ATT_QW:], kept_r[:, 2 * D_MODEL:], "tn")

        @pl.when(i == nt - 1)
        def _():
            dwbh_ref[...] = abh_ref[...].astype(BF16)
            dwba_ref[...] = aba_ref[...].astype(BF16)
            dwo_ref[...] = ao_ref[...].astype(BF16)

    row = lambda w: pl.BlockSpec((tr, w), lambda i: (i, 0))
    const = lambda a: pl.BlockSpec(a.shape, lambda i: (0, 0), pipeline_mode=pl.Buffered(1))
    vec = pl.BlockSpec((1, D_MODEL), lambda i: (0, 0))
    weights = (w_bh, w_ba, w_out)
    return pl.pallas_call(
        body, name="ln1_mix_bwd", grid=(nt,),
        in_specs=[row(D_MODEL), row(D_MODEL), row(1), vec, row(HG_W), row(ATT_QW), row(2 * D_MODEL), row(D_MODEL)]
                 + [const(w) for w in weights],
        out_specs=[row(D_MODEL), row(2 * D_MODEL), row(HG_W), row(ATT_QW), vec, vec]
                  + [pl.BlockSpec(w.shape, lambda i: (0, 0)) for w in weights],
        out_shape=[jax.ShapeDtypeStruct((p, D_MODEL), F32), jax.ShapeDtypeStruct((p, 2 * D_MODEL), BF16),
                   jax.ShapeDtypeStruct((p, HG_W), F32), jax.ShapeDtypeStruct((p, ATT_QW), F32),
                   jax.ShapeDtypeStruct((1, D_MODEL), F32), jax.ShapeDtypeStruct((1, D_MODEL), F32)]
                  + [jax.ShapeDtypeStruct(w.shape, BF16) for w in weights],
        scratch_shapes=[pltpu.VMEM(w.shape, F32) for w in weights]
                       + [pltpu.VMEM((group * tr, HG_W + ATT_QW + D_MODEL), BF16),
                          pltpu.VMEM((group * tr, 3 * D_MODEL), BF16)],
        compiler_params=_cparams(("arbitrary",)),
    )(dh1, xhat1, rstd1, g1, yh, oa, gates, mixin, w_bh, w_ba, w_out)


MIX_W = 4 * HG_W + ATT_QW + 2 * ATT_KVW
ATT_KEYS = BLOCK + 2 * N_META


def _mixers_bwd(proj_hg, proj_att, lbounds, norm_g, lv, states, scores, raw, probs, cos, sin, sinks, dyh, doa,
                parts, swapped):
    p = proj_hg.shape[0]
    nb = p // BLOCK
    n = len(parts)
    kvw = 2 * ATT_KVW
    rev = lambda s: nb - 1 - s
    c_in, c_out, c_shapes, c_sems = _comm_specs(parts, N_PEERS)

    def body(*refs):
        (x_ref, lb_ref, ng_ref, lv_ref, st_ref, a_ref, raw_ref, pr_ref, cur_ref, prev_ref, meta_ref, cc, sc, cp, sp,
         cm, sm, sink_ref, dy_ref, do_ref) = refs[:20]
        part_refs = refs[20:20 + n]
        dx_ref, dlb_ref, dng_ref, dsink_ref = refs[20 + n:24 + n]
        recv_refs = refs[24 + n:24 + 2 * n]
        dcarry_ref, dkv_next_ref, dkv_meta_ref = refs[24 + 2 * n:27 + 2 * n]
        starts, waits = _scatter_behind(part_refs, recv_refs, refs[27 + 2 * n:], swapped)
        step = pl.program_id(0)
        c = rev(step)

        @pl.when(step == 0)
        def _():
            dcarry_ref[...] = jnp.zeros_like(dcarry_ref)
            dkv_next_ref[...] = jnp.zeros_like(dkv_next_ref)
            dkv_meta_ref[...] = jnp.zeros_like(dkv_meta_ref)
            dlb_ref[...] = jnp.zeros_like(dlb_ref)
            dng_ref[...] = jnp.zeros_like(dng_ref)
            dsink_ref[...] = jnp.zeros_like(dsink_ref)
            for start in starts:
                start()

        fh = _first_half(BLOCK)
        qs, kc, vc = _att_load(cur_ref, cc, sc, True)
        _, kp, vp = _att_load(prev_ref, cp, sp, False)
        km, vm = _att_load_meta(meta_ref, cm, sm)
        own4, band4, meta4 = _att_masks(c)
        att0 = 4 * HG_W
        dkm = dkp = dkc = dvm = dvp = dvc = 0.0
        dsinkrows = []
        for g in range(2):
            pr = pr_ref[0, g, :BLOCK, :].astype(F32)
            pr_meta = pr_ref[0, g, BLOCK:BLOCK + N_META, :].astype(F32)
            pr_sink = jnp.max(pr_ref[0, g, BLOCK + N_META:, :].astype(F32), axis=0, keepdims=True)
            _, values_vjp = jax.vjp(lambda *a, g=g: _att_values(*a, g, own4), pr, pr_meta, vc, vp, vm)
            dpr, dpr_meta, dvc_g, dvp_g, dvm_g = values_vjp(
                [do_ref[:, (2 * g + j) * BLOCK:(2 * g + j + 1) * BLOCK] for j in range(2)])
            ds, ds_meta, dsinkrow = _att_probs_bwd(pr, pr_meta, pr_sink, dpr, dpr_meta)
            _, scores_vjp = jax.vjp(lambda *a, g=g: _att_scores(*a, g, own4, band4, meta4),
                                    qs[2 * g], qs[2 * g + 1], kc, kp, km)
            dqa, dqb, dkc_g, dkp_g, dkm_g = scores_vjp((ds, ds_meta))
            for j, dq in enumerate((dqa, dqb)):
                dx_ref[:, att0 + (2 * g + j) * BLOCK:att0 + (2 * g + j + 1) * BLOCK] = _rope_t(
                    dq, cc[...], sc[...], fh).astype(BF16)
            dkm, dkp, dkc = dkm + dkm_g, dkp + dkp_g, dkc + dkc_g
            dvm, dvp, dvc = dvm + dvm_g, dvp + dvp_g, dvc + dvc_g
            dsinkrows.append(dsinkrow)
        ds0, ds1 = dsinkrows
        dkv_meta_ref[:, :BLOCK] += _rope_t(dkm, cm[PAD:BLOCK, :], sm[PAD:BLOCK, :], _first_half(N_META))
        dkv_meta_ref[:, BLOCK:] += dvm
        last = jnp.where(c == 0, 1.0, 0.0)
        to_meta_rows = lambda m: jnp.concatenate([jnp.zeros((PAD, BLOCK), F32), last * m], axis=0)
        dk = _rope_t(dkc, cc[...], sc[...], fh) + dkv_next_ref[:, :BLOCK] + to_meta_rows(dkv_meta_ref[:, :BLOCK])
        dv = dvc + dkv_next_ref[:, BLOCK:] + to_meta_rows(dkv_meta_ref[:, BLOCK:])
        dx_ref[:, att0 + ATT_QW:att0 + ATT_QW + ATT_KVW] = dk.astype(BF16)
        dx_ref[:, att0 + ATT_QW + ATT_KVW:] = dv.astype(BF16)
        dkv_next_ref[:, :BLOCK] = _rope_t(dkp, cp[...], sp[...], fh)
        dkv_next_ref[:, BLOCK:] = dvp
        sink_rows = []
        for dsg in (ds0, ds1):
            for j in range(4):
                tot = jnp.sum(dsg[:, j * BLOCK:(j + 1) * BLOCK], axis=1, keepdims=True)
                sink_rows.append(jnp.broadcast_to(tot, (1, BLOCK)))
        dsink_ref[...] += jnp.concatenate(sink_rows, axis=0)

        valid = (c * BLOCK + lax.broadcasted_iota(jnp.int32, (BLOCK, 1), 0)) >= PAD
        (logf, k), gates_vjp = jax.vjp(lambda hf, a0, a1: _hgrn_gates(hf, a0, a1, valid),
                                       x_ref[:, HG_W:2 * HG_W], lb_ref[0:1, :], lb_ref[1:2, :])
        lvv = lv_ref[...]
        e = _split_dot(lvv, logf, "nn")
        dng = jnp.zeros((1, BLOCK), F32)
        dk, dseg = [], []
        for h in range(HG_HEADS):
            sl = lambda part: x_ref[:, part * HG_W + h * BLOCK: part * HG_W + (h + 1) * BLOCK]
            hs = slice(h * BLOCK, (h + 1) * BLOCK)
            seg = _seg_blocks(e, h)
            _, norm_vjp = jax.vjp(_hgrn_norm, raw_ref[:, hs], sl(3), ng_ref[...])
            draw, dhg, dngh = norm_vjp(dy_ref[:, hs])
            _, mix_vjp = jax.vjp(_hgrn_mix, sl(0), k[:, hs], sl(2), st_ref[0, h], a_ref[0, h].astype(F32), *seg[:3])
            dhq, dkh, dhi, dst, da, *dseg_mix = mix_vjp((draw, dcarry_ref[h]))
            _, scores_vjp = jax.vjp(_hgrn_scores, sl(0), k[:, hs], *seg[3:])
            dhq2, dkh2, *dseg_lvl = scores_vjp(da)
            for part, val in ((0, dhq + dhq2), (2, dhi), (3, dhg)):
                dx_ref[:, part * HG_W + h * BLOCK: part * HG_W + (h + 1) * BLOCK] = val.astype(BF16)
            dk.append(dkh + dkh2)
            dseg.append(jnp.concatenate(dseg_mix + dseg_lvl, axis=0))
            dng = dng + dngh
            dcarry_ref[h] = dst
        dlogf = _split_dot(lvv, jnp.concatenate(dseg, axis=1), "tn")
        dhf, da0, da1 = gates_vjp((dlogf, jnp.concatenate(dk, axis=1)))
        dx_ref[:, HG_W:2 * HG_W] = dhf.astype(BF16)
        dlb_ref[0:1, :] += da0
        dlb_ref[1:2, :] += da1
        dng_ref[...] += dng

        @pl.when(step == nb - 1)
        def _():
            for wait in waits:
                wait()

    const = lambda shape: pl.BlockSpec(shape, lambda s: (0,) * len(shape))
    per_head = pl.BlockSpec((1, HG_HEADS, BLOCK, BLOCK), lambda s: (rev(s), 0, 0, 0))
    return pl.pallas_call(
        body, name="mixers_bwd", grid=(nb,),
        in_specs=[pl.BlockSpec((BLOCK, 4 * HG_W), lambda s: (rev(s), 0)), const((2, HG_W)), const((1, BLOCK)),
                  const(lv.shape), per_head, per_head, pl.BlockSpec((BLOCK, HG_W), lambda s: (rev(s), 0)),
                  pl.BlockSpec((1, 2, ATT_KEYS, 4 * BLOCK), lambda s: (rev(s), 0, 0, 0))]
        + _att_specs(rev)
        + [pl.BlockSpec((BLOCK, HG_W), lambda s: (rev(s), 0)), pl.BlockSpec((BLOCK, ATT_QW), lambda s: (rev(s), 0))]
        + c_in,
        out_specs=[pl.BlockSpec((BLOCK, MIX_W), lambda s: (rev(s), 0)), const((2, HG_W)), const((1, BLOCK)),
                   const((ATT_HEADS, BLOCK))] + c_out,
        out_shape=[jax.ShapeDtypeStruct((p, MIX_W), BF16), jax.ShapeDtypeStruct((2, HG_W), F32),
                   jax.ShapeDtypeStruct((1, BLOCK), F32), jax.ShapeDtypeStruct((ATT_HEADS, BLOCK), F32)] + c_shapes,
        scratch_shapes=[pltpu.VMEM((HG_HEADS, BLOCK, BLOCK), F32), pltpu.VMEM((BLOCK, kvw), F32),
                        pltpu.VMEM((N_META, kvw), F32)] + c_sems,
        compiler_params=_cparams(("arbitrary",)),
    )(proj_hg, lbounds, norm_g, lv, states, scores, raw, probs, proj_att, proj_att, proj_att, cos, sin, cos, sin,
      cos, sin, sinks, dyh, doa, *parts)


_HBM = pl.BlockSpec(memory_space=pltpu.HBM)
_SEM = pl.BlockSpec(memory_space=pltpu.SEMAPHORE)
_ORDERED_BY_DATA = pltpu.CompilerParams(has_side_effects=pltpu.SideEffectType.DATAFLOW_SIDE_EFFECTING)


def _exchange_copies(part_ref, land_ref, send_sems, recv_sems):
    place = _place()
    return [pltpu.make_async_remote_copy(
        src_ref=part_ref.at[_slot(_peer(place, flip), False)], dst_ref=land_ref.at[r], send_sem=send_sems.at[r],
        recv_sem=recv_sems.at[r], device_id=_peer(place, flip), device_id_type=MESH) for r, flip in enumerate(_FLIPS)]


def _exchange_start(parts, name):
    def body(part_ref, land_ref, send_sems, recv_sems, part_thru, land_thru, token):
        for cp in _exchange_copies(part_ref, land_ref, send_sems, recv_sems):
            cp.start()
        token[...] = jnp.zeros_like(token)

    land = (N_PEERS,) + parts.shape[1:]
    return pl.pallas_call(
        body, name=name,
        out_shape=(pltpu.SemaphoreType.DMA((N_PEERS,)), pltpu.SemaphoreType.DMA((N_PEERS,)),
                   pltpu.HBM(parts.shape, parts.dtype), pltpu.HBM(land, parts.dtype), jax.ShapeDtypeStruct((8, BLOCK), F32)),
        in_specs=(_HBM, _HBM), out_specs=(_SEM, _SEM, _HBM, _HBM, pl.BlockSpec(memory_space=pltpu.VMEM)),
        input_output_aliases={0: 2, 1: 3}, compiler_params=_ORDERED_BY_DATA,
    )(pltpu.with_memory_space_constraint(parts, pltpu.HBM),
      pltpu.with_memory_space_constraint(lax.empty(land, parts.dtype), pltpu.HBM))


def _exchange_wait(send_sems, recv_sems, part_thru, land_thru, after, name):
    def body(part_ref, land_ref, send_sems, recv_sems, after_ref, part_out, land_out):
        for cp in _exchange_copies(part_ref, land_ref, send_sems, recv_sems):
            cp.wait_send()
            cp.wait_recv()

    return pl.pallas_call(
        body, name=name,
        out_shape=(pltpu.HBM(part_thru.shape, part_thru.dtype), pltpu.HBM(land_thru.shape, land_thru.dtype)),
        in_specs=(_HBM, _HBM, _SEM, _SEM, pl.BlockSpec(memory_space=pl.ANY)), out_specs=(_HBM, _HBM),
        input_output_aliases={0: 0, 1: 1}, compiler_params=_ORDERED_BY_DATA,
    )(part_thru, land_thru, send_sems, recv_sems, after)


def _embed_bwd(dmix, dgates, w_in_t, dr1, xhat0, rstd0, g0):
    p = dmix.shape[0]
    tm = _row_tile(p, 640)
    nm = p // tm

    def body(a_ref, g_ref, w_ref, dr_ref, xh_ref, rs_ref, g0_ref, gx_ref, lead_ref, dg_ref, db_ref, buf_ref, sem):
        i = pl.program_id(0)
        first = pltpu.make_async_copy(buf_ref.at[0, pl.ds(BLOCK, tm - BLOCK)], gx_ref.at[pl.ds(0, tm - BLOCK)],
                                      sem.at[0])
        later = lambda t: pltpu.make_async_copy(buf_ref.at[t % 2], gx_ref.at[pl.ds(t * tm - BLOCK, tm)], sem.at[t % 2])

        @pl.when(i == 2)
        def _():
            first.wait()

        @pl.when(i > 2)
        def _():
            later(i - 2).wait()

        dh0 = (ALPHA * dr_ref[...] + _dot(a_ref[...], w_ref[:MIX_W, :], "nn")
               + _dot(g_ref[...], w_ref[MIX_W:, :], "nn"))
        row = i * tm + lax.broadcasted_iota(jnp.int32, (tm, 1), 0)
        dx, dg, db = _ln_bwd(jnp.where(row >= PAD, dh0, 0.0), xh_ref[...], rs_ref[...], g0_ref[...])
        buf_ref[i % 2] = dx

        @pl.when(i == 0)
        def _():
            lead_ref[...] = dx[:BLOCK]
            dg_ref[...] = dg
            db_ref[...] = db
            first.start()

        @pl.when(i > 0)
        def _():
            dg_ref[...] += dg
            db_ref[...] += db
            later(i).start()

        @pl.when(i == nm - 1)
        def _():
            for t in (nm - 2, nm - 1):
                if t >= 0:
                    (first if t == 0 else later(t)).wait()

    row = lambda w: pl.BlockSpec((tm, w), lambda i: (i, 0))
    vec = pl.BlockSpec((1, D_MODEL), lambda i: (0, 0))
    return pl.pallas_call(
        body, name="embed_bwd", grid=(nm,),
        in_specs=[row(dmix.shape[1]), row(dgates.shape[1]),
                  pl.BlockSpec(w_in_t.shape, lambda i: (0, 0), pipeline_mode=pl.Buffered(1)), row(D_MODEL), row(D_MODEL),
                  row(1), vec],
        out_specs=[pl.BlockSpec(memory_space=pl.ANY), pl.BlockSpec((BLOCK, D_MODEL), lambda i: (0, 0)), vec, vec],
        out_shape=[jax.ShapeDtypeStruct((p - BLOCK, D_MODEL), F32), jax.ShapeDtypeStruct((BLOCK, D_MODEL), F32),
                   jax.ShapeDtypeStruct((1, D_MODEL), F32), jax.ShapeDtypeStruct((1, D_MODEL), F32)],
        scratch_shapes=[pltpu.VMEM((2, tm, D_MODEL), F32), pltpu.SemaphoreType.DMA((2,))],
        compiler_params=_cparams(("arbitrary",)),
    )(dmix, dgates, w_in_t, dr1, xhat0, rstd0, g0)


_LATE = ("w_branch_hg", "w_branch_attn", "w_out", "w_ffn_in", "w_ffn_out")
_TRANSPOSED = ("w_in", "w_ffn_in")
_COLUMN_SHARDED = ("meta_tokens", "w_branch_hg", "w_branch_attn")
_SWAPPED = ("w_ffn_in",)


def _whole(name, gathered):
    _, r, c = gathered.shape
    if name in _COLUMN_SHARDED:
        return jnp.transpose(gathered, (1, 0, 2)).reshape(r, N_DEV * c)
    return gathered.reshape(N_DEV * r, c)


def _slots(name, whole):
    r, c = whole.shape
    if name in _COLUMN_SHARDED:
        return jnp.transpose(whole.reshape(r, N_DEV, c // N_DEV), (1, 0, 2))
    return whole.reshape(N_DEV, r // N_DEV, c)


def _device_step(x, target, meta_shard, ln_emb_g, ln_emb_b, w_in_shard, lbounds, norm_g, sinks, late_shards,
                 ln1_g, ln1_b, ln2_g, ln2_b):
    p = x.shape[0] + BLOCK
    lv = _level_stack()
    cos, sin = _rope_tables(p)
    swapped = [n in _SWAPPED for n in _LATE]

    h0, h0b, xhat0, rstd0, _, g_win = _embed_ln(x, meta_shard, w_in_shard, ln_emb_g, ln_emb_b)
    w_in = _whole("w_in", g_win)
    proj_hg, proj_att = _in_proj(h0b, w_in)
    yh, oa, states, scores, raw, probs, *gathered = _mixers_fwd(
        proj_hg, proj_att, lbounds, norm_g, lv, cos, sin, sinks, late_shards, swapped)
    w_bh, w_ba, w_out, w_fi, w_fo = [_whole(n, g) for n, g in zip(_LATE, gathered)]
    gates, mixin, h1, h1b, xhat1, rstd1 = _mix_out_ln1(yh, oa, h0b, w_in, h0, w_bh, w_ba, w_out, ln1_g, ln1_b)
    au, sw = _ffn_in_swiglu(h1b, w_fi)
    dr2, dr2b, loss_part, dg2, db2 = _ffn_out_loss(sw, w_fo, h1, ln2_g, ln2_b, target)

    mtn = functools.partial(_tiled_matmul_tn, tm=_row_tile(p, 1664), out_dtype=BF16)
    d_wfo = mtn(sw, dr2b, tk=FF_T, tn=D_MODEL, name="grad_w_ffn_out")
    dau, dh1 = _ffn_bwd(dr2, w_fo, au, w_fi)
    d_wfi = _weight_grad_t([dau], h1b, tk=4 * BLOCK, name="grad_w_ffn_in")
    dr1, dgates, dyh, doa, dg1, db1, d_wbh, d_wba, d_wout = _ln1_mix_bwd(
        dh1, xhat1, rstd1, ln1_g, yh, oa, gates, mixin, w_bh, w_ba, w_out)
    late_parts = [_slots(n, g) for n, g in zip(_LATE, (d_wbh, d_wba, d_wout, d_wfi, d_wfo))]
    dmix, d_lb, d_ng, d_sink, *late_recv = _mixers_bwd(
        proj_hg, proj_att, lbounds, norm_g, lv, states, scores, raw, probs, cos, sin, sinks, dyh, doa, late_parts,
        swapped)
    d_win = _weight_grad_t([dmix, dgates], h0b, tk=2 * BLOCK, name="grad_w_in")
    *win_flight, token = _exchange_start(_slots("w_in", d_win), "w_in_grads_start")
    grad_x, dlead, dg0, db0 = _embed_bwd(dmix, dgates, w_in, dr1, xhat0, rstd0, ln_emb_g + token[0:1, 0:1])

    small = dict(ln_emb_g=dg0, ln_emb_b=db0, hg_lower_bounds=d_lb, hg_norm_g=d_ng, ln1_g=dg1, ln1_b=db1, ln2_g=dg2,
                 ln2_b=db2)
    big = dict(zip(_LATE, zip(late_parts, late_recv)))
    return _pack_small(small, d_sink, dlead, loss_part), grad_x, big, win_flight


def _all_gather(arrs, dtypes, name):
    n = len(arrs)

    def body(*refs):
        ins, outs, stages = refs[:n], refs[n:2 * n], refs[2 * n:3 * n]
        send_sems, recv_sems, local_sems = refs[3 * n:]
        x, y, c = _place()
        sibling = (x, y, 1 - c)
        chips = [(1 - x, y), (x, 1 - y), (1 - x, 1 - y)]
        slot = lambda px, py, pc: 4 * px + 2 * py + pc

        def copy(w, k, block, to, from_stage=False):
            return pltpu.make_async_remote_copy(
                src_ref=stages[w] if from_stage else outs[w].at[slot(*block)], dst_ref=outs[w].at[slot(*block)],
                send_sem=send_sems.at[w, k], recv_sem=recv_sems.at[w, k], device_id=to, device_id_type=MESH)

        mine, first, passed = [], [], []
        for w in range(n):
            stages[w][...] = ins[w][...].astype(dtypes[w])
            mine.append(pltpu.make_async_copy(stages[w], outs[w].at[slot(x, y, c)], local_sems.at[w]))
            mine[-1].start()
        for w in range(n):
            first.append(copy(w, 0, (x, y, c), sibling, from_stage=True))
            first += [copy(w, 1 + j, (x, y, c), (*chip, c), from_stage=True) for j, chip in enumerate(chips)]
        for cp in first:
            cp.start()
        for j, chip in enumerate(chips):
            for w in range(n):
                copy(w, 1 + j, (*chip, c), (x, y, c)).wait_recv()
                passed.append(copy(w, 4 + j, (*chip, c), sibling))
                passed[-1].start()
        for w in range(n):
            copy(w, 0, sibling, (x, y, c)).wait_recv()
            for j, chip in enumerate(chips):
                copy(w, 4 + j, (*chip, 1 - c), (x, y, c)).wait_recv()
        for cp in first + passed:
            cp.wait_send()
        for cp in mine:
            cp.wait()

    return pl.pallas_call(
        body, name=name,
        in_specs=[pl.BlockSpec(memory_space=pltpu.VMEM)] * n,
        out_specs=[pl.BlockSpec(memory_space=pl.ANY)] * n,
        out_shape=[jax.ShapeDtypeStruct((N_DEV,) + a.shape, dt) for a, dt in zip(arrs, dtypes)],
        scratch_shapes=[pltpu.VMEM(a.shape, dt) for a, dt in zip(arrs, dtypes)]
        + [pltpu.SemaphoreType.DMA((n, 7)), pltpu.SemaphoreType.DMA((n, 7)), pltpu.SemaphoreType.DMA((n,))],
        compiler_params=pltpu.CompilerParams(vmem_limit_bytes=VMEM_LIMIT_BYTES),
    )(*arrs)


def _cast_shards(arrs):
    def body(*refs):
        for src, dst in zip(refs[:len(arrs)], refs[len(arrs):]):
            dst[...] = src[...].astype(BF16)

    return pl.pallas_call(body, name="cast_shards", out_shape=[jax.ShapeDtypeStruct(a.shape, BF16) for a in arrs],
                          compiler_params=pltpu.CompilerParams(vmem_limit_bytes=VMEM_LIMIT_BYTES))(*arrs)


def _shard_rows(rows):
    return rows if rows <= 512 else max(t for t in range(16, 353, 16) if rows % t == 0)


def _adamw_math(w, g, m, v):
    m = ADAM_B1 * m + (1.0 - ADAM_B1) * g
    v = ADAM_B2 * v + (1.0 - ADAM_B2) * (g * g)
    m_hat = m / (1.0 - ADAM_B1 ** ADAM_STEP)
    v_hat = v / (1.0 - ADAM_B2 ** ADAM_STEP)
    delta = -ADAM_LR * (m_hat / (jnp.sqrt(v_hat) + ADAM_EPS) + ADAM_WD * w)
    return delta, m, v


def _reduce_adamw(parts, recv, own_slot, w, m, v, name):
    r, cdim = w.shape
    tr = _shard_rows(r)

    def body(idx_ref, p_ref, r_ref, w_ref, m_ref, v_ref, g_out, d_out, m_out, v_out):
        g = p_ref[0].astype(F32)
        for j in range(N_PEERS):
            g = g + r_ref[j].astype(F32)
        d, mn, vn = _adamw_math(w_ref[...], g, m_ref[...], v_ref[...])
        g_out[...] = g
        d_out[...] = d
        m_out[...] = mn
        v_out[...] = vn

    flat = pl.BlockSpec((tr, cdim), lambda i, idx_ref: (i, 0))
    return pl.pallas_call(
        body, name=name,
        grid_spec=pltpu.PrefetchScalarGridSpec(
            num_scalar_prefetch=1, grid=(r // tr,),
            in_specs=[pl.BlockSpec((1, tr, cdim), lambda i, idx_ref: (idx_ref[0], i, 0)),
                      pl.BlockSpec((N_PEERS, tr, cdim), lambda i, idx_ref: (0, i, 0)), flat, flat, flat],
            out_specs=[flat] * 4),
        out_shape=[jax.ShapeDtypeStruct((r, cdim), F32)] * 4,
        compiler_params=_cparams(("arbitrary",)),
    )(own_slot, parts, recv, w, m, v)


def _adamw_plain(w, g, m, v, name):
    def body(w_ref, g_ref, m_ref, v_ref, d_out, m_out, v_out):
        d_out[...], m_out[...], v_out[...] = _adamw_math(w_ref[...], g_ref[...], m_ref[...], v_ref[...])

    return pl.pallas_call(body, name=name, out_shape=[jax.ShapeDtypeStruct(w.shape, F32)] * 3)(w, g, m, v)


_SMALL = (("ln_emb_g", (1, D_MODEL)), ("ln_emb_b", (1, D_MODEL)), ("hg_lower_bounds", (2, HG_W)),
          ("hg_norm_g", (1, BLOCK)), ("attn_sinks", (1, ATT_HEADS)), ("ln1_g", (1, D_MODEL)), ("ln1_b", (1, D_MODEL)),
          ("ln2_g", (1, D_MODEL)), ("ln2_b", (1, D_MODEL)))
_SMALL_ROW, _LOSS_ROW = {}, 0
for _name, (_rows, _) in _SMALL:
    _SMALL_ROW[_name], _LOSS_ROW = _LOSS_ROW, _LOSS_ROW + _rows
_META_ROW = 16
SMALL_ROWS = _META_ROW + N_META
assert _LOSS_ROW < _META_ROW


def _pack_small(grads, d_sink, dlead, loss_part):
    names = [n for n, _ in _SMALL if n != "attn_sinks"]

    def body(*refs):
        ins = dict(zip(names, refs))
        sink_ref, lead_ref, loss_ref, o_ref = refs[len(names):]
        o_ref[...] = jnp.zeros_like(o_ref)
        for name, (rows, cols) in _SMALL:
            if name != "attn_sinks":
                o_ref[_SMALL_ROW[name]:_SMALL_ROW[name] + rows, :cols] = ins[name][...]
        head = lax.broadcasted_iota(jnp.int32, (ATT_HEADS, BLOCK), 0)
        lane = lax.broadcasted_iota(jnp.int32, (ATT_HEADS, BLOCK), 1)
        o_ref[_SMALL_ROW["attn_sinks"]:_SMALL_ROW["attn_sinks"] + 1, :BLOCK] = jnp.sum(
            jnp.where(head == lane, sink_ref[...], 0.0), axis=0, keepdims=True)
        o_ref[_LOSS_ROW:_LOSS_ROW + 1, :BLOCK] = loss_ref[...]
        o_ref[_META_ROW:, :] = lead_ref[PAD:BLOCK, :]

    return pl.pallas_call(body, name="pack_small", out_shape=jax.ShapeDtypeStruct((SMALL_ROWS, D_MODEL), F32))(
        *[grads[n] for n in names], d_sink, dlead, loss_part)


def _small_reduce_adamw(gathered, weights, mom1, mom2):
    n = len(_SMALL)

    def body(*refs):
        g_ref, w_refs, m_refs, v_refs = refs[0], refs[1:1 + n], refs[1 + n:1 + 2 * n], refs[1 + 2 * n:1 + 3 * n]
        outs = refs[1 + 3 * n:1 + 7 * n]
        meta_out, loss_out, sum_ref = refs[1 + 7 * n:]
        total = g_ref[0]
        for s in range(1, N_DEV):
            total = total + g_ref[s]
        sum_ref[...] = total
        for i, (name, (rows, cols)) in enumerate(_SMALL):
            g = sum_ref[_SMALL_ROW[name]:_SMALL_ROW[name] + rows, :cols]
            d, mn, vn = _adamw_math(w_refs[i][...], g, m_refs[i][...], v_refs[i][...])
            for out, val in zip(outs[4 * i:4 * i + 4], (g, d, mn, vn)):
                out[...] = val
        meta_out[...] = sum_ref[_META_ROW:, :]
        loss_out[...] = jnp.broadcast_to(jnp.sum(sum_ref[_LOSS_ROW:_LOSS_ROW + 1, :BLOCK]), (1, BLOCK))

    per_param = [jax.ShapeDtypeStruct(shape, F32) for _, shape in _SMALL for _ in range(4)]
    res = pl.pallas_call(
        body, name="small_reduce_adamw",
        out_shape=per_param + [jax.ShapeDtypeStruct((N_META, D_MODEL), F32), jax.ShapeDtypeStruct((1, BLOCK), F32)],
        scratch_shapes=[pltpu.VMEM((SMALL_ROWS, D_MODEL), F32)],
    )(gathered, *[d[name] for d in (weights, mom1, mom2) for name, _ in _SMALL])
    return {name: res[4 * i:4 * i + 4] for i, (name, _) in enumerate(_SMALL)}, res[-2], res[-1]


_WEIGHTS = ("meta_tokens", "ln_emb_g", "ln_emb_b", "w_in", "hg_lower_bounds", "hg_norm_g", "attn_sinks",
            "w_branch_hg", "w_branch_attn", "w_out", "ln1_g", "ln1_b", "w_ffn_in", "w_ffn_out", "ln2_g", "ln2_b")


def kernel(x, meta_tokens, ln_emb_g, ln_emb_b, w_in, hg_lower_bounds, hg_norm_g, attn_sinks, w_branch_hg, w_branch_attn, w_out, ln1_g, ln1_b, w_ffn_in, w_ffn_out, ln2_g, ln2_b, loss_target, m_meta_tokens, m_ln_emb_g, m_ln_emb_b, m_w_in, m_hg_lower_bounds, m_hg_norm_g, m_attn_sinks, m_w_branch_hg, m_w_branch_attn, m_w_out, m_ln1_g, m_ln1_b, m_w_ffn_in, m_w_ffn_out, m_ln2_g, m_ln2_b, v_meta_tokens, v_ln_emb_g, v_ln_emb_b, v_w_in, v_hg_lower_bounds, v_hg_norm_g, v_attn_sinks, v_w_branch_hg, v_w_branch_attn, v_w_out, v_ln1_g, v_ln1_b, v_w_ffn_in, v_w_ffn_out, v_ln2_g, v_ln2_b):
    given = dict(locals())
    weights = {n: given[n] for n in _WEIGHTS}
    mom1 = {n: given["m_" + n] for n in _WEIGHTS}
    mom2 = {n: given["v_" + n] for n in _WEIGHTS}
    shard2d = lambda n, a: a.reshape(a.shape[-2:]).T if n in _TRANSPOSED else a.reshape(a.shape[-2:])

    w_in_shard, *late_shards = _cast_shards([shard2d(n, weights[n]) for n in ("w_in",) + _LATE])
    packed, grad_x, big, win_flight = _device_step(
        x[0], loss_target[0], meta_tokens, ln_emb_g.reshape(1, -1), ln_emb_b.reshape(1, -1), w_in_shard,
        hg_lower_bounds, hg_norm_g, attn_sinks, late_shards, ln1_g, ln1_b, ln2_g, ln2_b)

    place = _place()
    out = {}

    def reduce_adamw(n, parts, recv):
        own = _slot(place, n in _SWAPPED).astype(jnp.int32).reshape(1)
        res = _reduce_adamw(parts, recv, own, shard2d(n, weights[n]), shard2d(n, mom1[n]), shard2d(n, mom2[n]),
                            "adamw_" + n)
        out[n] = [(r.T if n in _TRANSPOSED else r).reshape(weights[n].shape) for r in res]

    for n, (parts, recv) in big.items():
        reduce_adamw(n, parts, recv)

    all_small, = _all_gather([packed], [F32], "gather_small")
    as_2d = lambda d: {n: d[n].reshape(shape) for n, shape in _SMALL}
    small_out, meta_whole, loss_row = _small_reduce_adamw(all_small, as_2d(weights), as_2d(mom1), as_2d(mom2))
    for n, res in small_out.items():
        out[n] = [r.reshape(weights[n].shape) for r in res]
    loss = loss_row[0, 0]
    g_meta_mine = lax.dynamic_index_in_dim(meta_whole.reshape(N_META, N_DEV, D_MODEL // N_DEV), _slot(place, False),
                                           axis=1, keepdims=False)
    out["meta_tokens"] = [g_meta_mine, *_adamw_plain(meta_tokens, g_meta_mine, m_meta_tokens, v_meta_tokens,
                                                     "adamw_meta")]

    reduce_adamw("w_in", *_exchange_wait(*win_flight, after=all_small, name="w_in_grads_wait"))

    return (loss, grad_x[None], *[out[n][0] for n in _WEIGHTS], *[out[n][1] for n in _WEIGHTS],
            *[out[n][2] for n in _WEIGHTS], *[out[n][3] for n in _WEIGHTS])
```

```python
import functools

import numpy as np
import jax
import jax.numpy as jnp
from jax import lax
from jax.experimental import pallas as pl
from jax.experimental.pallas import tpu as pltpu

F32 = jnp.float32
BF16 = jnp.bfloat16

D_MODEL = 1024
N_META = 16
BLOCK = 128
PAD = BLOCK - N_META
HG_HEADS = 4
HG_W = 512
ATT_HEADS = 8
HEAD_DIM = 64
ATT_QW = 512
ATT_KVW = 128
D_FF = 2816
EPS = 1e-5
ALPHA = 2.0 ** 0.25
ROPE_THETA = 10000.0
N_DEV = 8

ADAM_LR = 0.001
ADAM_B1 = 0.9
ADAM_B2 = 0.999
ADAM_EPS = 1e-08
ADAM_WD = 0.01
ADAM_STEP = 10

VMEM_LIMIT_BYTES = 56 * 1024 * 1024
MESH = pl.DeviceIdType.MESH

_LEVELS = (64, 32, 16, 8, 4, 2, 1)


def _cparams(sem):
    return pltpu.CompilerParams(dimension_semantics=sem, vmem_limit_bytes=VMEM_LIMIT_BYTES)


def _row_tile(rows, target):
    nb = rows // BLOCK
    best = 1
    for d in range(1, nb + 1):
        if nb % d == 0 and d * BLOCK <= target:
            best = d
    return best * BLOCK


_DN = {"nn": (((1,), (0,)), ((), ())), "nt": (((1,), (1,)), ((), ())), "tn": (((0,), (0,)), ((), ()))}


def _dot(a, b, form):
    return lax.dot_general(a.astype(BF16), b.astype(BF16), _DN[form], preferred_element_type=F32)


@functools.partial(jax.custom_vjp, nondiff_argnums=(2,))
def _mm(a, b, form):
    return _dot(a, b, form)


def _mm_fwd(a, b, form):
    a, b = a.astype(BF16), b.astype(BF16)
    return _dot(a, b, form), (a, b)


def _mm_bwd(form, res, g):
    a, b = res
    if form == "nn":
        return _dot(g, b, "nt"), _dot(a, g, "tn")
    if form == "nt":
        return _dot(g, b, "nn"), _dot(g, a, "tn")
    return _dot(b, g, "nt"), _dot(a, g, "nn")


_mm.defvjp(_mm_fwd, _mm_bwd)


def _split_dot(lv, x, form):
    return lax.dot_general(lv, x.astype(BF16), _DN[form], preferred_element_type=F32)


@jax.custom_vjp
def _swap_halves(x):
    return pltpu.roll(x, 64, 1)


_swap_halves.defvjp(lambda x: (pltpu.roll(x, 64, 1), None), lambda _, g: (pltpu.roll(g, 64, 1),))


def _tiled_matmul_tn(a, b, *, tm, tk, tn, out_dtype, name):
    m, k = a.shape
    n = b.shape[1]
    assert m % tm == 0 and k % tk == 0 and n % tn == 0, (name, a.shape, b.shape, tm, tk, tn)
    nm = m // tm

    def body(a_ref, b_ref, o_ref, acc_ref):
        mi = pl.program_id(2)

        @pl.when(mi == 0)
        def _():
            acc_ref[...] = jnp.zeros_like(acc_ref)

        acc_ref[...] += _dot(a_ref[...], b_ref[...], "tn")

        @pl.when(mi == nm - 1)
        def _():
            o_ref[...] = acc_ref[...].astype(out_dtype)

    return pl.pallas_call(
        body, name=name, grid=(k // tk, n // tn, nm),
        in_specs=[pl.BlockSpec((tm, tk), lambda kk, j, i: (i, kk)), pl.BlockSpec((tm, tn), lambda kk, j, i: (i, j))],
        out_specs=pl.BlockSpec((tk, tn), lambda kk, j, i: (kk, j)),
        out_shape=jax.ShapeDtypeStruct((k, n), out_dtype),
        scratch_shapes=[pltpu.VMEM((tk, tn), F32)],
        compiler_params=_cparams(("arbitrary", "arbitrary", "arbitrary")),
    )(a, b)


def _weight_grad_t(cots, h, *, tk, name):
    p, d = h.shape
    steps = [c.shape[1] // tk for c in cots]
    assert all(c.shape == (p, n * tk) for c, n in zip(cots, steps)), (name, [c.shape for c in cots], tk)
    first = [sum(steps[:i]) for i in range(len(cots))]

    def body(*refs):
        h_ref, o_ref = refs[len(cots)], refs[len(cots) + 1]
        k = pl.program_id(0)
        for c_ref, lo, n in zip(refs, first, steps):
            @pl.when((k >= lo) & (k < lo + n))
            def _(c_ref=c_ref):
                o_ref[...] = _dot(c_ref[...], h_ref[...], "tn").astype(BF16)

    cot_spec = lambda lo, n: pl.BlockSpec((p, tk), lambda k: (0, jnp.clip(k - lo, 0, n - 1)))
    return pl.pallas_call(
        body, name=name, grid=(sum(steps),),
        in_specs=[cot_spec(lo, n) for lo, n in zip(first, steps)]
                 + [pl.BlockSpec((p, d), lambda k: (0, 0), pipeline_mode=pl.Buffered(1))],
        out_specs=pl.BlockSpec((tk, d), lambda k: (k, 0)),
        out_shape=jax.ShapeDtypeStruct((sum(steps) * tk, d), BF16),
        compiler_params=_cparams(("arbitrary",)),
    )(*cots, h)


def _ln_stats(r):
    mu = jnp.mean(r, axis=-1, keepdims=True)
    xc = r - mu
    var = jnp.mean(xc * xc, axis=-1, keepdims=True)
    rstd = lax.rsqrt(var + EPS)
    return xc * rstd, rstd


def _ln_bwd(dy, xhat, rstd, g):
    dxhat = dy * g
    m1 = jnp.mean(dxhat, axis=-1, keepdims=True)
    m2 = jnp.mean(dxhat * xhat, axis=-1, keepdims=True)
    dr = rstd * (dxhat - m1 - xhat * m2)
    return dr, jnp.sum(dy * xhat, axis=0, keepdims=True), jnp.sum(dy, axis=0, keepdims=True)


N_SEG = 3 + len(_LEVELS)


def _level_stack():
    t = np.arange(BLOCK)[:, None]
    r = np.arange(BLOCK)[None, :]
    mats = [r <= t, r > t, np.ones((BLOCK, BLOCK), bool)]
    for h in _LEVELS:
        same = (t // (2 * h)) == (r // (2 * h))
        up_t, up_r = (t % (2 * h)) >= h, (r % (2 * h)) >= h
        mats.append(same & ((up_t & up_r & (r <= t)) | (~up_t & ~up_r & (r > t))))
    return jnp.asarray(np.concatenate(mats, axis=0).astype(np.float32), dtype=BF16)


def _hgrn_gates(hf, a0, a1, valid):
    lb = jax.nn.sigmoid(a0 - a1)
    fg = lb + (1.0 - lb) * jax.nn.sigmoid(hf)
    return jnp.where(valid, jnp.log(fg), 0.0), jnp.where(valid, 1.0 - fg, 0.0)


def _hgrn_scores(hq, k, *levels):
    q = jax.nn.silu(hq)
    rows = lax.broadcasted_iota(jnp.int32, (BLOCK, BLOCK), 0)
    cols = lax.broadcasted_iota(jnp.int32, (BLOCK, BLOCK), 1)
    a = jnp.where(rows == cols, jnp.sum(q * k, axis=-1, keepdims=True), 0.0)
    differ = jnp.bitwise_xor(rows, cols)
    for h, lvl in zip(_LEVELS, levels):
        decay = jnp.exp(lvl)
        pair = (cols < rows) & (differ >= h) & (differ < 2 * h)
        a = a + jnp.where(pair, _mm(q * decay, k * decay, "nt"), 0.0)
    return a


def _hgrn_mix(hq, k, v, st_in, a, seg_incl, seg_after, seg_total):
    o = _mm(jax.nn.silu(hq) * jnp.exp(seg_incl), st_in, "nt") + _mm(a, v, "nn")
    return o, st_in * jnp.exp(seg_total) + _mm(v, k * jnp.exp(seg_after), "tn")


def _hgrn_norm(o, hg, ng):
    return o * lax.rsqrt(jnp.mean(o * o, axis=-1, keepdims=True) + EPS) * ng * jax.nn.silu(hg)


def _seg_blocks(e, h):
    return [e[i * BLOCK:(i + 1) * BLOCK, h * BLOCK:(h + 1) * BLOCK] for i in range(N_SEG)]


def _rope(x, cos, sin, first_half):
    partner = jnp.where(first_half, -pltpu.roll(x, 96, 1), pltpu.roll(x, 32, 1))
    return x * cos + partner * sin


def _rope_t(g, cos, sin, first_half):
    u = g * sin
    partner = jnp.where(first_half, pltpu.roll(u, 96, 1), -pltpu.roll(u, 32, 1))
    return g * cos + partner


def _low_half(x):
    return lax.broadcasted_iota(jnp.int32, x.shape, 1) < HEAD_DIM


def _both_halves(x, g):
    sw = _swap_halves(x)
    return jnp.where(_low_half(x), x, sw) if g == 0 else jnp.where(_low_half(x), sw, x)


def _att_scores(qa, qb, kc, kp, km, g, own4, band4, meta4):
    low = _low_half(qa)
    q4 = jnp.concatenate([jnp.where(low, qa, 0.0), jnp.where(low, 0.0, qa),
                          jnp.where(low, qb, 0.0), jnp.where(low, 0.0, qb)], axis=0)
    scale = HEAD_DIM ** -0.5
    neg = jnp.finfo(F32).min
    s = jnp.where(own4, _mm(_both_halves(kc, g), q4, "nt"), _mm(_both_halves(kp, g), q4, "nt"))
    return (jnp.where(band4, s * scale, neg), jnp.where(meta4, _mm(_both_halves(km, g), q4, "nt") * scale, neg))


def _att_probs(s, sm, sinkrow):
    mx = jnp.maximum(jnp.maximum(jnp.max(s, axis=0, keepdims=True), jnp.max(sm, axis=0, keepdims=True)), sinkrow)
    p, pm, ps = jnp.exp(s - mx), jnp.exp(sm - mx), jnp.exp(sinkrow - mx)
    inv = 1.0 / (jnp.sum(p, axis=0, keepdims=True) + jnp.sum(pm, axis=0, keepdims=True) + ps)
    return p * inv, pm * inv, ps * inv


def _att_probs_bwd(p, pm, ps, dp, dpm):
    r = jnp.sum(p * dp, axis=0, keepdims=True) + jnp.sum(pm * dpm, axis=0, keepdims=True)
    return p * (dp - r), pm * (dpm - r), -ps * r


def _att_values(p, pm, vc, vp, vm, g, own4):
    o4 = (_mm(jnp.where(own4, p, 0.0), _both_halves(vc, g), "tn") + _mm(jnp.where(own4, 0.0, p), _both_halves(vp, g), "tn")
          + _mm(pm, _both_halves(vm, g), "tn"))
    tiles = []
    for j in range(2):
        upper = o4[(2 * j) * BLOCK:(2 * j + 1) * BLOCK]
        tiles.append(jnp.where(_low_half(upper), upper, o4[(2 * j + 1) * BLOCK:(2 * j + 2) * BLOCK]))
    return tiles


def _att_masks(blk_idx):
    kidx = lax.broadcasted_iota(jnp.int32, (BLOCK, BLOCK), 0)
    qrow = lax.broadcasted_iota(jnp.int32, (BLOCK, BLOCK), 1)
    own_side = kidx <= qrow
    pos_own = blk_idx * BLOCK + kidx - PAD
    ok_band = (own_side & (pos_own >= N_META)) | (~own_side & (pos_own - BLOCK >= N_META) & (blk_idx >= 1))
    qpos = blk_idx * BLOCK + lax.broadcasted_iota(jnp.int32, (N_META, BLOCK), 1) - PAD
    ok_meta = lax.broadcasted_iota(jnp.int32, (N_META, BLOCK), 0) <= qpos
    return [jnp.concatenate([m] * 4, axis=1) for m in (own_side, ok_band, ok_meta)]


def _token_streams(tr, tile_of=lambda i: i):
    k = tr // BLOCK
    return [pl.BlockSpec((BLOCK, D_MODEL), lambda i, j=j: (jnp.maximum(k * tile_of(i) - 1 + j, 0), 0))
            for j in range(k)]


def _embed_ln(x, meta_shard, w_in_shard, g0, b0):
    p = x.shape[0] + BLOCK
    tr = _row_tile(p, 640)
    k = tr // BLOCK
    nt = p // tr
    tile_of = lambda s: (s + 1) % nt
    shards = [meta_shard, w_in_shard]
    c_in, c_out, c_shapes, c_sems = _comm_specs(shards, N_DEV)

    def body(*refs):
        g_ref, b_ref = refs[k:k + 2]
        h_ref, hb_ref, xh_ref, rs_ref = refs[k + 4:k + 8]
        out_refs = refs[k + 8:k + 10]
        lead_ref, meta_ref = refs[k + 10:k + 12]
        starts, passes, waits = _gather_behind(refs[k + 2:k + 4], out_refs, refs[k + 12:], [False, False])
        s = pl.program_id(0)
        t = tile_of(s)

        @pl.when(s == 0)
        def _():
            lead_ref[...] = jnp.zeros_like(lead_ref)
            for start in starts:
                start()

        @pl.when(s == nt - 1)
        def _():
            for step in passes + waits:
                step()
            pltpu.sync_copy(out_refs[0], meta_ref)
            for d in range(N_DEV):
                lead_ref[PAD:BLOCK, d * BLOCK:(d + 1) * BLOCK] = meta_ref[d]

        first = jnp.where(t == 0, lead_ref[...], refs[0][...])
        xhat, rstd = _ln_stats(jnp.concatenate([first] + [r[...] for r in refs[1:k]], axis=0))
        row = t * tr + lax.broadcasted_iota(jnp.int32, (tr, 1), 0)
        h = jnp.where(row >= PAD, xhat * g_ref[...] + b_ref[...], 0.0)
        h_ref[...] = h
        hb_ref[...] = h.astype(BF16)
        xh_ref[...] = xhat
        rs_ref[...] = rstd

    vec = pl.BlockSpec((1, D_MODEL), lambda s: (0, 0))
    rowsp = pl.BlockSpec((tr, D_MODEL), lambda s: (tile_of(s), 0))
    return pl.pallas_call(
        body, name="embed_ln", grid=(nt,),
        in_specs=_token_streams(tr, tile_of) + [vec, vec] + c_in,
        out_specs=[rowsp, rowsp, rowsp, pl.BlockSpec((tr, 1), lambda s: (tile_of(s), 0))] + c_out,
        out_shape=[jax.ShapeDtypeStruct((p, D_MODEL), F32), jax.ShapeDtypeStruct((p, D_MODEL), BF16),
                   jax.ShapeDtypeStruct((p, D_MODEL), F32), jax.ShapeDtypeStruct((p, 1), F32)] + c_shapes,
        scratch_shapes=[pltpu.VMEM((BLOCK, D_MODEL), F32), pltpu.VMEM((N_DEV, N_META, BLOCK), F32)] + c_sems,
        compiler_params=_cparams(("arbitrary",)),
    )(*([x] * k), g0, b0, *shards)


def _rope_tables(p):
    pos = (np.arange(p, dtype=np.int32) - PAD).astype(np.float32)
    half = HEAD_DIM // 2
    inv = np.float32(ROPE_THETA) ** (-np.arange(half, dtype=np.float32) / np.float32(half))
    ang = pos[:, None] * np.tile(inv.astype(np.float32), BLOCK // half)[None, :]
    return jnp.asarray(np.cos(ang), F32), jnp.asarray(np.sin(ang), F32)


def _att_sinkrows(sink_ref):
    lanehead = lax.broadcasted_iota(jnp.int32, (1, 4 * BLOCK), 1) // BLOCK
    rows = []
    for g in range(2):
        row = jnp.zeros((1, 4 * BLOCK), F32)
        for j in range(4):
            row = jnp.where(lanehead == j, sink_ref[0, 4 * g + j], row)
        rows.append(row)
    return rows


def _first_half(rows):
    return (lax.broadcasted_iota(jnp.int32, (rows, BLOCK), 1) % HEAD_DIM) < (HEAD_DIM // 2)


def _att_load(qkv_ref, cos_ref, sin_ref, with_q):
    cos, sin, fh = cos_ref[...], sin_ref[...], _first_half(BLOCK)
    qs = [_rope(qkv_ref[:, j * BLOCK:(j + 1) * BLOCK], cos, sin, fh) for j in range(4)] if with_q else None
    k = _rope(qkv_ref[:, ATT_QW:ATT_QW + ATT_KVW], cos, sin, fh)
    v = qkv_ref[:, ATT_QW + ATT_KVW:ATT_QW + 2 * ATT_KVW]
    return qs, k, v


def _att_load_meta(qkv_ref, cos_ref, sin_ref):
    k = _rope(qkv_ref[PAD:BLOCK, ATT_QW:ATT_QW + ATT_KVW], cos_ref[PAD:BLOCK, :], sin_ref[PAD:BLOCK, :],
              _first_half(N_META))
    return k, qkv_ref[PAD:BLOCK, ATT_QW + ATT_KVW:ATT_QW + 2 * ATT_KVW]


def _att_specs(blk):
    w = ATT_QW + 2 * ATT_KVW
    cur = lambda width: pl.BlockSpec((BLOCK, width), lambda i: (blk(i), 0))
    prev = lambda width: pl.BlockSpec((BLOCK, width), lambda i: (jnp.maximum(blk(i) - 1, 0), 0))
    meta = lambda width: pl.BlockSpec((BLOCK, width), lambda i: (0, 0))
    return [cur(w), prev(w), meta(w), cur(BLOCK), cur(BLOCK), prev(BLOCK), prev(BLOCK), meta(BLOCK), meta(BLOCK),
            pl.BlockSpec(memory_space=pltpu.SMEM)]


_FLIPS = [(dx, dy, dc) for dx in (0, 1) for dy in (0, 1) for dc in (0, 1)][1:]
N_PEERS = len(_FLIPS)


def _place():
    return lax.axis_index("x"), lax.axis_index("y"), lax.axis_index("c")


def _peer(place, flip):
    return tuple(1 - p if f else p for p, f in zip(place, flip))


def _slot(place, swapped):
    x, y, c = place
    return 4 * y + 2 * x + c if swapped else 4 * x + 2 * y + c


def _comm_specs(arrs, out_lead):
    n = len(arrs)
    outs = [jax.ShapeDtypeStruct((out_lead,) + a.shape[-2:], a.dtype) for a in arrs]
    sems = [pltpu.SemaphoreType.DMA((n, N_PEERS)), pltpu.SemaphoreType.DMA((n, N_PEERS)), pltpu.SemaphoreType.DMA((n,))]
    return [pl.BlockSpec(memory_space=pl.ANY)] * n, [pl.BlockSpec(memory_space=pl.ANY)] * n, outs, sems


def _gather_behind(shard_refs, out_refs, sems, swapped):
    send_sems, recv_sems, local_sems = sems
    x, y, c = _place()
    me, sibling = (x, y, c), (x, y, 1 - c)
    chips = [(1 - x, y), (x, 1 - y), (1 - x, 1 - y)]
    starts, passes, waits = [], [], []
    for w, (s, o) in enumerate(zip(shard_refs, out_refs)):
        def copy(k, block, to, from_shard=False, w=w, s=s, o=o):
            rows = o.at[_slot(block, swapped[w])]
            return pltpu.make_async_remote_copy(
                src_ref=s if from_shard else rows, dst_ref=rows, send_sem=send_sems.at[w, k],
                recv_sem=recv_sems.at[w, k], device_id=to, device_id_type=MESH)

        own = pltpu.make_async_copy(s, o.at[_slot(me, swapped[w])], local_sems.at[w])
        first = [copy(0, me, sibling, True)] + [copy(1 + j, me, (*chip, c), True) for j, chip in enumerate(chips)]
        handed = [copy(4 + j, (*chip, c), sibling) for j, chip in enumerate(chips)]
        starts += [own.start] + [cp.start for cp in first]
        for j, chip in enumerate(chips):
            passes += [copy(1 + j, (*chip, c), me).wait_recv, handed[j].start]
        waits.append(copy(0, sibling, me).wait_recv)
        waits += [copy(4 + j, (*chip, 1 - c), me).wait_recv for j, chip in enumerate(chips)]
        waits += [cp.wait_send for cp in first + handed] + [own.wait]
    return starts, passes, waits


def _scatter_behind(part_refs, recv_refs, sems, swapped):
    send_sems, recv_sems, _ = sems
    place = _place()
    starts, waits = [], []
    for w, (p, o) in enumerate(zip(part_refs, recv_refs)):
        for r, flip in enumerate(_FLIPS):
            peer = _peer(place, flip)
            cp = pltpu.make_async_remote_copy(
                src_ref=p.at[_slot(peer, swapped[w])], dst_ref=o.at[r], send_sem=send_sems.at[w, r],
                recv_sem=recv_sems.at[w, r], device_id=peer, device_id_type=MESH)
            starts.append(cp.start)
            waits += [cp.wait_recv, cp.wait_send]
    return starts, waits


def _mixers_fwd(proj_hg, proj_att, lbounds, norm_g, lv, cos, sin, sinks, shards, swapped):
    p = proj_hg.shape[0]
    nb = p // BLOCK
    n = len(shards)
    c_in, c_out, c_shapes, c_sems = _comm_specs(shards, N_DEV)
    pass_step = min(nb - 1, max(1, (5 * nb) // 8))

    def body(*refs):
        x_ref, lb_ref, ng_ref, lv_ref, cur_ref, prev_ref, meta_ref, cc, sc, cp, sp, cm, sm, sink_ref = refs[:14]
        shard_refs = refs[14:14 + n]
        y_ref, o_ref, st_ref, a_ref, raw_ref, pr_ref = refs[14 + n:20 + n]
        out_refs = refs[20 + n:20 + 2 * n]
        carry_ref = refs[20 + 2 * n]
        starts, passes, waits = _gather_behind(shard_refs, out_refs, refs[21 + 2 * n:], swapped)
        c = pl.program_id(0)

        @pl.when(c == 0)
        def _():
            carry_ref[...] = jnp.zeros_like(carry_ref)
            for start in starts:
                start()

        @pl.when(c == pass_step)
        def _():
            for step in passes:
                step()

        valid = (c * BLOCK + lax.broadcasted_iota(jnp.int32, (BLOCK, 1), 0)) >= PAD
        logf, k = _hgrn_gates(x_ref[:, HG_W:2 * HG_W], lb_ref[0:1, :], lb_ref[1:2, :], valid)
        e = _split_dot(lv_ref[...], logf, "nn")
        for h in range(HG_HEADS):
            sl = lambda part: x_ref[:, part * HG_W + h * BLOCK: part * HG_W + (h + 1) * BLOCK]
            hs = slice(h * BLOCK, (h + 1) * BLOCK)
            st_in = carry_ref[h]
            st_ref[0, h] = st_in
            seg = _seg_blocks(e, h)
            a = _hgrn_scores(sl(0), k[:, hs], *seg[3:])
            a_ref[0, h] = a.astype(BF16)
            raw, st_out = _hgrn_mix(sl(0), k[:, hs], sl(2), st_in, a, *seg[:3])
            raw_ref[:, hs] = raw
            y_ref[:, hs] = _hgrn_norm(raw, sl(3), ng_ref[...]).astype(BF16)
            carry_ref[h] = st_out

        qs, kc, vc = _att_load(cur_ref, cc, sc, True)
        _, kp, vp = _att_load(prev_ref, cp, sp, False)
        km, vm = _att_load_meta(meta_ref, cm, sm)
        sinkrows = _att_sinkrows(sink_ref)
        own4, band4, meta4 = _att_masks(c)
        for g in range(2):
            s, s_meta = _att_scores(qs[2 * g], qs[2 * g + 1], kc, kp, km, g, own4, band4, meta4)
            pr, pr_meta, pr_sink = _att_probs(s, s_meta, sinkrows[g])
            pr_ref[0, g, :BLOCK, :] = pr.astype(BF16)
            pr_ref[0, g, BLOCK:BLOCK + N_META, :] = pr_meta.astype(BF16)
            pr_ref[0, g, BLOCK + N_META:, :] = jnp.broadcast_to(pr_sink, (N_META, 4 * BLOCK)).astype(BF16)
            for j, tile in enumerate(_att_values(pr, pr_meta, vc, vp, vm, g, own4)):
                o_ref[:, (2 * g + j) * BLOCK:(2 * g + j + 1) * BLOCK] = tile.astype(BF16)

        @pl.when(c == nb - 1)
        def _():
            for wait in waits:
                wait()

    return pl.pallas_call(
        body, name="mixers_fwd", grid=(nb,),
        in_specs=[pl.BlockSpec((BLOCK, 4 * HG_W), lambda c: (c, 0)), pl.BlockSpec((2, HG_W), lambda c: (0, 0)),
                  pl.BlockSpec((1, BLOCK), lambda c: (0, 0)), pl.BlockSpec(lv.shape, lambda c: (0, 0))]
        + _att_specs(lambda c: c) + c_in,
        out_specs=[pl.BlockSpec((BLOCK, HG_W), lambda c: (c, 0)), pl.BlockSpec((BLOCK, ATT_QW), lambda c: (c, 0)),
                   pl.BlockSpec((1, HG_HEADS, BLOCK, BLOCK), lambda c: (c, 0, 0, 0)),
                   pl.BlockSpec((1, HG_HEADS, BLOCK, BLOCK), lambda c: (c, 0, 0, 0)),
                   pl.BlockSpec((BLOCK, HG_W), lambda c: (c, 0)),
                   pl.BlockSpec((1, 2, ATT_KEYS, 4 * BLOCK), lambda c: (c, 0, 0, 0))] + c_out,
        out_shape=[jax.ShapeDtypeStruct((p, HG_W), BF16), jax.ShapeDtypeStruct((p, ATT_QW), BF16),
                   jax.ShapeDtypeStruct((nb, HG_HEADS, BLOCK, BLOCK), F32),
                   jax.ShapeDtypeStruct((nb, HG_HEADS, BLOCK, BLOCK), BF16),
                   jax.ShapeDtypeStruct((p, HG_W), F32),
                   jax.ShapeDtypeStruct((nb, 2, ATT_KEYS, 4 * BLOCK), BF16)] + c_shapes,
        scratch_shapes=[pltpu.VMEM((HG_HEADS, BLOCK, BLOCK), F32)] + c_sems,
        compiler_params=_cparams(("arbitrary",)),
    )(proj_hg, lbounds, norm_g, lv, proj_att, proj_att, proj_att, cos, sin, cos, sin, cos, sin, sinks, *shards)


def _tile(rows, preferred):
    return preferred if rows % preferred == 0 else _row_tile(rows, preferred)


def _in_proj(h0b, w_in_t):
    p = h0b.shape[0]
    tm = _row_tile(p, 640)
    hg_end = 4 * HG_W

    def body(h_ref, w_ref, hg_ref, att_ref):
        h = h_ref[...]
        hg_ref[...] = _dot(h, w_ref[:hg_end, :], "nt")
        att_ref[...] = _dot(h, w_ref[hg_end:, :], "nt")

    row = lambda w: pl.BlockSpec((tm, w), lambda i: (i, 0))
    return pl.pallas_call(
        body, name="in_proj", grid=(p // tm,),
        in_specs=[row(D_MODEL), pl.BlockSpec((MIX_W, D_MODEL), lambda i: (0, 0), pipeline_mode=pl.Buffered(1))],
        out_specs=[row(hg_end), row(MIX_W - hg_end)],
        out_shape=[jax.ShapeDtypeStruct((p, hg_end), F32), jax.ShapeDtypeStruct((p, MIX_W - hg_end), F32)],
        compiler_params=_cparams(("arbitrary",)),
    )(h0b, w_in_t)


def _branch_mix(yh, oa, gates, w_bh, w_ba):
    y_hg = _dot(yh, w_bh, "nn")
    y_att = _dot(oa, w_ba, "nn")
    s1 = jax.nn.sigmoid(gates[:, :D_MODEL].astype(F32))
    s2 = jax.nn.sigmoid(gates[:, D_MODEL:].astype(F32))
    return s1 * y_hg + s2 * y_att, y_hg, y_att, s1, s2


def _mix_out_ln1(yh, oa, h0b, w_in_t, h0, w_bh, w_ba, w_out, g1, b1):
    p = yh.shape[0]
    tr = _tile(p, 416)

    def body(yh_ref, oa_ref, h0b_ref, wi_ref, h0_ref, wbh_ref, wba_ref, wo_ref, g1_ref, b1_ref,
             g_ref, mix_ref, h1_ref, h1b_ref, xh_ref, rs_ref):
        g_ref[...] = _dot(h0b_ref[...], wi_ref[MIX_W:, :], "nt").astype(BF16)
        mixin = _branch_mix(yh_ref[...], oa_ref[...], g_ref[...], wbh_ref[...], wba_ref[...])[0]
        mix_ref[...] = mixin.astype(BF16)
        xhat, rstd = _ln_stats(ALPHA * h0_ref[...] + _dot(mixin, wo_ref[...], "nn"))
        h1 = xhat * g1_ref[...] + b1_ref[...]
        h1_ref[...] = h1
        h1b_ref[...] = h1.astype(BF16)
        xh_ref[...] = xhat
        rs_ref[...] = rstd

    row = lambda w: pl.BlockSpec((tr, w), lambda i: (i, 0))
    const = lambda a: pl.BlockSpec(a.shape, lambda i: (0, 0))
    return pl.pallas_call(
        body, name="mix_out_ln1", grid=(p // tr,),
        in_specs=[row(HG_W), row(ATT_QW), row(D_MODEL),
                  pl.BlockSpec(w_in_t.shape, lambda i: (0, 0), pipeline_mode=pl.Buffered(1)), row(D_MODEL),
                  const(w_bh), const(w_ba), const(w_out), const(g1), const(b1)],
        out_specs=[row(2 * D_MODEL), row(D_MODEL), row(D_MODEL), row(D_MODEL), row(D_MODEL), row(1)],
        out_shape=[jax.ShapeDtypeStruct((p, 2 * D_MODEL), BF16), jax.ShapeDtypeStruct((p, D_MODEL), BF16),
                   jax.ShapeDtypeStruct((p, D_MODEL), F32), jax.ShapeDtypeStruct((p, D_MODEL), BF16),
                   jax.ShapeDtypeStruct((p, D_MODEL), F32), jax.ShapeDtypeStruct((p, 1), F32)],
        compiler_params=_cparams(("arbitrary",)),
    )(yh, oa, h0b, w_in_t, h0, w_bh, w_ba, w_out, g1, b1)


FF_T = D_FF // 2


def _ffn_in_swiglu(h1, w_fi_t):
    p = h1.shape[0]
    tm = _row_tile(p, 640)

    def body(h_ref, w_ref, au_ref, s_ref):
        au = _dot(h_ref[...], w_ref[...], "nt")
        au_ref[...] = au.astype(BF16)
        s_ref[...] = (jax.nn.silu(au[:, :FF_T]) * au[:, FF_T:]).astype(BF16)

    return pl.pallas_call(
        body, name="ffn_in_swiglu", grid=(D_FF // FF_T, p // tm),
        in_specs=[pl.BlockSpec((tm, D_MODEL), lambda j, i: (i, 0)), pl.BlockSpec((2 * FF_T, D_MODEL), lambda j, i: (j, 0))],
        out_specs=[pl.BlockSpec((tm, 2 * FF_T), lambda j, i: (i, j)), pl.BlockSpec((tm, FF_T), lambda j, i: (i, j))],
        out_shape=[jax.ShapeDtypeStruct((p, 2 * D_FF), BF16), jax.ShapeDtypeStruct((p, D_FF), BF16)],
        compiler_params=_cparams(("arbitrary", "arbitrary")),
    )(h1, w_fi_t)


def _ffn_out_loss(s, w_fo, h1, g2, b2, target):
    p = h1.shape[0]
    tr = _row_tile(p, 640)
    k = tr // BLOCK

    def body(*refs):
        s_ref, w_ref, h_ref, g_ref, b_ref = refs[:5]
        dr_ref, drb_ref, loss_ref, dg_ref, db_ref = refs[5 + k:]
        i = pl.program_id(0)
        xhat, rstd = _ln_stats(ALPHA * h_ref[...] + _dot(s_ref[...], w_ref[...], "nn"))
        y = xhat * g_ref[...] + b_ref[...]
        row = i * tr + lax.broadcasted_iota(jnp.int32, (tr, 1), 0)
        tgt = jnp.concatenate([r[...] for r in refs[5:5 + k]], axis=0)
        err = jnp.where(row >= BLOCK, y - tgt, 0.0)
        dr, dg, db = _ln_bwd(err * (1.0 / D_MODEL), xhat, rstd, g_ref[...])
        dr_ref[...] = dr
        drb_ref[...] = dr.astype(BF16)
        e2 = jnp.sum(err * err, axis=0, keepdims=True)
        part = e2[:, 0:BLOCK]
        for j in range(1, D_MODEL // BLOCK):
            part = part + e2[:, j * BLOCK:(j + 1) * BLOCK]
        part = part * (0.5 / D_MODEL)

        @pl.when(i == 0)
        def _():
            loss_ref[...] = part
            dg_ref[...] = dg
            db_ref[...] = db

        @pl.when(i > 0)
        def _():
            loss_ref[...] += part
            dg_ref[...] += dg
            db_ref[...] += db

    vec = pl.BlockSpec((1, D_MODEL), lambda i: (0, 0))
    rowsp = pl.BlockSpec((tr, D_MODEL), lambda i: (i, 0))
    return pl.pallas_call(
        body, name="ffn_out_loss", grid=(p // tr,),
        in_specs=[pl.BlockSpec((tr, D_FF), lambda i: (i, 0)), pl.BlockSpec((D_FF, D_MODEL), lambda i: (0, 0)),
                  rowsp, vec, vec] + _token_streams(tr),
        out_specs=[rowsp, rowsp, pl.BlockSpec((1, BLOCK), lambda i: (0, 0)), vec, vec],
        out_shape=[jax.ShapeDtypeStruct((p, D_MODEL), F32), jax.ShapeDtypeStruct((p, D_MODEL), BF16),
                   jax.ShapeDtypeStruct((1, BLOCK), F32), jax.ShapeDtypeStruct((1, D_MODEL), F32),
                   jax.ShapeDtypeStruct((1, D_MODEL), F32)],
        compiler_params=_cparams(("arbitrary",)),
    )(s, w_fo, h1, g2, b2, *([target] * k))


def _ffn_bwd(dr2, w_fo, au, w_fi_t):
    p = au.shape[0]
    tm = _tile(p, 416)

    def body(d_ref, wo_ref, au_ref, wi_ref, dau_ref, dh_ref):
        d = d_ref[...].astype(BF16)
        dh = ALPHA * d_ref[...]
        for j in range(D_FF // FF_T):
            a_cols = slice(2 * j * FF_T, (2 * j + 1) * FF_T)
            u_cols = slice((2 * j + 1) * FF_T, (2 * j + 2) * FF_T)
            ds = _dot(d, wo_ref[j * FF_T:(j + 1) * FF_T, :], "nt")
            _, vjp = jax.vjp(lambda a, u: jax.nn.silu(a) * u, au_ref[:, a_cols].astype(F32), au_ref[:, u_cols].astype(F32))
            da, du = vjp(ds)
            dau_ref[:, a_cols] = da.astype(BF16)
            dau_ref[:, u_cols] = du.astype(BF16)
            pair = slice(2 * j * FF_T, (2 * j + 2) * FF_T)
            dh = dh + _dot(dau_ref[:, pair], wi_ref[pair, :], "nn")
        dh_ref[...] = dh

    row = lambda w: pl.BlockSpec((tm, w), lambda i: (i, 0))
    kept = lambda a: pl.BlockSpec(a.shape, lambda i: (0, 0), pipeline_mode=pl.Buffered(1))
    return pl.pallas_call(
        body, name="ffn_bwd", grid=(p // tm,),
        in_specs=[row(D_MODEL), kept(w_fo), row(2 * D_FF), kept(w_fi_t)],
        out_specs=[row(2 * D_FF), row(D_MODEL)],
        out_shape=[jax.ShapeDtypeStruct((p, 2 * D_FF), BF16), jax.ShapeDtypeStruct((p, D_MODEL), F32)],
        compiler_params=_cparams(("arbitrary",)),
    )(dr2, w_fo, au, w_fi_t)


def _ln1_mix_bwd(dh1, xhat1, rstd1, g1, yh, oa, gates, mixin, w_bh, w_ba, w_out):
    p = yh.shape[0]
    tr = _tile(p, 320)
    nt = p // tr
    group = 2
    assert nt % group == 0, (p, tr)

    def body(dh_ref, xh_ref, rs_ref, g1_ref, yh_ref, oa_ref, g_ref, mix_ref, wbh_ref, wba_ref, wo_ref,
             dr_ref, dgt_ref, dyh_ref, doa_ref, dg_ref, db_ref, dwbh_ref, dwba_ref, dwo_ref,
             abh_ref, aba_ref, ao_ref, kept_l, kept_r):
        i = pl.program_id(0)
        dr, dg, db = _ln_bwd(dh_ref[...], xh_ref[...], rs_ref[...], g1_ref[...])
        dr_ref[...] = dr
        d = _dot(dr, wo_ref[...], "nt")
        _, y_hg, y_att, s1, s2 = _branch_mix(yh_ref[...], oa_ref[...], g_ref[...], wbh_ref[...], wba_ref[...])
        dy_hg = (d * s1).astype(BF16)
        dy_att = (d * s2).astype(BF16)
        dgt_ref[:, :D_MODEL] = (d * y_hg * s1 * (1.0 - s1)).astype(BF16)
        dgt_ref[:, D_MODEL:] = (d * y_att * s2 * (1.0 - s2)).astype(BF16)
        dyh_ref[...] = _dot(dy_hg, wbh_ref[...], "nt")
        doa_ref[...] = _dot(dy_att, wba_ref[...], "nt")

        rows = pl.ds(pl.multiple_of((i % group) * tr, tr), tr)
        kept_l[rows, :HG_W] = yh_ref[...]
        kept_l[rows, HG_W:HG_W + ATT_QW] = oa_ref[...]
        kept_l[rows, HG_W + ATT_QW:] = mix_ref[...]
        kept_r[rows, :D_MODEL] = dy_hg
        kept_r[rows, D_MODEL:2 * D_MODEL] = dy_att
        kept_r[rows, 2 * D_MODEL:] = dr.astype(BF16)

        @pl.when(i == 0)
        def _():
            dg_ref[...] = dg
            db_ref[...] = db
            for ref in (abh_ref, aba_ref, ao_ref):
                ref[...] = jnp.zeros_like(ref)

        @pl.when(i > 0)
        def _():
            dg_ref[...] += dg
            db_ref[...] += db

        @pl.when(i % group == group - 1)
        def _():
            abh_ref[...] += _dot(kept_l[:, :HG_W], kept_r[:, :D_MODEL], "tn")
            aba_ref[...] += _dot(kept_l[:, HG_W:HG_W + ATT_QW], kept_r[:, D_MODEL:2 * D_MODEL], "tn")
            ao_ref[...] += _dot(kept_l[:, HG_W + ATT_QW:], kept_r[:, 2 * D_MODEL:], "tn")

        @pl.when(i == nt - 1)
        def _():
            dwbh_ref[...] = abh_ref[...].astype(BF16)
            dwba_ref[...] = aba_ref[...].astype(BF16)
            dwo_ref[...] = ao_ref[...].astype(BF16)

    row = lambda w: pl.BlockSpec((tr, w), lambda i: (i, 0))
    const = lambda a: pl.BlockSpec(a.shape, lambda i: (0, 0), pipeline_mode=pl.Buffered(1))
    vec = pl.BlockSpec((1, D_MODEL), lambda i: (0, 0))
    weights = (w_bh, w_ba, w_out)
    return pl.pallas_call(
        body, name="ln1_mix_bwd", grid=(nt,),
        in_specs=[row(D_MODEL), row(D_MODEL), row(1), vec, row(HG_W), row(ATT_QW), row(2 * D_MODEL), row(D_MODEL)]
                 + [const(w) for w in weights],
        out_specs=[row(D_MODEL), row(2 * D_MODEL), row(HG_W), row(ATT_QW), vec, vec]
                  + [pl.BlockSpec(w.shape, lambda i: (0, 0)) for w in weights],
        out_shape=[jax.ShapeDtypeStruct((p, D_MODEL), F32), jax.ShapeDtypeStruct((p, 2 * D_MODEL), BF16),
                   jax.ShapeDtypeStruct((p, HG_W), F32), jax.ShapeDtypeStruct((p, ATT_QW), F32),
                   jax.ShapeDtypeStruct((1, D_MODEL), F32), jax.ShapeDtypeStruct((1, D_MODEL), F32)]
                  + [jax.ShapeDtypeStruct(w.shape, BF16) for w in weights],
        scratch_shapes=[pltpu.VMEM(w.shape, F32) for w in weights]
                       + [pltpu.VMEM((group * tr, HG_W + ATT_QW + D_MODEL), BF16),
                          pltpu.VMEM((group * tr, 3 * D_MODEL), BF16)],
        compiler_params=_cparams(("arbitrary",)),
    )(dh1, xhat1, rstd1, g1, yh, oa, gates, mixin, w_bh, w_ba, w_out)


MIX_W = 4 * HG_W + ATT_QW + 2 * ATT_KVW
ATT_KEYS = BLOCK + 2 * N_META


def _mixers_bwd(proj_hg, proj_att, lbounds, norm_g, lv, states, scores, raw, probs, cos, sin, sinks, dyh, doa,
                parts, swapped):
    p = proj_hg.shape[0]
    nb = p // BLOCK
    n = len(parts)
    kvw = 2 * ATT_KVW
    rev = lambda s: nb - 1 - s
    c_in, c_out, c_shapes, c_sems = _comm_specs(parts, N_PEERS)

    def body(*refs):
        (x_ref, lb_ref, ng_ref, lv_ref, st_ref, a_ref, raw_ref, pr_ref, cur_ref, prev_ref, meta_ref, cc, sc, cp, sp,
         cm, sm, sink_ref, dy_ref, do_ref) = refs[:20]
        part_refs = refs[20:20 + n]
        dx_ref, dlb_ref, dng_ref, dsink_ref = refs[20 + n:24 + n]
        recv_refs = refs[24 + n:24 + 2 * n]
        dcarry_ref, dkv_next_ref, dkv_meta_ref = refs[24 + 2 * n:27 + 2 * n]
        starts, waits = _scatter_behind(part_refs, recv_refs, refs[27 + 2 * n:], swapped)
        step = pl.program_id(0)
        c = rev(step)

        @pl.when(step == 0)
        def _():
            dcarry_ref[...] = jnp.zeros_like(dcarry_ref)
            dkv_next_ref[...] = jnp.zeros_like(dkv_next_ref)
            dkv_meta_ref[...] = jnp.zeros_like(dkv_meta_ref)
            dlb_ref[...] = jnp.zeros_like(dlb_ref)
            dng_ref[...] = jnp.zeros_like(dng_ref)
            dsink_ref[...] = jnp.zeros_like(dsink_ref)
            for start in starts:
                start()

        fh = _first_half(BLOCK)
        qs, kc, vc = _att_load(cur_ref, cc, sc, True)
        _, kp, vp = _att_load(prev_ref, cp, sp, False)
        km, vm = _att_load_meta(meta_ref, cm, sm)
        own4, band4, meta4 = _att_masks(c)
        att0 = 4 * HG_W
        dkm = dkp = dkc = dvm = dvp = dvc = 0.0
        dsinkrows = []
        for g in range(2):
            pr = pr_ref[0, g, :BLOCK, :].astype(F32)
            pr_meta = pr_ref[0, g, BLOCK:BLOCK + N_META, :].astype(F32)
            pr_sink = jnp.max(pr_ref[0, g, BLOCK + N_META:, :].astype(F32), axis=0, keepdims=True)
            _, values_vjp = jax.vjp(lambda *a, g=g: _att_values(*a, g, own4), pr, pr_meta, vc, vp, vm)
            dpr, dpr_meta, dvc_g, dvp_g, dvm_g = values_vjp(
                [do_ref[:, (2 * g + j) * BLOCK:(2 * g + j + 1) * BLOCK] for j in range(2)])
            ds, ds_meta, dsinkrow = _att_probs_bwd(pr, pr_meta, pr_sink, dpr, dpr_meta)
            _, scores_vjp = jax.vjp(lambda *a, g=g: _att_scores(*a, g, own4, band4, meta4),
                                    qs[2 * g], qs[2 * g + 1], kc, kp, km)
            dqa, dqb, dkc_g, dkp_g, dkm_g = scores_vjp((ds, ds_meta))
            for j, dq in enumerate((dqa, dqb)):
                dx_ref[:, att0 + (2 * g + j) * BLOCK:att0 + (2 * g + j + 1) * BLOCK] = _rope_t(
                    dq, cc[...], sc[...], fh).astype(BF16)
            dkm, dkp, dkc = dkm + dkm_g, dkp + dkp_g, dkc + dkc_g
            dvm, dvp, dvc = dvm + dvm_g, dvp + dvp_g, dvc + dvc_g
            dsinkrows.append(dsinkrow)
        ds0, ds1 = dsinkrows
        dkv_meta_ref[:, :BLOCK] += _rope_t(dkm, cm[PAD:BLOCK, :], sm[PAD:BLOCK, :], _first_half(N_META))
        dkv_meta_ref[:, BLOCK:] += dvm
        last = jnp.where(c == 0, 1.0, 0.0)
        to_meta_rows = lambda m: jnp.concatenate([jnp.zeros((PAD, BLOCK), F32), last * m], axis=0)
        dk = _rope_t(dkc, cc[...], sc[...], fh) + dkv_next_ref[:, :BLOCK] + to_meta_rows(dkv_meta_ref[:, :BLOCK])
        dv = dvc + dkv_next_ref[:, BLOCK:] + to_meta_rows(dkv_meta_ref[:, BLOCK:])
        dx_ref[:, att0 + ATT_QW:att0 + ATT_QW + ATT_KVW] = dk.astype(BF16)
        dx_ref[:, att0 + ATT_QW + ATT_KVW:] = dv.astype(BF16)
        dkv_next_ref[:, :BLOCK] = _rope_t(dkp, cp[...], sp[...], fh)
        dkv_next_ref[:, BLOCK:] = dvp
        sink_rows = []
        for dsg in (ds0, ds1):
            for j in range(4):
                tot = jnp.sum(dsg[:, j * BLOCK:(j + 1) * BLOCK], axis=1, keepdims=True)
                sink_rows.append(jnp.broadcast_to(tot, (1, BLOCK)))
        dsink_ref[...] += jnp.concatenate(sink_rows, axis=0)

        valid = (c * BLOCK + lax.broadcasted_iota(jnp.int32, (BLOCK, 1), 0)) >= PAD
        (logf, k), gates_vjp = jax.vjp(lambda hf, a0, a1: _hgrn_gates(hf, a0, a1, valid),
                                       x_ref[:, HG_W:2 * HG_W], lb_ref[0:1, :], lb_ref[1:2, :])
        lvv = lv_ref[...]
        e = _split_dot(lvv, logf, "nn")
        dng = jnp.zeros((1, BLOCK), F32)
        dk, dseg = [], []
        for h in range(HG_HEADS):
            sl = lambda part: x_ref[:, part * HG_W + h * BLOCK: part * HG_W + (h + 1) * BLOCK]
            hs = slice(h * BLOCK, (h + 1) * BLOCK)
            seg = _seg_blocks(e, h)
            _, norm_vjp = jax.vjp(_hgrn_norm, raw_ref[:, hs], sl(3), ng_ref[...])
            draw, dhg, dngh = norm_vjp(dy_ref[:, hs])
            _, mix_vjp = jax.vjp(_hgrn_mix, sl(0), k[:, hs], sl(2), st_ref[0, h], a_ref[0, h].astype(F32), *seg[:3])
            dhq, dkh, dhi, dst, da, *dseg_mix = mix_vjp((draw, dcarry_ref[h]))
            _, scores_vjp = jax.vjp(_hgrn_scores, sl(0), k[:, hs], *seg[3:])
            dhq2, dkh2, *dseg_lvl = scores_vjp(da)
            for part, val in ((0, dhq + dhq2), (2, dhi), (3, dhg)):
                dx_ref[:, part * HG_W + h * BLOCK: part * HG_W + (h + 1) * BLOCK] = val.astype(BF16)
            dk.append(dkh + dkh2)
            dseg.append(jnp.concatenate(dseg_mix + dseg_lvl, axis=0))
            dng = dng + dngh
            dcarry_ref[h] = dst
        dlogf = _split_dot(lvv, jnp.concatenate(dseg, axis=1), "tn")
        dhf, da0, da1 = gates_vjp((dlogf, jnp.concatenate(dk, axis=1)))
        dx_ref[:, HG_W:2 * HG_W] = dhf.astype(BF16)
        dlb_ref[0:1, :] += da0
        dlb_ref[1:2, :] += da1
        dng_ref[...] += dng

        @pl.when(step == nb - 1)
        def _():
            for wait in waits:
                wait()

    const = lambda shape: pl.BlockSpec(shape, lambda s: (0,) * len(shape))
    per_head = pl.BlockSpec((1, HG_HEADS, BLOCK, BLOCK), lambda s: (rev(s), 0, 0, 0))
    return pl.pallas_call(
        body, name="mixers_bwd", grid=(nb,),
        in_specs=[pl.BlockSpec((BLOCK, 4 * HG_W), lambda s: (rev(s), 0)), const((2, HG_W)), const((1, BLOCK)),
                  const(lv.shape), per_head, per_head, pl.BlockSpec((BLOCK, HG_W), lambda s: (rev(s), 0)),
                  pl.BlockSpec((1, 2, ATT_KEYS, 4 * BLOCK), lambda s: (rev(s), 0, 0, 0))]
        + _att_specs(rev)
        + [pl.BlockSpec((BLOCK, HG_W), lambda s: (rev(s), 0)), pl.BlockSpec((BLOCK, ATT_QW), lambda s: (rev(s), 0))]
        + c_in,
        out_specs=[pl.BlockSpec((BLOCK, MIX_W), lambda s: (rev(s), 0)), const((2, HG_W)), const((1, BLOCK)),
                   const((ATT_HEADS, BLOCK))] + c_out,
        out_shape=[jax.ShapeDtypeStruct((p, MIX_W), BF16), jax.ShapeDtypeStruct((2, HG_W), F32),
                   jax.ShapeDtypeStruct((1, BLOCK), F32), jax.ShapeDtypeStruct((ATT_HEADS, BLOCK), F32)] + c_shapes,
        scratch_shapes=[pltpu.VMEM((HG_HEADS, BLOCK, BLOCK), F32), pltpu.VMEM((BLOCK, kvw), F32),
                        pltpu.VMEM((N_META, kvw), F32)] + c_sems,
        compiler_params=_cparams(("arbitrary",)),
    )(proj_hg, lbounds, norm_g, lv, states, scores, raw, probs, proj_att, proj_att, proj_att, cos, sin, cos, sin,
      cos, sin, sinks, dyh, doa, *parts)


_HBM = pl.BlockSpec(memory_space=pltpu.HBM)
_SEM = pl.BlockSpec(memory_space=pltpu.SEMAPHORE)
_ORDERED_BY_DATA = pltpu.CompilerParams(has_side_effects=pltpu.SideEffectType.DATAFLOW_SIDE_EFFECTING)


def _exchange_copies(part_ref, land_ref, send_sems, recv_sems):
    place = _place()
    return [pltpu.make_async_remote_copy(
        src_ref=part_ref.at[_slot(_peer(place, flip), False)], dst_ref=land_ref.at[r], send_sem=send_sems.at[r],
        recv_sem=recv_sems.at[r], device_id=_peer(place, flip), device_id_type=MESH) for r, flip in enumerate(_FLIPS)]


def _exchange_start(parts, name):
    def body(part_ref, land_ref, send_sems, recv_sems, part_thru, land_thru, token):
        for cp in _exchange_copies(part_ref, land_ref, send_sems, recv_sems):
            cp.start()
        token[...] = jnp.zeros_like(token)

    land = (N_PEERS,) + parts.shape[1:]
    return pl.pallas_call(
        body, name=name,
        out_shape=(pltpu.SemaphoreType.DMA((N_PEERS,)), pltpu.SemaphoreType.DMA((N_PEERS,)),
                   pltpu.HBM(parts.shape, parts.dtype), pltpu.HBM(land, parts.dtype), jax.ShapeDtypeStruct((8, BLOCK), F32)),
        in_specs=(_HBM, _HBM), out_specs=(_SEM, _SEM, _HBM, _HBM, pl.BlockSpec(memory_space=pltpu.VMEM)),
        input_output_aliases={0: 2, 1: 3}, compiler_params=_ORDERED_BY_DATA,
    )(pltpu.with_memory_space_constraint(parts, pltpu.HBM),
      pltpu.with_memory_space_constraint(lax.empty(land, parts.dtype), pltpu.HBM))


def _exchange_wait(send_sems, recv_sems, part_thru, land_thru, after, name):
    def body(part_ref, land_ref, send_sems, recv_sems, after_ref, part_out, land_out):
        for cp in _exchange_copies(part_ref, land_ref, send_sems, recv_sems):
            cp.wait_send()
            cp.wait_recv()

    return pl.pallas_call(
        body, name=name,
        out_shape=(pltpu.HBM(part_thru.shape, part_thru.dtype), pltpu.HBM(land_thru.shape, land_thru.dtype)),
        in_specs=(_HBM, _HBM, _SEM, _SEM, pl.BlockSpec(memory_space=pl.ANY)), out_specs=(_HBM, _HBM),
        input_output_aliases={0: 0, 1: 1}, compiler_params=_ORDERED_BY_DATA,
    )(part_thru, land_thru, send_sems, recv_sems, after)


def _embed_bwd(dmix, dgates, w_in_t, dr1, xhat0, rstd0, g0):
    p = dmix.shape[0]
    tm = _row_tile(p, 640)
    nm = p // tm

    def body(a_ref, g_ref, w_ref, dr_ref, xh_ref, rs_ref, g0_ref, gx_ref, lead_ref, dg_ref, db_ref, buf_ref, sem):
        i = pl.program_id(0)
        first = pltpu.make_async_copy(buf_ref.at[0, pl.ds(BLOCK, tm - BLOCK)], gx_ref.at[pl.ds(0, tm - BLOCK)],
                                      sem.at[0])
        later = lambda t: pltpu.make_async_copy(buf_ref.at[t % 2], gx_ref.at[pl.ds(t * tm - BLOCK, tm)], sem.at[t % 2])

        @pl.when(i == 2)
        def _():
            first.wait()

        @pl.when(i > 2)
        def _():
            later(i - 2).wait()

        dh0 = (ALPHA * dr_ref[...] + _dot(a_ref[...], w_ref[:MIX_W, :], "nn")
               + _dot(g_ref[...], w_ref[MIX_W:, :], "nn"))
        row = i * tm + lax.broadcasted_iota(jnp.int32, (tm, 1), 0)
        dx, dg, db = _ln_bwd(jnp.where(row >= PAD, dh0, 0.0), xh_ref[...], rs_ref[...], g0_ref[...])
        buf_ref[i % 2] = dx

        @pl.when(i == 0)
        def _():
            lead_ref[...] = dx[:BLOCK]
            dg_ref[...] = dg
            db_ref[...] = db
            first.start()

        @pl.when(i > 0)
        def _():
            dg_ref[...] += dg
            db_ref[...] += db
            later(i).start()

        @pl.when(i == nm - 1)
        def _():
            for t in (nm - 2, nm - 1):
                if t >= 0:
                    (first if t == 0 else later(t)).wait()

    row = lambda w: pl.BlockSpec((tm, w), lambda i: (i, 0))
    vec = pl.BlockSpec((1, D_MODEL), lambda i: (0, 0))
    return pl.pallas_call(
        body, name="embed_bwd", grid=(nm,),
        in_specs=[row(dmix.shape[1]), row(dgates.shape[1]),
                  pl.BlockSpec(w_in_t.shape, lambda i: (0, 0), pipeline_mode=pl.Buffered(1)), row(D_MODEL), row(D_MODEL),
                  row(1), vec],
        out_specs=[pl.BlockSpec(memory_space=pl.ANY), pl.BlockSpec((BLOCK, D_MODEL), lambda i: (0, 0)), vec, vec],
        out_shape=[jax.ShapeDtypeStruct((p - BLOCK, D_MODEL), F32), jax.ShapeDtypeStruct((BLOCK, D_MODEL), F32),
                   jax.ShapeDtypeStruct((1, D_MODEL), F32), jax.ShapeDtypeStruct((1, D_MODEL), F32)],
        scratch_shapes=[pltpu.VMEM((2, tm, D_MODEL), F32), pltpu.SemaphoreType.DMA((2,))],
        compiler_params=_cparams(("arbitrary",)),
    )(dmix, dgates, w_in_t, dr1, xhat0, rstd0, g0)


_LATE = ("w_branch_hg", "w_branch_attn", "w_out", "w_ffn_in", "w_ffn_out")
_TRANSPOSED = ("w_in", "w_ffn_in")
_COLUMN_SHARDED = ("meta_tokens", "w_branch_hg", "w_branch_attn")
_SWAPPED = ("w_ffn_in",)


def _whole(name, gathered):
    _, r, c = gathered.shape
    if name in _COLUMN_SHARDED:
        return jnp.transpose(gathered, (1, 0, 2)).reshape(r, N_DEV * c)
    return gathered.reshape(N_DEV * r, c)


def _slots(name, whole):
    r, c = whole.shape
    if name in _COLUMN_SHARDED:
        return jnp.transpose(whole.reshape(r, N_DEV, c // N_DEV), (1, 0, 2))
    return whole.reshape(N_DEV, r // N_DEV, c)


def _device_step(x, target, meta_shard, ln_emb_g, ln_emb_b, w_in_shard, lbounds, norm_g, sinks, late_shards,
                 ln1_g, ln1_b, ln2_g, ln2_b):
    p = x.shape[0] + BLOCK
    lv = _level_stack()
    cos, sin = _rope_tables(p)
    swapped = [n in _SWAPPED for n in _LATE]

    h0, h0b, xhat0, rstd0, _, g_win = _embed_ln(x, meta_shard, w_in_shard, ln_emb_g, ln_emb_b)
    w_in = _whole("w_in", g_win)
    proj_hg, proj_att = _in_proj(h0b, w_in)
    yh, oa, states, scores, raw, probs, *gathered = _mixers_fwd(
        proj_hg, proj_att, lbounds, norm_g, lv, cos, sin, sinks, late_shards, swapped)
    w_bh, w_ba, w_out, w_fi, w_fo = [_whole(n, g) for n, g in zip(_LATE, gathered)]
    gates, mixin, h1, h1b, xhat1, rstd1 = _mix_out_ln1(yh, oa, h0b, w_in, h0, w_bh, w_ba, w_out, ln1_g, ln1_b)
    au, sw = _ffn_in_swiglu(h1b, w_fi)
    dr2, dr2b, loss_part, dg2, db2 = _ffn_out_loss(sw, w_fo, h1, ln2_g, ln2_b, target)

    mtn = functools.partial(_tiled_matmul_tn, tm=_row_tile(p, 1664), out_dtype=BF16)
    d_wfo = mtn(sw, dr2b, tk=FF_T, tn=D_MODEL, name="grad_w_ffn_out")
    dau, dh1 = _ffn_bwd(dr2, w_fo, au, w_fi)
    d_wfi = _weight_grad_t([dau], h1b, tk=4 * BLOCK, name="grad_w_ffn_in")
    dr1, dgates, dyh, doa, dg1, db1, d_wbh, d_wba, d_wout = _ln1_mix_bwd(
        dh1, xhat1, rstd1, ln1_g, yh, oa, gates, mixin, w_bh, w_ba, w_out)
    late_parts = [_slots(n, g) for n, g in zip(_LATE, (d_wbh, d_wba, d_wout, d_wfi, d_wfo))]
    dmix, d_lb, d_ng, d_sink, *late_recv = _mixers_bwd(
        proj_hg, proj_att, lbounds, norm_g, lv, states, scores, raw, probs, cos, sin, sinks, dyh, doa, late_parts,
        swapped)
    d_win = _weight_grad_t([dmix, dgates], h0b, tk=2 * BLOCK, name="grad_w_in")
    *win_flight, token = _exchange_start(_slots("w_in", d_win), "w_in_grads_start")
    grad_x, dlead, dg0, db0 = _embed_bwd(dmix, dgates, w_in, dr1, xhat0, rstd0, ln_emb_g + token[0:1, 0:1])

    small = dict(ln_emb_g=dg0, ln_emb_b=db0, hg_lower_bounds=d_lb, hg_norm_g=d_ng, ln1_g=dg1, ln1_b=db1, ln2_g=dg2,
                 ln2_b=db2)
    big = dict(zip(_LATE, zip(late_parts, late_recv)))
    return _pack_small(small, d_sink, dlead, loss_part), grad_x, big, win_flight


def _all_gather(arrs, dtypes, name):
    n = len(arrs)

    def body(*refs):
        ins, outs, stages = refs[:n], refs[n:2 * n], refs[2 * n:3 * n]
        send_sems, recv_sems, local_sems = refs[3 * n:]
        x, y, c = _place()
        sibling = (x, y, 1 - c)
        chips = [(1 - x, y), (x, 1 - y), (1 - x, 1 - y)]
        slot = lambda px, py, pc: 4 * px + 2 * py + pc

        def copy(w, k, block, to, from_stage=False):
            return pltpu.make_async_remote_copy(
                src_ref=stages[w] if from_stage else outs[w].at[slot(*block)], dst_ref=outs[w].at[slot(*block)],
                send_sem=send_sems.at[w, k], recv_sem=recv_sems.at[w, k], device_id=to, device_id_type=MESH)

        mine, first, passed = [], [], []
        for w in range(n):
            stages[w][...] = ins[w][...].astype(dtypes[w])
            mine.append(pltpu.make_async_copy(stages[w], outs[w].at[slot(x, y, c)], local_sems.at[w]))
            mine[-1].start()
        for w in range(n):
            first.append(copy(w, 0, (x, y, c), sibling, from_stage=True))
            first += [copy(w, 1 + j, (x, y, c), (*chip, c), from_stage=True) for j, chip in enumerate(chips)]
        for cp in first:
            cp.start()
        for j, chip in enumerate(chips):
            for w in range(n):
                copy(w, 1 + j, (*chip, c), (x, y, c)).wait_recv()
                passed.append(copy(w, 4 + j, (*chip, c), sibling))
                passed[-1].start()
        for w in range(n):
            copy(w, 0, sibling, (x, y, c)).wait_recv()
            for j, chip in enumerate(chips):
                copy(w, 4 + j, (*chip, 1 - c), (x, y, c)).wait_recv()
        for cp in first + passed:
            cp.wait_send()
        for cp in mine:
            cp.wait()

    return pl.pallas_call(
        body, name=name,
        in_specs=[pl.BlockSpec(memory_space=pltpu.VMEM)] * n,
        out_specs=[pl.BlockSpec(memory_space=pl.ANY)] * n,
        out_shape=[jax.ShapeDtypeStruct((N_DEV,) + a.shape, dt) for a, dt in zip(arrs, dtypes)],
        scratch_shapes=[pltpu.VMEM(a.shape, dt) for a, dt in zip(arrs, dtypes)]
        + [pltpu.SemaphoreType.DMA((n, 7)), pltpu.SemaphoreType.DMA((n, 7)), pltpu.SemaphoreType.DMA((n,))],
        compiler_params=pltpu.CompilerParams(vmem_limit_bytes=VMEM_LIMIT_BYTES),
    )(*arrs)


def _cast_shards(arrs):
    def body(*refs):
        for src, dst in zip(refs[:len(arrs)], refs[len(arrs):]):
            dst[...] = src[...].astype(BF16)

    return pl.pallas_call(body, name="cast_shards", out_shape=[jax.ShapeDtypeStruct(a.shape, BF16) for a in arrs],
                          compiler_params=pltpu.CompilerParams(vmem_limit_bytes=VMEM_LIMIT_BYTES))(*arrs)


def _shard_rows(rows):
    return rows if rows <= 512 else max(t for t in range(16, 353, 16) if rows % t == 0)


def _adamw_math(w, g, m, v):
    m = ADAM_B1 * m + (1.0 - ADAM_B1) * g
    v = ADAM_B2 * v + (1.0 - ADAM_B2) * (g * g)
    m_hat = m / (1.0 - ADAM_B1 ** ADAM_STEP)
    v_hat = v / (1.0 - ADAM_B2 ** ADAM_STEP)
    delta = -ADAM_LR * (m_hat / (jnp.sqrt(v_hat) + ADAM_EPS) + ADAM_WD * w)
    return delta, m, v


def _reduce_adamw(parts, recv, own_slot, w, m, v, name):
    r, cdim = w.shape
    tr = _shard_rows(r)

    def body(idx_ref, p_ref, r_ref, w_ref, m_ref, v_ref, g_out, d_out, m_out, v_out):
        g = p_ref[0].astype(F32)
        for j in range(N_PEERS):
            g = g + r_ref[j].astype(F32)
        d, mn, vn = _adamw_math(w_ref[...], g, m_ref[...], v_ref[...])
        g_out[...] = g
        d_out[...] = d
        m_out[...] = mn
        v_out[...] = vn

    flat = pl.BlockSpec((tr, cdim), lambda i, idx_ref: (i, 0))
    return pl.pallas_call(
        body, name=name,
        grid_spec=pltpu.PrefetchScalarGridSpec(
            num_scalar_prefetch=1, grid=(r // tr,),
            in_specs=[pl.BlockSpec((1, tr, cdim), lambda i, idx_ref: (idx_ref[0], i, 0)),
                      pl.BlockSpec((N_PEERS, tr, cdim), lambda i, idx_ref: (0, i, 0)), flat, flat, flat],
            out_specs=[flat] * 4),
        out_shape=[jax.ShapeDtypeStruct((r, cdim), F32)] * 4,
        compiler_params=_cparams(("arbitrary",)),
    )(own_slot, parts, recv, w, m, v)


def _adamw_plain(w, g, m, v, name):
    def body(w_ref, g_ref, m_ref, v_ref, d_out, m_out, v_out):
        d_out[...], m_out[...], v_out[...] = _adamw_math(w_ref[...], g_ref[...], m_ref[...], v_ref[...])

    return pl.pallas_call(body, name=name, out_shape=[jax.ShapeDtypeStruct(w.shape, F32)] * 3)(w, g, m, v)


_SMALL = (("ln_emb_g", (1, D_MODEL)), ("ln_emb_b", (1, D_MODEL)), ("hg_lower_bounds", (2, HG_W)),
          ("hg_norm_g", (1, BLOCK)), ("attn_sinks", (1, ATT_HEADS)), ("ln1_g", (1, D_MODEL)), ("ln1_b", (1, D_MODEL)),
          ("ln2_g", (1, D_MODEL)), ("ln2_b", (1, D_MODEL)))
_SMALL_ROW, _LOSS_ROW = {}, 0
for _name, (_rows, _) in _SMALL:
    _SMALL_ROW[_name], _LOSS_ROW = _LOSS_ROW, _LOSS_ROW + _rows
_META_ROW = 16
SMALL_ROWS = _META_ROW + N_META
assert _LOSS_ROW < _META_ROW


def _pack_small(grads, d_sink, dlead, loss_part):
    names = [n for n, _ in _SMALL if n != "attn_sinks"]

    def body(*refs):
        ins = dict(zip(names, refs))
        sink_ref, lead_ref, loss_ref, o_ref = refs[len(names):]
        o_ref[...] = jnp.zeros_like(o_ref)
        for name, (rows, cols) in _SMALL:
            if name != "attn_sinks":
                o_ref[_SMALL_ROW[name]:_SMALL_ROW[name] + rows, :cols] = ins[name][...]
        head = lax.broadcasted_iota(jnp.int32, (ATT_HEADS, BLOCK), 0)
        lane = lax.broadcasted_iota(jnp.int32, (ATT_HEADS, BLOCK), 1)
        o_ref[_SMALL_ROW["attn_sinks"]:_SMALL_ROW["attn_sinks"] + 1, :BLOCK] = jnp.sum(
            jnp.where(head == lane, sink_ref[...], 0.0), axis=0, keepdims=True)
        o_ref[_LOSS_ROW:_LOSS_ROW + 1, :BLOCK] = loss_ref[...]
        o_ref[_META_ROW:, :] = lead_ref[PAD:BLOCK, :]

    return pl.pallas_call(body, name="pack_small", out_shape=jax.ShapeDtypeStruct((SMALL_ROWS, D_MODEL), F32))(
        *[grads[n] for n in names], d_sink, dlead, loss_part)


def _small_reduce_adamw(gathered, weights, mom1, mom2):
    n = len(_SMALL)

    def body(*refs):
        g_ref, w_refs, m_refs, v_refs = refs[0], refs[1:1 + n], refs[1 + n:1 + 2 * n], refs[1 + 2 * n:1 + 3 * n]
        outs = refs[1 + 3 * n:1 + 7 * n]
        meta_out, loss_out, sum_ref = refs[1 + 7 * n:]
        total = g_ref[0]
        for s in range(1, N_DEV):
            total = total + g_ref[s]
        sum_ref[...] = total
        for i, (name, (rows, cols)) in enumerate(_SMALL):
            g = sum_ref[_SMALL_ROW[name]:_SMALL_ROW[name] + rows, :cols]
            d, mn, vn = _adamw_math(w_refs[i][...], g, m_refs[i][...], v_refs[i][...])
            for out, val in zip(outs[4 * i:4 * i + 4], (g, d, mn, vn)):
                out[...] = val
        meta_out[...] = sum_ref[_META_ROW:, :]
        loss_out[...] = jnp.broadcast_to(jnp.sum(sum_ref[_LOSS_ROW:_LOSS_ROW + 1, :BLOCK]), (1, BLOCK))

    per_param = [jax.ShapeDtypeStruct(shape, F32) for _, shape in _SMALL for _ in range(4)]
    res = pl.pallas_call(
        body, name="small_reduce_adamw",
        out_shape=per_param + [jax.ShapeDtypeStruct((N_META, D_MODEL), F32), jax.ShapeDtypeStruct((1, BLOCK), F32)],
        scratch_shapes=[pltpu.VMEM((SMALL_ROWS, D_MODEL), F32)],
    )(gathered, *[d[name] for d in (weights, mom1, mom2) for name, _ in _SMALL])
    return {name: res[4 * i:4 * i + 4] for i, (name, _) in enumerate(_SMALL)}, res[-2], res[-1]


_WEIGHTS = ("meta_tokens", "ln_emb_g", "ln_emb_b", "w_in", "hg_lower_bounds", "hg_norm_g", "attn_sinks",
            "w_branch_hg", "w_branch_attn", "w_out", "ln1_g", "ln1_b", "w_ffn_in", "w_ffn_out", "ln2_g", "ln2_b")


def kernel(x, meta_tokens, ln_emb_g, ln_emb_b, w_in, hg_lower_bounds, hg_norm_g, attn_sinks, w_branch_hg, w_branch_attn, w_out, ln1_g, ln1_b, w_ffn_in, w_ffn_out, ln2_g, ln2_b, loss_target, m_meta_tokens, m_ln_emb_g, m_ln_emb_b, m_w_in, m_hg_lower_bounds, m_hg_norm_g, m_attn_sinks, m_w_branch_hg, m_w_branch_attn, m_w_out, m_ln1_g, m_ln1_b, m_w_ffn_in, m_w_ffn_out, m_ln2_g, m_ln2_b, v_meta_tokens, v_ln_emb_g, v_ln_emb_b, v_w_in, v_hg_lower_bounds, v_hg_norm_g, v_attn_sinks, v_w_branch_hg, v_w_branch_attn, v_w_out, v_ln1_g, v_ln1_b, v_w_ffn_in, v_w_ffn_out, v_ln2_g, v_ln2_b):
    given = dict(locals())
    weights = {n: given[n] for n in _WEIGHTS}
    mom1 = {n: given["m_" + n] for n in _WEIGHTS}
    mom2 = {n: given["v_" + n] for n in _WEIGHTS}
    shard2d = lambda n, a: a.reshape(a.shape[-2:]).T if n in _TRANSPOSED else a.reshape(a.shape[-2:])

    w_in_shard, *late_shards = _cast_shards([shard2d(n, weights[n]) for n in ("w_in",) + _LATE])
    packed, grad_x, big, win_flight = _device_step(
        x[0], loss_target[0], meta_tokens, ln_emb_g.reshape(1, -1), ln_emb_b.reshape(1, -1), w_in_shard,
        hg_lower_bounds, hg_norm_g, attn_sinks, late_shards, ln1_g, ln1_b, ln2_g, ln2_b)

    place = _place()
    out = {}

    def reduce_adamw(n, parts, recv):
        own = _slot(place, n in _SWAPPED).astype(jnp.int32).reshape(1)
        res = _reduce_adamw(parts, recv, own, shard2d(n, weights[n]), shard2d(n, mom1[n]), shard2d(n, mom2[n]),
                            "adamw_" + n)
        out[n] = [(r.T if n in _TRANSPOSED else r).reshape(weights[n].shape) for r in res]

    for n, (parts, recv) in big.items():
        reduce_adamw(n, parts, recv)

    all_small, = _all_gather([packed], [F32], "gather_small")
    as_2d = lambda d: {n: d[n].reshape(shape) for n, shape in _SMALL}
    small_out, meta_whole, loss_row = _small_reduce_adamw(all_small, as_2d(weights), as_2d(mom1), as_2d(mom2))
    for n, res in small_out.items():
        out[n] = [r.reshape(weights[n].shape) for r in res]
    loss = loss_row[0, 0]
    g_meta_mine = lax.dynamic_index_in_dim(meta_whole.reshape(N_META, N_DEV, D_MODEL // N_DEV), _slot(place, False),
                                           axis=1, keepdims=False)
    out["meta_tokens"] = [g_meta_mine, *_adamw_plain(meta_tokens, g_meta_mine, m_meta_tokens, v_meta_tokens,
                                                     "adamw_meta")]

    reduce_adamw("w_in", *_exchange_wait(*win_flight, after=all_small, name="w_in_grads_wait"))

    return (loss, grad_x[None], *[out[n][0] for n in _WEIGHTS], *[out[n][1] for n in _WEIGHTS],
            *[out[n][2] for n in _WEIGHTS], *[out[n][3] for n in _WEIGHTS])
```

```python
import functools

import numpy as np
import jax
import jax.numpy as jnp
from jax import lax
from jax.experimental import pallas as pl
from jax.experimental.pallas import tpu as pltpu

F32 = jnp.float32
BF16 = jnp.bfloat16

D_MODEL = 1024
N_META = 16
BLOCK = 128
PAD = BLOCK - N_META
HG_HEADS = 4
HG_W = 512
ATT_HEADS = 8
HEAD_DIM = 64
ATT_QW = 512
ATT_KVW = 128
D_FF = 2816
EPS = 1e-5
ALPHA = 2.0 ** 0.25
ROPE_THETA = 10000.0
N_DEV = 8

ADAM_LR = 0.001
ADAM_B1 = 0.9
ADAM_B2 = 0.999
ADAM_EPS = 1e-08
ADAM_WD = 0.01
ADAM_STEP = 10

VMEM_LIMIT_BYTES = 56 * 1024 * 1024
MESH = pl.DeviceIdType.MESH

_LEVELS = (64, 32, 16, 8, 4, 2, 1)


def _cparams(sem):
    return pltpu.CompilerParams(dimension_semantics=sem, vmem_limit_bytes=VMEM_LIMIT_BYTES)


def _row_tile(rows, target):
    nb = rows // BLOCK
    best = 1
    for d in range(1, nb + 1):
        if nb % d == 0 and d * BLOCK <= target:
            best = d
    return best * BLOCK


_DN = {"nn": (((1,), (0,)), ((), ())), "nt": (((1,), (1,)), ((), ())), "tn": (((0,), (0,)), ((), ()))}


def _dot(a, b, form):
    return lax.dot_general(a.astype(BF16), b.astype(BF16), _DN[form], preferred_element_type=F32)


@functools.partial(jax.custom_vjp, nondiff_argnums=(2,))
def _mm(a, b, form):
    return _dot(a, b, form)


def _mm_fwd(a, b, form):
    a, b = a.astype(BF16), b.astype(BF16)
    return _dot(a, b, form), (a, b)


def _mm_bwd(form, res, g):
    a, b = res
    if form == "nn":
        return _dot(g, b, "nt"), _dot(a, g, "tn")
    if form == "nt":
        return _dot(g, b, "nn"), _dot(g, a, "tn")
    return _dot(b, g, "nt"), _dot(a, g, "nn")


_mm.defvjp(_mm_fwd, _mm_bwd)


def _split_dot(lv, x, form):
    return lax.dot_general(lv, x.astype(BF16), _DN[form], preferred_element_type=F32)


@jax.custom_vjp
def _swap_halves(x):
    return pltpu.roll(x, 64, 1)


_swap_halves.defvjp(lambda x: (pltpu.roll(x, 64, 1), None), lambda _, g: (pltpu.roll(g, 64, 1),))


def _tiled_matmul_tn(a, b, *, tm, tk, tn, out_dtype, name):
    m, k = a.shape
    n = b.shape[1]
    assert m % tm == 0 and k % tk == 0 and n % tn == 0, (name, a.shape, b.shape, tm, tk, tn)
    nm = m // tm

    def body(a_ref, b_ref, o_ref, acc_ref):
        mi = pl.program_id(2)

        @pl.when(mi == 0)
        def _():
            acc_ref[...] = jnp.zeros_like(acc_ref)

        acc_ref[...] += _dot(a_ref[...], b_ref[...], "tn")

        @pl.when(mi == nm - 1)
        def _():
            o_ref[...] = acc_ref[...].astype(out_dtype)

    return pl.pallas_call(
        body, name=name, grid=(k // tk, n // tn, nm),
        in_specs=[pl.BlockSpec((tm, tk), lambda kk, j, i: (i, kk)), pl.BlockSpec((tm, tn), lambda kk, j, i: (i, j))],
        out_specs=pl.BlockSpec((tk, tn), lambda kk, j, i: (kk, j)),
        out_shape=jax.ShapeDtypeStruct((k, n), out_dtype),
        scratch_shapes=[pltpu.VMEM((tk, tn), F32)],
        compiler_params=_cparams(("arbitrary", "arbitrary", "arbitrary")),
    )(a, b)


def _weight_grad_t(cots, h, *, tk, name):
    p, d = h.shape
    steps = [c.shape[1] // tk for c in cots]
    assert all(c.shape == (p, n * tk) for c, n in zip(cots, steps)), (name, [c.shape for c in cots], tk)
    first = [sum(steps[:i]) for i in range(len(cots))]

    def body(*refs):
        h_ref, o_ref = refs[len(cots)], refs[len(cots) + 1]
        k = pl.program_id(0)
        for c_ref, lo, n in zip(refs, first, steps):
            @pl.when((k >= lo) & (k < lo + n))
            def _(c_ref=c_ref):
                o_ref[...] = _dot(c_ref[...], h_ref[...], "tn").astype(BF16)

    cot_spec = lambda lo, n: pl.BlockSpec((p, tk), lambda k: (0, jnp.clip(k - lo, 0, n - 1)))
    return pl.pallas_call(
        body, name=name, grid=(sum(steps),),
        in_specs=[cot_spec(lo, n) for lo, n in zip(first, steps)]
                 + [pl.BlockSpec((p, d), lambda k: (0, 0), pipeline_mode=pl.Buffered(1))],
        out_specs=pl.BlockSpec((tk, d), lambda k: (k, 0)),
        out_shape=jax.ShapeDtypeStruct((sum(steps) * tk, d), BF16),
        compiler_params=_cparams(("arbitrary",)),
    )(*cots, h)


def _ln_stats(r):
    mu = jnp.mean(r, axis=-1, keepdims=True)
    xc = r - mu
    var = jnp.mean(xc * xc, axis=-1, keepdims=True)
    rstd = lax.rsqrt(var + EPS)
    return xc * rstd, rstd


def _ln_bwd(dy, xhat, rstd, g):
    dxhat = dy * g
    m1 = jnp.mean(dxhat, axis=-1, keepdims=True)
    m2 = jnp.mean(dxhat * xhat, axis=-1, keepdims=True)
    dr = rstd * (dxhat - m1 - xhat * m2)
    return dr, jnp.sum(dy * xhat, axis=0, keepdims=True), jnp.sum(dy, axis=0, keepdims=True)


N_SEG = 3 + len(_LEVELS)


def _level_stack():
    t = np.arange(BLOCK)[:, None]
    r = np.arange(BLOCK)[None, :]
    mats = [r <= t, r > t, np.ones((BLOCK, BLOCK), bool)]
    for h in _LEVELS:
        same = (t // (2 * h)) == (r // (2 * h))
        up_t, up_r = (t % (2 * h)) >= h, (r % (2 * h)) >= h
        mats.append(same & ((up_t & up_r & (r <= t)) | (~up_t & ~up_r & (r > t))))
    return jnp.asarray(np.concatenate(mats, axis=0).astype(np.float32), dtype=BF16)


def _hgrn_gates(hf, a0, a1, valid):
    lb = jax.nn.sigmoid(a0 - a1)
    fg = lb + (1.0 - lb) * jax.nn.sigmoid(hf)
    return jnp.where(valid, jnp.log(fg), 0.0), jnp.where(valid, 1.0 - fg, 0.0)


def _hgrn_scores(hq, k, *levels):
    q = jax.nn.silu(hq)
    rows = lax.broadcasted_iota(jnp.int32, (BLOCK, BLOCK), 0)
    cols = lax.broadcasted_iota(jnp.int32, (BLOCK, BLOCK), 1)
    a = jnp.where(rows == cols, jnp.sum(q * k, axis=-1, keepdims=True), 0.0)
    differ = jnp.bitwise_xor(rows, cols)
    for h, lvl in zip(_LEVELS, levels):
        decay = jnp.exp(lvl)
        pair = (cols < rows) & (differ >= h) & (differ < 2 * h)
        a = a + jnp.where(pair, _mm(q * decay, k * decay, "nt"), 0.0)
    return a


def _hgrn_mix(hq, k, v, st_in, a, seg_incl, seg_after, seg_total):
    o = _mm(jax.nn.silu(hq) * jnp.exp(seg_incl), st_in, "nt") + _mm(a, v, "nn")
    return o, st_in * jnp.exp(seg_total) + _mm(v, k * jnp.exp(seg_after), "tn")


def _hgrn_norm(o, hg, ng):
    return o * lax.rsqrt(jnp.mean(o * o, axis=-1, keepdims=True) + EPS) * ng * jax.nn.silu(hg)


def _seg_blocks(e, h):
    return [e[i * BLOCK:(i + 1) * BLOCK, h * BLOCK:(h + 1) * BLOCK] for i in range(N_SEG)]


def _rope(x, cos, sin, first_half):
    partner = jnp.where(first_half, -pltpu.roll(x, 96, 1), pltpu.roll(x, 32, 1))
    return x * cos + partner * sin


def _rope_t(g, cos, sin, first_half):
    u = g * sin
    partner = jnp.where(first_half, pltpu.roll(u, 96, 1), -pltpu.roll(u, 32, 1))
    return g * cos + partner


def _low_half(x):
    return lax.broadcasted_iota(jnp.int32, x.shape, 1) < HEAD_DIM


def _both_halves(x, g):
    sw = _swap_halves(x)
    return jnp.where(_low_half(x), x, sw) if g == 0 else jnp.where(_low_half(x), sw, x)


def _att_scores(qa, qb, kc, kp, km, g, own4, band4, meta4):
    low = _low_half(qa)
    q4 = jnp.concatenate([jnp.where(low, qa, 0.0), jnp.where(low, 0.0, qa),
                          jnp.where(low, qb, 0.0), jnp.where(low, 0.0, qb)], axis=0)
    scale = HEAD_DIM ** -0.5
    neg = jnp.finfo(F32).min
    s = jnp.where(own4, _mm(_both_halves(kc, g), q4, "nt"), _mm(_both_halves(kp, g), q4, "nt"))
    return (jnp.where(band4, s * scale, neg), jnp.where(meta4, _mm(_both_halves(km, g), q4, "nt") * scale, neg))


def _att_probs(s, sm, sinkrow):
    mx = jnp.maximum(jnp.maximum(jnp.max(s, axis=0, keepdims=True), jnp.max(sm, axis=0, keepdims=True)), sinkrow)
    p, pm, ps = jnp.exp(s - mx), jnp.exp(sm - mx), jnp.exp(sinkrow - mx)
    inv = 1.0 / (jnp.sum(p, axis=0, keepdims=True) + jnp.sum(pm, axis=0, keepdims=True) + ps)
    return p * inv, pm * inv, ps * inv


def _att_probs_bwd(p, pm, ps, dp, dpm):
    r = jnp.sum(p * dp, axis=0, keepdims=True) + jnp.sum(pm * dpm, axis=0, keepdims=True)
    return p * (dp - r), pm * (dpm - r), -ps * r


def _att_values(p, pm, vc, vp, vm, g, own4):
    o4 = (_mm(jnp.where(own4, p, 0.0), _both_halves(vc, g), "tn") + _mm(jnp.where(own4, 0.0, p), _both_halves(vp, g), "tn")
          + _mm(pm, _both_halves(vm, g), "tn"))
    tiles = []
    for j in range(2):
        upper = o4[(2 * j) * BLOCK:(2 * j + 1) * BLOCK]
        tiles.append(jnp.where(_low_half(upper), upper, o4[(2 * j + 1) * BLOCK:(2 * j + 2) * BLOCK]))
    return tiles


def _att_masks(blk_idx):
    kidx = lax.broadcasted_iota(jnp.int32, (BLOCK, BLOCK), 0)
    qrow = lax.broadcasted_iota(jnp.int32, (BLOCK, BLOCK), 1)
    own_side = kidx <= qrow
    pos_own = blk_idx * BLOCK + kidx - PAD
    ok_band = (own_side & (pos_own >= N_META)) | (~own_side & (pos_own - BLOCK >= N_META) & (blk_idx >= 1))
    qpos = blk_idx * BLOCK + lax.broadcasted_iota(jnp.int32, (N_META, BLOCK), 1) - PAD
    ok_meta = lax.broadcasted_iota(jnp.int32, (N_META, BLOCK), 0) <= qpos
    return [jnp.concatenate([m] * 4, axis=1) for m in (own_side, ok_band, ok_meta)]


def _token_streams(tr, tile_of=lambda i: i):
    k = tr // BLOCK
    return [pl.BlockSpec((BLOCK, D_MODEL), lambda i, j=j: (jnp.maximum(k * tile_of(i) - 1 + j, 0), 0))
            for j in range(k)]


def _embed_ln(x, meta_shard, w_in_shard, g0, b0):
    p = x.shape[0] + BLOCK
    tr = _row_tile(p, 640)
    k = tr // BLOCK
    nt = p // tr
    tile_of = lambda s: (s + 1) % nt
    shards = [meta_shard, w_in_shard]
    c_in, c_out, c_shapes, c_sems = _comm_specs(shards, N_DEV)

    def body(*refs):
        g_ref, b_ref = refs[k:k + 2]
        h_ref, hb_ref, xh_ref, rs_ref = refs[k + 4:k + 8]
        out_refs = refs[k + 8:k + 10]
        lead_ref, meta_ref = refs[k + 10:k + 12]
        starts, passes, waits = _gather_behind(refs[k + 2:k + 4], out_refs, refs[k + 12:], [False, False])
        s = pl.program_id(0)
        t = tile_of(s)

        @pl.when(s == 0)
        def _():
            lead_ref[...] = jnp.zeros_like(lead_ref)
            for start in starts:
                start()

        @pl.when(s == nt - 1)
        def _():
            for step in passes + waits:
                step()
            pltpu.sync_copy(out_refs[0], meta_ref)
            for d in range(N_DEV):
                lead_ref[PAD:BLOCK, d * BLOCK:(d + 1) * BLOCK] = meta_ref[d]

        first = jnp.where(t == 0, lead_ref[...], refs[0][...])
        xhat, rstd = _ln_stats(jnp.concatenate([first] + [r[...] for r in refs[1:k]], axis=0))
        row = t * tr + lax.broadcasted_iota(jnp.int32, (tr, 1), 0)
        h = jnp.where(row >= PAD, xhat * g_ref[...] + b_ref[...], 0.0)
        h_ref[...] = h
        hb_ref[...] = h.astype(BF16)
        xh_ref[...] = xhat
        rs_ref[...] = rstd

    vec = pl.BlockSpec((1, D_MODEL), lambda s: (0, 0))
    rowsp = pl.BlockSpec((tr, D_MODEL), lambda s: (tile_of(s), 0))
    return pl.pallas_call(
        body, name="embed_ln", grid=(nt,),
        in_specs=_token_streams(tr, tile_of) + [vec, vec] + c_in,
        out_specs=[rowsp, rowsp, rowsp, pl.BlockSpec((tr, 1), lambda s: (tile_of(s), 0))] + c_out,
        out_shape=[jax.ShapeDtypeStruct((p, D_MODEL), F32), jax.ShapeDtypeStruct((p, D_MODEL), BF16),
                   jax.ShapeDtypeStruct((p, D_MODEL), F32), jax.ShapeDtypeStruct((p, 1), F32)] + c_shapes,
        scratch_shapes=[pltpu.VMEM((BLOCK, D_MODEL), F32), pltpu.VMEM((N_DEV, N_META, BLOCK), F32)] + c_sems,
        compiler_params=_cparams(("arbitrary",)),
    )(*([x] * k), g0, b0, *shards)


def _rope_tables(p):
    pos = (np.arange(p, dtype=np.int32) - PAD).astype(np.float32)
    half = HEAD_DIM // 2
    inv = np.float32(ROPE_THETA) ** (-np.arange(half, dtype=np.float32) / np.float32(half))
    ang = pos[:, None] * np.tile(inv.astype(np.float32), BLOCK // half)[None, :]
    return jnp.asarray(np.cos(ang), F32), jnp.asarray(np.sin(ang), F32)


def _att_sinkrows(sink_ref):
    lanehead = lax.broadcasted_iota(jnp.int32, (1, 4 * BLOCK), 1) // BLOCK
    rows = []
    for g in range(2):
        row = jnp.zeros((1, 4 * BLOCK), F32)
        for j in range(4):
            row = jnp.where(lanehead == j, sink_ref[0, 4 * g + j], row)
        rows.append(row)
    return rows


def _first_half(rows):
    return (lax.broadcasted_iota(jnp.int32, (rows, BLOCK), 1) % HEAD_DIM) < (HEAD_DIM // 2)


def _att_load(qkv_ref, cos_ref, sin_ref, with_q):
    cos, sin, fh = cos_ref[...], sin_ref[...], _first_half(BLOCK)
    qs = [_rope(qkv_ref[:, j * BLOCK:(j + 1) * BLOCK], cos, sin, fh) for j in range(4)] if with_q else None
    k = _rope(qkv_ref[:, ATT_QW:ATT_QW + ATT_KVW], cos, sin, fh)
    v = qkv_ref[:, ATT_QW + ATT_KVW:ATT_QW + 2 * ATT_KVW]
    return qs, k, v


def _att_load_meta(qkv_ref, cos_ref, sin_ref):
    k = _rope(qkv_ref[PAD:BLOCK, ATT_QW:ATT_QW + ATT_KVW], cos_ref[PAD:BLOCK, :], sin_ref[PAD:BLOCK, :],
              _first_half(N_META))
    return k, qkv_ref[PAD:BLOCK, ATT_QW + ATT_KVW:ATT_QW + 2 * ATT_KVW]


def _att_specs(blk):
    w = ATT_QW + 2 * ATT_KVW
    cur = lambda width: pl.BlockSpec((BLOCK, width), lambda i: (blk(i), 0))
    prev = lambda width: pl.BlockSpec((BLOCK, width), lambda i: (jnp.maximum(blk(i) - 1, 0), 0))
    meta = lambda width: pl.BlockSpec((BLOCK, width), lambda i: (0, 0))
    return [cur(w), prev(w), meta(w), cur(BLOCK), cur(BLOCK), prev(BLOCK), prev(BLOCK), meta(BLOCK), meta(BLOCK),
            pl.BlockSpec(memory_space=pltpu.SMEM)]


_FLIPS = [(dx, dy, dc) for dx in (0, 1) for dy in (0, 1) for dc in (0, 1)][1:]
N_PEERS = len(_FLIPS)


def _place():
    return lax.axis_index("x"), lax.axis_index("y"), lax.axis_index("c")


def _peer(place, flip):
    return tuple(1 - p if f else p for p, f in zip(place, flip))


def _slot(place, swapped):
    x, y, c = place
    return 4 * y + 2 * x + c if swapped else 4 * x + 2 * y + c


def _comm_specs(arrs, out_lead):
    n = len(arrs)
    outs = [jax.ShapeDtypeStruct((out_lead,) + a.shape[-2:], a.dtype) for a in arrs]
    sems = [pltpu.SemaphoreType.DMA((n, N_PEERS)), pltpu.SemaphoreType.DMA((n, N_PEERS)), pltpu.SemaphoreType.DMA((n,))]
    return [pl.BlockSpec(memory_space=pl.ANY)] * n, [pl.BlockSpec(memory_space=pl.ANY)] * n, outs, sems


def _gather_behind(shard_refs, out_refs, sems, swapped):
    send_sems, recv_sems, local_sems = sems
    x, y, c = _place()
    me, sibling = (x, y, c), (x, y, 1 - c)
    chips = [(1 - x, y), (x, 1 - y), (1 - x, 1 - y)]
    starts, passes, waits = [], [], []
    for w, (s, o) in enumerate(zip(shard_refs, out_refs)):
        def copy(k, block, to, from_shard=False, w=w, s=s, o=o):
            rows = o.at[_slot(block, swapped[w])]
            return pltpu.make_async_remote_copy(
                src_ref=s if from_shard else rows, dst_ref=rows, send_sem=send_sems.at[w, k],
                recv_sem=recv_sems.at[w, k], device_id=to, device_id_type=MESH)

        own = pltpu.make_async_copy(s, o.at[_slot(me, swapped[w])], local_sems.at[w])
        first = [copy(0, me, sibling, True)] + [copy(1 + j, me, (*chip, c), True) for j, chip in enumerate(chips)]
        handed = [copy(4 + j, (*chip, c), sibling) for j, chip in enumerate(chips)]
        starts += [own.start] + [cp.start for cp in first]
        for j, chip in enumerate(chips):
            passes += [copy(1 + j, (*chip, c), me).wait_recv, handed[j].start]
        waits.append(copy(0, sibling, me).wait_recv)
        waits += [copy(4 + j, (*chip, 1 - c), me).wait_recv for j, chip in enumerate(chips)]
        waits += [cp.wait_send for cp in first + handed] + [own.wait]
    return starts, passes, waits


def _scatter_behind(part_refs, recv_refs, sems, swapped):
    send_sems, recv_sems, _ = sems
    place = _place()
    starts, waits = [], []
    for w, (p, o) in enumerate(zip(part_refs, recv_refs)):
        for r, flip in enumerate(_FLIPS):
            peer = _peer(place, flip)
            cp = pltpu.make_async_remote_copy(
                src_ref=p.at[_slot(peer, swapped[w])], dst_ref=o.at[r], send_sem=send_sems.at[w, r],
                recv_sem=recv_sems.at[w, r], device_id=peer, device_id_type=MESH)
            starts.append(cp.start)
            waits += [cp.wait_recv, cp.wait_send]
    return starts, waits


def _mixers_fwd(proj_hg, proj_att, lbounds, norm_g, lv, cos, sin, sinks, shards, swapped):
    p = proj_hg.shape[0]
    nb = p // BLOCK
    n = len(shards)
    c_in, c_out, c_shapes, c_sems = _comm_specs(shards, N_DEV)
    pass_step = min(nb - 1, max(1, (5 * nb) // 8))

    def body(*refs):
        x_ref, lb_ref, ng_ref, lv_ref, cur_ref, prev_ref, meta_ref, cc, sc, cp, sp, cm, sm, sink_ref = refs[:14]
        shard_refs = refs[14:14 + n]
        y_ref, o_ref, st_ref, a_ref, raw_ref, pr_ref = refs[14 + n:20 + n]
        out_refs = refs[20 + n:20 + 2 * n]
        carry_ref = refs[20 + 2 * n]
        starts, passes, waits = _gather_behind(shard_refs, out_refs, refs[21 + 2 * n:], swapped)
        c = pl.program_id(0)

        @pl.when(c == 0)
        def _():
            carry_ref[...] = jnp.zeros_like(carry_ref)
            for start in starts:
                start()

        @pl.when(c == pass_step)
        def _():
            for step in passes:
                step()

        valid = (c * BLOCK + lax.broadcasted_iota(jnp.int32, (BLOCK, 1), 0)) >= PAD
        logf, k = _hgrn_gates(x_ref[:, HG_W:2 * HG_W], lb_ref[0:1, :], lb_ref[1:2, :], valid)
        e = _split_dot(lv_ref[...], logf, "nn")
        for h in range(HG_HEADS):
            sl = lambda part: x_ref[:, part * HG_W + h * BLOCK: part * HG_W + (h + 1) * BLOCK]
            hs = slice(h * BLOCK, (h + 1) * BLOCK)
            st_in = carry_ref[h]
            st_ref[0, h] = st_in
            seg = _seg_blocks(e, h)
            a = _hgrn_scores(sl(0), k[:, hs], *seg[3:])
            a_ref[0, h] = a.astype(BF16)
            raw, st_out = _hgrn_mix(sl(0), k[:, hs], sl(2), st_in, a, *seg[:3])
            raw_ref[:, hs] = raw
            y_ref[:, hs] = _hgrn_norm(raw, sl(3), ng_ref[...]).astype(BF16)
            carry_ref[h] = st_out

        qs, kc, vc = _att_load(cur_ref, cc, sc, True)
        _, kp, vp = _att_load(prev_ref, cp, sp, False)
        km, vm = _att_load_meta(meta_ref, cm, sm)
        sinkrows = _att_sinkrows(sink_ref)
        own4, band4, meta4 = _att_masks(c)
        for g in range(2):
            s, s_meta = _att_scores(qs[2 * g], qs[2 * g + 1], kc, kp, km, g, own4, band4, meta4)
            pr, pr_meta, pr_sink = _att_probs(s, s_meta, sinkrows[g])
            pr_ref[0, g, :BLOCK, :] = pr.astype(BF16)
            pr_ref[0, g, BLOCK:BLOCK + N_META, :] = pr_meta.astype(BF16)
            pr_ref[0, g, BLOCK + N_META:, :] = jnp.broadcast_to(pr_sink, (N_META, 4 * BLOCK)).astype(BF16)
            for j, tile in enumerate(_att_values(pr, pr_meta, vc, vp, vm, g, own4)):
                o_ref[:, (2 * g + j) * BLOCK:(2 * g + j + 1) * BLOCK] = tile.astype(BF16)

        @pl.when(c == nb - 1)
        def _():
            for wait in waits:
                wait()

    return pl.pallas_call(
        body, name="mixers_fwd", grid=(nb,),
        in_specs=[pl.BlockSpec((BLOCK, 4 * HG_W), lambda c: (c, 0)), pl.BlockSpec((2, HG_W), lambda c: (0, 0)),
                  pl.BlockSpec((1, BLOCK), lambda c: (0, 0)), pl.BlockSpec(lv.shape, lambda c: (0, 0))]
        + _att_specs(lambda c: c) + c_in,
        out_specs=[pl.BlockSpec((BLOCK, HG_W), lambda c: (c, 0)), pl.BlockSpec((BLOCK, ATT_QW), lambda c: (c, 0)),
                   pl.BlockSpec((1, HG_HEADS, BLOCK, BLOCK), lambda c: (c, 0, 0, 0)),
                   pl.BlockSpec((1, HG_HEADS, BLOCK, BLOCK), lambda c: (c, 0, 0, 0)),
                   pl.BlockSpec((BLOCK, HG_W), lambda c: (c, 0)),
                   pl.BlockSpec((1, 2, ATT_KEYS, 4 * BLOCK), lambda c: (c, 0, 0, 0))] + c_out,
        out_shape=[jax.ShapeDtypeStruct((p, HG_W), BF16), jax.ShapeDtypeStruct((p, ATT_QW), BF16),
                   jax.ShapeDtypeStruct((nb, HG_HEADS, BLOCK, BLOCK), F32),
                   jax.ShapeDtypeStruct((nb, HG_HEADS, BLOCK, BLOCK), BF16),
                   jax.ShapeDtypeStruct((p, HG_W), F32),
                   jax.ShapeDtypeStruct((nb, 2, ATT_KEYS, 4 * BLOCK), BF16)] + c_shapes,
        scratch_shapes=[pltpu.VMEM((HG_HEADS, BLOCK, BLOCK), F32)] + c_sems,
        compiler_params=_cparams(("arbitrary",)),
    )(proj_hg, lbounds, norm_g, lv, proj_att, proj_att, proj_att, cos, sin, cos, sin, cos, sin, sinks, *shards)


def _tile(rows, preferred):
    return preferred if rows % preferred == 0 else _row_tile(rows, preferred)


def _in_proj(h0b, w_in_t):
    p = h0b.shape[0]
    tm = _tile(p, 1040)
    hg_end = 4 * HG_W

    def body(h_ref, w_ref, hg_ref, att_ref):
        h = h_ref[...]
        hg_ref[...] = _dot(h, w_ref[:hg_end, :], "nt")
        att_ref[...] = _dot(h, w_ref[hg_end:, :], "nt")

    row = lambda w: pl.BlockSpec((tm, w), lambda i: (i, 0))
    return pl.pallas_call(
        body, name="in_proj", grid=(p // tm,),
        in_specs=[row(D_MODEL), pl.BlockSpec((MIX_W, D_MODEL), lambda i: (0, 0), pipeline_mode=pl.Buffered(1))],
        out_specs=[row(hg_end), row(MIX_W - hg_end)],
        out_shape=[jax.ShapeDtypeStruct((p, hg_end), F32), jax.ShapeDtypeStruct((p, MIX_W - hg_end), F32)],
        compiler_params=_cparams(("arbitrary",)),
    )(h0b, w_in_t)


def _branch_mix(yh, oa, gates, w_bh, w_ba):
    y_hg = _dot(yh, w_bh, "nn")
    y_att = _dot(oa, w_ba, "nn")
    s1 = jax.nn.sigmoid(gates[:, :D_MODEL].astype(F32))
    s2 = jax.nn.sigmoid(gates[:, D_MODEL:].astype(F32))
    return s1 * y_hg + s2 * y_att, y_hg, y_att, s1, s2


def _mix_out_ln1(yh, oa, h0b, w_in_t, h0, w_bh, w_ba, w_out, g1, b1):
    p = yh.shape[0]
    tr = _tile(p, 416)

    def body(yh_ref, oa_ref, h0b_ref, wi_ref, h0_ref, wbh_ref, wba_ref, wo_ref, g1_ref, b1_ref,
             g_ref, mix_ref, h1_ref, h1b_ref, xh_ref, rs_ref):
        g_ref[...] = _dot(h0b_ref[...], wi_ref[MIX_W:, :], "nt").astype(BF16)
        mixin = _branch_mix(yh_ref[...], oa_ref[...], g_ref[...], wbh_ref[...], wba_ref[...])[0]
        mix_ref[...] = mixin.astype(BF16)
        xhat, rstd = _ln_stats(ALPHA * h0_ref[...] + _dot(mixin, wo_ref[...], "nn"))
        h1 = xhat * g1_ref[...] + b1_ref[...]
        h1_ref[...] = h1
        h1b_ref[...] = h1.astype(BF16)
        xh_ref[...] = xhat
        rs_ref[...] = rstd

    row = lambda w: pl.BlockSpec((tr, w), lambda i: (i, 0))
    const = lambda a: pl.BlockSpec(a.shape, lambda i: (0, 0))
    return pl.pallas_call(
        body, name="mix_out_ln1", grid=(p // tr,),
        in_specs=[row(HG_W), row(ATT_QW), row(D_MODEL),
                  pl.BlockSpec(w_in_t.shape, lambda i: (0, 0), pipeline_mode=pl.Buffered(1)), row(D_MODEL),
                  const(w_bh), const(w_ba), const(w_out), const(g1), const(b1)],
        out_specs=[row(2 * D_MODEL), row(D_MODEL), row(D_MODEL), row(D_MODEL), row(D_MODEL), row(1)],
        out_shape=[jax.ShapeDtypeStruct((p, 2 * D_MODEL), BF16), jax.ShapeDtypeStruct((p, D_MODEL), BF16),
                   jax.ShapeDtypeStruct((p, D_MODEL), F32), jax.ShapeDtypeStruct((p, D_MODEL), BF16),
                   jax.ShapeDtypeStruct((p, D_MODEL), F32), jax.ShapeDtypeStruct((p, 1), F32)],
        compiler_params=_cparams(("arbitrary",)),
    )(yh, oa, h0b, w_in_t, h0, w_bh, w_ba, w_out, g1, b1)


FF_T = D_FF // 2


def _ffn_in_swiglu(h1, w_fi_t):
    p = h1.shape[0]
    tm = _tile(p, 1040)

    def body(h_ref, w_ref, au_ref, s_ref):
        au = _dot(h_ref[...], w_ref[...], "nt")
        au_ref[...] = au.astype(BF16)
        s_ref[...] = (jax.nn.silu(au[:, :FF_T]) * au[:, FF_T:]).astype(BF16)

    return pl.pallas_call(
        body, name="ffn_in_swiglu", grid=(D_FF // FF_T, p // tm),
        in_specs=[pl.BlockSpec((tm, D_MODEL), lambda j, i: (i, 0)), pl.BlockSpec((2 * FF_T, D_MODEL), lambda j, i: (j, 0))],
        out_specs=[pl.BlockSpec((tm, 2 * FF_T), lambda j, i: (i, j)), pl.BlockSpec((tm, FF_T), lambda j, i: (i, j))],
        out_shape=[jax.ShapeDtypeStruct((p, 2 * D_FF), BF16), jax.ShapeDtypeStruct((p, D_FF), BF16)],
        compiler_params=_cparams(("arbitrary", "arbitrary")),
    )(h1, w_fi_t)


def _ffn_out_loss(s, w_fo, h1, g2, b2, target):
    p = h1.shape[0]
    tr = _row_tile(p, 640)
    k = tr // BLOCK

    def body(*refs):
        s_ref, w_ref, h_ref, g_ref, b_ref = refs[:5]
        dr_ref, drb_ref, loss_ref, dg_ref, db_ref = refs[5 + k:]
        i = pl.program_id(0)
        xhat, rstd = _ln_stats(ALPHA * h_ref[...] + _dot(s_ref[...], w_ref[...], "nn"))
        y = xhat * g_ref[...] + b_ref[...]
        row = i * tr + lax.broadcasted_iota(jnp.int32, (tr, 1), 0)
        tgt = jnp.concatenate([r[...] for r in refs[5:5 + k]], axis=0)
        err = jnp.where(row >= BLOCK, y - tgt, 0.0)
        dr, dg, db = _ln_bwd(err * (1.0 / D_MODEL), xhat, rstd, g_ref[...])
        dr_ref[...] = dr
        drb_ref[...] = dr.astype(BF16)
        e2 = jnp.sum(err * err, axis=0, keepdims=True)
        part = e2[:, 0:BLOCK]
        for j in range(1, D_MODEL // BLOCK):
            part = part + e2[:, j * BLOCK:(j + 1) * BLOCK]
        part = part * (0.5 / D_MODEL)

        @pl.when(i == 0)
        def _():
            loss_ref[...] = part
            dg_ref[...] = dg
            db_ref[...] = db

        @pl.when(i > 0)
        def _():
            loss_ref[...] += part
            dg_ref[...] += dg
            db_ref[...] += db

    vec = pl.BlockSpec((1, D_MODEL), lambda i: (0, 0))
    rowsp = pl.BlockSpec((tr, D_MODEL), lambda i: (i, 0))
    return pl.pallas_call(
        body, name="ffn_out_loss", grid=(p // tr,),
        in_specs=[pl.BlockSpec((tr, D_FF), lambda i: (i, 0)), pl.BlockSpec((D_FF, D_MODEL), lambda i: (0, 0)),
                  rowsp, vec, vec] + _token_streams(tr),
        out_specs=[rowsp, rowsp, pl.BlockSpec((1, BLOCK), lambda i: (0, 0)), vec, vec],
        out_shape=[jax.ShapeDtypeStruct((p, D_MODEL), F32), jax.ShapeDtypeStruct((p, D_MODEL), BF16),
                   jax.ShapeDtypeStruct((1, BLOCK), F32), jax.ShapeDtypeStruct((1, D_MODEL), F32),
                   jax.ShapeDtypeStruct((1, D_MODEL), F32)],
        compiler_params=_cparams(("arbitrary",)),
    )(s, w_fo, h1, g2, b2, *([target] * k))


def _ffn_bwd(dr2, w_fo, au, w_fi_t):
    p = au.shape[0]
    tm = _tile(p, 416)

    def body(d_ref, wo_ref, au_ref, wi_ref, dau_ref, dh_ref):
        d = d_ref[...].astype(BF16)
        dh = ALPHA * d_ref[...]
        for j in range(D_FF // FF_T):
            a_cols = slice(2 * j * FF_T, (2 * j + 1) * FF_T)
            u_cols = slice((2 * j + 1) * FF_T, (2 * j + 2) * FF_T)
            ds = _dot(d, wo_ref[j * FF_T:(j + 1) * FF_T, :], "nt")
            _, vjp = jax.vjp(lambda a, u: jax.nn.silu(a) * u, au_ref[:, a_cols].astype(F32), au_ref[:, u_cols].astype(F32))
            da, du = vjp(ds)
            dau_ref[:, a_cols] = da.astype(BF16)
            dau_ref[:, u_cols] = du.astype(BF16)
            pair = slice(2 * j * FF_T, (2 * j + 2) * FF_T)
            dh = dh + _dot(dau_ref[:, pair], wi_ref[pair, :], "nn")
        dh_ref[...] = dh

    row = lambda w: pl.BlockSpec((tm, w), lambda i: (i, 0))
    kept = lambda a: pl.BlockSpec(a.shape, lambda i: (0, 0), pipeline_mode=pl.Buffered(1))
    return pl.pallas_call(
        body, name="ffn_bwd", grid=(p // tm,),
        in_specs=[row(D_MODEL), kept(w_fo), row(2 * D_FF), kept(w_fi_t)],
        out_specs=[row(2 * D_FF), row(D_MODEL)],
        out_shape=[jax.ShapeDtypeStruct((p, 2 * D_FF), BF16), jax.ShapeDtypeStruct((p, D_MODEL), F32)],
        compiler_params=_cparams(("arbitrary",)),
    )(dr2, w_fo, au, w_fi_t)


def _ln1_mix_bwd(dh1, xhat1, rstd1, g1, yh, oa, gates, mixin, w_bh, w_ba, w_out):
    p = yh.shape[0]
    tr = _tile(p, 320)
    nt = p // tr
    group = 2
    assert nt % group == 0, (p, tr)

    def body(dh_ref, xh_ref, rs_ref, g1_ref, yh_ref, oa_ref, g_ref, mix_ref, wbh_ref, wba_ref, wo_ref,
             dr_ref, dgt_ref, dyh_ref, doa_ref, dg_ref, db_ref, dwbh_ref, dwba_ref, dwo_ref,
             abh_ref, aba_ref, ao_ref, kept_l, kept_r):
        i = pl.program_id(0)
        dr, dg, db = _ln_bwd(dh_ref[...], xh_ref[...], rs_ref[...], g1_ref[...])
        dr_ref[...] = dr
        d = _dot(dr, wo_ref[...], "nt")
        _, y_hg, y_att, s1, s2 = _branch_mix(yh_ref[...], oa_ref[...], g_ref[...], wbh_ref[...], wba_ref[...])
        dy_hg = (d * s1).astype(BF16)
        dy_att = (d * s2).astype(BF16)
        dgt_ref[:, :D_MODEL] = (d * y_hg * s1 * (1.0 - s1)).astype(BF16)
        dgt_ref[:, D_MODEL:] = (d * y_att * s2 * (1.0 - s2)).astype(BF16)
        dyh_ref[...] = _dot(dy_hg, wbh_ref[...], "nt")
        doa_ref[...] = _dot(dy_att, wba_ref[...], "nt")

        rows = pl.ds(pl.multiple_of((i % group) * tr, tr), tr)
        kept_l[rows, :HG_W] = yh_ref[...]
        kept_l[rows, HG_W:HG_W + ATT_QW] = oa_ref[...]
        kept_l[rows, HG_W + ATT_QW:] = mix_ref[...]
        kept_r[rows, :D_MODEL] = dy_hg
        kept_r[rows, D_MODEL:2 * D_MODEL] = dy_att
        kept_r[rows, 2 * D_MODEL:] = dr.astype(BF16)

        @pl.when(i == 0)
        def _():
            dg_ref[...] = dg
            db_ref[...] = db
            for ref in (abh_ref, aba_ref, ao_ref):
                ref[...] = jnp.zeros_like(ref)

        @pl.when(i > 0)
        def _():
            dg_ref[...] += dg
            db_ref[...] += db

        @pl.when(i % group == group - 1)
        def _():
            abh_ref[...] += _dot(kept_l[:, :HG_W], kept_r[:, :D_MODEL], "tn")
            aba_ref[...] += _dot(kept_l[:, HG_W:HG_W + ATT_QW], kept_r[:, D_MODEL:2 * D_MODEL], "tn")
            ao_ref[...] += _dot(kept_l[:, HG_W + ATT_QW:], kept_r[:, 2 * D_MODEL:], "tn")

        @pl.when(i == nt - 1)
        def _():
            dwbh_ref[...] = abh_ref[...].astype(BF16)
            dwba_ref[...] = aba_ref[...].astype(BF16)
            dwo_ref[...] = ao_ref[...].astype(BF16)

    row = lambda w: pl.BlockSpec((tr, w), lambda i: (i, 0))
    const = lambda a: pl.BlockSpec(a.shape, lambda i: (0, 0), pipeline_mode=pl.Buffered(1))
    vec = pl.BlockSpec((1, D_MODEL), lambda i: (0, 0))
    weights = (w_bh, w_ba, w_out)
    return pl.pallas_call(
        body, name="ln1_mix_bwd", grid=(nt,),
        in_specs=[row(D_MODEL), row(D_MODEL), row(1), vec, row(HG_W), row(ATT_QW), row(2 * D_MODEL), row(D_MODEL)]
                 + [const(w) for w in weights],
        out_specs=[row(D_MODEL), row(2 * D_MODEL), row(HG_W), row(ATT_QW), vec, vec]
                  + [pl.BlockSpec(w.shape, lambda i: (0, 0)) for w in weights],
        out_shape=[jax.ShapeDtypeStruct((p, D_MODEL), F32), jax.ShapeDtypeStruct((p, 2 * D_MODEL), BF16),
                   jax.ShapeDtypeStruct((p, HG_W), F32), jax.ShapeDtypeStruct((p, ATT_QW), F32),
                   jax.ShapeDtypeStruct((1, D_MODEL), F32), jax.ShapeDtypeStruct((1, D_MODEL), F32)]
                  + [jax.ShapeDtypeStruct(w.shape, BF16) for w in weights],
        scratch_shapes=[pltpu.VMEM(w.shape, F32) for w in weights]
                       + [pltpu.VMEM((group * tr, HG_W + ATT_QW + D_MODEL), BF16),
                          pltpu.VMEM((group * tr, 3 * D_MODEL), BF16)],
        compiler_params=_cparams(("arbitrary",)),
    )(dh1, xhat1, rstd1, g1, yh, oa, gates, mixin, w_bh, w_ba, w_out)


MIX_W = 4 * HG_W + ATT_QW + 2 * ATT_KVW
ATT_KEYS = BLOCK + 2 * N_META


def _mixers_bwd(proj_hg, proj_att, lbounds, norm_g, lv, states, scores, raw, probs, cos, sin, sinks, dyh, doa,
                parts, swapped):
    p = proj_hg.shape[0]
    nb = p // BLOCK
    n = len(parts)
    kvw = 2 * ATT_KVW
    rev = lambda s: nb - 1 - s
    c_in, c_out, c_shapes, c_sems = _comm_specs(parts, N_PEERS)

    def body(*refs):
        (x_ref, lb_ref, ng_ref, lv_ref, st_ref, a_ref, raw_ref, pr_ref, cur_ref, prev_ref, meta_ref, cc, sc, cp, sp,
         cm, sm, sink_ref, dy_ref, do_ref) = refs[:20]
        part_refs = refs[20:20 + n]
        dx_ref, dlb_ref, dng_ref, dsink_ref = refs[20 + n:24 + n]
        recv_refs = refs[24 + n:24 + 2 * n]
        dcarry_ref, dkv_next_ref, dkv_meta_ref = refs[24 + 2 * n:27 + 2 * n]
        starts, waits = _scatter_behind(part_refs, recv_refs, refs[27 + 2 * n:], swapped)
        step = pl.program_id(0)
        c = rev(step)

        @pl.when(step == 0)
        def _():
            dcarry_ref[...] = jnp.zeros_like(dcarry_ref)
            dkv_next_ref[...] = jnp.zeros_like(dkv_next_ref)
            dkv_meta_ref[...] = jnp.zeros_like(dkv_meta_ref)
            dlb_ref[...] = jnp.zeros_like(dlb_ref)
            dng_ref[...] = jnp.zeros_like(dng_ref)
            dsink_ref[...] = jnp.zeros_like(dsink_ref)
            for start in starts:
                start()

        fh = _first_half(BLOCK)
        qs, kc, vc = _att_load(cur_ref, cc, sc, True)
        _, kp, vp = _att_load(prev_ref, cp, sp, False)
        km, vm = _att_load_meta(meta_ref, cm, sm)
        own4, band4, meta4 = _att_masks(c)
        att0 = 4 * HG_W
        dkm = dkp = dkc = dvm = dvp = dvc = 0.0
        dsinkrows = []
        for g in range(2):
            pr = pr_ref[0, g, :BLOCK, :].astype(F32)
            pr_meta = pr_ref[0, g, BLOCK:BLOCK + N_META, :].astype(F32)
            pr_sink = jnp.max(pr_ref[0, g, BLOCK + N_META:, :].astype(F32), axis=0, keepdims=True)
            _, values_vjp = jax.vjp(lambda *a, g=g: _att_values(*a, g, own4), pr, pr_meta, vc, vp, vm)
            dpr, dpr_meta, dvc_g, dvp_g, dvm_g = values_vjp(
                [do_ref[:, (2 * g + j) * BLOCK:(2 * g + j + 1) * BLOCK] for j in range(2)])
            ds, ds_meta, dsinkrow = _att_probs_bwd(pr, pr_meta, pr_sink, dpr, dpr_meta)
            _, scores_vjp = jax.vjp(lambda *a, g=g: _att_scores(*a, g, own4, band4, meta4),
                                    qs[2 * g], qs[2 * g + 1], kc, kp, km)
            dqa, dqb, dkc_g, dkp_g, dkm_g = scores_vjp((ds, ds_meta))
            for j, dq in enumerate((dqa, dqb)):
                dx_ref[:, att0 + (2 * g + j) * BLOCK:att0 + (2 * g + j + 1) * BLOCK] = _rope_t(
                    dq, cc[...], sc[...], fh).astype(BF16)
            dkm, dkp, dkc = dkm + dkm_g, dkp + dkp_g, dkc + dkc_g
            dvm, dvp, dvc = dvm + dvm_g, dvp + dvp_g, dvc + dvc_g
            dsinkrows.append(dsinkrow)
        ds0, ds1 = dsinkrows
        dkv_meta_ref[:, :BLOCK] += _rope_t(dkm, cm[PAD:BLOCK, :], sm[PAD:BLOCK, :], _first_half(N_META))
        dkv_meta_ref[:, BLOCK:] += dvm
        last = jnp.where(c == 0, 1.0, 0.0)
        to_meta_rows = lambda m: jnp.concatenate([jnp.zeros((PAD, BLOCK), F32), last * m], axis=0)
        dk = _rope_t(dkc, cc[...], sc[...], fh) + dkv_next_ref[:, :BLOCK] + to_meta_rows(dkv_meta_ref[:, :BLOCK])
        dv = dvc + dkv_next_ref[:, BLOCK:] + to_meta_rows(dkv_meta_ref[:, BLOCK:])
        dx_ref[:, att0 + ATT_QW:att0 + ATT_QW + ATT_KVW] = dk.astype(BF16)
        dx_ref[:, att0 + ATT_QW + ATT_KVW:] = dv.astype(BF16)
        dkv_next_ref[:, :BLOCK] = _rope_t(dkp, cp[...], sp[...], fh)
        dkv_next_ref[:, BLOCK:] = dvp
        sink_rows = []
        for dsg in (ds0, ds1):
            for j in range(4):
                tot = jnp.sum(dsg[:, j * BLOCK:(j + 1) * BLOCK], axis=1, keepdims=True)
                sink_rows.append(jnp.broadcast_to(tot, (1, BLOCK)))
        dsink_ref[...] += jnp.concatenate(sink_rows, axis=0)

        valid = (c * BLOCK + lax.broadcasted_iota(jnp.int32, (BLOCK, 1), 0)) >= PAD
        (logf, k), gates_vjp = jax.vjp(lambda hf, a0, a1: _hgrn_gates(hf, a0, a1, valid),
                                       x_ref[:, HG_W:2 * HG_W], lb_ref[0:1, :], lb_ref[1:2, :])
        lvv = lv_ref[...]
        e = _split_dot(lvv, logf, "nn")
        dng = jnp.zeros((1, BLOCK), F32)
        dk, dseg = [], []
        for h in range(HG_HEADS):
            sl = lambda part: x_ref[:, part * HG_W + h * BLOCK: part * HG_W + (h + 1) * BLOCK]
            hs = slice(h * BLOCK, (h + 1) * BLOCK)
            seg = _seg_blocks(e, h)
            _, norm_vjp = jax.vjp(_hgrn_norm, raw_ref[:, hs], sl(3), ng_ref[...])
            draw, dhg, dngh = norm_vjp(dy_ref[:, hs])
            _, mix_vjp = jax.vjp(_hgrn_mix, sl(0), k[:, hs], sl(2), st_ref[0, h], a_ref[0, h].astype(F32), *seg[:3])
            dhq, dkh, dhi, dst, da, *dseg_mix = mix_vjp((draw, dcarry_ref[h]))
            _, scores_vjp = jax.vjp(_hgrn_scores, sl(0), k[:, hs], *seg[3:])
            dhq2, dkh2, *dseg_lvl = scores_vjp(da)
            for part, val in ((0, dhq + dhq2), (2, dhi), (3, dhg)):
                dx_ref[:, part * HG_W + h * BLOCK: part * HG_W + (h + 1) * BLOCK] = val.astype(BF16)
            dk.append(dkh + dkh2)
            dseg.append(jnp.concatenate(dseg_mix + dseg_lvl, axis=0))
            dng = dng + dngh
            dcarry_ref[h] = dst
        dlogf = _split_dot(lvv, jnp.concatenate(dseg, axis=1), "tn")
        dhf, da0, da1 = gates_vjp((dlogf, jnp.concatenate(dk, axis=1)))
        dx_ref[:, HG_W:2 * HG_W] = dhf.astype(BF16)
        dlb_ref[0:1, :] += da0
        dlb_ref[1:2, :] += da1
        dng_ref[...] += dng

        @pl.when(step == nb - 1)
        def _():
            for wait in waits:
                wait()

    const = lambda shape: pl.BlockSpec(shape, lambda s: (0,) * len(shape))
    per_head = pl.BlockSpec((1, HG_HEADS, BLOCK, BLOCK), lambda s: (rev(s), 0, 0, 0))
    return pl.pallas_call(
        body, name="mixers_bwd", grid=(nb,),
        in_specs=[pl.BlockSpec((BLOCK, 4 * HG_W), lambda s: (rev(s), 0)), const((2, HG_W)), const((1, BLOCK)),
                  const(lv.shape), per_head, per_head, pl.BlockSpec((BLOCK, HG_W), lambda s: (rev(s), 0)),
                  pl.BlockSpec((1, 2, ATT_KEYS, 4 * BLOCK), lambda s: (rev(s), 0, 0, 0))]
        + _att_specs(rev)
        + [pl.BlockSpec((BLOCK, HG_W), lambda s: (rev(s), 0)), pl.BlockSpec((BLOCK, ATT_QW), lambda s: (rev(s), 0))]
        + c_in,
        out_specs=[pl.BlockSpec((BLOCK, MIX_W), lambda s: (rev(s), 0)), const((2, HG_W)), const((1, BLOCK)),
                   const((ATT_HEADS, BLOCK))] + c_out,
        out_shape=[jax.ShapeDtypeStruct((p, MIX_W), BF16), jax.ShapeDtypeStruct((2, HG_W), F32),
                   jax.ShapeDtypeStruct((1, BLOCK), F32), jax.ShapeDtypeStruct((ATT_HEADS, BLOCK), F32)] + c_shapes,
        scratch_shapes=[pltpu.VMEM((HG_HEADS, BLOCK, BLOCK), F32), pltpu.VMEM((BLOCK, kvw), F32),
                        pltpu.VMEM((N_META, kvw), F32)] + c_sems,
        compiler_params=_cparams(("arbitrary",)),
    )(proj_hg, lbounds, norm_g, lv, states, scores, raw, probs, proj_att, proj_att, proj_att, cos, sin, cos, sin,
      cos, sin, sinks, dyh, doa, *parts)


_HBM = pl.BlockSpec(memory_space=pltpu.HBM)
_SEM = pl.BlockSpec(memory_space=pltpu.SEMAPHORE)
_ORDERED_BY_DATA = pltpu.CompilerParams(has_side_effects=pltpu.SideEffectType.DATAFLOW_SIDE_EFFECTING)


def _exchange_copies(part_ref, land_ref, send_sems, recv_sems):
    place = _place()
    return [pltpu.make_async_remote_copy(
        src_ref=part_ref.at[_slot(_peer(place, flip), False)], dst_ref=land_ref.at[r], send_sem=send_sems.at[r],
        recv_sem=recv_sems.at[r], device_id=_peer(place, flip), device_id_type=MESH) for r, flip in enumerate(_FLIPS)]


def _exchange_start(parts, name):
    def body(part_ref, land_ref, send_sems, recv_sems, part_thru, land_thru, token):
        for cp in _exchange_copies(part_ref, land_ref, send_sems, recv_sems):
            cp.start()
        token[...] = jnp.zeros_like(token)

    land = (N_PEERS,) + parts.shape[1:]
    return pl.pallas_call(
        body, name=name,
        out_shape=(pltpu.SemaphoreType.DMA((N_PEERS,)), pltpu.SemaphoreType.DMA((N_PEERS,)),
                   pltpu.HBM(parts.shape, parts.dtype), pltpu.HBM(land, parts.dtype), jax.ShapeDtypeStruct((8, BLOCK), F32)),
        in_specs=(_HBM, _HBM), out_specs=(_SEM, _SEM, _HBM, _HBM, pl.BlockSpec(memory_space=pltpu.VMEM)),
        input_output_aliases={0: 2, 1: 3}, compiler_params=_ORDERED_BY_DATA,
    )(pltpu.with_memory_space_constraint(parts, pltpu.HBM),
      pltpu.with_memory_space_constraint(lax.empty(land, parts.dtype), pltpu.HBM))


def _exchange_wait(send_sems, recv_sems, part_thru, land_thru, after, name):
    def body(part_ref, land_ref, send_sems, recv_sems, after_ref, part_out, land_out):
        for cp in _exchange_copies(part_ref, land_ref, send_sems, recv_sems):
            cp.wait_send()
            cp.wait_recv()

    return pl.pallas_call(
        body, name=name,
        out_shape=(pltpu.HBM(part_thru.shape, part_thru.dtype), pltpu.HBM(land_thru.shape, land_thru.dtype)),
        in_specs=(_HBM, _HBM, _SEM, _SEM, pl.BlockSpec(memory_space=pl.ANY)), out_specs=(_HBM, _HBM),
        input_output_aliases={0: 0, 1: 1}, compiler_params=_ORDERED_BY_DATA,
    )(part_thru, land_thru, send_sems, recv_sems, after)


def _embed_bwd(dmix, dgates, w_in_t, dr1, xhat0, rstd0, g0):
    p = dmix.shape[0]
    tm = _row_tile(p, 640)
    nm = p // tm

    def body(a_ref, g_ref, w_ref, dr_ref, xh_ref, rs_ref, g0_ref, gx_ref, lead_ref, dg_ref, db_ref, buf_ref, sem):
        i = pl.program_id(0)
        first = pltpu.make_async_copy(buf_ref.at[0, pl.ds(BLOCK, tm - BLOCK)], gx_ref.at[pl.ds(0, tm - BLOCK)],
                                      sem.at[0])
        later = lambda t: pltpu.make_async_copy(buf_ref.at[t % 2], gx_ref.at[pl.ds(t * tm - BLOCK, tm)], sem.at[t % 2])

        @pl.when(i == 2)
        def _():
            first.wait()

        @pl.when(i > 2)
        def _():
            later(i - 2).wait()

        dh0 = (ALPHA * dr_ref[...] + _dot(a_ref[...], w_ref[:MIX_W, :], "nn")
               + _dot(g_ref[...], w_ref[MIX_W:, :], "nn"))
        row = i * tm + lax.broadcasted_iota(jnp.int32, (tm, 1), 0)
        dx, dg, db = _ln_bwd(jnp.where(row >= PAD, dh0, 0.0), xh_ref[...], rs_ref[...], g0_ref[...])
        buf_ref[i % 2] = dx

        @pl.when(i == 0)
        def _():
            lead_ref[...] = dx[:BLOCK]
            dg_ref[...] = dg
            db_ref[...] = db
            first.start()

        @pl.when(i > 0)
        def _():
            dg_ref[...] += dg
            db_ref[...] += db
            later(i).start()

        @pl.when(i == nm - 1)
        def _():
            for t in (nm - 2, nm - 1):
                if t >= 0:
                    (first if t == 0 else later(t)).wait()

    row = lambda w: pl.BlockSpec((tm, w), lambda i: (i, 0))
    vec = pl.BlockSpec((1, D_MODEL), lambda i: (0, 0))
    return pl.pallas_call(
        body, name="embed_bwd", grid=(nm,),
        in_specs=[row(dmix.shape[1]), row(dgates.shape[1]),
                  pl.BlockSpec(w_in_t.shape, lambda i: (0, 0), pipeline_mode=pl.Buffered(1)), row(D_MODEL), row(D_MODEL),
                  row(1), vec],
        out_specs=[pl.BlockSpec(memory_space=pl.ANY), pl.BlockSpec((BLOCK, D_MODEL), lambda i: (0, 0)), vec, vec],
        out_shape=[jax.ShapeDtypeStruct((p - BLOCK, D_MODEL), F32), jax.ShapeDtypeStruct((BLOCK, D_MODEL), F32),
                   jax.ShapeDtypeStruct((1, D_MODEL), F32), jax.ShapeDtypeStruct((1, D_MODEL), F32)],
        scratch_shapes=[pltpu.VMEM((2, tm, D_MODEL), F32), pltpu.SemaphoreType.DMA((2,))],
        compiler_params=_cparams(("arbitrary",)),
    )(dmix, dgates, w_in_t, dr1, xhat0, rstd0, g0)


_LATE = ("w_branch_hg", "w_branch_attn", "w_out", "w_ffn_in", "w_ffn_out")
_TRANSPOSED = ("w_in", "w_ffn_in")
_COLUMN_SHARDED = ("meta_tokens", "w_branch_hg", "w_branch_attn")
_SWAPPED = ("w_ffn_in",)


def _whole(name, gathered):
    _, r, c = gathered.shape
    if name in _COLUMN_SHARDED:
        return jnp.transpose(gathered, (1, 0, 2)).reshape(r, N_DEV * c)
    return gathered.reshape(N_DEV * r, c)


def _slots(name, whole):
    r, c = whole.shape
    if name in _COLUMN_SHARDED:
        return jnp.transpose(whole.reshape(r, N_DEV, c // N_DEV), (1, 0, 2))
    return whole.reshape(N_DEV, r // N_DEV, c)


def _device_step(x, target, meta_shard, ln_emb_g, ln_emb_b, w_in_shard, lbounds, norm_g, sinks, late_shards,
                 ln1_g, ln1_b, ln2_g, ln2_b):
    p = x.shape[0] + BLOCK
    lv = _level_stack()
    cos, sin = _rope_tables(p)
    swapped = [n in _SWAPPED for n in _LATE]

    h0, h0b, xhat0, rstd0, _, g_win = _embed_ln(x, meta_shard, w_in_shard, ln_emb_g, ln_emb_b)
    w_in = _whole("w_in", g_win)
    proj_hg, proj_att = _in_proj(h0b, w_in)
    yh, oa, states, scores, raw, probs, *gathered = _mixers_fwd(
        proj_hg, proj_att, lbounds, norm_g, lv, cos, sin, sinks, late_shards, swapped)
    w_bh, w_ba, w_out, w_fi, w_fo = [_whole(n, g) for n, g in zip(_LATE, gathered)]
    gates, mixin, h1, h1b, xhat1, rstd1 = _mix_out_ln1(yh, oa, h0b, w_in, h0, w_bh, w_ba, w_out, ln1_g, ln1_b)
    au, sw = _ffn_in_swiglu(h1b, w_fi)
    dr2, dr2b, loss_part, dg2, db2 = _ffn_out_loss(sw, w_fo, h1, ln2_g, ln2_b, target)

    d_wfo = _tiled_matmul_tn(sw, dr2b, tm=_tile(p, 2080), tk=FF_T, tn=D_MODEL, out_dtype=BF16, name="grad_w_ffn_out")
    dau, dh1 = _ffn_bwd(dr2, w_fo, au, w_fi)
    d_wfi = _weight_grad_t([dau], h1b, tk=4 * BLOCK, name="grad_w_ffn_in")
    dr1, dgates, dyh, doa, dg1, db1, d_wbh, d_wba, d_wout = _ln1_mix_bwd(
        dh1, xhat1, rstd1, ln1_g, yh, oa, gates, mixin, w_bh, w_ba, w_out)
    late_parts = [_slots(n, g) for n, g in zip(_LATE, (d_wbh, d_wba, d_wout, d_wfi, d_wfo))]
    dmix, d_lb, d_ng, d_sink, *late_recv = _mixers_bwd(
        proj_hg, proj_att, lbounds, norm_g, lv, states, scores, raw, probs, cos, sin, sinks, dyh, doa, late_parts,
        swapped)
    d_win = _weight_grad_t([dmix, dgates], h0b, tk=2 * BLOCK, name="grad_w_in")
    *win_flight, token = _exchange_start(_slots("w_in", d_win), "w_in_grads_start")
    grad_x, dlead, dg0, db0 = _embed_bwd(dmix, dgates, w_in, dr1, xhat0, rstd0, ln_emb_g + token[0:1, 0:1])

    small = dict(ln_emb_g=dg0, ln_emb_b=db0, hg_lower_bounds=d_lb, hg_norm_g=d_ng, ln1_g=dg1, ln1_b=db1, ln2_g=dg2,
                 ln2_b=db2)
    big = dict(zip(_LATE, zip(late_parts, late_recv)))
    return _pack_small(small, d_sink, dlead, loss_part), grad_x, big, win_flight


def _all_gather(arrs, dtypes, name):
    n = len(arrs)

    def body(*refs):
        ins, outs, stages = refs[:n], refs[n:2 * n], refs[2 * n:3 * n]
        send_sems, recv_sems, local_sems = refs[3 * n:]
        x, y, c = _place()
        sibling = (x, y, 1 - c)
        chips = [(1 - x, y), (x, 1 - y), (1 - x, 1 - y)]
        slot = lambda px, py, pc: 4 * px + 2 * py + pc

        def copy(w, k, block, to, from_stage=False):
            return pltpu.make_async_remote_copy(
                src_ref=stages[w] if from_stage else outs[w].at[slot(*block)], dst_ref=outs[w].at[slot(*block)],
                send_sem=send_sems.at[w, k], recv_sem=recv_sems.at[w, k], device_id=to, device_id_type=MESH)

        mine, first, passed = [], [], []
        for w in range(n):
            stages[w][...] = ins[w][...].astype(dtypes[w])
            mine.append(pltpu.make_async_copy(stages[w], outs[w].at[slot(x, y, c)], local_sems.at[w]))
            mine[-1].start()
        for w in range(n):
            first.append(copy(w, 0, (x, y, c), sibling, from_stage=True))
            first += [copy(w, 1 + j, (x, y, c), (*chip, c), from_stage=True) for j, chip in enumerate(chips)]
        for cp in first:
            cp.start()
        for j, chip in enumerate(chips):
            for w in range(n):
                copy(w, 1 + j, (*chip, c), (x, y, c)).wait_recv()
                passed.append(copy(w, 4 + j, (*chip, c), sibling))
                passed[-1].start()
        for w in range(n):
            copy(w, 0, sibling, (x, y, c)).wait_recv()
            for j, chip in enumerate(chips):
                copy(w, 4 + j, (*chip, 1 - c), (x, y, c)).wait_recv()
        for cp in first + passed:
            cp.wait_send()
        for cp in mine:
            cp.wait()

    return pl.pallas_call(
        body, name=name,
        in_specs=[pl.BlockSpec(memory_space=pltpu.VMEM)] * n,
        out_specs=[pl.BlockSpec(memory_space=pl.ANY)] * n,
        out_shape=[jax.ShapeDtypeStruct((N_DEV,) + a.shape, dt) for a, dt in zip(arrs, dtypes)],
        scratch_shapes=[pltpu.VMEM(a.shape, dt) for a, dt in zip(arrs, dtypes)]
        + [pltpu.SemaphoreType.DMA((n, 7)), pltpu.SemaphoreType.DMA((n, 7)), pltpu.SemaphoreType.DMA((n,))],
        compiler_params=pltpu.CompilerParams(vmem_limit_bytes=VMEM_LIMIT_BYTES),
    )(*arrs)


def _cast_shards(arrs):
    def body(*refs):
        for src, dst in zip(refs[:len(arrs)], refs[len(arrs):]):
            dst[...] = src[...].astype(BF16)

    return pl.pallas_call(body, name="cast_shards", out_shape=[jax.ShapeDtypeStruct(a.shape, BF16) for a in arrs],
                          compiler_params=pltpu.CompilerParams(vmem_limit_bytes=VMEM_LIMIT_BYTES))(*arrs)


def _shard_rows(rows):
    return rows if rows <= 512 else max(t for t in range(16, 353, 16) if rows % t == 0)


def _adamw_math(w, g, m, v):
    m = ADAM_B1 * m + (1.0 - ADAM_B1) * g
    v = ADAM_B2 * v + (1.0 - ADAM_B2) * (g * g)
    m_hat = m / (1.0 - ADAM_B1 ** ADAM_STEP)
    v_hat = v / (1.0 - ADAM_B2 ** ADAM_STEP)
    delta = -ADAM_LR * (m_hat / (jnp.sqrt(v_hat) + ADAM_EPS) + ADAM_WD * w)
    return delta, m, v


def _reduce_adamw(parts, recv, own_slot, w, m, v, name):
    r, cdim = w.shape
    tr = _shard_rows(r)

    def body(idx_ref, p_ref, r_ref, w_ref, m_ref, v_ref, g_out, d_out, m_out, v_out):
        g = p_ref[0].astype(F32)
        for j in range(N_PEERS):
            g = g + r_ref[j].astype(F32)
        d, mn, vn = _adamw_math(w_ref[...], g, m_ref[...], v_ref[...])
        g_out[...] = g
        d_out[...] = d
        m_out[...] = mn
        v_out[...] = vn

    flat = pl.BlockSpec((tr, cdim), lambda i, idx_ref: (i, 0))
    return pl.pallas_call(
        body, name=name,
        grid_spec=pltpu.PrefetchScalarGridSpec(
            num_scalar_prefetch=1, grid=(r // tr,),
            in_specs=[pl.BlockSpec((1, tr, cdim), lambda i, idx_ref: (idx_ref[0], i, 0)),
                      pl.BlockSpec((N_PEERS, tr, cdim), lambda i, idx_ref: (0, i, 0)), flat, flat, flat],
            out_specs=[flat] * 4),
        out_shape=[jax.ShapeDtypeStruct((r, cdim), F32)] * 4,
        compiler_params=_cparams(("arbitrary",)),
    )(own_slot, parts, recv, w, m, v)


def _adamw_plain(w, g, m, v, name):
    def body(w_ref, g_ref, m_ref, v_ref, d_out, m_out, v_out):
        d_out[...], m_out[...], v_out[...] = _adamw_math(w_ref[...], g_ref[...], m_ref[...], v_ref[...])

    return pl.pallas_call(body, name=name, out_shape=[jax.ShapeDtypeStruct(w.shape, F32)] * 3)(w, g, m, v)


_SMALL = (("ln_emb_g", (1, D_MODEL)), ("ln_emb_b", (1, D_MODEL)), ("hg_lower_bounds", (2, HG_W)),
          ("hg_norm_g", (1, BLOCK)), ("attn_sinks", (1, ATT_HEADS)), ("ln1_g", (1, D_MODEL)), ("ln1_b", (1, D_MODEL)),
          ("ln2_g", (1, D_MODEL)), ("ln2_b", (1, D_MODEL)))
_SMALL_ROW, _LOSS_ROW = {}, 0
for _name, (_rows, _) in _SMALL:
    _SMALL_ROW[_name], _LOSS_ROW = _LOSS_ROW, _LOSS_ROW + _rows
_META_ROW = 16
SMALL_ROWS = _META_ROW + N_META
assert _LOSS_ROW < _META_ROW


def _pack_small(grads, d_sink, dlead, loss_part):
    names = [n for n, _ in _SMALL if n != "attn_sinks"]

    def body(*refs):
        ins = dict(zip(names, refs))
        sink_ref, lead_ref, loss_ref, o_ref = refs[len(names):]
        o_ref[...] = jnp.zeros_like(o_ref)
        for name, (rows, cols) in _SMALL:
            if name != "attn_sinks":
                o_ref[_SMALL_ROW[name]:_SMALL_ROW[name] + rows, :cols] = ins[name][...]
        head = lax.broadcasted_iota(jnp.int32, (ATT_HEADS, BLOCK), 0)
        lane = lax.broadcasted_iota(jnp.int32, (ATT_HEADS, BLOCK), 1)
        o_ref[_SMALL_ROW["attn_sinks"]:_SMALL_ROW["attn_sinks"] + 1, :BLOCK] = jnp.sum(
            jnp.where(head == lane, sink_ref[...], 0.0), axis=0, keepdims=True)
        o_ref[_LOSS_ROW:_LOSS_ROW + 1, :BLOCK] = loss_ref[...]
        o_ref[_META_ROW:, :] = lead_ref[PAD:BLOCK, :]

    return pl.pallas_call(body, name="pack_small", out_shape=jax.ShapeDtypeStruct((SMALL_ROWS, D_MODEL), F32))(
        *[grads[n] for n in names], d_sink, dlead, loss_part)


def _small_reduce_adamw(gathered, weights, mom1, mom2):
    n = len(_SMALL)

    def body(*refs):
        g_ref, w_refs, m_refs, v_refs = refs[0], refs[1:1 + n], refs[1 + n:1 + 2 * n], refs[1 + 2 * n:1 + 3 * n]
        outs = refs[1 + 3 * n:1 + 7 * n]
        meta_out, loss_out, sum_ref = refs[1 + 7 * n:]
        total = g_ref[0]
        for s in range(1, N_DEV):
            total = total + g_ref[s]
        sum_ref[...] = total
        for i, (name, (rows, cols)) in enumerate(_SMALL):
            g = sum_ref[_SMALL_ROW[name]:_SMALL_ROW[name] + rows, :cols]
            d, mn, vn = _adamw_math(w_refs[i][...], g, m_refs[i][...], v_refs[i][...])
            for out, val in zip(outs[4 * i:4 * i + 4], (g, d, mn, vn)):
                out[...] = val
        meta_out[...] = sum_ref[_META_ROW:, :]
        loss_out[...] = jnp.broadcast_to(jnp.sum(sum_ref[_LOSS_ROW:_LOSS_ROW + 1, :BLOCK]), (1, BLOCK))

    per_param = [jax.ShapeDtypeStruct(shape, F32) for _, shape in _SMALL for _ in range(4)]
    res = pl.pallas_call(
        body, name="small_reduce_adamw",
        out_shape=per_param + [jax.ShapeDtypeStruct((N_META, D_MODEL), F32), jax.ShapeDtypeStruct((1, BLOCK), F32)],
        scratch_shapes=[pltpu.VMEM((SMALL_ROWS, D_MODEL), F32)],
    )(gathered, *[d[name] for d in (weights, mom1, mom2) for name, _ in _SMALL])
    return {name: res[4 * i:4 * i + 4] for i, (name, _) in enumerate(_SMALL)}, res[-2], res[-1]


_WEIGHTS = ("meta_tokens", "ln_emb_g", "ln_emb_b", "w_in", "hg_lower_bounds", "hg_norm_g", "attn_sinks",
            "w_branch_hg", "w_branch_attn", "w_out", "ln1_g", "ln1_b", "w_ffn_in", "w_ffn_out", "ln2_g", "ln2_b")


def kernel(x, meta_tokens, ln_emb_g, ln_emb_b, w_in, hg_lower_bounds, hg_norm_g, attn_sinks, w_branch_hg, w_branch_attn, w_out, ln1_g, ln1_b, w_ffn_in, w_ffn_out, ln2_g, ln2_b, loss_target, m_meta_tokens, m_ln_emb_g, m_ln_emb_b, m_w_in, m_hg_lower_bounds, m_hg_norm_g, m_attn_sinks, m_w_branch_hg, m_w_branch_attn, m_w_out, m_ln1_g, m_ln1_b, m_w_ffn_in, m_w_ffn_out, m_ln2_g, m_ln2_b, v_meta_tokens, v_ln_emb_g, v_ln_emb_b, v_w_in, v_hg_lower_bounds, v_hg_norm_g, v_attn_sinks, v_w_branch_hg, v_w_branch_attn, v_w_out, v_ln1_g, v_ln1_b, v_w_ffn_in, v_w_ffn_out, v_ln2_g, v_ln2_b):
    given = dict(locals())
    weights = {n: given[n] for n in _WEIGHTS}
    mom1 = {n: given["m_" + n] for n in _WEIGHTS}
    mom2 = {n: given["v_" + n] for n in _WEIGHTS}
    shard2d = lambda n, a: a.reshape(a.shape[-2:]).T if n in _TRANSPOSED else a.reshape(a.shape[-2:])

    w_in_shard, *late_shards = _cast_shards([shard2d(n, weights[n]) for n in ("w_in",) + _LATE])
    packed, grad_x, big, win_flight = _device_step(
        x[0], loss_target[0], meta_tokens, ln_emb_g.reshape(1, -1), ln_emb_b.reshape(1, -1), w_in_shard,
        hg_lower_bounds, hg_norm_g, attn_sinks, late_shards, ln1_g, ln1_b, ln2_g, ln2_b)

    place = _place()
    out = {}

    def reduce_adamw(n, parts, recv):
        own = _slot(place, n in _SWAPPED).astype(jnp.int32).reshape(1)
        res = _reduce_adamw(parts, recv, own, shard2d(n, weights[n]), shard2d(n, mom1[n]), shard2d(n, mom2[n]),
                            "adamw_" + n)
        out[n] = [(r.T if n in _TRANSPOSED else r).reshape(weights[n].shape) for r in res]

    for n, (parts, recv) in big.items():
        reduce_adamw(n, parts, recv)

    all_small, = _all_gather([packed], [F32], "gather_small")
    as_2d = lambda d: {n: d[n].reshape(shape) for n, shape in _SMALL}
    small_out, meta_whole, loss_row = _small_reduce_adamw(all_small, as_2d(weights), as_2d(mom1), as_2d(mom2))
    for n, res in small_out.items():
        out[n] = [r.reshape(weights[n].shape) for r in res]
    loss = loss_row[0, 0]
    g_meta_mine = lax.dynamic_index_in_dim(meta_whole.reshape(N_META, N_DEV, D_MODEL // N_DEV), _slot(place, False),
                                           axis=1, keepdims=False)
    out["meta_tokens"] = [g_meta_mine, *_adamw_plain(meta_tokens, g_meta_mine, m_meta_tokens, v_meta_tokens,
                                                     "adamw_meta")]

    reduce_adamw("w_in", *_exchange_wait(*win_flight, after=all_small, name="w_in_grads_wait"))

    return (loss, grad_x[None], *[out[n][0] for n in _WEIGHTS], *[out[n][1] for n in _WEIGHTS],
            *[out[n][2] for n in _WEIGHTS], *[out[n][3] for n in _WEIGHTS])
```

```python
import functools

import numpy as np
import jax
import jax.numpy as jnp
from jax import lax
from jax.experimental import pallas as pl
from jax.experimental.pallas import tpu as pltpu

F32 = jnp.float32
BF16 = jnp.bfloat16

D_MODEL = 1024
N_META = 16
BLOCK = 128
PAD = BLOCK - N_META
HG_HEADS = 4
HG_W = 512
ATT_HEADS = 8
HEAD_DIM = 64
ATT_QW = 512
ATT_KVW = 128
D_FF = 2816
EPS = 1e-5
ALPHA = 2.0 ** 0.25
ROPE_THETA = 10000.0
N_DEV = 8

ADAM_LR = 0.001
ADAM_B1 = 0.9
ADAM_B2 = 0.999
ADAM_EPS = 1e-08
ADAM_WD = 0.01
ADAM_STEP = 10

VMEM_LIMIT_BYTES = 56 * 1024 * 1024
MESH = pl.DeviceIdType.MESH

_LEVELS = (64, 32, 16, 8, 4, 2, 1)


def _cparams(sem):
    return pltpu.CompilerParams(dimension_semantics=sem, vmem_limit_bytes=VMEM_LIMIT_BYTES)


def _row_tile(rows, target):
    nb = rows // BLOCK
    best = 1
    for d in range(1, nb + 1):
        if nb % d == 0 and d * BLOCK <= target:
            best = d
    return best * BLOCK


_DN = {"nn": (((1,), (0,)), ((), ())), "nt": (((1,), (1,)), ((), ())), "tn": (((0,), (0,)), ((), ()))}


def _dot(a, b, form):
    return lax.dot_general(a.astype(BF16), b.astype(BF16), _DN[form], preferred_element_type=F32)


@functools.partial(jax.custom_vjp, nondiff_argnums=(2,))
def _mm(a, b, form):
    return _dot(a, b, form)


def _mm_fwd(a, b, form):
    a, b = a.astype(BF16), b.astype(BF16)
    return _dot(a, b, form), (a, b)


def _mm_bwd(form, res, g):
    a, b = res
    if form == "nn":
        return _dot(g, b, "nt"), _dot(a, g, "tn")
    if form == "nt":
        return _dot(g, b, "nn"), _dot(g, a, "tn")
    return _dot(b, g, "nt"), _dot(a, g, "nn")


_mm.defvjp(_mm_fwd, _mm_bwd)


def _split_dot(lv, x, form):
    return lax.dot_general(lv, x.astype(BF16), _DN[form], preferred_element_type=F32)


@jax.custom_vjp
def _swap_halves(x):
    return pltpu.roll(x, 64, 1)


_swap_halves.defvjp(lambda x: (pltpu.roll(x, 64, 1), None), lambda _, g: (pltpu.roll(g, 64, 1),))


def _tiled_matmul_tn(a, b, *, tm, tk, tn, out_dtype, name):
    m, k = a.shape
    n = b.shape[1]
    assert m % tm == 0 and k % tk == 0 and n % tn == 0, (name, a.shape, b.shape, tm, tk, tn)
    nm = m // tm

    def body(a_ref, b_ref, o_ref, acc_ref):
        mi = pl.program_id(2)

        @pl.when(mi == 0)
        def _():
            acc_ref[...] = jnp.zeros_like(acc_ref)

        acc_ref[...] += _dot(a_ref[...], b_ref[...], "tn")

        @pl.when(mi == nm - 1)
        def _():
            o_ref[...] = acc_ref[...].astype(out_dtype)

    return pl.pallas_call(
        body, name=name, grid=(k // tk, n // tn, nm),
        in_specs=[pl.BlockSpec((tm, tk), lambda kk, j, i: (i, kk)), pl.BlockSpec((tm, tn), lambda kk, j, i: (i, j))],
        out_specs=pl.BlockSpec((tk, tn), lambda kk, j, i: (kk, j)),
        out_shape=jax.ShapeDtypeStruct((k, n), out_dtype),
        scratch_shapes=[pltpu.VMEM((tk, tn), F32)],
        compiler_params=_cparams(("arbitrary", "arbitrary", "arbitrary")),
    )(a, b)


def _weight_grad_t(cots, h, *, tk, name):
    p, d = h.shape
    steps = [c.shape[1] // tk for c in cots]
    assert all(c.shape == (p, n * tk) for c, n in zip(cots, steps)), (name, [c.shape for c in cots], tk)
    first = [sum(steps[:i]) for i in range(len(cots))]

    def body(*refs):
        h_ref, o_ref = refs[len(cots)], refs[len(cots) + 1]
        k = pl.program_id(0)
        for c_ref, lo, n in zip(refs, first, steps):
            @pl.when((k >= lo) & (k < lo + n))
            def _(c_ref=c_ref):
                o_ref[...] = _dot(c_ref[...], h_ref[...], "tn").astype(BF16)

    cot_spec = lambda lo, n: pl.BlockSpec((p, tk), lambda k: (0, jnp.clip(k - lo, 0, n - 1)))
    return pl.pallas_call(
        body, name=name, grid=(sum(steps),),
        in_specs=[cot_spec(lo, n) for lo, n in zip(first, steps)]
                 + [pl.BlockSpec((p, d), lambda k: (0, 0), pipeline_mode=pl.Buffered(1))],
        out_specs=pl.BlockSpec((tk, d), lambda k: (k, 0)),
        out_shape=jax.ShapeDtypeStruct((sum(steps) * tk, d), BF16),
        compiler_params=_cparams(("arbitrary",)),
    )(*cots, h)


def _ln_stats(r):
    mu = jnp.mean(r, axis=-1, keepdims=True)
    xc = r - mu
    var = jnp.mean(xc * xc, axis=-1, keepdims=True)
    rstd = lax.rsqrt(var + EPS)
    return xc * rstd, rstd


def _ln_bwd(dy, xhat, rstd, g):
    dxhat = dy * g
    m1 = jnp.mean(dxhat, axis=-1, keepdims=True)
    m2 = jnp.mean(dxhat * xhat, axis=-1, keepdims=True)
    dr = rstd * (dxhat - m1 - xhat * m2)
    return dr, jnp.sum(dy * xhat, axis=0, keepdims=True), jnp.sum(dy, axis=0, keepdims=True)


N_SEG = 3 + len(_LEVELS)


def _level_stack():
    t = np.arange(BLOCK)[:, None]
    r = np.arange(BLOCK)[None, :]
    mats = [r <= t, r > t, np.ones((BLOCK, BLOCK), bool)]
    for h in _LEVELS:
        same = (t // (2 * h)) == (r // (2 * h))
        up_t, up_r = (t % (2 * h)) >= h, (r % (2 * h)) >= h
        mats.append(same & ((up_t & up_r & (r <= t)) | (~up_t & ~up_r & (r > t))))
    return jnp.asarray(np.concatenate(mats, axis=0).astype(np.float32), dtype=BF16)


def _hgrn_gates(hf, a0, a1, valid):
    lb = jax.nn.sigmoid(a0 - a1)
    fg = lb + (1.0 - lb) * jax.nn.sigmoid(hf)
    return jnp.where(valid, jnp.log(fg), 0.0), jnp.where(valid, 1.0 - fg, 0.0)


def _hgrn_scores(hq, k, *levels):
    q = jax.nn.silu(hq)
    rows = lax.broadcasted_iota(jnp.int32, (BLOCK, BLOCK), 0)
    cols = lax.broadcasted_iota(jnp.int32, (BLOCK, BLOCK), 1)
    a = jnp.where(rows == cols, jnp.sum(q * k, axis=-1, keepdims=True), 0.0)
    differ = jnp.bitwise_xor(rows, cols)
    for h, lvl in zip(_LEVELS, levels):
        decay = jnp.exp(lvl)
        pair = (cols < rows) & (differ >= h) & (differ < 2 * h)
        a = a + jnp.where(pair, _mm(q * decay, k * decay, "nt"), 0.0)
    return a


def _hgrn_mix(hq, k, v, st_in, a, seg_incl, seg_after, seg_total):
    o = _mm(jax.nn.silu(hq) * jnp.exp(seg_incl), st_in, "nt") + _mm(a, v, "nn")
    return o, st_in * jnp.exp(seg_total) + _mm(v, k * jnp.exp(seg_after), "tn")


def _hgrn_norm(o, hg, ng):
    return o * lax.rsqrt(jnp.mean(o * o, axis=-1, keepdims=True) + EPS) * ng * jax.nn.silu(hg)


def _seg_blocks(e, h):
    return [e[i * BLOCK:(i + 1) * BLOCK, h * BLOCK:(h + 1) * BLOCK] for i in range(N_SEG)]


def _rope(x, cos, sin, first_half):
    partner = jnp.where(first_half, -pltpu.roll(x, 96, 1), pltpu.roll(x, 32, 1))
    return x * cos + partner * sin


def _rope_t(g, cos, sin, first_half):
    u = g * sin
    partner = jnp.where(first_half, pltpu.roll(u, 96, 1), -pltpu.roll(u, 32, 1))
    return g * cos + partner


def _low_half(x):
    return lax.broadcasted_iota(jnp.int32, x.shape, 1) < HEAD_DIM


def _both_halves(x, g):
    sw = _swap_halves(x)
    return jnp.where(_low_half(x), x, sw) if g == 0 else jnp.where(_low_half(x), sw, x)


def _att_scores(qa, qb, kc, kp, km, g, own4, band4, meta4):
    low = _low_half(qa)
    q4 = jnp.concatenate([jnp.where(low, qa, 0.0), jnp.where(low, 0.0, qa),
                          jnp.where(low, qb, 0.0), jnp.where(low, 0.0, qb)], axis=0)
    scale = HEAD_DIM ** -0.5
    neg = jnp.finfo(F32).min
    s = jnp.where(own4, _mm(_both_halves(kc, g), q4, "nt"), _mm(_both_halves(kp, g), q4, "nt"))
    return (jnp.where(band4, s * scale, neg), jnp.where(meta4, _mm(_both_halves(km, g), q4, "nt") * scale, neg))


def _att_probs(s, sm, sinkrow):
    mx = jnp.maximum(jnp.maximum(jnp.max(s, axis=0, keepdims=True), jnp.max(sm, axis=0, keepdims=True)), sinkrow)
    p, pm, ps = jnp.exp(s - mx), jnp.exp(sm - mx), jnp.exp(sinkrow - mx)
    inv = 1.0 / (jnp.sum(p, axis=0, keepdims=True) + jnp.sum(pm, axis=0, keepdims=True) + ps)
    return p * inv, pm * inv, ps * inv


def _att_probs_bwd(p, pm, ps, dp, dpm):
    r = jnp.sum(p * dp, axis=0, keepdims=True) + jnp.sum(pm * dpm, axis=0, keepdims=True)
    return p * (dp - r), pm * (dpm - r), -ps * r


def _att_values(p, pm, vc, vp, vm, g, own4):
    o4 = (_mm(jnp.where(own4, p, 0.0), _both_halves(vc, g), "tn") + _mm(jnp.where(own4, 0.0, p), _both_halves(vp, g), "tn")
          + _mm(pm, _both_halves(vm, g), "tn"))
    tiles = []
    for j in range(2):
        upper = o4[(2 * j) * BLOCK:(2 * j + 1) * BLOCK]
        tiles.append(jnp.where(_low_half(upper), upper, o4[(2 * j + 1) * BLOCK:(2 * j + 2) * BLOCK]))
    return tiles


def _att_masks(blk_idx):
    kidx = lax.broadcasted_iota(jnp.int32, (BLOCK, BLOCK), 0)
    qrow = lax.broadcasted_iota(jnp.int32, (BLOCK, BLOCK), 1)
    own_side = kidx <= qrow
    pos_own = blk_idx * BLOCK + kidx - PAD
    ok_band = (own_side & (pos_own >= N_META)) | (~own_side & (pos_own - BLOCK >= N_META) & (blk_idx >= 1))
    qpos = blk_idx * BLOCK + lax.broadcasted_iota(jnp.int32, (N_META, BLOCK), 1) - PAD
    ok_meta = lax.broadcasted_iota(jnp.int32, (N_META, BLOCK), 0) <= qpos
    return [jnp.concatenate([m] * 4, axis=1) for m in (own_side, ok_band, ok_meta)]


def _token_streams(tr, tile_of=lambda i: i):
    k = tr // BLOCK
    return [pl.BlockSpec((BLOCK, D_MODEL), lambda i, j=j: (jnp.maximum(k * tile_of(i) - 1 + j, 0), 0))
            for j in range(k)]


def _embed_ln(x, meta_shard, w_in_shard, g0, b0):
    p = x.shape[0] + BLOCK
    tr = _row_tile(p, 640)
    k = tr // BLOCK
    nt = p // tr
    tile_of = lambda s: (s + 1) % nt
    shards = [meta_shard, w_in_shard]
    c_in, c_out, c_shapes, c_sems = _comm_specs(shards, N_DEV)

    def body(*refs):
        g_ref, b_ref = refs[k:k + 2]
        h_ref, hb_ref, xh_ref, rs_ref = refs[k + 4:k + 8]
        out_refs = refs[k + 8:k + 10]
        lead_ref, meta_ref = refs[k + 10:k + 12]
        starts, passes, waits = _gather_behind(refs[k + 2:k + 4], out_refs, refs[k + 12:], [False, False])
        s = pl.program_id(0)
        t = tile_of(s)

        @pl.when(s == 0)
        def _():
            lead_ref[...] = jnp.zeros_like(lead_ref)
            for start in starts:
                start()

        @pl.when(s == nt - 1)
        def _():
            for step in passes + waits:
                step()
            pltpu.sync_copy(out_refs[0], meta_ref)
            for d in range(N_DEV):
                lead_ref[PAD:BLOCK, d * BLOCK:(d + 1) * BLOCK] = meta_ref[d]

        first = jnp.where(t == 0, lead_ref[...], refs[0][...])
        xhat, rstd = _ln_stats(jnp.concatenate([first] + [r[...] for r in refs[1:k]], axis=0))
        row = t * tr + lax.broadcasted_iota(jnp.int32, (tr, 1), 0)
        h = jnp.where(row >= PAD, xhat * g_ref[...] + b_ref[...], 0.0)
        h_ref[...] = h
        hb_ref[...] = h.astype(BF16)
        xh_ref[...] = xhat
        rs_ref[...] = rstd

    vec = pl.BlockSpec((1, D_MODEL), lambda s: (0, 0))
    rowsp = pl.BlockSpec((tr, D_MODEL), lambda s: (tile_of(s), 0))
    return pl.pallas_call(
        body, name="embed_ln", grid=(nt,),
        in_specs=_token_streams(tr, tile_of) + [vec, vec] + c_in,
        out_specs=[rowsp, rowsp, rowsp, pl.BlockSpec((tr, 1), lambda s: (tile_of(s), 0))] + c_out,
        out_shape=[jax.ShapeDtypeStruct((p, D_MODEL), F32), jax.ShapeDtypeStruct((p, D_MODEL), BF16),
                   jax.ShapeDtypeStruct((p, D_MODEL), F32), jax.ShapeDtypeStruct((p, 1), F32)] + c_shapes,
        scratch_shapes=[pltpu.VMEM((BLOCK, D_MODEL), F32), pltpu.VMEM((N_DEV, N_META, BLOCK), F32)] + c_sems,
        compiler_params=_cparams(("arbitrary",)),
    )(*([x] * k), g0, b0, *shards)


def _rope_tables(p):
    pos = (np.arange(p, dtype=np.int32) - PAD).astype(np.float32)
    half = HEAD_DIM // 2
    inv = np.float32(ROPE_THETA) ** (-np.arange(half, dtype=np.float32) / np.float32(half))
    ang = pos[:, None] * np.tile(inv.astype(np.float32), BLOCK // half)[None, :]
    return jnp.asarray(np.cos(ang), F32), jnp.asarray(np.sin(ang), F32)


def _att_sinkrows(sink_ref):
    lanehead = lax.broadcasted_iota(jnp.int32, (1, 4 * BLOCK), 1) // BLOCK
    rows = []
    for g in range(2):
        row = jnp.zeros((1, 4 * BLOCK), F32)
        for j in range(4):
            row = jnp.where(lanehead == j, sink_ref[0, 4 * g + j], row)
        rows.append(row)
    return rows


def _first_half(rows):
    return (lax.broadcasted_iota(jnp.int32, (rows, BLOCK), 1) % HEAD_DIM) < (HEAD_DIM // 2)


def _att_load(qkv_ref, cos_ref, sin_ref, with_q):
    cos, sin, fh = cos_ref[...], sin_ref[...], _first_half(BLOCK)
    qs = [_rope(qkv_ref[:, j * BLOCK:(j + 1) * BLOCK], cos, sin, fh) for j in range(4)] if with_q else None
    k = _rope(qkv_ref[:, ATT_QW:ATT_QW + ATT_KVW], cos, sin, fh)
    v = qkv_ref[:, ATT_QW + ATT_KVW:ATT_QW + 2 * ATT_KVW]
    return qs, k, v


def _att_load_meta(qkv_ref, cos_ref, sin_ref):
    k = _rope(qkv_ref[PAD:BLOCK, ATT_QW:ATT_QW + ATT_KVW], cos_ref[PAD:BLOCK, :], sin_ref[PAD:BLOCK, :],
              _first_half(N_META))
    return k, qkv_ref[PAD:BLOCK, ATT_QW + ATT_KVW:ATT_QW + 2 * ATT_KVW]


def _att_specs(blk):
    w = ATT_QW + 2 * ATT_KVW
    cur = lambda width: pl.BlockSpec((BLOCK, width), lambda i: (blk(i), 0))
    prev = lambda width: pl.BlockSpec((BLOCK, width), lambda i: (jnp.maximum(blk(i) - 1, 0), 0))
    meta = lambda width: pl.BlockSpec((BLOCK, width), lambda i: (0, 0))
    return [cur(w), prev(w), meta(w), cur(BLOCK), cur(BLOCK), prev(BLOCK), prev(BLOCK), meta(BLOCK), meta(BLOCK),
            pl.BlockSpec(memory_space=pltpu.SMEM)]


_FLIPS = [(dx, dy, dc) for dx in (0, 1) for dy in (0, 1) for dc in (0, 1)][1:]
N_PEERS = len(_FLIPS)


def _place():
    return lax.axis_index("x"), lax.axis_index("y"), lax.axis_index("c")


def _peer(place, flip):
    return tuple(1 - p if f else p for p, f in zip(place, flip))


def _slot(place, swapped):
    x, y, c = place
    return 4 * y + 2 * x + c if swapped else 4 * x + 2 * y + c


def _comm_specs(arrs, out_lead):
    n = len(arrs)
    outs = [jax.ShapeDtypeStruct((out_lead,) + a.shape[-2:], a.dtype) for a in arrs]
    sems = [pltpu.SemaphoreType.DMA((n, N_PEERS)), pltpu.SemaphoreType.DMA((n, N_PEERS)), pltpu.SemaphoreType.DMA((n,))]
    return [pl.BlockSpec(memory_space=pl.ANY)] * n, [pl.BlockSpec(memory_space=pl.ANY)] * n, outs, sems


def _gather_behind(shard_refs, out_refs, sems, swapped):
    send_sems, recv_sems, local_sems = sems
    x, y, c = _place()
    me, sibling = (x, y, c), (x, y, 1 - c)
    chips = [(1 - x, y), (x, 1 - y), (1 - x, 1 - y)]
    starts, passes, waits = [], [], []
    for w, (s, o) in enumerate(zip(shard_refs, out_refs)):
        def copy(k, block, to, from_shard=False, w=w, s=s, o=o):
            rows = o.at[_slot(block, swapped[w])]
            return pltpu.make_async_remote_copy(
                src_ref=s if from_shard else rows, dst_ref=rows, send_sem=send_sems.at[w, k],
                recv_sem=recv_sems.at[w, k], device_id=to, device_id_type=MESH)

        own = pltpu.make_async_copy(s, o.at[_slot(me, swapped[w])], local_sems.at[w])
        first = [copy(0, me, sibling, True)] + [copy(1 + j, me, (*chip, c), True) for j, chip in enumerate(chips)]
        handed = [copy(4 + j, (*chip, c), sibling) for j, chip in enumerate(chips)]
        starts += [own.start] + [cp.start for cp in first]
        for j, chip in enumerate(chips):
            passes += [copy(1 + j, (*chip, c), me).wait_recv, handed[j].start]
        waits.append(copy(0, sibling, me).wait_recv)
        waits += [copy(4 + j, (*chip, 1 - c), me).wait_recv for j, chip in enumerate(chips)]
        waits += [cp.wait_send for cp in first + handed] + [own.wait]
    return starts, passes, waits


def _scatter_behind(part_refs, recv_refs, sems, swapped):
    send_sems, recv_sems, _ = sems
    place = _place()
    starts, waits = [], []
    for w, (p, o) in enumerate(zip(part_refs, recv_refs)):
        for r, flip in enumerate(_FLIPS):
            peer = _peer(place, flip)
            cp = pltpu.make_async_remote_copy(
                src_ref=p.at[_slot(peer, swapped[w])], dst_ref=o.at[r], send_sem=send_sems.at[w, r],
                recv_sem=recv_sems.at[w, r], device_id=peer, device_id_type=MESH)
            starts.append(cp.start)
            waits += [cp.wait_recv, cp.wait_send]
    return starts, waits


def _mixers_fwd(proj_hg, proj_att, lbounds, norm_g, lv, cos, sin, sinks, shards, swapped):
    p = proj_hg.shape[0]
    nb = p // BLOCK
    n = len(shards)
    c_in, c_out, c_shapes, c_sems = _comm_specs(shards, N_DEV)
    pass_step = min(nb - 1, max(1, (5 * nb) // 8))

    def body(*refs):
        x_ref, lb_ref, ng_ref, lv_ref, cur_ref, prev_ref, meta_ref, cc, sc, cp, sp, cm, sm, sink_ref = refs[:14]
        shard_refs = refs[14:14 + n]
        y_ref, o_ref, st_ref, a_ref, raw_ref, pr_ref = refs[14 + n:20 + n]
        out_refs = refs[20 + n:20 + 2 * n]
        carry_ref = refs[20 + 2 * n]
        starts, passes, waits = _gather_behind(shard_refs, out_refs, refs[21 + 2 * n:], swapped)
        c = pl.program_id(0)

        @pl.when(c == 0)
        def _():
            carry_ref[...] = jnp.zeros_like(carry_ref)
            for start in starts:
                start()

        @pl.when(c == pass_step)
        def _():
            for step in passes:
                step()

        valid = (c * BLOCK + lax.broadcasted_iota(jnp.int32, (BLOCK, 1), 0)) >= PAD
        logf, k = _hgrn_gates(x_ref[:, HG_W:2 * HG_W], lb_ref[0:1, :], lb_ref[1:2, :], valid)
        e = _split_dot(lv_ref[...], logf, "nn")
        for h in range(HG_HEADS):
            sl = lambda part: x_ref[:, part * HG_W + h * BLOCK: part * HG_W + (h + 1) * BLOCK]
            hs = slice(h * BLOCK, (h + 1) * BLOCK)
            st_in = carry_ref[h]
            st_ref[0, h] = st_in
            seg = _seg_blocks(e, h)
            a = _hgrn_scores(sl(0), k[:, hs], *seg[3:])
            a_ref[0, h] = a.astype(BF16)
            raw, st_out = _hgrn_mix(sl(0), k[:, hs], sl(2), st_in, a, *seg[:3])
            raw_ref[:, hs] = raw
            y_ref[:, hs] = _hgrn_norm(raw, sl(3), ng_ref[...]).astype(BF16)
            carry_ref[h] = st_out

        qs, kc, vc = _att_load(cur_ref, cc, sc, True)
        _, kp, vp = _att_load(prev_ref, cp, sp, False)
        km, vm = _att_load_meta(meta_ref, cm, sm)
        sinkrows = _att_sinkrows(sink_ref)
        own4, band4, meta4 = _att_masks(c)
        for g in range(2):
            s, s_meta = _att_scores(qs[2 * g], qs[2 * g + 1], kc, kp, km, g, own4, band4, meta4)
            pr, pr_meta, pr_sink = _att_probs(s, s_meta, sinkrows[g])
            pr_ref[0, g, :BLOCK, :] = pr.astype(BF16)
            pr_ref[0, g, BLOCK:BLOCK + N_META, :] = pr_meta.astype(BF16)
            pr_ref[0, g, BLOCK + N_META:, :] = jnp.broadcast_to(pr_sink, (N_META, 4 * BLOCK)).astype(BF16)
            for j, tile in enumerate(_att_values(pr, pr_meta, vc, vp, vm, g, own4)):
                o_ref[:, (2 * g + j) * BLOCK:(2 * g + j + 1) * BLOCK] = tile.astype(BF16)

        @pl.when(c == nb - 1)
        def _():
            for wait in waits:
                wait()

    return pl.pallas_call(
        body, name="mixers_fwd", grid=(nb,),
        in_specs=[pl.BlockSpec((BLOCK, 4 * HG_W), lambda c: (c, 0)), pl.BlockSpec((2, HG_W), lambda c: (0, 0)),
                  pl.BlockSpec((1, BLOCK), lambda c: (0, 0)), pl.BlockSpec(lv.shape, lambda c: (0, 0))]
        + _att_specs(lambda c: c) + c_in,
        out_specs=[pl.BlockSpec((BLOCK, HG_W), lambda c: (c, 0)), pl.BlockSpec((BLOCK, ATT_QW), lambda c: (c, 0)),
                   pl.BlockSpec((1, HG_HEADS, BLOCK, BLOCK), lambda c: (c, 0, 0, 0)),
                   pl.BlockSpec((1, HG_HEADS, BLOCK, BLOCK), lambda c: (c, 0, 0, 0)),
                   pl.BlockSpec((BLOCK, HG_W), lambda c: (c, 0)),
                   pl.BlockSpec((1, 2, ATT_KEYS, 4 * BLOCK), lambda c: (c, 0, 0, 0))] + c_out,
        out_shape=[jax.ShapeDtypeStruct((p, HG_W), BF16), jax.ShapeDtypeStruct((p, ATT_QW), BF16),
                   jax.ShapeDtypeStruct((nb, HG_HEADS, BLOCK, BLOCK), F32),
                   jax.ShapeDtypeStruct((nb, HG_HEADS, BLOCK, BLOCK), BF16),
                   jax.ShapeDtypeStruct((p, HG_W), F32),
                   jax.ShapeDtypeStruct((nb, 2, ATT_KEYS, 4 * BLOCK), BF16)] + c_shapes,
        scratch_shapes=[pltpu.VMEM((HG_HEADS, BLOCK, BLOCK), F32)] + c_sems,
        compiler_params=_cparams(("arbitrary",)),
    )(proj_hg, lbounds, norm_g, lv, proj_att, proj_att, proj_att, cos, sin, cos, sin, cos, sin, sinks, *shards)


def _tile(rows, preferred):
    return preferred if rows % preferred == 0 else _row_tile(rows, preferred)


def _in_proj(h0b, w_in_t):
    p = h0b.shape[0]
    tm = _tile(p, 1040)
    hg_end = 4 * HG_W

    def body(h_ref, w_ref, hg_ref, att_ref):
        h = h_ref[...]
        hg_ref[...] = _dot(h, w_ref[:hg_end, :], "nt")
        att_ref[...] = _dot(h, w_ref[hg_end:, :], "nt")

    row = lambda w: pl.BlockSpec((tm, w), lambda i: (i, 0))
    return pl.pallas_call(
        body, name="in_proj", grid=(p // tm,),
        in_specs=[row(D_MODEL), pl.BlockSpec((MIX_W, D_MODEL), lambda i: (0, 0), pipeline_mode=pl.Buffered(1))],
        out_specs=[row(hg_end), row(MIX_W - hg_end)],
        out_shape=[jax.ShapeDtypeStruct((p, hg_end), F32), jax.ShapeDtypeStruct((p, MIX_W - hg_end), F32)],
        compiler_params=_cparams(("arbitrary",)),
    )(h0b, w_in_t)


def _branch_mix(yh, oa, gates, w_bh, w_ba):
    y_hg = _dot(yh, w_bh, "nn")
    y_att = _dot(oa, w_ba, "nn")
    s1 = jax.nn.sigmoid(gates[:, :D_MODEL].astype(F32))
    s2 = jax.nn.sigmoid(gates[:, D_MODEL:].astype(F32))
    return s1 * y_hg + s2 * y_att, y_hg, y_att, s1, s2


def _mix_out_ln1(yh, oa, h0b, w_in_t, h0, w_bh, w_ba, w_out, g1, b1):
    p = yh.shape[0]
    tr = _tile(p, 416)

    def body(yh_ref, oa_ref, h0b_ref, wi_ref, h0_ref, wbh_ref, wba_ref, wo_ref, g1_ref, b1_ref,
             g_ref, mix_ref, h1_ref, h1b_ref, xh_ref, rs_ref):
        g_ref[...] = _dot(h0b_ref[...], wi_ref[MIX_W:, :], "nt").astype(BF16)
        mixin = _branch_mix(yh_ref[...], oa_ref[...], g_ref[...], wbh_ref[...], wba_ref[...])[0]
        mix_ref[...] = mixin.astype(BF16)
        xhat, rstd = _ln_stats(ALPHA * h0_ref[...] + _dot(mixin, wo_ref[...], "nn"))
        h1 = xhat * g1_ref[...] + b1_ref[...]
        h1_ref[...] = h1
        h1b_ref[...] = h1.astype(BF16)
        xh_ref[...] = xhat
        rs_ref[...] = rstd

    row = lambda w: pl.BlockSpec((tr, w), lambda i: (i, 0))
    const = lambda a: pl.BlockSpec(a.shape, lambda i: (0, 0))
    return pl.pallas_call(
        body, name="mix_out_ln1", grid=(p // tr,),
        in_specs=[row(HG_W), row(ATT_QW), row(D_MODEL),
                  pl.BlockSpec(w_in_t.shape, lambda i: (0, 0), pipeline_mode=pl.Buffered(1)), row(D_MODEL),
                  const(w_bh), const(w_ba), const(w_out), const(g1), const(b1)],
        out_specs=[row(2 * D_MODEL), row(D_MODEL), row(D_MODEL), row(D_MODEL), row(D_MODEL), row(1)],
        out_shape=[jax.ShapeDtypeStruct((p, 2 * D_MODEL), BF16), jax.ShapeDtypeStruct((p, D_MODEL), BF16),
                   jax.ShapeDtypeStruct((p, D_MODEL), F32), jax.ShapeDtypeStruct((p, D_MODEL), BF16),
                   jax.ShapeDtypeStruct((p, D_MODEL), F32), jax.ShapeDtypeStruct((p, 1), F32)],
        compiler_params=_cparams(("arbitrary",)),
    )(yh, oa, h0b, w_in_t, h0, w_bh, w_ba, w_out, g1, b1)


FF_T = D_FF // 2


def _ffn_in_swiglu(h1, w_fi_t):
    p = h1.shape[0]
    tm = _tile(p, 1040)

    def body(h_ref, w_ref, au_ref, s_ref):
        au = _dot(h_ref[...], w_ref[...], "nt")
        au_ref[...] = au.astype(BF16)
        s_ref[...] = (jax.nn.silu(au[:, :FF_T]) * au[:, FF_T:]).astype(BF16)

    return pl.pallas_call(
        body, name="ffn_in_swiglu", grid=(D_FF // FF_T, p // tm),
        in_specs=[pl.BlockSpec((tm, D_MODEL), lambda j, i: (i, 0)), pl.BlockSpec((2 * FF_T, D_MODEL), lambda j, i: (j, 0))],
        out_specs=[pl.BlockSpec((tm, 2 * FF_T), lambda j, i: (i, j)), pl.BlockSpec((tm, FF_T), lambda j, i: (i, j))],
        out_shape=[jax.ShapeDtypeStruct((p, 2 * D_FF), BF16), jax.ShapeDtypeStruct((p, D_FF), BF16)],
        compiler_params=_cparams(("arbitrary", "arbitrary")),
    )(h1, w_fi_t)


def _ffn_out_loss(s, w_fo, h1, g2, b2, target):
    p = h1.shape[0]
    tr = _row_tile(p, 640)
    k = tr // BLOCK

    def body(*refs):
        s_ref, w_ref, h_ref, g_ref, b_ref = refs[:5]
        dr_ref, drb_ref, loss_ref, dg_ref, db_ref = refs[5 + k:]
        i = pl.program_id(0)
        xhat, rstd = _ln_stats(ALPHA * h_ref[...] + _dot(s_ref[...], w_ref[...], "nn"))
        y = xhat * g_ref[...] + b_ref[...]
        row = i * tr + lax.broadcasted_iota(jnp.int32, (tr, 1), 0)
        tgt = jnp.concatenate([r[...] for r in refs[5:5 + k]], axis=0)
        err = jnp.where(row >= BLOCK, y - tgt, 0.0)
        dr, dg, db = _ln_bwd(err * (1.0 / D_MODEL), xhat, rstd, g_ref[...])
        dr_ref[...] = dr
        drb_ref[...] = dr.astype(BF16)
        e2 = jnp.sum(err * err, axis=0, keepdims=True)
        part = e2[:, 0:BLOCK]
        for j in range(1, D_MODEL // BLOCK):
            part = part + e2[:, j * BLOCK:(j + 1) * BLOCK]
        part = part * (0.5 / D_MODEL)

        @pl.when(i == 0)
        def _():
            loss_ref[...] = part
            dg_ref[...] = dg
            db_ref[...] = db

        @pl.when(i > 0)
        def _():
            loss_ref[...] += part
            dg_ref[...] += dg
            db_ref[...] += db

    vec = pl.BlockSpec((1, D_MODEL), lambda i: (0, 0))
    rowsp = pl.BlockSpec((tr, D_MODEL), lambda i: (i, 0))
    return pl.pallas_call(
        body, name="ffn_out_loss", grid=(p // tr,),
        in_specs=[pl.BlockSpec((tr, D_FF), lambda i: (i, 0)), pl.BlockSpec((D_FF, D_MODEL), lambda i: (0, 0)),
                  rowsp, vec, vec] + _token_streams(tr),
        out_specs=[rowsp, rowsp, pl.BlockSpec((1, BLOCK), lambda i: (0, 0)), vec, vec],
        out_shape=[jax.ShapeDtypeStruct((p, D_MODEL), F32), jax.ShapeDtypeStruct((p, D_MODEL), BF16),
                   jax.ShapeDtypeStruct((1, BLOCK), F32), jax.ShapeDtypeStruct((1, D_MODEL), F32),
                   jax.ShapeDtypeStruct((1, D_MODEL), F32)],
        compiler_params=_cparams(("arbitrary",)),
    )(s, w_fo, h1, g2, b2, *([target] * k))


def _ffn_bwd(dr2, w_fo, au, w_fi_t):
    p = au.shape[0]
    tm = _tile(p, 416)

    def body(d_ref, wo_ref, au_ref, wi_ref, dau_ref, dh_ref):
        d = d_ref[...].astype(BF16)
        dh = ALPHA * d_ref[...]
        for j in range(D_FF // FF_T):
            a_cols = slice(2 * j * FF_T, (2 * j + 1) * FF_T)
            u_cols = slice((2 * j + 1) * FF_T, (2 * j + 2) * FF_T)
            ds = _dot(d, wo_ref[j * FF_T:(j + 1) * FF_T, :], "nt")
            _, vjp = jax.vjp(lambda a, u: jax.nn.silu(a) * u, au_ref[:, a_cols].astype(F32), au_ref[:, u_cols].astype(F32))
            da, du = vjp(ds)
            dau_ref[:, a_cols] = da.astype(BF16)
            dau_ref[:, u_cols] = du.astype(BF16)
            pair = slice(2 * j * FF_T, (2 * j + 2) * FF_T)
            dh = dh + _dot(dau_ref[:, pair], wi_ref[pair, :], "nn")
        dh_ref[...] = dh

    row = lambda w: pl.BlockSpec((tm, w), lambda i: (i, 0))
    kept = lambda a: pl.BlockSpec(a.shape, lambda i: (0, 0), pipeline_mode=pl.Buffered(1))
    return pl.pallas_call(
        body, name="ffn_bwd", grid=(p // tm,),
        in_specs=[row(D_MODEL), kept(w_fo), row(2 * D_FF), kept(w_fi_t)],
        out_specs=[row(2 * D_FF), row(D_MODEL)],
        out_shape=[jax.ShapeDtypeStruct((p, 2 * D_FF), BF16), jax.ShapeDtypeStruct((p, D_MODEL), F32)],
        compiler_params=_cparams(("arbitrary",)),
    )(dr2, w_fo, au, w_fi_t)


def _ln1_mix_bwd(dh1, xhat1, rstd1, g1, yh, oa, gates, mixin, w_bh, w_ba, w_out):
    p = yh.shape[0]
    tr = _tile(p, 320)
    nt = p // tr
    group = 2
    assert nt % group == 0, (p, tr)

    def body(dh_ref, xh_ref, rs_ref, g1_ref, yh_ref, oa_ref, g_ref, mix_ref, wbh_ref, wba_ref, wo_ref,
             dr_ref, dgt_ref, dyh_ref, doa_ref, dg_ref, db_ref, dwbh_ref, dwba_ref, dwo_ref,
             abh_ref, aba_ref, ao_ref, kept_l, kept_r):
        i = pl.program_id(0)
        dr, dg, db = _ln_bwd(dh_ref[...], xh_ref[...], rs_ref[...], g1_ref[...])
        dr_ref[...] = dr
        d = _dot(dr, wo_ref[...], "nt")
        _, y_hg, y_att, s1, s2 = _branch_mix(yh_ref[...], oa_ref[...], g_ref[...], wbh_ref[...], wba_ref[...])
        dy_hg = (d * s1).astype(BF16)
        dy_att = (d * s2).astype(BF16)
        dgt_ref[:, :D_MODEL] = (d * y_hg * s1 * (1.0 - s1)).astype(BF16)
        dgt_ref[:, D_MODEL:] = (d * y_att * s2 * (1.0 - s2)).astype(BF16)
        dyh_ref[...] = _dot(dy_hg, wbh_ref[...], "nt")
        doa_ref[...] = _dot(dy_att, wba_ref[...], "nt")

        rows = pl.ds(pl.multiple_of((i % group) * tr, tr), tr)
        kept_l[rows, :HG_W] = yh_ref[...]
        kept_l[rows, HG_W:HG_W + ATT_QW] = oa_ref[...]
        kept_l[rows, HG_W + ATT_QW:] = mix_ref[...]
        kept_r[rows, :D_MODEL] = dy_hg
        kept_r[rows, D_MODEL:2 * D_MODEL] = dy_att
        kept_r[rows, 2 * D_MODEL:] = dr.astype(BF16)

        @pl.when(i == 0)
        def _():
            dg_ref[...] = dg
            db_ref[...] = db
            for ref in (abh_ref, aba_ref, ao_ref):
                ref[...] = jnp.zeros_like(ref)

        @pl.when(i > 0)
        def _():
            dg_ref[...] += dg
            db_ref[...] += db

        @pl.when(i % group == group - 1)
        def _():
            abh_ref[...] += _dot(kept_l[:, :HG_W], kept_r[:, :D_MODEL], "tn")
            aba_ref[...] += _dot(kept_l[:, HG_W:HG_W + ATT_QW], kept_r[:, D_MODEL:2 * D_MODEL], "tn")
            ao_ref[...] += _dot(kept_l[:, HG_W + ATT_QW:], kept_r[:, 2 * D_MODEL:], "tn")

        @pl.when(i == nt - 1)
        def _():
            dwbh_ref[...] = abh_ref[...].astype(BF16)
            dwba_ref[...] = aba_ref[...].astype(BF16)
            dwo_ref[...] = ao_ref[...].astype(BF16)

    row = lambda w: pl.BlockSpec((tr, w), lambda i: (i, 0))
    const = lambda a: pl.BlockSpec(a.shape, lambda i: (0, 0), pipeline_mode=pl.Buffered(1))
    vec = pl.BlockSpec((1, D_MODEL), lambda i: (0, 0))
    weights = (w_bh, w_ba, w_out)
    return pl.pallas_call(
        body, name="ln1_mix_bwd", grid=(nt,),
        in_specs=[row(D_MODEL), row(D_MODEL), row(1), vec, row(HG_W), row(ATT_QW), row(2 * D_MODEL), row(D_MODEL)]
                 + [const(w) for w in weights],
        out_specs=[row(D_MODEL), row(2 * D_MODEL), row(HG_W), row(ATT_QW), vec, vec]
                  + [pl.BlockSpec(w.shape, lambda i: (0, 0)) for w in weights],
        out_shape=[jax.ShapeDtypeStruct((p, D_MODEL), F32), jax.ShapeDtypeStruct((p, 2 * D_MODEL), BF16),
                   jax.ShapeDtypeStruct((p, HG_W), F32), jax.ShapeDtypeStruct((p, ATT_QW), F32),
                   jax.ShapeDtypeStruct((1, D_MODEL), F32), jax.ShapeDtypeStruct((1, D_MODEL), F32)]
                  + [jax.ShapeDtypeStruct(w.shape, BF16) for w in weights],
        scratch_shapes=[pltpu.VMEM(w.shape, F32) for w in weights]
                       + [pltpu.VMEM((group * tr, HG_W + ATT_QW + D_MODEL), BF16),
                          pltpu.VMEM((group * tr, 3 * D_MODEL), BF16)],
        compiler_params=_cparams(("arbitrary",)),
    )(dh1, xhat1, rstd1, g1, yh, oa, gates, mixin, w_bh, w_ba, w_out)


MIX_W = 4 * HG_W + ATT_QW + 2 * ATT_KVW
ATT_KEYS = BLOCK + 2 * N_META


def _mixers_bwd(proj_hg, proj_att, lbounds, norm_g, lv, states, scores, raw, probs, cos, sin, sinks, dyh, doa,
                parts, swapped):
    p = proj_hg.shape[0]
    nb = p // BLOCK
    n = len(parts)
    kvw = 2 * ATT_KVW
    rev = lambda s: nb - 1 - s
    c_in, c_out, c_shapes, c_sems = _comm_specs(parts, N_PEERS)

    def body(*refs):
        (x_ref, lb_ref, ng_ref, lv_ref, st_ref, a_ref, raw_ref, pr_ref, cur_ref, prev_ref, meta_ref, cc, sc, cp, sp,
         cm, sm, sink_ref, dy_ref, do_ref) = refs[:20]
        part_refs = refs[20:20 + n]
        dx_ref, dlb_ref, dng_ref, dsink_ref = refs[20 + n:24 + n]
        recv_refs = refs[24 + n:24 + 2 * n]
        dcarry_ref, dkv_next_ref, dkv_meta_ref = refs[24 + 2 * n:27 + 2 * n]
        starts, waits = _scatter_behind(part_refs, recv_refs, refs[27 + 2 * n:], swapped)
        step = pl.program_id(0)
        c = rev(step)

        @pl.when(step == 0)
        def _():
            dcarry_ref[...] = jnp.zeros_like(dcarry_ref)
            dkv_next_ref[...] = jnp.zeros_like(dkv_next_ref)
            dkv_meta_ref[...] = jnp.zeros_like(dkv_meta_ref)
            dlb_ref[...] = jnp.zeros_like(dlb_ref)
            dng_ref[...] = jnp.zeros_like(dng_ref)
            dsink_ref[...] = jnp.zeros_like(dsink_ref)
            for start in starts:
                start()

        fh = _first_half(BLOCK)
        qs, kc, vc = _att_load(cur_ref, cc, sc, True)
        _, kp, vp = _att_load(prev_ref, cp, sp, False)
        km, vm = _att_load_meta(meta_ref, cm, sm)
        own4, band4, meta4 = _att_masks(c)
        att0 = 4 * HG_W
        dkm = dkp = dkc = dvm = dvp = dvc = 0.0
        dsinkrows = []
        for g in range(2):
            pr = pr_ref[0, g, :BLOCK, :].astype(F32)
            pr_meta = pr_ref[0, g, BLOCK:BLOCK + N_META, :].astype(F32)
            pr_sink = jnp.max(pr_ref[0, g, BLOCK + N_META:, :].astype(F32), axis=0, keepdims=True)
            _, values_vjp = jax.vjp(lambda *a, g=g: _att_values(*a, g, own4), pr, pr_meta, vc, vp, vm)
            dpr, dpr_meta, dvc_g, dvp_g, dvm_g = values_vjp(
                [do_ref[:, (2 * g + j) * BLOCK:(2 * g + j + 1) * BLOCK] for j in range(2)])
            ds, ds_meta, dsinkrow = _att_probs_bwd(pr, pr_meta, pr_sink, dpr, dpr_meta)
            _, scores_vjp = jax.vjp(lambda *a, g=g: _att_scores(*a, g, own4, band4, meta4),
                                    qs[2 * g], qs[2 * g + 1], kc, kp, km)
            dqa, dqb, dkc_g, dkp_g, dkm_g = scores_vjp((ds, ds_meta))
            for j, dq in enumerate((dqa, dqb)):
                dx_ref[:, att0 + (2 * g + j) * BLOCK:att0 + (2 * g + j + 1) * BLOCK] = _rope_t(
                    dq, cc[...], sc[...], fh).astype(BF16)
            dkm, dkp, dkc = dkm + dkm_g, dkp + dkp_g, dkc + dkc_g
            dvm, dvp, dvc = dvm + dvm_g, dvp + dvp_g, dvc + dvc_g
            dsinkrows.append(dsinkrow)
        ds0, ds1 = dsinkrows
        dkv_meta_ref[:, :BLOCK] += _rope_t(dkm, cm[PAD:BLOCK, :], sm[PAD:BLOCK, :], _first_half(N_META))
        dkv_meta_ref[:, BLOCK:] += dvm
        last = jnp.where(c == 0, 1.0, 0.0)
        to_meta_rows = lambda m: jnp.concatenate([jnp.zeros((PAD, BLOCK), F32), last * m], axis=0)
        dk = _rope_t(dkc, cc[...], sc[...], fh) + dkv_next_ref[:, :BLOCK] + to_meta_rows(dkv_meta_ref[:, :BLOCK])
        dv = dvc + dkv_next_ref[:, BLOCK:] + to_meta_rows(dkv_meta_ref[:, BLOCK:])
        dx_ref[:, att0 + ATT_QW:att0 + ATT_QW + ATT_KVW] = dk.astype(BF16)
        dx_ref[:, att0 + ATT_QW + ATT_KVW:] = dv.astype(BF16)
        dkv_next_ref[:, :BLOCK] = _rope_t(dkp, cp[...], sp[...], fh)
        dkv_next_ref[:, BLOCK:] = dvp
        sink_rows = []
        for dsg in (ds0, ds1):
            for j in range(4):
                tot = jnp.sum(dsg[:, j * BLOCK:(j + 1) * BLOCK], axis=1, keepdims=True)
                sink_rows.append(jnp.broadcast_to(tot, (1, BLOCK)))
        dsink_ref[...] += jnp.concatenate(sink_rows, axis=0)

        valid = (c * BLOCK + lax.broadcasted_iota(jnp.int32, (BLOCK, 1), 0)) >= PAD
        (logf, k), gates_vjp = jax.vjp(lambda hf, a0, a1: _hgrn_gates(hf, a0, a1, valid),
                                       x_ref[:, HG_W:2 * HG_W], lb_ref[0:1, :], lb_ref[1:2, :])
        lvv = lv_ref[...]
        e = _split_dot(lvv, logf, "nn")
        dng = jnp.zeros((1, BLOCK), F32)
        dk, dseg = [], []
        for h in range(HG_HEADS):
            sl = lambda part: x_ref[:, part * HG_W + h * BLOCK: part * HG_W + (h + 1) * BLOCK]
            hs = slice(h * BLOCK, (h + 1) * BLOCK)
            seg = _seg_blocks(e, h)
            _, norm_vjp = jax.vjp(_hgrn_norm, raw_ref[:, hs], sl(3), ng_ref[...])
            draw, dhg, dngh = norm_vjp(dy_ref[:, hs])
            _, mix_vjp = jax.vjp(_hgrn_mix, sl(0), k[:, hs], sl(2), st_ref[0, h], a_ref[0, h].astype(F32), *seg[:3])
            dhq, dkh, dhi, dst, da, *dseg_mix = mix_vjp((draw, dcarry_ref[h]))
            _, scores_vjp = jax.vjp(_hgrn_scores, sl(0), k[:, hs], *seg[3:])
            dhq2, dkh2, *dseg_lvl = scores_vjp(da)
            for part, val in ((0, dhq + dhq2), (2, dhi), (3, dhg)):
                dx_ref[:, part * HG_W + h * BLOCK: part * HG_W + (h + 1) * BLOCK] = val.astype(BF16)
            dk.append(dkh + dkh2)
            dseg.append(jnp.concatenate(dseg_mix + dseg_lvl, axis=0))
            dng = dng + dngh
            dcarry_ref[h] = dst
        dlogf = _split_dot(lvv, jnp.concatenate(dseg, axis=1), "tn")
        dhf, da0, da1 = gates_vjp((dlogf, jnp.concatenate(dk, axis=1)))
        dx_ref[:, HG_W:2 * HG_W] = dhf.astype(BF16)
        dlb_ref[0:1, :] += da0
        dlb_ref[1:2, :] += da1
        dng_ref[...] += dng

        @pl.when(step == nb - 1)
        def _():
            for wait in waits:
                wait()

    const = lambda shape: pl.BlockSpec(shape, lambda s: (0,) * len(shape))
    per_head = pl.BlockSpec((1, HG_HEADS, BLOCK, BLOCK), lambda s: (rev(s), 0, 0, 0))
    return pl.pallas_call(
        body, name="mixers_bwd", grid=(nb,),
        in_specs=[pl.BlockSpec((BLOCK, 4 * HG_W), lambda s: (rev(s), 0)), const((2, HG_W)), const((1, BLOCK)),
                  const(lv.shape), per_head, per_head, pl.BlockSpec((BLOCK, HG_W), lambda s: (rev(s), 0)),
                  pl.BlockSpec((1, 2, ATT_KEYS, 4 * BLOCK), lambda s: (rev(s), 0, 0, 0))]
        + _att_specs(rev)
        + [pl.BlockSpec((BLOCK, HG_W), lambda s: (rev(s), 0)), pl.BlockSpec((BLOCK, ATT_QW), lambda s: (rev(s), 0))]
        + c_in,
        out_specs=[pl.BlockSpec((BLOCK, MIX_W), lambda s: (rev(s), 0)), const((2, HG_W)), const((1, BLOCK)),
                   const((ATT_HEADS, BLOCK))] + c_out,
        out_shape=[jax.ShapeDtypeStruct((p, MIX_W), BF16), jax.ShapeDtypeStruct((2, HG_W), F32),
                   jax.ShapeDtypeStruct((1, BLOCK), F32), jax.ShapeDtypeStruct((ATT_HEADS, BLOCK), F32)] + c_shapes,
        scratch_shapes=[pltpu.VMEM((HG_HEADS, BLOCK, BLOCK), F32), pltpu.VMEM((BLOCK, kvw), F32),
                        pltpu.VMEM((N_META, kvw), F32)] + c_sems,
        compiler_params=_cparams(("arbitrary",)),
    )(proj_hg, lbounds, norm_g, lv, states, scores, raw, probs, proj_att, proj_att, proj_att, cos, sin, cos, sin,
      cos, sin, sinks, dyh, doa, *parts)


_HBM = pl.BlockSpec(memory_space=pltpu.HBM)
_SEM = pl.BlockSpec(memory_space=pltpu.SEMAPHORE)
_ORDERED_BY_DATA = pltpu.CompilerParams(has_side_effects=pltpu.SideEffectType.DATAFLOW_SIDE_EFFECTING)


def _exchange_copies(part_ref, land_ref, send_sems, recv_sems):
    place = _place()
    return [pltpu.make_async_remote_copy(
        src_ref=part_ref.at[_slot(_peer(place, flip), False)], dst_ref=land_ref.at[r], send_sem=send_sems.at[r],
        recv_sem=recv_sems.at[r], device_id=_peer(place, flip), device_id_type=MESH) for r, flip in enumerate(_FLIPS)]


def _exchange_start(parts, name):
    def body(part_ref, land_ref, send_sems, recv_sems, part_thru, land_thru, token):
        for cp in _exchange_copies(part_ref, land_ref, send_sems, recv_sems):
            cp.start()
        token[...] = jnp.zeros_like(token)

    land = (N_PEERS,) + parts.shape[1:]
    return pl.pallas_call(
        body, name=name,
        out_shape=(pltpu.SemaphoreType.DMA((N_PEERS,)), pltpu.SemaphoreType.DMA((N_PEERS,)),
                   pltpu.HBM(parts.shape, parts.dtype), pltpu.HBM(land, parts.dtype), jax.ShapeDtypeStruct((8, BLOCK), F32)),
        in_specs=(_HBM, _HBM), out_specs=(_SEM, _SEM, _HBM, _HBM, pl.BlockSpec(memory_space=pltpu.VMEM)),
        input_output_aliases={0: 2, 1: 3}, compiler_params=_ORDERED_BY_DATA,
    )(pltpu.with_memory_space_constraint(parts, pltpu.HBM),
      pltpu.with_memory_space_constraint(lax.empty(land, parts.dtype), pltpu.HBM))


def _exchange_wait(send_sems, recv_sems, part_thru, land_thru, after, name):
    def body(part_ref, land_ref, send_sems, recv_sems, after_ref, part_out, land_out):
        for cp in _exchange_copies(part_ref, land_ref, send_sems, recv_sems):
            cp.wait_send()
            cp.wait_recv()

    return pl.pallas_call(
        body, name=name,
        out_shape=(pltpu.HBM(part_thru.shape, part_thru.dtype), pltpu.HBM(land_thru.shape, land_thru.dtype)),
        in_specs=(_HBM, _HBM, _SEM, _SEM, pl.BlockSpec(memory_space=pl.ANY)), out_specs=(_HBM, _HBM),
        input_output_aliases={0: 0, 1: 1}, compiler_params=_ORDERED_BY_DATA,
    )(part_thru, land_thru, send_sems, recv_sems, after)


def _embed_bwd(dmix, dgates, w_in_t, dr1, xhat0, rstd0, g0):
    p = dmix.shape[0]
    tm = _row_tile(p, 640)
    nm = p // tm

    def body(a_ref, g_ref, w_ref, dr_ref, xh_ref, rs_ref, g0_ref, gx_ref, lead_ref, dg_ref, db_ref, buf_ref, sem):
        i = pl.program_id(0)
        first = pltpu.make_async_copy(buf_ref.at[0, pl.ds(BLOCK, tm - BLOCK)], gx_ref.at[pl.ds(0, tm - BLOCK)],
                                      sem.at[0])
        later = lambda t: pltpu.make_async_copy(buf_ref.at[t % 2], gx_ref.at[pl.ds(t * tm - BLOCK, tm)], sem.at[t % 2])

        @pl.when(i == 2)
        def _():
            first.wait()

        @pl.when(i > 2)
        def _():
            later(i - 2).wait()

        dh0 = (ALPHA * dr_ref[...] + _dot(a_ref[...], w_ref[:MIX_W, :], "nn")
               + _dot(g_ref[...], w_ref[MIX_W:, :], "nn"))
        row = i * tm + lax.broadcasted_iota(jnp.int32, (tm, 1), 0)
        dx, dg, db = _ln_bwd(jnp.where(row >= PAD, dh0, 0.0), xh_ref[...], rs_ref[...], g0_ref[...])
        buf_ref[i % 2] = dx

        @pl.when(i == 0)
        def _():
            lead_ref[...] = dx[:BLOCK]
            dg_ref[...] = dg
            db_ref[...] = db
            first.start()

        @pl.when(i > 0)
        def _():
            dg_ref[...] += dg
            db_ref[...] += db
            later(i).start()

        @pl.when(i == nm - 1)
        def _():
            for t in (nm - 2, nm - 1):
                if t >= 0:
                    (first if t == 0 else later(t)).wait()

    row = lambda w: pl.BlockSpec((tm, w), lambda i: (i, 0))
    vec = pl.BlockSpec((1, D_MODEL), lambda i: (0, 0))
    return pl.pallas_call(
        body, name="embed_bwd", grid=(nm,),
        in_specs=[row(dmix.shape[1]), row(dgates.shape[1]),
                  pl.BlockSpec(w_in_t.shape, lambda i: (0, 0), pipeline_mode=pl.Buffered(1)), row(D_MODEL), row(D_MODEL),
                  row(1), vec],
        out_specs=[pl.BlockSpec(memory_space=pl.ANY), pl.BlockSpec((BLOCK, D_MODEL), lambda i: (0, 0)), vec, vec],
        out_shape=[jax.ShapeDtypeStruct((p - BLOCK, D_MODEL), F32), jax.ShapeDtypeStruct((BLOCK, D_MODEL), F32),
                   jax.ShapeDtypeStruct((1, D_MODEL), F32), jax.ShapeDtypeStruct((1, D_MODEL), F32)],
        scratch_shapes=[pltpu.VMEM((2, tm, D_MODEL), F32), pltpu.SemaphoreType.DMA((2,))],
        compiler_params=_cparams(("arbitrary",)),
    )(dmix, dgates, w_in_t, dr1, xhat0, rstd0, g0)


_LATE = ("w_branch_hg", "w_branch_attn", "w_out", "w_ffn_in", "w_ffn_out")
_TRANSPOSED = ("w_in", "w_ffn_in")
_COLUMN_SHARDED = ("meta_tokens", "w_branch_hg", "w_branch_attn")
_SWAPPED = ("w_ffn_in",)


def _whole(name, gathered):
    _, r, c = gathered.shape
    if name in _COLUMN_SHARDED:
        return jnp.transpose(gathered, (1, 0, 2)).reshape(r, N_DEV * c)
    return gathered.reshape(N_DEV * r, c)


def _slots(name, whole):
    r, c = whole.shape
    if name in _COLUMN_SHARDED:
        return jnp.transpose(whole.reshape(r, N_DEV, c // N_DEV), (1, 0, 2))
    return whole.reshape(N_DEV, r // N_DEV, c)


def _device_step(x, target, meta_shard, ln_emb_g, ln_emb_b, w_in_shard, lbounds, norm_g, sinks, late_shards,
                 ln1_g, ln1_b, ln2_g, ln2_b):
    p = x.shape[0] + BLOCK
    lv = _level_stack()
    cos, sin = _rope_tables(p)
    swapped = [n in _SWAPPED for n in _LATE]

    h0, h0b, xhat0, rstd0, _, g_win = _embed_ln(x, meta_shard, w_in_shard, ln_emb_g, ln_emb_b)
    w_in = _whole("w_in", g_win)
    proj_hg, proj_att = _in_proj(h0b, w_in)
    yh, oa, states, scores, raw, probs, *gathered = _mixers_fwd(
        proj_hg, proj_att, lbounds, norm_g, lv, cos, sin, sinks, late_shards, swapped)
    w_bh, w_ba, w_out, w_fi, w_fo = [_whole(n, g) for n, g in zip(_LATE, gathered)]
    gates, mixin, h1, h1b, xhat1, rstd1 = _mix_out_ln1(yh, oa, h0b, w_in, h0, w_bh, w_ba, w_out, ln1_g, ln1_b)
    au, sw = _ffn_in_swiglu(h1b, w_fi)
    dr2, dr2b, loss_part, dg2, db2 = _ffn_out_loss(sw, w_fo, h1, ln2_g, ln2_b, target)

    d_wfo = _tiled_matmul_tn(sw, dr2b, tm=_row_tile(p, 1664), tk=FF_T, tn=D_MODEL, out_dtype=BF16, name="grad_w_ffn_out")
    dau, dh1 = _ffn_bwd(dr2, w_fo, au, w_fi)
    d_wfi = _weight_grad_t([dau], h1b, tk=4 * BLOCK, name="grad_w_ffn_in")
    dr1, dgates, dyh, doa, dg1, db1, d_wbh, d_wba, d_wout = _ln1_mix_bwd(
        dh1, xhat1, rstd1, ln1_g, yh, oa, gates, mixin, w_bh, w_ba, w_out)
    late_parts = [_slots(n, g) for n, g in zip(_LATE, (d_wbh, d_wba, d_wout, d_wfi, d_wfo))]
    dmix, d_lb, d_ng, d_sink, *late_recv = _mixers_bwd(
        proj_hg, proj_att, lbounds, norm_g, lv, states, scores, raw, probs, cos, sin, sinks, dyh, doa, late_parts,
        swapped)
    d_win = _weight_grad_t([dmix, dgates], h0b, tk=2 * BLOCK, name="grad_w_in")
    *win_flight, token = _exchange_start(_slots("w_in", d_win), "w_in_grads_start")
    grad_x, dlead, dg0, db0 = _embed_bwd(dmix, dgates, w_in, dr1, xhat0, rstd0, ln_emb_g + token[0:1, 0:1])

    small = dict(ln_emb_g=dg0, ln_emb_b=db0, hg_lower_bounds=d_lb, hg_norm_g=d_ng, ln1_g=dg1, ln1_b=db1, ln2_g=dg2,
                 ln2_b=db2)
    big = dict(zip(_LATE, zip(late_parts, late_recv)))
    return _pack_small(small, d_sink, dlead, loss_part), grad_x, big, win_flight


def _share_behind(tile_ref, recv_ref, sems):
    send_sems, recv_sems, _ = sems
    place = _place()
    starts, waits = [], []
    for r, flip in enumerate(_FLIPS):
        cp = pltpu.make_async_remote_copy(
            src_ref=tile_ref, dst_ref=recv_ref.at[r], send_sem=send_sems.at[0, r], recv_sem=recv_sems.at[0, r],
            device_id=_peer(place, flip), device_id_type=MESH)
        starts.append(cp.start)
        waits += [cp.wait_recv, cp.wait_send]
    return starts, waits


def _cast_shards(arrs):
    def body(*refs):
        for src, dst in zip(refs[:len(arrs)], refs[len(arrs):]):
            dst[...] = src[...].astype(BF16)

    return pl.pallas_call(body, name="cast_shards", out_shape=[jax.ShapeDtypeStruct(a.shape, BF16) for a in arrs],
                          compiler_params=pltpu.CompilerParams(vmem_limit_bytes=VMEM_LIMIT_BYTES))(*arrs)


def _shard_rows(rows):
    return rows if rows <= 512 else max(t for t in range(16, 353, 16) if rows % t == 0)


def _adamw_math(w, g, m, v):
    m = ADAM_B1 * m + (1.0 - ADAM_B1) * g
    v = ADAM_B2 * v + (1.0 - ADAM_B2) * (g * g)
    m_hat = m / (1.0 - ADAM_B1 ** ADAM_STEP)
    v_hat = v / (1.0 - ADAM_B2 ** ADAM_STEP)
    delta = -ADAM_LR * (m_hat / (jnp.sqrt(v_hat) + ADAM_EPS) + ADAM_WD * w)
    return delta, m, v


def _reduce_adamw(parts, recv, own_slot, w, m, v, name, shared=()):
    r, cdim = w.shape
    tr = _shard_rows(r)
    steps = r // tr
    c_in, c_out, c_shapes, c_sems = _comm_specs(list(shared), N_PEERS) if shared else ([], [], [], [])

    def body(idx_ref, p_ref, r_ref, w_ref, m_ref, v_ref, *rest):
        g_out, d_out, m_out, v_out = rest[len(shared):len(shared) + 4]
        if shared:
            starts, waits = _share_behind(rest[0], rest[5], rest[6:])

            @pl.when(pl.program_id(0) == 0)
            def _():
                for start in starts:
                    start()

        g = p_ref[0].astype(F32)
        for j in range(N_PEERS):
            g = g + r_ref[j].astype(F32)
        d, mn, vn = _adamw_math(w_ref[...], g, m_ref[...], v_ref[...])
        g_out[...] = g
        d_out[...] = d
        m_out[...] = mn
        v_out[...] = vn

        if shared:
            @pl.when(pl.program_id(0) == steps - 1)
            def _():
                for wait in waits:
                    wait()

    flat = pl.BlockSpec((tr, cdim), lambda i, idx_ref: (i, 0))
    return pl.pallas_call(
        body, name=name,
        grid_spec=pltpu.PrefetchScalarGridSpec(
            num_scalar_prefetch=1, grid=(steps,),
            in_specs=[pl.BlockSpec((1, tr, cdim), lambda i, idx_ref: (idx_ref[0], i, 0)),
                      pl.BlockSpec((N_PEERS, tr, cdim), lambda i, idx_ref: (0, i, 0)), flat, flat, flat] + c_in,
            out_specs=[flat] * 4 + c_out, scratch_shapes=c_sems),
        out_shape=[jax.ShapeDtypeStruct((r, cdim), F32)] * 4 + c_shapes,
        compiler_params=_cparams(("arbitrary",)),
    )(own_slot, parts, recv, w, m, v, *shared)


def _adamw_plain(w, g, m, v, name):
    def body(w_ref, g_ref, m_ref, v_ref, d_out, m_out, v_out):
        d_out[...], m_out[...], v_out[...] = _adamw_math(w_ref[...], g_ref[...], m_ref[...], v_ref[...])

    return pl.pallas_call(body, name=name, out_shape=[jax.ShapeDtypeStruct(w.shape, F32)] * 3)(w, g, m, v)


_SMALL = (("ln_emb_g", (1, D_MODEL)), ("ln_emb_b", (1, D_MODEL)), ("hg_lower_bounds", (2, HG_W)),
          ("hg_norm_g", (1, BLOCK)), ("attn_sinks", (1, ATT_HEADS)), ("ln1_g", (1, D_MODEL)), ("ln1_b", (1, D_MODEL)),
          ("ln2_g", (1, D_MODEL)), ("ln2_b", (1, D_MODEL)))
_SMALL_ROW, _LOSS_ROW = {}, 0
for _name, (_rows, _) in _SMALL:
    _SMALL_ROW[_name], _LOSS_ROW = _LOSS_ROW, _LOSS_ROW + _rows
_META_ROW = 16
SMALL_ROWS = _META_ROW + N_META
assert _LOSS_ROW < _META_ROW


def _pack_small(grads, d_sink, dlead, loss_part):
    names = [n for n, _ in _SMALL if n != "attn_sinks"]

    def body(*refs):
        ins = dict(zip(names, refs))
        sink_ref, lead_ref, loss_ref, o_ref = refs[len(names):]
        o_ref[...] = jnp.zeros_like(o_ref)
        for name, (rows, cols) in _SMALL:
            if name != "attn_sinks":
                o_ref[_SMALL_ROW[name]:_SMALL_ROW[name] + rows, :cols] = ins[name][...]
        head = lax.broadcasted_iota(jnp.int32, (ATT_HEADS, BLOCK), 0)
        lane = lax.broadcasted_iota(jnp.int32, (ATT_HEADS, BLOCK), 1)
        o_ref[_SMALL_ROW["attn_sinks"]:_SMALL_ROW["attn_sinks"] + 1, :BLOCK] = jnp.sum(
            jnp.where(head == lane, sink_ref[...], 0.0), axis=0, keepdims=True)
        o_ref[_LOSS_ROW:_LOSS_ROW + 1, :BLOCK] = loss_ref[...]
        o_ref[_META_ROW:, :] = lead_ref[PAD:BLOCK, :]

    return pl.pallas_call(body, name="pack_small", out_shape=jax.ShapeDtypeStruct((SMALL_ROWS, D_MODEL), F32))(
        *[grads[n] for n in names], d_sink, dlead, loss_part)


def _small_reduce_adamw(own_slot, mine, recv, weights, mom1, mom2):
    n = len(_SMALL)

    def body(*refs):
        idx_ref, mine_ref, recv_ref = refs[:3]
        refs = refs[2:]
        w_refs, m_refs, v_refs = refs[1:1 + n], refs[1 + n:1 + 2 * n], refs[1 + 2 * n:1 + 3 * n]
        outs = refs[1 + 3 * n:1 + 7 * n]
        meta_out, loss_out, sum_ref, by_flip = refs[1 + 7 * n:]
        by_flip[0] = mine_ref[...]
        by_flip[1:] = recv_ref[...]
        total = by_flip[idx_ref[0]]
        for s in range(1, N_DEV):
            total = total + by_flip[s ^ idx_ref[0]]
        sum_ref[...] = total
        for i, (name, (rows, cols)) in enumerate(_SMALL):
            g = sum_ref[_SMALL_ROW[name]:_SMALL_ROW[name] + rows, :cols]
            d, mn, vn = _adamw_math(w_refs[i][...], g, m_refs[i][...], v_refs[i][...])
            for out, val in zip(outs[4 * i:4 * i + 4], (g, d, mn, vn)):
                out[...] = val
        meta_out[...] = sum_ref[_META_ROW:, :]
        loss_out[...] = jnp.broadcast_to(jnp.sum(sum_ref[_LOSS_ROW:_LOSS_ROW + 1, :BLOCK]), (1, BLOCK))

    per_param = [jax.ShapeDtypeStruct(shape, F32) for _, shape in _SMALL for _ in range(4)]
    vmem = pl.BlockSpec(memory_space=pltpu.VMEM)
    res = pl.pallas_call(
        body, name="small_reduce_adamw",
        in_specs=[pl.BlockSpec(memory_space=pltpu.SMEM)] + [vmem] * (2 + 3 * n),
        out_shape=per_param + [jax.ShapeDtypeStruct((N_META, D_MODEL), F32), jax.ShapeDtypeStruct((1, BLOCK), F32)],
        scratch_shapes=[pltpu.VMEM((SMALL_ROWS, D_MODEL), F32), pltpu.VMEM((N_DEV, SMALL_ROWS, D_MODEL), F32)],
    )(own_slot, mine, recv, *[d[name] for d in (weights, mom1, mom2) for name, _ in _SMALL])
    return {name: res[4 * i:4 * i + 4] for i, (name, _) in enumerate(_SMALL)}, res[-2], res[-1]


_WEIGHTS = ("meta_tokens", "ln_emb_g", "ln_emb_b", "w_in", "hg_lower_bounds", "hg_norm_g", "attn_sinks",
            "w_branch_hg", "w_branch_attn", "w_out", "ln1_g", "ln1_b", "w_ffn_in", "w_ffn_out", "ln2_g", "ln2_b")


def kernel(x, meta_tokens, ln_emb_g, ln_emb_b, w_in, hg_lower_bounds, hg_norm_g, attn_sinks, w_branch_hg, w_branch_attn, w_out, ln1_g, ln1_b, w_ffn_in, w_ffn_out, ln2_g, ln2_b, loss_target, m_meta_tokens, m_ln_emb_g, m_ln_emb_b, m_w_in, m_hg_lower_bounds, m_hg_norm_g, m_attn_sinks, m_w_branch_hg, m_w_branch_attn, m_w_out, m_ln1_g, m_ln1_b, m_w_ffn_in, m_w_ffn_out, m_ln2_g, m_ln2_b, v_meta_tokens, v_ln_emb_g, v_ln_emb_b, v_w_in, v_hg_lower_bounds, v_hg_norm_g, v_attn_sinks, v_w_branch_hg, v_w_branch_attn, v_w_out, v_ln1_g, v_ln1_b, v_w_ffn_in, v_w_ffn_out, v_ln2_g, v_ln2_b):
    given = dict(locals())
    weights = {n: given[n] for n in _WEIGHTS}
    mom1 = {n: given["m_" + n] for n in _WEIGHTS}
    mom2 = {n: given["v_" + n] for n in _WEIGHTS}
    shard2d = lambda n, a: a.reshape(a.shape[-2:]).T if n in _TRANSPOSED else a.reshape(a.shape[-2:])

    w_in_shard, *late_shards = _cast_shards([shard2d(n, weights[n]) for n in ("w_in",) + _LATE])
    packed, grad_x, big, win_flight = _device_step(
        x[0], loss_target[0], meta_tokens, ln_emb_g.reshape(1, -1), ln_emb_b.reshape(1, -1), w_in_shard,
        hg_lower_bounds, hg_norm_g, attn_sinks, late_shards, ln1_g, ln1_b, ln2_g, ln2_b)

    place = _place()
    out = {}

    def reduce_adamw(n, parts, recv, shared=()):
        own = _slot(place, n in _SWAPPED).astype(jnp.int32).reshape(1)
        *res, = _reduce_adamw(parts, recv, own, shard2d(n, weights[n]), shard2d(n, mom1[n]), shard2d(n, mom2[n]),
                              "adamw_" + n, shared)
        out[n] = [(r.T if n in _TRANSPOSED else r).reshape(weights[n].shape) for r in res[:4]]
        return res[4:]

    for n, (parts, recv) in big.items():
        reduce_adamw(n, parts, recv)

    small_recv, = reduce_adamw("w_in", *_exchange_wait(*win_flight, after=packed, name="w_in_grads_wait"), (packed,))

    own_slot = _slot(place, False).astype(jnp.int32).reshape(1)
    as_2d = lambda d: {n: d[n].reshape(shape) for n, shape in _SMALL}
    small_out, meta_whole, loss_row = _small_reduce_adamw(own_slot, packed, small_recv, as_2d(weights), as_2d(mom1),
                                                          as_2d(mom2))
    for n, res in small_out.items():
        out[n] = [r.reshape(weights[n].shape) for r in res]
    loss = loss_row[0, 0]
    g_meta_mine = lax.dynamic_index_in_dim(meta_whole.reshape(N_META, N_DEV, D_MODEL // N_DEV), _slot(place, False),
                                           axis=1, keepdims=False)
    out["meta_tokens"] = [g_meta_mine, *_adamw_plain(meta_tokens, g_meta_mine, m_meta_tokens, v_meta_tokens,
                                                     "adamw_meta")]

    return (loss, grad_x[None], *[out[n][0] for n in _WEIGHTS], *[out[n][1] for n in _WEIGHTS],
            *[out[n][2] for n in _WEIGHTS], *[out[n][3] for n in _WEIGHTS])
```

```python
import functools

import numpy as np
import jax
import jax.numpy as jnp
from jax import lax
from jax.experimental import pallas as pl
from jax.experimental.pallas import tpu as pltpu

F32 = jnp.float32
BF16 = jnp.bfloat16

D_MODEL = 1024
N_META = 16
BLOCK = 128
PAD = BLOCK - N_META
HG_HEADS = 4
HG_W = 512
ATT_HEADS = 8
HEAD_DIM = 64
ATT_QW = 512
ATT_KVW = 128
D_FF = 2816
EPS = 1e-5
ALPHA = 2.0 ** 0.25
ROPE_THETA = 10000.0
N_DEV = 8

ADAM_LR = 0.001
ADAM_B1 = 0.9
ADAM_B2 = 0.999
ADAM_EPS = 1e-08
ADAM_WD = 0.01
ADAM_STEP = 10

VMEM_LIMIT_BYTES = 56 * 1024 * 1024
MESH = pl.DeviceIdType.MESH

_LEVELS = (64, 32, 16, 8, 4, 2, 1)


def _cparams(sem):
    return pltpu.CompilerParams(dimension_semantics=sem, vmem_limit_bytes=VMEM_LIMIT_BYTES)


def _row_tile(rows, target):
    nb = rows // BLOCK
    best = 1
    for d in range(1, nb + 1):
        if nb % d == 0 and d * BLOCK <= target:
            best = d
    return best * BLOCK


_DN = {"nn": (((1,), (0,)), ((), ())), "nt": (((1,), (1,)), ((), ())), "tn": (((0,), (0,)), ((), ()))}


def _dot(a, b, form):
    return lax.dot_general(a.astype(BF16), b.astype(BF16), _DN[form], preferred_element_type=F32)


@functools.partial(jax.custom_vjp, nondiff_argnums=(2,))
def _mm(a, b, form):
    return _dot(a, b, form)


def _mm_fwd(a, b, form):
    a, b = a.astype(BF16), b.astype(BF16)
    return _dot(a, b, form), (a, b)


def _mm_bwd(form, res, g):
    a, b = res
    if form == "nn":
        return _dot(g, b, "nt"), _dot(a, g, "tn")
    if form == "nt":
        return _dot(g, b, "nn"), _dot(g, a, "tn")
    return _dot(b, g, "nt"), _dot(a, g, "nn")


_mm.defvjp(_mm_fwd, _mm_bwd)


def _split_dot(lv, x, form):
    return lax.dot_general(lv, x.astype(BF16), _DN[form], preferred_element_type=F32)


@jax.custom_vjp
def _swap_halves(x):
    return pltpu.roll(x, 64, 1)


_swap_halves.defvjp(lambda x: (pltpu.roll(x, 64, 1), None), lambda _, g: (pltpu.roll(g, 64, 1),))


def _tiled_matmul_tn(a, b, *, tm, tk, tn, out_dtype, name):
    m, k = a.shape
    n = b.shape[1]
    assert m % tm == 0 and k % tk == 0 and n % tn == 0, (name, a.shape, b.shape, tm, tk, tn)
    nm = m // tm

    def body(a_ref, b_ref, o_ref, acc_ref):
        mi = pl.program_id(2)

        @pl.when(mi == 0)
        def _():
            acc_ref[...] = jnp.zeros_like(acc_ref)

        acc_ref[...] += _dot(a_ref[...], b_ref[...], "tn")

        @pl.when(mi == nm - 1)
        def _():
            o_ref[...] = acc_ref[...].astype(out_dtype)

    return pl.pallas_call(
        body, name=name, grid=(k // tk, n // tn, nm),
        in_specs=[pl.BlockSpec((tm, tk), lambda kk, j, i: (i, kk)), pl.BlockSpec((tm, tn), lambda kk, j, i: (i, j))],
        out_specs=pl.BlockSpec((tk, tn), lambda kk, j, i: (kk, j)),
        out_shape=jax.ShapeDtypeStruct((k, n), out_dtype),
        scratch_shapes=[pltpu.VMEM((tk, tn), F32)],
        compiler_params=_cparams(("arbitrary", "arbitrary", "arbitrary")),
    )(a, b)


def _weight_grad_t(cots, h, *, tk, name):
    p, d = h.shape
    steps = [c.shape[1] // tk for c in cots]
    assert all(c.shape == (p, n * tk) for c, n in zip(cots, steps)), (name, [c.shape for c in cots], tk)
    first = [sum(steps[:i]) for i in range(len(cots))]

    def body(*refs):
        h_ref, o_ref = refs[len(cots)], refs[len(cots) + 1]
        k = pl.program_id(0)
        for c_ref, lo, n in zip(refs, first, steps):
            @pl.when((k >= lo) & (k < lo + n))
            def _(c_ref=c_ref):
                o_ref[...] = _dot(c_ref[...], h_ref[...], "tn").astype(BF16)

    cot_spec = lambda lo, n: pl.BlockSpec((p, tk), lambda k: (0, jnp.clip(k - lo, 0, n - 1)))
    return pl.pallas_call(
        body, name=name, grid=(sum(steps),),
        in_specs=[cot_spec(lo, n) for lo, n in zip(first, steps)]
                 + [pl.BlockSpec((p, d), lambda k: (0, 0), pipeline_mode=pl.Buffered(1))],
        out_specs=pl.BlockSpec((tk, d), lambda k: (k, 0)),
        out_shape=jax.ShapeDtypeStruct((sum(steps) * tk, d), BF16),
        compiler_params=_cparams(("arbitrary",)),
    )(*cots, h)


def _ln_stats(r):
    mu = jnp.mean(r, axis=-1, keepdims=True)
    xc = r - mu
    var = jnp.mean(xc * xc, axis=-1, keepdims=True)
    rstd = lax.rsqrt(var + EPS)
    return xc * rstd, rstd


def _ln_bwd(dy, xhat, rstd, g):
    dxhat = dy * g
    m1 = jnp.mean(dxhat, axis=-1, keepdims=True)
    m2 = jnp.mean(dxhat * xhat, axis=-1, keepdims=True)
    dr = rstd * (dxhat - m1 - xhat * m2)
    return dr, jnp.sum(dy * xhat, axis=0, keepdims=True), jnp.sum(dy, axis=0, keepdims=True)


N_SEG = 3 + len(_LEVELS)


def _level_stack():
    t = np.arange(BLOCK)[:, None]
    r = np.arange(BLOCK)[None, :]
    mats = [r <= t, r > t, np.ones((BLOCK, BLOCK), bool)]
    for h in _LEVELS:
        same = (t // (2 * h)) == (r // (2 * h))
        up_t, up_r = (t % (2 * h)) >= h, (r % (2 * h)) >= h
        mats.append(same & ((up_t & up_r & (r <= t)) | (~up_t & ~up_r & (r > t))))
    return jnp.asarray(np.concatenate(mats, axis=0).astype(np.float32), dtype=BF16)


def _hgrn_gates(hf, a0, a1, valid):
    lb = jax.nn.sigmoid(a0 - a1)
    fg = lb + (1.0 - lb) * jax.nn.sigmoid(hf)
    return jnp.where(valid, jnp.log(fg), 0.0), jnp.where(valid, 1.0 - fg, 0.0)


def _hgrn_scores(hq, k, *levels):
    q = jax.nn.silu(hq)
    rows = lax.broadcasted_iota(jnp.int32, (BLOCK, BLOCK), 0)
    cols = lax.broadcasted_iota(jnp.int32, (BLOCK, BLOCK), 1)
    a = jnp.where(rows == cols, jnp.sum(q * k, axis=-1, keepdims=True), 0.0)
    differ = jnp.bitwise_xor(rows, cols)
    for h, lvl in zip(_LEVELS, levels):
        decay = jnp.exp(lvl)
        pair = (cols < rows) & (differ >= h) & (differ < 2 * h)
        a = a + jnp.where(pair, _mm(q * decay, k * decay, "nt"), 0.0)
    return a


def _hgrn_mix(hq, k, v, st_in, a, seg_incl, seg_after, seg_total):
    o = _mm(jax.nn.silu(hq) * jnp.exp(seg_incl), st_in, "nt") + _mm(a, v, "nn")
    return o, st_in * jnp.exp(seg_total) + _mm(v, k * jnp.exp(seg_after), "tn")


def _hgrn_norm(o, hg, ng):
    return o * lax.rsqrt(jnp.mean(o * o, axis=-1, keepdims=True) + EPS) * ng * jax.nn.silu(hg)


def _seg_blocks(e, h):
    return [e[i * BLOCK:(i + 1) * BLOCK, h * BLOCK:(h + 1) * BLOCK] for i in range(N_SEG)]


def _rope(x, cos, sin, first_half):
    partner = jnp.where(first_half, -pltpu.roll(x, 96, 1), pltpu.roll(x, 32, 1))
    return x * cos + partner * sin


def _rope_t(g, cos, sin, first_half):
    u = g * sin
    partner = jnp.where(first_half, pltpu.roll(u, 96, 1), -pltpu.roll(u, 32, 1))
    return g * cos + partner


def _low_half(x):
    return lax.broadcasted_iota(jnp.int32, x.shape, 1) < HEAD_DIM


def _both_halves(x, g):
    sw = _swap_halves(x)
    return jnp.where(_low_half(x), x, sw) if g == 0 else jnp.where(_low_half(x), sw, x)


def _att_scores(qa, qb, kc, kp, km, g, own4, band4, meta4):
    low = _low_half(qa)
    q4 = jnp.concatenate([jnp.where(low, qa, 0.0), jnp.where(low, 0.0, qa),
                          jnp.where(low, qb, 0.0), jnp.where(low, 0.0, qb)], axis=0)
    scale = HEAD_DIM ** -0.5
    neg = jnp.finfo(F32).min
    s = jnp.where(own4, _mm(_both_halves(kc, g), q4, "nt"), _mm(_both_halves(kp, g), q4, "nt"))
    return (jnp.where(band4, s * scale, neg), jnp.where(meta4, _mm(_both_halves(km, g), q4, "nt") * scale, neg))


def _att_probs(s, sm, sinkrow):
    mx = jnp.maximum(jnp.maximum(jnp.max(s, axis=0, keepdims=True), jnp.max(sm, axis=0, keepdims=True)), sinkrow)
    p, pm, ps = jnp.exp(s - mx), jnp.exp(sm - mx), jnp.exp(sinkrow - mx)
    inv = 1.0 / (jnp.sum(p, axis=0, keepdims=True) + jnp.sum(pm, axis=0, keepdims=True) + ps)
    return p * inv, pm * inv, ps * inv


def _att_probs_bwd(p, pm, ps, dp, dpm):
    r = jnp.sum(p * dp, axis=0, keepdims=True) + jnp.sum(pm * dpm, axis=0, keepdims=True)
    return p * (dp - r), pm * (dpm - r), -ps * r


def _att_values(p, pm, vc, vp, vm, g, own4):
    o4 = (_mm(jnp.where(own4, p, 0.0), _both_halves(vc, g), "tn") + _mm(jnp.where(own4, 0.0, p), _both_halves(vp, g), "tn")
          + _mm(pm, _both_halves(vm, g), "tn"))
    tiles = []
    for j in range(2):
        upper = o4[(2 * j) * BLOCK:(2 * j + 1) * BLOCK]
        tiles.append(jnp.where(_low_half(upper), upper, o4[(2 * j + 1) * BLOCK:(2 * j + 2) * BLOCK]))
    return tiles


def _att_masks(blk_idx):
    kidx = lax.broadcasted_iota(jnp.int32, (BLOCK, BLOCK), 0)
    qrow = lax.broadcasted_iota(jnp.int32, (BLOCK, BLOCK), 1)
    own_side = kidx <= qrow
    pos_own = blk_idx * BLOCK + kidx - PAD
    ok_band = (own_side & (pos_own >= N_META)) | (~own_side & (pos_own - BLOCK >= N_META) & (blk_idx >= 1))
    qpos = blk_idx * BLOCK + lax.broadcasted_iota(jnp.int32, (N_META, BLOCK), 1) - PAD
    ok_meta = lax.broadcasted_iota(jnp.int32, (N_META, BLOCK), 0) <= qpos
    return [jnp.concatenate([m] * 4, axis=1) for m in (own_side, ok_band, ok_meta)]


def _token_streams(tr, tile_of=lambda i: i):
    k = tr // BLOCK
    return [pl.BlockSpec((BLOCK, D_MODEL), lambda i, j=j: (jnp.maximum(k * tile_of(i) - 1 + j, 0), 0))
            for j in range(k)]


def _embed_ln(x, meta_shard, w_in_shard, g0, b0):
    p = x.shape[0] + BLOCK
    tr = _row_tile(p, 640)
    k = tr // BLOCK
    nt = p // tr
    tile_of = lambda s: (s + 1) % nt
    shards = [meta_shard, w_in_shard]
    c_in, c_out, c_shapes, c_sems = _comm_specs(shards, N_DEV)

    def body(*refs):
        g_ref, b_ref = refs[k:k + 2]
        h_ref, hb_ref, xh_ref, rs_ref = refs[k + 4:k + 8]
        out_refs = refs[k + 8:k + 10]
        lead_ref, meta_ref = refs[k + 10:k + 12]
        starts, passes, waits = _gather_behind(refs[k + 2:k + 4], out_refs, refs[k + 12:], [False, False])
        s = pl.program_id(0)
        t = tile_of(s)

        @pl.when(s == 0)
        def _():
            lead_ref[...] = jnp.zeros_like(lead_ref)
            for start in starts:
                start()

        @pl.when(s == nt - 1)
        def _():
            for step in passes + waits:
                step()
            pltpu.sync_copy(out_refs[0], meta_ref)
            for d in range(N_DEV):
                lead_ref[PAD:BLOCK, d * BLOCK:(d + 1) * BLOCK] = meta_ref[d]

        first = jnp.where(t == 0, lead_ref[...], refs[0][...])
        xhat, rstd = _ln_stats(jnp.concatenate([first] + [r[...] for r in refs[1:k]], axis=0))
        row = t * tr + lax.broadcasted_iota(jnp.int32, (tr, 1), 0)
        h = jnp.where(row >= PAD, xhat * g_ref[...] + b_ref[...], 0.0)
        h_ref[...] = h
        hb_ref[...] = h.astype(BF16)
        xh_ref[...] = xhat
        rs_ref[...] = rstd

    vec = pl.BlockSpec((1, D_MODEL), lambda s: (0, 0))
    rowsp = pl.BlockSpec((tr, D_MODEL), lambda s: (tile_of(s), 0))
    return pl.pallas_call(
        body, name="embed_ln", grid=(nt,),
        in_specs=_token_streams(tr, tile_of) + [vec, vec] + c_in,
        out_specs=[rowsp, rowsp, rowsp, pl.BlockSpec((tr, 1), lambda s: (tile_of(s), 0))] + c_out,
        out_shape=[jax.ShapeDtypeStruct((p, D_MODEL), F32), jax.ShapeDtypeStruct((p, D_MODEL), BF16),
                   jax.ShapeDtypeStruct((p, D_MODEL), F32), jax.ShapeDtypeStruct((p, 1), F32)] + c_shapes,
        scratch_shapes=[pltpu.VMEM((BLOCK, D_MODEL), F32), pltpu.VMEM((N_DEV, N_META, BLOCK), F32)] + c_sems,
        compiler_params=_cparams(("arbitrary",)),
    )(*([x] * k), g0, b0, *shards)


def _rope_tables(p):
    pos = (np.arange(p, dtype=np.int32) - PAD).astype(np.float32)
    half = HEAD_DIM // 2
    inv = np.float32(ROPE_THETA) ** (-np.arange(half, dtype=np.float32) / np.float32(half))
    ang = pos[:, None] * np.tile(inv.astype(np.float32), BLOCK // half)[None, :]
    return jnp.asarray(np.cos(ang), F32), jnp.asarray(np.sin(ang), F32)


def _att_sinkrows(sink_ref):
    lanehead = lax.broadcasted_iota(jnp.int32, (1, 4 * BLOCK), 1) // BLOCK
    rows = []
    for g in range(2):
        row = jnp.zeros((1, 4 * BLOCK), F32)
        for j in range(4):
            row = jnp.where(lanehead == j, sink_ref[0, 4 * g + j], row)
        rows.append(row)
    return rows


def _first_half(rows):
    return (lax.broadcasted_iota(jnp.int32, (rows, BLOCK), 1) % HEAD_DIM) < (HEAD_DIM // 2)


def _att_load(qkv_ref, cos_ref, sin_ref, with_q):
    cos, sin, fh = cos_ref[...], sin_ref[...], _first_half(BLOCK)
    qs = [_rope(qkv_ref[:, j * BLOCK:(j + 1) * BLOCK], cos, sin, fh) for j in range(4)] if with_q else None
    k = _rope(qkv_ref[:, ATT_QW:ATT_QW + ATT_KVW], cos, sin, fh)
    v = qkv_ref[:, ATT_QW + ATT_KVW:ATT_QW + 2 * ATT_KVW]
    return qs, k, v


def _att_load_meta(qkv_ref, cos_ref, sin_ref):
    k = _rope(qkv_ref[PAD:BLOCK, ATT_QW:ATT_QW + ATT_KVW], cos_ref[PAD:BLOCK, :], sin_ref[PAD:BLOCK, :],
              _first_half(N_META))
    return k, qkv_ref[PAD:BLOCK, ATT_QW + ATT_KVW:ATT_QW + 2 * ATT_KVW]


def _att_specs(blk):
    w = ATT_QW + 2 * ATT_KVW
    cur = lambda width: pl.BlockSpec((BLOCK, width), lambda i: (blk(i), 0))
    prev = lambda width: pl.BlockSpec((BLOCK, width), lambda i: (jnp.maximum(blk(i) - 1, 0), 0))
    meta = lambda width: pl.BlockSpec((BLOCK, width), lambda i: (0, 0))
    return [cur(w), prev(w), meta(w), cur(BLOCK), cur(BLOCK), prev(BLOCK), prev(BLOCK), meta(BLOCK), meta(BLOCK),
            pl.BlockSpec(memory_space=pltpu.SMEM)]


_FLIPS = [(dx, dy, dc) for dx in (0, 1) for dy in (0, 1) for dc in (0, 1)][1:]
N_PEERS = len(_FLIPS)


def _place():
    return lax.axis_index("x"), lax.axis_index("y"), lax.axis_index("c")


def _peer(place, flip):
    return tuple(1 - p if f else p for p, f in zip(place, flip))


def _slot(place, swapped):
    x, y, c = place
    return 4 * y + 2 * x + c if swapped else 4 * x + 2 * y + c


def _comm_specs(arrs, out_lead):
    n = len(arrs)
    outs = [jax.ShapeDtypeStruct((out_lead,) + a.shape[-2:], a.dtype) for a in arrs]
    sems = [pltpu.SemaphoreType.DMA((n, N_PEERS)), pltpu.SemaphoreType.DMA((n, N_PEERS)), pltpu.SemaphoreType.DMA((n,))]
    return [pl.BlockSpec(memory_space=pl.ANY)] * n, [pl.BlockSpec(memory_space=pl.ANY)] * n, outs, sems


def _gather_behind(shard_refs, out_refs, sems, swapped):
    send_sems, recv_sems, local_sems = sems
    x, y, c = _place()
    me, sibling = (x, y, c), (x, y, 1 - c)
    chips = [(1 - x, y), (x, 1 - y), (1 - x, 1 - y)]
    starts, passes, waits = [], [], []
    for w, (s, o) in enumerate(zip(shard_refs, out_refs)):
        def copy(k, block, to, from_shard=False, w=w, s=s, o=o):
            rows = o.at[_slot(block, swapped[w])]
            return pltpu.make_async_remote_copy(
                src_ref=s if from_shard else rows, dst_ref=rows, send_sem=send_sems.at[w, k],
                recv_sem=recv_sems.at[w, k], device_id=to, device_id_type=MESH)

        own = pltpu.make_async_copy(s, o.at[_slot(me, swapped[w])], local_sems.at[w])
        first = [copy(0, me, sibling, True)] + [copy(1 + j, me, (*chip, c), True) for j, chip in enumerate(chips)]
        handed = [copy(4 + j, (*chip, c), sibling) for j, chip in enumerate(chips)]
        starts += [own.start] + [cp.start for cp in first]
        for j, chip in enumerate(chips):
            passes += [copy(1 + j, (*chip, c), me).wait_recv, handed[j].start]
        waits.append(copy(0, sibling, me).wait_recv)
        waits += [copy(4 + j, (*chip, 1 - c), me).wait_recv for j, chip in enumerate(chips)]
        waits += [cp.wait_send for cp in first + handed] + [own.wait]
    return starts, passes, waits


def _scatter_behind(part_refs, recv_refs, sems, swapped):
    send_sems, recv_sems, _ = sems
    place = _place()
    starts, waits = [], []
    for w, (p, o) in enumerate(zip(part_refs, recv_refs)):
        for r, flip in enumerate(_FLIPS):
            peer = _peer(place, flip)
            cp = pltpu.make_async_remote_copy(
                src_ref=p.at[_slot(peer, swapped[w])], dst_ref=o.at[r], send_sem=send_sems.at[w, r],
                recv_sem=recv_sems.at[w, r], device_id=peer, device_id_type=MESH)
            starts.append(cp.start)
            waits += [cp.wait_recv, cp.wait_send]
    return starts, waits


def _mixers_fwd(proj_hg, proj_att, lbounds, norm_g, lv, cos, sin, sinks, shards, swapped):
    p = proj_hg.shape[0]
    nb = p // BLOCK
    n = len(shards)
    c_in, c_out, c_shapes, c_sems = _comm_specs(shards, N_DEV)
    pass_step = min(nb - 1, max(1, (5 * nb) // 8))

    def body(*refs):
        x_ref, lb_ref, ng_ref, lv_ref, cur_ref, prev_ref, meta_ref, cc, sc, cp, sp, cm, sm, sink_ref = refs[:14]
        shard_refs = refs[14:14 + n]
        y_ref, o_ref, st_ref, a_ref, raw_ref, pr_ref = refs[14 + n:20 + n]
        out_refs = refs[20 + n:20 + 2 * n]
        carry_ref = refs[20 + 2 * n]
        starts, passes, waits = _gather_behind(shard_refs, out_refs, refs[21 + 2 * n:], swapped)
        c = pl.program_id(0)

        @pl.when(c == 0)
        def _():
            carry_ref[...] = jnp.zeros_like(carry_ref)
            for start in starts:
                start()

        @pl.when(c == pass_step)
        def _():
            for step in passes:
                step()

        valid = (c * BLOCK + lax.broadcasted_iota(jnp.int32, (BLOCK, 1), 0)) >= PAD
        logf, k = _hgrn_gates(x_ref[:, HG_W:2 * HG_W], lb_ref[0:1, :], lb_ref[1:2, :], valid)
        e = _split_dot(lv_ref[...], logf, "nn")
        for h in range(HG_HEADS):
            sl = lambda part: x_ref[:, part * HG_W + h * BLOCK: part * HG_W + (h + 1) * BLOCK]
            hs = slice(h * BLOCK, (h + 1) * BLOCK)
            st_in = carry_ref[h]
            st_ref[0, h] = st_in
            seg = _seg_blocks(e, h)
            a = _hgrn_scores(sl(0), k[:, hs], *seg[3:])
            a_ref[0, h] = a.astype(BF16)
            raw, st_out = _hgrn_mix(sl(0), k[:, hs], sl(2), st_in, a, *seg[:3])
            raw_ref[:, hs] = raw
            y_ref[:, hs] = _hgrn_norm(raw, sl(3), ng_ref[...]).astype(BF16)
            carry_ref[h] = st_out

        qs, kc, vc = _att_load(cur_ref, cc, sc, True)
        _, kp, vp = _att_load(prev_ref, cp, sp, False)
        km, vm = _att_load_meta(meta_ref, cm, sm)
        sinkrows = _att_sinkrows(sink_ref)
        own4, band4, meta4 = _att_masks(c)
        for g in range(2):
            s, s_meta = _att_scores(qs[2 * g], qs[2 * g + 1], kc, kp, km, g, own4, band4, meta4)
            pr, pr_meta, pr_sink = _att_probs(s, s_meta, sinkrows[g])
            pr_ref[0, g, :BLOCK, :] = pr.astype(BF16)
            pr_ref[0, g, BLOCK:BLOCK + N_META, :] = pr_meta.astype(BF16)
            pr_ref[0, g, BLOCK + N_META:, :] = jnp.broadcast_to(pr_sink, (N_META, 4 * BLOCK)).astype(BF16)
            for j, tile in enumerate(_att_values(pr, pr_meta, vc, vp, vm, g, own4)):
                o_ref[:, (2 * g + j) * BLOCK:(2 * g + j + 1) * BLOCK] = tile.astype(BF16)

        @pl.when(c == nb - 1)
        def _():
            for wait in waits:
                wait()

    return pl.pallas_call(
        body, name="mixers_fwd", grid=(nb,),
        in_specs=[pl.BlockSpec((BLOCK, 4 * HG_W), lambda c: (c, 0)), pl.BlockSpec((2, HG_W), lambda c: (0, 0)),
                  pl.BlockSpec((1, BLOCK), lambda c: (0, 0)), pl.BlockSpec(lv.shape, lambda c: (0, 0))]
        + _att_specs(lambda c: c) + c_in,
        out_specs=[pl.BlockSpec((BLOCK, HG_W), lambda c: (c, 0)), pl.BlockSpec((BLOCK, ATT_QW), lambda c: (c, 0)),
                   pl.BlockSpec((1, HG_HEADS, BLOCK, BLOCK), lambda c: (c, 0, 0, 0)),
                   pl.BlockSpec((1, HG_HEADS, BLOCK, BLOCK), lambda c: (c, 0, 0, 0)),
                   pl.BlockSpec((BLOCK, HG_W), lambda c: (c, 0)),
                   pl.BlockSpec((1, 2, ATT_KEYS, 4 * BLOCK), lambda c: (c, 0, 0, 0))] + c_out,
        out_shape=[jax.ShapeDtypeStruct((p, HG_W), BF16), jax.ShapeDtypeStruct((p, ATT_QW), BF16),
                   jax.ShapeDtypeStruct((nb, HG_HEADS, BLOCK, BLOCK), F32),
                   jax.ShapeDtypeStruct((nb, HG_HEADS, BLOCK, BLOCK), BF16),
                   jax.ShapeDtypeStruct((p, HG_W), F32),
                   jax.ShapeDtypeStruct((nb, 2, ATT_KEYS, 4 * BLOCK), BF16)] + c_shapes,
        scratch_shapes=[pltpu.VMEM((HG_HEADS, BLOCK, BLOCK), F32)] + c_sems,
        compiler_params=_cparams(("arbitrary",)),
    )(proj_hg, lbounds, norm_g, lv, proj_att, proj_att, proj_att, cos, sin, cos, sin, cos, sin, sinks, *shards)


def _tile(rows, preferred):
    return preferred if rows % preferred == 0 else _row_tile(rows, preferred)


def _in_proj(h0b, w_in_t):
    p = h0b.shape[0]
    tm = _tile(p, 1040)
    hg_end = 4 * HG_W

    def body(h_ref, w_ref, hg_ref, att_ref):
        h = h_ref[...]
        hg_ref[...] = _dot(h, w_ref[:hg_end, :], "nt")
        att_ref[...] = _dot(h, w_ref[hg_end:, :], "nt")

    row = lambda w: pl.BlockSpec((tm, w), lambda i: (i, 0))
    return pl.pallas_call(
        body, name="in_proj", grid=(p // tm,),
        in_specs=[row(D_MODEL), pl.BlockSpec((MIX_W, D_MODEL), lambda i: (0, 0), pipeline_mode=pl.Buffered(1))],
        out_specs=[row(hg_end), row(MIX_W - hg_end)],
        out_shape=[jax.ShapeDtypeStruct((p, hg_end), F32), jax.ShapeDtypeStruct((p, MIX_W - hg_end), F32)],
        compiler_params=_cparams(("arbitrary",)),
    )(h0b, w_in_t)


def _branch_mix(yh, oa, gates, w_bh, w_ba):
    y_hg = _dot(yh, w_bh, "nn")
    y_att = _dot(oa, w_ba, "nn")
    s1 = jax.nn.sigmoid(gates[:, :D_MODEL].astype(F32))
    s2 = jax.nn.sigmoid(gates[:, D_MODEL:].astype(F32))
    return s1 * y_hg + s2 * y_att, y_hg, y_att, s1, s2


def _mix_out_ln1(yh, oa, h0b, w_in_t, h0, w_bh, w_ba, w_out, g1, b1):
    p = yh.shape[0]
    tr = _tile(p, 416)

    def body(yh_ref, oa_ref, h0b_ref, wi_ref, h0_ref, wbh_ref, wba_ref, wo_ref, g1_ref, b1_ref,
             g_ref, mix_ref, h1_ref, h1b_ref, xh_ref, rs_ref):
        g_ref[...] = _dot(h0b_ref[...], wi_ref[MIX_W:, :], "nt").astype(BF16)
        mixin = _branch_mix(yh_ref[...], oa_ref[...], g_ref[...], wbh_ref[...], wba_ref[...])[0]
        mix_ref[...] = mixin.astype(BF16)
        xhat, rstd = _ln_stats(ALPHA * h0_ref[...] + _dot(mixin, wo_ref[...], "nn"))
        h1 = xhat * g1_ref[...] + b1_ref[...]
        h1_ref[...] = h1
        h1b_ref[...] = h1.astype(BF16)
        xh_ref[...] = xhat
        rs_ref[...] = rstd

    row = lambda w: pl.BlockSpec((tr, w), lambda i: (i, 0))
    const = lambda a: pl.BlockSpec(a.shape, lambda i: (0, 0))
    return pl.pallas_call(
        body, name="mix_out_ln1", grid=(p // tr,),
        in_specs=[row(HG_W), row(ATT_QW), row(D_MODEL),
                  pl.BlockSpec(w_in_t.shape, lambda i: (0, 0), pipeline_mode=pl.Buffered(1)), row(D_MODEL),
                  const(w_bh), const(w_ba), const(w_out), const(g1), const(b1)],
        out_specs=[row(2 * D_MODEL), row(D_MODEL), row(D_MODEL), row(D_MODEL), row(D_MODEL), row(1)],
        out_shape=[jax.ShapeDtypeStruct((p, 2 * D_MODEL), BF16), jax.ShapeDtypeStruct((p, D_MODEL), BF16),
                   jax.ShapeDtypeStruct((p, D_MODEL), F32), jax.ShapeDtypeStruct((p, D_MODEL), BF16),
                   jax.ShapeDtypeStruct((p, D_MODEL), F32), jax.ShapeDtypeStruct((p, 1), F32)],
        compiler_params=_cparams(("arbitrary",)),
    )(yh, oa, h0b, w_in_t, h0, w_bh, w_ba, w_out, g1, b1)


FF_T = D_FF // 2


def _ffn_in_swiglu(h1, w_fi_t):
    p = h1.shape[0]
    tm = _tile(p, 1040)

    def body(h_ref, w_ref, au_ref, s_ref):
        au = _dot(h_ref[...], w_ref[...], "nt")
        au_ref[...] = au.astype(BF16)
        s_ref[...] = (jax.nn.silu(au[:, :FF_T]) * au[:, FF_T:]).astype(BF16)

    return pl.pallas_call(
        body, name="ffn_in_swiglu", grid=(D_FF // FF_T, p // tm),
        in_specs=[pl.BlockSpec((tm, D_MODEL), lambda j, i: (i, 0)), pl.BlockSpec((2 * FF_T, D_MODEL), lambda j, i: (j, 0))],
        out_specs=[pl.BlockSpec((tm, 2 * FF_T), lambda j, i: (i, j)), pl.BlockSpec((tm, FF_T), lambda j, i: (i, j))],
        out_shape=[jax.ShapeDtypeStruct((p, 2 * D_FF), BF16), jax.ShapeDtypeStruct((p, D_FF), BF16)],
        compiler_params=_cparams(("arbitrary", "arbitrary")),
    )(h1, w_fi_t)


def _ffn_out_loss(s, w_fo, h1, g2, b2, target):
    p = h1.shape[0]
    tr = _row_tile(p, 640)
    k = tr // BLOCK

    def body(*refs):
        s_ref, w_ref, h_ref, g_ref, b_ref = refs[:5]
        dr_ref, drb_ref, loss_ref, dg_ref, db_ref = refs[5 + k:]
        i = pl.program_id(0)
        xhat, rstd = _ln_stats(ALPHA * h_ref[...] + _dot(s_ref[...], w_ref[...], "nn"))
        y = xhat * g_ref[...] + b_ref[...]
        row = i * tr + lax.broadcasted_iota(jnp.int32, (tr, 1), 0)
        tgt = jnp.concatenate([r[...] for r in refs[5:5 + k]], axis=0)
        err = jnp.where(row >= BLOCK, y - tgt, 0.0)
        dr, dg, db = _ln_bwd(err * (1.0 / D_MODEL), xhat, rstd, g_ref[...])
        dr_ref[...] = dr
        drb_ref[...] = dr.astype(BF16)
        e2 = jnp.sum(err * err, axis=0, keepdims=True)
        part = e2[:, 0:BLOCK]
        for j in range(1, D_MODEL // BLOCK):
            part = part + e2[:, j * BLOCK:(j + 1) * BLOCK]
        part = part * (0.5 / D_MODEL)

        @pl.when(i == 0)
        def _():
            loss_ref[...] = part
            dg_ref[...] = dg
            db_ref[...] = db

        @pl.when(i > 0)
        def _():
            loss_ref[...] += part
            dg_ref[...] += dg
            db_ref[...] += db

    vec = pl.BlockSpec((1, D_MODEL), lambda i: (0, 0))
    rowsp = pl.BlockSpec((tr, D_MODEL), lambda i: (i, 0))
    return pl.pallas_call(
        body, name="ffn_out_loss", grid=(p // tr,),
        in_specs=[pl.BlockSpec((tr, D_FF), lambda i: (i, 0)), pl.BlockSpec((D_FF, D_MODEL), lambda i: (0, 0)),
                  rowsp, vec, vec] + _token_streams(tr),
        out_specs=[rowsp, rowsp, pl.BlockSpec((1, BLOCK), lambda i: (0, 0)), vec, vec],
        out_shape=[jax.ShapeDtypeStruct((p, D_MODEL), F32), jax.ShapeDtypeStruct((p, D_MODEL), BF16),
                   jax.ShapeDtypeStruct((1, BLOCK), F32), jax.ShapeDtypeStruct((1, D_MODEL), F32),
                   jax.ShapeDtypeStruct((1, D_MODEL), F32)],
        compiler_params=_cparams(("arbitrary",)),
    )(s, w_fo, h1, g2, b2, *([target] * k))


def _ffn_bwd(dr2, w_fo, au, w_fi_t):
    p = au.shape[0]
    tm = _tile(p, 416)

    def body(d_ref, wo_ref, au_ref, wi_ref, dau_ref, dh_ref):
        d = d_ref[...].astype(BF16)
        dh = ALPHA * d_ref[...]
        for j in range(D_FF // FF_T):
            a_cols = slice(2 * j * FF_T, (2 * j + 1) * FF_T)
            u_cols = slice((2 * j + 1) * FF_T, (2 * j + 2) * FF_T)
            ds = _dot(d, wo_ref[j * FF_T:(j + 1) * FF_T, :], "nt")
            _, vjp = jax.vjp(lambda a, u: jax.nn.silu(a) * u, au_ref[:, a_cols].astype(F32), au_ref[:, u_cols].astype(F32))
            da, du = vjp(ds)
            dau_ref[:, a_cols] = da.astype(BF16)
            dau_ref[:, u_cols] = du.astype(BF16)
            pair = slice(2 * j * FF_T, (2 * j + 2) * FF_T)
            dh = dh + _dot(dau_ref[:, pair], wi_ref[pair, :], "nn")
        dh_ref[...] = dh

    row = lambda w: pl.BlockSpec((tm, w), lambda i: (i, 0))
    kept = lambda a: pl.BlockSpec(a.shape, lambda i: (0, 0), pipeline_mode=pl.Buffered(1))
    return pl.pallas_call(
        body, name="ffn_bwd", grid=(p // tm,),
        in_specs=[row(D_MODEL), kept(w_fo), row(2 * D_FF), kept(w_fi_t)],
        out_specs=[row(2 * D_FF), row(D_MODEL)],
        out_shape=[jax.ShapeDtypeStruct((p, 2 * D_FF), BF16), jax.ShapeDtypeStruct((p, D_MODEL), F32)],
        compiler_params=_cparams(("arbitrary",)),
    )(dr2, w_fo, au, w_fi_t)


def _ln1_mix_bwd(dh1, xhat1, rstd1, g1, yh, oa, gates, mixin, w_bh, w_ba, w_out):
    p = yh.shape[0]
    tr = _tile(p, 320)
    nt = p // tr
    group = 2
    assert nt % group == 0, (p, tr)

    def body(dh_ref, xh_ref, rs_ref, g1_ref, yh_ref, oa_ref, g_ref, mix_ref, wbh_ref, wba_ref, wo_ref,
             dr_ref, dgt_ref, dyh_ref, doa_ref, dg_ref, db_ref, dwbh_ref, dwba_ref, dwo_ref,
             abh_ref, aba_ref, ao_ref, kept_l, kept_r):
        i = pl.program_id(0)
        dr, dg, db = _ln_bwd(dh_ref[...], xh_ref[...], rs_ref[...], g1_ref[...])
        dr_ref[...] = dr
        d = _dot(dr, wo_ref[...], "nt")
        _, y_hg, y_att, s1, s2 = _branch_mix(yh_ref[...], oa_ref[...], g_ref[...], wbh_ref[...], wba_ref[...])
        dy_hg = (d * s1).astype(BF16)
        dy_att = (d * s2).astype(BF16)
        dgt_ref[:, :D_MODEL] = (d * y_hg * s1 * (1.0 - s1)).astype(BF16)
        dgt_ref[:, D_MODEL:] = (d * y_att * s2 * (1.0 - s2)).astype(BF16)
        dyh_ref[...] = _dot(dy_hg, wbh_ref[...], "nt")
        doa_ref[...] = _dot(dy_att, wba_ref[...], "nt")

        rows = pl.ds(pl.multiple_of((i % group) * tr, tr), tr)
        kept_l[rows, :HG_W] = yh_ref[...]
        kept_l[rows, HG_W:HG_W + ATT_QW] = oa_ref[...]
        kept_l[rows, HG_W + ATT_QW:] = mix_ref[...]
        kept_r[rows, :D_MODEL] = dy_hg
        kept_r[rows, D_MODEL:2 * D_MODEL] = dy_att
        kept_r[rows, 2 * D_MODEL:] = dr.astype(BF16)

        @pl.when(i == 0)
        def _():
            dg_ref[...] = dg
            db_ref[...] = db
            for ref in (abh_ref, aba_ref, ao_ref):
                ref[...] = jnp.zeros_like(ref)

        @pl.when(i > 0)
        def _():
            dg_ref[...] += dg
            db_ref[...] += db

        @pl.when(i % group == group - 1)
        def _():
            abh_ref[...] += _dot(kept_l[:, :HG_W], kept_r[:, :D_MODEL], "tn")
            aba_ref[...] += _dot(kept_l[:, HG_W:HG_W + ATT_QW], kept_r[:, D_MODEL:2 * D_MODEL], "tn")
            ao_ref[...] += _dot(kept_l[:, HG_W + ATT_QW:], kept_r[:, 2 * D_MODEL:], "tn")

        @pl.when(i == nt - 1)
        def _():
            dwbh_ref[...] = abh_ref[...].astype(BF16)
            dwba_ref[...] = aba_ref[...].astype(BF16)
            dwo_ref[...] = ao_ref[...].astype(BF16)

    row = lambda w: pl.BlockSpec((tr, w), lambda i: (i, 0))
    const = lambda a: pl.BlockSpec(a.shape, lambda i: (0, 0), pipeline_mode=pl.Buffered(1))
    vec = pl.BlockSpec((1, D_MODEL), lambda i: (0, 0))
    weights = (w_bh, w_ba, w_out)
    return pl.pallas_call(
        body, name="ln1_mix_bwd", grid=(nt,),
        in_specs=[row(D_MODEL), row(D_MODEL), row(1), vec, row(HG_W), row(ATT_QW), row(2 * D_MODEL), row(D_MODEL)]
                 + [const(w) for w in weights],
        out_specs=[row(D_MODEL), row(2 * D_MODEL), row(HG_W), row(ATT_QW), vec, vec]
                  + [pl.BlockSpec(w.shape, lambda i: (0, 0)) for w in weights],
        out_shape=[jax.ShapeDtypeStruct((p, D_MODEL), F32), jax.ShapeDtypeStruct((p, 2 * D_MODEL), BF16),
                   jax.ShapeDtypeStruct((p, HG_W), F32), jax.ShapeDtypeStruct((p, ATT_QW), F32),
                   jax.ShapeDtypeStruct((1, D_MODEL), F32), jax.ShapeDtypeStruct((1, D_MODEL), F32)]
                  + [jax.ShapeDtypeStruct(w.shape, BF16) for w in weights],
        scratch_shapes=[pltpu.VMEM(w.shape, F32) for w in weights]
                       + [pltpu.VMEM((group * tr, HG_W + ATT_QW + D_MODEL), BF16),
                          pltpu.VMEM((group * tr, 3 * D_MODEL), BF16)],
        compiler_params=_cparams(("arbitrary",)),
    )(dh1, xhat1, rstd1, g1, yh, oa, gates, mixin, w_bh, w_ba, w_out)


MIX_W = 4 * HG_W + ATT_QW + 2 * ATT_KVW
ATT_KEYS = BLOCK + 2 * N_META


def _mixers_bwd(proj_hg, proj_att, lbounds, norm_g, lv, states, scores, raw, probs, cos, sin, sinks, dyh, doa,
                parts, swapped):
    p = proj_hg.shape[0]
    nb = p // BLOCK
    n = len(parts)
    kvw = 2 * ATT_KVW
    rev = lambda s: nb - 1 - s
    c_in, c_out, c_shapes, c_sems = _comm_specs(parts, N_PEERS)

    def body(*refs):
        (x_ref, lb_ref, ng_ref, lv_ref, st_ref, a_ref, raw_ref, pr_ref, cur_ref, prev_ref, meta_ref, cc, sc, cp, sp,
         cm, sm, sink_ref, dy_ref, do_ref) = refs[:20]
        part_refs = refs[20:20 + n]
        dx_ref, dlb_ref, dng_ref, dsink_ref = refs[20 + n:24 + n]
        recv_refs = refs[24 + n:24 + 2 * n]
        dcarry_ref, dkv_next_ref, dkv_meta_ref = refs[24 + 2 * n:27 + 2 * n]
        starts, waits = _scatter_behind(part_refs, recv_refs, refs[27 + 2 * n:], swapped)
        step = pl.program_id(0)
        c = rev(step)

        @pl.when(step == 0)
        def _():
            dcarry_ref[...] = jnp.zeros_like(dcarry_ref)
            dkv_next_ref[...] = jnp.zeros_like(dkv_next_ref)
            dkv_meta_ref[...] = jnp.zeros_like(dkv_meta_ref)
            dlb_ref[...] = jnp.zeros_like(dlb_ref)
            dng_ref[...] = jnp.zeros_like(dng_ref)
            dsink_ref[...] = jnp.zeros_like(dsink_ref)
            for start in starts:
                start()

        fh = _first_half(BLOCK)
        qs, kc, vc = _att_load(cur_ref, cc, sc, True)
        _, kp, vp = _att_load(prev_ref, cp, sp, False)
        km, vm = _att_load_meta(meta_ref, cm, sm)
        own4, band4, meta4 = _att_masks(c)
        att0 = 4 * HG_W
        dkm = dkp = dkc = dvm = dvp = dvc = 0.0
        dsinkrows = []
        for g in range(2):
            pr = pr_ref[0, g, :BLOCK, :].astype(F32)
            pr_meta = pr_ref[0, g, BLOCK:BLOCK + N_META, :].astype(F32)
            pr_sink = jnp.max(pr_ref[0, g, BLOCK + N_META:, :].astype(F32), axis=0, keepdims=True)
            _, values_vjp = jax.vjp(lambda *a, g=g: _att_values(*a, g, own4), pr, pr_meta, vc, vp, vm)
            dpr, dpr_meta, dvc_g, dvp_g, dvm_g = values_vjp(
                [do_ref[:, (2 * g + j) * BLOCK:(2 * g + j + 1) * BLOCK] for j in range(2)])
            ds, ds_meta, dsinkrow = _att_probs_bwd(pr, pr_meta, pr_sink, dpr, dpr_meta)
            _, scores_vjp = jax.vjp(lambda *a, g=g: _att_scores(*a, g, own4, band4, meta4),
                                    qs[2 * g], qs[2 * g + 1], kc, kp, km)
            dqa, dqb, dkc_g, dkp_g, dkm_g = scores_vjp((ds, ds_meta))
            for j, dq in enumerate((dqa, dqb)):
                dx_ref[:, att0 + (2 * g + j) * BLOCK:att0 + (2 * g + j + 1) * BLOCK] = _rope_t(
                    dq, cc[...], sc[...], fh).astype(BF16)
            dkm, dkp, dkc = dkm + dkm_g, dkp + dkp_g, dkc + dkc_g
            dvm, dvp, dvc = dvm + dvm_g, dvp + dvp_g, dvc + dvc_g
            dsinkrows.append(dsinkrow)
        ds0, ds1 = dsinkrows
        dkv_meta_ref[:, :BLOCK] += _rope_t(dkm, cm[PAD:BLOCK, :], sm[PAD:BLOCK, :], _first_half(N_META))
        dkv_meta_ref[:, BLOCK:] += dvm
        last = jnp.where(c == 0, 1.0, 0.0)
        to_meta_rows = lambda m: jnp.concatenate([jnp.zeros((PAD, BLOCK), F32), last * m], axis=0)
        dk = _rope_t(dkc, cc[...], sc[...], fh) + dkv_next_ref[:, :BLOCK] + to_meta_rows(dkv_meta_ref[:, :BLOCK])
        dv = dvc + dkv_next_ref[:, BLOCK:] + to_meta_rows(dkv_meta_ref[:, BLOCK:])
        dx_ref[:, att0 + ATT_QW:att0 + ATT_QW + ATT_KVW] = dk.astype(BF16)
        dx_ref[:, att0 + ATT_QW + ATT_KVW:] = dv.astype(BF16)
        dkv_next_ref[:, :BLOCK] = _rope_t(dkp, cp[...], sp[...], fh)
        dkv_next_ref[:, BLOCK:] = dvp
        sink_rows = []
        for dsg in (ds0, ds1):
            for j in range(4):
                tot = jnp.sum(dsg[:, j * BLOCK:(j + 1) * BLOCK], axis=1, keepdims=True)
                sink_rows.append(jnp.broadcast_to(tot, (1, BLOCK)))
        dsink_ref[...] += jnp.concatenate(sink_rows, axis=0)

        valid = (c * BLOCK + lax.broadcasted_iota(jnp.int32, (BLOCK, 1), 0)) >= PAD
        (logf, k), gates_vjp = jax.vjp(lambda hf, a0, a1: _hgrn_gates(hf, a0, a1, valid),
                                       x_ref[:, HG_W:2 * HG_W], lb_ref[0:1, :], lb_ref[1:2, :])
        lvv = lv_ref[...]
        e = _split_dot(lvv, logf, "nn")
        dng = jnp.zeros((1, BLOCK), F32)
        dk, dseg = [], []
        for h in range(HG_HEADS):
            sl = lambda part: x_ref[:, part * HG_W + h * BLOCK: part * HG_W + (h + 1) * BLOCK]
            hs = slice(h * BLOCK, (h + 1) * BLOCK)
            seg = _seg_blocks(e, h)
            _, norm_vjp = jax.vjp(_hgrn_norm, raw_ref[:, hs], sl(3), ng_ref[...])
            draw, dhg, dngh = norm_vjp(dy_ref[:, hs])
            _, mix_vjp = jax.vjp(_hgrn_mix, sl(0), k[:, hs], sl(2), st_ref[0, h], a_ref[0, h].astype(F32), *seg[:3])
            dhq, dkh, dhi, dst, da, *dseg_mix = mix_vjp((draw, dcarry_ref[h]))
            _, scores_vjp = jax.vjp(_hgrn_scores, sl(0), k[:, hs], *seg[3:])
            dhq2, dkh2, *dseg_lvl = scores_vjp(da)
            for part, val in ((0, dhq + dhq2), (2, dhi), (3, dhg)):
                dx_ref[:, part * HG_W + h * BLOCK: part * HG_W + (h + 1) * BLOCK] = val.astype(BF16)
            dk.append(dkh + dkh2)
            dseg.append(jnp.concatenate(dseg_mix + dseg_lvl, axis=0))
            dng = dng + dngh
            dcarry_ref[h] = dst
        dlogf = _split_dot(lvv, jnp.concatenate(dseg, axis=1), "tn")
        dhf, da0, da1 = gates_vjp((dlogf, jnp.concatenate(dk, axis=1)))
        dx_ref[:, HG_W:2 * HG_W] = dhf.astype(BF16)
        dlb_ref[0:1, :] += da0
        dlb_ref[1:2, :] += da1
        dng_ref[...] += dng

        @pl.when(step == nb - 1)
        def _():
            for wait in waits:
                wait()

    const = lambda shape: pl.BlockSpec(shape, lambda s: (0,) * len(shape))
    per_head = pl.BlockSpec((1, HG_HEADS, BLOCK, BLOCK), lambda s: (rev(s), 0, 0, 0))
    return pl.pallas_call(
        body, name="mixers_bwd", grid=(nb,),
        in_specs=[pl.BlockSpec((BLOCK, 4 * HG_W), lambda s: (rev(s), 0)), const((2, HG_W)), const((1, BLOCK)),
                  const(lv.shape), per_head, per_head, pl.BlockSpec((BLOCK, HG_W), lambda s: (rev(s), 0)),
                  pl.BlockSpec((1, 2, ATT_KEYS, 4 * BLOCK), lambda s: (rev(s), 0, 0, 0))]
        + _att_specs(rev)
        + [pl.BlockSpec((BLOCK, HG_W), lambda s: (rev(s), 0)), pl.BlockSpec((BLOCK, ATT_QW), lambda s: (rev(s), 0))]
        + c_in,
        out_specs=[pl.BlockSpec((BLOCK, MIX_W), lambda s: (rev(s), 0)), const((2, HG_W)), const((1, BLOCK)),
                   const((ATT_HEADS, BLOCK))] + c_out,
        out_shape=[jax.ShapeDtypeStruct((p, MIX_W), BF16), jax.ShapeDtypeStruct((2, HG_W), F32),
                   jax.ShapeDtypeStruct((1, BLOCK), F32), jax.ShapeDtypeStruct((ATT_HEADS, BLOCK), F32)] + c_shapes,
        scratch_shapes=[pltpu.VMEM((HG_HEADS, BLOCK, BLOCK), F32), pltpu.VMEM((BLOCK, kvw), F32),
                        pltpu.VMEM((N_META, kvw), F32)] + c_sems,
        compiler_params=_cparams(("arbitrary",)),
    )(proj_hg, lbounds, norm_g, lv, states, scores, raw, probs, proj_att, proj_att, proj_att, cos, sin, cos, sin,
      cos, sin, sinks, dyh, doa, *parts)


_HBM = pl.BlockSpec(memory_space=pltpu.HBM)
_SEM = pl.BlockSpec(memory_space=pltpu.SEMAPHORE)
_ORDERED_BY_DATA = pltpu.CompilerParams(has_side_effects=pltpu.SideEffectType.DATAFLOW_SIDE_EFFECTING)


def _exchange_copies(part_ref, land_ref, send_sems, recv_sems):
    place = _place()
    return [pltpu.make_async_remote_copy(
        src_ref=part_ref.at[_slot(_peer(place, flip), False)], dst_ref=land_ref.at[r], send_sem=send_sems.at[r],
        recv_sem=recv_sems.at[r], device_id=_peer(place, flip), device_id_type=MESH) for r, flip in enumerate(_FLIPS)]


def _exchange_start(parts, name):
    def body(part_ref, land_ref, send_sems, recv_sems, part_thru, land_thru, token):
        for cp in _exchange_copies(part_ref, land_ref, send_sems, recv_sems):
            cp.start()
        token[...] = jnp.zeros_like(token)

    land = (N_PEERS,) + parts.shape[1:]
    return pl.pallas_call(
        body, name=name,
        out_shape=(pltpu.SemaphoreType.DMA((N_PEERS,)), pltpu.SemaphoreType.DMA((N_PEERS,)),
                   pltpu.HBM(parts.shape, parts.dtype), pltpu.HBM(land, parts.dtype), jax.ShapeDtypeStruct((8, BLOCK), F32)),
        in_specs=(_HBM, _HBM), out_specs=(_SEM, _SEM, _HBM, _HBM, pl.BlockSpec(memory_space=pltpu.VMEM)),
        input_output_aliases={0: 2, 1: 3}, compiler_params=_ORDERED_BY_DATA,
    )(pltpu.with_memory_space_constraint(parts, pltpu.HBM),
      pltpu.with_memory_space_constraint(lax.empty(land, parts.dtype), pltpu.HBM))


def _exchange_wait(send_sems, recv_sems, part_thru, land_thru, after, name):
    def body(part_ref, land_ref, send_sems, recv_sems, after_ref, part_out, land_out):
        for cp in _exchange_copies(part_ref, land_ref, send_sems, recv_sems):
            cp.wait_send()
            cp.wait_recv()

    return pl.pallas_call(
        body, name=name,
        out_shape=(pltpu.HBM(part_thru.shape, part_thru.dtype), pltpu.HBM(land_thru.shape, land_thru.dtype)),
        in_specs=(_HBM, _HBM, _SEM, _SEM, pl.BlockSpec(memory_space=pl.ANY)), out_specs=(_HBM, _HBM),
        input_output_aliases={0: 0, 1: 1}, compiler_params=_ORDERED_BY_DATA,
    )(part_thru, land_thru, send_sems, recv_sems, after)


def _embed_bwd(dmix, dgates, w_in_t, dr1, xhat0, rstd0, g0):
    p = dmix.shape[0]
    tm = _row_tile(p, 640)
    nm = p // tm

    def body(a_ref, g_ref, w_ref, dr_ref, xh_ref, rs_ref, g0_ref, gx_ref, lead_ref, dg_ref, db_ref, buf_ref, sem):
        i = pl.program_id(0)
        first = pltpu.make_async_copy(buf_ref.at[0, pl.ds(BLOCK, tm - BLOCK)], gx_ref.at[pl.ds(0, tm - BLOCK)],
                                      sem.at[0])
        later = lambda t: pltpu.make_async_copy(buf_ref.at[t % 2], gx_ref.at[pl.ds(t * tm - BLOCK, tm)], sem.at[t % 2])

        @pl.when(i == 2)
        def _():
            first.wait()

        @pl.when(i > 2)
        def _():
            later(i - 2).wait()

        dh0 = (ALPHA * dr_ref[...] + _dot(a_ref[...], w_ref[:MIX_W, :], "nn")
               + _dot(g_ref[...], w_ref[MIX_W:, :], "nn"))
        row = i * tm + lax.broadcasted_iota(jnp.int32, (tm, 1), 0)
        dx, dg, db = _ln_bwd(jnp.where(row >= PAD, dh0, 0.0), xh_ref[...], rs_ref[...], g0_ref[...])
        buf_ref[i % 2] = dx

        @pl.when(i == 0)
        def _():
            lead_ref[...] = dx[:BLOCK]
            dg_ref[...] = dg
            db_ref[...] = db
            first.start()

        @pl.when(i > 0)
        def _():
            dg_ref[...] += dg
            db_ref[...] += db
            later(i).start()

        @pl.when(i == nm - 1)
        def _():
            for t in (nm - 2, nm - 1):
                if t >= 0:
                    (first if t == 0 else later(t)).wait()

    row = lambda w: pl.BlockSpec((tm, w), lambda i: (i, 0))
    vec = pl.BlockSpec((1, D_MODEL), lambda i: (0, 0))
    return pl.pallas_call(
        body, name="embed_bwd", grid=(nm,),
        in_specs=[row(dmix.shape[1]), row(dgates.shape[1]),
                  pl.BlockSpec(w_in_t.shape, lambda i: (0, 0), pipeline_mode=pl.Buffered(1)), row(D_MODEL), row(D_MODEL),
                  row(1), vec],
        out_specs=[pl.BlockSpec(memory_space=pl.ANY), pl.BlockSpec((BLOCK, D_MODEL), lambda i: (0, 0)), vec, vec],
        out_shape=[jax.ShapeDtypeStruct((p - BLOCK, D_MODEL), F32), jax.ShapeDtypeStruct((BLOCK, D_MODEL), F32),
                   jax.ShapeDtypeStruct((1, D_MODEL), F32), jax.ShapeDtypeStruct((1, D_MODEL), F32)],
        scratch_shapes=[pltpu.VMEM((2, tm, D_MODEL), F32), pltpu.SemaphoreType.DMA((2,))],
        compiler_params=_cparams(("arbitrary",)),
    )(dmix, dgates, w_in_t, dr1, xhat0, rstd0, g0)


_LATE = ("w_branch_hg", "w_branch_attn", "w_out", "w_ffn_in", "w_ffn_out")
_TRANSPOSED = ("w_in", "w_ffn_in")
_COLUMN_SHARDED = ("meta_tokens", "w_branch_hg", "w_branch_attn")
_SWAPPED = ("w_ffn_in",)


def _whole(name, gathered):
    _, r, c = gathered.shape
    if name in _COLUMN_SHARDED:
        return jnp.transpose(gathered, (1, 0, 2)).reshape(r, N_DEV * c)
    return gathered.reshape(N_DEV * r, c)


def _slots(name, whole):
    r, c = whole.shape
    if name in _COLUMN_SHARDED:
        return jnp.transpose(whole.reshape(r, N_DEV, c // N_DEV), (1, 0, 2))
    return whole.reshape(N_DEV, r // N_DEV, c)


def _device_step(x, target, meta_shard, ln_emb_g, ln_emb_b, w_in_shard, lbounds, norm_g, sinks, late_shards,
                 ln1_g, ln1_b, ln2_g, ln2_b):
    p = x.shape[0] + BLOCK
    lv = _level_stack()
    cos, sin = _rope_tables(p)
    swapped = [n in _SWAPPED for n in _LATE]

    h0, h0b, xhat0, rstd0, _, g_win = _embed_ln(x, meta_shard, w_in_shard, ln_emb_g, ln_emb_b)
    w_in = _whole("w_in", g_win)
    proj_hg, proj_att = _in_proj(h0b, w_in)
    yh, oa, states, scores, raw, probs, *gathered = _mixers_fwd(
        proj_hg, proj_att, lbounds, norm_g, lv, cos, sin, sinks, late_shards, swapped)
    w_bh, w_ba, w_out, w_fi, w_fo = [_whole(n, g) for n, g in zip(_LATE, gathered)]
    gates, mixin, h1, h1b, xhat1, rstd1 = _mix_out_ln1(yh, oa, h0b, w_in, h0, w_bh, w_ba, w_out, ln1_g, ln1_b)
    au, sw = _ffn_in_swiglu(h1b, w_fi)
    dr2, dr2b, loss_part, dg2, db2 = _ffn_out_loss(sw, w_fo, h1, ln2_g, ln2_b, target)

    d_wfo = _tiled_matmul_tn(sw, dr2b, tm=_row_tile(p, 1664), tk=FF_T, tn=D_MODEL, out_dtype=BF16, name="grad_w_ffn_out")
    dau, dh1 = _ffn_bwd(dr2, w_fo, au, w_fi)
    d_wfi = _weight_grad_t([dau], h1b, tk=4 * BLOCK, name="grad_w_ffn_in")
    dr1, dgates, dyh, doa, dg1, db1, d_wbh, d_wba, d_wout = _ln1_mix_bwd(
        dh1, xhat1, rstd1, ln1_g, yh, oa, gates, mixin, w_bh, w_ba, w_out)
    late_parts = [_slots(n, g) for n, g in zip(_LATE, (d_wbh, d_wba, d_wout, d_wfi, d_wfo))]
    dmix, d_lb, d_ng, d_sink, *late_recv = _mixers_bwd(
        proj_hg, proj_att, lbounds, norm_g, lv, states, scores, raw, probs, cos, sin, sinks, dyh, doa, late_parts,
        swapped)
    d_win = _weight_grad_t([dmix, dgates], h0b, tk=2 * BLOCK, name="grad_w_in")
    *win_flight, token = _exchange_start(_slots("w_in", d_win), "w_in_grads_start")
    grad_x, dlead, dg0, db0 = _embed_bwd(dmix, dgates, w_in, dr1, xhat0, rstd0, ln_emb_g + token[0:1, 0:1])

    small = dict(ln_emb_g=dg0, ln_emb_b=db0, hg_lower_bounds=d_lb, hg_norm_g=d_ng, ln1_g=dg1, ln1_b=db1, ln2_g=dg2,
                 ln2_b=db2)
    big = dict(zip(_LATE, zip(late_parts, late_recv)))
    return _pack_small(small, d_sink, dlead, loss_part), grad_x, big, win_flight


def _all_gather(arrs, dtypes, name):
    n = len(arrs)

    def body(*refs):
        ins, outs, stages = refs[:n], refs[n:2 * n], refs[2 * n:3 * n]
        send_sems, recv_sems, local_sems = refs[3 * n:]
        x, y, c = _place()
        sibling = (x, y, 1 - c)
        chips = [(1 - x, y), (x, 1 - y), (1 - x, 1 - y)]
        slot = lambda px, py, pc: 4 * px + 2 * py + pc

        def copy(w, k, block, to, from_stage=False):
            return pltpu.make_async_remote_copy(
                src_ref=stages[w] if from_stage else outs[w].at[slot(*block)], dst_ref=outs[w].at[slot(*block)],
                send_sem=send_sems.at[w, k], recv_sem=recv_sems.at[w, k], device_id=to, device_id_type=MESH)

        mine, first, passed = [], [], []
        for w in range(n):
            stages[w][...] = ins[w][...].astype(dtypes[w])
            mine.append(pltpu.make_async_copy(stages[w], outs[w].at[slot(x, y, c)], local_sems.at[w]))
            mine[-1].start()
        for w in range(n):
            first.append(copy(w, 0, (x, y, c), sibling, from_stage=True))
            first += [copy(w, 1 + j, (x, y, c), (*chip, c), from_stage=True) for j, chip in enumerate(chips)]
        for cp in first:
            cp.start()
        for j, chip in enumerate(chips):
            for w in range(n):
                copy(w, 1 + j, (*chip, c), (x, y, c)).wait_recv()
                passed.append(copy(w, 4 + j, (*chip, c), sibling))
                passed[-1].start()
        for w in range(n):
            copy(w, 0, sibling, (x, y, c)).wait_recv()
            for j, chip in enumerate(chips):
                copy(w, 4 + j, (*chip, 1 - c), (x, y, c)).wait_recv()
        for cp in first + passed:
            cp.wait_send()
        for cp in mine:
            cp.wait()

    return pl.pallas_call(
        body, name=name,
        in_specs=[pl.BlockSpec(memory_space=pltpu.VMEM)] * n,
        out_specs=[pl.BlockSpec(memory_space=pl.ANY)] * n,
        out_shape=[jax.ShapeDtypeStruct((N_DEV,) + a.shape, dt) for a, dt in zip(arrs, dtypes)],
        scratch_shapes=[pltpu.VMEM(a.shape, dt) for a, dt in zip(arrs, dtypes)]
        + [pltpu.SemaphoreType.DMA((n, 7)), pltpu.SemaphoreType.DMA((n, 7)), pltpu.SemaphoreType.DMA((n,))],
        compiler_params=pltpu.CompilerParams(vmem_limit_bytes=VMEM_LIMIT_BYTES),
    )(*arrs)


def _cast_shards(arrs):
    def body(*refs):
        for src, dst in zip(refs[:len(arrs)], refs[len(arrs):]):
            dst[...] = src[...].astype(BF16)

    return pl.pallas_call(body, name="cast_shards", out_shape=[jax.ShapeDtypeStruct(a.shape, BF16) for a in arrs],
                          compiler_params=pltpu.CompilerParams(vmem_limit_bytes=VMEM_LIMIT_BYTES))(*arrs)


def _shard_rows(rows):
    return rows if rows <= 512 else max(t for t in range(16, 353, 16) if rows % t == 0)


def _adamw_math(w, g, m, v):
    m = ADAM_B1 * m + (1.0 - ADAM_B1) * g
    v = ADAM_B2 * v + (1.0 - ADAM_B2) * (g * g)
    m_hat = m / (1.0 - ADAM_B1 ** ADAM_STEP)
    v_hat = v / (1.0 - ADAM_B2 ** ADAM_STEP)
    delta = -ADAM_LR * (m_hat / (jnp.sqrt(v_hat) + ADAM_EPS) + ADAM_WD * w)
    return delta, m, v


def _reduce_adamw(parts, recv, own_slot, w, m, v, name):
    r, cdim = w.shape
    tr = _shard_rows(r)

    def body(idx_ref, p_ref, r_ref, w_ref, m_ref, v_ref, g_out, d_out, m_out, v_out):
        g = p_ref[0].astype(F32)
        for j in range(N_PEERS):
            g = g + r_ref[j].astype(F32)
        d, mn, vn = _adamw_math(w_ref[...], g, m_ref[...], v_ref[...])
        g_out[...] = g
        d_out[...] = d
        m_out[...] = mn
        v_out[...] = vn

    flat = pl.BlockSpec((tr, cdim), lambda i, idx_ref: (i, 0))
    return pl.pallas_call(
        body, name=name,
        grid_spec=pltpu.PrefetchScalarGridSpec(
            num_scalar_prefetch=1, grid=(r // tr,),
            in_specs=[pl.BlockSpec((1, tr, cdim), lambda i, idx_ref: (idx_ref[0], i, 0)),
                      pl.BlockSpec((N_PEERS, tr, cdim), lambda i, idx_ref: (0, i, 0)), flat, flat, flat],
            out_specs=[flat] * 4),
        out_shape=[jax.ShapeDtypeStruct((r, cdim), F32)] * 4,
        compiler_params=_cparams(("arbitrary",)),
    )(own_slot, parts, recv, w, m, v)


def _adamw_plain(w, g, m, v, name):
    def body(w_ref, g_ref, m_ref, v_ref, d_out, m_out, v_out):
        d_out[...], m_out[...], v_out[...] = _adamw_math(w_ref[...], g_ref[...], m_ref[...], v_ref[...])

    return pl.pallas_call(body, name=name, out_shape=[jax.ShapeDtypeStruct(w.shape, F32)] * 3)(w, g, m, v)


_SMALL = (("ln_emb_g", (1, D_MODEL)), ("ln_emb_b", (1, D_MODEL)), ("hg_lower_bounds", (2, HG_W)),
          ("hg_norm_g", (1, BLOCK)), ("attn_sinks", (1, ATT_HEADS)), ("ln1_g", (1, D_MODEL)), ("ln1_b", (1, D_MODEL)),
          ("ln2_g", (1, D_MODEL)), ("ln2_b", (1, D_MODEL)))
_SMALL_ROW, _LOSS_ROW = {}, 0
for _name, (_rows, _) in _SMALL:
    _SMALL_ROW[_name], _LOSS_ROW = _LOSS_ROW, _LOSS_ROW + _rows
_META_ROW = 16
SMALL_ROWS = _META_ROW + N_META
assert _LOSS_ROW < _META_ROW


def _pack_small(grads, d_sink, dlead, loss_part):
    names = [n for n, _ in _SMALL if n != "attn_sinks"]

    def body(*refs):
        ins = dict(zip(names, refs))
        sink_ref, lead_ref, loss_ref, o_ref = refs[len(names):]
        o_ref[...] = jnp.zeros_like(o_ref)
        for name, (rows, cols) in _SMALL:
            if name != "attn_sinks":
                o_ref[_SMALL_ROW[name]:_SMALL_ROW[name] + rows, :cols] = ins[name][...]
        head = lax.broadcasted_iota(jnp.int32, (ATT_HEADS, BLOCK), 0)
        lane = lax.broadcasted_iota(jnp.int32, (ATT_HEADS, BLOCK), 1)
        o_ref[_SMALL_ROW["attn_sinks"]:_SMALL_ROW["attn_sinks"] + 1, :BLOCK] = jnp.sum(
            jnp.where(head == lane, sink_ref[...], 0.0), axis=0, keepdims=True)
        o_ref[_LOSS_ROW:_LOSS_ROW + 1, :BLOCK] = loss_ref[...]
        o_ref[_META_ROW:, :] = lead_ref[PAD:BLOCK, :]

    return pl.pallas_call(body, name="pack_small", out_shape=jax.ShapeDtypeStruct((SMALL_ROWS, D_MODEL), F32))(
        *[grads[n] for n in names], d_sink, dlead, loss_part)


def _small_reduce_adamw(gathered, weights, mom1, mom2):
    n = len(_SMALL)

    def body(*refs):
        g_ref, w_refs, m_refs, v_refs = refs[0], refs[1:1 + n], refs[1 + n:1 + 2 * n], refs[1 + 2 * n:1 + 3 * n]
        outs = refs[1 + 3 * n:1 + 7 * n]
        meta_out, loss_out, sum_ref = refs[1 + 7 * n:]
        total = g_ref[0]
        for s in range(1, N_DEV):
            total = total + g_ref[s]
        sum_ref[...] = total
        for i, (name, (rows, cols)) in enumerate(_SMALL):
            g = sum_ref[_SMALL_ROW[name]:_SMALL_ROW[name] + rows, :cols]
            d, mn, vn = _adamw_math(w_refs[i][...], g, m_refs[i][...], v_refs[i][...])
            for out, val in zip(outs[4 * i:4 * i + 4], (g, d, mn, vn)):
                out[...] = val
        meta_out[...] = sum_ref[_META_ROW:, :]
        loss_out[...] = jnp.broadcast_to(jnp.sum(sum_ref[_LOSS_ROW:_LOSS_ROW + 1, :BLOCK]), (1, BLOCK))

    per_param = [jax.ShapeDtypeStruct(shape, F32) for _, shape in _SMALL for _ in range(4)]
    res = pl.pallas_call(
        body, name="small_reduce_adamw",
        out_shape=per_param + [jax.ShapeDtypeStruct((N_META, D_MODEL), F32), jax.ShapeDtypeStruct((1, BLOCK), F32)],
        scratch_shapes=[pltpu.VMEM((SMALL_ROWS, D_MODEL), F32)],
    )(gathered, *[d[name] for d in (weights, mom1, mom2) for name, _ in _SMALL])
    return {name: res[4 * i:4 * i + 4] for i, (name, _) in enumerate(_SMALL)}, res[-2], res[-1]


_WEIGHTS = ("meta_tokens", "ln_emb_g", "ln_emb_b", "w_in", "hg_lower_bounds", "hg_norm_g", "attn_sinks",
            "w_branch_hg", "w_branch_attn", "w_out", "ln1_g", "ln1_b", "w_ffn_in", "w_ffn_out", "ln2_g", "ln2_b")


def kernel(x, meta_tokens, ln_emb_g, ln_emb_b, w_in, hg_lower_bounds, hg_norm_g, attn_sinks, w_branch_hg, w_branch_attn, w_out, ln1_g, ln1_b, w_ffn_in, w_ffn_out, ln2_g, ln2_b, loss_target, m_meta_tokens, m_ln_emb_g, m_ln_emb_b, m_w_in, m_hg_lower_bounds, m_hg_norm_g, m_attn_sinks, m_w_branch_hg, m_w_branch_attn, m_w_out, m_ln1_g, m_ln1_b, m_w_ffn_in, m_w_ffn_out, m_ln2_g, m_ln2_b, v_meta_tokens, v_ln_emb_g, v_ln_emb_b, v_w_in, v_hg_lower_bounds, v_hg_norm_g, v_attn_sinks, v_w_branch_hg, v_w_branch_attn, v_w_out, v_ln1_g, v_ln1_b, v_w_ffn_in, v_w_ffn_out, v_ln2_g, v_ln2_b):
    given = dict(locals())
    weights = {n: given[n] for n in _WEIGHTS}
    mom1 = {n: given["m_" + n] for n in _WEIGHTS}
    mom2 = {n: given["v_" + n] for n in _WEIGHTS}
    shard2d = lambda n, a: a.reshape(a.shape[-2:]).T if n in _TRANSPOSED else a.reshape(a.shape[-2:])

    w_in_shard, *late_shards = _cast_shards([shard2d(n, weights[n]) for n in ("w_in",) + _LATE])
    packed, grad_x, big, win_flight = _device_step(
        x[0], loss_target[0], meta_tokens, ln_emb_g.reshape(1, -1), ln_emb_b.reshape(1, -1), w_in_shard,
        hg_lower_bounds, hg_norm_g, attn_sinks, late_shards, ln1_g, ln1_b, ln2_g, ln2_b)

    place = _place()
    out = {}

    def reduce_adamw(n, parts, recv):
        own = _slot(place, n in _SWAPPED).astype(jnp.int32).reshape(1)
        res = _reduce_adamw(parts, recv, own, shard2d(n, weights[n]), shard2d(n, mom1[n]), shard2d(n, mom2[n]),
                            "adamw_" + n)
        out[n] = [(r.T if n in _TRANSPOSED else r).reshape(weights[n].shape) for r in res]

    for n, (parts, recv) in big.items():
        reduce_adamw(n, parts, recv)

    all_small, = _all_gather([packed], [F32], "gather_small")
    as_2d = lambda d: {n: d[n].reshape(shape) for n, shape in _SMALL}
    small_out, meta_whole, loss_row = _small_reduce_adamw(all_small, as_2d(weights), as_2d(mom1), as_2d(mom2))
    for n, res in small_out.items():
        out[n] = [r.reshape(weights[n].shape) for r in res]
    loss = loss_row[0, 0]
    g_meta_mine = lax.dynamic_index_in_dim(meta_whole.reshape(N_META, N_DEV, D_MODEL // N_DEV), _slot(place, False),
                                           axis=1, keepdims=False)
    out["meta_tokens"] = [g_meta_mine, *_adamw_plain(meta_tokens, g_meta_mine, m_meta_tokens, v_meta_tokens,
                                                     "adamw_meta")]

    reduce_adamw("w_in", *_exchange_wait(*win_flight, after=all_small, name="w_in_grads_wait"))

    return (loss, grad_x[None], *[out[n][0] for n in _WEIGHTS], *[out[n][1] for n in _WEIGHTS],
            *[out[n][2] for n in _WEIGHTS], *[out[n][3] for n in _WEIGHTS])
```

```python
import functools

import numpy as np
import jax
import jax.numpy as jnp
from jax import lax
from jax.experimental import pallas as pl
from jax.experimental.pallas import tpu as pltpu

F32 = jnp.float32
BF16 = jnp.bfloat16

D_MODEL = 1024
N_META = 16
BLOCK = 128
PAD = BLOCK - N_META
HG_HEADS = 4
HG_W = 512
ATT_HEADS = 8
HEAD_DIM = 64
ATT_QW = 512
ATT_KVW = 128
D_FF = 2816
EPS = 1e-5
ALPHA = 2.0 ** 0.25
ROPE_THETA = 10000.0
N_DEV = 8

ADAM_LR = 0.001
ADAM_B1 = 0.9
ADAM_B2 = 0.999
ADAM_EPS = 1e-08
ADAM_WD = 0.01
ADAM_STEP = 10

VMEM_LIMIT_BYTES = 56 * 1024 * 1024
MESH = pl.DeviceIdType.MESH

_LEVELS = (64, 32, 16, 8, 4, 2, 1)


def _cparams(sem):
    return pltpu.CompilerParams(dimension_semantics=sem, vmem_limit_bytes=VMEM_LIMIT_BYTES)


def _row_tile(rows, target):
    nb = rows // BLOCK
    best = 1
    for d in range(1, nb + 1):
        if nb % d == 0 and d * BLOCK <= target:
            best = d
    return best * BLOCK


_DN = {"nn": (((1,), (0,)), ((), ())), "nt": (((1,), (1,)), ((), ())), "tn": (((0,), (0,)), ((), ()))}


def _dot(a, b, form):
    return lax.dot_general(a.astype(BF16), b.astype(BF16), _DN[form], preferred_element_type=F32)


@functools.partial(jax.custom_vjp, nondiff_argnums=(2,))
def _mm(a, b, form):
    return _dot(a, b, form)


def _mm_fwd(a, b, form):
    a, b = a.astype(BF16), b.astype(BF16)
    return _dot(a, b, form), (a, b)


def _mm_bwd(form, res, g):
    a, b = res
    if form == "nn":
        return _dot(g, b, "nt"), _dot(a, g, "tn")
    if form == "nt":
        return _dot(g, b, "nn"), _dot(g, a, "tn")
    return _dot(b, g, "nt"), _dot(a, g, "nn")


_mm.defvjp(_mm_fwd, _mm_bwd)


def _split_dot(lv, x, form):
    return lax.dot_general(lv, x.astype(BF16), _DN[form], preferred_element_type=F32)


@jax.custom_vjp
def _swap_halves(x):
    return pltpu.roll(x, 64, 1)


_swap_halves.defvjp(lambda x: (pltpu.roll(x, 64, 1), None), lambda _, g: (pltpu.roll(g, 64, 1),))


def _weight_grad_t(cots, h, *, tk, name):
    p, d = h.shape
    steps = [c.shape[1] // tk for c in cots]
    assert all(c.shape == (p, n * tk) for c, n in zip(cots, steps)), (name, [c.shape for c in cots], tk)
    first = [sum(steps[:i]) for i in range(len(cots))]

    def body(*refs):
        h_ref, o_ref = refs[len(cots)], refs[len(cots) + 1]
        k = pl.program_id(0)
        for c_ref, lo, n in zip(refs, first, steps):
            @pl.when((k >= lo) & (k < lo + n))
            def _(c_ref=c_ref):
                o_ref[...] = _dot(c_ref[...], h_ref[...], "tn").astype(BF16)

    cot_spec = lambda lo, n: pl.BlockSpec((p, tk), lambda k: (0, jnp.clip(k - lo, 0, n - 1)))
    return pl.pallas_call(
        body, name=name, grid=(sum(steps),),
        in_specs=[cot_spec(lo, n) for lo, n in zip(first, steps)]
                 + [pl.BlockSpec((p, d), lambda k: (0, 0), pipeline_mode=pl.Buffered(1))],
        out_specs=pl.BlockSpec((tk, d), lambda k: (k, 0)),
        out_shape=jax.ShapeDtypeStruct((sum(steps) * tk, d), BF16),
        compiler_params=_cparams(("arbitrary",)),
    )(*cots, h)


def _ln_stats(r):
    mu = jnp.mean(r, axis=-1, keepdims=True)
    xc = r - mu
    var = jnp.mean(xc * xc, axis=-1, keepdims=True)
    rstd = lax.rsqrt(var + EPS)
    return xc * rstd, rstd


def _ln_bwd(dy, xhat, rstd, g):
    dxhat = dy * g
    m1 = jnp.mean(dxhat, axis=-1, keepdims=True)
    m2 = jnp.mean(dxhat * xhat, axis=-1, keepdims=True)
    dr = rstd * (dxhat - m1 - xhat * m2)
    return dr, jnp.sum(dy * xhat, axis=0, keepdims=True), jnp.sum(dy, axis=0, keepdims=True)


N_SEG = 3 + len(_LEVELS)


def _level_stack():
    t = np.arange(BLOCK)[:, None]
    r = np.arange(BLOCK)[None, :]
    mats = [r <= t, r > t, np.ones((BLOCK, BLOCK), bool)]
    for h in _LEVELS:
        same = (t // (2 * h)) == (r // (2 * h))
        up_t, up_r = (t % (2 * h)) >= h, (r % (2 * h)) >= h
        mats.append(same & ((up_t & up_r & (r <= t)) | (~up_t & ~up_r & (r > t))))
    return jnp.asarray(np.concatenate(mats, axis=0).astype(np.float32), dtype=BF16)


def _hgrn_gates(hf, a0, a1, valid):
    lb = jax.nn.sigmoid(a0 - a1)
    fg = lb + (1.0 - lb) * jax.nn.sigmoid(hf)
    return jnp.where(valid, jnp.log(fg), 0.0), jnp.where(valid, 1.0 - fg, 0.0)


def _hgrn_scores(hq, k, *levels):
    q = jax.nn.silu(hq)
    rows = lax.broadcasted_iota(jnp.int32, (BLOCK, BLOCK), 0)
    cols = lax.broadcasted_iota(jnp.int32, (BLOCK, BLOCK), 1)
    a = jnp.where(rows == cols, jnp.sum(q * k, axis=-1, keepdims=True), 0.0)
    differ = jnp.bitwise_xor(rows, cols)
    for h, lvl in zip(_LEVELS, levels):
        decay = jnp.exp(lvl)
        pair = (cols < rows) & (differ >= h) & (differ < 2 * h)
        a = a + jnp.where(pair, _mm(q * decay, k * decay, "nt"), 0.0)
    return a


def _hgrn_mix(hq, k, v, st_in, a, seg_incl, seg_after, seg_total):
    o = _mm(jax.nn.silu(hq) * jnp.exp(seg_incl), st_in, "nt") + _mm(a, v, "nn")
    return o, st_in * jnp.exp(seg_total) + _mm(v, k * jnp.exp(seg_after), "tn")


def _hgrn_norm(o, hg, ng):
    return o * lax.rsqrt(jnp.mean(o * o, axis=-1, keepdims=True) + EPS) * ng * jax.nn.silu(hg)


def _seg_blocks(e, h):
    return [e[i * BLOCK:(i + 1) * BLOCK, h * BLOCK:(h + 1) * BLOCK] for i in range(N_SEG)]


def _rope(x, cos, sin, first_half):
    partner = jnp.where(first_half, -pltpu.roll(x, 96, 1), pltpu.roll(x, 32, 1))
    return x * cos + partner * sin


def _rope_t(g, cos, sin, first_half):
    u = g * sin
    partner = jnp.where(first_half, pltpu.roll(u, 96, 1), -pltpu.roll(u, 32, 1))
    return g * cos + partner


def _low_half(x):
    return lax.broadcasted_iota(jnp.int32, x.shape, 1) < HEAD_DIM


def _both_halves(x, g):
    sw = _swap_halves(x)
    return jnp.where(_low_half(x), x, sw) if g == 0 else jnp.where(_low_half(x), sw, x)


def _att_scores(qa, qb, kc, kp, km, g, own4, band4, meta4):
    low = _low_half(qa)
    q4 = jnp.concatenate([jnp.where(low, qa, 0.0), jnp.where(low, 0.0, qa),
                          jnp.where(low, qb, 0.0), jnp.where(low, 0.0, qb)], axis=0)
    scale = HEAD_DIM ** -0.5
    neg = jnp.finfo(F32).min
    s = jnp.where(own4, _mm(_both_halves(kc, g), q4, "nt"), _mm(_both_halves(kp, g), q4, "nt"))
    return (jnp.where(band4, s * scale, neg), jnp.where(meta4, _mm(_both_halves(km, g), q4, "nt") * scale, neg))


def _att_probs(s, sm, sinkrow):
    mx = jnp.maximum(jnp.maximum(jnp.max(s, axis=0, keepdims=True), jnp.max(sm, axis=0, keepdims=True)), sinkrow)
    p, pm, ps = jnp.exp(s - mx), jnp.exp(sm - mx), jnp.exp(sinkrow - mx)
    inv = 1.0 / (jnp.sum(p, axis=0, keepdims=True) + jnp.sum(pm, axis=0, keepdims=True) + ps)
    return p * inv, pm * inv, ps * inv


def _att_probs_bwd(p, pm, ps, dp, dpm):
    r = jnp.sum(p * dp, axis=0, keepdims=True) + jnp.sum(pm * dpm, axis=0, keepdims=True)
    return p * (dp - r), pm * (dpm - r), -ps * r


def _att_values(p, pm, vc, vp, vm, g, own4):
    o4 = (_mm(jnp.where(own4, p, 0.0), _both_halves(vc, g), "tn") + _mm(jnp.where(own4, 0.0, p), _both_halves(vp, g), "tn")
          + _mm(pm, _both_halves(vm, g), "tn"))
    tiles = []
    for j in range(2):
        upper = o4[(2 * j) * BLOCK:(2 * j + 1) * BLOCK]
        tiles.append(jnp.where(_low_half(upper), upper, o4[(2 * j + 1) * BLOCK:(2 * j + 2) * BLOCK]))
    return tiles


def _att_masks(blk_idx):
    kidx = lax.broadcasted_iota(jnp.int32, (BLOCK, BLOCK), 0)
    qrow = lax.broadcasted_iota(jnp.int32, (BLOCK, BLOCK), 1)
    own_side = kidx <= qrow
    pos_own = blk_idx * BLOCK + kidx - PAD
    ok_band = (own_side & (pos_own >= N_META)) | (~own_side & (pos_own - BLOCK >= N_META) & (blk_idx >= 1))
    qpos = blk_idx * BLOCK + lax.broadcasted_iota(jnp.int32, (N_META, BLOCK), 1) - PAD
    ok_meta = lax.broadcasted_iota(jnp.int32, (N_META, BLOCK), 0) <= qpos
    return [jnp.concatenate([m] * 4, axis=1) for m in (own_side, ok_band, ok_meta)]


def _token_streams(tr, tile_of=lambda i: i):
    k = tr // BLOCK
    return [pl.BlockSpec((BLOCK, D_MODEL), lambda i, j=j: (jnp.maximum(k * tile_of(i) - 1 + j, 0), 0))
            for j in range(k)]


def _embed_ln(x, meta_shard, w_in_shard, g0, b0):
    p = x.shape[0] + BLOCK
    tr = _row_tile(p, 640)
    k = tr // BLOCK
    nt = p // tr
    tile_of = lambda s: (s + 1) % nt
    shards = [meta_shard, w_in_shard]
    c_in, c_out, c_shapes, c_sems = _comm_specs(shards, N_DEV)

    def body(*refs):
        g_ref, b_ref = refs[k:k + 2]
        h_ref, hb_ref, xh_ref, rs_ref = refs[k + 4:k + 8]
        out_refs = refs[k + 8:k + 10]
        lead_ref, meta_ref = refs[k + 10:k + 12]
        starts, passes, waits = _gather_behind(refs[k + 2:k + 4], out_refs, refs[k + 12:], [False, False])
        s = pl.program_id(0)
        t = tile_of(s)

        @pl.when(s == 0)
        def _():
            lead_ref[...] = jnp.zeros_like(lead_ref)
            for start in starts:
                start()

        @pl.when(s == nt - 1)
        def _():
            for step in passes + waits:
                step()
            pltpu.sync_copy(out_refs[0], meta_ref)
            for d in range(N_DEV):
                lead_ref[PAD:BLOCK, d * BLOCK:(d + 1) * BLOCK] = meta_ref[d]

        first = jnp.where(t == 0, lead_ref[...], refs[0][...])
        xhat, rstd = _ln_stats(jnp.concatenate([first] + [r[...] for r in refs[1:k]], axis=0))
        row = t * tr + lax.broadcasted_iota(jnp.int32, (tr, 1), 0)
        h = jnp.where(row >= PAD, xhat * g_ref[...] + b_ref[...], 0.0)
        h_ref[...] = h
        hb_ref[...] = h.astype(BF16)
        xh_ref[...] = xhat
        rs_ref[...] = rstd

    vec = pl.BlockSpec((1, D_MODEL), lambda s: (0, 0))
    rowsp = pl.BlockSpec((tr, D_MODEL), lambda s: (tile_of(s), 0))
    return pl.pallas_call(
        body, name="embed_ln", grid=(nt,),
        in_specs=_token_streams(tr, tile_of) + [vec, vec] + c_in,
        out_specs=[rowsp, rowsp, rowsp, pl.BlockSpec((tr, 1), lambda s: (tile_of(s), 0))] + c_out,
        out_shape=[jax.ShapeDtypeStruct((p, D_MODEL), F32), jax.ShapeDtypeStruct((p, D_MODEL), BF16),
                   jax.ShapeDtypeStruct((p, D_MODEL), F32), jax.ShapeDtypeStruct((p, 1), F32)] + c_shapes,
        scratch_shapes=[pltpu.VMEM((BLOCK, D_MODEL), F32), pltpu.VMEM((N_DEV, N_META, BLOCK), F32)] + c_sems,
        compiler_params=_cparams(("arbitrary",)),
    )(*([x] * k), g0, b0, *shards)


def _rope_tables(p):
    pos = (np.arange(p, dtype=np.int32) - PAD).astype(np.float32)
    half = HEAD_DIM // 2
    inv = np.float32(ROPE_THETA) ** (-np.arange(half, dtype=np.float32) / np.float32(half))
    ang = pos[:, None] * np.tile(inv.astype(np.float32), BLOCK // half)[None, :]
    return jnp.asarray(np.cos(ang), F32), jnp.asarray(np.sin(ang), F32)


def _att_sinkrows(sink_ref):
    lanehead = lax.broadcasted_iota(jnp.int32, (1, 4 * BLOCK), 1) // BLOCK
    rows = []
    for g in range(2):
        row = jnp.zeros((1, 4 * BLOCK), F32)
        for j in range(4):
            row = jnp.where(lanehead == j, sink_ref[0, 4 * g + j], row)
        rows.append(row)
    return rows


def _first_half(rows):
    return (lax.broadcasted_iota(jnp.int32, (rows, BLOCK), 1) % HEAD_DIM) < (HEAD_DIM // 2)


def _att_load(qkv_ref, cos_ref, sin_ref, with_q):
    cos, sin, fh = cos_ref[...], sin_ref[...], _first_half(BLOCK)
    qs = [_rope(qkv_ref[:, j * BLOCK:(j + 1) * BLOCK], cos, sin, fh) for j in range(4)] if with_q else None
    k = _rope(qkv_ref[:, ATT_QW:ATT_QW + ATT_KVW], cos, sin, fh)
    v = qkv_ref[:, ATT_QW + ATT_KVW:ATT_QW + 2 * ATT_KVW]
    return qs, k, v


def _att_load_meta(qkv_ref, cos_ref, sin_ref):
    k = _rope(qkv_ref[PAD:BLOCK, ATT_QW:ATT_QW + ATT_KVW], cos_ref[PAD:BLOCK, :], sin_ref[PAD:BLOCK, :],
              _first_half(N_META))
    return k, qkv_ref[PAD:BLOCK, ATT_QW + ATT_KVW:ATT_QW + 2 * ATT_KVW]


def _att_specs(blk):
    w = ATT_QW + 2 * ATT_KVW
    cur = lambda width: pl.BlockSpec((BLOCK, width), lambda i: (blk(i), 0))
    prev = lambda width: pl.BlockSpec((BLOCK, width), lambda i: (jnp.maximum(blk(i) - 1, 0), 0))
    meta = lambda width: pl.BlockSpec((BLOCK, width), lambda i: (0, 0))
    return [cur(w), prev(w), meta(w), cur(BLOCK), cur(BLOCK), prev(BLOCK), prev(BLOCK), meta(BLOCK), meta(BLOCK),
            pl.BlockSpec(memory_space=pltpu.SMEM)]


_FLIPS = [(dx, dy, dc) for dx in (0, 1) for dy in (0, 1) for dc in (0, 1)][1:]
N_PEERS = len(_FLIPS)


def _place():
    return lax.axis_index("x"), lax.axis_index("y"), lax.axis_index("c")


def _peer(place, flip):
    return tuple(1 - p if f else p for p, f in zip(place, flip))


def _slot(place, swapped):
    x, y, c = place
    return 4 * y + 2 * x + c if swapped else 4 * x + 2 * y + c


def _comm_specs(arrs, out_lead):
    n = len(arrs)
    outs = [jax.ShapeDtypeStruct((out_lead,) + a.shape[-2:], a.dtype) for a in arrs]
    sems = [pltpu.SemaphoreType.DMA((n, N_PEERS)), pltpu.SemaphoreType.DMA((n, N_PEERS)), pltpu.SemaphoreType.DMA((n,))]
    return [pl.BlockSpec(memory_space=pl.ANY)] * n, [pl.BlockSpec(memory_space=pl.ANY)] * n, outs, sems


def _gather_behind(shard_refs, out_refs, sems, swapped):
    send_sems, recv_sems, local_sems = sems
    x, y, c = _place()
    me, sibling = (x, y, c), (x, y, 1 - c)
    chips = [(1 - x, y), (x, 1 - y), (1 - x, 1 - y)]
    starts, passes, waits = [], [], []
    for w, (s, o) in enumerate(zip(shard_refs, out_refs)):
        def copy(k, block, to, from_shard=False, w=w, s=s, o=o):
            rows = o.at[_slot(block, swapped[w])]
            return pltpu.make_async_remote_copy(
                src_ref=s if from_shard else rows, dst_ref=rows, send_sem=send_sems.at[w, k],
                recv_sem=recv_sems.at[w, k], device_id=to, device_id_type=MESH)

        own = pltpu.make_async_copy(s, o.at[_slot(me, swapped[w])], local_sems.at[w])
        first = [copy(0, me, sibling, True)] + [copy(1 + j, me, (*chip, c), True) for j, chip in enumerate(chips)]
        handed = [copy(4 + j, (*chip, c), sibling) for j, chip in enumerate(chips)]
        starts += [own.start] + [cp.start for cp in first]
        for j, chip in enumerate(chips):
            passes += [copy(1 + j, (*chip, c), me).wait_recv, handed[j].start]
        waits.append(copy(0, sibling, me).wait_recv)
        waits += [copy(4 + j, (*chip, 1 - c), me).wait_recv for j, chip in enumerate(chips)]
        waits += [cp.wait_send for cp in first + handed] + [own.wait]
    return starts, passes, waits


def _scatter_behind(part_refs, recv_refs, sems, swapped):
    send_sems, recv_sems, _ = sems
    place = _place()
    starts, waits = [], []
    for w, (p, o) in enumerate(zip(part_refs, recv_refs)):
        for r, flip in enumerate(_FLIPS):
            peer = _peer(place, flip)
            cp = pltpu.make_async_remote_copy(
                src_ref=p.at[_slot(peer, swapped[w])], dst_ref=o.at[r], send_sem=send_sems.at[w, r],
                recv_sem=recv_sems.at[w, r], device_id=peer, device_id_type=MESH)
            starts.append(cp.start)
            waits += [cp.wait_recv, cp.wait_send]
    return starts, waits


def _mixers_fwd(proj_hg, proj_att, lbounds, norm_g, lv, cos, sin, sinks, shards, swapped):
    p = proj_hg.shape[0]
    nb = p // BLOCK
    n = len(shards)
    c_in, c_out, c_shapes, c_sems = _comm_specs(shards, N_DEV)
    pass_step = min(nb - 1, max(1, (5 * nb) // 8))

    def body(*refs):
        x_ref, lb_ref, ng_ref, lv_ref, cur_ref, prev_ref, meta_ref, cc, sc, cp, sp, cm, sm, sink_ref = refs[:14]
        shard_refs = refs[14:14 + n]
        y_ref, o_ref, st_ref, a_ref, raw_ref, pr_ref = refs[14 + n:20 + n]
        out_refs = refs[20 + n:20 + 2 * n]
        carry_ref = refs[20 + 2 * n]
        starts, passes, waits = _gather_behind(shard_refs, out_refs, refs[21 + 2 * n:], swapped)
        c = pl.program_id(0)

        @pl.when(c == 0)
        def _():
            carry_ref[...] = jnp.zeros_like(carry_ref)
            for start in starts:
                start()

        @pl.when(c == pass_step)
        def _():
            for step in passes:
                step()

        valid = (c * BLOCK + lax.broadcasted_iota(jnp.int32, (BLOCK, 1), 0)) >= PAD
        logf, k = _hgrn_gates(x_ref[:, HG_W:2 * HG_W], lb_ref[0:1, :], lb_ref[1:2, :], valid)
        e = _split_dot(lv_ref[...], logf, "nn")
        for h in range(HG_HEADS):
            sl = lambda part: x_ref[:, part * HG_W + h * BLOCK: part * HG_W + (h + 1) * BLOCK]
            hs = slice(h * BLOCK, (h + 1) * BLOCK)
            st_in = carry_ref[h]
            st_ref[0, h] = st_in
            seg = _seg_blocks(e, h)
            a = _hgrn_scores(sl(0), k[:, hs], *seg[3:])
            a_ref[0, h] = a.astype(BF16)
            raw, st_out = _hgrn_mix(sl(0), k[:, hs], sl(2), st_in, a, *seg[:3])
            raw_ref[:, hs] = raw
            y_ref[:, hs] = _hgrn_norm(raw, sl(3), ng_ref[...]).astype(BF16)
            carry_ref[h] = st_out

        qs, kc, vc = _att_load(cur_ref, cc, sc, True)
        _, kp, vp = _att_load(prev_ref, cp, sp, False)
        km, vm = _att_load_meta(meta_ref, cm, sm)
        sinkrows = _att_sinkrows(sink_ref)
        own4, band4, meta4 = _att_masks(c)
        for g in range(2):
            s, s_meta = _att_scores(qs[2 * g], qs[2 * g + 1], kc, kp, km, g, own4, band4, meta4)
            pr, pr_meta, pr_sink = _att_probs(s, s_meta, sinkrows[g])
            pr_ref[0, g, :BLOCK, :] = pr.astype(BF16)
            pr_ref[0, g, BLOCK:BLOCK + N_META, :] = pr_meta.astype(BF16)
            pr_ref[0, g, BLOCK + N_META:, :] = jnp.broadcast_to(pr_sink, (N_META, 4 * BLOCK)).astype(BF16)
            for j, tile in enumerate(_att_values(pr, pr_meta, vc, vp, vm, g, own4)):
                o_ref[:, (2 * g + j) * BLOCK:(2 * g + j + 1) * BLOCK] = tile.astype(BF16)

        @pl.when(c == nb - 1)
        def _():
            for wait in waits:
                wait()

    return pl.pallas_call(
        body, name="mixers_fwd", grid=(nb,),
        in_specs=[pl.BlockSpec((BLOCK, 4 * HG_W), lambda c: (c, 0)), pl.BlockSpec((2, HG_W), lambda c: (0, 0)),
                  pl.BlockSpec((1, BLOCK), lambda c: (0, 0)), pl.BlockSpec(lv.shape, lambda c: (0, 0))]
        + _att_specs(lambda c: c) + c_in,
        out_specs=[pl.BlockSpec((BLOCK, HG_W), lambda c: (c, 0)), pl.BlockSpec((BLOCK, ATT_QW), lambda c: (c, 0)),
                   pl.BlockSpec((1, HG_HEADS, BLOCK, BLOCK), lambda c: (c, 0, 0, 0)),
                   pl.BlockSpec((1, HG_HEADS, BLOCK, BLOCK), lambda c: (c, 0, 0, 0)),
                   pl.BlockSpec((BLOCK, HG_W), lambda c: (c, 0)),
                   pl.BlockSpec((1, 2, ATT_KEYS, 4 * BLOCK), lambda c: (c, 0, 0, 0))] + c_out,
        out_shape=[jax.ShapeDtypeStruct((p, HG_W), BF16), jax.ShapeDtypeStruct((p, ATT_QW), BF16),
                   jax.ShapeDtypeStruct((nb, HG_HEADS, BLOCK, BLOCK), F32),
                   jax.ShapeDtypeStruct((nb, HG_HEADS, BLOCK, BLOCK), BF16),
                   jax.ShapeDtypeStruct((p, HG_W), F32),
                   jax.ShapeDtypeStruct((nb, 2, ATT_KEYS, 4 * BLOCK), BF16)] + c_shapes,
        scratch_shapes=[pltpu.VMEM((HG_HEADS, BLOCK, BLOCK), F32)] + c_sems,
        compiler_params=_cparams(("arbitrary",)),
    )(proj_hg, lbounds, norm_g, lv, proj_att, proj_att, proj_att, cos, sin, cos, sin, cos, sin, sinks, *shards)


def _tile(rows, preferred):
    return preferred if rows % preferred == 0 else _row_tile(rows, preferred)


def _in_proj(h0b, w_in_t):
    p = h0b.shape[0]
    tm = _tile(p, 1040)
    hg_end = 4 * HG_W

    def body(h_ref, w_ref, hg_ref, att_ref):
        h = h_ref[...]
        hg_ref[...] = _dot(h, w_ref[:hg_end, :], "nt")
        att_ref[...] = _dot(h, w_ref[hg_end:, :], "nt")

    row = lambda w: pl.BlockSpec((tm, w), lambda i: (i, 0))
    return pl.pallas_call(
        body, name="in_proj", grid=(p // tm,),
        in_specs=[row(D_MODEL), pl.BlockSpec((MIX_W, D_MODEL), lambda i: (0, 0), pipeline_mode=pl.Buffered(1))],
        out_specs=[row(hg_end), row(MIX_W - hg_end)],
        out_shape=[jax.ShapeDtypeStruct((p, hg_end), F32), jax.ShapeDtypeStruct((p, MIX_W - hg_end), F32)],
        compiler_params=_cparams(("arbitrary",)),
    )(h0b, w_in_t)


def _branch_mix(yh, oa, gates, w_bh, w_ba):
    y_hg = _dot(yh, w_bh, "nn")
    y_att = _dot(oa, w_ba, "nn")
    s1 = jax.nn.sigmoid(gates[:, :D_MODEL].astype(F32))
    s2 = jax.nn.sigmoid(gates[:, D_MODEL:].astype(F32))
    return s1 * y_hg + s2 * y_att, y_hg, y_att, s1, s2


def _mix_out_ln1(yh, oa, h0b, w_in_t, h0, w_bh, w_ba, w_out, g1, b1):
    p = yh.shape[0]
    tr = _tile(p, 416)

    def body(yh_ref, oa_ref, h0b_ref, wi_ref, h0_ref, wbh_ref, wba_ref, wo_ref, g1_ref, b1_ref,
             g_ref, mix_ref, h1_ref, h1b_ref, xh_ref, rs_ref):
        g_ref[...] = _dot(h0b_ref[...], wi_ref[MIX_W:, :], "nt").astype(BF16)
        mixin = _branch_mix(yh_ref[...], oa_ref[...], g_ref[...], wbh_ref[...], wba_ref[...])[0]
        mix_ref[...] = mixin.astype(BF16)
        xhat, rstd = _ln_stats(ALPHA * h0_ref[...] + _dot(mixin, wo_ref[...], "nn"))
        h1 = xhat * g1_ref[...] + b1_ref[...]
        h1_ref[...] = h1
        h1b_ref[...] = h1.astype(BF16)
        xh_ref[...] = xhat
        rs_ref[...] = rstd

    row = lambda w: pl.BlockSpec((tr, w), lambda i: (i, 0))
    const = lambda a: pl.BlockSpec(a.shape, lambda i: (0, 0))
    return pl.pallas_call(
        body, name="mix_out_ln1", grid=(p // tr,),
        in_specs=[row(HG_W), row(ATT_QW), row(D_MODEL),
                  pl.BlockSpec(w_in_t.shape, lambda i: (0, 0), pipeline_mode=pl.Buffered(1)), row(D_MODEL),
                  const(w_bh), const(w_ba), const(w_out), const(g1), const(b1)],
        out_specs=[row(2 * D_MODEL), row(D_MODEL), row(D_MODEL), row(D_MODEL), row(D_MODEL), row(1)],
        out_shape=[jax.ShapeDtypeStruct((p, 2 * D_MODEL), BF16), jax.ShapeDtypeStruct((p, D_MODEL), BF16),
                   jax.ShapeDtypeStruct((p, D_MODEL), F32), jax.ShapeDtypeStruct((p, D_MODEL), BF16),
                   jax.ShapeDtypeStruct((p, D_MODEL), F32), jax.ShapeDtypeStruct((p, 1), F32)],
        compiler_params=_cparams(("arbitrary",)),
    )(yh, oa, h0b, w_in_t, h0, w_bh, w_ba, w_out, g1, b1)


FF_T = D_FF // 2


def _ffn_in_swiglu(h1, w_fi_t):
    p = h1.shape[0]
    tm = _tile(p, 1040)

    def body(h_ref, w_ref, au_ref, s_ref):
        au = _dot(h_ref[...], w_ref[...], "nt")
        au_ref[...] = au.astype(BF16)
        s_ref[...] = (jax.nn.silu(au[:, :FF_T]) * au[:, FF_T:]).astype(BF16)

    return pl.pallas_call(
        body, name="ffn_in_swiglu", grid=(D_FF // FF_T, p // tm),
        in_specs=[pl.BlockSpec((tm, D_MODEL), lambda j, i: (i, 0)), pl.BlockSpec((2 * FF_T, D_MODEL), lambda j, i: (j, 0))],
        out_specs=[pl.BlockSpec((tm, 2 * FF_T), lambda j, i: (i, j)), pl.BlockSpec((tm, FF_T), lambda j, i: (i, j))],
        out_shape=[jax.ShapeDtypeStruct((p, 2 * D_FF), BF16), jax.ShapeDtypeStruct((p, D_FF), BF16)],
        compiler_params=_cparams(("arbitrary", "arbitrary")),
    )(h1, w_fi_t)


def _ffn_out_loss(s, w_fo, h1, g2, b2, target):
    p = h1.shape[0]
    tr = _row_tile(p, 640)
    k = tr // BLOCK

    def body(*refs):
        s_ref, w_ref, h_ref, g_ref, b_ref = refs[:5]
        dr_ref, drb_ref, loss_ref, dg_ref, db_ref = refs[5 + k:]
        i = pl.program_id(0)
        xhat, rstd = _ln_stats(ALPHA * h_ref[...] + _dot(s_ref[...], w_ref[...], "nn"))
        y = xhat * g_ref[...] + b_ref[...]
        row = i * tr + lax.broadcasted_iota(jnp.int32, (tr, 1), 0)
        tgt = jnp.concatenate([r[...] for r in refs[5:5 + k]], axis=0)
        err = jnp.where(row >= BLOCK, y - tgt, 0.0)
        dr, dg, db = _ln_bwd(err * (1.0 / D_MODEL), xhat, rstd, g_ref[...])
        dr_ref[...] = dr
        drb_ref[...] = dr.astype(BF16)
        e2 = jnp.sum(err * err, axis=0, keepdims=True)
        part = e2[:, 0:BLOCK]
        for j in range(1, D_MODEL // BLOCK):
            part = part + e2[:, j * BLOCK:(j + 1) * BLOCK]
        part = part * (0.5 / D_MODEL)

        @pl.when(i == 0)
        def _():
            loss_ref[...] = part
            dg_ref[...] = dg
            db_ref[...] = db

        @pl.when(i > 0)
        def _():
            loss_ref[...] += part
            dg_ref[...] += dg
            db_ref[...] += db

    vec = pl.BlockSpec((1, D_MODEL), lambda i: (0, 0))
    rowsp = pl.BlockSpec((tr, D_MODEL), lambda i: (i, 0))
    return pl.pallas_call(
        body, name="ffn_out_loss", grid=(p // tr,),
        in_specs=[pl.BlockSpec((tr, D_FF), lambda i: (i, 0)), pl.BlockSpec((D_FF, D_MODEL), lambda i: (0, 0)),
                  rowsp, vec, vec] + _token_streams(tr),
        out_specs=[rowsp, rowsp, pl.BlockSpec((1, BLOCK), lambda i: (0, 0)), vec, vec],
        out_shape=[jax.ShapeDtypeStruct((p, D_MODEL), F32), jax.ShapeDtypeStruct((p, D_MODEL), BF16),
                   jax.ShapeDtypeStruct((1, BLOCK), F32), jax.ShapeDtypeStruct((1, D_MODEL), F32),
                   jax.ShapeDtypeStruct((1, D_MODEL), F32)],
        compiler_params=_cparams(("arbitrary",)),
    )(s, w_fo, h1, g2, b2, *([target] * k))


def _ffn_bwd(dr2, w_fo, au, w_fi_t):
    p = au.shape[0]
    tm = _tile(p, 416)

    def body(d_ref, wo_ref, au_ref, wi_ref, dau_ref, dh_ref):
        d = d_ref[...].astype(BF16)
        dh = ALPHA * d_ref[...]
        for j in range(D_FF // FF_T):
            a_cols = slice(2 * j * FF_T, (2 * j + 1) * FF_T)
            u_cols = slice((2 * j + 1) * FF_T, (2 * j + 2) * FF_T)
            ds = _dot(d, wo_ref[j * FF_T:(j + 1) * FF_T, :], "nt")
            _, vjp = jax.vjp(lambda a, u: jax.nn.silu(a) * u, au_ref[:, a_cols].astype(F32), au_ref[:, u_cols].astype(F32))
            da, du = vjp(ds)
            dau_ref[:, a_cols] = da.astype(BF16)
            dau_ref[:, u_cols] = du.astype(BF16)
            pair = slice(2 * j * FF_T, (2 * j + 2) * FF_T)
            dh = dh + _dot(dau_ref[:, pair], wi_ref[pair, :], "nn")
        dh_ref[...] = dh

    row = lambda w: pl.BlockSpec((tm, w), lambda i: (i, 0))
    kept = lambda a: pl.BlockSpec(a.shape, lambda i: (0, 0), pipeline_mode=pl.Buffered(1))
    return pl.pallas_call(
        body, name="ffn_bwd", grid=(p // tm,),
        in_specs=[row(D_MODEL), kept(w_fo), row(2 * D_FF), kept(w_fi_t)],
        out_specs=[row(2 * D_FF), row(D_MODEL)],
        out_shape=[jax.ShapeDtypeStruct((p, 2 * D_FF), BF16), jax.ShapeDtypeStruct((p, D_MODEL), F32)],
        compiler_params=_cparams(("arbitrary",)),
    )(dr2, w_fo, au, w_fi_t)


def _ln1_mix_bwd(dh1, xhat1, rstd1, g1, yh, oa, gates, mixin, w_bh, w_ba, w_out):
    p = yh.shape[0]
    tr = _tile(p, 320)
    nt = p // tr
    group = 2
    assert nt % group == 0, (p, tr)

    def body(dh_ref, xh_ref, rs_ref, g1_ref, yh_ref, oa_ref, g_ref, mix_ref, wbh_ref, wba_ref, wo_ref,
             dr_ref, dgt_ref, dyh_ref, doa_ref, dg_ref, db_ref, dwbh_ref, dwba_ref, dwo_ref,
             abh_ref, aba_ref, ao_ref, kept_l, kept_r):
        i = pl.program_id(0)
        dr, dg, db = _ln_bwd(dh_ref[...], xh_ref[...], rs_ref[...], g1_ref[...])
        dr_ref[...] = dr
        d = _dot(dr, wo_ref[...], "nt")
        _, y_hg, y_att, s1, s2 = _branch_mix(yh_ref[...], oa_ref[...], g_ref[...], wbh_ref[...], wba_ref[...])
        dy_hg = (d * s1).astype(BF16)
        dy_att = (d * s2).astype(BF16)
        dgt_ref[:, :D_MODEL] = (d * y_hg * s1 * (1.0 - s1)).astype(BF16)
        dgt_ref[:, D_MODEL:] = (d * y_att * s2 * (1.0 - s2)).astype(BF16)
        dyh_ref[...] = _dot(dy_hg, wbh_ref[...], "nt")
        doa_ref[...] = _dot(dy_att, wba_ref[...], "nt")

        rows = pl.ds(pl.multiple_of((i % group) * tr, tr), tr)
        kept_l[rows, :HG_W] = yh_ref[...]
        kept_l[rows, HG_W:HG_W + ATT_QW] = oa_ref[...]
        kept_l[rows, HG_W + ATT_QW:] = mix_ref[...]
        kept_r[rows, :D_MODEL] = dy_hg
        kept_r[rows, D_MODEL:2 * D_MODEL] = dy_att
        kept_r[rows, 2 * D_MODEL:] = dr.astype(BF16)

        @pl.when(i == 0)
        def _():
            dg_ref[...] = dg
            db_ref[...] = db
            for ref in (abh_ref, aba_ref, ao_ref):
                ref[...] = jnp.zeros_like(ref)

        @pl.when(i > 0)
        def _():
            dg_ref[...] += dg
            db_ref[...] += db

        @pl.when(i % group == group - 1)
        def _():
            abh_ref[...] += _dot(kept_l[:, :HG_W], kept_r[:, :D_MODEL], "tn")
            aba_ref[...] += _dot(kept_l[:, HG_W:HG_W + ATT_QW], kept_r[:, D_MODEL:2 * D_MODEL], "tn")
            ao_ref[...] += _dot(kept_l[:, HG_W + ATT_QW:], kept_r[:, 2 * D_MODEL:], "tn")

        @pl.when(i == nt - 1)
        def _():
            dwbh_ref[...] = abh_ref[...].astype(BF16)
            dwba_ref[...] = aba_ref[...].astype(BF16)
            dwo_ref[...] = ao_ref[...].astype(BF16)

    row = lambda w: pl.BlockSpec((tr, w), lambda i: (i, 0))
    const = lambda a: pl.BlockSpec(a.shape, lambda i: (0, 0), pipeline_mode=pl.Buffered(1))
    vec = pl.BlockSpec((1, D_MODEL), lambda i: (0, 0))
    weights = (w_bh, w_ba, w_out)
    return pl.pallas_call(
        body, name="ln1_mix_bwd", grid=(nt,),
        in_specs=[row(D_MODEL), row(D_MODEL), row(1), vec, row(HG_W), row(ATT_QW), row(2 * D_MODEL), row(D_MODEL)]
                 + [const(w) for w in weights],
        out_specs=[row(D_MODEL), row(2 * D_MODEL), row(HG_W), row(ATT_QW), vec, vec]
                  + [pl.BlockSpec(w.shape, lambda i: (0, 0)) for w in weights],
        out_shape=[jax.ShapeDtypeStruct((p, D_MODEL), F32), jax.ShapeDtypeStruct((p, 2 * D_MODEL), BF16),
                   jax.ShapeDtypeStruct((p, HG_W), F32), jax.ShapeDtypeStruct((p, ATT_QW), F32),
                   jax.ShapeDtypeStruct((1, D_MODEL), F32), jax.ShapeDtypeStruct((1, D_MODEL), F32)]
                  + [jax.ShapeDtypeStruct(w.shape, BF16) for w in weights],
        scratch_shapes=[pltpu.VMEM(w.shape, F32) for w in weights]
                       + [pltpu.VMEM((group * tr, HG_W + ATT_QW + D_MODEL), BF16),
                          pltpu.VMEM((group * tr, 3 * D_MODEL), BF16)],
        compiler_params=_cparams(("arbitrary",)),
    )(dh1, xhat1, rstd1, g1, yh, oa, gates, mixin, w_bh, w_ba, w_out)


MIX_W = 4 * HG_W + ATT_QW + 2 * ATT_KVW
ATT_KEYS = BLOCK + 2 * N_META


def _mixers_bwd(proj_hg, proj_att, lbounds, norm_g, lv, states, scores, raw, probs, cos, sin, sinks, dyh, doa,
                parts, swapped):
    p = proj_hg.shape[0]
    nb = p // BLOCK
    n = len(parts)
    kvw = 2 * ATT_KVW
    rev = lambda s: nb - 1 - s
    c_in, c_out, c_shapes, c_sems = _comm_specs(parts, N_PEERS)

    def body(*refs):
        (x_ref, lb_ref, ng_ref, lv_ref, st_ref, a_ref, raw_ref, pr_ref, cur_ref, prev_ref, meta_ref, cc, sc, cp, sp,
         cm, sm, sink_ref, dy_ref, do_ref) = refs[:20]
        part_refs = refs[20:20 + n]
        dx_ref, dlb_ref, dng_ref, dsink_ref = refs[20 + n:24 + n]
        recv_refs = refs[24 + n:24 + 2 * n]
        dcarry_ref, dkv_next_ref, dkv_meta_ref = refs[24 + 2 * n:27 + 2 * n]
        starts, waits = _scatter_behind(part_refs, recv_refs, refs[27 + 2 * n:], swapped)
        step = pl.program_id(0)
        c = rev(step)

        @pl.when(step == 0)
        def _():
            dcarry_ref[...] = jnp.zeros_like(dcarry_ref)
            dkv_next_ref[...] = jnp.zeros_like(dkv_next_ref)
            dkv_meta_ref[...] = jnp.zeros_like(dkv_meta_ref)
            dlb_ref[...] = jnp.zeros_like(dlb_ref)
            dng_ref[...] = jnp.zeros_like(dng_ref)
            dsink_ref[...] = jnp.zeros_like(dsink_ref)
            for start in starts:
                start()

        fh = _first_half(BLOCK)
        qs, kc, vc = _att_load(cur_ref, cc, sc, True)
        _, kp, vp = _att_load(prev_ref, cp, sp, False)
        km, vm = _att_load_meta(meta_ref, cm, sm)
        own4, band4, meta4 = _att_masks(c)
        att0 = 4 * HG_W
        dkm = dkp = dkc = dvm = dvp = dvc = 0.0
        dsinkrows = []
        for g in range(2):
            pr = pr_ref[0, g, :BLOCK, :].astype(F32)
            pr_meta = pr_ref[0, g, BLOCK:BLOCK + N_META, :].astype(F32)
            pr_sink = jnp.max(pr_ref[0, g, BLOCK + N_META:, :].astype(F32), axis=0, keepdims=True)
            _, values_vjp = jax.vjp(lambda *a, g=g: _att_values(*a, g, own4), pr, pr_meta, vc, vp, vm)
            dpr, dpr_meta, dvc_g, dvp_g, dvm_g = values_vjp(
                [do_ref[:, (2 * g + j) * BLOCK:(2 * g + j + 1) * BLOCK] for j in range(2)])
            ds, ds_meta, dsinkrow = _att_probs_bwd(pr, pr_meta, pr_sink, dpr, dpr_meta)
            _, scores_vjp = jax.vjp(lambda *a, g=g: _att_scores(*a, g, own4, band4, meta4),
                                    qs[2 * g], qs[2 * g + 1], kc, kp, km)
            dqa, dqb, dkc_g, dkp_g, dkm_g = scores_vjp((ds, ds_meta))
            for j, dq in enumerate((dqa, dqb)):
                dx_ref[:, att0 + (2 * g + j) * BLOCK:att0 + (2 * g + j + 1) * BLOCK] = _rope_t(
                    dq, cc[...], sc[...], fh).astype(BF16)
            dkm, dkp, dkc = dkm + dkm_g, dkp + dkp_g, dkc + dkc_g
            dvm, dvp, dvc = dvm + dvm_g, dvp + dvp_g, dvc + dvc_g
            dsinkrows.append(dsinkrow)
        ds0, ds1 = dsinkrows
        dkv_meta_ref[:, :BLOCK] += _rope_t(dkm, cm[PAD:BLOCK, :], sm[PAD:BLOCK, :], _first_half(N_META))
        dkv_meta_ref[:, BLOCK:] += dvm
        last = jnp.where(c == 0, 1.0, 0.0)
        to_meta_rows = lambda m: jnp.concatenate([jnp.zeros((PAD, BLOCK), F32), last * m], axis=0)
        dk = _rope_t(dkc, cc[...], sc[...], fh) + dkv_next_ref[:, :BLOCK] + to_meta_rows(dkv_meta_ref[:, :BLOCK])
        dv = dvc + dkv_next_ref[:, BLOCK:] + to_meta_rows(dkv_meta_ref[:, BLOCK:])
        dx_ref[:, att0 + ATT_QW:att0 + ATT_QW + ATT_KVW] = dk.astype(BF16)
        dx_ref[:, att0 + ATT_QW + ATT_KVW:] = dv.astype(BF16)
        dkv_next_ref[:, :BLOCK] = _rope_t(dkp, cp[...], sp[...], fh)
        dkv_next_ref[:, BLOCK:] = dvp
        sink_rows = []
        for dsg in (ds0, ds1):
            for j in range(4):
                tot = jnp.sum(dsg[:, j * BLOCK:(j + 1) * BLOCK], axis=1, keepdims=True)
                sink_rows.append(jnp.broadcast_to(tot, (1, BLOCK)))
        dsink_ref[...] += jnp.concatenate(sink_rows, axis=0)

        valid = (c * BLOCK + lax.broadcasted_iota(jnp.int32, (BLOCK, 1), 0)) >= PAD
        (logf, k), gates_vjp = jax.vjp(lambda hf, a0, a1: _hgrn_gates(hf, a0, a1, valid),
                                       x_ref[:, HG_W:2 * HG_W], lb_ref[0:1, :], lb_ref[1:2, :])
        lvv = lv_ref[...]
        e = _split_dot(lvv, logf, "nn")
        dng = jnp.zeros((1, BLOCK), F32)
        dk, dseg = [], []
        for h in range(HG_HEADS):
            sl = lambda part: x_ref[:, part * HG_W + h * BLOCK: part * HG_W + (h + 1) * BLOCK]
            hs = slice(h * BLOCK, (h + 1) * BLOCK)
            seg = _seg_blocks(e, h)
            _, norm_vjp = jax.vjp(_hgrn_norm, raw_ref[:, hs], sl(3), ng_ref[...])
            draw, dhg, dngh = norm_vjp(dy_ref[:, hs])
            _, mix_vjp = jax.vjp(_hgrn_mix, sl(0), k[:, hs], sl(2), st_ref[0, h], a_ref[0, h].astype(F32), *seg[:3])
            dhq, dkh, dhi, dst, da, *dseg_mix = mix_vjp((draw, dcarry_ref[h]))
            _, scores_vjp = jax.vjp(_hgrn_scores, sl(0), k[:, hs], *seg[3:])
            dhq2, dkh2, *dseg_lvl = scores_vjp(da)
            for part, val in ((0, dhq + dhq2), (2, dhi), (3, dhg)):
                dx_ref[:, part * HG_W + h * BLOCK: part * HG_W + (h + 1) * BLOCK] = val.astype(BF16)
            dk.append(dkh + dkh2)
            dseg.append(jnp.concatenate(dseg_mix + dseg_lvl, axis=0))
            dng = dng + dngh
            dcarry_ref[h] = dst
        dlogf = _split_dot(lvv, jnp.concatenate(dseg, axis=1), "tn")
        dhf, da0, da1 = gates_vjp((dlogf, jnp.concatenate(dk, axis=1)))
        dx_ref[:, HG_W:2 * HG_W] = dhf.astype(BF16)
        dlb_ref[0:1, :] += da0
        dlb_ref[1:2, :] += da1
        dng_ref[...] += dng

        @pl.when(step == nb - 1)
        def _():
            for wait in waits:
                wait()

    const = lambda shape: pl.BlockSpec(shape, lambda s: (0,) * len(shape))
    per_head = pl.BlockSpec((1, HG_HEADS, BLOCK, BLOCK), lambda s: (rev(s), 0, 0, 0))
    return pl.pallas_call(
        body, name="mixers_bwd", grid=(nb,),
        in_specs=[pl.BlockSpec((BLOCK, 4 * HG_W), lambda s: (rev(s), 0)), const((2, HG_W)), const((1, BLOCK)),
                  const(lv.shape), per_head, per_head, pl.BlockSpec((BLOCK, HG_W), lambda s: (rev(s), 0)),
                  pl.BlockSpec((1, 2, ATT_KEYS, 4 * BLOCK), lambda s: (rev(s), 0, 0, 0))]
        + _att_specs(rev)
        + [pl.BlockSpec((BLOCK, HG_W), lambda s: (rev(s), 0)), pl.BlockSpec((BLOCK, ATT_QW), lambda s: (rev(s), 0))]
        + c_in,
        out_specs=[pl.BlockSpec((BLOCK, MIX_W), lambda s: (rev(s), 0)), const((2, HG_W)), const((1, BLOCK)),
                   const((ATT_HEADS, BLOCK))] + c_out,
        out_shape=[jax.ShapeDtypeStruct((p, MIX_W), BF16), jax.ShapeDtypeStruct((2, HG_W), F32),
                   jax.ShapeDtypeStruct((1, BLOCK), F32), jax.ShapeDtypeStruct((ATT_HEADS, BLOCK), F32)] + c_shapes,
        scratch_shapes=[pltpu.VMEM((HG_HEADS, BLOCK, BLOCK), F32), pltpu.VMEM((BLOCK, kvw), F32),
                        pltpu.VMEM((N_META, kvw), F32)] + c_sems,
        compiler_params=_cparams(("arbitrary",)),
    )(proj_hg, lbounds, norm_g, lv, states, scores, raw, probs, proj_att, proj_att, proj_att, cos, sin, cos, sin,
      cos, sin, sinks, dyh, doa, *parts)


_HBM = pl.BlockSpec(memory_space=pltpu.HBM)
_SEM = pl.BlockSpec(memory_space=pltpu.SEMAPHORE)
_ORDERED_BY_DATA = pltpu.CompilerParams(has_side_effects=pltpu.SideEffectType.DATAFLOW_SIDE_EFFECTING)


def _exchange_copies(part_ref, land_ref, send_sems, recv_sems):
    place = _place()
    return [pltpu.make_async_remote_copy(
        src_ref=part_ref.at[_slot(_peer(place, flip), False)], dst_ref=land_ref.at[r], send_sem=send_sems.at[r],
        recv_sem=recv_sems.at[r], device_id=_peer(place, flip), device_id_type=MESH) for r, flip in enumerate(_FLIPS)]


def _exchange_start(parts, name):
    def body(part_ref, land_ref, send_sems, recv_sems, part_thru, land_thru, token):
        for cp in _exchange_copies(part_ref, land_ref, send_sems, recv_sems):
            cp.start()
        token[...] = jnp.zeros_like(token)

    land = (N_PEERS,) + parts.shape[1:]
    return pl.pallas_call(
        body, name=name,
        out_shape=(pltpu.SemaphoreType.DMA((N_PEERS,)), pltpu.SemaphoreType.DMA((N_PEERS,)),
                   pltpu.HBM(parts.shape, parts.dtype), pltpu.HBM(land, parts.dtype), jax.ShapeDtypeStruct((8, BLOCK), F32)),
        in_specs=(_HBM, _HBM), out_specs=(_SEM, _SEM, _HBM, _HBM, pl.BlockSpec(memory_space=pltpu.VMEM)),
        input_output_aliases={0: 2, 1: 3}, compiler_params=_ORDERED_BY_DATA,
    )(pltpu.with_memory_space_constraint(parts, pltpu.HBM),
      pltpu.with_memory_space_constraint(lax.empty(land, parts.dtype), pltpu.HBM))


def _exchange_wait(send_sems, recv_sems, part_thru, land_thru, after, name):
    def body(part_ref, land_ref, send_sems, recv_sems, after_ref, part_out, land_out):
        for cp in _exchange_copies(part_ref, land_ref, send_sems, recv_sems):
            cp.wait_send()
            cp.wait_recv()

    return pl.pallas_call(
        body, name=name,
        out_shape=(pltpu.HBM(part_thru.shape, part_thru.dtype), pltpu.HBM(land_thru.shape, land_thru.dtype)),
        in_specs=(_HBM, _HBM, _SEM, _SEM, pl.BlockSpec(memory_space=pl.ANY)), out_specs=(_HBM, _HBM),
        input_output_aliases={0: 0, 1: 1}, compiler_params=_ORDERED_BY_DATA,
    )(part_thru, land_thru, send_sems, recv_sems, after)


def _embed_bwd(dmix, dgates, w_in_t, dr1, xhat0, rstd0, g0):
    p = dmix.shape[0]
    tm = _row_tile(p, 640)
    nm = p // tm

    def body(a_ref, g_ref, w_ref, dr_ref, xh_ref, rs_ref, g0_ref, gx_ref, lead_ref, dg_ref, db_ref, buf_ref, sem):
        i = pl.program_id(0)
        first = pltpu.make_async_copy(buf_ref.at[0, pl.ds(BLOCK, tm - BLOCK)], gx_ref.at[pl.ds(0, tm - BLOCK)],
                                      sem.at[0])
        later = lambda t: pltpu.make_async_copy(buf_ref.at[t % 2], gx_ref.at[pl.ds(t * tm - BLOCK, tm)], sem.at[t % 2])

        @pl.when(i == 2)
        def _():
            first.wait()

        @pl.when(i > 2)
        def _():
            later(i - 2).wait()

        dh0 = (ALPHA * dr_ref[...] + _dot(a_ref[...], w_ref[:MIX_W, :], "nn")
               + _dot(g_ref[...], w_ref[MIX_W:, :], "nn"))
        row = i * tm + lax.broadcasted_iota(jnp.int32, (tm, 1), 0)
        dx, dg, db = _ln_bwd(jnp.where(row >= PAD, dh0, 0.0), xh_ref[...], rs_ref[...], g0_ref[...])
        buf_ref[i % 2] = dx

        @pl.when(i == 0)
        def _():
            lead_ref[...] = dx[:BLOCK]
            dg_ref[...] = dg
            db_ref[...] = db
            first.start()

        @pl.when(i > 0)
        def _():
            dg_ref[...] += dg
            db_ref[...] += db
            later(i).start()

        @pl.when(i == nm - 1)
        def _():
            for t in (nm - 2, nm - 1):
                if t >= 0:
                    (first if t == 0 else later(t)).wait()

    row = lambda w: pl.BlockSpec((tm, w), lambda i: (i, 0))
    vec = pl.BlockSpec((1, D_MODEL), lambda i: (0, 0))
    return pl.pallas_call(
        body, name="embed_bwd", grid=(nm,),
        in_specs=[row(dmix.shape[1]), row(dgates.shape[1]),
                  pl.BlockSpec(w_in_t.shape, lambda i: (0, 0), pipeline_mode=pl.Buffered(1)), row(D_MODEL), row(D_MODEL),
                  row(1), vec],
        out_specs=[pl.BlockSpec(memory_space=pl.ANY), pl.BlockSpec((BLOCK, D_MODEL), lambda i: (0, 0)), vec, vec],
        out_shape=[jax.ShapeDtypeStruct((p - BLOCK, D_MODEL), F32), jax.ShapeDtypeStruct((BLOCK, D_MODEL), F32),
                   jax.ShapeDtypeStruct((1, D_MODEL), F32), jax.ShapeDtypeStruct((1, D_MODEL), F32)],
        scratch_shapes=[pltpu.VMEM((2, tm, D_MODEL), F32), pltpu.SemaphoreType.DMA((2,))],
        compiler_params=_cparams(("arbitrary",)),
    )(dmix, dgates, w_in_t, dr1, xhat0, rstd0, g0)


_LATE = ("w_branch_hg", "w_branch_attn", "w_out", "w_ffn_in", "w_ffn_out")
_TRANSPOSED = ("w_in", "w_ffn_in")
_COLUMN_SHARDED = ("meta_tokens", "w_branch_hg", "w_branch_attn")
_SWAPPED = ("w_ffn_in",)


def _whole(name, gathered):
    _, r, c = gathered.shape
    if name in _COLUMN_SHARDED:
        return jnp.transpose(gathered, (1, 0, 2)).reshape(r, N_DEV * c)
    return gathered.reshape(N_DEV * r, c)


def _slots(name, whole):
    r, c = whole.shape
    if name in _COLUMN_SHARDED:
        return jnp.transpose(whole.reshape(r, N_DEV, c // N_DEV), (1, 0, 2))
    return whole.reshape(N_DEV, r // N_DEV, c)


def _device_step(x, target, meta_shard, ln_emb_g, ln_emb_b, w_in_shard, lbounds, norm_g, sinks, late_shards,
                 ln1_g, ln1_b, ln2_g, ln2_b):
    p = x.shape[0] + BLOCK
    lv = _level_stack()
    cos, sin = _rope_tables(p)
    swapped = [n in _SWAPPED for n in _LATE]

    h0, h0b, xhat0, rstd0, _, g_win = _embed_ln(x, meta_shard, w_in_shard, ln_emb_g, ln_emb_b)
    w_in = _whole("w_in", g_win)
    proj_hg, proj_att = _in_proj(h0b, w_in)
    yh, oa, states, scores, raw, probs, *gathered = _mixers_fwd(
        proj_hg, proj_att, lbounds, norm_g, lv, cos, sin, sinks, late_shards, swapped)
    w_bh, w_ba, w_out, w_fi, w_fo = [_whole(n, g) for n, g in zip(_LATE, gathered)]
    gates, mixin, h1, h1b, xhat1, rstd1 = _mix_out_ln1(yh, oa, h0b, w_in, h0, w_bh, w_ba, w_out, ln1_g, ln1_b)
    au, sw = _ffn_in_swiglu(h1b, w_fi)
    dr2, dr2b, loss_part, dg2, db2 = _ffn_out_loss(sw, w_fo, h1, ln2_g, ln2_b, target)

    d_wfo = _weight_grad_t([sw], dr2b, tk=2 * BLOCK, name="grad_w_ffn_out")
    dau, dh1 = _ffn_bwd(dr2, w_fo, au, w_fi)
    d_wfi = _weight_grad_t([dau], h1b, tk=4 * BLOCK, name="grad_w_ffn_in")
    dr1, dgates, dyh, doa, dg1, db1, d_wbh, d_wba, d_wout = _ln1_mix_bwd(
        dh1, xhat1, rstd1, ln1_g, yh, oa, gates, mixin, w_bh, w_ba, w_out)
    late_parts = [_slots(n, g) for n, g in zip(_LATE, (d_wbh, d_wba, d_wout, d_wfi, d_wfo))]
    dmix, d_lb, d_ng, d_sink, *late_recv = _mixers_bwd(
        proj_hg, proj_att, lbounds, norm_g, lv, states, scores, raw, probs, cos, sin, sinks, dyh, doa, late_parts,
        swapped)
    d_win = _weight_grad_t([dmix, dgates], h0b, tk=2 * BLOCK, name="grad_w_in")
    *win_flight, token = _exchange_start(_slots("w_in", d_win), "w_in_grads_start")
    grad_x, dlead, dg0, db0 = _embed_bwd(dmix, dgates, w_in, dr1, xhat0, rstd0, ln_emb_g + token[0:1, 0:1])

    small = dict(ln_emb_g=dg0, ln_emb_b=db0, hg_lower_bounds=d_lb, hg_norm_g=d_ng, ln1_g=dg1, ln1_b=db1, ln2_g=dg2,
                 ln2_b=db2)
    big = dict(zip(_LATE, zip(late_parts, late_recv)))
    return _pack_small(small, d_sink, dlead, loss_part), grad_x, big, win_flight


def _all_gather(arrs, dtypes, name):
    n = len(arrs)

    def body(*refs):
        ins, outs, stages = refs[:n], refs[n:2 * n], refs[2 * n:3 * n]
        send_sems, recv_sems, local_sems = refs[3 * n:]
        x, y, c = _place()
        sibling = (x, y, 1 - c)
        chips = [(1 - x, y), (x, 1 - y), (1 - x, 1 - y)]
        slot = lambda px, py, pc: 4 * px + 2 * py + pc

        def copy(w, k, block, to, from_stage=False):
            return pltpu.make_async_remote_copy(
                src_ref=stages[w] if from_stage else outs[w].at[slot(*block)], dst_ref=outs[w].at[slot(*block)],
                send_sem=send_sems.at[w, k], recv_sem=recv_sems.at[w, k], device_id=to, device_id_type=MESH)

        mine, first, passed = [], [], []
        for w in range(n):
            stages[w][...] = ins[w][...].astype(dtypes[w])
            mine.append(pltpu.make_async_copy(stages[w], outs[w].at[slot(x, y, c)], local_sems.at[w]))
            mine[-1].start()
        for w in range(n):
            first.append(copy(w, 0, (x, y, c), sibling, from_stage=True))
            first += [copy(w, 1 + j, (x, y, c), (*chip, c), from_stage=True) for j, chip in enumerate(chips)]
        for cp in first:
            cp.start()
        for j, chip in enumerate(chips):
            for w in range(n):
                copy(w, 1 + j, (*chip, c), (x, y, c)).wait_recv()
                passed.append(copy(w, 4 + j, (*chip, c), sibling))
                passed[-1].start()
        for w in range(n):
            copy(w, 0, sibling, (x, y, c)).wait_recv()
            for j, chip in enumerate(chips):
                copy(w, 4 + j, (*chip, 1 - c), (x, y, c)).wait_recv()
        for cp in first + passed:
            cp.wait_send()
        for cp in mine:
            cp.wait()

    return pl.pallas_call(
        body, name=name,
        in_specs=[pl.BlockSpec(memory_space=pltpu.VMEM)] * n,
        out_specs=[pl.BlockSpec(memory_space=pl.ANY)] * n,
        out_shape=[jax.ShapeDtypeStruct((N_DEV,) + a.shape, dt) for a, dt in zip(arrs, dtypes)],
        scratch_shapes=[pltpu.VMEM(a.shape, dt) for a, dt in zip(arrs, dtypes)]
        + [pltpu.SemaphoreType.DMA((n, 7)), pltpu.SemaphoreType.DMA((n, 7)), pltpu.SemaphoreType.DMA((n,))],
        compiler_params=pltpu.CompilerParams(vmem_limit_bytes=VMEM_LIMIT_BYTES),
    )(*arrs)


def _cast_shards(arrs):
    def body(*refs):
        for src, dst in zip(refs[:len(arrs)], refs[len(arrs):]):
            dst[...] = src[...].astype(BF16)

    return pl.pallas_call(body, name="cast_shards", out_shape=[jax.ShapeDtypeStruct(a.shape, BF16) for a in arrs],
                          compiler_params=pltpu.CompilerParams(vmem_limit_bytes=VMEM_LIMIT_BYTES))(*arrs)


def _shard_rows(rows):
    return rows if rows <= 512 else max(t for t in range(16, 353, 16) if rows % t == 0)


def _adamw_math(w, g, m, v):
    m = ADAM_B1 * m + (1.0 - ADAM_B1) * g
    v = ADAM_B2 * v + (1.0 - ADAM_B2) * (g * g)
    m_hat = m / (1.0 - ADAM_B1 ** ADAM_STEP)
    v_hat = v / (1.0 - ADAM_B2 ** ADAM_STEP)
    delta = -ADAM_LR * (m_hat / (jnp.sqrt(v_hat) + ADAM_EPS) + ADAM_WD * w)
    return delta, m, v


def _reduce_adamw(parts, recv, own_slot, w, m, v, name):
    r, cdim = w.shape
    tr = _shard_rows(r)

    def body(idx_ref, p_ref, r_ref, w_ref, m_ref, v_ref, g_out, d_out, m_out, v_out):
        g = p_ref[0].astype(F32)
        for j in range(N_PEERS):
            g = g + r_ref[j].astype(F32)
        d, mn, vn = _adamw_math(w_ref[...], g, m_ref[...], v_ref[...])
        g_out[...] = g
        d_out[...] = d
        m_out[...] = mn
        v_out[...] = vn

    flat = pl.BlockSpec((tr, cdim), lambda i, idx_ref: (i, 0))
    return pl.pallas_call(
        body, name=name,
        grid_spec=pltpu.PrefetchScalarGridSpec(
            num_scalar_prefetch=1, grid=(r // tr,),
            in_specs=[pl.BlockSpec((1, tr, cdim), lambda i, idx_ref: (idx_ref[0], i, 0)),
                      pl.BlockSpec((N_PEERS, tr, cdim), lambda i, idx_ref: (0, i, 0)), flat, flat, flat],
            out_specs=[flat] * 4),
        out_shape=[jax.ShapeDtypeStruct((r, cdim), F32)] * 4,
        compiler_params=_cparams(("arbitrary",)),
    )(own_slot, parts, recv, w, m, v)


def _adamw_plain(w, g, m, v, name):
    def body(w_ref, g_ref, m_ref, v_ref, d_out, m_out, v_out):
        d_out[...], m_out[...], v_out[...] = _adamw_math(w_ref[...], g_ref[...], m_ref[...], v_ref[...])

    return pl.pallas_call(body, name=name, out_shape=[jax.ShapeDtypeStruct(w.shape, F32)] * 3)(w, g, m, v)


_SMALL = (("ln_emb_g", (1, D_MODEL)), ("ln_emb_b", (1, D_MODEL)), ("hg_lower_bounds", (2, HG_W)),
          ("hg_norm_g", (1, BLOCK)), ("attn_sinks", (1, ATT_HEADS)), ("ln1_g", (1, D_MODEL)), ("ln1_b", (1, D_MODEL)),
          ("ln2_g", (1, D_MODEL)), ("ln2_b", (1, D_MODEL)))
_SMALL_ROW, _LOSS_ROW = {}, 0
for _name, (_rows, _) in _SMALL:
    _SMALL_ROW[_name], _LOSS_ROW = _LOSS_ROW, _LOSS_ROW + _rows
_META_ROW = 16
SMALL_ROWS = _META_ROW + N_META
assert _LOSS_ROW < _META_ROW


def _pack_small(grads, d_sink, dlead, loss_part):
    names = [n for n, _ in _SMALL if n != "attn_sinks"]

    def body(*refs):
        ins = dict(zip(names, refs))
        sink_ref, lead_ref, loss_ref, o_ref = refs[len(names):]
        o_ref[...] = jnp.zeros_like(o_ref)
        for name, (rows, cols) in _SMALL:
            if name != "attn_sinks":
                o_ref[_SMALL_ROW[name]:_SMALL_ROW[name] + rows, :cols] = ins[name][...]
        head = lax.broadcasted_iota(jnp.int32, (ATT_HEADS, BLOCK), 0)
        lane = lax.broadcasted_iota(jnp.int32, (ATT_HEADS, BLOCK), 1)
        o_ref[_SMALL_ROW["attn_sinks"]:_SMALL_ROW["attn_sinks"] + 1, :BLOCK] = jnp.sum(
            jnp.where(head == lane, sink_ref[...], 0.0), axis=0, keepdims=True)
        o_ref[_LOSS_ROW:_LOSS_ROW + 1, :BLOCK] = loss_ref[...]
        o_ref[_META_ROW:, :] = lead_ref[PAD:BLOCK, :]

    return pl.pallas_call(body, name="pack_small", out_shape=jax.ShapeDtypeStruct((SMALL_ROWS, D_MODEL), F32))(
        *[grads[n] for n in names], d_sink, dlead, loss_part)


def _small_reduce_adamw(gathered, weights, mom1, mom2):
    n = len(_SMALL)

    def body(*refs):
        g_ref, w_refs, m_refs, v_refs = refs[0], refs[1:1 + n], refs[1 + n:1 + 2 * n], refs[1 + 2 * n:1 + 3 * n]
        outs = refs[1 + 3 * n:1 + 7 * n]
        meta_out, loss_out, sum_ref = refs[1 + 7 * n:]
        total = g_ref[0]
        for s in range(1, N_DEV):
            total = total + g_ref[s]
        sum_ref[...] = total
        for i, (name, (rows, cols)) in enumerate(_SMALL):
            g = sum_ref[_SMALL_ROW[name]:_SMALL_ROW[name] + rows, :cols]
            d, mn, vn = _adamw_math(w_refs[i][...], g, m_refs[i][...], v_refs[i][...])
            for out, val in zip(outs[4 * i:4 * i + 4], (g, d, mn, vn)):
                out[...] = val
        meta_out[...] = sum_ref[_META_ROW:, :]
        loss_out[...] = jnp.broadcast_to(jnp.sum(sum_ref[_LOSS_ROW:_LOSS_ROW + 1, :BLOCK]), (1, BLOCK))

    per_param = [jax.ShapeDtypeStruct(shape, F32) for _, shape in _SMALL for _ in range(4)]
    res = pl.pallas_call(
        body, name="small_reduce_adamw",
        out_shape=per_param + [jax.ShapeDtypeStruct((N_META, D_MODEL), F32), jax.ShapeDtypeStruct((1, BLOCK), F32)],
        scratch_shapes=[pltpu.VMEM((SMALL_ROWS, D_MODEL), F32)],
    )(gathered, *[d[name] for d in (weights, mom1, mom2) for name, _ in _SMALL])
    return {name: res[4 * i:4 * i + 4] for i, (name, _) in enumerate(_SMALL)}, res[-2], res[-1]


_WEIGHTS = ("meta_tokens", "ln_emb_g", "ln_emb_b", "w_in", "hg_lower_bounds", "hg_norm_g", "attn_sinks",
            "w_branch_hg", "w_branch_attn", "w_out", "ln1_g", "ln1_b", "w_ffn_in", "w_ffn_out", "ln2_g", "ln2_b")


def kernel(x, meta_tokens, ln_emb_g, ln_emb_b, w_in, hg_lower_bounds, hg_norm_g, attn_sinks, w_branch_hg, w_branch_attn, w_out, ln1_g, ln1_b, w_ffn_in, w_ffn_out, ln2_g, ln2_b, loss_target, m_meta_tokens, m_ln_emb_g, m_ln_emb_b, m_w_in, m_hg_lower_bounds, m_hg_norm_g, m_attn_sinks, m_w_branch_hg, m_w_branch_attn, m_w_out, m_ln1_g, m_ln1_b, m_w_ffn_in, m_w_ffn_out, m_ln2_g, m_ln2_b, v_meta_tokens, v_ln_emb_g, v_ln_emb_b, v_w_in, v_hg_lower_bounds, v_hg_norm_g, v_attn_sinks, v_w_branch_hg, v_w_branch_attn, v_w_out, v_ln1_g, v_ln1_b, v_w_ffn_in, v_w_ffn_out, v_ln2_g, v_ln2_b):
    given = dict(locals())
    weights = {n: given[n] for n in _WEIGHTS}
    mom1 = {n: given["m_" + n] for n in _WEIGHTS}
    mom2 = {n: given["v_" + n] for n in _WEIGHTS}
    shard2d = lambda n, a: a.reshape(a.shape[-2:]).T if n in _TRANSPOSED else a.reshape(a.shape[-2:])

    w_in_shard, *late_shards = _cast_shards([shard2d(n, weights[n]) for n in ("w_in",) + _LATE])
    packed, grad_x, big, win_flight = _device_step(
        x[0], loss_target[0], meta_tokens, ln_emb_g.reshape(1, -1), ln_emb_b.reshape(1, -1), w_in_shard,
        hg_lower_bounds, hg_norm_g, attn_sinks, late_shards, ln1_g, ln1_b, ln2_g, ln2_b)

    place = _place()
    out = {}

    def reduce_adamw(n, parts, recv):
        own = _slot(place, n in _SWAPPED).astype(jnp.int32).reshape(1)
        res = _reduce_adamw(parts, recv, own, shard2d(n, weights[n]), shard2d(n, mom1[n]), shard2d(n, mom2[n]),
                            "adamw_" + n)
        out[n] = [(r.T if n in _TRANSPOSED else r).reshape(weights[n].shape) for r in res]

    for n, (parts, recv) in big.items():
        reduce_adamw(n, parts, recv)

    all_small, = _all_gather([packed], [F32], "gather_small")
    as_2d = lambda d: {n: d[n].reshape(shape) for n, shape in _SMALL}
    small_out, meta_whole, loss_row = _small_reduce_adamw(all_small, as_2d(weights), as_2d(mom1), as_2d(mom2))
    for n, res in small_out.items():
        out[n] = [r.reshape(weights[n].shape) for r in res]
    loss = loss_row[0, 0]
    g_meta_mine = lax.dynamic_index_in_dim(meta_whole.reshape(N_META, N_DEV, D_MODEL // N_DEV), _slot(place, False),
                                           axis=1, keepdims=False)
    out["meta_tokens"] = [g_meta_mine, *_adamw_plain(meta_tokens, g_meta_mine, m_meta_tokens, v_meta_tokens,
                                                     "adamw_meta")]

    reduce_adamw("w_in", *_exchange_wait(*win_flight, after=all_small, name="w_in_grads_wait"))

    return (loss, grad_x[None], *[out[n][0] for n in _WEIGHTS], *[out[n][1] for n in _WEIGHTS],
            *[out[n][2] for n in _WEIGHTS], *[out[n][3] for n in _WEIGHTS])
```

```python
import functools

import numpy as np
import jax
import jax.numpy as jnp
from jax import lax
from jax.experimental import pallas as pl
from jax.experimental.pallas import tpu as pltpu

F32 = jnp.float32
BF16 = jnp.bfloat16

D_MODEL = 1024
N_META = 16
BLOCK = 128
PAD = BLOCK - N_META
HG_HEADS = 4
HG_W = 512
ATT_HEADS = 8
HEAD_DIM = 64
ATT_QW = 512
ATT_KVW = 128
D_FF = 2816
EPS = 1e-5
ALPHA = 2.0 ** 0.25
ROPE_THETA = 10000.0
N_DEV = 8

ADAM_LR = 0.001
ADAM_B1 = 0.9
ADAM_B2 = 0.999
ADAM_EPS = 1e-08
ADAM_WD = 0.01
ADAM_STEP = 10

VMEM_LIMIT_BYTES = 56 * 1024 * 1024
MESH = pl.DeviceIdType.MESH

_LEVELS = (64, 32, 16, 8, 4, 2, 1)


def _cparams(sem):
    return pltpu.CompilerParams(dimension_semantics=sem, vmem_limit_bytes=VMEM_LIMIT_BYTES)


def _row_tile(rows, target):
    nb = rows // BLOCK
    best = 1
    for d in range(1, nb + 1):
        if nb % d == 0 and d * BLOCK <= target:
            best = d
    return best * BLOCK


_DN = {"nn": (((1,), (0,)), ((), ())), "nt": (((1,), (1,)), ((), ())), "tn": (((0,), (0,)), ((), ()))}


def _dot(a, b, form):
    return lax.dot_general(a.astype(BF16), b.astype(BF16), _DN[form], preferred_element_type=F32)


@functools.partial(jax.custom_vjp, nondiff_argnums=(2,))
def _mm(a, b, form):
    return _dot(a, b, form)


def _mm_fwd(a, b, form):
    a, b = a.astype(BF16), b.astype(BF16)
    return _dot(a, b, form), (a, b)


def _mm_bwd(form, res, g):
    a, b = res
    if form == "nn":
        return _dot(g, b, "nt"), _dot(a, g, "tn")
    if form == "nt":
        return _dot(g, b, "nn"), _dot(g, a, "tn")
    return _dot(b, g, "nt"), _dot(a, g, "nn")


_mm.defvjp(_mm_fwd, _mm_bwd)


def _split_dot(lv, x, form):
    return lax.dot_general(lv, x.astype(BF16), _DN[form], preferred_element_type=F32)


@jax.custom_vjp
def _swap_halves(x):
    return pltpu.roll(x, 64, 1)


_swap_halves.defvjp(lambda x: (pltpu.roll(x, 64, 1), None), lambda _, g: (pltpu.roll(g, 64, 1),))


def _weight_grad_t(cots, h, *, tk, name):
    p, d = h.shape
    steps = [c.shape[1] // tk for c in cots]
    assert all(c.shape == (p, n * tk) for c, n in zip(cots, steps)), (name, [c.shape for c in cots], tk)
    first = [sum(steps[:i]) for i in range(len(cots))]

    def body(*refs):
        h_ref, o_ref = refs[len(cots)], refs[len(cots) + 1]
        k = pl.program_id(0)
        for c_ref, lo, n in zip(refs, first, steps):
            @pl.when((k >= lo) & (k < lo + n))
            def _(c_ref=c_ref):
                o_ref[...] = _dot(c_ref[...], h_ref[...], "tn").astype(BF16)

    cot_spec = lambda lo, n: pl.BlockSpec((p, tk), lambda k: (0, jnp.clip(k - lo, 0, n - 1)))
    return pl.pallas_call(
        body, name=name, grid=(sum(steps),),
        in_specs=[cot_spec(lo, n) for lo, n in zip(first, steps)]
                 + [pl.BlockSpec((p, d), lambda k: (0, 0), pipeline_mode=pl.Buffered(1))],
        out_specs=pl.BlockSpec((tk, d), lambda k: (k, 0)),
        out_shape=jax.ShapeDtypeStruct((sum(steps) * tk, d), BF16),
        compiler_params=_cparams(("arbitrary",)),
    )(*cots, h)


def _ln_stats(r):
    mu = jnp.mean(r, axis=-1, keepdims=True)
    xc = r - mu
    var = jnp.mean(xc * xc, axis=-1, keepdims=True)
    rstd = lax.rsqrt(var + EPS)
    return xc * rstd, rstd


def _ln_bwd(dy, xhat, rstd, g):
    dxhat = dy * g
    m1 = jnp.mean(dxhat, axis=-1, keepdims=True)
    m2 = jnp.mean(dxhat * xhat, axis=-1, keepdims=True)
    dr = rstd * (dxhat - m1 - xhat * m2)
    return dr, jnp.sum(dy * xhat, axis=0, keepdims=True), jnp.sum(dy, axis=0, keepdims=True)


N_SEG = 3 + len(_LEVELS)


def _level_stack():
    t = np.arange(BLOCK)[:, None]
    r = np.arange(BLOCK)[None, :]
    mats = [r <= t, r > t, np.ones((BLOCK, BLOCK), bool)]
    for h in _LEVELS:
        same = (t // (2 * h)) == (r // (2 * h))
        up_t, up_r = (t % (2 * h)) >= h, (r % (2 * h)) >= h
        mats.append(same & ((up_t & up_r & (r <= t)) | (~up_t & ~up_r & (r > t))))
    return jnp.asarray(np.concatenate(mats, axis=0).astype(np.float32), dtype=BF16)


def _hgrn_gates(hf, a0, a1, valid):
    lb = jax.nn.sigmoid(a0 - a1)
    fg = lb + (1.0 - lb) * jax.nn.sigmoid(hf)
    return jnp.where(valid, jnp.log(fg), 0.0), jnp.where(valid, 1.0 - fg, 0.0)


def _hgrn_scores(hq, k, *levels):
    q = jax.nn.silu(hq)
    rows = lax.broadcasted_iota(jnp.int32, (BLOCK, BLOCK), 0)
    cols = lax.broadcasted_iota(jnp.int32, (BLOCK, BLOCK), 1)
    a = jnp.where(rows == cols, jnp.sum(q * k, axis=-1, keepdims=True), 0.0)
    differ = jnp.bitwise_xor(rows, cols)
    for h, lvl in zip(_LEVELS, levels):
        decay = jnp.exp(lvl)
        pair = (cols < rows) & (differ >= h) & (differ < 2 * h)
        a = a + jnp.where(pair, _mm(q * decay, k * decay, "nt"), 0.0)
    return a


def _hgrn_mix(hq, k, v, st_in, a, seg_incl, seg_after, seg_total):
    o = _mm(jax.nn.silu(hq) * jnp.exp(seg_incl), st_in, "nt") + _mm(a, v, "nn")
    return o, st_in * jnp.exp(seg_total) + _mm(v, k * jnp.exp(seg_after), "tn")


def _hgrn_norm(o, hg, ng):
    return o * lax.rsqrt(jnp.mean(o * o, axis=-1, keepdims=True) + EPS) * ng * jax.nn.silu(hg)


def _seg_blocks(e, h):
    return [e[i * BLOCK:(i + 1) * BLOCK, h * BLOCK:(h + 1) * BLOCK] for i in range(N_SEG)]


def _rope(x, cos, sin, first_half):
    partner = jnp.where(first_half, -pltpu.roll(x, 96, 1), pltpu.roll(x, 32, 1))
    return x * cos + partner * sin


def _rope_t(g, cos, sin, first_half):
    u = g * sin
    partner = jnp.where(first_half, pltpu.roll(u, 96, 1), -pltpu.roll(u, 32, 1))
    return g * cos + partner


def _low_half(x):
    return lax.broadcasted_iota(jnp.int32, x.shape, 1) < HEAD_DIM


def _both_halves(x, g):
    sw = _swap_halves(x)
    return jnp.where(_low_half(x), x, sw) if g == 0 else jnp.where(_low_half(x), sw, x)


def _att_scores(qa, qb, kc, kp, km, g, own4, band4, meta4):
    low = _low_half(qa)
    q4 = jnp.concatenate([jnp.where(low, qa, 0.0), jnp.where(low, 0.0, qa),
                          jnp.where(low, qb, 0.0), jnp.where(low, 0.0, qb)], axis=0)
    scale = HEAD_DIM ** -0.5
    neg = jnp.finfo(F32).min
    s = jnp.where(own4, _mm(_both_halves(kc, g), q4, "nt"), _mm(_both_halves(kp, g), q4, "nt"))
    return (jnp.where(band4, s * scale, neg), jnp.where(meta4, _mm(_both_halves(km, g), q4, "nt") * scale, neg))


def _att_probs(s, sm, sinkrow):
    mx = jnp.maximum(jnp.maximum(jnp.max(s, axis=0, keepdims=True), jnp.max(sm, axis=0, keepdims=True)), sinkrow)
    p, pm, ps = jnp.exp(s - mx), jnp.exp(sm - mx), jnp.exp(sinkrow - mx)
    inv = 1.0 / (jnp.sum(p, axis=0, keepdims=True) + jnp.sum(pm, axis=0, keepdims=True) + ps)
    return p * inv, pm * inv, ps * inv


def _att_probs_bwd(p, pm, ps, dp, dpm):
    r = jnp.sum(p * dp, axis=0, keepdims=True) + jnp.sum(pm * dpm, axis=0, keepdims=True)
    return p * (dp - r), pm * (dpm - r), -ps * r


def _att_values(p, pm, vc, vp, vm, g, own4):
    o4 = (_mm(jnp.where(own4, p, 0.0), _both_halves(vc, g), "tn") + _mm(jnp.where(own4, 0.0, p), _both_halves(vp, g), "tn")
          + _mm(pm, _both_halves(vm, g), "tn"))
    tiles = []
    for j in range(2):
        upper = o4[(2 * j) * BLOCK:(2 * j + 1) * BLOCK]
        tiles.append(jnp.where(_low_half(upper), upper, o4[(2 * j + 1) * BLOCK:(2 * j + 2) * BLOCK]))
    return tiles


def _att_masks(blk_idx):
    kidx = lax.broadcasted_iota(jnp.int32, (BLOCK, BLOCK), 0)
    qrow = lax.broadcasted_iota(jnp.int32, (BLOCK, BLOCK), 1)
    own_side = kidx <= qrow
    pos_own = blk_idx * BLOCK + kidx - PAD
    ok_band = (own_side & (pos_own >= N_META)) | (~own_side & (pos_own - BLOCK >= N_META) & (blk_idx >= 1))
    qpos = blk_idx * BLOCK + lax.broadcasted_iota(jnp.int32, (N_META, BLOCK), 1) - PAD
    ok_meta = lax.broadcasted_iota(jnp.int32, (N_META, BLOCK), 0) <= qpos
    return [jnp.concatenate([m] * 4, axis=1) for m in (own_side, ok_band, ok_meta)]


def _token_streams(tr, tile_of=lambda i: i):
    k = tr // BLOCK
    return [pl.BlockSpec((BLOCK, D_MODEL), lambda i, j=j: (jnp.maximum(k * tile_of(i) - 1 + j, 0), 0))
            for j in range(k)]


def _embed_ln(x, meta_shard, w_in_shard, g0, b0):
    p = x.shape[0] + BLOCK
    tr = _row_tile(p, 640)
    k = tr // BLOCK
    nt = p // tr
    tile_of = lambda s: (s + 1) % nt
    shards = [meta_shard, w_in_shard]
    c_in, c_out, c_shapes, c_sems = _comm_specs(shards, N_DEV)

    def body(*refs):
        g_ref, b_ref = refs[k:k + 2]
        h_ref, hb_ref, xh_ref, rs_ref = refs[k + 4:k + 8]
        out_refs = refs[k + 8:k + 10]
        lead_ref, meta_ref = refs[k + 10:k + 12]
        starts, passes, waits = _gather_behind(refs[k + 2:k + 4], out_refs, refs[k + 12:], [False, False])
        s = pl.program_id(0)
        t = tile_of(s)

        @pl.when(s == 0)
        def _():
            lead_ref[...] = jnp.zeros_like(lead_ref)
            for start in starts:
                start()

        @pl.when(s == nt - 1)
        def _():
            for step in passes + waits:
                step()
            pltpu.sync_copy(out_refs[0], meta_ref)
            for d in range(N_DEV):
                lead_ref[PAD:BLOCK, d * BLOCK:(d + 1) * BLOCK] = meta_ref[d]

        first = jnp.where(t == 0, lead_ref[...], refs[0][...])
        xhat, rstd = _ln_stats(jnp.concatenate([first] + [r[...] for r in refs[1:k]], axis=0))
        row = t * tr + lax.broadcasted_iota(jnp.int32, (tr, 1), 0)
        h = jnp.where(row >= PAD, xhat * g_ref[...] + b_ref[...], 0.0)
        h_ref[...] = h
        hb_ref[...] = h.astype(BF16)
        xh_ref[...] = xhat
        rs_ref[...] = rstd

    vec = pl.BlockSpec((1, D_MODEL), lambda s: (0, 0))
    rowsp = pl.BlockSpec((tr, D_MODEL), lambda s: (tile_of(s), 0))
    return pl.pallas_call(
        body, name="embed_ln", grid=(nt,),
        in_specs=_token_streams(tr, tile_of) + [vec, vec] + c_in,
        out_specs=[rowsp, rowsp, rowsp, pl.BlockSpec((tr, 1), lambda s: (tile_of(s), 0))] + c_out,
        out_shape=[jax.ShapeDtypeStruct((p, D_MODEL), F32), jax.ShapeDtypeStruct((p, D_MODEL), BF16),
                   jax.ShapeDtypeStruct((p, D_MODEL), F32), jax.ShapeDtypeStruct((p, 1), F32)] + c_shapes,
        scratch_shapes=[pltpu.VMEM((BLOCK, D_MODEL), F32), pltpu.VMEM((N_DEV, N_META, BLOCK), F32)] + c_sems,
        compiler_params=_cparams(("arbitrary",)),
    )(*([x] * k), g0, b0, *shards)


def _rope_tables(p):
    pos = (np.arange(p, dtype=np.int32) - PAD).astype(np.float32)
    half = HEAD_DIM // 2
    inv = np.float32(ROPE_THETA) ** (-np.arange(half, dtype=np.float32) / np.float32(half))
    ang = pos[:, None] * np.tile(inv.astype(np.float32), BLOCK // half)[None, :]
    return jnp.asarray(np.cos(ang), F32), jnp.asarray(np.sin(ang), F32)


def _att_sinkrows(sink_ref):
    lanehead = lax.broadcasted_iota(jnp.int32, (1, 4 * BLOCK), 1) // BLOCK
    rows = []
    for g in range(2):
        row = jnp.zeros((1, 4 * BLOCK), F32)
        for j in range(4):
            row = jnp.where(lanehead == j, sink_ref[0, 4 * g + j], row)
        rows.append(row)
    return rows


def _first_half(rows):
    return (lax.broadcasted_iota(jnp.int32, (rows, BLOCK), 1) % HEAD_DIM) < (HEAD_DIM // 2)


def _att_load(qkv_ref, cos_ref, sin_ref, with_q):
    cos, sin, fh = cos_ref[...], sin_ref[...], _first_half(BLOCK)
    qs = [_rope(qkv_ref[:, j * BLOCK:(j + 1) * BLOCK], cos, sin, fh) for j in range(4)] if with_q else None
    k = _rope(qkv_ref[:, ATT_QW:ATT_QW + ATT_KVW], cos, sin, fh)
    v = qkv_ref[:, ATT_QW + ATT_KVW:ATT_QW + 2 * ATT_KVW]
    return qs, k, v


def _att_load_meta(qkv_ref, cos_ref, sin_ref):
    k = _rope(qkv_ref[PAD:BLOCK, ATT_QW:ATT_QW + ATT_KVW], cos_ref[PAD:BLOCK, :], sin_ref[PAD:BLOCK, :],
              _first_half(N_META))
    return k, qkv_ref[PAD:BLOCK, ATT_QW + ATT_KVW:ATT_QW + 2 * ATT_KVW]


def _att_specs(blk):
    w = ATT_QW + 2 * ATT_KVW
    cur = lambda width: pl.BlockSpec((BLOCK, width), lambda i: (blk(i), 0))
    prev = lambda width: pl.BlockSpec((BLOCK, width), lambda i: (jnp.maximum(blk(i) - 1, 0), 0))
    meta = lambda width: pl.BlockSpec((BLOCK, width), lambda i: (0, 0))
    return [cur(w), prev(w), meta(w), cur(BLOCK), cur(BLOCK), prev(BLOCK), prev(BLOCK), meta(BLOCK), meta(BLOCK),
            pl.BlockSpec(memory_space=pltpu.SMEM)]


_FLIPS = [(dx, dy, dc) for dx in (0, 1) for dy in (0, 1) for dc in (0, 1)][1:]
N_PEERS = len(_FLIPS)


def _place():
    return lax.axis_index("x"), lax.axis_index("y"), lax.axis_index("c")


def _peer(place, flip):
    return tuple(1 - p if f else p for p, f in zip(place, flip))


def _slot(place, swapped):
    x, y, c = place
    return 4 * y + 2 * x + c if swapped else 4 * x + 2 * y + c


def _comm_specs(arrs, out_lead):
    n = len(arrs)
    outs = [jax.ShapeDtypeStruct((out_lead,) + a.shape[-2:], a.dtype) for a in arrs]
    sems = [pltpu.SemaphoreType.DMA((n, N_PEERS)), pltpu.SemaphoreType.DMA((n, N_PEERS)), pltpu.SemaphoreType.DMA((n,))]
    return [pl.BlockSpec(memory_space=pl.ANY)] * n, [pl.BlockSpec(memory_space=pl.ANY)] * n, outs, sems


def _gather_behind(shard_refs, out_refs, sems, swapped):
    send_sems, recv_sems, local_sems = sems
    x, y, c = _place()
    me, sibling = (x, y, c), (x, y, 1 - c)
    chips = [(1 - x, y), (x, 1 - y), (1 - x, 1 - y)]
    starts, passes, waits = [], [], []
    for w, (s, o) in enumerate(zip(shard_refs, out_refs)):
        def copy(k, block, to, from_shard=False, w=w, s=s, o=o):
            rows = o.at[_slot(block, swapped[w])]
            return pltpu.make_async_remote_copy(
                src_ref=s if from_shard else rows, dst_ref=rows, send_sem=send_sems.at[w, k],
                recv_sem=recv_sems.at[w, k], device_id=to, device_id_type=MESH)

        own = pltpu.make_async_copy(s, o.at[_slot(me, swapped[w])], local_sems.at[w])
        first = [copy(0, me, sibling, True)] + [copy(1 + j, me, (*chip, c), True) for j, chip in enumerate(chips)]
        handed = [copy(4 + j, (*chip, c), sibling) for j, chip in enumerate(chips)]
        starts += [own.start] + [cp.start for cp in first]
        for j, chip in enumerate(chips):
            passes += [copy(1 + j, (*chip, c), me).wait_recv, handed[j].start]
        waits.append(copy(0, sibling, me).wait_recv)
        waits += [copy(4 + j, (*chip, 1 - c), me).wait_recv for j, chip in enumerate(chips)]
        waits += [cp.wait_send for cp in first + handed] + [own.wait]
    return starts, passes, waits


def _scatter_behind(part_refs, recv_refs, sems, swapped):
    send_sems, recv_sems, _ = sems
    place = _place()
    starts, waits = [], []
    for w, (p, o) in enumerate(zip(part_refs, recv_refs)):
        for r, flip in enumerate(_FLIPS):
            peer = _peer(place, flip)
            cp = pltpu.make_async_remote_copy(
                src_ref=p.at[_slot(peer, swapped[w])], dst_ref=o.at[r], send_sem=send_sems.at[w, r],
                recv_sem=recv_sems.at[w, r], device_id=peer, device_id_type=MESH)
            starts.append(cp.start)
            waits += [cp.wait_recv, cp.wait_send]
    return starts, waits


def _mixers_fwd(proj_hg, proj_att, lbounds, norm_g, lv, cos, sin, sinks, shards, swapped):
    p = proj_hg.shape[0]
    nb = p // BLOCK
    n = len(shards)
    c_in, c_out, c_shapes, c_sems = _comm_specs(shards, N_DEV)
    pass_step = min(nb - 1, max(1, (5 * nb) // 8))

    def body(*refs):
        x_ref, lb_ref, ng_ref, lv_ref, cur_ref, prev_ref, meta_ref, cc, sc, cp, sp, cm, sm, sink_ref = refs[:14]
        shard_refs = refs[14:14 + n]
        y_ref, o_ref, st_ref, a_ref, raw_ref, pr_ref = refs[14 + n:20 + n]
        out_refs = refs[20 + n:20 + 2 * n]
        carry_ref = refs[20 + 2 * n]
        starts, passes, waits = _gather_behind(shard_refs, out_refs, refs[21 + 2 * n:], swapped)
        c = pl.program_id(0)

        @pl.when(c == 0)
        def _():
            carry_ref[...] = jnp.zeros_like(carry_ref)
            for start in starts:
                start()

        @pl.when(c == pass_step)
        def _():
            for step in passes:
                step()

        valid = (c * BLOCK + lax.broadcasted_iota(jnp.int32, (BLOCK, 1), 0)) >= PAD
        logf, k = _hgrn_gates(x_ref[:, HG_W:2 * HG_W], lb_ref[0:1, :], lb_ref[1:2, :], valid)
        e = _split_dot(lv_ref[...], logf, "nn")
        for h in range(HG_HEADS):
            sl = lambda part: x_ref[:, part * HG_W + h * BLOCK: part * HG_W + (h + 1) * BLOCK]
            hs = slice(h * BLOCK, (h + 1) * BLOCK)
            st_in = carry_ref[h]
            st_ref[0, h] = st_in
            seg = _seg_blocks(e, h)
            a = _hgrn_scores(sl(0), k[:, hs], *seg[3:])
            a_ref[0, h] = a.astype(BF16)
            raw, st_out = _hgrn_mix(sl(0), k[:, hs], sl(2), st_in, a, *seg[:3])
            raw_ref[:, hs] = raw
            y_ref[:, hs] = _hgrn_norm(raw, sl(3), ng_ref[...]).astype(BF16)
            carry_ref[h] = st_out

        qs, kc, vc = _att_load(cur_ref, cc, sc, True)
        _, kp, vp = _att_load(prev_ref, cp, sp, False)
        km, vm = _att_load_meta(meta_ref, cm, sm)
        sinkrows = _att_sinkrows(sink_ref)
        own4, band4, meta4 = _att_masks(c)
        for g in range(2):
            s, s_meta = _att_scores(qs[2 * g], qs[2 * g + 1], kc, kp, km, g, own4, band4, meta4)
            pr, pr_meta, pr_sink = _att_probs(s, s_meta, sinkrows[g])
            pr_ref[0, g, :BLOCK, :] = pr.astype(BF16)
            pr_ref[0, g, BLOCK:BLOCK + N_META, :] = pr_meta.astype(BF16)
            pr_ref[0, g, BLOCK + N_META:, :] = jnp.broadcast_to(pr_sink, (N_META, 4 * BLOCK)).astype(BF16)
            for j, tile in enumerate(_att_values(pr, pr_meta, vc, vp, vm, g, own4)):
                o_ref[:, (2 * g + j) * BLOCK:(2 * g + j + 1) * BLOCK] = tile.astype(BF16)

        @pl.when(c == nb - 1)
        def _():
            for wait in waits:
                wait()

    return pl.pallas_call(
        body, name="mixers_fwd", grid=(nb,),
        in_specs=[pl.BlockSpec((BLOCK, 4 * HG_W), lambda c: (c, 0)), pl.BlockSpec((2, HG_W), lambda c: (0, 0)),
                  pl.BlockSpec((1, BLOCK), lambda c: (0, 0)), pl.BlockSpec(lv.shape, lambda c: (0, 0))]
        + _att_specs(lambda c: c) + c_in,
        out_specs=[pl.BlockSpec((BLOCK, HG_W), lambda c: (c, 0)), pl.BlockSpec((BLOCK, ATT_QW), lambda c: (c, 0)),
                   pl.BlockSpec((1, HG_HEADS, BLOCK, BLOCK), lambda c: (c, 0, 0, 0)),
                   pl.BlockSpec((1, HG_HEADS, BLOCK, BLOCK), lambda c: (c, 0, 0, 0)),
                   pl.BlockSpec((BLOCK, HG_W), lambda c: (c, 0)),
                   pl.BlockSpec((1, 2, ATT_KEYS, 4 * BLOCK), lambda c: (c, 0, 0, 0))] + c_out,
        out_shape=[jax.ShapeDtypeStruct((p, HG_W), BF16), jax.ShapeDtypeStruct((p, ATT_QW), BF16),
                   jax.ShapeDtypeStruct((nb, HG_HEADS, BLOCK, BLOCK), F32),
                   jax.ShapeDtypeStruct((nb, HG_HEADS, BLOCK, BLOCK), BF16),
                   jax.ShapeDtypeStruct((p, HG_W), F32),
                   jax.ShapeDtypeStruct((nb, 2, ATT_KEYS, 4 * BLOCK), BF16)] + c_shapes,
        scratch_shapes=[pltpu.VMEM((HG_HEADS, BLOCK, BLOCK), F32)] + c_sems,
        compiler_params=_cparams(("arbitrary",)),
    )(proj_hg, lbounds, norm_g, lv, proj_att, proj_att, proj_att, cos, sin, cos, sin, cos, sin, sinks, *shards)


def _tile(rows, preferred):
    return preferred if rows % preferred == 0 else _row_tile(rows, preferred)


def _in_proj(h0b, w_in_t):
    p = h0b.shape[0]
    tm = _tile(p, 1040)
    hg_end = 4 * HG_W

    def body(h_ref, w_ref, hg_ref, att_ref):
        h = h_ref[...]
        hg_ref[...] = _dot(h, w_ref[:hg_end, :], "nt")
        att_ref[...] = _dot(h, w_ref[hg_end:, :], "nt")

    row = lambda w: pl.BlockSpec((tm, w), lambda i: (i, 0))
    return pl.pallas_call(
        body, name="in_proj", grid=(p // tm,),
        in_specs=[row(D_MODEL), pl.BlockSpec((MIX_W, D_MODEL), lambda i: (0, 0), pipeline_mode=pl.Buffered(1))],
        out_specs=[row(hg_end), row(MIX_W - hg_end)],
        out_shape=[jax.ShapeDtypeStruct((p, hg_end), F32), jax.ShapeDtypeStruct((p, MIX_W - hg_end), F32)],
        compiler_params=_cparams(("arbitrary",)),
    )(h0b, w_in_t)


def _branch_mix(yh, oa, gates, w_bh, w_ba):
    y_hg = _dot(yh, w_bh, "nn")
    y_att = _dot(oa, w_ba, "nn")
    s1 = jax.nn.sigmoid(gates[:, :D_MODEL].astype(F32))
    s2 = jax.nn.sigmoid(gates[:, D_MODEL:].astype(F32))
    return s1 * y_hg + s2 * y_att, y_hg, y_att, s1, s2


def _mix_out_ln1(yh, oa, h0b, w_in_t, h0, w_bh, w_ba, w_out, g1, b1):
    p = yh.shape[0]
    tr = _tile(p, 416)

    def body(yh_ref, oa_ref, h0b_ref, wi_ref, h0_ref, wbh_ref, wba_ref, wo_ref, g1_ref, b1_ref,
             g_ref, mix_ref, h1_ref, h1b_ref, xh_ref, rs_ref):
        g_ref[...] = _dot(h0b_ref[...], wi_ref[MIX_W:, :], "nt").astype(BF16)
        mixin = _branch_mix(yh_ref[...], oa_ref[...], g_ref[...], wbh_ref[...], wba_ref[...])[0]
        mix_ref[...] = mixin.astype(BF16)
        xhat, rstd = _ln_stats(ALPHA * h0_ref[...] + _dot(mixin, wo_ref[...], "nn"))
        h1 = xhat * g1_ref[...] + b1_ref[...]
        h1_ref[...] = h1
        h1b_ref[...] = h1.astype(BF16)
        xh_ref[...] = xhat
        rs_ref[...] = rstd

    row = lambda w: pl.BlockSpec((tr, w), lambda i: (i, 0))
    const = lambda a: pl.BlockSpec(a.shape, lambda i: (0, 0))
    return pl.pallas_call(
        body, name="mix_out_ln1", grid=(p // tr,),
        in_specs=[row(HG_W), row(ATT_QW), row(D_MODEL),
                  pl.BlockSpec(w_in_t.shape, lambda i: (0, 0), pipeline_mode=pl.Buffered(1)), row(D_MODEL),
                  const(w_bh), const(w_ba), const(w_out), const(g1), const(b1)],
        out_specs=[row(2 * D_MODEL), row(D_MODEL), row(D_MODEL), row(D_MODEL), row(D_MODEL), row(1)],
        out_shape=[jax.ShapeDtypeStruct((p, 2 * D_MODEL), BF16), jax.ShapeDtypeStruct((p, D_MODEL), BF16),
                   jax.ShapeDtypeStruct((p, D_MODEL), F32), jax.ShapeDtypeStruct((p, D_MODEL), BF16),
                   jax.ShapeDtypeStruct((p, D_MODEL), F32), jax.ShapeDtypeStruct((p, 1), F32)],
        compiler_params=_cparams(("arbitrary",)),
    )(yh, oa, h0b, w_in_t, h0, w_bh, w_ba, w_out, g1, b1)


FF_T = D_FF // 2


def _ffn_in_swiglu(h1, w_fi_t):
    p = h1.shape[0]
    tm = _tile(p, 1040)

    def body(h_ref, w_ref, au_ref, s_ref):
        au = _dot(h_ref[...], w_ref[...], "nt")
        au_ref[...] = au.astype(BF16)
        s_ref[...] = (jax.nn.silu(au[:, :FF_T]) * au[:, FF_T:]).astype(BF16)

    return pl.pallas_call(
        body, name="ffn_in_swiglu", grid=(D_FF // FF_T, p // tm),
        in_specs=[pl.BlockSpec((tm, D_MODEL), lambda j, i: (i, 0)), pl.BlockSpec((2 * FF_T, D_MODEL), lambda j, i: (j, 0))],
        out_specs=[pl.BlockSpec((tm, 2 * FF_T), lambda j, i: (i, j)), pl.BlockSpec((tm, FF_T), lambda j, i: (i, j))],
        out_shape=[jax.ShapeDtypeStruct((p, 2 * D_FF), BF16), jax.ShapeDtypeStruct((p, D_FF), BF16)],
        compiler_params=_cparams(("arbitrary", "arbitrary")),
    )(h1, w_fi_t)


def _ffn_out_loss(s, w_fo, h1, g2, b2, target):
    p = h1.shape[0]
    tr = _row_tile(p, 640)
    k = tr // BLOCK

    def body(*refs):
        s_ref, w_ref, h_ref, g_ref, b_ref = refs[:5]
        dr_ref, drb_ref, loss_ref, dg_ref, db_ref = refs[5 + k:]
        i = pl.program_id(0)
        xhat, rstd = _ln_stats(ALPHA * h_ref[...] + _dot(s_ref[...], w_ref[...], "nn"))
        y = xhat * g_ref[...] + b_ref[...]
        row = i * tr + lax.broadcasted_iota(jnp.int32, (tr, 1), 0)
        tgt = jnp.concatenate([r[...] for r in refs[5:5 + k]], axis=0)
        err = jnp.where(row >= BLOCK, y - tgt, 0.0)
        dr, dg, db = _ln_bwd(err * (1.0 / D_MODEL), xhat, rstd, g_ref[...])
        dr_ref[...] = dr
        drb_ref[...] = dr.astype(BF16)
        e2 = jnp.sum(err * err, axis=0, keepdims=True)
        part = e2[:, 0:BLOCK]
        for j in range(1, D_MODEL // BLOCK):
            part = part + e2[:, j * BLOCK:(j + 1) * BLOCK]
        part = part * (0.5 / D_MODEL)

        @pl.when(i == 0)
        def _():
            loss_ref[...] = part
            dg_ref[...] = dg
            db_ref[...] = db

        @pl.when(i > 0)
        def _():
            loss_ref[...] += part
            dg_ref[...] += dg
            db_ref[...] += db

    vec = pl.BlockSpec((1, D_MODEL), lambda i: (0, 0))
    rowsp = pl.BlockSpec((tr, D_MODEL), lambda i: (i, 0))
    return pl.pallas_call(
        body, name="ffn_out_loss", grid=(p // tr,),
        in_specs=[pl.BlockSpec((tr, D_FF), lambda i: (i, 0)), pl.BlockSpec((D_FF, D_MODEL), lambda i: (0, 0)),
                  rowsp, vec, vec] + _token_streams(tr),
        out_specs=[rowsp, rowsp, pl.BlockSpec((1, BLOCK), lambda i: (0, 0)), vec, vec],
        out_shape=[jax.ShapeDtypeStruct((p, D_MODEL), F32), jax.ShapeDtypeStruct((p, D_MODEL), BF16),
                   jax.ShapeDtypeStruct((1, BLOCK), F32), jax.ShapeDtypeStruct((1, D_MODEL), F32),
                   jax.ShapeDtypeStruct((1, D_MODEL), F32)],
        compiler_params=_cparams(("arbitrary",)),
    )(s, w_fo, h1, g2, b2, *([target] * k))


def _ffn_bwd(dr2, w_fo, au, w_fi_t):
    p = au.shape[0]
    tm = _tile(p, 416)
    nm = p // tm
    ring = 3

    def body(d_ref, wo_ref, au_hbm, wi_ref, dau_ref, dh_ref, au_buf, sems):
        i = pl.program_id(0)
        fetch = lambda t: pltpu.make_async_copy(au_hbm.at[pl.ds(pl.multiple_of(t * tm, tm), tm)], au_buf.at[t % ring],
                                                sems.at[t % ring])

        @pl.when(i == 0)
        def _():
            for t in range(min(ring - 1, nm)):
                fetch(t).start()

        @pl.when(i + ring - 1 < nm)
        def _():
            fetch(i + ring - 1).start()

        fetch(i).wait()
        au_ref = au_buf.at[i % ring]
        d = d_ref[...].astype(BF16)
        dh = ALPHA * d_ref[...]
        for j in range(D_FF // FF_T):
            a_cols = slice(2 * j * FF_T, (2 * j + 1) * FF_T)
            u_cols = slice((2 * j + 1) * FF_T, (2 * j + 2) * FF_T)
            ds = _dot(d, wo_ref[j * FF_T:(j + 1) * FF_T, :], "nt")
            _, vjp = jax.vjp(lambda a, u: jax.nn.silu(a) * u, au_ref[:, a_cols].astype(F32), au_ref[:, u_cols].astype(F32))
            da, du = vjp(ds)
            dau_ref[:, a_cols] = da.astype(BF16)
            dau_ref[:, u_cols] = du.astype(BF16)
            pair = slice(2 * j * FF_T, (2 * j + 2) * FF_T)
            dh = dh + _dot(dau_ref[:, pair], wi_ref[pair, :], "nn")
        dh_ref[...] = dh

    row = lambda w: pl.BlockSpec((tm, w), lambda i: (i, 0))
    kept = lambda a: pl.BlockSpec(a.shape, lambda i: (0, 0), pipeline_mode=pl.Buffered(1))
    return pl.pallas_call(
        body, name="ffn_bwd", grid=(nm,),
        in_specs=[row(D_MODEL), kept(w_fo), pl.BlockSpec(memory_space=pl.ANY), kept(w_fi_t)],
        out_specs=[row(2 * D_FF), row(D_MODEL)],
        out_shape=[jax.ShapeDtypeStruct((p, 2 * D_FF), BF16), jax.ShapeDtypeStruct((p, D_MODEL), F32)],
        scratch_shapes=[pltpu.VMEM((ring, tm, 2 * D_FF), BF16), pltpu.SemaphoreType.DMA((ring,))],
        compiler_params=_cparams(("arbitrary",)),
    )(dr2, w_fo, au, w_fi_t)


def _ln1_mix_bwd(dh1, xhat1, rstd1, g1, yh, oa, gates, mixin, w_bh, w_ba, w_out):
    p = yh.shape[0]
    tr = _tile(p, 320)
    nt = p // tr
    group = 2
    assert nt % group == 0, (p, tr)

    def body(dh_ref, xh_ref, rs_ref, g1_ref, yh_ref, oa_ref, g_ref, mix_ref, wbh_ref, wba_ref, wo_ref,
             dr_ref, dgt_ref, dyh_ref, doa_ref, dg_ref, db_ref, dwbh_ref, dwba_ref, dwo_ref,
             abh_ref, aba_ref, ao_ref, kept_l, kept_r):
        i = pl.program_id(0)
        dr, dg, db = _ln_bwd(dh_ref[...], xh_ref[...], rs_ref[...], g1_ref[...])
        dr_ref[...] = dr
        d = _dot(dr, wo_ref[...], "nt")
        _, y_hg, y_att, s1, s2 = _branch_mix(yh_ref[...], oa_ref[...], g_ref[...], wbh_ref[...], wba_ref[...])
        dy_hg = (d * s1).astype(BF16)
        dy_att = (d * s2).astype(BF16)
        dgt_ref[:, :D_MODEL] = (d * y_hg * s1 * (1.0 - s1)).astype(BF16)
        dgt_ref[:, D_MODEL:] = (d * y_att * s2 * (1.0 - s2)).astype(BF16)
        dyh_ref[...] = _dot(dy_hg, wbh_ref[...], "nt")
        doa_ref[...] = _dot(dy_att, wba_ref[...], "nt")

        rows = pl.ds(pl.multiple_of((i % group) * tr, tr), tr)
        kept_l[rows, :HG_W] = yh_ref[...]
        kept_l[rows, HG_W:HG_W + ATT_QW] = oa_ref[...]
        kept_l[rows, HG_W + ATT_QW:] = mix_ref[...]
        kept_r[rows, :D_MODEL] = dy_hg
        kept_r[rows, D_MODEL:2 * D_MODEL] = dy_att
        kept_r[rows, 2 * D_MODEL:] = dr.astype(BF16)

        @pl.when(i == 0)
        def _():
            dg_ref[...] = dg
            db_ref[...] = db
            for ref in (abh_ref, aba_ref, ao_ref):
                ref[...] = jnp.zeros_like(ref)

        @pl.when(i > 0)
        def _():
            dg_ref[...] += dg
            db_ref[...] += db

        @pl.when(i % group == group - 1)
        def _():
            abh_ref[...] += _dot(kept_l[:, :HG_W], kept_r[:, :D_MODEL], "tn")
            aba_ref[...] += _dot(kept_l[:, HG_W:HG_W + ATT_QW], kept_r[:, D_MODEL:2 * D_MODEL], "tn")
            ao_ref[...] += _dot(kept_l[:, HG_W + ATT_QW:], kept_r[:, 2 * D_MODEL:], "tn")

        @pl.when(i == nt - 1)
        def _():
            dwbh_ref[...] = abh_ref[...].astype(BF16)
            dwba_ref[...] = aba_ref[...].astype(BF16)
            dwo_ref[...] = ao_ref[...].astype(BF16)

    row = lambda w: pl.BlockSpec((tr, w), lambda i: (i, 0))
    const = lambda a: pl.BlockSpec(a.shape, lambda i: (0, 0), pipeline_mode=pl.Buffered(1))
    vec = pl.BlockSpec((1, D_MODEL), lambda i: (0, 0))
    weights = (w_bh, w_ba, w_out)
    return pl.pallas_call(
        body, name="ln1_mix_bwd", grid=(nt,),
        in_specs=[row(D_MODEL), row(D_MODEL), row(1), vec, row(HG_W), row(ATT_QW), row(2 * D_MODEL), row(D_MODEL)]
                 + [const(w) for w in weights],
        out_specs=[row(D_MODEL), row(2 * D_MODEL), row(HG_W), row(ATT_QW), vec, vec]
                  + [pl.BlockSpec(w.shape, lambda i: (0, 0)) for w in weights],
        out_shape=[jax.ShapeDtypeStruct((p, D_MODEL), F32), jax.ShapeDtypeStruct((p, 2 * D_MODEL), BF16),
                   jax.ShapeDtypeStruct((p, HG_W), F32), jax.ShapeDtypeStruct((p, ATT_QW), F32),
                   jax.ShapeDtypeStruct((1, D_MODEL), F32), jax.ShapeDtypeStruct((1, D_MODEL), F32)]
                  + [jax.ShapeDtypeStruct(w.shape, BF16) for w in weights],
        scratch_shapes=[pltpu.VMEM(w.shape, F32) for w in weights]
                       + [pltpu.VMEM((group * tr, HG_W + ATT_QW + D_MODEL), BF16),
                          pltpu.VMEM((group * tr, 3 * D_MODEL), BF16)],
        compiler_params=_cparams(("arbitrary",)),
    )(dh1, xhat1, rstd1, g1, yh, oa, gates, mixin, w_bh, w_ba, w_out)


MIX_W = 4 * HG_W + ATT_QW + 2 * ATT_KVW
ATT_KEYS = BLOCK + 2 * N_META


def _mixers_bwd(proj_hg, proj_att, lbounds, norm_g, lv, states, scores, raw, probs, cos, sin, sinks, dyh, doa,
                parts, swapped):
    p = proj_hg.shape[0]
    nb = p // BLOCK
    n = len(parts)
    kvw = 2 * ATT_KVW
    rev = lambda s: nb - 1 - s
    c_in, c_out, c_shapes, c_sems = _comm_specs(parts, N_PEERS)

    def body(*refs):
        (x_ref, lb_ref, ng_ref, lv_ref, st_ref, a_ref, raw_ref, pr_ref, cur_ref, prev_ref, meta_ref, cc, sc, cp, sp,
         cm, sm, sink_ref, dy_ref, do_ref) = refs[:20]
        part_refs = refs[20:20 + n]
        dx_ref, dlb_ref, dng_ref, dsink_ref = refs[20 + n:24 + n]
        recv_refs = refs[24 + n:24 + 2 * n]
        dcarry_ref, dkv_next_ref, dkv_meta_ref = refs[24 + 2 * n:27 + 2 * n]
        starts, waits = _scatter_behind(part_refs, recv_refs, refs[27 + 2 * n:], swapped)
        step = pl.program_id(0)
        c = rev(step)

        @pl.when(step == 0)
        def _():
            dcarry_ref[...] = jnp.zeros_like(dcarry_ref)
            dkv_next_ref[...] = jnp.zeros_like(dkv_next_ref)
            dkv_meta_ref[...] = jnp.zeros_like(dkv_meta_ref)
            dlb_ref[...] = jnp.zeros_like(dlb_ref)
            dng_ref[...] = jnp.zeros_like(dng_ref)
            dsink_ref[...] = jnp.zeros_like(dsink_ref)
            for start in starts:
                start()

        fh = _first_half(BLOCK)
        qs, kc, vc = _att_load(cur_ref, cc, sc, True)
        _, kp, vp = _att_load(prev_ref, cp, sp, False)
        km, vm = _att_load_meta(meta_ref, cm, sm)
        own4, band4, meta4 = _att_masks(c)
        att0 = 4 * HG_W
        dkm = dkp = dkc = dvm = dvp = dvc = 0.0
        dsinkrows = []
        for g in range(2):
            pr = pr_ref[0, g, :BLOCK, :].astype(F32)
            pr_meta = pr_ref[0, g, BLOCK:BLOCK + N_META, :].astype(F32)
            pr_sink = jnp.max(pr_ref[0, g, BLOCK + N_META:, :].astype(F32), axis=0, keepdims=True)
            _, values_vjp = jax.vjp(lambda *a, g=g: _att_values(*a, g, own4), pr, pr_meta, vc, vp, vm)
            dpr, dpr_meta, dvc_g, dvp_g, dvm_g = values_vjp(
                [do_ref[:, (2 * g + j) * BLOCK:(2 * g + j + 1) * BLOCK] for j in range(2)])
            ds, ds_meta, dsinkrow = _att_probs_bwd(pr, pr_meta, pr_sink, dpr, dpr_meta)
            _, scores_vjp = jax.vjp(lambda *a, g=g: _att_scores(*a, g, own4, band4, meta4),
                                    qs[2 * g], qs[2 * g + 1], kc, kp, km)
            dqa, dqb, dkc_g, dkp_g, dkm_g = scores_vjp((ds, ds_meta))
            for j, dq in enumerate((dqa, dqb)):
                dx_ref[:, att0 + (2 * g + j) * BLOCK:att0 + (2 * g + j + 1) * BLOCK] = _rope_t(
                    dq, cc[...], sc[...], fh).astype(BF16)
            dkm, dkp, dkc = dkm + dkm_g, dkp + dkp_g, dkc + dkc_g
            dvm, dvp, dvc = dvm + dvm_g, dvp + dvp_g, dvc + dvc_g
            dsinkrows.append(dsinkrow)
        ds0, ds1 = dsinkrows
        dkv_meta_ref[:, :BLOCK] += _rope_t(dkm, cm[PAD:BLOCK, :], sm[PAD:BLOCK, :], _first_half(N_META))
        dkv_meta_ref[:, BLOCK:] += dvm
        last = jnp.where(c == 0, 1.0, 0.0)
        to_meta_rows = lambda m: jnp.concatenate([jnp.zeros((PAD, BLOCK), F32), last * m], axis=0)
        dk = _rope_t(dkc, cc[...], sc[...], fh) + dkv_next_ref[:, :BLOCK] + to_meta_rows(dkv_meta_ref[:, :BLOCK])
        dv = dvc + dkv_next_ref[:, BLOCK:] + to_meta_rows(dkv_meta_ref[:, BLOCK:])
        dx_ref[:, att0 + ATT_QW:att0 + ATT_QW + ATT_KVW] = dk.astype(BF16)
        dx_ref[:, att0 + ATT_QW + ATT_KVW:] = dv.astype(BF16)
        dkv_next_ref[:, :BLOCK] = _rope_t(dkp, cp[...], sp[...], fh)
        dkv_next_ref[:, BLOCK:] = dvp
        sink_rows = []
        for dsg in (ds0, ds1):
            for j in range(4):
                tot = jnp.sum(dsg[:, j * BLOCK:(j + 1) * BLOCK], axis=1, keepdims=True)
                sink_rows.append(jnp.broadcast_to(tot, (1, BLOCK)))
        dsink_ref[...] += jnp.concatenate(sink_rows, axis=0)

        valid = (c * BLOCK + lax.broadcasted_iota(jnp.int32, (BLOCK, 1), 0)) >= PAD
        (logf, k), gates_vjp = jax.vjp(lambda hf, a0, a1: _hgrn_gates(hf, a0, a1, valid),
                                       x_ref[:, HG_W:2 * HG_W], lb_ref[0:1, :], lb_ref[1:2, :])
        lvv = lv_ref[...]
        e = _split_dot(lvv, logf, "nn")
        dng = jnp.zeros((1, BLOCK), F32)
        dk, dseg = [], []
        for h in range(HG_HEADS):
            sl = lambda part: x_ref[:, part * HG_W + h * BLOCK: part * HG_W + (h + 1) * BLOCK]
            hs = slice(h * BLOCK, (h + 1) * BLOCK)
            seg = _seg_blocks(e, h)
            _, norm_vjp = jax.vjp(_hgrn_norm, raw_ref[:, hs], sl(3), ng_ref[...])
            draw, dhg, dngh = norm_vjp(dy_ref[:, hs])
            _, mix_vjp = jax.vjp(_hgrn_mix, sl(0), k[:, hs], sl(2), st_ref[0, h], a_ref[0, h].astype(F32), *seg[:3])
            dhq, dkh, dhi, dst, da, *dseg_mix = mix_vjp((draw, dcarry_ref[h]))
            _, scores_vjp = jax.vjp(_hgrn_scores, sl(0), k[:, hs], *seg[3:])
            dhq2, dkh2, *dseg_lvl = scores_vjp(da)
            for part, val in ((0, dhq + dhq2), (2, dhi), (3, dhg)):
                dx_ref[:, part * HG_W + h * BLOCK: part * HG_W + (h + 1) * BLOCK] = val.astype(BF16)
            dk.append(dkh + dkh2)
            dseg.append(jnp.concatenate(dseg_mix + dseg_lvl, axis=0))
            dng = dng + dngh
            dcarry_ref[h] = dst
        dlogf = _split_dot(lvv, jnp.concatenate(dseg, axis=1), "tn")
        dhf, da0, da1 = gates_vjp((dlogf, jnp.concatenate(dk, axis=1)))
        dx_ref[:, HG_W:2 * HG_W] = dhf.astype(BF16)
        dlb_ref[0:1, :] += da0
        dlb_ref[1:2, :] += da1
        dng_ref[...] += dng

        @pl.when(step == nb - 1)
        def _():
            for wait in waits:
                wait()

    const = lambda shape: pl.BlockSpec(shape, lambda s: (0,) * len(shape))
    per_head = pl.BlockSpec((1, HG_HEADS, BLOCK, BLOCK), lambda s: (rev(s), 0, 0, 0))
    return pl.pallas_call(
        body, name="mixers_bwd", grid=(nb,),
        in_specs=[pl.BlockSpec((BLOCK, 4 * HG_W), lambda s: (rev(s), 0)), const((2, HG_W)), const((1, BLOCK)),
                  const(lv.shape), per_head, per_head, pl.BlockSpec((BLOCK, HG_W), lambda s: (rev(s), 0)),
                  pl.BlockSpec((1, 2, ATT_KEYS, 4 * BLOCK), lambda s: (rev(s), 0, 0, 0))]
        + _att_specs(rev)
        + [pl.BlockSpec((BLOCK, HG_W), lambda s: (rev(s), 0)), pl.BlockSpec((BLOCK, ATT_QW), lambda s: (rev(s), 0))]
        + c_in,
        out_specs=[pl.BlockSpec((BLOCK, MIX_W), lambda s: (rev(s), 0)), const((2, HG_W)), const((1, BLOCK)),
                   const((ATT_HEADS, BLOCK))] + c_out,
        out_shape=[jax.ShapeDtypeStruct((p, MIX_W), BF16), jax.ShapeDtypeStruct((2, HG_W), F32),
                   jax.ShapeDtypeStruct((1, BLOCK), F32), jax.ShapeDtypeStruct((ATT_HEADS, BLOCK), F32)] + c_shapes,
        scratch_shapes=[pltpu.VMEM((HG_HEADS, BLOCK, BLOCK), F32), pltpu.VMEM((BLOCK, kvw), F32),
                        pltpu.VMEM((N_META, kvw), F32)] + c_sems,
        compiler_params=_cparams(("arbitrary",)),
    )(proj_hg, lbounds, norm_g, lv, states, scores, raw, probs, proj_att, proj_att, proj_att, cos, sin, cos, sin,
      cos, sin, sinks, dyh, doa, *parts)


_HBM = pl.BlockSpec(memory_space=pltpu.HBM)
_SEM = pl.BlockSpec(memory_space=pltpu.SEMAPHORE)
_ORDERED_BY_DATA = pltpu.CompilerParams(has_side_effects=pltpu.SideEffectType.DATAFLOW_SIDE_EFFECTING)


def _exchange_copies(part_ref, land_ref, send_sems, recv_sems):
    place = _place()
    return [pltpu.make_async_remote_copy(
        src_ref=part_ref.at[_slot(_peer(place, flip), False)], dst_ref=land_ref.at[r], send_sem=send_sems.at[r],
        recv_sem=recv_sems.at[r], device_id=_peer(place, flip), device_id_type=MESH) for r, flip in enumerate(_FLIPS)]


def _exchange_start(parts, name):
    def body(part_ref, land_ref, send_sems, recv_sems, part_thru, land_thru, token):
        for cp in _exchange_copies(part_ref, land_ref, send_sems, recv_sems):
            cp.start()
        token[...] = jnp.zeros_like(token)

    land = (N_PEERS,) + parts.shape[1:]
    return pl.pallas_call(
        body, name=name,
        out_shape=(pltpu.SemaphoreType.DMA((N_PEERS,)), pltpu.SemaphoreType.DMA((N_PEERS,)),
                   pltpu.HBM(parts.shape, parts.dtype), pltpu.HBM(land, parts.dtype), jax.ShapeDtypeStruct((8, BLOCK), F32)),
        in_specs=(_HBM, _HBM), out_specs=(_SEM, _SEM, _HBM, _HBM, pl.BlockSpec(memory_space=pltpu.VMEM)),
        input_output_aliases={0: 2, 1: 3}, compiler_params=_ORDERED_BY_DATA,
    )(pltpu.with_memory_space_constraint(parts, pltpu.HBM),
      pltpu.with_memory_space_constraint(lax.empty(land, parts.dtype), pltpu.HBM))


def _exchange_wait(send_sems, recv_sems, part_thru, land_thru, after, name):
    def body(part_ref, land_ref, send_sems, recv_sems, after_ref, part_out, land_out):
        for cp in _exchange_copies(part_ref, land_ref, send_sems, recv_sems):
            cp.wait_send()
            cp.wait_recv()

    return pl.pallas_call(
        body, name=name,
        out_shape=(pltpu.HBM(part_thru.shape, part_thru.dtype), pltpu.HBM(land_thru.shape, land_thru.dtype)),
        in_specs=(_HBM, _HBM, _SEM, _SEM, pl.BlockSpec(memory_space=pl.ANY)), out_specs=(_HBM, _HBM),
        input_output_aliases={0: 0, 1: 1}, compiler_params=_ORDERED_BY_DATA,
    )(part_thru, land_thru, send_sems, recv_sems, after)


def _embed_bwd(dmix, dgates, w_in_t, dr1, xhat0, rstd0, g0):
    p = dmix.shape[0]
    tm = _row_tile(p, 640)
    nm = p // tm

    def body(a_ref, g_ref, w_ref, dr_ref, xh_ref, rs_ref, g0_ref, gx_ref, lead_ref, dg_ref, db_ref, buf_ref, sem):
        i = pl.program_id(0)
        first = pltpu.make_async_copy(buf_ref.at[0, pl.ds(BLOCK, tm - BLOCK)], gx_ref.at[pl.ds(0, tm - BLOCK)],
                                      sem.at[0])
        later = lambda t: pltpu.make_async_copy(buf_ref.at[t % 2], gx_ref.at[pl.ds(t * tm - BLOCK, tm)], sem.at[t % 2])

        @pl.when(i == 2)
        def _():
            first.wait()

        @pl.when(i > 2)
        def _():
            later(i - 2).wait()

        dh0 = (ALPHA * dr_ref[...] + _dot(a_ref[...], w_ref[:MIX_W, :], "nn")
               + _dot(g_ref[...], w_ref[MIX_W:, :], "nn"))
        row = i * tm + lax.broadcasted_iota(jnp.int32, (tm, 1), 0)
        dx, dg, db = _ln_bwd(jnp.where(row >= PAD, dh0, 0.0), xh_ref[...], rs_ref[...], g0_ref[...])
        buf_ref[i % 2] = dx

        @pl.when(i == 0)
        def _():
            lead_ref[...] = dx[:BLOCK]
            dg_ref[...] = dg
            db_ref[...] = db
            first.start()

        @pl.when(i > 0)
        def _():
            dg_ref[...] += dg
            db_ref[...] += db
            later(i).start()

        @pl.when(i == nm - 1)
        def _():
            for t in (nm - 2, nm - 1):
                if t >= 0:
                    (first if t == 0 else later(t)).wait()

    row = lambda w: pl.BlockSpec((tm, w), lambda i: (i, 0))
    vec = pl.BlockSpec((1, D_MODEL), lambda i: (0, 0))
    return pl.pallas_call(
        body, name="embed_bwd", grid=(nm,),
        in_specs=[row(dmix.shape[1]), row(dgates.shape[1]),
                  pl.BlockSpec(w_in_t.shape, lambda i: (0, 0), pipeline_mode=pl.Buffered(1)), row(D_MODEL), row(D_MODEL),
                  row(1), vec],
        out_specs=[pl.BlockSpec(memory_space=pl.ANY), pl.BlockSpec((BLOCK, D_MODEL), lambda i: (0, 0)), vec, vec],
        out_shape=[jax.ShapeDtypeStruct((p - BLOCK, D_MODEL), F32), jax.ShapeDtypeStruct((BLOCK, D_MODEL), F32),
                   jax.ShapeDtypeStruct((1, D_MODEL), F32), jax.ShapeDtypeStruct((1, D_MODEL), F32)],
        scratch_shapes=[pltpu.VMEM((2, tm, D_MODEL), F32), pltpu.SemaphoreType.DMA((2,))],
        compiler_params=_cparams(("arbitrary",)),
    )(dmix, dgates, w_in_t, dr1, xhat0, rstd0, g0)


_LATE = ("w_branch_hg", "w_branch_attn", "w_out", "w_ffn_in", "w_ffn_out")
_TRANSPOSED = ("w_in", "w_ffn_in")
_COLUMN_SHARDED = ("meta_tokens", "w_branch_hg", "w_branch_attn")
_SWAPPED = ("w_ffn_in",)


def _whole(name, gathered):
    _, r, c = gathered.shape
    if name in _COLUMN_SHARDED:
        return jnp.transpose(gathered, (1, 0, 2)).reshape(r, N_DEV * c)
    return gathered.reshape(N_DEV * r, c)


def _slots(name, whole):
    r, c = whole.shape
    if name in _COLUMN_SHARDED:
        return jnp.transpose(whole.reshape(r, N_DEV, c // N_DEV), (1, 0, 2))
    return whole.reshape(N_DEV, r // N_DEV, c)


def _device_step(x, target, meta_shard, ln_emb_g, ln_emb_b, w_in_shard, lbounds, norm_g, sinks, late_shards,
                 ln1_g, ln1_b, ln2_g, ln2_b):
    p = x.shape[0] + BLOCK
    lv = _level_stack()
    cos, sin = _rope_tables(p)
    swapped = [n in _SWAPPED for n in _LATE]

    h0, h0b, xhat0, rstd0, _, g_win = _embed_ln(x, meta_shard, w_in_shard, ln_emb_g, ln_emb_b)
    w_in = _whole("w_in", g_win)
    proj_hg, proj_att = _in_proj(h0b, w_in)
    yh, oa, states, scores, raw, probs, *gathered = _mixers_fwd(
        proj_hg, proj_att, lbounds, norm_g, lv, cos, sin, sinks, late_shards, swapped)
    w_bh, w_ba, w_out, w_fi, w_fo = [_whole(n, g) for n, g in zip(_LATE, gathered)]
    gates, mixin, h1, h1b, xhat1, rstd1 = _mix_out_ln1(yh, oa, h0b, w_in, h0, w_bh, w_ba, w_out, ln1_g, ln1_b)
    au, sw = _ffn_in_swiglu(h1b, w_fi)
    dr2, dr2b, loss_part, dg2, db2 = _ffn_out_loss(sw, w_fo, h1, ln2_g, ln2_b, target)

    d_wfo = _weight_grad_t([sw], dr2b, tk=2 * BLOCK, name="grad_w_ffn_out")
    dau, dh1 = _ffn_bwd(dr2, w_fo, au, w_fi)
    d_wfi = _weight_grad_t([dau], h1b, tk=4 * BLOCK, name="grad_w_ffn_in")
    dr1, dgates, dyh, doa, dg1, db1, d_wbh, d_wba, d_wout = _ln1_mix_bwd(
        dh1, xhat1, rstd1, ln1_g, yh, oa, gates, mixin, w_bh, w_ba, w_out)
    late_parts = [_slots(n, g) for n, g in zip(_LATE, (d_wbh, d_wba, d_wout, d_wfi, d_wfo))]
    dmix, d_lb, d_ng, d_sink, *late_recv = _mixers_bwd(
        proj_hg, proj_att, lbounds, norm_g, lv, states, scores, raw, probs, cos, sin, sinks, dyh, doa, late_parts,
        swapped)
    d_win = _weight_grad_t([dmix, dgates], h0b, tk=2 * BLOCK, name="grad_w_in")
    *win_flight, token = _exchange_start(_slots("w_in", d_win), "w_in_grads_start")
    grad_x, dlead, dg0, db0 = _embed_bwd(dmix, dgates, w_in, dr1, xhat0, rstd0, ln_emb_g + token[0:1, 0:1])

    small = dict(ln_emb_g=dg0, ln_emb_b=db0, hg_lower_bounds=d_lb, hg_norm_g=d_ng, ln1_g=dg1, ln1_b=db1, ln2_g=dg2,
                 ln2_b=db2)
    big = dict(zip(_LATE, zip(late_parts, late_recv)))
    return _pack_small(small, d_sink, dlead, loss_part), grad_x, big, win_flight


def _all_gather(arrs, dtypes, name):
    n = len(arrs)

    def body(*refs):
        ins, outs, stages = refs[:n], refs[n:2 * n], refs[2 * n:3 * n]
        send_sems, recv_sems, local_sems = refs[3 * n:]
        x, y, c = _place()
        sibling = (x, y, 1 - c)
        chips = [(1 - x, y), (x, 1 - y), (1 - x, 1 - y)]
        slot = lambda px, py, pc: 4 * px + 2 * py + pc

        def copy(w, k, block, to, from_stage=False):
            return pltpu.make_async_remote_copy(
                src_ref=stages[w] if from_stage else outs[w].at[slot(*block)], dst_ref=outs[w].at[slot(*block)],
                send_sem=send_sems.at[w, k], recv_sem=recv_sems.at[w, k], device_id=to, device_id_type=MESH)

        mine, first, passed = [], [], []
        for w in range(n):
            stages[w][...] = ins[w][...].astype(dtypes[w])
            mine.append(pltpu.make_async_copy(stages[w], outs[w].at[slot(x, y, c)], local_sems.at[w]))
            mine[-1].start()
        for w in range(n):
            first.append(copy(w, 0, (x, y, c), sibling, from_stage=True))
            first += [copy(w, 1 + j, (x, y, c), (*chip, c), from_stage=True) for j, chip in enumerate(chips)]
        for cp in first:
            cp.start()
        for j, chip in enumerate(chips):
            for w in range(n):
                copy(w, 1 + j, (*chip, c), (x, y, c)).wait_recv()
                passed.append(copy(w, 4 + j, (*chip, c), sibling))
                passed[-1].start()
        for w in range(n):
            copy(w, 0, sibling, (x, y, c)).wait_recv()
            for j, chip in enumerate(chips):
                copy(w, 4 + j, (*chip, 1 - c), (x, y, c)).wait_recv()
        for cp in first + passed:
            cp.wait_send()
        for cp in mine:
            cp.wait()

    return pl.pallas_call(
        body, name=name,
        in_specs=[pl.BlockSpec(memory_space=pltpu.VMEM)] * n,
        out_specs=[pl.BlockSpec(memory_space=pl.ANY)] * n,
        out_shape=[jax.ShapeDtypeStruct((N_DEV,) + a.shape, dt) for a, dt in zip(arrs, dtypes)],
        scratch_shapes=[pltpu.VMEM(a.shape, dt) for a, dt in zip(arrs, dtypes)]
        + [pltpu.SemaphoreType.DMA((n, 7)), pltpu.SemaphoreType.DMA((n, 7)), pltpu.SemaphoreType.DMA((n,))],
        compiler_params=pltpu.CompilerParams(vmem_limit_bytes=VMEM_LIMIT_BYTES),
    )(*arrs)


def _cast_shards(arrs):
    def body(*refs):
        for src, dst in zip(refs[:len(arrs)], refs[len(arrs):]):
            dst[...] = src[...].astype(BF16)

    return pl.pallas_call(body, name="cast_shards", out_shape=[jax.ShapeDtypeStruct(a.shape, BF16) for a in arrs],
                          compiler_params=pltpu.CompilerParams(vmem_limit_bytes=VMEM_LIMIT_BYTES))(*arrs)


def _shard_rows(rows):
    return rows if rows <= 512 else max(t for t in range(16, 353, 16) if rows % t == 0)


def _adamw_math(w, g, m, v):
    m = ADAM_B1 * m + (1.0 - ADAM_B1) * g
    v = ADAM_B2 * v + (1.0 - ADAM_B2) * (g * g)
    m_hat = m / (1.0 - ADAM_B1 ** ADAM_STEP)
    v_hat = v / (1.0 - ADAM_B2 ** ADAM_STEP)
    delta = -ADAM_LR * (m_hat / (jnp.sqrt(v_hat) + ADAM_EPS) + ADAM_WD * w)
    return delta, m, v


def _reduce_adamw(parts, recv, own_slot, w, m, v, name):
    r, cdim = w.shape
    tr = _shard_rows(r)

    def body(idx_ref, p_ref, r_ref, w_ref, m_ref, v_ref, g_out, d_out, m_out, v_out):
        g = p_ref[0].astype(F32)
        for j in range(N_PEERS):
            g = g + r_ref[j].astype(F32)
        d, mn, vn = _adamw_math(w_ref[...], g, m_ref[...], v_ref[...])
        g_out[...] = g
        d_out[...] = d
        m_out[...] = mn
        v_out[...] = vn

    flat = pl.BlockSpec((tr, cdim), lambda i, idx_ref: (i, 0))
    return pl.pallas_call(
        body, name=name,
        grid_spec=pltpu.PrefetchScalarGridSpec(
            num_scalar_prefetch=1, grid=(r // tr,),
            in_specs=[pl.BlockSpec((1, tr, cdim), lambda i, idx_ref: (idx_ref[0], i, 0)),
                      pl.BlockSpec((N_PEERS, tr, cdim), lambda i, idx_ref: (0, i, 0)), flat, flat, flat],
            out_specs=[flat] * 4),
        out_shape=[jax.ShapeDtypeStruct((r, cdim), F32)] * 4,
        compiler_params=_cparams(("arbitrary",)),
    )(own_slot, parts, recv, w, m, v)


def _adamw_plain(w, g, m, v, name):
    def body(w_ref, g_ref, m_ref, v_ref, d_out, m_out, v_out):
        d_out[...], m_out[...], v_out[...] = _adamw_math(w_ref[...], g_ref[...], m_ref[...], v_ref[...])

    return pl.pallas_call(body, name=name, out_shape=[jax.ShapeDtypeStruct(w.shape, F32)] * 3)(w, g, m, v)


_SMALL = (("ln_emb_g", (1, D_MODEL)), ("ln_emb_b", (1, D_MODEL)), ("hg_lower_bounds", (2, HG_W)),
          ("hg_norm_g", (1, BLOCK)), ("attn_sinks", (1, ATT_HEADS)), ("ln1_g", (1, D_MODEL)), ("ln1_b", (1, D_MODEL)),
          ("ln2_g", (1, D_MODEL)), ("ln2_b", (1, D_MODEL)))
_SMALL_ROW, _LOSS_ROW = {}, 0
for _name, (_rows, _) in _SMALL:
    _SMALL_ROW[_name], _LOSS_ROW = _LOSS_ROW, _LOSS_ROW + _rows
_META_ROW = 16
SMALL_ROWS = _META_ROW + N_META
assert _LOSS_ROW < _META_ROW


def _pack_small(grads, d_sink, dlead, loss_part):
    names = [n for n, _ in _SMALL if n != "attn_sinks"]

    def body(*refs):
        ins = dict(zip(names, refs))
        sink_ref, lead_ref, loss_ref, o_ref = refs[len(names):]
        o_ref[...] = jnp.zeros_like(o_ref)
        for name, (rows, cols) in _SMALL:
            if name != "attn_sinks":
                o_ref[_SMALL_ROW[name]:_SMALL_ROW[name] + rows, :cols] = ins[name][...]
        head = lax.broadcasted_iota(jnp.int32, (ATT_HEADS, BLOCK), 0)
        lane = lax.broadcasted_iota(jnp.int32, (ATT_HEADS, BLOCK), 1)
        o_ref[_SMALL_ROW["attn_sinks"]:_SMALL_ROW["attn_sinks"] + 1, :BLOCK] = jnp.sum(
            jnp.where(head == lane, sink_ref[...], 0.0), axis=0, keepdims=True)
        o_ref[_LOSS_ROW:_LOSS_ROW + 1, :BLOCK] = loss_ref[...]
        o_ref[_META_ROW:, :] = lead_ref[PAD:BLOCK, :]

    return pl.pallas_call(body, name="pack_small", out_shape=jax.ShapeDtypeStruct((SMALL_ROWS, D_MODEL), F32))(
        *[grads[n] for n in names], d_sink, dlead, loss_part)


def _small_reduce_adamw(gathered, weights, mom1, mom2):
    n = len(_SMALL)

    def body(*refs):
        g_ref, w_refs, m_refs, v_refs = refs[0], refs[1:1 + n], refs[1 + n:1 + 2 * n], refs[1 + 2 * n:1 + 3 * n]
        outs = refs[1 + 3 * n:1 + 7 * n]
        meta_out, loss_out, sum_ref = refs[1 + 7 * n:]
        total = g_ref[0]
        for s in range(1, N_DEV):
            total = total + g_ref[s]
        sum_ref[...] = total
        for i, (name, (rows, cols)) in enumerate(_SMALL):
            g = sum_ref[_SMALL_ROW[name]:_SMALL_ROW[name] + rows, :cols]
            d, mn, vn = _adamw_math(w_refs[i][...], g, m_refs[i][...], v_refs[i][...])
            for out, val in zip(outs[4 * i:4 * i + 4], (g, d, mn, vn)):
                out[...] = val
        meta_out[...] = sum_ref[_META_ROW:, :]
        loss_out[...] = jnp.broadcast_to(jnp.sum(sum_ref[_LOSS_ROW:_LOSS_ROW + 1, :BLOCK]), (1, BLOCK))

    per_param = [jax.ShapeDtypeStruct(shape, F32) for _, shape in _SMALL for _ in range(4)]
    res = pl.pallas_call(
        body, name="small_reduce_adamw",
        out_shape=per_param + [jax.ShapeDtypeStruct((N_META, D_MODEL), F32), jax.ShapeDtypeStruct((1, BLOCK), F32)],
        scratch_shapes=[pltpu.VMEM((SMALL_ROWS, D_MODEL), F32)],
    )(gathered, *[d[name] for d in (weights, mom1, mom2) for name, _ in _SMALL])
    return {name: res[4 * i:4 * i + 4] for i, (name, _) in enumerate(_SMALL)}, res[-2], res[-1]


_WEIGHTS = ("meta_tokens", "ln_emb_g", "ln_emb_b", "w_in", "hg_lower_bounds", "hg_norm_g", "attn_sinks",
            "w_branch_hg", "w_branch_attn", "w_out", "ln1_g", "ln1_b", "w_ffn_in", "w_ffn_out", "ln2_g", "ln2_b")


def kernel(x, meta_tokens, ln_emb_g, ln_emb_b, w_in, hg_lower_bounds, hg_norm_g, attn_sinks, w_branch_hg, w_branch_attn, w_out, ln1_g, ln1_b, w_ffn_in, w_ffn_out, ln2_g, ln2_b, loss_target, m_meta_tokens, m_ln_emb_g, m_ln_emb_b, m_w_in, m_hg_lower_bounds, m_hg_norm_g, m_attn_sinks, m_w_branch_hg, m_w_branch_attn, m_w_out, m_ln1_g, m_ln1_b, m_w_ffn_in, m_w_ffn_out, m_ln2_g, m_ln2_b, v_meta_tokens, v_ln_emb_g, v_ln_emb_b, v_w_in, v_hg_lower_bounds, v_hg_norm_g, v_attn_sinks, v_w_branch_hg, v_w_branch_attn, v_w_out, v_ln1_g, v_ln1_b, v_w_ffn_in, v_w_ffn_out, v_ln2_g, v_ln2_b):
    given = dict(locals())
    weights = {n: given[n] for n in _WEIGHTS}
    mom1 = {n: given["m_" + n] for n in _WEIGHTS}
    mom2 = {n: given["v_" + n] for n in _WEIGHTS}
    shard2d = lambda n, a: a.reshape(a.shape[-2:]).T if n in _TRANSPOSED else a.reshape(a.shape[-2:])

    w_in_shard, *late_shards = _cast_shards([shard2d(n, weights[n]) for n in ("w_in",) + _LATE])
    packed, grad_x, big, win_flight = _device_step(
        x[0], loss_target[0], meta_tokens, ln_emb_g.reshape(1, -1), ln_emb_b.reshape(1, -1), w_in_shard,
        hg_lower_bounds, hg_norm_g, attn_sinks, late_shards, ln1_g, ln1_b, ln2_g, ln2_b)

    place = _place()
    out = {}

    def reduce_adamw(n, parts, recv):
        own = _slot(place, n in _SWAPPED).astype(jnp.int32).reshape(1)
        res = _reduce_adamw(parts, recv, own, shard2d(n, weights[n]), shard2d(n, mom1[n]), shard2d(n, mom2[n]),
                            "adamw_" + n)
        out[n] = [(r.T if n in _TRANSPOSED else r).reshape(weights[n].shape) for r in res]

    for n, (parts, recv) in big.items():
        reduce_adamw(n, parts, recv)

    all_small, = _all_gather([packed], [F32], "gather_small")
    as_2d = lambda d: {n: d[n].reshape(shape) for n, shape in _SMALL}
    small_out, meta_whole, loss_row = _small_reduce_adamw(all_small, as_2d(weights), as_2d(mom1), as_2d(mom2))
    for n, res in small_out.items():
        out[n] = [r.reshape(weights[n].shape) for r in res]
    loss = loss_row[0, 0]
    g_meta_mine = lax.dynamic_index_in_dim(meta_whole.reshape(N_META, N_DEV, D_MODEL // N_DEV), _slot(place, False),
                                           axis=1, keepdims=False)
    out["meta_tokens"] = [g_meta_mine, *_adamw_plain(meta_tokens, g_meta_mine, m_meta_tokens, v_meta_tokens,
                                                     "adamw_meta")]

    reduce_adamw("w_in", *_exchange_wait(*win_flight, after=all_small, name="w_in_grads_wait"))

    return (loss, grad_x[None], *[out[n][0] for n in _WEIGHTS], *[out[n][1] for n in _WEIGHTS],
            *[out[n][2] for n in _WEIGHTS], *[out[n][3] for n in _WEIGHTS])
```
